```python
import jax, jax.numpy as jnp
from jax import lax
import numpy as np

D_MODEL = 1024
BATCH = 8
SEQ = 4096
DEPTH = 1

MLA_HEADS = 4
QK_NOPE = 128
QK_ROPE = 64
V_HEAD = 128
Q_LORA = 256
KV_LORA = 256
QK_HEAD = QK_NOPE + QK_ROPE
MLA_WIDTH = MLA_HEADS * V_HEAD
HG_HEADS = 4
HG_DK = 128
HG_DV = 128
HG_FDIM = HG_HEADS * HG_DK
HG_WIDTH = HG_HEADS * HG_DV
CHUNK = 64
D_MIX = MLA_WIDTH + HG_WIDTH
D_FF = -(-8 * D_MODEL // (3 * 256)) * 256
PLE_DIM = 256
ROPE_THETA = 10000.0
EPS = 1e-6
Q_BLOCK = 128
IN_SIZES = (Q_LORA, KV_LORA, QK_ROPE, HG_FDIM, HG_FDIM, HG_FDIM, HG_WIDTH, HG_WIDTH)
D_IN = sum(IN_SIZES)
IN_SPLITS = tuple(int(v) for v in np.cumsum(IN_SIZES)[:-1])

kernel_name = "hymba_mla_hgrn2_ple_block"


def rms_norm(x, g):
    xf = x.astype(jnp.float32)
    y = xf * lax.rsqrt(jnp.mean(xf * xf, axis=-1, keepdims=True) + EPS)
    return (y * g.astype(jnp.float32)).astype(x.dtype)


def rope(x, cos, sin):
    x1, x2 = jnp.split(x.astype(jnp.float32), 2, axis=-1)
    return jnp.concatenate([x1 * cos - x2 * sin, x1 * sin + x2 * cos], axis=-1).astype(x.dtype)


def blocked_attention(q, k, v):
    B, S, H, D = q.shape
    nb = S // Q_BLOCK
    qb = q.reshape(B, nb, Q_BLOCK, H, D).transpose(1, 0, 2, 3, 4)
    scale = QK_HEAD ** -0.5

    def one_block(q_blk):
        s = jnp.einsum('bqhd,bkhd->bhqk', q_blk, k, preferred_element_type=jnp.float32) * scale
        w = jax.nn.softmax(s, axis=-1).astype(v.dtype)
        return jnp.einsum('bhqk,bkhd->bqhd', w, v)

    o = lax.map(one_block, qb)
    return o.transpose(1, 0, 2, 3, 4).reshape(B, S, H * v.shape[-1])


def mla(c_q, c_kv, k_r, cos, sin, g_qa, g_kva, w_qb, w_kvb, g_qn, g_kn):
    B, S, _ = c_q.shape
    q = (rms_norm(c_q, g_qa) @ w_qb).reshape(B, S, MLA_HEADS, QK_HEAD)
    kv = (rms_norm(c_kv, g_kva) @ w_kvb).reshape(B, S, MLA_HEADS, QK_NOPE + V_HEAD)
    k_nope, v = kv[..., :QK_NOPE], kv[..., QK_NOPE:]
    k = jnp.concatenate([k_nope, jnp.broadcast_to(k_r[:, :, None, :], (B, S, MLA_HEADS, QK_ROPE))], axis=-1)
    q = rms_norm(q, g_qn)
    k = rms_norm(k, g_kn)
    c, s = cos[:, :, None, :], sin[:, :, None, :]
    q = jnp.concatenate([q[..., :QK_NOPE], rope(q[..., QK_NOPE:], c, s)], axis=-1)
    k = jnp.concatenate([k[..., :QK_NOPE], rope(k[..., QK_NOPE:], c, s)], axis=-1)
    return blocked_attention(q, k, v)


def gla_chunkwise(q, k, v, logf):
    B, H, S, DK = q.shape
    DV = v.shape[-1]
    N = S // CHUNK
    q = q.reshape(B, H, N, CHUNK, DK)
    k = k.reshape(B, H, N, CHUNK, DK)
    v = v.reshape(B, H, N, CHUNK, DV)
    b = jnp.cumsum(logf.reshape(B, H, N, CHUNK, DK), axis=3)
    b_last = b[:, :, :, -1:, :]
    b_mid = b[:, :, :, CHUNK // 2 - 1:CHUNK // 2, :]
    q_intra = q * jnp.exp(b - b_mid)
    k_intra = k * jnp.exp(b_mid - b)
    mask = jnp.tril(jnp.ones((CHUNK, CHUNK), jnp.float32))
    A = jnp.einsum('bhntd,bhnsd->bhnts', q_intra, k_intra) * mask
    o_intra = jnp.einsum('bhnts,bhnsv->bhntv', A, v)
    kv = jnp.einsum('bhnsd,bhnsv->bhndv', k * jnp.exp(b_last - b), v)
    decay = jnp.exp(b_last[:, :, :, 0, :])

    def step(state, inp):
        dec, kv_n = inp
        return dec[..., None] * state + kv_n, state

    _, s_before = lax.scan(step, jnp.zeros((B, H, DK, DV), jnp.float32),
                           (jnp.moveaxis(decay, 2, 0), jnp.moveaxis(kv, 2, 0)))
    s_before = jnp.moveaxis(s_before, 0, 2)
    o_inter = jnp.einsum('bhntd,bhndv->bhntv', q * jnp.exp(b), s_before)
    return (o_intra + o_inter).reshape(B, H, S, DV)


def hgrn2(hq, hf_fwd, hf_bwd, hi, hg, lb, g_out):
    B, S, _ = hq.shape

    def heads(t):
        return t.reshape(B, S, HG_HEADS, -1).transpose(0, 2, 1, 3)

    q = heads(jax.nn.silu(hq.astype(jnp.float32)))
    v = heads(hi.astype(jnp.float32))

    def gates(f_pre, lower):
        f = lower + (1.0 - lower) * jax.nn.sigmoid(f_pre.astype(jnp.float32))
        return heads(1.0 - f), heads(jnp.log(f))

    k_f, lf_f = gates(hf_fwd, lb[0])
    k_b, lf_b = gates(hf_bwd, lb[1])
    o_f = gla_chunkwise(q, k_f, v, lf_f)
    o_b = gla_chunkwise(q[:, :, ::-1], k_b[:, :, ::-1], v[:, :, ::-1], lf_b[:, :, ::-1])[:, :, ::-1]
    o = (o_f + o_b).transpose(0, 2, 1, 3)
    o = rms_norm(o, g_out).reshape(B, S, HG_WIDTH)
    return (o * jax.nn.silu(hg.astype(jnp.float32))).astype(hq.dtype)


def _fwd_setup_inputs(seed: int = 0) -> dict:
    key = jax.random.key(seed)
    ks = jax.random.split(key, 32)
    f32 = jnp.float32

    def w(k, shape, fan_in):
        return jax.random.normal(k, shape, f32) * (fan_in ** -0.5)

    def gain(k, shape):
        return 1.0 + 0.05 * jax.random.normal(k, shape, f32)

    x = jax.random.normal(ks[0], (BATCH, SEQ, D_MODEL), f32)
    p = jax.random.normal(ks[1], (DEPTH, BATCH, SEQ, PLE_DIM), f32)
    offsets = jax.random.randint(ks[2], (BATCH, 1), 0, 1024, jnp.int32)
    positions = offsets + jnp.arange(SEQ, dtype=jnp.int32)[None, :]
    return {
        "x": x,
        "p": p,
        "positions": positions,
        "g_mix": gain(ks[3], (DEPTH, D_MODEL)),
        "w_in": w(ks[4], (DEPTH, D_MODEL, D_IN), D_MODEL),
        "g_qa": gain(ks[5], (DEPTH, Q_LORA)),
        "g_kva": gain(ks[6], (DEPTH, KV_LORA)),
        "w_qb": w(ks[7], (DEPTH, Q_LORA, MLA_HEADS * QK_HEAD), Q_LORA),
        "w_kvb": w(ks[8], (DEPTH, KV_LORA, MLA_HEADS * (QK_NOPE + V_HEAD)), KV_LORA),
        "g_qn": gain(ks[9], (DEPTH, QK_HEAD)),
        "g_kn": gain(ks[10], (DEPTH, QK_HEAD)),
        "lb_param": 0.1 * jax.random.normal(ks[11], (DEPTH + 1, 2, HG_FDIM), f32),
        "g_hgo": gain(ks[12], (DEPTH, HG_HEADS, HG_DV)),
        "w_o": w(ks[13], (DEPTH, D_MIX, D_MODEL), D_MIX),
        "g_ffn": gain(ks[14], (DEPTH, D_MODEL)),
        "w_gate": w(ks[15], (DEPTH, D_MODEL, D_FF), D_MODEL),
        "w_up": w(ks[16], (DEPTH, D_MODEL, D_FF), D_MODEL),
        "w_down": w(ks[17], (DEPTH, D_FF, D_MODEL), D_FF),
        "g_ple": gain(ks[18], (DEPTH, D_MODEL)),
        "w_ple_gate": w(ks[19], (DEPTH, D_MODEL, D_MODEL), D_MODEL),
        "w_ple_proj": w(ks[20], (DEPTH, PLE_DIM, D_MODEL), PLE_DIM),
    }


def _fwd_reference(x, p, positions, g_mix, w_in, g_qa, g_kva, w_qb, w_kvb, g_qn, g_kn,
              lb_param, g_hgo, w_o, g_ffn, w_gate, w_up, w_down, g_ple, w_ple_gate, w_ple_proj):
    inv_freq = ROPE_THETA ** (-jnp.arange(0, QK_ROPE, 2, dtype=jnp.float32) / QK_ROPE)
    ang = positions.astype(jnp.float32)[..., None] * inv_freq
    cos, sin = jnp.cos(ang), jnp.sin(ang)
    lower_bounds = jnp.cumsum(jax.nn.softmax(lb_param.astype(jnp.float32), axis=0), axis=0)

    for l in range(DEPTH):
        h = rms_norm(x, g_mix[l])
        z = h @ w_in[l]
        c_q, c_kv, k_r, hq, hf_f, hf_b, hi, hg = jnp.split(z, IN_SPLITS, axis=-1)
        a = mla(c_q, c_kv, k_r, cos, sin, g_qa[l], g_kva[l], w_qb[l], w_kvb[l], g_qn[l], g_kn[l])
        r = hgrn2(hq, hf_f, hf_b, hi, hg, lower_bounds[l], g_hgo[l])
        x = x + jnp.concatenate([a, r], axis=-1) @ w_o[l]
        h = rms_norm(x, g_ffn[l])
        x = x + (jax.nn.silu(h @ w_gate[l]) * (h @ w_up[l])) @ w_down[l]
        gate = jax.nn.sigmoid(rms_norm(x, g_ple[l]) @ w_ple_gate[l])
        x = x + gate * (p[l].astype(x.dtype) @ w_ple_proj[l])
    return x


import jax as _jax
import jax.numpy as _jnp

TWIN_FORMAT = 'train_step'
FWD_PARAMS = ['x', 'p', 'positions', 'g_mix', 'w_in', 'g_qa', 'g_kva', 'w_qb', 'w_kvb', 'g_qn', 'g_kn', 'lb_param', 'g_hgo', 'w_o', 'g_ffn', 'w_gate', 'w_up', 'w_down', 'g_ple', 'w_ple_gate', 'w_ple_proj']
TWIN_WEIGHTS = ['g_mix', 'w_in', 'g_qa', 'g_kva', 'w_qb', 'w_kvb', 'g_qn', 'g_kn', 'lb_param', 'g_hgo', 'w_o', 'g_ffn', 'w_gate', 'w_up', 'w_down', 'g_ple', 'w_ple_gate', 'w_ple_proj']
TWIN_DIFF_INPUT = 'x'
TWIN_INPUTS = ['x', 'p', 'positions', 'g_mix', 'w_in', 'g_qa', 'g_kva', 'w_qb', 'w_kvb', 'g_qn', 'g_kn', 'lb_param', 'g_hgo', 'w_o', 'g_ffn', 'w_gate', 'w_up', 'w_down', 'g_ple', 'w_ple_gate', 'w_ple_proj', 'loss_target', 'm_g_mix', 'm_w_in', 'm_g_qa', 'm_g_kva', 'm_w_qb', 'm_w_kvb', 'm_g_qn', 'm_g_kn', 'm_lb_param', 'm_g_hgo', 'm_w_o', 'm_g_ffn', 'm_w_gate', 'm_w_up', 'm_w_down', 'm_g_ple', 'm_w_ple_gate', 'm_w_ple_proj', 'v_g_mix', 'v_w_in', 'v_g_qa', 'v_g_kva', 'v_w_qb', 'v_w_kvb', 'v_g_qn', 'v_g_kn', 'v_lb_param', 'v_g_hgo', 'v_w_o', 'v_g_ffn', 'v_w_gate', 'v_w_up', 'v_w_down', 'v_g_ple', 'v_w_ple_gate', 'v_w_ple_proj']
TWIN_OUTPUTS = ['loss', 'grad_x', 'grad_g_mix', 'grad_w_in', 'grad_g_qa', 'grad_g_kva', 'grad_w_qb', 'grad_w_kvb', 'grad_g_qn', 'grad_g_kn', 'grad_lb_param', 'grad_g_hgo', 'grad_w_o', 'grad_g_ffn', 'grad_w_gate', 'grad_w_up', 'grad_w_down', 'grad_g_ple', 'grad_w_ple_gate', 'grad_w_ple_proj', 'delta_g_mix', 'delta_w_in', 'delta_g_qa', 'delta_g_kva', 'delta_w_qb', 'delta_w_kvb', 'delta_g_qn', 'delta_g_kn', 'delta_lb_param', 'delta_g_hgo', 'delta_w_o', 'delta_g_ffn', 'delta_w_gate', 'delta_w_up', 'delta_w_down', 'delta_g_ple', 'delta_w_ple_gate', 'delta_w_ple_proj', 'new_m_g_mix', 'new_m_w_in', 'new_m_g_qa', 'new_m_g_kva', 'new_m_w_qb', 'new_m_w_kvb', 'new_m_g_qn', 'new_m_g_kn', 'new_m_lb_param', 'new_m_g_hgo', 'new_m_w_o', 'new_m_g_ffn', 'new_m_w_gate', 'new_m_w_up', 'new_m_w_down', 'new_m_g_ple', 'new_m_w_ple_gate', 'new_m_w_ple_proj', 'new_v_g_mix', 'new_v_w_in', 'new_v_g_qa', 'new_v_g_kva', 'new_v_w_qb', 'new_v_w_kvb', 'new_v_g_qn', 'new_v_g_kn', 'new_v_lb_param', 'new_v_g_hgo', 'new_v_w_o', 'new_v_g_ffn', 'new_v_w_gate', 'new_v_w_up', 'new_v_w_down', 'new_v_g_ple', 'new_v_w_ple_gate', 'new_v_w_ple_proj']
TWIN_LEAF_KINDS = {'loss': 'loss', 'grad_x': 'grad_x', 'grad_g_mix': 'grad_w', 'grad_w_in': 'grad_w', 'grad_g_qa': 'grad_w', 'grad_g_kva': 'grad_w', 'grad_w_qb': 'grad_w', 'grad_w_kvb': 'grad_w', 'grad_g_qn': 'grad_w', 'grad_g_kn': 'grad_w', 'grad_lb_param': 'grad_w', 'grad_g_hgo': 'grad_w', 'grad_w_o': 'grad_w', 'grad_g_ffn': 'grad_w', 'grad_w_gate': 'grad_w', 'grad_w_up': 'grad_w', 'grad_w_down': 'grad_w', 'grad_g_ple': 'grad_w', 'grad_w_ple_gate': 'grad_w', 'grad_w_ple_proj': 'grad_w', 'delta_g_mix': 'delta_w', 'delta_w_in': 'delta_w', 'delta_g_qa': 'delta_w', 'delta_g_kva': 'delta_w', 'delta_w_qb': 'delta_w', 'delta_w_kvb': 'delta_w', 'delta_g_qn': 'delta_w', 'delta_g_kn': 'delta_w', 'delta_lb_param': 'delta_w', 'delta_g_hgo': 'delta_w', 'delta_w_o': 'delta_w', 'delta_g_ffn': 'delta_w', 'delta_w_gate': 'delta_w', 'delta_w_up': 'delta_w', 'delta_w_down': 'delta_w', 'delta_g_ple': 'delta_w', 'delta_w_ple_gate': 'delta_w', 'delta_w_ple_proj': 'delta_w', 'new_m_g_mix': 'new_m', 'new_m_w_in': 'new_m', 'new_m_g_qa': 'new_m', 'new_m_g_kva': 'new_m', 'new_m_w_qb': 'new_m', 'new_m_w_kvb': 'new_m', 'new_m_g_qn': 'new_m', 'new_m_g_kn': 'new_m', 'new_m_lb_param': 'new_m', 'new_m_g_hgo': 'new_m', 'new_m_w_o': 'new_m', 'new_m_g_ffn': 'new_m', 'new_m_w_gate': 'new_m', 'new_m_w_up': 'new_m', 'new_m_w_down': 'new_m', 'new_m_g_ple': 'new_m', 'new_m_w_ple_gate': 'new_m', 'new_m_w_ple_proj': 'new_m', 'new_v_g_mix': 'new_v', 'new_v_w_in': 'new_v', 'new_v_g_qa': 'new_v', 'new_v_g_kva': 'new_v', 'new_v_w_qb': 'new_v', 'new_v_w_kvb': 'new_v', 'new_v_g_qn': 'new_v', 'new_v_g_kn': 'new_v', 'new_v_lb_param': 'new_v', 'new_v_g_hgo': 'new_v', 'new_v_w_o': 'new_v', 'new_v_g_ffn': 'new_v', 'new_v_w_gate': 'new_v', 'new_v_w_up': 'new_v', 'new_v_w_down': 'new_v', 'new_v_g_ple': 'new_v', 'new_v_w_ple_gate': 'new_v', 'new_v_w_ple_proj': 'new_v'}


def _forward(args):
    return _fwd_reference(*[args[k] for k in FWD_PARAMS])


def _output_shape():
    out = _jax.eval_shape(lambda: _forward(_fwd_setup_inputs(0)))
    return out.shape, out.dtype

N_MICROBATCH = 1
ADAM_LR = 0.001
ADAM_B1 = 0.9
ADAM_B2 = 0.999
ADAM_EPS = 1e-08
ADAM_WD = 0.01
ADAM_STEP = 10
PER_EXAMPLE_BATCH_AXIS = {'x': 0, 'p': 1, 'positions': 0, 'loss_target': 0}
SHARED_INPUTS = []
_WEIGHT_DTYPES = {'g_mix': _jnp.float32, 'w_in': _jnp.float32, 'g_qa': _jnp.float32, 'g_kva': _jnp.float32, 'w_qb': _jnp.float32, 'w_kvb': _jnp.float32, 'g_qn': _jnp.float32, 'g_kn': _jnp.float32, 'lb_param': _jnp.float32, 'g_hgo': _jnp.float32, 'w_o': _jnp.float32, 'g_ffn': _jnp.float32, 'w_gate': _jnp.float32, 'w_up': _jnp.float32, 'w_down': _jnp.float32, 'g_ple': _jnp.float32, 'w_ple_gate': _jnp.float32, 'w_ple_proj': _jnp.float32}
MOMENT_SCALE = {'g_mix': 6.608488e+00, 'w_in': 2.049625e-01, 'g_qa': 6.600482e-02, 'g_kva': 1.952785e-01, 'w_qb': 3.971587e-02, 'w_kvb': 5.205287e-02, 'g_qn': 2.064485e-01, 'g_kn': 2.081488e-01, 'lb_param': 1.505406e-02, 'g_hgo': 1.143859e+01, 'w_o': 2.465615e-01, 'g_ffn': 2.473568e+01, 'w_gate': 1.613772e-01, 'w_up': 1.687598e-01, 'w_down': 2.595229e-01, 'g_ple': 9.457090e-01, 'w_ple_gate': 7.707643e-02, 'w_ple_proj': 4.220218e-01}


def _to_microbatches(a, axis):
    t = _jnp.moveaxis(a, axis, 0)
    t = t.reshape((N_MICROBATCH, t.shape[0] // N_MICROBATCH) + t.shape[1:])
    return _jnp.moveaxis(t, 1, axis + 1)


def setup_inputs(seed: int = 0) -> dict:
    inp = _fwd_setup_inputs(seed)
    key = _jax.random.fold_in(_jax.random.key(seed), 7919)
    shape, _ = _output_shape()
    out = dict(inp)
    out["loss_target"] = _jax.random.normal(_jax.random.fold_in(key, 0), shape, _jnp.float32)
    for i, name in enumerate(TWIN_WEIGHTS):
        w = inp[name].astype(_jnp.float32)
        if MOMENT_SCALE is None:
            s = _jnp.sqrt(_jnp.mean(_jnp.square(w)) + 1e-30)
        else:
            s = MOMENT_SCALE[name]
        km, kv = _jax.random.split(_jax.random.fold_in(key, i + 1))
        out[name] = w
        out["m_" + name] = s * _jax.random.normal(km, w.shape, _jnp.float32)
        out["v_" + name] = (s * s) * _jax.random.uniform(kv, w.shape, _jnp.float32, 0.5, 1.5)
    if N_MICROBATCH > 1:
        for name, axis in PER_EXAMPLE_BATCH_AXIS.items():
            out[name] = _to_microbatches(out[name], axis)
    return {'x': out['x'], 'p': out['p'], 'positions': out['positions'], 'g_mix': out['g_mix'], 'w_in': out['w_in'], 'g_qa': out['g_qa'], 'g_kva': out['g_kva'], 'w_qb': out['w_qb'], 'w_kvb': out['w_kvb'], 'g_qn': out['g_qn'], 'g_kn': out['g_kn'], 'lb_param': out['lb_param'], 'g_hgo': out['g_hgo'], 'w_o': out['w_o'], 'g_ffn': out['g_ffn'], 'w_gate': out['w_gate'], 'w_up': out['w_up'], 'w_down': out['w_down'], 'g_ple': out['g_ple'], 'w_ple_gate': out['w_ple_gate'], 'w_ple_proj': out['w_ple_proj'], 'loss_target': out['loss_target'], 'm_g_mix': out['m_g_mix'], 'm_w_in': out['m_w_in'], 'm_g_qa': out['m_g_qa'], 'm_g_kva': out['m_g_kva'], 'm_w_qb': out['m_w_qb'], 'm_w_kvb': out['m_w_kvb'], 'm_g_qn': out['m_g_qn'], 'm_g_kn': out['m_g_kn'], 'm_lb_param': out['m_lb_param'], 'm_g_hgo': out['m_g_hgo'], 'm_w_o': out['m_w_o'], 'm_g_ffn': out['m_g_ffn'], 'm_w_gate': out['m_w_gate'], 'm_w_up': out['m_w_up'], 'm_w_down': out['m_w_down'], 'm_g_ple': out['m_g_ple'], 'm_w_ple_gate': out['m_w_ple_gate'], 'm_w_ple_proj': out['m_w_ple_proj'], 'v_g_mix': out['v_g_mix'], 'v_w_in': out['v_w_in'], 'v_g_qa': out['v_g_qa'], 'v_g_kva': out['v_g_kva'], 'v_w_qb': out['v_w_qb'], 'v_w_kvb': out['v_w_kvb'], 'v_g_qn': out['v_g_qn'], 'v_g_kn': out['v_g_kn'], 'v_lb_param': out['v_lb_param'], 'v_g_hgo': out['v_g_hgo'], 'v_w_o': out['v_w_o'], 'v_g_ffn': out['v_g_ffn'], 'v_w_gate': out['v_w_gate'], 'v_w_up': out['v_w_up'], 'v_w_down': out['v_w_down'], 'v_g_ple': out['v_g_ple'], 'v_w_ple_gate': out['v_w_ple_gate'], 'v_w_ple_proj': out['v_w_ple_proj']}


def _loss(weights, diff, rest, loss_target):
    with _jax.named_scope("forward"):
        args = {**rest, TWIN_DIFF_INPUT: diff, **{k: w.astype(_WEIGHT_DTYPES[k]) for k, w in weights.items()}}
        y = _forward(args)
    with _jax.named_scope("loss_head"):
        err = _jnp.square(y.astype(_jnp.float32) - loss_target)
        return 0.5 * _jnp.sum(_jnp.mean(err, axis=-1)) if err.ndim else 0.5 * err


def _adamw(w, g, m, v):
    m = ADAM_B1 * m + (1.0 - ADAM_B1) * g
    v = ADAM_B2 * v + (1.0 - ADAM_B2) * _jnp.square(g)
    m_hat = m / (1.0 - ADAM_B1 ** ADAM_STEP)
    v_hat = v / (1.0 - ADAM_B2 ** ADAM_STEP)
    delta = -ADAM_LR * (m_hat / (_jnp.sqrt(v_hat) + ADAM_EPS) + ADAM_WD * w)
    return delta, m, v


def reference(x, p, positions, g_mix, w_in, g_qa, g_kva, w_qb, w_kvb, g_qn, g_kn, lb_param, g_hgo, w_o, g_ffn, w_gate, w_up, w_down, g_ple, w_ple_gate, w_ple_proj, loss_target, m_g_mix, m_w_in, m_g_qa, m_g_kva, m_w_qb, m_w_kvb, m_g_qn, m_g_kn, m_lb_param, m_g_hgo, m_w_o, m_g_ffn, m_w_gate, m_w_up, m_w_down, m_g_ple, m_w_ple_gate, m_w_ple_proj, v_g_mix, v_w_in, v_g_qa, v_g_kva, v_w_qb, v_w_kvb, v_g_qn, v_g_kn, v_lb_param, v_g_hgo, v_w_o, v_g_ffn, v_w_gate, v_w_up, v_w_down, v_g_ple, v_w_ple_gate, v_w_ple_proj):
    given = dict(x=x, p=p, positions=positions, g_mix=g_mix, w_in=w_in, g_qa=g_qa, g_kva=g_kva, w_qb=w_qb, w_kvb=w_kvb, g_qn=g_qn, g_kn=g_kn, lb_param=lb_param, g_hgo=g_hgo, w_o=w_o, g_ffn=g_ffn, w_gate=w_gate, w_up=w_up, w_down=w_down, g_ple=g_ple, w_ple_gate=w_ple_gate, w_ple_proj=w_ple_proj, loss_target=loss_target, m_g_mix=m_g_mix, m_w_in=m_w_in, m_g_qa=m_g_qa, m_g_kva=m_g_kva, m_w_qb=m_w_qb, m_w_kvb=m_w_kvb, m_g_qn=m_g_qn, m_g_kn=m_g_kn, m_lb_param=m_lb_param, m_g_hgo=m_g_hgo, m_w_o=m_w_o, m_g_ffn=m_g_ffn, m_w_gate=m_w_gate, m_w_up=m_w_up, m_w_down=m_w_down, m_g_ple=m_g_ple, m_w_ple_gate=m_w_ple_gate, m_w_ple_proj=m_w_ple_proj, v_g_mix=v_g_mix, v_w_in=v_w_in, v_g_qa=v_g_qa, v_g_kva=v_g_kva, v_w_qb=v_w_qb, v_w_kvb=v_w_kvb, v_g_qn=v_g_qn, v_g_kn=v_g_kn, v_lb_param=v_lb_param, v_g_hgo=v_g_hgo, v_w_o=v_w_o, v_g_ffn=v_g_ffn, v_w_gate=v_w_gate, v_w_up=v_w_up, v_w_down=v_w_down, v_g_ple=v_g_ple, v_w_ple_gate=v_w_ple_gate, v_w_ple_proj=v_w_ple_proj)
    weights = {n: given[n] for n in TWIN_WEIGHTS}
    shared = {n: given[n] for n in SHARED_INPUTS}
    per_example = {n: given[n] for n in ['x', 'p', 'positions']}
    grad_fn = _jax.value_and_grad(_loss, argnums=(0, 1))

    def one_microbatch(ex, loss_target):
        ex = dict(ex)
        diff = ex.pop(TWIN_DIFF_INPUT)
        return grad_fn(weights, diff, {**shared, **ex}, loss_target)

    if N_MICROBATCH == 1:
        loss, (grad_w, grad_x) = one_microbatch(per_example, given["loss_target"])
    else:
        def body(carry, xs):
            loss_sum, grad_sum = carry
            l_k, (gw_k, gx_k) = one_microbatch(xs[0], xs[1])
            with _jax.named_scope("update"):
                return (loss_sum + l_k, _jax.tree.map(_jnp.add, grad_sum, gw_k)), gx_k

        init = (_jnp.zeros((), _jnp.float32), _jax.tree.map(_jnp.zeros_like, weights))
        (loss, grad_w), grad_x = _jax.lax.scan(body, init, (per_example, given["loss_target"]))
    with _jax.named_scope("update"):
        delta_w, new_m, new_v = {}, {}, {}
        for n in TWIN_WEIGHTS:
            delta_w[n], new_m[n], new_v[n] = _adamw(weights[n], grad_w[n], given["m_" + n], given["v_" + n])
    return (loss, grad_x, *[grad_w[n] for n in TWIN_WEIGHTS], *[delta_w[n] for n in TWIN_WEIGHTS],
            *[new_m[n] for n in TWIN_WEIGHTS], *[new_v[n] for n in TWIN_WEIGHTS])
```

```python
import functools
import math

import jax
import jax.numpy as jnp
from jax import lax
from jax.experimental import pallas as pl
from jax.experimental.pallas import tpu as pltpu

f32 = jnp.float32
bf16 = jnp.bfloat16

N_DEV = 8
D_MODEL = 1024
MLA_HEADS = 4
QK_NOPE = 128
QK_ROPE = 64
QK_HEAD = QK_NOPE + QK_ROPE
QK_PAD = 256
V_HEAD = 128
Q_LORA = 256
KV_LORA = 256
HG_HEADS = 4
HG_DK = 128
CHUNK = 64
D_FF = 2816
PLE_DIM = 256
ROPE_THETA = 10000.0
EPS = 1e-6
ATTN_SCALE = QK_HEAD ** -0.5
IN_SIZES = (256, 256, 64, 512, 512, 512, 512, 512)
D_IN = sum(IN_SIZES)
Z_HQ, Z_HFF, Z_HFB, Z_HI, Z_HG, Z_CQ, Z_CKV, Z_KR, Z_W = 0, 512, 1024, 1536, 2048, 2560, 2816, 3072, 3200

ADAM_LR, ADAM_B1, ADAM_B2, ADAM_EPS, ADAM_WD, ADAM_STEP = 0.001, 0.9, 0.999, 1e-08, 0.01, 10

LANES = 128
PACK_TILE = 1200
BIG = ("w_in", "w_qb", "w_kvb", "w_o", "w_gate", "w_up", "w_down", "w_ple_gate", "w_ple_proj", "lb_param")
PART_ROWS = 16
BIG_ROWS = (3136, 192, 256, 1024, 2816, 2816, 2816, 1024, 256, PART_ROWS)
BIG_OFF = tuple(sum(BIG_ROWS[:i]) for i in range(len(BIG_ROWS)))
PACK_ROWS = 14400
SMALL = ("g_mix", "g_qa", "g_kva", "g_qn", "g_kn", "g_hgo", "g_ffn", "g_ple")
_GAIN_SHAPE = {"g_mix": (1024,), "g_qa": (256,), "g_kva": (256,), "g_qn": (192,), "g_kn": (192,), "g_hgo": (512,),
               "g_ffn": (1024,), "g_ple": (1024,)}
SMALL_ROWS = (PART_ROWS,) * len(SMALL)
SMALL_OFF = tuple(sum(SMALL_ROWS[:i]) for i in range(len(SMALL_ROWS)))
SMALL_LOSS_ROW = sum(SMALL_ROWS)
SMALL_PACK_ROWS = SMALL_LOSS_ROW + PART_ROWS

VMEM_LIMIT = 56 * 1024 * 1024
MESH = pl.DeviceIdType.MESH


def _cp(sem=None, vmem=None):
    return pltpu.CompilerParams(dimension_semantics=sem, vmem_limit_bytes=vmem)


def _const_spec(shape):
    nd = len(shape)
    return pl.BlockSpec(shape, lambda *_: (0,) * nd, pipeline_mode=pl.Buffered(1))


def _acc_spec(shape):
    nd = len(shape)
    return pl.BlockSpec(shape, lambda *_: (0,) * nd)


def _sigmoid(x):
    return jax.nn.sigmoid(x)


def _dot(a, b):
    return jnp.dot(a, b, preferred_element_type=f32)


def _dot_nt(a, b):
    return lax.dot_general(a, b, (((1,), (1,)), ((), ())), preferred_element_type=f32)


def _dot_tn(a, b):
    return lax.dot_general(a, b, (((0,), (0,)), ((), ())), preferred_element_type=f32)


def _rms_fwd(x, g, width):
    r = lax.rsqrt(jnp.sum(x * x, axis=-1, keepdims=True) * (1.0 / width) + EPS)
    return x * r * g, r


def _rms_bwd(dy, x, r, g, width):
    u = dy * g
    dx = r * u - x * (r * r * r) * (jnp.sum(u * x, axis=-1, keepdims=True) * (1.0 / width))
    return dx, dy * x * r


def _rope(b, c, sa, sb):
    return b * c + pltpu.roll(b, 32, 1) * sa + pltpu.roll(b, 96, 1) * sb


def _all_gather(x_shard, name):
    rows, cols = x_shard.shape

    def body(x_ref, out_ref, send_sems, recv_sems, local_sem):
        x, y, c = lax.axis_index("x"), lax.axis_index("y"), lax.axis_index("c")
        me, sibling = (x, y, c), (x, y, 1 - c)
        chips = [(1 - x, y), (x, 1 - y), (1 - x, 1 - y)]

        def slot(px, py, pc):
            return out_ref.at[4 * px + 2 * py + pc]

        def copy(k, block, to, src=None):
            return pltpu.make_async_remote_copy(
                src_ref=slot(*block) if src is None else src, dst_ref=slot(*block),
                send_sem=send_sems.at[k], recv_sem=recv_sems.at[k], device_id=to, device_id_type=MESH)

        mine = pltpu.make_async_copy(x_ref, slot(*me), local_sem)
        mine.start()
        first = [copy(0, me, sibling, src=x_ref)]
        first += [copy(1 + j, me, (*chip, c), src=x_ref) for j, chip in enumerate(chips)]
        for cp in first:
            cp.start()
        passed = [copy(4 + j, (*chip, c), sibling) for j, chip in enumerate(chips)]
        for j, chip in enumerate(chips):
            copy(1 + j, (*chip, c), me).wait_recv()
            passed[j].start()
        copy(0, sibling, me).wait_recv()
        for j, chip in enumerate(chips):
            copy(4 + j, (*chip, 1 - c), me).wait_recv()
        for cp in first + passed:
            cp.wait_send()
        mine.wait()

    return pl.pallas_call(
        body, name=name,
        out_shape=jax.ShapeDtypeStruct((N_DEV, rows, cols), x_shard.dtype),
        in_specs=[pl.BlockSpec(memory_space=pl.ANY)],
        out_specs=pl.BlockSpec(memory_space=pl.ANY),
        scratch_shapes=[pltpu.SemaphoreType.DMA((7,)), pltpu.SemaphoreType.DMA((7,)), pltpu.SemaphoreType.DMA],
    )(x_shard)


def _sibling_exchange(g4):
    n_chip, _, rows, cols = g4.shape

    def body(g_ref, out_ref, send_sem, recv_sem):
        x, y, c = lax.axis_index("x"), lax.axis_index("y"), lax.axis_index("c")
        cp = pltpu.make_async_remote_copy(
            src_ref=g_ref.at[:, pl.ds(1 - c, 1)], dst_ref=out_ref,
            send_sem=send_sem, recv_sem=recv_sem, device_id=(x, y, 1 - c), device_id_type=MESH)
        cp.start()
        cp.wait()

    return pl.pallas_call(
        body, name="rs_sibling_exchange",
        out_shape=jax.ShapeDtypeStruct((n_chip, 1, rows, cols), g4.dtype),
        in_specs=[pl.BlockSpec(memory_space=pl.ANY)],
        out_specs=pl.BlockSpec(memory_space=pl.ANY),
        scratch_shapes=[pltpu.SemaphoreType.DMA, pltpu.SemaphoreType.DMA],
    )(g4)


def _chip_partials(g4, from_sibling, c_idx):
    n_chip, _, rows, cols = g4.shape

    def body(c_ref, a_ref, b_ref, o_ref):
        o_ref[...] = (a_ref[...] + b_ref[...]).astype(bf16)

    blk = (None, None, PACK_TILE, cols)
    return pl.pallas_call(
        body, name="rs_chip_partials",
        grid_spec=pltpu.PrefetchScalarGridSpec(
            num_scalar_prefetch=1, grid=(n_chip, rows // PACK_TILE),
            in_specs=[pl.BlockSpec(blk, lambda k, i, c: (k, c[0], i, 0)),
                      pl.BlockSpec(blk, lambda k, i, c: (k, 0, i, 0))],
            out_specs=pl.BlockSpec((None, PACK_TILE, cols), lambda k, i, c: (k, i, 0))),
        out_shape=jax.ShapeDtypeStruct((n_chip, rows, cols), bf16),
        compiler_params=_cp(("parallel", "parallel")),
    )(c_idx, g4, from_sibling)


def _chip_exchange(partials):
    _, rows, cols = partials.shape

    def body(p_ref, out_ref, send_sems, recv_sems):
        x, y, c = lax.axis_index("x"), lax.axis_index("y"), lax.axis_index("c")
        chips = [(1 - x, y), (x, 1 - y), (1 - x, 1 - y)]
        cps = [pltpu.make_async_remote_copy(
            src_ref=p_ref.at[2 * px + py], dst_ref=out_ref.at[k],
            send_sem=send_sems.at[k], recv_sem=recv_sems.at[k], device_id=(px, py, c), device_id_type=MESH)
            for k, (px, py) in enumerate(chips)]
        for cp in cps:
            cp.start()
        for cp in cps:
            cp.wait()

    return pl.pallas_call(
        body, name="rs_chip_exchange",
        out_shape=jax.ShapeDtypeStruct((3, rows, cols), partials.dtype),
        in_specs=[pl.BlockSpec(memory_space=pl.ANY)],
        out_specs=pl.BlockSpec(memory_space=pl.ANY),
        scratch_shapes=[pltpu.SemaphoreType.DMA((3,)), pltpu.SemaphoreType.DMA((3,))],
    )(partials)


def _adam_math(w, g, m, v):
    m = ADAM_B1 * m + (1.0 - ADAM_B1) * g
    v = ADAM_B2 * v + (1.0 - ADAM_B2) * (g * g)
    m_hat = m / (1.0 - ADAM_B1 ** ADAM_STEP)
    v_hat = v / (1.0 - ADAM_B2 ** ADAM_STEP)
    delta = -ADAM_LR * (m_hat / (jnp.sqrt(v_hat) + ADAM_EPS) + ADAM_WD * w)
    return delta, m, v


def _adam_big(idx, g4, from_sibling, from_chips, w, m, v):
    rows, cols = w.shape

    def body(i_ref, g_ref, a_ref, b_ref, w_ref, m_ref, v_ref, go_ref, d_ref, mo_ref, vo_ref):
        g = g_ref[...] + a_ref[...]
        for k in range(3):
            g = g + b_ref[k].astype(f32)
        delta, m_new, v_new = _adam_math(w_ref[...], g, m_ref[...], v_ref[...])
        go_ref[...] = g
        d_ref[...] = delta
        mo_ref[...] = m_new
        vo_ref[...] = v_new

    blk4 = (None, None, PACK_TILE, cols)
    row = pl.BlockSpec((PACK_TILE, cols), lambda i, s: (i, 0))
    out = jax.ShapeDtypeStruct((rows, cols), f32)
    return pl.pallas_call(
        body, name="adamw_shards",
        grid_spec=pltpu.PrefetchScalarGridSpec(
            num_scalar_prefetch=1, grid=(rows // PACK_TILE,),
            in_specs=[pl.BlockSpec(blk4, lambda i, s: (s[1], s[0], i, 0)),
                      pl.BlockSpec(blk4, lambda i, s: (s[1], 0, i, 0)),
                      pl.BlockSpec((3, PACK_TILE, cols), lambda i, s: (0, i, 0)),
                      row, row, row],
            out_specs=[row, row, row, row]),
        out_shape=[out, out, out, out],
        compiler_params=_cp(("parallel",)),
    )(idx, g4, from_sibling, from_chips, w, m, v)


def _adam_small(parts, w, m, v):
    def body(p_ref, w_ref, m_ref, v_ref, go_ref, d_ref, mo_ref, vo_ref):
        g = p_ref[0]
        for k in range(1, N_DEV):
            g = g + p_ref[k]
        delta, m_new, v_new = _adam_math(w_ref[...], g, m_ref[...], v_ref[...])
        go_ref[...] = g
        d_ref[...] = delta
        mo_ref[...] = m_new
        vo_ref[...] = v_new

    out = jax.ShapeDtypeStruct(w.shape, f32)
    return pl.pallas_call(body, name="adamw_gains", out_shape=[out, out, out, out])(parts, w, m, v)


def _fwd_in(x, g_mix, wz, tm):
    s, d = x.shape

    def body(x_ref, g_ref, w_ref, h_ref, z_ref):
        h, _ = _rms_fwd(x_ref[...], g_ref[...], d)
        hb = h.astype(bf16)
        h_ref[...] = hb
        z_ref[...] = _dot(hb, w_ref[...])

    return pl.pallas_call(
        body, name="fwd_in", grid=(s // tm,),
        in_specs=[pl.BlockSpec((tm, d), lambda i: (i, 0)), _const_spec((1, d)), _const_spec((d, Z_W))],
        out_specs=[pl.BlockSpec((tm, d), lambda i: (i, 0)), pl.BlockSpec((tm, Z_W), lambda i: (i, 0))],
        out_shape=[jax.ShapeDtypeStruct((s, d), bf16), jax.ShapeDtypeStruct((s, Z_W), f32)],
        compiler_params=_cp(("parallel",), VMEM_LIMIT),
    )(x, g_mix, wz)


def _mla_qk_fwd(cq, ckv, kr, g_qa, g_kva, wqb, wkvb, g_qn, g_kn):
    cqn, rq = _rms_fwd(cq, g_qa, Q_LORA)
    ckvn, rkv = _rms_fwd(ckv, g_kva, KV_LORA)
    cqn_b, ckvn_b = cqn.astype(bf16), ckvn.astype(bf16)
    q0 = _dot(cqn_b, wqb)
    kv0 = _dot(ckvn_b, wkvb)
    return cqn_b, rq, ckvn_b, rkv, q0, kv0


def _fwd_mla_proj(z, cosb, sina, sinb, g_qa, g_kva, wqb, wkvb, g_qn, g_kn, tm):
    s = z.shape[0]
    hh = MLA_HEADS

    def body(cq_ref, ckv_ref, kr_ref, c_ref, sa_ref, sb_ref, gqa_ref, gkva_ref, wqb_ref, wkvb_ref, gqn_ref, gkn_ref,
             q_ref, k_ref, v_ref):
        _, _, _, _, q0, kv0 = _mla_qk_fwd(cq_ref[...], ckv_ref[...], kr_ref[...], gqa_ref[...], gkva_ref[...],
                                          wqb_ref[...], wkvb_ref[...], gqn_ref[...], gkn_ref[...])
        kr = kr_ref[...]
        c, sa, sb = c_ref[...], sa_ref[...], sb_ref[...]
        gqn, gkn = gqn_ref[...], gkn_ref[...]
        kr_sq = jnp.sum(kr * kr, axis=-1, keepdims=True)
        for h in range(hh):
            qh = q0[:, QK_PAD * h:QK_PAD * (h + 1)]
            qn, _ = _rms_fwd(qh, gqn, QK_HEAD)
            q_ref[h, :, 0:128] = qn[:, 0:128].astype(bf16)
            q_ref[h, :, 128:256] = _rope(qn[:, 128:256], c, sa, sb).astype(bf16)
            kn_ = kv0[:, 256 * h:256 * h + 128]
            rk = lax.rsqrt((jnp.sum(kn_ * kn_, axis=-1, keepdims=True) + kr_sq) * (1.0 / QK_HEAD) + EPS)
            k_ref[h, :, 0:128] = (kn_ * rk * gkn[:, 0:128]).astype(bf16)
            k_ref[h, :, 128:256] = _rope(kr * rk * gkn[:, 128:256], c, sa, sb).astype(bf16)
            v_ref[h] = kv0[:, 256 * h + 128:256 * h + 256].astype(bf16)

    row128 = pl.BlockSpec((tm, 128), lambda i: (i, 0))
    return pl.pallas_call(
        body, name="fwd_mla_proj", grid=(s // tm,),
        in_specs=[pl.BlockSpec((tm, 256), lambda i: (i, Z_CQ // 256)), pl.BlockSpec((tm, 256), lambda i: (i, Z_CKV // 256)),
                  pl.BlockSpec((tm, 128), lambda i: (i, Z_KR // 128)), row128, row128, row128,
                  _const_spec((1, 256)), _const_spec((1, 256)), _const_spec((256, 1024)), _const_spec((256, 1024)),
                  _const_spec((1, 256)), _const_spec((1, 256))],
        out_specs=[pl.BlockSpec((hh, tm, QK_PAD), lambda i: (0, i, 0)), pl.BlockSpec((hh, tm, QK_PAD), lambda i: (0, i, 0)),
                   pl.BlockSpec((hh, tm, V_HEAD), lambda i: (0, i, 0))],
        out_shape=[jax.ShapeDtypeStruct((hh, s, QK_PAD), bf16), jax.ShapeDtypeStruct((hh, s, QK_PAD), bf16),
                   jax.ShapeDtypeStruct((hh, s, V_HEAD), bf16)],
        compiler_params=_cp(("parallel",), VMEM_LIMIT),
    )(z, z, z, cosb, sina, sinb, g_qa, g_kva, wqb, wkvb, g_qn, g_kn)


def _fwd_attn(q, k, v, tq):
    hh, s, _ = q.shape

    def body(q_ref, k_ref, v_ref, o_ref):
        sc = _dot_nt(q_ref[...], k_ref[...]) * ATTN_SCALE
        p = jnp.exp(sc - jnp.max(sc, axis=-1, keepdims=True))
        l = jnp.sum(p, axis=-1, keepdims=True)
        o_ref[...] = (_dot(p.astype(bf16), v_ref[...]) * (1.0 / l)).astype(bf16)

    return pl.pallas_call(
        body, name="fwd_attn", grid=(hh, s // tq),
        in_specs=[pl.BlockSpec((None, tq, QK_PAD), lambda h, i: (h, i, 0)),
                  pl.BlockSpec((None, s, QK_PAD), lambda h, i: (h, 0, 0)),
                  pl.BlockSpec((None, s, V_HEAD), lambda h, i: (h, 0, 0))],
        out_specs=pl.BlockSpec((tq, V_HEAD), lambda h, i: (i, h)),
        out_shape=jax.ShapeDtypeStruct((s, hh * V_HEAD), bf16),
        compiler_params=_cp(("parallel", "parallel"), VMEM_LIMIT),
    )(q, k, v)


def _split3(x):
    hi = x.astype(bf16)
    r1 = x - hi.astype(f32)
    mid = r1.astype(bf16)
    lo = (r1 - mid.astype(f32)).astype(bf16)
    return jnp.concatenate([hi, mid, lo], axis=-1)


def _tri_sum(tri, x):
    y = _dot(tri, _split3(x))
    return y[:, 0:128] + y[:, 128:256] + y[:, 256:384]


def _gla_masks(rev):
    row = lax.broadcasted_iota(jnp.int32, (CHUNK, CHUNK), 0)
    col = lax.broadcasted_iota(jnp.int32, (CHUNK, CHUNK), 1)
    diff = (row - col) * jnp.where(rev, -1, 1)
    return (diff >= 0).astype(f32), (diff >= 0).astype(bf16), (diff <= 0).astype(bf16)


def _gla_gates(hq, hf, lower):
    sg = _sigmoid(hf)
    f = lower + (1.0 - lower) * sg
    return hq * _sigmoid(hq), 1.0 - f, jnp.log(f), f, sg


def _gla_chunk(q, k, logf, rev, tri):
    b = _tri_sum(tri, logf)
    b_last = jnp.where(rev, b[0:1, :], b[CHUNK - 1:CHUNK, :])
    b_mid = jnp.where(rev, b[CHUNK // 2:CHUNK // 2 + 1, :], b[CHUNK // 2 - 1:CHUNK // 2, :])
    return b, b_last, b_mid


def _fwd_gla(z, lb4):
    s = z.shape[0]
    n_chunks = s // CHUNK

    def body(hq_ref, hf_ref, hi_ref, lb_ref, o_ref, st_ref):
        d = pl.program_id(1)
        rev = d == 1
        lower = _sigmoid(lb_ref[pl.ds(d, 1), :] - lb_ref[pl.ds(2 + d, 1), :])
        maskf, tri, _ = _gla_masks(rev)
        st_ref[...] = jnp.zeros_like(st_ref)

        def step(n, carry):
            ne = jnp.where(rev, n_chunks - 1 - n, n)
            rows = pl.ds(pl.multiple_of(ne * CHUNK, CHUNK), CHUNK)
            q, k, logf, _, _ = _gla_gates(hq_ref[rows, :], hf_ref[rows, :], lower)
            v = hi_ref[rows, :]
            b, b_last, b_mid = _gla_chunk(q, k, logf, rev, tri)
            qi = (q * jnp.exp(b - b_mid)).astype(bf16)
            ki = (k * jnp.exp(b_mid - b)).astype(bf16)
            vb = v.astype(bf16)
            a = (_dot_nt(qi, ki) * maskf).astype(bf16)
            st = st_ref[...]
            o = _dot(a, vb) + _dot_nt((q * jnp.exp(b)).astype(bf16), st.astype(bf16))
            kt = (k * jnp.exp(b_last - b)).astype(bf16)
            st_ref[...] = st * jnp.exp(b_last) + _dot_tn(vb, kt)

            @pl.when(d == 0)
            def _():
                o_ref[rows, :] = o

            @pl.when(d == 1)
            def _():
                o_ref[rows, :] += o
            return carry

        lax.fori_loop(0, n_chunks, step, 0)

    col = lambda base: pl.BlockSpec((s, 128), lambda h, d: (0, base // 128 + h))
    return pl.pallas_call(
        body, name="fwd_gla", grid=(HG_HEADS, 2),
        in_specs=[col(Z_HQ), pl.BlockSpec((s, 128), lambda h, d: (0, Z_HFF // 128 + 4 * d + h)), col(Z_HI),
                  pl.BlockSpec((4, 128), lambda h, d: (0, h))],
        out_specs=pl.BlockSpec((s, 128), lambda h, d: (0, h)),
        out_shape=jax.ShapeDtypeStruct((s, HG_HEADS * 128), f32),
        scratch_shapes=[pltpu.VMEM((128, 128), f32)],
        compiler_params=_cp(("parallel", "arbitrary"), VMEM_LIMIT),
    )(z, z, z, lb4)


def _hg_out(o, hg, g_hgo):
    outs, ons, rs = [], [], []
    for h in range(HG_HEADS):
        oh = o[:, 128 * h:128 * (h + 1)]
        on, r = _rms_fwd(oh, g_hgo[:, 128 * h:128 * (h + 1)], 128)
        ons.append(on)
        rs.append(r)
    on = jnp.concatenate(ons, axis=-1)
    sg = _sigmoid(hg)
    return on * (hg * sg), on, rs, sg


def _fwd_mix(a, o, z, g_hgo, x, w_o, tm):
    s, d = x.shape

    def body(a_ref, o_ref, hg_ref, g_ref, x_ref, w_ref, x2_ref, cat_ref):
        r, _, _, _ = _hg_out(o_ref[...], hg_ref[...], g_ref[...])
        cat = jnp.concatenate([a_ref[...], r.astype(bf16)], axis=-1)
        cat_ref[...] = cat
        x2_ref[...] = x_ref[...] + _dot(cat, w_ref[...])

    row512 = pl.BlockSpec((tm, 512), lambda i: (i, 0))
    rowd = pl.BlockSpec((tm, d), lambda i: (i, 0))
    return pl.pallas_call(
        body, name="fwd_mix", grid=(s // tm,),
        in_specs=[row512, row512, pl.BlockSpec((tm, 512), lambda i: (i, Z_HG // 512)), _const_spec((1, 512)), rowd,
                  _const_spec((d, d))],
        out_specs=[rowd, rowd],
        out_shape=[jax.ShapeDtypeStruct((s, d), f32), jax.ShapeDtypeStruct((s, d), bf16)],
        compiler_params=_cp(("parallel",), VMEM_LIMIT),
    )(a, o, z, g_hgo, x, w_o)


def _fwd_ffn(x2, g_ffn, w_gate, w_up, w_down, tm):
    s, d = x2.shape

    def body(x_ref, g_ref, wg_ref, wu_ref, wd_ref, x3_ref, gp_ref, up_ref):
        x = x_ref[...]
        h, _ = _rms_fwd(x, g_ref[...], d)
        hb = h.astype(bf16)
        gp = _dot(hb, wg_ref[...])
        up = _dot(hb, wu_ref[...])
        gp_ref[...] = gp
        up_ref[...] = up
        act = (gp * _sigmoid(gp) * up).astype(bf16)
        x3_ref[...] = x + _dot(act, wd_ref[...])

    rowd = pl.BlockSpec((tm, d), lambda i: (i, 0))
    rowf = pl.BlockSpec((tm, D_FF), lambda i: (i, 0))
    return pl.pallas_call(
        body, name="fwd_ffn", grid=(s // tm,),
        in_specs=[rowd, _const_spec((1, d)), _const_spec((d, D_FF)), _const_spec((d, D_FF)), _const_spec((D_FF, d))],
        out_specs=[rowd, rowf, rowf],
        out_shape=[jax.ShapeDtypeStruct((s, d), f32), jax.ShapeDtypeStruct((s, D_FF), f32),
                   jax.ShapeDtypeStruct((s, D_FF), f32)],
        compiler_params=_cp(("parallel",), VMEM_LIMIT),
    )(x2, g_ffn, w_gate, w_up, w_down)


def _ple_loss_fwd_bwd(x3, g_ple, w_pg, p, w_pp, target, tm):
    s, d = x3.shape

    def body(x_ref, g_ref, wg_ref, p_ref, wp_ref, t_ref, dx_ref, h_ref, dpre_ref, dpp_ref, dg_ref, loss_ref):
        @pl.when(pl.program_id(0) == 0)
        def _():
            dg_ref[...] = jnp.zeros_like(dg_ref)
            loss_ref[...] = jnp.zeros_like(loss_ref)

        x = x_ref[...]
        g = g_ref[...]
        h, r = _rms_fwd(x, g, d)
        hb = h.astype(bf16)
        gate = _sigmoid(_dot(hb, wg_ref[...]))
        pp = _dot(p_ref[...].astype(bf16), wp_ref[...])
        e = x + gate * pp - t_ref[...]
        loss_ref[...] += 0.5 * jnp.sum(e * e) * (1.0 / d)
        dy = e * (1.0 / d)
        dpre = (dy * pp * gate * (1.0 - gate)).astype(bf16)
        dx, dgx = _rms_bwd(_dot_nt(dpre, wg_ref[...]), x, r, g, d)
        dx_ref[...] = dy + dx
        dg_ref[...] += jnp.sum(dgx, axis=0, keepdims=True)
        h_ref[...] = hb
        dpre_ref[...] = dpre
        dpp_ref[...] = (dy * gate).astype(bf16)

    rowd = pl.BlockSpec((tm, d), lambda i: (i, 0))
    return pl.pallas_call(
        body, name="ple_loss_fwd_bwd", grid=(s // tm,),
        in_specs=[rowd, _const_spec((1, d)), _const_spec((d, d)), pl.BlockSpec((tm, PLE_DIM), lambda i: (i, 0)),
                  _const_spec((PLE_DIM, d)), rowd],
        out_specs=[rowd, rowd, rowd, rowd, _acc_spec((1, d)), _acc_spec((8, 128))],
        out_shape=[jax.ShapeDtypeStruct((s, d), f32), jax.ShapeDtypeStruct((s, d), bf16), jax.ShapeDtypeStruct((s, d), bf16),
                   jax.ShapeDtypeStruct((s, d), bf16), jax.ShapeDtypeStruct((1, d), f32), jax.ShapeDtypeStruct((8, 128), f32)],
        compiler_params=_cp(("arbitrary",), VMEM_LIMIT),
    )(x3, g_ple, w_pg, p, w_pp, target)


def _bwd_ffn(d3, x2, gp, up, g_ffn, w_gate, w_up, w_down, tm):
    s, d = x2.shape

    def body(d3_ref, x_ref, gp_ref, up_ref, g_ref, wg_ref, wu_ref, wd_ref, d2_ref, h_ref, act_ref, dgu_ref, dg_ref):
        @pl.when(pl.program_id(0) == 0)
        def _():
            dg_ref[...] = jnp.zeros_like(dg_ref)

        x = x_ref[...]
        g = g_ref[...]
        d3 = d3_ref[...]
        h, r = _rms_fwd(x, g, d)
        h_ref[...] = h.astype(bf16)
        gp, up = gp_ref[...], up_ref[...]
        sg = _sigmoid(gp)
        silu = gp * sg
        act_ref[...] = (silu * up).astype(bf16)
        dact = _dot_nt(d3.astype(bf16), wd_ref[...])
        dgp = (dact * up * (sg * (1.0 + gp * (1.0 - sg)))).astype(bf16)
        dup = (dact * silu).astype(bf16)
        dgu_ref[:, 0:D_FF] = dgp
        dgu_ref[:, D_FF:2 * D_FF] = dup
        dh = _dot_nt(dgp, wg_ref[...]) + _dot_nt(dup, wu_ref[...])
        dx, dgx = _rms_bwd(dh, x, r, g, d)
        d2_ref[...] = d3 + dx
        dg_ref[...] += jnp.sum(dgx, axis=0, keepdims=True)

    rowd = pl.BlockSpec((tm, d), lambda i: (i, 0))
    rowf = pl.BlockSpec((tm, D_FF), lambda i: (i, 0))
    return pl.pallas_call(
        body, name="bwd_ffn", grid=(s // tm,),
        in_specs=[rowd, rowd, rowf, rowf, _const_spec((1, d)), _const_spec((d, D_FF)), _const_spec((d, D_FF)),
                  _const_spec((D_FF, d))],
        out_specs=[rowd, rowd, rowf, pl.BlockSpec((tm, 2 * D_FF), lambda i: (i, 0)), _acc_spec((1, d))],
        out_shape=[jax.ShapeDtypeStruct((s, d), f32), jax.ShapeDtypeStruct((s, d), bf16), jax.ShapeDtypeStruct((s, D_FF), bf16),
                   jax.ShapeDtypeStruct((s, 2 * D_FF), bf16), jax.ShapeDtypeStruct((1, d), f32)],
        compiler_params=_cp(("arbitrary",), VMEM_LIMIT),
    )(d3, x2, gp, up, g_ffn, w_gate, w_up, w_down)


def _bwd_mix(d2, w_o, o, z, g_hgo, tm):
    s, d = d2.shape

    def body(d2_ref, w_ref, o_ref, hg_ref, g_ref, da_ref, do_ref, dhg_ref, dg_ref):
        @pl.when(pl.program_id(0) == 0)
        def _():
            dg_ref[...] = jnp.zeros_like(dg_ref)

        dcat = _dot_nt(d2_ref[...].astype(bf16), w_ref[...])
        da_ref[...] = dcat[:, 0:512].astype(bf16)
        dr = dcat[:, 512:1024]
        o, hg, g = o_ref[...], hg_ref[...], g_ref[...]
        _, on, rs, sg = _hg_out(o, hg, g)
        dhg_ref[...] = (dr * on * (sg * (1.0 + hg * (1.0 - sg)))).astype(bf16)
        don = dr * (hg * sg)
        dgs = []
        for h in range(HG_HEADS):
            cols = slice(128 * h, 128 * (h + 1))
            dx, dgx = _rms_bwd(don[:, cols], o[:, cols], rs[h], g[:, cols], 128)
            do_ref[:, cols] = dx
            dgs.append(jnp.sum(dgx, axis=0, keepdims=True))
        dg_ref[...] += jnp.concatenate(dgs, axis=-1)

    row512 = pl.BlockSpec((tm, 512), lambda i: (i, 0))
    return pl.pallas_call(
        body, name="bwd_mix", grid=(s // tm,),
        in_specs=[pl.BlockSpec((tm, d), lambda i: (i, 0)), _const_spec((d, d)), row512,
                  pl.BlockSpec((tm, 512), lambda i: (i, Z_HG // 512)), _const_spec((1, 512))],
        out_specs=[row512, row512, row512, _acc_spec((1, 512))],
        out_shape=[jax.ShapeDtypeStruct((s, 512), bf16), jax.ShapeDtypeStruct((s, 512), f32), jax.ShapeDtypeStruct((s, 512), bf16),
                   jax.ShapeDtypeStruct((1, 512), f32)],
        compiler_params=_cp(("arbitrary",), VMEM_LIMIT),
    )(d2, w_o, o, z, g_hgo)


def _bwd_gla(z, lb4, do):
    s = z.shape[0]
    n_chunks = s // CHUNK

    def body(hq_ref, hf_ref, hi_ref, lb_ref, do_ref, dhq_ref, dhf_ref, dhi_ref, dlb_ref, st_all, b_all, dst_ref, dq_acc, dv_acc, dlow_ref):
        d = pl.program_id(1)
        rev = d == 1
        lower = _sigmoid(lb_ref[pl.ds(d, 1), :] - lb_ref[pl.ds(2 + d, 1), :])
        maskf, tri, tri_t = _gla_masks(rev)

        def rows_of(n):
            ne = jnp.where(rev, n_chunks - 1 - n, n)
            return pl.ds(pl.multiple_of(ne * CHUNK, CHUNK), CHUNK)

        def fwd_step(n, st):
            rows = rows_of(n)
            _, k, logf, _, _ = _gla_gates(hq_ref[rows, :], hf_ref[rows, :], lower)
            b, b_last, _ = _gla_chunk(None, k, logf, rev, tri)
            b_all[rows, :] = b
            st_all[n] = st
            kt = (k * jnp.exp(b_last - b)).astype(bf16)
            return st * jnp.exp(b_last) + _dot_tn(hi_ref[rows, :].astype(bf16), kt)

        lax.fori_loop(0, n_chunks, fwd_step, jnp.zeros((128, 128), f32))

        dst_ref[...] = jnp.zeros_like(dst_ref)
        dlow_ref[...] = jnp.zeros_like(dlow_ref)

        def bwd_step(j, carry):
            n = n_chunks - 1 - j
            rows = rows_of(n)
            hq, hf = hq_ref[rows, :], hf_ref[rows, :]
            q, k, _, f, sg = _gla_gates(hq, hf, lower)
            v = hi_ref[rows, :]
            dout = do_ref[rows, :]
            b = b_all[rows, :]
            b_last = jnp.where(rev, b[0:1, :], b[CHUNK - 1:CHUNK, :])
            b_mid = jnp.where(rev, b[CHUNK // 2:CHUNK // 2 + 1, :], b[CHUNK // 2 - 1:CHUNK // 2, :])
            e1, e2, e3, e4 = jnp.exp(b - b_mid), jnp.exp(b_mid - b), jnp.exp(b_last - b), jnp.exp(b)
            decay = jnp.exp(b_last)
            qi, ki, kt, qt = q * e1, k * e2, k * e3, q * e4
            qib, kib, ktb, qtb = qi.astype(bf16), ki.astype(bf16), kt.astype(bf16), qt.astype(bf16)
            vb, dob = v.astype(bf16), dout.astype(bf16)
            st = st_all[n]
            dst = dst_ref[...]
            stb, dstb = st.astype(bf16), dst.astype(bf16)
            a = (_dot_nt(qib, kib) * maskf).astype(bf16)
            dv = _dot_tn(a, dob) + _dot_nt(ktb, dstb)
            da = (_dot_nt(dob, vb) * maskf).astype(bf16)
            dqi = _dot(da, kib)
            dki = _dot_tn(da, qib)
            dqt = _dot(dob, stb)
            dkt = _dot(vb, dstb)
            ddecay = jnp.sum(dst * st, axis=0, keepdims=True)
            dq = dqi * e1 + dqt * e4
            dk = dki * e2 + dkt * e3
            db = dqi * qi - dki * ki + dqt * qt - dkt * kt
            dlast = jnp.sum(dkt * kt, axis=0, keepdims=True) + ddecay * decay
            dlogf = _tri_sum(tri_t, db) + dlast
            dst_ref[...] = dst * decay + _dot_tn(dob, qtb)
            df = dlogf / f - dk
            dhf_ref[rows, :] = (df * (1.0 - lower) * sg * (1.0 - sg)).astype(bf16)
            dlow_ref[...] += jnp.sum(df * (1.0 - sg), axis=0, keepdims=True)
            sq = _sigmoid(hq)
            dhq = dq * (sq * (1.0 + hq * (1.0 - sq)))

            @pl.when(d == 0)
            def _():
                dq_acc[rows, :] = dhq
                dv_acc[rows, :] = dv

            @pl.when(d == 1)
            def _():
                dhq_ref[rows, :] = (dq_acc[rows, :] + dhq).astype(bf16)
                dhi_ref[rows, :] = (dv_acc[rows, :] + dv).astype(bf16)
            return carry

        lax.fori_loop(0, n_chunks, bwd_step, 0)

        dl = dlow_ref[...] * lower * (1.0 - lower)

        @pl.when(d == 0)
        def _():
            dlb_ref[...] = jnp.zeros_like(dlb_ref)
        dlb_ref[pl.ds(d, 1), :] = dl
        dlb_ref[pl.ds(2 + d, 1), :] = -dl

    col = lambda base: pl.BlockSpec((s, 128), lambda h, d: (0, base // 128 + h))
    out_col = lambda base: pl.BlockSpec((s, 128), lambda h, d: (0, base // 128 + h))
    return pl.pallas_call(
        body, name="bwd_gla", grid=(HG_HEADS, 2),
        in_specs=[col(Z_HQ), pl.BlockSpec((s, 128), lambda h, d: (0, Z_HFF // 128 + 4 * d + h)), col(Z_HI),
                  pl.BlockSpec((4, 128), lambda h, d: (0, h)), pl.BlockSpec((s, 128), lambda h, d: (0, h))],
        out_specs=[pl.BlockSpec((s, 128), lambda h, d: (0, h)), pl.BlockSpec((s, 128), lambda h, d: (0, 4 * d + h)),
                   pl.BlockSpec((s, 128), lambda h, d: (0, h)), pl.BlockSpec((4, 128), lambda h, d: (0, h))],
        out_shape=[jax.ShapeDtypeStruct((s, 512), bf16), jax.ShapeDtypeStruct((s, 1024), bf16),
                   jax.ShapeDtypeStruct((s, 512), bf16), jax.ShapeDtypeStruct((4, 512), f32)],
        scratch_shapes=[pltpu.VMEM((n_chunks, 128, 128), f32), pltpu.VMEM((s, 128), f32), pltpu.VMEM((128, 128), f32),
                        pltpu.VMEM((s, 128), f32), pltpu.VMEM((s, 128), f32), pltpu.VMEM((1, 128), f32)],
        compiler_params=_cp(("parallel", "arbitrary"), VMEM_LIMIT),
    )(z, z, z, lb4, do)


def _bwd_attn(q, k, v, da, tq):
    hh, s, _ = q.shape

    def body(q_ref, k_ref, v_ref, do_ref, dq_ref, dk_ref, dv_ref):
        @pl.when(pl.program_id(1) == 0)
        def _():
            dk_ref[...] = jnp.zeros_like(dk_ref)
            dv_ref[...] = jnp.zeros_like(dv_ref)

        qb, kb, vb, dob = q_ref[...], k_ref[...], v_ref[...], do_ref[...]
        sc = _dot_nt(qb, kb) * ATTN_SCALE
        p = jnp.exp(sc - jnp.max(sc, axis=-1, keepdims=True))
        w = p * (1.0 / jnp.sum(p, axis=-1, keepdims=True))
        dw = _dot_nt(dob, vb)
        ds = (w * (dw - jnp.sum(dw * w, axis=-1, keepdims=True)) * ATTN_SCALE).astype(bf16)
        dq_ref[...] = _dot(ds, kb)
        dk_ref[...] += _dot_tn(ds, qb)
        dv_ref[...] += _dot_tn(w.astype(bf16), dob)

    return pl.pallas_call(
        body, name="bwd_attn", grid=(hh, s // tq),
        in_specs=[pl.BlockSpec((None, tq, QK_PAD), lambda h, i: (h, i, 0)),
                  pl.BlockSpec((None, s, QK_PAD), lambda h, i: (h, 0, 0)),
                  pl.BlockSpec((None, s, V_HEAD), lambda h, i: (h, 0, 0)),
                  pl.BlockSpec((tq, V_HEAD), lambda h, i: (i, h))],
        out_specs=[pl.BlockSpec((None, tq, QK_PAD), lambda h, i: (h, i, 0)),
                   pl.BlockSpec((None, s, QK_PAD), lambda h, i: (h, 0, 0)),
                   pl.BlockSpec((None, s, V_HEAD), lambda h, i: (h, 0, 0))],
        out_shape=[jax.ShapeDtypeStruct((hh, s, QK_PAD), f32), jax.ShapeDtypeStruct((hh, s, QK_PAD), f32),
                   jax.ShapeDtypeStruct((hh, s, V_HEAD), f32)],
        compiler_params=_cp(("parallel", "arbitrary"), VMEM_LIMIT),
    )(q, k, v, da)


def _bwd_mla_proj(z, dq, dk, dv, cosb, sina, sinb, g_qa, g_kva, wqb, wkvb, g_qn, g_kn, tm):
    s = z.shape[0]
    hh = MLA_HEADS

    def body(cq_ref, ckv_ref, kr_ref, dq_ref, dk_ref, dv_ref, c_ref, sa_ref, sb_ref, gqa_ref, gkva_ref, wqb_ref, wkvb_ref,
             gqn_ref, gkn_ref, dz_ref, cqn_ref, ckvn_ref, dq0_ref, dkv0_ref, dgqa_ref, dgkva_ref, dgqn_ref, dgkn_ref):
        @pl.when(pl.program_id(0) == 0)
        def _():
            for r in (dgqa_ref, dgkva_ref, dgqn_ref, dgkn_ref):
                r[...] = jnp.zeros_like(r)

        cq, ckv, kr = cq_ref[...], ckv_ref[...], kr_ref[...]
        gqa, gkva, gqn, gkn = gqa_ref[...], gkva_ref[...], gqn_ref[...], gkn_ref[...]
        cqn_b, rq, ckvn_b, rkv, q0, kv0 = _mla_qk_fwd(cq, ckv, kr, gqa, gkva, wqb_ref[...], wkvb_ref[...], gqn, gkn)
        cqn_ref[...] = cqn_b
        ckvn_ref[...] = ckvn_b
        c, sa, sb = c_ref[...], -sa_ref[...], -sb_ref[...]
        kr_sq = jnp.sum(kr * kr, axis=-1, keepdims=True)
        dkr = jnp.zeros_like(kr)
        dgqn = jnp.zeros((1, QK_PAD), f32)
        dgkn = jnp.zeros((1, QK_PAD), f32)
        for h in range(hh):
            qh = q0[:, QK_PAD * h:QK_PAD * (h + 1)]
            rh = lax.rsqrt(jnp.sum(qh * qh, axis=-1, keepdims=True) * (1.0 / QK_HEAD) + EPS)
            dqh = dq_ref[h]
            dqn = jnp.concatenate([dqh[:, 0:128], _rope(dqh[:, 128:256], c, sa, sb)], axis=-1)
            dq0h, dgx = _rms_bwd(dqn, qh, rh, gqn, QK_HEAD)
            dq0_ref[:, QK_PAD * h:QK_PAD * (h + 1)] = dq0h.astype(bf16)
            dgqn = dgqn + jnp.sum(dgx, axis=0, keepdims=True)

            kn_ = kv0[:, 256 * h:256 * h + 128]
            k0 = jnp.concatenate([kn_, kr], axis=-1)
            rk = lax.rsqrt((jnp.sum(kn_ * kn_, axis=-1, keepdims=True) + kr_sq) * (1.0 / QK_HEAD) + EPS)
            dkh = dk_ref[h]
            dkn = jnp.concatenate([dkh[:, 0:128], _rope(dkh[:, 128:256], c, sa, sb)], axis=-1)
            dk0, dgx = _rms_bwd(dkn, k0, rk, gkn, QK_HEAD)
            dgkn = dgkn + jnp.sum(dgx, axis=0, keepdims=True)
            dkv0_ref[:, 256 * h:256 * h + 128] = dk0[:, 0:128].astype(bf16)
            dkv0_ref[:, 256 * h + 128:256 * h + 256] = dv_ref[h].astype(bf16)
            dkr = dkr + dk0[:, 128:256]
        dgqn_ref[...] += dgqn
        dgkn_ref[...] += dgkn
        dcq, dgx = _rms_bwd(_dot_nt(dq0_ref[...], wqb_ref[...]), cq, rq, gqa, Q_LORA)
        dgqa_ref[...] += jnp.sum(dgx, axis=0, keepdims=True)
        dckv, dgx = _rms_bwd(_dot_nt(dkv0_ref[...], wkvb_ref[...]), ckv, rkv, gkva, KV_LORA)
        dgkva_ref[...] += jnp.sum(dgx, axis=0, keepdims=True)
        dz_ref[:, 0:256] = dcq.astype(bf16)
        dz_ref[:, 256:512] = dckv.astype(bf16)
        dz_ref[:, 512:640] = dkr.astype(bf16)

    row128 = pl.BlockSpec((tm, 128), lambda i: (i, 0))
    row256 = pl.BlockSpec((tm, 256), lambda i: (i, 0))
    row1024 = pl.BlockSpec((tm, 1024), lambda i: (i, 0))
    hd = lambda w: pl.BlockSpec((hh, tm, w), lambda i: (0, i, 0))
    return pl.pallas_call(
        body, name="bwd_mla_proj", grid=(s // tm,),
        in_specs=[pl.BlockSpec((tm, 256), lambda i: (i, Z_CQ // 256)), pl.BlockSpec((tm, 256), lambda i: (i, Z_CKV // 256)),
                  pl.BlockSpec((tm, 128), lambda i: (i, Z_KR // 128)), hd(QK_PAD), hd(QK_PAD), hd(V_HEAD),
                  row128, row128, row128,
                  _const_spec((1, 256)), _const_spec((1, 256)), _const_spec((256, 1024)), _const_spec((256, 1024)),
                  _const_spec((1, 256)), _const_spec((1, 256))],
        out_specs=[pl.BlockSpec((tm, 640), lambda i: (i, 0)), row256, row256, row1024, row1024,
                   _acc_spec((1, 256)), _acc_spec((1, 256)), _acc_spec((1, 256)), _acc_spec((1, 256))],
        out_shape=[jax.ShapeDtypeStruct((s, 640), bf16), jax.ShapeDtypeStruct((s, 256), bf16), jax.ShapeDtypeStruct((s, 256), bf16),
                   jax.ShapeDtypeStruct((s, 1024), bf16), jax.ShapeDtypeStruct((s, 1024), bf16)]
        + [jax.ShapeDtypeStruct((1, 256), f32)] * 4,
        compiler_params=_cp(("arbitrary",), VMEM_LIMIT),
    )(z, z, z, dq, dk, dv, cosb, sina, sinb, g_qa, g_kva, wqb, wkvb, g_qn, g_kn)


def _bwd_in(segments, wz, x, g_mix, d2, tm):
    s, d = x.shape
    n_seg = len(segments)

    def body(*refs):
        dz_refs, w_refs = refs[:n_seg], refs[n_seg:2 * n_seg]
        x_ref, g_ref, d2_ref, gx_ref, dg_ref = refs[2 * n_seg:]

        @pl.when(pl.program_id(0) == 0)
        def _():
            dg_ref[...] = jnp.zeros_like(dg_ref)

        dh = _dot_nt(dz_refs[0][...], w_refs[0][...])
        for a_ref, w_ref in zip(dz_refs[1:], w_refs[1:]):
            dh = dh + _dot_nt(a_ref[...], w_ref[...])
        x, g = x_ref[...], g_ref[...]
        r = lax.rsqrt(jnp.sum(x * x, axis=-1, keepdims=True) * (1.0 / d) + EPS)
        dx, dgx = _rms_bwd(dh, x, r, g, d)
        gx_ref[...] = d2_ref[...] + dx
        dg_ref[...] += jnp.sum(dgx, axis=0, keepdims=True)

    rowd = pl.BlockSpec((tm, d), lambda i: (i, 0))
    dz_specs = [pl.BlockSpec((tm, w), functools.partial(lambda i, j: (i, j), j=ja)) for _, w, ja, _ in segments]
    w_specs = [pl.BlockSpec((d, w), functools.partial(lambda i, j: (0, j), j=jw), pipeline_mode=pl.Buffered(1))
               for _, w, _, jw in segments]
    return pl.pallas_call(
        body, name="bwd_in", grid=(s // tm,),
        in_specs=dz_specs + w_specs + [rowd, _const_spec((1, d)), rowd],
        out_specs=[rowd, _acc_spec((1, d))],
        out_shape=[jax.ShapeDtypeStruct((s, d), f32), jax.ShapeDtypeStruct((1, d), f32)],
        compiler_params=_cp(("arbitrary",), VMEM_LIMIT),
    )(*[a for a, _, _, _ in segments], *([wz] * n_seg), x, g_mix, d2)


def _pick_tile(n, cap):
    best = None
    for t in range(LANES, cap + 1, LANES):
        if n % t == 0:
            best = t
    return best if best is not None else n


def _mm_tn(a, b, name):
    kk, m = a.shape
    _, n = b.shape
    tm = _pick_tile(m, 1408)
    tn = _pick_tile(n, 1408)
    tk = min(512, kk)

    def body(a_ref, b_ref, o_ref):
        @pl.when(pl.program_id(2) == 0)
        def _():
            o_ref[...] = jnp.zeros_like(o_ref)
        o_ref[...] += _dot_tn(a_ref[...].astype(bf16), b_ref[...].astype(bf16))

    return pl.pallas_call(
        body, name=name, grid=(m // tm, n // tn, kk // tk),
        in_specs=[pl.BlockSpec((tk, tm), lambda i, j, k: (k, i)), pl.BlockSpec((tk, tn), lambda i, j, k: (k, j))],
        out_specs=pl.BlockSpec((tm, tn), lambda i, j, k: (i, j)),
        out_shape=jax.ShapeDtypeStruct((m, n), f32),
        compiler_params=_cp(("parallel", "parallel", "arbitrary"), VMEM_LIMIT),
    )(a, b)


def _rows(a):
    return a.reshape(-1, LANES)


def _part(a, nrows):
    flat = a.reshape(-1)
    return jnp.pad(flat, (0, nrows * LANES - flat.size)).reshape(nrows, LANES)


def _unpart(rows, shape):
    return rows.reshape(-1)[:math.prod(shape)].reshape(shape)


def _pack(parts, nrows, total_rows):
    rows = [_part(a, nr) for a, nr in zip(parts, nrows)]
    rows.append(jnp.zeros((total_rows - sum(nrows), LANES), rows[0].dtype))
    return jnp.concatenate(rows, axis=0)


def _pack_big(parts):
    return _pack([parts[n] for n in BIG], BIG_ROWS, PACK_ROWS)


def _unpack_big(buf, shapes):
    return {n: _unpart(buf[off:off + nr], shapes[n]) for n, off, nr in zip(BIG, BIG_OFF, BIG_ROWS)}


def _gathered_cols(buf, off, nrows, r, c):
    return buf[:, off:off + nrows].reshape(N_DEV, r, c).transpose(1, 0, 2).reshape(r, N_DEV * c)


def _scatter_cols(g, r, c):
    return g.reshape(r, N_DEV, c).transpose(1, 0, 2).reshape(N_DEV, -1, LANES)


def _z_layout(w_in_full):
    cq, ckv, kr, hq, hff, hfb, hi, hg = jnp.split(w_in_full, [256, 512, 576, 1088, 1600, 2112, 2624], axis=1)
    return jnp.concatenate([hq, hff, hfb, hi, hg, cq, ckv, kr, jnp.zeros_like(kr)], axis=1)


def _pad_heads(a, lead):
    return jnp.pad(a.reshape(lead, MLA_HEADS, QK_HEAD), ((0, 0), (0, 0), (0, QK_PAD - QK_HEAD))).reshape(lead, MLA_HEADS * QK_PAD)


def _rope_tables(positions):
    inv_freq = ROPE_THETA ** (-jnp.arange(0, QK_ROPE, 2, dtype=f32) / QK_ROPE)
    ang = positions.astype(f32)[:, None] * inv_freq
    cos, sin = jnp.cos(ang), jnp.sin(ang)
    zero = jnp.zeros_like(cos)
    return (jnp.concatenate([cos, cos, zero, zero], axis=1), jnp.concatenate([zero, sin, zero, zero], axis=1),
            jnp.concatenate([-sin, zero, zero, zero], axis=1))


def _pad256(g):
    return jnp.pad(g.reshape(1, QK_HEAD), ((0, 0), (0, QK_PAD - QK_HEAD)))


def _pack_small(parts, loss_row=None):
    rows = [parts[n].reshape(-1)[:math.prod(_GAIN_SHAPE[n])] for n in SMALL]
    rows.append(jnp.zeros((LANES,), f32) if loss_row is None else loss_row)
    return _pack(rows, SMALL_ROWS + (PART_ROWS,), SMALL_PACK_ROWS)


def _unpack_small(buf, shapes):
    return {n: _unpart(buf[off:off + nr], shapes[n]) for n, off, nr in zip(SMALL, SMALL_OFF, SMALL_ROWS)}


def _local_step(x, p, positions, target, gains, full):
    s = x.shape[0]
    tm = min(256, s)
    tm_ffn = min(128, s)
    tq_f = min(512, s)
    tq_b = min(256, s)
    g_mix, g_qa, g_kva, g_qn, g_kn, g_hgo, g_ffn, g_ple = (gains[n] for n in SMALL)
    g_qn_p, g_kn_p = _pad256(g_qn), _pad256(g_kn)
    g_hgo_f = g_hgo.reshape(1, 512)
    wz = _z_layout(full["w_in"])
    wqb = _pad_heads(full["w_qb"], Q_LORA)
    wkvb, w_o, w_gate, w_up, w_down = full["w_kvb"], full["w_o"], full["w_gate"], full["w_up"], full["w_down"]
    w_pg, w_pp = full["w_ple_gate"], full["w_ple_proj"]
    lb4 = full["lb_param"].reshape(4, 512)
    cosb, sina, sinb = _rope_tables(positions)

    h1, z = _fwd_in(x, g_mix, wz, tm)
    q, k, v = _fwd_mla_proj(z, cosb, sina, sinb, g_qa, g_kva, wqb, wkvb, g_qn_p, g_kn_p, tm)
    a = _fwd_attn(q, k, v, tq_f)
    o = _fwd_gla(z, lb4)
    x2, cat = _fwd_mix(a, o, z, g_hgo_f, x, w_o, tm)
    x3, gp, up = _fwd_ffn(x2, g_ffn, w_gate, w_up, w_down, tm)

    d3, h3, dpre, dpp, dg_ple, loss = _ple_loss_fwd_bwd(x3, g_ple, w_pg, p, w_pp, target, tm)
    d2, h2, act, dgu, dg_ffn = _bwd_ffn(d3, x2, gp, up, g_ffn, w_gate, w_up, w_down, tm_ffn)
    da, do, dz_hg, dg_hgo = _bwd_mix(d2, w_o, o, z, g_hgo_f, tm)
    dz_hq, dz_hf, dz_hi, dlb4 = _bwd_gla(z, lb4, do)
    dq, dk, dv = _bwd_attn(q, k, v, da, tq_b)
    dz_mla, cqn, ckvn, dq0, dkv0, dg_qa, dg_kva, dg_qn, dg_kn = _bwd_mla_proj(
        z, dq, dk, dv, cosb, sina, sinb, g_qa, g_kva, wqb, wkvb, g_qn_p, g_kn_p, tm)
    segments = [(dz_hq, 512, 0, Z_HQ // 512), (dz_hf, 512, 0, Z_HFF // 512), (dz_hf, 512, 1, Z_HFB // 512),
                (dz_hi, 512, 0, Z_HI // 512), (dz_hg, 512, 0, Z_HG // 512), (dz_mla, 640, 0, Z_CQ // 640)]
    grad_x, dg_mix = _bwd_in(segments, wz, x, g_mix, d2, tm)

    gw_gu = _mm_tn(h2, dgu, "dw_gate_up")
    gw_qb = _mm_tn(cqn, dq0, "dw_qb").reshape(Q_LORA, MLA_HEADS, QK_PAD)[:, :, :QK_HEAD].reshape(Q_LORA, MLA_HEADS * QK_HEAD)
    gz_mla = _mm_tn(h1, dz_mla, "dw_in_mla")
    gw_in = jnp.concatenate([gz_mla[:, :576], _mm_tn(h1, dz_hq, "dw_in_hq"), _mm_tn(h1, dz_hf, "dw_in_hf"),
                             _mm_tn(h1, dz_hi, "dw_in_hi"), _mm_tn(h1, dz_hg, "dw_in_hg")], axis=1)
    grads = {
        "w_in": gw_in, "w_qb": gw_qb, "w_kvb": _mm_tn(ckvn, dkv0, "dw_kvb"), "w_o": _mm_tn(cat, d2, "dw_o"),
        "w_gate": gw_gu[:, :D_FF], "w_up": gw_gu[:, D_FF:], "w_down": _mm_tn(act, d3, "dw_down"),
        "w_ple_gate": _mm_tn(h3, dpre, "dw_ple_gate"), "w_ple_proj": _mm_tn(p, dpp, "dw_ple_proj"),
        "lb_param": dlb4.reshape(2, 2, 512),
    }
    dgains = {"g_mix": dg_mix, "g_qa": dg_qa, "g_kva": dg_kva, "g_qn": dg_qn, "g_kn": dg_kn, "g_hgo": dg_hgo,
              "g_ffn": dg_ffn, "g_ple": dg_ple}
    return loss, grad_x, grads, dgains


_SHARD = {"w_in": (1024, 392, True), "w_qb": (256, 96, True), "w_kvb": (256, 128, True), "w_o": (128, 1024, False),
          "w_gate": (1024, 352, True), "w_up": (1024, 352, True), "w_down": (352, 1024, False),
          "w_ple_gate": (128, 1024, False), "w_ple_proj": (256, 128, True)}


def _full_weights(gathered):
    full = {}
    for n, off, nr in zip(BIG, BIG_OFF, BIG_ROWS):
        if n == "lb_param":
            bits = gathered[:, off:off + 4].reshape(N_DEV, 2 * 2 * 64, 2)
            lb = lax.bitcast_convert_type(bits, f32).reshape(N_DEV, 2, 2, 64)
            full[n] = lb.transpose(1, 2, 0, 3).reshape(2, 2, N_DEV * 64)
            continue
        r, c, by_cols = _SHARD[n]
        if by_cols:
            full[n] = _gathered_cols(gathered, off, nr, r, c)
        else:
            full[n] = gathered[:, off:off + nr].reshape(N_DEV * r, c)
    return full


def _grad_blocks(grads):
    blocks = []
    for n in BIG:
        g = grads[n]
        if n == "lb_param":
            lb = g.reshape(2, 2, N_DEV, 64).transpose(2, 0, 1, 3).reshape(N_DEV, 2, LANES)
            blocks.append(jnp.pad(lb, ((0, 0), (0, PART_ROWS - 2), (0, 0))))
            continue
        r, c, by_cols = _SHARD[n]
        blocks.append(_scatter_cols(g, r, c) if by_cols else g.reshape(N_DEV, -1, LANES))
    used = sum(b.shape[1] for b in blocks)
    blocks.append(jnp.zeros((N_DEV, PACK_ROWS - used, LANES), f32))
    return jnp.concatenate(blocks, axis=1)


def kernel(x, p, positions, g_mix, w_in, g_qa, g_kva, w_qb, w_kvb, g_qn, g_kn, lb_param, g_hgo, w_o, g_ffn, w_gate, w_up, w_down, g_ple, w_ple_gate, w_ple_proj, loss_target, m_g_mix, m_w_in, m_g_qa, m_g_kva, m_w_qb, m_w_kvb, m_g_qn, m_g_kn, m_lb_param, m_g_hgo, m_w_o, m_g_ffn, m_w_gate, m_w_up, m_w_down, m_g_ple, m_w_ple_gate, m_w_ple_proj, v_g_mix, v_w_in, v_g_qa, v_g_kva, v_w_qb, v_w_kvb, v_g_qn, v_g_kn, v_lb_param, v_g_hgo, v_w_o, v_g_ffn, v_w_gate, v_w_up, v_w_down, v_g_ple, v_w_ple_gate, v_w_ple_proj):
    names = SMALL + BIG
    w_all = dict(g_mix=g_mix, g_qa=g_qa, g_kva=g_kva, g_qn=g_qn, g_kn=g_kn, g_hgo=g_hgo, g_ffn=g_ffn, g_ple=g_ple,
                 w_in=w_in, w_qb=w_qb, w_kvb=w_kvb, w_o=w_o, w_gate=w_gate, w_up=w_up, w_down=w_down,
                 w_ple_gate=w_ple_gate, w_ple_proj=w_ple_proj, lb_param=lb_param)
    m_all = dict(g_mix=m_g_mix, g_qa=m_g_qa, g_kva=m_g_kva, g_qn=m_g_qn, g_kn=m_g_kn, g_hgo=m_g_hgo, g_ffn=m_g_ffn,
                 g_ple=m_g_ple, w_in=m_w_in, w_qb=m_w_qb, w_kvb=m_w_kvb, w_o=m_w_o, w_gate=m_w_gate, w_up=m_w_up,
                 w_down=m_w_down, w_ple_gate=m_w_ple_gate, w_ple_proj=m_w_ple_proj, lb_param=m_lb_param)
    v_all = dict(g_mix=v_g_mix, g_qa=v_g_qa, g_kva=v_g_kva, g_qn=v_g_qn, g_kn=v_g_kn, g_hgo=v_g_hgo, g_ffn=v_g_ffn,
                 g_ple=v_g_ple, w_in=v_w_in, w_qb=v_w_qb, w_kvb=v_w_kvb, w_o=v_w_o, w_gate=v_w_gate, w_up=v_w_up,
                 w_down=v_w_down, w_ple_gate=v_w_ple_gate, w_ple_proj=v_w_ple_proj, lb_param=v_lb_param)
    shapes = {n: w_all[n].shape for n in names}
    xi, yi, ci = lax.axis_index("x"), lax.axis_index("y"), lax.axis_index("c")

    send = [w_all[n].astype(bf16) for n in BIG[:-1]] + [lax.bitcast_convert_type(lb_param, bf16)]
    full = _full_weights(_all_gather(_pack(send, BIG_ROWS, PACK_ROWS), "ag_weights"))

    gains = {n: w_all[n].reshape(1, -1) for n in SMALL}
    loss_tile, grad_x, grads, dgains = _local_step(x[0], p[0, 0], positions[0], loss_target[0], gains, full)

    g4 = _grad_blocks(grads).reshape(4, 2, PACK_ROWS, LANES)
    from_sibling = _sibling_exchange(g4)
    partials = _chip_partials(g4, from_sibling, jnp.stack([ci]).astype(jnp.int32))
    from_chips = _chip_exchange(partials)
    idx = jnp.stack([ci, 2 * xi + yi]).astype(jnp.int32)
    big = _adam_big(idx, g4, from_sibling, from_chips, _pack_big(w_all), _pack_big(m_all), _pack_big(v_all))
    big = [_unpack_big(b, shapes) for b in big]

    parts = _all_gather(_pack_small(dgains, loss_tile[0]), "ag_gains")
    small = _adam_small(parts, _pack_small(w_all), _pack_small(m_all), _pack_small(v_all))
    loss = small[0][SMALL_LOSS_ROW, 0]
    small = [_unpack_small(b, shapes) for b in small]

    order = ("g_mix", "w_in", "g_qa", "g_kva", "w_qb", "w_kvb", "g_qn", "g_kn", "lb_param", "g_hgo", "w_o", "g_ffn",
             "w_gate", "w_up", "w_down", "g_ple", "w_ple_gate", "w_ple_proj")
    outs = [loss, grad_x[None]]
    for k in range(4):
        outs += [small[k][n] if n in SMALL else big[k][n] for n in order]
    return tuple(outs)
```

```python
import functools
import math

import jax
import jax.numpy as jnp
from jax import lax
from jax.experimental import pallas as pl
from jax.experimental.pallas import tpu as pltpu

f32 = jnp.float32
bf16 = jnp.bfloat16

N_DEV = 8
D_MODEL = 1024
MLA_HEADS = 4
QK_NOPE = 128
QK_ROPE = 64
QK_HEAD = QK_NOPE + QK_ROPE
QK_PAD = 256
V_HEAD = 128
Q_LORA = 256
KV_LORA = 256
HG_HEADS = 4
HG_DK = 128
CHUNK = 64
D_FF = 2816
PLE_DIM = 256
ROPE_THETA = 10000.0
EPS = 1e-6
ATTN_SCALE = QK_HEAD ** -0.5
IN_SIZES = (256, 256, 64, 512, 512, 512, 512, 512)
D_IN = sum(IN_SIZES)
Z_HQ, Z_HFF, Z_HFB, Z_HI, Z_HG, Z_CQ, Z_CKV, Z_KR, Z_W = 0, 512, 1024, 1536, 2048, 2560, 2816, 3072, 3200

ADAM_LR, ADAM_B1, ADAM_B2, ADAM_EPS, ADAM_WD, ADAM_STEP = 0.001, 0.9, 0.999, 1e-08, 0.01, 10

LANES = 128
BIG = {"w_in": (1024, 392), "w_qb": (256, 96), "w_kvb": (256, 128), "w_o": (128, 1024), "w_gate": (1024, 352),
       "w_up": (1024, 352), "w_down": (352, 1024), "w_ple_gate": (128, 1024), "w_ple_proj": (256, 128),
       "lb_param": (4, 64)}
ROW_BLOCKS = {("w_in", "w_qb", "w_kvb", "w_o", "w_gate", "w_up", "w_ple_gate", "w_ple_proj"): 8, ("w_down", "lb_param"): 2}
SMALL = {"g_mix": (0, 1024), "g_qa": (1024, 256), "g_kva": (1280, 256), "g_qn": (1536, 192), "g_kn": (1792, 192),
         "g_hgo": (2048, 512), "g_ffn": (2560, 1024), "g_ple": (3584, 1024)}
LOSS_OFF = 4608
GAIN_VEC = LOSS_OFF + LANES
Z_SEGMENTS = ((0, 256, Z_CQ), (256, 512, Z_CKV), (512, 576, Z_KR), (576, 1088, Z_HQ), (1088, 1600, Z_HFF),
              (1600, 2112, Z_HFB), (2112, 2624, Z_HI), (2624, 3136, Z_HG))

VMEM_LIMIT = 56 * 1024 * 1024
MESH = pl.DeviceIdType.MESH


def _cp(sem=None, vmem=None):
    return pltpu.CompilerParams(dimension_semantics=sem, vmem_limit_bytes=vmem)


def _const_spec(shape):
    nd = len(shape)
    return pl.BlockSpec(shape, lambda *_: (0,) * nd, pipeline_mode=pl.Buffered(1))


def _acc_spec(shape):
    nd = len(shape)
    return pl.BlockSpec(shape, lambda *_: (0,) * nd)


def _sigmoid(x):
    return jax.nn.sigmoid(x)


def _dot(a, b):
    return jnp.dot(a, b, preferred_element_type=f32)


def _dot_nt(a, b):
    return lax.dot_general(a, b, (((1,), (1,)), ((), ())), preferred_element_type=f32)


def _dot_tn(a, b):
    return lax.dot_general(a, b, (((0,), (0,)), ((), ())), preferred_element_type=f32)


def _rms_fwd(x, g, width):
    r = lax.rsqrt(jnp.sum(x * x, axis=-1, keepdims=True) * (1.0 / width) + EPS)
    return x * r * g, r


def _rms_bwd(dy, x, r, g, width):
    u = dy * g
    dx = r * u - x * (r * r * r) * (jnp.sum(u * x, axis=-1, keepdims=True) * (1.0 / width))
    return dx, dy * x * r


def _rope(b, c, sa, sb):
    return b * c + pltpu.roll(b, 32, 1) * sa + pltpu.roll(b, 96, 1) * sb


def _all_gather(shards, dtypes, name):
    n = len(shards)

    def body(*refs):
        in_refs, out_refs, stage = refs[:n], refs[n:2 * n], refs[2 * n:3 * n]
        send_sems, recv_sems, local_sems = refs[3 * n:]
        for w in range(n):
            stage[w][...] = in_refs[w][...].astype(stage[w].dtype)
        x, y, c = lax.axis_index("x"), lax.axis_index("y"), lax.axis_index("c")
        me, sibling = (x, y, c), (x, y, 1 - c)
        chips = [(1 - x, y), (x, 1 - y), (1 - x, 1 - y)]

        def slot(w, px, py, pc):
            return out_refs[w].at[4 * px + 2 * py + pc]

        def copy(w, k, block, to, src=None):
            return pltpu.make_async_remote_copy(
                src_ref=slot(w, *block) if src is None else src, dst_ref=slot(w, *block),
                send_sem=send_sems.at[w, k], recv_sem=recv_sems.at[w, k], device_id=to, device_id_type=MESH)

        first = []
        for j, chip in enumerate(chips):
            first += [copy(w, 1 + j, me, (*chip, c), src=stage[w]) for w in range(n)]
        first += [copy(w, 0, me, sibling, src=stage[w]) for w in range(n)]
        mine = [pltpu.make_async_copy(stage[w], slot(w, *me), local_sems.at[w]) for w in range(n)]
        for cp in first + mine:
            cp.start()
        passed = []
        for j, chip in enumerate(chips):
            for w in range(n):
                copy(w, 1 + j, (*chip, c), me).wait_recv()
                passed.append(copy(w, 4 + j, (*chip, c), sibling))
                passed[-1].start()
        for w in range(n):
            copy(w, 0, sibling, me).wait_recv()
        for j, chip in enumerate(chips):
            for w in range(n):
                copy(w, 4 + j, (*chip, 1 - c), me).wait_recv()
        for cp in first + passed:
            cp.wait_send()
        for cp in mine:
            cp.wait()

    return pl.pallas_call(
        body, name=name,
        out_shape=[jax.ShapeDtypeStruct((N_DEV, *s.shape), dt) for s, dt in zip(shards, dtypes)],
        in_specs=[pl.BlockSpec(memory_space=pltpu.VMEM)] * n,
        out_specs=[pl.BlockSpec(memory_space=pl.ANY)] * n,
        scratch_shapes=[pltpu.VMEM(s.shape, dt) for s, dt in zip(shards, dtypes)]
        + [pltpu.SemaphoreType.DMA((n, 7)), pltpu.SemaphoreType.DMA((n, 7)), pltpu.SemaphoreType.DMA((n,))],
        compiler_params=_cp(None, VMEM_LIMIT),
    )(*shards)


def _sibling_exchange(g4s):
    n = len(g4s)

    def body(*refs):
        g_refs, out_refs, (send_sems, recv_sems) = refs[:n], refs[n:2 * n], refs[2 * n:]
        x, y, c = lax.axis_index("x"), lax.axis_index("y"), lax.axis_index("c")
        cps = [pltpu.make_async_remote_copy(
            src_ref=g_refs[w].at[:, pl.ds(1 - c, 1)], dst_ref=out_refs[w],
            send_sem=send_sems.at[w], recv_sem=recv_sems.at[w], device_id=(x, y, 1 - c), device_id_type=MESH)
            for w in range(n)]
        for cp in cps:
            cp.start()
        for cp in cps:
            cp.wait()

    return pl.pallas_call(
        body, name="rs_sibling_exchange",
        out_shape=[jax.ShapeDtypeStruct((g.shape[0], 1, *g.shape[2:]), g.dtype) for g in g4s],
        in_specs=[pl.BlockSpec(memory_space=pl.ANY)] * n,
        out_specs=[pl.BlockSpec(memory_space=pl.ANY)] * n,
        scratch_shapes=[pltpu.SemaphoreType.DMA((n,)), pltpu.SemaphoreType.DMA((n,))],
    )(*g4s)


def _row_block(rows, n_blocks):
    return (rows // n_blocks, True) if rows % (16 * n_blocks) == 0 else (rows, False)


def _chip_partials(g4s, sibs, c_idx, n_blocks, name):
    n = len(g4s)

    def body(c_ref, *refs):
        for w in range(n):
            refs[2 * n + w][...] = (refs[w][...] + refs[n + w][...]).astype(bf16)

    in_specs, out_specs = [], []
    for own in (True, False):
        for g in g4s:
            rb, cut = _row_block(g.shape[2], n_blocks)
            in_specs.append(pl.BlockSpec(
                (None, None, rb, g.shape[3]),
                functools.partial(lambda k, i, c, cut, own: (k, c[0] if own else 0, i if cut else 0, 0), cut=cut, own=own)))
    for g in g4s:
        rb, cut = _row_block(g.shape[2], n_blocks)
        out_specs.append(pl.BlockSpec((None, rb, g.shape[3]), functools.partial(lambda k, i, c, cut: (k, i if cut else 0, 0), cut=cut)))
    return pl.pallas_call(
        body, name=name,
        grid_spec=pltpu.PrefetchScalarGridSpec(num_scalar_prefetch=1, grid=(4, n_blocks), in_specs=in_specs, out_specs=out_specs),
        out_shape=[jax.ShapeDtypeStruct((4, *g.shape[2:]), bf16) for g in g4s],
        compiler_params=_cp(("arbitrary", "arbitrary"), VMEM_LIMIT),
    )(c_idx, *g4s, *sibs)


def _chip_exchange(partials):
    n = len(partials)

    def body(*refs):
        p_refs, out_refs, (send_sems, recv_sems) = refs[:n], refs[n:2 * n], refs[2 * n:]
        x, y, c = lax.axis_index("x"), lax.axis_index("y"), lax.axis_index("c")
        chips = [(1 - x, y), (x, 1 - y), (1 - x, 1 - y)]
        cps = [pltpu.make_async_remote_copy(
            src_ref=p_refs[w].at[2 * px + py], dst_ref=out_refs[w].at[k],
            send_sem=send_sems.at[w, k], recv_sem=recv_sems.at[w, k], device_id=(px, py, c), device_id_type=MESH)
            for k, (px, py) in enumerate(chips) for w in range(n)]
        for cp in cps:
            cp.start()
        for cp in cps:
            cp.wait()

    return pl.pallas_call(
        body, name="rs_chip_exchange",
        out_shape=[jax.ShapeDtypeStruct((3, *p.shape[1:]), p.dtype) for p in partials],
        in_specs=[pl.BlockSpec(memory_space=pl.ANY)] * n,
        out_specs=[pl.BlockSpec(memory_space=pl.ANY)] * n,
        scratch_shapes=[pltpu.SemaphoreType.DMA((n, 3)), pltpu.SemaphoreType.DMA((n, 3))],
    )(*partials)


def _adam_math(w, g, m, v):
    m = ADAM_B1 * m + (1.0 - ADAM_B1) * g
    v = ADAM_B2 * v + (1.0 - ADAM_B2) * (g * g)
    m_hat = m / (1.0 - ADAM_B1 ** ADAM_STEP)
    v_hat = v / (1.0 - ADAM_B2 ** ADAM_STEP)
    delta = -ADAM_LR * (m_hat / (jnp.sqrt(v_hat) + ADAM_EPS) + ADAM_WD * w)
    return delta, m, v


def _adam_shards(idx, g4s, sibs, chips, ws, ms, vs, n_blocks, name):
    n = len(g4s)

    def body(i_ref, *refs):
        ins, outs = refs[:6 * n], refs[6 * n:]
        for w in range(n):
            g_ref, a_ref, b_ref, w_ref, m_ref, v_ref = (ins[t * n + w] for t in range(6))
            g = g_ref[...] + a_ref[...]
            for k in range(3):
                g = g + b_ref[k].astype(f32)
            if len(w_ref.shape) == 2:
                pieces = [(slice(None), g)]
            else:
                pieces = [(a, g[2 * a:2 * a + 2]) for a in range(2)]
            for at, gp in pieces:
                vals = (gp,) + _adam_math(w_ref[at], gp, m_ref[at], v_ref[at])
                for t, val in enumerate(vals):
                    outs[4 * w + t][at] = val

    specs = [[] for _ in range(6)]
    out_specs, out_shape = [], []
    for g, wt in zip(g4s, ws):
        rows, cols = g.shape[2:]
        rb, cut = _row_block(rows, n_blocks)
        sel = functools.partial(lambda i, s, cut, own: (s[1], s[0] if own else 0, i if cut else 0, 0), cut=cut)
        specs[0].append(pl.BlockSpec((None, None, rb, cols), functools.partial(sel, own=True)))
        specs[1].append(pl.BlockSpec((None, None, rb, cols), functools.partial(sel, own=False)))
        specs[2].append(pl.BlockSpec((3, rb, cols), functools.partial(lambda i, s, cut: (0, i if cut else 0, 0), cut=cut)))
        if wt.shape[0] == 1:
            shard = pl.BlockSpec((None, rb, cols), functools.partial(lambda i, s, cut: (0, i if cut else 0, 0), cut=cut))
        else:
            shard = pl.BlockSpec(wt.shape, functools.partial(lambda i, s, nd: (0,) * nd, nd=wt.ndim))
        for t in (3, 4, 5):
            specs[t].append(shard)
        out_specs += [shard] * 4
        out_shape += [jax.ShapeDtypeStruct(wt.shape, f32)] * 4
    outs = pl.pallas_call(
        body, name=name,
        grid_spec=pltpu.PrefetchScalarGridSpec(num_scalar_prefetch=1, grid=(n_blocks,), in_specs=sum(specs, []), out_specs=out_specs),
        out_shape=out_shape,
        compiler_params=_cp(("arbitrary",), VMEM_LIMIT),
    )(idx, *g4s, *sibs, *chips, *ws, *ms, *vs)
    return [outs[4 * w:4 * w + 4] for w in range(n)]


def _adam_gains(parts, ws, ms, vs):
    n = len(ws)

    def body(p_ref, *refs):
        ins, outs = refs[:3 * n], refs[3 * n:]
        g_all = p_ref[0]
        for k in range(1, N_DEV):
            g_all = g_all + p_ref[k]
        for w, (off, lanes) in enumerate(SMALL.values()):
            w_ref, m_ref, v_ref = ins[w], ins[n + w], ins[2 * n + w]
            if len(w_ref.shape) == 2:
                pieces = [(slice(None), off, lanes)]
            else:
                pieces = [((slice(None), h), off + LANES * h, LANES) for h in range(w_ref.shape[1])]
            for at, o, ln in pieces:
                g = g_all[:, o:o + ln]
                vals = (g,) + _adam_math(w_ref[at], g, m_ref[at], v_ref[at])
                for t, val in enumerate(vals):
                    outs[4 * w + t][at] = val
        outs[4 * n][...] = g_all[:, LOSS_OFF:LOSS_OFF + LANES]

    out_shape = sum([[jax.ShapeDtypeStruct(w.shape, f32)] * 4 for w in ws], []) + [jax.ShapeDtypeStruct((1, LANES), f32)]
    outs = pl.pallas_call(body, name="adamw_gains", out_shape=out_shape)(parts, *ws, *ms, *vs)
    return [outs[4 * w:4 * w + 4] for w in range(n)], outs[4 * n]


def _fwd_in(x, g_mix, wz, tm):
    s, d = x.shape

    def body(x_ref, g_ref, w_ref, h_ref, z_ref):
        h, _ = _rms_fwd(x_ref[...], g_ref[...], d)
        hb = h.astype(bf16)
        h_ref[...] = hb
        z_ref[...] = _dot(hb, w_ref[...])

    return pl.pallas_call(
        body, name="fwd_in", grid=(s // tm,),
        in_specs=[pl.BlockSpec((tm, d), lambda i: (i, 0)), _const_spec((1, d)), _const_spec((d, Z_W))],
        out_specs=[pl.BlockSpec((tm, d), lambda i: (i, 0)), pl.BlockSpec((tm, Z_W), lambda i: (i, 0))],
        out_shape=[jax.ShapeDtypeStruct((s, d), bf16), jax.ShapeDtypeStruct((s, Z_W), f32)],
        compiler_params=_cp(("parallel",), VMEM_LIMIT),
    )(x, g_mix, wz)


def _mla_qk_fwd(cq, ckv, kr, g_qa, g_kva, wqb, wkvb, g_qn, g_kn):
    cqn, rq = _rms_fwd(cq, g_qa, Q_LORA)
    ckvn, rkv = _rms_fwd(ckv, g_kva, KV_LORA)
    cqn_b, ckvn_b = cqn.astype(bf16), ckvn.astype(bf16)
    q0 = _dot(cqn_b, wqb)
    kv0 = _dot(ckvn_b, wkvb)
    return cqn_b, rq, ckvn_b, rkv, q0, kv0


def _fwd_mla_proj(z, cosb, sina, sinb, g_qa, g_kva, wqb, wkvb, g_qn, g_kn, tm):
    s = z.shape[0]
    hh = MLA_HEADS

    def body(cq_ref, ckv_ref, kr_ref, c_ref, sa_ref, sb_ref, gqa_ref, gkva_ref, wqb_ref, wkvb_ref, gqn_ref, gkn_ref,
             q_ref, k_ref, v_ref):
        _, _, _, _, q0, kv0 = _mla_qk_fwd(cq_ref[...], ckv_ref[...], kr_ref[...], gqa_ref[...], gkva_ref[...],
                                          wqb_ref[...], wkvb_ref[...], gqn_ref[...], gkn_ref[...])
        kr = kr_ref[...]
        c, sa, sb = c_ref[...], sa_ref[...], sb_ref[...]
        gqn, gkn = gqn_ref[...], gkn_ref[...]
        kr_sq = jnp.sum(kr * kr, axis=-1, keepdims=True)
        for h in range(hh):
            qh = q0[:, QK_PAD * h:QK_PAD * (h + 1)]
            qn, _ = _rms_fwd(qh, gqn, QK_HEAD)
            q_ref[h, :, 0:128] = qn[:, 0:128].astype(bf16)
            q_ref[h, :, 128:256] = _rope(qn[:, 128:256], c, sa, sb).astype(bf16)
            kn_ = kv0[:, 256 * h:256 * h + 128]
            rk = lax.rsqrt((jnp.sum(kn_ * kn_, axis=-1, keepdims=True) + kr_sq) * (1.0 / QK_HEAD) + EPS)
            k_ref[h, :, 0:128] = (kn_ * rk * gkn[:, 0:128]).astype(bf16)
            k_ref[h, :, 128:256] = _rope(kr * rk * gkn[:, 128:256], c, sa, sb).astype(bf16)
            v_ref[h] = kv0[:, 256 * h + 128:256 * h + 256].astype(bf16)

    row128 = pl.BlockSpec((tm, 128), lambda i: (i, 0))
    return pl.pallas_call(
        body, name="fwd_mla_proj", grid=(s // tm,),
        in_specs=[pl.BlockSpec((tm, 256), lambda i: (i, Z_CQ // 256)), pl.BlockSpec((tm, 256), lambda i: (i, Z_CKV // 256)),
                  pl.BlockSpec((tm, 128), lambda i: (i, Z_KR // 128)), row128, row128, row128,
                  _const_spec((1, 256)), _const_spec((1, 256)), _const_spec((256, 1024)), _const_spec((256, 1024)),
                  _const_spec((1, 256)), _const_spec((1, 256))],
        out_specs=[pl.BlockSpec((hh, tm, QK_PAD), lambda i: (0, i, 0)), pl.BlockSpec((hh, tm, QK_PAD), lambda i: (0, i, 0)),
                   pl.BlockSpec((hh, tm, V_HEAD), lambda i: (0, i, 0))],
        out_shape=[jax.ShapeDtypeStruct((hh, s, QK_PAD), bf16), jax.ShapeDtypeStruct((hh, s, QK_PAD), bf16),
                   jax.ShapeDtypeStruct((hh, s, V_HEAD), bf16)],
        compiler_params=_cp(("parallel",), VMEM_LIMIT),
    )(z, z, z, cosb, sina, sinb, g_qa, g_kva, wqb, wkvb, g_qn, g_kn)


def _fwd_attn(q, k, v, tq):
    hh, s, _ = q.shape

    def body(q_ref, k_ref, v_ref, o_ref):
        sc = _dot_nt(q_ref[...], k_ref[...]) * ATTN_SCALE
        p = jnp.exp(sc - jnp.max(sc, axis=-1, keepdims=True))
        l = jnp.sum(p, axis=-1, keepdims=True)
        o_ref[...] = (_dot(p.astype(bf16), v_ref[...]) * (1.0 / l)).astype(bf16)

    return pl.pallas_call(
        body, name="fwd_attn", grid=(hh, s // tq),
        in_specs=[pl.BlockSpec((None, tq, QK_PAD), lambda h, i: (h, i, 0)),
                  pl.BlockSpec((None, s, QK_PAD), lambda h, i: (h, 0, 0)),
                  pl.BlockSpec((None, s, V_HEAD), lambda h, i: (h, 0, 0))],
        out_specs=pl.BlockSpec((tq, V_HEAD), lambda h, i: (i, h)),
        out_shape=jax.ShapeDtypeStruct((s, hh * V_HEAD), bf16),
        compiler_params=_cp(("parallel", "parallel"), VMEM_LIMIT),
    )(q, k, v)


def _split3(x):
    hi = x.astype(bf16)
    r1 = x - hi.astype(f32)
    mid = r1.astype(bf16)
    lo = (r1 - mid.astype(f32)).astype(bf16)
    return jnp.concatenate([hi, mid, lo], axis=-1)


def _tri_sum(tri, x):
    y = _dot(tri, _split3(x))
    return y[:, 0:128] + y[:, 128:256] + y[:, 256:384]


def _gla_masks(rev):
    row = lax.broadcasted_iota(jnp.int32, (CHUNK, CHUNK), 0)
    col = lax.broadcasted_iota(jnp.int32, (CHUNK, CHUNK), 1)
    diff = (row - col) * jnp.where(rev, -1, 1)
    return (diff >= 0).astype(f32), (diff >= 0).astype(bf16), (diff <= 0).astype(bf16)


def _gla_gates(hq, hf, lower):
    sg = _sigmoid(hf)
    f = lower + (1.0 - lower) * sg
    return hq * _sigmoid(hq), 1.0 - f, jnp.log(f), f, sg


def _gla_chunk(q, k, logf, rev, tri):
    b = _tri_sum(tri, logf)
    b_last = jnp.where(rev, b[0:1, :], b[CHUNK - 1:CHUNK, :])
    b_mid = jnp.where(rev, b[CHUNK // 2:CHUNK // 2 + 1, :], b[CHUNK // 2 - 1:CHUNK // 2, :])
    return b, b_last, b_mid


def _fwd_gla(z, lb4):
    s = z.shape[0]
    n_chunks = s // CHUNK

    def body(hq_ref, hf_ref, hi_ref, lb_ref, o_ref, st_ref):
        d = pl.program_id(1)
        rev = d == 1
        lower = _sigmoid(lb_ref[pl.ds(d, 1), :] - lb_ref[pl.ds(2 + d, 1), :])
        maskf, tri, _ = _gla_masks(rev)
        st_ref[...] = jnp.zeros_like(st_ref)

        def step(n, carry):
            ne = jnp.where(rev, n_chunks - 1 - n, n)
            rows = pl.ds(pl.multiple_of(ne * CHUNK, CHUNK), CHUNK)
            q, k, logf, _, _ = _gla_gates(hq_ref[rows, :], hf_ref[rows, :], lower)
            v = hi_ref[rows, :]
            b, b_last, b_mid = _gla_chunk(q, k, logf, rev, tri)
            qi = (q * jnp.exp(b - b_mid)).astype(bf16)
            ki = (k * jnp.exp(b_mid - b)).astype(bf16)
            vb = v.astype(bf16)
            a = (_dot_nt(qi, ki) * maskf).astype(bf16)
            st = st_ref[...]
            o = _dot(a, vb) + _dot_nt((q * jnp.exp(b)).astype(bf16), st.astype(bf16))
            kt = (k * jnp.exp(b_last - b)).astype(bf16)
            st_ref[...] = st * jnp.exp(b_last) + _dot_tn(vb, kt)

            @pl.when(d == 0)
            def _():
                o_ref[rows, :] = o

            @pl.when(d == 1)
            def _():
                o_ref[rows, :] += o
            return carry

        lax.fori_loop(0, n_chunks, step, 0)

    col = lambda base: pl.BlockSpec((s, 128), lambda h, d: (0, base // 128 + h))
    return pl.pallas_call(
        body, name="fwd_gla", grid=(HG_HEADS, 2),
        in_specs=[col(Z_HQ), pl.BlockSpec((s, 128), lambda h, d: (0, Z_HFF // 128 + 4 * d + h)), col(Z_HI),
                  pl.BlockSpec((4, 128), lambda h, d: (0, h))],
        out_specs=pl.BlockSpec((s, 128), lambda h, d: (0, h)),
        out_shape=jax.ShapeDtypeStruct((s, HG_HEADS * 128), f32),
        scratch_shapes=[pltpu.VMEM((128, 128), f32)],
        compiler_params=_cp(("parallel", "arbitrary"), VMEM_LIMIT),
    )(z, z, z, lb4)


def _hg_out(o, hg, g_hgo):
    outs, ons, rs = [], [], []
    for h in range(HG_HEADS):
        oh = o[:, 128 * h:128 * (h + 1)]
        on, r = _rms_fwd(oh, g_hgo[:, 128 * h:128 * (h + 1)], 128)
        ons.append(on)
        rs.append(r)
    on = jnp.concatenate(ons, axis=-1)
    sg = _sigmoid(hg)
    return on * (hg * sg), on, rs, sg


def _fwd_mix(a, o, z, g_hgo, x, w_o, tm):
    s, d = x.shape

    def body(a_ref, o_ref, hg_ref, g_ref, x_ref, w_ref, x2_ref, cat_ref):
        r, _, _, _ = _hg_out(o_ref[...], hg_ref[...], g_ref[...])
        cat = jnp.concatenate([a_ref[...], r.astype(bf16)], axis=-1)
        cat_ref[...] = cat
        x2_ref[...] = x_ref[...] + _dot(cat, w_ref[...])

    row512 = pl.BlockSpec((tm, 512), lambda i: (i, 0))
    rowd = pl.BlockSpec((tm, d), lambda i: (i, 0))
    return pl.pallas_call(
        body, name="fwd_mix", grid=(s // tm,),
        in_specs=[row512, row512, pl.BlockSpec((tm, 512), lambda i: (i, Z_HG // 512)), _const_spec((1, 512)), rowd,
                  _const_spec((d, d))],
        out_specs=[rowd, rowd],
        out_shape=[jax.ShapeDtypeStruct((s, d), f32), jax.ShapeDtypeStruct((s, d), bf16)],
        compiler_params=_cp(("parallel",), VMEM_LIMIT),
    )(a, o, z, g_hgo, x, w_o)


def _fwd_ffn(x2, g_ffn, w_gate, w_up, w_down, tm):
    s, d = x2.shape

    def body(x_ref, g_ref, wg_ref, wu_ref, wd_ref, x3_ref, gp_ref, up_ref):
        x = x_ref[...]
        h, _ = _rms_fwd(x, g_ref[...], d)
        hb = h.astype(bf16)
        gp = _dot(hb, wg_ref[...])
        up = _dot(hb, wu_ref[...])
        gp_ref[...] = gp
        up_ref[...] = up
        act = (gp * _sigmoid(gp) * up).astype(bf16)
        x3_ref[...] = x + _dot(act, wd_ref[...])

    rowd = pl.BlockSpec((tm, d), lambda i: (i, 0))
    rowf = pl.BlockSpec((tm, D_FF), lambda i: (i, 0))
    return pl.pallas_call(
        body, name="fwd_ffn", grid=(s // tm,),
        in_specs=[rowd, _const_spec((1, d)), _const_spec((d, D_FF)), _const_spec((d, D_FF)), _const_spec((D_FF, d))],
        out_specs=[rowd, rowf, rowf],
        out_shape=[jax.ShapeDtypeStruct((s, d), f32), jax.ShapeDtypeStruct((s, D_FF), f32),
                   jax.ShapeDtypeStruct((s, D_FF), f32)],
        compiler_params=_cp(("parallel",), VMEM_LIMIT),
    )(x2, g_ffn, w_gate, w_up, w_down)


def _ple_loss_fwd_bwd(x3, g_ple, w_pg, p, w_pp, target, tm):
    s, d = x3.shape

    def body(x_ref, g_ref, wg_ref, p_ref, wp_ref, t_ref, dx_ref, h_ref, dpre_ref, dpp_ref, dg_ref, loss_ref):
        @pl.when(pl.program_id(0) == 0)
        def _():
            dg_ref[...] = jnp.zeros_like(dg_ref)
            loss_ref[...] = jnp.zeros_like(loss_ref)

        x = x_ref[...]
        g = g_ref[...]
        h, r = _rms_fwd(x, g, d)
        hb = h.astype(bf16)
        gate = _sigmoid(_dot(hb, wg_ref[...]))
        pp = _dot(p_ref[...].astype(bf16), wp_ref[...])
        e = x + gate * pp - t_ref[...]
        loss_ref[...] += 0.5 * jnp.sum(e * e) * (1.0 / d)
        dy = e * (1.0 / d)
        dpre = (dy * pp * gate * (1.0 - gate)).astype(bf16)
        dx, dgx = _rms_bwd(_dot_nt(dpre, wg_ref[...]), x, r, g, d)
        dx_ref[...] = dy + dx
        dg_ref[...] += jnp.sum(dgx, axis=0, keepdims=True)
        h_ref[...] = hb
        dpre_ref[...] = dpre
        dpp_ref[...] = (dy * gate).astype(bf16)

    rowd = pl.BlockSpec((tm, d), lambda i: (i, 0))
    return pl.pallas_call(
        body, name="ple_loss_fwd_bwd", grid=(s // tm,),
        in_specs=[rowd, _const_spec((1, d)), _const_spec((d, d)), pl.BlockSpec((tm, PLE_DIM), lambda i: (i, 0)),
                  _const_spec((PLE_DIM, d)), rowd],
        out_specs=[rowd, rowd, rowd, rowd, _acc_spec((1, d)), _acc_spec((8, 128))],
        out_shape=[jax.ShapeDtypeStruct((s, d), f32), jax.ShapeDtypeStruct((s, d), bf16), jax.ShapeDtypeStruct((s, d), bf16),
                   jax.ShapeDtypeStruct((s, d), bf16), jax.ShapeDtypeStruct((1, d), f32), jax.ShapeDtypeStruct((8, 128), f32)],
        compiler_params=_cp(("arbitrary",), VMEM_LIMIT),
    )(x3, g_ple, w_pg, p, w_pp, target)


def _bwd_ffn(d3, x2, gp, up, g_ffn, w_gate, w_up, w_down, tm):
    s, d = x2.shape

    def body(d3_ref, x_ref, gp_ref, up_ref, g_ref, wg_ref, wu_ref, wd_ref, d2_ref, h_ref, act_ref, dgu_ref, dg_ref):
        @pl.when(pl.program_id(0) == 0)
        def _():
            dg_ref[...] = jnp.zeros_like(dg_ref)

        x = x_ref[...]
        g = g_ref[...]
        d3 = d3_ref[...]
        h, r = _rms_fwd(x, g, d)
        h_ref[...] = h.astype(bf16)
        gp, up = gp_ref[...], up_ref[...]
        sg = _sigmoid(gp)
        silu = gp * sg
        act_ref[...] = (silu * up).astype(bf16)
        dact = _dot_nt(d3.astype(bf16), wd_ref[...])
        dgp = (dact * up * (sg * (1.0 + gp * (1.0 - sg)))).astype(bf16)
        dup = (dact * silu).astype(bf16)
        dgu_ref[:, 0:D_FF] = dgp
        dgu_ref[:, D_FF:2 * D_FF] = dup
        dh = _dot_nt(dgp, wg_ref[...]) + _dot_nt(dup, wu_ref[...])
        dx, dgx = _rms_bwd(dh, x, r, g, d)
        d2_ref[...] = d3 + dx
        dg_ref[...] += jnp.sum(dgx, axis=0, keepdims=True)

    rowd = pl.BlockSpec((tm, d), lambda i: (i, 0))
    rowf = pl.BlockSpec((tm, D_FF), lambda i: (i, 0))
    return pl.pallas_call(
        body, name="bwd_ffn", grid=(s // tm,),
        in_specs=[rowd, rowd, rowf, rowf, _const_spec((1, d)), _const_spec((d, D_FF)), _const_spec((d, D_FF)),
                  _const_spec((D_FF, d))],
        out_specs=[rowd, rowd, rowf, pl.BlockSpec((tm, 2 * D_FF), lambda i: (i, 0)), _acc_spec((1, d))],
        out_shape=[jax.ShapeDtypeStruct((s, d), f32), jax.ShapeDtypeStruct((s, d), bf16), jax.ShapeDtypeStruct((s, D_FF), bf16),
                   jax.ShapeDtypeStruct((s, 2 * D_FF), bf16), jax.ShapeDtypeStruct((1, d), f32)],
        compiler_params=_cp(("arbitrary",), VMEM_LIMIT),
    )(d3, x2, gp, up, g_ffn, w_gate, w_up, w_down)


def _bwd_mix(d2, w_o, o, z, g_hgo, tm):
    s, d = d2.shape

    def body(d2_ref, w_ref, o_ref, hg_ref, g_ref, da_ref, do_ref, dhg_ref, dg_ref):
        @pl.when(pl.program_id(0) == 0)
        def _():
            dg_ref[...] = jnp.zeros_like(dg_ref)

        dcat = _dot_nt(d2_ref[...].astype(bf16), w_ref[...])
        da_ref[...] = dcat[:, 0:512].astype(bf16)
        dr = dcat[:, 512:1024]
        o, hg, g = o_ref[...], hg_ref[...], g_ref[...]
        _, on, rs, sg = _hg_out(o, hg, g)
        dhg_ref[...] = (dr * on * (sg * (1.0 + hg * (1.0 - sg)))).astype(bf16)
        don = dr * (hg * sg)
        dgs = []
        for h in range(HG_HEADS):
            cols = slice(128 * h, 128 * (h + 1))
            dx, dgx = _rms_bwd(don[:, cols], o[:, cols], rs[h], g[:, cols], 128)
            do_ref[:, cols] = dx
            dgs.append(jnp.sum(dgx, axis=0, keepdims=True))
        dg_ref[...] += jnp.concatenate(dgs, axis=-1)

    row512 = pl.BlockSpec((tm, 512), lambda i: (i, 0))
    return pl.pallas_call(
        body, name="bwd_mix", grid=(s // tm,),
        in_specs=[pl.BlockSpec((tm, d), lambda i: (i, 0)), _const_spec((d, d)), row512,
                  pl.BlockSpec((tm, 512), lambda i: (i, Z_HG // 512)), _const_spec((1, 512))],
        out_specs=[row512, row512, row512, _acc_spec((1, 512))],
        out_shape=[jax.ShapeDtypeStruct((s, 512), bf16), jax.ShapeDtypeStruct((s, 512), f32), jax.ShapeDtypeStruct((s, 512), bf16),
                   jax.ShapeDtypeStruct((1, 512), f32)],
        compiler_params=_cp(("arbitrary",), VMEM_LIMIT),
    )(d2, w_o, o, z, g_hgo)


def _bwd_gla(z, lb4, do):
    s = z.shape[0]
    n_chunks = s // CHUNK

    def body(hq_ref, hf_ref, hi_ref, lb_ref, do_ref, dhq_ref, dhf_ref, dhi_ref, dlb_ref, st_all, b_all, dst_ref, dq_acc, dv_acc, dlow_ref):
        d = pl.program_id(1)
        rev = d == 1
        lower = _sigmoid(lb_ref[pl.ds(d, 1), :] - lb_ref[pl.ds(2 + d, 1), :])
        maskf, tri, tri_t = _gla_masks(rev)

        def rows_of(n):
            ne = jnp.where(rev, n_chunks - 1 - n, n)
            return pl.ds(pl.multiple_of(ne * CHUNK, CHUNK), CHUNK)

        def fwd_step(n, st):
            rows = rows_of(n)
            _, k, logf, _, _ = _gla_gates(hq_ref[rows, :], hf_ref[rows, :], lower)
            b, b_last, _ = _gla_chunk(None, k, logf, rev, tri)
            b_all[rows, :] = b
            st_all[n] = st
            kt = (k * jnp.exp(b_last - b)).astype(bf16)
            return st * jnp.exp(b_last) + _dot_tn(hi_ref[rows, :].astype(bf16), kt)

        lax.fori_loop(0, n_chunks, fwd_step, jnp.zeros((128, 128), f32))

        dst_ref[...] = jnp.zeros_like(dst_ref)
        dlow_ref[...] = jnp.zeros_like(dlow_ref)

        def bwd_step(j, carry):
            n = n_chunks - 1 - j
            rows = rows_of(n)
            hq, hf = hq_ref[rows, :], hf_ref[rows, :]
            q, k, _, f, sg = _gla_gates(hq, hf, lower)
            v = hi_ref[rows, :]
            dout = do_ref[rows, :]
            b = b_all[rows, :]
            b_last = jnp.where(rev, b[0:1, :], b[CHUNK - 1:CHUNK, :])
            b_mid = jnp.where(rev, b[CHUNK // 2:CHUNK // 2 + 1, :], b[CHUNK // 2 - 1:CHUNK // 2, :])
            e1, e2, e3, e4 = jnp.exp(b - b_mid), jnp.exp(b_mid - b), jnp.exp(b_last - b), jnp.exp(b)
            decay = jnp.exp(b_last)
            qi, ki, kt, qt = q * e1, k * e2, k * e3, q * e4
            qib, kib, ktb, qtb = qi.astype(bf16), ki.astype(bf16), kt.astype(bf16), qt.astype(bf16)
            vb, dob = v.astype(bf16), dout.astype(bf16)
            st = st_all[n]
            dst = dst_ref[...]
            stb, dstb = st.astype(bf16), dst.astype(bf16)
            a = (_dot_nt(qib, kib) * maskf).astype(bf16)
            dv = _dot_tn(a, dob) + _dot_nt(ktb, dstb)
            da = (_dot_nt(dob, vb) * maskf).astype(bf16)
            dqi = _dot(da, kib)
            dki = _dot_tn(da, qib)
            dqt = _dot(dob, stb)
            dkt = _dot(vb, dstb)
            ddecay = jnp.sum(dst * st, axis=0, keepdims=True)
            dq = dqi * e1 + dqt * e4
            dk = dki * e2 + dkt * e3
            db = dqi * qi - dki * ki + dqt * qt - dkt * kt
            dlast = jnp.sum(dkt * kt, axis=0, keepdims=True) + ddecay * decay
            dlogf = _tri_sum(tri_t, db) + dlast
            dst_ref[...] = dst * decay + _dot_tn(dob, qtb)
            df = dlogf / f - dk
            dhf_ref[rows, :] = (df * (1.0 - lower) * sg * (1.0 - sg)).astype(bf16)
            dlow_ref[...] += jnp.sum(df * (1.0 - sg), axis=0, keepdims=True)
            sq = _sigmoid(hq)
            dhq = dq * (sq * (1.0 + hq * (1.0 - sq)))

            @pl.when(d == 0)
            def _():
                dq_acc[rows, :] = dhq
                dv_acc[rows, :] = dv

            @pl.when(d == 1)
            def _():
                dhq_ref[rows, :] = (dq_acc[rows, :] + dhq).astype(bf16)
                dhi_ref[rows, :] = (dv_acc[rows, :] + dv).astype(bf16)
            return carry

        lax.fori_loop(0, n_chunks, bwd_step, 0)

        dl = dlow_ref[...] * lower * (1.0 - lower)

        @pl.when(d == 0)
        def _():
            dlb_ref[...] = jnp.zeros_like(dlb_ref)
        dlb_ref[pl.ds(d, 1), :] = dl
        dlb_ref[pl.ds(2 + d, 1), :] = -dl

    col = lambda base: pl.BlockSpec((s, 128), lambda h, d: (0, base // 128 + h))
    out_col = lambda base: pl.BlockSpec((s, 128), lambda h, d: (0, base // 128 + h))
    return pl.pallas_call(
        body, name="bwd_gla", grid=(HG_HEADS, 2),
        in_specs=[col(Z_HQ), pl.BlockSpec((s, 128), lambda h, d: (0, Z_HFF // 128 + 4 * d + h)), col(Z_HI),
                  pl.BlockSpec((4, 128), lambda h, d: (0, h)), pl.BlockSpec((s, 128), lambda h, d: (0, h))],
        out_specs=[pl.BlockSpec((s, 128), lambda h, d: (0, h)), pl.BlockSpec((s, 128), lambda h, d: (0, 4 * d + h)),
                   pl.BlockSpec((s, 128), lambda h, d: (0, h)), pl.BlockSpec((4, 128), lambda h, d: (0, h))],
        out_shape=[jax.ShapeDtypeStruct((s, 512), bf16), jax.ShapeDtypeStruct((s, 1024), bf16),
                   jax.ShapeDtypeStruct((s, 512), bf16), jax.ShapeDtypeStruct((4, 512), f32)],
        scratch_shapes=[pltpu.VMEM((n_chunks, 128, 128), f32), pltpu.VMEM((s, 128), f32), pltpu.VMEM((128, 128), f32),
                        pltpu.VMEM((s, 128), f32), pltpu.VMEM((s, 128), f32), pltpu.VMEM((1, 128), f32)],
        compiler_params=_cp(("parallel", "arbitrary"), VMEM_LIMIT),
    )(z, z, z, lb4, do)


def _bwd_attn(q, k, v, da, tq):
    hh, s, _ = q.shape

    def body(q_ref, k_ref, v_ref, do_ref, dq_ref, dk_ref, dv_ref):
        @pl.when(pl.program_id(1) == 0)
        def _():
            dk_ref[...] = jnp.zeros_like(dk_ref)
            dv_ref[...] = jnp.zeros_like(dv_ref)

        qb, kb, vb, dob = q_ref[...], k_ref[...], v_ref[...], do_ref[...]
        sc = _dot_nt(qb, kb) * ATTN_SCALE
        p = jnp.exp(sc - jnp.max(sc, axis=-1, keepdims=True))
        w = p * (1.0 / jnp.sum(p, axis=-1, keepdims=True))
        dw = _dot_nt(dob, vb)
        ds = (w * (dw - jnp.sum(dw * w, axis=-1, keepdims=True)) * ATTN_SCALE).astype(bf16)
        dq_ref[...] = _dot(ds, kb)
        dk_ref[...] += _dot_tn(ds, qb)
        dv_ref[...] += _dot_tn(w.astype(bf16), dob)

    return pl.pallas_call(
        body, name="bwd_attn", grid=(hh, s // tq),
        in_specs=[pl.BlockSpec((None, tq, QK_PAD), lambda h, i: (h, i, 0)),
                  pl.BlockSpec((None, s, QK_PAD), lambda h, i: (h, 0, 0)),
                  pl.BlockSpec((None, s, V_HEAD), lambda h, i: (h, 0, 0)),
                  pl.BlockSpec((tq, V_HEAD), lambda h, i: (i, h))],
        out_specs=[pl.BlockSpec((None, tq, QK_PAD), lambda h, i: (h, i, 0)),
                   pl.BlockSpec((None, s, QK_PAD), lambda h, i: (h, 0, 0)),
                   pl.BlockSpec((None, s, V_HEAD), lambda h, i: (h, 0, 0))],
        out_shape=[jax.ShapeDtypeStruct((hh, s, QK_PAD), f32), jax.ShapeDtypeStruct((hh, s, QK_PAD), f32),
                   jax.ShapeDtypeStruct((hh, s, V_HEAD), f32)],
        compiler_params=_cp(("parallel", "arbitrary"), VMEM_LIMIT),
    )(q, k, v, da)


def _bwd_mla_proj(z, dq, dk, dv, cosb, sina, sinb, g_qa, g_kva, wqb, wkvb, g_qn, g_kn, tm):
    s = z.shape[0]
    hh = MLA_HEADS

    def body(cq_ref, ckv_ref, kr_ref, dq_ref, dk_ref, dv_ref, c_ref, sa_ref, sb_ref, gqa_ref, gkva_ref, wqb_ref, wkvb_ref,
             gqn_ref, gkn_ref, dz_ref, cqn_ref, ckvn_ref, dq0_ref, dkv0_ref, dgqa_ref, dgkva_ref, dgqn_ref, dgkn_ref):
        @pl.when(pl.program_id(0) == 0)
        def _():
            for r in (dgqa_ref, dgkva_ref, dgqn_ref, dgkn_ref):
                r[...] = jnp.zeros_like(r)

        cq, ckv, kr = cq_ref[...], ckv_ref[...], kr_ref[...]
        gqa, gkva, gqn, gkn = gqa_ref[...], gkva_ref[...], gqn_ref[...], gkn_ref[...]
        cqn_b, rq, ckvn_b, rkv, q0, kv0 = _mla_qk_fwd(cq, ckv, kr, gqa, gkva, wqb_ref[...], wkvb_ref[...], gqn, gkn)
        cqn_ref[...] = cqn_b
        ckvn_ref[...] = ckvn_b
        c, sa, sb = c_ref[...], -sa_ref[...], -sb_ref[...]
        kr_sq = jnp.sum(kr * kr, axis=-1, keepdims=True)
        dkr = jnp.zeros_like(kr)
        dgqn = jnp.zeros((1, QK_PAD), f32)
        dgkn = jnp.zeros((1, QK_PAD), f32)
        for h in range(hh):
            qh = q0[:, QK_PAD * h:QK_PAD * (h + 1)]
            rh = lax.rsqrt(jnp.sum(qh * qh, axis=-1, keepdims=True) * (1.0 / QK_HEAD) + EPS)
            dqh = dq_ref[h]
            dqn = jnp.concatenate([dqh[:, 0:128], _rope(dqh[:, 128:256], c, sa, sb)], axis=-1)
            dq0h, dgx = _rms_bwd(dqn, qh, rh, gqn, QK_HEAD)
            dq0_ref[:, QK_PAD * h:QK_PAD * (h + 1)] = dq0h.astype(bf16)
            dgqn = dgqn + jnp.sum(dgx, axis=0, keepdims=True)

            kn_ = kv0[:, 256 * h:256 * h + 128]
            k0 = jnp.concatenate([kn_, kr], axis=-1)
            rk = lax.rsqrt((jnp.sum(kn_ * kn_, axis=-1, keepdims=True) + kr_sq) * (1.0 / QK_HEAD) + EPS)
            dkh = dk_ref[h]
            dkn = jnp.concatenate([dkh[:, 0:128], _rope(dkh[:, 128:256], c, sa, sb)], axis=-1)
            dk0, dgx = _rms_bwd(dkn, k0, rk, gkn, QK_HEAD)
            dgkn = dgkn + jnp.sum(dgx, axis=0, keepdims=True)
            dkv0_ref[:, 256 * h:256 * h + 128] = dk0[:, 0:128].astype(bf16)
            dkv0_ref[:, 256 * h + 128:256 * h + 256] = dv_ref[h].astype(bf16)
            dkr = dkr + dk0[:, 128:256]
        dgqn_ref[...] += dgqn
        dgkn_ref[...] += dgkn
        dcq, dgx = _rms_bwd(_dot_nt(dq0_ref[...], wqb_ref[...]), cq, rq, gqa, Q_LORA)
        dgqa_ref[...] += jnp.sum(dgx, axis=0, keepdims=True)
        dckv, dgx = _rms_bwd(_dot_nt(dkv0_ref[...], wkvb_ref[...]), ckv, rkv, gkva, KV_LORA)
        dgkva_ref[...] += jnp.sum(dgx, axis=0, keepdims=True)
        dz_ref[:, 0:256] = dcq.astype(bf16)
        dz_ref[:, 256:512] = dckv.astype(bf16)
        dz_ref[:, 512:640] = dkr.astype(bf16)

    row128 = pl.BlockSpec((tm, 128), lambda i: (i, 0))
    row256 = pl.BlockSpec((tm, 256), lambda i: (i, 0))
    row1024 = pl.BlockSpec((tm, 1024), lambda i: (i, 0))
    hd = lambda w: pl.BlockSpec((hh, tm, w), lambda i: (0, i, 0))
    return pl.pallas_call(
        body, name="bwd_mla_proj", grid=(s // tm,),
        in_specs=[pl.BlockSpec((tm, 256), lambda i: (i, Z_CQ // 256)), pl.BlockSpec((tm, 256), lambda i: (i, Z_CKV // 256)),
                  pl.BlockSpec((tm, 128), lambda i: (i, Z_KR // 128)), hd(QK_PAD), hd(QK_PAD), hd(V_HEAD),
                  row128, row128, row128,
                  _const_spec((1, 256)), _const_spec((1, 256)), _const_spec((256, 1024)), _const_spec((256, 1024)),
                  _const_spec((1, 256)), _const_spec((1, 256))],
        out_specs=[pl.BlockSpec((tm, 640), lambda i: (i, 0)), row256, row256, row1024, row1024,
                   _acc_spec((1, 256)), _acc_spec((1, 256)), _acc_spec((1, 256)), _acc_spec((1, 256))],
        out_shape=[jax.ShapeDtypeStruct((s, 640), bf16), jax.ShapeDtypeStruct((s, 256), bf16), jax.ShapeDtypeStruct((s, 256), bf16),
                   jax.ShapeDtypeStruct((s, 1024), bf16), jax.ShapeDtypeStruct((s, 1024), bf16)]
        + [jax.ShapeDtypeStruct((1, 256), f32)] * 4,
        compiler_params=_cp(("arbitrary",), VMEM_LIMIT),
    )(z, z, z, dq, dk, dv, cosb, sina, sinb, g_qa, g_kva, wqb, wkvb, g_qn, g_kn)


def _bwd_in(segments, wz, x, g_mix, d2, tm):
    s, d = x.shape
    n_seg = len(segments)

    def body(*refs):
        dz_refs, w_refs = refs[:n_seg], refs[n_seg:2 * n_seg]
        x_ref, g_ref, d2_ref, gx_ref, dg_ref = refs[2 * n_seg:]

        @pl.when(pl.program_id(0) == 0)
        def _():
            dg_ref[...] = jnp.zeros_like(dg_ref)

        dh = _dot_nt(dz_refs[0][...], w_refs[0][...])
        for a_ref, w_ref in zip(dz_refs[1:], w_refs[1:]):
            dh = dh + _dot_nt(a_ref[...], w_ref[...])
        x, g = x_ref[...], g_ref[...]
        r = lax.rsqrt(jnp.sum(x * x, axis=-1, keepdims=True) * (1.0 / d) + EPS)
        dx, dgx = _rms_bwd(dh, x, r, g, d)
        gx_ref[...] = d2_ref[...] + dx
        dg_ref[...] += jnp.sum(dgx, axis=0, keepdims=True)

    rowd = pl.BlockSpec((tm, d), lambda i: (i, 0))
    dz_specs = [pl.BlockSpec((tm, w), functools.partial(lambda i, j: (i, j), j=ja)) for _, w, ja, _ in segments]
    w_specs = [pl.BlockSpec((d, w), functools.partial(lambda i, j: (0, j), j=jw), pipeline_mode=pl.Buffered(1))
               for _, w, _, jw in segments]
    return pl.pallas_call(
        body, name="bwd_in", grid=(s // tm,),
        in_specs=dz_specs + w_specs + [rowd, _const_spec((1, d)), rowd],
        out_specs=[rowd, _acc_spec((1, d))],
        out_shape=[jax.ShapeDtypeStruct((s, d), f32), jax.ShapeDtypeStruct((1, d), f32)],
        compiler_params=_cp(("arbitrary",), VMEM_LIMIT),
    )(*[a for a, _, _, _ in segments], *([wz] * n_seg), x, g_mix, d2)


def _pick_tile(n, cap):
    best = None
    for t in range(LANES, cap + 1, LANES):
        if n % t == 0:
            best = t
    return best if best is not None else n


def _mm_tn(a, b, name):
    kk, m = a.shape
    _, n = b.shape
    tm = _pick_tile(m, 1408)
    tn = _pick_tile(n, 1408)
    tk = min(512, kk)

    def body(a_ref, b_ref, o_ref):
        @pl.when(pl.program_id(2) == 0)
        def _():
            o_ref[...] = jnp.zeros_like(o_ref)
        o_ref[...] += _dot_tn(a_ref[...].astype(bf16), b_ref[...].astype(bf16))

    return pl.pallas_call(
        body, name=name, grid=(m // tm, n // tn, kk // tk),
        in_specs=[pl.BlockSpec((tk, tm), lambda i, j, k: (k, i)), pl.BlockSpec((tk, tn), lambda i, j, k: (k, j))],
        out_specs=pl.BlockSpec((tm, tn), lambda i, j, k: (i, j)),
        out_shape=jax.ShapeDtypeStruct((m, n), f32),
        compiler_params=_cp(("parallel", "parallel", "arbitrary"), VMEM_LIMIT),
    )(a, b)


def _rope_tables(positions):
    inv_freq = ROPE_THETA ** (-jnp.arange(0, QK_ROPE, 2, dtype=f32) / QK_ROPE)
    ang = positions.astype(f32)[:, None] * inv_freq
    cos, sin = jnp.cos(ang), jnp.sin(ang)
    zero = jnp.zeros_like(cos)
    return (jnp.concatenate([cos, cos, zero, zero], axis=1), jnp.concatenate([zero, sin, zero, zero], axis=1),
            jnp.concatenate([-sin, zero, zero, zero], axis=1))


def _pad256(g):
    return jnp.pad(g.reshape(1, QK_HEAD), ((0, 0), (0, QK_PAD - QK_HEAD)))


def _local_step(x, p, positions, target, gains, wk):
    s = x.shape[0]
    tm = min(256, s)
    tm_ffn = min(128, s)
    tq_f = min(512, s)
    tq_b = min(256, s)
    g_mix, g_qa, g_kva, g_qn, g_kn, g_hgo, g_ffn, g_ple = (gains[n] for n in SMALL)
    g_qn_p, g_kn_p = _pad256(g_qn), _pad256(g_kn)
    g_hgo_f = g_hgo.reshape(1, 512)
    wz, wqb, wkvb, w_o, w_gate, w_up, w_down = (wk[n] for n in ("w_in", "w_qb", "w_kvb", "w_o", "w_gate", "w_up", "w_down"))
    w_pg, w_pp, lb4 = wk["w_ple_gate"], wk["w_ple_proj"], wk["lb_param"]
    cosb, sina, sinb = _rope_tables(positions)

    h1, z = _fwd_in(x, g_mix, wz, tm)
    q, k, v = _fwd_mla_proj(z, cosb, sina, sinb, g_qa, g_kva, wqb, wkvb, g_qn_p, g_kn_p, tm)
    a = _fwd_attn(q, k, v, tq_f)
    o = _fwd_gla(z, lb4)
    x2, cat = _fwd_mix(a, o, z, g_hgo_f, x, w_o, tm)
    x3, gp, up = _fwd_ffn(x2, g_ffn, w_gate, w_up, w_down, tm)

    d3, h3, dpre, dpp, dg_ple, loss = _ple_loss_fwd_bwd(x3, g_ple, w_pg, p, w_pp, target, tm)
    d2, h2, act, dgu, dg_ffn = _bwd_ffn(d3, x2, gp, up, g_ffn, w_gate, w_up, w_down, tm_ffn)
    da, do, dz_hg, dg_hgo = _bwd_mix(d2, w_o, o, z, g_hgo_f, tm)
    dz_hq, dz_hf, dz_hi, dlb4 = _bwd_gla(z, lb4, do)
    dq, dk, dv = _bwd_attn(q, k, v, da, tq_b)
    dz_mla, cqn, ckvn, dq0, dkv0, dg_qa, dg_kva, dg_qn, dg_kn = _bwd_mla_proj(
        z, dq, dk, dv, cosb, sina, sinb, g_qa, g_kva, wqb, wkvb, g_qn_p, g_kn_p, tm)
    segments = [(dz_hq, 512, 0, Z_HQ // 512), (dz_hf, 512, 0, Z_HFF // 512), (dz_hf, 512, 1, Z_HFB // 512),
                (dz_hi, 512, 0, Z_HI // 512), (dz_hg, 512, 0, Z_HG // 512), (dz_mla, 640, 0, Z_CQ // 640)]
    grad_x, dg_mix = _bwd_in(segments, wz, x, g_mix, d2, tm)

    gz = ((Z_HQ, _mm_tn(h1, dz_hq, "dw_in_hq")), (Z_HFF, _mm_tn(h1, dz_hf, "dw_in_hf")), (Z_HI, _mm_tn(h1, dz_hi, "dw_in_hi")),
          (Z_HG, _mm_tn(h1, dz_hg, "dw_in_hg")), (Z_CQ, _mm_tn(h1, dz_mla, "dw_in_mla")))
    grads = {
        "w_in": gz, "w_qb": _mm_tn(cqn, dq0, "dw_qb"), "w_kvb": _mm_tn(ckvn, dkv0, "dw_kvb"), "w_o": _mm_tn(cat, d2, "dw_o"),
        "w_gate_up": _mm_tn(h2, dgu, "dw_gate_up"), "w_down": _mm_tn(act, d3, "dw_down"),
        "w_ple_gate": _mm_tn(h3, dpre, "dw_ple_gate"), "w_ple_proj": _mm_tn(p, dpp, "dw_ple_proj"), "lb_param": dlb4,
    }
    dgains = {"g_mix": dg_mix, "g_qa": dg_qa, "g_kva": dg_kva, "g_qn": dg_qn, "g_kn": dg_kn, "g_hgo": dg_hgo,
              "g_ffn": dg_ffn, "g_ple": dg_ple}
    return loss, grad_x, grads, dgains


COL_SHARDED = ("w_in", "w_gate", "w_up", "w_qb", "w_kvb", "w_ple_proj", "lb_param")
RELAYOUT_BLOCKS = 8


def _col_moves(j):
    lo = BIG["w_in"][1] * j
    w_in = [(max(lo, a) - lo, min(lo + BIG["w_in"][1], b) - lo, d + max(lo, a) - a)
            for a, b, d in Z_SEGMENTS if max(lo, a) < min(lo + BIG["w_in"][1], b)]
    head, half = divmod(j, 2)
    whole = lambda n: [(0, BIG[n][1], BIG[n][1] * j)]
    return {"w_in": w_in, "w_gate": whole("w_gate"), "w_up": whole("w_up"),
            "w_qb": [(0, 96, QK_PAD * head + 96 * half)], "w_kvb": whole("w_kvb"), "w_ple_proj": whole("w_ple_proj"),
            "lb_param": whole("lb_param")}


def _relayout_specs(by_dev):
    specs = []
    for n in COL_SHARDED:
        rows, cols = BIG[n]
        width = {"w_in": Z_W, "w_qb": MLA_HEADS * QK_PAD}.get(n, N_DEV * cols)
        if n == "lb_param":
            specs.append(_acc_spec((N_DEV, rows, cols) if by_dev else (rows, width)))
        elif by_dev:
            specs.append(pl.BlockSpec((N_DEV, rows // RELAYOUT_BLOCKS, cols), lambda i: (0, i, 0)))
        else:
            specs.append(pl.BlockSpec((rows // RELAYOUT_BLOCKS, width), lambda i: (i, 0)))
    return specs


def _weights_in(gathered):
    n = len(COL_SHARDED)

    def body(*refs):
        ins, outs = dict(zip(COL_SHARDED, refs[:n])), dict(zip(COL_SHARDED, refs[n:]))
        outs["w_in"][:, Z_KR + QK_ROPE:Z_W] = jnp.zeros((outs["w_in"].shape[0], Z_W - Z_KR - QK_ROPE), bf16)
        for h in range(MLA_HEADS):
            outs["w_qb"][:, QK_PAD * h + QK_HEAD:QK_PAD * (h + 1)] = jnp.zeros((outs["w_qb"].shape[0], QK_PAD - QK_HEAD), bf16)
        for j in range(N_DEV):
            for name, moves in _col_moves(j).items():
                for s0, s1, d0 in moves:
                    outs[name][:, d0:d0 + s1 - s0] = ins[name][j, :, s0:s1]

    out_shape = []
    for name in COL_SHARDED:
        rows, cols = BIG[name]
        width = {"w_in": Z_W, "w_qb": MLA_HEADS * QK_PAD}.get(name, N_DEV * cols)
        out_shape.append(jax.ShapeDtypeStruct((rows, width), gathered[name].dtype))
    outs = pl.pallas_call(
        body, name="weights_in", grid=(RELAYOUT_BLOCKS,), in_specs=_relayout_specs(True), out_specs=_relayout_specs(False),
        out_shape=out_shape, compiler_params=_cp(("arbitrary",), VMEM_LIMIT),
    )(*[gathered[name] for name in COL_SHARDED])
    wk = dict(zip(COL_SHARDED, outs))
    for name in ("w_o", "w_down", "w_ple_gate"):
        rows, cols = BIG[name]
        wk[name] = gathered[name].reshape(N_DEV * rows, cols)
    return wk


def _grads_out(grads):
    gz = grads["w_in"]
    srcs = [a for _, a in gz] + [grads["w_gate_up"], grads["w_qb"], grads["w_kvb"], grads["w_ple_proj"], grads["lb_param"]]
    n_in, n = len(srcs), len(COL_SHARDED)

    def body(*refs):
        z_refs = refs[:len(gz)]
        gu_ref, qb_ref, kvb_ref, pp_ref, lb_ref = refs[len(gz):n_in]
        outs = dict(zip(COL_SHARDED, refs[n_in:]))

        def z_cols(c0, c1):
            for (start, arr), ref in zip(gz, z_refs):
                if start <= c0 and c1 <= start + arr.shape[1]:
                    return ref[:, c0 - start:c1 - start]

        src = {"w_gate": lambda c0, c1: gu_ref[:, c0:c1], "w_up": lambda c0, c1: gu_ref[:, D_FF + c0:D_FF + c1],
               "w_in": z_cols, "w_qb": lambda c0, c1: qb_ref[:, c0:c1], "w_kvb": lambda c0, c1: kvb_ref[:, c0:c1],
               "w_ple_proj": lambda c0, c1: pp_ref[:, c0:c1], "lb_param": lambda c0, c1: lb_ref[:, c0:c1]}
        for j in range(N_DEV):
            for name, moves in _col_moves(j).items():
                for s0, s1, d0 in moves:
                    outs[name][j, :, s0:s1] = src[name](d0, d0 + s1 - s0)

    in_specs = [pl.BlockSpec((a.shape[0] // RELAYOUT_BLOCKS, a.shape[1]), lambda i: (i, 0)) for a in srcs[:-1]]
    in_specs.append(_acc_spec(srcs[-1].shape))
    outs = pl.pallas_call(
        body, name="grads_out", grid=(RELAYOUT_BLOCKS,), in_specs=in_specs, out_specs=_relayout_specs(True),
        out_shape=[jax.ShapeDtypeStruct((N_DEV, *BIG[name]), f32) for name in COL_SHARDED],
        compiler_params=_cp(("arbitrary",), VMEM_LIMIT),
    )(*srcs)
    blocks = dict(zip(COL_SHARDED, outs))
    for name in ("w_o", "w_down", "w_ple_gate"):
        blocks[name] = grads[name].reshape(N_DEV, *BIG[name])
    return blocks


def kernel(x, p, positions, g_mix, w_in, g_qa, g_kva, w_qb, w_kvb, g_qn, g_kn, lb_param, g_hgo, w_o, g_ffn, w_gate, w_up, w_down, g_ple, w_ple_gate, w_ple_proj, loss_target, m_g_mix, m_w_in, m_g_qa, m_g_kva, m_w_qb, m_w_kvb, m_g_qn, m_g_kn, m_lb_param, m_g_hgo, m_w_o, m_g_ffn, m_w_gate, m_w_up, m_w_down, m_g_ple, m_w_ple_gate, m_w_ple_proj, v_g_mix, v_w_in, v_g_qa, v_g_kva, v_w_qb, v_w_kvb, v_g_qn, v_g_kn, v_lb_param, v_g_hgo, v_w_o, v_g_ffn, v_w_gate, v_w_up, v_w_down, v_g_ple, v_w_ple_gate, v_w_ple_proj):
    w_all = dict(g_mix=g_mix, g_qa=g_qa, g_kva=g_kva, g_qn=g_qn, g_kn=g_kn, g_hgo=g_hgo, g_ffn=g_ffn, g_ple=g_ple,
                 w_in=w_in, w_qb=w_qb, w_kvb=w_kvb, w_o=w_o, w_gate=w_gate, w_up=w_up, w_down=w_down,
                 w_ple_gate=w_ple_gate, w_ple_proj=w_ple_proj, lb_param=lb_param)
    m_all = dict(g_mix=m_g_mix, g_qa=m_g_qa, g_kva=m_g_kva, g_qn=m_g_qn, g_kn=m_g_kn, g_hgo=m_g_hgo, g_ffn=m_g_ffn,
                 g_ple=m_g_ple, w_in=m_w_in, w_qb=m_w_qb, w_kvb=m_w_kvb, w_o=m_w_o, w_gate=m_w_gate, w_up=m_w_up,
                 w_down=m_w_down, w_ple_gate=m_w_ple_gate, w_ple_proj=m_w_ple_proj, lb_param=m_lb_param)
    v_all = dict(g_mix=v_g_mix, g_qa=v_g_qa, g_kva=v_g_kva, g_qn=v_g_qn, g_kn=v_g_kn, g_hgo=v_g_hgo, g_ffn=v_g_ffn,
                 g_ple=v_g_ple, w_in=v_w_in, w_qb=v_w_qb, w_kvb=v_w_kvb, w_o=v_w_o, w_gate=v_w_gate, w_up=v_w_up,
                 w_down=v_w_down, w_ple_gate=v_w_ple_gate, w_ple_proj=v_w_ple_proj, lb_param=v_lb_param)
    xi, yi, ci = lax.axis_index("x"), lax.axis_index("y"), lax.axis_index("c")

    shards = [w_all[n].reshape(BIG[n]) for n in BIG]
    dtypes = [f32 if n == "lb_param" else bf16 for n in BIG]
    wk = _weights_in(dict(zip(BIG, _all_gather(shards, dtypes, "ag_weights"))))

    gains = {n: w_all[n].reshape(1, -1) for n in SMALL}
    loss_tile, grad_x, grads, dgains = _local_step(x[0], p[0, 0], positions[0], loss_target[0], gains, wk)

    blocks = _grads_out(grads)
    g4s = [blocks[n].reshape(4, 2, *BIG[n]) for n in BIG]
    sibs = _sibling_exchange(g4s)
    c_idx = jnp.stack([ci]).astype(jnp.int32)
    idx = jnp.stack([ci, 2 * xi + yi]).astype(jnp.int32)
    at = {n: k for k, n in enumerate(BIG)}
    partials = [None] * len(BIG)
    for group, n_blocks in ROW_BLOCKS.items():
        outs = _chip_partials([g4s[at[n]] for n in group], [sibs[at[n]] for n in group], c_idx, n_blocks, "rs_chip_partials_" + group[0])
        for n, o in zip(group, outs):
            partials[at[n]] = o
    chips = _chip_exchange(partials)
    result = {}
    for group, n_blocks in ROW_BLOCKS.items():
        pick = lambda arrs: [arrs[at[n]] for n in group]
        outs = _adam_shards(idx, pick(g4s), pick(sibs), pick(chips), [w_all[n] for n in group], [m_all[n] for n in group],
                            [v_all[n] for n in group], n_blocks, "adamw_" + group[0])
        result.update(zip(group, outs))

    vec = jnp.concatenate([dgains[n] for n in SMALL] + [loss_tile[0:1]], axis=1)
    parts = _all_gather([vec], [f32], "ag_gains")[0]
    outs, loss_row = _adam_gains(parts, [w_all[n] for n in SMALL], [m_all[n] for n in SMALL], [v_all[n] for n in SMALL])
    result.update(zip(SMALL, outs))

    order = ("g_mix", "w_in", "g_qa", "g_kva", "w_qb", "w_kvb", "g_qn", "g_kn", "lb_param", "g_hgo", "w_o", "g_ffn",
             "w_gate", "w_up", "w_down", "g_ple", "w_ple_gate", "w_ple_proj")
    return (loss_row[0, 0], grad_x[None], *[result[n][k] for k in range(4) for n in order])
```

```python
import functools
import math

import jax
import jax.numpy as jnp
from jax import lax
from jax.experimental import pallas as pl
from jax.experimental.pallas import tpu as pltpu

f32 = jnp.float32
bf16 = jnp.bfloat16

N_DEV = 8
D_MODEL = 1024
MLA_HEADS = 4
QK_NOPE = 128
QK_ROPE = 64
QK_HEAD = QK_NOPE + QK_ROPE
QK_PAD = 256
V_HEAD = 128
Q_LORA = 256
KV_LORA = 256
HG_HEADS = 4
HG_DK = 128
CHUNK = 64
D_FF = 2816
PLE_DIM = 256
ROPE_THETA = 10000.0
EPS = 1e-6
ATTN_SCALE = QK_HEAD ** -0.5
IN_SIZES = (256, 256, 64, 512, 512, 512, 512, 512)
D_IN = sum(IN_SIZES)
Z_HQ, Z_HFF, Z_HFB, Z_HI, Z_HG, Z_CQ, Z_CKV, Z_KR, Z_W = 0, 512, 1024, 1536, 2048, 2560, 2816, 3072, 3200

ADAM_LR, ADAM_B1, ADAM_B2, ADAM_EPS, ADAM_WD, ADAM_STEP = 0.001, 0.9, 0.999, 1e-08, 0.01, 10

LANES = 128
BIG = {"w_in": (1024, 392), "w_qb": (256, 96), "w_kvb": (256, 128), "w_o": (128, 1024), "w_gate": (1024, 352),
       "w_up": (1024, 352), "w_down": (352, 1024), "w_ple_gate": (128, 1024), "w_ple_proj": (256, 128),
       "lb_param": (4, 64)}
ROW_BLOCKS = {("w_in", "w_qb", "w_kvb", "w_o", "w_gate", "w_up", "w_ple_gate", "w_ple_proj"): 8, ("w_down", "lb_param"): 2}
SMALL = {"g_mix": (0, 1024), "g_qa": (1024, 256), "g_kva": (1280, 256), "g_qn": (1536, 192), "g_kn": (1792, 192),
         "g_hgo": (2048, 512), "g_ffn": (2560, 1024), "g_ple": (3584, 1024)}
LOSS_OFF = 4608
GAIN_VEC = LOSS_OFF + LANES
Z_SEGMENTS = ((0, 256, Z_CQ), (256, 512, Z_CKV), (512, 576, Z_KR), (576, 1088, Z_HQ), (1088, 1600, Z_HFF),
              (1600, 2112, Z_HFB), (2112, 2624, Z_HI), (2624, 3136, Z_HG))

VMEM_LIMIT = 56 * 1024 * 1024
MESH = pl.DeviceIdType.MESH


def _cp(sem=None, vmem=None):
    return pltpu.CompilerParams(dimension_semantics=sem, vmem_limit_bytes=vmem)


def _const_spec(shape):
    nd = len(shape)
    return pl.BlockSpec(shape, lambda *_: (0,) * nd, pipeline_mode=pl.Buffered(1))


def _acc_spec(shape):
    nd = len(shape)
    return pl.BlockSpec(shape, lambda *_: (0,) * nd)


def _sigmoid(x):
    return jax.nn.sigmoid(x)


def _dot(a, b):
    return jnp.dot(a, b, preferred_element_type=f32)


def _dot_nt(a, b):
    return lax.dot_general(a, b, (((1,), (1,)), ((), ())), preferred_element_type=f32)


def _dot_tn(a, b):
    return lax.dot_general(a, b, (((0,), (0,)), ((), ())), preferred_element_type=f32)


def _rms_fwd(x, g, width):
    r = lax.rsqrt(jnp.sum(x * x, axis=-1, keepdims=True) * (1.0 / width) + EPS)
    return x * r * g, r


def _rms_bwd(dy, x, r, g, width):
    u = dy * g
    dx = r * u - x * (r * r * r) * (jnp.sum(u * x, axis=-1, keepdims=True) * (1.0 / width))
    return dx, dy * x * r


def _rope(b, c, sa, sb):
    return b * c + pltpu.roll(b, 32, 1) * sa + pltpu.roll(b, 96, 1) * sb


def _all_gather(shards, dtypes, name):
    n = len(shards)

    def body(*refs):
        in_refs, out_refs, stage = refs[:n], refs[n:2 * n], refs[2 * n:3 * n]
        send_sems, recv_sems, local_sems = refs[3 * n:]
        for w in range(n):
            stage[w][...] = in_refs[w][...].astype(stage[w].dtype)
        x, y, c = lax.axis_index("x"), lax.axis_index("y"), lax.axis_index("c")
        me, sibling = (x, y, c), (x, y, 1 - c)
        chips = [(1 - x, y), (x, 1 - y), (1 - x, 1 - y)]

        def slot(w, px, py, pc):
            return out_refs[w].at[4 * px + 2 * py + pc]

        def copy(w, k, block, to, src=None):
            return pltpu.make_async_remote_copy(
                src_ref=slot(w, *block) if src is None else src, dst_ref=slot(w, *block),
                send_sem=send_sems.at[w, k], recv_sem=recv_sems.at[w, k], device_id=to, device_id_type=MESH)

        first = []
        for j, chip in enumerate(chips):
            first += [copy(w, 1 + j, me, (*chip, c), src=stage[w]) for w in range(n)]
        first += [copy(w, 0, me, sibling, src=stage[w]) for w in range(n)]
        mine = [pltpu.make_async_copy(stage[w], slot(w, *me), local_sems.at[w]) for w in range(n)]
        for cp in first + mine:
            cp.start()
        passed = []
        for j, chip in enumerate(chips):
            for w in range(n):
                copy(w, 1 + j, (*chip, c), me).wait_recv()
                passed.append(copy(w, 4 + j, (*chip, c), sibling))
                passed[-1].start()
        for w in range(n):
            copy(w, 0, sibling, me).wait_recv()
        for j, chip in enumerate(chips):
            for w in range(n):
                copy(w, 4 + j, (*chip, 1 - c), me).wait_recv()
        for cp in first + passed:
            cp.wait_send()
        for cp in mine:
            cp.wait()

    return pl.pallas_call(
        body, name=name,
        out_shape=[jax.ShapeDtypeStruct((N_DEV, *s.shape), dt) for s, dt in zip(shards, dtypes)],
        in_specs=[pl.BlockSpec(memory_space=pltpu.VMEM)] * n,
        out_specs=[pl.BlockSpec(memory_space=pl.ANY)] * n,
        scratch_shapes=[pltpu.VMEM(s.shape, dt) for s, dt in zip(shards, dtypes)]
        + [pltpu.SemaphoreType.DMA((n, 7)), pltpu.SemaphoreType.DMA((n, 7)), pltpu.SemaphoreType.DMA((n,))],
        compiler_params=_cp(None, VMEM_LIMIT),
    )(*shards)


N_PEERS = N_DEV - 1
HBM_SPEC = pl.BlockSpec(memory_space=pltpu.HBM)
SEM_SPEC = pl.BlockSpec(memory_space=pltpu.SEMAPHORE)
DATAFLOW = pltpu.SideEffectType.DATAFLOW_SIDE_EFFECTING


def _me():
    return 4 * lax.axis_index("x") + 2 * lax.axis_index("y") + lax.axis_index("c")


def _peer(k):
    x, y, c = lax.axis_index("x"), lax.axis_index("y"), lax.axis_index("c")
    px = 1 - x if k & 4 else x
    py = 1 - y if k & 2 else y
    pc = 1 - c if k & 1 else c
    return (px, py, pc), 4 * px + 2 * py + pc


def _exchange_copies(src_refs, land_refs, send_sems, recv_sems, gather):
    cps = []
    me = _me()
    for k in range(1, N_DEV):
        peer, peer_idx = _peer(k)
        for w, land in enumerate(land_refs):
            src = land.at[me] if gather else src_refs[w].at[peer_idx]
            dst = land.at[me] if gather else land.at[k - 1]
            cps.append(pltpu.make_async_remote_copy(
                src_ref=src, dst_ref=dst, send_sem=send_sems.at[N_PEERS * w + k - 1], recv_sem=recv_sems.at[N_PEERS * w + k - 1],
                device_id=peer, device_id_type=MESH))
    return cps


def _exchange_start(srcs, lands, name):
    n_src, n = len(srcs), len(lands)

    def body(*refs):
        src_refs, land_refs = refs[:n_src], refs[n_src:n_src + n]
        send_sems, recv_sems = refs[n_src + n], refs[n_src + n + 1]
        token = refs[-1]
        for cp in _exchange_copies(src_refs, land_refs, send_sems, recv_sems, gather=not n_src):
            cp.start()
        token[...] = jnp.zeros_like(token)

    arrays = [pltpu.with_memory_space_constraint(a, pltpu.HBM) for a in (*srcs, *lands)]
    outs = pl.pallas_call(
        body, name=name,
        out_shape=(pltpu.SemaphoreType.DMA((n * N_PEERS,)), pltpu.SemaphoreType.DMA((n * N_PEERS,)),
                   *[pltpu.HBM(a.shape, a.dtype) for a in arrays], jax.ShapeDtypeStruct((8, LANES), f32)),
        in_specs=[HBM_SPEC] * len(arrays),
        out_specs=(SEM_SPEC, SEM_SPEC, *[HBM_SPEC] * len(arrays), pl.BlockSpec(memory_space=pltpu.VMEM)),
        input_output_aliases={i: 2 + i for i in range(len(arrays))},
        compiler_params=pltpu.CompilerParams(has_side_effects=DATAFLOW),
    )(*arrays)
    return (outs[0], outs[1], outs[2:2 + n_src], outs[2 + n_src:2 + n_src + n]), outs[-1]


def _exchange_wait(state, after, name):
    send_sems, recv_sems, srcs, lands = state
    n_src, n = len(srcs), len(lands)

    def body(*refs):
        src_refs, land_refs = refs[:n_src], refs[n_src:n_src + n]
        send_ref, recv_ref = refs[n_src + n], refs[n_src + n + 1]
        for cp in _exchange_copies(src_refs, land_refs, send_ref, recv_ref, gather=not n_src):
            cp.wait_send()
            cp.wait_recv()

    arrays = (*srcs, *lands)
    outs = pl.pallas_call(
        body, name=name,
        out_shape=tuple(pltpu.HBM(a.shape, a.dtype) for a in arrays),
        in_specs=[HBM_SPEC] * len(arrays) + [SEM_SPEC, SEM_SPEC, pl.BlockSpec(memory_space=pl.ANY)],
        out_specs=tuple([HBM_SPEC] * len(arrays)),
        input_output_aliases={i: i for i in range(len(arrays))},
        compiler_params=pltpu.CompilerParams(has_side_effects=DATAFLOW),
    )(*arrays, send_sems, recv_sems, after)
    return outs[n_src:]


def _cast_to_slot(shards, me_idx):
    n = len(shards)

    def body(i_ref, *refs):
        for w in range(n):
            refs[n + w][...] = refs[w][...].astype(bf16)

    return pl.pallas_call(
        body, name="cast_to_slot",
        grid_spec=pltpu.PrefetchScalarGridSpec(
            num_scalar_prefetch=1, grid=(1,),
            in_specs=[pl.BlockSpec(s.shape, lambda i, m: (0, 0)) for s in shards],
            out_specs=[pl.BlockSpec((None, *s.shape), lambda i, m: (m[0], 0, 0)) for s in shards]),
        out_shape=[jax.ShapeDtypeStruct((N_DEV, *s.shape), bf16) for s in shards],
        compiler_params=_cp(("arbitrary",), VMEM_LIMIT),
    )(me_idx, *shards)


def _row_block(rows, n_blocks):
    return (rows // n_blocks, True) if rows % (16 * n_blocks) == 0 else (rows, False)


def _adam_math(w, g, m, v):
    m = ADAM_B1 * m + (1.0 - ADAM_B1) * g
    v = ADAM_B2 * v + (1.0 - ADAM_B2) * (g * g)
    m_hat = m / (1.0 - ADAM_B1 ** ADAM_STEP)
    v_hat = v / (1.0 - ADAM_B2 ** ADAM_STEP)
    delta = -ADAM_LR * (m_hat / (jnp.sqrt(v_hat) + ADAM_EPS) + ADAM_WD * w)
    return delta, m, v


def _adam_shards(me_idx, blocks, lands, ws, ms, vs, n_blocks, name):
    n = len(blocks)

    def body(i_ref, *refs):
        ins, outs = refs[:5 * n], refs[5 * n:]
        for w in range(n):
            g_ref, b_ref, w_ref, m_ref, v_ref = (ins[t * n + w] for t in range(5))
            g = g_ref[...].astype(f32)
            for k in range(N_PEERS):
                g = g + b_ref[k].astype(f32)
            if len(w_ref.shape) == 2:
                pieces = [(slice(None), g)]
            else:
                pieces = [(a, g[2 * a:2 * a + 2]) for a in range(2)]
            for at, gp in pieces:
                vals = (gp,) + _adam_math(w_ref[at], gp, m_ref[at], v_ref[at])
                for t, val in enumerate(vals):
                    outs[4 * w + t][at] = val

    specs = [[] for _ in range(5)]
    out_specs, out_shape = [], []
    for g, wt in zip(blocks, ws):
        rows, cols = g.shape[1:]
        rb, cut = _row_block(rows, n_blocks)
        specs[0].append(pl.BlockSpec((None, rb, cols), functools.partial(lambda i, s, cut: (s[0], i if cut else 0, 0), cut=cut)))
        specs[1].append(pl.BlockSpec((N_PEERS, rb, cols), functools.partial(lambda i, s, cut: (0, i if cut else 0, 0), cut=cut)))
        if wt.shape[0] == 1:
            shard = pl.BlockSpec((None, rb, cols), functools.partial(lambda i, s, cut: (0, i if cut else 0, 0), cut=cut))
        else:
            shard = pl.BlockSpec(wt.shape, functools.partial(lambda i, s, nd: (0,) * nd, nd=wt.ndim))
        for t in (2, 3, 4):
            specs[t].append(shard)
        out_specs += [shard] * 4
        out_shape += [jax.ShapeDtypeStruct(wt.shape, f32)] * 4
    outs = pl.pallas_call(
        body, name=name,
        grid_spec=pltpu.PrefetchScalarGridSpec(num_scalar_prefetch=1, grid=(n_blocks,), in_specs=sum(specs, []), out_specs=out_specs),
        out_shape=out_shape,
        compiler_params=_cp(("arbitrary",), VMEM_LIMIT),
    )(me_idx, *blocks, *lands, *ws, *ms, *vs)
    return [outs[4 * w:4 * w + 4] for w in range(n)]


def _adam_gains(parts, ws, ms, vs):
    n = len(ws)

    def body(p_ref, *refs):
        ins, outs = refs[:3 * n], refs[3 * n:]
        g_all = p_ref[0]
        for k in range(1, N_DEV):
            g_all = g_all + p_ref[k]
        for w, (off, lanes) in enumerate(SMALL.values()):
            w_ref, m_ref, v_ref = ins[w], ins[n + w], ins[2 * n + w]
            if len(w_ref.shape) == 2:
                pieces = [(slice(None), off, lanes)]
            else:
                pieces = [((slice(None), h), off + LANES * h, LANES) for h in range(w_ref.shape[1])]
            for at, o, ln in pieces:
                g = g_all[:, o:o + ln]
                vals = (g,) + _adam_math(w_ref[at], g, m_ref[at], v_ref[at])
                for t, val in enumerate(vals):
                    outs[4 * w + t][at] = val
        outs[4 * n][...] = g_all[:, LOSS_OFF:LOSS_OFF + LANES]

    out_shape = sum([[jax.ShapeDtypeStruct(w.shape, f32)] * 4 for w in ws], []) + [jax.ShapeDtypeStruct((1, LANES), f32)]
    outs = pl.pallas_call(body, name="adamw_gains", out_shape=out_shape)(parts, *ws, *ms, *vs)
    return [outs[4 * w:4 * w + 4] for w in range(n)], outs[4 * n]


def _fwd_in(x, g_mix, wz, tm):
    s, d = x.shape

    def body(x_ref, g_ref, w_ref, h_ref, z_ref):
        h, _ = _rms_fwd(x_ref[...], g_ref[...], d)
        hb = h.astype(bf16)
        h_ref[...] = hb
        z_ref[...] = _dot(hb, w_ref[...])

    return pl.pallas_call(
        body, name="fwd_in", grid=(s // tm,),
        in_specs=[pl.BlockSpec((tm, d), lambda i: (i, 0)), _const_spec((1, d)), _const_spec((d, Z_W))],
        out_specs=[pl.BlockSpec((tm, d), lambda i: (i, 0)), pl.BlockSpec((tm, Z_W), lambda i: (i, 0))],
        out_shape=[jax.ShapeDtypeStruct((s, d), bf16), jax.ShapeDtypeStruct((s, Z_W), f32)],
        compiler_params=_cp(("parallel",), VMEM_LIMIT),
    )(x, g_mix, wz)


def _mla_qk_fwd(cq, ckv, kr, g_qa, g_kva, wqb, wkvb, g_qn, g_kn):
    cqn, rq = _rms_fwd(cq, g_qa, Q_LORA)
    ckvn, rkv = _rms_fwd(ckv, g_kva, KV_LORA)
    cqn_b, ckvn_b = cqn.astype(bf16), ckvn.astype(bf16)
    q0 = _dot(cqn_b, wqb)
    kv0 = _dot(ckvn_b, wkvb)
    return cqn_b, rq, ckvn_b, rkv, q0, kv0


def _fwd_mla_proj(z, cosb, sina, sinb, g_qa, g_kva, wqb, wkvb, g_qn, g_kn, tm):
    s = z.shape[0]
    hh = MLA_HEADS

    def body(cq_ref, ckv_ref, kr_ref, c_ref, sa_ref, sb_ref, gqa_ref, gkva_ref, wqb_ref, wkvb_ref, gqn_ref, gkn_ref,
             q_ref, k_ref, v_ref):
        _, _, _, _, q0, kv0 = _mla_qk_fwd(cq_ref[...], ckv_ref[...], kr_ref[...], gqa_ref[...], gkva_ref[...],
                                          wqb_ref[...], wkvb_ref[...], gqn_ref[...], gkn_ref[...])
        kr = kr_ref[...]
        c, sa, sb = c_ref[...], sa_ref[...], sb_ref[...]
        gqn, gkn = gqn_ref[...], gkn_ref[...]
        kr_sq = jnp.sum(kr * kr, axis=-1, keepdims=True)
        for h in range(hh):
            qh = q0[:, QK_PAD * h:QK_PAD * (h + 1)]
            qn, _ = _rms_fwd(qh, gqn, QK_HEAD)
            q_ref[h, :, 0:128] = qn[:, 0:128].astype(bf16)
            q_ref[h, :, 128:256] = _rope(qn[:, 128:256], c, sa, sb).astype(bf16)
            kn_ = kv0[:, 256 * h:256 * h + 128]
            rk = lax.rsqrt((jnp.sum(kn_ * kn_, axis=-1, keepdims=True) + kr_sq) * (1.0 / QK_HEAD) + EPS)
            k_ref[h, :, 0:128] = (kn_ * rk * gkn[:, 0:128]).astype(bf16)
            k_ref[h, :, 128:256] = _rope(kr * rk * gkn[:, 128:256], c, sa, sb).astype(bf16)
            v_ref[h] = kv0[:, 256 * h + 128:256 * h + 256].astype(bf16)

    row128 = pl.BlockSpec((tm, 128), lambda i: (i, 0))
    return pl.pallas_call(
        body, name="fwd_mla_proj", grid=(s // tm,),
        in_specs=[pl.BlockSpec((tm, 256), lambda i: (i, Z_CQ // 256)), pl.BlockSpec((tm, 256), lambda i: (i, Z_CKV // 256)),
                  pl.BlockSpec((tm, 128), lambda i: (i, Z_KR // 128)), row128, row128, row128,
                  _const_spec((1, 256)), _const_spec((1, 256)), _const_spec((256, 1024)), _const_spec((256, 1024)),
                  _const_spec((1, 256)), _const_spec((1, 256))],
        out_specs=[pl.BlockSpec((hh, tm, QK_PAD), lambda i: (0, i, 0)), pl.BlockSpec((hh, tm, QK_PAD), lambda i: (0, i, 0)),
                   pl.BlockSpec((hh, tm, V_HEAD), lambda i: (0, i, 0))],
        out_shape=[jax.ShapeDtypeStruct((hh, s, QK_PAD), bf16), jax.ShapeDtypeStruct((hh, s, QK_PAD), bf16),
                   jax.ShapeDtypeStruct((hh, s, V_HEAD), bf16)],
        compiler_params=_cp(("parallel",), VMEM_LIMIT),
    )(z, z, z, cosb, sina, sinb, g_qa, g_kva, wqb, wkvb, g_qn, g_kn)


def _fwd_attn(q, k, v, tq):
    hh, s, _ = q.shape

    def body(q_ref, k_ref, v_ref, o_ref):
        sc = _dot_nt(q_ref[...], k_ref[...]) * ATTN_SCALE
        p = jnp.exp(sc - jnp.max(sc, axis=-1, keepdims=True))
        l = jnp.sum(p, axis=-1, keepdims=True)
        o_ref[...] = (_dot(p.astype(bf16), v_ref[...]) * (1.0 / l)).astype(bf16)

    return pl.pallas_call(
        body, name="fwd_attn", grid=(hh, s // tq),
        in_specs=[pl.BlockSpec((None, tq, QK_PAD), lambda h, i: (h, i, 0)),
                  pl.BlockSpec((None, s, QK_PAD), lambda h, i: (h, 0, 0)),
                  pl.BlockSpec((None, s, V_HEAD), lambda h, i: (h, 0, 0))],
        out_specs=pl.BlockSpec((tq, V_HEAD), lambda h, i: (i, h)),
        out_shape=jax.ShapeDtypeStruct((s, hh * V_HEAD), bf16),
        compiler_params=_cp(("parallel", "parallel"), VMEM_LIMIT),
    )(q, k, v)


def _split3(x):
    hi = x.astype(bf16)
    r1 = x - hi.astype(f32)
    mid = r1.astype(bf16)
    lo = (r1 - mid.astype(f32)).astype(bf16)
    return jnp.concatenate([hi, mid, lo], axis=-1)


def _tri_sum(tri, x):
    y = _dot(tri, _split3(x))
    return y[:, 0:128] + y[:, 128:256] + y[:, 256:384]


def _gla_masks(rev):
    row = lax.broadcasted_iota(jnp.int32, (CHUNK, CHUNK), 0)
    col = lax.broadcasted_iota(jnp.int32, (CHUNK, CHUNK), 1)
    keep, keep_t = (row <= col, row >= col) if rev else (row >= col, row <= col)
    return keep.astype(f32), keep.astype(bf16), keep_t.astype(bf16)


def _gla_gates(hq, hf, lower):
    sg = _sigmoid(hf)
    f = lower + (1.0 - lower) * sg
    return hq * _sigmoid(hq), 1.0 - f, jnp.log(f), f, sg


def _gla_last_mid(b, rev):
    if rev:
        return b[0:1, :], b[CHUNK // 2:CHUNK // 2 + 1, :]
    return b[CHUNK - 1:CHUNK, :], b[CHUNK // 2 - 1:CHUNK // 2, :]


def _gla_rows(n, n_chunks, rev):
    ne = n_chunks - 1 - n if rev else n
    return pl.ds(pl.multiple_of(ne * CHUNK, CHUNK), CHUNK)


GLA_HEADS_PER_STEP = 2


def _fwd_gla(z, lb4):
    s = z.shape[0]
    n_chunks = s // CHUNK
    hp = GLA_HEADS_PER_STEP
    chains = [(hh, rev) for hh in range(hp) for rev in (False, True)]

    def body(hq_ref, hff_ref, hfb_ref, hi_ref, lb_ref, o_ref, st_ref):
        st_ref[...] = jnp.zeros_like(st_ref)
        masks = {rev: _gla_masks(rev) for rev in (False, True)}
        lowers = [_sigmoid(lb_ref[int(rev):int(rev) + 1, 128 * hh:128 * (hh + 1)]
                           - lb_ref[2 + int(rev):3 + int(rev), 128 * hh:128 * (hh + 1)]) for hh, rev in chains]

        def make_step(first):
            def step(n, carry):
                for ci, (hh, rev) in enumerate(chains):
                    cols = slice(128 * hh, 128 * (hh + 1))
                    rows = _gla_rows(n, n_chunks, rev)
                    maskf, tri, _ = masks[rev]
                    hf_ref = hfb_ref if rev else hff_ref
                    q, k, logf, _, _ = _gla_gates(hq_ref[rows, cols], hf_ref[rows, cols], lowers[ci])
                    vb = hi_ref[rows, cols].astype(bf16)
                    b = _tri_sum(tri, logf)
                    b_last, b_mid = _gla_last_mid(b, rev)
                    qi = (q * jnp.exp(b - b_mid)).astype(bf16)
                    ki = (k * jnp.exp(b_mid - b)).astype(bf16)
                    a = (_dot_nt(qi, ki) * maskf).astype(bf16)
                    st = st_ref[ci]
                    o = _dot(a, vb) + _dot_nt((q * jnp.exp(b)).astype(bf16), st.astype(bf16))
                    kt = (k * jnp.exp(b_last - b)).astype(bf16)
                    st_ref[ci] = st * jnp.exp(b_last) + _dot_tn(vb, kt)
                    if first:
                        o_ref[rows, cols] = o
                    else:
                        o_ref[rows, cols] += o
                return carry
            return step

        lax.fori_loop(0, n_chunks // 2, make_step(True), 0)
        lax.fori_loop(n_chunks // 2, n_chunks, make_step(False), 0)

    w = 128 * hp
    col = lambda base: pl.BlockSpec((s, w), lambda h: (0, base // w + h))
    return pl.pallas_call(
        body, name="fwd_gla", grid=(HG_HEADS // hp,),
        in_specs=[col(Z_HQ), col(Z_HFF), col(Z_HFB), col(Z_HI), pl.BlockSpec((4, w), lambda h: (0, h))],
        out_specs=pl.BlockSpec((s, w), lambda h: (0, h)),
        out_shape=jax.ShapeDtypeStruct((s, HG_HEADS * 128), f32),
        scratch_shapes=[pltpu.VMEM((len(chains), 128, 128), f32)],
        compiler_params=_cp(("parallel",), VMEM_LIMIT),
    )(z, z, z, z, lb4)


def _hg_out(o, hg, g_hgo):
    outs, ons, rs = [], [], []
    for h in range(HG_HEADS):
        oh = o[:, 128 * h:128 * (h + 1)]
        on, r = _rms_fwd(oh, g_hgo[:, 128 * h:128 * (h + 1)], 128)
        ons.append(on)
        rs.append(r)
    on = jnp.concatenate(ons, axis=-1)
    sg = _sigmoid(hg)
    return on * (hg * sg), on, rs, sg


def _fwd_mix(a, o, z, g_hgo, x, w_o, tm):
    s, d = x.shape

    def body(a_ref, o_ref, hg_ref, g_ref, x_ref, w_ref, x2_ref, cat_ref):
        r, _, _, _ = _hg_out(o_ref[...], hg_ref[...], g_ref[...])
        cat = jnp.concatenate([a_ref[...], r.astype(bf16)], axis=-1)
        cat_ref[...] = cat
        x2_ref[...] = x_ref[...] + _dot(cat, w_ref[...])

    row512 = pl.BlockSpec((tm, 512), lambda i: (i, 0))
    rowd = pl.BlockSpec((tm, d), lambda i: (i, 0))
    return pl.pallas_call(
        body, name="fwd_mix", grid=(s // tm,),
        in_specs=[row512, row512, pl.BlockSpec((tm, 512), lambda i: (i, Z_HG // 512)), _const_spec((1, 512)), rowd,
                  _const_spec((d, d))],
        out_specs=[rowd, rowd],
        out_shape=[jax.ShapeDtypeStruct((s, d), f32), jax.ShapeDtypeStruct((s, d), bf16)],
        compiler_params=_cp(("parallel",), VMEM_LIMIT),
    )(a, o, z, g_hgo, x, w_o)


def _fwd_ffn(x2, g_ffn, w_gate, w_up, w_down, tm):
    s, d = x2.shape

    def body(x_ref, g_ref, wg_ref, wu_ref, wd_ref, x3_ref, gp_ref, up_ref):
        x = x_ref[...]
        h, _ = _rms_fwd(x, g_ref[...], d)
        hb = h.astype(bf16)
        gp = _dot(hb, wg_ref[...])
        up = _dot(hb, wu_ref[...])
        gp_ref[...] = gp
        up_ref[...] = up
        act = (gp * _sigmoid(gp) * up).astype(bf16)
        x3_ref[...] = x + _dot(act, wd_ref[...])

    rowd = pl.BlockSpec((tm, d), lambda i: (i, 0))
    rowf = pl.BlockSpec((tm, D_FF), lambda i: (i, 0))
    return pl.pallas_call(
        body, name="fwd_ffn", grid=(s // tm,),
        in_specs=[rowd, _const_spec((1, d)), _const_spec((d, D_FF)), _const_spec((d, D_FF)), _const_spec((D_FF, d))],
        out_specs=[rowd, rowf, rowf],
        out_shape=[jax.ShapeDtypeStruct((s, d), f32), jax.ShapeDtypeStruct((s, D_FF), f32),
                   jax.ShapeDtypeStruct((s, D_FF), f32)],
        compiler_params=_cp(("parallel",), VMEM_LIMIT),
    )(x2, g_ffn, w_gate, w_up, w_down)


def _ple_loss_fwd_bwd(x3, g_ple, w_pg, p, w_pp, target, tm):
    s, d = x3.shape

    def body(x_ref, g_ref, wg_ref, p_ref, wp_ref, t_ref, dx_ref, h_ref, dpre_ref, dpp_ref, dg_ref, loss_ref):
        @pl.when(pl.program_id(0) == 0)
        def _():
            dg_ref[...] = jnp.zeros_like(dg_ref)
            loss_ref[...] = jnp.zeros_like(loss_ref)

        x = x_ref[...]
        g = g_ref[...]
        h, r = _rms_fwd(x, g, d)
        hb = h.astype(bf16)
        gate = _sigmoid(_dot(hb, wg_ref[...]))
        pp = _dot(p_ref[...].astype(bf16), wp_ref[...])
        e = x + gate * pp - t_ref[...]
        loss_ref[...] += 0.5 * jnp.sum(e * e) * (1.0 / d)
        dy = e * (1.0 / d)
        dpre = (dy * pp * gate * (1.0 - gate)).astype(bf16)
        dx, dgx = _rms_bwd(_dot_nt(dpre, wg_ref[...]), x, r, g, d)
        dx_ref[...] = dy + dx
        dg_ref[...] += jnp.sum(dgx, axis=0, keepdims=True)
        h_ref[...] = hb
        dpre_ref[...] = dpre
        dpp_ref[...] = (dy * gate).astype(bf16)

    rowd = pl.BlockSpec((tm, d), lambda i: (i, 0))
    return pl.pallas_call(
        body, name="ple_loss_fwd_bwd", grid=(s // tm,),
        in_specs=[rowd, _const_spec((1, d)), _const_spec((d, d)), pl.BlockSpec((tm, PLE_DIM), lambda i: (i, 0)),
                  _const_spec((PLE_DIM, d)), rowd],
        out_specs=[rowd, rowd, rowd, rowd, _acc_spec((1, d)), _acc_spec((8, 128))],
        out_shape=[jax.ShapeDtypeStruct((s, d), f32), jax.ShapeDtypeStruct((s, d), bf16), jax.ShapeDtypeStruct((s, d), bf16),
                   jax.ShapeDtypeStruct((s, d), bf16), jax.ShapeDtypeStruct((1, d), f32), jax.ShapeDtypeStruct((8, 128), f32)],
        compiler_params=_cp(("arbitrary",), VMEM_LIMIT),
    )(x3, g_ple, w_pg, p, w_pp, target)


def _bwd_ffn(d3, x2, gp, up, g_ffn, w_gate, w_up, w_down, tm):
    s, d = x2.shape

    def body(d3_ref, x_ref, gp_ref, up_ref, g_ref, wg_ref, wu_ref, wd_ref, d2_ref, h_ref, act_ref, dgp_ref, dup_ref, dg_ref):
        @pl.when(pl.program_id(0) == 0)
        def _():
            dg_ref[...] = jnp.zeros_like(dg_ref)

        x = x_ref[...]
        g = g_ref[...]
        d3 = d3_ref[...]
        h, r = _rms_fwd(x, g, d)
        h_ref[...] = h.astype(bf16)
        gp, up = gp_ref[...], up_ref[...]
        sg = _sigmoid(gp)
        silu = gp * sg
        act_ref[...] = (silu * up).astype(bf16)
        dact = _dot_nt(d3.astype(bf16), wd_ref[...])
        dgp = (dact * up * (sg * (1.0 + gp * (1.0 - sg)))).astype(bf16)
        dup = (dact * silu).astype(bf16)
        dgp_ref[...] = dgp
        dup_ref[...] = dup
        dh = _dot_nt(dgp, wg_ref[...]) + _dot_nt(dup, wu_ref[...])
        dx, dgx = _rms_bwd(dh, x, r, g, d)
        d2_ref[...] = d3 + dx
        dg_ref[...] += jnp.sum(dgx, axis=0, keepdims=True)

    rowd = pl.BlockSpec((tm, d), lambda i: (i, 0))
    rowf = pl.BlockSpec((tm, D_FF), lambda i: (i, 0))
    return pl.pallas_call(
        body, name="bwd_ffn", grid=(s // tm,),
        in_specs=[rowd, rowd, rowf, rowf, _const_spec((1, d)), _const_spec((d, D_FF)), _const_spec((d, D_FF)),
                  _const_spec((D_FF, d))],
        out_specs=[rowd, rowd, rowf, rowf, rowf, _acc_spec((1, d))],
        out_shape=[jax.ShapeDtypeStruct((s, d), f32), jax.ShapeDtypeStruct((s, d), bf16)] + [jax.ShapeDtypeStruct((s, D_FF), bf16)] * 3
        + [jax.ShapeDtypeStruct((1, d), f32)],
        compiler_params=_cp(("arbitrary",), VMEM_LIMIT),
    )(d3, x2, gp, up, g_ffn, w_gate, w_up, w_down)


def _bwd_mix(d2, w_o, o, z, g_hgo, tm):
    s, d = d2.shape

    def body(d2_ref, w_ref, o_ref, hg_ref, g_ref, da_ref, do_ref, dhg_ref, dg_ref):
        @pl.when(pl.program_id(0) == 0)
        def _():
            dg_ref[...] = jnp.zeros_like(dg_ref)

        dcat = _dot_nt(d2_ref[...].astype(bf16), w_ref[...])
        da_ref[...] = dcat[:, 0:512].astype(bf16)
        dr = dcat[:, 512:1024]
        o, hg, g = o_ref[...], hg_ref[...], g_ref[...]
        _, on, rs, sg = _hg_out(o, hg, g)
        dhg_ref[...] = (dr * on * (sg * (1.0 + hg * (1.0 - sg)))).astype(bf16)
        don = dr * (hg * sg)
        dgs = []
        for h in range(HG_HEADS):
            cols = slice(128 * h, 128 * (h + 1))
            dx, dgx = _rms_bwd(don[:, cols], o[:, cols], rs[h], g[:, cols], 128)
            do_ref[:, cols] = dx
            dgs.append(jnp.sum(dgx, axis=0, keepdims=True))
        dg_ref[...] += jnp.concatenate(dgs, axis=-1)

    row512 = pl.BlockSpec((tm, 512), lambda i: (i, 0))
    return pl.pallas_call(
        body, name="bwd_mix", grid=(s // tm,),
        in_specs=[pl.BlockSpec((tm, d), lambda i: (i, 0)), _const_spec((d, d)), row512,
                  pl.BlockSpec((tm, 512), lambda i: (i, Z_HG // 512)), _const_spec((1, 512))],
        out_specs=[row512, row512, row512, _acc_spec((1, 512))],
        out_shape=[jax.ShapeDtypeStruct((s, 512), bf16), jax.ShapeDtypeStruct((s, 512), f32), jax.ShapeDtypeStruct((s, 512), bf16),
                   jax.ShapeDtypeStruct((1, 512), f32)],
        compiler_params=_cp(("arbitrary",), VMEM_LIMIT),
    )(d2, w_o, o, z, g_hgo)


def _bwd_gla(z, lb4, do):
    s = z.shape[0]
    n_chunks = s // CHUNK

    def body(hq_ref, hff_ref, hfb_ref, hi_ref, lb_ref, do_ref, dhq_ref, dhff_ref, dhfb_ref, dhi_ref, dlb_ref,
             st_all, b_all, dst_ref, dq_acc, dv_acc, dlow_ref):
        dirs = (False, True)
        masks = [_gla_masks(rev) for rev in dirs]
        lowers = [_sigmoid(lb_ref[int(rev):int(rev) + 1, :] - lb_ref[2 + int(rev):3 + int(rev), :]) for rev in dirs]
        hf_refs, dhf_refs = (hff_ref, hfb_ref), (dhff_ref, dhfb_ref)

        def fwd_step(n, sts):
            new = []
            for d, rev in enumerate(dirs):
                rows = _gla_rows(n, n_chunks, rev)
                _, k, logf, _, _ = _gla_gates(hq_ref[rows, :], hf_refs[d][rows, :], lowers[d])
                b = _tri_sum(masks[d][1], logf)
                b_last, _ = _gla_last_mid(b, rev)
                b_all[d, rows, :] = b
                st_all[d, n] = sts[d]
                kt = (k * jnp.exp(b_last - b)).astype(bf16)
                new.append(sts[d] * jnp.exp(b_last) + _dot_tn(hi_ref[rows, :].astype(bf16), kt))
            return tuple(new)

        zero = jnp.zeros((128, 128), f32)
        lax.fori_loop(0, n_chunks, fwd_step, (zero, zero))

        dst_ref[...] = jnp.zeros_like(dst_ref)
        dlow_ref[...] = jnp.zeros_like(dlow_ref)

        def make_bwd_step(first):
            def bwd_step(j, carry):
                n = n_chunks - 1 - j
                for d, rev in enumerate(dirs):
                    maskf, _, tri_t = masks[d]
                    lower = lowers[d]
                    rows = _gla_rows(n, n_chunks, rev)
                    hq, hf = hq_ref[rows, :], hf_refs[d][rows, :]
                    q, k, _, f, sg = _gla_gates(hq, hf, lower)
                    v = hi_ref[rows, :]
                    dout = do_ref[rows, :]
                    b = b_all[d, rows, :]
                    b_last, b_mid = _gla_last_mid(b, rev)
                    e1, e2, e3, e4 = jnp.exp(b - b_mid), jnp.exp(b_mid - b), jnp.exp(b_last - b), jnp.exp(b)
                    decay = jnp.exp(b_last)
                    qi, ki, kt, qt = q * e1, k * e2, k * e3, q * e4
                    qib, kib, ktb, qtb = qi.astype(bf16), ki.astype(bf16), kt.astype(bf16), qt.astype(bf16)
                    vb, dob = v.astype(bf16), dout.astype(bf16)
                    st = st_all[d, n]
                    dst = dst_ref[d]
                    stb, dstb = st.astype(bf16), dst.astype(bf16)
                    a = (_dot_nt(qib, kib) * maskf).astype(bf16)
                    dv = _dot_tn(a, dob) + _dot_nt(ktb, dstb)
                    da = (_dot_nt(dob, vb) * maskf).astype(bf16)
                    dqi = _dot(da, kib)
                    dki = _dot_tn(da, qib)
                    dqt = _dot(dob, stb)
                    dkt = _dot(vb, dstb)
                    ddecay = jnp.sum(dst * st, axis=0, keepdims=True)
                    dq = dqi * e1 + dqt * e4
                    dk = dki * e2 + dkt * e3
                    db = dqi * qi - dki * ki + dqt * qt - dkt * kt
                    dlast = jnp.sum(dkt * kt, axis=0, keepdims=True) + ddecay * decay
                    dlogf = _tri_sum(tri_t, db) + dlast
                    dst_ref[d] = dst * decay + _dot_tn(dob, qtb)
                    df = dlogf / f - dk
                    dhf_refs[d][rows, :] = (df * (1.0 - lower) * sg * (1.0 - sg)).astype(bf16)
                    dlow_ref[d:d + 1, :] += jnp.sum(df * (1.0 - sg), axis=0, keepdims=True)
                    sq = _sigmoid(hq)
                    dhq = dq * (sq * (1.0 + hq * (1.0 - sq)))
                    if first:
                        dq_acc[rows, :] = dhq
                        dv_acc[rows, :] = dv
                    else:
                        dhq_ref[rows, :] = (dq_acc[rows, :] + dhq).astype(bf16)
                        dhi_ref[rows, :] = (dv_acc[rows, :] + dv).astype(bf16)
                return carry
            return bwd_step

        lax.fori_loop(0, n_chunks // 2, make_bwd_step(True), 0)
        lax.fori_loop(n_chunks // 2, n_chunks, make_bwd_step(False), 0)

        for d in range(2):
            dl = dlow_ref[d:d + 1, :] * lowers[d] * (1.0 - lowers[d])
            dlb_ref[d:d + 1, :] = dl
            dlb_ref[2 + d:3 + d, :] = -dl

    col = lambda base: pl.BlockSpec((s, 128), lambda h: (0, base // 128 + h))
    return pl.pallas_call(
        body, name="bwd_gla", grid=(HG_HEADS,),
        in_specs=[col(Z_HQ), col(Z_HFF), col(Z_HFB), col(Z_HI), pl.BlockSpec((4, 128), lambda h: (0, h)), col(0)],
        out_specs=[col(0), col(0), col(0), col(0), pl.BlockSpec((4, 128), lambda h: (0, h))],
        out_shape=[jax.ShapeDtypeStruct((s, 512), bf16)] * 4 + [jax.ShapeDtypeStruct((4, 512), f32)],
        scratch_shapes=[pltpu.VMEM((2, n_chunks, 128, 128), f32), pltpu.VMEM((2, s, 128), f32), pltpu.VMEM((2, 128, 128), f32),
                        pltpu.VMEM((s, 128), f32), pltpu.VMEM((s, 128), f32), pltpu.VMEM((2, 128), f32)],
        compiler_params=_cp(("parallel",), VMEM_LIMIT),
    )(z, z, z, z, lb4, do)


def _bwd_attn(q, k, v, da, tq):
    hh, s, _ = q.shape

    def body(q_ref, k_ref, v_ref, do_ref, dq_ref, dk_ref, dv_ref):
        @pl.when(pl.program_id(1) == 0)
        def _():
            dk_ref[...] = jnp.zeros_like(dk_ref)
            dv_ref[...] = jnp.zeros_like(dv_ref)

        qb, kb, vb, dob = q_ref[...], k_ref[...], v_ref[...], do_ref[...]
        sc = _dot_nt(qb, kb) * ATTN_SCALE
        p = jnp.exp(sc - jnp.max(sc, axis=-1, keepdims=True))
        w = p * (1.0 / jnp.sum(p, axis=-1, keepdims=True))
        dw = _dot_nt(dob, vb)
        ds = (w * (dw - jnp.sum(dw * w, axis=-1, keepdims=True)) * ATTN_SCALE).astype(bf16)
        dq_ref[...] = _dot(ds, kb)
        dk_ref[...] += _dot_tn(ds, qb)
        dv_ref[...] += _dot_tn(w.astype(bf16), dob)

    return pl.pallas_call(
        body, name="bwd_attn", grid=(hh, s // tq),
        in_specs=[pl.BlockSpec((None, tq, QK_PAD), lambda h, i: (h, i, 0)),
                  pl.BlockSpec((None, s, QK_PAD), lambda h, i: (h, 0, 0)),
                  pl.BlockSpec((None, s, V_HEAD), lambda h, i: (h, 0, 0)),
                  pl.BlockSpec((tq, V_HEAD), lambda h, i: (i, h))],
        out_specs=[pl.BlockSpec((None, tq, QK_PAD), lambda h, i: (h, i, 0)),
                   pl.BlockSpec((None, s, QK_PAD), lambda h, i: (h, 0, 0)),
                   pl.BlockSpec((None, s, V_HEAD), lambda h, i: (h, 0, 0))],
        out_shape=[jax.ShapeDtypeStruct((hh, s, QK_PAD), f32), jax.ShapeDtypeStruct((hh, s, QK_PAD), f32),
                   jax.ShapeDtypeStruct((hh, s, V_HEAD), f32)],
        compiler_params=_cp(("parallel", "arbitrary"), VMEM_LIMIT),
    )(q, k, v, da)


def _bwd_mla_proj(z, dq, dk, dv, cosb, sina, sinb, g_qa, g_kva, wqb, wkvb, g_qn, g_kn, tm):
    s = z.shape[0]
    hh = MLA_HEADS

    def body(cq_ref, ckv_ref, kr_ref, dq_ref, dk_ref, dv_ref, c_ref, sa_ref, sb_ref, gqa_ref, gkva_ref, wqb_ref, wkvb_ref,
             gqn_ref, gkn_ref, dz_ref, cqn_ref, ckvn_ref, dq0_ref, dkv0_ref, dgqa_ref, dgkva_ref, dgqn_ref, dgkn_ref):
        @pl.when(pl.program_id(0) == 0)
        def _():
            for r in (dgqa_ref, dgkva_ref, dgqn_ref, dgkn_ref):
                r[...] = jnp.zeros_like(r)

        cq, ckv, kr = cq_ref[...], ckv_ref[...], kr_ref[...]
        gqa, gkva, gqn, gkn = gqa_ref[...], gkva_ref[...], gqn_ref[...], gkn_ref[...]
        cqn_b, rq, ckvn_b, rkv, q0, kv0 = _mla_qk_fwd(cq, ckv, kr, gqa, gkva, wqb_ref[...], wkvb_ref[...], gqn, gkn)
        cqn_ref[...] = cqn_b
        ckvn_ref[...] = ckvn_b
        c, sa, sb = c_ref[...], -sa_ref[...], -sb_ref[...]
        kr_sq = jnp.sum(kr * kr, axis=-1, keepdims=True)
        dkr = jnp.zeros_like(kr)
        dgqn = jnp.zeros((1, QK_PAD), f32)
        dgkn = jnp.zeros((1, QK_PAD), f32)
        for h in range(hh):
            qh = q0[:, QK_PAD * h:QK_PAD * (h + 1)]
            rh = lax.rsqrt(jnp.sum(qh * qh, axis=-1, keepdims=True) * (1.0 / QK_HEAD) + EPS)
            dqh = dq_ref[h]
            dqn = jnp.concatenate([dqh[:, 0:128], _rope(dqh[:, 128:256], c, sa, sb)], axis=-1)
            dq0h, dgx = _rms_bwd(dqn, qh, rh, gqn, QK_HEAD)
            dq0_ref[:, QK_PAD * h:QK_PAD * (h + 1)] = dq0h.astype(bf16)
            dgqn = dgqn + jnp.sum(dgx, axis=0, keepdims=True)

            kn_ = kv0[:, 256 * h:256 * h + 128]
            k0 = jnp.concatenate([kn_, kr], axis=-1)
            rk = lax.rsqrt((jnp.sum(kn_ * kn_, axis=-1, keepdims=True) + kr_sq) * (1.0 / QK_HEAD) + EPS)
            dkh = dk_ref[h]
            dkn = jnp.concatenate([dkh[:, 0:128], _rope(dkh[:, 128:256], c, sa, sb)], axis=-1)
            dk0, dgx = _rms_bwd(dkn, k0, rk, gkn, QK_HEAD)
            dgkn = dgkn + jnp.sum(dgx, axis=0, keepdims=True)
            dkv0_ref[:, 256 * h:256 * h + 128] = dk0[:, 0:128].astype(bf16)
            dkv0_ref[:, 256 * h + 128:256 * h + 256] = dv_ref[h].astype(bf16)
            dkr = dkr + dk0[:, 128:256]
        dgqn_ref[...] += dgqn
        dgkn_ref[...] += dgkn
        dcq, dgx = _rms_bwd(_dot_nt(dq0_ref[...], wqb_ref[...]), cq, rq, gqa, Q_LORA)
        dgqa_ref[...] += jnp.sum(dgx, axis=0, keepdims=True)
        dckv, dgx = _rms_bwd(_dot_nt(dkv0_ref[...], wkvb_ref[...]), ckv, rkv, gkva, KV_LORA)
        dgkva_ref[...] += jnp.sum(dgx, axis=0, keepdims=True)
        dz_ref[:, 0:256] = dcq.astype(bf16)
        dz_ref[:, 256:512] = dckv.astype(bf16)
        dz_ref[:, 512:640] = dkr.astype(bf16)

    row128 = pl.BlockSpec((tm, 128), lambda i: (i, 0))
    row256 = pl.BlockSpec((tm, 256), lambda i: (i, 0))
    row1024 = pl.BlockSpec((tm, 1024), lambda i: (i, 0))
    hd = lambda w: pl.BlockSpec((hh, tm, w), lambda i: (0, i, 0))
    return pl.pallas_call(
        body, name="bwd_mla_proj", grid=(s // tm,),
        in_specs=[pl.BlockSpec((tm, 256), lambda i: (i, Z_CQ // 256)), pl.BlockSpec((tm, 256), lambda i: (i, Z_CKV // 256)),
                  pl.BlockSpec((tm, 128), lambda i: (i, Z_KR // 128)), hd(QK_PAD), hd(QK_PAD), hd(V_HEAD),
                  row128, row128, row128,
                  _const_spec((1, 256)), _const_spec((1, 256)), _const_spec((256, 1024)), _const_spec((256, 1024)),
                  _const_spec((1, 256)), _const_spec((1, 256))],
        out_specs=[pl.BlockSpec((tm, 640), lambda i: (i, 0)), row256, row256, row1024, row1024,
                   _acc_spec((1, 256)), _acc_spec((1, 256)), _acc_spec((1, 256)), _acc_spec((1, 256))],
        out_shape=[jax.ShapeDtypeStruct((s, 640), bf16), jax.ShapeDtypeStruct((s, 256), bf16), jax.ShapeDtypeStruct((s, 256), bf16),
                   jax.ShapeDtypeStruct((s, 1024), bf16), jax.ShapeDtypeStruct((s, 1024), bf16)]
        + [jax.ShapeDtypeStruct((1, 256), f32)] * 4,
        compiler_params=_cp(("arbitrary",), VMEM_LIMIT),
    )(z, z, z, dq, dk, dv, cosb, sina, sinb, g_qa, g_kva, wqb, wkvb, g_qn, g_kn)


def _bwd_in(segments, wz, x, g_mix, d2, tm):
    s, d = x.shape
    n_seg = len(segments)

    def body(*refs):
        dz_refs, w_refs = refs[:n_seg], refs[n_seg:2 * n_seg]
        x_ref, g_ref, d2_ref, gx_ref, dg_ref = refs[2 * n_seg:]

        @pl.when(pl.program_id(0) == 0)
        def _():
            dg_ref[...] = jnp.zeros_like(dg_ref)

        dh = _dot_nt(dz_refs[0][...], w_refs[0][...])
        for a_ref, w_ref in zip(dz_refs[1:], w_refs[1:]):
            dh = dh + _dot_nt(a_ref[...], w_ref[...])
        x, g = x_ref[...], g_ref[...]
        r = lax.rsqrt(jnp.sum(x * x, axis=-1, keepdims=True) * (1.0 / d) + EPS)
        dx, dgx = _rms_bwd(dh, x, r, g, d)
        gx_ref[...] = d2_ref[...] + dx
        dg_ref[...] += jnp.sum(dgx, axis=0, keepdims=True)

    rowd = pl.BlockSpec((tm, d), lambda i: (i, 0))
    dz_specs = [pl.BlockSpec((tm, w), functools.partial(lambda i, j: (i, j), j=ja)) for _, w, ja, _ in segments]
    w_specs = [pl.BlockSpec((d, w), functools.partial(lambda i, j: (0, j), j=jw), pipeline_mode=pl.Buffered(1))
               for _, w, _, jw in segments]
    return pl.pallas_call(
        body, name="bwd_in", grid=(s // tm,),
        in_specs=dz_specs + w_specs + [rowd, _const_spec((1, d)), rowd],
        out_specs=[rowd, _acc_spec((1, d))],
        out_shape=[jax.ShapeDtypeStruct((s, d), f32), jax.ShapeDtypeStruct((1, d), f32)],
        compiler_params=_cp(("arbitrary",), VMEM_LIMIT),
    )(*[a for a, _, _, _ in segments], *([wz] * n_seg), x, g_mix, d2)


def _pick_tile(n, cap):
    best = None
    for t in range(LANES, cap + 1, LANES):
        if n % t == 0:
            best = t
    return best if best is not None else n


def _mm_tn(a, b, name):
    kk, m = a.shape
    _, n = b.shape
    tm = _pick_tile(m, 1408)
    tn = _pick_tile(n, 1408)
    tk = min(512, kk)

    n_k = kk // tk

    def body(a_ref, b_ref, o_ref, acc_ref):
        @pl.when(pl.program_id(2) == 0)
        def _():
            acc_ref[...] = jnp.zeros_like(acc_ref)
        acc_ref[...] += _dot_tn(a_ref[...].astype(bf16), b_ref[...].astype(bf16))

        @pl.when(pl.program_id(2) == n_k - 1)
        def _():
            o_ref[...] = acc_ref[...].astype(bf16)

    return pl.pallas_call(
        body, name=name, grid=(m // tm, n // tn, n_k),
        in_specs=[pl.BlockSpec((tk, tm), lambda i, j, k: (k, i)), pl.BlockSpec((tk, tn), lambda i, j, k: (k, j))],
        out_specs=pl.BlockSpec((tm, tn), lambda i, j, k: (i, j)),
        out_shape=jax.ShapeDtypeStruct((m, n), bf16),
        scratch_shapes=[pltpu.VMEM((tm, tn), f32)],
        compiler_params=_cp(("parallel", "parallel", "arbitrary"), VMEM_LIMIT),
    )(a, b)


def _rope_tables(positions):
    inv_freq = ROPE_THETA ** (-jnp.arange(0, QK_ROPE, 2, dtype=f32) / QK_ROPE)
    ang = positions.astype(f32)[:, None] * inv_freq
    cos, sin = jnp.cos(ang), jnp.sin(ang)
    zero = jnp.zeros_like(cos)
    return (jnp.concatenate([cos, cos, zero, zero], axis=1), jnp.concatenate([zero, sin, zero, zero], axis=1),
            jnp.concatenate([-sin, zero, zero, zero], axis=1))


def _pad256(g):
    return jnp.pad(g.reshape(1, QK_HEAD), ((0, 0), (0, QK_PAD - QK_HEAD)))


RELAYOUT_BLOCKS = 8
FIRST = ("w_in", "w_qb", "w_kvb", "lb_param")
SECOND = ("w_o", "w_gate", "w_up", "w_down", "w_ple_gate", "w_ple_proj")
ROW_SHARDED = ("w_o", "w_down", "w_ple_gate")


def _col_moves(j):
    lo = BIG["w_in"][1] * j
    w_in = [(max(lo, a) - lo, min(lo + BIG["w_in"][1], b) - lo, d + max(lo, a) - a)
            for a, b, d in Z_SEGMENTS if max(lo, a) < min(lo + BIG["w_in"][1], b)]
    head, half = divmod(j, 2)
    whole = lambda n: [(0, BIG[n][1], BIG[n][1] * j)]
    return {"w_in": w_in, "w_gate": whole("w_gate"), "w_up": whole("w_up"),
            "w_qb": [(0, 96, QK_PAD * head + 96 * half)], "w_kvb": whole("w_kvb"), "w_ple_proj": whole("w_ple_proj"),
            "lb_param": whole("lb_param")}


def _kernel_width(name):
    return {"w_in": Z_W, "w_qb": MLA_HEADS * QK_PAD}.get(name, N_DEV * BIG[name][1])


def _relayout_specs(names, by_dev):
    specs = []
    for n in names:
        rows, cols = BIG[n]
        if n == "lb_param":
            specs.append(_acc_spec((N_DEV, rows, cols) if by_dev else (rows, _kernel_width(n))))
        elif by_dev:
            specs.append(pl.BlockSpec((N_DEV, rows // RELAYOUT_BLOCKS, cols), lambda i: (0, i, 0)))
        else:
            specs.append(pl.BlockSpec((rows // RELAYOUT_BLOCKS, _kernel_width(n)), lambda i: (i, 0)))
    return specs


def _weights_in(gathered, names, name):
    n = len(names)

    def body(*refs):
        ins, outs = dict(zip(names, refs[:n])), dict(zip(names, refs[n:]))
        if "w_in" in outs:
            outs["w_in"][:, Z_KR + QK_ROPE:Z_W] = jnp.zeros((outs["w_in"].shape[0], Z_W - Z_KR - QK_ROPE), bf16)
        if "w_qb" in outs:
            for h in range(MLA_HEADS):
                outs["w_qb"][:, QK_PAD * h + QK_HEAD:QK_PAD * (h + 1)] = jnp.zeros((outs["w_qb"].shape[0], QK_PAD - QK_HEAD), bf16)
        for j in range(N_DEV):
            for wn, moves in _col_moves(j).items():
                if wn in outs:
                    for s0, s1, d0 in moves:
                        outs[wn][:, d0:d0 + s1 - s0] = ins[wn][j, :, s0:s1]

    outs = pl.pallas_call(
        body, name=name, grid=(RELAYOUT_BLOCKS,), in_specs=_relayout_specs(names, True), out_specs=_relayout_specs(names, False),
        out_shape=[jax.ShapeDtypeStruct((BIG[wn][0], _kernel_width(wn)), gathered[wn].dtype) for wn in names],
        compiler_params=_cp(("arbitrary",), VMEM_LIMIT),
    )(*[gathered[wn] for wn in names])
    return dict(zip(names, outs))


def _grads_out(sources, names, name):
    pieces = [(wn, start, arr) for wn in names for start, arr in sources[wn]]
    n_in = len(pieces)

    def body(*refs):
        outs = dict(zip(names, refs[n_in:]))

        def cols(wn, c0, c1):
            for (pn, start, arr), ref in zip(pieces, refs[:n_in]):
                if pn == wn and start <= c0 and c1 <= start + arr.shape[1]:
                    return ref[:, c0 - start:c1 - start]

        for j in range(N_DEV):
            for wn, moves in _col_moves(j).items():
                if wn in outs:
                    for s0, s1, d0 in moves:
                        outs[wn][j, :, s0:s1] = cols(wn, d0, d0 + s1 - s0).astype(bf16)

    in_specs = [_acc_spec(arr.shape) if wn == "lb_param" else pl.BlockSpec((arr.shape[0] // RELAYOUT_BLOCKS, arr.shape[1]), lambda i: (i, 0))
                for wn, _, arr in pieces]
    outs = pl.pallas_call(
        body, name=name, grid=(RELAYOUT_BLOCKS,), in_specs=in_specs, out_specs=_relayout_specs(names, True),
        out_shape=[jax.ShapeDtypeStruct((N_DEV, *BIG[wn]), bf16) for wn in names],
        compiler_params=_cp(("arbitrary",), VMEM_LIMIT),
    )(*[arr for _, _, arr in pieces])
    return dict(zip(names, outs))


def kernel(x, p, positions, g_mix, w_in, g_qa, g_kva, w_qb, w_kvb, g_qn, g_kn, lb_param, g_hgo, w_o, g_ffn, w_gate, w_up, w_down, g_ple, w_ple_gate, w_ple_proj, loss_target, m_g_mix, m_w_in, m_g_qa, m_g_kva, m_w_qb, m_w_kvb, m_g_qn, m_g_kn, m_lb_param, m_g_hgo, m_w_o, m_g_ffn, m_w_gate, m_w_up, m_w_down, m_g_ple, m_w_ple_gate, m_w_ple_proj, v_g_mix, v_w_in, v_g_qa, v_g_kva, v_w_qb, v_w_kvb, v_g_qn, v_g_kn, v_lb_param, v_g_hgo, v_w_o, v_g_ffn, v_w_gate, v_w_up, v_w_down, v_g_ple, v_w_ple_gate, v_w_ple_proj):
    w_all = dict(g_mix=g_mix, g_qa=g_qa, g_kva=g_kva, g_qn=g_qn, g_kn=g_kn, g_hgo=g_hgo, g_ffn=g_ffn, g_ple=g_ple,
                 w_in=w_in, w_qb=w_qb, w_kvb=w_kvb, w_o=w_o, w_gate=w_gate, w_up=w_up, w_down=w_down,
                 w_ple_gate=w_ple_gate, w_ple_proj=w_ple_proj, lb_param=lb_param)
    m_all = dict(g_mix=m_g_mix, g_qa=m_g_qa, g_kva=m_g_kva, g_qn=m_g_qn, g_kn=m_g_kn, g_hgo=m_g_hgo, g_ffn=m_g_ffn,
                 g_ple=m_g_ple, w_in=m_w_in, w_qb=m_w_qb, w_kvb=m_w_kvb, w_o=m_w_o, w_gate=m_w_gate, w_up=m_w_up,
                 w_down=m_w_down, w_ple_gate=m_w_ple_gate, w_ple_proj=m_w_ple_proj, lb_param=m_lb_param)
    v_all = dict(g_mix=v_g_mix, g_qa=v_g_qa, g_kva=v_g_kva, g_qn=v_g_qn, g_kn=v_g_kn, g_hgo=v_g_hgo, g_ffn=v_g_ffn,
                 g_ple=v_g_ple, w_in=v_w_in, w_qb=v_w_qb, w_kvb=v_w_kvb, w_o=v_w_o, w_gate=v_w_gate, w_up=v_w_up,
                 w_down=v_w_down, w_ple_gate=v_w_ple_gate, w_ple_proj=v_w_ple_proj, lb_param=v_lb_param)
    me_idx = jnp.stack([_me()]).astype(jnp.int32)
    x, p, positions, target = x[0], p[0, 0], positions[0], loss_target[0]
    s = x.shape[0]
    tm, tm_ffn, tq_f, tq_b = min(256, s), min(128, s), min(512, s), min(256, s)
    g_mix, g_qa, g_kva, g_qn, g_kn, g_hgo, g_ffn, g_ple = (w_all[n].reshape(1, -1) for n in SMALL)
    g_qn_p, g_kn_p = _pad256(g_qn), _pad256(g_kn)
    cosb, sina, sinb = _rope_tables(positions)
    shard = lambda n: w_all[n].reshape(BIG[n])

    lands = _cast_to_slot([shard(n) for n in SECOND], me_idx)
    ag2, token = _exchange_start([], lands, "ag_second_start")
    first = _all_gather([shard(n) for n in FIRST], [f32 if n == "lb_param" else bf16 for n in FIRST], "ag_first")
    wk = _weights_in(dict(zip(FIRST, first)), FIRST, "weights_in_first")
    wz, wqb, wkvb, lb4 = (wk[n] for n in FIRST)

    h1, z = _fwd_in(x, g_mix, wz, tm)
    q, k, v = _fwd_mla_proj(z, cosb + token[0, 0], sina, sinb, g_qa, g_kva, wqb, wkvb, g_qn_p, g_kn_p, tm)
    a = _fwd_attn(q, k, v, tq_f)
    o = _fwd_gla(z, lb4)

    second = dict(zip(SECOND, _exchange_wait(ag2, o, "ag_second_wait")))
    wk = _weights_in(second, ("w_gate", "w_up", "w_ple_proj"), "weights_in_second")
    w_gate, w_up, w_pp = wk["w_gate"], wk["w_up"], wk["w_ple_proj"]
    w_o, w_down, w_pg = (second[n].reshape(N_DEV * BIG[n][0], BIG[n][1]) for n in ROW_SHARDED)

    x2, cat = _fwd_mix(a, o, z, g_hgo, x, w_o, tm)
    x3, gp, up = _fwd_ffn(x2, g_ffn, w_gate, w_up, w_down, tm)
    d3, h3, dpre, dpp, dg_ple, loss_tile = _ple_loss_fwd_bwd(x3, g_ple, w_pg, p, w_pp, target, tm)
    d2, h2, act, dgp, dup, dg_ffn = _bwd_ffn(d3, x2, gp, up, g_ffn, w_gate, w_up, w_down, tm_ffn)

    blocks = _grads_out({"w_gate": [(0, _mm_tn(h2, dgp, "dw_gate"))], "w_up": [(0, _mm_tn(h2, dup, "dw_up"))],
                         "w_ple_proj": [(0, _mm_tn(p, dpp, "dw_ple_proj"))]}, ("w_gate", "w_up", "w_ple_proj"), "grads_out_second")
    row_grads = {"w_o": _mm_tn(cat, d2, "dw_o"), "w_down": _mm_tn(act, d3, "dw_down"), "w_ple_gate": _mm_tn(h3, dpre, "dw_ple_gate")}
    blocks.update({n: g.reshape(N_DEV, *BIG[n]) for n, g in row_grads.items()})
    empty = lambda names: [lax.empty((N_PEERS, *BIG[n]), bf16) for n in names]
    rs2, token = _exchange_start([blocks[n] for n in SECOND], empty(SECOND), "rs_second_start")

    da, do, dz_hg, dg_hgo = _bwd_mix(d2, w_o, o, z, g_hgo + token[0, 0], tm)
    dz_hq, dz_hff, dz_hfb, dz_hi, dlb4 = _bwd_gla(z, lb4, do)
    dq, dk, dv = _bwd_attn(q, k, v, da, tq_b)
    dz_mla, cqn, ckvn, dq0, dkv0, dg_qa, dg_kva, dg_qn, dg_kn = _bwd_mla_proj(
        z, dq, dk, dv, cosb, sina, sinb, g_qa, g_kva, wqb, wkvb, g_qn_p, g_kn_p, tm)

    gz = [(Z_HQ, _mm_tn(h1, dz_hq, "dw_in_hq")), (Z_HFF, _mm_tn(h1, dz_hff, "dw_in_hff")),
          (Z_HFB, _mm_tn(h1, dz_hfb, "dw_in_hfb")), (Z_HI, _mm_tn(h1, dz_hi, "dw_in_hi")),
          (Z_HG, _mm_tn(h1, dz_hg, "dw_in_hg")), (Z_CQ, _mm_tn(h1, dz_mla, "dw_in_mla"))]
    blocks1 = _grads_out({"w_in": gz, "w_qb": [(0, _mm_tn(cqn, dq0, "dw_qb"))], "w_kvb": [(0, _mm_tn(ckvn, dkv0, "dw_kvb"))],
                          "lb_param": [(0, dlb4)]}, FIRST, "grads_out_first")
    rs1, token = _exchange_start([blocks1[n] for n in FIRST], empty(FIRST), "rs_first_start")

    result = {}

    def adam(names, lands, src, n_blocks):
        outs = _adam_shards(me_idx, [src[n] for n in names], lands, [w_all[n] for n in names], [m_all[n] for n in names],
                            [v_all[n] for n in names], n_blocks, "adamw_" + names[0])
        result.update(zip(names, outs))

    lands2 = dict(zip(SECOND, _exchange_wait(rs2, dz_mla, "rs_second_wait")))
    by8 = tuple(n for n in SECOND if n != "w_down")
    adam(by8, [lands2[n] for n in by8], blocks, 8)
    adam(("w_down",), [lands2["w_down"]], blocks, 2)

    segments = [(dz_hq, 512, 0, Z_HQ // 512), (dz_hff, 512, 0, Z_HFF // 512), (dz_hfb, 512, 0, Z_HFB // 512),
                (dz_hi, 512, 0, Z_HI // 512), (dz_hg, 512, 0, Z_HG // 512), (dz_mla, 640, 0, Z_CQ // 640)]
    grad_x, dg_mix = _bwd_in(segments, wz, x, g_mix + token[0, 0], d2, tm)
    dgains = (dg_mix, dg_qa, dg_kva, dg_qn, dg_kn, dg_hgo, dg_ffn, dg_ple)

    lands1 = _exchange_wait(rs1, grad_x, "rs_first_wait")
    adam(FIRST, lands1, blocks1, 8)

    vec = jnp.concatenate(list(dgains) + [loss_tile[0:1]], axis=1)
    parts = _all_gather([vec], [f32], "ag_gains")[0]
    outs, loss_row = _adam_gains(parts, [w_all[n] for n in SMALL], [m_all[n] for n in SMALL], [v_all[n] for n in SMALL])
    result.update(zip(SMALL, outs))

    order = ("g_mix", "w_in", "g_qa", "g_kva", "w_qb", "w_kvb", "g_qn", "g_kn", "lb_param", "g_hgo", "w_o", "g_ffn",
             "w_gate", "w_up", "w_down", "g_ple", "w_ple_gate", "w_ple_proj")
    return (loss_row[0, 0], grad_x[None], *[result[n][k] for k in range(4) for n in order])
```

```python
import functools
import math

import jax
import jax.numpy as jnp
from jax import lax
from jax.experimental import pallas as pl
from jax.experimental.pallas import tpu as pltpu

f32 = jnp.float32
bf16 = jnp.bfloat16

N_DEV = 8
D_MODEL = 1024
MLA_HEADS = 4
QK_NOPE = 128
QK_ROPE = 64
QK_HEAD = QK_NOPE + QK_ROPE
QK_PAD = 256
V_HEAD = 128
Q_LORA = 256
KV_LORA = 256
HG_HEADS = 4
HG_DK = 128
CHUNK = 64
D_FF = 2816
PLE_DIM = 256
ROPE_THETA = 10000.0
EPS = 1e-6
ATTN_SCALE = QK_HEAD ** -0.5
IN_SIZES = (256, 256, 64, 512, 512, 512, 512, 512)
D_IN = sum(IN_SIZES)
Z_HQ, Z_HFF, Z_HFB, Z_HI, Z_HG, Z_CQ, Z_CKV, Z_KR, Z_W = 0, 512, 1024, 1536, 2048, 2560, 2816, 3072, 3200

ADAM_LR, ADAM_B1, ADAM_B2, ADAM_EPS, ADAM_WD, ADAM_STEP = 0.001, 0.9, 0.999, 1e-08, 0.01, 10

LANES = 128
BIG = {"w_in": (1024, 392), "w_qb": (256, 96), "w_kvb": (256, 128), "w_o": (128, 1024), "w_gate": (1024, 352),
       "w_up": (1024, 352), "w_down": (352, 1024), "w_ple_gate": (128, 1024), "w_ple_proj": (256, 128),
       "lb_param": (4, 64)}
ROW_BLOCKS = {("w_in", "w_qb", "w_kvb", "w_o", "w_gate", "w_up", "w_ple_gate", "w_ple_proj"): 8, ("w_down", "lb_param"): 2}
SMALL = {"g_mix": (0, 1024), "g_qa": (1024, 256), "g_kva": (1280, 256), "g_qn": (1536, 192), "g_kn": (1792, 192),
         "g_hgo": (2048, 512), "g_ffn": (2560, 1024), "g_ple": (3584, 1024)}
LOSS_OFF = 4608
GAIN_VEC = LOSS_OFF + LANES
Z_SEGMENTS = ((0, 256, Z_CQ), (256, 512, Z_CKV), (512, 576, Z_KR), (576, 1088, Z_HQ), (1088, 1600, Z_HFF),
              (1600, 2112, Z_HFB), (2112, 2624, Z_HI), (2624, 3136, Z_HG))

VMEM_LIMIT = 56 * 1024 * 1024
MESH = pl.DeviceIdType.MESH


def _cp(sem=None, vmem=None):
    return pltpu.CompilerParams(dimension_semantics=sem, vmem_limit_bytes=vmem)


def _const_spec(shape):
    nd = len(shape)
    return pl.BlockSpec(shape, lambda *_: (0,) * nd, pipeline_mode=pl.Buffered(1))


def _acc_spec(shape):
    nd = len(shape)
    return pl.BlockSpec(shape, lambda *_: (0,) * nd)


def _sigmoid(x):
    return jax.nn.sigmoid(x)


def _dot(a, b):
    return jnp.dot(a, b, preferred_element_type=f32)


def _dot_nt(a, b):
    return lax.dot_general(a, b, (((1,), (1,)), ((), ())), preferred_element_type=f32)


def _dot_tn(a, b):
    return lax.dot_general(a, b, (((0,), (0,)), ((), ())), preferred_element_type=f32)


def _rms_fwd(x, g, width):
    r = lax.rsqrt(jnp.sum(x * x, axis=-1, keepdims=True) * (1.0 / width) + EPS)
    return x * r * g, r


def _rms_bwd(dy, x, r, g, width):
    u = dy * g
    dx = r * u - x * (r * r * r) * (jnp.sum(u * x, axis=-1, keepdims=True) * (1.0 / width))
    return dx, dy * x * r


def _rope(b, c, sa, sb):
    return b * c + pltpu.roll(b, 32, 1) * sa + pltpu.roll(b, 96, 1) * sb


def _all_gather(shards, dtypes, name):
    n = len(shards)

    def body(*refs):
        in_refs, out_refs, stage = refs[:n], refs[n:2 * n], refs[2 * n:3 * n]
        send_sems, recv_sems, local_sems = refs[3 * n:]
        for w in range(n):
            stage[w][...] = in_refs[w][...].astype(stage[w].dtype)
        x, y, c = lax.axis_index("x"), lax.axis_index("y"), lax.axis_index("c")
        me, sibling = (x, y, c), (x, y, 1 - c)
        chips = [(1 - x, y), (x, 1 - y), (1 - x, 1 - y)]

        def slot(w, px, py, pc):
            return out_refs[w].at[4 * px + 2 * py + pc]

        def copy(w, k, block, to, src=None):
            return pltpu.make_async_remote_copy(
                src_ref=slot(w, *block) if src is None else src, dst_ref=slot(w, *block),
                send_sem=send_sems.at[w, k], recv_sem=recv_sems.at[w, k], device_id=to, device_id_type=MESH)

        first = []
        for j, chip in enumerate(chips):
            first += [copy(w, 1 + j, me, (*chip, c), src=stage[w]) for w in range(n)]
        first += [copy(w, 0, me, sibling, src=stage[w]) for w in range(n)]
        mine = [pltpu.make_async_copy(stage[w], slot(w, *me), local_sems.at[w]) for w in range(n)]
        for cp in first + mine:
            cp.start()
        passed = []
        for j, chip in enumerate(chips):
            for w in range(n):
                copy(w, 1 + j, (*chip, c), me).wait_recv()
                passed.append(copy(w, 4 + j, (*chip, c), sibling))
                passed[-1].start()
        for w in range(n):
            copy(w, 0, sibling, me).wait_recv()
        for j, chip in enumerate(chips):
            for w in range(n):
                copy(w, 4 + j, (*chip, 1 - c), me).wait_recv()
        for cp in first + passed:
            cp.wait_send()
        for cp in mine:
            cp.wait()

    return pl.pallas_call(
        body, name=name,
        out_shape=[jax.ShapeDtypeStruct((N_DEV, *s.shape), dt) for s, dt in zip(shards, dtypes)],
        in_specs=[pl.BlockSpec(memory_space=pltpu.VMEM)] * n,
        out_specs=[pl.BlockSpec(memory_space=pl.ANY)] * n,
        scratch_shapes=[pltpu.VMEM(s.shape, dt) for s, dt in zip(shards, dtypes)]
        + [pltpu.SemaphoreType.DMA((n, 7)), pltpu.SemaphoreType.DMA((n, 7)), pltpu.SemaphoreType.DMA((n,))],
        compiler_params=_cp(None, VMEM_LIMIT),
    )(*shards)


N_PEERS = N_DEV - 1
HBM_SPEC = pl.BlockSpec(memory_space=pltpu.HBM)
SEM_SPEC = pl.BlockSpec(memory_space=pltpu.SEMAPHORE)
DATAFLOW = pltpu.SideEffectType.DATAFLOW_SIDE_EFFECTING


def _me():
    return 4 * lax.axis_index("x") + 2 * lax.axis_index("y") + lax.axis_index("c")


def _peer(k):
    x, y, c = lax.axis_index("x"), lax.axis_index("y"), lax.axis_index("c")
    px = 1 - x if k & 4 else x
    py = 1 - y if k & 2 else y
    pc = 1 - c if k & 1 else c
    return (px, py, pc), 4 * px + 2 * py + pc


def _exchange_copies(src_refs, land_refs, send_sems, recv_sems, gather):
    cps = []
    me = _me()
    for k in range(1, N_DEV):
        peer, peer_idx = _peer(k)
        for w, land in enumerate(land_refs):
            src = land.at[me] if gather else src_refs[w].at[peer_idx]
            dst = land.at[me] if gather else land.at[k - 1]
            cps.append(pltpu.make_async_remote_copy(
                src_ref=src, dst_ref=dst, send_sem=send_sems.at[N_PEERS * w + k - 1], recv_sem=recv_sems.at[N_PEERS * w + k - 1],
                device_id=peer, device_id_type=MESH))
    return cps


def _exchange_start(srcs, lands, name):
    n_src, n = len(srcs), len(lands)

    def body(*refs):
        src_refs, land_refs = refs[:n_src], refs[n_src:n_src + n]
        send_sems, recv_sems = refs[n_src + n], refs[n_src + n + 1]
        token = refs[-1]
        for cp in _exchange_copies(src_refs, land_refs, send_sems, recv_sems, gather=not n_src):
            cp.start()
        token[...] = jnp.zeros_like(token)

    arrays = [pltpu.with_memory_space_constraint(a, pltpu.HBM) for a in (*srcs, *lands)]
    outs = pl.pallas_call(
        body, name=name,
        out_shape=(pltpu.SemaphoreType.DMA((n * N_PEERS,)), pltpu.SemaphoreType.DMA((n * N_PEERS,)),
                   *[pltpu.HBM(a.shape, a.dtype) for a in arrays], jax.ShapeDtypeStruct((8, LANES), f32)),
        in_specs=[HBM_SPEC] * len(arrays),
        out_specs=(SEM_SPEC, SEM_SPEC, *[HBM_SPEC] * len(arrays), pl.BlockSpec(memory_space=pltpu.VMEM)),
        input_output_aliases={i: 2 + i for i in range(len(arrays))},
        compiler_params=pltpu.CompilerParams(has_side_effects=DATAFLOW),
    )(*arrays)
    return (outs[0], outs[1], outs[2:2 + n_src], outs[2 + n_src:2 + n_src + n]), outs[-1]


def _exchange_wait(state, after, name):
    send_sems, recv_sems, srcs, lands = state
    n_src, n = len(srcs), len(lands)

    def body(*refs):
        src_refs, land_refs = refs[:n_src], refs[n_src:n_src + n]
        send_ref, recv_ref = refs[n_src + n], refs[n_src + n + 1]
        for cp in _exchange_copies(src_refs, land_refs, send_ref, recv_ref, gather=not n_src):
            cp.wait_send()
            cp.wait_recv()

    arrays = (*srcs, *lands)
    outs = pl.pallas_call(
        body, name=name,
        out_shape=tuple(pltpu.HBM(a.shape, a.dtype) for a in arrays),
        in_specs=[HBM_SPEC] * len(arrays) + [SEM_SPEC, SEM_SPEC, pl.BlockSpec(memory_space=pl.ANY)],
        out_specs=tuple([HBM_SPEC] * len(arrays)),
        input_output_aliases={i: i for i in range(len(arrays))},
        compiler_params=pltpu.CompilerParams(has_side_effects=DATAFLOW),
    )(*arrays, send_sems, recv_sems, after)
    return outs[n_src:]


def _cast_to_slot(shards, me_idx, after):
    n = len(shards)

    def body(i_ref, *refs):
        for w in range(n):
            refs[n + 1 + w][...] = refs[w][...].astype(bf16)

    return pl.pallas_call(
        body, name="cast_to_slot",
        grid_spec=pltpu.PrefetchScalarGridSpec(
            num_scalar_prefetch=1, grid=(1,),
            in_specs=[pl.BlockSpec(s.shape, lambda i, m: (0, 0)) for s in shards] + [pl.BlockSpec(memory_space=pl.ANY)],
            out_specs=[pl.BlockSpec((None, *s.shape), lambda i, m: (m[0], 0, 0)) for s in shards]),
        out_shape=[jax.ShapeDtypeStruct((N_DEV, *s.shape), bf16) for s in shards],
        compiler_params=_cp(("arbitrary",), VMEM_LIMIT),
    )(me_idx, *shards, after)


def _row_block(rows, n_blocks):
    return (rows // n_blocks, True) if rows % (16 * n_blocks) == 0 else (rows, False)


def _adam_math(w, g, m, v):
    m = ADAM_B1 * m + (1.0 - ADAM_B1) * g
    v = ADAM_B2 * v + (1.0 - ADAM_B2) * (g * g)
    m_hat = m / (1.0 - ADAM_B1 ** ADAM_STEP)
    v_hat = v / (1.0 - ADAM_B2 ** ADAM_STEP)
    delta = -ADAM_LR * (m_hat / (jnp.sqrt(v_hat) + ADAM_EPS) + ADAM_WD * w)
    return delta, m, v


def _adam_shards(me_idx, blocks, lands, ws, ms, vs, n_blocks, name):
    n = len(blocks)

    def body(i_ref, *refs):
        ins, outs = refs[:5 * n], refs[5 * n:]
        for w in range(n):
            g_ref, b_ref, w_ref, m_ref, v_ref = (ins[t * n + w] for t in range(5))
            g = g_ref[...].astype(f32)
            for k in range(N_PEERS):
                g = g + b_ref[k].astype(f32)
            if len(w_ref.shape) == 2:
                pieces = [(slice(None), g)]
            else:
                pieces = [(a, g[2 * a:2 * a + 2]) for a in range(2)]
            for at, gp in pieces:
                vals = (gp,) + _adam_math(w_ref[at], gp, m_ref[at], v_ref[at])
                for t, val in enumerate(vals):
                    outs[4 * w + t][at] = val

    specs = [[] for _ in range(5)]
    out_specs, out_shape = [], []
    for g, wt in zip(blocks, ws):
        rows, cols = g.shape[1:]
        rb, cut = _row_block(rows, n_blocks)
        specs[0].append(pl.BlockSpec((None, rb, cols), functools.partial(lambda i, s, cut: (s[0], i if cut else 0, 0), cut=cut)))
        specs[1].append(pl.BlockSpec((N_PEERS, rb, cols), functools.partial(lambda i, s, cut: (0, i if cut else 0, 0), cut=cut)))
        if wt.shape[0] == 1:
            shard = pl.BlockSpec((None, rb, cols), functools.partial(lambda i, s, cut: (0, i if cut else 0, 0), cut=cut))
        else:
            shard = pl.BlockSpec(wt.shape, functools.partial(lambda i, s, nd: (0,) * nd, nd=wt.ndim))
        for t in (2, 3, 4):
            specs[t].append(shard)
        out_specs += [shard] * 4
        out_shape += [jax.ShapeDtypeStruct(wt.shape, f32)] * 4
    outs = pl.pallas_call(
        body, name=name,
        grid_spec=pltpu.PrefetchScalarGridSpec(num_scalar_prefetch=1, grid=(n_blocks,), in_specs=sum(specs, []), out_specs=out_specs),
        out_shape=out_shape,
        compiler_params=_cp(("arbitrary",), VMEM_LIMIT),
    )(me_idx, *blocks, *lands, *ws, *ms, *vs)
    return [outs[4 * w:4 * w + 4] for w in range(n)]


def _adam_gains(parts, ws, ms, vs):
    n = len(ws)

    def body(p_ref, *refs):
        ins, outs = refs[:3 * n], refs[3 * n:]
        g_all = p_ref[0]
        for k in range(1, N_DEV):
            g_all = g_all + p_ref[k]
        for w, (off, lanes) in enumerate(SMALL.values()):
            w_ref, m_ref, v_ref = ins[w], ins[n + w], ins[2 * n + w]
            if len(w_ref.shape) == 2:
                pieces = [(slice(None), off, lanes)]
            else:
                pieces = [((slice(None), h), off + LANES * h, LANES) for h in range(w_ref.shape[1])]
            for at, o, ln in pieces:
                g = g_all[:, o:o + ln]
                vals = (g,) + _adam_math(w_ref[at], g, m_ref[at], v_ref[at])
                for t, val in enumerate(vals):
                    outs[4 * w + t][at] = val
        outs[4 * n][...] = g_all[:, LOSS_OFF:LOSS_OFF + LANES]

    out_shape = sum([[jax.ShapeDtypeStruct(w.shape, f32)] * 4 for w in ws], []) + [jax.ShapeDtypeStruct((1, LANES), f32)]
    outs = pl.pallas_call(body, name="adamw_gains", out_shape=out_shape)(parts, *ws, *ms, *vs)
    return [outs[4 * w:4 * w + 4] for w in range(n)], outs[4 * n]


def _fwd_in(x, g_mix, wz, tm):
    s, d = x.shape

    def body(x_ref, g_ref, w_ref, h_ref, z_ref):
        h, _ = _rms_fwd(x_ref[...], g_ref[...], d)
        hb = h.astype(bf16)
        h_ref[...] = hb
        z_ref[...] = _dot(hb, w_ref[...])

    return pl.pallas_call(
        body, name="fwd_in", grid=(s // tm,),
        in_specs=[pl.BlockSpec((tm, d), lambda i: (i, 0)), _const_spec((1, d)), _const_spec((d, Z_W))],
        out_specs=[pl.BlockSpec((tm, d), lambda i: (i, 0)), pl.BlockSpec((tm, Z_W), lambda i: (i, 0))],
        out_shape=[jax.ShapeDtypeStruct((s, d), bf16), jax.ShapeDtypeStruct((s, Z_W), f32)],
        compiler_params=_cp(("parallel",), VMEM_LIMIT),
    )(x, g_mix, wz)


def _mla_qk_fwd(cq, ckv, kr, g_qa, g_kva, wqb, wkvb, g_qn, g_kn):
    cqn, rq = _rms_fwd(cq, g_qa, Q_LORA)
    ckvn, rkv = _rms_fwd(ckv, g_kva, KV_LORA)
    cqn_b, ckvn_b = cqn.astype(bf16), ckvn.astype(bf16)
    q0 = _dot(cqn_b, wqb)
    kv0 = _dot(ckvn_b, wkvb)
    return cqn_b, rq, ckvn_b, rkv, q0, kv0


def _fwd_mla_proj(z, cosb, sina, sinb, g_qa, g_kva, wqb, wkvb, g_qn, g_kn, tm):
    s = z.shape[0]
    hh = MLA_HEADS

    def body(cq_ref, ckv_ref, kr_ref, c_ref, sa_ref, sb_ref, gqa_ref, gkva_ref, wqb_ref, wkvb_ref, gqn_ref, gkn_ref,
             q_ref, k_ref, v_ref):
        _, _, _, _, q0, kv0 = _mla_qk_fwd(cq_ref[...], ckv_ref[...], kr_ref[...], gqa_ref[...], gkva_ref[...],
                                          wqb_ref[...], wkvb_ref[...], gqn_ref[...], gkn_ref[...])
        kr = kr_ref[...]
        c, sa, sb = c_ref[...], sa_ref[...], sb_ref[...]
        gqn, gkn = gqn_ref[...], gkn_ref[...]
        kr_sq = jnp.sum(kr * kr, axis=-1, keepdims=True)
        for h in range(hh):
            qh = q0[:, QK_PAD * h:QK_PAD * (h + 1)]
            qn, _ = _rms_fwd(qh, gqn, QK_HEAD)
            q_ref[h, :, 0:128] = qn[:, 0:128].astype(bf16)
            q_ref[h, :, 128:256] = _rope(qn[:, 128:256], c, sa, sb).astype(bf16)
            kn_ = kv0[:, 256 * h:256 * h + 128]
            rk = lax.rsqrt((jnp.sum(kn_ * kn_, axis=-1, keepdims=True) + kr_sq) * (1.0 / QK_HEAD) + EPS)
            k_ref[h, :, 0:128] = (kn_ * rk * gkn[:, 0:128]).astype(bf16)
            k_ref[h, :, 128:256] = _rope(kr * rk * gkn[:, 128:256], c, sa, sb).astype(bf16)
            v_ref[h] = kv0[:, 256 * h + 128:256 * h + 256].astype(bf16)

    row128 = pl.BlockSpec((tm, 128), lambda i: (i, 0))
    return pl.pallas_call(
        body, name="fwd_mla_proj", grid=(s // tm,),
        in_specs=[pl.BlockSpec((tm, 256), lambda i: (i, Z_CQ // 256)), pl.BlockSpec((tm, 256), lambda i: (i, Z_CKV // 256)),
                  pl.BlockSpec((tm, 128), lambda i: (i, Z_KR // 128)), row128, row128, row128,
                  _const_spec((1, 256)), _const_spec((1, 256)), _const_spec((256, 1024)), _const_spec((256, 1024)),
                  _const_spec((1, 256)), _const_spec((1, 256))],
        out_specs=[pl.BlockSpec((hh, tm, QK_PAD), lambda i: (0, i, 0)), pl.BlockSpec((hh, tm, QK_PAD), lambda i: (0, i, 0)),
                   pl.BlockSpec((hh, tm, V_HEAD), lambda i: (0, i, 0))],
        out_shape=[jax.ShapeDtypeStruct((hh, s, QK_PAD), bf16), jax.ShapeDtypeStruct((hh, s, QK_PAD), bf16),
                   jax.ShapeDtypeStruct((hh, s, V_HEAD), bf16)],
        compiler_params=_cp(("parallel",), VMEM_LIMIT),
    )(z, z, z, cosb, sina, sinb, g_qa, g_kva, wqb, wkvb, g_qn, g_kn)


def _fwd_attn(q, k, v, tq):
    hh, s, _ = q.shape

    def body(q_ref, k_ref, v_ref, o_ref):
        sc = _dot_nt(q_ref[...], k_ref[...]) * ATTN_SCALE
        p = jnp.exp(sc - jnp.max(sc, axis=-1, keepdims=True))
        l = jnp.sum(p, axis=-1, keepdims=True)
        o_ref[...] = (_dot(p.astype(bf16), v_ref[...]) * (1.0 / l)).astype(bf16)

    return pl.pallas_call(
        body, name="fwd_attn", grid=(hh, s // tq),
        in_specs=[pl.BlockSpec((None, tq, QK_PAD), lambda h, i: (h, i, 0)),
                  pl.BlockSpec((None, s, QK_PAD), lambda h, i: (h, 0, 0)),
                  pl.BlockSpec((None, s, V_HEAD), lambda h, i: (h, 0, 0))],
        out_specs=pl.BlockSpec((tq, V_HEAD), lambda h, i: (i, h)),
        out_shape=jax.ShapeDtypeStruct((s, hh * V_HEAD), bf16),
        compiler_params=_cp(("parallel", "parallel"), VMEM_LIMIT),
    )(q, k, v)


def _split3(x):
    hi = x.astype(bf16)
    r1 = x - hi.astype(f32)
    mid = r1.astype(bf16)
    lo = (r1 - mid.astype(f32)).astype(bf16)
    return jnp.concatenate([hi, mid, lo], axis=-1)


def _tri_sum(tri, x):
    y = _dot(tri, _split3(x))
    return y[:, 0:128] + y[:, 128:256] + y[:, 256:384]


def _gla_masks(rev):
    row = lax.broadcasted_iota(jnp.int32, (CHUNK, CHUNK), 0)
    col = lax.broadcasted_iota(jnp.int32, (CHUNK, CHUNK), 1)
    keep, keep_t = (row <= col, row >= col) if rev else (row >= col, row <= col)
    return keep.astype(f32), keep.astype(bf16), keep_t.astype(bf16)


def _gla_gates(hq, hf, lower):
    sg = _sigmoid(hf)
    f = lower + (1.0 - lower) * sg
    return hq * _sigmoid(hq), 1.0 - f, jnp.log(f), f, sg


def _gla_last_mid(b, rev):
    if rev:
        return b[0:1, :], b[CHUNK // 2:CHUNK // 2 + 1, :]
    return b[CHUNK - 1:CHUNK, :], b[CHUNK // 2 - 1:CHUNK // 2, :]


def _gla_rows(n, n_chunks, rev):
    ne = n_chunks - 1 - n if rev else n
    return pl.ds(pl.multiple_of(ne * CHUNK, CHUNK), CHUNK)


GLA_HEADS_PER_STEP = 2


def _fwd_gla(z, lb4):
    s = z.shape[0]
    n_chunks = s // CHUNK
    hp = GLA_HEADS_PER_STEP
    chains = [(hh, rev) for hh in range(hp) for rev in (False, True)]

    def body(hq_ref, hff_ref, hfb_ref, hi_ref, lb_ref, o_ref, st_ref):
        st_ref[...] = jnp.zeros_like(st_ref)
        masks = {rev: _gla_masks(rev) for rev in (False, True)}
        lowers = [_sigmoid(lb_ref[int(rev):int(rev) + 1, 128 * hh:128 * (hh + 1)]
                           - lb_ref[2 + int(rev):3 + int(rev), 128 * hh:128 * (hh + 1)]) for hh, rev in chains]

        def make_step(first):
            def step(n, carry):
                for ci, (hh, rev) in enumerate(chains):
                    cols = slice(128 * hh, 128 * (hh + 1))
                    rows = _gla_rows(n, n_chunks, rev)
                    maskf, tri, _ = masks[rev]
                    hf_ref = hfb_ref if rev else hff_ref
                    q, k, logf, _, _ = _gla_gates(hq_ref[rows, cols], hf_ref[rows, cols], lowers[ci])
                    vb = hi_ref[rows, cols].astype(bf16)
                    b = _tri_sum(tri, logf)
                    b_last, b_mid = _gla_last_mid(b, rev)
                    qi = (q * jnp.exp(b - b_mid)).astype(bf16)
                    ki = (k * jnp.exp(b_mid - b)).astype(bf16)
                    a = (_dot_nt(qi, ki) * maskf).astype(bf16)
                    st = st_ref[ci]
                    o = _dot(a, vb) + _dot_nt((q * jnp.exp(b)).astype(bf16), st.astype(bf16))
                    kt = (k * jnp.exp(b_last - b)).astype(bf16)
                    st_ref[ci] = st * jnp.exp(b_last) + _dot_tn(vb, kt)
                    if first:
                        o_ref[rows, cols] = o
                    else:
                        o_ref[rows, cols] += o
                return carry
            return step

        lax.fori_loop(0, n_chunks // 2, make_step(True), 0)
        lax.fori_loop(n_chunks // 2, n_chunks, make_step(False), 0)

    w = 128 * hp
    col = lambda base: pl.BlockSpec((s, w), lambda h: (0, base // w + h))
    return pl.pallas_call(
        body, name="fwd_gla", grid=(HG_HEADS // hp,),
        in_specs=[col(Z_HQ), col(Z_HFF), col(Z_HFB), col(Z_HI), pl.BlockSpec((4, w), lambda h: (0, h))],
        out_specs=pl.BlockSpec((s, w), lambda h: (0, h)),
        out_shape=jax.ShapeDtypeStruct((s, HG_HEADS * 128), f32),
        scratch_shapes=[pltpu.VMEM((len(chains), 128, 128), f32)],
        compiler_params=_cp(("parallel",), VMEM_LIMIT),
    )(z, z, z, z, lb4)


def _hg_out(o, hg, g_hgo):
    outs, ons, rs = [], [], []
    for h in range(HG_HEADS):
        oh = o[:, 128 * h:128 * (h + 1)]
        on, r = _rms_fwd(oh, g_hgo[:, 128 * h:128 * (h + 1)], 128)
        ons.append(on)
        rs.append(r)
    on = jnp.concatenate(ons, axis=-1)
    sg = _sigmoid(hg)
    return on * (hg * sg), on, rs, sg


def _fwd_mix(a, o, z, g_hgo, x, w_o, tm):
    s, d = x.shape

    def body(a_ref, o_ref, hg_ref, g_ref, x_ref, w_ref, x2_ref, cat_ref):
        r, _, _, _ = _hg_out(o_ref[...], hg_ref[...], g_ref[...])
        cat = jnp.concatenate([a_ref[...], r.astype(bf16)], axis=-1)
        cat_ref[...] = cat
        x2_ref[...] = x_ref[...] + _dot(cat, w_ref[...])

    row512 = pl.BlockSpec((tm, 512), lambda i: (i, 0))
    rowd = pl.BlockSpec((tm, d), lambda i: (i, 0))
    return pl.pallas_call(
        body, name="fwd_mix", grid=(s // tm,),
        in_specs=[row512, row512, pl.BlockSpec((tm, 512), lambda i: (i, Z_HG // 512)), _const_spec((1, 512)), rowd,
                  _const_spec((d, d))],
        out_specs=[rowd, rowd],
        out_shape=[jax.ShapeDtypeStruct((s, d), f32), jax.ShapeDtypeStruct((s, d), bf16)],
        compiler_params=_cp(("parallel",), VMEM_LIMIT),
    )(a, o, z, g_hgo, x, w_o)


def _fwd_ffn(x2, g_ffn, w_gate, w_up, w_down, tm):
    s, d = x2.shape

    def body(x_ref, g_ref, wg_ref, wu_ref, wd_ref, x3_ref, gp_ref, up_ref):
        x = x_ref[...]
        h, _ = _rms_fwd(x, g_ref[...], d)
        hb = h.astype(bf16)
        gp = _dot(hb, wg_ref[...])
        up = _dot(hb, wu_ref[...])
        gp_ref[...] = gp
        up_ref[...] = up
        act = (gp * _sigmoid(gp) * up).astype(bf16)
        x3_ref[...] = x + _dot(act, wd_ref[...])

    rowd = pl.BlockSpec((tm, d), lambda i: (i, 0))
    rowf = pl.BlockSpec((tm, D_FF), lambda i: (i, 0))
    return pl.pallas_call(
        body, name="fwd_ffn", grid=(s // tm,),
        in_specs=[rowd, _const_spec((1, d)), _const_spec((d, D_FF)), _const_spec((d, D_FF)), _const_spec((D_FF, d))],
        out_specs=[rowd, rowf, rowf],
        out_shape=[jax.ShapeDtypeStruct((s, d), f32), jax.ShapeDtypeStruct((s, D_FF), f32),
                   jax.ShapeDtypeStruct((s, D_FF), f32)],
        compiler_params=_cp(("parallel",), VMEM_LIMIT),
    )(x2, g_ffn, w_gate, w_up, w_down)


def _ple_loss_fwd_bwd(x3, g_ple, w_pg, p, w_pp, target, tm):
    s, d = x3.shape

    def body(x_ref, g_ref, wg_ref, p_ref, wp_ref, t_ref, dx_ref, h_ref, dpre_ref, dpp_ref, dg_ref, loss_ref):
        @pl.when(pl.program_id(0) == 0)
        def _():
            dg_ref[...] = jnp.zeros_like(dg_ref)
            loss_ref[...] = jnp.zeros_like(loss_ref)

        x = x_ref[...]
        g = g_ref[...]
        h, r = _rms_fwd(x, g, d)
        hb = h.astype(bf16)
        gate = _sigmoid(_dot(hb, wg_ref[...]))
        pp = _dot(p_ref[...].astype(bf16), wp_ref[...])
        e = x + gate * pp - t_ref[...]
        loss_ref[...] += 0.5 * jnp.sum(e * e) * (1.0 / d)
        dy = e * (1.0 / d)
        dpre = (dy * pp * gate * (1.0 - gate)).astype(bf16)
        dx, dgx = _rms_bwd(_dot_nt(dpre, wg_ref[...]), x, r, g, d)
        dx_ref[...] = dy + dx
        dg_ref[...] += jnp.sum(dgx, axis=0, keepdims=True)
        h_ref[...] = hb
        dpre_ref[...] = dpre
        dpp_ref[...] = (dy * gate).astype(bf16)

    rowd = pl.BlockSpec((tm, d), lambda i: (i, 0))
    return pl.pallas_call(
        body, name="ple_loss_fwd_bwd", grid=(s // tm,),
        in_specs=[rowd, _const_spec((1, d)), _const_spec((d, d)), pl.BlockSpec((tm, PLE_DIM), lambda i: (i, 0)),
                  _const_spec((PLE_DIM, d)), rowd],
        out_specs=[rowd, rowd, rowd, rowd, _acc_spec((1, d)), _acc_spec((8, 128))],
        out_shape=[jax.ShapeDtypeStruct((s, d), f32), jax.ShapeDtypeStruct((s, d), bf16), jax.ShapeDtypeStruct((s, d), bf16),
                   jax.ShapeDtypeStruct((s, d), bf16), jax.ShapeDtypeStruct((1, d), f32), jax.ShapeDtypeStruct((8, 128), f32)],
        compiler_params=_cp(("arbitrary",), VMEM_LIMIT),
    )(x3, g_ple, w_pg, p, w_pp, target)


def _bwd_ffn(d3, x2, gp, up, g_ffn, w_gate, w_up, w_down, tm):
    s, d = x2.shape

    def body(d3_ref, x_ref, gp_ref, up_ref, g_ref, wg_ref, wu_ref, wd_ref, d2_ref, h_ref, act_ref, dgp_ref, dup_ref, dg_ref):
        @pl.when(pl.program_id(0) == 0)
        def _():
            dg_ref[...] = jnp.zeros_like(dg_ref)

        x = x_ref[...]
        g = g_ref[...]
        d3 = d3_ref[...]
        h, r = _rms_fwd(x, g, d)
        h_ref[...] = h.astype(bf16)
        gp, up = gp_ref[...], up_ref[...]
        sg = _sigmoid(gp)
        silu = gp * sg
        act_ref[...] = (silu * up).astype(bf16)
        dact = _dot_nt(d3.astype(bf16), wd_ref[...])
        dgp = (dact * up * (sg * (1.0 + gp * (1.0 - sg)))).astype(bf16)
        dup = (dact * silu).astype(bf16)
        dgp_ref[...] = dgp
        dup_ref[...] = dup
        dh = _dot_nt(dgp, wg_ref[...]) + _dot_nt(dup, wu_ref[...])
        dx, dgx = _rms_bwd(dh, x, r, g, d)
        d2_ref[...] = d3 + dx
        dg_ref[...] += jnp.sum(dgx, axis=0, keepdims=True)

    rowd = pl.BlockSpec((tm, d), lambda i: (i, 0))
    rowf = pl.BlockSpec((tm, D_FF), lambda i: (i, 0))
    return pl.pallas_call(
        body, name="bwd_ffn", grid=(s // tm,),
        in_specs=[rowd, rowd, rowf, rowf, _const_spec((1, d)), _const_spec((d, D_FF)), _const_spec((d, D_FF)),
                  _const_spec((D_FF, d))],
        out_specs=[rowd, rowd, rowf, rowf, rowf, _acc_spec((1, d))],
        out_shape=[jax.ShapeDtypeStruct((s, d), f32), jax.ShapeDtypeStruct((s, d), bf16)] + [jax.ShapeDtypeStruct((s, D_FF), bf16)] * 3
        + [jax.ShapeDtypeStruct((1, d), f32)],
        compiler_params=_cp(("arbitrary",), VMEM_LIMIT),
    )(d3, x2, gp, up, g_ffn, w_gate, w_up, w_down)


def _bwd_mix(d2, w_o, o, z, g_hgo, tm):
    s, d = d2.shape

    def body(d2_ref, w_ref, o_ref, hg_ref, g_ref, da_ref, do_ref, dhg_ref, dg_ref):
        @pl.when(pl.program_id(0) == 0)
        def _():
            dg_ref[...] = jnp.zeros_like(dg_ref)

        dcat = _dot_nt(d2_ref[...].astype(bf16), w_ref[...])
        da_ref[...] = dcat[:, 0:512].astype(bf16)
        dr = dcat[:, 512:1024]
        o, hg, g = o_ref[...], hg_ref[...], g_ref[...]
        _, on, rs, sg = _hg_out(o, hg, g)
        dhg_ref[...] = (dr * on * (sg * (1.0 + hg * (1.0 - sg)))).astype(bf16)
        don = dr * (hg * sg)
        dgs = []
        for h in range(HG_HEADS):
            cols = slice(128 * h, 128 * (h + 1))
            dx, dgx = _rms_bwd(don[:, cols], o[:, cols], rs[h], g[:, cols], 128)
            do_ref[:, cols] = dx
            dgs.append(jnp.sum(dgx, axis=0, keepdims=True))
        dg_ref[...] += jnp.concatenate(dgs, axis=-1)

    row512 = pl.BlockSpec((tm, 512), lambda i: (i, 0))
    return pl.pallas_call(
        body, name="bwd_mix", grid=(s // tm,),
        in_specs=[pl.BlockSpec((tm, d), lambda i: (i, 0)), _const_spec((d, d)), row512,
                  pl.BlockSpec((tm, 512), lambda i: (i, Z_HG // 512)), _const_spec((1, 512))],
        out_specs=[row512, row512, row512, _acc_spec((1, 512))],
        out_shape=[jax.ShapeDtypeStruct((s, 512), bf16), jax.ShapeDtypeStruct((s, 512), f32), jax.ShapeDtypeStruct((s, 512), bf16),
                   jax.ShapeDtypeStruct((1, 512), f32)],
        compiler_params=_cp(("arbitrary",), VMEM_LIMIT),
    )(d2, w_o, o, z, g_hgo)


def _bwd_gla(z, lb4, do):
    s = z.shape[0]
    n_chunks = s // CHUNK

    def body(hq_ref, hff_ref, hfb_ref, hi_ref, lb_ref, do_ref, dhq_ref, dhff_ref, dhfb_ref, dhi_ref, dlb_ref,
             st_all, b_all, dst_ref, dq_acc, dv_acc, dlow_ref):
        dirs = (False, True)
        masks = [_gla_masks(rev) for rev in dirs]
        lowers = [_sigmoid(lb_ref[int(rev):int(rev) + 1, :] - lb_ref[2 + int(rev):3 + int(rev), :]) for rev in dirs]
        hf_refs, dhf_refs = (hff_ref, hfb_ref), (dhff_ref, dhfb_ref)

        def fwd_step(n, sts):
            new = []
            for d, rev in enumerate(dirs):
                rows = _gla_rows(n, n_chunks, rev)
                _, k, logf, _, _ = _gla_gates(hq_ref[rows, :], hf_refs[d][rows, :], lowers[d])
                b = _tri_sum(masks[d][1], logf)
                b_last, _ = _gla_last_mid(b, rev)
                b_all[d, rows, :] = b
                st_all[d, n] = sts[d]
                kt = (k * jnp.exp(b_last - b)).astype(bf16)
                new.append(sts[d] * jnp.exp(b_last) + _dot_tn(hi_ref[rows, :].astype(bf16), kt))
            return tuple(new)

        zero = jnp.zeros((128, 128), f32)
        lax.fori_loop(0, n_chunks, fwd_step, (zero, zero))

        dst_ref[...] = jnp.zeros_like(dst_ref)
        dlow_ref[...] = jnp.zeros_like(dlow_ref)

        def make_bwd_step(first):
            def bwd_step(j, carry):
                n = n_chunks - 1 - j
                for d, rev in enumerate(dirs):
                    maskf, _, tri_t = masks[d]
                    lower = lowers[d]
                    rows = _gla_rows(n, n_chunks, rev)
                    hq, hf = hq_ref[rows, :], hf_refs[d][rows, :]
                    q, k, _, f, sg = _gla_gates(hq, hf, lower)
                    v = hi_ref[rows, :]
                    dout = do_ref[rows, :]
                    b = b_all[d, rows, :]
                    b_last, b_mid = _gla_last_mid(b, rev)
                    e1, e2, e3, e4 = jnp.exp(b - b_mid), jnp.exp(b_mid - b), jnp.exp(b_last - b), jnp.exp(b)
                    decay = jnp.exp(b_last)
                    qi, ki, kt, qt = q * e1, k * e2, k * e3, q * e4
                    qib, kib, ktb, qtb = qi.astype(bf16), ki.astype(bf16), kt.astype(bf16), qt.astype(bf16)
                    vb, dob = v.astype(bf16), dout.astype(bf16)
                    st = st_all[d, n]
                    dst = dst_ref[d]
                    stb, dstb = st.astype(bf16), dst.astype(bf16)
                    a = (_dot_nt(qib, kib) * maskf).astype(bf16)
                    dv = _dot_tn(a, dob) + _dot_nt(ktb, dstb)
                    da = (_dot_nt(dob, vb) * maskf).astype(bf16)
                    dqi = _dot(da, kib)
                    dki = _dot_tn(da, qib)
                    dqt = _dot(dob, stb)
                    dkt = _dot(vb, dstb)
                    ddecay = jnp.sum(dst * st, axis=0, keepdims=True)
                    dq = dqi * e1 + dqt * e4
                    dk = dki * e2 + dkt * e3
                    db = dqi * qi - dki * ki + dqt * qt - dkt * kt
                    dlast = jnp.sum(dkt * kt, axis=0, keepdims=True) + ddecay * decay
                    dlogf = _tri_sum(tri_t, db) + dlast
                    dst_ref[d] = dst * decay + _dot_tn(dob, qtb)
                    df = dlogf / f - dk
                    dhf_refs[d][rows, :] = (df * (1.0 - lower) * sg * (1.0 - sg)).astype(bf16)
                    dlow_ref[d:d + 1, :] += jnp.sum(df * (1.0 - sg), axis=0, keepdims=True)
                    sq = _sigmoid(hq)
                    dhq = dq * (sq * (1.0 + hq * (1.0 - sq)))
                    if first:
                        dq_acc[rows, :] = dhq
                        dv_acc[rows, :] = dv
                    else:
                        dhq_ref[rows, :] = (dq_acc[rows, :] + dhq).astype(bf16)
                        dhi_ref[rows, :] = (dv_acc[rows, :] + dv).astype(bf16)
                return carry
            return bwd_step

        lax.fori_loop(0, n_chunks // 2, make_bwd_step(True), 0)
        lax.fori_loop(n_chunks // 2, n_chunks, make_bwd_step(False), 0)

        for d in range(2):
            dl = dlow_ref[d:d + 1, :] * lowers[d] * (1.0 - lowers[d])
            dlb_ref[d:d + 1, :] = dl
            dlb_ref[2 + d:3 + d, :] = -dl

    col = lambda base: pl.BlockSpec((s, 128), lambda h: (0, base // 128 + h))
    return pl.pallas_call(
        body, name="bwd_gla", grid=(HG_HEADS,),
        in_specs=[col(Z_HQ), col(Z_HFF), col(Z_HFB), col(Z_HI), pl.BlockSpec((4, 128), lambda h: (0, h)), col(0)],
        out_specs=[col(0), col(0), col(0), col(0), pl.BlockSpec((4, 128), lambda h: (0, h))],
        out_shape=[jax.ShapeDtypeStruct((s, 512), bf16)] * 4 + [jax.ShapeDtypeStruct((4, 512), f32)],
        scratch_shapes=[pltpu.VMEM((2, n_chunks, 128, 128), f32), pltpu.VMEM((2, s, 128), f32), pltpu.VMEM((2, 128, 128), f32),
                        pltpu.VMEM((s, 128), f32), pltpu.VMEM((s, 128), f32), pltpu.VMEM((2, 128), f32)],
        compiler_params=_cp(("parallel",), VMEM_LIMIT),
    )(z, z, z, z, lb4, do)


def _bwd_attn(q, k, v, da, tq):
    hh, s, _ = q.shape

    def body(q_ref, k_ref, v_ref, do_ref, dq_ref, dk_ref, dv_ref):
        @pl.when(pl.program_id(1) == 0)
        def _():
            dk_ref[...] = jnp.zeros_like(dk_ref)
            dv_ref[...] = jnp.zeros_like(dv_ref)

        qb, kb, vb, dob = q_ref[...], k_ref[...], v_ref[...], do_ref[...]
        sc = _dot_nt(qb, kb) * ATTN_SCALE
        p = jnp.exp(sc - jnp.max(sc, axis=-1, keepdims=True))
        w = p * (1.0 / jnp.sum(p, axis=-1, keepdims=True))
        dw = _dot_nt(dob, vb)
        ds = (w * (dw - jnp.sum(dw * w, axis=-1, keepdims=True)) * ATTN_SCALE).astype(bf16)
        dq_ref[...] = _dot(ds, kb)
        dk_ref[...] += _dot_tn(ds, qb)
        dv_ref[...] += _dot_tn(w.astype(bf16), dob)

    return pl.pallas_call(
        body, name="bwd_attn", grid=(hh, s // tq),
        in_specs=[pl.BlockSpec((None, tq, QK_PAD), lambda h, i: (h, i, 0)),
                  pl.BlockSpec((None, s, QK_PAD), lambda h, i: (h, 0, 0)),
                  pl.BlockSpec((None, s, V_HEAD), lambda h, i: (h, 0, 0)),
                  pl.BlockSpec((tq, V_HEAD), lambda h, i: (i, h))],
        out_specs=[pl.BlockSpec((None, tq, QK_PAD), lambda h, i: (h, i, 0)),
                   pl.BlockSpec((None, s, QK_PAD), lambda h, i: (h, 0, 0)),
                   pl.BlockSpec((None, s, V_HEAD), lambda h, i: (h, 0, 0))],
        out_shape=[jax.ShapeDtypeStruct((hh, s, QK_PAD), f32), jax.ShapeDtypeStruct((hh, s, QK_PAD), f32),
                   jax.ShapeDtypeStruct((hh, s, V_HEAD), f32)],
        compiler_params=_cp(("parallel", "arbitrary"), VMEM_LIMIT),
    )(q, k, v, da)


def _bwd_mla_proj(z, dq, dk, dv, cosb, sina, sinb, g_qa, g_kva, wqb, wkvb, g_qn, g_kn, tm):
    s = z.shape[0]
    hh = MLA_HEADS

    def body(cq_ref, ckv_ref, kr_ref, dq_ref, dk_ref, dv_ref, c_ref, sa_ref, sb_ref, gqa_ref, gkva_ref, wqb_ref, wkvb_ref,
             gqn_ref, gkn_ref, dz_ref, cqn_ref, ckvn_ref, dq0_ref, dkv0_ref, dgqa_ref, dgkva_ref, dgqn_ref, dgkn_ref):
        @pl.when(pl.program_id(0) == 0)
        def _():
            for r in (dgqa_ref, dgkva_ref, dgqn_ref, dgkn_ref):
                r[...] = jnp.zeros_like(r)

        cq, ckv, kr = cq_ref[...], ckv_ref[...], kr_ref[...]
        gqa, gkva, gqn, gkn = gqa_ref[...], gkva_ref[...], gqn_ref[...], gkn_ref[...]
        cqn_b, rq, ckvn_b, rkv, q0, kv0 = _mla_qk_fwd(cq, ckv, kr, gqa, gkva, wqb_ref[...], wkvb_ref[...], gqn, gkn)
        cqn_ref[...] = cqn_b
        ckvn_ref[...] = ckvn_b
        c, sa, sb = c_ref[...], -sa_ref[...], -sb_ref[...]
        kr_sq = jnp.sum(kr * kr, axis=-1, keepdims=True)
        dkr = jnp.zeros_like(kr)
        dgqn = jnp.zeros((1, QK_PAD), f32)
        dgkn = jnp.zeros((1, QK_PAD), f32)
        for h in range(hh):
            qh = q0[:, QK_PAD * h:QK_PAD * (h + 1)]
            rh = lax.rsqrt(jnp.sum(qh * qh, axis=-1, keepdims=True) * (1.0 / QK_HEAD) + EPS)
            dqh = dq_ref[h]
            dqn = jnp.concatenate([dqh[:, 0:128], _rope(dqh[:, 128:256], c, sa, sb)], axis=-1)
            dq0h, dgx = _rms_bwd(dqn, qh, rh, gqn, QK_HEAD)
            dq0_ref[:, QK_PAD * h:QK_PAD * (h + 1)] = dq0h.astype(bf16)
            dgqn = dgqn + jnp.sum(dgx, axis=0, keepdims=True)

            kn_ = kv0[:, 256 * h:256 * h + 128]
            k0 = jnp.concatenate([kn_, kr], axis=-1)
            rk = lax.rsqrt((jnp.sum(kn_ * kn_, axis=-1, keepdims=True) + kr_sq) * (1.0 / QK_HEAD) + EPS)
            dkh = dk_ref[h]
            dkn = jnp.concatenate([dkh[:, 0:128], _rope(dkh[:, 128:256], c, sa, sb)], axis=-1)
            dk0, dgx = _rms_bwd(dkn, k0, rk, gkn, QK_HEAD)
            dgkn = dgkn + jnp.sum(dgx, axis=0, keepdims=True)
            dkv0_ref[:, 256 * h:256 * h + 128] = dk0[:, 0:128].astype(bf16)
            dkv0_ref[:, 256 * h + 128:256 * h + 256] = dv_ref[h].astype(bf16)
            dkr = dkr + dk0[:, 128:256]
        dgqn_ref[...] += dgqn
        dgkn_ref[...] += dgkn
        dcq, dgx = _rms_bwd(_dot_nt(dq0_ref[...], wqb_ref[...]), cq, rq, gqa, Q_LORA)
        dgqa_ref[...] += jnp.sum(dgx, axis=0, keepdims=True)
        dckv, dgx = _rms_bwd(_dot_nt(dkv0_ref[...], wkvb_ref[...]), ckv, rkv, gkva, KV_LORA)
        dgkva_ref[...] += jnp.sum(dgx, axis=0, keepdims=True)
        dz_ref[:, 0:256] = dcq.astype(bf16)
        dz_ref[:, 256:512] = dckv.astype(bf16)
        dz_ref[:, 512:640] = dkr.astype(bf16)

    row128 = pl.BlockSpec((tm, 128), lambda i: (i, 0))
    row256 = pl.BlockSpec((tm, 256), lambda i: (i, 0))
    row1024 = pl.BlockSpec((tm, 1024), lambda i: (i, 0))
    hd = lambda w: pl.BlockSpec((hh, tm, w), lambda i: (0, i, 0))
    return pl.pallas_call(
        body, name="bwd_mla_proj", grid=(s // tm,),
        in_specs=[pl.BlockSpec((tm, 256), lambda i: (i, Z_CQ // 256)), pl.BlockSpec((tm, 256), lambda i: (i, Z_CKV // 256)),
                  pl.BlockSpec((tm, 128), lambda i: (i, Z_KR // 128)), hd(QK_PAD), hd(QK_PAD), hd(V_HEAD),
                  row128, row128, row128,
                  _const_spec((1, 256)), _const_spec((1, 256)), _const_spec((256, 1024)), _const_spec((256, 1024)),
                  _const_spec((1, 256)), _const_spec((1, 256))],
        out_specs=[pl.BlockSpec((tm, 640), lambda i: (i, 0)), row256, row256, row1024, row1024,
                   _acc_spec((1, 256)), _acc_spec((1, 256)), _acc_spec((1, 256)), _acc_spec((1, 256))],
        out_shape=[jax.ShapeDtypeStruct((s, 640), bf16), jax.ShapeDtypeStruct((s, 256), bf16), jax.ShapeDtypeStruct((s, 256), bf16),
                   jax.ShapeDtypeStruct((s, 1024), bf16), jax.ShapeDtypeStruct((s, 1024), bf16)]
        + [jax.ShapeDtypeStruct((1, 256), f32)] * 4,
        compiler_params=_cp(("arbitrary",), VMEM_LIMIT),
    )(z, z, z, dq, dk, dv, cosb, sina, sinb, g_qa, g_kva, wqb, wkvb, g_qn, g_kn)


def _bwd_in(segments, wz, x, g_mix, d2, tm):
    s, d = x.shape
    n_seg = len(segments)

    def body(*refs):
        dz_refs, w_refs = refs[:n_seg], refs[n_seg:2 * n_seg]
        x_ref, g_ref, d2_ref, gx_ref, dg_ref = refs[2 * n_seg:]

        @pl.when(pl.program_id(0) == 0)
        def _():
            dg_ref[...] = jnp.zeros_like(dg_ref)

        dh = _dot_nt(dz_refs[0][...], w_refs[0][...])
        for a_ref, w_ref in zip(dz_refs[1:], w_refs[1:]):
            dh = dh + _dot_nt(a_ref[...], w_ref[...])
        x, g = x_ref[...], g_ref[...]
        r = lax.rsqrt(jnp.sum(x * x, axis=-1, keepdims=True) * (1.0 / d) + EPS)
        dx, dgx = _rms_bwd(dh, x, r, g, d)
        gx_ref[...] = d2_ref[...] + dx
        dg_ref[...] += jnp.sum(dgx, axis=0, keepdims=True)

    rowd = pl.BlockSpec((tm, d), lambda i: (i, 0))
    dz_specs = [pl.BlockSpec((tm, w), functools.partial(lambda i, j: (i, j), j=ja)) for _, w, ja, _ in segments]
    w_specs = [pl.BlockSpec((d, w), functools.partial(lambda i, j: (0, j), j=jw), pipeline_mode=pl.Buffered(1))
               for _, w, _, jw in segments]
    return pl.pallas_call(
        body, name="bwd_in", grid=(s // tm,),
        in_specs=dz_specs + w_specs + [rowd, _const_spec((1, d)), rowd],
        out_specs=[rowd, _acc_spec((1, d))],
        out_shape=[jax.ShapeDtypeStruct((s, d), f32), jax.ShapeDtypeStruct((1, d), f32)],
        compiler_params=_cp(("arbitrary",), VMEM_LIMIT),
    )(*[a for a, _, _, _ in segments], *([wz] * n_seg), x, g_mix, d2)


def _pick_tile(n, cap):
    best = None
    for t in range(LANES, cap + 1, LANES):
        if n % t == 0:
            best = t
    return best if best is not None else n


def _mm_tn(a, b, name):
    kk, m = a.shape
    _, n = b.shape
    tm = _pick_tile(m, 1408)
    tn = _pick_tile(n, 1408)
    tk = min(512, kk)

    n_k = kk // tk

    def body(a_ref, b_ref, o_ref, acc_ref):
        @pl.when(pl.program_id(2) == 0)
        def _():
            acc_ref[...] = jnp.zeros_like(acc_ref)
        acc_ref[...] += _dot_tn(a_ref[...].astype(bf16), b_ref[...].astype(bf16))

        @pl.when(pl.program_id(2) == n_k - 1)
        def _():
            o_ref[...] = acc_ref[...].astype(bf16)

    return pl.pallas_call(
        body, name=name, grid=(m // tm, n // tn, n_k),
        in_specs=[pl.BlockSpec((tk, tm), lambda i, j, k: (k, i)), pl.BlockSpec((tk, tn), lambda i, j, k: (k, j))],
        out_specs=pl.BlockSpec((tm, tn), lambda i, j, k: (i, j)),
        out_shape=jax.ShapeDtypeStruct((m, n), bf16),
        scratch_shapes=[pltpu.VMEM((tm, tn), f32)],
        compiler_params=_cp(("parallel", "parallel", "arbitrary"), VMEM_LIMIT),
    )(a, b)


def _rope_tables(positions):
    inv_freq = ROPE_THETA ** (-jnp.arange(0, QK_ROPE, 2, dtype=f32) / QK_ROPE)
    ang = positions.astype(f32)[:, None] * inv_freq
    cos, sin = jnp.cos(ang), jnp.sin(ang)
    zero = jnp.zeros_like(cos)
    return (jnp.concatenate([cos, cos, zero, zero], axis=1), jnp.concatenate([zero, sin, zero, zero], axis=1),
            jnp.concatenate([-sin, zero, zero, zero], axis=1))


def _pad256(g):
    return jnp.pad(g.reshape(1, QK_HEAD), ((0, 0), (0, QK_PAD - QK_HEAD)))


RELAYOUT_BLOCKS = 8
FIRST = ("w_in", "w_qb", "w_kvb", "lb_param")
SECOND = ("w_o", "w_gate", "w_up", "w_down", "w_ple_gate", "w_ple_proj")
ROW_SHARDED = ("w_o", "w_down", "w_ple_gate")


def _col_moves(j):
    lo = BIG["w_in"][1] * j
    w_in = [(max(lo, a) - lo, min(lo + BIG["w_in"][1], b) - lo, d + max(lo, a) - a)
            for a, b, d in Z_SEGMENTS if max(lo, a) < min(lo + BIG["w_in"][1], b)]
    head, half = divmod(j, 2)
    whole = lambda n: [(0, BIG[n][1], BIG[n][1] * j)]
    return {"w_in": w_in, "w_gate": whole("w_gate"), "w_up": whole("w_up"),
            "w_qb": [(0, 96, QK_PAD * head + 96 * half)], "w_kvb": whole("w_kvb"), "w_ple_proj": whole("w_ple_proj"),
            "lb_param": whole("lb_param")}


def _kernel_width(name):
    return {"w_in": Z_W, "w_qb": MLA_HEADS * QK_PAD}.get(name, N_DEV * BIG[name][1])


def _relayout_specs(names, by_dev):
    specs = []
    for n in names:
        rows, cols = BIG[n]
        if n == "lb_param":
            specs.append(_acc_spec((N_DEV, rows, cols) if by_dev else (rows, _kernel_width(n))))
        elif by_dev:
            specs.append(pl.BlockSpec((N_DEV, rows // RELAYOUT_BLOCKS, cols), lambda i: (0, i, 0)))
        else:
            specs.append(pl.BlockSpec((rows // RELAYOUT_BLOCKS, _kernel_width(n)), lambda i: (i, 0)))
    return specs


def _weights_in(gathered, names, name):
    n = len(names)

    def body(*refs):
        ins, outs = dict(zip(names, refs[:n])), dict(zip(names, refs[n:]))
        if "w_in" in outs:
            outs["w_in"][:, Z_KR + QK_ROPE:Z_W] = jnp.zeros((outs["w_in"].shape[0], Z_W - Z_KR - QK_ROPE), bf16)
        if "w_qb" in outs:
            for h in range(MLA_HEADS):
                outs["w_qb"][:, QK_PAD * h + QK_HEAD:QK_PAD * (h + 1)] = jnp.zeros((outs["w_qb"].shape[0], QK_PAD - QK_HEAD), bf16)
        for j in range(N_DEV):
            for wn, moves in _col_moves(j).items():
                if wn in outs:
                    for s0, s1, d0 in moves:
                        outs[wn][:, d0:d0 + s1 - s0] = ins[wn][j, :, s0:s1]

    outs = pl.pallas_call(
        body, name=name, grid=(RELAYOUT_BLOCKS,), in_specs=_relayout_specs(names, True), out_specs=_relayout_specs(names, False),
        out_shape=[jax.ShapeDtypeStruct((BIG[wn][0], _kernel_width(wn)), gathered[wn].dtype) for wn in names],
        compiler_params=_cp(("arbitrary",), VMEM_LIMIT),
    )(*[gathered[wn] for wn in names])
    return dict(zip(names, outs))


def _grads_out(sources, names, name):
    pieces = [(wn, start, arr) for wn in names for start, arr in sources[wn]]
    n_in = len(pieces)

    def body(*refs):
        outs = dict(zip(names, refs[n_in:]))

        def cols(wn, c0, c1):
            for (pn, start, arr), ref in zip(pieces, refs[:n_in]):
                if pn == wn and start <= c0 and c1 <= start + arr.shape[1]:
                    return ref[:, c0 - start:c1 - start]

        for j in range(N_DEV):
            for wn, moves in _col_moves(j).items():
                if wn in outs:
                    for s0, s1, d0 in moves:
                        outs[wn][j, :, s0:s1] = cols(wn, d0, d0 + s1 - s0).astype(bf16)

    in_specs = [_acc_spec(arr.shape) if wn == "lb_param" else pl.BlockSpec((arr.shape[0] // RELAYOUT_BLOCKS, arr.shape[1]), lambda i: (i, 0))
                for wn, _, arr in pieces]
    outs = pl.pallas_call(
        body, name=name, grid=(RELAYOUT_BLOCKS,), in_specs=in_specs, out_specs=_relayout_specs(names, True),
        out_shape=[jax.ShapeDtypeStruct((N_DEV, *BIG[wn]), bf16) for wn in names],
        compiler_params=_cp(("arbitrary",), VMEM_LIMIT),
    )(*[arr for _, _, arr in pieces])
    return dict(zip(names, outs))


def kernel(x, p, positions, g_mix, w_in, g_qa, g_kva, w_qb, w_kvb, g_qn, g_kn, lb_param, g_hgo, w_o, g_ffn, w_gate, w_up, w_down, g_ple, w_ple_gate, w_ple_proj, loss_target, m_g_mix, m_w_in, m_g_qa, m_g_kva, m_w_qb, m_w_kvb, m_g_qn, m_g_kn, m_lb_param, m_g_hgo, m_w_o, m_g_ffn, m_w_gate, m_w_up, m_w_down, m_g_ple, m_w_ple_gate, m_w_ple_proj, v_g_mix, v_w_in, v_g_qa, v_g_kva, v_w_qb, v_w_kvb, v_g_qn, v_g_kn, v_lb_param, v_g_hgo, v_w_o, v_g_ffn, v_w_gate, v_w_up, v_w_down, v_g_ple, v_w_ple_gate, v_w_ple_proj):
    w_all = dict(g_mix=g_mix, g_qa=g_qa, g_kva=g_kva, g_qn=g_qn, g_kn=g_kn, g_hgo=g_hgo, g_ffn=g_ffn, g_ple=g_ple,
                 w_in=w_in, w_qb=w_qb, w_kvb=w_kvb, w_o=w_o, w_gate=w_gate, w_up=w_up, w_down=w_down,
                 w_ple_gate=w_ple_gate, w_ple_proj=w_ple_proj, lb_param=lb_param)
    m_all = dict(g_mix=m_g_mix, g_qa=m_g_qa, g_kva=m_g_kva, g_qn=m_g_qn, g_kn=m_g_kn, g_hgo=m_g_hgo, g_ffn=m_g_ffn,
                 g_ple=m_g_ple, w_in=m_w_in, w_qb=m_w_qb, w_kvb=m_w_kvb, w_o=m_w_o, w_gate=m_w_gate, w_up=m_w_up,
                 w_down=m_w_down, w_ple_gate=m_w_ple_gate, w_ple_proj=m_w_ple_proj, lb_param=m_lb_param)
    v_all = dict(g_mix=v_g_mix, g_qa=v_g_qa, g_kva=v_g_kva, g_qn=v_g_qn, g_kn=v_g_kn, g_hgo=v_g_hgo, g_ffn=v_g_ffn,
                 g_ple=v_g_ple, w_in=v_w_in, w_qb=v_w_qb, w_kvb=v_w_kvb, w_o=v_w_o, w_gate=v_w_gate, w_up=v_w_up,
                 w_down=v_w_down, w_ple_gate=v_w_ple_gate, w_ple_proj=v_w_ple_proj, lb_param=v_lb_param)
    me_idx = jnp.stack([_me()]).astype(jnp.int32)
    x, p, positions, target = x[0], p[0, 0], positions[0], loss_target[0]
    s = x.shape[0]
    tm, tm_ffn, tq_f, tq_b = min(256, s), min(128, s), min(512, s), min(256, s)
    g_mix, g_qa, g_kva, g_qn, g_kn, g_hgo, g_ffn, g_ple = (w_all[n].reshape(1, -1) for n in SMALL)
    g_qn_p, g_kn_p = _pad256(g_qn), _pad256(g_kn)
    cosb, sina, sinb = _rope_tables(positions)
    shard = lambda n: w_all[n].reshape(BIG[n])

    first = _all_gather([shard(n) for n in FIRST], [f32 if n == "lb_param" else bf16 for n in FIRST], "ag_first")
    lands = _cast_to_slot([shard(n) for n in SECOND], me_idx, first[0])
    ag2, token = _exchange_start([], lands, "ag_second_start")
    wk = _weights_in(dict(zip(FIRST, first)), FIRST, "weights_in_first")
    wz, wqb, wkvb, lb4 = (wk[n] for n in FIRST)

    h1, z = _fwd_in(x, g_mix, wz, tm)
    q, k, v = _fwd_mla_proj(z, cosb + token[0, 0], sina, sinb, g_qa, g_kva, wqb, wkvb, g_qn_p, g_kn_p, tm)
    a = _fwd_attn(q, k, v, tq_f)
    o = _fwd_gla(z, lb4)

    second = dict(zip(SECOND, _exchange_wait(ag2, o, "ag_second_wait")))
    wk = _weights_in(second, ("w_gate", "w_up", "w_ple_proj"), "weights_in_second")
    w_gate, w_up, w_pp = wk["w_gate"], wk["w_up"], wk["w_ple_proj"]
    w_o, w_down, w_pg = (second[n].reshape(N_DEV * BIG[n][0], BIG[n][1]) for n in ROW_SHARDED)

    x2, cat = _fwd_mix(a, o, z, g_hgo, x, w_o, tm)
    x3, gp, up = _fwd_ffn(x2, g_ffn, w_gate, w_up, w_down, tm)
    d3, h3, dpre, dpp, dg_ple, loss_tile = _ple_loss_fwd_bwd(x3, g_ple, w_pg, p, w_pp, target, tm)
    d2, h2, act, dgp, dup, dg_ffn = _bwd_ffn(d3, x2, gp, up, g_ffn, w_gate, w_up, w_down, tm_ffn)

    blocks = _grads_out({"w_gate": [(0, _mm_tn(h2, dgp, "dw_gate"))], "w_up": [(0, _mm_tn(h2, dup, "dw_up"))],
                         "w_ple_proj": [(0, _mm_tn(p, dpp, "dw_ple_proj"))]}, ("w_gate", "w_up", "w_ple_proj"), "grads_out_second")
    row_grads = {"w_o": _mm_tn(cat, d2, "dw_o"), "w_down": _mm_tn(act, d3, "dw_down"), "w_ple_gate": _mm_tn(h3, dpre, "dw_ple_gate")}
    blocks.update({n: g.reshape(N_DEV, *BIG[n]) for n, g in row_grads.items()})
    empty = lambda names: [lax.empty((N_PEERS, *BIG[n]), bf16) for n in names]
    rs2, token = _exchange_start([blocks[n] for n in SECOND], empty(SECOND), "rs_second_start")

    da, do, dz_hg, dg_hgo = _bwd_mix(d2, w_o, o, z, g_hgo + token[0, 0], tm)
    dz_hq, dz_hff, dz_hfb, dz_hi, dlb4 = _bwd_gla(z, lb4, do)
    dq, dk, dv = _bwd_attn(q, k, v, da, tq_b)
    dz_mla, cqn, ckvn, dq0, dkv0, dg_qa, dg_kva, dg_qn, dg_kn = _bwd_mla_proj(
        z, dq, dk, dv, cosb, sina, sinb, g_qa, g_kva, wqb, wkvb, g_qn_p, g_kn_p, tm)

    gz = [(Z_HQ, _mm_tn(h1, dz_hq, "dw_in_hq")), (Z_HFF, _mm_tn(h1, dz_hff, "dw_in_hff")),
          (Z_HFB, _mm_tn(h1, dz_hfb, "dw_in_hfb")), (Z_HI, _mm_tn(h1, dz_hi, "dw_in_hi")),
          (Z_HG, _mm_tn(h1, dz_hg, "dw_in_hg")), (Z_CQ, _mm_tn(h1, dz_mla, "dw_in_mla"))]
    blocks1 = _grads_out({"w_in": gz, "w_qb": [(0, _mm_tn(cqn, dq0, "dw_qb"))], "w_kvb": [(0, _mm_tn(ckvn, dkv0, "dw_kvb"))],
                          "lb_param": [(0, dlb4)]}, FIRST, "grads_out_first")
    rs1, token = _exchange_start([blocks1[n] for n in FIRST], empty(FIRST), "rs_first_start")

    result = {}

    def adam(names, lands, src, n_blocks):
        outs = _adam_shards(me_idx, [src[n] for n in names], lands, [w_all[n] for n in names], [m_all[n] for n in names],
                            [v_all[n] for n in names], n_blocks, "adamw_" + names[0])
        result.update(zip(names, outs))

    lands2 = dict(zip(SECOND, _exchange_wait(rs2, dz_mla, "rs_second_wait")))
    by8 = tuple(n for n in SECOND if n != "w_down")
    adam(by8, [lands2[n] for n in by8], blocks, 8)
    adam(("w_down",), [lands2["w_down"]], blocks, 2)

    segments = [(dz_hq, 512, 0, Z_HQ // 512), (dz_hff, 512, 0, Z_HFF // 512), (dz_hfb, 512, 0, Z_HFB // 512),
                (dz_hi, 512, 0, Z_HI // 512), (dz_hg, 512, 0, Z_HG // 512), (dz_mla, 640, 0, Z_CQ // 640)]
    grad_x, dg_mix = _bwd_in(segments, wz, x, g_mix + token[0, 0], d2, tm)
    dgains = (dg_mix, dg_qa, dg_kva, dg_qn, dg_kn, dg_hgo, dg_ffn, dg_ple)

    lands1 = _exchange_wait(rs1, grad_x, "rs_first_wait")
    adam(FIRST, lands1, blocks1, 8)

    vec = jnp.concatenate(list(dgains) + [loss_tile[0:1]], axis=1)
    parts = _all_gather([vec], [f32], "ag_gains")[0]
    outs, loss_row = _adam_gains(parts, [w_all[n] for n in SMALL], [m_all[n] for n in SMALL], [v_all[n] for n in SMALL])
    result.update(zip(SMALL, outs))

    order = ("g_mix", "w_in", "g_qa", "g_kva", "w_qb", "w_kvb", "g_qn", "g_kn", "lb_param", "g_hgo", "w_o", "g_ffn",
             "w_gate", "w_up", "w_down", "g_ple", "w_ple_gate", "w_ple_proj")
    return (loss_row[0, 0], grad_x[None], *[result[n][k] for k in range(4) for n in order])
```

```python
import functools
import math

import jax
import jax.numpy as jnp
from jax import lax
from jax.experimental import pallas as pl
from jax.experimental.pallas import tpu as pltpu

f32 = jnp.float32
bf16 = jnp.bfloat16

N_DEV = 8
D_MODEL = 1024
MLA_HEADS = 4
QK_NOPE = 128
QK_ROPE = 64
QK_HEAD = QK_NOPE + QK_ROPE
QK_PAD = 256
V_HEAD = 128
Q_LORA = 256
KV_LORA = 256
HG_HEADS = 4
HG_DK = 128
CHUNK = 64
D_FF = 2816
PLE_DIM = 256
ROPE_THETA = 10000.0
EPS = 1e-6
ATTN_SCALE = QK_HEAD ** -0.5
IN_SIZES = (256, 256, 64, 512, 512, 512, 512, 512)
D_IN = sum(IN_SIZES)
Z_HQ, Z_HFF, Z_HFB, Z_HI, Z_HG, Z_CQ, Z_CKV, Z_KR, Z_W = 0, 512, 1024, 1536, 2048, 2560, 2816, 3072, 3200

ADAM_LR, ADAM_B1, ADAM_B2, ADAM_EPS, ADAM_WD, ADAM_STEP = 0.001, 0.9, 0.999, 1e-08, 0.01, 10

LANES = 128
BIG = {"w_in": (1024, 392), "w_qb": (256, 96), "w_kvb": (256, 128), "w_o": (128, 1024), "w_gate": (1024, 352),
       "w_up": (1024, 352), "w_down": (352, 1024), "w_ple_gate": (128, 1024), "w_ple_proj": (256, 128),
       "lb_param": (4, 64)}
ROW_BLOCKS = {("w_in", "w_qb", "w_kvb", "w_o", "w_gate", "w_up", "w_ple_gate", "w_ple_proj"): 8, ("w_down", "lb_param"): 2}
SMALL = {"g_mix": (0, 1024), "g_qa": (1024, 256), "g_kva": (1280, 256), "g_qn": (1536, 192), "g_kn": (1792, 192),
         "g_hgo": (2048, 512), "g_ffn": (2560, 1024), "g_ple": (3584, 1024)}
LOSS_OFF = 4608
GAIN_VEC = LOSS_OFF + LANES
Z_SEGMENTS = ((0, 256, Z_CQ), (256, 512, Z_CKV), (512, 576, Z_KR), (576, 1088, Z_HQ), (1088, 1600, Z_HFF),
              (1600, 2112, Z_HFB), (2112, 2624, Z_HI), (2624, 3136, Z_HG))

VMEM_LIMIT = 56 * 1024 * 1024
MESH = pl.DeviceIdType.MESH


def _cp(sem=None, vmem=None):
    return pltpu.CompilerParams(dimension_semantics=sem, vmem_limit_bytes=vmem)


def _const_spec(shape):
    nd = len(shape)
    return pl.BlockSpec(shape, lambda *_: (0,) * nd, pipeline_mode=pl.Buffered(1))


def _acc_spec(shape):
    nd = len(shape)
    return pl.BlockSpec(shape, lambda *_: (0,) * nd)


def _sigmoid(x):
    return jax.nn.sigmoid(x)


def _dot(a, b):
    return jnp.dot(a, b, preferred_element_type=f32)


def _dot_nt(a, b):
    return lax.dot_general(a, b, (((1,), (1,)), ((), ())), preferred_element_type=f32)


def _dot_tn(a, b):
    return lax.dot_general(a, b, (((0,), (0,)), ((), ())), preferred_element_type=f32)


def _rms_fwd(x, g, width):
    r = lax.rsqrt(jnp.sum(x * x, axis=-1, keepdims=True) * (1.0 / width) + EPS)
    return x * r * g, r


def _rms_bwd(dy, x, r, g, width):
    u = dy * g
    dx = r * u - x * (r * r * r) * (jnp.sum(u * x, axis=-1, keepdims=True) * (1.0 / width))
    return dx, dy * x * r


def _rope(b, c, sa, sb):
    return b * c + pltpu.roll(b, 32, 1) * sa + pltpu.roll(b, 96, 1) * sb


def _all_gather(shards, dtypes, name):
    n = len(shards)

    def body(*refs):
        in_refs, out_refs, stage = refs[:n], refs[n:2 * n], refs[2 * n:3 * n]
        send_sems, recv_sems, local_sems = refs[3 * n:]
        for w in range(n):
            stage[w][...] = in_refs[w][...].astype(stage[w].dtype)
        x, y, c = lax.axis_index("x"), lax.axis_index("y"), lax.axis_index("c")
        me, sibling = (x, y, c), (x, y, 1 - c)
        chips = [(1 - x, y), (x, 1 - y), (1 - x, 1 - y)]

        def slot(w, px, py, pc):
            return out_refs[w].at[4 * px + 2 * py + pc]

        def copy(w, k, block, to, src=None):
            return pltpu.make_async_remote_copy(
                src_ref=slot(w, *block) if src is None else src, dst_ref=slot(w, *block),
                send_sem=send_sems.at[w, k], recv_sem=recv_sems.at[w, k], device_id=to, device_id_type=MESH)

        first = []
        for j, chip in enumerate(chips):
            first += [copy(w, 1 + j, me, (*chip, c), src=stage[w]) for w in range(n)]
        first += [copy(w, 0, me, sibling, src=stage[w]) for w in range(n)]
        mine = [pltpu.make_async_copy(stage[w], slot(w, *me), local_sems.at[w]) for w in range(n)]
        for cp in first + mine:
            cp.start()
        passed = []
        for j, chip in enumerate(chips):
            for w in range(n):
                copy(w, 1 + j, (*chip, c), me).wait_recv()
                passed.append(copy(w, 4 + j, (*chip, c), sibling))
                passed[-1].start()
        for w in range(n):
            copy(w, 0, sibling, me).wait_recv()
        for j, chip in enumerate(chips):
            for w in range(n):
                copy(w, 4 + j, (*chip, 1 - c), me).wait_recv()
        for cp in first + passed:
            cp.wait_send()
        for cp in mine:
            cp.wait()

    return pl.pallas_call(
        body, name=name,
        out_shape=[jax.ShapeDtypeStruct((N_DEV, *s.shape), dt) for s, dt in zip(shards, dtypes)],
        in_specs=[pl.BlockSpec(memory_space=pltpu.VMEM)] * n,
        out_specs=[pl.BlockSpec(memory_space=pl.ANY)] * n,
        scratch_shapes=[pltpu.VMEM(s.shape, dt) for s, dt in zip(shards, dtypes)]
        + [pltpu.SemaphoreType.DMA((n, 7)), pltpu.SemaphoreType.DMA((n, 7)), pltpu.SemaphoreType.DMA((n,))],
        compiler_params=_cp(None, VMEM_LIMIT),
    )(*shards)


N_PEERS = N_DEV - 1
HBM_SPEC = pl.BlockSpec(memory_space=pltpu.HBM)
SEM_SPEC = pl.BlockSpec(memory_space=pltpu.SEMAPHORE)
DATAFLOW = pltpu.SideEffectType.DATAFLOW_SIDE_EFFECTING


def _me():
    return 4 * lax.axis_index("x") + 2 * lax.axis_index("y") + lax.axis_index("c")


def _peer(k):
    x, y, c = lax.axis_index("x"), lax.axis_index("y"), lax.axis_index("c")
    px = 1 - x if k & 4 else x
    py = 1 - y if k & 2 else y
    pc = 1 - c if k & 1 else c
    return (px, py, pc), 4 * px + 2 * py + pc


def _exchange_copies(src_refs, land_refs, send_sems, recv_sems, gather):
    cps = []
    me = _me()
    for k in range(1, N_DEV):
        peer, peer_idx = _peer(k)
        for w, land in enumerate(land_refs):
            src = land.at[me] if gather else src_refs[w].at[peer_idx]
            dst = land.at[me] if gather else land.at[k - 1]
            cps.append(pltpu.make_async_remote_copy(
                src_ref=src, dst_ref=dst, send_sem=send_sems.at[N_PEERS * w + k - 1], recv_sem=recv_sems.at[N_PEERS * w + k - 1],
                device_id=peer, device_id_type=MESH))
    return cps


def _exchange_start(srcs, lands, name):
    n_src, n = len(srcs), len(lands)

    def body(*refs):
        src_refs, land_refs = refs[:n_src], refs[n_src:n_src + n]
        send_sems, recv_sems = refs[n_src + n], refs[n_src + n + 1]
        token = refs[-1]
        for cp in _exchange_copies(src_refs, land_refs, send_sems, recv_sems, gather=not n_src):
            cp.start()
        token[...] = jnp.zeros_like(token)

    arrays = [pltpu.with_memory_space_constraint(a, pltpu.HBM) for a in (*srcs, *lands)]
    outs = pl.pallas_call(
        body, name=name,
        out_shape=(pltpu.SemaphoreType.DMA((n * N_PEERS,)), pltpu.SemaphoreType.DMA((n * N_PEERS,)),
                   *[pltpu.HBM(a.shape, a.dtype) for a in arrays], jax.ShapeDtypeStruct((8, LANES), f32)),
        in_specs=[HBM_SPEC] * len(arrays),
        out_specs=(SEM_SPEC, SEM_SPEC, *[HBM_SPEC] * len(arrays), pl.BlockSpec(memory_space=pltpu.VMEM)),
        input_output_aliases={i: 2 + i for i in range(len(arrays))},
        compiler_params=pltpu.CompilerParams(has_side_effects=DATAFLOW),
    )(*arrays)
    return (outs[0], outs[1], outs[2:2 + n_src], outs[2 + n_src:2 + n_src + n]), outs[-1]


def _exchange_wait(state, after, name):
    send_sems, recv_sems, srcs, lands = state
    n_src, n = len(srcs), len(lands)

    def body(*refs):
        src_refs, land_refs = refs[:n_src], refs[n_src:n_src + n]
        send_ref, recv_ref = refs[n_src + n], refs[n_src + n + 1]
        for cp in _exchange_copies(src_refs, land_refs, send_ref, recv_ref, gather=not n_src):
            cp.wait_send()
            cp.wait_recv()

    arrays = (*srcs, *lands)
    outs = pl.pallas_call(
        body, name=name,
        out_shape=tuple(pltpu.HBM(a.shape, a.dtype) for a in arrays),
        in_specs=[HBM_SPEC] * len(arrays) + [SEM_SPEC, SEM_SPEC, pl.BlockSpec(memory_space=pl.ANY)],
        out_specs=tuple([HBM_SPEC] * len(arrays)),
        input_output_aliases={i: i for i in range(len(arrays))},
        compiler_params=pltpu.CompilerParams(has_side_effects=DATAFLOW),
    )(*arrays, send_sems, recv_sems, after)
    return outs[n_src:]


def _cast_to_slot(shards, me_idx, after):
    n = len(shards)

    def body(i_ref, *refs):
        for w in range(n):
            refs[n + 1 + w][...] = refs[w][...].astype(bf16)

    return pl.pallas_call(
        body, name="cast_to_slot",
        grid_spec=pltpu.PrefetchScalarGridSpec(
            num_scalar_prefetch=1, grid=(1,),
            in_specs=[pl.BlockSpec(s.shape, lambda i, m: (0, 0)) for s in shards] + [pl.BlockSpec(memory_space=pl.ANY)],
            out_specs=[pl.BlockSpec((None, *s.shape), lambda i, m: (m[0], 0, 0)) for s in shards]),
        out_shape=[jax.ShapeDtypeStruct((N_DEV, *s.shape), bf16) for s in shards],
        compiler_params=_cp(("arbitrary",), VMEM_LIMIT),
    )(me_idx, *shards, after)


def _row_block(rows, n_blocks):
    return (rows // n_blocks, True) if rows % (16 * n_blocks) == 0 else (rows, False)


def _adam_math(w, g, m, v):
    m = ADAM_B1 * m + (1.0 - ADAM_B1) * g
    v = ADAM_B2 * v + (1.0 - ADAM_B2) * (g * g)
    m_hat = m / (1.0 - ADAM_B1 ** ADAM_STEP)
    v_hat = v / (1.0 - ADAM_B2 ** ADAM_STEP)
    delta = -ADAM_LR * (m_hat / (jnp.sqrt(v_hat) + ADAM_EPS) + ADAM_WD * w)
    return delta, m, v


def _adam_shards(me_idx, blocks, lands, ws, ms, vs, n_blocks, name):
    n = len(blocks)

    def body(i_ref, *refs):
        ins, outs = refs[:5 * n], refs[5 * n:]
        for w in range(n):
            g_ref, b_ref, w_ref, m_ref, v_ref = (ins[t * n + w] for t in range(5))
            g = g_ref[...].astype(f32)
            for k in range(N_PEERS):
                g = g + b_ref[k].astype(f32)
            if len(w_ref.shape) == 2:
                pieces = [(slice(None), g)]
            else:
                pieces = [(a, g[2 * a:2 * a + 2]) for a in range(2)]
            for at, gp in pieces:
                vals = (gp,) + _adam_math(w_ref[at], gp, m_ref[at], v_ref[at])
                for t, val in enumerate(vals):
                    outs[4 * w + t][at] = val

    specs = [[] for _ in range(5)]
    out_specs, out_shape = [], []
    for g, wt in zip(blocks, ws):
        rows, cols = g.shape[1:]
        rb, cut = _row_block(rows, n_blocks)
        specs[0].append(pl.BlockSpec((None, rb, cols), functools.partial(lambda i, s, cut: (s[0], i if cut else 0, 0), cut=cut)))
        specs[1].append(pl.BlockSpec((N_PEERS, rb, cols), functools.partial(lambda i, s, cut: (0, i if cut else 0, 0), cut=cut)))
        if wt.shape[0] == 1:
            shard = pl.BlockSpec((None, rb, cols), functools.partial(lambda i, s, cut: (0, i if cut else 0, 0), cut=cut))
        else:
            shard = pl.BlockSpec(wt.shape, functools.partial(lambda i, s, nd: (0,) * nd, nd=wt.ndim))
        for t in (2, 3, 4):
            specs[t].append(shard)
        out_specs += [shard] * 4
        out_shape += [jax.ShapeDtypeStruct(wt.shape, f32)] * 4
    outs = pl.pallas_call(
        body, name=name,
        grid_spec=pltpu.PrefetchScalarGridSpec(num_scalar_prefetch=1, grid=(n_blocks,), in_specs=sum(specs, []), out_specs=out_specs),
        out_shape=out_shape,
        compiler_params=_cp(("arbitrary",), VMEM_LIMIT),
    )(me_idx, *blocks, *lands, *ws, *ms, *vs)
    return [outs[4 * w:4 * w + 4] for w in range(n)]


def _adam_gains(parts, ws, ms, vs):
    n = len(ws)

    def body(p_ref, *refs):
        ins, outs = refs[:3 * n], refs[3 * n:]
        g_all = p_ref[0]
        for k in range(1, N_DEV):
            g_all = g_all + p_ref[k]
        for w, (off, lanes) in enumerate(SMALL.values()):
            w_ref, m_ref, v_ref = ins[w], ins[n + w], ins[2 * n + w]
            if len(w_ref.shape) == 2:
                pieces = [(slice(None), off, lanes)]
            else:
                pieces = [((slice(None), h), off + LANES * h, LANES) for h in range(w_ref.shape[1])]
            for at, o, ln in pieces:
                g = g_all[:, o:o + ln]
                vals = (g,) + _adam_math(w_ref[at], g, m_ref[at], v_ref[at])
                for t, val in enumerate(vals):
                    outs[4 * w + t][at] = val
        outs[4 * n][...] = g_all[:, LOSS_OFF:LOSS_OFF + LANES]

    out_shape = sum([[jax.ShapeDtypeStruct(w.shape, f32)] * 4 for w in ws], []) + [jax.ShapeDtypeStruct((1, LANES), f32)]
    outs = pl.pallas_call(body, name="adamw_gains", out_shape=out_shape)(parts, *ws, *ms, *vs)
    return [outs[4 * w:4 * w + 4] for w in range(n)], outs[4 * n]


def _fwd_in(x, g_mix, wz, tm):
    s, d = x.shape

    def body(x_ref, g_ref, w_ref, h_ref, z_ref):
        h, _ = _rms_fwd(x_ref[...], g_ref[...], d)
        hb = h.astype(bf16)
        h_ref[...] = hb
        z_ref[...] = _dot(hb, w_ref[...])

    return pl.pallas_call(
        body, name="fwd_in", grid=(s // tm,),
        in_specs=[pl.BlockSpec((tm, d), lambda i: (i, 0)), _const_spec((1, d)), _const_spec((d, Z_W))],
        out_specs=[pl.BlockSpec((tm, d), lambda i: (i, 0)), pl.BlockSpec((tm, Z_W), lambda i: (i, 0))],
        out_shape=[jax.ShapeDtypeStruct((s, d), bf16), jax.ShapeDtypeStruct((s, Z_W), f32)],
        compiler_params=_cp(("parallel",), VMEM_LIMIT),
    )(x, g_mix, wz)


def _mla_qk_fwd(cq, ckv, kr, g_qa, g_kva, wqb, wkvb, g_qn, g_kn):
    cqn, rq = _rms_fwd(cq, g_qa, Q_LORA)
    ckvn, rkv = _rms_fwd(ckv, g_kva, KV_LORA)
    cqn_b, ckvn_b = cqn.astype(bf16), ckvn.astype(bf16)
    q0 = _dot(cqn_b, wqb)
    kv0 = _dot(ckvn_b, wkvb)
    return cqn_b, rq, ckvn_b, rkv, q0, kv0


def _fwd_mla_proj(z, cosb, sina, sinb, g_qa, g_kva, wqb, wkvb, g_qn, g_kn, tm):
    s = z.shape[0]
    hh = MLA_HEADS

    def body(cq_ref, ckv_ref, kr_ref, c_ref, sa_ref, sb_ref, gqa_ref, gkva_ref, wqb_ref, wkvb_ref, gqn_ref, gkn_ref,
             q_ref, k_ref, v_ref):
        _, _, _, _, q0, kv0 = _mla_qk_fwd(cq_ref[...], ckv_ref[...], kr_ref[...], gqa_ref[...], gkva_ref[...],
                                          wqb_ref[...], wkvb_ref[...], gqn_ref[...], gkn_ref[...])
        kr = kr_ref[...]
        c, sa, sb = c_ref[...], sa_ref[...], sb_ref[...]
        gqn, gkn = gqn_ref[...], gkn_ref[...]
        kr_sq = jnp.sum(kr * kr, axis=-1, keepdims=True)
        for h in range(hh):
            qh = q0[:, QK_PAD * h:QK_PAD * (h + 1)]
            qn, _ = _rms_fwd(qh, gqn, QK_HEAD)
            q_ref[h, :, 0:128] = qn[:, 0:128].astype(bf16)
            q_ref[h, :, 128:256] = _rope(qn[:, 128:256], c, sa, sb).astype(bf16)
            kn_ = kv0[:, 256 * h:256 * h + 128]
            rk = lax.rsqrt((jnp.sum(kn_ * kn_, axis=-1, keepdims=True) + kr_sq) * (1.0 / QK_HEAD) + EPS)
            k_ref[h, :, 0:128] = (kn_ * rk * gkn[:, 0:128]).astype(bf16)
            k_ref[h, :, 128:256] = _rope(kr * rk * gkn[:, 128:256], c, sa, sb).astype(bf16)
            v_ref[h] = kv0[:, 256 * h + 128:256 * h + 256].astype(bf16)

    row128 = pl.BlockSpec((tm, 128), lambda i: (i, 0))
    return pl.pallas_call(
        body, name="fwd_mla_proj", grid=(s // tm,),
        in_specs=[pl.BlockSpec((tm, 256), lambda i: (i, Z_CQ // 256)), pl.BlockSpec((tm, 256), lambda i: (i, Z_CKV // 256)),
                  pl.BlockSpec((tm, 128), lambda i: (i, Z_KR // 128)), row128, row128, row128,
                  _const_spec((1, 256)), _const_spec((1, 256)), _const_spec((256, 1024)), _const_spec((256, 1024)),
                  _const_spec((1, 256)), _const_spec((1, 256))],
        out_specs=[pl.BlockSpec((hh, tm, QK_PAD), lambda i: (0, i, 0)), pl.BlockSpec((hh, tm, QK_PAD), lambda i: (0, i, 0)),
                   pl.BlockSpec((hh, tm, V_HEAD), lambda i: (0, i, 0))],
        out_shape=[jax.ShapeDtypeStruct((hh, s, QK_PAD), bf16), jax.ShapeDtypeStruct((hh, s, QK_PAD), bf16),
                   jax.ShapeDtypeStruct((hh, s, V_HEAD), bf16)],
        compiler_params=_cp(("parallel",), VMEM_LIMIT),
    )(z, z, z, cosb, sina, sinb, g_qa, g_kva, wqb, wkvb, g_qn, g_kn)


def _fwd_attn(q, k, v, tq):
    hh, s, _ = q.shape

    def body(q_ref, k_ref, v_ref, o_ref):
        sc = _dot_nt(q_ref[...], k_ref[...]) * ATTN_SCALE
        p = jnp.exp(sc - jnp.max(sc, axis=-1, keepdims=True))
        l = jnp.sum(p, axis=-1, keepdims=True)
        o_ref[...] = (_dot(p.astype(bf16), v_ref[...]) * (1.0 / l)).astype(bf16)

    return pl.pallas_call(
        body, name="fwd_attn", grid=(hh, s // tq),
        in_specs=[pl.BlockSpec((None, tq, QK_PAD), lambda h, i: (h, i, 0)),
                  pl.BlockSpec((None, s, QK_PAD), lambda h, i: (h, 0, 0)),
                  pl.BlockSpec((None, s, V_HEAD), lambda h, i: (h, 0, 0))],
        out_specs=pl.BlockSpec((tq, V_HEAD), lambda h, i: (i, h)),
        out_shape=jax.ShapeDtypeStruct((s, hh * V_HEAD), bf16),
        compiler_params=_cp(("parallel", "parallel"), VMEM_LIMIT),
    )(q, k, v)


def _split3(x):
    hi = x.astype(bf16)
    r1 = x - hi.astype(f32)
    mid = r1.astype(bf16)
    lo = (r1 - mid.astype(f32)).astype(bf16)
    return jnp.concatenate([hi, mid, lo], axis=-1)


def _tri_sum(tri, x):
    y = _dot(tri, _split3(x))
    return y[:, 0:128] + y[:, 128:256] + y[:, 256:384]


GLA_GROUP = 4
GLA_ROWS = GLA_GROUP * CHUNK
GLA_HEADS_PER_STEP = 2


def _gla_masks(rev):
    row = lax.broadcasted_iota(jnp.int32, (GLA_ROWS, GLA_ROWS), 0)
    col = lax.broadcasted_iota(jnp.int32, (GLA_ROWS, GLA_ROWS), 1)
    shift = CHUNK.bit_length() - 1
    same = (jnp.right_shift(row, shift) == jnp.right_shift(col, shift)).astype(f32)
    lower, upper = (row >= col).astype(f32) * same, (row <= col).astype(f32) * same
    keep, keep_t = (upper, lower) if rev else (lower, upper)
    chunk_of = jnp.right_shift(lax.broadcasted_iota(jnp.int32, (GLA_ROWS, 1), 0), shift)
    return keep, keep.astype(bf16), keep_t.astype(bf16), [(chunk_of == c).astype(f32) for c in range(GLA_GROUP)]


def _gla_gates(hq, hf, lower):
    sg = _sigmoid(hf)
    f = lower + (1.0 - lower) * sg
    return hq * _sigmoid(hq), 1.0 - f, jnp.log(f), f, sg


def _gla_last_mid(b, rev):
    b3 = b.reshape(GLA_GROUP, CHUNK, 128)
    last, mid = (0, CHUNK // 2) if rev else (CHUNK - 1, CHUNK // 2 - 1)
    return b3[:, last:last + 1, :], b3[:, mid:mid + 1, :]


def _gla_per_row(per_chunk):
    return jnp.broadcast_to(per_chunk, (GLA_GROUP, CHUNK, 128)).reshape(GLA_ROWS, 128)


def _gla_block_diag(x, row_masks):
    return jnp.concatenate([(x * m).astype(bf16) for m in row_masks], axis=-1)


def _gla_diag(y):
    return jnp.concatenate([y[CHUNK * c:CHUNK * (c + 1), 128 * c:128 * (c + 1)] for c in range(GLA_GROUP)], axis=0)


def _gla_rows(n, n_groups, rev):
    ne = n_groups - 1 - n if rev else n
    return pl.ds(pl.multiple_of(ne * GLA_ROWS, GLA_ROWS), GLA_ROWS), ne * GLA_GROUP


def _gla_scan_order(rev):
    return tuple(reversed(range(GLA_GROUP))) if rev else tuple(range(GLA_GROUP))


def _fwd_gla(z, lb4):
    s = z.shape[0]
    n_groups = s // GLA_ROWS
    assert n_groups % 2 == 0
    hp = GLA_HEADS_PER_STEP
    chains = [(hh, rev) for hh in range(hp) for rev in (False, True)]

    def body(hq_ref, hff_ref, hfb_ref, hi_ref, lb_ref, o_ref, st_ref):
        st_ref[...] = jnp.zeros_like(st_ref)
        masks = {rev: _gla_masks(rev) for rev in (False, True)}
        lowers = [_sigmoid(lb_ref[int(rev):int(rev) + 1, 128 * hh:128 * (hh + 1)]
                           - lb_ref[2 + int(rev):3 + int(rev), 128 * hh:128 * (hh + 1)]) for hh, rev in chains]

        def make_step(first):
            def step(n, carry):
                for ci, (hh, rev) in enumerate(chains):
                    cols = slice(128 * hh, 128 * (hh + 1))
                    rows, _ = _gla_rows(n, n_groups, rev)
                    maskf, tri, _, row_masks = masks[rev]
                    hf_ref = hfb_ref if rev else hff_ref
                    q, k, logf, _, _ = _gla_gates(hq_ref[rows, cols], hf_ref[rows, cols], lowers[ci])
                    vb = hi_ref[rows, cols].astype(bf16)
                    b = _tri_sum(tri, logf)
                    b_last3, b_mid3 = _gla_last_mid(b, rev)
                    b_last, b_mid = _gla_per_row(b_last3), _gla_per_row(b_mid3)
                    qi = (q * jnp.exp(b - b_mid)).astype(bf16)
                    ki = (k * jnp.exp(b_mid - b)).astype(bf16)
                    a = (_dot_nt(qi, ki) * maskf).astype(bf16)
                    kv = _dot_tn(vb, _gla_block_diag(k * jnp.exp(b_last - b), row_masks))
                    decay3 = jnp.exp(b_last3)
                    st = st_ref[ci]
                    before = [None] * GLA_GROUP
                    for c in _gla_scan_order(rev):
                        before[c] = st.astype(bf16)
                        st = st * decay3[c] + kv[:, 128 * c:128 * (c + 1)]
                    st_ref[ci] = st
                    inter = _dot_nt((q * jnp.exp(b)).astype(bf16), jnp.concatenate(before, axis=0))
                    o = _dot(a, vb) + _gla_diag(inter)
                    if first:
                        o_ref[rows, cols] = o
                    else:
                        o_ref[rows, cols] += o
                return carry
            return step

        lax.fori_loop(0, n_groups // 2, make_step(True), 0)
        lax.fori_loop(n_groups // 2, n_groups, make_step(False), 0)

    w = 128 * hp
    col = lambda base: pl.BlockSpec((s, w), lambda h: (0, base // w + h))
    return pl.pallas_call(
        body, name="fwd_gla", grid=(HG_HEADS // hp,),
        in_specs=[col(Z_HQ), col(Z_HFF), col(Z_HFB), col(Z_HI), pl.BlockSpec((4, w), lambda h: (0, h))],
        out_specs=pl.BlockSpec((s, w), lambda h: (0, h)),
        out_shape=jax.ShapeDtypeStruct((s, HG_HEADS * 128), f32),
        scratch_shapes=[pltpu.VMEM((len(chains), 128, 128), f32)],
        compiler_params=_cp(("parallel",), VMEM_LIMIT),
    )(z, z, z, z, lb4)


def _hg_out(o, hg, g_hgo):
    outs, ons, rs = [], [], []
    for h in range(HG_HEADS):
        oh = o[:, 128 * h:128 * (h + 1)]
        on, r = _rms_fwd(oh, g_hgo[:, 128 * h:128 * (h + 1)], 128)
        ons.append(on)
        rs.append(r)
    on = jnp.concatenate(ons, axis=-1)
    sg = _sigmoid(hg)
    return on * (hg * sg), on, rs, sg


def _fwd_mix(a, o, z, g_hgo, x, w_o, tm):
    s, d = x.shape

    def body(a_ref, o_ref, hg_ref, g_ref, x_ref, w_ref, x2_ref, cat_ref):
        r, _, _, _ = _hg_out(o_ref[...], hg_ref[...], g_ref[...])
        cat = jnp.concatenate([a_ref[...], r.astype(bf16)], axis=-1)
        cat_ref[...] = cat
        x2_ref[...] = x_ref[...] + _dot(cat, w_ref[...])

    row512 = pl.BlockSpec((tm, 512), lambda i: (i, 0))
    rowd = pl.BlockSpec((tm, d), lambda i: (i, 0))
    return pl.pallas_call(
        body, name="fwd_mix", grid=(s // tm,),
        in_specs=[row512, row512, pl.BlockSpec((tm, 512), lambda i: (i, Z_HG // 512)), _const_spec((1, 512)), rowd,
                  _const_spec((d, d))],
        out_specs=[rowd, rowd],
        out_shape=[jax.ShapeDtypeStruct((s, d), f32), jax.ShapeDtypeStruct((s, d), bf16)],
        compiler_params=_cp(("parallel",), VMEM_LIMIT),
    )(a, o, z, g_hgo, x, w_o)


def _fwd_ffn(x2, g_ffn, w_gate, w_up, w_down, tm):
    s, d = x2.shape

    def body(x_ref, g_ref, wg_ref, wu_ref, wd_ref, x3_ref, gp_ref, up_ref):
        x = x_ref[...]
        h, _ = _rms_fwd(x, g_ref[...], d)
        hb = h.astype(bf16)
        gp = _dot(hb, wg_ref[...])
        up = _dot(hb, wu_ref[...])
        gp_ref[...] = gp
        up_ref[...] = up
        act = (gp * _sigmoid(gp) * up).astype(bf16)
        x3_ref[...] = x + _dot(act, wd_ref[...])

    rowd = pl.BlockSpec((tm, d), lambda i: (i, 0))
    rowf = pl.BlockSpec((tm, D_FF), lambda i: (i, 0))
    return pl.pallas_call(
        body, name="fwd_ffn", grid=(s // tm,),
        in_specs=[rowd, _const_spec((1, d)), _const_spec((d, D_FF)), _const_spec((d, D_FF)), _const_spec((D_FF, d))],
        out_specs=[rowd, rowf, rowf],
        out_shape=[jax.ShapeDtypeStruct((s, d), f32), jax.ShapeDtypeStruct((s, D_FF), f32),
                   jax.ShapeDtypeStruct((s, D_FF), f32)],
        compiler_params=_cp(("parallel",), VMEM_LIMIT),
    )(x2, g_ffn, w_gate, w_up, w_down)


def _ple_loss_fwd_bwd(x3, g_ple, w_pg, p, w_pp, target, tm):
    s, d = x3.shape

    def body(x_ref, g_ref, wg_ref, p_ref, wp_ref, t_ref, dx_ref, h_ref, dpre_ref, dpp_ref, dg_ref, loss_ref):
        @pl.when(pl.program_id(0) == 0)
        def _():
            dg_ref[...] = jnp.zeros_like(dg_ref)
            loss_ref[...] = jnp.zeros_like(loss_ref)

        x = x_ref[...]
        g = g_ref[...]
        h, r = _rms_fwd(x, g, d)
        hb = h.astype(bf16)
        gate = _sigmoid(_dot(hb, wg_ref[...]))
        pp = _dot(p_ref[...].astype(bf16), wp_ref[...])
        e = x + gate * pp - t_ref[...]
        loss_ref[...] += 0.5 * jnp.sum(e * e) * (1.0 / d)
        dy = e * (1.0 / d)
        dpre = (dy * pp * gate * (1.0 - gate)).astype(bf16)
        dx, dgx = _rms_bwd(_dot_nt(dpre, wg_ref[...]), x, r, g, d)
        dx_ref[...] = dy + dx
        dg_ref[...] += jnp.sum(dgx, axis=0, keepdims=True)
        h_ref[...] = hb
        dpre_ref[...] = dpre
        dpp_ref[...] = (dy * gate).astype(bf16)

    rowd = pl.BlockSpec((tm, d), lambda i: (i, 0))
    return pl.pallas_call(
        body, name="ple_loss_fwd_bwd", grid=(s // tm,),
        in_specs=[rowd, _const_spec((1, d)), _const_spec((d, d)), pl.BlockSpec((tm, PLE_DIM), lambda i: (i, 0)),
                  _const_spec((PLE_DIM, d)), rowd],
        out_specs=[rowd, rowd, rowd, rowd, _acc_spec((1, d)), _acc_spec((8, 128))],
        out_shape=[jax.ShapeDtypeStruct((s, d), f32), jax.ShapeDtypeStruct((s, d), bf16), jax.ShapeDtypeStruct((s, d), bf16),
                   jax.ShapeDtypeStruct((s, d), bf16), jax.ShapeDtypeStruct((1, d), f32), jax.ShapeDtypeStruct((8, 128), f32)],
        compiler_params=_cp(("arbitrary",), VMEM_LIMIT),
    )(x3, g_ple, w_pg, p, w_pp, target)


def _bwd_ffn(d3, x2, gp, up, g_ffn, w_gate, w_up, w_down, tm):
    s, d = x2.shape

    def body(d3_ref, x_ref, gp_ref, up_ref, g_ref, wg_ref, wu_ref, wd_ref, d2_ref, h_ref, act_ref, dgp_ref, dup_ref, dg_ref):
        @pl.when(pl.program_id(0) == 0)
        def _():
            dg_ref[...] = jnp.zeros_like(dg_ref)

        x = x_ref[...]
        g = g_ref[...]
        d3 = d3_ref[...]
        h, r = _rms_fwd(x, g, d)
        h_ref[...] = h.astype(bf16)
        gp, up = gp_ref[...], up_ref[...]
        sg = _sigmoid(gp)
        silu = gp * sg
        act_ref[...] = (silu * up).astype(bf16)
        dact = _dot_nt(d3.astype(bf16), wd_ref[...])
        dgp = (dact * up * (sg * (1.0 + gp * (1.0 - sg)))).astype(bf16)
        dup = (dact * silu).astype(bf16)
        dgp_ref[...] = dgp
        dup_ref[...] = dup
        dh = _dot_nt(dgp, wg_ref[...]) + _dot_nt(dup, wu_ref[...])
        dx, dgx = _rms_bwd(dh, x, r, g, d)
        d2_ref[...] = d3 + dx
        dg_ref[...] += jnp.sum(dgx, axis=0, keepdims=True)

    rowd = pl.BlockSpec((tm, d), lambda i: (i, 0))
    rowf = pl.BlockSpec((tm, D_FF), lambda i: (i, 0))
    return pl.pallas_call(
        body, name="bwd_ffn", grid=(s // tm,),
        in_specs=[rowd, rowd, rowf, rowf, _const_spec((1, d)), _const_spec((d, D_FF)), _const_spec((d, D_FF)),
                  _const_spec((D_FF, d))],
        out_specs=[rowd, rowd, rowf, rowf, rowf, _acc_spec((1, d))],
        out_shape=[jax.ShapeDtypeStruct((s, d), f32), jax.ShapeDtypeStruct((s, d), bf16)] + [jax.ShapeDtypeStruct((s, D_FF), bf16)] * 3
        + [jax.ShapeDtypeStruct((1, d), f32)],
        compiler_params=_cp(("arbitrary",), VMEM_LIMIT),
    )(d3, x2, gp, up, g_ffn, w_gate, w_up, w_down)


def _bwd_mix(d2, w_o, o, z, g_hgo, tm):
    s, d = d2.shape

    def body(d2_ref, w_ref, o_ref, hg_ref, g_ref, da_ref, do_ref, dhg_ref, dg_ref):
        @pl.when(pl.program_id(0) == 0)
        def _():
            dg_ref[...] = jnp.zeros_like(dg_ref)

        dcat = _dot_nt(d2_ref[...].astype(bf16), w_ref[...])
        da_ref[...] = dcat[:, 0:512].astype(bf16)
        dr = dcat[:, 512:1024]
        o, hg, g = o_ref[...], hg_ref[...], g_ref[...]
        _, on, rs, sg = _hg_out(o, hg, g)
        dhg_ref[...] = (dr * on * (sg * (1.0 + hg * (1.0 - sg)))).astype(bf16)
        don = dr * (hg * sg)
        dgs = []
        for h in range(HG_HEADS):
            cols = slice(128 * h, 128 * (h + 1))
            dx, dgx = _rms_bwd(don[:, cols], o[:, cols], rs[h], g[:, cols], 128)
            do_ref[:, cols] = dx
            dgs.append(jnp.sum(dgx, axis=0, keepdims=True))
        dg_ref[...] += jnp.concatenate(dgs, axis=-1)

    row512 = pl.BlockSpec((tm, 512), lambda i: (i, 0))
    return pl.pallas_call(
        body, name="bwd_mix", grid=(s // tm,),
        in_specs=[pl.BlockSpec((tm, d), lambda i: (i, 0)), _const_spec((d, d)), row512,
                  pl.BlockSpec((tm, 512), lambda i: (i, Z_HG // 512)), _const_spec((1, 512))],
        out_specs=[row512, row512, row512, _acc_spec((1, 512))],
        out_shape=[jax.ShapeDtypeStruct((s, 512), bf16), jax.ShapeDtypeStruct((s, 512), f32), jax.ShapeDtypeStruct((s, 512), bf16),
                   jax.ShapeDtypeStruct((1, 512), f32)],
        compiler_params=_cp(("arbitrary",), VMEM_LIMIT),
    )(d2, w_o, o, z, g_hgo)


def _bwd_gla(z, lb4, do):
    s = z.shape[0]
    n_chunks = s // CHUNK
    n_groups = s // GLA_ROWS
    assert n_groups % 2 == 0

    def body(hq_ref, hff_ref, hfb_ref, hi_ref, lb_ref, do_ref, dhq_ref, dhff_ref, dhfb_ref, dhi_ref, dlb_ref,
             st_all, b_all, dst_ref, dq_acc, dv_acc, dlow_ref):
        dirs = (False, True)
        masks = [_gla_masks(rev) for rev in dirs]
        lowers = [_sigmoid(lb_ref[int(rev):int(rev) + 1, :] - lb_ref[2 + int(rev):3 + int(rev), :]) for rev in dirs]
        hf_refs, dhf_refs = (hff_ref, hfb_ref), (dhff_ref, dhfb_ref)

        def fwd_step(n, sts):
            new = []
            for d, rev in enumerate(dirs):
                rows, chunk0 = _gla_rows(n, n_groups, rev)
                _, k, logf, _, _ = _gla_gates(hq_ref[rows, :], hf_refs[d][rows, :], lowers[d])
                b = _tri_sum(masks[d][1], logf)
                b_last3, _ = _gla_last_mid(b, rev)
                b_all[d, rows, :] = b
                kv = _dot_tn(hi_ref[rows, :].astype(bf16), _gla_block_diag(k * jnp.exp(_gla_per_row(b_last3) - b), masks[d][3]))
                decay3 = jnp.exp(b_last3)
                st = sts[d]
                for c in _gla_scan_order(rev):
                    st_all[d, chunk0 + c] = st
                    st = st * decay3[c] + kv[:, 128 * c:128 * (c + 1)]
                new.append(st)
            return tuple(new)

        zero = jnp.zeros((128, 128), f32)
        lax.fori_loop(0, n_groups, fwd_step, (zero, zero))

        dst_ref[...] = jnp.zeros_like(dst_ref)
        dlow_ref[...] = jnp.zeros_like(dlow_ref)

        def make_bwd_step(first):
            def bwd_step(j, carry):
                n = n_groups - 1 - j
                for d, rev in enumerate(dirs):
                    maskf, _, tri_t, row_masks = masks[d]
                    lower = lowers[d]
                    rows, chunk0 = _gla_rows(n, n_groups, rev)
                    hq, hf = hq_ref[rows, :], hf_refs[d][rows, :]
                    q, k, _, f, sg = _gla_gates(hq, hf, lower)
                    v = hi_ref[rows, :]
                    dout = do_ref[rows, :]
                    b = b_all[d, rows, :]
                    b_last3, b_mid3 = _gla_last_mid(b, rev)
                    b_last, b_mid = _gla_per_row(b_last3), _gla_per_row(b_mid3)
                    e1, e2, e3, e4 = jnp.exp(b - b_mid), jnp.exp(b_mid - b), jnp.exp(b_last - b), jnp.exp(b)
                    decay3 = jnp.exp(b_last3)
                    qi, ki, kt, qt = q * e1, k * e2, k * e3, q * e4
                    qib, kib, ktb = qi.astype(bf16), ki.astype(bf16), kt.astype(bf16)
                    vb, dob = v.astype(bf16), dout.astype(bf16)
                    a = (_dot_nt(qib, kib) * maskf).astype(bf16)
                    da = (_dot_nt(dob, vb) * maskf).astype(bf16)
                    dqi = _dot(da, kib)
                    dki = _dot_tn(da, qib)
                    into_state = _dot_tn(dob, _gla_block_diag(qt, row_masks))
                    dst = dst_ref[d]
                    sts, dsts, ddecay = [None] * GLA_GROUP, [None] * GLA_GROUP, [None] * GLA_GROUP
                    for c in reversed(_gla_scan_order(rev)):
                        sts[c] = st_all[d, chunk0 + c]
                        dsts[c] = dst.astype(bf16)
                        ddecay[c] = jnp.sum(dst * sts[c], axis=0, keepdims=True)[None]
                        dst = dst * decay3[c] + into_state[:, 128 * c:128 * (c + 1)]
                    dst_ref[d] = dst
                    dv = _dot_tn(a, dob) + _gla_diag(_dot_nt(ktb, jnp.concatenate(dsts, axis=0)))
                    dqt = _gla_diag(_dot(dob, jnp.concatenate([x.astype(bf16) for x in sts], axis=-1)))
                    dkt = _gla_diag(_dot(vb, jnp.concatenate(dsts, axis=-1)))
                    dq = dqi * e1 + dqt * e4
                    dk = dki * e2 + dkt * e3
                    db = dqi * qi - dki * ki + dqt * qt - dkt * kt
                    dlast3 = (jnp.sum((dkt * kt).reshape(GLA_GROUP, CHUNK, 128), axis=1, keepdims=True)
                              + jnp.concatenate(ddecay, axis=0) * decay3)
                    dlogf = _tri_sum(tri_t, db) + _gla_per_row(dlast3)
                    df = dlogf / f - dk
                    dhf_refs[d][rows, :] = (df * (1.0 - lower) * sg * (1.0 - sg)).astype(bf16)
                    dlow_ref[d:d + 1, :] += jnp.sum(df * (1.0 - sg), axis=0, keepdims=True)
                    sq = _sigmoid(hq)
                    dhq = dq * (sq * (1.0 + hq * (1.0 - sq)))
                    if first:
                        dq_acc[rows, :] = dhq
                        dv_acc[rows, :] = dv
                    else:
                        dhq_ref[rows, :] = (dq_acc[rows, :] + dhq).astype(bf16)
                        dhi_ref[rows, :] = (dv_acc[rows, :] + dv).astype(bf16)
                return carry
            return bwd_step

        lax.fori_loop(0, n_groups // 2, make_bwd_step(True), 0)
        lax.fori_loop(n_groups // 2, n_groups, make_bwd_step(False), 0)

        for d in range(2):
            dl = dlow_ref[d:d + 1, :] * lowers[d] * (1.0 - lowers[d])
            dlb_ref[d:d + 1, :] = dl
            dlb_ref[2 + d:3 + d, :] = -dl

    col = lambda base: pl.BlockSpec((s, 128), lambda h: (0, base // 128 + h))
    return pl.pallas_call(
        body, name="bwd_gla", grid=(HG_HEADS,),
        in_specs=[col(Z_HQ), col(Z_HFF), col(Z_HFB), col(Z_HI), pl.BlockSpec((4, 128), lambda h: (0, h)), col(0)],
        out_specs=[col(0), col(0), col(0), col(0), pl.BlockSpec((4, 128), lambda h: (0, h))],
        out_shape=[jax.ShapeDtypeStruct((s, 512), bf16)] * 4 + [jax.ShapeDtypeStruct((4, 512), f32)],
        scratch_shapes=[pltpu.VMEM((2, n_chunks, 128, 128), f32), pltpu.VMEM((2, s, 128), f32), pltpu.VMEM((2, 128, 128), f32),
                        pltpu.VMEM((s, 128), f32), pltpu.VMEM((s, 128), f32), pltpu.VMEM((2, 128), f32)],
        compiler_params=_cp(("parallel",), VMEM_LIMIT),
    )(z, z, z, z, lb4, do)


def _bwd_attn(q, k, v, da, tq):
    hh, s, _ = q.shape

    def body(q_ref, k_ref, v_ref, do_ref, dq_ref, dk_ref, dv_ref):
        @pl.when(pl.program_id(1) == 0)
        def _():
            dk_ref[...] = jnp.zeros_like(dk_ref)
            dv_ref[...] = jnp.zeros_like(dv_ref)

        qb, kb, vb, dob = q_ref[...], k_ref[...], v_ref[...], do_ref[...]
        sc = _dot_nt(qb, kb) * ATTN_SCALE
        p = jnp.exp(sc - jnp.max(sc, axis=-1, keepdims=True))
        w = p * (1.0 / jnp.sum(p, axis=-1, keepdims=True))
        dw = _dot_nt(dob, vb)
        ds = (w * (dw - jnp.sum(dw * w, axis=-1, keepdims=True)) * ATTN_SCALE).astype(bf16)
        dq_ref[...] = _dot(ds, kb)
        dk_ref[...] += _dot_tn(ds, qb)
        dv_ref[...] += _dot_tn(w.astype(bf16), dob)

    return pl.pallas_call(
        body, name="bwd_attn", grid=(hh, s // tq),
        in_specs=[pl.BlockSpec((None, tq, QK_PAD), lambda h, i: (h, i, 0)),
                  pl.BlockSpec((None, s, QK_PAD), lambda h, i: (h, 0, 0)),
                  pl.BlockSpec((None, s, V_HEAD), lambda h, i: (h, 0, 0)),
                  pl.BlockSpec((tq, V_HEAD), lambda h, i: (i, h))],
        out_specs=[pl.BlockSpec((None, tq, QK_PAD), lambda h, i: (h, i, 0)),
                   pl.BlockSpec((None, s, QK_PAD), lambda h, i: (h, 0, 0)),
                   pl.BlockSpec((None, s, V_HEAD), lambda h, i: (h, 0, 0))],
        out_shape=[jax.ShapeDtypeStruct((hh, s, QK_PAD), f32), jax.ShapeDtypeStruct((hh, s, QK_PAD), f32),
                   jax.ShapeDtypeStruct((hh, s, V_HEAD), f32)],
        compiler_params=_cp(("parallel", "arbitrary"), VMEM_LIMIT),
    )(q, k, v, da)


def _bwd_mla_proj(z, dq, dk, dv, cosb, sina, sinb, g_qa, g_kva, wqb, wkvb, g_qn, g_kn, tm):
    s = z.shape[0]
    hh = MLA_HEADS

    def body(cq_ref, ckv_ref, kr_ref, dq_ref, dk_ref, dv_ref, c_ref, sa_ref, sb_ref, gqa_ref, gkva_ref, wqb_ref, wkvb_ref,
             gqn_ref, gkn_ref, dz_ref, cqn_ref, ckvn_ref, dq0_ref, dkv0_ref, dgqa_ref, dgkva_ref, dgqn_ref, dgkn_ref):
        @pl.when(pl.program_id(0) == 0)
        def _():
            for r in (dgqa_ref, dgkva_ref, dgqn_ref, dgkn_ref):
                r[...] = jnp.zeros_like(r)

        cq, ckv, kr = cq_ref[...], ckv_ref[...], kr_ref[...]
        gqa, gkva, gqn, gkn = gqa_ref[...], gkva_ref[...], gqn_ref[...], gkn_ref[...]
        cqn_b, rq, ckvn_b, rkv, q0, kv0 = _mla_qk_fwd(cq, ckv, kr, gqa, gkva, wqb_ref[...], wkvb_ref[...], gqn, gkn)
        cqn_ref[...] = cqn_b
        ckvn_ref[...] = ckvn_b
        c, sa, sb = c_ref[...], -sa_ref[...], -sb_ref[...]
        kr_sq = jnp.sum(kr * kr, axis=-1, keepdims=True)
        dkr = jnp.zeros_like(kr)
        dgqn = jnp.zeros((1, QK_PAD), f32)
        dgkn = jnp.zeros((1, QK_PAD), f32)
        for h in range(hh):
            qh = q0[:, QK_PAD * h:QK_PAD * (h + 1)]
            rh = lax.rsqrt(jnp.sum(qh * qh, axis=-1, keepdims=True) * (1.0 / QK_HEAD) + EPS)
            dqh = dq_ref[h]
            dqn = jnp.concatenate([dqh[:, 0:128], _rope(dqh[:, 128:256], c, sa, sb)], axis=-1)
            dq0h, dgx = _rms_bwd(dqn, qh, rh, gqn, QK_HEAD)
            dq0_ref[:, QK_PAD * h:QK_PAD * (h + 1)] = dq0h.astype(bf16)
            dgqn = dgqn + jnp.sum(dgx, axis=0, keepdims=True)

            kn_ = kv0[:, 256 * h:256 * h + 128]
            k0 = jnp.concatenate([kn_, kr], axis=-1)
            rk = lax.rsqrt((jnp.sum(kn_ * kn_, axis=-1, keepdims=True) + kr_sq) * (1.0 / QK_HEAD) + EPS)
            dkh = dk_ref[h]
            dkn = jnp.concatenate([dkh[:, 0:128], _rope(dkh[:, 128:256], c, sa, sb)], axis=-1)
            dk0, dgx = _rms_bwd(dkn, k0, rk, gkn, QK_HEAD)
            dgkn = dgkn + jnp.sum(dgx, axis=0, keepdims=True)
            dkv0_ref[:, 256 * h:256 * h + 128] = dk0[:, 0:128].astype(bf16)
            dkv0_ref[:, 256 * h + 128:256 * h + 256] = dv_ref[h].astype(bf16)
            dkr = dkr + dk0[:, 128:256]
        dgqn_ref[...] += dgqn
        dgkn_ref[...] += dgkn
        dcq, dgx = _rms_bwd(_dot_nt(dq0_ref[...], wqb_ref[...]), cq, rq, gqa, Q_LORA)
        dgqa_ref[...] += jnp.sum(dgx, axis=0, keepdims=True)
        dckv, dgx = _rms_bwd(_dot_nt(dkv0_ref[...], wkvb_ref[...]), ckv, rkv, gkva, KV_LORA)
        dgkva_ref[...] += jnp.sum(dgx, axis=0, keepdims=True)
        dz_ref[:, 0:256] = dcq.astype(bf16)
        dz_ref[:, 256:512] = dckv.astype(bf16)
        dz_ref[:, 512:640] = dkr.astype(bf16)

    row128 = pl.BlockSpec((tm, 128), lambda i: (i, 0))
    row256 = pl.BlockSpec((tm, 256), lambda i: (i, 0))
    row1024 = pl.BlockSpec((tm, 1024), lambda i: (i, 0))
    hd = lambda w: pl.BlockSpec((hh, tm, w), lambda i: (0, i, 0))
    return pl.pallas_call(
        body, name="bwd_mla_proj", grid=(s // tm,),
        in_specs=[pl.BlockSpec((tm, 256), lambda i: (i, Z_CQ // 256)), pl.BlockSpec((tm, 256), lambda i: (i, Z_CKV // 256)),
                  pl.BlockSpec((tm, 128), lambda i: (i, Z_KR // 128)), hd(QK_PAD), hd(QK_PAD), hd(V_HEAD),
                  row128, row128, row128,
                  _const_spec((1, 256)), _const_spec((1, 256)), _const_spec((256, 1024)), _const_spec((256, 1024)),
                  _const_spec((1, 256)), _const_spec((1, 256))],
        out_specs=[pl.BlockSpec((tm, 640), lambda i: (i, 0)), row256, row256, row1024, row1024,
                   _acc_spec((1, 256)), _acc_spec((1, 256)), _acc_spec((1, 256)), _acc_spec((1, 256))],
        out_shape=[jax.ShapeDtypeStruct((s, 640), bf16), jax.ShapeDtypeStruct((s, 256), bf16), jax.ShapeDtypeStruct((s, 256), bf16),
                   jax.ShapeDtypeStruct((s, 1024), bf16), jax.ShapeDtypeStruct((s, 1024), bf16)]
        + [jax.ShapeDtypeStruct((1, 256), f32)] * 4,
        compiler_params=_cp(("arbitrary",), VMEM_LIMIT),
    )(z, z, z, dq, dk, dv, cosb, sina, sinb, g_qa, g_kva, wqb, wkvb, g_qn, g_kn)


def _bwd_in(segments, wz, x, g_mix, d2, tm):
    s, d = x.shape
    n_seg = len(segments)

    def body(*refs):
        dz_refs, w_refs = refs[:n_seg], refs[n_seg:2 * n_seg]
        x_ref, g_ref, d2_ref, gx_ref, dg_ref = refs[2 * n_seg:]

        @pl.when(pl.program_id(0) == 0)
        def _():
            dg_ref[...] = jnp.zeros_like(dg_ref)

        dh = _dot_nt(dz_refs[0][...], w_refs[0][...])
        for a_ref, w_ref in zip(dz_refs[1:], w_refs[1:]):
            dh = dh + _dot_nt(a_ref[...], w_ref[...])
        x, g = x_ref[...], g_ref[...]
        r = lax.rsqrt(jnp.sum(x * x, axis=-1, keepdims=True) * (1.0 / d) + EPS)
        dx, dgx = _rms_bwd(dh, x, r, g, d)
        gx_ref[...] = d2_ref[...] + dx
        dg_ref[...] += jnp.sum(dgx, axis=0, keepdims=True)

    rowd = pl.BlockSpec((tm, d), lambda i: (i, 0))
    dz_specs = [pl.BlockSpec((tm, w), functools.partial(lambda i, j: (i, j), j=ja)) for _, w, ja, _ in segments]
    w_specs = [pl.BlockSpec((d, w), functools.partial(lambda i, j: (0, j), j=jw), pipeline_mode=pl.Buffered(1))
               for _, w, _, jw in segments]
    return pl.pallas_call(
        body, name="bwd_in", grid=(s // tm,),
        in_specs=dz_specs + w_specs + [rowd, _const_spec((1, d)), rowd],
        out_specs=[rowd, _acc_spec((1, d))],
        out_shape=[jax.ShapeDtypeStruct((s, d), f32), jax.ShapeDtypeStruct((1, d), f32)],
        compiler_params=_cp(("arbitrary",), VMEM_LIMIT),
    )(*[a for a, _, _, _ in segments], *([wz] * n_seg), x, g_mix, d2)


def _pick_tile(n, cap):
    best = None
    for t in range(LANES, cap + 1, LANES):
        if n % t == 0:
            best = t
    return best if best is not None else n


def _mm_tn(a, b, name):
    kk, m = a.shape
    _, n = b.shape
    tm = _pick_tile(m, 1408)
    tn = _pick_tile(n, 1408)
    tk = min(512, kk)

    n_k = kk // tk

    def body(a_ref, b_ref, o_ref, acc_ref):
        @pl.when(pl.program_id(2) == 0)
        def _():
            acc_ref[...] = jnp.zeros_like(acc_ref)
        acc_ref[...] += _dot_tn(a_ref[...].astype(bf16), b_ref[...].astype(bf16))

        @pl.when(pl.program_id(2) == n_k - 1)
        def _():
            o_ref[...] = acc_ref[...].astype(bf16)

    return pl.pallas_call(
        body, name=name, grid=(m // tm, n // tn, n_k),
        in_specs=[pl.BlockSpec((tk, tm), lambda i, j, k: (k, i)), pl.BlockSpec((tk, tn), lambda i, j, k: (k, j))],
        out_specs=pl.BlockSpec((tm, tn), lambda i, j, k: (i, j)),
        out_shape=jax.ShapeDtypeStruct((m, n), bf16),
        scratch_shapes=[pltpu.VMEM((tm, tn), f32)],
        compiler_params=_cp(("parallel", "parallel", "arbitrary"), VMEM_LIMIT),
    )(a, b)


def _rope_tables(positions):
    inv_freq = ROPE_THETA ** (-jnp.arange(0, QK_ROPE, 2, dtype=f32) / QK_ROPE)
    ang = positions.astype(f32)[:, None] * inv_freq
    cos, sin = jnp.cos(ang), jnp.sin(ang)
    zero = jnp.zeros_like(cos)
    return (jnp.concatenate([cos, cos, zero, zero], axis=1), jnp.concatenate([zero, sin, zero, zero], axis=1),
            jnp.concatenate([-sin, zero, zero, zero], axis=1))


def _pad256(g):
    return jnp.pad(g.reshape(1, QK_HEAD), ((0, 0), (0, QK_PAD - QK_HEAD)))


RELAYOUT_BLOCKS = 8
FIRST = ("w_in", "w_qb", "w_kvb", "lb_param")
SECOND = ("w_o", "w_gate", "w_up", "w_down", "w_ple_gate", "w_ple_proj")
ROW_SHARDED = ("w_o", "w_down", "w_ple_gate")


def _col_moves(j):
    lo = BIG["w_in"][1] * j
    w_in = [(max(lo, a) - lo, min(lo + BIG["w_in"][1], b) - lo, d + max(lo, a) - a)
            for a, b, d in Z_SEGMENTS if max(lo, a) < min(lo + BIG["w_in"][1], b)]
    head, half = divmod(j, 2)
    whole = lambda n: [(0, BIG[n][1], BIG[n][1] * j)]
    return {"w_in": w_in, "w_gate": whole("w_gate"), "w_up": whole("w_up"),
            "w_qb": [(0, 96, QK_PAD * head + 96 * half)], "w_kvb": whole("w_kvb"), "w_ple_proj": whole("w_ple_proj"),
            "lb_param": whole("lb_param")}


def _kernel_width(name):
    return {"w_in": Z_W, "w_qb": MLA_HEADS * QK_PAD}.get(name, N_DEV * BIG[name][1])


def _relayout_specs(names, by_dev):
    specs = []
    for n in names:
        rows, cols = BIG[n]
        if n == "lb_param":
            specs.append(_acc_spec((N_DEV, rows, cols) if by_dev else (rows, _kernel_width(n))))
        elif by_dev:
            specs.append(pl.BlockSpec((N_DEV, rows // RELAYOUT_BLOCKS, cols), lambda i: (0, i, 0)))
        else:
            specs.append(pl.BlockSpec((rows // RELAYOUT_BLOCKS, _kernel_width(n)), lambda i: (i, 0)))
    return specs


def _weights_in(gathered, names, name):
    n = len(names)

    def body(*refs):
        ins, outs = dict(zip(names, refs[:n])), dict(zip(names, refs[n:]))
        if "w_in" in outs:
            outs["w_in"][:, Z_KR + QK_ROPE:Z_W] = jnp.zeros((outs["w_in"].shape[0], Z_W - Z_KR - QK_ROPE), bf16)
        if "w_qb" in outs:
            for h in range(MLA_HEADS):
                outs["w_qb"][:, QK_PAD * h + QK_HEAD:QK_PAD * (h + 1)] = jnp.zeros((outs["w_qb"].shape[0], QK_PAD - QK_HEAD), bf16)
        for j in range(N_DEV):
            for wn, moves in _col_moves(j).items():
                if wn in outs:
                    for s0, s1, d0 in moves:
                        outs[wn][:, d0:d0 + s1 - s0] = ins[wn][j, :, s0:s1]

    outs = pl.pallas_call(
        body, name=name, grid=(RELAYOUT_BLOCKS,), in_specs=_relayout_specs(names, True), out_specs=_relayout_specs(names, False),
        out_shape=[jax.ShapeDtypeStruct((BIG[wn][0], _kernel_width(wn)), gathered[wn].dtype) for wn in names],
        compiler_params=_cp(("arbitrary",), VMEM_LIMIT),
    )(*[gathered[wn] for wn in names])
    return dict(zip(names, outs))


def _grads_out(sources, names, name):
    pieces = [(wn, start, arr) for wn in names for start, arr in sources[wn]]
    n_in = len(pieces)

    def body(*refs):
        outs = dict(zip(names, refs[n_in:]))

        def cols(wn, c0, c1):
            for (pn, start, arr), ref in zip(pieces, refs[:n_in]):
                if pn == wn and start <= c0 and c1 <= start + arr.shape[1]:
                    return ref[:, c0 - start:c1 - start]

        for j in range(N_DEV):
            for wn, moves in _col_moves(j).items():
                if wn in outs:
                    for s0, s1, d0 in moves:
                        outs[wn][j, :, s0:s1] = cols(wn, d0, d0 + s1 - s0).astype(bf16)

    in_specs = [_acc_spec(arr.shape) if wn == "lb_param" else pl.BlockSpec((arr.shape[0] // RELAYOUT_BLOCKS, arr.shape[1]), lambda i: (i, 0))
                for wn, _, arr in pieces]
    outs = pl.pallas_call(
        body, name=name, grid=(RELAYOUT_BLOCKS,), in_specs=in_specs, out_specs=_relayout_specs(names, True),
        out_shape=[jax.ShapeDtypeStruct((N_DEV, *BIG[wn]), bf16) for wn in names],
        compiler_params=_cp(("arbitrary",), VMEM_LIMIT),
    )(*[arr for _, _, arr in pieces])
    return dict(zip(names, outs))


def kernel(x, p, positions, g_mix, w_in, g_qa, g_kva, w_qb, w_kvb, g_qn, g_kn, lb_param, g_hgo, w_o, g_ffn, w_gate, w_up, w_down, g_ple, w_ple_gate, w_ple_proj, loss_target, m_g_mix, m_w_in, m_g_qa, m_g_kva, m_w_qb, m_w_kvb, m_g_qn, m_g_kn, m_lb_param, m_g_hgo, m_w_o, m_g_ffn, m_w_gate, m_w_up, m_w_down, m_g_ple, m_w_ple_gate, m_w_ple_proj, v_g_mix, v_w_in, v_g_qa, v_g_kva, v_w_qb, v_w_kvb, v_g_qn, v_g_kn, v_lb_param, v_g_hgo, v_w_o, v_g_ffn, v_w_gate, v_w_up, v_w_down, v_g_ple, v_w_ple_gate, v_w_ple_proj):
    w_all = dict(g_mix=g_mix, g_qa=g_qa, g_kva=g_kva, g_qn=g_qn, g_kn=g_kn, g_hgo=g_hgo, g_ffn=g_ffn, g_ple=g_ple,
                 w_in=w_in, w_qb=w_qb, w_kvb=w_kvb, w_o=w_o, w_gate=w_gate, w_up=w_up, w_down=w_down,
                 w_ple_gate=w_ple_gate, w_ple_proj=w_ple_proj, lb_param=lb_param)
    m_all = dict(g_mix=m_g_mix, g_qa=m_g_qa, g_kva=m_g_kva, g_qn=m_g_qn, g_kn=m_g_kn, g_hgo=m_g_hgo, g_ffn=m_g_ffn,
                 g_ple=m_g_ple, w_in=m_w_in, w_qb=m_w_qb, w_kvb=m_w_kvb, w_o=m_w_o, w_gate=m_w_gate, w_up=m_w_up,
                 w_down=m_w_down, w_ple_gate=m_w_ple_gate, w_ple_proj=m_w_ple_proj, lb_param=m_lb_param)
    v_all = dict(g_mix=v_g_mix, g_qa=v_g_qa, g_kva=v_g_kva, g_qn=v_g_qn, g_kn=v_g_kn, g_hgo=v_g_hgo, g_ffn=v_g_ffn,
                 g_ple=v_g_ple, w_in=v_w_in, w_qb=v_w_qb, w_kvb=v_w_kvb, w_o=v_w_o, w_gate=v_w_gate, w_up=v_w_up,
                 w_down=v_w_down, w_ple_gate=v_w_ple_gate, w_ple_proj=v_w_ple_proj, lb_param=v_lb_param)
    me_idx = jnp.stack([_me()]).astype(jnp.int32)
    x, p, positions, target = x[0], p[0, 0], positions[0], loss_target[0]
    s = x.shape[0]
    tm, tm_ffn, tq_f, tq_b = min(256, s), min(128, s), min(512, s), min(256, s)
    g_mix, g_qa, g_kva, g_qn, g_kn, g_hgo, g_ffn, g_ple = (w_all[n].reshape(1, -1) for n in SMALL)
    g_qn_p, g_kn_p = _pad256(g_qn), _pad256(g_kn)
    cosb, sina, sinb = _rope_tables(positions)
    shard = lambda n: w_all[n].reshape(BIG[n])

    first = _all_gather([shard(n) for n in FIRST], [f32 if n == "lb_param" else bf16 for n in FIRST], "ag_first")
    lands = _cast_to_slot([shard(n) for n in SECOND], me_idx, first[0])
    ag2, token = _exchange_start([], lands, "ag_second_start")
    wk = _weights_in(dict(zip(FIRST, first)), FIRST, "weights_in_first")
    wz, wqb, wkvb, lb4 = (wk[n] for n in FIRST)

    h1, z = _fwd_in(x, g_mix, wz, tm)
    q, k, v = _fwd_mla_proj(z, cosb + token[0, 0], sina, sinb, g_qa, g_kva, wqb, wkvb, g_qn_p, g_kn_p, tm)
    a = _fwd_attn(q, k, v, tq_f)
    o = _fwd_gla(z, lb4)

    second = dict(zip(SECOND, _exchange_wait(ag2, o, "ag_second_wait")))
    wk = _weights_in(second, ("w_gate", "w_up", "w_ple_proj"), "weights_in_second")
    w_gate, w_up, w_pp = wk["w_gate"], wk["w_up"], wk["w_ple_proj"]
    w_o, w_down, w_pg = (second[n].reshape(N_DEV * BIG[n][0], BIG[n][1]) for n in ROW_SHARDED)

    x2, cat = _fwd_mix(a, o, z, g_hgo, x, w_o, tm)
    x3, gp, up = _fwd_ffn(x2, g_ffn, w_gate, w_up, w_down, tm)
    d3, h3, dpre, dpp, dg_ple, loss_tile = _ple_loss_fwd_bwd(x3, g_ple, w_pg, p, w_pp, target, tm)
    d2, h2, act, dgp, dup, dg_ffn = _bwd_ffn(d3, x2, gp, up, g_ffn, w_gate, w_up, w_down, tm_ffn)

    blocks = _grads_out({"w_gate": [(0, _mm_tn(h2, dgp, "dw_gate"))], "w_up": [(0, _mm_tn(h2, dup, "dw_up"))],
                         "w_ple_proj": [(0, _mm_tn(p, dpp, "dw_ple_proj"))]}, ("w_gate", "w_up", "w_ple_proj"), "grads_out_second")
    row_grads = {"w_o": _mm_tn(cat, d2, "dw_o"), "w_down": _mm_tn(act, d3, "dw_down"), "w_ple_gate": _mm_tn(h3, dpre, "dw_ple_gate")}
    blocks.update({n: g.reshape(N_DEV, *BIG[n]) for n, g in row_grads.items()})
    empty = lambda names: [lax.empty((N_PEERS, *BIG[n]), bf16) for n in names]
    rs2, token = _exchange_start([blocks[n] for n in SECOND], empty(SECOND), "rs_second_start")

    da, do, dz_hg, dg_hgo = _bwd_mix(d2, w_o, o, z, g_hgo + token[0, 0], tm)
    dz_hq, dz_hff, dz_hfb, dz_hi, dlb4 = _bwd_gla(z, lb4, do)
    dq, dk, dv = _bwd_attn(q, k, v, da, tq_b)
    dz_mla, cqn, ckvn, dq0, dkv0, dg_qa, dg_kva, dg_qn, dg_kn = _bwd_mla_proj(
        z, dq, dk, dv, cosb, sina, sinb, g_qa, g_kva, wqb, wkvb, g_qn_p, g_kn_p, tm)

    gz = [(Z_HQ, _mm_tn(h1, dz_hq, "dw_in_hq")), (Z_HFF, _mm_tn(h1, dz_hff, "dw_in_hff")),
          (Z_HFB, _mm_tn(h1, dz_hfb, "dw_in_hfb")), (Z_HI, _mm_tn(h1, dz_hi, "dw_in_hi")),
          (Z_HG, _mm_tn(h1, dz_hg, "dw_in_hg")), (Z_CQ, _mm_tn(h1, dz_mla, "dw_in_mla"))]
    blocks1 = _grads_out({"w_in": gz, "w_qb": [(0, _mm_tn(cqn, dq0, "dw_qb"))], "w_kvb": [(0, _mm_tn(ckvn, dkv0, "dw_kvb"))],
                          "lb_param": [(0, dlb4)]}, FIRST, "grads_out_first")
    rs1, token = _exchange_start([blocks1[n] for n in FIRST], empty(FIRST), "rs_first_start")

    result = {}

    def adam(names, lands, src, n_blocks):
        outs = _adam_shards(me_idx, [src[n] for n in names], lands, [w_all[n] for n in names], [m_all[n] for n in names],
                            [v_all[n] for n in names], n_blocks, "adamw_" + names[0])
        result.update(zip(names, outs))

    lands2 = dict(zip(SECOND, _exchange_wait(rs2, dz_mla, "rs_second_wait")))
    by8 = tuple(n for n in SECOND if n != "w_down")
    adam(by8, [lands2[n] for n in by8], blocks, 8)
    adam(("w_down",), [lands2["w_down"]], blocks, 2)

    segments = [(dz_hq, 512, 0, Z_HQ // 512), (dz_hff, 512, 0, Z_HFF // 512), (dz_hfb, 512, 0, Z_HFB // 512),
                (dz_hi, 512, 0, Z_HI // 512), (dz_hg, 512, 0, Z_HG // 512), (dz_mla, 640, 0, Z_CQ // 640)]
    grad_x, dg_mix = _bwd_in(segments, wz, x, g_mix + token[0, 0], d2, tm)
    dgains = (dg_mix, dg_qa, dg_kva, dg_qn, dg_kn, dg_hgo, dg_ffn, dg_ple)

    lands1 = _exchange_wait(rs1, grad_x, "rs_first_wait")
    adam(FIRST, lands1, blocks1, 8)

    vec = jnp.concatenate(list(dgains) + [loss_tile[0:1]], axis=1)
    parts = _all_gather([vec], [f32], "ag_gains")[0]
    outs, loss_row = _adam_gains(parts, [w_all[n] for n in SMALL], [m_all[n] for n in SMALL], [v_all[n] for n in SMALL])
    result.update(zip(SMALL, outs))

    order = ("g_mix", "w_in", "g_qa", "g_kva", "w_qb", "w_kvb", "g_qn", "g_kn", "lb_param", "g_hgo", "w_o", "g_ffn",
             "w_gate", "w_up", "w_down", "g_ple", "w_ple_gate", "w_ple_proj")
    return (loss_row[0, 0], grad_x[None], *[result[n][k] for k in range(4) for n in order])
```

```python
import functools
import math

import jax
import jax.numpy as jnp
from jax import lax
from jax.experimental import pallas as pl
from jax.experimental.pallas import tpu as pltpu

f32 = jnp.float32
bf16 = jnp.bfloat16

N_DEV = 8
D_MODEL = 1024
MLA_HEADS = 4
QK_NOPE = 128
QK_ROPE = 64
QK_HEAD = QK_NOPE + QK_ROPE
QK_PAD = 256
V_HEAD = 128
Q_LORA = 256
KV_LORA = 256
HG_HEADS = 4
HG_DK = 128
CHUNK = 64
D_FF = 2816
PLE_DIM = 256
ROPE_THETA = 10000.0
EPS = 1e-6
ATTN_SCALE = QK_HEAD ** -0.5
LOG2_E = math.log2(math.e)
ATTN_SUB_ROWS = 256
IN_SIZES = (256, 256, 64, 512, 512, 512, 512, 512)
D_IN = sum(IN_SIZES)
Z_HQ, Z_HFF, Z_HFB, Z_HI, Z_HG, Z_CQ, Z_CKV, Z_KR, Z_W = 0, 512, 1024, 1536, 2048, 2560, 2816, 3072, 3200

ADAM_LR, ADAM_B1, ADAM_B2, ADAM_EPS, ADAM_WD, ADAM_STEP = 0.001, 0.9, 0.999, 1e-08, 0.01, 10

LANES = 128
BIG = {"w_in": (1024, 392), "w_qb": (256, 96), "w_kvb": (256, 128), "w_o": (128, 1024), "w_gate": (1024, 352),
       "w_up": (1024, 352), "w_down": (352, 1024), "w_ple_gate": (128, 1024), "w_ple_proj": (256, 128),
       "lb_param": (4, 64)}
ROW_BLOCKS = {("w_in", "w_qb", "w_kvb", "w_o", "w_gate", "w_up", "w_ple_gate", "w_ple_proj"): 8, ("w_down", "lb_param"): 2}
SMALL = {"g_mix": (0, 1024), "g_qa": (1024, 256), "g_kva": (1280, 256), "g_qn": (1536, 192), "g_kn": (1792, 192),
         "g_hgo": (2048, 512), "g_ffn": (2560, 1024), "g_ple": (3584, 1024)}
LOSS_OFF = 4608
GAIN_VEC = LOSS_OFF + LANES
Z_SEGMENTS = ((0, 256, Z_CQ), (256, 512, Z_CKV), (512, 576, Z_KR), (576, 1088, Z_HQ), (1088, 1600, Z_HFF),
              (1600, 2112, Z_HFB), (2112, 2624, Z_HI), (2624, 3136, Z_HG))

VMEM_LIMIT = 56 * 1024 * 1024
MESH = pl.DeviceIdType.MESH


def _cp(sem=None, vmem=None):
    return pltpu.CompilerParams(dimension_semantics=sem, vmem_limit_bytes=vmem)


def _const_spec(shape):
    nd = len(shape)
    return pl.BlockSpec(shape, lambda *_: (0,) * nd, pipeline_mode=pl.Buffered(1))


def _acc_spec(shape):
    nd = len(shape)
    return pl.BlockSpec(shape, lambda *_: (0,) * nd)


def _sigmoid(x):
    return jax.nn.sigmoid(x)


def _dot(a, b):
    return jnp.dot(a, b, preferred_element_type=f32)


def _dot_nt(a, b):
    return lax.dot_general(a, b, (((1,), (1,)), ((), ())), preferred_element_type=f32)


def _dot_tn(a, b):
    return lax.dot_general(a, b, (((0,), (0,)), ((), ())), preferred_element_type=f32)


def _rms_fwd(x, g, width):
    r = lax.rsqrt(jnp.sum(x * x, axis=-1, keepdims=True) * (1.0 / width) + EPS)
    return x * r * g, r


def _rms_bwd(dy, x, r, g, width):
    u = dy * g
    dx = r * u - x * (r * r * r) * (jnp.sum(u * x, axis=-1, keepdims=True) * (1.0 / width))
    return dx, dy * x * r


def _rope(b, c, sa, sb):
    return b * c + pltpu.roll(b, 32, 1) * sa + pltpu.roll(b, 96, 1) * sb


def _all_gather(shards, dtypes, name):
    n = len(shards)

    def body(*refs):
        in_refs, out_refs, stage = refs[:n], refs[n:2 * n], refs[2 * n:3 * n]
        send_sems, recv_sems, local_sems = refs[3 * n:]
        for w in range(n):
            stage[w][...] = in_refs[w][...].astype(stage[w].dtype)
        x, y, c = lax.axis_index("x"), lax.axis_index("y"), lax.axis_index("c")
        me, sibling = (x, y, c), (x, y, 1 - c)
        chips = [(1 - x, y), (x, 1 - y), (1 - x, 1 - y)]

        def slot(w, px, py, pc):
            return out_refs[w].at[4 * px + 2 * py + pc]

        def copy(w, k, block, to, src=None):
            return pltpu.make_async_remote_copy(
                src_ref=slot(w, *block) if src is None else src, dst_ref=slot(w, *block),
                send_sem=send_sems.at[w, k], recv_sem=recv_sems.at[w, k], device_id=to, device_id_type=MESH)

        first = []
        for j, chip in enumerate(chips):
            first += [copy(w, 1 + j, me, (*chip, c), src=stage[w]) for w in range(n)]
        first += [copy(w, 0, me, sibling, src=stage[w]) for w in range(n)]
        mine = [pltpu.make_async_copy(stage[w], slot(w, *me), local_sems.at[w]) for w in range(n)]
        for cp in first + mine:
            cp.start()
        passed = []
        for j, chip in enumerate(chips):
            for w in range(n):
                copy(w, 1 + j, (*chip, c), me).wait_recv()
                passed.append(copy(w, 4 + j, (*chip, c), sibling))
                passed[-1].start()
        for w in range(n):
            copy(w, 0, sibling, me).wait_recv()
        for j, chip in enumerate(chips):
            for w in range(n):
                copy(w, 4 + j, (*chip, 1 - c), me).wait_recv()
        for cp in first + passed:
            cp.wait_send()
        for cp in mine:
            cp.wait()

    return pl.pallas_call(
        body, name=name,
        out_shape=[jax.ShapeDtypeStruct((N_DEV, *s.shape), dt) for s, dt in zip(shards, dtypes)],
        in_specs=[pl.BlockSpec(memory_space=pltpu.VMEM)] * n,
        out_specs=[pl.BlockSpec(memory_space=pl.ANY)] * n,
        scratch_shapes=[pltpu.VMEM(s.shape, dt) for s, dt in zip(shards, dtypes)]
        + [pltpu.SemaphoreType.DMA((n, 7)), pltpu.SemaphoreType.DMA((n, 7)), pltpu.SemaphoreType.DMA((n,))],
        compiler_params=_cp(None, VMEM_LIMIT),
    )(*shards)


N_PEERS = N_DEV - 1
HBM_SPEC = pl.BlockSpec(memory_space=pltpu.HBM)
SEM_SPEC = pl.BlockSpec(memory_space=pltpu.SEMAPHORE)
DATAFLOW = pltpu.SideEffectType.DATAFLOW_SIDE_EFFECTING


def _me():
    return 4 * lax.axis_index("x") + 2 * lax.axis_index("y") + lax.axis_index("c")


def _peer(k):
    x, y, c = lax.axis_index("x"), lax.axis_index("y"), lax.axis_index("c")
    px = 1 - x if k & 4 else x
    py = 1 - y if k & 2 else y
    pc = 1 - c if k & 1 else c
    return (px, py, pc), 4 * px + 2 * py + pc


def _exchange_copies(src_refs, land_refs, send_sems, recv_sems, gather):
    cps = []
    me = _me()
    for k in range(1, N_DEV):
        peer, peer_idx = _peer(k)
        for w, land in enumerate(land_refs):
            src = land.at[me] if gather else src_refs[w].at[peer_idx]
            dst = land.at[me] if gather else land.at[k - 1]
            cps.append(pltpu.make_async_remote_copy(
                src_ref=src, dst_ref=dst, send_sem=send_sems.at[N_PEERS * w + k - 1], recv_sem=recv_sems.at[N_PEERS * w + k - 1],
                device_id=peer, device_id_type=MESH))
    return cps


def _exchange_start(srcs, lands, name):
    n_src, n = len(srcs), len(lands)

    def body(*refs):
        src_refs, land_refs = refs[:n_src], refs[n_src:n_src + n]
        send_sems, recv_sems = refs[n_src + n], refs[n_src + n + 1]
        token = refs[-1]
        for cp in _exchange_copies(src_refs, land_refs, send_sems, recv_sems, gather=not n_src):
            cp.start()
        token[...] = jnp.zeros_like(token)

    arrays = [pltpu.with_memory_space_constraint(a, pltpu.HBM) for a in (*srcs, *lands)]
    outs = pl.pallas_call(
        body, name=name,
        out_shape=(pltpu.SemaphoreType.DMA((n * N_PEERS,)), pltpu.SemaphoreType.DMA((n * N_PEERS,)),
                   *[pltpu.HBM(a.shape, a.dtype) for a in arrays], jax.ShapeDtypeStruct((8, LANES), f32)),
        in_specs=[HBM_SPEC] * len(arrays),
        out_specs=(SEM_SPEC, SEM_SPEC, *[HBM_SPEC] * len(arrays), pl.BlockSpec(memory_space=pltpu.VMEM)),
        input_output_aliases={i: 2 + i for i in range(len(arrays))},
        compiler_params=pltpu.CompilerParams(has_side_effects=DATAFLOW),
    )(*arrays)
    return (outs[0], outs[1], outs[2:2 + n_src], outs[2 + n_src:2 + n_src + n]), outs[-1]


def _exchange_wait(state, after, name):
    send_sems, recv_sems, srcs, lands = state
    n_src, n = len(srcs), len(lands)

    def body(*refs):
        src_refs, land_refs = refs[:n_src], refs[n_src:n_src + n]
        send_ref, recv_ref = refs[n_src + n], refs[n_src + n + 1]
        for cp in _exchange_copies(src_refs, land_refs, send_ref, recv_ref, gather=not n_src):
            cp.wait_send()
            cp.wait_recv()

    arrays = (*srcs, *lands)
    outs = pl.pallas_call(
        body, name=name,
        out_shape=tuple(pltpu.HBM(a.shape, a.dtype) for a in arrays),
        in_specs=[HBM_SPEC] * len(arrays) + [SEM_SPEC, SEM_SPEC, pl.BlockSpec(memory_space=pl.ANY)],
        out_specs=tuple([HBM_SPEC] * len(arrays)),
        input_output_aliases={i: i for i in range(len(arrays))},
        compiler_params=pltpu.CompilerParams(has_side_effects=DATAFLOW),
    )(*arrays, send_sems, recv_sems, after)
    return outs[n_src:]


def _cast_to_slot(shards, me_idx, after):
    n = len(shards)

    def body(i_ref, *refs):
        for w in range(n):
            refs[n + 1 + w][...] = refs[w][...].astype(bf16)

    return pl.pallas_call(
        body, name="cast_to_slot",
        grid_spec=pltpu.PrefetchScalarGridSpec(
            num_scalar_prefetch=1, grid=(1,),
            in_specs=[pl.BlockSpec(s.shape, lambda i, m: (0, 0)) for s in shards] + [pl.BlockSpec(memory_space=pl.ANY)],
            out_specs=[pl.BlockSpec((None, *s.shape), lambda i, m: (m[0], 0, 0)) for s in shards]),
        out_shape=[jax.ShapeDtypeStruct((N_DEV, *s.shape), bf16) for s in shards],
        compiler_params=_cp(("arbitrary",), VMEM_LIMIT),
    )(me_idx, *shards, after)


def _row_block(rows, n_blocks):
    return (rows // n_blocks, True) if rows % (16 * n_blocks) == 0 else (rows, False)


def _adam_math(w, g, m, v):
    m = ADAM_B1 * m + (1.0 - ADAM_B1) * g
    v = ADAM_B2 * v + (1.0 - ADAM_B2) * (g * g)
    m_hat = m / (1.0 - ADAM_B1 ** ADAM_STEP)
    v_hat = v / (1.0 - ADAM_B2 ** ADAM_STEP)
    delta = -ADAM_LR * (m_hat / (jnp.sqrt(v_hat) + ADAM_EPS) + ADAM_WD * w)
    return delta, m, v


def _adam_shards(me_idx, blocks, lands, ws, ms, vs, n_blocks, name):
    n = len(blocks)

    def body(i_ref, *refs):
        ins, outs = refs[:5 * n], refs[5 * n:]
        for w in range(n):
            g_ref, b_ref, w_ref, m_ref, v_ref = (ins[t * n + w] for t in range(5))
            g = g_ref[...].astype(f32)
            for k in range(N_PEERS):
                g = g + b_ref[k].astype(f32)
            if len(w_ref.shape) == 2:
                pieces = [(slice(None), g)]
            else:
                pieces = [(a, g[2 * a:2 * a + 2]) for a in range(2)]
            for at, gp in pieces:
                vals = (gp,) + _adam_math(w_ref[at], gp, m_ref[at], v_ref[at])
                for t, val in enumerate(vals):
                    outs[4 * w + t][at] = val

    specs = [[] for _ in range(5)]
    out_specs, out_shape = [], []
    for g, wt in zip(blocks, ws):
        rows, cols = g.shape[1:]
        rb, cut = _row_block(rows, n_blocks)
        specs[0].append(pl.BlockSpec((None, rb, cols), functools.partial(lambda i, s, cut: (s[0], i if cut else 0, 0), cut=cut)))
        specs[1].append(pl.BlockSpec((N_PEERS, rb, cols), functools.partial(lambda i, s, cut: (0, i if cut else 0, 0), cut=cut)))
        if wt.shape[0] == 1:
            shard = pl.BlockSpec((None, rb, cols), functools.partial(lambda i, s, cut: (0, i if cut else 0, 0), cut=cut))
        else:
            shard = pl.BlockSpec(wt.shape, functools.partial(lambda i, s, nd: (0,) * nd, nd=wt.ndim))
        for t in (2, 3, 4):
            specs[t].append(shard)
        out_specs += [shard] * 4
        out_shape += [jax.ShapeDtypeStruct(wt.shape, f32)] * 4
    outs = pl.pallas_call(
        body, name=name,
        grid_spec=pltpu.PrefetchScalarGridSpec(num_scalar_prefetch=1, grid=(n_blocks,), in_specs=sum(specs, []), out_specs=out_specs),
        out_shape=out_shape,
        compiler_params=_cp(("arbitrary",), VMEM_LIMIT),
    )(me_idx, *blocks, *lands, *ws, *ms, *vs)
    return [outs[4 * w:4 * w + 4] for w in range(n)]


def _adam_gains(parts, ws, ms, vs):
    n = len(ws)

    def body(p_ref, *refs):
        ins, outs = refs[:3 * n], refs[3 * n:]
        g_all = p_ref[0]
        for k in range(1, N_DEV):
            g_all = g_all + p_ref[k]
        for w, (off, lanes) in enumerate(SMALL.values()):
            w_ref, m_ref, v_ref = ins[w], ins[n + w], ins[2 * n + w]
            if len(w_ref.shape) == 2:
                pieces = [(slice(None), off, lanes)]
            else:
                pieces = [((slice(None), h), off + LANES * h, LANES) for h in range(w_ref.shape[1])]
            for at, o, ln in pieces:
                g = g_all[:, o:o + ln]
                vals = (g,) + _adam_math(w_ref[at], g, m_ref[at], v_ref[at])
                for t, val in enumerate(vals):
                    outs[4 * w + t][at] = val
        outs[4 * n][...] = g_all[:, LOSS_OFF:LOSS_OFF + LANES]

    out_shape = sum([[jax.ShapeDtypeStruct(w.shape, f32)] * 4 for w in ws], []) + [jax.ShapeDtypeStruct((1, LANES), f32)]
    outs = pl.pallas_call(body, name="adamw_gains", out_shape=out_shape)(parts, *ws, *ms, *vs)
    return [outs[4 * w:4 * w + 4] for w in range(n)], outs[4 * n]


def _fwd_in(x, g_mix, wz, tm):
    s, d = x.shape

    def body(x_ref, g_ref, w_ref, h_ref, z_ref):
        h, _ = _rms_fwd(x_ref[...], g_ref[...], d)
        hb = h.astype(bf16)
        h_ref[...] = hb
        z_ref[...] = _dot(hb, w_ref[...])

    return pl.pallas_call(
        body, name="fwd_in", grid=(s // tm,),
        in_specs=[pl.BlockSpec((tm, d), lambda i: (i, 0)), _const_spec((1, d)), _const_spec((d, Z_W))],
        out_specs=[pl.BlockSpec((tm, d), lambda i: (i, 0)), pl.BlockSpec((tm, Z_W), lambda i: (i, 0))],
        out_shape=[jax.ShapeDtypeStruct((s, d), bf16), jax.ShapeDtypeStruct((s, Z_W), f32)],
        compiler_params=_cp(("parallel",), VMEM_LIMIT),
    )(x, g_mix, wz)


def _mla_qk_fwd(cq, ckv, kr, g_qa, g_kva, wqb, wkvb, g_qn, g_kn):
    cqn, rq = _rms_fwd(cq, g_qa, Q_LORA)
    ckvn, rkv = _rms_fwd(ckv, g_kva, KV_LORA)
    cqn_b, ckvn_b = cqn.astype(bf16), ckvn.astype(bf16)
    q0 = _dot(cqn_b, wqb)
    kv0 = _dot(ckvn_b, wkvb)
    return cqn_b, rq, ckvn_b, rkv, q0, kv0


def _fwd_mla_proj(z, cosb, sina, sinb, g_qa, g_kva, wqb, wkvb, g_qn, g_kn, tm):
    s = z.shape[0]
    hh = MLA_HEADS

    def body(cq_ref, ckv_ref, kr_ref, c_ref, sa_ref, sb_ref, gqa_ref, gkva_ref, wqb_ref, wkvb_ref, gqn_ref, gkn_ref,
             q_ref, k_ref, v_ref):
        _, _, _, _, q0, kv0 = _mla_qk_fwd(cq_ref[...], ckv_ref[...], kr_ref[...], gqa_ref[...], gkva_ref[...],
                                          wqb_ref[...], wkvb_ref[...], gqn_ref[...], gkn_ref[...])
        kr = kr_ref[...]
        c, sa, sb = c_ref[...], sa_ref[...], sb_ref[...]
        gqn, gkn = gqn_ref[...], gkn_ref[...]
        kr_sq = jnp.sum(kr * kr, axis=-1, keepdims=True)
        for h in range(hh):
            qh = q0[:, QK_PAD * h:QK_PAD * (h + 1)]
            qn, _ = _rms_fwd(qh, gqn, QK_HEAD)
            q_ref[h, :, 0:128] = qn[:, 0:128].astype(bf16)
            q_ref[h, :, 128:256] = _rope(qn[:, 128:256], c, sa, sb).astype(bf16)
            kn_ = kv0[:, 256 * h:256 * h + 128]
            rk = lax.rsqrt((jnp.sum(kn_ * kn_, axis=-1, keepdims=True) + kr_sq) * (1.0 / QK_HEAD) + EPS)
            k_ref[h, :, 0:128] = (kn_ * rk * gkn[:, 0:128]).astype(bf16)
            k_ref[h, :, 128:256] = _rope(kr * rk * gkn[:, 128:256], c, sa, sb).astype(bf16)
            v_ref[h] = kv0[:, 256 * h + 128:256 * h + 256].astype(bf16)

    row128 = pl.BlockSpec((tm, 128), lambda i: (i, 0))
    return pl.pallas_call(
        body, name="fwd_mla_proj", grid=(s // tm,),
        in_specs=[pl.BlockSpec((tm, 256), lambda i: (i, Z_CQ // 256)), pl.BlockSpec((tm, 256), lambda i: (i, Z_CKV // 256)),
                  pl.BlockSpec((tm, 128), lambda i: (i, Z_KR // 128)), row128, row128, row128,
                  _const_spec((1, 256)), _const_spec((1, 256)), _const_spec((256, 1024)), _const_spec((256, 1024)),
                  _const_spec((1, 256)), _const_spec((1, 256))],
        out_specs=[pl.BlockSpec((hh, tm, QK_PAD), lambda i: (0, i, 0)), pl.BlockSpec((hh, tm, QK_PAD), lambda i: (0, i, 0)),
                   pl.BlockSpec((hh, tm, V_HEAD), lambda i: (0, i, 0))],
        out_shape=[jax.ShapeDtypeStruct((hh, s, QK_PAD), bf16), jax.ShapeDtypeStruct((hh, s, QK_PAD), bf16),
                   jax.ShapeDtypeStruct((hh, s, V_HEAD), bf16)],
        compiler_params=_cp(("parallel",), VMEM_LIMIT),
    )(z, z, z, cosb, sina, sinb, g_qa, g_kva, wqb, wkvb, g_qn, g_kn)


def _fwd_attn(q, k, v, tq):
    hh, s, _ = q.shape

    n_sub = max(1, tq // ATTN_SUB_ROWS)

    def body(q_ref, k_ref, v_ref, o_ref):
        for t in range(n_sub):
            rows = slice(t * (tq // n_sub), (t + 1) * (tq // n_sub))
            sc = _dot_nt(q_ref[rows, :], k_ref[...])
            p = jnp.exp2((sc - jnp.max(sc, axis=-1, keepdims=True)) * (ATTN_SCALE * LOG2_E))
            l = jnp.sum(p, axis=-1, keepdims=True)
            o_ref[rows, :] = (_dot(p.astype(bf16), v_ref[...]) * (1.0 / l)).astype(bf16)

    return pl.pallas_call(
        body, name="fwd_attn", grid=(hh, s // tq),
        in_specs=[pl.BlockSpec((None, tq, QK_PAD), lambda h, i: (h, i, 0)),
                  pl.BlockSpec((None, s, QK_PAD), lambda h, i: (h, 0, 0)),
                  pl.BlockSpec((None, s, V_HEAD), lambda h, i: (h, 0, 0))],
        out_specs=pl.BlockSpec((tq, V_HEAD), lambda h, i: (i, h)),
        out_shape=jax.ShapeDtypeStruct((s, hh * V_HEAD), bf16),
        compiler_params=_cp(("parallel", "parallel"), VMEM_LIMIT),
    )(q, k, v)


def _split3(x):
    hi = x.astype(bf16)
    r1 = x - hi.astype(f32)
    mid = r1.astype(bf16)
    lo = (r1 - mid.astype(f32)).astype(bf16)
    return jnp.concatenate([hi, mid, lo], axis=-1)


def _tri_sum(tri, x):
    y = _dot(tri, _split3(x))
    return y[:, 0:128] + y[:, 128:256] + y[:, 256:384]


GLA_GROUP = 4
GLA_ROWS = GLA_GROUP * CHUNK
GLA_HEADS_PER_STEP = 2


def _gla_masks(rev):
    row = lax.broadcasted_iota(jnp.int32, (GLA_ROWS, GLA_ROWS), 0)
    col = lax.broadcasted_iota(jnp.int32, (GLA_ROWS, GLA_ROWS), 1)
    shift = CHUNK.bit_length() - 1
    same = (jnp.right_shift(row, shift) == jnp.right_shift(col, shift)).astype(f32)
    lower, upper = (row >= col).astype(f32) * same, (row <= col).astype(f32) * same
    keep, keep_t = (upper, lower) if rev else (lower, upper)
    chunk_of = jnp.right_shift(lax.broadcasted_iota(jnp.int32, (GLA_ROWS, 1), 0), shift)
    return keep, keep.astype(bf16), keep_t.astype(bf16), [(chunk_of == c).astype(f32) for c in range(GLA_GROUP)]


def _gla_gates(hq, hf, lower):
    sg = _sigmoid(hf)
    f = lower + (1.0 - lower) * sg
    return hq * _sigmoid(hq), 1.0 - f, jnp.log(f), f, sg


def _gla_last_mid(b, rev):
    b3 = b.reshape(GLA_GROUP, CHUNK, 128)
    last, mid = (0, CHUNK // 2) if rev else (CHUNK - 1, CHUNK // 2 - 1)
    return b3[:, last:last + 1, :], b3[:, mid:mid + 1, :]


def _gla_per_row(per_chunk):
    return jnp.broadcast_to(per_chunk, (GLA_GROUP, CHUNK, 128)).reshape(GLA_ROWS, 128)


def _gla_block_diag(x, row_masks):
    return jnp.concatenate([(x * m).astype(bf16) for m in row_masks], axis=-1)


def _gla_diag(y):
    return jnp.concatenate([y[CHUNK * c:CHUNK * (c + 1), 128 * c:128 * (c + 1)] for c in range(GLA_GROUP)], axis=0)


def _gla_rows(n, n_groups, rev):
    ne = n_groups - 1 - n if rev else n
    return pl.ds(pl.multiple_of(ne * GLA_ROWS, GLA_ROWS), GLA_ROWS), ne * GLA_GROUP


def _gla_scan_order(rev):
    return tuple(reversed(range(GLA_GROUP))) if rev else tuple(range(GLA_GROUP))


def _fwd_gla(z, lb4):
    s = z.shape[0]
    n_groups = s // GLA_ROWS
    assert n_groups % 2 == 0
    hp = GLA_HEADS_PER_STEP
    chains = [(hh, rev) for hh in range(hp) for rev in (False, True)]

    def body(hq_ref, hff_ref, hfb_ref, hi_ref, lb_ref, o_ref, st_ref):
        st_ref[...] = jnp.zeros_like(st_ref)
        masks = {rev: _gla_masks(rev) for rev in (False, True)}
        lowers = [_sigmoid(lb_ref[int(rev):int(rev) + 1, 128 * hh:128 * (hh + 1)]
                           - lb_ref[2 + int(rev):3 + int(rev), 128 * hh:128 * (hh + 1)]) for hh, rev in chains]

        def make_step(first):
            def step(n, carry):
                for ci, (hh, rev) in enumerate(chains):
                    cols = slice(128 * hh, 128 * (hh + 1))
                    rows, _ = _gla_rows(n, n_groups, rev)
                    maskf, tri, _, row_masks = masks[rev]
                    hf_ref = hfb_ref if rev else hff_ref
                    q, k, logf, _, _ = _gla_gates(hq_ref[rows, cols], hf_ref[rows, cols], lowers[ci])
                    vb = hi_ref[rows, cols].astype(bf16)
                    b = _tri_sum(tri, logf)
                    b_last3, b_mid3 = _gla_last_mid(b, rev)
                    b_last, b_mid = _gla_per_row(b_last3), _gla_per_row(b_mid3)
                    qi = (q * jnp.exp(b - b_mid)).astype(bf16)
                    ki = (k * jnp.exp(b_mid - b)).astype(bf16)
                    a = (_dot_nt(qi, ki) * maskf).astype(bf16)
                    kv = _dot_tn(vb, _gla_block_diag(k * jnp.exp(b_last - b), row_masks))
                    decay3 = jnp.exp(b_last3)
                    st = st_ref[ci]
                    before = [None] * GLA_GROUP
                    for c in _gla_scan_order(rev):
                        before[c] = st.astype(bf16)
                        st = st * decay3[c] + kv[:, 128 * c:128 * (c + 1)]
                    st_ref[ci] = st
                    inter = _dot_nt((q * jnp.exp(b)).astype(bf16), jnp.concatenate(before, axis=0))
                    o = _dot(a, vb) + _gla_diag(inter)
                    if first:
                        o_ref[rows, cols] = o
                    else:
                        o_ref[rows, cols] += o
                return carry
            return step

        lax.fori_loop(0, n_groups // 2, make_step(True), 0)
        lax.fori_loop(n_groups // 2, n_groups, make_step(False), 0)

    w = 128 * hp
    col = lambda base: pl.BlockSpec((s, w), lambda h: (0, base // w + h))
    return pl.pallas_call(
        body, name="fwd_gla", grid=(HG_HEADS // hp,),
        in_specs=[col(Z_HQ), col(Z_HFF), col(Z_HFB), col(Z_HI), pl.BlockSpec((4, w), lambda h: (0, h))],
        out_specs=pl.BlockSpec((s, w), lambda h: (0, h)),
        out_shape=jax.ShapeDtypeStruct((s, HG_HEADS * 128), f32),
        scratch_shapes=[pltpu.VMEM((len(chains), 128, 128), f32)],
        compiler_params=_cp(("parallel",), VMEM_LIMIT),
    )(z, z, z, z, lb4)


def _hg_out(o, hg, g_hgo):
    outs, ons, rs = [], [], []
    for h in range(HG_HEADS):
        oh = o[:, 128 * h:128 * (h + 1)]
        on, r = _rms_fwd(oh, g_hgo[:, 128 * h:128 * (h + 1)], 128)
        ons.append(on)
        rs.append(r)
    on = jnp.concatenate(ons, axis=-1)
    sg = _sigmoid(hg)
    return on * (hg * sg), on, rs, sg


def _fwd_mix(a, o, z, g_hgo, x, w_o, tm):
    s, d = x.shape

    def body(a_ref, o_ref, hg_ref, g_ref, x_ref, w_ref, x2_ref, cat_ref):
        r, _, _, _ = _hg_out(o_ref[...], hg_ref[...], g_ref[...])
        cat = jnp.concatenate([a_ref[...], r.astype(bf16)], axis=-1)
        cat_ref[...] = cat
        x2_ref[...] = x_ref[...] + _dot(cat, w_ref[...])

    row512 = pl.BlockSpec((tm, 512), lambda i: (i, 0))
    rowd = pl.BlockSpec((tm, d), lambda i: (i, 0))
    return pl.pallas_call(
        body, name="fwd_mix", grid=(s // tm,),
        in_specs=[row512, row512, pl.BlockSpec((tm, 512), lambda i: (i, Z_HG // 512)), _const_spec((1, 512)), rowd,
                  _const_spec((d, d))],
        out_specs=[rowd, rowd],
        out_shape=[jax.ShapeDtypeStruct((s, d), f32), jax.ShapeDtypeStruct((s, d), bf16)],
        compiler_params=_cp(("parallel",), VMEM_LIMIT),
    )(a, o, z, g_hgo, x, w_o)


def _fwd_ffn(x2, g_ffn, w_gate, w_up, w_down, tm):
    s, d = x2.shape

    def body(x_ref, g_ref, wg_ref, wu_ref, wd_ref, x3_ref, gp_ref, up_ref):
        x = x_ref[...]
        h, _ = _rms_fwd(x, g_ref[...], d)
        hb = h.astype(bf16)
        gp = _dot(hb, wg_ref[...])
        up = _dot(hb, wu_ref[...])
        gp_ref[...] = gp
        up_ref[...] = up
        act = (gp * _sigmoid(gp) * up).astype(bf16)
        x3_ref[...] = x + _dot(act, wd_ref[...])

    rowd = pl.BlockSpec((tm, d), lambda i: (i, 0))
    rowf = pl.BlockSpec((tm, D_FF), lambda i: (i, 0))
    return pl.pallas_call(
        body, name="fwd_ffn", grid=(s // tm,),
        in_specs=[rowd, _const_spec((1, d)), _const_spec((d, D_FF)), _const_spec((d, D_FF)), _const_spec((D_FF, d))],
        out_specs=[rowd, rowf, rowf],
        out_shape=[jax.ShapeDtypeStruct((s, d), f32), jax.ShapeDtypeStruct((s, D_FF), f32),
                   jax.ShapeDtypeStruct((s, D_FF), f32)],
        compiler_params=_cp(("parallel",), VMEM_LIMIT),
    )(x2, g_ffn, w_gate, w_up, w_down)


def _ple_loss_fwd_bwd(x3, g_ple, w_pg, p, w_pp, target, tm):
    s, d = x3.shape

    def body(x_ref, g_ref, wg_ref, p_ref, wp_ref, t_ref, dx_ref, h_ref, dpre_ref, dpp_ref, dg_ref, loss_ref):
        @pl.when(pl.program_id(0) == 0)
        def _():
            dg_ref[...] = jnp.zeros_like(dg_ref)
            loss_ref[...] = jnp.zeros_like(loss_ref)

        x = x_ref[...]
        g = g_ref[...]
        h, r = _rms_fwd(x, g, d)
        hb = h.astype(bf16)
        gate = _sigmoid(_dot(hb, wg_ref[...]))
        pp = _dot(p_ref[...].astype(bf16), wp_ref[...])
        e = x + gate * pp - t_ref[...]
        loss_ref[...] += 0.5 * jnp.sum(e * e) * (1.0 / d)
        dy = e * (1.0 / d)
        dpre = (dy * pp * gate * (1.0 - gate)).astype(bf16)
        dx, dgx = _rms_bwd(_dot_nt(dpre, wg_ref[...]), x, r, g, d)
        dx_ref[...] = dy + dx
        dg_ref[...] += jnp.sum(dgx, axis=0, keepdims=True)
        h_ref[...] = hb
        dpre_ref[...] = dpre
        dpp_ref[...] = (dy * gate).astype(bf16)

    rowd = pl.BlockSpec((tm, d), lambda i: (i, 0))
    return pl.pallas_call(
        body, name="ple_loss_fwd_bwd", grid=(s // tm,),
        in_specs=[rowd, _const_spec((1, d)), _const_spec((d, d)), pl.BlockSpec((tm, PLE_DIM), lambda i: (i, 0)),
                  _const_spec((PLE_DIM, d)), rowd],
        out_specs=[rowd, rowd, rowd, rowd, _acc_spec((1, d)), _acc_spec((8, 128))],
        out_shape=[jax.ShapeDtypeStruct((s, d), f32), jax.ShapeDtypeStruct((s, d), bf16), jax.ShapeDtypeStruct((s, d), bf16),
                   jax.ShapeDtypeStruct((s, d), bf16), jax.ShapeDtypeStruct((1, d), f32), jax.ShapeDtypeStruct((8, 128), f32)],
        compiler_params=_cp(("arbitrary",), VMEM_LIMIT),
    )(x3, g_ple, w_pg, p, w_pp, target)


def _bwd_ffn(d3, x2, gp, up, g_ffn, w_gate, w_up, w_down, tm):
    s, d = x2.shape

    def body(d3_ref, x_ref, gp_ref, up_ref, g_ref, wg_ref, wu_ref, wd_ref, d2_ref, h_ref, act_ref, dgp_ref, dup_ref, dg_ref):
        @pl.when(pl.program_id(0) == 0)
        def _():
            dg_ref[...] = jnp.zeros_like(dg_ref)

        x = x_ref[...]
        g = g_ref[...]
        d3 = d3_ref[...]
        h, r = _rms_fwd(x, g, d)
        h_ref[...] = h.astype(bf16)
        gp, up = gp_ref[...], up_ref[...]
        sg = _sigmoid(gp)
        silu = gp * sg
        act_ref[...] = (silu * up).astype(bf16)
        dact = _dot_nt(d3.astype(bf16), wd_ref[...])
        dgp = (dact * up * (sg * (1.0 + gp * (1.0 - sg)))).astype(bf16)
        dup = (dact * silu).astype(bf16)
        dgp_ref[...] = dgp
        dup_ref[...] = dup
        dh = _dot_nt(dgp, wg_ref[...]) + _dot_nt(dup, wu_ref[...])
        dx, dgx = _rms_bwd(dh, x, r, g, d)
        d2_ref[...] = d3 + dx
        dg_ref[...] += jnp.sum(dgx, axis=0, keepdims=True)

    rowd = pl.BlockSpec((tm, d), lambda i: (i, 0))
    rowf = pl.BlockSpec((tm, D_FF), lambda i: (i, 0))
    return pl.pallas_call(
        body, name="bwd_ffn", grid=(s // tm,),
        in_specs=[rowd, rowd, rowf, rowf, _const_spec((1, d)), _const_spec((d, D_FF)), _const_spec((d, D_FF)),
                  _const_spec((D_FF, d))],
        out_specs=[rowd, rowd, rowf, rowf, rowf, _acc_spec((1, d))],
        out_shape=[jax.ShapeDtypeStruct((s, d), f32), jax.ShapeDtypeStruct((s, d), bf16)] + [jax.ShapeDtypeStruct((s, D_FF), bf16)] * 3
        + [jax.ShapeDtypeStruct((1, d), f32)],
        compiler_params=_cp(("arbitrary",), VMEM_LIMIT),
    )(d3, x2, gp, up, g_ffn, w_gate, w_up, w_down)


def _bwd_mix(d2, w_o, o, z, g_hgo, tm):
    s, d = d2.shape

    def body(d2_ref, w_ref, o_ref, hg_ref, g_ref, da_ref, do_ref, dhg_ref, dg_ref):
        @pl.when(pl.program_id(0) == 0)
        def _():
            dg_ref[...] = jnp.zeros_like(dg_ref)

        dcat = _dot_nt(d2_ref[...].astype(bf16), w_ref[...])
        da_ref[...] = dcat[:, 0:512].astype(bf16)
        dr = dcat[:, 512:1024]
        o, hg, g = o_ref[...], hg_ref[...], g_ref[...]
        _, on, rs, sg = _hg_out(o, hg, g)
        dhg_ref[...] = (dr * on * (sg * (1.0 + hg * (1.0 - sg)))).astype(bf16)
        don = dr * (hg * sg)
        dgs = []
        for h in range(HG_HEADS):
            cols = slice(128 * h, 128 * (h + 1))
            dx, dgx = _rms_bwd(don[:, cols], o[:, cols], rs[h], g[:, cols], 128)
            do_ref[:, cols] = dx
            dgs.append(jnp.sum(dgx, axis=0, keepdims=True))
        dg_ref[...] += jnp.concatenate(dgs, axis=-1)

    row512 = pl.BlockSpec((tm, 512), lambda i: (i, 0))
    return pl.pallas_call(
        body, name="bwd_mix", grid=(s // tm,),
        in_specs=[pl.BlockSpec((tm, d), lambda i: (i, 0)), _const_spec((d, d)), row512,
                  pl.BlockSpec((tm, 512), lambda i: (i, Z_HG // 512)), _const_spec((1, 512))],
        out_specs=[row512, row512, row512, _acc_spec((1, 512))],
        out_shape=[jax.ShapeDtypeStruct((s, 512), bf16), jax.ShapeDtypeStruct((s, 512), f32), jax.ShapeDtypeStruct((s, 512), bf16),
                   jax.ShapeDtypeStruct((1, 512), f32)],
        compiler_params=_cp(("arbitrary",), VMEM_LIMIT),
    )(d2, w_o, o, z, g_hgo)


def _bwd_gla(z, lb4, do):
    s = z.shape[0]
    n_chunks = s // CHUNK
    n_groups = s // GLA_ROWS
    assert n_groups % 2 == 0

    def body(hq_ref, hff_ref, hfb_ref, hi_ref, lb_ref, do_ref, dhq_ref, dhff_ref, dhfb_ref, dhi_ref, dlb_ref,
             st_all, b_all, dst_ref, dq_acc, dv_acc, dlow_ref):
        dirs = (False, True)
        masks = [_gla_masks(rev) for rev in dirs]
        lowers = [_sigmoid(lb_ref[int(rev):int(rev) + 1, :] - lb_ref[2 + int(rev):3 + int(rev), :]) for rev in dirs]
        hf_refs, dhf_refs = (hff_ref, hfb_ref), (dhff_ref, dhfb_ref)

        def fwd_step(n, sts):
            new = []
            for d, rev in enumerate(dirs):
                rows, chunk0 = _gla_rows(n, n_groups, rev)
                _, k, logf, _, _ = _gla_gates(hq_ref[rows, :], hf_refs[d][rows, :], lowers[d])
                b = _tri_sum(masks[d][1], logf)
                b_last3, _ = _gla_last_mid(b, rev)
                b_all[d, rows, :] = b
                kv = _dot_tn(hi_ref[rows, :].astype(bf16), _gla_block_diag(k * jnp.exp(_gla_per_row(b_last3) - b), masks[d][3]))
                decay3 = jnp.exp(b_last3)
                st = sts[d]
                for c in _gla_scan_order(rev):
                    st_all[d, chunk0 + c] = st
                    st = st * decay3[c] + kv[:, 128 * c:128 * (c + 1)]
                new.append(st)
            return tuple(new)

        zero = jnp.zeros((128, 128), f32)
        lax.fori_loop(0, n_groups, fwd_step, (zero, zero))

        dst_ref[...] = jnp.zeros_like(dst_ref)
        dlow_ref[...] = jnp.zeros_like(dlow_ref)

        def make_bwd_step(first):
            def bwd_step(j, carry):
                n = n_groups - 1 - j
                for d, rev in enumerate(dirs):
                    maskf, _, tri_t, row_masks = masks[d]
                    lower = lowers[d]
                    rows, chunk0 = _gla_rows(n, n_groups, rev)
                    hq, hf = hq_ref[rows, :], hf_refs[d][rows, :]
                    q, k, _, f, sg = _gla_gates(hq, hf, lower)
                    v = hi_ref[rows, :]
                    dout = do_ref[rows, :]
                    b = b_all[d, rows, :]
                    b_last3, b_mid3 = _gla_last_mid(b, rev)
                    b_last, b_mid = _gla_per_row(b_last3), _gla_per_row(b_mid3)
                    e1, e2, e3, e4 = jnp.exp(b - b_mid), jnp.exp(b_mid - b), jnp.exp(b_last - b), jnp.exp(b)
                    decay3 = jnp.exp(b_last3)
                    qi, ki, kt, qt = q * e1, k * e2, k * e3, q * e4
                    qib, kib, ktb = qi.astype(bf16), ki.astype(bf16), kt.astype(bf16)
                    vb, dob = v.astype(bf16), dout.astype(bf16)
                    a = (_dot_nt(qib, kib) * maskf).astype(bf16)
                    da = (_dot_nt(dob, vb) * maskf).astype(bf16)
                    dqi = _dot(da, kib)
                    dki = _dot_tn(da, qib)
                    into_state = _dot_tn(dob, _gla_block_diag(qt, row_masks))
                    dst = dst_ref[d]
                    sts, dsts, ddecay = [None] * GLA_GROUP, [None] * GLA_GROUP, [None] * GLA_GROUP
                    for c in reversed(_gla_scan_order(rev)):
                        sts[c] = st_all[d, chunk0 + c]
                        dsts[c] = dst.astype(bf16)
                        ddecay[c] = jnp.sum(dst * sts[c], axis=0, keepdims=True)[None]
                        dst = dst * decay3[c] + into_state[:, 128 * c:128 * (c + 1)]
                    dst_ref[d] = dst
                    dv = _dot_tn(a, dob) + _gla_diag(_dot_nt(ktb, jnp.concatenate(dsts, axis=0)))
                    dqt = _gla_diag(_dot(dob, jnp.concatenate([x.astype(bf16) for x in sts], axis=-1)))
                    dkt = _gla_diag(_dot(vb, jnp.concatenate(dsts, axis=-1)))
                    dq = dqi * e1 + dqt * e4
                    dk = dki * e2 + dkt * e3
                    db = dqi * qi - dki * ki + dqt * qt - dkt * kt
                    dlast3 = (jnp.sum((dkt * kt).reshape(GLA_GROUP, CHUNK, 128), axis=1, keepdims=True)
                              + jnp.concatenate(ddecay, axis=0) * decay3)
                    dlogf = _tri_sum(tri_t, db) + _gla_per_row(dlast3)
                    df = dlogf / f - dk
                    dhf_refs[d][rows, :] = (df * (1.0 - lower) * sg * (1.0 - sg)).astype(bf16)
                    dlow_ref[d:d + 1, :] += jnp.sum(df * (1.0 - sg), axis=0, keepdims=True)
                    sq = _sigmoid(hq)
                    dhq = dq * (sq * (1.0 + hq * (1.0 - sq)))
                    if first:
                        dq_acc[rows, :] = dhq
                        dv_acc[rows, :] = dv
                    else:
                        dhq_ref[rows, :] = (dq_acc[rows, :] + dhq).astype(bf16)
                        dhi_ref[rows, :] = (dv_acc[rows, :] + dv).astype(bf16)
                return carry
            return bwd_step

        lax.fori_loop(0, n_groups // 2, make_bwd_step(True), 0)
        lax.fori_loop(n_groups // 2, n_groups, make_bwd_step(False), 0)

        for d in range(2):
            dl = dlow_ref[d:d + 1, :] * lowers[d] * (1.0 - lowers[d])
            dlb_ref[d:d + 1, :] = dl
            dlb_ref[2 + d:3 + d, :] = -dl

    col = lambda base: pl.BlockSpec((s, 128), lambda h: (0, base // 128 + h))
    return pl.pallas_call(
        body, name="bwd_gla", grid=(HG_HEADS,),
        in_specs=[col(Z_HQ), col(Z_HFF), col(Z_HFB), col(Z_HI), pl.BlockSpec((4, 128), lambda h: (0, h)), col(0)],
        out_specs=[col(0), col(0), col(0), col(0), pl.BlockSpec((4, 128), lambda h: (0, h))],
        out_shape=[jax.ShapeDtypeStruct((s, 512), bf16)] * 4 + [jax.ShapeDtypeStruct((4, 512), f32)],
        scratch_shapes=[pltpu.VMEM((2, n_chunks, 128, 128), f32), pltpu.VMEM((2, s, 128), f32), pltpu.VMEM((2, 128, 128), f32),
                        pltpu.VMEM((s, 128), f32), pltpu.VMEM((s, 128), f32), pltpu.VMEM((2, 128), f32)],
        compiler_params=_cp(("parallel",), VMEM_LIMIT),
    )(z, z, z, z, lb4, do)


def _bwd_attn(q, k, v, da, tq):
    hh, s, _ = q.shape

    n_sub = max(1, tq // ATTN_SUB_ROWS)

    def body(q_ref, k_ref, v_ref, do_ref, dq_ref, dk_ref, dv_ref, w_all, ds_all):
        @pl.when(pl.program_id(1) == 0)
        def _():
            dk_ref[...] = jnp.zeros_like(dk_ref)
            dv_ref[...] = jnp.zeros_like(dv_ref)

        kb, vb = k_ref[...], v_ref[...]
        for t in range(n_sub):
            rows = slice(t * (tq // n_sub), (t + 1) * (tq // n_sub))
            sc = _dot_nt(q_ref[rows, :], kb)
            p = jnp.exp2((sc - jnp.max(sc, axis=-1, keepdims=True)) * (ATTN_SCALE * LOG2_E))
            w = p * (1.0 / jnp.sum(p, axis=-1, keepdims=True))
            dw = _dot_nt(do_ref[rows, :], vb)
            ds = (w * (dw - jnp.sum(dw * w, axis=-1, keepdims=True)) * ATTN_SCALE).astype(bf16)
            dq_ref[rows, :] = _dot(ds, kb)
            w_all[rows, :] = w.astype(bf16)
            ds_all[rows, :] = ds
        dk_ref[...] += _dot_tn(ds_all[...], q_ref[...])
        dv_ref[...] += _dot_tn(w_all[...], do_ref[...])

    return pl.pallas_call(
        body, name="bwd_attn", grid=(hh, s // tq),
        in_specs=[pl.BlockSpec((None, tq, QK_PAD), lambda h, i: (h, i, 0)),
                  pl.BlockSpec((None, s, QK_PAD), lambda h, i: (h, 0, 0)),
                  pl.BlockSpec((None, s, V_HEAD), lambda h, i: (h, 0, 0)),
                  pl.BlockSpec((tq, V_HEAD), lambda h, i: (i, h))],
        out_specs=[pl.BlockSpec((None, tq, QK_PAD), lambda h, i: (h, i, 0)),
                   pl.BlockSpec((None, s, QK_PAD), lambda h, i: (h, 0, 0)),
                   pl.BlockSpec((None, s, V_HEAD), lambda h, i: (h, 0, 0))],
        out_shape=[jax.ShapeDtypeStruct((hh, s, QK_PAD), f32), jax.ShapeDtypeStruct((hh, s, QK_PAD), f32),
                   jax.ShapeDtypeStruct((hh, s, V_HEAD), f32)],
        scratch_shapes=[pltpu.VMEM((tq, s), bf16), pltpu.VMEM((tq, s), bf16)],
        compiler_params=_cp(("parallel", "arbitrary"), VMEM_LIMIT),
    )(q, k, v, da)


def _bwd_mla_proj(z, dq, dk, dv, cosb, sina, sinb, g_qa, g_kva, wqb, wkvb, g_qn, g_kn, tm):
    s = z.shape[0]
    hh = MLA_HEADS

    def body(cq_ref, ckv_ref, kr_ref, dq_ref, dk_ref, dv_ref, c_ref, sa_ref, sb_ref, gqa_ref, gkva_ref, wqb_ref, wkvb_ref,
             gqn_ref, gkn_ref, dz_ref, cqn_ref, ckvn_ref, dq0_ref, dkv0_ref, dgqa_ref, dgkva_ref, dgqn_ref, dgkn_ref):
        @pl.when(pl.program_id(0) == 0)
        def _():
            for r in (dgqa_ref, dgkva_ref, dgqn_ref, dgkn_ref):
                r[...] = jnp.zeros_like(r)

        cq, ckv, kr = cq_ref[...], ckv_ref[...], kr_ref[...]
        gqa, gkva, gqn, gkn = gqa_ref[...], gkva_ref[...], gqn_ref[...], gkn_ref[...]
        cqn_b, rq, ckvn_b, rkv, q0, kv0 = _mla_qk_fwd(cq, ckv, kr, gqa, gkva, wqb_ref[...], wkvb_ref[...], gqn, gkn)
        cqn_ref[...] = cqn_b
        ckvn_ref[...] = ckvn_b
        c, sa, sb = c_ref[...], -sa_ref[...], -sb_ref[...]
        kr_sq = jnp.sum(kr * kr, axis=-1, keepdims=True)
        dkr = jnp.zeros_like(kr)
        dgqn = jnp.zeros((1, QK_PAD), f32)
        dgkn = jnp.zeros((1, QK_PAD), f32)
        for h in range(hh):
            qh = q0[:, QK_PAD * h:QK_PAD * (h + 1)]
            rh = lax.rsqrt(jnp.sum(qh * qh, axis=-1, keepdims=True) * (1.0 / QK_HEAD) + EPS)
            dqh = dq_ref[h]
            dqn = jnp.concatenate([dqh[:, 0:128], _rope(dqh[:, 128:256], c, sa, sb)], axis=-1)
            dq0h, dgx = _rms_bwd(dqn, qh, rh, gqn, QK_HEAD)
            dq0_ref[:, QK_PAD * h:QK_PAD * (h + 1)] = dq0h.astype(bf16)
            dgqn = dgqn + jnp.sum(dgx, axis=0, keepdims=True)

            kn_ = kv0[:, 256 * h:256 * h + 128]
            k0 = jnp.concatenate([kn_, kr], axis=-1)
            rk = lax.rsqrt((jnp.sum(kn_ * kn_, axis=-1, keepdims=True) + kr_sq) * (1.0 / QK_HEAD) + EPS)
            dkh = dk_ref[h]
            dkn = jnp.concatenate([dkh[:, 0:128], _rope(dkh[:, 128:256], c, sa, sb)], axis=-1)
            dk0, dgx = _rms_bwd(dkn, k0, rk, gkn, QK_HEAD)
            dgkn = dgkn + jnp.sum(dgx, axis=0, keepdims=True)
            dkv0_ref[:, 256 * h:256 * h + 128] = dk0[:, 0:128].astype(bf16)
            dkv0_ref[:, 256 * h + 128:256 * h + 256] = dv_ref[h].astype(bf16)
            dkr = dkr + dk0[:, 128:256]
        dgqn_ref[...] += dgqn
        dgkn_ref[...] += dgkn
        dcq, dgx = _rms_bwd(_dot_nt(dq0_ref[...], wqb_ref[...]), cq, rq, gqa, Q_LORA)
        dgqa_ref[...] += jnp.sum(dgx, axis=0, keepdims=True)
        dckv, dgx = _rms_bwd(_dot_nt(dkv0_ref[...], wkvb_ref[...]), ckv, rkv, gkva, KV_LORA)
        dgkva_ref[...] += jnp.sum(dgx, axis=0, keepdims=True)
        dz_ref[:, 0:256] = dcq.astype(bf16)
        dz_ref[:, 256:512] = dckv.astype(bf16)
        dz_ref[:, 512:640] = dkr.astype(bf16)

    row128 = pl.BlockSpec((tm, 128), lambda i: (i, 0))
    row256 = pl.BlockSpec((tm, 256), lambda i: (i, 0))
    row1024 = pl.BlockSpec((tm, 1024), lambda i: (i, 0))
    hd = lambda w: pl.BlockSpec((hh, tm, w), lambda i: (0, i, 0))
    return pl.pallas_call(
        body, name="bwd_mla_proj", grid=(s // tm,),
        in_specs=[pl.BlockSpec((tm, 256), lambda i: (i, Z_CQ // 256)), pl.BlockSpec((tm, 256), lambda i: (i, Z_CKV // 256)),
                  pl.BlockSpec((tm, 128), lambda i: (i, Z_KR // 128)), hd(QK_PAD), hd(QK_PAD), hd(V_HEAD),
                  row128, row128, row128,
                  _const_spec((1, 256)), _const_spec((1, 256)), _const_spec((256, 1024)), _const_spec((256, 1024)),
                  _const_spec((1, 256)), _const_spec((1, 256))],
        out_specs=[pl.BlockSpec((tm, 640), lambda i: (i, 0)), row256, row256, row1024, row1024,
                   _acc_spec((1, 256)), _acc_spec((1, 256)), _acc_spec((1, 256)), _acc_spec((1, 256))],
        out_shape=[jax.ShapeDtypeStruct((s, 640), bf16), jax.ShapeDtypeStruct((s, 256), bf16), jax.ShapeDtypeStruct((s, 256), bf16),
                   jax.ShapeDtypeStruct((s, 1024), bf16), jax.ShapeDtypeStruct((s, 1024), bf16)]
        + [jax.ShapeDtypeStruct((1, 256), f32)] * 4,
        compiler_params=_cp(("arbitrary",), VMEM_LIMIT),
    )(z, z, z, dq, dk, dv, cosb, sina, sinb, g_qa, g_kva, wqb, wkvb, g_qn, g_kn)


def _bwd_in(segments, wz, x, g_mix, d2, tm):
    s, d = x.shape
    n_seg = len(segments)

    def body(*refs):
        dz_refs, w_refs = refs[:n_seg], refs[n_seg:2 * n_seg]
        x_ref, g_ref, d2_ref, gx_ref, dg_ref = refs[2 * n_seg:]

        @pl.when(pl.program_id(0) == 0)
        def _():
            dg_ref[...] = jnp.zeros_like(dg_ref)

        dh = _dot_nt(dz_refs[0][...], w_refs[0][...])
        for a_ref, w_ref in zip(dz_refs[1:], w_refs[1:]):
            dh = dh + _dot_nt(a_ref[...], w_ref[...])
        x, g = x_ref[...], g_ref[...]
        r = lax.rsqrt(jnp.sum(x * x, axis=-1, keepdims=True) * (1.0 / d) + EPS)
        dx, dgx = _rms_bwd(dh, x, r, g, d)
        gx_ref[...] = d2_ref[...] + dx
        dg_ref[...] += jnp.sum(dgx, axis=0, keepdims=True)

    rowd = pl.BlockSpec((tm, d), lambda i: (i, 0))
    dz_specs = [pl.BlockSpec((tm, w), functools.partial(lambda i, j: (i, j), j=ja)) for _, w, ja, _ in segments]
    w_specs = [pl.BlockSpec((d, w), functools.partial(lambda i, j: (0, j), j=jw), pipeline_mode=pl.Buffered(1))
               for _, w, _, jw in segments]
    return pl.pallas_call(
        body, name="bwd_in", grid=(s // tm,),
        in_specs=dz_specs + w_specs + [rowd, _const_spec((1, d)), rowd],
        out_specs=[rowd, _acc_spec((1, d))],
        out_shape=[jax.ShapeDtypeStruct((s, d), f32), jax.ShapeDtypeStruct((1, d), f32)],
        compiler_params=_cp(("arbitrary",), VMEM_LIMIT),
    )(*[a for a, _, _, _ in segments], *([wz] * n_seg), x, g_mix, d2)


def _pick_tile(n, cap):
    best = None
    for t in range(LANES, cap + 1, LANES):
        if n % t == 0:
            best = t
    return best if best is not None else n


def _mm_tn(a, b, name):
    kk, m = a.shape
    _, n = b.shape
    tm = _pick_tile(m, 1408)
    tn = _pick_tile(n, 1408)
    tk = min(512, kk)

    n_k = kk // tk

    def body(a_ref, b_ref, o_ref, acc_ref):
        @pl.when(pl.program_id(2) == 0)
        def _():
            acc_ref[...] = jnp.zeros_like(acc_ref)
        acc_ref[...] += _dot_tn(a_ref[...].astype(bf16), b_ref[...].astype(bf16))

        @pl.when(pl.program_id(2) == n_k - 1)
        def _():
            o_ref[...] = acc_ref[...].astype(bf16)

    return pl.pallas_call(
        body, name=name, grid=(m // tm, n // tn, n_k),
        in_specs=[pl.BlockSpec((tk, tm), lambda i, j, k: (k, i)), pl.BlockSpec((tk, tn), lambda i, j, k: (k, j))],
        out_specs=pl.BlockSpec((tm, tn), lambda i, j, k: (i, j)),
        out_shape=jax.ShapeDtypeStruct((m, n), bf16),
        scratch_shapes=[pltpu.VMEM((tm, tn), f32)],
        compiler_params=_cp(("parallel", "parallel", "arbitrary"), VMEM_LIMIT),
    )(a, b)


def _rope_tables(positions):
    inv_freq = ROPE_THETA ** (-jnp.arange(0, QK_ROPE, 2, dtype=f32) / QK_ROPE)
    ang = positions.astype(f32)[:, None] * inv_freq
    cos, sin = jnp.cos(ang), jnp.sin(ang)
    zero = jnp.zeros_like(cos)
    return (jnp.concatenate([cos, cos, zero, zero], axis=1), jnp.concatenate([zero, sin, zero, zero], axis=1),
            jnp.concatenate([-sin, zero, zero, zero], axis=1))


def _pad256(g):
    return jnp.pad(g.reshape(1, QK_HEAD), ((0, 0), (0, QK_PAD - QK_HEAD)))


RELAYOUT_BLOCKS = 8
FIRST = ("w_in", "w_qb", "w_kvb", "lb_param")
SECOND = ("w_o", "w_gate", "w_up", "w_down", "w_ple_gate", "w_ple_proj")
ROW_SHARDED = ("w_o", "w_down", "w_ple_gate")


def _col_moves(j):
    lo = BIG["w_in"][1] * j
    w_in = [(max(lo, a) - lo, min(lo + BIG["w_in"][1], b) - lo, d + max(lo, a) - a)
            for a, b, d in Z_SEGMENTS if max(lo, a) < min(lo + BIG["w_in"][1], b)]
    head, half = divmod(j, 2)
    whole = lambda n: [(0, BIG[n][1], BIG[n][1] * j)]
    return {"w_in": w_in, "w_gate": whole("w_gate"), "w_up": whole("w_up"),
            "w_qb": [(0, 96, QK_PAD * head + 96 * half)], "w_kvb": whole("w_kvb"), "w_ple_proj": whole("w_ple_proj"),
            "lb_param": whole("lb_param")}


def _kernel_width(name):
    return {"w_in": Z_W, "w_qb": MLA_HEADS * QK_PAD}.get(name, N_DEV * BIG[name][1])


def _relayout_specs(names, by_dev):
    specs = []
    for n in names:
        rows, cols = BIG[n]
        if n == "lb_param":
            specs.append(_acc_spec((N_DEV, rows, cols) if by_dev else (rows, _kernel_width(n))))
        elif by_dev:
            specs.append(pl.BlockSpec((N_DEV, rows // RELAYOUT_BLOCKS, cols), lambda i: (0, i, 0)))
        else:
            specs.append(pl.BlockSpec((rows // RELAYOUT_BLOCKS, _kernel_width(n)), lambda i: (i, 0)))
    return specs


def _weights_in(gathered, names, name):
    n = len(names)

    def body(*refs):
        ins, outs = dict(zip(names, refs[:n])), dict(zip(names, refs[n:]))
        if "w_in" in outs:
            outs["w_in"][:, Z_KR + QK_ROPE:Z_W] = jnp.zeros((outs["w_in"].shape[0], Z_W - Z_KR - QK_ROPE), bf16)
        if "w_qb" in outs:
            for h in range(MLA_HEADS):
                outs["w_qb"][:, QK_PAD * h + QK_HEAD:QK_PAD * (h + 1)] = jnp.zeros((outs["w_qb"].shape[0], QK_PAD - QK_HEAD), bf16)
        for j in range(N_DEV):
            for wn, moves in _col_moves(j).items():
                if wn in outs:
                    for s0, s1, d0 in moves:
                        outs[wn][:, d0:d0 + s1 - s0] = ins[wn][j, :, s0:s1]

    outs = pl.pallas_call(
        body, name=name, grid=(RELAYOUT_BLOCKS,), in_specs=_relayout_specs(names, True), out_specs=_relayout_specs(names, False),
        out_shape=[jax.ShapeDtypeStruct((BIG[wn][0], _kernel_width(wn)), gathered[wn].dtype) for wn in names],
        compiler_params=_cp(("arbitrary",), VMEM_LIMIT),
    )(*[gathered[wn] for wn in names])
    return dict(zip(names, outs))


def _grads_out(sources, names, name):
    pieces = [(wn, start, arr) for wn in names for start, arr in sources[wn]]
    n_in = len(pieces)

    def body(*refs):
        outs = dict(zip(names, refs[n_in:]))

        def cols(wn, c0, c1):
            for (pn, start, arr), ref in zip(pieces, refs[:n_in]):
                if pn == wn and start <= c0 and c1 <= start + arr.shape[1]:
                    return ref[:, c0 - start:c1 - start]

        for j in range(N_DEV):
            for wn, moves in _col_moves(j).items():
                if wn in outs:
                    for s0, s1, d0 in moves:
                        outs[wn][j, :, s0:s1] = cols(wn, d0, d0 + s1 - s0).astype(bf16)

    in_specs = [_acc_spec(arr.shape) if wn == "lb_param" else pl.BlockSpec((arr.shape[0] // RELAYOUT_BLOCKS, arr.shape[1]), lambda i: (i, 0))
                for wn, _, arr in pieces]
    outs = pl.pallas_call(
        body, name=name, grid=(RELAYOUT_BLOCKS,), in_specs=in_specs, out_specs=_relayout_specs(names, True),
        out_shape=[jax.ShapeDtypeStruct((N_DEV, *BIG[wn]), bf16) for wn in names],
        compiler_params=_cp(("arbitrary",), VMEM_LIMIT),
    )(*[arr for _, _, arr in pieces])
    return dict(zip(names, outs))


def kernel(x, p, positions, g_mix, w_in, g_qa, g_kva, w_qb, w_kvb, g_qn, g_kn, lb_param, g_hgo, w_o, g_ffn, w_gate, w_up, w_down, g_ple, w_ple_gate, w_ple_proj, loss_target, m_g_mix, m_w_in, m_g_qa, m_g_kva, m_w_qb, m_w_kvb, m_g_qn, m_g_kn, m_lb_param, m_g_hgo, m_w_o, m_g_ffn, m_w_gate, m_w_up, m_w_down, m_g_ple, m_w_ple_gate, m_w_ple_proj, v_g_mix, v_w_in, v_g_qa, v_g_kva, v_w_qb, v_w_kvb, v_g_qn, v_g_kn, v_lb_param, v_g_hgo, v_w_o, v_g_ffn, v_w_gate, v_w_up, v_w_down, v_g_ple, v_w_ple_gate, v_w_ple_proj):
    w_all = dict(g_mix=g_mix, g_qa=g_qa, g_kva=g_kva, g_qn=g_qn, g_kn=g_kn, g_hgo=g_hgo, g_ffn=g_ffn, g_ple=g_ple,
                 w_in=w_in, w_qb=w_qb, w_kvb=w_kvb, w_o=w_o, w_gate=w_gate, w_up=w_up, w_down=w_down,
                 w_ple_gate=w_ple_gate, w_ple_proj=w_ple_proj, lb_param=lb_param)
    m_all = dict(g_mix=m_g_mix, g_qa=m_g_qa, g_kva=m_g_kva, g_qn=m_g_qn, g_kn=m_g_kn, g_hgo=m_g_hgo, g_ffn=m_g_ffn,
                 g_ple=m_g_ple, w_in=m_w_in, w_qb=m_w_qb, w_kvb=m_w_kvb, w_o=m_w_o, w_gate=m_w_gate, w_up=m_w_up,
                 w_down=m_w_down, w_ple_gate=m_w_ple_gate, w_ple_proj=m_w_ple_proj, lb_param=m_lb_param)
    v_all = dict(g_mix=v_g_mix, g_qa=v_g_qa, g_kva=v_g_kva, g_qn=v_g_qn, g_kn=v_g_kn, g_hgo=v_g_hgo, g_ffn=v_g_ffn,
                 g_ple=v_g_ple, w_in=v_w_in, w_qb=v_w_qb, w_kvb=v_w_kvb, w_o=v_w_o, w_gate=v_w_gate, w_up=v_w_up,
                 w_down=v_w_down, w_ple_gate=v_w_ple_gate, w_ple_proj=v_w_ple_proj, lb_param=v_lb_param)
    me_idx = jnp.stack([_me()]).astype(jnp.int32)
    x, p, positions, target = x[0], p[0, 0], positions[0], loss_target[0]
    s = x.shape[0]
    tm, tm_ffn, tq_f, tq_b = min(256, s), min(128, s), min(2048, s), min(1024, s)
    g_mix, g_qa, g_kva, g_qn, g_kn, g_hgo, g_ffn, g_ple = (w_all[n].reshape(1, -1) for n in SMALL)
    g_qn_p, g_kn_p = _pad256(g_qn), _pad256(g_kn)
    cosb, sina, sinb = _rope_tables(positions)
    shard = lambda n: w_all[n].reshape(BIG[n])

    first = _all_gather([shard(n) for n in FIRST], [f32 if n == "lb_param" else bf16 for n in FIRST], "ag_first")
    lands = _cast_to_slot([shard(n) for n in SECOND], me_idx, first[0])
    ag2, token = _exchange_start([], lands, "ag_second_start")
    wk = _weights_in(dict(zip(FIRST, first)), FIRST, "weights_in_first")
    wz, wqb, wkvb, lb4 = (wk[n] for n in FIRST)

    h1, z = _fwd_in(x, g_mix, wz, tm)
    q, k, v = _fwd_mla_proj(z, cosb + token[0, 0], sina, sinb, g_qa, g_kva, wqb, wkvb, g_qn_p, g_kn_p, tm)
    a = _fwd_attn(q, k, v, tq_f)
    o = _fwd_gla(z, lb4)

    second = dict(zip(SECOND, _exchange_wait(ag2, o, "ag_second_wait")))
    wk = _weights_in(second, ("w_gate", "w_up", "w_ple_proj"), "weights_in_second")
    w_gate, w_up, w_pp = wk["w_gate"], wk["w_up"], wk["w_ple_proj"]
    w_o, w_down, w_pg = (second[n].reshape(N_DEV * BIG[n][0], BIG[n][1]) for n in ROW_SHARDED)

    x2, cat = _fwd_mix(a, o, z, g_hgo, x, w_o, tm)
    x3, gp, up = _fwd_ffn(x2, g_ffn, w_gate, w_up, w_down, tm)
    d3, h3, dpre, dpp, dg_ple, loss_tile = _ple_loss_fwd_bwd(x3, g_ple, w_pg, p, w_pp, target, tm)
    d2, h2, act, dgp, dup, dg_ffn = _bwd_ffn(d3, x2, gp, up, g_ffn, w_gate, w_up, w_down, tm_ffn)

    blocks = _grads_out({"w_gate": [(0, _mm_tn(h2, dgp, "dw_gate"))], "w_up": [(0, _mm_tn(h2, dup, "dw_up"))],
                         "w_ple_proj": [(0, _mm_tn(p, dpp, "dw_ple_proj"))]}, ("w_gate", "w_up", "w_ple_proj"), "grads_out_second")
    row_grads = {"w_o": _mm_tn(cat, d2, "dw_o"), "w_down": _mm_tn(act, d3, "dw_down"), "w_ple_gate": _mm_tn(h3, dpre, "dw_ple_gate")}
    blocks.update({n: g.reshape(N_DEV, *BIG[n]) for n, g in row_grads.items()})
    empty = lambda names: [lax.empty((N_PEERS, *BIG[n]), bf16) for n in names]
    rs2, token = _exchange_start([blocks[n] for n in SECOND], empty(SECOND), "rs_second_start")

    da, do, dz_hg, dg_hgo = _bwd_mix(d2, w_o, o, z, g_hgo + token[0, 0], tm)
    dz_hq, dz_hff, dz_hfb, dz_hi, dlb4 = _bwd_gla(z, lb4, do)
    dq, dk, dv = _bwd_attn(q, k, v, da, tq_b)
    dz_mla, cqn, ckvn, dq0, dkv0, dg_qa, dg_kva, dg_qn, dg_kn = _bwd_mla_proj(
        z, dq, dk, dv, cosb, sina, sinb, g_qa, g_kva, wqb, wkvb, g_qn_p, g_kn_p, tm)

    gz = [(Z_HQ, _mm_tn(h1, dz_hq, "dw_in_hq")), (Z_HFF, _mm_tn(h1, dz_hff, "dw_in_hff")),
          (Z_HFB, _mm_tn(h1, dz_hfb, "dw_in_hfb")), (Z_HI, _mm_tn(h1, dz_hi, "dw_in_hi")),
          (Z_HG, _mm_tn(h1, dz_hg, "dw_in_hg")), (Z_CQ, _mm_tn(h1, dz_mla, "dw_in_mla"))]
    blocks1 = _grads_out({"w_in": gz, "w_qb": [(0, _mm_tn(cqn, dq0, "dw_qb"))], "w_kvb": [(0, _mm_tn(ckvn, dkv0, "dw_kvb"))],
                          "lb_param": [(0, dlb4)]}, FIRST, "grads_out_first")
    rs1, token = _exchange_start([blocks1[n] for n in FIRST], empty(FIRST), "rs_first_start")

    result = {}

    def adam(names, lands, src, n_blocks):
        outs = _adam_shards(me_idx, [src[n] for n in names], lands, [w_all[n] for n in names], [m_all[n] for n in names],
                            [v_all[n] for n in names], n_blocks, "adamw_" + names[0])
        result.update(zip(names, outs))

    lands2 = dict(zip(SECOND, _exchange_wait(rs2, dz_mla, "rs_second_wait")))
    by8 = tuple(n for n in SECOND if n != "w_down")
    adam(by8, [lands2[n] for n in by8], blocks, 8)
    adam(("w_down",), [lands2["w_down"]], blocks, 2)

    segments = [(dz_hq, 512, 0, Z_HQ // 512), (dz_hff, 512, 0, Z_HFF // 512), (dz_hfb, 512, 0, Z_HFB // 512),
                (dz_hi, 512, 0, Z_HI // 512), (dz_hg, 512, 0, Z_HG // 512), (dz_mla, 640, 0, Z_CQ // 640)]
    grad_x, dg_mix = _bwd_in(segments, wz, x, g_mix + token[0, 0], d2, tm)
    dgains = (dg_mix, dg_qa, dg_kva, dg_qn, dg_kn, dg_hgo, dg_ffn, dg_ple)

    lands1 = _exchange_wait(rs1, grad_x, "rs_first_wait")
    adam(FIRST, lands1, blocks1, 8)

    vec = jnp.concatenate(list(dgains) + [loss_tile[0:1]], axis=1)
    parts = _all_gather([vec], [f32], "ag_gains")[0]
    outs, loss_row = _adam_gains(parts, [w_all[n] for n in SMALL], [m_all[n] for n in SMALL], [v_all[n] for n in SMALL])
    result.update(zip(SMALL, outs))

    order = ("g_mix", "w_in", "g_qa", "g_kva", "w_qb", "w_kvb", "g_qn", "g_kn", "lb_param", "g_hgo", "w_o", "g_ffn",
             "w_gate", "w_up", "w_down", "g_ple", "w_ple_gate", "w_ple_proj")
    return (loss_row[0, 0], grad_x[None], *[result[n][k] for k in range(4) for n in order])
```

```python
import functools
import math

import jax
import jax.numpy as jnp
from jax import lax
from jax.experimental import pallas as pl
from jax.experimental.pallas import tpu as pltpu

f32 = jnp.float32
bf16 = jnp.bfloat16

N_DEV = 8
D_MODEL = 1024
MLA_HEADS = 4
QK_NOPE = 128
QK_ROPE = 64
QK_HEAD = QK_NOPE + QK_ROPE
QK_PAD = 256
V_HEAD = 128
Q_LORA = 256
KV_LORA = 256
HG_HEADS = 4
HG_DK = 128
CHUNK = 64
D_FF = 2816
PLE_DIM = 256
ROPE_THETA = 10000.0
EPS = 1e-6
ATTN_SCALE = QK_HEAD ** -0.5
LOG2_E = math.log2(math.e)
ATTN_SUB_ROWS = 256
IN_SIZES = (256, 256, 64, 512, 512, 512, 512, 512)
D_IN = sum(IN_SIZES)
Z_HQ, Z_HFF, Z_HFB, Z_HI, Z_HG, Z_CQ, Z_CKV, Z_KR, Z_W = 0, 512, 1024, 1536, 2048, 2560, 2816, 3072, 3200

ADAM_LR, ADAM_B1, ADAM_B2, ADAM_EPS, ADAM_WD, ADAM_STEP = 0.001, 0.9, 0.999, 1e-08, 0.01, 10

LANES = 128
BIG = {"w_in": (1024, 392), "w_qb": (256, 96), "w_kvb": (256, 128), "w_o": (128, 1024), "w_gate": (1024, 352),
       "w_up": (1024, 352), "w_down": (352, 1024), "w_ple_gate": (128, 1024), "w_ple_proj": (256, 128),
       "lb_param": (4, 64)}
ROW_BLOCKS = {("w_in", "w_qb", "w_kvb", "w_o", "w_gate", "w_up", "w_ple_gate", "w_ple_proj"): 8, ("w_down", "lb_param"): 2}
SMALL = {"g_mix": (0, 1024), "g_qa": (1024, 256), "g_kva": (1280, 256), "g_qn": (1536, 192), "g_kn": (1792, 192),
         "g_hgo": (2048, 512), "g_ffn": (2560, 1024), "g_ple": (3584, 1024)}
LOSS_OFF = 4608
GAIN_VEC = LOSS_OFF + LANES
Z_SEGMENTS = ((0, 256, Z_CQ), (256, 512, Z_CKV), (512, 576, Z_KR), (576, 1088, Z_HQ), (1088, 1600, Z_HFF),
              (1600, 2112, Z_HFB), (2112, 2624, Z_HI), (2624, 3136, Z_HG))

VMEM_LIMIT = 56 * 1024 * 1024
MESH = pl.DeviceIdType.MESH


def _cp(sem=None, vmem=None):
    return pltpu.CompilerParams(dimension_semantics=sem, vmem_limit_bytes=vmem)


def _const_spec(shape):
    nd = len(shape)
    return pl.BlockSpec(shape, lambda *_: (0,) * nd, pipeline_mode=pl.Buffered(1))


def _acc_spec(shape):
    nd = len(shape)
    return pl.BlockSpec(shape, lambda *_: (0,) * nd)


def _sigmoid(x):
    return jax.nn.sigmoid(x)


def _dot(a, b):
    return jnp.dot(a, b, preferred_element_type=f32)


def _dot_nt(a, b):
    return lax.dot_general(a, b, (((1,), (1,)), ((), ())), preferred_element_type=f32)


def _dot_tn(a, b):
    return lax.dot_general(a, b, (((0,), (0,)), ((), ())), preferred_element_type=f32)


def _rms_fwd(x, g, width):
    r = lax.rsqrt(jnp.sum(x * x, axis=-1, keepdims=True) * (1.0 / width) + EPS)
    return x * r * g, r


def _rms_bwd(dy, x, r, g, width):
    u = dy * g
    dx = r * u - x * (r * r * r) * (jnp.sum(u * x, axis=-1, keepdims=True) * (1.0 / width))
    return dx, dy * x * r


def _rope(b, c, sa, sb):
    return b * c + pltpu.roll(b, 32, 1) * sa + pltpu.roll(b, 96, 1) * sb


def _all_gather(shards, dtypes, name):
    n = len(shards)

    def body(*refs):
        in_refs, out_refs, stage = refs[:n], refs[n:2 * n], refs[2 * n:3 * n]
        send_sems, recv_sems, local_sems = refs[3 * n:]
        for w in range(n):
            stage[w][...] = in_refs[w][...].astype(stage[w].dtype)
        x, y, c = lax.axis_index("x"), lax.axis_index("y"), lax.axis_index("c")
        me, sibling = (x, y, c), (x, y, 1 - c)
        chips = [(1 - x, y), (x, 1 - y), (1 - x, 1 - y)]

        def slot(w, px, py, pc):
            return out_refs[w].at[4 * px + 2 * py + pc]

        def copy(w, k, block, to, src=None):
            return pltpu.make_async_remote_copy(
                src_ref=slot(w, *block) if src is None else src, dst_ref=slot(w, *block),
                send_sem=send_sems.at[w, k], recv_sem=recv_sems.at[w, k], device_id=to, device_id_type=MESH)

        first = []
        for j, chip in enumerate(chips):
            first += [copy(w, 1 + j, me, (*chip, c), src=stage[w]) for w in range(n)]
        first += [copy(w, 0, me, sibling, src=stage[w]) for w in range(n)]
        mine = [pltpu.make_async_copy(stage[w], slot(w, *me), local_sems.at[w]) for w in range(n)]
        for cp in first + mine:
            cp.start()
        passed = []
        for j, chip in enumerate(chips):
            for w in range(n):
                copy(w, 1 + j, (*chip, c), me).wait_recv()
                passed.append(copy(w, 4 + j, (*chip, c), sibling))
                passed[-1].start()
        for w in range(n):
            copy(w, 0, sibling, me).wait_recv()
        for j, chip in enumerate(chips):
            for w in range(n):
                copy(w, 4 + j, (*chip, 1 - c), me).wait_recv()
        for cp in first + passed:
            cp.wait_send()
        for cp in mine:
            cp.wait()

    return pl.pallas_call(
        body, name=name,
        out_shape=[jax.ShapeDtypeStruct((N_DEV, *s.shape), dt) for s, dt in zip(shards, dtypes)],
        in_specs=[pl.BlockSpec(memory_space=pltpu.VMEM)] * n,
        out_specs=[pl.BlockSpec(memory_space=pl.ANY)] * n,
        scratch_shapes=[pltpu.VMEM(s.shape, dt) for s, dt in zip(shards, dtypes)]
        + [pltpu.SemaphoreType.DMA((n, 7)), pltpu.SemaphoreType.DMA((n, 7)), pltpu.SemaphoreType.DMA((n,))],
        compiler_params=_cp(None, VMEM_LIMIT),
    )(*shards)


N_PEERS = N_DEV - 1
HBM_SPEC = pl.BlockSpec(memory_space=pltpu.HBM)
SEM_SPEC = pl.BlockSpec(memory_space=pltpu.SEMAPHORE)
DATAFLOW = pltpu.SideEffectType.DATAFLOW_SIDE_EFFECTING


def _me():
    return 4 * lax.axis_index("x") + 2 * lax.axis_index("y") + lax.axis_index("c")


def _peer(k):
    x, y, c = lax.axis_index("x"), lax.axis_index("y"), lax.axis_index("c")
    px = 1 - x if k & 4 else x
    py = 1 - y if k & 2 else y
    pc = 1 - c if k & 1 else c
    return (px, py, pc), 4 * px + 2 * py + pc


def _exchange_copies(src_refs, land_refs, send_sems, recv_sems, gather):
    cps = []
    me = _me()
    for k in range(1, N_DEV):
        peer, peer_idx = _peer(k)
        for w, land in enumerate(land_refs):
            src = land.at[me] if gather else src_refs[w].at[peer_idx]
            dst = land.at[me] if gather else land.at[k - 1]
            cps.append(pltpu.make_async_remote_copy(
                src_ref=src, dst_ref=dst, send_sem=send_sems.at[N_PEERS * w + k - 1], recv_sem=recv_sems.at[N_PEERS * w + k - 1],
                device_id=peer, device_id_type=MESH))
    return cps


def _exchange_start(srcs, lands, name):
    n_src, n = len(srcs), len(lands)

    def body(*refs):
        src_refs, land_refs = refs[:n_src], refs[n_src:n_src + n]
        send_sems, recv_sems = refs[n_src + n], refs[n_src + n + 1]
        token = refs[-1]
        for cp in _exchange_copies(src_refs, land_refs, send_sems, recv_sems, gather=not n_src):
            cp.start()
        token[...] = jnp.zeros_like(token)

    arrays = [pltpu.with_memory_space_constraint(a, pltpu.HBM) for a in (*srcs, *lands)]
    outs = pl.pallas_call(
        body, name=name,
        out_shape=(pltpu.SemaphoreType.DMA((n * N_PEERS,)), pltpu.SemaphoreType.DMA((n * N_PEERS,)),
                   *[pltpu.HBM(a.shape, a.dtype) for a in arrays], jax.ShapeDtypeStruct((8, LANES), f32)),
        in_specs=[HBM_SPEC] * len(arrays),
        out_specs=(SEM_SPEC, SEM_SPEC, *[HBM_SPEC] * len(arrays), pl.BlockSpec(memory_space=pltpu.VMEM)),
        input_output_aliases={i: 2 + i for i in range(len(arrays))},
        compiler_params=pltpu.CompilerParams(has_side_effects=DATAFLOW),
    )(*arrays)
    return (outs[0], outs[1], outs[2:2 + n_src], outs[2 + n_src:2 + n_src + n]), outs[-1]


def _exchange_wait(state, after, name):
    send_sems, recv_sems, srcs, lands = state
    n_src, n = len(srcs), len(lands)

    def body(*refs):
        src_refs, land_refs = refs[:n_src], refs[n_src:n_src + n]
        send_ref, recv_ref = refs[n_src + n], refs[n_src + n + 1]
        for cp in _exchange_copies(src_refs, land_refs, send_ref, recv_ref, gather=not n_src):
            cp.wait_send()
            cp.wait_recv()

    arrays = (*srcs, *lands)
    outs = pl.pallas_call(
        body, name=name,
        out_shape=tuple(pltpu.HBM(a.shape, a.dtype) for a in arrays),
        in_specs=[HBM_SPEC] * len(arrays) + [SEM_SPEC, SEM_SPEC] + [pl.BlockSpec(memory_space=pl.ANY)] * len(after),
        out_specs=tuple([HBM_SPEC] * len(arrays)),
        input_output_aliases={i: i for i in range(len(arrays))},
        compiler_params=pltpu.CompilerParams(has_side_effects=DATAFLOW),
    )(*arrays, send_sems, recv_sems, *after)
    return outs[n_src:]


def _cast_to_slot(shards, me_idx, after):
    n = len(shards)

    def body(i_ref, *refs):
        for w in range(n):
            refs[n + 1 + w][...] = refs[w][...].astype(bf16)

    return pl.pallas_call(
        body, name="cast_to_slot",
        grid_spec=pltpu.PrefetchScalarGridSpec(
            num_scalar_prefetch=1, grid=(1,),
            in_specs=[pl.BlockSpec(s.shape, lambda i, m: (0, 0)) for s in shards] + [pl.BlockSpec(memory_space=pl.ANY)],
            out_specs=[pl.BlockSpec((None, *s.shape), lambda i, m: (m[0], 0, 0)) for s in shards]),
        out_shape=[jax.ShapeDtypeStruct((N_DEV, *s.shape), bf16) for s in shards],
        compiler_params=_cp(("arbitrary",), VMEM_LIMIT),
    )(me_idx, *shards, after)


def _row_block(rows, n_blocks):
    return (rows // n_blocks, True) if rows % (16 * n_blocks) == 0 else (rows, False)


def _adam_math(w, g, m, v):
    m = ADAM_B1 * m + (1.0 - ADAM_B1) * g
    v = ADAM_B2 * v + (1.0 - ADAM_B2) * (g * g)
    m_hat = m / (1.0 - ADAM_B1 ** ADAM_STEP)
    v_hat = v / (1.0 - ADAM_B2 ** ADAM_STEP)
    delta = -ADAM_LR * (m_hat / (jnp.sqrt(v_hat) + ADAM_EPS) + ADAM_WD * w)
    return delta, m, v


def _adam_shards(me_idx, blocks, lands, ws, ms, vs, n_blocks, name, after=()):
    n = len(blocks)

    def body(i_ref, *refs):
        ins, outs = refs[:5 * n], refs[5 * n + len(after):]
        for w in range(n):
            g_ref, b_ref, w_ref, m_ref, v_ref = (ins[t * n + w] for t in range(5))
            g = g_ref[...].astype(f32)
            for k in range(N_PEERS):
                g = g + b_ref[k].astype(f32)
            if len(w_ref.shape) == 2:
                pieces = [(slice(None), g)]
            else:
                pieces = [(a, g[2 * a:2 * a + 2]) for a in range(2)]
            for at, gp in pieces:
                vals = (gp,) + _adam_math(w_ref[at], gp, m_ref[at], v_ref[at])
                for t, val in enumerate(vals):
                    outs[4 * w + t][at] = val

    specs = [[] for _ in range(5)]
    out_specs, out_shape = [], []
    for g, wt in zip(blocks, ws):
        rows, cols = g.shape[1:]
        rb, cut = _row_block(rows, n_blocks)
        specs[0].append(pl.BlockSpec((None, rb, cols), functools.partial(lambda i, s, cut: (s[0], i if cut else 0, 0), cut=cut)))
        specs[1].append(pl.BlockSpec((N_PEERS, rb, cols), functools.partial(lambda i, s, cut: (0, i if cut else 0, 0), cut=cut)))
        if wt.shape[0] == 1:
            shard = pl.BlockSpec((None, rb, cols), functools.partial(lambda i, s, cut: (0, i if cut else 0, 0), cut=cut))
        else:
            shard = pl.BlockSpec(wt.shape, functools.partial(lambda i, s, nd: (0,) * nd, nd=wt.ndim))
        for t in (2, 3, 4):
            specs[t].append(shard)
        out_specs += [shard] * 4
        out_shape += [jax.ShapeDtypeStruct(wt.shape, f32)] * 4
    outs = pl.pallas_call(
        body, name=name,
        grid_spec=pltpu.PrefetchScalarGridSpec(
            num_scalar_prefetch=1, grid=(n_blocks,), in_specs=sum(specs, []) + [pl.BlockSpec(memory_space=pl.ANY)] * len(after),
            out_specs=out_specs),
        out_shape=out_shape,
        compiler_params=_cp(("arbitrary",), VMEM_LIMIT),
    )(me_idx, *blocks, *lands, *ws, *ms, *vs, *after)
    return [outs[4 * w:4 * w + 4] for w in range(n)]


def _adam_gains(parts, ws, ms, vs):
    n = len(ws)

    def body(p_ref, *refs):
        ins, outs = refs[:3 * n], refs[3 * n:]
        g_all = p_ref[0]
        for k in range(1, N_DEV):
            g_all = g_all + p_ref[k]
        for w, (off, lanes) in enumerate(SMALL.values()):
            w_ref, m_ref, v_ref = ins[w], ins[n + w], ins[2 * n + w]
            if len(w_ref.shape) == 2:
                pieces = [(slice(None), off, lanes)]
            else:
                pieces = [((slice(None), h), off + LANES * h, LANES) for h in range(w_ref.shape[1])]
            for at, o, ln in pieces:
                g = g_all[:, o:o + ln]
                vals = (g,) + _adam_math(w_ref[at], g, m_ref[at], v_ref[at])
                for t, val in enumerate(vals):
                    outs[4 * w + t][at] = val
        outs[4 * n][...] = g_all[:, LOSS_OFF:LOSS_OFF + LANES]

    out_shape = sum([[jax.ShapeDtypeStruct(w.shape, f32)] * 4 for w in ws], []) + [jax.ShapeDtypeStruct((1, LANES), f32)]
    outs = pl.pallas_call(body, name="adamw_gains", out_shape=out_shape)(parts, *ws, *ms, *vs)
    return [outs[4 * w:4 * w + 4] for w in range(n)], outs[4 * n]


def _fwd_in(x, g_mix, wz, tm):
    s, d = x.shape

    def body(x_ref, g_ref, w_ref, h_ref, z_ref):
        h, _ = _rms_fwd(x_ref[...], g_ref[...], d)
        hb = h.astype(bf16)
        h_ref[...] = hb
        z_ref[...] = _dot(hb, w_ref[...])

    return pl.pallas_call(
        body, name="fwd_in", grid=(s // tm,),
        in_specs=[pl.BlockSpec((tm, d), lambda i: (i, 0)), _const_spec((1, d)), _const_spec((d, Z_W))],
        out_specs=[pl.BlockSpec((tm, d), lambda i: (i, 0)), pl.BlockSpec((tm, Z_W), lambda i: (i, 0))],
        out_shape=[jax.ShapeDtypeStruct((s, d), bf16), jax.ShapeDtypeStruct((s, Z_W), f32)],
        compiler_params=_cp(("parallel",), VMEM_LIMIT),
    )(x, g_mix, wz)


def _mla_qk_fwd(cq, ckv, kr, g_qa, g_kva, wqb, wkvb, g_qn, g_kn):
    cqn, rq = _rms_fwd(cq, g_qa, Q_LORA)
    ckvn, rkv = _rms_fwd(ckv, g_kva, KV_LORA)
    cqn_b, ckvn_b = cqn.astype(bf16), ckvn.astype(bf16)
    q0 = _dot(cqn_b, wqb)
    kv0 = _dot(ckvn_b, wkvb)
    return cqn_b, rq, ckvn_b, rkv, q0, kv0


def _fwd_mla_proj(z, cosb, sina, sinb, g_qa, g_kva, wqb, wkvb, g_qn, g_kn, tm):
    s = z.shape[0]
    hh = MLA_HEADS

    def body(cq_ref, ckv_ref, kr_ref, c_ref, sa_ref, sb_ref, gqa_ref, gkva_ref, wqb_ref, wkvb_ref, gqn_ref, gkn_ref,
             q_ref, k_ref, v_ref):
        _, _, _, _, q0, kv0 = _mla_qk_fwd(cq_ref[...], ckv_ref[...], kr_ref[...], gqa_ref[...], gkva_ref[...],
                                          wqb_ref[...], wkvb_ref[...], gqn_ref[...], gkn_ref[...])
        kr = kr_ref[...]
        c, sa, sb = c_ref[...], sa_ref[...], sb_ref[...]
        gqn, gkn = gqn_ref[...], gkn_ref[...]
        kr_sq = jnp.sum(kr * kr, axis=-1, keepdims=True)
        for h in range(hh):
            qh = q0[:, QK_PAD * h:QK_PAD * (h + 1)]
            qn, _ = _rms_fwd(qh, gqn, QK_HEAD)
            q_ref[h, :, 0:128] = qn[:, 0:128].astype(bf16)
            q_ref[h, :, 128:256] = _rope(qn[:, 128:256], c, sa, sb).astype(bf16)
            kn_ = kv0[:, 256 * h:256 * h + 128]
            rk = lax.rsqrt((jnp.sum(kn_ * kn_, axis=-1, keepdims=True) + kr_sq) * (1.0 / QK_HEAD) + EPS)
            k_ref[h, :, 0:128] = (kn_ * rk * gkn[:, 0:128]).astype(bf16)
            k_ref[h, :, 128:256] = _rope(kr * rk * gkn[:, 128:256], c, sa, sb).astype(bf16)
            v_ref[h] = kv0[:, 256 * h + 128:256 * h + 256].astype(bf16)

    row128 = pl.BlockSpec((tm, 128), lambda i: (i, 0))
    return pl.pallas_call(
        body, name="fwd_mla_proj", grid=(s // tm,),
        in_specs=[pl.BlockSpec((tm, 256), lambda i: (i, Z_CQ // 256)), pl.BlockSpec((tm, 256), lambda i: (i, Z_CKV // 256)),
                  pl.BlockSpec((tm, 128), lambda i: (i, Z_KR // 128)), row128, row128, row128,
                  _const_spec((1, 256)), _const_spec((1, 256)), _const_spec((256, 1024)), _const_spec((256, 1024)),
                  _const_spec((1, 256)), _const_spec((1, 256))],
        out_specs=[pl.BlockSpec((hh, tm, QK_PAD), lambda i: (0, i, 0)), pl.BlockSpec((hh, tm, QK_PAD), lambda i: (0, i, 0)),
                   pl.BlockSpec((hh, tm, V_HEAD), lambda i: (0, i, 0))],
        out_shape=[jax.ShapeDtypeStruct((hh, s, QK_PAD), bf16), jax.ShapeDtypeStruct((hh, s, QK_PAD), bf16),
                   jax.ShapeDtypeStruct((hh, s, V_HEAD), bf16)],
        compiler_params=_cp(("parallel",), VMEM_LIMIT),
    )(z, z, z, cosb, sina, sinb, g_qa, g_kva, wqb, wkvb, g_qn, g_kn)


def _fwd_attn(q, k, v, tq):
    hh, s, _ = q.shape

    n_sub = max(1, tq // ATTN_SUB_ROWS)

    def body(q_ref, k_ref, v_ref, o_ref):
        for t in range(n_sub):
            rows = slice(t * (tq // n_sub), (t + 1) * (tq // n_sub))
            sc = _dot_nt(q_ref[rows, :], k_ref[...])
            p = jnp.exp2((sc - jnp.max(sc, axis=-1, keepdims=True)) * (ATTN_SCALE * LOG2_E))
            l = jnp.sum(p, axis=-1, keepdims=True)
            o_ref[rows, :] = (_dot(p.astype(bf16), v_ref[...]) * (1.0 / l)).astype(bf16)

    return pl.pallas_call(
        body, name="fwd_attn", grid=(hh, s // tq),
        in_specs=[pl.BlockSpec((None, tq, QK_PAD), lambda h, i: (h, i, 0)),
                  pl.BlockSpec((None, s, QK_PAD), lambda h, i: (h, 0, 0)),
                  pl.BlockSpec((None, s, V_HEAD), lambda h, i: (h, 0, 0))],
        out_specs=pl.BlockSpec((tq, V_HEAD), lambda h, i: (i, h)),
        out_shape=jax.ShapeDtypeStruct((s, hh * V_HEAD), bf16),
        compiler_params=_cp(("parallel", "parallel"), VMEM_LIMIT),
    )(q, k, v)


def _split3(x):
    hi = x.astype(bf16)
    r1 = x - hi.astype(f32)
    mid = r1.astype(bf16)
    lo = (r1 - mid.astype(f32)).astype(bf16)
    return jnp.concatenate([hi, mid, lo], axis=-1)


def _tri_sum(tri, x):
    y = _dot(tri, _split3(x))
    return y[:, 0:128] + y[:, 128:256] + y[:, 256:384]


GLA_GROUP = 4
GLA_ROWS = GLA_GROUP * CHUNK
GLA_HEADS_PER_STEP = 2


def _gla_masks(rev):
    row = lax.broadcasted_iota(jnp.int32, (GLA_ROWS, GLA_ROWS), 0)
    col = lax.broadcasted_iota(jnp.int32, (GLA_ROWS, GLA_ROWS), 1)
    shift = CHUNK.bit_length() - 1
    same = (jnp.right_shift(row, shift) == jnp.right_shift(col, shift)).astype(f32)
    lower, upper = (row >= col).astype(f32) * same, (row <= col).astype(f32) * same
    keep, keep_t = (upper, lower) if rev else (lower, upper)
    chunk_of = jnp.right_shift(lax.broadcasted_iota(jnp.int32, (GLA_ROWS, 1), 0), shift)
    return keep, keep.astype(bf16), keep_t.astype(bf16), [(chunk_of == c).astype(f32) for c in range(GLA_GROUP)]


def _gla_gates(hq, hf, lower):
    sg = _sigmoid(hf)
    f = lower + (1.0 - lower) * sg
    return hq * _sigmoid(hq), 1.0 - f, jnp.log(f), f, sg


def _gla_last_mid(b, rev):
    b3 = b.reshape(GLA_GROUP, CHUNK, 128)
    last, mid = (0, CHUNK // 2) if rev else (CHUNK - 1, CHUNK // 2 - 1)
    return b3[:, last:last + 1, :], b3[:, mid:mid + 1, :]


def _gla_per_row(per_chunk):
    return jnp.broadcast_to(per_chunk, (GLA_GROUP, CHUNK, 128)).reshape(GLA_ROWS, 128)


def _gla_block_diag(x, row_masks):
    return jnp.concatenate([(x * m).astype(bf16) for m in row_masks], axis=-1)


def _gla_diag(y):
    return jnp.concatenate([y[CHUNK * c:CHUNK * (c + 1), 128 * c:128 * (c + 1)] for c in range(GLA_GROUP)], axis=0)


def _gla_rows(n, n_groups, rev):
    ne = n_groups - 1 - n if rev else n
    return pl.ds(pl.multiple_of(ne * GLA_ROWS, GLA_ROWS), GLA_ROWS), ne * GLA_GROUP


def _gla_scan_order(rev):
    return tuple(reversed(range(GLA_GROUP))) if rev else tuple(range(GLA_GROUP))


def _fwd_gla(z, lb4):
    s = z.shape[0]
    n_groups = s // GLA_ROWS
    assert n_groups % 2 == 0
    hp = GLA_HEADS_PER_STEP
    chains = [(hh, rev) for hh in range(hp) for rev in (False, True)]

    def body(hq_ref, hff_ref, hfb_ref, hi_ref, lb_ref, o_ref, st_ref):
        st_ref[...] = jnp.zeros_like(st_ref)
        masks = {rev: _gla_masks(rev) for rev in (False, True)}
        lowers = [_sigmoid(lb_ref[int(rev):int(rev) + 1, 128 * hh:128 * (hh + 1)]
                           - lb_ref[2 + int(rev):3 + int(rev), 128 * hh:128 * (hh + 1)]) for hh, rev in chains]

        def make_step(first):
            def step(n, carry):
                for ci, (hh, rev) in enumerate(chains):
                    cols = slice(128 * hh, 128 * (hh + 1))
                    rows, _ = _gla_rows(n, n_groups, rev)
                    maskf, tri, _, row_masks = masks[rev]
                    hf_ref = hfb_ref if rev else hff_ref
                    q, k, logf, _, _ = _gla_gates(hq_ref[rows, cols], hf_ref[rows, cols], lowers[ci])
                    vb = hi_ref[rows, cols].astype(bf16)
                    b = _tri_sum(tri, logf)
                    b_last3, b_mid3 = _gla_last_mid(b, rev)
                    b_last, b_mid = _gla_per_row(b_last3), _gla_per_row(b_mid3)
                    qi = (q * jnp.exp(b - b_mid)).astype(bf16)
                    ki = (k * jnp.exp(b_mid - b)).astype(bf16)
                    a = (_dot_nt(qi, ki) * maskf).astype(bf16)
                    kv = _dot_tn(vb, _gla_block_diag(k * jnp.exp(b_last - b), row_masks))
                    decay3 = jnp.exp(b_last3)
                    st = st_ref[ci]
                    before = [None] * GLA_GROUP
                    for c in _gla_scan_order(rev):
                        before[c] = st.astype(bf16)
                        st = st * decay3[c] + kv[:, 128 * c:128 * (c + 1)]
                    st_ref[ci] = st
                    inter = _dot_nt((q * jnp.exp(b)).astype(bf16), jnp.concatenate(before, axis=0))
                    o = _dot(a, vb) + _gla_diag(inter)
                    if first:
                        o_ref[rows, cols] = o
                    else:
                        o_ref[rows, cols] += o
                return carry
            return step

        lax.fori_loop(0, n_groups // 2, make_step(True), 0)
        lax.fori_loop(n_groups // 2, n_groups, make_step(False), 0)

    w = 128 * hp
    col = lambda base: pl.BlockSpec((s, w), lambda h: (0, base // w + h))
    return pl.pallas_call(
        body, name="fwd_gla", grid=(HG_HEADS // hp,),
        in_specs=[col(Z_HQ), col(Z_HFF), col(Z_HFB), col(Z_HI), pl.BlockSpec((4, w), lambda h: (0, h))],
        out_specs=pl.BlockSpec((s, w), lambda h: (0, h)),
        out_shape=jax.ShapeDtypeStruct((s, HG_HEADS * 128), f32),
        scratch_shapes=[pltpu.VMEM((len(chains), 128, 128), f32)],
        compiler_params=_cp(("parallel",), VMEM_LIMIT),
    )(z, z, z, z, lb4)


def _hg_out(o, hg, g_hgo):
    outs, ons, rs = [], [], []
    for h in range(HG_HEADS):
        oh = o[:, 128 * h:128 * (h + 1)]
        on, r = _rms_fwd(oh, g_hgo[:, 128 * h:128 * (h + 1)], 128)
        ons.append(on)
        rs.append(r)
    on = jnp.concatenate(ons, axis=-1)
    sg = _sigmoid(hg)
    return on * (hg * sg), on, rs, sg


def _fwd_mix(a, o, z, g_hgo, x, w_o, tm):
    s, d = x.shape

    def body(a_ref, o_ref, hg_ref, g_ref, x_ref, w_ref, x2_ref, cat_ref):
        r, _, _, _ = _hg_out(o_ref[...], hg_ref[...], g_ref[...])
        cat = jnp.concatenate([a_ref[...], r.astype(bf16)], axis=-1)
        cat_ref[...] = cat
        x2_ref[...] = x_ref[...] + _dot(cat, w_ref[...])

    row512 = pl.BlockSpec((tm, 512), lambda i: (i, 0))
    rowd = pl.BlockSpec((tm, d), lambda i: (i, 0))
    return pl.pallas_call(
        body, name="fwd_mix", grid=(s // tm,),
        in_specs=[row512, row512, pl.BlockSpec((tm, 512), lambda i: (i, Z_HG // 512)), _const_spec((1, 512)), rowd,
                  _const_spec((d, d))],
        out_specs=[rowd, rowd],
        out_shape=[jax.ShapeDtypeStruct((s, d), f32), jax.ShapeDtypeStruct((s, d), bf16)],
        compiler_params=_cp(("parallel",), VMEM_LIMIT),
    )(a, o, z, g_hgo, x, w_o)


def _fwd_ffn(x2, g_ffn, w_gate, w_up, w_down, tm):
    s, d = x2.shape

    def body(x_ref, g_ref, wg_ref, wu_ref, wd_ref, x3_ref, gp_ref, up_ref):
        x = x_ref[...]
        h, _ = _rms_fwd(x, g_ref[...], d)
        hb = h.astype(bf16)
        gp = _dot(hb, wg_ref[...])
        up = _dot(hb, wu_ref[...])
        gp_ref[...] = gp
        up_ref[...] = up
        act = (gp * _sigmoid(gp) * up).astype(bf16)
        x3_ref[...] = x + _dot(act, wd_ref[...])

    rowd = pl.BlockSpec((tm, d), lambda i: (i, 0))
    rowf = pl.BlockSpec((tm, D_FF), lambda i: (i, 0))
    return pl.pallas_call(
        body, name="fwd_ffn", grid=(s // tm,),
        in_specs=[rowd, _const_spec((1, d)), _const_spec((d, D_FF)), _const_spec((d, D_FF)), _const_spec((D_FF, d))],
        out_specs=[rowd, rowf, rowf],
        out_shape=[jax.ShapeDtypeStruct((s, d), f32), jax.ShapeDtypeStruct((s, D_FF), f32),
                   jax.ShapeDtypeStruct((s, D_FF), f32)],
        compiler_params=_cp(("parallel",), VMEM_LIMIT),
    )(x2, g_ffn, w_gate, w_up, w_down)


def _ple_loss_fwd_bwd(x3, g_ple, w_pg, p, w_pp, target, tm):
    s, d = x3.shape

    def body(x_ref, g_ref, wg_ref, p_ref, wp_ref, t_ref, dx_ref, h_ref, dpre_ref, dpp_ref, dg_ref, loss_ref):
        @pl.when(pl.program_id(0) == 0)
        def _():
            dg_ref[...] = jnp.zeros_like(dg_ref)
            loss_ref[...] = jnp.zeros_like(loss_ref)

        x = x_ref[...]
        g = g_ref[...]
        h, r = _rms_fwd(x, g, d)
        hb = h.astype(bf16)
        gate = _sigmoid(_dot(hb, wg_ref[...]))
        pp = _dot(p_ref[...].astype(bf16), wp_ref[...])
        e = x + gate * pp - t_ref[...]
        loss_ref[...] += 0.5 * jnp.sum(e * e) * (1.0 / d)
        dy = e * (1.0 / d)
        dpre = (dy * pp * gate * (1.0 - gate)).astype(bf16)
        dx, dgx = _rms_bwd(_dot_nt(dpre, wg_ref[...]), x, r, g, d)
        dx_ref[...] = dy + dx
        dg_ref[...] += jnp.sum(dgx, axis=0, keepdims=True)
        h_ref[...] = hb
        dpre_ref[...] = dpre
        dpp_ref[...] = (dy * gate).astype(bf16)

    rowd = pl.BlockSpec((tm, d), lambda i: (i, 0))
    return pl.pallas_call(
        body, name="ple_loss_fwd_bwd", grid=(s // tm,),
        in_specs=[rowd, _const_spec((1, d)), _const_spec((d, d)), pl.BlockSpec((tm, PLE_DIM), lambda i: (i, 0)),
                  _const_spec((PLE_DIM, d)), rowd],
        out_specs=[rowd, rowd, rowd, rowd, _acc_spec((1, d)), _acc_spec((8, 128))],
        out_shape=[jax.ShapeDtypeStruct((s, d), f32), jax.ShapeDtypeStruct((s, d), bf16), jax.ShapeDtypeStruct((s, d), bf16),
                   jax.ShapeDtypeStruct((s, d), bf16), jax.ShapeDtypeStruct((1, d), f32), jax.ShapeDtypeStruct((8, 128), f32)],
        compiler_params=_cp(("arbitrary",), VMEM_LIMIT),
    )(x3, g_ple, w_pg, p, w_pp, target)


def _bwd_ffn(d3, x2, gp, up, g_ffn, w_gate, w_up, w_down, tm):
    s, d = x2.shape

    def body(d3_ref, x_ref, gp_ref, up_ref, g_ref, wg_ref, wu_ref, wd_ref, d2_ref, h_ref, act_ref, dgp_ref, dup_ref, dg_ref):
        @pl.when(pl.program_id(0) == 0)
        def _():
            dg_ref[...] = jnp.zeros_like(dg_ref)

        x = x_ref[...]
        g = g_ref[...]
        d3 = d3_ref[...]
        h, r = _rms_fwd(x, g, d)
        h_ref[...] = h.astype(bf16)
        gp, up = gp_ref[...], up_ref[...]
        sg = _sigmoid(gp)
        silu = gp * sg
        act_ref[...] = (silu * up).astype(bf16)
        dact = _dot_nt(d3.astype(bf16), wd_ref[...])
        dgp = (dact * up * (sg * (1.0 + gp * (1.0 - sg)))).astype(bf16)
        dup = (dact * silu).astype(bf16)
        dgp_ref[...] = dgp
        dup_ref[...] = dup
        dh = _dot_nt(dgp, wg_ref[...]) + _dot_nt(dup, wu_ref[...])
        dx, dgx = _rms_bwd(dh, x, r, g, d)
        d2_ref[...] = d3 + dx
        dg_ref[...] += jnp.sum(dgx, axis=0, keepdims=True)

    rowd = pl.BlockSpec((tm, d), lambda i: (i, 0))
    rowf = pl.BlockSpec((tm, D_FF), lambda i: (i, 0))
    return pl.pallas_call(
        body, name="bwd_ffn", grid=(s // tm,),
        in_specs=[rowd, rowd, rowf, rowf, _const_spec((1, d)), _const_spec((d, D_FF)), _const_spec((d, D_FF)),
                  _const_spec((D_FF, d))],
        out_specs=[rowd, rowd, rowf, rowf, rowf, _acc_spec((1, d))],
        out_shape=[jax.ShapeDtypeStruct((s, d), f32), jax.ShapeDtypeStruct((s, d), bf16)] + [jax.ShapeDtypeStruct((s, D_FF), bf16)] * 3
        + [jax.ShapeDtypeStruct((1, d), f32)],
        compiler_params=_cp(("arbitrary",), VMEM_LIMIT),
    )(d3, x2, gp, up, g_ffn, w_gate, w_up, w_down)


def _bwd_mix(d2, w_o, o, z, g_hgo, tm):
    s, d = d2.shape

    def body(d2_ref, w_ref, o_ref, hg_ref, g_ref, da_ref, do_ref, dhg_ref, dg_ref):
        @pl.when(pl.program_id(0) == 0)
        def _():
            dg_ref[...] = jnp.zeros_like(dg_ref)

        dcat = _dot_nt(d2_ref[...].astype(bf16), w_ref[...])
        da_ref[...] = dcat[:, 0:512].astype(bf16)
        dr = dcat[:, 512:1024]
        o, hg, g = o_ref[...], hg_ref[...], g_ref[...]
        _, on, rs, sg = _hg_out(o, hg, g)
        dhg_ref[...] = (dr * on * (sg * (1.0 + hg * (1.0 - sg)))).astype(bf16)
        don = dr * (hg * sg)
        dgs = []
        for h in range(HG_HEADS):
            cols = slice(128 * h, 128 * (h + 1))
            dx, dgx = _rms_bwd(don[:, cols], o[:, cols], rs[h], g[:, cols], 128)
            do_ref[:, cols] = dx
            dgs.append(jnp.sum(dgx, axis=0, keepdims=True))
        dg_ref[...] += jnp.concatenate(dgs, axis=-1)

    row512 = pl.BlockSpec((tm, 512), lambda i: (i, 0))
    return pl.pallas_call(
        body, name="bwd_mix", grid=(s // tm,),
        in_specs=[pl.BlockSpec((tm, d), lambda i: (i, 0)), _const_spec((d, d)), row512,
                  pl.BlockSpec((tm, 512), lambda i: (i, Z_HG // 512)), _const_spec((1, 512))],
        out_specs=[row512, row512, row512, _acc_spec((1, 512))],
        out_shape=[jax.ShapeDtypeStruct((s, 512), bf16), jax.ShapeDtypeStruct((s, 512), f32), jax.ShapeDtypeStruct((s, 512), bf16),
                   jax.ShapeDtypeStruct((1, 512), f32)],
        compiler_params=_cp(("arbitrary",), VMEM_LIMIT),
    )(d2, w_o, o, z, g_hgo)


def _bwd_gla(z, lb4, do):
    s = z.shape[0]
    n_chunks = s // CHUNK
    n_groups = s // GLA_ROWS
    assert n_groups % 2 == 0

    def body(hq_ref, hff_ref, hfb_ref, hi_ref, lb_ref, do_ref, dhq_ref, dhff_ref, dhfb_ref, dhi_ref, dlb_ref,
             st_all, b_all, dst_ref, dq_acc, dv_acc, dlow_ref):
        dirs = (False, True)
        masks = [_gla_masks(rev) for rev in dirs]
        lowers = [_sigmoid(lb_ref[int(rev):int(rev) + 1, :] - lb_ref[2 + int(rev):3 + int(rev), :]) for rev in dirs]
        hf_refs, dhf_refs = (hff_ref, hfb_ref), (dhff_ref, dhfb_ref)

        def fwd_step(n, sts):
            new = []
            for d, rev in enumerate(dirs):
                rows, chunk0 = _gla_rows(n, n_groups, rev)
                _, k, logf, _, _ = _gla_gates(hq_ref[rows, :], hf_refs[d][rows, :], lowers[d])
                b = _tri_sum(masks[d][1], logf)
                b_last3, _ = _gla_last_mid(b, rev)
                b_all[d, rows, :] = b
                kv = _dot_tn(hi_ref[rows, :].astype(bf16), _gla_block_diag(k * jnp.exp(_gla_per_row(b_last3) - b), masks[d][3]))
                decay3 = jnp.exp(b_last3)
                st = sts[d]
                for c in _gla_scan_order(rev):
                    st_all[d, chunk0 + c] = st
                    st = st * decay3[c] + kv[:, 128 * c:128 * (c + 1)]
                new.append(st)
            return tuple(new)

        zero = jnp.zeros((128, 128), f32)
        lax.fori_loop(0, n_groups, fwd_step, (zero, zero))

        dst_ref[...] = jnp.zeros_like(dst_ref)
        dlow_ref[...] = jnp.zeros_like(dlow_ref)

        def make_bwd_step(first):
            def bwd_step(j, carry):
                n = n_groups - 1 - j
                for d, rev in enumerate(dirs):
                    maskf, _, tri_t, row_masks = masks[d]
                    lower = lowers[d]
                    rows, chunk0 = _gla_rows(n, n_groups, rev)
                    hq, hf = hq_ref[rows, :], hf_refs[d][rows, :]
                    q, k, _, f, sg = _gla_gates(hq, hf, lower)
                    v = hi_ref[rows, :]
                    dout = do_ref[rows, :]
                    b = b_all[d, rows, :]
                    b_last3, b_mid3 = _gla_last_mid(b, rev)
                    b_last, b_mid = _gla_per_row(b_last3), _gla_per_row(b_mid3)
                    e1, e2, e3, e4 = jnp.exp(b - b_mid), jnp.exp(b_mid - b), jnp.exp(b_last - b), jnp.exp(b)
                    decay3 = jnp.exp(b_last3)
                    qi, ki, kt, qt = q * e1, k * e2, k * e3, q * e4
                    qib, kib, ktb = qi.astype(bf16), ki.astype(bf16), kt.astype(bf16)
                    vb, dob = v.astype(bf16), dout.astype(bf16)
                    a = (_dot_nt(qib, kib) * maskf).astype(bf16)
                    da = (_dot_nt(dob, vb) * maskf).astype(bf16)
                    dqi = _dot(da, kib)
                    dki = _dot_tn(da, qib)
                    into_state = _dot_tn(dob, _gla_block_diag(qt, row_masks))
                    dst = dst_ref[d]
                    sts, dsts, ddecay = [None] * GLA_GROUP, [None] * GLA_GROUP, [None] * GLA_GROUP
                    for c in reversed(_gla_scan_order(rev)):
                        sts[c] = st_all[d, chunk0 + c]
                        dsts[c] = dst.astype(bf16)
                        ddecay[c] = jnp.sum(dst * sts[c], axis=0, keepdims=True)[None]
                        dst = dst * decay3[c] + into_state[:, 128 * c:128 * (c + 1)]
                    dst_ref[d] = dst
                    dv = _dot_tn(a, dob) + _gla_diag(_dot_nt(ktb, jnp.concatenate(dsts, axis=0)))
                    dqt = _gla_diag(_dot(dob, jnp.concatenate([x.astype(bf16) for x in sts], axis=-1)))
                    dkt = _gla_diag(_dot(vb, jnp.concatenate(dsts, axis=-1)))
                    dq = dqi * e1 + dqt * e4
                    dk = dki * e2 + dkt * e3
                    db = dqi * qi - dki * ki + dqt * qt - dkt * kt
                    dlast3 = (jnp.sum((dkt * kt).reshape(GLA_GROUP, CHUNK, 128), axis=1, keepdims=True)
                              + jnp.concatenate(ddecay, axis=0) * decay3)
                    dlogf = _tri_sum(tri_t, db) + _gla_per_row(dlast3)
                    df = dlogf / f - dk
                    dhf_refs[d][rows, :] = (df * (1.0 - lower) * sg * (1.0 - sg)).astype(bf16)
                    dlow_ref[d:d + 1, :] += jnp.sum(df * (1.0 - sg), axis=0, keepdims=True)
                    sq = _sigmoid(hq)
                    dhq = dq * (sq * (1.0 + hq * (1.0 - sq)))
                    if first:
                        dq_acc[rows, :] = dhq
                        dv_acc[rows, :] = dv
                    else:
                        dhq_ref[rows, :] = (dq_acc[rows, :] + dhq).astype(bf16)
                        dhi_ref[rows, :] = (dv_acc[rows, :] + dv).astype(bf16)
                return carry
            return bwd_step

        lax.fori_loop(0, n_groups // 2, make_bwd_step(True), 0)
        lax.fori_loop(n_groups // 2, n_groups, make_bwd_step(False), 0)

        for d in range(2):
            dl = dlow_ref[d:d + 1, :] * lowers[d] * (1.0 - lowers[d])
            dlb_ref[d:d + 1, :] = dl
            dlb_ref[2 + d:3 + d, :] = -dl

    col = lambda base: pl.BlockSpec((s, 128), lambda h: (0, base // 128 + h))
    return pl.pallas_call(
        body, name="bwd_gla", grid=(HG_HEADS,),
        in_specs=[col(Z_HQ), col(Z_HFF), col(Z_HFB), col(Z_HI), pl.BlockSpec((4, 128), lambda h: (0, h)), col(0)],
        out_specs=[col(0), col(0), col(0), col(0), pl.BlockSpec((4, 128), lambda h: (0, h))],
        out_shape=[jax.ShapeDtypeStruct((s, 512), bf16)] * 4 + [jax.ShapeDtypeStruct((4, 512), f32)],
        scratch_shapes=[pltpu.VMEM((2, n_chunks, 128, 128), f32), pltpu.VMEM((2, s, 128), f32), pltpu.VMEM((2, 128, 128), f32),
                        pltpu.VMEM((s, 128), f32), pltpu.VMEM((s, 128), f32), pltpu.VMEM((2, 128), f32)],
        compiler_params=_cp(("parallel",), VMEM_LIMIT),
    )(z, z, z, z, lb4, do)


def _bwd_attn(q, k, v, da, tq):
    hh, s, _ = q.shape

    n_sub = max(1, tq // ATTN_SUB_ROWS)

    def body(q_ref, k_ref, v_ref, do_ref, dq_ref, dk_ref, dv_ref, w_all, ds_all):
        @pl.when(pl.program_id(1) == 0)
        def _():
            dk_ref[...] = jnp.zeros_like(dk_ref)
            dv_ref[...] = jnp.zeros_like(dv_ref)

        kb, vb = k_ref[...], v_ref[...]
        for t in range(n_sub):
            rows = slice(t * (tq // n_sub), (t + 1) * (tq // n_sub))
            sc = _dot_nt(q_ref[rows, :], kb)
            p = jnp.exp2((sc - jnp.max(sc, axis=-1, keepdims=True)) * (ATTN_SCALE * LOG2_E))
            w = p * (1.0 / jnp.sum(p, axis=-1, keepdims=True))
            dw = _dot_nt(do_ref[rows, :], vb)
            ds = (w * (dw - jnp.sum(dw * w, axis=-1, keepdims=True)) * ATTN_SCALE).astype(bf16)
            dq_ref[rows, :] = _dot(ds, kb)
            w_all[rows, :] = w.astype(bf16)
            ds_all[rows, :] = ds
        dk_ref[...] += _dot_tn(ds_all[...], q_ref[...])
        dv_ref[...] += _dot_tn(w_all[...], do_ref[...])

    return pl.pallas_call(
        body, name="bwd_attn", grid=(hh, s // tq),
        in_specs=[pl.BlockSpec((None, tq, QK_PAD), lambda h, i: (h, i, 0)),
                  pl.BlockSpec((None, s, QK_PAD), lambda h, i: (h, 0, 0)),
                  pl.BlockSpec((None, s, V_HEAD), lambda h, i: (h, 0, 0)),
                  pl.BlockSpec((tq, V_HEAD), lambda h, i: (i, h))],
        out_specs=[pl.BlockSpec((None, tq, QK_PAD), lambda h, i: (h, i, 0)),
                   pl.BlockSpec((None, s, QK_PAD), lambda h, i: (h, 0, 0)),
                   pl.BlockSpec((None, s, V_HEAD), lambda h, i: (h, 0, 0))],
        out_shape=[jax.ShapeDtypeStruct((hh, s, QK_PAD), f32), jax.ShapeDtypeStruct((hh, s, QK_PAD), f32),
                   jax.ShapeDtypeStruct((hh, s, V_HEAD), f32)],
        scratch_shapes=[pltpu.VMEM((tq, s), bf16), pltpu.VMEM((tq, s), bf16)],
        compiler_params=_cp(("parallel", "arbitrary"), VMEM_LIMIT),
    )(q, k, v, da)


def _bwd_mla_proj(z, dq, dk, dv, cosb, sina, sinb, g_qa, g_kva, wqb, wkvb, g_qn, g_kn, tm):
    s = z.shape[0]
    hh = MLA_HEADS

    def body(cq_ref, ckv_ref, kr_ref, dq_ref, dk_ref, dv_ref, c_ref, sa_ref, sb_ref, gqa_ref, gkva_ref, wqb_ref, wkvb_ref,
             gqn_ref, gkn_ref, dz_ref, cqn_ref, ckvn_ref, dq0_ref, dkv0_ref, dgqa_ref, dgkva_ref, dgqn_ref, dgkn_ref):
        @pl.when(pl.program_id(0) == 0)
        def _():
            for r in (dgqa_ref, dgkva_ref, dgqn_ref, dgkn_ref):
                r[...] = jnp.zeros_like(r)

        cq, ckv, kr = cq_ref[...], ckv_ref[...], kr_ref[...]
        gqa, gkva, gqn, gkn = gqa_ref[...], gkva_ref[...], gqn_ref[...], gkn_ref[...]
        cqn_b, rq, ckvn_b, rkv, q0, kv0 = _mla_qk_fwd(cq, ckv, kr, gqa, gkva, wqb_ref[...], wkvb_ref[...], gqn, gkn)
        cqn_ref[...] = cqn_b
        ckvn_ref[...] = ckvn_b
        c, sa, sb = c_ref[...], -sa_ref[...], -sb_ref[...]
        kr_sq = jnp.sum(kr * kr, axis=-1, keepdims=True)
        dkr = jnp.zeros_like(kr)
        dgqn = jnp.zeros((1, QK_PAD), f32)
        dgkn = jnp.zeros((1, QK_PAD), f32)
        for h in range(hh):
            qh = q0[:, QK_PAD * h:QK_PAD * (h + 1)]
            rh = lax.rsqrt(jnp.sum(qh * qh, axis=-1, keepdims=True) * (1.0 / QK_HEAD) + EPS)
            dqh = dq_ref[h]
            dqn = jnp.concatenate([dqh[:, 0:128], _rope(dqh[:, 128:256], c, sa, sb)], axis=-1)
            dq0h, dgx = _rms_bwd(dqn, qh, rh, gqn, QK_HEAD)
            dq0_ref[:, QK_PAD * h:QK_PAD * (h + 1)] = dq0h.astype(bf16)
            dgqn = dgqn + jnp.sum(dgx, axis=0, keepdims=True)

            kn_ = kv0[:, 256 * h:256 * h + 128]
            k0 = jnp.concatenate([kn_, kr], axis=-1)
            rk = lax.rsqrt((jnp.sum(kn_ * kn_, axis=-1, keepdims=True) + kr_sq) * (1.0 / QK_HEAD) + EPS)
            dkh = dk_ref[h]
            dkn = jnp.concatenate([dkh[:, 0:128], _rope(dkh[:, 128:256], c, sa, sb)], axis=-1)
            dk0, dgx = _rms_bwd(dkn, k0, rk, gkn, QK_HEAD)
            dgkn = dgkn + jnp.sum(dgx, axis=0, keepdims=True)
            dkv0_ref[:, 256 * h:256 * h + 128] = dk0[:, 0:128].astype(bf16)
            dkv0_ref[:, 256 * h + 128:256 * h + 256] = dv_ref[h].astype(bf16)
            dkr = dkr + dk0[:, 128:256]
        dgqn_ref[...] += dgqn
        dgkn_ref[...] += dgkn
        dcq, dgx = _rms_bwd(_dot_nt(dq0_ref[...], wqb_ref[...]), cq, rq, gqa, Q_LORA)
        dgqa_ref[...] += jnp.sum(dgx, axis=0, keepdims=True)
        dckv, dgx = _rms_bwd(_dot_nt(dkv0_ref[...], wkvb_ref[...]), ckv, rkv, gkva, KV_LORA)
        dgkva_ref[...] += jnp.sum(dgx, axis=0, keepdims=True)
        dz_ref[:, 0:256] = dcq.astype(bf16)
        dz_ref[:, 256:512] = dckv.astype(bf16)
        dz_ref[:, 512:640] = dkr.astype(bf16)

    row128 = pl.BlockSpec((tm, 128), lambda i: (i, 0))
    row256 = pl.BlockSpec((tm, 256), lambda i: (i, 0))
    row1024 = pl.BlockSpec((tm, 1024), lambda i: (i, 0))
    hd = lambda w: pl.BlockSpec((hh, tm, w), lambda i: (0, i, 0))
    return pl.pallas_call(
        body, name="bwd_mla_proj", grid=(s // tm,),
        in_specs=[pl.BlockSpec((tm, 256), lambda i: (i, Z_CQ // 256)), pl.BlockSpec((tm, 256), lambda i: (i, Z_CKV // 256)),
                  pl.BlockSpec((tm, 128), lambda i: (i, Z_KR // 128)), hd(QK_PAD), hd(QK_PAD), hd(V_HEAD),
                  row128, row128, row128,
                  _const_spec((1, 256)), _const_spec((1, 256)), _const_spec((256, 1024)), _const_spec((256, 1024)),
                  _const_spec((1, 256)), _const_spec((1, 256))],
        out_specs=[pl.BlockSpec((tm, 640), lambda i: (i, 0)), row256, row256, row1024, row1024,
                   _acc_spec((1, 256)), _acc_spec((1, 256)), _acc_spec((1, 256)), _acc_spec((1, 256))],
        out_shape=[jax.ShapeDtypeStruct((s, 640), bf16), jax.ShapeDtypeStruct((s, 256), bf16), jax.ShapeDtypeStruct((s, 256), bf16),
                   jax.ShapeDtypeStruct((s, 1024), bf16), jax.ShapeDtypeStruct((s, 1024), bf16)]
        + [jax.ShapeDtypeStruct((1, 256), f32)] * 4,
        compiler_params=_cp(("arbitrary",), VMEM_LIMIT),
    )(z, z, z, dq, dk, dv, cosb, sina, sinb, g_qa, g_kva, wqb, wkvb, g_qn, g_kn)


def _bwd_in(segments, wz, x, g_mix, d2, tm):
    s, d = x.shape
    n_seg = len(segments)

    def body(*refs):
        dz_refs, w_refs = refs[:n_seg], refs[n_seg:2 * n_seg]
        x_ref, g_ref, d2_ref, gx_ref, dg_ref = refs[2 * n_seg:]

        @pl.when(pl.program_id(0) == 0)
        def _():
            dg_ref[...] = jnp.zeros_like(dg_ref)

        dh = _dot_nt(dz_refs[0][...], w_refs[0][...])
        for a_ref, w_ref in zip(dz_refs[1:], w_refs[1:]):
            dh = dh + _dot_nt(a_ref[...], w_ref[...])
        x, g = x_ref[...], g_ref[...]
        r = lax.rsqrt(jnp.sum(x * x, axis=-1, keepdims=True) * (1.0 / d) + EPS)
        dx, dgx = _rms_bwd(dh, x, r, g, d)
        gx_ref[...] = d2_ref[...] + dx
        dg_ref[...] += jnp.sum(dgx, axis=0, keepdims=True)

    rowd = pl.BlockSpec((tm, d), lambda i: (i, 0))
    dz_specs = [pl.BlockSpec((tm, w), functools.partial(lambda i, j: (i, j), j=ja)) for _, w, ja, _ in segments]
    w_specs = [pl.BlockSpec((d, w), functools.partial(lambda i, j: (0, j), j=jw), pipeline_mode=pl.Buffered(1))
               for _, w, _, jw in segments]
    return pl.pallas_call(
        body, name="bwd_in", grid=(s // tm,),
        in_specs=dz_specs + w_specs + [rowd, _const_spec((1, d)), rowd],
        out_specs=[rowd, _acc_spec((1, d))],
        out_shape=[jax.ShapeDtypeStruct((s, d), f32), jax.ShapeDtypeStruct((1, d), f32)],
        compiler_params=_cp(("arbitrary",), VMEM_LIMIT),
    )(*[a for a, _, _, _ in segments], *([wz] * n_seg), x, g_mix, d2)


def _pick_tile(n, cap):
    best = None
    for t in range(LANES, cap + 1, LANES):
        if n % t == 0:
            best = t
    return best if best is not None else n


def _mm_tn(a, b, name):
    kk, m = a.shape
    _, n = b.shape
    tm = _pick_tile(m, 1408)
    tn = _pick_tile(n, 1408)
    tk = min(512, kk)

    n_k = kk // tk

    def body(a_ref, b_ref, o_ref, acc_ref):
        @pl.when(pl.program_id(2) == 0)
        def _():
            acc_ref[...] = jnp.zeros_like(acc_ref)
        acc_ref[...] += _dot_tn(a_ref[...].astype(bf16), b_ref[...].astype(bf16))

        @pl.when(pl.program_id(2) == n_k - 1)
        def _():
            o_ref[...] = acc_ref[...].astype(bf16)

    return pl.pallas_call(
        body, name=name, grid=(m // tm, n // tn, n_k),
        in_specs=[pl.BlockSpec((tk, tm), lambda i, j, k: (k, i)), pl.BlockSpec((tk, tn), lambda i, j, k: (k, j))],
        out_specs=pl.BlockSpec((tm, tn), lambda i, j, k: (i, j)),
        out_shape=jax.ShapeDtypeStruct((m, n), bf16),
        scratch_shapes=[pltpu.VMEM((tm, tn), f32)],
        compiler_params=_cp(("parallel", "parallel", "arbitrary"), VMEM_LIMIT),
    )(a, b)


def _rope_tables(positions):
    inv_freq = ROPE_THETA ** (-jnp.arange(0, QK_ROPE, 2, dtype=f32) / QK_ROPE)
    ang = positions.astype(f32)[:, None] * inv_freq
    cos, sin = jnp.cos(ang), jnp.sin(ang)
    zero = jnp.zeros_like(cos)
    return (jnp.concatenate([cos, cos, zero, zero], axis=1), jnp.concatenate([zero, sin, zero, zero], axis=1),
            jnp.concatenate([-sin, zero, zero, zero], axis=1))


def _pad256(g):
    return jnp.pad(g.reshape(1, QK_HEAD), ((0, 0), (0, QK_PAD - QK_HEAD)))


RELAYOUT_BLOCKS = 8
FIRST = ("w_in", "w_qb", "w_kvb", "lb_param")
SECOND = ("w_o", "w_gate", "w_up", "w_down", "w_ple_gate", "w_ple_proj")
ROW_SHARDED = ("w_o", "w_down", "w_ple_gate")


def _col_moves(j):
    lo = BIG["w_in"][1] * j
    w_in = [(max(lo, a) - lo, min(lo + BIG["w_in"][1], b) - lo, d + max(lo, a) - a)
            for a, b, d in Z_SEGMENTS if max(lo, a) < min(lo + BIG["w_in"][1], b)]
    head, half = divmod(j, 2)
    whole = lambda n: [(0, BIG[n][1], BIG[n][1] * j)]
    return {"w_in": w_in, "w_gate": whole("w_gate"), "w_up": whole("w_up"),
            "w_qb": [(0, 96, QK_PAD * head + 96 * half)], "w_kvb": whole("w_kvb"), "w_ple_proj": whole("w_ple_proj"),
            "lb_param": whole("lb_param")}


def _kernel_width(name):
    return {"w_in": Z_W, "w_qb": MLA_HEADS * QK_PAD}.get(name, N_DEV * BIG[name][1])


def _relayout_specs(names, by_dev):
    specs = []
    for n in names:
        rows, cols = BIG[n]
        if n == "lb_param":
            specs.append(_acc_spec((N_DEV, rows, cols) if by_dev else (rows, _kernel_width(n))))
        elif by_dev:
            specs.append(pl.BlockSpec((N_DEV, rows // RELAYOUT_BLOCKS, cols), lambda i: (0, i, 0)))
        else:
            specs.append(pl.BlockSpec((rows // RELAYOUT_BLOCKS, _kernel_width(n)), lambda i: (i, 0)))
    return specs


def _weights_in(gathered, names, name):
    n = len(names)

    def body(*refs):
        ins, outs = dict(zip(names, refs[:n])), dict(zip(names, refs[n:]))
        if "w_in" in outs:
            outs["w_in"][:, Z_KR + QK_ROPE:Z_W] = jnp.zeros((outs["w_in"].shape[0], Z_W - Z_KR - QK_ROPE), bf16)
        if "w_qb" in outs:
            for h in range(MLA_HEADS):
                outs["w_qb"][:, QK_PAD * h + QK_HEAD:QK_PAD * (h + 1)] = jnp.zeros((outs["w_qb"].shape[0], QK_PAD - QK_HEAD), bf16)
        for j in range(N_DEV):
            for wn, moves in _col_moves(j).items():
                if wn in outs:
                    for s0, s1, d0 in moves:
                        outs[wn][:, d0:d0 + s1 - s0] = ins[wn][j, :, s0:s1]

    outs = pl.pallas_call(
        body, name=name, grid=(RELAYOUT_BLOCKS,), in_specs=_relayout_specs(names, True), out_specs=_relayout_specs(names, False),
        out_shape=[jax.ShapeDtypeStruct((BIG[wn][0], _kernel_width(wn)), gathered[wn].dtype) for wn in names],
        compiler_params=_cp(("arbitrary",), VMEM_LIMIT),
    )(*[gathered[wn] for wn in names])
    return dict(zip(names, outs))


def _grads_out(sources, names, name):
    pieces = [(wn, start, arr) for wn in names for start, arr in sources[wn]]
    n_in = len(pieces)

    def body(*refs):
        outs = dict(zip(names, refs[n_in:]))

        def cols(wn, c0, c1):
            for (pn, start, arr), ref in zip(pieces, refs[:n_in]):
                if pn == wn and start <= c0 and c1 <= start + arr.shape[1]:
                    return ref[:, c0 - start:c1 - start]

        for j in range(N_DEV):
            for wn, moves in _col_moves(j).items():
                if wn in outs:
                    for s0, s1, d0 in moves:
                        outs[wn][j, :, s0:s1] = cols(wn, d0, d0 + s1 - s0).astype(bf16)

    in_specs = [_acc_spec(arr.shape) if wn == "lb_param" else pl.BlockSpec((arr.shape[0] // RELAYOUT_BLOCKS, arr.shape[1]), lambda i: (i, 0))
                for wn, _, arr in pieces]
    outs = pl.pallas_call(
        body, name=name, grid=(RELAYOUT_BLOCKS,), in_specs=in_specs, out_specs=_relayout_specs(names, True),
        out_shape=[jax.ShapeDtypeStruct((N_DEV, *BIG[wn]), bf16) for wn in names],
        compiler_params=_cp(("arbitrary",), VMEM_LIMIT),
    )(*[arr for _, _, arr in pieces])
    return dict(zip(names, outs))


def kernel(x, p, positions, g_mix, w_in, g_qa, g_kva, w_qb, w_kvb, g_qn, g_kn, lb_param, g_hgo, w_o, g_ffn, w_gate, w_up, w_down, g_ple, w_ple_gate, w_ple_proj, loss_target, m_g_mix, m_w_in, m_g_qa, m_g_kva, m_w_qb, m_w_kvb, m_g_qn, m_g_kn, m_lb_param, m_g_hgo, m_w_o, m_g_ffn, m_w_gate, m_w_up, m_w_down, m_g_ple, m_w_ple_gate, m_w_ple_proj, v_g_mix, v_w_in, v_g_qa, v_g_kva, v_w_qb, v_w_kvb, v_g_qn, v_g_kn, v_lb_param, v_g_hgo, v_w_o, v_g_ffn, v_w_gate, v_w_up, v_w_down, v_g_ple, v_w_ple_gate, v_w_ple_proj):
    w_all = dict(g_mix=g_mix, g_qa=g_qa, g_kva=g_kva, g_qn=g_qn, g_kn=g_kn, g_hgo=g_hgo, g_ffn=g_ffn, g_ple=g_ple,
                 w_in=w_in, w_qb=w_qb, w_kvb=w_kvb, w_o=w_o, w_gate=w_gate, w_up=w_up, w_down=w_down,
                 w_ple_gate=w_ple_gate, w_ple_proj=w_ple_proj, lb_param=lb_param)
    m_all = dict(g_mix=m_g_mix, g_qa=m_g_qa, g_kva=m_g_kva, g_qn=m_g_qn, g_kn=m_g_kn, g_hgo=m_g_hgo, g_ffn=m_g_ffn,
                 g_ple=m_g_ple, w_in=m_w_in, w_qb=m_w_qb, w_kvb=m_w_kvb, w_o=m_w_o, w_gate=m_w_gate, w_up=m_w_up,
                 w_down=m_w_down, w_ple_gate=m_w_ple_gate, w_ple_proj=m_w_ple_proj, lb_param=m_lb_param)
    v_all = dict(g_mix=v_g_mix, g_qa=v_g_qa, g_kva=v_g_kva, g_qn=v_g_qn, g_kn=v_g_kn, g_hgo=v_g_hgo, g_ffn=v_g_ffn,
                 g_ple=v_g_ple, w_in=v_w_in, w_qb=v_w_qb, w_kvb=v_w_kvb, w_o=v_w_o, w_gate=v_w_gate, w_up=v_w_up,
                 w_down=v_w_down, w_ple_gate=v_w_ple_gate, w_ple_proj=v_w_ple_proj, lb_param=v_lb_param)
    me_idx = jnp.stack([_me()]).astype(jnp.int32)
    x, p, positions, target = x[0], p[0, 0], positions[0], loss_target[0]
    s = x.shape[0]
    tm, tm_ffn, tq_f, tq_b = min(256, s), min(128, s), min(2048, s), min(1024, s)
    g_mix, g_qa, g_kva, g_qn, g_kn, g_hgo, g_ffn, g_ple = (w_all[n].reshape(1, -1) for n in SMALL)
    g_qn_p, g_kn_p = _pad256(g_qn), _pad256(g_kn)
    cosb, sina, sinb = _rope_tables(positions)
    shard = lambda n: w_all[n].reshape(BIG[n])

    first = _all_gather([shard(n) for n in FIRST], [f32 if n == "lb_param" else bf16 for n in FIRST], "ag_first")
    lands = _cast_to_slot([shard(n) for n in SECOND], me_idx, first[0])
    ag2, token = _exchange_start([], lands, "ag_second_start")
    wk = _weights_in(dict(zip(FIRST, first)), FIRST, "weights_in_first")
    wz, wqb, wkvb, lb4 = (wk[n] for n in FIRST)

    h1, z = _fwd_in(x, g_mix, wz, tm)
    q, k, v = _fwd_mla_proj(z, cosb + token[0, 0], sina, sinb, g_qa, g_kva, wqb, wkvb, g_qn_p, g_kn_p, tm)
    a = _fwd_attn(q, k, v, tq_f)
    o = _fwd_gla(z, lb4)

    second = dict(zip(SECOND, _exchange_wait(ag2, [a, o], "ag_second_wait")))
    wk = _weights_in(second, ("w_gate", "w_up", "w_ple_proj"), "weights_in_second")
    w_gate, w_up, w_pp = wk["w_gate"], wk["w_up"], wk["w_ple_proj"]
    w_o, w_down, w_pg = (second[n].reshape(N_DEV * BIG[n][0], BIG[n][1]) for n in ROW_SHARDED)

    x2, cat = _fwd_mix(a, o, z, g_hgo, x, w_o, tm)
    x3, gp, up = _fwd_ffn(x2, g_ffn, w_gate, w_up, w_down, tm)
    d3, h3, dpre, dpp, dg_ple, loss_tile = _ple_loss_fwd_bwd(x3, g_ple, w_pg, p, w_pp, target, tm)
    d2, h2, act, dgp, dup, dg_ffn = _bwd_ffn(d3, x2, gp, up, g_ffn, w_gate, w_up, w_down, tm_ffn)

    blocks = _grads_out({"w_gate": [(0, _mm_tn(h2, dgp, "dw_gate"))], "w_up": [(0, _mm_tn(h2, dup, "dw_up"))],
                         "w_ple_proj": [(0, _mm_tn(p, dpp, "dw_ple_proj"))]}, ("w_gate", "w_up", "w_ple_proj"), "grads_out_second")
    row_grads = {"w_o": _mm_tn(cat, d2, "dw_o"), "w_down": _mm_tn(act, d3, "dw_down"), "w_ple_gate": _mm_tn(h3, dpre, "dw_ple_gate")}
    blocks.update({n: g.reshape(N_DEV, *BIG[n]) for n, g in row_grads.items()})
    empty = lambda names: [lax.empty((N_PEERS, *BIG[n]), bf16) for n in names]
    rs2, token = _exchange_start([blocks[n] for n in SECOND], empty(SECOND), "rs_second_start")

    da, do, dz_hg, dg_hgo = _bwd_mix(d2, w_o, o, z, g_hgo + token[0, 0], tm)
    dz_hq, dz_hff, dz_hfb, dz_hi, dlb4 = _bwd_gla(z, lb4, do)
    dq, dk, dv = _bwd_attn(q, k, v, da, tq_b)
    dz_mla, cqn, ckvn, dq0, dkv0, dg_qa, dg_kva, dg_qn, dg_kn = _bwd_mla_proj(
        z, dq, dk, dv, cosb, sina, sinb, g_qa, g_kva, wqb, wkvb, g_qn_p, g_kn_p, tm)

    gz = [(Z_HQ, _mm_tn(h1, dz_hq, "dw_in_hq")), (Z_HFF, _mm_tn(h1, dz_hff, "dw_in_hff")),
          (Z_HFB, _mm_tn(h1, dz_hfb, "dw_in_hfb")), (Z_HI, _mm_tn(h1, dz_hi, "dw_in_hi")),
          (Z_HG, _mm_tn(h1, dz_hg, "dw_in_hg")), (Z_CQ, _mm_tn(h1, dz_mla, "dw_in_mla"))]
    blocks1 = _grads_out({"w_in": gz, "w_qb": [(0, _mm_tn(cqn, dq0, "dw_qb"))], "w_kvb": [(0, _mm_tn(ckvn, dkv0, "dw_kvb"))],
                          "lb_param": [(0, dlb4)]}, FIRST, "grads_out_first")
    rs1, token = _exchange_start([blocks1[n] for n in FIRST], empty(FIRST), "rs_first_start")

    result = {}

    def adam(names, lands, src, n_blocks, after=()):
        outs = _adam_shards(me_idx, [src[n] for n in names], lands, [w_all[n] for n in names], [m_all[n] for n in names],
                            [v_all[n] for n in names], n_blocks, "adamw_" + names[0], after)
        result.update(zip(names, outs))
        return outs[0][0]

    lands2 = dict(zip(SECOND, _exchange_wait(rs2, [token], "rs_second_wait")))
    by8 = tuple(n for n in SECOND if n != "w_down")
    done = [adam(by8, [lands2[n] for n in by8], blocks, 8), adam(("w_down",), [lands2["w_down"]], blocks, 2)]

    segments = [(dz_hq, 512, 0, Z_HQ // 512), (dz_hff, 512, 0, Z_HFF // 512), (dz_hfb, 512, 0, Z_HFB // 512),
                (dz_hi, 512, 0, Z_HI // 512), (dz_hg, 512, 0, Z_HG // 512), (dz_mla, 640, 0, Z_CQ // 640)]
    grad_x, dg_mix = _bwd_in(segments, wz, x, g_mix + token[0, 0], d2, tm)
    dgains = (dg_mix, dg_qa, dg_kva, dg_qn, dg_kn, dg_hgo, dg_ffn, dg_ple)

    vec = jnp.concatenate(list(dgains) + [loss_tile[0:1]], axis=1)
    parts = _all_gather([vec], [f32], "ag_gains")[0]
    outs, loss_row = _adam_gains(parts, [w_all[n] for n in SMALL], [m_all[n] for n in SMALL], [v_all[n] for n in SMALL])
    result.update(zip(SMALL, outs))

    lands1 = _exchange_wait(rs1, [grad_x, loss_row, *done], "rs_first_wait")
    adam(FIRST, lands1, blocks1, 8)

    order = ("g_mix", "w_in", "g_qa", "g_kva", "w_qb", "w_kvb", "g_qn", "g_kn", "lb_param", "g_hgo", "w_o", "g_ffn",
             "w_gate", "w_up", "w_down", "g_ple", "w_ple_gate", "w_ple_proj")
    return (loss_row[0, 0], grad_x[None], *[result[n][k] for k in range(4) for n in order])
```

```python
import functools
import math

import jax
import jax.numpy as jnp
from jax import lax
from jax.experimental import pallas as pl
from jax.experimental.pallas import tpu as pltpu

f32 = jnp.float32
bf16 = jnp.bfloat16

N_DEV = 8
D_MODEL = 1024
MLA_HEADS = 4
QK_NOPE = 128
QK_ROPE = 64
QK_HEAD = QK_NOPE + QK_ROPE
QK_PAD = 256
V_HEAD = 128
Q_LORA = 256
KV_LORA = 256
HG_HEADS = 4
HG_DK = 128
CHUNK = 64
D_FF = 2816
PLE_DIM = 256
ROPE_THETA = 10000.0
EPS = 1e-6
ATTN_SCALE = QK_HEAD ** -0.5
LOG2_E = math.log2(math.e)
ATTN_SUB_ROWS = 256
IN_SIZES = (256, 256, 64, 512, 512, 512, 512, 512)
D_IN = sum(IN_SIZES)
Z_HQ, Z_HFF, Z_HFB, Z_HI, Z_HG, Z_CQ, Z_CKV, Z_KR, Z_W = 0, 512, 1024, 1536, 2048, 2560, 2816, 3072, 3200

ADAM_LR, ADAM_B1, ADAM_B2, ADAM_EPS, ADAM_WD, ADAM_STEP = 0.001, 0.9, 0.999, 1e-08, 0.01, 10

LANES = 128
BIG = {"w_in": (1024, 392), "w_qb": (256, 96), "w_kvb": (256, 128), "w_o": (128, 1024), "w_gate": (1024, 352),
       "w_up": (1024, 352), "w_down": (352, 1024), "w_ple_gate": (128, 1024), "w_ple_proj": (256, 128),
       "lb_param": (4, 64)}
ROW_BLOCKS = {("w_in", "w_qb", "w_kvb", "w_o", "w_gate", "w_up", "w_ple_gate", "w_ple_proj"): 8, ("w_down", "lb_param"): 2}
SMALL = {"g_mix": (0, 1024), "g_qa": (1024, 256), "g_kva": (1280, 256), "g_qn": (1536, 192), "g_kn": (1792, 192),
         "g_hgo": (2048, 512), "g_ffn": (2560, 1024), "g_ple": (3584, 1024)}
LOSS_OFF = 4608
GAIN_VEC = LOSS_OFF + LANES
Z_SEGMENTS = ((0, 256, Z_CQ), (256, 512, Z_CKV), (512, 576, Z_KR), (576, 1088, Z_HQ), (1088, 1600, Z_HFF),
              (1600, 2112, Z_HFB), (2112, 2624, Z_HI), (2624, 3136, Z_HG))

VMEM_LIMIT = 56 * 1024 * 1024
MESH = pl.DeviceIdType.MESH


def _cp(sem=None, vmem=None):
    return pltpu.CompilerParams(dimension_semantics=sem, vmem_limit_bytes=vmem)


def _const_spec(shape):
    nd = len(shape)
    return pl.BlockSpec(shape, lambda *_: (0,) * nd, pipeline_mode=pl.Buffered(1))


def _acc_spec(shape):
    nd = len(shape)
    return pl.BlockSpec(shape, lambda *_: (0,) * nd)


def _sigmoid(x):
    return jax.nn.sigmoid(x)


def _dot(a, b):
    return jnp.dot(a, b, preferred_element_type=f32)


def _dot_nt(a, b):
    return lax.dot_general(a, b, (((1,), (1,)), ((), ())), preferred_element_type=f32)


def _dot_tn(a, b):
    return lax.dot_general(a, b, (((0,), (0,)), ((), ())), preferred_element_type=f32)


def _rms_fwd(x, g, width):
    r = lax.rsqrt(jnp.sum(x * x, axis=-1, keepdims=True) * (1.0 / width) + EPS)
    return x * r * g, r


def _rms_bwd(dy, x, r, g, width):
    u = dy * g
    dx = r * u - x * (r * r * r) * (jnp.sum(u * x, axis=-1, keepdims=True) * (1.0 / width))
    return dx, dy * x * r


def _rope(b, c, sa, sb):
    return b * c + pltpu.roll(b, 32, 1) * sa + pltpu.roll(b, 96, 1) * sb


def _all_gather(shards, dtypes, name):
    n = len(shards)

    def body(*refs):
        in_refs, out_refs, stage = refs[:n], refs[n:2 * n], refs[2 * n:3 * n]
        send_sems, recv_sems, local_sems = refs[3 * n:]
        for w in range(n):
            stage[w][...] = in_refs[w][...].astype(stage[w].dtype)
        x, y, c = lax.axis_index("x"), lax.axis_index("y"), lax.axis_index("c")
        me, sibling = (x, y, c), (x, y, 1 - c)
        chips = [(1 - x, y), (x, 1 - y), (1 - x, 1 - y)]

        def slot(w, px, py, pc):
            return out_refs[w].at[4 * px + 2 * py + pc]

        def copy(w, k, block, to, src=None):
            return pltpu.make_async_remote_copy(
                src_ref=slot(w, *block) if src is None else src, dst_ref=slot(w, *block),
                send_sem=send_sems.at[w, k], recv_sem=recv_sems.at[w, k], device_id=to, device_id_type=MESH)

        first = []
        for j, chip in enumerate(chips):
            first += [copy(w, 1 + j, me, (*chip, c), src=stage[w]) for w in range(n)]
        first += [copy(w, 0, me, sibling, src=stage[w]) for w in range(n)]
        mine = [pltpu.make_async_copy(stage[w], slot(w, *me), local_sems.at[w]) for w in range(n)]
        for cp in first + mine:
            cp.start()
        passed = []
        for j, chip in enumerate(chips):
            for w in range(n):
                copy(w, 1 + j, (*chip, c), me).wait_recv()
                passed.append(copy(w, 4 + j, (*chip, c), sibling))
                passed[-1].start()
        for w in range(n):
            copy(w, 0, sibling, me).wait_recv()
        for j, chip in enumerate(chips):
            for w in range(n):
                copy(w, 4 + j, (*chip, 1 - c), me).wait_recv()
        for cp in first + passed:
            cp.wait_send()
        for cp in mine:
            cp.wait()

    return pl.pallas_call(
        body, name=name,
        out_shape=[jax.ShapeDtypeStruct((N_DEV, *s.shape), dt) for s, dt in zip(shards, dtypes)],
        in_specs=[pl.BlockSpec(memory_space=pltpu.VMEM)] * n,
        out_specs=[pl.BlockSpec(memory_space=pl.ANY)] * n,
        scratch_shapes=[pltpu.VMEM(s.shape, dt) for s, dt in zip(shards, dtypes)]
        + [pltpu.SemaphoreType.DMA((n, 7)), pltpu.SemaphoreType.DMA((n, 7)), pltpu.SemaphoreType.DMA((n,))],
        compiler_params=_cp(None, VMEM_LIMIT),
    )(*shards)


N_PEERS = N_DEV - 1
HBM_SPEC = pl.BlockSpec(memory_space=pltpu.HBM)
SEM_SPEC = pl.BlockSpec(memory_space=pltpu.SEMAPHORE)
DATAFLOW = pltpu.SideEffectType.DATAFLOW_SIDE_EFFECTING


def _me():
    return 4 * lax.axis_index("x") + 2 * lax.axis_index("y") + lax.axis_index("c")


def _peer(k):
    x, y, c = lax.axis_index("x"), lax.axis_index("y"), lax.axis_index("c")
    px = 1 - x if k & 4 else x
    py = 1 - y if k & 2 else y
    pc = 1 - c if k & 1 else c
    return (px, py, pc), 4 * px + 2 * py + pc


def _exchange_copies(src_refs, land_refs, send_sems, recv_sems, gather):
    cps = []
    me = _me()
    for k in range(1, N_DEV):
        peer, peer_idx = _peer(k)
        for w, land in enumerate(land_refs):
            src = land.at[me] if gather else src_refs[w].at[peer_idx]
            dst = land.at[me] if gather else land.at[k - 1]
            cps.append(pltpu.make_async_remote_copy(
                src_ref=src, dst_ref=dst, send_sem=send_sems.at[N_PEERS * w + k - 1], recv_sem=recv_sems.at[N_PEERS * w + k - 1],
                device_id=peer, device_id_type=MESH))
    return cps


def _exchange_start(srcs, lands, name):
    n_src, n = len(srcs), len(lands)

    def body(*refs):
        src_refs, land_refs = refs[:n_src], refs[n_src:n_src + n]
        send_sems, recv_sems = refs[n_src + n], refs[n_src + n + 1]
        token = refs[-1]
        for cp in _exchange_copies(src_refs, land_refs, send_sems, recv_sems, gather=not n_src):
            cp.start()
        token[...] = jnp.zeros_like(token)

    arrays = [pltpu.with_memory_space_constraint(a, pltpu.HBM) for a in (*srcs, *lands)]
    outs = pl.pallas_call(
        body, name=name,
        out_shape=(pltpu.SemaphoreType.DMA((n * N_PEERS,)), pltpu.SemaphoreType.DMA((n * N_PEERS,)),
                   *[pltpu.HBM(a.shape, a.dtype) for a in arrays], jax.ShapeDtypeStruct((8, LANES), f32)),
        in_specs=[HBM_SPEC] * len(arrays),
        out_specs=(SEM_SPEC, SEM_SPEC, *[HBM_SPEC] * len(arrays), pl.BlockSpec(memory_space=pltpu.VMEM)),
        input_output_aliases={i: 2 + i for i in range(len(arrays))},
        compiler_params=pltpu.CompilerParams(has_side_effects=DATAFLOW),
    )(*arrays)
    return (outs[0], outs[1], outs[2:2 + n_src], outs[2 + n_src:2 + n_src + n]), outs[-1]


def _exchange_wait(state, after, name):
    send_sems, recv_sems, srcs, lands = state
    n_src, n = len(srcs), len(lands)

    def body(*refs):
        src_refs, land_refs = refs[:n_src], refs[n_src:n_src + n]
        send_ref, recv_ref = refs[n_src + n], refs[n_src + n + 1]
        for cp in _exchange_copies(src_refs, land_refs, send_ref, recv_ref, gather=not n_src):
            cp.wait_send()
            cp.wait_recv()

    arrays = (*srcs, *lands)
    outs = pl.pallas_call(
        body, name=name,
        out_shape=tuple(pltpu.HBM(a.shape, a.dtype) for a in arrays),
        in_specs=[HBM_SPEC] * len(arrays) + [SEM_SPEC, SEM_SPEC] + [pl.BlockSpec(memory_space=pl.ANY)] * len(after),
        out_specs=tuple([HBM_SPEC] * len(arrays)),
        input_output_aliases={i: i for i in range(len(arrays))},
        compiler_params=pltpu.CompilerParams(has_side_effects=DATAFLOW),
    )(*arrays, send_sems, recv_sems, *after)
    return outs[:n_src], outs[n_src:]


def _cast_to_slot(shards, me_idx, after):
    n = len(shards)

    def body(i_ref, *refs):
        for w in range(n):
            refs[n + 1 + w][...] = refs[w][...].astype(bf16)

    return pl.pallas_call(
        body, name="cast_to_slot",
        grid_spec=pltpu.PrefetchScalarGridSpec(
            num_scalar_prefetch=1, grid=(1,),
            in_specs=[pl.BlockSpec(s.shape, lambda i, m: (0, 0)) for s in shards] + [pl.BlockSpec(memory_space=pl.ANY)],
            out_specs=[pl.BlockSpec((None, *s.shape), lambda i, m: (m[0], 0, 0)) for s in shards]),
        out_shape=[jax.ShapeDtypeStruct((N_DEV, *s.shape), bf16) for s in shards],
        compiler_params=_cp(("arbitrary",), VMEM_LIMIT),
    )(me_idx, *shards, after)


def _row_block(rows, n_blocks):
    return (rows // n_blocks, True) if rows % (16 * n_blocks) == 0 else (rows, False)


def _adam_math(w, g, m, v):
    m = ADAM_B1 * m + (1.0 - ADAM_B1) * g
    v = ADAM_B2 * v + (1.0 - ADAM_B2) * (g * g)
    m_hat = m / (1.0 - ADAM_B1 ** ADAM_STEP)
    v_hat = v / (1.0 - ADAM_B2 ** ADAM_STEP)
    delta = -ADAM_LR * (m_hat / (jnp.sqrt(v_hat) + ADAM_EPS) + ADAM_WD * w)
    return delta, m, v


def _adam_shards(me_idx, blocks, lands, ws, ms, vs, n_blocks, name, after=()):
    n = len(blocks)

    def body(i_ref, *refs):
        ins, outs = refs[:5 * n], refs[5 * n + len(after):]
        for w in range(n):
            g_ref, b_ref, w_ref, m_ref, v_ref = (ins[t * n + w] for t in range(5))
            g = g_ref[...].astype(f32)
            for k in range(N_PEERS):
                g = g + b_ref[k].astype(f32)
            if len(w_ref.shape) == 2:
                pieces = [(slice(None), g)]
            else:
                pieces = [(a, g[2 * a:2 * a + 2]) for a in range(2)]
            for at, gp in pieces:
                vals = (gp,) + _adam_math(w_ref[at], gp, m_ref[at], v_ref[at])
                for t, val in enumerate(vals):
                    outs[4 * w + t][at] = val

    specs = [[] for _ in range(5)]
    out_specs, out_shape = [], []
    for g, wt in zip(blocks, ws):
        rows, cols = g.shape[1:]
        rb, cut = _row_block(rows, n_blocks)
        specs[0].append(pl.BlockSpec((None, rb, cols), functools.partial(lambda i, s, cut: (s[0], i if cut else 0, 0), cut=cut)))
        specs[1].append(pl.BlockSpec((N_PEERS, rb, cols), functools.partial(lambda i, s, cut: (0, i if cut else 0, 0), cut=cut)))
        if wt.shape[0] == 1:
            shard = pl.BlockSpec((None, rb, cols), functools.partial(lambda i, s, cut: (0, i if cut else 0, 0), cut=cut))
        else:
            shard = pl.BlockSpec(wt.shape, functools.partial(lambda i, s, nd: (0,) * nd, nd=wt.ndim))
        for t in (2, 3, 4):
            specs[t].append(shard)
        out_specs += [shard] * 4
        out_shape += [jax.ShapeDtypeStruct(wt.shape, f32)] * 4
    outs = pl.pallas_call(
        body, name=name,
        grid_spec=pltpu.PrefetchScalarGridSpec(
            num_scalar_prefetch=1, grid=(n_blocks,), in_specs=sum(specs, []) + [pl.BlockSpec(memory_space=pl.ANY)] * len(after),
            out_specs=out_specs),
        out_shape=out_shape,
        compiler_params=_cp(("arbitrary",), VMEM_LIMIT),
    )(me_idx, *blocks, *lands, *ws, *ms, *vs, *after)
    return [outs[4 * w:4 * w + 4] for w in range(n)]


def _adam_gains(parts, ws, ms, vs):
    n = len(ws)

    def body(p_ref, *refs):
        ins, outs = refs[:3 * n], refs[3 * n:]
        g_all = p_ref[0]
        for k in range(1, N_DEV):
            g_all = g_all + p_ref[k]
        for w, (off, lanes) in enumerate(SMALL.values()):
            w_ref, m_ref, v_ref = ins[w], ins[n + w], ins[2 * n + w]
            if len(w_ref.shape) == 2:
                pieces = [(slice(None), off, lanes)]
            else:
                pieces = [((slice(None), h), off + LANES * h, LANES) for h in range(w_ref.shape[1])]
            for at, o, ln in pieces:
                g = g_all[:, o:o + ln]
                vals = (g,) + _adam_math(w_ref[at], g, m_ref[at], v_ref[at])
                for t, val in enumerate(vals):
                    outs[4 * w + t][at] = val
        outs[4 * n][...] = g_all[:, LOSS_OFF:LOSS_OFF + LANES]

    out_shape = sum([[jax.ShapeDtypeStruct(w.shape, f32)] * 4 for w in ws], []) + [jax.ShapeDtypeStruct((1, LANES), f32)]
    outs = pl.pallas_call(body, name="adamw_gains", out_shape=out_shape)(parts, *ws, *ms, *vs)
    return [outs[4 * w:4 * w + 4] for w in range(n)], outs[4 * n]


def _fwd_in(x, g_mix, wz, tm):
    s, d = x.shape

    def body(x_ref, g_ref, w_ref, h_ref, z_ref):
        h, _ = _rms_fwd(x_ref[...], g_ref[...], d)
        hb = h.astype(bf16)
        h_ref[...] = hb
        z_ref[...] = _dot(hb, w_ref[...])

    return pl.pallas_call(
        body, name="fwd_in", grid=(s // tm,),
        in_specs=[pl.BlockSpec((tm, d), lambda i: (i, 0)), _const_spec((1, d)), _const_spec((d, Z_W))],
        out_specs=[pl.BlockSpec((tm, d), lambda i: (i, 0)), pl.BlockSpec((tm, Z_W), lambda i: (i, 0))],
        out_shape=[jax.ShapeDtypeStruct((s, d), bf16), jax.ShapeDtypeStruct((s, Z_W), f32)],
        compiler_params=_cp(("parallel",), VMEM_LIMIT),
    )(x, g_mix, wz)


def _mla_qk_fwd(cq, ckv, kr, g_qa, g_kva, wqb, wkvb, g_qn, g_kn):
    cqn, rq = _rms_fwd(cq, g_qa, Q_LORA)
    ckvn, rkv = _rms_fwd(ckv, g_kva, KV_LORA)
    cqn_b, ckvn_b = cqn.astype(bf16), ckvn.astype(bf16)
    q0 = _dot(cqn_b, wqb)
    kv0 = _dot(ckvn_b, wkvb)
    return cqn_b, rq, ckvn_b, rkv, q0, kv0


def _fwd_mla_proj(z, cosb, sina, sinb, g_qa, g_kva, wqb, wkvb, g_qn, g_kn, tm):
    s = z.shape[0]
    hh = MLA_HEADS

    def body(cq_ref, ckv_ref, kr_ref, c_ref, sa_ref, sb_ref, gqa_ref, gkva_ref, wqb_ref, wkvb_ref, gqn_ref, gkn_ref,
             q_ref, k_ref, v_ref):
        _, _, _, _, q0, kv0 = _mla_qk_fwd(cq_ref[...], ckv_ref[...], kr_ref[...], gqa_ref[...], gkva_ref[...],
                                          wqb_ref[...], wkvb_ref[...], gqn_ref[...], gkn_ref[...])
        kr = kr_ref[...]
        c, sa, sb = c_ref[...], sa_ref[...], sb_ref[...]
        gqn, gkn = gqn_ref[...], gkn_ref[...]
        kr_sq = jnp.sum(kr * kr, axis=-1, keepdims=True)
        for h in range(hh):
            qh = q0[:, QK_PAD * h:QK_PAD * (h + 1)]
            qn, _ = _rms_fwd(qh, gqn, QK_HEAD)
            q_ref[h, :, 0:128] = qn[:, 0:128].astype(bf16)
            q_ref[h, :, 128:256] = _rope(qn[:, 128:256], c, sa, sb).astype(bf16)
            kn_ = kv0[:, 256 * h:256 * h + 128]
            rk = lax.rsqrt((jnp.sum(kn_ * kn_, axis=-1, keepdims=True) + kr_sq) * (1.0 / QK_HEAD) + EPS)
            k_ref[h, :, 0:128] = (kn_ * rk * gkn[:, 0:128]).astype(bf16)
            k_ref[h, :, 128:256] = _rope(kr * rk * gkn[:, 128:256], c, sa, sb).astype(bf16)
            v_ref[h] = kv0[:, 256 * h + 128:256 * h + 256].astype(bf16)

    row128 = pl.BlockSpec((tm, 128), lambda i: (i, 0))
    return pl.pallas_call(
        body, name="fwd_mla_proj", grid=(s // tm,),
        in_specs=[pl.BlockSpec((tm, 256), lambda i: (i, Z_CQ // 256)), pl.BlockSpec((tm, 256), lambda i: (i, Z_CKV // 256)),
                  pl.BlockSpec((tm, 128), lambda i: (i, Z_KR // 128)), row128, row128, row128,
                  _const_spec((1, 256)), _const_spec((1, 256)), _const_spec((256, 1024)), _const_spec((256, 1024)),
                  _const_spec((1, 256)), _const_spec((1, 256))],
        out_specs=[pl.BlockSpec((hh, tm, QK_PAD), lambda i: (0, i, 0)), pl.BlockSpec((hh, tm, QK_PAD), lambda i: (0, i, 0)),
                   pl.BlockSpec((hh, tm, V_HEAD), lambda i: (0, i, 0))],
        out_shape=[jax.ShapeDtypeStruct((hh, s, QK_PAD), bf16), jax.ShapeDtypeStruct((hh, s, QK_PAD), bf16),
                   jax.ShapeDtypeStruct((hh, s, V_HEAD), bf16)],
        compiler_params=_cp(("parallel",), VMEM_LIMIT),
    )(z, z, z, cosb, sina, sinb, g_qa, g_kva, wqb, wkvb, g_qn, g_kn)


def _fwd_attn(q, k, v, tq):
    hh, s, _ = q.shape

    n_sub = max(1, tq // ATTN_SUB_ROWS)

    def body(q_ref, k_ref, v_ref, o_ref):
        for t in range(n_sub):
            rows = slice(t * (tq // n_sub), (t + 1) * (tq // n_sub))
            sc = _dot_nt(q_ref[rows, :], k_ref[...])
            p = jnp.exp2((sc - jnp.max(sc, axis=-1, keepdims=True)) * (ATTN_SCALE * LOG2_E))
            l = jnp.sum(p, axis=-1, keepdims=True)
            o_ref[rows, :] = (_dot(p.astype(bf16), v_ref[...]) * (1.0 / l)).astype(bf16)

    return pl.pallas_call(
        body, name="fwd_attn", grid=(hh, s // tq),
        in_specs=[pl.BlockSpec((None, tq, QK_PAD), lambda h, i: (h, i, 0)),
                  pl.BlockSpec((None, s, QK_PAD), lambda h, i: (h, 0, 0)),
                  pl.BlockSpec((None, s, V_HEAD), lambda h, i: (h, 0, 0))],
        out_specs=pl.BlockSpec((tq, V_HEAD), lambda h, i: (i, h)),
        out_shape=jax.ShapeDtypeStruct((s, hh * V_HEAD), bf16),
        compiler_params=_cp(("parallel", "parallel"), VMEM_LIMIT),
    )(q, k, v)


def _split3(x):
    hi = x.astype(bf16)
    r1 = x - hi.astype(f32)
    mid = r1.astype(bf16)
    lo = (r1 - mid.astype(f32)).astype(bf16)
    return jnp.concatenate([hi, mid, lo], axis=-1)


def _tri_sum(tri, x):
    y = _dot(tri, _split3(x))
    return y[:, 0:128] + y[:, 128:256] + y[:, 256:384]


GLA_GROUP = 4
GLA_ROWS = GLA_GROUP * CHUNK
GLA_HEADS_PER_STEP = 2


def _gla_masks(rev):
    row = lax.broadcasted_iota(jnp.int32, (GLA_ROWS, GLA_ROWS), 0)
    col = lax.broadcasted_iota(jnp.int32, (GLA_ROWS, GLA_ROWS), 1)
    shift = CHUNK.bit_length() - 1
    same = (jnp.right_shift(row, shift) == jnp.right_shift(col, shift)).astype(f32)
    lower, upper = (row >= col).astype(f32) * same, (row <= col).astype(f32) * same
    keep, keep_t = (upper, lower) if rev else (lower, upper)
    chunk_of = jnp.right_shift(lax.broadcasted_iota(jnp.int32, (GLA_ROWS, 1), 0), shift)
    return keep, keep.astype(bf16), keep_t.astype(bf16), [(chunk_of == c).astype(f32) for c in range(GLA_GROUP)]


def _gla_gates(hq, hf, lower):
    sg = _sigmoid(hf)
    f = lower + (1.0 - lower) * sg
    return hq * _sigmoid(hq), 1.0 - f, jnp.log(f), f, sg


def _gla_last_mid(b, rev):
    b3 = b.reshape(GLA_GROUP, CHUNK, 128)
    last, mid = (0, CHUNK // 2) if rev else (CHUNK - 1, CHUNK // 2 - 1)
    return b3[:, last:last + 1, :], b3[:, mid:mid + 1, :]


def _gla_per_row(per_chunk):
    return jnp.broadcast_to(per_chunk, (GLA_GROUP, CHUNK, 128)).reshape(GLA_ROWS, 128)


def _gla_block_diag(x, row_masks):
    return jnp.concatenate([(x * m).astype(bf16) for m in row_masks], axis=-1)


def _gla_diag(y):
    return jnp.concatenate([y[CHUNK * c:CHUNK * (c + 1), 128 * c:128 * (c + 1)] for c in range(GLA_GROUP)], axis=0)


def _gla_rows(n, n_groups, rev):
    ne = n_groups - 1 - n if rev else n
    return pl.ds(pl.multiple_of(ne * GLA_ROWS, GLA_ROWS), GLA_ROWS), ne * GLA_GROUP


def _gla_scan_order(rev):
    return tuple(reversed(range(GLA_GROUP))) if rev else tuple(range(GLA_GROUP))


def _fwd_gla(z, lb4):
    s = z.shape[0]
    n_groups = s // GLA_ROWS
    assert n_groups % 2 == 0
    hp = GLA_HEADS_PER_STEP
    chains = [(hh, rev) for hh in range(hp) for rev in (False, True)]

    def body(hq_ref, hff_ref, hfb_ref, hi_ref, lb_ref, o_ref, st_ref):
        st_ref[...] = jnp.zeros_like(st_ref)
        masks = {rev: _gla_masks(rev) for rev in (False, True)}
        lowers = [_sigmoid(lb_ref[int(rev):int(rev) + 1, 128 * hh:128 * (hh + 1)]
                           - lb_ref[2 + int(rev):3 + int(rev), 128 * hh:128 * (hh + 1)]) for hh, rev in chains]

        def make_step(first):
            def step(n, carry):
                for ci, (hh, rev) in enumerate(chains):
                    cols = slice(128 * hh, 128 * (hh + 1))
                    rows, _ = _gla_rows(n, n_groups, rev)
                    maskf, tri, _, row_masks = masks[rev]
                    hf_ref = hfb_ref if rev else hff_ref
                    q, k, logf, _, _ = _gla_gates(hq_ref[rows, cols], hf_ref[rows, cols], lowers[ci])
                    vb = hi_ref[rows, cols].astype(bf16)
                    b = _tri_sum(tri, logf)
                    b_last3, b_mid3 = _gla_last_mid(b, rev)
                    b_last, b_mid = _gla_per_row(b_last3), _gla_per_row(b_mid3)
                    qi = (q * jnp.exp(b - b_mid)).astype(bf16)
                    ki = (k * jnp.exp(b_mid - b)).astype(bf16)
                    a = (_dot_nt(qi, ki) * maskf).astype(bf16)
                    kv = _dot_tn(vb, _gla_block_diag(k * jnp.exp(b_last - b), row_masks))
                    decay3 = jnp.exp(b_last3)
                    st = st_ref[ci]
                    before = [None] * GLA_GROUP
                    for c in _gla_scan_order(rev):
                        before[c] = st.astype(bf16)
                        st = st * decay3[c] + kv[:, 128 * c:128 * (c + 1)]
                    st_ref[ci] = st
                    inter = _dot_nt((q * jnp.exp(b)).astype(bf16), jnp.concatenate(before, axis=0))
                    o = _dot(a, vb) + _gla_diag(inter)
                    if first:
                        o_ref[rows, cols] = o
                    else:
                        o_ref[rows, cols] += o
                return carry
            return step

        lax.fori_loop(0, n_groups // 2, make_step(True), 0)
        lax.fori_loop(n_groups // 2, n_groups, make_step(False), 0)

    w = 128 * hp
    col = lambda base: pl.BlockSpec((s, w), lambda h: (0, base // w + h))
    return pl.pallas_call(
        body, name="fwd_gla", grid=(HG_HEADS // hp,),
        in_specs=[col(Z_HQ), col(Z_HFF), col(Z_HFB), col(Z_HI), pl.BlockSpec((4, w), lambda h: (0, h))],
        out_specs=pl.BlockSpec((s, w), lambda h: (0, h)),
        out_shape=jax.ShapeDtypeStruct((s, HG_HEADS * 128), f32),
        scratch_shapes=[pltpu.VMEM((len(chains), 128, 128), f32)],
        compiler_params=_cp(("parallel",), VMEM_LIMIT),
    )(z, z, z, z, lb4)


def _hg_out(o, hg, g_hgo):
    outs, ons, rs = [], [], []
    for h in range(HG_HEADS):
        oh = o[:, 128 * h:128 * (h + 1)]
        on, r = _rms_fwd(oh, g_hgo[:, 128 * h:128 * (h + 1)], 128)
        ons.append(on)
        rs.append(r)
    on = jnp.concatenate(ons, axis=-1)
    sg = _sigmoid(hg)
    return on * (hg * sg), on, rs, sg


def _fwd_mix(a, o, z, g_hgo, x, w_o, tm):
    s, d = x.shape

    def body(a_ref, o_ref, hg_ref, g_ref, x_ref, w_ref, x2_ref, cat_ref):
        r, _, _, _ = _hg_out(o_ref[...], hg_ref[...], g_ref[...])
        cat = jnp.concatenate([a_ref[...], r.astype(bf16)], axis=-1)
        cat_ref[...] = cat
        x2_ref[...] = x_ref[...] + _dot(cat, w_ref[...])

    row512 = pl.BlockSpec((tm, 512), lambda i: (i, 0))
    rowd = pl.BlockSpec((tm, d), lambda i: (i, 0))
    return pl.pallas_call(
        body, name="fwd_mix", grid=(s // tm,),
        in_specs=[row512, row512, pl.BlockSpec((tm, 512), lambda i: (i, Z_HG // 512)), _const_spec((1, 512)), rowd,
                  _const_spec((d, d))],
        out_specs=[rowd, rowd],
        out_shape=[jax.ShapeDtypeStruct((s, d), f32), jax.ShapeDtypeStruct((s, d), bf16)],
        compiler_params=_cp(("parallel",), VMEM_LIMIT),
    )(a, o, z, g_hgo, x, w_o)


def _fwd_ffn(x2, g_ffn, w_gate, w_up, w_down, tm):
    s, d = x2.shape

    def body(x_ref, g_ref, wg_ref, wu_ref, wd_ref, x3_ref, gp_ref, up_ref):
        x = x_ref[...]
        h, _ = _rms_fwd(x, g_ref[...], d)
        hb = h.astype(bf16)
        gp = _dot(hb, wg_ref[...])
        up = _dot(hb, wu_ref[...])
        gp_ref[...] = gp
        up_ref[...] = up
        act = (gp * _sigmoid(gp) * up).astype(bf16)
        x3_ref[...] = x + _dot(act, wd_ref[...])

    rowd = pl.BlockSpec((tm, d), lambda i: (i, 0))
    rowf = pl.BlockSpec((tm, D_FF), lambda i: (i, 0))
    return pl.pallas_call(
        body, name="fwd_ffn", grid=(s // tm,),
        in_specs=[rowd, _const_spec((1, d)), _const_spec((d, D_FF)), _const_spec((d, D_FF)), _const_spec((D_FF, d))],
        out_specs=[rowd, rowf, rowf],
        out_shape=[jax.ShapeDtypeStruct((s, d), f32), jax.ShapeDtypeStruct((s, D_FF), f32),
                   jax.ShapeDtypeStruct((s, D_FF), f32)],
        compiler_params=_cp(("parallel",), VMEM_LIMIT),
    )(x2, g_ffn, w_gate, w_up, w_down)


def _ple_loss_fwd_bwd(x3, g_ple, w_pg, p, w_pp, target, tm):
    s, d = x3.shape

    def body(x_ref, g_ref, wg_ref, p_ref, wp_ref, t_ref, dx_ref, h_ref, dpre_ref, dpp_ref, dg_ref, loss_ref):
        @pl.when(pl.program_id(0) == 0)
        def _():
            dg_ref[...] = jnp.zeros_like(dg_ref)
            loss_ref[...] = jnp.zeros_like(loss_ref)

        x = x_ref[...]
        g = g_ref[...]
        h, r = _rms_fwd(x, g, d)
        hb = h.astype(bf16)
        gate = _sigmoid(_dot(hb, wg_ref[...]))
        pp = _dot(p_ref[...].astype(bf16), wp_ref[...])
        e = x + gate * pp - t_ref[...]
        loss_ref[...] += 0.5 * jnp.sum(e * e) * (1.0 / d)
        dy = e * (1.0 / d)
        dpre = (dy * pp * gate * (1.0 - gate)).astype(bf16)
        dx, dgx = _rms_bwd(_dot_nt(dpre, wg_ref[...]), x, r, g, d)
        dx_ref[...] = dy + dx
        dg_ref[...] += jnp.sum(dgx, axis=0, keepdims=True)
        h_ref[...] = hb
        dpre_ref[...] = dpre
        dpp_ref[...] = (dy * gate).astype(bf16)

    rowd = pl.BlockSpec((tm, d), lambda i: (i, 0))
    return pl.pallas_call(
        body, name="ple_loss_fwd_bwd", grid=(s // tm,),
        in_specs=[rowd, _const_spec((1, d)), _const_spec((d, d)), pl.BlockSpec((tm, PLE_DIM), lambda i: (i, 0)),
                  _const_spec((PLE_DIM, d)), rowd],
        out_specs=[rowd, rowd, rowd, rowd, _acc_spec((1, d)), _acc_spec((8, 128))],
        out_shape=[jax.ShapeDtypeStruct((s, d), f32), jax.ShapeDtypeStruct((s, d), bf16), jax.ShapeDtypeStruct((s, d), bf16),
                   jax.ShapeDtypeStruct((s, d), bf16), jax.ShapeDtypeStruct((1, d), f32), jax.ShapeDtypeStruct((8, 128), f32)],
        compiler_params=_cp(("arbitrary",), VMEM_LIMIT),
    )(x3, g_ple, w_pg, p, w_pp, target)


def _bwd_ffn(d3, x2, gp, up, g_ffn, w_gate, w_up, w_down, tm, tf):
    s, d = x2.shape
    n_f = D_FF // tf

    def body(d3_ref, x_ref, gp_ref, up_ref, g_ref, wg_ref, wu_ref, wd_ref, d2_ref, h_ref, act_ref, dgp_ref, dup_ref, dg_ref,
             d3b_ref, dh_ref):
        i, f = pl.program_id(0), pl.program_id(1)

        @pl.when((i == 0) & (f == 0))
        def _():
            dg_ref[...] = jnp.zeros_like(dg_ref)

        @pl.when(f == 0)
        def _():
            h, _ = _rms_fwd(x_ref[...], g_ref[...], d)
            h_ref[...] = h.astype(bf16)
            d3b_ref[...] = d3_ref[...].astype(bf16)
            dh_ref[...] = jnp.zeros_like(dh_ref)

        gp, up = gp_ref[...], up_ref[...]
        sg = _sigmoid(gp)
        silu = gp * sg
        act_ref[...] = (silu * up).astype(bf16)
        dact = _dot_nt(d3b_ref[...], wd_ref[...])
        dgp = (dact * up * (sg * (1.0 + gp * (1.0 - sg)))).astype(bf16)
        dup = (dact * silu).astype(bf16)
        dgp_ref[...] = dgp
        dup_ref[...] = dup
        dh_ref[...] += _dot_nt(dgp, wg_ref[...]) + _dot_nt(dup, wu_ref[...])

        @pl.when(f == n_f - 1)
        def _():
            x, g = x_ref[...], g_ref[...]
            r = lax.rsqrt(jnp.sum(x * x, axis=-1, keepdims=True) * (1.0 / d) + EPS)
            dx, dgx = _rms_bwd(dh_ref[...], x, r, g, d)
            d2_ref[...] = d3_ref[...] + dx
            dg_ref[...] += jnp.sum(dgx, axis=0, keepdims=True)

    rowd = pl.BlockSpec((tm, d), lambda i, f: (i, 0))
    rowf = pl.BlockSpec((tm, tf), lambda i, f: (i, f))
    wcol = pl.BlockSpec((d, tf), lambda i, f: (0, f))
    return pl.pallas_call(
        body, name="bwd_ffn", grid=(s // tm, n_f),
        in_specs=[rowd, rowd, rowf, rowf, _const_spec((1, d)), wcol, wcol, pl.BlockSpec((tf, d), lambda i, f: (f, 0))],
        out_specs=[rowd, rowd, rowf, rowf, rowf, _acc_spec((1, d))],
        out_shape=[jax.ShapeDtypeStruct((s, d), f32), jax.ShapeDtypeStruct((s, d), bf16)] + [jax.ShapeDtypeStruct((s, D_FF), bf16)] * 3
        + [jax.ShapeDtypeStruct((1, d), f32)],
        scratch_shapes=[pltpu.VMEM((tm, d), bf16), pltpu.VMEM((tm, d), f32)],
        compiler_params=_cp(("arbitrary", "arbitrary"), VMEM_LIMIT),
    )(d3, x2, gp, up, g_ffn, w_gate, w_up, w_down)


def _bwd_mix(d2, w_o, o, z, g_hgo, tm):
    s, d = d2.shape

    def body(d2_ref, w_ref, o_ref, hg_ref, g_ref, da_ref, do_ref, dhg_ref, dg_ref):
        @pl.when(pl.program_id(0) == 0)
        def _():
            dg_ref[...] = jnp.zeros_like(dg_ref)

        dcat = _dot_nt(d2_ref[...].astype(bf16), w_ref[...])
        da_ref[...] = dcat[:, 0:512].astype(bf16)
        dr = dcat[:, 512:1024]
        o, hg, g = o_ref[...], hg_ref[...], g_ref[...]
        _, on, rs, sg = _hg_out(o, hg, g)
        dhg_ref[...] = (dr * on * (sg * (1.0 + hg * (1.0 - sg)))).astype(bf16)
        don = dr * (hg * sg)
        dgs = []
        for h in range(HG_HEADS):
            cols = slice(128 * h, 128 * (h + 1))
            dx, dgx = _rms_bwd(don[:, cols], o[:, cols], rs[h], g[:, cols], 128)
            do_ref[:, cols] = dx
            dgs.append(jnp.sum(dgx, axis=0, keepdims=True))
        dg_ref[...] += jnp.concatenate(dgs, axis=-1)

    row512 = pl.BlockSpec((tm, 512), lambda i: (i, 0))
    return pl.pallas_call(
        body, name="bwd_mix", grid=(s // tm,),
        in_specs=[pl.BlockSpec((tm, d), lambda i: (i, 0)), _const_spec((d, d)), row512,
                  pl.BlockSpec((tm, 512), lambda i: (i, Z_HG // 512)), _const_spec((1, 512))],
        out_specs=[row512, row512, row512, _acc_spec((1, 512))],
        out_shape=[jax.ShapeDtypeStruct((s, 512), bf16), jax.ShapeDtypeStruct((s, 512), f32), jax.ShapeDtypeStruct((s, 512), bf16),
                   jax.ShapeDtypeStruct((1, 512), f32)],
        compiler_params=_cp(("arbitrary",), VMEM_LIMIT),
    )(d2, w_o, o, z, g_hgo)


def _bwd_gla(z, lb4, do):
    s = z.shape[0]
    n_chunks = s // CHUNK
    n_groups = s // GLA_ROWS
    assert n_groups % 2 == 0

    def body(hq_ref, hff_ref, hfb_ref, hi_ref, lb_ref, do_ref, dhq_ref, dhff_ref, dhfb_ref, dhi_ref, dlb_ref,
             st_all, b_all, dst_ref, dq_acc, dv_acc, dlow_ref):
        dirs = (False, True)
        masks = [_gla_masks(rev) for rev in dirs]
        lowers = [_sigmoid(lb_ref[int(rev):int(rev) + 1, :] - lb_ref[2 + int(rev):3 + int(rev), :]) for rev in dirs]
        hf_refs, dhf_refs = (hff_ref, hfb_ref), (dhff_ref, dhfb_ref)

        def fwd_step(n, sts):
            new = []
            for d, rev in enumerate(dirs):
                rows, chunk0 = _gla_rows(n, n_groups, rev)
                _, k, logf, _, _ = _gla_gates(hq_ref[rows, :], hf_refs[d][rows, :], lowers[d])
                b = _tri_sum(masks[d][1], logf)
                b_last3, _ = _gla_last_mid(b, rev)
                b_all[d, rows, :] = b
                kv = _dot_tn(hi_ref[rows, :].astype(bf16), _gla_block_diag(k * jnp.exp(_gla_per_row(b_last3) - b), masks[d][3]))
                decay3 = jnp.exp(b_last3)
                st = sts[d]
                for c in _gla_scan_order(rev):
                    st_all[d, chunk0 + c] = st
                    st = st * decay3[c] + kv[:, 128 * c:128 * (c + 1)]
                new.append(st)
            return tuple(new)

        zero = jnp.zeros((128, 128), f32)
        lax.fori_loop(0, n_groups, fwd_step, (zero, zero))

        dst_ref[...] = jnp.zeros_like(dst_ref)
        dlow_ref[...] = jnp.zeros_like(dlow_ref)

        def make_bwd_step(first):
            def bwd_step(j, carry):
                n = n_groups - 1 - j
                for d, rev in enumerate(dirs):
                    maskf, _, tri_t, row_masks = masks[d]
                    lower = lowers[d]
                    rows, chunk0 = _gla_rows(n, n_groups, rev)
                    hq, hf = hq_ref[rows, :], hf_refs[d][rows, :]
                    q, k, _, f, sg = _gla_gates(hq, hf, lower)
                    v = hi_ref[rows, :]
                    dout = do_ref[rows, :]
                    b = b_all[d, rows, :]
                    b_last3, b_mid3 = _gla_last_mid(b, rev)
                    b_last, b_mid = _gla_per_row(b_last3), _gla_per_row(b_mid3)
                    e1, e2, e3, e4 = jnp.exp(b - b_mid), jnp.exp(b_mid - b), jnp.exp(b_last - b), jnp.exp(b)
                    decay3 = jnp.exp(b_last3)
                    qi, ki, kt, qt = q * e1, k * e2, k * e3, q * e4
                    qib, kib, ktb = qi.astype(bf16), ki.astype(bf16), kt.astype(bf16)
                    vb, dob = v.astype(bf16), dout.astype(bf16)
                    a = (_dot_nt(qib, kib) * maskf).astype(bf16)
                    da = (_dot_nt(dob, vb) * maskf).astype(bf16)
                    dqi = _dot(da, kib)
                    dki = _dot_tn(da, qib)
                    into_state = _dot_tn(dob, _gla_block_diag(qt, row_masks))
                    dst = dst_ref[d]
                    sts, dsts, ddecay = [None] * GLA_GROUP, [None] * GLA_GROUP, [None] * GLA_GROUP
                    for c in reversed(_gla_scan_order(rev)):
                        sts[c] = st_all[d, chunk0 + c]
                        dsts[c] = dst.astype(bf16)
                        ddecay[c] = jnp.sum(dst * sts[c], axis=0, keepdims=True)[None]
                        dst = dst * decay3[c] + into_state[:, 128 * c:128 * (c + 1)]
                    dst_ref[d] = dst
                    dv = _dot_tn(a, dob) + _gla_diag(_dot_nt(ktb, jnp.concatenate(dsts, axis=0)))
                    dqt = _gla_diag(_dot(dob, jnp.concatenate([x.astype(bf16) for x in sts], axis=-1)))
                    dkt = _gla_diag(_dot(vb, jnp.concatenate(dsts, axis=-1)))
                    dq = dqi * e1 + dqt * e4
                    dk = dki * e2 + dkt * e3
                    db = dqi * qi - dki * ki + dqt * qt - dkt * kt
                    dlast3 = (jnp.sum((dkt * kt).reshape(GLA_GROUP, CHUNK, 128), axis=1, keepdims=True)
                              + jnp.concatenate(ddecay, axis=0) * decay3)
                    dlogf = _tri_sum(tri_t, db) + _gla_per_row(dlast3)
                    df = dlogf / f - dk
                    dhf_refs[d][rows, :] = (df * (1.0 - lower) * sg * (1.0 - sg)).astype(bf16)
                    dlow_ref[d:d + 1, :] += jnp.sum(df * (1.0 - sg), axis=0, keepdims=True)
                    sq = _sigmoid(hq)
                    dhq = dq * (sq * (1.0 + hq * (1.0 - sq)))
                    if first:
                        dq_acc[rows, :] = dhq
                        dv_acc[rows, :] = dv
                    else:
                        dhq_ref[rows, :] = (dq_acc[rows, :] + dhq).astype(bf16)
                        dhi_ref[rows, :] = (dv_acc[rows, :] + dv).astype(bf16)
                return carry
            return bwd_step

        lax.fori_loop(0, n_groups // 2, make_bwd_step(True), 0)
        lax.fori_loop(n_groups // 2, n_groups, make_bwd_step(False), 0)

        for d in range(2):
            dl = dlow_ref[d:d + 1, :] * lowers[d] * (1.0 - lowers[d])
            dlb_ref[d:d + 1, :] = dl
            dlb_ref[2 + d:3 + d, :] = -dl

    col = lambda base: pl.BlockSpec((s, 128), lambda h: (0, base // 128 + h))
    return pl.pallas_call(
        body, name="bwd_gla", grid=(HG_HEADS,),
        in_specs=[col(Z_HQ), col(Z_HFF), col(Z_HFB), col(Z_HI), pl.BlockSpec((4, 128), lambda h: (0, h)), col(0)],
        out_specs=[col(0), col(0), col(0), col(0), pl.BlockSpec((4, 128), lambda h: (0, h))],
        out_shape=[jax.ShapeDtypeStruct((s, 512), bf16)] * 4 + [jax.ShapeDtypeStruct((4, 512), f32)],
        scratch_shapes=[pltpu.VMEM((2, n_chunks, 128, 128), f32), pltpu.VMEM((2, s, 128), f32), pltpu.VMEM((2, 128, 128), f32),
                        pltpu.VMEM((s, 128), f32), pltpu.VMEM((s, 128), f32), pltpu.VMEM((2, 128), f32)],
        compiler_params=_cp(("parallel",), VMEM_LIMIT),
    )(z, z, z, z, lb4, do)


def _bwd_attn(q, k, v, da, tq):
    hh, s, _ = q.shape

    n_sub = max(1, tq // ATTN_SUB_ROWS)

    def body(q_ref, k_ref, v_ref, do_ref, dq_ref, dk_ref, dv_ref, w_all, ds_all):
        @pl.when(pl.program_id(1) == 0)
        def _():
            dk_ref[...] = jnp.zeros_like(dk_ref)
            dv_ref[...] = jnp.zeros_like(dv_ref)

        kb, vb = k_ref[...], v_ref[...]
        for t in range(n_sub):
            rows = slice(t * (tq // n_sub), (t + 1) * (tq // n_sub))
            sc = _dot_nt(q_ref[rows, :], kb)
            p = jnp.exp2((sc - jnp.max(sc, axis=-1, keepdims=True)) * (ATTN_SCALE * LOG2_E))
            w = p * (1.0 / jnp.sum(p, axis=-1, keepdims=True))
            dw = _dot_nt(do_ref[rows, :], vb)
            ds = (w * (dw - jnp.sum(dw * w, axis=-1, keepdims=True)) * ATTN_SCALE).astype(bf16)
            dq_ref[rows, :] = _dot(ds, kb)
            w_all[rows, :] = w.astype(bf16)
            ds_all[rows, :] = ds
        dk_ref[...] += _dot_tn(ds_all[...], q_ref[...])
        dv_ref[...] += _dot_tn(w_all[...], do_ref[...])

    return pl.pallas_call(
        body, name="bwd_attn", grid=(hh, s // tq),
        in_specs=[pl.BlockSpec((None, tq, QK_PAD), lambda h, i: (h, i, 0)),
                  pl.BlockSpec((None, s, QK_PAD), lambda h, i: (h, 0, 0)),
                  pl.BlockSpec((None, s, V_HEAD), lambda h, i: (h, 0, 0)),
                  pl.BlockSpec((tq, V_HEAD), lambda h, i: (i, h))],
        out_specs=[pl.BlockSpec((None, tq, QK_PAD), lambda h, i: (h, i, 0)),
                   pl.BlockSpec((None, s, QK_PAD), lambda h, i: (h, 0, 0)),
                   pl.BlockSpec((None, s, V_HEAD), lambda h, i: (h, 0, 0))],
        out_shape=[jax.ShapeDtypeStruct((hh, s, QK_PAD), f32), jax.ShapeDtypeStruct((hh, s, QK_PAD), f32),
                   jax.ShapeDtypeStruct((hh, s, V_HEAD), f32)],
        scratch_shapes=[pltpu.VMEM((tq, s), bf16), pltpu.VMEM((tq, s), bf16)],
        compiler_params=_cp(("parallel", "arbitrary"), VMEM_LIMIT),
    )(q, k, v, da)


def _bwd_mla_proj(z, dq, dk, dv, cosb, sina, sinb, g_qa, g_kva, wqb, wkvb, g_qn, g_kn, tm):
    s = z.shape[0]
    hh = MLA_HEADS

    def body(cq_ref, ckv_ref, kr_ref, dq_ref, dk_ref, dv_ref, c_ref, sa_ref, sb_ref, gqa_ref, gkva_ref, wqb_ref, wkvb_ref,
             gqn_ref, gkn_ref, dz_ref, cqn_ref, ckvn_ref, dq0_ref, dkv0_ref, dgqa_ref, dgkva_ref, dgqn_ref, dgkn_ref):
        @pl.when(pl.program_id(0) == 0)
        def _():
            for r in (dgqa_ref, dgkva_ref, dgqn_ref, dgkn_ref):
                r[...] = jnp.zeros_like(r)

        cq, ckv, kr = cq_ref[...], ckv_ref[...], kr_ref[...]
        gqa, gkva, gqn, gkn = gqa_ref[...], gkva_ref[...], gqn_ref[...], gkn_ref[...]
        cqn_b, rq, ckvn_b, rkv, q0, kv0 = _mla_qk_fwd(cq, ckv, kr, gqa, gkva, wqb_ref[...], wkvb_ref[...], gqn, gkn)
        cqn_ref[...] = cqn_b
        ckvn_ref[...] = ckvn_b
        c, sa, sb = c_ref[...], -sa_ref[...], -sb_ref[...]
        kr_sq = jnp.sum(kr * kr, axis=-1, keepdims=True)
        dkr = jnp.zeros_like(kr)
        dgqn = jnp.zeros((1, QK_PAD), f32)
        dgkn = jnp.zeros((1, QK_PAD), f32)
        for h in range(hh):
            qh = q0[:, QK_PAD * h:QK_PAD * (h + 1)]
            rh = lax.rsqrt(jnp.sum(qh * qh, axis=-1, keepdims=True) * (1.0 / QK_HEAD) + EPS)
            dqh = dq_ref[h]
            dqn = jnp.concatenate([dqh[:, 0:128], _rope(dqh[:, 128:256], c, sa, sb)], axis=-1)
            dq0h, dgx = _rms_bwd(dqn, qh, rh, gqn, QK_HEAD)
            dq0_ref[:, QK_PAD * h:QK_PAD * (h + 1)] = dq0h.astype(bf16)
            dgqn = dgqn + jnp.sum(dgx, axis=0, keepdims=True)

            kn_ = kv0[:, 256 * h:256 * h + 128]
            k0 = jnp.concatenate([kn_, kr], axis=-1)
            rk = lax.rsqrt((jnp.sum(kn_ * kn_, axis=-1, keepdims=True) + kr_sq) * (1.0 / QK_HEAD) + EPS)
            dkh = dk_ref[h]
            dkn = jnp.concatenate([dkh[:, 0:128], _rope(dkh[:, 128:256], c, sa, sb)], axis=-1)
            dk0, dgx = _rms_bwd(dkn, k0, rk, gkn, QK_HEAD)
            dgkn = dgkn + jnp.sum(dgx, axis=0, keepdims=True)
            dkv0_ref[:, 256 * h:256 * h + 128] = dk0[:, 0:128].astype(bf16)
            dkv0_ref[:, 256 * h + 128:256 * h + 256] = dv_ref[h].astype(bf16)
            dkr = dkr + dk0[:, 128:256]
        dgqn_ref[...] += dgqn
        dgkn_ref[...] += dgkn
        dcq, dgx = _rms_bwd(_dot_nt(dq0_ref[...], wqb_ref[...]), cq, rq, gqa, Q_LORA)
        dgqa_ref[...] += jnp.sum(dgx, axis=0, keepdims=True)
        dckv, dgx = _rms_bwd(_dot_nt(dkv0_ref[...], wkvb_ref[...]), ckv, rkv, gkva, KV_LORA)
        dgkva_ref[...] += jnp.sum(dgx, axis=0, keepdims=True)
        dz_ref[:, 0:256] = dcq.astype(bf16)
        dz_ref[:, 256:512] = dckv.astype(bf16)
        dz_ref[:, 512:640] = dkr.astype(bf16)

    row128 = pl.BlockSpec((tm, 128), lambda i: (i, 0))
    row256 = pl.BlockSpec((tm, 256), lambda i: (i, 0))
    row1024 = pl.BlockSpec((tm, 1024), lambda i: (i, 0))
    hd = lambda w: pl.BlockSpec((hh, tm, w), lambda i: (0, i, 0))
    return pl.pallas_call(
        body, name="bwd_mla_proj", grid=(s // tm,),
        in_specs=[pl.BlockSpec((tm, 256), lambda i: (i, Z_CQ // 256)), pl.BlockSpec((tm, 256), lambda i: (i, Z_CKV // 256)),
                  pl.BlockSpec((tm, 128), lambda i: (i, Z_KR // 128)), hd(QK_PAD), hd(QK_PAD), hd(V_HEAD),
                  row128, row128, row128,
                  _const_spec((1, 256)), _const_spec((1, 256)), _const_spec((256, 1024)), _const_spec((256, 1024)),
                  _const_spec((1, 256)), _const_spec((1, 256))],
        out_specs=[pl.BlockSpec((tm, 640), lambda i: (i, 0)), row256, row256, row1024, row1024,
                   _acc_spec((1, 256)), _acc_spec((1, 256)), _acc_spec((1, 256)), _acc_spec((1, 256))],
        out_shape=[jax.ShapeDtypeStruct((s, 640), bf16), jax.ShapeDtypeStruct((s, 256), bf16), jax.ShapeDtypeStruct((s, 256), bf16),
                   jax.ShapeDtypeStruct((s, 1024), bf16), jax.ShapeDtypeStruct((s, 1024), bf16)]
        + [jax.ShapeDtypeStruct((1, 256), f32)] * 4,
        compiler_params=_cp(("arbitrary",), VMEM_LIMIT),
    )(z, z, z, dq, dk, dv, cosb, sina, sinb, g_qa, g_kva, wqb, wkvb, g_qn, g_kn)


def _bwd_in(segments, wz, x, g_mix, d2, tm):
    s, d = x.shape
    n_seg = len(segments)

    def body(*refs):
        dz_refs, w_refs = refs[:n_seg], refs[n_seg:2 * n_seg]
        x_ref, g_ref, d2_ref, gx_ref, dg_ref = refs[2 * n_seg:]

        @pl.when(pl.program_id(0) == 0)
        def _():
            dg_ref[...] = jnp.zeros_like(dg_ref)

        dh = _dot_nt(dz_refs[0][...], w_refs[0][...])
        for a_ref, w_ref in zip(dz_refs[1:], w_refs[1:]):
            dh = dh + _dot_nt(a_ref[...], w_ref[...])
        x, g = x_ref[...], g_ref[...]
        r = lax.rsqrt(jnp.sum(x * x, axis=-1, keepdims=True) * (1.0 / d) + EPS)
        dx, dgx = _rms_bwd(dh, x, r, g, d)
        gx_ref[...] = d2_ref[...] + dx
        dg_ref[...] += jnp.sum(dgx, axis=0, keepdims=True)

    rowd = pl.BlockSpec((tm, d), lambda i: (i, 0))
    dz_specs = [pl.BlockSpec((tm, w), functools.partial(lambda i, j: (i, j), j=ja)) for _, w, ja, _ in segments]
    w_specs = [pl.BlockSpec((d, w), functools.partial(lambda i, j: (0, j), j=jw), pipeline_mode=pl.Buffered(1))
               for _, w, _, jw in segments]
    return pl.pallas_call(
        body, name="bwd_in", grid=(s // tm,),
        in_specs=dz_specs + w_specs + [rowd, _const_spec((1, d)), rowd],
        out_specs=[rowd, _acc_spec((1, d))],
        out_shape=[jax.ShapeDtypeStruct((s, d), f32), jax.ShapeDtypeStruct((1, d), f32)],
        compiler_params=_cp(("arbitrary",), VMEM_LIMIT),
    )(*[a for a, _, _, _ in segments], *([wz] * n_seg), x, g_mix, d2)


def _pick_tile(n, cap):
    best = None
    for t in range(LANES, cap + 1, LANES):
        if n % t == 0:
            best = t
    return best if best is not None else n


def _mm_tn(a, b, name):
    kk, m = a.shape
    _, n = b.shape
    tm = _pick_tile(m, 1408)
    tn = _pick_tile(n, 1408)
    tk = min(512, kk)

    n_k = kk // tk

    def body(a_ref, b_ref, o_ref, acc_ref):
        @pl.when(pl.program_id(2) == 0)
        def _():
            acc_ref[...] = jnp.zeros_like(acc_ref)
        acc_ref[...] += _dot_tn(a_ref[...].astype(bf16), b_ref[...].astype(bf16))

        @pl.when(pl.program_id(2) == n_k - 1)
        def _():
            o_ref[...] = acc_ref[...].astype(bf16)

    return pl.pallas_call(
        body, name=name, grid=(m // tm, n // tn, n_k),
        in_specs=[pl.BlockSpec((tk, tm), lambda i, j, k: (k, i)), pl.BlockSpec((tk, tn), lambda i, j, k: (k, j))],
        out_specs=pl.BlockSpec((tm, tn), lambda i, j, k: (i, j)),
        out_shape=jax.ShapeDtypeStruct((m, n), bf16),
        scratch_shapes=[pltpu.VMEM((tm, tn), f32)],
        compiler_params=_cp(("parallel", "parallel", "arbitrary"), VMEM_LIMIT),
    )(a, b)


def _rope_tables(positions):
    inv_freq = ROPE_THETA ** (-jnp.arange(0, QK_ROPE, 2, dtype=f32) / QK_ROPE)
    ang = positions.astype(f32)[:, None] * inv_freq
    cos, sin = jnp.cos(ang), jnp.sin(ang)
    zero = jnp.zeros_like(cos)
    return (jnp.concatenate([cos, cos, zero, zero], axis=1), jnp.concatenate([zero, sin, zero, zero], axis=1),
            jnp.concatenate([-sin, zero, zero, zero], axis=1))


def _pad256(g):
    return jnp.pad(g.reshape(1, QK_HEAD), ((0, 0), (0, QK_PAD - QK_HEAD)))


RELAYOUT_BLOCKS = 8
FIRST = ("w_in", "w_qb", "w_kvb", "lb_param")
SECOND = ("w_o", "w_gate", "w_up", "w_down", "w_ple_gate", "w_ple_proj")
ROW_SHARDED = ("w_o", "w_down", "w_ple_gate")


def _col_moves(j):
    lo = BIG["w_in"][1] * j
    w_in = [(max(lo, a) - lo, min(lo + BIG["w_in"][1], b) - lo, d + max(lo, a) - a)
            for a, b, d in Z_SEGMENTS if max(lo, a) < min(lo + BIG["w_in"][1], b)]
    head, half = divmod(j, 2)
    whole = lambda n: [(0, BIG[n][1], BIG[n][1] * j)]
    return {"w_in": w_in, "w_gate": whole("w_gate"), "w_up": whole("w_up"),
            "w_qb": [(0, 96, QK_PAD * head + 96 * half)], "w_kvb": whole("w_kvb"), "w_ple_proj": whole("w_ple_proj"),
            "lb_param": whole("lb_param")}


def _kernel_width(name):
    return {"w_in": Z_W, "w_qb": MLA_HEADS * QK_PAD}.get(name, N_DEV * BIG[name][1])


def _relayout_specs(names, by_dev):
    specs = []
    for n in names:
        rows, cols = BIG[n]
        if n == "lb_param":
            specs.append(_acc_spec((N_DEV, rows, cols) if by_dev else (rows, _kernel_width(n))))
        elif by_dev:
            specs.append(pl.BlockSpec((N_DEV, rows // RELAYOUT_BLOCKS, cols), lambda i: (0, i, 0)))
        else:
            specs.append(pl.BlockSpec((rows // RELAYOUT_BLOCKS, _kernel_width(n)), lambda i: (i, 0)))
    return specs


def _weights_in(gathered, names, name):
    n = len(names)

    def body(*refs):
        ins, outs = dict(zip(names, refs[:n])), dict(zip(names, refs[n:]))
        if "w_in" in outs:
            outs["w_in"][:, Z_KR + QK_ROPE:Z_W] = jnp.zeros((outs["w_in"].shape[0], Z_W - Z_KR - QK_ROPE), bf16)
        if "w_qb" in outs:
            for h in range(MLA_HEADS):
                outs["w_qb"][:, QK_PAD * h + QK_HEAD:QK_PAD * (h + 1)] = jnp.zeros((outs["w_qb"].shape[0], QK_PAD - QK_HEAD), bf16)
        for j in range(N_DEV):
            for wn, moves in _col_moves(j).items():
                if wn in outs:
                    for s0, s1, d0 in moves:
                        outs[wn][:, d0:d0 + s1 - s0] = ins[wn][j, :, s0:s1]

    outs = pl.pallas_call(
        body, name=name, grid=(RELAYOUT_BLOCKS,), in_specs=_relayout_specs(names, True), out_specs=_relayout_specs(names, False),
        out_shape=[jax.ShapeDtypeStruct((BIG[wn][0], _kernel_width(wn)), gathered[wn].dtype) for wn in names],
        compiler_params=_cp(("arbitrary",), VMEM_LIMIT),
    )(*[gathered[wn] for wn in names])
    return dict(zip(names, outs))


def _grads_out(sources, names, name):
    pieces = [(wn, start, arr) for wn in names for start, arr in sources[wn]]
    n_in = len(pieces)

    def body(*refs):
        outs = dict(zip(names, refs[n_in:]))

        def cols(wn, c0, c1):
            for (pn, start, arr), ref in zip(pieces, refs[:n_in]):
                if pn == wn and start <= c0 and c1 <= start + arr.shape[1]:
                    return ref[:, c0 - start:c1 - start]

        for j in range(N_DEV):
            for wn, moves in _col_moves(j).items():
                if wn in outs:
                    for s0, s1, d0 in moves:
                        outs[wn][j, :, s0:s1] = cols(wn, d0, d0 + s1 - s0).astype(bf16)

    in_specs = [_acc_spec(arr.shape) if wn == "lb_param" else pl.BlockSpec((arr.shape[0] // RELAYOUT_BLOCKS, arr.shape[1]), lambda i: (i, 0))
                for wn, _, arr in pieces]
    outs = pl.pallas_call(
        body, name=name, grid=(RELAYOUT_BLOCKS,), in_specs=in_specs, out_specs=_relayout_specs(names, True),
        out_shape=[jax.ShapeDtypeStruct((N_DEV, *BIG[wn]), bf16) for wn in names],
        compiler_params=_cp(("arbitrary",), VMEM_LIMIT),
    )(*[arr for _, _, arr in pieces])
    return dict(zip(names, outs))


def kernel(x, p, positions, g_mix, w_in, g_qa, g_kva, w_qb, w_kvb, g_qn, g_kn, lb_param, g_hgo, w_o, g_ffn, w_gate, w_up, w_down, g_ple, w_ple_gate, w_ple_proj, loss_target, m_g_mix, m_w_in, m_g_qa, m_g_kva, m_w_qb, m_w_kvb, m_g_qn, m_g_kn, m_lb_param, m_g_hgo, m_w_o, m_g_ffn, m_w_gate, m_w_up, m_w_down, m_g_ple, m_w_ple_gate, m_w_ple_proj, v_g_mix, v_w_in, v_g_qa, v_g_kva, v_w_qb, v_w_kvb, v_g_qn, v_g_kn, v_lb_param, v_g_hgo, v_w_o, v_g_ffn, v_w_gate, v_w_up, v_w_down, v_g_ple, v_w_ple_gate, v_w_ple_proj):
    w_all = dict(g_mix=g_mix, g_qa=g_qa, g_kva=g_kva, g_qn=g_qn, g_kn=g_kn, g_hgo=g_hgo, g_ffn=g_ffn, g_ple=g_ple,
                 w_in=w_in, w_qb=w_qb, w_kvb=w_kvb, w_o=w_o, w_gate=w_gate, w_up=w_up, w_down=w_down,
                 w_ple_gate=w_ple_gate, w_ple_proj=w_ple_proj, lb_param=lb_param)
    m_all = dict(g_mix=m_g_mix, g_qa=m_g_qa, g_kva=m_g_kva, g_qn=m_g_qn, g_kn=m_g_kn, g_hgo=m_g_hgo, g_ffn=m_g_ffn,
                 g_ple=m_g_ple, w_in=m_w_in, w_qb=m_w_qb, w_kvb=m_w_kvb, w_o=m_w_o, w_gate=m_w_gate, w_up=m_w_up,
                 w_down=m_w_down, w_ple_gate=m_w_ple_gate, w_ple_proj=m_w_ple_proj, lb_param=m_lb_param)
    v_all = dict(g_mix=v_g_mix, g_qa=v_g_qa, g_kva=v_g_kva, g_qn=v_g_qn, g_kn=v_g_kn, g_hgo=v_g_hgo, g_ffn=v_g_ffn,
                 g_ple=v_g_ple, w_in=v_w_in, w_qb=v_w_qb, w_kvb=v_w_kvb, w_o=v_w_o, w_gate=v_w_gate, w_up=v_w_up,
                 w_down=v_w_down, w_ple_gate=v_w_ple_gate, w_ple_proj=v_w_ple_proj, lb_param=v_lb_param)
    me_idx = jnp.stack([_me()]).astype(jnp.int32)
    x, p, positions, target = x[0], p[0, 0], positions[0], loss_target[0]
    s = x.shape[0]
    tm, tm_ffn, tq_f, tq_b = min(256, s), min(1024, s), min(2048, s), min(1024, s)
    g_mix, g_qa, g_kva, g_qn, g_kn, g_hgo, g_ffn, g_ple = (w_all[n].reshape(1, -1) for n in SMALL)
    g_qn_p, g_kn_p = _pad256(g_qn), _pad256(g_kn)
    cosb, sina, sinb = _rope_tables(positions)
    shard = lambda n: w_all[n].reshape(BIG[n])

    first = _all_gather([shard(n) for n in FIRST], [f32 if n == "lb_param" else bf16 for n in FIRST], "ag_first")
    lands = _cast_to_slot([shard(n) for n in SECOND], me_idx, first[0])
    ag2, token = _exchange_start([], lands, "ag_second_start")
    wk = _weights_in(dict(zip(FIRST, first)), FIRST, "weights_in_first")
    wz, wqb, wkvb, lb4 = (wk[n] for n in FIRST)

    h1, z = _fwd_in(x, g_mix, wz, tm)
    q, k, v = _fwd_mla_proj(z, cosb + token[0, 0], sina, sinb, g_qa, g_kva, wqb, wkvb, g_qn_p, g_kn_p, tm)
    a = _fwd_attn(q, k, v, tq_f)
    o = _fwd_gla(z, lb4)

    second = dict(zip(SECOND, _exchange_wait(ag2, [a, o], "ag_second_wait")[1]))
    wk = _weights_in(second, ("w_gate", "w_up", "w_ple_proj"), "weights_in_second")
    w_gate, w_up, w_pp = wk["w_gate"], wk["w_up"], wk["w_ple_proj"]
    w_o, w_down, w_pg = (second[n].reshape(N_DEV * BIG[n][0], BIG[n][1]) for n in ROW_SHARDED)

    x2, cat = _fwd_mix(a, o, z, g_hgo, x, w_o, tm)
    x3, gp, up = _fwd_ffn(x2, g_ffn, w_gate, w_up, w_down, tm)
    d3, h3, dpre, dpp, dg_ple, loss_tile = _ple_loss_fwd_bwd(x3, g_ple, w_pg, p, w_pp, target, tm)
    d2, h2, act, dgp, dup, dg_ffn = _bwd_ffn(d3, x2, gp, up, g_ffn, w_gate, w_up, w_down, tm_ffn, 256)

    blocks = _grads_out({"w_gate": [(0, _mm_tn(h2, dgp, "dw_gate"))], "w_up": [(0, _mm_tn(h2, dup, "dw_up"))],
                         "w_ple_proj": [(0, _mm_tn(p, dpp, "dw_ple_proj"))]}, ("w_gate", "w_up", "w_ple_proj"), "grads_out_second")
    row_grads = {"w_o": _mm_tn(cat, d2, "dw_o"), "w_down": _mm_tn(act, d3, "dw_down"), "w_ple_gate": _mm_tn(h3, dpre, "dw_ple_gate")}
    blocks.update({n: g.reshape(N_DEV, *BIG[n]) for n, g in row_grads.items()})
    empty = lambda names: [lax.empty((N_PEERS, *BIG[n]), bf16) for n in names]
    rs2, token = _exchange_start([blocks[n] for n in SECOND], empty(SECOND), "rs_second_start")

    da, do, dz_hg, dg_hgo = _bwd_mix(d2, w_o, o, z, g_hgo + token[0, 0], tm)
    dz_hq, dz_hff, dz_hfb, dz_hi, dlb4 = _bwd_gla(z, lb4, do)
    dq, dk, dv = _bwd_attn(q, k, v, da, tq_b)
    dz_mla, cqn, ckvn, dq0, dkv0, dg_qa, dg_kva, dg_qn, dg_kn = _bwd_mla_proj(
        z, dq, dk, dv, cosb, sina, sinb, g_qa, g_kva, wqb, wkvb, g_qn_p, g_kn_p, tm)

    gz = [(Z_HQ, _mm_tn(h1, dz_hq, "dw_in_hq")), (Z_HFF, _mm_tn(h1, dz_hff, "dw_in_hff")),
          (Z_HFB, _mm_tn(h1, dz_hfb, "dw_in_hfb")), (Z_HI, _mm_tn(h1, dz_hi, "dw_in_hi")),
          (Z_HG, _mm_tn(h1, dz_hg, "dw_in_hg")), (Z_CQ, _mm_tn(h1, dz_mla, "dw_in_mla"))]
    blocks1 = _grads_out({"w_in": gz, "w_qb": [(0, _mm_tn(cqn, dq0, "dw_qb"))], "w_kvb": [(0, _mm_tn(ckvn, dkv0, "dw_kvb"))],
                          "lb_param": [(0, dlb4)]}, FIRST, "grads_out_first")
    rs1, token = _exchange_start([blocks1[n] for n in FIRST], empty(FIRST), "rs_first_start")

    result = {}

    def adam(names, lands, src, n_blocks, after=()):
        outs = _adam_shards(me_idx, [src[n] for n in names], lands, [w_all[n] for n in names], [m_all[n] for n in names],
                            [v_all[n] for n in names], n_blocks, "adamw_" + names[0], after)
        result.update(zip(names, outs))
        return outs[0][0]

    blocks2, lands2 = (dict(zip(SECOND, arrs)) for arrs in _exchange_wait(rs2, [token], "rs_second_wait"))
    by8 = tuple(n for n in SECOND if n != "w_down")
    done = [adam(by8, [lands2[n] for n in by8], blocks2, 8), adam(("w_down",), [lands2["w_down"]], blocks2, 2)]

    segments = [(dz_hq, 512, 0, Z_HQ // 512), (dz_hff, 512, 0, Z_HFF // 512), (dz_hfb, 512, 0, Z_HFB // 512),
                (dz_hi, 512, 0, Z_HI // 512), (dz_hg, 512, 0, Z_HG // 512), (dz_mla, 640, 0, Z_CQ // 640)]
    grad_x, dg_mix = _bwd_in(segments, wz, x, g_mix + token[0, 0], d2, tm)
    dgains = (dg_mix, dg_qa, dg_kva, dg_qn, dg_kn, dg_hgo, dg_ffn, dg_ple)

    vec = jnp.concatenate(list(dgains) + [loss_tile[0:1]], axis=1)
    parts = _all_gather([vec], [f32], "ag_gains")[0]
    outs, loss_row = _adam_gains(parts, [w_all[n] for n in SMALL], [m_all[n] for n in SMALL], [v_all[n] for n in SMALL])
    result.update(zip(SMALL, outs))

    blocks1, lands1 = _exchange_wait(rs1, [grad_x, loss_row, *done], "rs_first_wait")
    adam(FIRST, lands1, dict(zip(FIRST, blocks1)), 8)

    order = ("g_mix", "w_in", "g_qa", "g_kva", "w_qb", "w_kvb", "g_qn", "g_kn", "lb_param", "g_hgo", "w_o", "g_ffn",
             "w_gate", "w_up", "w_down", "g_ple", "w_ple_gate", "w_ple_proj")
    return (loss_row[0, 0], grad_x[None], *[result[n][k] for k in range(4) for n in order])
```

```python
import functools
import math

import jax
import jax.numpy as jnp
from jax import lax
from jax.experimental import pallas as pl
from jax.experimental.pallas import tpu as pltpu

f32 = jnp.float32
bf16 = jnp.bfloat16

N_DEV = 8
D_MODEL = 1024
MLA_HEADS = 4
QK_NOPE = 128
QK_ROPE = 64
QK_HEAD = QK_NOPE + QK_ROPE
QK_PAD = 256
V_HEAD = 128
Q_LORA = 256
KV_LORA = 256
HG_HEADS = 4
HG_DK = 128
CHUNK = 64
D_FF = 2816
PLE_DIM = 256
ROPE_THETA = 10000.0
EPS = 1e-6
ATTN_SCALE = QK_HEAD ** -0.5
LOG2_E = math.log2(math.e)
ATTN_SUB_ROWS = 256
IN_SIZES = (256, 256, 64, 512, 512, 512, 512, 512)
D_IN = sum(IN_SIZES)
Z_HQ, Z_HFF, Z_HFB, Z_HI, Z_HG, Z_CQ, Z_CKV, Z_KR, Z_W = 0, 512, 1024, 1536, 2048, 2560, 2816, 3072, 3200

ADAM_LR, ADAM_B1, ADAM_B2, ADAM_EPS, ADAM_WD, ADAM_STEP = 0.001, 0.9, 0.999, 1e-08, 0.01, 10

LANES = 128
BIG = {"w_in": (1024, 392), "w_qb": (256, 96), "w_kvb": (256, 128), "w_o": (128, 1024), "w_gate": (1024, 352),
       "w_up": (1024, 352), "w_down": (352, 1024), "w_ple_gate": (128, 1024), "w_ple_proj": (256, 128),
       "lb_param": (4, 64)}
ROW_BLOCKS = {("w_in", "w_qb", "w_kvb", "w_o", "w_gate", "w_up", "w_ple_gate", "w_ple_proj"): 8, ("w_down", "lb_param"): 2}
SMALL = {"g_mix": (0, 1024), "g_qa": (1024, 256), "g_kva": (1280, 256), "g_qn": (1536, 192), "g_kn": (1792, 192),
         "g_hgo": (2048, 512), "g_ffn": (2560, 1024), "g_ple": (3584, 1024)}
LOSS_OFF = 4608
GAIN_VEC = LOSS_OFF + LANES
Z_SEGMENTS = ((0, 256, Z_CQ), (256, 512, Z_CKV), (512, 576, Z_KR), (576, 1088, Z_HQ), (1088, 1600, Z_HFF),
              (1600, 2112, Z_HFB), (2112, 2624, Z_HI), (2624, 3136, Z_HG))

VMEM_LIMIT = 56 * 1024 * 1024
MESH = pl.DeviceIdType.MESH


def _cp(sem=None, vmem=None):
    return pltpu.CompilerParams(dimension_semantics=sem, vmem_limit_bytes=vmem)


def _const_spec(shape):
    nd = len(shape)
    return pl.BlockSpec(shape, lambda *_: (0,) * nd, pipeline_mode=pl.Buffered(1))


def _acc_spec(shape):
    nd = len(shape)
    return pl.BlockSpec(shape, lambda *_: (0,) * nd)


def _sigmoid(x):
    return jax.nn.sigmoid(x)


def _dot(a, b):
    return jnp.dot(a, b, preferred_element_type=f32)


def _dot_nt(a, b):
    return lax.dot_general(a, b, (((1,), (1,)), ((), ())), preferred_element_type=f32)


def _dot_tn(a, b):
    return lax.dot_general(a, b, (((0,), (0,)), ((), ())), preferred_element_type=f32)


def _rms_fwd(x, g, width):
    r = lax.rsqrt(jnp.sum(x * x, axis=-1, keepdims=True) * (1.0 / width) + EPS)
    return x * r * g, r


def _rms_bwd(dy, x, r, g, width):
    u = dy * g
    dx = r * u - x * (r * r * r) * (jnp.sum(u * x, axis=-1, keepdims=True) * (1.0 / width))
    return dx, dy * x * r


def _rope(b, c, sa, sb):
    return b * c + pltpu.roll(b, 32, 1) * sa + pltpu.roll(b, 96, 1) * sb


def _all_gather(shards, dtypes, name):
    n = len(shards)

    def body(*refs):
        in_refs, out_refs, stage = refs[:n], refs[n:2 * n], refs[2 * n:3 * n]
        send_sems, recv_sems, local_sems = refs[3 * n:]
        for w in range(n):
            stage[w][...] = in_refs[w][...].astype(stage[w].dtype)
        x, y, c = lax.axis_index("x"), lax.axis_index("y"), lax.axis_index("c")
        me, sibling = (x, y, c), (x, y, 1 - c)
        chips = [(1 - x, y), (x, 1 - y), (1 - x, 1 - y)]

        def slot(w, px, py, pc):
            return out_refs[w].at[4 * px + 2 * py + pc]

        def copy(w, k, block, to, src=None):
            return pltpu.make_async_remote_copy(
                src_ref=slot(w, *block) if src is None else src, dst_ref=slot(w, *block),
                send_sem=send_sems.at[w, k], recv_sem=recv_sems.at[w, k], device_id=to, device_id_type=MESH)

        first = []
        for j, chip in enumerate(chips):
            first += [copy(w, 1 + j, me, (*chip, c), src=stage[w]) for w in range(n)]
        first += [copy(w, 0, me, sibling, src=stage[w]) for w in range(n)]
        mine = [pltpu.make_async_copy(stage[w], slot(w, *me), local_sems.at[w]) for w in range(n)]
        for cp in first + mine:
            cp.start()
        passed = []
        for j, chip in enumerate(chips):
            for w in range(n):
                copy(w, 1 + j, (*chip, c), me).wait_recv()
                passed.append(copy(w, 4 + j, (*chip, c), sibling))
                passed[-1].start()
        for w in range(n):
            copy(w, 0, sibling, me).wait_recv()
        for j, chip in enumerate(chips):
            for w in range(n):
                copy(w, 4 + j, (*chip, 1 - c), me).wait_recv()
        for cp in first + passed:
            cp.wait_send()
        for cp in mine:
            cp.wait()

    return pl.pallas_call(
        body, name=name,
        out_shape=[jax.ShapeDtypeStruct((N_DEV, *s.shape), dt) for s, dt in zip(shards, dtypes)],
        in_specs=[pl.BlockSpec(memory_space=pltpu.VMEM)] * n,
        out_specs=[pl.BlockSpec(memory_space=pl.ANY)] * n,
        scratch_shapes=[pltpu.VMEM(s.shape, dt) for s, dt in zip(shards, dtypes)]
        + [pltpu.SemaphoreType.DMA((n, 7)), pltpu.SemaphoreType.DMA((n, 7)), pltpu.SemaphoreType.DMA((n,))],
        compiler_params=_cp(None, VMEM_LIMIT),
    )(*shards)


N_PEERS = N_DEV - 1
HBM_SPEC = pl.BlockSpec(memory_space=pltpu.HBM)
SEM_SPEC = pl.BlockSpec(memory_space=pltpu.SEMAPHORE)
DATAFLOW = pltpu.SideEffectType.DATAFLOW_SIDE_EFFECTING


def _me():
    return 4 * lax.axis_index("x") + 2 * lax.axis_index("y") + lax.axis_index("c")


def _peer(k):
    x, y, c = lax.axis_index("x"), lax.axis_index("y"), lax.axis_index("c")
    px = 1 - x if k & 4 else x
    py = 1 - y if k & 2 else y
    pc = 1 - c if k & 1 else c
    return (px, py, pc), 4 * px + 2 * py + pc


def _exchange_copies(src_refs, land_refs, send_sems, recv_sems, gather):
    cps = []
    me = _me()
    for k in range(1, N_DEV):
        peer, peer_idx = _peer(k)
        for w, land in enumerate(land_refs):
            src = land.at[me] if gather else src_refs[w].at[peer_idx]
            dst = land.at[me] if gather else land.at[k - 1]
            cps.append(pltpu.make_async_remote_copy(
                src_ref=src, dst_ref=dst, send_sem=send_sems.at[N_PEERS * w + k - 1], recv_sem=recv_sems.at[N_PEERS * w + k - 1],
                device_id=peer, device_id_type=MESH))
    return cps


def _exchange_start(srcs, lands, name):
    n_src, n = len(srcs), len(lands)

    def body(*refs):
        src_refs, land_refs = refs[:n_src], refs[n_src:n_src + n]
        send_sems, recv_sems = refs[n_src + n], refs[n_src + n + 1]
        token = refs[-1]
        for cp in _exchange_copies(src_refs, land_refs, send_sems, recv_sems, gather=not n_src):
            cp.start()
        token[...] = jnp.zeros_like(token)

    arrays = [pltpu.with_memory_space_constraint(a, pltpu.HBM) for a in (*srcs, *lands)]
    outs = pl.pallas_call(
        body, name=name,
        out_shape=(pltpu.SemaphoreType.DMA((n * N_PEERS,)), pltpu.SemaphoreType.DMA((n * N_PEERS,)),
                   *[pltpu.HBM(a.shape, a.dtype) for a in arrays], jax.ShapeDtypeStruct((8, LANES), f32)),
        in_specs=[HBM_SPEC] * len(arrays),
        out_specs=(SEM_SPEC, SEM_SPEC, *[HBM_SPEC] * len(arrays), pl.BlockSpec(memory_space=pltpu.VMEM)),
        input_output_aliases={i: 2 + i for i in range(len(arrays))},
        compiler_params=pltpu.CompilerParams(has_side_effects=DATAFLOW),
    )(*arrays)
    return (outs[0], outs[1], outs[2:2 + n_src], outs[2 + n_src:2 + n_src + n]), outs[-1]


def _exchange_wait(state, after, name):
    send_sems, recv_sems, srcs, lands = state
    n_src, n = len(srcs), len(lands)

    def body(*refs):
        src_refs, land_refs = refs[:n_src], refs[n_src:n_src + n]
        send_ref, recv_ref = refs[n_src + n], refs[n_src + n + 1]
        for cp in _exchange_copies(src_refs, land_refs, send_ref, recv_ref, gather=not n_src):
            cp.wait_send()
            cp.wait_recv()

    arrays = (*srcs, *lands)
    outs = pl.pallas_call(
        body, name=name,
        out_shape=tuple(pltpu.HBM(a.shape, a.dtype) for a in arrays),
        in_specs=[HBM_SPEC] * len(arrays) + [SEM_SPEC, SEM_SPEC] + [pl.BlockSpec(memory_space=pl.ANY)] * len(after),
        out_specs=tuple([HBM_SPEC] * len(arrays)),
        input_output_aliases={i: i for i in range(len(arrays))},
        compiler_params=pltpu.CompilerParams(has_side_effects=DATAFLOW),
    )(*arrays, send_sems, recv_sems, *after)
    return outs[:n_src], outs[n_src:]


def _cast_to_slot(shards, me_idx, after):
    n = len(shards)

    def body(i_ref, *refs):
        for w in range(n):
            refs[n + 1 + w][...] = refs[w][...].astype(bf16)

    return pl.pallas_call(
        body, name="cast_to_slot",
        grid_spec=pltpu.PrefetchScalarGridSpec(
            num_scalar_prefetch=1, grid=(1,),
            in_specs=[pl.BlockSpec(s.shape, lambda i, m: (0, 0)) for s in shards] + [pl.BlockSpec(memory_space=pl.ANY)],
            out_specs=[pl.BlockSpec((None, *s.shape), lambda i, m: (m[0], 0, 0)) for s in shards]),
        out_shape=[jax.ShapeDtypeStruct((N_DEV, *s.shape), bf16) for s in shards],
        compiler_params=_cp(("arbitrary",), VMEM_LIMIT),
    )(me_idx, *shards, after)


def _row_block(rows, n_blocks):
    return (rows // n_blocks, True) if rows % (16 * n_blocks) == 0 else (rows, False)


def _adam_math(w, g, m, v):
    m = ADAM_B1 * m + (1.0 - ADAM_B1) * g
    v = ADAM_B2 * v + (1.0 - ADAM_B2) * (g * g)
    m_hat = m / (1.0 - ADAM_B1 ** ADAM_STEP)
    v_hat = v / (1.0 - ADAM_B2 ** ADAM_STEP)
    delta = -ADAM_LR * (m_hat / (jnp.sqrt(v_hat) + ADAM_EPS) + ADAM_WD * w)
    return delta, m, v


def _adam_shards(me_idx, blocks, lands, ws, ms, vs, n_blocks, name, after=()):
    n = len(blocks)

    def body(i_ref, *refs):
        ins, outs = refs[:5 * n], refs[5 * n + len(after):]
        for w in range(n):
            g_ref, b_ref, w_ref, m_ref, v_ref = (ins[t * n + w] for t in range(5))
            g = g_ref[...].astype(f32)
            for k in range(N_PEERS):
                g = g + b_ref[k].astype(f32)
            if len(w_ref.shape) == 2:
                pieces = [(slice(None), g)]
            else:
                pieces = [(a, g[2 * a:2 * a + 2]) for a in range(2)]
            for at, gp in pieces:
                vals = (gp,) + _adam_math(w_ref[at], gp, m_ref[at], v_ref[at])
                for t, val in enumerate(vals):
                    outs[4 * w + t][at] = val

    specs = [[] for _ in range(5)]
    out_specs, out_shape = [], []
    for g, wt in zip(blocks, ws):
        rows, cols = g.shape[1:]
        rb, cut = _row_block(rows, n_blocks)
        specs[0].append(pl.BlockSpec((None, rb, cols), functools.partial(lambda i, s, cut: (s[0], i if cut else 0, 0), cut=cut)))
        specs[1].append(pl.BlockSpec((N_PEERS, rb, cols), functools.partial(lambda i, s, cut: (0, i if cut else 0, 0), cut=cut)))
        if wt.shape[0] == 1:
            shard = pl.BlockSpec((None, rb, cols), functools.partial(lambda i, s, cut: (0, i if cut else 0, 0), cut=cut))
        else:
            shard = pl.BlockSpec(wt.shape, functools.partial(lambda i, s, nd: (0,) * nd, nd=wt.ndim))
        for t in (2, 3, 4):
            specs[t].append(shard)
        out_specs += [shard] * 4
        out_shape += [jax.ShapeDtypeStruct(wt.shape, f32)] * 4
    outs = pl.pallas_call(
        body, name=name,
        grid_spec=pltpu.PrefetchScalarGridSpec(
            num_scalar_prefetch=1, grid=(n_blocks,), in_specs=sum(specs, []) + [pl.BlockSpec(memory_space=pl.ANY)] * len(after),
            out_specs=out_specs),
        out_shape=out_shape,
        compiler_params=_cp(("arbitrary",), VMEM_LIMIT),
    )(me_idx, *blocks, *lands, *ws, *ms, *vs, *after)
    return [outs[4 * w:4 * w + 4] for w in range(n)]


def _adam_gains(parts, ws, ms, vs):
    n = len(ws)

    def body(p_ref, *refs):
        ins, outs = refs[:3 * n], refs[3 * n:]
        g_all = p_ref[0]
        for k in range(1, N_DEV):
            g_all = g_all + p_ref[k]
        for w, (off, lanes) in enumerate(SMALL.values()):
            w_ref, m_ref, v_ref = ins[w], ins[n + w], ins[2 * n + w]
            if len(w_ref.shape) == 2:
                pieces = [(slice(None), off, lanes)]
            else:
                pieces = [((slice(None), h), off + LANES * h, LANES) for h in range(w_ref.shape[1])]
            for at, o, ln in pieces:
                g = g_all[:, o:o + ln]
                vals = (g,) + _adam_math(w_ref[at], g, m_ref[at], v_ref[at])
                for t, val in enumerate(vals):
                    outs[4 * w + t][at] = val
        outs[4 * n][...] = g_all[:, LOSS_OFF:LOSS_OFF + LANES]

    out_shape = sum([[jax.ShapeDtypeStruct(w.shape, f32)] * 4 for w in ws], []) + [jax.ShapeDtypeStruct((1, LANES), f32)]
    outs = pl.pallas_call(body, name="adamw_gains", out_shape=out_shape)(parts, *ws, *ms, *vs)
    return [outs[4 * w:4 * w + 4] for w in range(n)], outs[4 * n]


def _fwd_in(x, g_mix, wz, tm):
    s, d = x.shape

    def body(x_ref, g_ref, w_ref, h_ref, z_ref):
        h, _ = _rms_fwd(x_ref[...], g_ref[...], d)
        hb = h.astype(bf16)
        h_ref[...] = hb
        z_ref[...] = _dot(hb, w_ref[...])

    return pl.pallas_call(
        body, name="fwd_in", grid=(s // tm,),
        in_specs=[pl.BlockSpec((tm, d), lambda i: (i, 0)), _const_spec((1, d)), _const_spec((d, Z_W))],
        out_specs=[pl.BlockSpec((tm, d), lambda i: (i, 0)), pl.BlockSpec((tm, Z_W), lambda i: (i, 0))],
        out_shape=[jax.ShapeDtypeStruct((s, d), bf16), jax.ShapeDtypeStruct((s, Z_W), f32)],
        compiler_params=_cp(("parallel",), VMEM_LIMIT),
    )(x, g_mix, wz)


def _mla_qk_fwd(cq, ckv, kr, g_qa, g_kva, wqb, wkvb, g_qn, g_kn):
    cqn, rq = _rms_fwd(cq, g_qa, Q_LORA)
    ckvn, rkv = _rms_fwd(ckv, g_kva, KV_LORA)
    cqn_b, ckvn_b = cqn.astype(bf16), ckvn.astype(bf16)
    q0 = _dot(cqn_b, wqb)
    kv0 = _dot(ckvn_b, wkvb)
    return cqn_b, rq, ckvn_b, rkv, q0, kv0


def _fwd_mla_proj(z, cosb, sina, sinb, g_qa, g_kva, wqb, wkvb, g_qn, g_kn, tm):
    s = z.shape[0]
    hh = MLA_HEADS

    def body(cq_ref, ckv_ref, kr_ref, c_ref, sa_ref, sb_ref, gqa_ref, gkva_ref, wqb_ref, wkvb_ref, gqn_ref, gkn_ref,
             q_ref, k_ref, v_ref):
        _, _, _, _, q0, kv0 = _mla_qk_fwd(cq_ref[...], ckv_ref[...], kr_ref[...], gqa_ref[...], gkva_ref[...],
                                          wqb_ref[...], wkvb_ref[...], gqn_ref[...], gkn_ref[...])
        kr = kr_ref[...]
        c, sa, sb = c_ref[...], sa_ref[...], sb_ref[...]
        gqn, gkn = gqn_ref[...], gkn_ref[...]
        kr_sq = jnp.sum(kr * kr, axis=-1, keepdims=True)
        for h in range(hh):
            qh = q0[:, QK_PAD * h:QK_PAD * (h + 1)]
            qn, _ = _rms_fwd(qh, gqn, QK_HEAD)
            q_ref[h, :, 0:128] = qn[:, 0:128].astype(bf16)
            q_ref[h, :, 128:256] = _rope(qn[:, 128:256], c, sa, sb).astype(bf16)
            kn_ = kv0[:, 256 * h:256 * h + 128]
            rk = lax.rsqrt((jnp.sum(kn_ * kn_, axis=-1, keepdims=True) + kr_sq) * (1.0 / QK_HEAD) + EPS)
            k_ref[h, :, 0:128] = (kn_ * rk * gkn[:, 0:128]).astype(bf16)
            k_ref[h, :, 128:256] = _rope(kr * rk * gkn[:, 128:256], c, sa, sb).astype(bf16)
            v_ref[h] = kv0[:, 256 * h + 128:256 * h + 256].astype(bf16)

    row128 = pl.BlockSpec((tm, 128), lambda i: (i, 0))
    return pl.pallas_call(
        body, name="fwd_mla_proj", grid=(s // tm,),
        in_specs=[pl.BlockSpec((tm, 256), lambda i: (i, Z_CQ // 256)), pl.BlockSpec((tm, 256), lambda i: (i, Z_CKV // 256)),
                  pl.BlockSpec((tm, 128), lambda i: (i, Z_KR // 128)), row128, row128, row128,
                  _const_spec((1, 256)), _const_spec((1, 256)), _const_spec((256, 1024)), _const_spec((256, 1024)),
                  _const_spec((1, 256)), _const_spec((1, 256))],
        out_specs=[pl.BlockSpec((hh, tm, QK_PAD), lambda i: (0, i, 0)), pl.BlockSpec((hh, tm, QK_PAD), lambda i: (0, i, 0)),
                   pl.BlockSpec((hh, tm, V_HEAD), lambda i: (0, i, 0))],
        out_shape=[jax.ShapeDtypeStruct((hh, s, QK_PAD), bf16), jax.ShapeDtypeStruct((hh, s, QK_PAD), bf16),
                   jax.ShapeDtypeStruct((hh, s, V_HEAD), bf16)],
        compiler_params=_cp(("parallel",), VMEM_LIMIT),
    )(z, z, z, cosb, sina, sinb, g_qa, g_kva, wqb, wkvb, g_qn, g_kn)


def _fwd_attn(q, k, v, tq):
    hh, s, _ = q.shape

    n_sub = max(1, tq // ATTN_SUB_ROWS)

    def body(q_ref, k_ref, v_ref, o_ref, o32_ref):
        for t in range(n_sub):
            rows = slice(t * (tq // n_sub), (t + 1) * (tq // n_sub))
            sc = _dot_nt(q_ref[rows, :], k_ref[...])
            p = jnp.exp2((sc - jnp.max(sc, axis=-1, keepdims=True)) * (ATTN_SCALE * LOG2_E))
            l = jnp.sum(p, axis=-1, keepdims=True)
            o = _dot(p.astype(bf16), v_ref[...]) * (1.0 / l)
            o_ref[rows, :] = o.astype(bf16)
            o32_ref[rows, :] = o

    out = pl.BlockSpec((tq, V_HEAD), lambda h, i: (i, h))
    return pl.pallas_call(
        body, name="fwd_attn", grid=(hh, s // tq),
        in_specs=[pl.BlockSpec((None, tq, QK_PAD), lambda h, i: (h, i, 0)),
                  pl.BlockSpec((None, s, QK_PAD), lambda h, i: (h, 0, 0)),
                  pl.BlockSpec((None, s, V_HEAD), lambda h, i: (h, 0, 0))],
        out_specs=[out, out],
        out_shape=[jax.ShapeDtypeStruct((s, hh * V_HEAD), bf16), jax.ShapeDtypeStruct((s, hh * V_HEAD), f32)],
        compiler_params=_cp(("parallel", "parallel"), VMEM_LIMIT),
    )(q, k, v)


def _split3(x):
    hi = x.astype(bf16)
    r1 = x - hi.astype(f32)
    mid = r1.astype(bf16)
    lo = (r1 - mid.astype(f32)).astype(bf16)
    return jnp.concatenate([hi, mid, lo], axis=-1)


def _tri_sum(tri, x):
    y = _dot(tri, _split3(x))
    return y[:, 0:128] + y[:, 128:256] + y[:, 256:384]


GLA_GROUP = 4
GLA_ROWS = GLA_GROUP * CHUNK
GLA_HEADS_PER_STEP = 2


def _gla_masks(rev):
    row = lax.broadcasted_iota(jnp.int32, (GLA_ROWS, GLA_ROWS), 0)
    col = lax.broadcasted_iota(jnp.int32, (GLA_ROWS, GLA_ROWS), 1)
    shift = CHUNK.bit_length() - 1
    same = (jnp.right_shift(row, shift) == jnp.right_shift(col, shift)).astype(f32)
    lower, upper = (row >= col).astype(f32) * same, (row <= col).astype(f32) * same
    keep, keep_t = (upper, lower) if rev else (lower, upper)
    chunk_of = jnp.right_shift(lax.broadcasted_iota(jnp.int32, (GLA_ROWS, 1), 0), shift)
    return keep, keep.astype(bf16), keep_t.astype(bf16), [(chunk_of == c).astype(f32) for c in range(GLA_GROUP)]


def _gla_gates(hq, hf, lower):
    sg = _sigmoid(hf)
    f = lower + (1.0 - lower) * sg
    return hq * _sigmoid(hq), 1.0 - f, jnp.log(f), f, sg


def _gla_last_mid(b, rev):
    b3 = b.reshape(GLA_GROUP, CHUNK, 128)
    last, mid = (0, CHUNK // 2) if rev else (CHUNK - 1, CHUNK // 2 - 1)
    return b3[:, last:last + 1, :], b3[:, mid:mid + 1, :]


def _gla_per_row(per_chunk):
    return jnp.broadcast_to(per_chunk, (GLA_GROUP, CHUNK, 128)).reshape(GLA_ROWS, 128)


def _gla_block_diag(x, row_masks):
    return jnp.concatenate([(x * m).astype(bf16) for m in row_masks], axis=-1)


def _gla_diag(y):
    return jnp.concatenate([y[CHUNK * c:CHUNK * (c + 1), 128 * c:128 * (c + 1)] for c in range(GLA_GROUP)], axis=0)


def _gla_rows(n, n_groups, rev):
    ne = n_groups - 1 - n if rev else n
    return pl.ds(pl.multiple_of(ne * GLA_ROWS, GLA_ROWS), GLA_ROWS), ne * GLA_GROUP


def _gla_scan_order(rev):
    return tuple(reversed(range(GLA_GROUP))) if rev else tuple(range(GLA_GROUP))


def _fwd_gla(z, lb4):
    s = z.shape[0]
    n_groups = s // GLA_ROWS
    assert n_groups % 2 == 0
    hp = GLA_HEADS_PER_STEP
    chains = [(hh, rev) for hh in range(hp) for rev in (False, True)]

    def body(hq_ref, hff_ref, hfb_ref, hi_ref, lb_ref, o_ref, st_ref):
        st_ref[...] = jnp.zeros_like(st_ref)
        masks = {rev: _gla_masks(rev) for rev in (False, True)}
        lowers = [_sigmoid(lb_ref[int(rev):int(rev) + 1, 128 * hh:128 * (hh + 1)]
                           - lb_ref[2 + int(rev):3 + int(rev), 128 * hh:128 * (hh + 1)]) for hh, rev in chains]

        def make_step(first):
            def step(n, carry):
                for ci, (hh, rev) in enumerate(chains):
                    cols = slice(128 * hh, 128 * (hh + 1))
                    rows, _ = _gla_rows(n, n_groups, rev)
                    maskf, tri, _, row_masks = masks[rev]
                    hf_ref = hfb_ref if rev else hff_ref
                    q, k, logf, _, _ = _gla_gates(hq_ref[rows, cols], hf_ref[rows, cols], lowers[ci])
                    vb = hi_ref[rows, cols].astype(bf16)
                    b = _tri_sum(tri, logf)
                    b_last3, b_mid3 = _gla_last_mid(b, rev)
                    b_last, b_mid = _gla_per_row(b_last3), _gla_per_row(b_mid3)
                    qi = (q * jnp.exp(b - b_mid)).astype(bf16)
                    ki = (k * jnp.exp(b_mid - b)).astype(bf16)
                    a = (_dot_nt(qi, ki) * maskf).astype(bf16)
                    kv = _dot_tn(vb, _gla_block_diag(k * jnp.exp(b_last - b), row_masks))
                    decay3 = jnp.exp(b_last3)
                    st = st_ref[ci]
                    before = [None] * GLA_GROUP
                    for c in _gla_scan_order(rev):
                        before[c] = st.astype(bf16)
                        st = st * decay3[c] + kv[:, 128 * c:128 * (c + 1)]
                    st_ref[ci] = st
                    inter = _dot_nt((q * jnp.exp(b)).astype(bf16), jnp.concatenate(before, axis=0))
                    o = _dot(a, vb) + _gla_diag(inter)
                    if first:
                        o_ref[rows, cols] = o
                    else:
                        o_ref[rows, cols] += o
                return carry
            return step

        lax.fori_loop(0, n_groups // 2, make_step(True), 0)
        lax.fori_loop(n_groups // 2, n_groups, make_step(False), 0)

    w = 128 * hp
    col = lambda base: pl.BlockSpec((s, w), lambda h: (0, base // w + h))
    return pl.pallas_call(
        body, name="fwd_gla", grid=(HG_HEADS // hp,),
        in_specs=[col(Z_HQ), col(Z_HFF), col(Z_HFB), col(Z_HI), pl.BlockSpec((4, w), lambda h: (0, h))],
        out_specs=pl.BlockSpec((s, w), lambda h: (0, h)),
        out_shape=jax.ShapeDtypeStruct((s, HG_HEADS * 128), f32),
        scratch_shapes=[pltpu.VMEM((len(chains), 128, 128), f32)],
        compiler_params=_cp(("parallel",), VMEM_LIMIT),
    )(z, z, z, z, lb4)


def _hg_out(o, hg, g_hgo):
    outs, ons, rs = [], [], []
    for h in range(HG_HEADS):
        oh = o[:, 128 * h:128 * (h + 1)]
        on, r = _rms_fwd(oh, g_hgo[:, 128 * h:128 * (h + 1)], 128)
        ons.append(on)
        rs.append(r)
    on = jnp.concatenate(ons, axis=-1)
    sg = _sigmoid(hg)
    return on * (hg * sg), on, rs, sg


def _fwd_mix(a, o, z, g_hgo, x, w_o, tm):
    s, d = x.shape

    def body(a_ref, o_ref, hg_ref, g_ref, x_ref, w_ref, x2_ref, cat_ref):
        r, _, _, _ = _hg_out(o_ref[...], hg_ref[...], g_ref[...])
        cat = jnp.concatenate([a_ref[...], r.astype(bf16)], axis=-1)
        cat_ref[...] = cat
        x2_ref[...] = x_ref[...] + _dot(cat, w_ref[...])

    row512 = pl.BlockSpec((tm, 512), lambda i: (i, 0))
    rowd = pl.BlockSpec((tm, d), lambda i: (i, 0))
    return pl.pallas_call(
        body, name="fwd_mix", grid=(s // tm,),
        in_specs=[row512, row512, pl.BlockSpec((tm, 512), lambda i: (i, Z_HG // 512)), _const_spec((1, 512)), rowd,
                  _const_spec((d, d))],
        out_specs=[rowd, rowd],
        out_shape=[jax.ShapeDtypeStruct((s, d), f32), jax.ShapeDtypeStruct((s, d), bf16)],
        compiler_params=_cp(("parallel",), VMEM_LIMIT),
    )(a, o, z, g_hgo, x, w_o)


def _fwd_ffn(x2, g_ffn, w_gate, w_up, w_down, tm):
    s, d = x2.shape

    def body(x_ref, g_ref, wg_ref, wu_ref, wd_ref, x3_ref, gp_ref, up_ref):
        x = x_ref[...]
        h, _ = _rms_fwd(x, g_ref[...], d)
        hb = h.astype(bf16)
        gp = _dot(hb, wg_ref[...])
        up = _dot(hb, wu_ref[...])
        gp_ref[...] = gp
        up_ref[...] = up
        act = (gp * _sigmoid(gp) * up).astype(bf16)
        x3_ref[...] = x + _dot(act, wd_ref[...])

    rowd = pl.BlockSpec((tm, d), lambda i: (i, 0))
    rowf = pl.BlockSpec((tm, D_FF), lambda i: (i, 0))
    return pl.pallas_call(
        body, name="fwd_ffn", grid=(s // tm,),
        in_specs=[rowd, _const_spec((1, d)), _const_spec((d, D_FF)), _const_spec((d, D_FF)), _const_spec((D_FF, d))],
        out_specs=[rowd, rowf, rowf],
        out_shape=[jax.ShapeDtypeStruct((s, d), f32), jax.ShapeDtypeStruct((s, D_FF), f32),
                   jax.ShapeDtypeStruct((s, D_FF), f32)],
        compiler_params=_cp(("parallel",), VMEM_LIMIT),
    )(x2, g_ffn, w_gate, w_up, w_down)


def _ple_loss_fwd_bwd(x3, g_ple, w_pg, p, w_pp, target, tm):
    s, d = x3.shape

    def body(x_ref, g_ref, wg_ref, p_ref, wp_ref, t_ref, dx_ref, h_ref, dpre_ref, dpp_ref, dg_ref, loss_ref):
        @pl.when(pl.program_id(0) == 0)
        def _():
            dg_ref[...] = jnp.zeros_like(dg_ref)
            loss_ref[...] = jnp.zeros_like(loss_ref)

        x = x_ref[...]
        g = g_ref[...]
        h, r = _rms_fwd(x, g, d)
        hb = h.astype(bf16)
        gate = _sigmoid(_dot(hb, wg_ref[...]))
        pp = _dot(p_ref[...].astype(bf16), wp_ref[...])
        e = x + gate * pp - t_ref[...]
        loss_ref[...] += 0.5 * jnp.sum(e * e) * (1.0 / d)
        dy = e * (1.0 / d)
        dpre = (dy * pp * gate * (1.0 - gate)).astype(bf16)
        dx, dgx = _rms_bwd(_dot_nt(dpre, wg_ref[...]), x, r, g, d)
        dx_ref[...] = dy + dx
        dg_ref[...] += jnp.sum(dgx, axis=0, keepdims=True)
        h_ref[...] = hb
        dpre_ref[...] = dpre
        dpp_ref[...] = (dy * gate).astype(bf16)

    rowd = pl.BlockSpec((tm, d), lambda i: (i, 0))
    return pl.pallas_call(
        body, name="ple_loss_fwd_bwd", grid=(s // tm,),
        in_specs=[rowd, _const_spec((1, d)), _const_spec((d, d)), pl.BlockSpec((tm, PLE_DIM), lambda i: (i, 0)),
                  _const_spec((PLE_DIM, d)), rowd],
        out_specs=[rowd, rowd, rowd, rowd, _acc_spec((1, d)), _acc_spec((8, 128))],
        out_shape=[jax.ShapeDtypeStruct((s, d), f32), jax.ShapeDtypeStruct((s, d), bf16), jax.ShapeDtypeStruct((s, d), bf16),
                   jax.ShapeDtypeStruct((s, d), bf16), jax.ShapeDtypeStruct((1, d), f32), jax.ShapeDtypeStruct((8, 128), f32)],
        compiler_params=_cp(("arbitrary",), VMEM_LIMIT),
    )(x3, g_ple, w_pg, p, w_pp, target)


def _bwd_ffn(d3, x2, gp, up, g_ffn, w_gate, w_up, w_down, tm, tf):
    s, d = x2.shape
    n_f = D_FF // tf

    def body(d3_ref, x_ref, gp_ref, up_ref, g_ref, wg_ref, wu_ref, wd_ref, d2_ref, h_ref, act_ref, dgp_ref, dup_ref, dg_ref,
             d3b_ref, dh_ref):
        i, f = pl.program_id(0), pl.program_id(1)

        @pl.when((i == 0) & (f == 0))
        def _():
            dg_ref[...] = jnp.zeros_like(dg_ref)

        @pl.when(f == 0)
        def _():
            h, _ = _rms_fwd(x_ref[...], g_ref[...], d)
            h_ref[...] = h.astype(bf16)
            d3b_ref[...] = d3_ref[...].astype(bf16)
            dh_ref[...] = jnp.zeros_like(dh_ref)

        gp, up = gp_ref[...], up_ref[...]
        sg = _sigmoid(gp)
        silu = gp * sg
        act_ref[...] = (silu * up).astype(bf16)
        dact = _dot_nt(d3b_ref[...], wd_ref[...])
        dgp = (dact * up * (sg * (1.0 + gp * (1.0 - sg)))).astype(bf16)
        dup = (dact * silu).astype(bf16)
        dgp_ref[...] = dgp
        dup_ref[...] = dup
        dh_ref[...] += _dot_nt(dgp, wg_ref[...]) + _dot_nt(dup, wu_ref[...])

        @pl.when(f == n_f - 1)
        def _():
            x, g = x_ref[...], g_ref[...]
            r = lax.rsqrt(jnp.sum(x * x, axis=-1, keepdims=True) * (1.0 / d) + EPS)
            dx, dgx = _rms_bwd(dh_ref[...], x, r, g, d)
            d2_ref[...] = d3_ref[...] + dx
            dg_ref[...] += jnp.sum(dgx, axis=0, keepdims=True)

    rowd = pl.BlockSpec((tm, d), lambda i, f: (i, 0))
    rowf = pl.BlockSpec((tm, tf), lambda i, f: (i, f))
    wcol = pl.BlockSpec((d, tf), lambda i, f: (0, f))
    return pl.pallas_call(
        body, name="bwd_ffn", grid=(s // tm, n_f),
        in_specs=[rowd, rowd, rowf, rowf, _const_spec((1, d)), wcol, wcol, pl.BlockSpec((tf, d), lambda i, f: (f, 0))],
        out_specs=[rowd, rowd, rowf, rowf, rowf, _acc_spec((1, d))],
        out_shape=[jax.ShapeDtypeStruct((s, d), f32), jax.ShapeDtypeStruct((s, d), bf16)] + [jax.ShapeDtypeStruct((s, D_FF), bf16)] * 3
        + [jax.ShapeDtypeStruct((1, d), f32)],
        scratch_shapes=[pltpu.VMEM((tm, d), bf16), pltpu.VMEM((tm, d), f32)],
        compiler_params=_cp(("arbitrary", "arbitrary"), VMEM_LIMIT),
    )(d3, x2, gp, up, g_ffn, w_gate, w_up, w_down)


def _bwd_mix(d2, w_o, o, z, g_hgo, tm):
    s, d = d2.shape

    def body(d2_ref, w_ref, o_ref, hg_ref, g_ref, da_ref, do_ref, dhg_ref, dg_ref):
        @pl.when(pl.program_id(0) == 0)
        def _():
            dg_ref[...] = jnp.zeros_like(dg_ref)

        dcat = _dot_nt(d2_ref[...].astype(bf16), w_ref[...])
        da_ref[...] = dcat[:, 0:512].astype(bf16)
        dr = dcat[:, 512:1024]
        o, hg, g = o_ref[...], hg_ref[...], g_ref[...]
        _, on, rs, sg = _hg_out(o, hg, g)
        dhg_ref[...] = (dr * on * (sg * (1.0 + hg * (1.0 - sg)))).astype(bf16)
        don = dr * (hg * sg)
        dgs = []
        for h in range(HG_HEADS):
            cols = slice(128 * h, 128 * (h + 1))
            dx, dgx = _rms_bwd(don[:, cols], o[:, cols], rs[h], g[:, cols], 128)
            do_ref[:, cols] = dx
            dgs.append(jnp.sum(dgx, axis=0, keepdims=True))
        dg_ref[...] += jnp.concatenate(dgs, axis=-1)

    row512 = pl.BlockSpec((tm, 512), lambda i: (i, 0))
    return pl.pallas_call(
        body, name="bwd_mix", grid=(s // tm,),
        in_specs=[pl.BlockSpec((tm, d), lambda i: (i, 0)), _const_spec((d, d)), row512,
                  pl.BlockSpec((tm, 512), lambda i: (i, Z_HG // 512)), _const_spec((1, 512))],
        out_specs=[row512, row512, row512, _acc_spec((1, 512))],
        out_shape=[jax.ShapeDtypeStruct((s, 512), bf16), jax.ShapeDtypeStruct((s, 512), f32), jax.ShapeDtypeStruct((s, 512), bf16),
                   jax.ShapeDtypeStruct((1, 512), f32)],
        compiler_params=_cp(("arbitrary",), VMEM_LIMIT),
    )(d2, w_o, o, z, g_hgo)


def _bwd_gla(z, lb4, do):
    s = z.shape[0]
    n_chunks = s // CHUNK
    n_groups = s // GLA_ROWS
    assert n_groups % 2 == 0

    def body(hq_ref, hff_ref, hfb_ref, hi_ref, lb_ref, do_ref, dhq_ref, dhff_ref, dhfb_ref, dhi_ref, dlb_ref,
             st_all, b_all, dst_ref, dq_acc, dv_acc, dlow_ref):
        dirs = (False, True)
        masks = [_gla_masks(rev) for rev in dirs]
        lowers = [_sigmoid(lb_ref[int(rev):int(rev) + 1, :] - lb_ref[2 + int(rev):3 + int(rev), :]) for rev in dirs]
        hf_refs, dhf_refs = (hff_ref, hfb_ref), (dhff_ref, dhfb_ref)

        def fwd_step(n, sts):
            new = []
            for d, rev in enumerate(dirs):
                rows, chunk0 = _gla_rows(n, n_groups, rev)
                _, k, logf, _, _ = _gla_gates(hq_ref[rows, :], hf_refs[d][rows, :], lowers[d])
                b = _tri_sum(masks[d][1], logf)
                b_last3, _ = _gla_last_mid(b, rev)
                b_all[d, rows, :] = b
                kv = _dot_tn(hi_ref[rows, :].astype(bf16), _gla_block_diag(k * jnp.exp(_gla_per_row(b_last3) - b), masks[d][3]))
                decay3 = jnp.exp(b_last3)
                st = sts[d]
                for c in _gla_scan_order(rev):
                    st_all[d, chunk0 + c] = st
                    st = st * decay3[c] + kv[:, 128 * c:128 * (c + 1)]
                new.append(st)
            return tuple(new)

        zero = jnp.zeros((128, 128), f32)
        lax.fori_loop(0, n_groups, fwd_step, (zero, zero))

        dst_ref[...] = jnp.zeros_like(dst_ref)
        dlow_ref[...] = jnp.zeros_like(dlow_ref)

        def make_bwd_step(first):
            def bwd_step(j, carry):
                n = n_groups - 1 - j
                for d, rev in enumerate(dirs):
                    maskf, _, tri_t, row_masks = masks[d]
                    lower = lowers[d]
                    rows, chunk0 = _gla_rows(n, n_groups, rev)
                    hq, hf = hq_ref[rows, :], hf_refs[d][rows, :]
                    q, k, _, f, sg = _gla_gates(hq, hf, lower)
                    v = hi_ref[rows, :]
                    dout = do_ref[rows, :]
                    b = b_all[d, rows, :]
                    b_last3, b_mid3 = _gla_last_mid(b, rev)
                    b_last, b_mid = _gla_per_row(b_last3), _gla_per_row(b_mid3)
                    e1, e2, e3, e4 = jnp.exp(b - b_mid), jnp.exp(b_mid - b), jnp.exp(b_last - b), jnp.exp(b)
                    decay3 = jnp.exp(b_last3)
                    qi, ki, kt, qt = q * e1, k * e2, k * e3, q * e4
                    qib, kib, ktb = qi.astype(bf16), ki.astype(bf16), kt.astype(bf16)
                    vb, dob = v.astype(bf16), dout.astype(bf16)
                    a = (_dot_nt(qib, kib) * maskf).astype(bf16)
                    da = (_dot_nt(dob, vb) * maskf).astype(bf16)
                    dqi = _dot(da, kib)
                    dki = _dot_tn(da, qib)
                    into_state = _dot_tn(dob, _gla_block_diag(qt, row_masks))
                    dst = dst_ref[d]
                    sts, dsts, ddecay = [None] * GLA_GROUP, [None] * GLA_GROUP, [None] * GLA_GROUP
                    for c in reversed(_gla_scan_order(rev)):
                        sts[c] = st_all[d, chunk0 + c]
                        dsts[c] = dst.astype(bf16)
                        ddecay[c] = jnp.sum(dst * sts[c], axis=0, keepdims=True)[None]
                        dst = dst * decay3[c] + into_state[:, 128 * c:128 * (c + 1)]
                    dst_ref[d] = dst
                    dv = _dot_tn(a, dob) + _gla_diag(_dot_nt(ktb, jnp.concatenate(dsts, axis=0)))
                    dqt = _gla_diag(_dot(dob, jnp.concatenate([x.astype(bf16) for x in sts], axis=-1)))
                    dkt = _gla_diag(_dot(vb, jnp.concatenate(dsts, axis=-1)))
                    dq = dqi * e1 + dqt * e4
                    dk = dki * e2 + dkt * e3
                    db = dqi * qi - dki * ki + dqt * qt - dkt * kt
                    dlast3 = (jnp.sum((dkt * kt).reshape(GLA_GROUP, CHUNK, 128), axis=1, keepdims=True)
                              + jnp.concatenate(ddecay, axis=0) * decay3)
                    dlogf = _tri_sum(tri_t, db) + _gla_per_row(dlast3)
                    df = dlogf / f - dk
                    dhf_refs[d][rows, :] = (df * (1.0 - lower) * sg * (1.0 - sg)).astype(bf16)
                    dlow_ref[d:d + 1, :] += jnp.sum(df * (1.0 - sg), axis=0, keepdims=True)
                    sq = _sigmoid(hq)
                    dhq = dq * (sq * (1.0 + hq * (1.0 - sq)))
                    if first:
                        dq_acc[rows, :] = dhq
                        dv_acc[rows, :] = dv
                    else:
                        dhq_ref[rows, :] = (dq_acc[rows, :] + dhq).astype(bf16)
                        dhi_ref[rows, :] = (dv_acc[rows, :] + dv).astype(bf16)
                return carry
            return bwd_step

        lax.fori_loop(0, n_groups // 2, make_bwd_step(True), 0)
        lax.fori_loop(n_groups // 2, n_groups, make_bwd_step(False), 0)

        for d in range(2):
            dl = dlow_ref[d:d + 1, :] * lowers[d] * (1.0 - lowers[d])
            dlb_ref[d:d + 1, :] = dl
            dlb_ref[2 + d:3 + d, :] = -dl

    col = lambda base: pl.BlockSpec((s, 128), lambda h: (0, base // 128 + h))
    return pl.pallas_call(
        body, name="bwd_gla", grid=(HG_HEADS,),
        in_specs=[col(Z_HQ), col(Z_HFF), col(Z_HFB), col(Z_HI), pl.BlockSpec((4, 128), lambda h: (0, h)), col(0)],
        out_specs=[col(0), col(0), col(0), col(0), pl.BlockSpec((4, 128), lambda h: (0, h))],
        out_shape=[jax.ShapeDtypeStruct((s, 512), bf16)] * 4 + [jax.ShapeDtypeStruct((4, 512), f32)],
        scratch_shapes=[pltpu.VMEM((2, n_chunks, 128, 128), f32), pltpu.VMEM((2, s, 128), f32), pltpu.VMEM((2, 128, 128), f32),
                        pltpu.VMEM((s, 128), f32), pltpu.VMEM((s, 128), f32), pltpu.VMEM((2, 128), f32)],
        compiler_params=_cp(("parallel",), VMEM_LIMIT),
    )(z, z, z, z, lb4, do)


def _bwd_attn(q, k, v, da, a32, tq):
    hh, s, _ = q.shape

    n_sub = max(1, tq // ATTN_SUB_ROWS)

    def body(q_ref, k_ref, v_ref, do_ref, o_ref, dq_ref, dk_ref, dv_ref, w_all, ds_all):
        @pl.when(pl.program_id(1) == 0)
        def _():
            dk_ref[...] = jnp.zeros_like(dk_ref)
            dv_ref[...] = jnp.zeros_like(dv_ref)

        kb, vb = k_ref[...], v_ref[...]
        for t in range(n_sub):
            rows = slice(t * (tq // n_sub), (t + 1) * (tq // n_sub))
            sc = _dot_nt(q_ref[rows, :], kb)
            p = jnp.exp2((sc - jnp.max(sc, axis=-1, keepdims=True)) * (ATTN_SCALE * LOG2_E))
            w = (p * (1.0 / jnp.sum(p, axis=-1, keepdims=True))).astype(bf16)
            dob = do_ref[rows, :]
            delta = jnp.sum(dob.astype(f32) * o_ref[rows, :], axis=-1, keepdims=True)
            ds = w * (_dot_nt(dob, vb) - delta).astype(bf16)
            dq_ref[rows, :] = _dot(ds, kb) * ATTN_SCALE
            w_all[rows, :] = w
            ds_all[rows, :] = ds
        dk_ref[...] += _dot_tn(ds_all[...], q_ref[...])
        dv_ref[...] += _dot_tn(w_all[...], do_ref[...])

        @pl.when(pl.program_id(1) == s // tq - 1)
        def _():
            dk_ref[...] = dk_ref[...] * ATTN_SCALE

    return pl.pallas_call(
        body, name="bwd_attn", grid=(hh, s // tq),
        in_specs=[pl.BlockSpec((None, tq, QK_PAD), lambda h, i: (h, i, 0)),
                  pl.BlockSpec((None, s, QK_PAD), lambda h, i: (h, 0, 0)),
                  pl.BlockSpec((None, s, V_HEAD), lambda h, i: (h, 0, 0)),
                  pl.BlockSpec((tq, V_HEAD), lambda h, i: (i, h)), pl.BlockSpec((tq, V_HEAD), lambda h, i: (i, h))],
        out_specs=[pl.BlockSpec((None, tq, QK_PAD), lambda h, i: (h, i, 0)),
                   pl.BlockSpec((None, s, QK_PAD), lambda h, i: (h, 0, 0)),
                   pl.BlockSpec((None, s, V_HEAD), lambda h, i: (h, 0, 0))],
        out_shape=[jax.ShapeDtypeStruct((hh, s, QK_PAD), f32), jax.ShapeDtypeStruct((hh, s, QK_PAD), f32),
                   jax.ShapeDtypeStruct((hh, s, V_HEAD), f32)],
        scratch_shapes=[pltpu.VMEM((tq, s), bf16), pltpu.VMEM((tq, s), bf16)],
        compiler_params=_cp(("parallel", "arbitrary"), VMEM_LIMIT),
    )(q, k, v, da, a32)


def _bwd_mla_proj(z, dq, dk, dv, cosb, sina, sinb, g_qa, g_kva, wqb, wkvb, g_qn, g_kn, tm):
    s = z.shape[0]
    hh = MLA_HEADS

    def body(cq_ref, ckv_ref, kr_ref, dq_ref, dk_ref, dv_ref, c_ref, sa_ref, sb_ref, gqa_ref, gkva_ref, wqb_ref, wkvb_ref,
             gqn_ref, gkn_ref, dz_ref, cqn_ref, ckvn_ref, dq0_ref, dkv0_ref, dgqa_ref, dgkva_ref, dgqn_ref, dgkn_ref):
        @pl.when(pl.program_id(0) == 0)
        def _():
            for r in (dgqa_ref, dgkva_ref, dgqn_ref, dgkn_ref):
                r[...] = jnp.zeros_like(r)

        cq, ckv, kr = cq_ref[...], ckv_ref[...], kr_ref[...]
        gqa, gkva, gqn, gkn = gqa_ref[...], gkva_ref[...], gqn_ref[...], gkn_ref[...]
        cqn_b, rq, ckvn_b, rkv, q0, kv0 = _mla_qk_fwd(cq, ckv, kr, gqa, gkva, wqb_ref[...], wkvb_ref[...], gqn, gkn)
        cqn_ref[...] = cqn_b
        ckvn_ref[...] = ckvn_b
        c, sa, sb = c_ref[...], -sa_ref[...], -sb_ref[...]
        kr_sq = jnp.sum(kr * kr, axis=-1, keepdims=True)
        dkr = jnp.zeros_like(kr)
        dgqn = jnp.zeros((1, QK_PAD), f32)
        dgkn = jnp.zeros((1, QK_PAD), f32)
        for h in range(hh):
            qh = q0[:, QK_PAD * h:QK_PAD * (h + 1)]
            rh = lax.rsqrt(jnp.sum(qh * qh, axis=-1, keepdims=True) * (1.0 / QK_HEAD) + EPS)
            dqh = dq_ref[h]
            dqn = jnp.concatenate([dqh[:, 0:128], _rope(dqh[:, 128:256], c, sa, sb)], axis=-1)
            dq0h, dgx = _rms_bwd(dqn, qh, rh, gqn, QK_HEAD)
            dq0_ref[:, QK_PAD * h:QK_PAD * (h + 1)] = dq0h.astype(bf16)
            dgqn = dgqn + jnp.sum(dgx, axis=0, keepdims=True)

            kn_ = kv0[:, 256 * h:256 * h + 128]
            k0 = jnp.concatenate([kn_, kr], axis=-1)
            rk = lax.rsqrt((jnp.sum(kn_ * kn_, axis=-1, keepdims=True) + kr_sq) * (1.0 / QK_HEAD) + EPS)
            dkh = dk_ref[h]
            dkn = jnp.concatenate([dkh[:, 0:128], _rope(dkh[:, 128:256], c, sa, sb)], axis=-1)
            dk0, dgx = _rms_bwd(dkn, k0, rk, gkn, QK_HEAD)
            dgkn = dgkn + jnp.sum(dgx, axis=0, keepdims=True)
            dkv0_ref[:, 256 * h:256 * h + 128] = dk0[:, 0:128].astype(bf16)
            dkv0_ref[:, 256 * h + 128:256 * h + 256] = dv_ref[h].astype(bf16)
            dkr = dkr + dk0[:, 128:256]
        dgqn_ref[...] += dgqn
        dgkn_ref[...] += dgkn
        dcq, dgx = _rms_bwd(_dot_nt(dq0_ref[...], wqb_ref[...]), cq, rq, gqa, Q_LORA)
        dgqa_ref[...] += jnp.sum(dgx, axis=0, keepdims=True)
        dckv, dgx = _rms_bwd(_dot_nt(dkv0_ref[...], wkvb_ref[...]), ckv, rkv, gkva, KV_LORA)
        dgkva_ref[...] += jnp.sum(dgx, axis=0, keepdims=True)
        dz_ref[:, 0:256] = dcq.astype(bf16)
        dz_ref[:, 256:512] = dckv.astype(bf16)
        dz_ref[:, 512:640] = dkr.astype(bf16)

    row128 = pl.BlockSpec((tm, 128), lambda i: (i, 0))
    row256 = pl.BlockSpec((tm, 256), lambda i: (i, 0))
    row1024 = pl.BlockSpec((tm, 1024), lambda i: (i, 0))
    hd = lambda w: pl.BlockSpec((hh, tm, w), lambda i: (0, i, 0))
    return pl.pallas_call(
        body, name="bwd_mla_proj", grid=(s // tm,),
        in_specs=[pl.BlockSpec((tm, 256), lambda i: (i, Z_CQ // 256)), pl.BlockSpec((tm, 256), lambda i: (i, Z_CKV // 256)),
                  pl.BlockSpec((tm, 128), lambda i: (i, Z_KR // 128)), hd(QK_PAD), hd(QK_PAD), hd(V_HEAD),
                  row128, row128, row128,
                  _const_spec((1, 256)), _const_spec((1, 256)), _const_spec((256, 1024)), _const_spec((256, 1024)),
                  _const_spec((1, 256)), _const_spec((1, 256))],
        out_specs=[pl.BlockSpec((tm, 640), lambda i: (i, 0)), row256, row256, row1024, row1024,
                   _acc_spec((1, 256)), _acc_spec((1, 256)), _acc_spec((1, 256)), _acc_spec((1, 256))],
        out_shape=[jax.ShapeDtypeStruct((s, 640), bf16), jax.ShapeDtypeStruct((s, 256), bf16), jax.ShapeDtypeStruct((s, 256), bf16),
                   jax.ShapeDtypeStruct((s, 1024), bf16), jax.ShapeDtypeStruct((s, 1024), bf16)]
        + [jax.ShapeDtypeStruct((1, 256), f32)] * 4,
        compiler_params=_cp(("arbitrary",), VMEM_LIMIT),
    )(z, z, z, dq, dk, dv, cosb, sina, sinb, g_qa, g_kva, wqb, wkvb, g_qn, g_kn)


def _bwd_in(segments, wz, x, g_mix, d2, tm):
    s, d = x.shape
    n_seg = len(segments)

    def body(*refs):
        dz_refs, w_refs = refs[:n_seg], refs[n_seg:2 * n_seg]
        x_ref, g_ref, d2_ref, gx_ref, dg_ref = refs[2 * n_seg:]

        @pl.when(pl.program_id(0) == 0)
        def _():
            dg_ref[...] = jnp.zeros_like(dg_ref)

        dh = _dot_nt(dz_refs[0][...], w_refs[0][...])
        for a_ref, w_ref in zip(dz_refs[1:], w_refs[1:]):
            dh = dh + _dot_nt(a_ref[...], w_ref[...])
        x, g = x_ref[...], g_ref[...]
        r = lax.rsqrt(jnp.sum(x * x, axis=-1, keepdims=True) * (1.0 / d) + EPS)
        dx, dgx = _rms_bwd(dh, x, r, g, d)
        gx_ref[...] = d2_ref[...] + dx
        dg_ref[...] += jnp.sum(dgx, axis=0, keepdims=True)

    rowd = pl.BlockSpec((tm, d), lambda i: (i, 0))
    dz_specs = [pl.BlockSpec((tm, w), functools.partial(lambda i, j: (i, j), j=ja)) for _, w, ja, _ in segments]
    w_specs = [pl.BlockSpec((d, w), functools.partial(lambda i, j: (0, j), j=jw), pipeline_mode=pl.Buffered(1))
               for _, w, _, jw in segments]
    return pl.pallas_call(
        body, name="bwd_in", grid=(s // tm,),
        in_specs=dz_specs + w_specs + [rowd, _const_spec((1, d)), rowd],
        out_specs=[rowd, _acc_spec((1, d))],
        out_shape=[jax.ShapeDtypeStruct((s, d), f32), jax.ShapeDtypeStruct((1, d), f32)],
        compiler_params=_cp(("arbitrary",), VMEM_LIMIT),
    )(*[a for a, _, _, _ in segments], *([wz] * n_seg), x, g_mix, d2)


def _pick_tile(n, cap):
    best = None
    for t in range(LANES, cap + 1, LANES):
        if n % t == 0:
            best = t
    return best if best is not None else n


def _mm_tn_many(a, bs, name, tm):
    kk, m = a.shape
    n_b = len(bs)
    tk = min(512, kk)
    n_k = kk // tk

    def body(a_ref, *refs):
        b_refs, o_refs, acc_refs = refs[:n_b], refs[n_b:2 * n_b], refs[2 * n_b:]

        @pl.when(pl.program_id(1) == 0)
        def _():
            for acc in acc_refs:
                acc[...] = jnp.zeros_like(acc)
        a_blk = a_ref[...].astype(bf16)
        for b_ref, acc in zip(b_refs, acc_refs):
            acc[...] += _dot_tn(a_blk, b_ref[...].astype(bf16))

        @pl.when(pl.program_id(1) == n_k - 1)
        def _():
            for o_ref, acc in zip(o_refs, acc_refs):
                o_ref[...] = acc[...].astype(bf16)

    return pl.pallas_call(
        body, name=name, grid=(m // tm, n_k),
        in_specs=[pl.BlockSpec((tk, tm), lambda i, k: (k, i))] + [pl.BlockSpec((tk, b.shape[1]), lambda i, k: (k, 0)) for b in bs],
        out_specs=[pl.BlockSpec((tm, b.shape[1]), lambda i, k: (i, 0)) for b in bs],
        out_shape=[jax.ShapeDtypeStruct((m, b.shape[1]), bf16) for b in bs],
        scratch_shapes=[pltpu.VMEM((tm, b.shape[1]), f32) for b in bs],
        compiler_params=_cp(("parallel", "arbitrary"), VMEM_LIMIT),
    )(a, *bs)


def _mm_tn(a, b, name):
    kk, m = a.shape
    _, n = b.shape
    tm = _pick_tile(m, 1408)
    tn = _pick_tile(n, 1408)
    tk = min(512, kk)

    n_k = kk // tk

    def body(a_ref, b_ref, o_ref, acc_ref):
        @pl.when(pl.program_id(2) == 0)
        def _():
            acc_ref[...] = jnp.zeros_like(acc_ref)
        acc_ref[...] += _dot_tn(a_ref[...].astype(bf16), b_ref[...].astype(bf16))

        @pl.when(pl.program_id(2) == n_k - 1)
        def _():
            o_ref[...] = acc_ref[...].astype(bf16)

    return pl.pallas_call(
        body, name=name, grid=(m // tm, n // tn, n_k),
        in_specs=[pl.BlockSpec((tk, tm), lambda i, j, k: (k, i)), pl.BlockSpec((tk, tn), lambda i, j, k: (k, j))],
        out_specs=pl.BlockSpec((tm, tn), lambda i, j, k: (i, j)),
        out_shape=jax.ShapeDtypeStruct((m, n), bf16),
        scratch_shapes=[pltpu.VMEM((tm, tn), f32)],
        compiler_params=_cp(("parallel", "parallel", "arbitrary"), VMEM_LIMIT),
    )(a, b)


def _rope_tables(positions):
    inv_freq = ROPE_THETA ** (-jnp.arange(0, QK_ROPE, 2, dtype=f32) / QK_ROPE)
    ang = positions.astype(f32)[:, None] * inv_freq
    cos, sin = jnp.cos(ang), jnp.sin(ang)
    zero = jnp.zeros_like(cos)
    return (jnp.concatenate([cos, cos, zero, zero], axis=1), jnp.concatenate([zero, sin, zero, zero], axis=1),
            jnp.concatenate([-sin, zero, zero, zero], axis=1))


def _pad256(g):
    return jnp.pad(g.reshape(1, QK_HEAD), ((0, 0), (0, QK_PAD - QK_HEAD)))


RELAYOUT_BLOCKS = 8
FIRST = ("w_in", "w_qb", "w_kvb", "lb_param")
SECOND = ("w_o", "w_gate", "w_up", "w_down", "w_ple_gate", "w_ple_proj")
ROW_SHARDED = ("w_o", "w_down", "w_ple_gate")


def _col_moves(j):
    lo = BIG["w_in"][1] * j
    w_in = [(max(lo, a) - lo, min(lo + BIG["w_in"][1], b) - lo, d + max(lo, a) - a)
            for a, b, d in Z_SEGMENTS if max(lo, a) < min(lo + BIG["w_in"][1], b)]
    head, half = divmod(j, 2)
    whole = lambda n: [(0, BIG[n][1], BIG[n][1] * j)]
    return {"w_in": w_in, "w_gate": whole("w_gate"), "w_up": whole("w_up"),
            "w_qb": [(0, 96, QK_PAD * head + 96 * half)], "w_kvb": whole("w_kvb"), "w_ple_proj": whole("w_ple_proj"),
            "lb_param": whole("lb_param")}


def _kernel_width(name):
    return {"w_in": Z_W, "w_qb": MLA_HEADS * QK_PAD}.get(name, N_DEV * BIG[name][1])


def _relayout_specs(names, by_dev):
    specs = []
    for n in names:
        rows, cols = BIG[n]
        if n == "lb_param":
            specs.append(_acc_spec((N_DEV, rows, cols) if by_dev else (rows, _kernel_width(n))))
        elif by_dev:
            specs.append(pl.BlockSpec((N_DEV, rows // RELAYOUT_BLOCKS, cols), lambda i: (0, i, 0)))
        else:
            specs.append(pl.BlockSpec((rows // RELAYOUT_BLOCKS, _kernel_width(n)), lambda i: (i, 0)))
    return specs


def _weights_in(gathered, names, name):
    n = len(names)

    def body(*refs):
        ins, outs = dict(zip(names, refs[:n])), dict(zip(names, refs[n:]))
        if "w_in" in outs:
            outs["w_in"][:, Z_KR + QK_ROPE:Z_W] = jnp.zeros((outs["w_in"].shape[0], Z_W - Z_KR - QK_ROPE), bf16)
        if "w_qb" in outs:
            for h in range(MLA_HEADS):
                outs["w_qb"][:, QK_PAD * h + QK_HEAD:QK_PAD * (h + 1)] = jnp.zeros((outs["w_qb"].shape[0], QK_PAD - QK_HEAD), bf16)
        for j in range(N_DEV):
            for wn, moves in _col_moves(j).items():
                if wn in outs:
                    for s0, s1, d0 in moves:
                        outs[wn][:, d0:d0 + s1 - s0] = ins[wn][j, :, s0:s1]

    outs = pl.pallas_call(
        body, name=name, grid=(RELAYOUT_BLOCKS,), in_specs=_relayout_specs(names, True), out_specs=_relayout_specs(names, False),
        out_shape=[jax.ShapeDtypeStruct((BIG[wn][0], _kernel_width(wn)), gathered[wn].dtype) for wn in names],
        compiler_params=_cp(("arbitrary",), VMEM_LIMIT),
    )(*[gathered[wn] for wn in names])
    return dict(zip(names, outs))


def _grads_out(sources, names, name):
    pieces = [(wn, start, arr) for wn in names for start, arr in sources[wn]]
    n_in = len(pieces)

    def body(*refs):
        outs = dict(zip(names, refs[n_in:]))

        def cols(wn, c0, c1):
            for (pn, start, arr), ref in zip(pieces, refs[:n_in]):
                if pn == wn and start <= c0 and c1 <= start + arr.shape[1]:
                    return ref[:, c0 - start:c1 - start]

        for j in range(N_DEV):
            for wn, moves in _col_moves(j).items():
                if wn in outs:
                    for s0, s1, d0 in moves:
                        outs[wn][j, :, s0:s1] = cols(wn, d0, d0 + s1 - s0).astype(bf16)

    in_specs = [_acc_spec(arr.shape) if wn == "lb_param" else pl.BlockSpec((arr.shape[0] // RELAYOUT_BLOCKS, arr.shape[1]), lambda i: (i, 0))
                for wn, _, arr in pieces]
    outs = pl.pallas_call(
        body, name=name, grid=(RELAYOUT_BLOCKS,), in_specs=in_specs, out_specs=_relayout_specs(names, True),
        out_shape=[jax.ShapeDtypeStruct((N_DEV, *BIG[wn]), bf16) for wn in names],
        compiler_params=_cp(("arbitrary",), VMEM_LIMIT),
    )(*[arr for _, _, arr in pieces])
    return dict(zip(names, outs))


def kernel(x, p, positions, g_mix, w_in, g_qa, g_kva, w_qb, w_kvb, g_qn, g_kn, lb_param, g_hgo, w_o, g_ffn, w_gate, w_up, w_down, g_ple, w_ple_gate, w_ple_proj, loss_target, m_g_mix, m_w_in, m_g_qa, m_g_kva, m_w_qb, m_w_kvb, m_g_qn, m_g_kn, m_lb_param, m_g_hgo, m_w_o, m_g_ffn, m_w_gate, m_w_up, m_w_down, m_g_ple, m_w_ple_gate, m_w_ple_proj, v_g_mix, v_w_in, v_g_qa, v_g_kva, v_w_qb, v_w_kvb, v_g_qn, v_g_kn, v_lb_param, v_g_hgo, v_w_o, v_g_ffn, v_w_gate, v_w_up, v_w_down, v_g_ple, v_w_ple_gate, v_w_ple_proj):
    w_all = dict(g_mix=g_mix, g_qa=g_qa, g_kva=g_kva, g_qn=g_qn, g_kn=g_kn, g_hgo=g_hgo, g_ffn=g_ffn, g_ple=g_ple,
                 w_in=w_in, w_qb=w_qb, w_kvb=w_kvb, w_o=w_o, w_gate=w_gate, w_up=w_up, w_down=w_down,
                 w_ple_gate=w_ple_gate, w_ple_proj=w_ple_proj, lb_param=lb_param)
    m_all = dict(g_mix=m_g_mix, g_qa=m_g_qa, g_kva=m_g_kva, g_qn=m_g_qn, g_kn=m_g_kn, g_hgo=m_g_hgo, g_ffn=m_g_ffn,
                 g_ple=m_g_ple, w_in=m_w_in, w_qb=m_w_qb, w_kvb=m_w_kvb, w_o=m_w_o, w_gate=m_w_gate, w_up=m_w_up,
                 w_down=m_w_down, w_ple_gate=m_w_ple_gate, w_ple_proj=m_w_ple_proj, lb_param=m_lb_param)
    v_all = dict(g_mix=v_g_mix, g_qa=v_g_qa, g_kva=v_g_kva, g_qn=v_g_qn, g_kn=v_g_kn, g_hgo=v_g_hgo, g_ffn=v_g_ffn,
                 g_ple=v_g_ple, w_in=v_w_in, w_qb=v_w_qb, w_kvb=v_w_kvb, w_o=v_w_o, w_gate=v_w_gate, w_up=v_w_up,
                 w_down=v_w_down, w_ple_gate=v_w_ple_gate, w_ple_proj=v_w_ple_proj, lb_param=v_lb_param)
    me_idx = jnp.stack([_me()]).astype(jnp.int32)
    x, p, positions, target = x[0], p[0, 0], positions[0], loss_target[0]
    s = x.shape[0]
    tm, tm_ffn, tq_f, tq_b = min(256, s), min(1024, s), min(2048, s), min(1024, s)
    g_mix, g_qa, g_kva, g_qn, g_kn, g_hgo, g_ffn, g_ple = (w_all[n].reshape(1, -1) for n in SMALL)
    g_qn_p, g_kn_p = _pad256(g_qn), _pad256(g_kn)
    cosb, sina, sinb = _rope_tables(positions)
    shard = lambda n: w_all[n].reshape(BIG[n])

    first = _all_gather([shard(n) for n in FIRST], [f32 if n == "lb_param" else bf16 for n in FIRST], "ag_first")
    lands = _cast_to_slot([shard(n) for n in SECOND], me_idx, first[0])
    ag2, token = _exchange_start([], lands, "ag_second_start")
    wk = _weights_in(dict(zip(FIRST, first)), FIRST, "weights_in_first")
    wz, wqb, wkvb, lb4 = (wk[n] for n in FIRST)

    h1, z = _fwd_in(x, g_mix, wz, tm)
    q, k, v = _fwd_mla_proj(z, cosb + token[0, 0], sina, sinb, g_qa, g_kva, wqb, wkvb, g_qn_p, g_kn_p, tm)
    a, a32 = _fwd_attn(q, k, v, tq_f)
    o = _fwd_gla(z, lb4)

    second = dict(zip(SECOND, _exchange_wait(ag2, [a, o], "ag_second_wait")[1]))
    wk = _weights_in(second, ("w_gate", "w_up", "w_ple_proj"), "weights_in_second")
    w_gate, w_up, w_pp = wk["w_gate"], wk["w_up"], wk["w_ple_proj"]
    w_o, w_down, w_pg = (second[n].reshape(N_DEV * BIG[n][0], BIG[n][1]) for n in ROW_SHARDED)

    x2, cat = _fwd_mix(a, o, z, g_hgo, x, w_o, tm)
    x3, gp, up = _fwd_ffn(x2, g_ffn, w_gate, w_up, w_down, tm)
    d3, h3, dpre, dpp, dg_ple, loss_tile = _ple_loss_fwd_bwd(x3, g_ple, w_pg, p, w_pp, target, tm)
    d2, h2, act, dgp, dup, dg_ffn = _bwd_ffn(d3, x2, gp, up, g_ffn, w_gate, w_up, w_down, tm_ffn, 256)

    gw_gate, gw_up = _mm_tn_many(h2, [dgp, dup], "dw_gate_up", 512)
    blocks = _grads_out({"w_gate": [(0, gw_gate)], "w_up": [(0, gw_up)], "w_ple_proj": [(0, _mm_tn(p, dpp, "dw_ple_proj"))]},
                        ("w_gate", "w_up", "w_ple_proj"), "grads_out_second")
    row_grads = {"w_o": _mm_tn(cat, d2, "dw_o"), "w_down": _mm_tn(act, d3, "dw_down"), "w_ple_gate": _mm_tn(h3, dpre, "dw_ple_gate")}
    blocks.update({n: g.reshape(N_DEV, *BIG[n]) for n, g in row_grads.items()})
    empty = lambda names: [lax.empty((N_PEERS, *BIG[n]), bf16) for n in names]
    rs2, token = _exchange_start([blocks[n] for n in SECOND], empty(SECOND), "rs_second_start")

    da, do, dz_hg, dg_hgo = _bwd_mix(d2, w_o, o, z, g_hgo + token[0, 0], tm)
    dz_hq, dz_hff, dz_hfb, dz_hi, dlb4 = _bwd_gla(z, lb4, do)
    dq, dk, dv = _bwd_attn(q, k, v, da, a32, tq_b)
    dz_mla, cqn, ckvn, dq0, dkv0, dg_qa, dg_kva, dg_qn, dg_kn = _bwd_mla_proj(
        z, dq, dk, dv, cosb, sina, sinb, g_qa, g_kva, wqb, wkvb, g_qn_p, g_kn_p, tm)

    gz = list(zip((Z_HQ, Z_HFF, Z_HFB, Z_HI, Z_HG, Z_CQ),
                  _mm_tn_many(h1, [dz_hq, dz_hff, dz_hfb, dz_hi, dz_hg, dz_mla], "dw_in", 1024)))
    blocks1 = _grads_out({"w_in": gz, "w_qb": [(0, _mm_tn(cqn, dq0, "dw_qb"))], "w_kvb": [(0, _mm_tn(ckvn, dkv0, "dw_kvb"))],
                          "lb_param": [(0, dlb4)]}, FIRST, "grads_out_first")
    rs1, token = _exchange_start([blocks1[n] for n in FIRST], empty(FIRST), "rs_first_start")

    result = {}

    def adam(names, lands, src, n_blocks, after=()):
        outs = _adam_shards(me_idx, [src[n] for n in names], lands, [w_all[n] for n in names], [m_all[n] for n in names],
                            [v_all[n] for n in names], n_blocks, "adamw_" + names[0], after)
        result.update(zip(names, outs))
        return outs[0][0]

    blocks2, lands2 = (dict(zip(SECOND, arrs)) for arrs in _exchange_wait(rs2, [token], "rs_second_wait"))
    by8 = tuple(n for n in SECOND if n != "w_down")
    done = [adam(by8, [lands2[n] for n in by8], blocks2, 8), adam(("w_down",), [lands2["w_down"]], blocks2, 2)]

    segments = [(dz_hq, 512, 0, Z_HQ // 512), (dz_hff, 512, 0, Z_HFF // 512), (dz_hfb, 512, 0, Z_HFB // 512),
                (dz_hi, 512, 0, Z_HI // 512), (dz_hg, 512, 0, Z_HG // 512), (dz_mla, 640, 0, Z_CQ // 640)]
    grad_x, dg_mix = _bwd_in(segments, wz, x, g_mix + token[0, 0], d2, tm)
    dgains = (dg_mix, dg_qa, dg_kva, dg_qn, dg_kn, dg_hgo, dg_ffn, dg_ple)

    vec = jnp.concatenate(list(dgains) + [loss_tile[0:1]], axis=1)
    parts = _all_gather([vec], [f32], "ag_gains")[0]
    outs, loss_row = _adam_gains(parts, [w_all[n] for n in SMALL], [m_all[n] for n in SMALL], [v_all[n] for n in SMALL])
    result.update(zip(SMALL, outs))

    blocks1, lands1 = _exchange_wait(rs1, [grad_x, loss_row, *done], "rs_first_wait")
    adam(FIRST, lands1, dict(zip(FIRST, blocks1)), 8)

    order = ("g_mix", "w_in", "g_qa", "g_kva", "w_qb", "w_kvb", "g_qn", "g_kn", "lb_param", "g_hgo", "w_o", "g_ffn",
             "w_gate", "w_up", "w_down", "g_ple", "w_ple_gate", "w_ple_proj")
    return (loss_row[0, 0], grad_x[None], *[result[n][k] for k in range(4) for n in order])
```

```python
import functools
import math

import jax
import jax.numpy as jnp
from jax import lax
from jax.experimental import pallas as pl
from jax.experimental.pallas import tpu as pltpu

f32 = jnp.float32
bf16 = jnp.bfloat16

N_DEV = 8
D_MODEL = 1024
MLA_HEADS = 4
QK_NOPE = 128
QK_ROPE = 64
QK_HEAD = QK_NOPE + QK_ROPE
QK_PAD = 256
V_HEAD = 128
Q_LORA = 256
KV_LORA = 256
HG_HEADS = 4
HG_DK = 128
CHUNK = 64
D_FF = 2816
PLE_DIM = 256
ROPE_THETA = 10000.0
EPS = 1e-6
ATTN_SCALE = QK_HEAD ** -0.5
LOG2_E = math.log2(math.e)
ATTN_SUB_ROWS = 256
IN_SIZES = (256, 256, 64, 512, 512, 512, 512, 512)
D_IN = sum(IN_SIZES)
Z_HQ, Z_HFF, Z_HFB, Z_HI, Z_HG, Z_CQ, Z_CKV, Z_KR, Z_W = 0, 512, 1024, 1536, 2048, 2560, 2816, 3072, 3200

ADAM_LR, ADAM_B1, ADAM_B2, ADAM_EPS, ADAM_WD, ADAM_STEP = 0.001, 0.9, 0.999, 1e-08, 0.01, 10

LANES = 128
BIG = {"w_in": (1024, 392), "w_qb": (256, 96), "w_kvb": (256, 128), "w_o": (128, 1024), "w_gate": (1024, 352),
       "w_up": (1024, 352), "w_down": (352, 1024), "w_ple_gate": (128, 1024), "w_ple_proj": (256, 128),
       "lb_param": (4, 64)}
ROW_BLOCKS = {("w_in", "w_qb", "w_kvb", "w_o", "w_gate", "w_up", "w_ple_gate", "w_ple_proj"): 8, ("w_down", "lb_param"): 2}
SMALL = {"g_mix": (0, 1024), "g_qa": (1024, 256), "g_kva": (1280, 256), "g_qn": (1536, 192), "g_kn": (1792, 192),
         "g_hgo": (2048, 512), "g_ffn": (2560, 1024), "g_ple": (3584, 1024)}
LOSS_OFF = 4608
GAIN_VEC = LOSS_OFF + LANES
Z_SEGMENTS = ((0, 256, Z_CQ), (256, 512, Z_CKV), (512, 576, Z_KR), (576, 1088, Z_HQ), (1088, 1600, Z_HFF),
              (1600, 2112, Z_HFB), (2112, 2624, Z_HI), (2624, 3136, Z_HG))

VMEM_LIMIT = 56 * 1024 * 1024
MESH = pl.DeviceIdType.MESH


def _cp(sem=None, vmem=None):
    return pltpu.CompilerParams(dimension_semantics=sem, vmem_limit_bytes=vmem)


def _const_spec(shape):
    nd = len(shape)
    return pl.BlockSpec(shape, lambda *_: (0,) * nd, pipeline_mode=pl.Buffered(1))


def _acc_spec(shape):
    nd = len(shape)
    return pl.BlockSpec(shape, lambda *_: (0,) * nd)


def _sigmoid(x):
    return jax.nn.sigmoid(x)


def _dot(a, b):
    return jnp.dot(a, b, preferred_element_type=f32)


def _dot_nt(a, b):
    return lax.dot_general(a, b, (((1,), (1,)), ((), ())), preferred_element_type=f32)


def _dot_tn(a, b):
    return lax.dot_general(a, b, (((0,), (0,)), ((), ())), preferred_element_type=f32)


def _rms_fwd(x, g, width):
    r = lax.rsqrt(jnp.sum(x * x, axis=-1, keepdims=True) * (1.0 / width) + EPS)
    return x * r * g, r


def _rms_bwd(dy, x, r, g, width):
    u = dy * g
    dx = r * u - x * (r * r * r) * (jnp.sum(u * x, axis=-1, keepdims=True) * (1.0 / width))
    return dx, dy * x * r


def _rope(b, c, sa, sb):
    return b * c + pltpu.roll(b, 32, 1) * sa + pltpu.roll(b, 96, 1) * sb


def _all_gather(shards, dtypes, name):
    n = len(shards)

    def body(*refs):
        in_refs, out_refs, stage = refs[:n], refs[n:2 * n], refs[2 * n:3 * n]
        send_sems, recv_sems, local_sems = refs[3 * n:]
        for w in range(n):
            stage[w][...] = in_refs[w][...].astype(stage[w].dtype)
        x, y, c = lax.axis_index("x"), lax.axis_index("y"), lax.axis_index("c")
        me, sibling = (x, y, c), (x, y, 1 - c)
        chips = [(1 - x, y), (x, 1 - y), (1 - x, 1 - y)]

        def slot(w, px, py, pc):
            return out_refs[w].at[4 * px + 2 * py + pc]

        def copy(w, k, block, to, src=None):
            return pltpu.make_async_remote_copy(
                src_ref=slot(w, *block) if src is None else src, dst_ref=slot(w, *block),
                send_sem=send_sems.at[w, k], recv_sem=recv_sems.at[w, k], device_id=to, device_id_type=MESH)

        first = []
        for j, chip in enumerate(chips):
            first += [copy(w, 1 + j, me, (*chip, c), src=stage[w]) for w in range(n)]
        first += [copy(w, 0, me, sibling, src=stage[w]) for w in range(n)]
        mine = [pltpu.make_async_copy(stage[w], slot(w, *me), local_sems.at[w]) for w in range(n)]
        for cp in first + mine:
            cp.start()
        passed = []
        for j, chip in enumerate(chips):
            for w in range(n):
                copy(w, 1 + j, (*chip, c), me).wait_recv()
                passed.append(copy(w, 4 + j, (*chip, c), sibling))
                passed[-1].start()
        for w in range(n):
            copy(w, 0, sibling, me).wait_recv()
        for j, chip in enumerate(chips):
            for w in range(n):
                copy(w, 4 + j, (*chip, 1 - c), me).wait_recv()
        for cp in first + passed:
            cp.wait_send()
        for cp in mine:
            cp.wait()

    return pl.pallas_call(
        body, name=name,
        out_shape=[jax.ShapeDtypeStruct((N_DEV, *s.shape), dt) for s, dt in zip(shards, dtypes)],
        in_specs=[pl.BlockSpec(memory_space=pltpu.VMEM)] * n,
        out_specs=[pl.BlockSpec(memory_space=pl.ANY)] * n,
        scratch_shapes=[pltpu.VMEM(s.shape, dt) for s, dt in zip(shards, dtypes)]
        + [pltpu.SemaphoreType.DMA((n, 7)), pltpu.SemaphoreType.DMA((n, 7)), pltpu.SemaphoreType.DMA((n,))],
        compiler_params=_cp(None, VMEM_LIMIT),
    )(*shards)


N_PEERS = N_DEV - 1
HBM_SPEC = pl.BlockSpec(memory_space=pltpu.HBM)
SEM_SPEC = pl.BlockSpec(memory_space=pltpu.SEMAPHORE)
DATAFLOW = pltpu.SideEffectType.DATAFLOW_SIDE_EFFECTING


def _me():
    return 4 * lax.axis_index("x") + 2 * lax.axis_index("y") + lax.axis_index("c")


def _peer(k):
    x, y, c = lax.axis_index("x"), lax.axis_index("y"), lax.axis_index("c")
    px = 1 - x if k & 4 else x
    py = 1 - y if k & 2 else y
    pc = 1 - c if k & 1 else c
    return (px, py, pc), 4 * px + 2 * py + pc


def _exchange_copies(src_refs, land_refs, send_sems, recv_sems, gather):
    cps = []
    me = _me()
    for k in range(1, N_DEV):
        peer, peer_idx = _peer(k)
        for w, land in enumerate(land_refs):
            src = land.at[me] if gather else src_refs[w].at[peer_idx]
            dst = land.at[me] if gather else land.at[k - 1]
            cps.append(pltpu.make_async_remote_copy(
                src_ref=src, dst_ref=dst, send_sem=send_sems.at[N_PEERS * w + k - 1], recv_sem=recv_sems.at[N_PEERS * w + k - 1],
                device_id=peer, device_id_type=MESH))
    return cps


def _exchange_start(srcs, lands, name):
    n_src, n = len(srcs), len(lands)

    def body(*refs):
        src_refs, land_refs = refs[:n_src], refs[n_src:n_src + n]
        send_sems, recv_sems = refs[n_src + n], refs[n_src + n + 1]
        token = refs[-1]
        for cp in _exchange_copies(src_refs, land_refs, send_sems, recv_sems, gather=not n_src):
            cp.start()
        token[...] = jnp.zeros_like(token)

    arrays = [pltpu.with_memory_space_constraint(a, pltpu.HBM) for a in (*srcs, *lands)]
    outs = pl.pallas_call(
        body, name=name,
        out_shape=(pltpu.SemaphoreType.DMA((n * N_PEERS,)), pltpu.SemaphoreType.DMA((n * N_PEERS,)),
                   *[pltpu.HBM(a.shape, a.dtype) for a in arrays], jax.ShapeDtypeStruct((8, LANES), f32)),
        in_specs=[HBM_SPEC] * len(arrays),
        out_specs=(SEM_SPEC, SEM_SPEC, *[HBM_SPEC] * len(arrays), pl.BlockSpec(memory_space=pltpu.VMEM)),
        input_output_aliases={i: 2 + i for i in range(len(arrays))},
        compiler_params=pltpu.CompilerParams(has_side_effects=DATAFLOW),
    )(*arrays)
    return (outs[0], outs[1], outs[2:2 + n_src], outs[2 + n_src:2 + n_src + n]), outs[-1]


def _exchange_wait(state, after, name):
    send_sems, recv_sems, srcs, lands = state
    n_src, n = len(srcs), len(lands)

    def body(*refs):
        src_refs, land_refs = refs[:n_src], refs[n_src:n_src + n]
        send_ref, recv_ref = refs[n_src + n], refs[n_src + n + 1]
        for cp in _exchange_copies(src_refs, land_refs, send_ref, recv_ref, gather=not n_src):
            cp.wait_send()
            cp.wait_recv()

    arrays = (*srcs, *lands)
    outs = pl.pallas_call(
        body, name=name,
        out_shape=tuple(pltpu.HBM(a.shape, a.dtype) for a in arrays),
        in_specs=[HBM_SPEC] * len(arrays) + [SEM_SPEC, SEM_SPEC] + [pl.BlockSpec(memory_space=pl.ANY)] * len(after),
        out_specs=tuple([HBM_SPEC] * len(arrays)),
        input_output_aliases={i: i for i in range(len(arrays))},
        compiler_params=pltpu.CompilerParams(has_side_effects=DATAFLOW),
    )(*arrays, send_sems, recv_sems, *after)
    return outs[:n_src], outs[n_src:]


def _cast_to_slot(shards, me_idx, after):
    n = len(shards)

    def body(i_ref, *refs):
        for w in range(n):
            refs[n + 1 + w][...] = refs[w][...].astype(bf16)

    return pl.pallas_call(
        body, name="cast_to_slot",
        grid_spec=pltpu.PrefetchScalarGridSpec(
            num_scalar_prefetch=1, grid=(1,),
            in_specs=[pl.BlockSpec(s.shape, lambda i, m: (0, 0)) for s in shards] + [pl.BlockSpec(memory_space=pl.ANY)],
            out_specs=[pl.BlockSpec((None, *s.shape), lambda i, m: (m[0], 0, 0)) for s in shards]),
        out_shape=[jax.ShapeDtypeStruct((N_DEV, *s.shape), bf16) for s in shards],
        compiler_params=_cp(("arbitrary",), VMEM_LIMIT),
    )(me_idx, *shards, after)


def _row_block(rows, n_blocks):
    return (rows // n_blocks, True) if rows % (16 * n_blocks) == 0 else (rows, False)


def _adam_math(w, g, m, v):
    m = ADAM_B1 * m + (1.0 - ADAM_B1) * g
    v = ADAM_B2 * v + (1.0 - ADAM_B2) * (g * g)
    m_hat = m / (1.0 - ADAM_B1 ** ADAM_STEP)
    v_hat = v / (1.0 - ADAM_B2 ** ADAM_STEP)
    delta = -ADAM_LR * (m_hat / (jnp.sqrt(v_hat) + ADAM_EPS) + ADAM_WD * w)
    return delta, m, v


def _adam_shards(me_idx, blocks, lands, ws, ms, vs, n_blocks, name, after=()):
    n = len(blocks)

    def body(i_ref, *refs):
        ins, outs = refs[:5 * n], refs[5 * n + len(after):]
        for w in range(n):
            g_ref, b_ref, w_ref, m_ref, v_ref = (ins[t * n + w] for t in range(5))
            g = g_ref[...].astype(f32)
            for k in range(N_PEERS):
                g = g + b_ref[k].astype(f32)
            if len(w_ref.shape) == 2:
                pieces = [(slice(None), g)]
            else:
                pieces = [(a, g[2 * a:2 * a + 2]) for a in range(2)]
            for at, gp in pieces:
                vals = (gp,) + _adam_math(w_ref[at], gp, m_ref[at], v_ref[at])
                for t, val in enumerate(vals):
                    outs[4 * w + t][at] = val

    specs = [[] for _ in range(5)]
    out_specs, out_shape = [], []
    for g, wt in zip(blocks, ws):
        rows, cols = g.shape[1:]
        rb, cut = _row_block(rows, n_blocks)
        specs[0].append(pl.BlockSpec((None, rb, cols), functools.partial(lambda i, s, cut: (s[0], i if cut else 0, 0), cut=cut)))
        specs[1].append(pl.BlockSpec((N_PEERS, rb, cols), functools.partial(lambda i, s, cut: (0, i if cut else 0, 0), cut=cut)))
        if wt.shape[0] == 1:
            shard = pl.BlockSpec((None, rb, cols), functools.partial(lambda i, s, cut: (0, i if cut else 0, 0), cut=cut))
        else:
            shard = pl.BlockSpec(wt.shape, functools.partial(lambda i, s, nd: (0,) * nd, nd=wt.ndim))
        for t in (2, 3, 4):
            specs[t].append(shard)
        out_specs += [shard] * 4
        out_shape += [jax.ShapeDtypeStruct(wt.shape, f32)] * 4
    outs = pl.pallas_call(
        body, name=name,
        grid_spec=pltpu.PrefetchScalarGridSpec(
            num_scalar_prefetch=1, grid=(n_blocks,), in_specs=sum(specs, []) + [pl.BlockSpec(memory_space=pl.ANY)] * len(after),
            out_specs=out_specs),
        out_shape=out_shape,
        compiler_params=_cp(("arbitrary",), VMEM_LIMIT),
    )(me_idx, *blocks, *lands, *ws, *ms, *vs, *after)
    return [outs[4 * w:4 * w + 4] for w in range(n)]


def _adam_gains(parts, ws, ms, vs):
    n = len(ws)

    def body(p_ref, *refs):
        ins, outs = refs[:3 * n], refs[3 * n:]
        g_all = p_ref[0]
        for k in range(1, N_DEV):
            g_all = g_all + p_ref[k]
        for w, (off, lanes) in enumerate(SMALL.values()):
            w_ref, m_ref, v_ref = ins[w], ins[n + w], ins[2 * n + w]
            if len(w_ref.shape) == 2:
                pieces = [(slice(None), off, lanes)]
            else:
                pieces = [((slice(None), h), off + LANES * h, LANES) for h in range(w_ref.shape[1])]
            for at, o, ln in pieces:
                g = g_all[:, o:o + ln]
                vals = (g,) + _adam_math(w_ref[at], g, m_ref[at], v_ref[at])
                for t, val in enumerate(vals):
                    outs[4 * w + t][at] = val
        outs[4 * n][...] = g_all[:, LOSS_OFF:LOSS_OFF + LANES]

    out_shape = sum([[jax.ShapeDtypeStruct(w.shape, f32)] * 4 for w in ws], []) + [jax.ShapeDtypeStruct((1, LANES), f32)]
    outs = pl.pallas_call(body, name="adamw_gains", out_shape=out_shape)(parts, *ws, *ms, *vs)
    return [outs[4 * w:4 * w + 4] for w in range(n)], outs[4 * n]


def _fwd_in(x, g_mix, wz, tm):
    s, d = x.shape

    def body(x_ref, g_ref, w_ref, h_ref, z_ref):
        h, _ = _rms_fwd(x_ref[...], g_ref[...], d)
        hb = h.astype(bf16)
        h_ref[...] = hb
        z_ref[...] = _dot(hb, w_ref[...])

    return pl.pallas_call(
        body, name="fwd_in", grid=(s // tm,),
        in_specs=[pl.BlockSpec((tm, d), lambda i: (i, 0)), _const_spec((1, d)), _const_spec((d, Z_W))],
        out_specs=[pl.BlockSpec((tm, d), lambda i: (i, 0)), pl.BlockSpec((tm, Z_W), lambda i: (i, 0))],
        out_shape=[jax.ShapeDtypeStruct((s, d), bf16), jax.ShapeDtypeStruct((s, Z_W), f32)],
        compiler_params=_cp(("parallel",), VMEM_LIMIT),
    )(x, g_mix, wz)


def _mla_qk_fwd(cq, ckv, kr, g_qa, g_kva, wqb, wkvb, g_qn, g_kn):
    cqn, rq = _rms_fwd(cq, g_qa, Q_LORA)
    ckvn, rkv = _rms_fwd(ckv, g_kva, KV_LORA)
    cqn_b, ckvn_b = cqn.astype(bf16), ckvn.astype(bf16)
    q0 = _dot(cqn_b, wqb)
    kv0 = _dot(ckvn_b, wkvb)
    return cqn_b, rq, ckvn_b, rkv, q0, kv0


def _fwd_mla_proj(z, cosb, sina, sinb, g_qa, g_kva, wqb, wkvb, g_qn, g_kn, tm):
    s = z.shape[0]
    hh = MLA_HEADS

    def body(cq_ref, ckv_ref, kr_ref, c_ref, sa_ref, sb_ref, gqa_ref, gkva_ref, wqb_ref, wkvb_ref, gqn_ref, gkn_ref,
             q_ref, k_ref, v_ref):
        _, _, _, _, q0, kv0 = _mla_qk_fwd(cq_ref[...], ckv_ref[...], kr_ref[...], gqa_ref[...], gkva_ref[...],
                                          wqb_ref[...], wkvb_ref[...], gqn_ref[...], gkn_ref[...])
        kr = kr_ref[...]
        c, sa, sb = c_ref[...], sa_ref[...], sb_ref[...]
        gqn, gkn = gqn_ref[...], gkn_ref[...]
        kr_sq = jnp.sum(kr * kr, axis=-1, keepdims=True)
        for h in range(hh):
            qh = q0[:, QK_PAD * h:QK_PAD * (h + 1)]
            qn, _ = _rms_fwd(qh, gqn, QK_HEAD)
            q_ref[h, :, 0:128] = qn[:, 0:128].astype(bf16)
            q_ref[h, :, 128:256] = _rope(qn[:, 128:256], c, sa, sb).astype(bf16)
            kn_ = kv0[:, 256 * h:256 * h + 128]
            rk = lax.rsqrt((jnp.sum(kn_ * kn_, axis=-1, keepdims=True) + kr_sq) * (1.0 / QK_HEAD) + EPS)
            k_ref[h, :, 0:128] = (kn_ * rk * gkn[:, 0:128]).astype(bf16)
            k_ref[h, :, 128:256] = _rope(kr * rk * gkn[:, 128:256], c, sa, sb).astype(bf16)
            v_ref[h] = kv0[:, 256 * h + 128:256 * h + 256].astype(bf16)

    row128 = pl.BlockSpec((tm, 128), lambda i: (i, 0))
    return pl.pallas_call(
        body, name="fwd_mla_proj", grid=(s // tm,),
        in_specs=[pl.BlockSpec((tm, 256), lambda i: (i, Z_CQ // 256)), pl.BlockSpec((tm, 256), lambda i: (i, Z_CKV // 256)),
                  pl.BlockSpec((tm, 128), lambda i: (i, Z_KR // 128)), row128, row128, row128,
                  _const_spec((1, 256)), _const_spec((1, 256)), _const_spec((256, 1024)), _const_spec((256, 1024)),
                  _const_spec((1, 256)), _const_spec((1, 256))],
        out_specs=[pl.BlockSpec((hh, tm, QK_PAD), lambda i: (0, i, 0)), pl.BlockSpec((hh, tm, QK_PAD), lambda i: (0, i, 0)),
                   pl.BlockSpec((hh, tm, V_HEAD), lambda i: (0, i, 0))],
        out_shape=[jax.ShapeDtypeStruct((hh, s, QK_PAD), bf16), jax.ShapeDtypeStruct((hh, s, QK_PAD), bf16),
                   jax.ShapeDtypeStruct((hh, s, V_HEAD), bf16)],
        compiler_params=_cp(("parallel",), VMEM_LIMIT),
    )(z, z, z, cosb, sina, sinb, g_qa, g_kva, wqb, wkvb, g_qn, g_kn)


def _fwd_attn(q, k, v, tq):
    hh, s, _ = q.shape

    n_sub = max(1, tq // ATTN_SUB_ROWS)

    def body(q_ref, k_ref, v_ref, o_ref, o32_ref):
        for t in range(n_sub):
            rows = slice(t * (tq // n_sub), (t + 1) * (tq // n_sub))
            sc = _dot_nt(q_ref[rows, :], k_ref[...])
            p = jnp.exp2((sc - jnp.max(sc, axis=-1, keepdims=True)) * (ATTN_SCALE * LOG2_E))
            l = jnp.sum(p, axis=-1, keepdims=True)
            o = _dot(p.astype(bf16), v_ref[...]) * (1.0 / l)
            o_ref[rows, :] = o.astype(bf16)
            o32_ref[rows, :] = o

    out = pl.BlockSpec((tq, V_HEAD), lambda h, i: (i, h))
    return pl.pallas_call(
        body, name="fwd_attn", grid=(hh, s // tq),
        in_specs=[pl.BlockSpec((None, tq, QK_PAD), lambda h, i: (h, i, 0)),
                  pl.BlockSpec((None, s, QK_PAD), lambda h, i: (h, 0, 0)),
                  pl.BlockSpec((None, s, V_HEAD), lambda h, i: (h, 0, 0))],
        out_specs=[out, out],
        out_shape=[jax.ShapeDtypeStruct((s, hh * V_HEAD), bf16), jax.ShapeDtypeStruct((s, hh * V_HEAD), f32)],
        compiler_params=_cp(("parallel", "parallel"), VMEM_LIMIT),
    )(q, k, v)


def _split3(x):
    hi = x.astype(bf16)
    r1 = x - hi.astype(f32)
    mid = r1.astype(bf16)
    lo = (r1 - mid.astype(f32)).astype(bf16)
    return jnp.concatenate([hi, mid, lo], axis=-1)


def _tri_sum(tri, x):
    y = _dot(tri, _split3(x))
    return y[:, 0:128] + y[:, 128:256] + y[:, 256:384]


GLA_GROUP = 4
GLA_ROWS = GLA_GROUP * CHUNK
GLA_HEADS_PER_STEP = 2


def _gla_masks(rev):
    row = lax.broadcasted_iota(jnp.int32, (GLA_ROWS, GLA_ROWS), 0)
    col = lax.broadcasted_iota(jnp.int32, (GLA_ROWS, GLA_ROWS), 1)
    shift = CHUNK.bit_length() - 1
    same = (jnp.right_shift(row, shift) == jnp.right_shift(col, shift)).astype(f32)
    lower, upper = (row >= col).astype(f32) * same, (row <= col).astype(f32) * same
    keep, keep_t = (upper, lower) if rev else (lower, upper)
    chunk_of = jnp.right_shift(lax.broadcasted_iota(jnp.int32, (GLA_ROWS, 1), 0), shift)
    return keep, keep.astype(bf16), keep_t.astype(bf16), [(chunk_of == c).astype(f32) for c in range(GLA_GROUP)]


def _gla_gates(hq, hf, lower):
    sg = _sigmoid(hf)
    f = lower + (1.0 - lower) * sg
    return hq * _sigmoid(hq), 1.0 - f, jnp.log(f), f, sg


def _gla_last_mid(b, rev):
    b3 = b.reshape(GLA_GROUP, CHUNK, 128)
    last, mid = (0, CHUNK // 2) if rev else (CHUNK - 1, CHUNK // 2 - 1)
    return b3[:, last:last + 1, :], b3[:, mid:mid + 1, :]


def _gla_per_row(per_chunk):
    return jnp.broadcast_to(per_chunk, (GLA_GROUP, CHUNK, 128)).reshape(GLA_ROWS, 128)


def _gla_block_diag(x, row_masks):
    return jnp.concatenate([(x * m).astype(bf16) for m in row_masks], axis=-1)


def _gla_diag(y):
    return jnp.concatenate([y[CHUNK * c:CHUNK * (c + 1), 128 * c:128 * (c + 1)] for c in range(GLA_GROUP)], axis=0)


def _gla_rows(n, n_groups, rev):
    ne = n_groups - 1 - n if rev else n
    return pl.ds(pl.multiple_of(ne * GLA_ROWS, GLA_ROWS), GLA_ROWS), ne * GLA_GROUP


def _gla_scan_order(rev):
    return tuple(reversed(range(GLA_GROUP))) if rev else tuple(range(GLA_GROUP))


def _fwd_gla(z, lb4):
    s = z.shape[0]
    n_groups = s // GLA_ROWS
    assert n_groups % 2 == 0
    hp = GLA_HEADS_PER_STEP
    chains = [(hh, rev) for hh in range(hp) for rev in (False, True)]

    def body(hq_ref, hff_ref, hfb_ref, hi_ref, lb_ref, o_ref, st_ref):
        st_ref[...] = jnp.zeros_like(st_ref)
        masks = {rev: _gla_masks(rev) for rev in (False, True)}
        lowers = [_sigmoid(lb_ref[int(rev):int(rev) + 1, 128 * hh:128 * (hh + 1)]
                           - lb_ref[2 + int(rev):3 + int(rev), 128 * hh:128 * (hh + 1)]) for hh, rev in chains]

        def make_step(first):
            def step(n, carry):
                for ci, (hh, rev) in enumerate(chains):
                    cols = slice(128 * hh, 128 * (hh + 1))
                    rows, _ = _gla_rows(n, n_groups, rev)
                    maskf, tri, _, row_masks = masks[rev]
                    hf_ref = hfb_ref if rev else hff_ref
                    q, k, logf, _, _ = _gla_gates(hq_ref[rows, cols], hf_ref[rows, cols], lowers[ci])
                    vb = hi_ref[rows, cols].astype(bf16)
                    b = _tri_sum(tri, logf)
                    b_last3, b_mid3 = _gla_last_mid(b, rev)
                    b_last, b_mid = _gla_per_row(b_last3), _gla_per_row(b_mid3)
                    qi = (q * jnp.exp(b - b_mid)).astype(bf16)
                    ki = (k * jnp.exp(b_mid - b)).astype(bf16)
                    a = (_dot_nt(qi, ki) * maskf).astype(bf16)
                    kv = _dot_tn(vb, _gla_block_diag(k * jnp.exp(b_last - b), row_masks))
                    decay3 = jnp.exp(b_last3)
                    st = st_ref[ci]
                    before = [None] * GLA_GROUP
                    for c in _gla_scan_order(rev):
                        before[c] = st.astype(bf16)
                        st = st * decay3[c] + kv[:, 128 * c:128 * (c + 1)]
                    st_ref[ci] = st
                    inter = _dot_nt((q * jnp.exp(b)).astype(bf16), jnp.concatenate(before, axis=0))
                    o = _dot(a, vb) + _gla_diag(inter)
                    if first:
                        o_ref[rows, cols] = o
                    else:
                        o_ref[rows, cols] += o
                return carry
            return step

        lax.fori_loop(0, n_groups // 2, make_step(True), 0)
        lax.fori_loop(n_groups // 2, n_groups, make_step(False), 0)

    w = 128 * hp
    col = lambda base: pl.BlockSpec((s, w), lambda h: (0, base // w + h))
    return pl.pallas_call(
        body, name="fwd_gla", grid=(HG_HEADS // hp,),
        in_specs=[col(Z_HQ), col(Z_HFF), col(Z_HFB), col(Z_HI), pl.BlockSpec((4, w), lambda h: (0, h))],
        out_specs=pl.BlockSpec((s, w), lambda h: (0, h)),
        out_shape=jax.ShapeDtypeStruct((s, HG_HEADS * 128), f32),
        scratch_shapes=[pltpu.VMEM((len(chains), 128, 128), f32)],
        compiler_params=_cp(("parallel",), VMEM_LIMIT),
    )(z, z, z, z, lb4)


def _hg_out(o, hg, g_hgo):
    outs, ons, rs = [], [], []
    for h in range(HG_HEADS):
        oh = o[:, 128 * h:128 * (h + 1)]
        on, r = _rms_fwd(oh, g_hgo[:, 128 * h:128 * (h + 1)], 128)
        ons.append(on)
        rs.append(r)
    on = jnp.concatenate(ons, axis=-1)
    sg = _sigmoid(hg)
    return on * (hg * sg), on, rs, sg


def _fwd_mix(a, o, z, g_hgo, x, w_o, tm):
    s, d = x.shape

    def body(a_ref, o_ref, hg_ref, g_ref, x_ref, w_ref, x2_ref, cat_ref):
        r, _, _, _ = _hg_out(o_ref[...], hg_ref[...], g_ref[...])
        cat = jnp.concatenate([a_ref[...], r.astype(bf16)], axis=-1)
        cat_ref[...] = cat
        x2_ref[...] = x_ref[...] + _dot(cat, w_ref[...])

    row512 = pl.BlockSpec((tm, 512), lambda i: (i, 0))
    rowd = pl.BlockSpec((tm, d), lambda i: (i, 0))
    return pl.pallas_call(
        body, name="fwd_mix", grid=(s // tm,),
        in_specs=[row512, row512, pl.BlockSpec((tm, 512), lambda i: (i, Z_HG // 512)), _const_spec((1, 512)), rowd,
                  _const_spec((d, d))],
        out_specs=[rowd, rowd],
        out_shape=[jax.ShapeDtypeStruct((s, d), f32), jax.ShapeDtypeStruct((s, d), bf16)],
        compiler_params=_cp(("parallel",), VMEM_LIMIT),
    )(a, o, z, g_hgo, x, w_o)


def _fwd_ffn(x2, g_ffn, w_gate, w_up, w_down, tm):
    s, d = x2.shape

    def body(x_ref, g_ref, wg_ref, wu_ref, wd_ref, x3_ref, gp_ref, up_ref):
        x = x_ref[...]
        h, _ = _rms_fwd(x, g_ref[...], d)
        hb = h.astype(bf16)
        gp = _dot(hb, wg_ref[...])
        up = _dot(hb, wu_ref[...])
        gp_ref[...] = gp
        up_ref[...] = up
        act = (gp * _sigmoid(gp) * up).astype(bf16)
        x3_ref[...] = x + _dot(act, wd_ref[...])

    rowd = pl.BlockSpec((tm, d), lambda i: (i, 0))
    rowf = pl.BlockSpec((tm, D_FF), lambda i: (i, 0))
    return pl.pallas_call(
        body, name="fwd_ffn", grid=(s // tm,),
        in_specs=[rowd, _const_spec((1, d)), _const_spec((d, D_FF)), _const_spec((d, D_FF)), _const_spec((D_FF, d))],
        out_specs=[rowd, rowf, rowf],
        out_shape=[jax.ShapeDtypeStruct((s, d), f32), jax.ShapeDtypeStruct((s, D_FF), f32),
                   jax.ShapeDtypeStruct((s, D_FF), f32)],
        compiler_params=_cp(("parallel",), VMEM_LIMIT),
    )(x2, g_ffn, w_gate, w_up, w_down)


def _ple_loss_fwd_bwd(x3, g_ple, w_pg, p, w_pp, target, tm):
    s, d = x3.shape

    def body(x_ref, g_ref, wg_ref, p_ref, wp_ref, t_ref, dx_ref, h_ref, dpre_ref, dpp_ref, dg_ref, loss_ref):
        @pl.when(pl.program_id(0) == 0)
        def _():
            dg_ref[...] = jnp.zeros_like(dg_ref)
            loss_ref[...] = jnp.zeros_like(loss_ref)

        x = x_ref[...]
        g = g_ref[...]
        h, r = _rms_fwd(x, g, d)
        hb = h.astype(bf16)
        gate = _sigmoid(_dot(hb, wg_ref[...]))
        pp = _dot(p_ref[...].astype(bf16), wp_ref[...])
        e = x + gate * pp - t_ref[...]
        loss_ref[...] += 0.5 * jnp.sum(e * e) * (1.0 / d)
        dy = e * (1.0 / d)
        dpre = (dy * pp * gate * (1.0 - gate)).astype(bf16)
        dx, dgx = _rms_bwd(_dot_nt(dpre, wg_ref[...]), x, r, g, d)
        dx_ref[...] = dy + dx
        dg_ref[...] += jnp.sum(dgx, axis=0, keepdims=True)
        h_ref[...] = hb
        dpre_ref[...] = dpre
        dpp_ref[...] = (dy * gate).astype(bf16)

    rowd = pl.BlockSpec((tm, d), lambda i: (i, 0))
    return pl.pallas_call(
        body, name="ple_loss_fwd_bwd", grid=(s // tm,),
        in_specs=[rowd, _const_spec((1, d)), _const_spec((d, d)), pl.BlockSpec((tm, PLE_DIM), lambda i: (i, 0)),
                  _const_spec((PLE_DIM, d)), rowd],
        out_specs=[rowd, rowd, rowd, rowd, _acc_spec((1, d)), _acc_spec((8, 128))],
        out_shape=[jax.ShapeDtypeStruct((s, d), f32), jax.ShapeDtypeStruct((s, d), bf16), jax.ShapeDtypeStruct((s, d), bf16),
                   jax.ShapeDtypeStruct((s, d), bf16), jax.ShapeDtypeStruct((1, d), f32), jax.ShapeDtypeStruct((8, 128), f32)],
        compiler_params=_cp(("arbitrary",), VMEM_LIMIT),
    )(x3, g_ple, w_pg, p, w_pp, target)


def _bwd_ffn_hidden(d3, x2, gp, up, g_ffn, w_down, tm, tf):
    s, d = x2.shape

    def body(d3_ref, x_ref, gp_ref, up_ref, g_ref, wd_ref, h_ref, act_ref, dgp_ref, dup_ref, d3b_ref):
        @pl.when(pl.program_id(1) == 0)
        def _():
            h, _ = _rms_fwd(x_ref[...], g_ref[...], d)
            h_ref[...] = h.astype(bf16)
            d3b_ref[...] = d3_ref[...].astype(bf16)

        gp, up = gp_ref[...], up_ref[...]
        sg = _sigmoid(gp)
        silu = gp * sg
        act_ref[...] = (silu * up).astype(bf16)
        dact = _dot_nt(d3b_ref[...], wd_ref[...])
        dgp_ref[...] = (dact * up * (sg * (1.0 + gp * (1.0 - sg)))).astype(bf16)
        dup_ref[...] = (dact * silu).astype(bf16)

    rowd = pl.BlockSpec((tm, d), lambda i, f: (i, 0))
    rowf = pl.BlockSpec((tm, tf), lambda i, f: (i, f))
    return pl.pallas_call(
        body, name="bwd_ffn_hidden", grid=(s // tm, D_FF // tf),
        in_specs=[rowd, rowd, rowf, rowf, _const_spec((1, d)), pl.BlockSpec((tf, d), lambda i, f: (f, 0))],
        out_specs=[rowd, rowf, rowf, rowf],
        out_shape=[jax.ShapeDtypeStruct((s, d), bf16)] + [jax.ShapeDtypeStruct((s, D_FF), bf16)] * 3,
        scratch_shapes=[pltpu.VMEM((tm, d), bf16)],
        compiler_params=_cp(("parallel", "arbitrary"), VMEM_LIMIT),
    )(d3, x2, gp, up, g_ffn, w_down)


def _bwd_ffn_in(d3, x2, dgp, dup, g_ffn, w_gate, w_up, tm):
    s, d = x2.shape

    def body(d3_ref, x_ref, dgp_ref, dup_ref, g_ref, wg_ref, wu_ref, d2_ref, dg_ref):
        @pl.when(pl.program_id(0) == 0)
        def _():
            dg_ref[...] = jnp.zeros_like(dg_ref)

        x, g = x_ref[...], g_ref[...]
        dh = _dot_nt(dgp_ref[...], wg_ref[...]) + _dot_nt(dup_ref[...], wu_ref[...])
        r = lax.rsqrt(jnp.sum(x * x, axis=-1, keepdims=True) * (1.0 / d) + EPS)
        dx, dgx = _rms_bwd(dh, x, r, g, d)
        d2_ref[...] = d3_ref[...] + dx
        dg_ref[...] += jnp.sum(dgx, axis=0, keepdims=True)

    rowd = pl.BlockSpec((tm, d), lambda i: (i, 0))
    rowf = pl.BlockSpec((tm, D_FF), lambda i: (i, 0))
    return pl.pallas_call(
        body, name="bwd_ffn_in", grid=(s // tm,),
        in_specs=[rowd, rowd, rowf, rowf, _const_spec((1, d)), _const_spec((d, D_FF)), _const_spec((d, D_FF))],
        out_specs=[rowd, _acc_spec((1, d))],
        out_shape=[jax.ShapeDtypeStruct((s, d), f32), jax.ShapeDtypeStruct((1, d), f32)],
        compiler_params=_cp(("arbitrary",), VMEM_LIMIT),
    )(d3, x2, dgp, dup, g_ffn, w_gate, w_up)


def _bwd_mix(d2, w_o, o, z, g_hgo, tm):
    s, d = d2.shape

    def body(d2_ref, w_ref, o_ref, hg_ref, g_ref, da_ref, do_ref, dhg_ref, dg_ref):
        @pl.when(pl.program_id(0) == 0)
        def _():
            dg_ref[...] = jnp.zeros_like(dg_ref)

        dcat = _dot_nt(d2_ref[...].astype(bf16), w_ref[...])
        da_ref[...] = dcat[:, 0:512].astype(bf16)
        dr = dcat[:, 512:1024]
        o, hg, g = o_ref[...], hg_ref[...], g_ref[...]
        _, on, rs, sg = _hg_out(o, hg, g)
        dhg_ref[...] = (dr * on * (sg * (1.0 + hg * (1.0 - sg)))).astype(bf16)
        don = dr * (hg * sg)
        dgs = []
        for h in range(HG_HEADS):
            cols = slice(128 * h, 128 * (h + 1))
            dx, dgx = _rms_bwd(don[:, cols], o[:, cols], rs[h], g[:, cols], 128)
            do_ref[:, cols] = dx
            dgs.append(jnp.sum(dgx, axis=0, keepdims=True))
        dg_ref[...] += jnp.concatenate(dgs, axis=-1)

    row512 = pl.BlockSpec((tm, 512), lambda i: (i, 0))
    return pl.pallas_call(
        body, name="bwd_mix", grid=(s // tm,),
        in_specs=[pl.BlockSpec((tm, d), lambda i: (i, 0)), _const_spec((d, d)), row512,
                  pl.BlockSpec((tm, 512), lambda i: (i, Z_HG // 512)), _const_spec((1, 512))],
        out_specs=[row512, row512, row512, _acc_spec((1, 512))],
        out_shape=[jax.ShapeDtypeStruct((s, 512), bf16), jax.ShapeDtypeStruct((s, 512), f32), jax.ShapeDtypeStruct((s, 512), bf16),
                   jax.ShapeDtypeStruct((1, 512), f32)],
        compiler_params=_cp(("arbitrary",), VMEM_LIMIT),
    )(d2, w_o, o, z, g_hgo)


def _bwd_gla(z, lb4, do):
    s = z.shape[0]
    n_chunks = s // CHUNK
    n_groups = s // GLA_ROWS
    assert n_groups % 2 == 0

    def body(hq_ref, hff_ref, hfb_ref, hi_ref, lb_ref, do_ref, dhq_ref, dhff_ref, dhfb_ref, dhi_ref, dlb_ref,
             st_all, b_all, dst_ref, dq_acc, dv_acc, dlow_ref):
        dirs = (False, True)
        masks = [_gla_masks(rev) for rev in dirs]
        lowers = [_sigmoid(lb_ref[int(rev):int(rev) + 1, :] - lb_ref[2 + int(rev):3 + int(rev), :]) for rev in dirs]
        hf_refs, dhf_refs = (hff_ref, hfb_ref), (dhff_ref, dhfb_ref)

        def fwd_step(n, sts):
            new = []
            for d, rev in enumerate(dirs):
                rows, chunk0 = _gla_rows(n, n_groups, rev)
                _, k, logf, _, _ = _gla_gates(hq_ref[rows, :], hf_refs[d][rows, :], lowers[d])
                b = _tri_sum(masks[d][1], logf)
                b_last3, _ = _gla_last_mid(b, rev)
                b_all[d, rows, :] = b
                kv = _dot_tn(hi_ref[rows, :].astype(bf16), _gla_block_diag(k * jnp.exp(_gla_per_row(b_last3) - b), masks[d][3]))
                decay3 = jnp.exp(b_last3)
                st = sts[d]
                for c in _gla_scan_order(rev):
                    st_all[d, chunk0 + c] = st
                    st = st * decay3[c] + kv[:, 128 * c:128 * (c + 1)]
                new.append(st)
            return tuple(new)

        zero = jnp.zeros((128, 128), f32)
        lax.fori_loop(0, n_groups, fwd_step, (zero, zero))

        dst_ref[...] = jnp.zeros_like(dst_ref)
        dlow_ref[...] = jnp.zeros_like(dlow_ref)

        def make_bwd_step(first):
            def bwd_step(j, carry):
                n = n_groups - 1 - j
                for d, rev in enumerate(dirs):
                    maskf, _, tri_t, row_masks = masks[d]
                    lower = lowers[d]
                    rows, chunk0 = _gla_rows(n, n_groups, rev)
                    hq, hf = hq_ref[rows, :], hf_refs[d][rows, :]
                    q, k, _, f, sg = _gla_gates(hq, hf, lower)
                    v = hi_ref[rows, :]
                    dout = do_ref[rows, :]
                    b = b_all[d, rows, :]
                    b_last3, b_mid3 = _gla_last_mid(b, rev)
                    b_last, b_mid = _gla_per_row(b_last3), _gla_per_row(b_mid3)
                    e1, e2, e3, e4 = jnp.exp(b - b_mid), jnp.exp(b_mid - b), jnp.exp(b_last - b), jnp.exp(b)
                    decay3 = jnp.exp(b_last3)
                    qi, ki, kt, qt = q * e1, k * e2, k * e3, q * e4
                    qib, kib, ktb = qi.astype(bf16), ki.astype(bf16), kt.astype(bf16)
                    vb, dob = v.astype(bf16), dout.astype(bf16)
                    a = (_dot_nt(qib, kib) * maskf).astype(bf16)
                    da = (_dot_nt(dob, vb) * maskf).astype(bf16)
                    dqi = _dot(da, kib)
                    dki = _dot_tn(da, qib)
                    into_state = _dot_tn(dob, _gla_block_diag(qt, row_masks))
                    dst = dst_ref[d]
                    sts, dsts, ddecay = [None] * GLA_GROUP, [None] * GLA_GROUP, [None] * GLA_GROUP
                    for c in reversed(_gla_scan_order(rev)):
                        sts[c] = st_all[d, chunk0 + c]
                        dsts[c] = dst.astype(bf16)
                        ddecay[c] = jnp.sum(dst * sts[c], axis=0, keepdims=True)[None]
                        dst = dst * decay3[c] + into_state[:, 128 * c:128 * (c + 1)]
                    dst_ref[d] = dst
                    dv = _dot_tn(a, dob) + _gla_diag(_dot_nt(ktb, jnp.concatenate(dsts, axis=0)))
                    dqt = _gla_diag(_dot(dob, jnp.concatenate([x.astype(bf16) for x in sts], axis=-1)))
                    dkt = _gla_diag(_dot(vb, jnp.concatenate(dsts, axis=-1)))
                    dq = dqi * e1 + dqt * e4
                    dk = dki * e2 + dkt * e3
                    db = dqi * qi - dki * ki + dqt * qt - dkt * kt
                    dlast3 = (jnp.sum((dkt * kt).reshape(GLA_GROUP, CHUNK, 128), axis=1, keepdims=True)
                              + jnp.concatenate(ddecay, axis=0) * decay3)
                    dlogf = _tri_sum(tri_t, db) + _gla_per_row(dlast3)
                    df = dlogf / f - dk
                    dhf_refs[d][rows, :] = (df * (1.0 - lower) * sg * (1.0 - sg)).astype(bf16)
                    dlow_ref[d:d + 1, :] += jnp.sum(df * (1.0 - sg), axis=0, keepdims=True)
                    sq = _sigmoid(hq)
                    dhq = dq * (sq * (1.0 + hq * (1.0 - sq)))
                    if first:
                        dq_acc[rows, :] = dhq
                        dv_acc[rows, :] = dv
                    else:
                        dhq_ref[rows, :] = (dq_acc[rows, :] + dhq).astype(bf16)
                        dhi_ref[rows, :] = (dv_acc[rows, :] + dv).astype(bf16)
                return carry
            return bwd_step

        lax.fori_loop(0, n_groups // 2, make_bwd_step(True), 0)
        lax.fori_loop(n_groups // 2, n_groups, make_bwd_step(False), 0)

        for d in range(2):
            dl = dlow_ref[d:d + 1, :] * lowers[d] * (1.0 - lowers[d])
            dlb_ref[d:d + 1, :] = dl
            dlb_ref[2 + d:3 + d, :] = -dl

    col = lambda base: pl.BlockSpec((s, 128), lambda h: (0, base // 128 + h))
    return pl.pallas_call(
        body, name="bwd_gla", grid=(HG_HEADS,),
        in_specs=[col(Z_HQ), col(Z_HFF), col(Z_HFB), col(Z_HI), pl.BlockSpec((4, 128), lambda h: (0, h)), col(0)],
        out_specs=[col(0), col(0), col(0), col(0), pl.BlockSpec((4, 128), lambda h: (0, h))],
        out_shape=[jax.ShapeDtypeStruct((s, 512), bf16)] * 4 + [jax.ShapeDtypeStruct((4, 512), f32)],
        scratch_shapes=[pltpu.VMEM((2, n_chunks, 128, 128), f32), pltpu.VMEM((2, s, 128), f32), pltpu.VMEM((2, 128, 128), f32),
                        pltpu.VMEM((s, 128), f32), pltpu.VMEM((s, 128), f32), pltpu.VMEM((2, 128), f32)],
        compiler_params=_cp(("parallel",), VMEM_LIMIT),
    )(z, z, z, z, lb4, do)


def _bwd_attn(q, k, v, da, a32, tq):
    hh, s, _ = q.shape

    n_sub = max(1, tq // ATTN_SUB_ROWS)

    def body(q_ref, k_ref, v_ref, do_ref, o_ref, dq_ref, dk_ref, dv_ref, w_all, ds_all):
        @pl.when(pl.program_id(1) == 0)
        def _():
            dk_ref[...] = jnp.zeros_like(dk_ref)
            dv_ref[...] = jnp.zeros_like(dv_ref)

        kb, vb = k_ref[...], v_ref[...]
        for t in range(n_sub):
            rows = slice(t * (tq // n_sub), (t + 1) * (tq // n_sub))
            sc = _dot_nt(q_ref[rows, :], kb)
            p = jnp.exp2((sc - jnp.max(sc, axis=-1, keepdims=True)) * (ATTN_SCALE * LOG2_E))
            w = (p * (1.0 / jnp.sum(p, axis=-1, keepdims=True))).astype(bf16)
            dob = do_ref[rows, :]
            delta = jnp.sum(dob.astype(f32) * o_ref[rows, :], axis=-1, keepdims=True)
            ds = w * (_dot_nt(dob, vb) - delta).astype(bf16)
            dq_ref[rows, :] = _dot(ds, kb) * ATTN_SCALE
            w_all[rows, :] = w
            ds_all[rows, :] = ds
        dk_ref[...] += _dot_tn(ds_all[...], q_ref[...])
        dv_ref[...] += _dot_tn(w_all[...], do_ref[...])

        @pl.when(pl.program_id(1) == s // tq - 1)
        def _():
            dk_ref[...] = dk_ref[...] * ATTN_SCALE

    return pl.pallas_call(
        body, name="bwd_attn", grid=(hh, s // tq),
        in_specs=[pl.BlockSpec((None, tq, QK_PAD), lambda h, i: (h, i, 0)),
                  pl.BlockSpec((None, s, QK_PAD), lambda h, i: (h, 0, 0)),
                  pl.BlockSpec((None, s, V_HEAD), lambda h, i: (h, 0, 0)),
                  pl.BlockSpec((tq, V_HEAD), lambda h, i: (i, h)), pl.BlockSpec((tq, V_HEAD), lambda h, i: (i, h))],
        out_specs=[pl.BlockSpec((None, tq, QK_PAD), lambda h, i: (h, i, 0)),
                   pl.BlockSpec((None, s, QK_PAD), lambda h, i: (h, 0, 0)),
                   pl.BlockSpec((None, s, V_HEAD), lambda h, i: (h, 0, 0))],
        out_shape=[jax.ShapeDtypeStruct((hh, s, QK_PAD), f32), jax.ShapeDtypeStruct((hh, s, QK_PAD), f32),
                   jax.ShapeDtypeStruct((hh, s, V_HEAD), f32)],
        scratch_shapes=[pltpu.VMEM((tq, s), bf16), pltpu.VMEM((tq, s), bf16)],
        compiler_params=_cp(("parallel", "arbitrary"), VMEM_LIMIT),
    )(q, k, v, da, a32)


def _bwd_mla_proj(z, dq, dk, dv, cosb, sina, sinb, g_qa, g_kva, wqb, wkvb, g_qn, g_kn, tm):
    s = z.shape[0]
    hh = MLA_HEADS

    def body(cq_ref, ckv_ref, kr_ref, dq_ref, dk_ref, dv_ref, c_ref, sa_ref, sb_ref, gqa_ref, gkva_ref, wqb_ref, wkvb_ref,
             gqn_ref, gkn_ref, dz_ref, cqn_ref, ckvn_ref, dq0_ref, dkv0_ref, dgqa_ref, dgkva_ref, dgqn_ref, dgkn_ref):
        @pl.when(pl.program_id(0) == 0)
        def _():
            for r in (dgqa_ref, dgkva_ref, dgqn_ref, dgkn_ref):
                r[...] = jnp.zeros_like(r)

        cq, ckv, kr = cq_ref[...], ckv_ref[...], kr_ref[...]
        gqa, gkva, gqn, gkn = gqa_ref[...], gkva_ref[...], gqn_ref[...], gkn_ref[...]
        cqn_b, rq, ckvn_b, rkv, q0, kv0 = _mla_qk_fwd(cq, ckv, kr, gqa, gkva, wqb_ref[...], wkvb_ref[...], gqn, gkn)
        cqn_ref[...] = cqn_b
        ckvn_ref[...] = ckvn_b
        c, sa, sb = c_ref[...], -sa_ref[...], -sb_ref[...]
        kr_sq = jnp.sum(kr * kr, axis=-1, keepdims=True)
        dkr = jnp.zeros_like(kr)
        dgqn = jnp.zeros((1, QK_PAD), f32)
        dgkn = jnp.zeros((1, QK_PAD), f32)
        for h in range(hh):
            qh = q0[:, QK_PAD * h:QK_PAD * (h + 1)]
            rh = lax.rsqrt(jnp.sum(qh * qh, axis=-1, keepdims=True) * (1.0 / QK_HEAD) + EPS)
            dqh = dq_ref[h]
            dqn = jnp.concatenate([dqh[:, 0:128], _rope(dqh[:, 128:256], c, sa, sb)], axis=-1)
            dq0h, dgx = _rms_bwd(dqn, qh, rh, gqn, QK_HEAD)
            dq0_ref[:, QK_PAD * h:QK_PAD * (h + 1)] = dq0h.astype(bf16)
            dgqn = dgqn + jnp.sum(dgx, axis=0, keepdims=True)

            kn_ = kv0[:, 256 * h:256 * h + 128]
            k0 = jnp.concatenate([kn_, kr], axis=-1)
            rk = lax.rsqrt((jnp.sum(kn_ * kn_, axis=-1, keepdims=True) + kr_sq) * (1.0 / QK_HEAD) + EPS)
            dkh = dk_ref[h]
            dkn = jnp.concatenate([dkh[:, 0:128], _rope(dkh[:, 128:256], c, sa, sb)], axis=-1)
            dk0, dgx = _rms_bwd(dkn, k0, rk, gkn, QK_HEAD)
            dgkn = dgkn + jnp.sum(dgx, axis=0, keepdims=True)
            dkv0_ref[:, 256 * h:256 * h + 128] = dk0[:, 0:128].astype(bf16)
            dkv0_ref[:, 256 * h + 128:256 * h + 256] = dv_ref[h].astype(bf16)
            dkr = dkr + dk0[:, 128:256]
        dgqn_ref[...] += dgqn
        dgkn_ref[...] += dgkn
        dcq, dgx = _rms_bwd(_dot_nt(dq0_ref[...], wqb_ref[...]), cq, rq, gqa, Q_LORA)
        dgqa_ref[...] += jnp.sum(dgx, axis=0, keepdims=True)
        dckv, dgx = _rms_bwd(_dot_nt(dkv0_ref[...], wkvb_ref[...]), ckv, rkv, gkva, KV_LORA)
        dgkva_ref[...] += jnp.sum(dgx, axis=0, keepdims=True)
        dz_ref[:, 0:256] = dcq.astype(bf16)
        dz_ref[:, 256:512] = dckv.astype(bf16)
        dz_ref[:, 512:640] = dkr.astype(bf16)

    row128 = pl.BlockSpec((tm, 128), lambda i: (i, 0))
    row256 = pl.BlockSpec((tm, 256), lambda i: (i, 0))
    row1024 = pl.BlockSpec((tm, 1024), lambda i: (i, 0))
    hd = lambda w: pl.BlockSpec((hh, tm, w), lambda i: (0, i, 0))
    return pl.pallas_call(
        body, name="bwd_mla_proj", grid=(s // tm,),
        in_specs=[pl.BlockSpec((tm, 256), lambda i: (i, Z_CQ // 256)), pl.BlockSpec((tm, 256), lambda i: (i, Z_CKV // 256)),
                  pl.BlockSpec((tm, 128), lambda i: (i, Z_KR // 128)), hd(QK_PAD), hd(QK_PAD), hd(V_HEAD),
                  row128, row128, row128,
                  _const_spec((1, 256)), _const_spec((1, 256)), _const_spec((256, 1024)), _const_spec((256, 1024)),
                  _const_spec((1, 256)), _const_spec((1, 256))],
        out_specs=[pl.BlockSpec((tm, 640), lambda i: (i, 0)), row256, row256, row1024, row1024,
                   _acc_spec((1, 256)), _acc_spec((1, 256)), _acc_spec((1, 256)), _acc_spec((1, 256))],
        out_shape=[jax.ShapeDtypeStruct((s, 640), bf16), jax.ShapeDtypeStruct((s, 256), bf16), jax.ShapeDtypeStruct((s, 256), bf16),
                   jax.ShapeDtypeStruct((s, 1024), bf16), jax.ShapeDtypeStruct((s, 1024), bf16)]
        + [jax.ShapeDtypeStruct((1, 256), f32)] * 4,
        compiler_params=_cp(("arbitrary",), VMEM_LIMIT),
    )(z, z, z, dq, dk, dv, cosb, sina, sinb, g_qa, g_kva, wqb, wkvb, g_qn, g_kn)


def _bwd_in(segments, wz, x, g_mix, d2, tm):
    s, d = x.shape
    n_seg = len(segments)

    def body(*refs):
        dz_refs, w_refs = refs[:n_seg], refs[n_seg:2 * n_seg]
        x_ref, g_ref, d2_ref, gx_ref, dg_ref = refs[2 * n_seg:]

        @pl.when(pl.program_id(0) == 0)
        def _():
            dg_ref[...] = jnp.zeros_like(dg_ref)

        dh = _dot_nt(dz_refs[0][...], w_refs[0][...])
        for a_ref, w_ref in zip(dz_refs[1:], w_refs[1:]):
            dh = dh + _dot_nt(a_ref[...], w_ref[...])
        x, g = x_ref[...], g_ref[...]
        r = lax.rsqrt(jnp.sum(x * x, axis=-1, keepdims=True) * (1.0 / d) + EPS)
        dx, dgx = _rms_bwd(dh, x, r, g, d)
        gx_ref[...] = d2_ref[...] + dx
        dg_ref[...] += jnp.sum(dgx, axis=0, keepdims=True)

    rowd = pl.BlockSpec((tm, d), lambda i: (i, 0))
    dz_specs = [pl.BlockSpec((tm, w), functools.partial(lambda i, j: (i, j), j=ja)) for _, w, ja, _ in segments]
    w_specs = [pl.BlockSpec((d, w), functools.partial(lambda i, j: (0, j), j=jw), pipeline_mode=pl.Buffered(1))
               for _, w, _, jw in segments]
    return pl.pallas_call(
        body, name="bwd_in", grid=(s // tm,),
        in_specs=dz_specs + w_specs + [rowd, _const_spec((1, d)), rowd],
        out_specs=[rowd, _acc_spec((1, d))],
        out_shape=[jax.ShapeDtypeStruct((s, d), f32), jax.ShapeDtypeStruct((1, d), f32)],
        compiler_params=_cp(("arbitrary",), VMEM_LIMIT),
    )(*[a for a, _, _, _ in segments], *([wz] * n_seg), x, g_mix, d2)


def _pick_tile(n, cap):
    best = None
    for t in range(LANES, cap + 1, LANES):
        if n % t == 0:
            best = t
    return best if best is not None else n


def _mm_tn_many(a, bs, name, tm):
    kk, m = a.shape
    n_b = len(bs)
    tk = min(512, kk)
    n_k = kk // tk

    def body(a_ref, *refs):
        b_refs, o_refs, acc_refs = refs[:n_b], refs[n_b:2 * n_b], refs[2 * n_b:]

        @pl.when(pl.program_id(1) == 0)
        def _():
            for acc in acc_refs:
                acc[...] = jnp.zeros_like(acc)
        a_blk = a_ref[...].astype(bf16)
        for b_ref, acc in zip(b_refs, acc_refs):
            acc[...] += _dot_tn(a_blk, b_ref[...].astype(bf16))

        @pl.when(pl.program_id(1) == n_k - 1)
        def _():
            for o_ref, acc in zip(o_refs, acc_refs):
                o_ref[...] = acc[...].astype(bf16)

    return pl.pallas_call(
        body, name=name, grid=(m // tm, n_k),
        in_specs=[pl.BlockSpec((tk, tm), lambda i, k: (k, i))] + [pl.BlockSpec((tk, b.shape[1]), lambda i, k: (k, 0)) for b in bs],
        out_specs=[pl.BlockSpec((tm, b.shape[1]), lambda i, k: (i, 0)) for b in bs],
        out_shape=[jax.ShapeDtypeStruct((m, b.shape[1]), bf16) for b in bs],
        scratch_shapes=[pltpu.VMEM((tm, b.shape[1]), f32) for b in bs],
        compiler_params=_cp(("parallel", "arbitrary"), VMEM_LIMIT),
    )(a, *bs)


def _mm_tn(a, b, name):
    kk, m = a.shape
    _, n = b.shape
    tm = _pick_tile(m, 1408)
    tn = _pick_tile(n, 1408)
    tk = min(512, kk)

    n_k = kk // tk

    def body(a_ref, b_ref, o_ref, acc_ref):
        @pl.when(pl.program_id(2) == 0)
        def _():
            acc_ref[...] = jnp.zeros_like(acc_ref)
        acc_ref[...] += _dot_tn(a_ref[...].astype(bf16), b_ref[...].astype(bf16))

        @pl.when(pl.program_id(2) == n_k - 1)
        def _():
            o_ref[...] = acc_ref[...].astype(bf16)

    return pl.pallas_call(
        body, name=name, grid=(m // tm, n // tn, n_k),
        in_specs=[pl.BlockSpec((tk, tm), lambda i, j, k: (k, i)), pl.BlockSpec((tk, tn), lambda i, j, k: (k, j))],
        out_specs=pl.BlockSpec((tm, tn), lambda i, j, k: (i, j)),
        out_shape=jax.ShapeDtypeStruct((m, n), bf16),
        scratch_shapes=[pltpu.VMEM((tm, tn), f32)],
        compiler_params=_cp(("parallel", "parallel", "arbitrary"), VMEM_LIMIT),
    )(a, b)


def _rope_tables(positions):
    inv_freq = ROPE_THETA ** (-jnp.arange(0, QK_ROPE, 2, dtype=f32) / QK_ROPE)
    ang = positions.astype(f32)[:, None] * inv_freq
    cos, sin = jnp.cos(ang), jnp.sin(ang)
    zero = jnp.zeros_like(cos)
    return (jnp.concatenate([cos, cos, zero, zero], axis=1), jnp.concatenate([zero, sin, zero, zero], axis=1),
            jnp.concatenate([-sin, zero, zero, zero], axis=1))


def _pad256(g):
    return jnp.pad(g.reshape(1, QK_HEAD), ((0, 0), (0, QK_PAD - QK_HEAD)))


RELAYOUT_BLOCKS = 8
FIRST = ("w_in", "w_qb", "w_kvb", "lb_param")
SECOND = ("w_o", "w_gate", "w_up", "w_down", "w_ple_gate", "w_ple_proj")
ROW_SHARDED = ("w_o", "w_down", "w_ple_gate")


def _col_moves(j):
    lo = BIG["w_in"][1] * j
    w_in = [(max(lo, a) - lo, min(lo + BIG["w_in"][1], b) - lo, d + max(lo, a) - a)
            for a, b, d in Z_SEGMENTS if max(lo, a) < min(lo + BIG["w_in"][1], b)]
    head, half = divmod(j, 2)
    whole = lambda n: [(0, BIG[n][1], BIG[n][1] * j)]
    return {"w_in": w_in, "w_gate": whole("w_gate"), "w_up": whole("w_up"),
            "w_qb": [(0, 96, QK_PAD * head + 96 * half)], "w_kvb": whole("w_kvb"), "w_ple_proj": whole("w_ple_proj"),
            "lb_param": whole("lb_param")}


def _kernel_width(name):
    return {"w_in": Z_W, "w_qb": MLA_HEADS * QK_PAD}.get(name, N_DEV * BIG[name][1])


def _relayout_specs(names, by_dev):
    specs = []
    for n in names:
        rows, cols = BIG[n]
        if n == "lb_param":
            specs.append(_acc_spec((N_DEV, rows, cols) if by_dev else (rows, _kernel_width(n))))
        elif by_dev:
            specs.append(pl.BlockSpec((N_DEV, rows // RELAYOUT_BLOCKS, cols), lambda i: (0, i, 0)))
        else:
            specs.append(pl.BlockSpec((rows // RELAYOUT_BLOCKS, _kernel_width(n)), lambda i: (i, 0)))
    return specs


def _weights_in(gathered, names, name):
    n = len(names)

    def body(*refs):
        ins, outs = dict(zip(names, refs[:n])), dict(zip(names, refs[n:]))
        if "w_in" in outs:
            outs["w_in"][:, Z_KR + QK_ROPE:Z_W] = jnp.zeros((outs["w_in"].shape[0], Z_W - Z_KR - QK_ROPE), bf16)
        if "w_qb" in outs:
            for h in range(MLA_HEADS):
                outs["w_qb"][:, QK_PAD * h + QK_HEAD:QK_PAD * (h + 1)] = jnp.zeros((outs["w_qb"].shape[0], QK_PAD - QK_HEAD), bf16)
        for j in range(N_DEV):
            for wn, moves in _col_moves(j).items():
                if wn in outs:
                    for s0, s1, d0 in moves:
                        outs[wn][:, d0:d0 + s1 - s0] = ins[wn][j, :, s0:s1]

    outs = pl.pallas_call(
        body, name=name, grid=(RELAYOUT_BLOCKS,), in_specs=_relayout_specs(names, True), out_specs=_relayout_specs(names, False),
        out_shape=[jax.ShapeDtypeStruct((BIG[wn][0], _kernel_width(wn)), gathered[wn].dtype) for wn in names],
        compiler_params=_cp(("arbitrary",), VMEM_LIMIT),
    )(*[gathered[wn] for wn in names])
    return dict(zip(names, outs))


def _grads_out(sources, names, name):
    pieces = [(wn, start, arr) for wn in names for start, arr in sources[wn]]
    n_in = len(pieces)

    def body(*refs):
        outs = dict(zip(names, refs[n_in:]))

        def cols(wn, c0, c1):
            for (pn, start, arr), ref in zip(pieces, refs[:n_in]):
                if pn == wn and start <= c0 and c1 <= start + arr.shape[1]:
                    return ref[:, c0 - start:c1 - start]

        for j in range(N_DEV):
            for wn, moves in _col_moves(j).items():
                if wn in outs:
                    for s0, s1, d0 in moves:
                        outs[wn][j, :, s0:s1] = cols(wn, d0, d0 + s1 - s0).astype(bf16)

    in_specs = [_acc_spec(arr.shape) if wn == "lb_param" else pl.BlockSpec((arr.shape[0] // RELAYOUT_BLOCKS, arr.shape[1]), lambda i: (i, 0))
                for wn, _, arr in pieces]
    outs = pl.pallas_call(
        body, name=name, grid=(RELAYOUT_BLOCKS,), in_specs=in_specs, out_specs=_relayout_specs(names, True),
        out_shape=[jax.ShapeDtypeStruct((N_DEV, *BIG[wn]), bf16) for wn in names],
        compiler_params=_cp(("arbitrary",), VMEM_LIMIT),
    )(*[arr for _, _, arr in pieces])
    return dict(zip(names, outs))


def kernel(x, p, positions, g_mix, w_in, g_qa, g_kva, w_qb, w_kvb, g_qn, g_kn, lb_param, g_hgo, w_o, g_ffn, w_gate, w_up, w_down, g_ple, w_ple_gate, w_ple_proj, loss_target, m_g_mix, m_w_in, m_g_qa, m_g_kva, m_w_qb, m_w_kvb, m_g_qn, m_g_kn, m_lb_param, m_g_hgo, m_w_o, m_g_ffn, m_w_gate, m_w_up, m_w_down, m_g_ple, m_w_ple_gate, m_w_ple_proj, v_g_mix, v_w_in, v_g_qa, v_g_kva, v_w_qb, v_w_kvb, v_g_qn, v_g_kn, v_lb_param, v_g_hgo, v_w_o, v_g_ffn, v_w_gate, v_w_up, v_w_down, v_g_ple, v_w_ple_gate, v_w_ple_proj):
    w_all = dict(g_mix=g_mix, g_qa=g_qa, g_kva=g_kva, g_qn=g_qn, g_kn=g_kn, g_hgo=g_hgo, g_ffn=g_ffn, g_ple=g_ple,
                 w_in=w_in, w_qb=w_qb, w_kvb=w_kvb, w_o=w_o, w_gate=w_gate, w_up=w_up, w_down=w_down,
                 w_ple_gate=w_ple_gate, w_ple_proj=w_ple_proj, lb_param=lb_param)
    m_all = dict(g_mix=m_g_mix, g_qa=m_g_qa, g_kva=m_g_kva, g_qn=m_g_qn, g_kn=m_g_kn, g_hgo=m_g_hgo, g_ffn=m_g_ffn,
                 g_ple=m_g_ple, w_in=m_w_in, w_qb=m_w_qb, w_kvb=m_w_kvb, w_o=m_w_o, w_gate=m_w_gate, w_up=m_w_up,
                 w_down=m_w_down, w_ple_gate=m_w_ple_gate, w_ple_proj=m_w_ple_proj, lb_param=m_lb_param)
    v_all = dict(g_mix=v_g_mix, g_qa=v_g_qa, g_kva=v_g_kva, g_qn=v_g_qn, g_kn=v_g_kn, g_hgo=v_g_hgo, g_ffn=v_g_ffn,
                 g_ple=v_g_ple, w_in=v_w_in, w_qb=v_w_qb, w_kvb=v_w_kvb, w_o=v_w_o, w_gate=v_w_gate, w_up=v_w_up,
                 w_down=v_w_down, w_ple_gate=v_w_ple_gate, w_ple_proj=v_w_ple_proj, lb_param=v_lb_param)
    me_idx = jnp.stack([_me()]).astype(jnp.int32)
    x, p, positions, target = x[0], p[0, 0], positions[0], loss_target[0]
    s = x.shape[0]
    tm, tm_ffn, tq_f, tq_b = min(512, s), min(1024, s), min(2048, s), min(1024, s)
    g_mix, g_qa, g_kva, g_qn, g_kn, g_hgo, g_ffn, g_ple = (w_all[n].reshape(1, -1) for n in SMALL)
    g_qn_p, g_kn_p = _pad256(g_qn), _pad256(g_kn)
    cosb, sina, sinb = _rope_tables(positions)
    shard = lambda n: w_all[n].reshape(BIG[n])

    first = _all_gather([shard(n) for n in FIRST], [f32 if n == "lb_param" else bf16 for n in FIRST], "ag_first")
    lands = _cast_to_slot([shard(n) for n in SECOND], me_idx, first[0])
    ag2, token = _exchange_start([], lands, "ag_second_start")
    wk = _weights_in(dict(zip(FIRST, first)), FIRST, "weights_in_first")
    wz, wqb, wkvb, lb4 = (wk[n] for n in FIRST)

    h1, z = _fwd_in(x, g_mix, wz, tm)
    q, k, v = _fwd_mla_proj(z, cosb + token[0, 0], sina, sinb, g_qa, g_kva, wqb, wkvb, g_qn_p, g_kn_p, tm)
    a, a32 = _fwd_attn(q, k, v, tq_f)
    o = _fwd_gla(z, lb4)

    second = dict(zip(SECOND, _exchange_wait(ag2, [a, o], "ag_second_wait")[1]))
    wk = _weights_in(second, ("w_gate", "w_up", "w_ple_proj"), "weights_in_second")
    w_gate, w_up, w_pp = wk["w_gate"], wk["w_up"], wk["w_ple_proj"]
    w_o, w_down, w_pg = (second[n].reshape(N_DEV * BIG[n][0], BIG[n][1]) for n in ROW_SHARDED)

    x2, cat = _fwd_mix(a, o, z, g_hgo, x, w_o, tm)
    x3, gp, up = _fwd_ffn(x2, g_ffn, w_gate, w_up, w_down, min(256, s))
    d3, h3, dpre, dpp, dg_ple, loss_tile = _ple_loss_fwd_bwd(x3, g_ple, w_pg, p, w_pp, target, tm)
    h2, act, dgp, dup = _bwd_ffn_hidden(d3, x2, gp, up, g_ffn, w_down, tm_ffn, 256)
    d2, dg_ffn = _bwd_ffn_in(d3, x2, dgp, dup, g_ffn, w_gate, w_up, tm)

    gw_gate, gw_up = _mm_tn_many(h2, [dgp, dup], "dw_gate_up", 512)
    blocks = _grads_out({"w_gate": [(0, gw_gate)], "w_up": [(0, gw_up)], "w_ple_proj": [(0, _mm_tn(p, dpp, "dw_ple_proj"))]},
                        ("w_gate", "w_up", "w_ple_proj"), "grads_out_second")
    row_grads = {"w_o": _mm_tn(cat, d2, "dw_o"), "w_down": _mm_tn(act, d3, "dw_down"), "w_ple_gate": _mm_tn(h3, dpre, "dw_ple_gate")}
    blocks.update({n: g.reshape(N_DEV, *BIG[n]) for n, g in row_grads.items()})
    empty = lambda names: [lax.empty((N_PEERS, *BIG[n]), bf16) for n in names]
    rs2, token = _exchange_start([blocks[n] for n in SECOND], empty(SECOND), "rs_second_start")

    da, do, dz_hg, dg_hgo = _bwd_mix(d2, w_o, o, z, g_hgo + token[0, 0], tm)
    dz_hq, dz_hff, dz_hfb, dz_hi, dlb4 = _bwd_gla(z, lb4, do)
    dq, dk, dv = _bwd_attn(q, k, v, da, a32, tq_b)
    dz_mla, cqn, ckvn, dq0, dkv0, dg_qa, dg_kva, dg_qn, dg_kn = _bwd_mla_proj(
        z, dq, dk, dv, cosb, sina, sinb, g_qa, g_kva, wqb, wkvb, g_qn_p, g_kn_p, tm)

    gz = list(zip((Z_HQ, Z_HFF, Z_HFB, Z_HI, Z_HG, Z_CQ),
                  _mm_tn_many(h1, [dz_hq, dz_hff, dz_hfb, dz_hi, dz_hg, dz_mla], "dw_in", 1024)))
    blocks1 = _grads_out({"w_in": gz, "w_qb": [(0, _mm_tn(cqn, dq0, "dw_qb"))], "w_kvb": [(0, _mm_tn(ckvn, dkv0, "dw_kvb"))],
                          "lb_param": [(0, dlb4)]}, FIRST, "grads_out_first")
    rs1, token = _exchange_start([blocks1[n] for n in FIRST], empty(FIRST), "rs_first_start")

    result = {}

    def adam(names, lands, src, n_blocks, after=()):
        outs = _adam_shards(me_idx, [src[n] for n in names], lands, [w_all[n] for n in names], [m_all[n] for n in names],
                            [v_all[n] for n in names], n_blocks, "adamw_" + names[0], after)
        result.update(zip(names, outs))
        return outs[0][0]

    blocks2, lands2 = (dict(zip(SECOND, arrs)) for arrs in _exchange_wait(rs2, [token], "rs_second_wait"))
    by8 = tuple(n for n in SECOND if n != "w_down")
    done = [adam(by8, [lands2[n] for n in by8], blocks2, 8), adam(("w_down",), [lands2["w_down"]], blocks2, 2)]

    segments = [(dz_hq, 512, 0, Z_HQ // 512), (dz_hff, 512, 0, Z_HFF // 512), (dz_hfb, 512, 0, Z_HFB // 512),
                (dz_hi, 512, 0, Z_HI // 512), (dz_hg, 512, 0, Z_HG // 512), (dz_mla, 640, 0, Z_CQ // 640)]
    grad_x, dg_mix = _bwd_in(segments, wz, x, g_mix + token[0, 0], d2, tm)
    dgains = (dg_mix, dg_qa, dg_kva, dg_qn, dg_kn, dg_hgo, dg_ffn, dg_ple)

    vec = jnp.concatenate(list(dgains) + [loss_tile[0:1]], axis=1)
    parts = _all_gather([vec], [f32], "ag_gains")[0]
    outs, loss_row = _adam_gains(parts, [w_all[n] for n in SMALL], [m_all[n] for n in SMALL], [v_all[n] for n in SMALL])
    result.update(zip(SMALL, outs))

    blocks1, lands1 = _exchange_wait(rs1, [grad_x, loss_row, *done], "rs_first_wait")
    adam(FIRST, lands1, dict(zip(FIRST, blocks1)), 8)

    order = ("g_mix", "w_in", "g_qa", "g_kva", "w_qb", "w_kvb", "g_qn", "g_kn", "lb_param", "g_hgo", "w_o", "g_ffn",
             "w_gate", "w_up", "w_down", "g_ple", "w_ple_gate", "w_ple_proj")
    return (loss_row[0, 0], grad_x[None], *[result[n][k] for k in range(4) for n in order])
```

```python
import functools
import math

import jax
import jax.numpy as jnp
from jax import lax
from jax.experimental import pallas as pl
from jax.experimental.pallas import tpu as pltpu

f32 = jnp.float32
bf16 = jnp.bfloat16

N_DEV = 8
D_MODEL = 1024
MLA_HEADS = 4
QK_NOPE = 128
QK_ROPE = 64
QK_HEAD = QK_NOPE + QK_ROPE
QK_PAD = 256
V_HEAD = 128
Q_LORA = 256
KV_LORA = 256
HG_HEADS = 4
HG_DK = 128
CHUNK = 64
D_FF = 2816
PLE_DIM = 256
ROPE_THETA = 10000.0
EPS = 1e-6
ATTN_SCALE = QK_HEAD ** -0.5
LOG2_E = math.log2(math.e)
ATTN_SUB_ROWS = 256
IN_SIZES = (256, 256, 64, 512, 512, 512, 512, 512)
D_IN = sum(IN_SIZES)
Z_HQ, Z_HFF, Z_HFB, Z_HI, Z_HG, Z_CQ, Z_CKV, Z_KR, Z_W = 0, 512, 1024, 1536, 2048, 2560, 2816, 3072, 3200

ADAM_LR, ADAM_B1, ADAM_B2, ADAM_EPS, ADAM_WD, ADAM_STEP = 0.001, 0.9, 0.999, 1e-08, 0.01, 10

LANES = 128
BIG = {"w_in": (1024, 392), "w_qb": (256, 96), "w_kvb": (256, 128), "w_o": (128, 1024), "w_gate": (1024, 352),
       "w_up": (1024, 352), "w_down": (352, 1024), "w_ple_gate": (128, 1024), "w_ple_proj": (256, 128),
       "lb_param": (4, 64)}
ROW_BLOCKS = {("w_in", "w_qb", "w_kvb", "w_o", "w_gate", "w_up", "w_ple_gate", "w_ple_proj"): 8, ("w_down", "lb_param"): 2}
SMALL = {"g_mix": (0, 1024), "g_qa": (1024, 256), "g_kva": (1280, 256), "g_qn": (1536, 192), "g_kn": (1792, 192),
         "g_hgo": (2048, 512), "g_ffn": (2560, 1024), "g_ple": (3584, 1024)}
LOSS_OFF = 4608
GAIN_VEC = LOSS_OFF + LANES
Z_SEGMENTS = ((0, 256, Z_CQ), (256, 512, Z_CKV), (512, 576, Z_KR), (576, 1088, Z_HQ), (1088, 1600, Z_HFF),
              (1600, 2112, Z_HFB), (2112, 2624, Z_HI), (2624, 3136, Z_HG))

VMEM_LIMIT = 56 * 1024 * 1024
MESH = pl.DeviceIdType.MESH


def _cp(sem=None, vmem=None):
    return pltpu.CompilerParams(dimension_semantics=sem, vmem_limit_bytes=vmem)


def _const_spec(shape):
    nd = len(shape)
    return pl.BlockSpec(shape, lambda *_: (0,) * nd, pipeline_mode=pl.Buffered(1))


def _acc_spec(shape):
    nd = len(shape)
    return pl.BlockSpec(shape, lambda *_: (0,) * nd)


def _sigmoid(x):
    return jax.nn.sigmoid(x)


def _dot(a, b):
    return jnp.dot(a, b, preferred_element_type=f32)


def _dot_nt(a, b):
    return lax.dot_general(a, b, (((1,), (1,)), ((), ())), preferred_element_type=f32)


def _dot_tn(a, b):
    return lax.dot_general(a, b, (((0,), (0,)), ((), ())), preferred_element_type=f32)


def _rms_fwd(x, g, width):
    r = lax.rsqrt(jnp.sum(x * x, axis=-1, keepdims=True) * (1.0 / width) + EPS)
    return x * r * g, r


def _rms_bwd(dy, x, r, g, width):
    u = dy * g
    dx = r * u - x * (r * r * r) * (jnp.sum(u * x, axis=-1, keepdims=True) * (1.0 / width))
    return dx, dy * x * r


def _rope(b, c, sa, sb):
    return b * c + pltpu.roll(b, 32, 1) * sa + pltpu.roll(b, 96, 1) * sb


def _all_gather(shards, dtypes, name):
    n = len(shards)

    def body(*refs):
        in_refs, out_refs, stage = refs[:n], refs[n:2 * n], refs[2 * n:3 * n]
        send_sems, recv_sems, local_sems = refs[3 * n:]
        for w in range(n):
            stage[w][...] = in_refs[w][...].astype(stage[w].dtype)
        x, y, c = lax.axis_index("x"), lax.axis_index("y"), lax.axis_index("c")
        me, sibling = (x, y, c), (x, y, 1 - c)
        chips = [(1 - x, y), (x, 1 - y), (1 - x, 1 - y)]

        def slot(w, px, py, pc):
            return out_refs[w].at[4 * px + 2 * py + pc]

        def copy(w, k, block, to, src=None):
            return pltpu.make_async_remote_copy(
                src_ref=slot(w, *block) if src is None else src, dst_ref=slot(w, *block),
                send_sem=send_sems.at[w, k], recv_sem=recv_sems.at[w, k], device_id=to, device_id_type=MESH)

        first = []
        for j, chip in enumerate(chips):
            first += [copy(w, 1 + j, me, (*chip, c), src=stage[w]) for w in range(n)]
        first += [copy(w, 0, me, sibling, src=stage[w]) for w in range(n)]
        mine = [pltpu.make_async_copy(stage[w], slot(w, *me), local_sems.at[w]) for w in range(n)]
        for cp in first + mine:
            cp.start()
        passed = []
        for j, chip in enumerate(chips):
            for w in range(n):
                copy(w, 1 + j, (*chip, c), me).wait_recv()
                passed.append(copy(w, 4 + j, (*chip, c), sibling))
                passed[-1].start()
        for w in range(n):
            copy(w, 0, sibling, me).wait_recv()
        for j, chip in enumerate(chips):
            for w in range(n):
                copy(w, 4 + j, (*chip, 1 - c), me).wait_recv()
        for cp in first + passed:
            cp.wait_send()
        for cp in mine:
            cp.wait()

    return pl.pallas_call(
        body, name=name,
        out_shape=[jax.ShapeDtypeStruct((N_DEV, *s.shape), dt) for s, dt in zip(shards, dtypes)],
        in_specs=[pl.BlockSpec(memory_space=pltpu.VMEM)] * n,
        out_specs=[pl.BlockSpec(memory_space=pl.ANY)] * n,
        scratch_shapes=[pltpu.VMEM(s.shape, dt) for s, dt in zip(shards, dtypes)]
        + [pltpu.SemaphoreType.DMA((n, 7)), pltpu.SemaphoreType.DMA((n, 7)), pltpu.SemaphoreType.DMA((n,))],
        compiler_params=_cp(None, VMEM_LIMIT),
    )(*shards)


N_PEERS = N_DEV - 1
HBM_SPEC = pl.BlockSpec(memory_space=pltpu.HBM)
SEM_SPEC = pl.BlockSpec(memory_space=pltpu.SEMAPHORE)
DATAFLOW = pltpu.SideEffectType.DATAFLOW_SIDE_EFFECTING


def _me():
    return 4 * lax.axis_index("x") + 2 * lax.axis_index("y") + lax.axis_index("c")


def _peer(k):
    x, y, c = lax.axis_index("x"), lax.axis_index("y"), lax.axis_index("c")
    px = 1 - x if k & 4 else x
    py = 1 - y if k & 2 else y
    pc = 1 - c if k & 1 else c
    return (px, py, pc), 4 * px + 2 * py + pc


def _exchange_copies(src_refs, land_refs, send_sems, recv_sems, gather):
    cps = []
    me = _me()
    for k in range(1, N_DEV):
        peer, peer_idx = _peer(k)
        for w, land in enumerate(land_refs):
            src = land.at[me] if gather else src_refs[w].at[peer_idx]
            dst = land.at[me] if gather else land.at[k - 1]
            cps.append(pltpu.make_async_remote_copy(
                src_ref=src, dst_ref=dst, send_sem=send_sems.at[N_PEERS * w + k - 1], recv_sem=recv_sems.at[N_PEERS * w + k - 1],
                device_id=peer, device_id_type=MESH))
    return cps


def _exchange_start(srcs, lands, name):
    n_src, n = len(srcs), len(lands)

    def body(*refs):
        src_refs, land_refs = refs[:n_src], refs[n_src:n_src + n]
        send_sems, recv_sems = refs[n_src + n], refs[n_src + n + 1]
        token = refs[-1]
        for cp in _exchange_copies(src_refs, land_refs, send_sems, recv_sems, gather=not n_src):
            cp.start()
        token[...] = jnp.zeros_like(token)

    arrays = [pltpu.with_memory_space_constraint(a, pltpu.HBM) for a in (*srcs, *lands)]
    outs = pl.pallas_call(
        body, name=name,
        out_shape=(pltpu.SemaphoreType.DMA((n * N_PEERS,)), pltpu.SemaphoreType.DMA((n * N_PEERS,)),
                   *[pltpu.HBM(a.shape, a.dtype) for a in arrays], jax.ShapeDtypeStruct((8, LANES), f32)),
        in_specs=[HBM_SPEC] * len(arrays),
        out_specs=(SEM_SPEC, SEM_SPEC, *[HBM_SPEC] * len(arrays), pl.BlockSpec(memory_space=pltpu.VMEM)),
        input_output_aliases={i: 2 + i for i in range(len(arrays))},
        compiler_params=pltpu.CompilerParams(has_side_effects=DATAFLOW),
    )(*arrays)
    return (outs[0], outs[1], outs[2:2 + n_src], outs[2 + n_src:2 + n_src + n]), outs[-1]


def _exchange_wait(state, after, name):
    send_sems, recv_sems, srcs, lands = state
    n_src, n = len(srcs), len(lands)

    def body(*refs):
        src_refs, land_refs = refs[:n_src], refs[n_src:n_src + n]
        send_ref, recv_ref = refs[n_src + n], refs[n_src + n + 1]
        for cp in _exchange_copies(src_refs, land_refs, send_ref, recv_ref, gather=not n_src):
            cp.wait_send()
            cp.wait_recv()

    arrays = (*srcs, *lands)
    outs = pl.pallas_call(
        body, name=name,
        out_shape=tuple(pltpu.HBM(a.shape, a.dtype) for a in arrays),
        in_specs=[HBM_SPEC] * len(arrays) + [SEM_SPEC, SEM_SPEC] + [pl.BlockSpec(memory_space=pl.ANY)] * len(after),
        out_specs=tuple([HBM_SPEC] * len(arrays)),
        input_output_aliases={i: i for i in range(len(arrays))},
        compiler_params=pltpu.CompilerParams(has_side_effects=DATAFLOW),
    )(*arrays, send_sems, recv_sems, *after)
    return outs[:n_src], outs[n_src:]


def _cast_to_slot(shards, me_idx, after):
    n = len(shards)

    def body(i_ref, *refs):
        for w in range(n):
            refs[n + 1 + w][...] = refs[w][...].astype(bf16)

    return pl.pallas_call(
        body, name="cast_to_slot",
        grid_spec=pltpu.PrefetchScalarGridSpec(
            num_scalar_prefetch=1, grid=(1,),
            in_specs=[pl.BlockSpec(s.shape, lambda i, m: (0, 0)) for s in shards] + [pl.BlockSpec(memory_space=pl.ANY)],
            out_specs=[pl.BlockSpec((None, *s.shape), lambda i, m: (m[0], 0, 0)) for s in shards]),
        out_shape=[jax.ShapeDtypeStruct((N_DEV, *s.shape), bf16) for s in shards],
        compiler_params=_cp(("arbitrary",), VMEM_LIMIT),
    )(me_idx, *shards, after)


def _row_block(rows, n_blocks):
    return (rows // n_blocks, True) if rows % (16 * n_blocks) == 0 else (rows, False)


def _adam_math(w, g, m, v):
    m = ADAM_B1 * m + (1.0 - ADAM_B1) * g
    v = ADAM_B2 * v + (1.0 - ADAM_B2) * (g * g)
    m_hat = m / (1.0 - ADAM_B1 ** ADAM_STEP)
    v_hat = v / (1.0 - ADAM_B2 ** ADAM_STEP)
    delta = -ADAM_LR * (m_hat / (jnp.sqrt(v_hat) + ADAM_EPS) + ADAM_WD * w)
    return delta, m, v


def _adam_shards(me_idx, blocks, lands, ws, ms, vs, n_blocks, name, after=()):
    n = len(blocks)

    def body(i_ref, *refs):
        ins, outs = refs[:5 * n], refs[5 * n + len(after):]
        for w in range(n):
            g_ref, b_ref, w_ref, m_ref, v_ref = (ins[t * n + w] for t in range(5))
            g = g_ref[...].astype(f32)
            for k in range(N_PEERS):
                g = g + b_ref[k].astype(f32)
            if len(w_ref.shape) == 2:
                pieces = [(slice(None), g)]
            else:
                pieces = [(a, g[2 * a:2 * a + 2]) for a in range(2)]
            for at, gp in pieces:
                vals = (gp,) + _adam_math(w_ref[at], gp, m_ref[at], v_ref[at])
                for t, val in enumerate(vals):
                    outs[4 * w + t][at] = val

    specs = [[] for _ in range(5)]
    out_specs, out_shape = [], []
    for g, wt in zip(blocks, ws):
        rows, cols = g.shape[1:]
        rb, cut = _row_block(rows, n_blocks)
        specs[0].append(pl.BlockSpec((None, rb, cols), functools.partial(lambda i, s, cut: (s[0], i if cut else 0, 0), cut=cut)))
        specs[1].append(pl.BlockSpec((N_PEERS, rb, cols), functools.partial(lambda i, s, cut: (0, i if cut else 0, 0), cut=cut)))
        if wt.shape[0] == 1:
            shard = pl.BlockSpec((None, rb, cols), functools.partial(lambda i, s, cut: (0, i if cut else 0, 0), cut=cut))
        else:
            shard = pl.BlockSpec(wt.shape, functools.partial(lambda i, s, nd: (0,) * nd, nd=wt.ndim))
        for t in (2, 3, 4):
            specs[t].append(shard)
        out_specs += [shard] * 4
        out_shape += [jax.ShapeDtypeStruct(wt.shape, f32)] * 4
    outs = pl.pallas_call(
        body, name=name,
        grid_spec=pltpu.PrefetchScalarGridSpec(
            num_scalar_prefetch=1, grid=(n_blocks,), in_specs=sum(specs, []) + [pl.BlockSpec(memory_space=pl.ANY)] * len(after),
            out_specs=out_specs),
        out_shape=out_shape,
        compiler_params=_cp(("arbitrary",), VMEM_LIMIT),
    )(me_idx, *blocks, *lands, *ws, *ms, *vs, *after)
    return [outs[4 * w:4 * w + 4] for w in range(n)]


def _adam_gains(parts, ws, ms, vs):
    n = len(ws)

    def body(p_ref, *refs):
        ins, outs = refs[:3 * n], refs[3 * n:]
        g_all = p_ref[0]
        for k in range(1, N_DEV):
            g_all = g_all + p_ref[k]
        for w, (off, lanes) in enumerate(SMALL.values()):
            w_ref, m_ref, v_ref = ins[w], ins[n + w], ins[2 * n + w]
            if len(w_ref.shape) == 2:
                pieces = [(slice(None), off, lanes)]
            else:
                pieces = [((slice(None), h), off + LANES * h, LANES) for h in range(w_ref.shape[1])]
            for at, o, ln in pieces:
                g = g_all[:, o:o + ln]
                vals = (g,) + _adam_math(w_ref[at], g, m_ref[at], v_ref[at])
                for t, val in enumerate(vals):
                    outs[4 * w + t][at] = val
        outs[4 * n][...] = g_all[:, LOSS_OFF:LOSS_OFF + LANES]

    out_shape = sum([[jax.ShapeDtypeStruct(w.shape, f32)] * 4 for w in ws], []) + [jax.ShapeDtypeStruct((1, LANES), f32)]
    outs = pl.pallas_call(body, name="adamw_gains", out_shape=out_shape)(parts, *ws, *ms, *vs)
    return [outs[4 * w:4 * w + 4] for w in range(n)], outs[4 * n]


def _fwd_in(x, g_mix, wz, tm):
    s, d = x.shape

    def body(x_ref, g_ref, w_ref, h_ref, z_ref):
        h, _ = _rms_fwd(x_ref[...], g_ref[...], d)
        hb = h.astype(bf16)
        h_ref[...] = hb
        z_ref[...] = _dot(hb, w_ref[...])

    return pl.pallas_call(
        body, name="fwd_in", grid=(s // tm,),
        in_specs=[pl.BlockSpec((tm, d), lambda i: (i, 0)), _const_spec((1, d)), _const_spec((d, Z_W))],
        out_specs=[pl.BlockSpec((tm, d), lambda i: (i, 0)), pl.BlockSpec((tm, Z_W), lambda i: (i, 0))],
        out_shape=[jax.ShapeDtypeStruct((s, d), bf16), jax.ShapeDtypeStruct((s, Z_W), f32)],
        compiler_params=_cp(("parallel",), VMEM_LIMIT),
    )(x, g_mix, wz)


def _mla_qk_fwd(cq, ckv, kr, g_qa, g_kva, wqb, wkvb, g_qn, g_kn):
    cqn, rq = _rms_fwd(cq, g_qa, Q_LORA)
    ckvn, rkv = _rms_fwd(ckv, g_kva, KV_LORA)
    cqn_b, ckvn_b = cqn.astype(bf16), ckvn.astype(bf16)
    q0 = _dot(cqn_b, wqb)
    kv0 = _dot(ckvn_b, wkvb)
    return cqn_b, rq, ckvn_b, rkv, q0, kv0


def _fwd_mla_proj(z, cosb, sina, sinb, g_qa, g_kva, wqb, wkvb, g_qn, g_kn, tm):
    s = z.shape[0]
    hh = MLA_HEADS

    def body(cq_ref, ckv_ref, kr_ref, c_ref, sa_ref, sb_ref, gqa_ref, gkva_ref, wqb_ref, wkvb_ref, gqn_ref, gkn_ref,
             q_ref, k_ref, v_ref):
        _, _, _, _, q0, kv0 = _mla_qk_fwd(cq_ref[...], ckv_ref[...], kr_ref[...], gqa_ref[...], gkva_ref[...],
                                          wqb_ref[...], wkvb_ref[...], gqn_ref[...], gkn_ref[...])
        kr = kr_ref[...]
        c, sa, sb = c_ref[...], sa_ref[...], sb_ref[...]
        gqn, gkn = gqn_ref[...], gkn_ref[...]
        kr_sq = jnp.sum(kr * kr, axis=-1, keepdims=True)
        for h in range(hh):
            qh = q0[:, QK_PAD * h:QK_PAD * (h + 1)]
            qn, _ = _rms_fwd(qh, gqn, QK_HEAD)
            q_ref[h, :, 0:128] = qn[:, 0:128].astype(bf16)
            q_ref[h, :, 128:256] = _rope(qn[:, 128:256], c, sa, sb).astype(bf16)
            kn_ = kv0[:, 256 * h:256 * h + 128]
            rk = lax.rsqrt((jnp.sum(kn_ * kn_, axis=-1, keepdims=True) + kr_sq) * (1.0 / QK_HEAD) + EPS)
            k_ref[h, :, 0:128] = (kn_ * rk * gkn[:, 0:128]).astype(bf16)
            k_ref[h, :, 128:256] = _rope(kr * rk * gkn[:, 128:256], c, sa, sb).astype(bf16)
            v_ref[h] = kv0[:, 256 * h + 128:256 * h + 256].astype(bf16)

    row128 = pl.BlockSpec((tm, 128), lambda i: (i, 0))
    return pl.pallas_call(
        body, name="fwd_mla_proj", grid=(s // tm,),
        in_specs=[pl.BlockSpec((tm, 256), lambda i: (i, Z_CQ // 256)), pl.BlockSpec((tm, 256), lambda i: (i, Z_CKV // 256)),
                  pl.BlockSpec((tm, 128), lambda i: (i, Z_KR // 128)), row128, row128, row128,
                  _const_spec((1, 256)), _const_spec((1, 256)), _const_spec((256, 1024)), _const_spec((256, 1024)),
                  _const_spec((1, 256)), _const_spec((1, 256))],
        out_specs=[pl.BlockSpec((hh, tm, QK_PAD), lambda i: (0, i, 0)), pl.BlockSpec((hh, tm, QK_PAD), lambda i: (0, i, 0)),
                   pl.BlockSpec((hh, tm, V_HEAD), lambda i: (0, i, 0))],
        out_shape=[jax.ShapeDtypeStruct((hh, s, QK_PAD), bf16), jax.ShapeDtypeStruct((hh, s, QK_PAD), bf16),
                   jax.ShapeDtypeStruct((hh, s, V_HEAD), bf16)],
        compiler_params=_cp(("parallel",), VMEM_LIMIT),
    )(z, z, z, cosb, sina, sinb, g_qa, g_kva, wqb, wkvb, g_qn, g_kn)


def _fwd_attn(q, k, v, tq):
    hh, s, _ = q.shape

    n_sub = max(1, tq // ATTN_SUB_ROWS)

    def body(q_ref, k_ref, v_ref, o_ref, o32_ref):
        for t in range(n_sub):
            rows = slice(t * (tq // n_sub), (t + 1) * (tq // n_sub))
            sc = _dot_nt(q_ref[rows, :], k_ref[...])
            p = jnp.exp2((sc - jnp.max(sc, axis=-1, keepdims=True)) * (ATTN_SCALE * LOG2_E))
            l = jnp.sum(p, axis=-1, keepdims=True)
            o = _dot(p.astype(bf16), v_ref[...]) * (1.0 / l)
            o_ref[rows, :] = o.astype(bf16)
            o32_ref[rows, :] = o

    out = pl.BlockSpec((tq, V_HEAD), lambda h, i: (i, h))
    return pl.pallas_call(
        body, name="fwd_attn", grid=(hh, s // tq),
        in_specs=[pl.BlockSpec((None, tq, QK_PAD), lambda h, i: (h, i, 0)),
                  pl.BlockSpec((None, s, QK_PAD), lambda h, i: (h, 0, 0)),
                  pl.BlockSpec((None, s, V_HEAD), lambda h, i: (h, 0, 0))],
        out_specs=[out, out],
        out_shape=[jax.ShapeDtypeStruct((s, hh * V_HEAD), bf16), jax.ShapeDtypeStruct((s, hh * V_HEAD), f32)],
        compiler_params=_cp(("parallel", "parallel"), VMEM_LIMIT),
    )(q, k, v)


def _split3(x):
    hi = x.astype(bf16)
    r1 = x - hi.astype(f32)
    mid = r1.astype(bf16)
    lo = (r1 - mid.astype(f32)).astype(bf16)
    return jnp.concatenate([hi, mid, lo], axis=-1)


def _tri_sum(tri, x):
    y = _dot(tri, _split3(x))
    return y[:, 0:128] + y[:, 128:256] + y[:, 256:384]


GLA_GROUP = 4
GLA_ROWS = GLA_GROUP * CHUNK
GLA_HEADS_PER_STEP = 2


def _gla_masks(rev):
    row = lax.broadcasted_iota(jnp.int32, (GLA_ROWS, GLA_ROWS), 0)
    col = lax.broadcasted_iota(jnp.int32, (GLA_ROWS, GLA_ROWS), 1)
    shift = CHUNK.bit_length() - 1
    same = (jnp.right_shift(row, shift) == jnp.right_shift(col, shift)).astype(f32)
    lower, upper = (row >= col).astype(f32) * same, (row <= col).astype(f32) * same
    keep, keep_t = (upper, lower) if rev else (lower, upper)
    chunk_of = jnp.right_shift(lax.broadcasted_iota(jnp.int32, (GLA_ROWS, 1), 0), shift)
    return keep, keep.astype(bf16), keep_t.astype(bf16), [(chunk_of == c).astype(f32) for c in range(GLA_GROUP)]


def _gla_gates(hq, hf, lower):
    sg = _sigmoid(hf)
    f = lower + (1.0 - lower) * sg
    return hq * _sigmoid(hq), 1.0 - f, jnp.log(f), f, sg


def _gla_last_mid(b, rev):
    b3 = b.reshape(GLA_GROUP, CHUNK, 128)
    last, mid = (0, CHUNK // 2) if rev else (CHUNK - 1, CHUNK // 2 - 1)
    return b3[:, last:last + 1, :], b3[:, mid:mid + 1, :]


def _gla_per_row(per_chunk):
    return jnp.broadcast_to(per_chunk, (GLA_GROUP, CHUNK, 128)).reshape(GLA_ROWS, 128)


def _gla_block_diag(x, row_masks):
    return jnp.concatenate([(x * m).astype(bf16) for m in row_masks], axis=-1)


def _gla_diag(y):
    return jnp.concatenate([y[CHUNK * c:CHUNK * (c + 1), 128 * c:128 * (c + 1)] for c in range(GLA_GROUP)], axis=0)


def _gla_rows(n, n_groups, rev):
    ne = n_groups - 1 - n if rev else n
    return pl.ds(pl.multiple_of(ne * GLA_ROWS, GLA_ROWS), GLA_ROWS), ne * GLA_GROUP


def _gla_scan_order(rev):
    return tuple(reversed(range(GLA_GROUP))) if rev else tuple(range(GLA_GROUP))


def _fwd_gla(z, lb4):
    s = z.shape[0]
    n_groups = s // GLA_ROWS
    assert n_groups % 2 == 0
    hp = GLA_HEADS_PER_STEP
    chains = [(hh, rev) for hh in range(hp) for rev in (False, True)]

    def body(hq_ref, hff_ref, hfb_ref, hi_ref, lb_ref, o_ref, st_ref):
        st_ref[...] = jnp.zeros_like(st_ref)
        masks = {rev: _gla_masks(rev) for rev in (False, True)}
        lowers = [_sigmoid(lb_ref[int(rev):int(rev) + 1, 128 * hh:128 * (hh + 1)]
                           - lb_ref[2 + int(rev):3 + int(rev), 128 * hh:128 * (hh + 1)]) for hh, rev in chains]

        def make_step(first):
            def step(n, carry):
                for ci, (hh, rev) in enumerate(chains):
                    cols = slice(128 * hh, 128 * (hh + 1))
                    rows, _ = _gla_rows(n, n_groups, rev)
                    maskf, tri, _, row_masks = masks[rev]
                    hf_ref = hfb_ref if rev else hff_ref
                    q, k, logf, _, _ = _gla_gates(hq_ref[rows, cols], hf_ref[rows, cols], lowers[ci])
                    vb = hi_ref[rows, cols].astype(bf16)
                    b = _tri_sum(tri, logf)
                    b_last3, b_mid3 = _gla_last_mid(b, rev)
                    b_last, b_mid = _gla_per_row(b_last3), _gla_per_row(b_mid3)
                    qi = (q * jnp.exp(b - b_mid)).astype(bf16)
                    ki = (k * jnp.exp(b_mid - b)).astype(bf16)
                    a = (_dot_nt(qi, ki) * maskf).astype(bf16)
                    kv = _dot_tn(vb, _gla_block_diag(k * jnp.exp(b_last - b), row_masks))
                    decay3 = jnp.exp(b_last3)
                    st = st_ref[ci]
                    before = [None] * GLA_GROUP
                    for c in _gla_scan_order(rev):
                        before[c] = st.astype(bf16)
                        st = st * decay3[c] + kv[:, 128 * c:128 * (c + 1)]
                    st_ref[ci] = st
                    inter = _dot_nt((q * jnp.exp(b)).astype(bf16), jnp.concatenate(before, axis=0))
                    o = _dot(a, vb) + _gla_diag(inter)
                    if first:
                        o_ref[rows, cols] = o
                    else:
                        o_ref[rows, cols] += o
                return carry
            return step

        lax.fori_loop(0, n_groups // 2, make_step(True), 0)
        lax.fori_loop(n_groups // 2, n_groups, make_step(False), 0)

    w = 128 * hp
    col = lambda base: pl.BlockSpec((s, w), lambda h: (0, base // w + h))
    return pl.pallas_call(
        body, name="fwd_gla", grid=(HG_HEADS // hp,),
        in_specs=[col(Z_HQ), col(Z_HFF), col(Z_HFB), col(Z_HI), pl.BlockSpec((4, w), lambda h: (0, h))],
        out_specs=pl.BlockSpec((s, w), lambda h: (0, h)),
        out_shape=jax.ShapeDtypeStruct((s, HG_HEADS * 128), f32),
        scratch_shapes=[pltpu.VMEM((len(chains), 128, 128), f32)],
        compiler_params=_cp(("parallel",), VMEM_LIMIT),
    )(z, z, z, z, lb4)


def _hg_out(o, hg, g_hgo):
    outs, ons, rs = [], [], []
    for h in range(HG_HEADS):
        oh = o[:, 128 * h:128 * (h + 1)]
        on, r = _rms_fwd(oh, g_hgo[:, 128 * h:128 * (h + 1)], 128)
        ons.append(on)
        rs.append(r)
    on = jnp.concatenate(ons, axis=-1)
    sg = _sigmoid(hg)
    return on * (hg * sg), on, rs, sg


def _fwd_mix(a, o, z, g_hgo, x, w_o, tm):
    s, d = x.shape

    def body(a_ref, o_ref, hg_ref, g_ref, x_ref, w_ref, x2_ref, cat_ref):
        r, _, _, _ = _hg_out(o_ref[...], hg_ref[...], g_ref[...])
        cat = jnp.concatenate([a_ref[...], r.astype(bf16)], axis=-1)
        cat_ref[...] = cat
        x2_ref[...] = x_ref[...] + _dot(cat, w_ref[...])

    row512 = pl.BlockSpec((tm, 512), lambda i: (i, 0))
    rowd = pl.BlockSpec((tm, d), lambda i: (i, 0))
    return pl.pallas_call(
        body, name="fwd_mix", grid=(s // tm,),
        in_specs=[row512, row512, pl.BlockSpec((tm, 512), lambda i: (i, Z_HG // 512)), _const_spec((1, 512)), rowd,
                  _const_spec((d, d))],
        out_specs=[rowd, rowd],
        out_shape=[jax.ShapeDtypeStruct((s, d), f32), jax.ShapeDtypeStruct((s, d), bf16)],
        compiler_params=_cp(("parallel",), VMEM_LIMIT),
    )(a, o, z, g_hgo, x, w_o)


def _fwd_ffn(x2, g_ffn, w_gate, w_up, w_down, tm):
    s, d = x2.shape

    def body(x_ref, g_ref, wg_ref, wu_ref, wd_ref, x3_ref, gp_ref, up_ref):
        x = x_ref[...]
        h, _ = _rms_fwd(x, g_ref[...], d)
        hb = h.astype(bf16)
        gp = _dot(hb, wg_ref[...])
        up = _dot(hb, wu_ref[...])
        gp_ref[...] = gp
        up_ref[...] = up
        act = (gp * _sigmoid(gp) * up).astype(bf16)
        x3_ref[...] = x + _dot(act, wd_ref[...])

    rowd = pl.BlockSpec((tm, d), lambda i: (i, 0))
    rowf = pl.BlockSpec((tm, D_FF), lambda i: (i, 0))
    return pl.pallas_call(
        body, name="fwd_ffn", grid=(s // tm,),
        in_specs=[rowd, _const_spec((1, d)), _const_spec((d, D_FF)), _const_spec((d, D_FF)), _const_spec((D_FF, d))],
        out_specs=[rowd, rowf, rowf],
        out_shape=[jax.ShapeDtypeStruct((s, d), f32), jax.ShapeDtypeStruct((s, D_FF), f32),
                   jax.ShapeDtypeStruct((s, D_FF), f32)],
        compiler_params=_cp(("parallel",), VMEM_LIMIT),
    )(x2, g_ffn, w_gate, w_up, w_down)


def _ple_loss_fwd_bwd(x3, g_ple, w_pg, p, w_pp, target, tm):
    s, d = x3.shape

    def body(x_ref, g_ref, wg_ref, p_ref, wp_ref, t_ref, dx_ref, h_ref, dpre_ref, dpp_ref, dg_ref, loss_ref):
        @pl.when(pl.program_id(0) == 0)
        def _():
            dg_ref[...] = jnp.zeros_like(dg_ref)
            loss_ref[...] = jnp.zeros_like(loss_ref)

        x = x_ref[...]
        g = g_ref[...]
        h, r = _rms_fwd(x, g, d)
        hb = h.astype(bf16)
        gate = _sigmoid(_dot(hb, wg_ref[...]))
        pp = _dot(p_ref[...].astype(bf16), wp_ref[...])
        e = x + gate * pp - t_ref[...]
        loss_ref[...] += 0.5 * jnp.sum(e * e) * (1.0 / d)
        dy = e * (1.0 / d)
        dpre = (dy * pp * gate * (1.0 - gate)).astype(bf16)
        dx, dgx = _rms_bwd(_dot_nt(dpre, wg_ref[...]), x, r, g, d)
        dx_ref[...] = dy + dx
        dg_ref[...] += jnp.sum(dgx, axis=0, keepdims=True)
        h_ref[...] = hb
        dpre_ref[...] = dpre
        dpp_ref[...] = (dy * gate).astype(bf16)

    rowd = pl.BlockSpec((tm, d), lambda i: (i, 0))
    return pl.pallas_call(
        body, name="ple_loss_fwd_bwd", grid=(s // tm,),
        in_specs=[rowd, _const_spec((1, d)), _const_spec((d, d)), pl.BlockSpec((tm, PLE_DIM), lambda i: (i, 0)),
                  _const_spec((PLE_DIM, d)), rowd],
        out_specs=[rowd, rowd, rowd, rowd, _acc_spec((1, d)), _acc_spec((8, 128))],
        out_shape=[jax.ShapeDtypeStruct((s, d), f32), jax.ShapeDtypeStruct((s, d), bf16), jax.ShapeDtypeStruct((s, d), bf16),
                   jax.ShapeDtypeStruct((s, d), bf16), jax.ShapeDtypeStruct((1, d), f32), jax.ShapeDtypeStruct((8, 128), f32)],
        compiler_params=_cp(("arbitrary",), VMEM_LIMIT),
    )(x3, g_ple, w_pg, p, w_pp, target)


def _bwd_ffn_hidden(d3, x2, gp, up, g_ffn, w_down, tm, tf):
    s, d = x2.shape

    def body(d3_ref, x_ref, gp_ref, up_ref, g_ref, wd_ref, h_ref, act_ref, dgp_ref, dup_ref, d3b_ref):
        @pl.when(pl.program_id(1) == 0)
        def _():
            h, _ = _rms_fwd(x_ref[...], g_ref[...], d)
            h_ref[...] = h.astype(bf16)
            d3b_ref[...] = d3_ref[...].astype(bf16)

        gp, up = gp_ref[...], up_ref[...]
        sg = _sigmoid(gp)
        silu = gp * sg
        act_ref[...] = (silu * up).astype(bf16)
        dact = _dot_nt(d3b_ref[...], wd_ref[...])
        dgp_ref[...] = (dact * up * (sg * (1.0 + gp * (1.0 - sg)))).astype(bf16)
        dup_ref[...] = (dact * silu).astype(bf16)

    rowd = pl.BlockSpec((tm, d), lambda i, f: (i, 0))
    rowf = pl.BlockSpec((tm, tf), lambda i, f: (i, f))
    return pl.pallas_call(
        body, name="bwd_ffn_hidden", grid=(s // tm, D_FF // tf),
        in_specs=[rowd, rowd, rowf, rowf, _const_spec((1, d)), pl.BlockSpec((tf, d), lambda i, f: (f, 0))],
        out_specs=[rowd, rowf, rowf, rowf],
        out_shape=[jax.ShapeDtypeStruct((s, d), bf16)] + [jax.ShapeDtypeStruct((s, D_FF), bf16)] * 3,
        scratch_shapes=[pltpu.VMEM((tm, d), bf16)],
        compiler_params=_cp(("parallel", "arbitrary"), VMEM_LIMIT),
    )(d3, x2, gp, up, g_ffn, w_down)


def _bwd_ffn_in(d3, x2, dgp, dup, g_ffn, w_gate, w_up, tm):
    s, d = x2.shape

    def body(d3_ref, x_ref, dgp_ref, dup_ref, g_ref, wg_ref, wu_ref, d2_ref, dg_ref):
        @pl.when(pl.program_id(0) == 0)
        def _():
            dg_ref[...] = jnp.zeros_like(dg_ref)

        x, g = x_ref[...], g_ref[...]
        dh = _dot_nt(dgp_ref[...], wg_ref[...]) + _dot_nt(dup_ref[...], wu_ref[...])
        r = lax.rsqrt(jnp.sum(x * x, axis=-1, keepdims=True) * (1.0 / d) + EPS)
        dx, dgx = _rms_bwd(dh, x, r, g, d)
        d2_ref[...] = d3_ref[...] + dx
        dg_ref[...] += jnp.sum(dgx, axis=0, keepdims=True)

    rowd = pl.BlockSpec((tm, d), lambda i: (i, 0))
    rowf = pl.BlockSpec((tm, D_FF), lambda i: (i, 0))
    return pl.pallas_call(
        body, name="bwd_ffn_in", grid=(s // tm,),
        in_specs=[rowd, rowd, rowf, rowf, _const_spec((1, d)), _const_spec((d, D_FF)), _const_spec((d, D_FF))],
        out_specs=[rowd, _acc_spec((1, d))],
        out_shape=[jax.ShapeDtypeStruct((s, d), f32), jax.ShapeDtypeStruct((1, d), f32)],
        compiler_params=_cp(("arbitrary",), VMEM_LIMIT),
    )(d3, x2, dgp, dup, g_ffn, w_gate, w_up)


def _bwd_mix(d2, w_o, o, z, g_hgo, tm):
    s, d = d2.shape

    def body(d2_ref, w_ref, o_ref, hg_ref, g_ref, da_ref, do_ref, dhg_ref, dg_ref):
        @pl.when(pl.program_id(0) == 0)
        def _():
            dg_ref[...] = jnp.zeros_like(dg_ref)

        dcat = _dot_nt(d2_ref[...].astype(bf16), w_ref[...])
        da_ref[...] = dcat[:, 0:512].astype(bf16)
        dr = dcat[:, 512:1024]
        o, hg, g = o_ref[...], hg_ref[...], g_ref[...]
        _, on, rs, sg = _hg_out(o, hg, g)
        dhg_ref[...] = (dr * on * (sg * (1.0 + hg * (1.0 - sg)))).astype(bf16)
        don = dr * (hg * sg)
        dgs = []
        for h in range(HG_HEADS):
            cols = slice(128 * h, 128 * (h + 1))
            dx, dgx = _rms_bwd(don[:, cols], o[:, cols], rs[h], g[:, cols], 128)
            do_ref[:, cols] = dx
            dgs.append(jnp.sum(dgx, axis=0, keepdims=True))
        dg_ref[...] += jnp.concatenate(dgs, axis=-1)

    row512 = pl.BlockSpec((tm, 512), lambda i: (i, 0))
    return pl.pallas_call(
        body, name="bwd_mix", grid=(s // tm,),
        in_specs=[pl.BlockSpec((tm, d), lambda i: (i, 0)), _const_spec((d, d)), row512,
                  pl.BlockSpec((tm, 512), lambda i: (i, Z_HG // 512)), _const_spec((1, 512))],
        out_specs=[row512, row512, row512, _acc_spec((1, 512))],
        out_shape=[jax.ShapeDtypeStruct((s, 512), bf16), jax.ShapeDtypeStruct((s, 512), f32), jax.ShapeDtypeStruct((s, 512), bf16),
                   jax.ShapeDtypeStruct((1, 512), f32)],
        compiler_params=_cp(("arbitrary",), VMEM_LIMIT),
    )(d2, w_o, o, z, g_hgo)


def _bwd_gla(z, lb4, do):
    s = z.shape[0]
    n_chunks = s // CHUNK
    n_groups = s // GLA_ROWS
    assert n_groups % 2 == 0

    def body(hq_ref, hff_ref, hfb_ref, hi_ref, lb_ref, do_ref, dhq_ref, dhff_ref, dhfb_ref, dhi_ref, dlb_ref,
             st_all, b_all, dst_ref, dq_acc, dv_acc, dlow_ref):
        dirs = (False, True)
        masks = [_gla_masks(rev) for rev in dirs]
        lowers = [_sigmoid(lb_ref[int(rev):int(rev) + 1, :] - lb_ref[2 + int(rev):3 + int(rev), :]) for rev in dirs]
        hf_refs, dhf_refs = (hff_ref, hfb_ref), (dhff_ref, dhfb_ref)

        def fwd_step(n, sts):
            new = []
            for d, rev in enumerate(dirs):
                rows, chunk0 = _gla_rows(n, n_groups, rev)
                _, k, logf, _, _ = _gla_gates(hq_ref[rows, :], hf_refs[d][rows, :], lowers[d])
                b = _tri_sum(masks[d][1], logf)
                b_last3, _ = _gla_last_mid(b, rev)
                b_all[d, rows, :] = b
                kv = _dot_tn(hi_ref[rows, :].astype(bf16), _gla_block_diag(k * jnp.exp(_gla_per_row(b_last3) - b), masks[d][3]))
                decay3 = jnp.exp(b_last3)
                st = sts[d]
                for c in _gla_scan_order(rev):
                    st_all[d, chunk0 + c] = st
                    st = st * decay3[c] + kv[:, 128 * c:128 * (c + 1)]
                new.append(st)
            return tuple(new)

        zero = jnp.zeros((128, 128), f32)
        lax.fori_loop(0, n_groups, fwd_step, (zero, zero))

        dst_ref[...] = jnp.zeros_like(dst_ref)
        dlow_ref[...] = jnp.zeros_like(dlow_ref)

        def make_bwd_step(first):
            def bwd_step(j, carry):
                n = n_groups - 1 - j
                for d, rev in enumerate(dirs):
                    maskf, _, tri_t, row_masks = masks[d]
                    lower = lowers[d]
                    rows, chunk0 = _gla_rows(n, n_groups, rev)
                    hq, hf = hq_ref[rows, :], hf_refs[d][rows, :]
                    q, k, _, f, sg = _gla_gates(hq, hf, lower)
                    v = hi_ref[rows, :]
                    dout = do_ref[rows, :]
                    b = b_all[d, rows, :]
                    b_last3, b_mid3 = _gla_last_mid(b, rev)
                    b_last, b_mid = _gla_per_row(b_last3), _gla_per_row(b_mid3)
                    e1, e2, e3, e4 = jnp.exp(b - b_mid), jnp.exp(b_mid - b), jnp.exp(b_last - b), jnp.exp(b)
                    decay3 = jnp.exp(b_last3)
                    qi, ki, kt, qt = q * e1, k * e2, k * e3, q * e4
                    qib, kib, ktb = qi.astype(bf16), ki.astype(bf16), kt.astype(bf16)
                    vb, dob = v.astype(bf16), dout.astype(bf16)
                    a = (_dot_nt(qib, kib) * maskf).astype(bf16)
                    da = (_dot_nt(dob, vb) * maskf).astype(bf16)
                    dqi = _dot(da, kib)
                    dki = _dot_tn(da, qib)
                    into_state = _dot_tn(dob, _gla_block_diag(qt, row_masks))
                    dst = dst_ref[d]
                    sts, dsts, ddecay = [None] * GLA_GROUP, [None] * GLA_GROUP, [None] * GLA_GROUP
                    for c in reversed(_gla_scan_order(rev)):
                        sts[c] = st_all[d, chunk0 + c]
                        dsts[c] = dst.astype(bf16)
                        ddecay[c] = jnp.sum(dst * sts[c], axis=0, keepdims=True)[None]
                        dst = dst * decay3[c] + into_state[:, 128 * c:128 * (c + 1)]
                    dst_ref[d] = dst
                    dv = _dot_tn(a, dob) + _gla_diag(_dot_nt(ktb, jnp.concatenate(dsts, axis=0)))
                    dqt = _gla_diag(_dot(dob, jnp.concatenate([x.astype(bf16) for x in sts], axis=-1)))
                    dkt = _gla_diag(_dot(vb, jnp.concatenate(dsts, axis=-1)))
                    dq = dqi * e1 + dqt * e4
                    dk = dki * e2 + dkt * e3
                    db = dqi * qi - dki * ki + dqt * qt - dkt * kt
                    dlast3 = (jnp.sum((dkt * kt).reshape(GLA_GROUP, CHUNK, 128), axis=1, keepdims=True)
                              + jnp.concatenate(ddecay, axis=0) * decay3)
                    dlogf = _tri_sum(tri_t, db) + _gla_per_row(dlast3)
                    df = dlogf / f - dk
                    dhf_refs[d][rows, :] = (df * (1.0 - lower) * sg * (1.0 - sg)).astype(bf16)
                    dlow_ref[d:d + 1, :] += jnp.sum(df * (1.0 - sg), axis=0, keepdims=True)
                    sq = _sigmoid(hq)
                    dhq = dq * (sq * (1.0 + hq * (1.0 - sq)))
                    if first:
                        dq_acc[rows, :] = dhq
                        dv_acc[rows, :] = dv
                    else:
                        dhq_ref[rows, :] = (dq_acc[rows, :] + dhq).astype(bf16)
                        dhi_ref[rows, :] = (dv_acc[rows, :] + dv).astype(bf16)
                return carry
            return bwd_step

        lax.fori_loop(0, n_groups // 2, make_bwd_step(True), 0)
        lax.fori_loop(n_groups // 2, n_groups, make_bwd_step(False), 0)

        for d in range(2):
            dl = dlow_ref[d:d + 1, :] * lowers[d] * (1.0 - lowers[d])
            dlb_ref[d:d + 1, :] = dl
            dlb_ref[2 + d:3 + d, :] = -dl

    col = lambda base: pl.BlockSpec((s, 128), lambda h: (0, base // 128 + h))
    return pl.pallas_call(
        body, name="bwd_gla", grid=(HG_HEADS,),
        in_specs=[col(Z_HQ), col(Z_HFF), col(Z_HFB), col(Z_HI), pl.BlockSpec((4, 128), lambda h: (0, h)), col(0)],
        out_specs=[col(0), col(0), col(0), col(0), pl.BlockSpec((4, 128), lambda h: (0, h))],
        out_shape=[jax.ShapeDtypeStruct((s, 512), bf16)] * 4 + [jax.ShapeDtypeStruct((4, 512), f32)],
        scratch_shapes=[pltpu.VMEM((2, n_chunks, 128, 128), f32), pltpu.VMEM((2, s, 128), f32), pltpu.VMEM((2, 128, 128), f32),
                        pltpu.VMEM((s, 128), f32), pltpu.VMEM((s, 128), f32), pltpu.VMEM((2, 128), f32)],
        compiler_params=_cp(("parallel",), VMEM_LIMIT),
    )(z, z, z, z, lb4, do)


def _bwd_attn(q, k, v, da, a32, tq):
    hh, s, _ = q.shape

    n_sub = max(1, tq // ATTN_SUB_ROWS)

    def body(q_ref, k_ref, v_ref, do_ref, o_ref, dq_ref, dk_ref, dv_ref, p_all, ds_all, dol_ref, dkt_ref, dvt_ref):
        @pl.when(pl.program_id(1) == 0)
        def _():
            dkt_ref[...] = jnp.zeros_like(dkt_ref)
            dvt_ref[...] = jnp.zeros_like(dvt_ref)

        kb, vb = k_ref[...], v_ref[...]
        for t in range(n_sub):
            rows = slice(t * (tq // n_sub), (t + 1) * (tq // n_sub))
            sc = _dot_nt(q_ref[rows, :], kb)
            p = jnp.exp2((sc - jnp.max(sc, axis=-1, keepdims=True)) * (ATTN_SCALE * LOG2_E))
            inv_l = 1.0 / jnp.sum(p, axis=-1, keepdims=True)
            pb = p.astype(bf16)
            dob = do_ref[rows, :]
            dof = dob.astype(f32)
            delta = jnp.sum(dof * o_ref[rows, :], axis=-1, keepdims=True)
            ds = pb * ((_dot_nt(dob, vb) - delta) * inv_l).astype(bf16)
            dq_ref[rows, :] = _dot(ds, kb) * ATTN_SCALE
            p_all[rows, :] = pb
            ds_all[rows, :] = ds
            dol_ref[rows, :] = (dof * inv_l).astype(bf16)
        dkt_ref[...] += _dot_tn(q_ref[...], ds_all[...])
        dvt_ref[...] += _dot_tn(dol_ref[...], p_all[...])

        @pl.when(pl.program_id(1) == s // tq - 1)
        def _():
            dk_ref[...] = dkt_ref[...].T * ATTN_SCALE
            dv_ref[...] = dvt_ref[...].T

    return pl.pallas_call(
        body, name="bwd_attn", grid=(hh, s // tq),
        in_specs=[pl.BlockSpec((None, tq, QK_PAD), lambda h, i: (h, i, 0)),
                  pl.BlockSpec((None, s, QK_PAD), lambda h, i: (h, 0, 0)),
                  pl.BlockSpec((None, s, V_HEAD), lambda h, i: (h, 0, 0)),
                  pl.BlockSpec((tq, V_HEAD), lambda h, i: (i, h)), pl.BlockSpec((tq, V_HEAD), lambda h, i: (i, h))],
        out_specs=[pl.BlockSpec((None, tq, QK_PAD), lambda h, i: (h, i, 0)),
                   pl.BlockSpec((None, s, QK_PAD), lambda h, i: (h, 0, 0)),
                   pl.BlockSpec((None, s, V_HEAD), lambda h, i: (h, 0, 0))],
        out_shape=[jax.ShapeDtypeStruct((hh, s, QK_PAD), f32), jax.ShapeDtypeStruct((hh, s, QK_PAD), f32),
                   jax.ShapeDtypeStruct((hh, s, V_HEAD), f32)],
        scratch_shapes=[pltpu.VMEM((tq, s), bf16), pltpu.VMEM((tq, s), bf16), pltpu.VMEM((tq, V_HEAD), bf16),
                        pltpu.VMEM((QK_PAD, s), f32), pltpu.VMEM((V_HEAD, s), f32)],
        compiler_params=_cp(("parallel", "arbitrary"), VMEM_LIMIT),
    )(q, k, v, da, a32)


def _bwd_mla_proj(z, dq, dk, dv, cosb, sina, sinb, g_qa, g_kva, wqb, wkvb, g_qn, g_kn, tm):
    s = z.shape[0]
    hh = MLA_HEADS

    def body(cq_ref, ckv_ref, kr_ref, dq_ref, dk_ref, dv_ref, c_ref, sa_ref, sb_ref, gqa_ref, gkva_ref, wqb_ref, wkvb_ref,
             gqn_ref, gkn_ref, dz_ref, cqn_ref, ckvn_ref, dq0_ref, dkv0_ref, dgqa_ref, dgkva_ref, dgqn_ref, dgkn_ref):
        @pl.when(pl.program_id(0) == 0)
        def _():
            for r in (dgqa_ref, dgkva_ref, dgqn_ref, dgkn_ref):
                r[...] = jnp.zeros_like(r)

        cq, ckv, kr = cq_ref[...], ckv_ref[...], kr_ref[...]
        gqa, gkva, gqn, gkn = gqa_ref[...], gkva_ref[...], gqn_ref[...], gkn_ref[...]
        cqn_b, rq, ckvn_b, rkv, q0, kv0 = _mla_qk_fwd(cq, ckv, kr, gqa, gkva, wqb_ref[...], wkvb_ref[...], gqn, gkn)
        cqn_ref[...] = cqn_b
        ckvn_ref[...] = ckvn_b
        c, sa, sb = c_ref[...], -sa_ref[...], -sb_ref[...]
        kr_sq = jnp.sum(kr * kr, axis=-1, keepdims=True)
        dkr = jnp.zeros_like(kr)
        dgqn = jnp.zeros((1, QK_PAD), f32)
        dgkn = jnp.zeros((1, QK_PAD), f32)
        for h in range(hh):
            qh = q0[:, QK_PAD * h:QK_PAD * (h + 1)]
            rh = lax.rsqrt(jnp.sum(qh * qh, axis=-1, keepdims=True) * (1.0 / QK_HEAD) + EPS)
            dqh = dq_ref[h]
            dqn = jnp.concatenate([dqh[:, 0:128], _rope(dqh[:, 128:256], c, sa, sb)], axis=-1)
            dq0h, dgx = _rms_bwd(dqn, qh, rh, gqn, QK_HEAD)
            dq0_ref[:, QK_PAD * h:QK_PAD * (h + 1)] = dq0h.astype(bf16)
            dgqn = dgqn + jnp.sum(dgx, axis=0, keepdims=True)

            kn_ = kv0[:, 256 * h:256 * h + 128]
            k0 = jnp.concatenate([kn_, kr], axis=-1)
            rk = lax.rsqrt((jnp.sum(kn_ * kn_, axis=-1, keepdims=True) + kr_sq) * (1.0 / QK_HEAD) + EPS)
            dkh = dk_ref[h]
            dkn = jnp.concatenate([dkh[:, 0:128], _rope(dkh[:, 128:256], c, sa, sb)], axis=-1)
            dk0, dgx = _rms_bwd(dkn, k0, rk, gkn, QK_HEAD)
            dgkn = dgkn + jnp.sum(dgx, axis=0, keepdims=True)
            dkv0_ref[:, 256 * h:256 * h + 128] = dk0[:, 0:128].astype(bf16)
            dkv0_ref[:, 256 * h + 128:256 * h + 256] = dv_ref[h].astype(bf16)
            dkr = dkr + dk0[:, 128:256]
        dgqn_ref[...] += dgqn
        dgkn_ref[...] += dgkn
        dcq, dgx = _rms_bwd(_dot_nt(dq0_ref[...], wqb_ref[...]), cq, rq, gqa, Q_LORA)
        dgqa_ref[...] += jnp.sum(dgx, axis=0, keepdims=True)
        dckv, dgx = _rms_bwd(_dot_nt(dkv0_ref[...], wkvb_ref[...]), ckv, rkv, gkva, KV_LORA)
        dgkva_ref[...] += jnp.sum(dgx, axis=0, keepdims=True)
        dz_ref[:, 0:256] = dcq.astype(bf16)
        dz_ref[:, 256:512] = dckv.astype(bf16)
        dz_ref[:, 512:640] = dkr.astype(bf16)

    row128 = pl.BlockSpec((tm, 128), lambda i: (i, 0))
    row256 = pl.BlockSpec((tm, 256), lambda i: (i, 0))
    row1024 = pl.BlockSpec((tm, 1024), lambda i: (i, 0))
    hd = lambda w: pl.BlockSpec((hh, tm, w), lambda i: (0, i, 0))
    return pl.pallas_call(
        body, name="bwd_mla_proj", grid=(s // tm,),
        in_specs=[pl.BlockSpec((tm, 256), lambda i: (i, Z_CQ // 256)), pl.BlockSpec((tm, 256), lambda i: (i, Z_CKV // 256)),
                  pl.BlockSpec((tm, 128), lambda i: (i, Z_KR // 128)), hd(QK_PAD), hd(QK_PAD), hd(V_HEAD),
                  row128, row128, row128,
                  _const_spec((1, 256)), _const_spec((1, 256)), _const_spec((256, 1024)), _const_spec((256, 1024)),
                  _const_spec((1, 256)), _const_spec((1, 256))],
        out_specs=[pl.BlockSpec((tm, 640), lambda i: (i, 0)), row256, row256, row1024, row1024,
                   _acc_spec((1, 256)), _acc_spec((1, 256)), _acc_spec((1, 256)), _acc_spec((1, 256))],
        out_shape=[jax.ShapeDtypeStruct((s, 640), bf16), jax.ShapeDtypeStruct((s, 256), bf16), jax.ShapeDtypeStruct((s, 256), bf16),
                   jax.ShapeDtypeStruct((s, 1024), bf16), jax.ShapeDtypeStruct((s, 1024), bf16)]
        + [jax.ShapeDtypeStruct((1, 256), f32)] * 4,
        compiler_params=_cp(("arbitrary",), VMEM_LIMIT),
    )(z, z, z, dq, dk, dv, cosb, sina, sinb, g_qa, g_kva, wqb, wkvb, g_qn, g_kn)


def _bwd_in(segments, wz, x, g_mix, d2, tm):
    s, d = x.shape
    n_seg = len(segments)

    def body(*refs):
        dz_refs, w_refs = refs[:n_seg], refs[n_seg:2 * n_seg]
        x_ref, g_ref, d2_ref, gx_ref, dg_ref = refs[2 * n_seg:]

        @pl.when(pl.program_id(0) == 0)
        def _():
            dg_ref[...] = jnp.zeros_like(dg_ref)

        dh = _dot_nt(dz_refs[0][...], w_refs[0][...])
        for a_ref, w_ref in zip(dz_refs[1:], w_refs[1:]):
            dh = dh + _dot_nt(a_ref[...], w_ref[...])
        x, g = x_ref[...], g_ref[...]
        r = lax.rsqrt(jnp.sum(x * x, axis=-1, keepdims=True) * (1.0 / d) + EPS)
        dx, dgx = _rms_bwd(dh, x, r, g, d)
        gx_ref[...] = d2_ref[...] + dx
        dg_ref[...] += jnp.sum(dgx, axis=0, keepdims=True)

    rowd = pl.BlockSpec((tm, d), lambda i: (i, 0))
    dz_specs = [pl.BlockSpec((tm, w), functools.partial(lambda i, j: (i, j), j=ja)) for _, w, ja, _ in segments]
    w_specs = [pl.BlockSpec((d, w), functools.partial(lambda i, j: (0, j), j=jw), pipeline_mode=pl.Buffered(1))
               for _, w, _, jw in segments]
    return pl.pallas_call(
        body, name="bwd_in", grid=(s // tm,),
        in_specs=dz_specs + w_specs + [rowd, _const_spec((1, d)), rowd],
        out_specs=[rowd, _acc_spec((1, d))],
        out_shape=[jax.ShapeDtypeStruct((s, d), f32), jax.ShapeDtypeStruct((1, d), f32)],
        compiler_params=_cp(("arbitrary",), VMEM_LIMIT),
    )(*[a for a, _, _, _ in segments], *([wz] * n_seg), x, g_mix, d2)


def _pick_tile(n, cap):
    best = None
    for t in range(LANES, cap + 1, LANES):
        if n % t == 0:
            best = t
    return best if best is not None else n


def _mm_tn_many(a, bs, name, tm):
    kk, m = a.shape
    n_b = len(bs)
    tk = min(512, kk)
    n_k = kk // tk

    def body(a_ref, *refs):
        b_refs, o_refs, acc_refs = refs[:n_b], refs[n_b:2 * n_b], refs[2 * n_b:]

        @pl.when(pl.program_id(1) == 0)
        def _():
            for acc in acc_refs:
                acc[...] = jnp.zeros_like(acc)
        a_blk = a_ref[...].astype(bf16)
        for b_ref, acc in zip(b_refs, acc_refs):
            acc[...] += _dot_tn(a_blk, b_ref[...].astype(bf16))

        @pl.when(pl.program_id(1) == n_k - 1)
        def _():
            for o_ref, acc in zip(o_refs, acc_refs):
                o_ref[...] = acc[...].astype(bf16)

    return pl.pallas_call(
        body, name=name, grid=(m // tm, n_k),
        in_specs=[pl.BlockSpec((tk, tm), lambda i, k: (k, i))] + [pl.BlockSpec((tk, b.shape[1]), lambda i, k: (k, 0)) for b in bs],
        out_specs=[pl.BlockSpec((tm, b.shape[1]), lambda i, k: (i, 0)) for b in bs],
        out_shape=[jax.ShapeDtypeStruct((m, b.shape[1]), bf16) for b in bs],
        scratch_shapes=[pltpu.VMEM((tm, b.shape[1]), f32) for b in bs],
        compiler_params=_cp(("parallel", "arbitrary"), VMEM_LIMIT),
    )(a, *bs)


def _mm_tn(a, b, name):
    kk, m = a.shape
    _, n = b.shape
    tm = _pick_tile(m, 1408)
    tn = _pick_tile(n, 1408)
    tk = min(512, kk)

    n_k = kk // tk

    def body(a_ref, b_ref, o_ref, acc_ref):
        @pl.when(pl.program_id(2) == 0)
        def _():
            acc_ref[...] = jnp.zeros_like(acc_ref)
        acc_ref[...] += _dot_tn(a_ref[...].astype(bf16), b_ref[...].astype(bf16))

        @pl.when(pl.program_id(2) == n_k - 1)
        def _():
            o_ref[...] = acc_ref[...].astype(bf16)

    return pl.pallas_call(
        body, name=name, grid=(m // tm, n // tn, n_k),
        in_specs=[pl.BlockSpec((tk, tm), lambda i, j, k: (k, i)), pl.BlockSpec((tk, tn), lambda i, j, k: (k, j))],
        out_specs=pl.BlockSpec((tm, tn), lambda i, j, k: (i, j)),
        out_shape=jax.ShapeDtypeStruct((m, n), bf16),
        scratch_shapes=[pltpu.VMEM((tm, tn), f32)],
        compiler_params=_cp(("parallel", "parallel", "arbitrary"), VMEM_LIMIT),
    )(a, b)


def _rope_tables(positions):
    inv_freq = ROPE_THETA ** (-jnp.arange(0, QK_ROPE, 2, dtype=f32) / QK_ROPE)
    ang = positions.astype(f32)[:, None] * inv_freq
    cos, sin = jnp.cos(ang), jnp.sin(ang)
    zero = jnp.zeros_like(cos)
    return (jnp.concatenate([cos, cos, zero, zero], axis=1), jnp.concatenate([zero, sin, zero, zero], axis=1),
            jnp.concatenate([-sin, zero, zero, zero], axis=1))


def _pad256(g):
    return jnp.pad(g.reshape(1, QK_HEAD), ((0, 0), (0, QK_PAD - QK_HEAD)))


RELAYOUT_BLOCKS = 8
FIRST = ("w_in", "w_qb", "w_kvb", "lb_param")
SECOND = ("w_o", "w_gate", "w_up", "w_down", "w_ple_gate", "w_ple_proj")
ROW_SHARDED = ("w_o", "w_down", "w_ple_gate")


def _col_moves(j):
    lo = BIG["w_in"][1] * j
    w_in = [(max(lo, a) - lo, min(lo + BIG["w_in"][1], b) - lo, d + max(lo, a) - a)
            for a, b, d in Z_SEGMENTS if max(lo, a) < min(lo + BIG["w_in"][1], b)]
    head, half = divmod(j, 2)
    whole = lambda n: [(0, BIG[n][1], BIG[n][1] * j)]
    return {"w_in": w_in, "w_gate": whole("w_gate"), "w_up": whole("w_up"),
            "w_qb": [(0, 96, QK_PAD * head + 96 * half)], "w_kvb": whole("w_kvb"), "w_ple_proj": whole("w_ple_proj"),
            "lb_param": whole("lb_param")}


def _kernel_width(name):
    return {"w_in": Z_W, "w_qb": MLA_HEADS * QK_PAD}.get(name, N_DEV * BIG[name][1])


def _relayout_specs(names, by_dev):
    specs = []
    for n in names:
        rows, cols = BIG[n]
        if n == "lb_param":
            specs.append(_acc_spec((N_DEV, rows, cols) if by_dev else (rows, _kernel_width(n))))
        elif by_dev:
            specs.append(pl.BlockSpec((N_DEV, rows // RELAYOUT_BLOCKS, cols), lambda i: (0, i, 0)))
        else:
            specs.append(pl.BlockSpec((rows // RELAYOUT_BLOCKS, _kernel_width(n)), lambda i: (i, 0)))
    return specs


def _weights_in(gathered, names, name):
    n = len(names)

    def body(*refs):
        ins, outs = dict(zip(names, refs[:n])), dict(zip(names, refs[n:]))
        if "w_in" in outs:
            outs["w_in"][:, Z_KR + QK_ROPE:Z_W] = jnp.zeros((outs["w_in"].shape[0], Z_W - Z_KR - QK_ROPE), bf16)
        if "w_qb" in outs:
            for h in range(MLA_HEADS):
                outs["w_qb"][:, QK_PAD * h + QK_HEAD:QK_PAD * (h + 1)] = jnp.zeros((outs["w_qb"].shape[0], QK_PAD - QK_HEAD), bf16)
        for j in range(N_DEV):
            for wn, moves in _col_moves(j).items():
                if wn in outs:
                    for s0, s1, d0 in moves:
                        outs[wn][:, d0:d0 + s1 - s0] = ins[wn][j, :, s0:s1]

    outs = pl.pallas_call(
        body, name=name, grid=(RELAYOUT_BLOCKS,), in_specs=_relayout_specs(names, True), out_specs=_relayout_specs(names, False),
        out_shape=[jax.ShapeDtypeStruct((BIG[wn][0], _kernel_width(wn)), gathered[wn].dtype) for wn in names],
        compiler_params=_cp(("arbitrary",), VMEM_LIMIT),
    )(*[gathered[wn] for wn in names])
    return dict(zip(names, outs))


def _grads_out(sources, names, name):
    pieces = [(wn, start, arr) for wn in names for start, arr in sources[wn]]
    n_in = len(pieces)

    def body(*refs):
        outs = dict(zip(names, refs[n_in:]))

        def cols(wn, c0, c1):
            for (pn, start, arr), ref in zip(pieces, refs[:n_in]):
                if pn == wn and start <= c0 and c1 <= start + arr.shape[1]:
                    return ref[:, c0 - start:c1 - start]

        for j in range(N_DEV):
            for wn, moves in _col_moves(j).items():
                if wn in outs:
                    for s0, s1, d0 in moves:
                        outs[wn][j, :, s0:s1] = cols(wn, d0, d0 + s1 - s0).astype(bf16)

    in_specs = [_acc_spec(arr.shape) if wn == "lb_param" else pl.BlockSpec((arr.shape[0] // RELAYOUT_BLOCKS, arr.shape[1]), lambda i: (i, 0))
                for wn, _, arr in pieces]
    outs = pl.pallas_call(
        body, name=name, grid=(RELAYOUT_BLOCKS,), in_specs=in_specs, out_specs=_relayout_specs(names, True),
        out_shape=[jax.ShapeDtypeStruct((N_DEV, *BIG[wn]), bf16) for wn in names],
        compiler_params=_cp(("arbitrary",), VMEM_LIMIT),
    )(*[arr for _, _, arr in pieces])
    return dict(zip(names, outs))


def kernel(x, p, positions, g_mix, w_in, g_qa, g_kva, w_qb, w_kvb, g_qn, g_kn, lb_param, g_hgo, w_o, g_ffn, w_gate, w_up, w_down, g_ple, w_ple_gate, w_ple_proj, loss_target, m_g_mix, m_w_in, m_g_qa, m_g_kva, m_w_qb, m_w_kvb, m_g_qn, m_g_kn, m_lb_param, m_g_hgo, m_w_o, m_g_ffn, m_w_gate, m_w_up, m_w_down, m_g_ple, m_w_ple_gate, m_w_ple_proj, v_g_mix, v_w_in, v_g_qa, v_g_kva, v_w_qb, v_w_kvb, v_g_qn, v_g_kn, v_lb_param, v_g_hgo, v_w_o, v_g_ffn, v_w_gate, v_w_up, v_w_down, v_g_ple, v_w_ple_gate, v_w_ple_proj):
    w_all = dict(g_mix=g_mix, g_qa=g_qa, g_kva=g_kva, g_qn=g_qn, g_kn=g_kn, g_hgo=g_hgo, g_ffn=g_ffn, g_ple=g_ple,
                 w_in=w_in, w_qb=w_qb, w_kvb=w_kvb, w_o=w_o, w_gate=w_gate, w_up=w_up, w_down=w_down,
                 w_ple_gate=w_ple_gate, w_ple_proj=w_ple_proj, lb_param=lb_param)
    m_all = dict(g_mix=m_g_mix, g_qa=m_g_qa, g_kva=m_g_kva, g_qn=m_g_qn, g_kn=m_g_kn, g_hgo=m_g_hgo, g_ffn=m_g_ffn,
                 g_ple=m_g_ple, w_in=m_w_in, w_qb=m_w_qb, w_kvb=m_w_kvb, w_o=m_w_o, w_gate=m_w_gate, w_up=m_w_up,
                 w_down=m_w_down, w_ple_gate=m_w_ple_gate, w_ple_proj=m_w_ple_proj, lb_param=m_lb_param)
    v_all = dict(g_mix=v_g_mix, g_qa=v_g_qa, g_kva=v_g_kva, g_qn=v_g_qn, g_kn=v_g_kn, g_hgo=v_g_hgo, g_ffn=v_g_ffn,
                 g_ple=v_g_ple, w_in=v_w_in, w_qb=v_w_qb, w_kvb=v_w_kvb, w_o=v_w_o, w_gate=v_w_gate, w_up=v_w_up,
                 w_down=v_w_down, w_ple_gate=v_w_ple_gate, w_ple_proj=v_w_ple_proj, lb_param=v_lb_param)
    me_idx = jnp.stack([_me()]).astype(jnp.int32)
    x, p, positions, target = x[0], p[0, 0], positions[0], loss_target[0]
    s = x.shape[0]
    tm, tm_ffn, tq_f, tq_b = min(512, s), min(1024, s), min(2048, s), min(1024, s)
    g_mix, g_qa, g_kva, g_qn, g_kn, g_hgo, g_ffn, g_ple = (w_all[n].reshape(1, -1) for n in SMALL)
    g_qn_p, g_kn_p = _pad256(g_qn), _pad256(g_kn)
    cosb, sina, sinb = _rope_tables(positions)
    shard = lambda n: w_all[n].reshape(BIG[n])

    first = _all_gather([shard(n) for n in FIRST], [f32 if n == "lb_param" else bf16 for n in FIRST], "ag_first")
    lands = _cast_to_slot([shard(n) for n in SECOND], me_idx, first[0])
    ag2, token = _exchange_start([], lands, "ag_second_start")
    wk = _weights_in(dict(zip(FIRST, first)), FIRST, "weights_in_first")
    wz, wqb, wkvb, lb4 = (wk[n] for n in FIRST)

    h1, z = _fwd_in(x, g_mix, wz, tm)
    q, k, v = _fwd_mla_proj(z, cosb + token[0, 0], sina, sinb, g_qa, g_kva, wqb, wkvb, g_qn_p, g_kn_p, tm)
    a, a32 = _fwd_attn(q, k, v, tq_f)
    o = _fwd_gla(z, lb4)

    second = dict(zip(SECOND, _exchange_wait(ag2, [a, o], "ag_second_wait")[1]))
    wk = _weights_in(second, ("w_gate", "w_up", "w_ple_proj"), "weights_in_second")
    w_gate, w_up, w_pp = wk["w_gate"], wk["w_up"], wk["w_ple_proj"]
    w_o, w_down, w_pg = (second[n].reshape(N_DEV * BIG[n][0], BIG[n][1]) for n in ROW_SHARDED)

    x2, cat = _fwd_mix(a, o, z, g_hgo, x, w_o, tm)
    x3, gp, up = _fwd_ffn(x2, g_ffn, w_gate, w_up, w_down, min(256, s))
    d3, h3, dpre, dpp, dg_ple, loss_tile = _ple_loss_fwd_bwd(x3, g_ple, w_pg, p, w_pp, target, tm)
    h2, act, dgp, dup = _bwd_ffn_hidden(d3, x2, gp, up, g_ffn, w_down, tm, D_FF // 2)
    d2, dg_ffn = _bwd_ffn_in(d3, x2, dgp, dup, g_ffn, w_gate, w_up, tm)

    gw_gate, gw_up = _mm_tn_many(h2, [dgp, dup], "dw_gate_up", 512)
    blocks = _grads_out({"w_gate": [(0, gw_gate)], "w_up": [(0, gw_up)], "w_ple_proj": [(0, _mm_tn(p, dpp, "dw_ple_proj"))]},
                        ("w_gate", "w_up", "w_ple_proj"), "grads_out_second")
    row_grads = {"w_o": _mm_tn(cat, d2, "dw_o"), "w_down": _mm_tn(act, d3, "dw_down"), "w_ple_gate": _mm_tn(h3, dpre, "dw_ple_gate")}
    blocks.update({n: g.reshape(N_DEV, *BIG[n]) for n, g in row_grads.items()})
    empty = lambda names: [lax.empty((N_PEERS, *BIG[n]), bf16) for n in names]
    rs2, token = _exchange_start([blocks[n] for n in SECOND], empty(SECOND), "rs_second_start")

    da, do, dz_hg, dg_hgo = _bwd_mix(d2, w_o, o, z, g_hgo + token[0, 0], tm)
    dz_hq, dz_hff, dz_hfb, dz_hi, dlb4 = _bwd_gla(z, lb4, do)
    dq, dk, dv = _bwd_attn(q, k, v, da, a32, tq_b)
    dz_mla, cqn, ckvn, dq0, dkv0, dg_qa, dg_kva, dg_qn, dg_kn = _bwd_mla_proj(
        z, dq, dk, dv, cosb, sina, sinb, g_qa, g_kva, wqb, wkvb, g_qn_p, g_kn_p, tm)

    gz = list(zip((Z_HQ, Z_HFF, Z_HFB, Z_HI, Z_HG, Z_CQ),
                  _mm_tn_many(h1, [dz_hq, dz_hff, dz_hfb, dz_hi, dz_hg, dz_mla], "dw_in", 1024)))
    blocks1 = _grads_out({"w_in": gz, "w_qb": [(0, _mm_tn(cqn, dq0, "dw_qb"))], "w_kvb": [(0, _mm_tn(ckvn, dkv0, "dw_kvb"))],
                          "lb_param": [(0, dlb4)]}, FIRST, "grads_out_first")
    rs1, token = _exchange_start([blocks1[n] for n in FIRST], empty(FIRST), "rs_first_start")

    result = {}

    def adam(names, lands, src, n_blocks, after=()):
        outs = _adam_shards(me_idx, [src[n] for n in names], lands, [w_all[n] for n in names], [m_all[n] for n in names],
                            [v_all[n] for n in names], n_blocks, "adamw_" + names[0], after)
        result.update(zip(names, outs))
        return outs[0][0]

    blocks2, lands2 = (dict(zip(SECOND, arrs)) for arrs in _exchange_wait(rs2, [token], "rs_second_wait"))
    by8 = tuple(n for n in SECOND if n != "w_down")
    done = [adam(by8, [lands2[n] for n in by8], blocks2, 8), adam(("w_down",), [lands2["w_down"]], blocks2, 2)]

    segments = [(dz_hq, 512, 0, Z_HQ // 512), (dz_hff, 512, 0, Z_HFF // 512), (dz_hfb, 512, 0, Z_HFB // 512),
                (dz_hi, 512, 0, Z_HI // 512), (dz_hg, 512, 0, Z_HG // 512), (dz_mla, 640, 0, Z_CQ // 640)]
    grad_x, dg_mix = _bwd_in(segments, wz, x, g_mix + token[0, 0], d2, tm)
    dgains = (dg_mix, dg_qa, dg_kva, dg_qn, dg_kn, dg_hgo, dg_ffn, dg_ple)

    vec = jnp.concatenate(list(dgains) + [loss_tile[0:1]], axis=1)
    parts = _all_gather([vec], [f32], "ag_gains")[0]
    outs, loss_row = _adam_gains(parts, [w_all[n] for n in SMALL], [m_all[n] for n in SMALL], [v_all[n] for n in SMALL])
    result.update(zip(SMALL, outs))

    blocks1, lands1 = _exchange_wait(rs1, [grad_x, loss_row, *done], "rs_first_wait")
    adam(FIRST, lands1, dict(zip(FIRST, blocks1)), 8)

    order = ("g_mix", "w_in", "g_qa", "g_kva", "w_qb", "w_kvb", "g_qn", "g_kn", "lb_param", "g_hgo", "w_o", "g_ffn",
             "w_gate", "w_up", "w_down", "g_ple", "w_ple_gate", "w_ple_proj")
    return (loss_row[0, 0], grad_x[None], *[result[n][k] for k in range(4) for n in order])
```

```python
import functools
import math

import jax
import jax.numpy as jnp
from jax import lax
from jax.experimental import pallas as pl
from jax.experimental.pallas import tpu as pltpu

f32 = jnp.float32
bf16 = jnp.bfloat16

N_DEV = 8
D_MODEL = 1024
MLA_HEADS = 4
QK_NOPE = 128
QK_ROPE = 64
QK_HEAD = QK_NOPE + QK_ROPE
QK_PAD = 256
V_HEAD = 128
Q_LORA = 256
KV_LORA = 256
HG_HEADS = 4
HG_DK = 128
CHUNK = 64
D_FF = 2816
PLE_DIM = 256
ROPE_THETA = 10000.0
EPS = 1e-6
ATTN_SCALE = QK_HEAD ** -0.5
LOG2_E = math.log2(math.e)
ATTN_SUB_ROWS = 256
IN_SIZES = (256, 256, 64, 512, 512, 512, 512, 512)
D_IN = sum(IN_SIZES)
Z_HQ, Z_HFF, Z_HFB, Z_HI, Z_HG, Z_CQ, Z_CKV, Z_KR, Z_W = 0, 512, 1024, 1536, 2048, 2560, 2816, 3072, 3200

ADAM_LR, ADAM_B1, ADAM_B2, ADAM_EPS, ADAM_WD, ADAM_STEP = 0.001, 0.9, 0.999, 1e-08, 0.01, 10

LANES = 128
BIG = {"w_in": (1024, 392), "w_qb": (256, 96), "w_kvb": (256, 128), "w_o": (128, 1024), "w_gate": (1024, 352),
       "w_up": (1024, 352), "w_down": (352, 1024), "w_ple_gate": (128, 1024), "w_ple_proj": (256, 128),
       "lb_param": (4, 64)}
ROW_BLOCKS = {("w_in", "w_qb", "w_kvb", "w_o", "w_gate", "w_up", "w_ple_gate", "w_ple_proj"): 8, ("w_down", "lb_param"): 2}
SMALL = {"g_mix": (0, 1024), "g_qa": (1024, 256), "g_kva": (1280, 256), "g_qn": (1536, 192), "g_kn": (1792, 192),
         "g_hgo": (2048, 512), "g_ffn": (2560, 1024), "g_ple": (3584, 1024)}
LOSS_OFF = 4608
GAIN_VEC = LOSS_OFF + LANES
Z_SEGMENTS = ((0, 256, Z_CQ), (256, 512, Z_CKV), (512, 576, Z_KR), (576, 1088, Z_HQ), (1088, 1600, Z_HFF),
              (1600, 2112, Z_HFB), (2112, 2624, Z_HI), (2624, 3136, Z_HG))

VMEM_LIMIT = 56 * 1024 * 1024
MESH = pl.DeviceIdType.MESH


def _cp(sem=None, vmem=None):
    return pltpu.CompilerParams(dimension_semantics=sem, vmem_limit_bytes=vmem)


def _const_spec(shape):
    nd = len(shape)
    return pl.BlockSpec(shape, lambda *_: (0,) * nd, pipeline_mode=pl.Buffered(1))


def _acc_spec(shape):
    nd = len(shape)
    return pl.BlockSpec(shape, lambda *_: (0,) * nd)


def _sigmoid(x):
    return jax.nn.sigmoid(x)


def _dot(a, b):
    return jnp.dot(a, b, preferred_element_type=f32)


def _dot_nt(a, b):
    return lax.dot_general(a, b, (((1,), (1,)), ((), ())), preferred_element_type=f32)


def _dot_tn(a, b):
    return lax.dot_general(a, b, (((0,), (0,)), ((), ())), preferred_element_type=f32)


def _rms_fwd(x, g, width):
    r = lax.rsqrt(jnp.sum(x * x, axis=-1, keepdims=True) * (1.0 / width) + EPS)
    return x * r * g, r


def _rms_bwd(dy, x, r, g, width):
    u = dy * g
    dx = r * u - x * (r * r * r) * (jnp.sum(u * x, axis=-1, keepdims=True) * (1.0 / width))
    return dx, dy * x * r


def _rope(b, c, sa, sb):
    return b * c + pltpu.roll(b, 32, 1) * sa + pltpu.roll(b, 96, 1) * sb


def _all_gather(shards, dtypes, name):
    n = len(shards)

    def body(*refs):
        in_refs, out_refs, stage = refs[:n], refs[n:2 * n], refs[2 * n:3 * n]
        send_sems, recv_sems, local_sems = refs[3 * n:]
        for w in range(n):
            stage[w][...] = in_refs[w][...].astype(stage[w].dtype)
        x, y, c = lax.axis_index("x"), lax.axis_index("y"), lax.axis_index("c")
        me, sibling = (x, y, c), (x, y, 1 - c)
        chips = [(1 - x, y), (x, 1 - y), (1 - x, 1 - y)]

        def slot(w, px, py, pc):
            return out_refs[w].at[4 * px + 2 * py + pc]

        def copy(w, k, block, to, src=None):
            return pltpu.make_async_remote_copy(
                src_ref=slot(w, *block) if src is None else src, dst_ref=slot(w, *block),
                send_sem=send_sems.at[w, k], recv_sem=recv_sems.at[w, k], device_id=to, device_id_type=MESH)

        first = []
        for j, chip in enumerate(chips):
            first += [copy(w, 1 + j, me, (*chip, c), src=stage[w]) for w in range(n)]
        first += [copy(w, 0, me, sibling, src=stage[w]) for w in range(n)]
        mine = [pltpu.make_async_copy(stage[w], slot(w, *me), local_sems.at[w]) for w in range(n)]
        for cp in first + mine:
            cp.start()
        passed = []
        for j, chip in enumerate(chips):
            for w in range(n):
                copy(w, 1 + j, (*chip, c), me).wait_recv()
                passed.append(copy(w, 4 + j, (*chip, c), sibling))
                passed[-1].start()
        for w in range(n):
            copy(w, 0, sibling, me).wait_recv()
        for j, chip in enumerate(chips):
            for w in range(n):
                copy(w, 4 + j, (*chip, 1 - c), me).wait_recv()
        for cp in first + passed:
            cp.wait_send()
        for cp in mine:
            cp.wait()

    return pl.pallas_call(
        body, name=name,
        out_shape=[jax.ShapeDtypeStruct((N_DEV, *s.shape), dt) for s, dt in zip(shards, dtypes)],
        in_specs=[pl.BlockSpec(memory_space=pltpu.VMEM)] * n,
        out_specs=[pl.BlockSpec(memory_space=pl.ANY)] * n,
        scratch_shapes=[pltpu.VMEM(s.shape, dt) for s, dt in zip(shards, dtypes)]
        + [pltpu.SemaphoreType.DMA((n, 7)), pltpu.SemaphoreType.DMA((n, 7)), pltpu.SemaphoreType.DMA((n,))],
        compiler_params=_cp(None, VMEM_LIMIT),
    )(*shards)


N_PEERS = N_DEV - 1
HBM_SPEC = pl.BlockSpec(memory_space=pltpu.HBM)
SEM_SPEC = pl.BlockSpec(memory_space=pltpu.SEMAPHORE)
DATAFLOW = pltpu.SideEffectType.DATAFLOW_SIDE_EFFECTING


def _me():
    return 4 * lax.axis_index("x") + 2 * lax.axis_index("y") + lax.axis_index("c")


def _peer(k):
    x, y, c = lax.axis_index("x"), lax.axis_index("y"), lax.axis_index("c")
    px = 1 - x if k & 4 else x
    py = 1 - y if k & 2 else y
    pc = 1 - c if k & 1 else c
    return (px, py, pc), 4 * px + 2 * py + pc


def _exchange_copies(src_refs, land_refs, send_sems, recv_sems, gather):
    cps = []
    me = _me()
    for k in range(1, N_DEV):
        peer, peer_idx = _peer(k)
        for w, land in enumerate(land_refs):
            src = land.at[me] if gather else src_refs[w].at[peer_idx]
            dst = land.at[me] if gather else land.at[k - 1]
            cps.append(pltpu.make_async_remote_copy(
                src_ref=src, dst_ref=dst, send_sem=send_sems.at[N_PEERS * w + k - 1], recv_sem=recv_sems.at[N_PEERS * w + k - 1],
                device_id=peer, device_id_type=MESH))
    return cps


def _exchange_start(srcs, lands, name):
    n_src, n = len(srcs), len(lands)

    def body(*refs):
        src_refs, land_refs = refs[:n_src], refs[n_src:n_src + n]
        send_sems, recv_sems = refs[n_src + n], refs[n_src + n + 1]
        token = refs[-1]
        for cp in _exchange_copies(src_refs, land_refs, send_sems, recv_sems, gather=not n_src):
            cp.start()
        token[...] = jnp.zeros_like(token)

    arrays = [pltpu.with_memory_space_constraint(a, pltpu.HBM) for a in (*srcs, *lands)]
    outs = pl.pallas_call(
        body, name=name,
        out_shape=(pltpu.SemaphoreType.DMA((n * N_PEERS,)), pltpu.SemaphoreType.DMA((n * N_PEERS,)),
                   *[pltpu.HBM(a.shape, a.dtype) for a in arrays], jax.ShapeDtypeStruct((8, LANES), f32)),
        in_specs=[HBM_SPEC] * len(arrays),
        out_specs=(SEM_SPEC, SEM_SPEC, *[HBM_SPEC] * len(arrays), pl.BlockSpec(memory_space=pltpu.VMEM)),
        input_output_aliases={i: 2 + i for i in range(len(arrays))},
        compiler_params=pltpu.CompilerParams(has_side_effects=DATAFLOW),
    )(*arrays)
    return (outs[0], outs[1], outs[2:2 + n_src], outs[2 + n_src:2 + n_src + n]), outs[-1]


def _exchange_wait(state, after, name):
    send_sems, recv_sems, srcs, lands = state
    n_src, n = len(srcs), len(lands)

    def body(*refs):
        src_refs, land_refs = refs[:n_src], refs[n_src:n_src + n]
        send_ref, recv_ref = refs[n_src + n], refs[n_src + n + 1]
        for cp in _exchange_copies(src_refs, land_refs, send_ref, recv_ref, gather=not n_src):
            cp.wait_send()
            cp.wait_recv()

    arrays = (*srcs, *lands)
    outs = pl.pallas_call(
        body, name=name,
        out_shape=tuple(pltpu.HBM(a.shape, a.dtype) for a in arrays),
        in_specs=[HBM_SPEC] * len(arrays) + [SEM_SPEC, SEM_SPEC] + [pl.BlockSpec(memory_space=pl.ANY)] * len(after),
        out_specs=tuple([HBM_SPEC] * len(arrays)),
        input_output_aliases={i: i for i in range(len(arrays))},
        compiler_params=pltpu.CompilerParams(has_side_effects=DATAFLOW),
    )(*arrays, send_sems, recv_sems, *after)
    return outs[:n_src], outs[n_src:]


def _cast_to_slot(shards, me_idx, after):
    n = len(shards)

    def body(i_ref, *refs):
        for w in range(n):
            refs[n + 1 + w][...] = refs[w][...].astype(bf16)

    return pl.pallas_call(
        body, name="cast_to_slot",
        grid_spec=pltpu.PrefetchScalarGridSpec(
            num_scalar_prefetch=1, grid=(1,),
            in_specs=[pl.BlockSpec(s.shape, lambda i, m: (0, 0)) for s in shards] + [pl.BlockSpec(memory_space=pl.ANY)],
            out_specs=[pl.BlockSpec((None, *s.shape), lambda i, m: (m[0], 0, 0)) for s in shards]),
        out_shape=[jax.ShapeDtypeStruct((N_DEV, *s.shape), bf16) for s in shards],
        compiler_params=_cp(("arbitrary",), VMEM_LIMIT),
    )(me_idx, *shards, after)


def _row_block(rows, n_blocks):
    return (rows // n_blocks, True) if rows % (16 * n_blocks) == 0 else (rows, False)


def _adam_math(w, g, m, v):
    m = ADAM_B1 * m + (1.0 - ADAM_B1) * g
    v = ADAM_B2 * v + (1.0 - ADAM_B2) * (g * g)
    m_hat = m / (1.0 - ADAM_B1 ** ADAM_STEP)
    v_hat = v / (1.0 - ADAM_B2 ** ADAM_STEP)
    delta = -ADAM_LR * (m_hat / (jnp.sqrt(v_hat) + ADAM_EPS) + ADAM_WD * w)
    return delta, m, v


def _adam_shards(me_idx, blocks, lands, ws, ms, vs, n_blocks, name, after=()):
    n = len(blocks)

    def body(i_ref, *refs):
        ins, outs = refs[:5 * n], refs[5 * n + len(after):]
        for w in range(n):
            g_ref, b_ref, w_ref, m_ref, v_ref = (ins[t * n + w] for t in range(5))
            g = g_ref[...].astype(f32)
            for k in range(N_PEERS):
                g = g + b_ref[k].astype(f32)
            if len(w_ref.shape) == 2:
                pieces = [(slice(None), g)]
            else:
                pieces = [(a, g[2 * a:2 * a + 2]) for a in range(2)]
            for at, gp in pieces:
                vals = (gp,) + _adam_math(w_ref[at], gp, m_ref[at], v_ref[at])
                for t, val in enumerate(vals):
                    outs[4 * w + t][at] = val

    specs = [[] for _ in range(5)]
    out_specs, out_shape = [], []
    for g, wt in zip(blocks, ws):
        rows, cols = g.shape[1:]
        rb, cut = _row_block(rows, n_blocks)
        specs[0].append(pl.BlockSpec((None, rb, cols), functools.partial(lambda i, s, cut: (s[0], i if cut else 0, 0), cut=cut)))
        specs[1].append(pl.BlockSpec((N_PEERS, rb, cols), functools.partial(lambda i, s, cut: (0, i if cut else 0, 0), cut=cut)))
        if wt.shape[0] == 1:
            shard = pl.BlockSpec((None, rb, cols), functools.partial(lambda i, s, cut: (0, i if cut else 0, 0), cut=cut))
        else:
            shard = pl.BlockSpec(wt.shape, functools.partial(lambda i, s, nd: (0,) * nd, nd=wt.ndim))
        for t in (2, 3, 4):
            specs[t].append(shard)
        out_specs += [shard] * 4
        out_shape += [jax.ShapeDtypeStruct(wt.shape, f32)] * 4
    outs = pl.pallas_call(
        body, name=name,
        grid_spec=pltpu.PrefetchScalarGridSpec(
            num_scalar_prefetch=1, grid=(n_blocks,), in_specs=sum(specs, []) + [pl.BlockSpec(memory_space=pl.ANY)] * len(after),
            out_specs=out_specs),
        out_shape=out_shape,
        compiler_params=_cp(("arbitrary",), VMEM_LIMIT),
    )(me_idx, *blocks, *lands, *ws, *ms, *vs, *after)
    return [outs[4 * w:4 * w + 4] for w in range(n)]


def _adam_gains(parts, ws, ms, vs):
    n = len(ws)

    def body(p_ref, *refs):
        ins, outs = refs[:3 * n], refs[3 * n:]
        g_all = p_ref[0]
        for k in range(1, N_DEV):
            g_all = g_all + p_ref[k]
        for w, (off, lanes) in enumerate(SMALL.values()):
            w_ref, m_ref, v_ref = ins[w], ins[n + w], ins[2 * n + w]
            if len(w_ref.shape) == 2:
                pieces = [(slice(None), off, lanes)]
            else:
                pieces = [((slice(None), h), off + LANES * h, LANES) for h in range(w_ref.shape[1])]
            for at, o, ln in pieces:
                g = g_all[:, o:o + ln]
                vals = (g,) + _adam_math(w_ref[at], g, m_ref[at], v_ref[at])
                for t, val in enumerate(vals):
                    outs[4 * w + t][at] = val
        outs[4 * n][...] = g_all[:, LOSS_OFF:LOSS_OFF + LANES]

    out_shape = sum([[jax.ShapeDtypeStruct(w.shape, f32)] * 4 for w in ws], []) + [jax.ShapeDtypeStruct((1, LANES), f32)]
    outs = pl.pallas_call(body, name="adamw_gains", out_shape=out_shape)(parts, *ws, *ms, *vs)
    return [outs[4 * w:4 * w + 4] for w in range(n)], outs[4 * n]


def _fwd_in(x, g_mix, wz, tm):
    s, d = x.shape

    def body(x_ref, g_ref, w_ref, h_ref, z_ref):
        h, _ = _rms_fwd(x_ref[...], g_ref[...], d)
        hb = h.astype(bf16)
        h_ref[...] = hb
        z_ref[...] = _dot(hb, w_ref[...])

    return pl.pallas_call(
        body, name="fwd_in", grid=(s // tm,),
        in_specs=[pl.BlockSpec((tm, d), lambda i: (i, 0)), _const_spec((1, d)), _const_spec((d, Z_W))],
        out_specs=[pl.BlockSpec((tm, d), lambda i: (i, 0)), pl.BlockSpec((tm, Z_W), lambda i: (i, 0))],
        out_shape=[jax.ShapeDtypeStruct((s, d), bf16), jax.ShapeDtypeStruct((s, Z_W), f32)],
        compiler_params=_cp(("parallel",), VMEM_LIMIT),
    )(x, g_mix, wz)


def _mla_qk_fwd(cq, ckv, kr, g_qa, g_kva, wqb, wkvb, g_qn, g_kn):
    cqn, rq = _rms_fwd(cq, g_qa, Q_LORA)
    ckvn, rkv = _rms_fwd(ckv, g_kva, KV_LORA)
    cqn_b, ckvn_b = cqn.astype(bf16), ckvn.astype(bf16)
    q0 = _dot(cqn_b, wqb)
    kv0 = _dot(ckvn_b, wkvb)
    return cqn_b, rq, ckvn_b, rkv, q0, kv0


def _fwd_mla_proj(z, cosb, sina, sinb, g_qa, g_kva, wqb, wkvb, g_qn, g_kn, tm):
    s = z.shape[0]
    hh = MLA_HEADS

    def body(cq_ref, ckv_ref, kr_ref, c_ref, sa_ref, sb_ref, gqa_ref, gkva_ref, wqb_ref, wkvb_ref, gqn_ref, gkn_ref,
             q_ref, k_ref, v_ref):
        _, _, _, _, q0, kv0 = _mla_qk_fwd(cq_ref[...], ckv_ref[...], kr_ref[...], gqa_ref[...], gkva_ref[...],
                                          wqb_ref[...], wkvb_ref[...], gqn_ref[...], gkn_ref[...])
        kr = kr_ref[...]
        c, sa, sb = c_ref[...], sa_ref[...], sb_ref[...]
        gqn, gkn = gqn_ref[...], gkn_ref[...]
        kr_sq = jnp.sum(kr * kr, axis=-1, keepdims=True)
        for h in range(hh):
            qh = q0[:, QK_PAD * h:QK_PAD * (h + 1)]
            qn, _ = _rms_fwd(qh, gqn, QK_HEAD)
            q_ref[h, :, 0:128] = qn[:, 0:128].astype(bf16)
            q_ref[h, :, 128:256] = _rope(qn[:, 128:256], c, sa, sb).astype(bf16)
            kn_ = kv0[:, 256 * h:256 * h + 128]
            rk = lax.rsqrt((jnp.sum(kn_ * kn_, axis=-1, keepdims=True) + kr_sq) * (1.0 / QK_HEAD) + EPS)
            k_ref[h, :, 0:128] = (kn_ * rk * gkn[:, 0:128]).astype(bf16)
            k_ref[h, :, 128:256] = _rope(kr * rk * gkn[:, 128:256], c, sa, sb).astype(bf16)
            v_ref[h] = kv0[:, 256 * h + 128:256 * h + 256].astype(bf16)

    row128 = pl.BlockSpec((tm, 128), lambda i: (i, 0))
    return pl.pallas_call(
        body, name="fwd_mla_proj", grid=(s // tm,),
        in_specs=[pl.BlockSpec((tm, 256), lambda i: (i, Z_CQ // 256)), pl.BlockSpec((tm, 256), lambda i: (i, Z_CKV // 256)),
                  pl.BlockSpec((tm, 128), lambda i: (i, Z_KR // 128)), row128, row128, row128,
                  _const_spec((1, 256)), _const_spec((1, 256)), _const_spec((256, 1024)), _const_spec((256, 1024)),
                  _const_spec((1, 256)), _const_spec((1, 256))],
        out_specs=[pl.BlockSpec((hh, tm, QK_PAD), lambda i: (0, i, 0)), pl.BlockSpec((hh, tm, QK_PAD), lambda i: (0, i, 0)),
                   pl.BlockSpec((hh, tm, V_HEAD), lambda i: (0, i, 0))],
        out_shape=[jax.ShapeDtypeStruct((hh, s, QK_PAD), bf16), jax.ShapeDtypeStruct((hh, s, QK_PAD), bf16),
                   jax.ShapeDtypeStruct((hh, s, V_HEAD), bf16)],
        compiler_params=_cp(("parallel",), VMEM_LIMIT),
    )(z, z, z, cosb, sina, sinb, g_qa, g_kva, wqb, wkvb, g_qn, g_kn)


def _fwd_attn(q, k, v, tq):
    hh, s, _ = q.shape

    n_sub = max(1, tq // ATTN_SUB_ROWS)

    def body(q_ref, k_ref, v_ref, o_ref, o32_ref):
        for t in range(n_sub):
            rows = slice(t * (tq // n_sub), (t + 1) * (tq // n_sub))
            sc = _dot_nt(q_ref[rows, :], k_ref[...])
            p = jnp.exp2((sc - jnp.max(sc, axis=-1, keepdims=True)) * (ATTN_SCALE * LOG2_E))
            l = jnp.sum(p, axis=-1, keepdims=True)
            o = _dot(p.astype(bf16), v_ref[...]) * (1.0 / l)
            o_ref[rows, :] = o.astype(bf16)
            o32_ref[rows, :] = o

    out = pl.BlockSpec((tq, V_HEAD), lambda h, i: (i, h))
    return pl.pallas_call(
        body, name="fwd_attn", grid=(hh, s // tq),
        in_specs=[pl.BlockSpec((None, tq, QK_PAD), lambda h, i: (h, i, 0)),
                  pl.BlockSpec((None, s, QK_PAD), lambda h, i: (h, 0, 0)),
                  pl.BlockSpec((None, s, V_HEAD), lambda h, i: (h, 0, 0))],
        out_specs=[out, out],
        out_shape=[jax.ShapeDtypeStruct((s, hh * V_HEAD), bf16), jax.ShapeDtypeStruct((s, hh * V_HEAD), f32)],
        compiler_params=_cp(("parallel", "parallel"), VMEM_LIMIT),
    )(q, k, v)


def _split3(x):
    hi = x.astype(bf16)
    r1 = x - hi.astype(f32)
    mid = r1.astype(bf16)
    lo = (r1 - mid.astype(f32)).astype(bf16)
    return jnp.concatenate([hi, mid, lo], axis=-1)


def _tri_sum(tri, x):
    y = _dot(tri, _split3(x))
    return y[:, 0:128] + y[:, 128:256] + y[:, 256:384]


GLA_GROUP = 4
GLA_ROWS = GLA_GROUP * CHUNK
GLA_HEADS_PER_STEP = 2


def _gla_masks(rev):
    row = lax.broadcasted_iota(jnp.int32, (GLA_ROWS, GLA_ROWS), 0)
    col = lax.broadcasted_iota(jnp.int32, (GLA_ROWS, GLA_ROWS), 1)
    shift = CHUNK.bit_length() - 1
    same = (jnp.right_shift(row, shift) == jnp.right_shift(col, shift)).astype(f32)
    lower, upper = (row >= col).astype(f32) * same, (row <= col).astype(f32) * same
    keep, keep_t = (upper, lower) if rev else (lower, upper)
    chunk_of = jnp.right_shift(lax.broadcasted_iota(jnp.int32, (GLA_ROWS, 1), 0), shift)
    return keep, keep.astype(bf16), keep_t.astype(bf16), [(chunk_of == c).astype(f32) for c in range(GLA_GROUP)]


def _gla_gates(hq, hf, lower):
    sg = _sigmoid(hf)
    f = lower + (1.0 - lower) * sg
    return hq * _sigmoid(hq), 1.0 - f, jnp.log(f), f, sg


def _gla_last_mid(b, rev):
    b3 = b.reshape(GLA_GROUP, CHUNK, 128)
    last, mid = (0, CHUNK // 2) if rev else (CHUNK - 1, CHUNK // 2 - 1)
    return b3[:, last:last + 1, :], b3[:, mid:mid + 1, :]


def _gla_per_row(per_chunk):
    return jnp.broadcast_to(per_chunk, (GLA_GROUP, CHUNK, 128)).reshape(GLA_ROWS, 128)


def _gla_block_diag(x, row_masks):
    return jnp.concatenate([(x * m).astype(bf16) for m in row_masks], axis=-1)


def _gla_diag(y):
    return jnp.concatenate([y[CHUNK * c:CHUNK * (c + 1), 128 * c:128 * (c + 1)] for c in range(GLA_GROUP)], axis=0)


def _gla_rows(n, n_groups, rev):
    ne = n_groups - 1 - n if rev else n
    return pl.ds(pl.multiple_of(ne * GLA_ROWS, GLA_ROWS), GLA_ROWS), ne * GLA_GROUP


def _gla_scan_order(rev):
    return tuple(reversed(range(GLA_GROUP))) if rev else tuple(range(GLA_GROUP))


def _fwd_gla(z, lb4):
    s = z.shape[0]
    n_groups = s // GLA_ROWS
    assert n_groups % 2 == 0
    hp = GLA_HEADS_PER_STEP
    chains = [(hh, rev) for hh in range(hp) for rev in (False, True)]

    def body(hq_ref, hff_ref, hfb_ref, hi_ref, lb_ref, o_ref, b_ref, states_ref, st_ref, stage_ref, sems):
        st_ref[...] = jnp.zeros_like(st_ref)
        masks = {rev: _gla_masks(rev) for rev in (False, True)}
        lowers = [_sigmoid(lb_ref[int(rev):int(rev) + 1, 128 * hh:128 * (hh + 1)]
                           - lb_ref[2 + int(rev):3 + int(rev), 128 * hh:128 * (hh + 1)]) for hh, rev in chains]

        def states_out(slot, ci, chunk0):
            hh, rev = chains[ci]
            return pltpu.make_async_copy(stage_ref.at[slot, ci],
                                         states_ref.at[pl.program_id(0) * hp + hh, int(rev), pl.ds(chunk0, GLA_GROUP)],
                                         sems.at[slot, ci])

        def make_step(first):
            def step(n, carry):
                slot = n % 2

                @pl.when(n >= 2)
                def _():
                    for ci in range(len(chains)):
                        states_out(slot, ci, 0).wait()

                for ci, (hh, rev) in enumerate(chains):
                    cols = slice(128 * hh, 128 * (hh + 1))
                    rows, chunk0 = _gla_rows(n, n_groups, rev)
                    maskf, tri, _, row_masks = masks[rev]
                    hf_ref = hfb_ref if rev else hff_ref
                    q, k, logf, _, _ = _gla_gates(hq_ref[rows, cols], hf_ref[rows, cols], lowers[ci])
                    vb = hi_ref[rows, cols].astype(bf16)
                    b = _tri_sum(tri, logf)
                    b_ref[int(rev), rows, cols] = b
                    b_last3, b_mid3 = _gla_last_mid(b, rev)
                    b_last, b_mid = _gla_per_row(b_last3), _gla_per_row(b_mid3)
                    qi = (q * jnp.exp(b - b_mid)).astype(bf16)
                    ki = (k * jnp.exp(b_mid - b)).astype(bf16)
                    a = (_dot_nt(qi, ki) * maskf).astype(bf16)
                    kv = _dot_tn(vb, _gla_block_diag(k * jnp.exp(b_last - b), row_masks))
                    decay3 = jnp.exp(b_last3)
                    st = st_ref[ci]
                    before = [None] * GLA_GROUP
                    for c in _gla_scan_order(rev):
                        stage_ref[slot, ci, c] = st
                        before[c] = st.astype(bf16)
                        st = st * decay3[c] + kv[:, 128 * c:128 * (c + 1)]
                    st_ref[ci] = st
                    states_out(slot, ci, chunk0).start()
                    inter = _dot_nt((q * jnp.exp(b)).astype(bf16), jnp.concatenate(before, axis=0))
                    o = _dot(a, vb) + _gla_diag(inter)
                    if first:
                        o_ref[rows, cols] = o
                    else:
                        o_ref[rows, cols] += o
                return carry
            return step

        lax.fori_loop(0, n_groups // 2, make_step(True), 0)
        lax.fori_loop(n_groups // 2, n_groups, make_step(False), 0)
        for slot in range(2):
            for ci in range(len(chains)):
                states_out(slot, ci, 0).wait()

    w = 128 * hp
    col = lambda base: pl.BlockSpec((s, w), lambda h: (0, base // w + h), pipeline_mode=pl.Buffered(1))
    return pl.pallas_call(
        body, name="fwd_gla", grid=(HG_HEADS // hp,),
        in_specs=[col(Z_HQ), col(Z_HFF), col(Z_HFB), col(Z_HI), pl.BlockSpec((4, w), lambda h: (0, h))],
        out_specs=[pl.BlockSpec((s, w), lambda h: (0, h)), pl.BlockSpec((2, s, w), lambda h: (0, 0, h)),
                   pl.BlockSpec(memory_space=pl.ANY)],
        out_shape=[jax.ShapeDtypeStruct((s, HG_HEADS * 128), f32), jax.ShapeDtypeStruct((2, s, HG_HEADS * 128), f32),
                   jax.ShapeDtypeStruct((HG_HEADS, 2, s // CHUNK, 128, 128), f32)],
        scratch_shapes=[pltpu.VMEM((len(chains), 128, 128), f32), pltpu.VMEM((2, len(chains), GLA_GROUP, 128, 128), f32),
                        pltpu.SemaphoreType.DMA((2, len(chains)))],
        compiler_params=_cp(("parallel",), VMEM_LIMIT),
    )(z, z, z, z, lb4)


def _hg_out(o, hg, g_hgo):
    outs, ons, rs = [], [], []
    for h in range(HG_HEADS):
        oh = o[:, 128 * h:128 * (h + 1)]
        on, r = _rms_fwd(oh, g_hgo[:, 128 * h:128 * (h + 1)], 128)
        ons.append(on)
        rs.append(r)
    on = jnp.concatenate(ons, axis=-1)
    sg = _sigmoid(hg)
    return on * (hg * sg), on, rs, sg


def _fwd_mix(a, o, z, g_hgo, x, w_o, tm):
    s, d = x.shape

    def body(a_ref, o_ref, hg_ref, g_ref, x_ref, w_ref, x2_ref, cat_ref):
        r, _, _, _ = _hg_out(o_ref[...], hg_ref[...], g_ref[...])
        cat = jnp.concatenate([a_ref[...], r.astype(bf16)], axis=-1)
        cat_ref[...] = cat
        x2_ref[...] = x_ref[...] + _dot(cat, w_ref[...])

    row512 = pl.BlockSpec((tm, 512), lambda i: (i, 0))
    rowd = pl.BlockSpec((tm, d), lambda i: (i, 0))
    return pl.pallas_call(
        body, name="fwd_mix", grid=(s // tm,),
        in_specs=[row512, row512, pl.BlockSpec((tm, 512), lambda i: (i, Z_HG // 512)), _const_spec((1, 512)), rowd,
                  _const_spec((d, d))],
        out_specs=[rowd, rowd],
        out_shape=[jax.ShapeDtypeStruct((s, d), f32), jax.ShapeDtypeStruct((s, d), bf16)],
        compiler_params=_cp(("parallel",), VMEM_LIMIT),
    )(a, o, z, g_hgo, x, w_o)


def _fwd_ffn(x2, g_ffn, w_gate, w_up, w_down, tm):
    s, d = x2.shape

    def body(x_ref, g_ref, wg_ref, wu_ref, wd_ref, x3_ref, gp_ref, up_ref):
        x = x_ref[...]
        h, _ = _rms_fwd(x, g_ref[...], d)
        hb = h.astype(bf16)
        gp = _dot(hb, wg_ref[...])
        up = _dot(hb, wu_ref[...])
        gp_ref[...] = gp.astype(bf16)
        up_ref[...] = up.astype(bf16)
        act = (gp * _sigmoid(gp) * up).astype(bf16)
        x3_ref[...] = x + _dot(act, wd_ref[...])

    rowd = pl.BlockSpec((tm, d), lambda i: (i, 0))
    rowf = pl.BlockSpec((tm, D_FF), lambda i: (i, 0))
    return pl.pallas_call(
        body, name="fwd_ffn", grid=(s // tm,),
        in_specs=[rowd, _const_spec((1, d)), _const_spec((d, D_FF)), _const_spec((d, D_FF)), _const_spec((D_FF, d))],
        out_specs=[rowd, rowf, rowf],
        out_shape=[jax.ShapeDtypeStruct((s, d), f32), jax.ShapeDtypeStruct((s, D_FF), bf16),
                   jax.ShapeDtypeStruct((s, D_FF), bf16)],
        compiler_params=_cp(("parallel",), VMEM_LIMIT),
    )(x2, g_ffn, w_gate, w_up, w_down)


def _ple_loss_fwd_bwd(x3, g_ple, w_pg, p, w_pp, target, tm):
    s, d = x3.shape

    def body(x_ref, g_ref, wg_ref, p_ref, wp_ref, t_ref, dx_ref, h_ref, dpre_ref, dpp_ref, dg_ref, loss_ref):
        @pl.when(pl.program_id(0) == 0)
        def _():
            dg_ref[...] = jnp.zeros_like(dg_ref)
            loss_ref[...] = jnp.zeros_like(loss_ref)

        x = x_ref[...]
        g = g_ref[...]
        h, r = _rms_fwd(x, g, d)
        hb = h.astype(bf16)
        gate = _sigmoid(_dot(hb, wg_ref[...]))
        pp = _dot(p_ref[...].astype(bf16), wp_ref[...])
        e = x + gate * pp - t_ref[...]
        loss_ref[...] += 0.5 * jnp.sum(e * e) * (1.0 / d)
        dy = e * (1.0 / d)
        dpre = (dy * pp * gate * (1.0 - gate)).astype(bf16)
        dx, dgx = _rms_bwd(_dot_nt(dpre, wg_ref[...]), x, r, g, d)
        dx_ref[...] = dy + dx
        dg_ref[...] += jnp.sum(dgx, axis=0, keepdims=True)
        h_ref[...] = hb
        dpre_ref[...] = dpre
        dpp_ref[...] = (dy * gate).astype(bf16)

    rowd = pl.BlockSpec((tm, d), lambda i: (i, 0))
    return pl.pallas_call(
        body, name="ple_loss_fwd_bwd", grid=(s // tm,),
        in_specs=[rowd, _const_spec((1, d)), _const_spec((d, d)), pl.BlockSpec((tm, PLE_DIM), lambda i: (i, 0)),
                  _const_spec((PLE_DIM, d)), rowd],
        out_specs=[rowd, rowd, rowd, rowd, _acc_spec((1, d)), _acc_spec((8, 128))],
        out_shape=[jax.ShapeDtypeStruct((s, d), f32), jax.ShapeDtypeStruct((s, d), bf16), jax.ShapeDtypeStruct((s, d), bf16),
                   jax.ShapeDtypeStruct((s, d), bf16), jax.ShapeDtypeStruct((1, d), f32), jax.ShapeDtypeStruct((8, 128), f32)],
        compiler_params=_cp(("arbitrary",), VMEM_LIMIT),
    )(x3, g_ple, w_pg, p, w_pp, target)


def _bwd_ffn_hidden(d3, x2, gp, up, g_ffn, w_down, tm, tf):
    s, d = x2.shape

    def body(d3_ref, x_ref, gp_ref, up_ref, g_ref, wd_ref, h_ref, act_ref, dgp_ref, dup_ref, d3b_ref):
        @pl.when(pl.program_id(1) == 0)
        def _():
            h, _ = _rms_fwd(x_ref[...], g_ref[...], d)
            h_ref[...] = h.astype(bf16)
            d3b_ref[...] = d3_ref[...].astype(bf16)

        gp, up = gp_ref[...].astype(f32), up_ref[...].astype(f32)
        sg = _sigmoid(gp)
        silu = gp * sg
        act_ref[...] = (silu * up).astype(bf16)
        dact = _dot_nt(d3b_ref[...], wd_ref[...])
        dgp_ref[...] = (dact * up * (sg * (1.0 + gp * (1.0 - sg)))).astype(bf16)
        dup_ref[...] = (dact * silu).astype(bf16)

    rowd = pl.BlockSpec((tm, d), lambda i, f: (i, 0))
    rowf = pl.BlockSpec((tm, tf), lambda i, f: (i, f))
    return pl.pallas_call(
        body, name="bwd_ffn_hidden", grid=(s // tm, D_FF // tf),
        in_specs=[rowd, rowd, rowf, rowf, _const_spec((1, d)), pl.BlockSpec((tf, d), lambda i, f: (f, 0))],
        out_specs=[rowd, rowf, rowf, rowf],
        out_shape=[jax.ShapeDtypeStruct((s, d), bf16)] + [jax.ShapeDtypeStruct((s, D_FF), bf16)] * 3,
        scratch_shapes=[pltpu.VMEM((tm, d), bf16)],
        compiler_params=_cp(("parallel", "arbitrary"), VMEM_LIMIT),
    )(d3, x2, gp, up, g_ffn, w_down)


def _bwd_ffn_in(d3, x2, dgp, dup, g_ffn, w_gate, w_up, tm):
    s, d = x2.shape

    def body(d3_ref, x_ref, dgp_ref, dup_ref, g_ref, wg_ref, wu_ref, d2_ref, dg_ref):
        @pl.when(pl.program_id(0) == 0)
        def _():
            dg_ref[...] = jnp.zeros_like(dg_ref)

        x, g = x_ref[...], g_ref[...]
        dh = _dot_nt(dgp_ref[...], wg_ref[...]) + _dot_nt(dup_ref[...], wu_ref[...])
        r = lax.rsqrt(jnp.sum(x * x, axis=-1, keepdims=True) * (1.0 / d) + EPS)
        dx, dgx = _rms_bwd(dh, x, r, g, d)
        d2_ref[...] = d3_ref[...] + dx
        dg_ref[...] += jnp.sum(dgx, axis=0, keepdims=True)

    rowd = pl.BlockSpec((tm, d), lambda i: (i, 0))
    rowf = pl.BlockSpec((tm, D_FF), lambda i: (i, 0))
    return pl.pallas_call(
        body, name="bwd_ffn_in", grid=(s // tm,),
        in_specs=[rowd, rowd, rowf, rowf, _const_spec((1, d)), _const_spec((d, D_FF)), _const_spec((d, D_FF))],
        out_specs=[rowd, _acc_spec((1, d))],
        out_shape=[jax.ShapeDtypeStruct((s, d), f32), jax.ShapeDtypeStruct((1, d), f32)],
        compiler_params=_cp(("arbitrary",), VMEM_LIMIT),
    )(d3, x2, dgp, dup, g_ffn, w_gate, w_up)


def _bwd_mix(d2, w_o, o, z, g_hgo, tm):
    s, d = d2.shape

    def body(d2_ref, w_ref, o_ref, hg_ref, g_ref, da_ref, do_ref, dhg_ref, dg_ref):
        @pl.when(pl.program_id(0) == 0)
        def _():
            dg_ref[...] = jnp.zeros_like(dg_ref)

        dcat = _dot_nt(d2_ref[...].astype(bf16), w_ref[...])
        da_ref[...] = dcat[:, 0:512].astype(bf16)
        dr = dcat[:, 512:1024]
        o, hg, g = o_ref[...], hg_ref[...], g_ref[...]
        _, on, rs, sg = _hg_out(o, hg, g)
        dhg_ref[...] = (dr * on * (sg * (1.0 + hg * (1.0 - sg)))).astype(bf16)
        don = dr * (hg * sg)
        dgs = []
        for h in range(HG_HEADS):
            cols = slice(128 * h, 128 * (h + 1))
            dx, dgx = _rms_bwd(don[:, cols], o[:, cols], rs[h], g[:, cols], 128)
            do_ref[:, cols] = dx
            dgs.append(jnp.sum(dgx, axis=0, keepdims=True))
        dg_ref[...] += jnp.concatenate(dgs, axis=-1)

    row512 = pl.BlockSpec((tm, 512), lambda i: (i, 0))
    return pl.pallas_call(
        body, name="bwd_mix", grid=(s // tm,),
        in_specs=[pl.BlockSpec((tm, d), lambda i: (i, 0)), _const_spec((d, d)), row512,
                  pl.BlockSpec((tm, 512), lambda i: (i, Z_HG // 512)), _const_spec((1, 512))],
        out_specs=[row512, row512, row512, _acc_spec((1, 512))],
        out_shape=[jax.ShapeDtypeStruct((s, 512), bf16), jax.ShapeDtypeStruct((s, 512), f32), jax.ShapeDtypeStruct((s, 512), bf16),
                   jax.ShapeDtypeStruct((1, 512), f32)],
        compiler_params=_cp(("arbitrary",), VMEM_LIMIT),
    )(d2, w_o, o, z, g_hgo)


def _bwd_gla(z, lb4, do, b_fwd, states):
    s = z.shape[0]
    n_chunks = s // CHUNK
    n_groups = s // GLA_ROWS
    assert n_groups % 2 == 0

    def body(hq_ref, hff_ref, hfb_ref, hi_ref, lb_ref, do_ref, b_all, st_all, dhq_ref, dhff_ref, dhfb_ref, dhi_ref, dlb_ref,
             dst_ref, dq_acc, dv_acc, dlow_ref):
        dirs = (False, True)
        masks = [_gla_masks(rev) for rev in dirs]
        lowers = [_sigmoid(lb_ref[int(rev):int(rev) + 1, :] - lb_ref[2 + int(rev):3 + int(rev), :]) for rev in dirs]
        hf_refs, dhf_refs = (hff_ref, hfb_ref), (dhff_ref, dhfb_ref)

        dst_ref[...] = jnp.zeros_like(dst_ref)
        dlow_ref[...] = jnp.zeros_like(dlow_ref)

        def make_bwd_step(first):
            def bwd_step(j, carry):
                n = n_groups - 1 - j
                for d, rev in enumerate(dirs):
                    maskf, _, tri_t, row_masks = masks[d]
                    lower = lowers[d]
                    rows, chunk0 = _gla_rows(n, n_groups, rev)
                    hq, hf = hq_ref[rows, :], hf_refs[d][rows, :]
                    q, k, _, f, sg = _gla_gates(hq, hf, lower)
                    v = hi_ref[rows, :]
                    dout = do_ref[rows, :]
                    b = b_all[d, rows, :]
                    b_last3, b_mid3 = _gla_last_mid(b, rev)
                    b_last, b_mid = _gla_per_row(b_last3), _gla_per_row(b_mid3)
                    e1, e2, e3, e4 = jnp.exp(b - b_mid), jnp.exp(b_mid - b), jnp.exp(b_last - b), jnp.exp(b)
                    decay3 = jnp.exp(b_last3)
                    qi, ki, kt, qt = q * e1, k * e2, k * e3, q * e4
                    qib, kib, ktb = qi.astype(bf16), ki.astype(bf16), kt.astype(bf16)
                    vb, dob = v.astype(bf16), dout.astype(bf16)
                    a = (_dot_nt(qib, kib) * maskf).astype(bf16)
                    da = (_dot_nt(dob, vb) * maskf).astype(bf16)
                    dqi = _dot(da, kib)
                    dki = _dot_tn(da, qib)
                    into_state = _dot_tn(dob, _gla_block_diag(qt, row_masks))
                    dst = dst_ref[d]
                    sts, dsts, ddecay = [None] * GLA_GROUP, [None] * GLA_GROUP, [None] * GLA_GROUP
                    for c in reversed(_gla_scan_order(rev)):
                        sts[c] = st_all[d, chunk0 + c]
                        dsts[c] = dst.astype(bf16)
                        ddecay[c] = jnp.sum(dst * sts[c], axis=0, keepdims=True)[None]
                        dst = dst * decay3[c] + into_state[:, 128 * c:128 * (c + 1)]
                    dst_ref[d] = dst
                    dv = _dot_tn(a, dob) + _gla_diag(_dot_nt(ktb, jnp.concatenate(dsts, axis=0)))
                    dqt = _gla_diag(_dot(dob, jnp.concatenate([x.astype(bf16) for x in sts], axis=-1)))
                    dkt = _gla_diag(_dot(vb, jnp.concatenate(dsts, axis=-1)))
                    dq = dqi * e1 + dqt * e4
                    dk = dki * e2 + dkt * e3
                    db = dqi * qi - dki * ki + dqt * qt - dkt * kt
                    dlast3 = (jnp.sum((dkt * kt).reshape(GLA_GROUP, CHUNK, 128), axis=1, keepdims=True)
                              + jnp.concatenate(ddecay, axis=0) * decay3)
                    dlogf = _tri_sum(tri_t, db) + _gla_per_row(dlast3)
                    df = dlogf / f - dk
                    dhf_refs[d][rows, :] = (df * (1.0 - lower) * sg * (1.0 - sg)).astype(bf16)
                    dlow_ref[d:d + 1, :] += jnp.sum(df * (1.0 - sg), axis=0, keepdims=True)
                    sq = _sigmoid(hq)
                    dhq = dq * (sq * (1.0 + hq * (1.0 - sq)))
                    if first:
                        dq_acc[rows, :] = dhq
                        dv_acc[rows, :] = dv
                    else:
                        dhq_ref[rows, :] = (dq_acc[rows, :] + dhq).astype(bf16)
                        dhi_ref[rows, :] = (dv_acc[rows, :] + dv).astype(bf16)
                return carry
            return bwd_step

        lax.fori_loop(0, n_groups // 2, make_bwd_step(True), 0)
        lax.fori_loop(n_groups // 2, n_groups, make_bwd_step(False), 0)

        for d in range(2):
            dl = dlow_ref[d:d + 1, :] * lowers[d] * (1.0 - lowers[d])
            dlb_ref[d:d + 1, :] = dl
            dlb_ref[2 + d:3 + d, :] = -dl

    col = lambda base: pl.BlockSpec((s, 128), lambda h: (0, base // 128 + h))
    return pl.pallas_call(
        body, name="bwd_gla", grid=(HG_HEADS,),
        in_specs=[col(Z_HQ), col(Z_HFF), col(Z_HFB), col(Z_HI), pl.BlockSpec((4, 128), lambda h: (0, h)), col(0),
                  pl.BlockSpec((2, s, 128), lambda h: (0, 0, h)),
                  pl.BlockSpec((None, 2, n_chunks, 128, 128), lambda h: (h, 0, 0, 0, 0), pipeline_mode=pl.Buffered(1))],
        out_specs=[col(0), col(0), col(0), col(0), pl.BlockSpec((4, 128), lambda h: (0, h))],
        out_shape=[jax.ShapeDtypeStruct((s, 512), bf16)] * 4 + [jax.ShapeDtypeStruct((4, 512), f32)],
        scratch_shapes=[pltpu.VMEM((2, 128, 128), f32), pltpu.VMEM((s, 128), f32), pltpu.VMEM((s, 128), f32),
                        pltpu.VMEM((2, 128), f32)],
        compiler_params=_cp(("parallel",), VMEM_LIMIT),
    )(z, z, z, z, lb4, do, b_fwd, states)


def _bwd_attn(q, k, v, da, a32, tq):
    hh, s, _ = q.shape

    n_sub = max(1, tq // ATTN_SUB_ROWS)

    def body(q_ref, k_ref, v_ref, do_ref, o_ref, dq_ref, dk_ref, dv_ref, p_all, ds_all, dol_ref, dkt_ref, dvt_ref):
        @pl.when(pl.program_id(1) == 0)
        def _():
            dkt_ref[...] = jnp.zeros_like(dkt_ref)
            dvt_ref[...] = jnp.zeros_like(dvt_ref)

        kb, vb = k_ref[...], v_ref[...]
        for t in range(n_sub):
            rows = slice(t * (tq // n_sub), (t + 1) * (tq // n_sub))
            sc = _dot_nt(q_ref[rows, :], kb)
            p = jnp.exp2((sc - jnp.max(sc, axis=-1, keepdims=True)) * (ATTN_SCALE * LOG2_E))
            inv_l = 1.0 / jnp.sum(p, axis=-1, keepdims=True)
            pb = p.astype(bf16)
            dob = do_ref[rows, :]
            dof = dob.astype(f32)
            delta = jnp.sum(dof * o_ref[rows, :], axis=-1, keepdims=True)
            ds = pb * ((_dot_nt(dob, vb) - delta) * inv_l).astype(bf16)
            dq_ref[rows, :] = _dot(ds, kb) * ATTN_SCALE
            p_all[rows, :] = pb
            ds_all[rows, :] = ds
            dol_ref[rows, :] = (dof * inv_l).astype(bf16)
        dkt_ref[...] += _dot_tn(q_ref[...], ds_all[...])
        dvt_ref[...] += _dot_tn(dol_ref[...], p_all[...])

        @pl.when(pl.program_id(1) == s // tq - 1)
        def _():
            dk_ref[...] = dkt_ref[...].T * ATTN_SCALE
            dv_ref[...] = dvt_ref[...].T

    return pl.pallas_call(
        body, name="bwd_attn", grid=(hh, s // tq),
        in_specs=[pl.BlockSpec((None, tq, QK_PAD), lambda h, i: (h, i, 0)),
                  pl.BlockSpec((None, s, QK_PAD), lambda h, i: (h, 0, 0)),
                  pl.BlockSpec((None, s, V_HEAD), lambda h, i: (h, 0, 0)),
                  pl.BlockSpec((tq, V_HEAD), lambda h, i: (i, h)), pl.BlockSpec((tq, V_HEAD), lambda h, i: (i, h))],
        out_specs=[pl.BlockSpec((None, tq, QK_PAD), lambda h, i: (h, i, 0)),
                   pl.BlockSpec((None, s, QK_PAD), lambda h, i: (h, 0, 0)),
                   pl.BlockSpec((None, s, V_HEAD), lambda h, i: (h, 0, 0))],
        out_shape=[jax.ShapeDtypeStruct((hh, s, QK_PAD), f32), jax.ShapeDtypeStruct((hh, s, QK_PAD), f32),
                   jax.ShapeDtypeStruct((hh, s, V_HEAD), f32)],
        scratch_shapes=[pltpu.VMEM((tq, s), bf16), pltpu.VMEM((tq, s), bf16), pltpu.VMEM((tq, V_HEAD), bf16),
                        pltpu.VMEM((QK_PAD, s), f32), pltpu.VMEM((V_HEAD, s), f32)],
        compiler_params=_cp(("parallel", "arbitrary"), VMEM_LIMIT),
    )(q, k, v, da, a32)


def _bwd_mla_proj(z, dq, dk, dv, cosb, sina, sinb, g_qa, g_kva, wqb, wkvb, g_qn, g_kn, tm):
    s = z.shape[0]
    hh = MLA_HEADS

    def body(cq_ref, ckv_ref, kr_ref, dq_ref, dk_ref, dv_ref, c_ref, sa_ref, sb_ref, gqa_ref, gkva_ref, wqb_ref, wkvb_ref,
             gqn_ref, gkn_ref, dz_ref, cqn_ref, ckvn_ref, dq0_ref, dkv0_ref, dgqa_ref, dgkva_ref, dgqn_ref, dgkn_ref):
        @pl.when(pl.program_id(0) == 0)
        def _():
            for r in (dgqa_ref, dgkva_ref, dgqn_ref, dgkn_ref):
                r[...] = jnp.zeros_like(r)

        cq, ckv, kr = cq_ref[...], ckv_ref[...], kr_ref[...]
        gqa, gkva, gqn, gkn = gqa_ref[...], gkva_ref[...], gqn_ref[...], gkn_ref[...]
        cqn_b, rq, ckvn_b, rkv, q0, kv0 = _mla_qk_fwd(cq, ckv, kr, gqa, gkva, wqb_ref[...], wkvb_ref[...], gqn, gkn)
        cqn_ref[...] = cqn_b
        ckvn_ref[...] = ckvn_b
        c, sa, sb = c_ref[...], -sa_ref[...], -sb_ref[...]
        kr_sq = jnp.sum(kr * kr, axis=-1, keepdims=True)
        dkr = jnp.zeros_like(kr)
        dgqn = jnp.zeros((1, QK_PAD), f32)
        dgkn = jnp.zeros((1, QK_PAD), f32)
        for h in range(hh):
            qh = q0[:, QK_PAD * h:QK_PAD * (h + 1)]
            rh = lax.rsqrt(jnp.sum(qh * qh, axis=-1, keepdims=True) * (1.0 / QK_HEAD) + EPS)
            dqh = dq_ref[h]
            dqn = jnp.concatenate([dqh[:, 0:128], _rope(dqh[:, 128:256], c, sa, sb)], axis=-1)
            dq0h, dgx = _rms_bwd(dqn, qh, rh, gqn, QK_HEAD)
            dq0_ref[:, QK_PAD * h:QK_PAD * (h + 1)] = dq0h.astype(bf16)
            dgqn = dgqn + jnp.sum(dgx, axis=0, keepdims=True)

            kn_ = kv0[:, 256 * h:256 * h + 128]
            k0 = jnp.concatenate([kn_, kr], axis=-1)
            rk = lax.rsqrt((jnp.sum(kn_ * kn_, axis=-1, keepdims=True) + kr_sq) * (1.0 / QK_HEAD) + EPS)
            dkh = dk_ref[h]
            dkn = jnp.concatenate([dkh[:, 0:128], _rope(dkh[:, 128:256], c, sa, sb)], axis=-1)
            dk0, dgx = _rms_bwd(dkn, k0, rk, gkn, QK_HEAD)
            dgkn = dgkn + jnp.sum(dgx, axis=0, keepdims=True)
            dkv0_ref[:, 256 * h:256 * h + 128] = dk0[:, 0:128].astype(bf16)
            dkv0_ref[:, 256 * h + 128:256 * h + 256] = dv_ref[h].astype(bf16)
            dkr = dkr + dk0[:, 128:256]
        dgqn_ref[...] += dgqn
        dgkn_ref[...] += dgkn
        dcq, dgx = _rms_bwd(_dot_nt(dq0_ref[...], wqb_ref[...]), cq, rq, gqa, Q_LORA)
        dgqa_ref[...] += jnp.sum(dgx, axis=0, keepdims=True)
        dckv, dgx = _rms_bwd(_dot_nt(dkv0_ref[...], wkvb_ref[...]), ckv, rkv, gkva, KV_LORA)
        dgkva_ref[...] += jnp.sum(dgx, axis=0, keepdims=True)
        dz_ref[:, 0:256] = dcq.astype(bf16)
        dz_ref[:, 256:512] = dckv.astype(bf16)
        dz_ref[:, 512:640] = dkr.astype(bf16)

    row128 = pl.BlockSpec((tm, 128), lambda i: (i, 0))
    row256 = pl.BlockSpec((tm, 256), lambda i: (i, 0))
    row1024 = pl.BlockSpec((tm, 1024), lambda i: (i, 0))
    hd = lambda w: pl.BlockSpec((hh, tm, w), lambda i: (0, i, 0))
    return pl.pallas_call(
        body, name="bwd_mla_proj", grid=(s // tm,),
        in_specs=[pl.BlockSpec((tm, 256), lambda i: (i, Z_CQ // 256)), pl.BlockSpec((tm, 256), lambda i: (i, Z_CKV // 256)),
                  pl.BlockSpec((tm, 128), lambda i: (i, Z_KR // 128)), hd(QK_PAD), hd(QK_PAD), hd(V_HEAD),
                  row128, row128, row128,
                  _const_spec((1, 256)), _const_spec((1, 256)), _const_spec((256, 1024)), _const_spec((256, 1024)),
                  _const_spec((1, 256)), _const_spec((1, 256))],
        out_specs=[pl.BlockSpec((tm, 640), lambda i: (i, 0)), row256, row256, row1024, row1024,
                   _acc_spec((1, 256)), _acc_spec((1, 256)), _acc_spec((1, 256)), _acc_spec((1, 256))],
        out_shape=[jax.ShapeDtypeStruct((s, 640), bf16), jax.ShapeDtypeStruct((s, 256), bf16), jax.ShapeDtypeStruct((s, 256), bf16),
                   jax.ShapeDtypeStruct((s, 1024), bf16), jax.ShapeDtypeStruct((s, 1024), bf16)]
        + [jax.ShapeDtypeStruct((1, 256), f32)] * 4,
        compiler_params=_cp(("arbitrary",), VMEM_LIMIT),
    )(z, z, z, dq, dk, dv, cosb, sina, sinb, g_qa, g_kva, wqb, wkvb, g_qn, g_kn)


def _bwd_in(segments, wz, x, g_mix, d2, tm):
    s, d = x.shape
    n_seg = len(segments)

    def body(*refs):
        dz_refs, w_refs = refs[:n_seg], refs[n_seg:2 * n_seg]
        x_ref, g_ref, d2_ref, gx_ref, dg_ref = refs[2 * n_seg:]

        @pl.when(pl.program_id(0) == 0)
        def _():
            dg_ref[...] = jnp.zeros_like(dg_ref)

        dh = _dot_nt(dz_refs[0][...], w_refs[0][...])
        for a_ref, w_ref in zip(dz_refs[1:], w_refs[1:]):
            dh = dh + _dot_nt(a_ref[...], w_ref[...])
        x, g = x_ref[...], g_ref[...]
        r = lax.rsqrt(jnp.sum(x * x, axis=-1, keepdims=True) * (1.0 / d) + EPS)
        dx, dgx = _rms_bwd(dh, x, r, g, d)
        gx_ref[...] = d2_ref[...] + dx
        dg_ref[...] += jnp.sum(dgx, axis=0, keepdims=True)

    rowd = pl.BlockSpec((tm, d), lambda i: (i, 0))
    dz_specs = [pl.BlockSpec((tm, w), functools.partial(lambda i, j: (i, j), j=ja)) for _, w, ja, _ in segments]
    w_specs = [pl.BlockSpec((d, w), functools.partial(lambda i, j: (0, j), j=jw), pipeline_mode=pl.Buffered(1))
               for _, w, _, jw in segments]
    return pl.pallas_call(
        body, name="bwd_in", grid=(s // tm,),
        in_specs=dz_specs + w_specs + [rowd, _const_spec((1, d)), rowd],
        out_specs=[rowd, _acc_spec((1, d))],
        out_shape=[jax.ShapeDtypeStruct((s, d), f32), jax.ShapeDtypeStruct((1, d), f32)],
        compiler_params=_cp(("arbitrary",), VMEM_LIMIT),
    )(*[a for a, _, _, _ in segments], *([wz] * n_seg), x, g_mix, d2)


def _pick_tile(n, cap):
    best = None
    for t in range(LANES, cap + 1, LANES):
        if n % t == 0:
            best = t
    return best if best is not None else n


def _mm_tn_many(a, bs, name, tm):
    kk, m = a.shape
    n_b = len(bs)
    tk = min(512, kk)
    n_k = kk // tk

    def body(a_ref, *refs):
        b_refs, o_refs, acc_refs = refs[:n_b], refs[n_b:2 * n_b], refs[2 * n_b:]

        @pl.when(pl.program_id(1) == 0)
        def _():
            for acc in acc_refs:
                acc[...] = jnp.zeros_like(acc)
        a_blk = a_ref[...].astype(bf16)
        for b_ref, acc in zip(b_refs, acc_refs):
            acc[...] += _dot_tn(a_blk, b_ref[...].astype(bf16))

        @pl.when(pl.program_id(1) == n_k - 1)
        def _():
            for o_ref, acc in zip(o_refs, acc_refs):
                o_ref[...] = acc[...].astype(bf16)

    return pl.pallas_call(
        body, name=name, grid=(m // tm, n_k),
        in_specs=[pl.BlockSpec((tk, tm), lambda i, k: (k, i))] + [pl.BlockSpec((tk, b.shape[1]), lambda i, k: (k, 0)) for b in bs],
        out_specs=[pl.BlockSpec((tm, b.shape[1]), lambda i, k: (i, 0)) for b in bs],
        out_shape=[jax.ShapeDtypeStruct((m, b.shape[1]), bf16) for b in bs],
        scratch_shapes=[pltpu.VMEM((tm, b.shape[1]), f32) for b in bs],
        compiler_params=_cp(("parallel", "arbitrary"), VMEM_LIMIT),
    )(a, *bs)


def _mm_tn(a, b, name):
    kk, m = a.shape
    _, n = b.shape
    tm = _pick_tile(m, 1408)
    tn = _pick_tile(n, 1408)
    tk = min(512, kk)

    n_k = kk // tk

    def body(a_ref, b_ref, o_ref, acc_ref):
        @pl.when(pl.program_id(2) == 0)
        def _():
            acc_ref[...] = jnp.zeros_like(acc_ref)
        acc_ref[...] += _dot_tn(a_ref[...].astype(bf16), b_ref[...].astype(bf16))

        @pl.when(pl.program_id(2) == n_k - 1)
        def _():
            o_ref[...] = acc_ref[...].astype(bf16)

    return pl.pallas_call(
        body, name=name, grid=(m // tm, n // tn, n_k),
        in_specs=[pl.BlockSpec((tk, tm), lambda i, j, k: (k, i)), pl.BlockSpec((tk, tn), lambda i, j, k: (k, j))],
        out_specs=pl.BlockSpec((tm, tn), lambda i, j, k: (i, j)),
        out_shape=jax.ShapeDtypeStruct((m, n), bf16),
        scratch_shapes=[pltpu.VMEM((tm, tn), f32)],
        compiler_params=_cp(("parallel", "parallel", "arbitrary"), VMEM_LIMIT),
    )(a, b)


def _rope_tables(positions):
    inv_freq = ROPE_THETA ** (-jnp.arange(0, QK_ROPE, 2, dtype=f32) / QK_ROPE)
    ang = positions.astype(f32)[:, None] * inv_freq
    cos, sin = jnp.cos(ang), jnp.sin(ang)
    zero = jnp.zeros_like(cos)
    return (jnp.concatenate([cos, cos, zero, zero], axis=1), jnp.concatenate([zero, sin, zero, zero], axis=1),
            jnp.concatenate([-sin, zero, zero, zero], axis=1))


def _pad256(g):
    return jnp.pad(g.reshape(1, QK_HEAD), ((0, 0), (0, QK_PAD - QK_HEAD)))


RELAYOUT_BLOCKS = 8
FIRST = ("w_in", "w_qb", "w_kvb", "lb_param")
SECOND = ("w_o", "w_gate", "w_up", "w_down", "w_ple_gate", "w_ple_proj")
ROW_SHARDED = ("w_o", "w_down", "w_ple_gate")


def _col_moves(j):
    lo = BIG["w_in"][1] * j
    w_in = [(max(lo, a) - lo, min(lo + BIG["w_in"][1], b) - lo, d + max(lo, a) - a)
            for a, b, d in Z_SEGMENTS if max(lo, a) < min(lo + BIG["w_in"][1], b)]
    head, half = divmod(j, 2)
    whole = lambda n: [(0, BIG[n][1], BIG[n][1] * j)]
    return {"w_in": w_in, "w_gate": whole("w_gate"), "w_up": whole("w_up"),
            "w_qb": [(0, 96, QK_PAD * head + 96 * half)], "w_kvb": whole("w_kvb"), "w_ple_proj": whole("w_ple_proj"),
            "lb_param": whole("lb_param")}


def _kernel_width(name):
    return {"w_in": Z_W, "w_qb": MLA_HEADS * QK_PAD}.get(name, N_DEV * BIG[name][1])


def _relayout_specs(names, by_dev):
    specs = []
    for n in names:
        rows, cols = BIG[n]
        if n == "lb_param":
            specs.append(_acc_spec((N_DEV, rows, cols) if by_dev else (rows, _kernel_width(n))))
        elif by_dev:
            specs.append(pl.BlockSpec((N_DEV, rows // RELAYOUT_BLOCKS, cols), lambda i: (0, i, 0)))
        else:
            specs.append(pl.BlockSpec((rows // RELAYOUT_BLOCKS, _kernel_width(n)), lambda i: (i, 0)))
    return specs


def _weights_in(gathered, names, name):
    n = len(names)

    def body(*refs):
        ins, outs = dict(zip(names, refs[:n])), dict(zip(names, refs[n:]))
        if "w_in" in outs:
            outs["w_in"][:, Z_KR + QK_ROPE:Z_W] = jnp.zeros((outs["w_in"].shape[0], Z_W - Z_KR - QK_ROPE), bf16)
        if "w_qb" in outs:
            for h in range(MLA_HEADS):
                outs["w_qb"][:, QK_PAD * h + QK_HEAD:QK_PAD * (h + 1)] = jnp.zeros((outs["w_qb"].shape[0], QK_PAD - QK_HEAD), bf16)
        for j in range(N_DEV):
            for wn, moves in _col_moves(j).items():
                if wn in outs:
                    for s0, s1, d0 in moves:
                        outs[wn][:, d0:d0 + s1 - s0] = ins[wn][j, :, s0:s1]

    outs = pl.pallas_call(
        body, name=name, grid=(RELAYOUT_BLOCKS,), in_specs=_relayout_specs(names, True), out_specs=_relayout_specs(names, False),
        out_shape=[jax.ShapeDtypeStruct((BIG[wn][0], _kernel_width(wn)), gathered[wn].dtype) for wn in names],
        compiler_params=_cp(("arbitrary",), VMEM_LIMIT),
    )(*[gathered[wn] for wn in names])
    return dict(zip(names, outs))


def _grads_out(sources, names, name):
    pieces = [(wn, start, arr) for wn in names for start, arr in sources[wn]]
    n_in = len(pieces)

    def body(*refs):
        outs = dict(zip(names, refs[n_in:]))

        def cols(wn, c0, c1):
            for (pn, start, arr), ref in zip(pieces, refs[:n_in]):
                if pn == wn and start <= c0 and c1 <= start + arr.shape[1]:
                    return ref[:, c0 - start:c1 - start]

        for j in range(N_DEV):
            for wn, moves in _col_moves(j).items():
                if wn in outs:
                    for s0, s1, d0 in moves:
                        outs[wn][j, :, s0:s1] = cols(wn, d0, d0 + s1 - s0).astype(bf16)

    in_specs = [_acc_spec(arr.shape) if wn == "lb_param" else pl.BlockSpec((arr.shape[0] // RELAYOUT_BLOCKS, arr.shape[1]), lambda i: (i, 0))
                for wn, _, arr in pieces]
    outs = pl.pallas_call(
        body, name=name, grid=(RELAYOUT_BLOCKS,), in_specs=in_specs, out_specs=_relayout_specs(names, True),
        out_shape=[jax.ShapeDtypeStruct((N_DEV, *BIG[wn]), bf16) for wn in names],
        compiler_params=_cp(("arbitrary",), VMEM_LIMIT),
    )(*[arr for _, _, arr in pieces])
    return dict(zip(names, outs))


def kernel(x, p, positions, g_mix, w_in, g_qa, g_kva, w_qb, w_kvb, g_qn, g_kn, lb_param, g_hgo, w_o, g_ffn, w_gate, w_up, w_down, g_ple, w_ple_gate, w_ple_proj, loss_target, m_g_mix, m_w_in, m_g_qa, m_g_kva, m_w_qb, m_w_kvb, m_g_qn, m_g_kn, m_lb_param, m_g_hgo, m_w_o, m_g_ffn, m_w_gate, m_w_up, m_w_down, m_g_ple, m_w_ple_gate, m_w_ple_proj, v_g_mix, v_w_in, v_g_qa, v_g_kva, v_w_qb, v_w_kvb, v_g_qn, v_g_kn, v_lb_param, v_g_hgo, v_w_o, v_g_ffn, v_w_gate, v_w_up, v_w_down, v_g_ple, v_w_ple_gate, v_w_ple_proj):
    w_all = dict(g_mix=g_mix, g_qa=g_qa, g_kva=g_kva, g_qn=g_qn, g_kn=g_kn, g_hgo=g_hgo, g_ffn=g_ffn, g_ple=g_ple,
                 w_in=w_in, w_qb=w_qb, w_kvb=w_kvb, w_o=w_o, w_gate=w_gate, w_up=w_up, w_down=w_down,
                 w_ple_gate=w_ple_gate, w_ple_proj=w_ple_proj, lb_param=lb_param)
    m_all = dict(g_mix=m_g_mix, g_qa=m_g_qa, g_kva=m_g_kva, g_qn=m_g_qn, g_kn=m_g_kn, g_hgo=m_g_hgo, g_ffn=m_g_ffn,
                 g_ple=m_g_ple, w_in=m_w_in, w_qb=m_w_qb, w_kvb=m_w_kvb, w_o=m_w_o, w_gate=m_w_gate, w_up=m_w_up,
                 w_down=m_w_down, w_ple_gate=m_w_ple_gate, w_ple_proj=m_w_ple_proj, lb_param=m_lb_param)
    v_all = dict(g_mix=v_g_mix, g_qa=v_g_qa, g_kva=v_g_kva, g_qn=v_g_qn, g_kn=v_g_kn, g_hgo=v_g_hgo, g_ffn=v_g_ffn,
                 g_ple=v_g_ple, w_in=v_w_in, w_qb=v_w_qb, w_kvb=v_w_kvb, w_o=v_w_o, w_gate=v_w_gate, w_up=v_w_up,
                 w_down=v_w_down, w_ple_gate=v_w_ple_gate, w_ple_proj=v_w_ple_proj, lb_param=v_lb_param)
    me_idx = jnp.stack([_me()]).astype(jnp.int32)
    x, p, positions, target = x[0], p[0, 0], positions[0], loss_target[0]
    s = x.shape[0]
    tm, tm_ffn, tq_f, tq_b = min(512, s), min(1024, s), min(2048, s), min(1024, s)
    g_mix, g_qa, g_kva, g_qn, g_kn, g_hgo, g_ffn, g_ple = (w_all[n].reshape(1, -1) for n in SMALL)
    g_qn_p, g_kn_p = _pad256(g_qn), _pad256(g_kn)
    cosb, sina, sinb = _rope_tables(positions)
    shard = lambda n: w_all[n].reshape(BIG[n])

    first = _all_gather([shard(n) for n in FIRST], [f32 if n == "lb_param" else bf16 for n in FIRST], "ag_first")
    lands = _cast_to_slot([shard(n) for n in SECOND], me_idx, first[0])
    ag2, token = _exchange_start([], lands, "ag_second_start")
    wk = _weights_in(dict(zip(FIRST, first)), FIRST, "weights_in_first")
    wz, wqb, wkvb, lb4 = (wk[n] for n in FIRST)

    h1, z = _fwd_in(x, g_mix, wz, tm)
    q, k, v = _fwd_mla_proj(z, cosb + token[0, 0], sina, sinb, g_qa, g_kva, wqb, wkvb, g_qn_p, g_kn_p, tm)
    a, a32 = _fwd_attn(q, k, v, tq_f)
    o, gla_b, gla_states = _fwd_gla(z, lb4)

    second = dict(zip(SECOND, _exchange_wait(ag2, [a, o], "ag_second_wait")[1]))
    wk = _weights_in(second, ("w_gate", "w_up", "w_ple_proj"), "weights_in_second")
    w_gate, w_up, w_pp = wk["w_gate"], wk["w_up"], wk["w_ple_proj"]
    w_o, w_down, w_pg = (second[n].reshape(N_DEV * BIG[n][0], BIG[n][1]) for n in ROW_SHARDED)

    x2, cat = _fwd_mix(a, o, z, g_hgo, x, w_o, tm)
    x3, gp, up = _fwd_ffn(x2, g_ffn, w_gate, w_up, w_down, min(256, s))
    d3, h3, dpre, dpp, dg_ple, loss_tile = _ple_loss_fwd_bwd(x3, g_ple, w_pg, p, w_pp, target, tm)
    h2, act, dgp, dup = _bwd_ffn_hidden(d3, x2, gp, up, g_ffn, w_down, tm, D_FF // 2)
    d2, dg_ffn = _bwd_ffn_in(d3, x2, dgp, dup, g_ffn, w_gate, w_up, tm)

    gw_gate, gw_up = _mm_tn_many(h2, [dgp, dup], "dw_gate_up", 512)
    blocks = _grads_out({"w_gate": [(0, gw_gate)], "w_up": [(0, gw_up)], "w_ple_proj": [(0, _mm_tn(p, dpp, "dw_ple_proj"))]},
                        ("w_gate", "w_up", "w_ple_proj"), "grads_out_second")
    row_grads = {"w_o": _mm_tn(cat, d2, "dw_o"), "w_down": _mm_tn(act, d3, "dw_down"), "w_ple_gate": _mm_tn(h3, dpre, "dw_ple_gate")}
    blocks.update({n: g.reshape(N_DEV, *BIG[n]) for n, g in row_grads.items()})
    empty = lambda names: [lax.empty((N_PEERS, *BIG[n]), bf16) for n in names]
    rs2, token = _exchange_start([blocks[n] for n in SECOND], empty(SECOND), "rs_second_start")

    da, do, dz_hg, dg_hgo = _bwd_mix(d2, w_o, o, z, g_hgo + token[0, 0], tm)
    dz_hq, dz_hff, dz_hfb, dz_hi, dlb4 = _bwd_gla(z, lb4, do, gla_b, gla_states)
    dq, dk, dv = _bwd_attn(q, k, v, da, a32, tq_b)
    dz_mla, cqn, ckvn, dq0, dkv0, dg_qa, dg_kva, dg_qn, dg_kn = _bwd_mla_proj(
        z, dq, dk, dv, cosb, sina, sinb, g_qa, g_kva, wqb, wkvb, g_qn_p, g_kn_p, tm)

    gz = list(zip((Z_HQ, Z_HFF, Z_HFB, Z_HI, Z_HG, Z_CQ),
                  _mm_tn_many(h1, [dz_hq, dz_hff, dz_hfb, dz_hi, dz_hg, dz_mla], "dw_in", 1024)))
    blocks1 = _grads_out({"w_in": gz, "w_qb": [(0, _mm_tn(cqn, dq0, "dw_qb"))], "w_kvb": [(0, _mm_tn(ckvn, dkv0, "dw_kvb"))],
                          "lb_param": [(0, dlb4)]}, FIRST, "grads_out_first")
    rs1, token = _exchange_start([blocks1[n] for n in FIRST], empty(FIRST), "rs_first_start")

    result = {}

    def adam(names, lands, src, n_blocks, after=()):
        outs = _adam_shards(me_idx, [src[n] for n in names], lands, [w_all[n] for n in names], [m_all[n] for n in names],
                            [v_all[n] for n in names], n_blocks, "adamw_" + names[0], after)
        result.update(zip(names, outs))
        return outs[0][0]

    blocks2, lands2 = (dict(zip(SECOND, arrs)) for arrs in _exchange_wait(rs2, [token], "rs_second_wait"))
    by8 = tuple(n for n in SECOND if n != "w_down")
    done = [adam(by8, [lands2[n] for n in by8], blocks2, 8), adam(("w_down",), [lands2["w_down"]], blocks2, 2)]

    segments = [(dz_hq, 512, 0, Z_HQ // 512), (dz_hff, 512, 0, Z_HFF // 512), (dz_hfb, 512, 0, Z_HFB // 512),
                (dz_hi, 512, 0, Z_HI // 512), (dz_hg, 512, 0, Z_HG // 512), (dz_mla, 640, 0, Z_CQ // 640)]
    grad_x, dg_mix = _bwd_in(segments, wz, x, g_mix + token[0, 0], d2, tm)
    dgains = (dg_mix, dg_qa, dg_kva, dg_qn, dg_kn, dg_hgo, dg_ffn, dg_ple)

    vec = jnp.concatenate(list(dgains) + [loss_tile[0:1]], axis=1)
    parts = _all_gather([vec], [f32], "ag_gains")[0]
    outs, loss_row = _adam_gains(parts, [w_all[n] for n in SMALL], [m_all[n] for n in SMALL], [v_all[n] for n in SMALL])
    result.update(zip(SMALL, outs))

    blocks1, lands1 = _exchange_wait(rs1, [grad_x, loss_row, *done], "rs_first_wait")
    adam(FIRST, lands1, dict(zip(FIRST, blocks1)), 8)

    order = ("g_mix", "w_in", "g_qa", "g_kva", "w_qb", "w_kvb", "g_qn", "g_kn", "lb_param", "g_hgo", "w_o", "g_ffn",
             "w_gate", "w_up", "w_down", "g_ple", "w_ple_gate", "w_ple_proj")
    return (loss_row[0, 0], grad_x[None], *[result[n][k] for k in range(4) for n in order])
```

```python
import functools
import math

import jax
import jax.numpy as jnp
from jax import lax
from jax.experimental import pallas as pl
from jax.experimental.pallas import tpu as pltpu

f32 = jnp.float32
bf16 = jnp.bfloat16

N_DEV = 8
D_MODEL = 1024
MLA_HEADS = 4
QK_NOPE = 128
QK_ROPE = 64
QK_HEAD = QK_NOPE + QK_ROPE
QK_PAD = 256
V_HEAD = 128
Q_LORA = 256
KV_LORA = 256
HG_HEADS = 4
HG_DK = 128
CHUNK = 64
D_FF = 2816
PLE_DIM = 256
ROPE_THETA = 10000.0
EPS = 1e-6
ATTN_SCALE = QK_HEAD ** -0.5
LOG2_E = math.log2(math.e)
ATTN_SUB_ROWS = 256
IN_SIZES = (256, 256, 64, 512, 512, 512, 512, 512)
D_IN = sum(IN_SIZES)
Z_HQ, Z_HFF, Z_HFB, Z_HI, Z_HG, Z_CQ, Z_CKV, Z_KR, Z_W = 0, 512, 1024, 1536, 2048, 2560, 2816, 3072, 3200

ADAM_LR, ADAM_B1, ADAM_B2, ADAM_EPS, ADAM_WD, ADAM_STEP = 0.001, 0.9, 0.999, 1e-08, 0.01, 10

LANES = 128
BIG = {"w_in": (1024, 392), "w_qb": (256, 96), "w_kvb": (256, 128), "w_o": (128, 1024), "w_gate": (1024, 352),
       "w_up": (1024, 352), "w_down": (352, 1024), "w_ple_gate": (128, 1024), "w_ple_proj": (256, 128),
       "lb_param": (4, 64)}
ROW_BLOCKS = {("w_in", "w_qb", "w_kvb", "w_o", "w_gate", "w_up", "w_ple_gate", "w_ple_proj"): 8, ("w_down", "lb_param"): 2}
SMALL = {"g_mix": (0, 1024), "g_qa": (1024, 256), "g_kva": (1280, 256), "g_qn": (1536, 192), "g_kn": (1792, 192),
         "g_hgo": (2048, 512), "g_ffn": (2560, 1024), "g_ple": (3584, 1024)}
LOSS_OFF = 4608
GAIN_VEC = LOSS_OFF + LANES
Z_SEGMENTS = ((0, 256, Z_CQ), (256, 512, Z_CKV), (512, 576, Z_KR), (576, 1088, Z_HQ), (1088, 1600, Z_HFF),
              (1600, 2112, Z_HFB), (2112, 2624, Z_HI), (2624, 3136, Z_HG))

VMEM_LIMIT = 56 * 1024 * 1024
MESH = pl.DeviceIdType.MESH


def _cp(sem=None, vmem=None):
    return pltpu.CompilerParams(dimension_semantics=sem, vmem_limit_bytes=vmem)


def _const_spec(shape):
    nd = len(shape)
    return pl.BlockSpec(shape, lambda *_: (0,) * nd, pipeline_mode=pl.Buffered(1))


def _acc_spec(shape):
    nd = len(shape)
    return pl.BlockSpec(shape, lambda *_: (0,) * nd)


def _sigmoid(x):
    return jax.nn.sigmoid(x)


def _dot(a, b):
    return jnp.dot(a, b, preferred_element_type=f32)


def _dot_nt(a, b):
    return lax.dot_general(a, b, (((1,), (1,)), ((), ())), preferred_element_type=f32)


def _dot_tn(a, b):
    return lax.dot_general(a, b, (((0,), (0,)), ((), ())), preferred_element_type=f32)


def _rms_fwd(x, g, width):
    r = lax.rsqrt(jnp.sum(x * x, axis=-1, keepdims=True) * (1.0 / width) + EPS)
    return x * r * g, r


def _rms_bwd(dy, x, r, g, width):
    u = dy * g
    dx = r * u - x * (r * r * r) * (jnp.sum(u * x, axis=-1, keepdims=True) * (1.0 / width))
    return dx, dy * x * r


class _Both:
    def __init__(self, *copies):
        self.copies = copies

    def start(self):
        for cp in self.copies:
            cp.start()

    def wait(self):
        for cp in self.copies:
            cp.wait()


def _rope(b, c, sa, sb):
    return b * c + pltpu.roll(b, 32, 1) * sa + pltpu.roll(b, 96, 1) * sb


def _all_gather(shards, dtypes, name):
    n = len(shards)

    def body(*refs):
        in_refs, out_refs, stage = refs[:n], refs[n:2 * n], refs[2 * n:3 * n]
        send_sems, recv_sems, local_sems = refs[3 * n:]
        for w in range(n):
            stage[w][...] = in_refs[w][...].astype(stage[w].dtype)
        x, y, c = lax.axis_index("x"), lax.axis_index("y"), lax.axis_index("c")
        me, sibling = (x, y, c), (x, y, 1 - c)
        chips = [(1 - x, y), (x, 1 - y), (1 - x, 1 - y)]

        def slot(w, px, py, pc):
            return out_refs[w].at[4 * px + 2 * py + pc]

        def copy(w, k, block, to, src=None):
            return pltpu.make_async_remote_copy(
                src_ref=slot(w, *block) if src is None else src, dst_ref=slot(w, *block),
                send_sem=send_sems.at[w, k], recv_sem=recv_sems.at[w, k], device_id=to, device_id_type=MESH)

        first = []
        for j, chip in enumerate(chips):
            first += [copy(w, 1 + j, me, (*chip, c), src=stage[w]) for w in range(n)]
        first += [copy(w, 0, me, sibling, src=stage[w]) for w in range(n)]
        mine = [pltpu.make_async_copy(stage[w], slot(w, *me), local_sems.at[w]) for w in range(n)]
        for cp in first + mine:
            cp.start()
        passed = []
        for j, chip in enumerate(chips):
            for w in range(n):
                copy(w, 1 + j, (*chip, c), me).wait_recv()
                passed.append(copy(w, 4 + j, (*chip, c), sibling))
                passed[-1].start()
        for w in range(n):
            copy(w, 0, sibling, me).wait_recv()
        for j, chip in enumerate(chips):
            for w in range(n):
                copy(w, 4 + j, (*chip, 1 - c), me).wait_recv()
        for cp in first + passed:
            cp.wait_send()
        for cp in mine:
            cp.wait()

    return pl.pallas_call(
        body, name=name,
        out_shape=[jax.ShapeDtypeStruct((N_DEV, *s.shape), dt) for s, dt in zip(shards, dtypes)],
        in_specs=[pl.BlockSpec(memory_space=pltpu.VMEM)] * n,
        out_specs=[pl.BlockSpec(memory_space=pl.ANY)] * n,
        scratch_shapes=[pltpu.VMEM(s.shape, dt) for s, dt in zip(shards, dtypes)]
        + [pltpu.SemaphoreType.DMA((n, 7)), pltpu.SemaphoreType.DMA((n, 7)), pltpu.SemaphoreType.DMA((n,))],
        compiler_params=_cp(None, VMEM_LIMIT),
    )(*shards)


N_PEERS = N_DEV - 1
HBM_SPEC = pl.BlockSpec(memory_space=pltpu.HBM)
SEM_SPEC = pl.BlockSpec(memory_space=pltpu.SEMAPHORE)
DATAFLOW = pltpu.SideEffectType.DATAFLOW_SIDE_EFFECTING


def _me():
    return 4 * lax.axis_index("x") + 2 * lax.axis_index("y") + lax.axis_index("c")


def _peer(k):
    x, y, c = lax.axis_index("x"), lax.axis_index("y"), lax.axis_index("c")
    px = 1 - x if k & 4 else x
    py = 1 - y if k & 2 else y
    pc = 1 - c if k & 1 else c
    return (px, py, pc), 4 * px + 2 * py + pc


def _exchange_copies(src_refs, land_refs, send_sems, recv_sems, gather):
    cps = []
    me = _me()
    for k in range(1, N_DEV):
        peer, peer_idx = _peer(k)
        for w, land in enumerate(land_refs):
            src = land.at[me] if gather else src_refs[w].at[peer_idx]
            dst = land.at[me] if gather else land.at[k - 1]
            cps.append(pltpu.make_async_remote_copy(
                src_ref=src, dst_ref=dst, send_sem=send_sems.at[N_PEERS * w + k - 1], recv_sem=recv_sems.at[N_PEERS * w + k - 1],
                device_id=peer, device_id_type=MESH))
    return cps


def _exchange_start(srcs, lands, name):
    n_src, n = len(srcs), len(lands)

    def body(*refs):
        src_refs, land_refs = refs[:n_src], refs[n_src:n_src + n]
        send_sems, recv_sems = refs[n_src + n], refs[n_src + n + 1]
        token = refs[-1]
        for cp in _exchange_copies(src_refs, land_refs, send_sems, recv_sems, gather=not n_src):
            cp.start()
        token[...] = jnp.zeros_like(token)

    arrays = [pltpu.with_memory_space_constraint(a, pltpu.HBM) for a in (*srcs, *lands)]
    outs = pl.pallas_call(
        body, name=name,
        out_shape=(pltpu.SemaphoreType.DMA((n * N_PEERS,)), pltpu.SemaphoreType.DMA((n * N_PEERS,)),
                   *[pltpu.HBM(a.shape, a.dtype) for a in arrays], jax.ShapeDtypeStruct((8, LANES), f32)),
        in_specs=[HBM_SPEC] * len(arrays),
        out_specs=(SEM_SPEC, SEM_SPEC, *[HBM_SPEC] * len(arrays), pl.BlockSpec(memory_space=pltpu.VMEM)),
        input_output_aliases={i: 2 + i for i in range(len(arrays))},
        compiler_params=pltpu.CompilerParams(has_side_effects=DATAFLOW),
    )(*arrays)
    return (outs[0], outs[1], outs[2:2 + n_src], outs[2 + n_src:2 + n_src + n]), outs[-1]


def _exchange_wait(state, after, name):
    send_sems, recv_sems, srcs, lands = state
    n_src, n = len(srcs), len(lands)

    def body(*refs):
        src_refs, land_refs = refs[:n_src], refs[n_src:n_src + n]
        send_ref, recv_ref = refs[n_src + n], refs[n_src + n + 1]
        for cp in _exchange_copies(src_refs, land_refs, send_ref, recv_ref, gather=not n_src):
            cp.wait_send()
            cp.wait_recv()

    arrays = (*srcs, *lands)
    outs = pl.pallas_call(
        body, name=name,
        out_shape=tuple(pltpu.HBM(a.shape, a.dtype) for a in arrays),
        in_specs=[HBM_SPEC] * len(arrays) + [SEM_SPEC, SEM_SPEC] + [pl.BlockSpec(memory_space=pl.ANY)] * len(after),
        out_specs=tuple([HBM_SPEC] * len(arrays)),
        input_output_aliases={i: i for i in range(len(arrays))},
        compiler_params=pltpu.CompilerParams(has_side_effects=DATAFLOW),
    )(*arrays, send_sems, recv_sems, *after)
    return outs[:n_src], outs[n_src:]


def _cast_to_slot(shards, me_idx, after):
    n = len(shards)

    def body(i_ref, *refs):
        for w in range(n):
            refs[n + 1 + w][...] = refs[w][...].astype(bf16)

    return pl.pallas_call(
        body, name="cast_to_slot",
        grid_spec=pltpu.PrefetchScalarGridSpec(
            num_scalar_prefetch=1, grid=(1,),
            in_specs=[pl.BlockSpec(s.shape, lambda i, m: (0, 0)) for s in shards] + [pl.BlockSpec(memory_space=pl.ANY)],
            out_specs=[pl.BlockSpec((None, *s.shape), lambda i, m: (m[0], 0, 0)) for s in shards]),
        out_shape=[jax.ShapeDtypeStruct((N_DEV, *s.shape), bf16) for s in shards],
        compiler_params=_cp(("arbitrary",), VMEM_LIMIT),
    )(me_idx, *shards, after)


def _row_block(rows, n_blocks):
    return (rows // n_blocks, True) if rows % (16 * n_blocks) == 0 else (rows, False)


def _adam_math(w, g, m, v):
    m = ADAM_B1 * m + (1.0 - ADAM_B1) * g
    v = ADAM_B2 * v + (1.0 - ADAM_B2) * (g * g)
    m_hat = m / (1.0 - ADAM_B1 ** ADAM_STEP)
    v_hat = v / (1.0 - ADAM_B2 ** ADAM_STEP)
    delta = -ADAM_LR * (m_hat / (jnp.sqrt(v_hat) + ADAM_EPS) + ADAM_WD * w)
    return delta, m, v


def _adam_shards(me_idx, blocks, lands, ws, ms, vs, n_blocks, name, after=()):
    n = len(blocks)

    def body(i_ref, *refs):
        ins, outs = refs[:5 * n], refs[5 * n + len(after):]
        for w in range(n):
            g_ref, b_ref, w_ref, m_ref, v_ref = (ins[t * n + w] for t in range(5))
            g = g_ref[...].astype(f32)
            for k in range(N_PEERS):
                g = g + b_ref[k].astype(f32)
            if len(w_ref.shape) == 2:
                pieces = [(slice(None), g)]
            else:
                pieces = [(a, g[2 * a:2 * a + 2]) for a in range(2)]
            for at, gp in pieces:
                vals = (gp,) + _adam_math(w_ref[at], gp, m_ref[at], v_ref[at])
                for t, val in enumerate(vals):
                    outs[4 * w + t][at] = val

    specs = [[] for _ in range(5)]
    out_specs, out_shape = [], []
    for g, wt in zip(blocks, ws):
        rows, cols = g.shape[1:]
        rb, cut = _row_block(rows, n_blocks)
        specs[0].append(pl.BlockSpec((None, rb, cols), functools.partial(lambda i, s, cut: (s[0], i if cut else 0, 0), cut=cut)))
        specs[1].append(pl.BlockSpec((N_PEERS, rb, cols), functools.partial(lambda i, s, cut: (0, i if cut else 0, 0), cut=cut)))
        if wt.shape[0] == 1:
            shard = pl.BlockSpec((None, rb, cols), functools.partial(lambda i, s, cut: (0, i if cut else 0, 0), cut=cut))
        else:
            shard = pl.BlockSpec(wt.shape, functools.partial(lambda i, s, nd: (0,) * nd, nd=wt.ndim))
        for t in (2, 3, 4):
            specs[t].append(shard)
        out_specs += [shard] * 4
        out_shape += [jax.ShapeDtypeStruct(wt.shape, f32)] * 4
    outs = pl.pallas_call(
        body, name=name,
        grid_spec=pltpu.PrefetchScalarGridSpec(
            num_scalar_prefetch=1, grid=(n_blocks,), in_specs=sum(specs, []) + [pl.BlockSpec(memory_space=pl.ANY)] * len(after),
            out_specs=out_specs),
        out_shape=out_shape,
        compiler_params=_cp(("arbitrary",), VMEM_LIMIT),
    )(me_idx, *blocks, *lands, *ws, *ms, *vs, *after)
    return [outs[4 * w:4 * w + 4] for w in range(n)]


def _adam_gains(parts, ws, ms, vs):
    n = len(ws)

    def body(p_ref, *refs):
        ins, outs = refs[:3 * n], refs[3 * n:]
        g_all = p_ref[0]
        for k in range(1, N_DEV):
            g_all = g_all + p_ref[k]
        for w, (off, lanes) in enumerate(SMALL.values()):
            w_ref, m_ref, v_ref = ins[w], ins[n + w], ins[2 * n + w]
            if len(w_ref.shape) == 2:
                pieces = [(slice(None), off, lanes)]
            else:
                pieces = [((slice(None), h), off + LANES * h, LANES) for h in range(w_ref.shape[1])]
            for at, o, ln in pieces:
                g = g_all[:, o:o + ln]
                vals = (g,) + _adam_math(w_ref[at], g, m_ref[at], v_ref[at])
                for t, val in enumerate(vals):
                    outs[4 * w + t][at] = val
        outs[4 * n][...] = g_all[:, LOSS_OFF:LOSS_OFF + LANES]

    out_shape = sum([[jax.ShapeDtypeStruct(w.shape, f32)] * 4 for w in ws], []) + [jax.ShapeDtypeStruct((1, LANES), f32)]
    outs = pl.pallas_call(body, name="adamw_gains", out_shape=out_shape)(parts, *ws, *ms, *vs)
    return [outs[4 * w:4 * w + 4] for w in range(n)], outs[4 * n]


def _fwd_in(x, g_mix, wz, tm):
    s, d = x.shape

    def body(x_ref, g_ref, w_ref, h_ref, z_ref):
        h, _ = _rms_fwd(x_ref[...], g_ref[...], d)
        hb = h.astype(bf16)
        h_ref[...] = hb
        z_ref[...] = _dot(hb, w_ref[...])

    return pl.pallas_call(
        body, name="fwd_in", grid=(s // tm,),
        in_specs=[pl.BlockSpec((tm, d), lambda i: (i, 0)), _const_spec((1, d)), _const_spec((d, Z_W))],
        out_specs=[pl.BlockSpec((tm, d), lambda i: (i, 0)), pl.BlockSpec((tm, Z_W), lambda i: (i, 0))],
        out_shape=[jax.ShapeDtypeStruct((s, d), bf16), jax.ShapeDtypeStruct((s, Z_W), f32)],
        compiler_params=_cp(("parallel",), VMEM_LIMIT),
    )(x, g_mix, wz)


def _mla_qk_fwd(cq, ckv, kr, g_qa, g_kva, wqb, wkvb, g_qn, g_kn):
    cqn, rq = _rms_fwd(cq, g_qa, Q_LORA)
    ckvn, rkv = _rms_fwd(ckv, g_kva, KV_LORA)
    cqn_b, ckvn_b = cqn.astype(bf16), ckvn.astype(bf16)
    q0 = _dot(cqn_b, wqb)
    kv0 = _dot(ckvn_b, wkvb)
    return cqn_b, rq, ckvn_b, rkv, q0, kv0


def _fwd_mla_proj(z, cosb, sina, sinb, g_qa, g_kva, wqb, wkvb, g_qn, g_kn, tm):
    s = z.shape[0]
    hh = MLA_HEADS

    def body(cq_ref, ckv_ref, kr_ref, c_ref, sa_ref, sb_ref, gqa_ref, gkva_ref, wqb_ref, wkvb_ref, gqn_ref, gkn_ref,
             q_ref, k_ref, v_ref):
        _, _, _, _, q0, kv0 = _mla_qk_fwd(cq_ref[...], ckv_ref[...], kr_ref[...], gqa_ref[...], gkva_ref[...],
                                          wqb_ref[...], wkvb_ref[...], gqn_ref[...], gkn_ref[...])
        kr = kr_ref[...]
        c, sa, sb = c_ref[...], sa_ref[...], sb_ref[...]
        gqn, gkn = gqn_ref[...], gkn_ref[...]
        kr_sq = jnp.sum(kr * kr, axis=-1, keepdims=True)
        for h in range(hh):
            qh = q0[:, QK_PAD * h:QK_PAD * (h + 1)]
            qn, _ = _rms_fwd(qh, gqn, QK_HEAD)
            q_ref[h, :, 0:128] = qn[:, 0:128].astype(bf16)
            q_ref[h, :, 128:256] = _rope(qn[:, 128:256], c, sa, sb).astype(bf16)
            kn_ = kv0[:, 256 * h:256 * h + 128]
            rk = lax.rsqrt((jnp.sum(kn_ * kn_, axis=-1, keepdims=True) + kr_sq) * (1.0 / QK_HEAD) + EPS)
            k_ref[h, :, 0:128] = (kn_ * rk * gkn[:, 0:128]).astype(bf16)
            k_ref[h, :, 128:256] = _rope(kr * rk * gkn[:, 128:256], c, sa, sb).astype(bf16)
            v_ref[h] = kv0[:, 256 * h + 128:256 * h + 256].astype(bf16)

    row128 = pl.BlockSpec((tm, 128), lambda i: (i, 0))
    return pl.pallas_call(
        body, name="fwd_mla_proj", grid=(s // tm,),
        in_specs=[pl.BlockSpec((tm, 256), lambda i: (i, Z_CQ // 256)), pl.BlockSpec((tm, 256), lambda i: (i, Z_CKV // 256)),
                  pl.BlockSpec((tm, 128), lambda i: (i, Z_KR // 128)), row128, row128, row128,
                  _const_spec((1, 256)), _const_spec((1, 256)), _const_spec((256, 1024)), _const_spec((256, 1024)),
                  _const_spec((1, 256)), _const_spec((1, 256))],
        out_specs=[pl.BlockSpec((hh, tm, QK_PAD), lambda i: (0, i, 0)), pl.BlockSpec((hh, tm, QK_PAD), lambda i: (0, i, 0)),
                   pl.BlockSpec((hh, tm, V_HEAD), lambda i: (0, i, 0))],
        out_shape=[jax.ShapeDtypeStruct((hh, s, QK_PAD), bf16), jax.ShapeDtypeStruct((hh, s, QK_PAD), bf16),
                   jax.ShapeDtypeStruct((hh, s, V_HEAD), bf16)],
        compiler_params=_cp(("parallel",), VMEM_LIMIT),
    )(z, z, z, cosb, sina, sinb, g_qa, g_kva, wqb, wkvb, g_qn, g_kn)


def _fwd_attn(q, k, v, tq):
    hh, s, _ = q.shape

    n_sub = max(1, tq // ATTN_SUB_ROWS)

    def body(q_ref, k_ref, v_ref, o_ref, o32_ref):
        for t in range(n_sub):
            rows = slice(t * (tq // n_sub), (t + 1) * (tq // n_sub))
            sc = _dot_nt(q_ref[rows, :], k_ref[...])
            p = jnp.exp2((sc - jnp.max(sc, axis=-1, keepdims=True)) * (ATTN_SCALE * LOG2_E))
            l = jnp.sum(p, axis=-1, keepdims=True)
            o = _dot(p.astype(bf16), v_ref[...]) * (1.0 / l)
            o_ref[rows, :] = o.astype(bf16)
            o32_ref[rows, :] = o

    out = pl.BlockSpec((tq, V_HEAD), lambda h, i: (i, h))
    return pl.pallas_call(
        body, name="fwd_attn", grid=(hh, s // tq),
        in_specs=[pl.BlockSpec((None, tq, QK_PAD), lambda h, i: (h, i, 0)),
                  pl.BlockSpec((None, s, QK_PAD), lambda h, i: (h, 0, 0)),
                  pl.BlockSpec((None, s, V_HEAD), lambda h, i: (h, 0, 0))],
        out_specs=[out, out],
        out_shape=[jax.ShapeDtypeStruct((s, hh * V_HEAD), bf16), jax.ShapeDtypeStruct((s, hh * V_HEAD), f32)],
        compiler_params=_cp(("parallel", "parallel"), VMEM_LIMIT),
    )(q, k, v)


def _split3(x):
    hi = x.astype(bf16)
    r1 = x - hi.astype(f32)
    mid = r1.astype(bf16)
    lo = (r1 - mid.astype(f32)).astype(bf16)
    return jnp.concatenate([hi, mid, lo], axis=-1)


def _tri_sum(tri, x):
    y = _dot(tri, _split3(x))
    return y[:, 0:128] + y[:, 128:256] + y[:, 256:384]


GLA_GROUP = 4
GLA_ROWS = GLA_GROUP * CHUNK
GLA_HEADS_PER_STEP = 2


def _gla_masks(rev):
    row = lax.broadcasted_iota(jnp.int32, (GLA_ROWS, GLA_ROWS), 0)
    col = lax.broadcasted_iota(jnp.int32, (GLA_ROWS, GLA_ROWS), 1)
    shift = CHUNK.bit_length() - 1
    same = (jnp.right_shift(row, shift) == jnp.right_shift(col, shift)).astype(f32)
    lower, upper = (row >= col).astype(f32) * same, (row <= col).astype(f32) * same
    keep, keep_t = (upper, lower) if rev else (lower, upper)
    chunk_of = jnp.right_shift(lax.broadcasted_iota(jnp.int32, (GLA_ROWS, 1), 0), shift)
    return keep, keep.astype(bf16), keep_t.astype(bf16), [(chunk_of == c).astype(f32) for c in range(GLA_GROUP)]


def _gla_gates(hq, hf, lower):
    sg = _sigmoid(hf)
    f = lower + (1.0 - lower) * sg
    return hq * _sigmoid(hq), 1.0 - f, jnp.log(f), f, sg


def _gla_last_mid(b, rev):
    b3 = b.reshape(GLA_GROUP, CHUNK, 128)
    last, mid = (0, CHUNK // 2) if rev else (CHUNK - 1, CHUNK // 2 - 1)
    return b3[:, last:last + 1, :], b3[:, mid:mid + 1, :]


def _gla_per_row(per_chunk):
    return jnp.broadcast_to(per_chunk, (GLA_GROUP, CHUNK, 128)).reshape(GLA_ROWS, 128)


def _gla_block_diag(x, row_masks):
    return jnp.concatenate([(x * m).astype(bf16) for m in row_masks], axis=-1)


def _gla_diag(y):
    return jnp.concatenate([y[CHUNK * c:CHUNK * (c + 1), 128 * c:128 * (c + 1)] for c in range(GLA_GROUP)], axis=0)


def _gla_rows(n, n_groups, rev):
    ne = n_groups - 1 - n if rev else n
    return pl.ds(pl.multiple_of(ne * GLA_ROWS, GLA_ROWS), GLA_ROWS), ne * GLA_GROUP


def _gla_scan_order(rev):
    return tuple(reversed(range(GLA_GROUP))) if rev else tuple(range(GLA_GROUP))


def _fwd_gla(z, lb4):
    s = z.shape[0]
    n_groups = s // GLA_ROWS
    assert n_groups % 2 == 0
    hp = GLA_HEADS_PER_STEP
    chains = [(hh, rev) for hh in range(hp) for rev in (False, True)]

    def body(hq_ref, hff_ref, hfb_ref, hi_ref, lb_ref, o_ref, b_ref, states_ref, st_ref, stage_ref, b_stage, sems):
        st_ref[...] = jnp.zeros_like(st_ref)
        masks = {rev: _gla_masks(rev) for rev in (False, True)}
        lowers = [_sigmoid(lb_ref[int(rev):int(rev) + 1, 128 * hh:128 * (hh + 1)]
                           - lb_ref[2 + int(rev):3 + int(rev), 128 * hh:128 * (hh + 1)]) for hh, rev in chains]

        def states_out(slot, ci, chunk0):
            hh, rev = chains[ci]
            head = pl.program_id(0) * hp + hh
            rows = pl.ds(pl.multiple_of(chunk0 * CHUNK, GLA_ROWS), GLA_ROWS)
            return _Both(
                pltpu.make_async_copy(stage_ref.at[slot, ci], states_ref.at[head, int(rev), pl.ds(chunk0, GLA_GROUP)],
                                      sems.at[slot, ci]),
                pltpu.make_async_copy(b_stage.at[slot, ci], b_ref.at[int(rev), rows, pl.ds(pl.multiple_of(head * 128, 128), 128)],
                                      sems.at[slot, len(chains) + ci]))

        def make_step(first):
            def step(n, carry):
                slot = n % 2

                @pl.when(n >= 2)
                def _():
                    for ci in range(len(chains)):
                        states_out(slot, ci, 0).wait()

                for ci, (hh, rev) in enumerate(chains):
                    cols = slice(128 * hh, 128 * (hh + 1))
                    rows, chunk0 = _gla_rows(n, n_groups, rev)
                    maskf, tri, _, row_masks = masks[rev]
                    hf_ref = hfb_ref if rev else hff_ref
                    q, k, logf, _, _ = _gla_gates(hq_ref[rows, cols], hf_ref[rows, cols], lowers[ci])
                    vb = hi_ref[rows, cols].astype(bf16)
                    b = _tri_sum(tri, logf)
                    b_stage[slot, ci] = b
                    b_last3, b_mid3 = _gla_last_mid(b, rev)
                    b_last, b_mid = _gla_per_row(b_last3), _gla_per_row(b_mid3)
                    qi = (q * jnp.exp(b - b_mid)).astype(bf16)
                    ki = (k * jnp.exp(b_mid - b)).astype(bf16)
                    a = (_dot_nt(qi, ki) * maskf).astype(bf16)
                    kv = _dot_tn(vb, _gla_block_diag(k * jnp.exp(b_last - b), row_masks))
                    decay3 = jnp.exp(b_last3)
                    st = st_ref[ci]
                    before = [None] * GLA_GROUP
                    for c in _gla_scan_order(rev):
                        stage_ref[slot, ci, c] = st
                        before[c] = st.astype(bf16)
                        st = st * decay3[c] + kv[:, 128 * c:128 * (c + 1)]
                    st_ref[ci] = st
                    states_out(slot, ci, chunk0).start()
                    inter = _dot_nt((q * jnp.exp(b)).astype(bf16), jnp.concatenate(before, axis=0))
                    o = _dot(a, vb) + _gla_diag(inter)
                    if first:
                        o_ref[rows, cols] = o
                    else:
                        o_ref[rows, cols] += o
                return carry
            return step

        lax.fori_loop(0, n_groups // 2, make_step(True), 0)
        lax.fori_loop(n_groups // 2, n_groups, make_step(False), 0)
        for slot in range(2):
            for ci in range(len(chains)):
                states_out(slot, ci, 0).wait()

    w = 128 * hp
    col = lambda base: pl.BlockSpec((s, w), lambda h: (0, base // w + h))
    return pl.pallas_call(
        body, name="fwd_gla", grid=(HG_HEADS // hp,),
        in_specs=[col(Z_HQ), col(Z_HFF), col(Z_HFB), col(Z_HI), pl.BlockSpec((4, w), lambda h: (0, h))],
        out_specs=[pl.BlockSpec((s, w), lambda h: (0, h)), pl.BlockSpec(memory_space=pl.ANY), pl.BlockSpec(memory_space=pl.ANY)],
        out_shape=[jax.ShapeDtypeStruct((s, HG_HEADS * 128), f32), jax.ShapeDtypeStruct((2, s, HG_HEADS * 128), f32),
                   jax.ShapeDtypeStruct((HG_HEADS, 2, s // CHUNK, 128, 128), f32)],
        scratch_shapes=[pltpu.VMEM((len(chains), 128, 128), f32), pltpu.VMEM((2, len(chains), GLA_GROUP, 128, 128), f32),
                        pltpu.VMEM((2, len(chains), GLA_ROWS, 128), f32), pltpu.SemaphoreType.DMA((2, 2 * len(chains)))],
        compiler_params=_cp(("parallel",), VMEM_LIMIT),
    )(z, z, z, z, lb4)


def _hg_out(o, hg, g_hgo):
    outs, ons, rs = [], [], []
    for h in range(HG_HEADS):
        oh = o[:, 128 * h:128 * (h + 1)]
        on, r = _rms_fwd(oh, g_hgo[:, 128 * h:128 * (h + 1)], 128)
        ons.append(on)
        rs.append(r)
    on = jnp.concatenate(ons, axis=-1)
    sg = _sigmoid(hg)
    return on * (hg * sg), on, rs, sg


def _fwd_mix(a, o, z, g_hgo, x, w_o, tm):
    s, d = x.shape

    def body(a_ref, o_ref, hg_ref, g_ref, x_ref, w_ref, x2_ref, cat_ref):
        r, _, _, _ = _hg_out(o_ref[...], hg_ref[...], g_ref[...])
        cat = jnp.concatenate([a_ref[...], r.astype(bf16)], axis=-1)
        cat_ref[...] = cat
        x2_ref[...] = x_ref[...] + _dot(cat, w_ref[...])

    row512 = pl.BlockSpec((tm, 512), lambda i: (i, 0))
    rowd = pl.BlockSpec((tm, d), lambda i: (i, 0))
    return pl.pallas_call(
        body, name="fwd_mix", grid=(s // tm,),
        in_specs=[row512, row512, pl.BlockSpec((tm, 512), lambda i: (i, Z_HG // 512)), _const_spec((1, 512)), rowd,
                  _const_spec((d, d))],
        out_specs=[rowd, rowd],
        out_shape=[jax.ShapeDtypeStruct((s, d), f32), jax.ShapeDtypeStruct((s, d), bf16)],
        compiler_params=_cp(("parallel",), VMEM_LIMIT),
    )(a, o, z, g_hgo, x, w_o)


def _fwd_ffn(x2, g_ffn, w_gate, w_up, w_down, tm):
    s, d = x2.shape

    def body(x_ref, g_ref, wg_ref, wu_ref, wd_ref, x3_ref, gp_ref, up_ref):
        x = x_ref[...]
        h, _ = _rms_fwd(x, g_ref[...], d)
        hb = h.astype(bf16)
        gp = _dot(hb, wg_ref[...])
        up = _dot(hb, wu_ref[...])
        gp_ref[...] = gp.astype(bf16)
        up_ref[...] = up.astype(bf16)
        act = (gp * _sigmoid(gp) * up).astype(bf16)
        x3_ref[...] = x + _dot(act, wd_ref[...])

    rowd = pl.BlockSpec((tm, d), lambda i: (i, 0))
    rowf = pl.BlockSpec((tm, D_FF), lambda i: (i, 0))
    return pl.pallas_call(
        body, name="fwd_ffn", grid=(s // tm,),
        in_specs=[rowd, _const_spec((1, d)), _const_spec((d, D_FF)), _const_spec((d, D_FF)), _const_spec((D_FF, d))],
        out_specs=[rowd, rowf, rowf],
        out_shape=[jax.ShapeDtypeStruct((s, d), f32), jax.ShapeDtypeStruct((s, D_FF), bf16),
                   jax.ShapeDtypeStruct((s, D_FF), bf16)],
        compiler_params=_cp(("parallel",), VMEM_LIMIT),
    )(x2, g_ffn, w_gate, w_up, w_down)


def _ple_loss_fwd_bwd(x3, g_ple, w_pg, p, w_pp, target, tm):
    s, d = x3.shape

    def body(x_ref, g_ref, wg_ref, p_ref, wp_ref, t_ref, dx_ref, h_ref, dpre_ref, dpp_ref, dg_ref, loss_ref):
        @pl.when(pl.program_id(0) == 0)
        def _():
            dg_ref[...] = jnp.zeros_like(dg_ref)
            loss_ref[...] = jnp.zeros_like(loss_ref)

        x = x_ref[...]
        g = g_ref[...]
        h, r = _rms_fwd(x, g, d)
        hb = h.astype(bf16)
        gate = _sigmoid(_dot(hb, wg_ref[...]))
        pp = _dot(p_ref[...].astype(bf16), wp_ref[...])
        e = x + gate * pp - t_ref[...]
        loss_ref[...] += 0.5 * jnp.sum(e * e) * (1.0 / d)
        dy = e * (1.0 / d)
        dpre = (dy * pp * gate * (1.0 - gate)).astype(bf16)
        dx, dgx = _rms_bwd(_dot_nt(dpre, wg_ref[...]), x, r, g, d)
        dx_ref[...] = dy + dx
        dg_ref[...] += jnp.sum(dgx, axis=0, keepdims=True)
        h_ref[...] = hb
        dpre_ref[...] = dpre
        dpp_ref[...] = (dy * gate).astype(bf16)

    rowd = pl.BlockSpec((tm, d), lambda i: (i, 0))
    return pl.pallas_call(
        body, name="ple_loss_fwd_bwd", grid=(s // tm,),
        in_specs=[rowd, _const_spec((1, d)), _const_spec((d, d)), pl.BlockSpec((tm, PLE_DIM), lambda i: (i, 0)),
                  _const_spec((PLE_DIM, d)), rowd],
        out_specs=[rowd, rowd, rowd, rowd, _acc_spec((1, d)), _acc_spec((8, 128))],
        out_shape=[jax.ShapeDtypeStruct((s, d), f32), jax.ShapeDtypeStruct((s, d), bf16), jax.ShapeDtypeStruct((s, d), bf16),
                   jax.ShapeDtypeStruct((s, d), bf16), jax.ShapeDtypeStruct((1, d), f32), jax.ShapeDtypeStruct((8, 128), f32)],
        compiler_params=_cp(("arbitrary",), VMEM_LIMIT),
    )(x3, g_ple, w_pg, p, w_pp, target)


def _bwd_ffn_hidden(d3, x2, gp, up, g_ffn, w_down, tm, tf):
    s, d = x2.shape

    def body(d3_ref, x_ref, gp_ref, up_ref, g_ref, wd_ref, h_ref, act_ref, dgp_ref, dup_ref, d3b_ref):
        @pl.when(pl.program_id(1) == 0)
        def _():
            h, _ = _rms_fwd(x_ref[...], g_ref[...], d)
            h_ref[...] = h.astype(bf16)
            d3b_ref[...] = d3_ref[...].astype(bf16)

        gp, up = gp_ref[...].astype(f32), up_ref[...].astype(f32)
        sg = _sigmoid(gp)
        silu = gp * sg
        act_ref[...] = (silu * up).astype(bf16)
        dact = _dot_nt(d3b_ref[...], wd_ref[...])
        dgp_ref[...] = (dact * up * (sg * (1.0 + gp * (1.0 - sg)))).astype(bf16)
        dup_ref[...] = (dact * silu).astype(bf16)

    rowd = pl.BlockSpec((tm, d), lambda i, f: (i, 0))
    rowf = pl.BlockSpec((tm, tf), lambda i, f: (i, f))
    return pl.pallas_call(
        body, name="bwd_ffn_hidden", grid=(s // tm, D_FF // tf),
        in_specs=[rowd, rowd, rowf, rowf, _const_spec((1, d)), pl.BlockSpec((tf, d), lambda i, f: (f, 0))],
        out_specs=[rowd, rowf, rowf, rowf],
        out_shape=[jax.ShapeDtypeStruct((s, d), bf16)] + [jax.ShapeDtypeStruct((s, D_FF), bf16)] * 3,
        scratch_shapes=[pltpu.VMEM((tm, d), bf16)],
        compiler_params=_cp(("parallel", "arbitrary"), VMEM_LIMIT),
    )(d3, x2, gp, up, g_ffn, w_down)


def _bwd_ffn_in(d3, x2, dgp, dup, g_ffn, w_gate, w_up, tm):
    s, d = x2.shape

    def body(d3_ref, x_ref, dgp_ref, dup_ref, g_ref, wg_ref, wu_ref, d2_ref, dg_ref):
        @pl.when(pl.program_id(0) == 0)
        def _():
            dg_ref[...] = jnp.zeros_like(dg_ref)

        x, g = x_ref[...], g_ref[...]
        dh = _dot_nt(dgp_ref[...], wg_ref[...]) + _dot_nt(dup_ref[...], wu_ref[...])
        r = lax.rsqrt(jnp.sum(x * x, axis=-1, keepdims=True) * (1.0 / d) + EPS)
        dx, dgx = _rms_bwd(dh, x, r, g, d)
        d2_ref[...] = d3_ref[...] + dx
        dg_ref[...] += jnp.sum(dgx, axis=0, keepdims=True)

    rowd = pl.BlockSpec((tm, d), lambda i: (i, 0))
    rowf = pl.BlockSpec((tm, D_FF), lambda i: (i, 0))
    return pl.pallas_call(
        body, name="bwd_ffn_in", grid=(s // tm,),
        in_specs=[rowd, rowd, rowf, rowf, _const_spec((1, d)), _const_spec((d, D_FF)), _const_spec((d, D_FF))],
        out_specs=[rowd, _acc_spec((1, d))],
        out_shape=[jax.ShapeDtypeStruct((s, d), f32), jax.ShapeDtypeStruct((1, d), f32)],
        compiler_params=_cp(("arbitrary",), VMEM_LIMIT),
    )(d3, x2, dgp, dup, g_ffn, w_gate, w_up)


def _bwd_mix(d2, w_o, o, z, g_hgo, tm):
    s, d = d2.shape

    def body(d2_ref, w_ref, o_ref, hg_ref, g_ref, da_ref, do_ref, dhg_ref, dg_ref):
        @pl.when(pl.program_id(0) == 0)
        def _():
            dg_ref[...] = jnp.zeros_like(dg_ref)

        dcat = _dot_nt(d2_ref[...].astype(bf16), w_ref[...])
        da_ref[...] = dcat[:, 0:512].astype(bf16)
        dr = dcat[:, 512:1024]
        o, hg, g = o_ref[...], hg_ref[...], g_ref[...]
        _, on, rs, sg = _hg_out(o, hg, g)
        dhg_ref[...] = (dr * on * (sg * (1.0 + hg * (1.0 - sg)))).astype(bf16)
        don = dr * (hg * sg)
        dgs = []
        for h in range(HG_HEADS):
            cols = slice(128 * h, 128 * (h + 1))
            dx, dgx = _rms_bwd(don[:, cols], o[:, cols], rs[h], g[:, cols], 128)
            do_ref[:, cols] = dx
            dgs.append(jnp.sum(dgx, axis=0, keepdims=True))
        dg_ref[...] += jnp.concatenate(dgs, axis=-1)

    row512 = pl.BlockSpec((tm, 512), lambda i: (i, 0))
    return pl.pallas_call(
        body, name="bwd_mix", grid=(s // tm,),
        in_specs=[pl.BlockSpec((tm, d), lambda i: (i, 0)), _const_spec((d, d)), row512,
                  pl.BlockSpec((tm, 512), lambda i: (i, Z_HG // 512)), _const_spec((1, 512))],
        out_specs=[row512, row512, row512, _acc_spec((1, 512))],
        out_shape=[jax.ShapeDtypeStruct((s, 512), bf16), jax.ShapeDtypeStruct((s, 512), f32), jax.ShapeDtypeStruct((s, 512), bf16),
                   jax.ShapeDtypeStruct((1, 512), f32)],
        compiler_params=_cp(("arbitrary",), VMEM_LIMIT),
    )(d2, w_o, o, z, g_hgo)


def _bwd_gla(z, lb4, do, b_fwd, states):
    s = z.shape[0]
    n_chunks = s // CHUNK
    n_groups = s // GLA_ROWS
    assert n_groups % 2 == 0

    def body(hq_ref, hff_ref, hfb_ref, hi_ref, lb_ref, do_ref, b_all, st_all, dhq_ref, dhff_ref, dhfb_ref, dhi_ref, dlb_ref,
             dst_ref, dq_acc, dv_acc, dlow_ref):
        dirs = (False, True)
        masks = [_gla_masks(rev) for rev in dirs]
        lowers = [_sigmoid(lb_ref[int(rev):int(rev) + 1, :] - lb_ref[2 + int(rev):3 + int(rev), :]) for rev in dirs]
        hf_refs, dhf_refs = (hff_ref, hfb_ref), (dhff_ref, dhfb_ref)

        dst_ref[...] = jnp.zeros_like(dst_ref)
        dlow_ref[...] = jnp.zeros_like(dlow_ref)

        def make_bwd_step(first):
            def bwd_step(j, carry):
                n = n_groups - 1 - j
                for d, rev in enumerate(dirs):
                    maskf, _, tri_t, row_masks = masks[d]
                    lower = lowers[d]
                    rows, chunk0 = _gla_rows(n, n_groups, rev)
                    hq, hf = hq_ref[rows, :], hf_refs[d][rows, :]
                    q, k, _, f, sg = _gla_gates(hq, hf, lower)
                    v = hi_ref[rows, :]
                    dout = do_ref[rows, :]
                    b = b_all[d, rows, :]
                    b_last3, b_mid3 = _gla_last_mid(b, rev)
                    b_last, b_mid = _gla_per_row(b_last3), _gla_per_row(b_mid3)
                    e1, e2, e3, e4 = jnp.exp(b - b_mid), jnp.exp(b_mid - b), jnp.exp(b_last - b), jnp.exp(b)
                    decay3 = jnp.exp(b_last3)
                    qi, ki, kt, qt = q * e1, k * e2, k * e3, q * e4
                    qib, kib, ktb = qi.astype(bf16), ki.astype(bf16), kt.astype(bf16)
                    vb, dob = v.astype(bf16), dout.astype(bf16)
                    a = (_dot_nt(qib, kib) * maskf).astype(bf16)
                    da = (_dot_nt(dob, vb) * maskf).astype(bf16)
                    dqi = _dot(da, kib)
                    dki = _dot_tn(da, qib)
                    into_state = _dot_tn(dob, _gla_block_diag(qt, row_masks))
                    dst = dst_ref[d]
                    sts, dsts, ddecay = [None] * GLA_GROUP, [None] * GLA_GROUP, [None] * GLA_GROUP
                    for c in reversed(_gla_scan_order(rev)):
                        sts[c] = st_all[d, chunk0 + c]
                        dsts[c] = dst.astype(bf16)
                        ddecay[c] = jnp.sum(dst * sts[c], axis=0, keepdims=True)[None]
                        dst = dst * decay3[c] + into_state[:, 128 * c:128 * (c + 1)]
                    dst_ref[d] = dst
                    dv = _dot_tn(a, dob) + _gla_diag(_dot_nt(ktb, jnp.concatenate(dsts, axis=0)))
                    dqt = _gla_diag(_dot(dob, jnp.concatenate([x.astype(bf16) for x in sts], axis=-1)))
                    dkt = _gla_diag(_dot(vb, jnp.concatenate(dsts, axis=-1)))
                    dq = dqi * e1 + dqt * e4
                    dk = dki * e2 + dkt * e3
                    db = dqi * qi - dki * ki + dqt * qt - dkt * kt
                    dlast3 = (jnp.sum((dkt * kt).reshape(GLA_GROUP, CHUNK, 128), axis=1, keepdims=True)
                              + jnp.concatenate(ddecay, axis=0) * decay3)
                    dlogf = _tri_sum(tri_t, db) + _gla_per_row(dlast3)
                    df = dlogf / f - dk
                    dhf_refs[d][rows, :] = (df * (1.0 - lower) * sg * (1.0 - sg)).astype(bf16)
                    dlow_ref[d:d + 1, :] += jnp.sum(df * (1.0 - sg), axis=0, keepdims=True)
                    sq = _sigmoid(hq)
                    dhq = dq * (sq * (1.0 + hq * (1.0 - sq)))
                    if first:
                        dq_acc[rows, :] = dhq
                        dv_acc[rows, :] = dv
                    else:
                        dhq_ref[rows, :] = (dq_acc[rows, :] + dhq).astype(bf16)
                        dhi_ref[rows, :] = (dv_acc[rows, :] + dv).astype(bf16)
                return carry
            return bwd_step

        lax.fori_loop(0, n_groups // 2, make_bwd_step(True), 0)
        lax.fori_loop(n_groups // 2, n_groups, make_bwd_step(False), 0)

        for d in range(2):
            dl = dlow_ref[d:d + 1, :] * lowers[d] * (1.0 - lowers[d])
            dlb_ref[d:d + 1, :] = dl
            dlb_ref[2 + d:3 + d, :] = -dl

    col = lambda base: pl.BlockSpec((s, 128), lambda h: (0, base // 128 + h))
    return pl.pallas_call(
        body, name="bwd_gla", grid=(HG_HEADS,),
        in_specs=[col(Z_HQ), col(Z_HFF), col(Z_HFB), col(Z_HI), pl.BlockSpec((4, 128), lambda h: (0, h)), col(0),
                  pl.BlockSpec((2, s, 128), lambda h: (0, 0, h)),
                  pl.BlockSpec((None, 2, n_chunks, 128, 128), lambda h: (h, 0, 0, 0, 0), pipeline_mode=pl.Buffered(1))],
        out_specs=[col(0), col(0), col(0), col(0), pl.BlockSpec((4, 128), lambda h: (0, h))],
        out_shape=[jax.ShapeDtypeStruct((s, 512), bf16)] * 4 + [jax.ShapeDtypeStruct((4, 512), f32)],
        scratch_shapes=[pltpu.VMEM((2, 128, 128), f32), pltpu.VMEM((s, 128), f32), pltpu.VMEM((s, 128), f32),
                        pltpu.VMEM((2, 128), f32)],
        compiler_params=_cp(("parallel",), VMEM_LIMIT),
    )(z, z, z, z, lb4, do, b_fwd, states)


def _bwd_attn(q, k, v, da, a32, tq):
    hh, s, _ = q.shape

    n_sub = max(1, tq // ATTN_SUB_ROWS)

    def body(q_ref, k_ref, v_ref, do_ref, o_ref, dq_ref, dk_ref, dv_ref, p_all, ds_all, dol_ref, dkt_ref, dvt_ref):
        @pl.when(pl.program_id(1) == 0)
        def _():
            dkt_ref[...] = jnp.zeros_like(dkt_ref)
            dvt_ref[...] = jnp.zeros_like(dvt_ref)

        kb, vb = k_ref[...], v_ref[...]
        for t in range(n_sub):
            rows = slice(t * (tq // n_sub), (t + 1) * (tq // n_sub))
            sc = _dot_nt(q_ref[rows, :], kb)
            p = jnp.exp2((sc - jnp.max(sc, axis=-1, keepdims=True)) * (ATTN_SCALE * LOG2_E))
            inv_l = 1.0 / jnp.sum(p, axis=-1, keepdims=True)
            pb = p.astype(bf16)
            dob = do_ref[rows, :]
            dof = dob.astype(f32)
            delta = jnp.sum(dof * o_ref[rows, :], axis=-1, keepdims=True)
            ds = pb * ((_dot_nt(dob, vb) - delta) * inv_l).astype(bf16)
            dq_ref[rows, :] = _dot(ds, kb) * ATTN_SCALE
            p_all[rows, :] = pb
            ds_all[rows, :] = ds
            dol_ref[rows, :] = (dof * inv_l).astype(bf16)
        dkt_ref[...] += _dot_tn(q_ref[...], ds_all[...])
        dvt_ref[...] += _dot_tn(dol_ref[...], p_all[...])

        @pl.when(pl.program_id(1) == s // tq - 1)
        def _():
            dk_ref[...] = dkt_ref[...].T * ATTN_SCALE
            dv_ref[...] = dvt_ref[...].T

    return pl.pallas_call(
        body, name="bwd_attn", grid=(hh, s // tq),
        in_specs=[pl.BlockSpec((None, tq, QK_PAD), lambda h, i: (h, i, 0)),
                  pl.BlockSpec((None, s, QK_PAD), lambda h, i: (h, 0, 0)),
                  pl.BlockSpec((None, s, V_HEAD), lambda h, i: (h, 0, 0)),
                  pl.BlockSpec((tq, V_HEAD), lambda h, i: (i, h)), pl.BlockSpec((tq, V_HEAD), lambda h, i: (i, h))],
        out_specs=[pl.BlockSpec((None, tq, QK_PAD), lambda h, i: (h, i, 0)),
                   pl.BlockSpec((None, s, QK_PAD), lambda h, i: (h, 0, 0)),
                   pl.BlockSpec((None, s, V_HEAD), lambda h, i: (h, 0, 0))],
        out_shape=[jax.ShapeDtypeStruct((hh, s, QK_PAD), f32), jax.ShapeDtypeStruct((hh, s, QK_PAD), f32),
                   jax.ShapeDtypeStruct((hh, s, V_HEAD), f32)],
        scratch_shapes=[pltpu.VMEM((tq, s), bf16), pltpu.VMEM((tq, s), bf16), pltpu.VMEM((tq, V_HEAD), bf16),
                        pltpu.VMEM((QK_PAD, s), f32), pltpu.VMEM((V_HEAD, s), f32)],
        compiler_params=_cp(("parallel", "arbitrary"), VMEM_LIMIT),
    )(q, k, v, da, a32)


def _bwd_mla_proj(z, dq, dk, dv, cosb, sina, sinb, g_qa, g_kva, wqb, wkvb, g_qn, g_kn, tm):
    s = z.shape[0]
    hh = MLA_HEADS

    def body(cq_ref, ckv_ref, kr_ref, dq_ref, dk_ref, dv_ref, c_ref, sa_ref, sb_ref, gqa_ref, gkva_ref, wqb_ref, wkvb_ref,
             gqn_ref, gkn_ref, dz_ref, cqn_ref, ckvn_ref, dq0_ref, dkv0_ref, dgqa_ref, dgkva_ref, dgqn_ref, dgkn_ref):
        @pl.when(pl.program_id(0) == 0)
        def _():
            for r in (dgqa_ref, dgkva_ref, dgqn_ref, dgkn_ref):
                r[...] = jnp.zeros_like(r)

        cq, ckv, kr = cq_ref[...], ckv_ref[...], kr_ref[...]
        gqa, gkva, gqn, gkn = gqa_ref[...], gkva_ref[...], gqn_ref[...], gkn_ref[...]
        cqn_b, rq, ckvn_b, rkv, q0, kv0 = _mla_qk_fwd(cq, ckv, kr, gqa, gkva, wqb_ref[...], wkvb_ref[...], gqn, gkn)
        cqn_ref[...] = cqn_b
        ckvn_ref[...] = ckvn_b
        c, sa, sb = c_ref[...], -sa_ref[...], -sb_ref[...]
        kr_sq = jnp.sum(kr * kr, axis=-1, keepdims=True)
        dkr = jnp.zeros_like(kr)
        dgqn = jnp.zeros((1, QK_PAD), f32)
        dgkn = jnp.zeros((1, QK_PAD), f32)
        for h in range(hh):
            qh = q0[:, QK_PAD * h:QK_PAD * (h + 1)]
            rh = lax.rsqrt(jnp.sum(qh * qh, axis=-1, keepdims=True) * (1.0 / QK_HEAD) + EPS)
            dqh = dq_ref[h]
            dqn = jnp.concatenate([dqh[:, 0:128], _rope(dqh[:, 128:256], c, sa, sb)], axis=-1)
            dq0h, dgx = _rms_bwd(dqn, qh, rh, gqn, QK_HEAD)
            dq0_ref[:, QK_PAD * h:QK_PAD * (h + 1)] = dq0h.astype(bf16)
            dgqn = dgqn + jnp.sum(dgx, axis=0, keepdims=True)

            kn_ = kv0[:, 256 * h:256 * h + 128]
            k0 = jnp.concatenate([kn_, kr], axis=-1)
            rk = lax.rsqrt((jnp.sum(kn_ * kn_, axis=-1, keepdims=True) + kr_sq) * (1.0 / QK_HEAD) + EPS)
            dkh = dk_ref[h]
            dkn = jnp.concatenate([dkh[:, 0:128], _rope(dkh[:, 128:256], c, sa, sb)], axis=-1)
            dk0, dgx = _rms_bwd(dkn, k0, rk, gkn, QK_HEAD)
            dgkn = dgkn + jnp.sum(dgx, axis=0, keepdims=True)
            dkv0_ref[:, 256 * h:256 * h + 128] = dk0[:, 0:128].astype(bf16)
            dkv0_ref[:, 256 * h + 128:256 * h + 256] = dv_ref[h].astype(bf16)
            dkr = dkr + dk0[:, 128:256]
        dgqn_ref[...] += dgqn
        dgkn_ref[...] += dgkn
        dcq, dgx = _rms_bwd(_dot_nt(dq0_ref[...], wqb_ref[...]), cq, rq, gqa, Q_LORA)
        dgqa_ref[...] += jnp.sum(dgx, axis=0, keepdims=True)
        dckv, dgx = _rms_bwd(_dot_nt(dkv0_ref[...], wkvb_ref[...]), ckv, rkv, gkva, KV_LORA)
        dgkva_ref[...] += jnp.sum(dgx, axis=0, keepdims=True)
        dz_ref[:, 0:256] = dcq.astype(bf16)
        dz_ref[:, 256:512] = dckv.astype(bf16)
        dz_ref[:, 512:640] = dkr.astype(bf16)

    row128 = pl.BlockSpec((tm, 128), lambda i: (i, 0))
    row256 = pl.BlockSpec((tm, 256), lambda i: (i, 0))
    row1024 = pl.BlockSpec((tm, 1024), lambda i: (i, 0))
    hd = lambda w: pl.BlockSpec((hh, tm, w), lambda i: (0, i, 0))
    return pl.pallas_call(
        body, name="bwd_mla_proj", grid=(s // tm,),
        in_specs=[pl.BlockSpec((tm, 256), lambda i: (i, Z_CQ // 256)), pl.BlockSpec((tm, 256), lambda i: (i, Z_CKV // 256)),
                  pl.BlockSpec((tm, 128), lambda i: (i, Z_KR // 128)), hd(QK_PAD), hd(QK_PAD), hd(V_HEAD),
                  row128, row128, row128,
                  _const_spec((1, 256)), _const_spec((1, 256)), _const_spec((256, 1024)), _const_spec((256, 1024)),
                  _const_spec((1, 256)), _const_spec((1, 256))],
        out_specs=[pl.BlockSpec((tm, 640), lambda i: (i, 0)), row256, row256, row1024, row1024,
                   _acc_spec((1, 256)), _acc_spec((1, 256)), _acc_spec((1, 256)), _acc_spec((1, 256))],
        out_shape=[jax.ShapeDtypeStruct((s, 640), bf16), jax.ShapeDtypeStruct((s, 256), bf16), jax.ShapeDtypeStruct((s, 256), bf16),
                   jax.ShapeDtypeStruct((s, 1024), bf16), jax.ShapeDtypeStruct((s, 1024), bf16)]
        + [jax.ShapeDtypeStruct((1, 256), f32)] * 4,
        compiler_params=_cp(("arbitrary",), VMEM_LIMIT),
    )(z, z, z, dq, dk, dv, cosb, sina, sinb, g_qa, g_kva, wqb, wkvb, g_qn, g_kn)


def _bwd_in(segments, wz, x, g_mix, d2, tm):
    s, d = x.shape
    n_seg = len(segments)

    def body(*refs):
        dz_refs, w_refs = refs[:n_seg], refs[n_seg:2 * n_seg]
        x_ref, g_ref, d2_ref, gx_ref, dg_ref = refs[2 * n_seg:]

        @pl.when(pl.program_id(0) == 0)
        def _():
            dg_ref[...] = jnp.zeros_like(dg_ref)

        dh = _dot_nt(dz_refs[0][...], w_refs[0][...])
        for a_ref, w_ref in zip(dz_refs[1:], w_refs[1:]):
            dh = dh + _dot_nt(a_ref[...], w_ref[...])
        x, g = x_ref[...], g_ref[...]
        r = lax.rsqrt(jnp.sum(x * x, axis=-1, keepdims=True) * (1.0 / d) + EPS)
        dx, dgx = _rms_bwd(dh, x, r, g, d)
        gx_ref[...] = d2_ref[...] + dx
        dg_ref[...] += jnp.sum(dgx, axis=0, keepdims=True)

    rowd = pl.BlockSpec((tm, d), lambda i: (i, 0))
    dz_specs = [pl.BlockSpec((tm, w), functools.partial(lambda i, j: (i, j), j=ja)) for _, w, ja, _ in segments]
    w_specs = [pl.BlockSpec((d, w), functools.partial(lambda i, j: (0, j), j=jw), pipeline_mode=pl.Buffered(1))
               for _, w, _, jw in segments]
    return pl.pallas_call(
        body, name="bwd_in", grid=(s // tm,),
        in_specs=dz_specs + w_specs + [rowd, _const_spec((1, d)), rowd],
        out_specs=[rowd, _acc_spec((1, d))],
        out_shape=[jax.ShapeDtypeStruct((s, d), f32), jax.ShapeDtypeStruct((1, d), f32)],
        compiler_params=_cp(("arbitrary",), VMEM_LIMIT),
    )(*[a for a, _, _, _ in segments], *([wz] * n_seg), x, g_mix, d2)


def _pick_tile(n, cap):
    best = None
    for t in range(LANES, cap + 1, LANES):
        if n % t == 0:
            best = t
    return best if best is not None else n


def _mm_tn_many(a, bs, name, tm):
    kk, m = a.shape
    n_b = len(bs)
    tk = min(1024, kk)
    n_k = kk // tk

    def body(a_ref, *refs):
        b_refs, o_refs, acc_refs = refs[:n_b], refs[n_b:2 * n_b], refs[2 * n_b:]

        @pl.when(pl.program_id(1) == 0)
        def _():
            for acc in acc_refs:
                acc[...] = jnp.zeros_like(acc)
        a_blk = a_ref[...].astype(bf16)
        for b_ref, acc in zip(b_refs, acc_refs):
            acc[...] += _dot_tn(a_blk, b_ref[...].astype(bf16))

        @pl.when(pl.program_id(1) == n_k - 1)
        def _():
            for o_ref, acc in zip(o_refs, acc_refs):
                o_ref[...] = acc[...].astype(bf16)

    return pl.pallas_call(
        body, name=name, grid=(m // tm, n_k),
        in_specs=[pl.BlockSpec((tk, tm), lambda i, k: (k, i))] + [pl.BlockSpec((tk, b.shape[1]), lambda i, k: (k, 0)) for b in bs],
        out_specs=[pl.BlockSpec((tm, b.shape[1]), lambda i, k: (i, 0)) for b in bs],
        out_shape=[jax.ShapeDtypeStruct((m, b.shape[1]), bf16) for b in bs],
        scratch_shapes=[pltpu.VMEM((tm, b.shape[1]), f32) for b in bs],
        compiler_params=_cp(("parallel", "arbitrary"), VMEM_LIMIT),
    )(a, *bs)


def _mm_tn(a, b, name):
    kk, m = a.shape
    _, n = b.shape
    tm = _pick_tile(m, 1408)
    tn = _pick_tile(n, 1408)
    tk = min(1024, kk)

    n_k = kk // tk

    def body(a_ref, b_ref, o_ref, acc_ref):
        @pl.when(pl.program_id(2) == 0)
        def _():
            acc_ref[...] = jnp.zeros_like(acc_ref)
        acc_ref[...] += _dot_tn(a_ref[...].astype(bf16), b_ref[...].astype(bf16))

        @pl.when(pl.program_id(2) == n_k - 1)
        def _():
            o_ref[...] = acc_ref[...].astype(bf16)

    return pl.pallas_call(
        body, name=name, grid=(m // tm, n // tn, n_k),
        in_specs=[pl.BlockSpec((tk, tm), lambda i, j, k: (k, i)), pl.BlockSpec((tk, tn), lambda i, j, k: (k, j))],
        out_specs=pl.BlockSpec((tm, tn), lambda i, j, k: (i, j)),
        out_shape=jax.ShapeDtypeStruct((m, n), bf16),
        scratch_shapes=[pltpu.VMEM((tm, tn), f32)],
        compiler_params=_cp(("parallel", "parallel", "arbitrary"), VMEM_LIMIT),
    )(a, b)


def _rope_tables(positions):
    inv_freq = ROPE_THETA ** (-jnp.arange(0, QK_ROPE, 2, dtype=f32) / QK_ROPE)
    ang = positions.astype(f32)[:, None] * inv_freq
    cos, sin = jnp.cos(ang), jnp.sin(ang)
    zero = jnp.zeros_like(cos)
    return (jnp.concatenate([cos, cos, zero, zero], axis=1), jnp.concatenate([zero, sin, zero, zero], axis=1),
            jnp.concatenate([-sin, zero, zero, zero], axis=1))


def _pad256(g):
    return jnp.pad(g.reshape(1, QK_HEAD), ((0, 0), (0, QK_PAD - QK_HEAD)))


RELAYOUT_BLOCKS = 8
FIRST = ("w_in", "w_qb", "w_kvb", "lb_param")
SECOND = ("w_o", "w_gate", "w_up", "w_down", "w_ple_gate", "w_ple_proj")
ROW_SHARDED = ("w_o", "w_down", "w_ple_gate")


def _col_moves(j):
    lo = BIG["w_in"][1] * j
    w_in = [(max(lo, a) - lo, min(lo + BIG["w_in"][1], b) - lo, d + max(lo, a) - a)
            for a, b, d in Z_SEGMENTS if max(lo, a) < min(lo + BIG["w_in"][1], b)]
    head, half = divmod(j, 2)
    whole = lambda n: [(0, BIG[n][1], BIG[n][1] * j)]
    return {"w_in": w_in, "w_gate": whole("w_gate"), "w_up": whole("w_up"),
            "w_qb": [(0, 96, QK_PAD * head + 96 * half)], "w_kvb": whole("w_kvb"), "w_ple_proj": whole("w_ple_proj"),
            "lb_param": whole("lb_param")}


def _kernel_width(name):
    return {"w_in": Z_W, "w_qb": MLA_HEADS * QK_PAD}.get(name, N_DEV * BIG[name][1])


def _relayout_specs(names, by_dev):
    specs = []
    for n in names:
        rows, cols = BIG[n]
        if n == "lb_param":
            specs.append(_acc_spec((N_DEV, rows, cols) if by_dev else (rows, _kernel_width(n))))
        elif by_dev:
            specs.append(pl.BlockSpec((N_DEV, rows // RELAYOUT_BLOCKS, cols), lambda i: (0, i, 0)))
        else:
            specs.append(pl.BlockSpec((rows // RELAYOUT_BLOCKS, _kernel_width(n)), lambda i: (i, 0)))
    return specs


def _weights_in(gathered, names, name):
    n = len(names)

    def body(*refs):
        ins, outs = dict(zip(names, refs[:n])), dict(zip(names, refs[n:]))
        if "w_in" in outs:
            outs["w_in"][:, Z_KR + QK_ROPE:Z_W] = jnp.zeros((outs["w_in"].shape[0], Z_W - Z_KR - QK_ROPE), bf16)
        if "w_qb" in outs:
            for h in range(MLA_HEADS):
                outs["w_qb"][:, QK_PAD * h + QK_HEAD:QK_PAD * (h + 1)] = jnp.zeros((outs["w_qb"].shape[0], QK_PAD - QK_HEAD), bf16)
        for j in range(N_DEV):
            for wn, moves in _col_moves(j).items():
                if wn in outs:
                    for s0, s1, d0 in moves:
                        outs[wn][:, d0:d0 + s1 - s0] = ins[wn][j, :, s0:s1]

    outs = pl.pallas_call(
        body, name=name, grid=(RELAYOUT_BLOCKS,), in_specs=_relayout_specs(names, True), out_specs=_relayout_specs(names, False),
        out_shape=[jax.ShapeDtypeStruct((BIG[wn][0], _kernel_width(wn)), gathered[wn].dtype) for wn in names],
        compiler_params=_cp(("arbitrary",), VMEM_LIMIT),
    )(*[gathered[wn] for wn in names])
    return dict(zip(names, outs))


def _grads_out(sources, names, name):
    pieces = [(wn, start, arr) for wn in names for start, arr in sources[wn]]
    n_in = len(pieces)

    def body(*refs):
        outs = dict(zip(names, refs[n_in:]))

        def cols(wn, c0, c1):
            for (pn, start, arr), ref in zip(pieces, refs[:n_in]):
                if pn == wn and start <= c0 and c1 <= start + arr.shape[1]:
                    return ref[:, c0 - start:c1 - start]

        for j in range(N_DEV):
            for wn, moves in _col_moves(j).items():
                if wn in outs:
                    for s0, s1, d0 in moves:
                        outs[wn][j, :, s0:s1] = cols(wn, d0, d0 + s1 - s0).astype(bf16)

    in_specs = [_acc_spec(arr.shape) if wn == "lb_param" else pl.BlockSpec((arr.shape[0] // RELAYOUT_BLOCKS, arr.shape[1]), lambda i: (i, 0))
                for wn, _, arr in pieces]
    outs = pl.pallas_call(
        body, name=name, grid=(RELAYOUT_BLOCKS,), in_specs=in_specs, out_specs=_relayout_specs(names, True),
        out_shape=[jax.ShapeDtypeStruct((N_DEV, *BIG[wn]), bf16) for wn in names],
        compiler_params=_cp(("arbitrary",), VMEM_LIMIT),
    )(*[arr for _, _, arr in pieces])
    return dict(zip(names, outs))


def kernel(x, p, positions, g_mix, w_in, g_qa, g_kva, w_qb, w_kvb, g_qn, g_kn, lb_param, g_hgo, w_o, g_ffn, w_gate, w_up, w_down, g_ple, w_ple_gate, w_ple_proj, loss_target, m_g_mix, m_w_in, m_g_qa, m_g_kva, m_w_qb, m_w_kvb, m_g_qn, m_g_kn, m_lb_param, m_g_hgo, m_w_o, m_g_ffn, m_w_gate, m_w_up, m_w_down, m_g_ple, m_w_ple_gate, m_w_ple_proj, v_g_mix, v_w_in, v_g_qa, v_g_kva, v_w_qb, v_w_kvb, v_g_qn, v_g_kn, v_lb_param, v_g_hgo, v_w_o, v_g_ffn, v_w_gate, v_w_up, v_w_down, v_g_ple, v_w_ple_gate, v_w_ple_proj):
    w_all = dict(g_mix=g_mix, g_qa=g_qa, g_kva=g_kva, g_qn=g_qn, g_kn=g_kn, g_hgo=g_hgo, g_ffn=g_ffn, g_ple=g_ple,
                 w_in=w_in, w_qb=w_qb, w_kvb=w_kvb, w_o=w_o, w_gate=w_gate, w_up=w_up, w_down=w_down,
                 w_ple_gate=w_ple_gate, w_ple_proj=w_ple_proj, lb_param=lb_param)
    m_all = dict(g_mix=m_g_mix, g_qa=m_g_qa, g_kva=m_g_kva, g_qn=m_g_qn, g_kn=m_g_kn, g_hgo=m_g_hgo, g_ffn=m_g_ffn,
                 g_ple=m_g_ple, w_in=m_w_in, w_qb=m_w_qb, w_kvb=m_w_kvb, w_o=m_w_o, w_gate=m_w_gate, w_up=m_w_up,
                 w_down=m_w_down, w_ple_gate=m_w_ple_gate, w_ple_proj=m_w_ple_proj, lb_param=m_lb_param)
    v_all = dict(g_mix=v_g_mix, g_qa=v_g_qa, g_kva=v_g_kva, g_qn=v_g_qn, g_kn=v_g_kn, g_hgo=v_g_hgo, g_ffn=v_g_ffn,
                 g_ple=v_g_ple, w_in=v_w_in, w_qb=v_w_qb, w_kvb=v_w_kvb, w_o=v_w_o, w_gate=v_w_gate, w_up=v_w_up,
                 w_down=v_w_down, w_ple_gate=v_w_ple_gate, w_ple_proj=v_w_ple_proj, lb_param=v_lb_param)
    me_idx = jnp.stack([_me()]).astype(jnp.int32)
    x, p, positions, target = x[0], p[0, 0], positions[0], loss_target[0]
    s = x.shape[0]
    tm, tm_ffn, tq_f, tq_b = min(512, s), min(1024, s), min(2048, s), min(1024, s)
    g_mix, g_qa, g_kva, g_qn, g_kn, g_hgo, g_ffn, g_ple = (w_all[n].reshape(1, -1) for n in SMALL)
    g_qn_p, g_kn_p = _pad256(g_qn), _pad256(g_kn)
    cosb, sina, sinb = _rope_tables(positions)
    shard = lambda n: w_all[n].reshape(BIG[n])

    first = _all_gather([shard(n) for n in FIRST], [f32 if n == "lb_param" else bf16 for n in FIRST], "ag_first")
    lands = _cast_to_slot([shard(n) for n in SECOND], me_idx, first[0])
    ag2, token = _exchange_start([], lands, "ag_second_start")
    wk = _weights_in(dict(zip(FIRST, first)), FIRST, "weights_in_first")
    wz, wqb, wkvb, lb4 = (wk[n] for n in FIRST)

    h1, z = _fwd_in(x, g_mix, wz, tm)
    q, k, v = _fwd_mla_proj(z, cosb + token[0, 0], sina, sinb, g_qa, g_kva, wqb, wkvb, g_qn_p, g_kn_p, tm)
    a, a32 = _fwd_attn(q, k, v, tq_f)
    o, gla_b, gla_states = _fwd_gla(z, lb4)

    second = dict(zip(SECOND, _exchange_wait(ag2, [a, o], "ag_second_wait")[1]))
    wk = _weights_in(second, ("w_gate", "w_up", "w_ple_proj"), "weights_in_second")
    w_gate, w_up, w_pp = wk["w_gate"], wk["w_up"], wk["w_ple_proj"]
    w_o, w_down, w_pg = (second[n].reshape(N_DEV * BIG[n][0], BIG[n][1]) for n in ROW_SHARDED)

    x2, cat = _fwd_mix(a, o, z, g_hgo, x, w_o, tm)
    x3, gp, up = _fwd_ffn(x2, g_ffn, w_gate, w_up, w_down, min(256, s))
    d3, h3, dpre, dpp, dg_ple, loss_tile = _ple_loss_fwd_bwd(x3, g_ple, w_pg, p, w_pp, target, tm)
    h2, act, dgp, dup = _bwd_ffn_hidden(d3, x2, gp, up, g_ffn, w_down, tm, D_FF // 2)
    d2, dg_ffn = _bwd_ffn_in(d3, x2, dgp, dup, g_ffn, w_gate, w_up, tm)

    gw_gate, gw_up = _mm_tn_many(h2, [dgp, dup], "dw_gate_up", 512)
    blocks = _grads_out({"w_gate": [(0, gw_gate)], "w_up": [(0, gw_up)], "w_ple_proj": [(0, _mm_tn(p, dpp, "dw_ple_proj"))]},
                        ("w_gate", "w_up", "w_ple_proj"), "grads_out_second")
    row_grads = {"w_o": _mm_tn(cat, d2, "dw_o"), "w_down": _mm_tn(act, d3, "dw_down"), "w_ple_gate": _mm_tn(h3, dpre, "dw_ple_gate")}
    blocks.update({n: g.reshape(N_DEV, *BIG[n]) for n, g in row_grads.items()})
    empty = lambda names: [lax.empty((N_PEERS, *BIG[n]), bf16) for n in names]
    rs2, token = _exchange_start([blocks[n] for n in SECOND], empty(SECOND), "rs_second_start")

    da, do, dz_hg, dg_hgo = _bwd_mix(d2, w_o, o, z, g_hgo + token[0, 0], tm)
    dz_hq, dz_hff, dz_hfb, dz_hi, dlb4 = _bwd_gla(z, lb4, do, gla_b, gla_states)
    dq, dk, dv = _bwd_attn(q, k, v, da, a32, tq_b)
    dz_mla, cqn, ckvn, dq0, dkv0, dg_qa, dg_kva, dg_qn, dg_kn = _bwd_mla_proj(
        z, dq, dk, dv, cosb, sina, sinb, g_qa, g_kva, wqb, wkvb, g_qn_p, g_kn_p, tm)

    gz = list(zip((Z_HQ, Z_HFF, Z_HFB, Z_HI, Z_HG, Z_CQ),
                  _mm_tn_many(h1, [dz_hq, dz_hff, dz_hfb, dz_hi, dz_hg, dz_mla], "dw_in", 1024)))
    blocks1 = _grads_out({"w_in": gz, "w_qb": [(0, _mm_tn(cqn, dq0, "dw_qb"))], "w_kvb": [(0, _mm_tn(ckvn, dkv0, "dw_kvb"))],
                          "lb_param": [(0, dlb4)]}, FIRST, "grads_out_first")
    rs1, token = _exchange_start([blocks1[n] for n in FIRST], empty(FIRST), "rs_first_start")

    result = {}

    def adam(names, lands, src, n_blocks, after=()):
        outs = _adam_shards(me_idx, [src[n] for n in names], lands, [w_all[n] for n in names], [m_all[n] for n in names],
                            [v_all[n] for n in names], n_blocks, "adamw_" + names[0], after)
        result.update(zip(names, outs))
        return outs[0][0]

    blocks2, lands2 = (dict(zip(SECOND, arrs)) for arrs in _exchange_wait(rs2, [token], "rs_second_wait"))
    by8 = tuple(n for n in SECOND if n != "w_down")
    done = [adam(by8, [lands2[n] for n in by8], blocks2, 8), adam(("w_down",), [lands2["w_down"]], blocks2, 2)]

    segments = [(dz_hq, 512, 0, Z_HQ // 512), (dz_hff, 512, 0, Z_HFF // 512), (dz_hfb, 512, 0, Z_HFB // 512),
                (dz_hi, 512, 0, Z_HI // 512), (dz_hg, 512, 0, Z_HG // 512), (dz_mla, 640, 0, Z_CQ // 640)]
    grad_x, dg_mix = _bwd_in(segments, wz, x, g_mix + token[0, 0], d2, tm)
    dgains = (dg_mix, dg_qa, dg_kva, dg_qn, dg_kn, dg_hgo, dg_ffn, dg_ple)

    vec = jnp.concatenate(list(dgains) + [loss_tile[0:1]], axis=1)
    parts = _all_gather([vec], [f32], "ag_gains")[0]
    outs, loss_row = _adam_gains(parts, [w_all[n] for n in SMALL], [m_all[n] for n in SMALL], [v_all[n] for n in SMALL])
    result.update(zip(SMALL, outs))

    blocks1, lands1 = _exchange_wait(rs1, [grad_x, loss_row, *done], "rs_first_wait")
    adam(FIRST, lands1, dict(zip(FIRST, blocks1)), 8)

    order = ("g_mix", "w_in", "g_qa", "g_kva", "w_qb", "w_kvb", "g_qn", "g_kn", "lb_param", "g_hgo", "w_o", "g_ffn",
             "w_gate", "w_up", "w_down", "g_ple", "w_ple_gate", "w_ple_proj")
    return (loss_row[0, 0], grad_x[None], *[result[n][k] for k in range(4) for n in order])
```

```python
import functools
import math

import jax
import jax.numpy as jnp
from jax import lax
from jax.experimental import pallas as pl
from jax.experimental.pallas import tpu as pltpu

f32 = jnp.float32
bf16 = jnp.bfloat16

N_DEV = 8
MLA_HEADS = 4
QK_NOPE = 128
QK_ROPE = 64
QK_HEAD = QK_NOPE + QK_ROPE
QK_PAD = 256
V_HEAD = 128
Q_LORA = 256
KV_LORA = 256
HG_HEADS = 4
CHUNK = 64
D_FF = 2816
PLE_DIM = 256
ROPE_THETA = 10000.0
EPS = 1e-6
ATTN_SCALE = QK_HEAD ** -0.5
LOG2_E = math.log2(math.e)
ATTN_SUB_ROWS = 256
Z_HQ, Z_HFF, Z_HFB, Z_HI, Z_HG, Z_CQ, Z_CKV, Z_KR, Z_W = 0, 512, 1024, 1536, 2048, 2560, 2816, 3072, 3200

ADAM_LR, ADAM_B1, ADAM_B2, ADAM_EPS, ADAM_WD, ADAM_STEP = 0.001, 0.9, 0.999, 1e-08, 0.01, 10

LANES = 128
BIG = {"w_in": (1024, 392), "w_qb": (256, 96), "w_kvb": (256, 128), "w_o": (128, 1024), "w_gate": (352, 1024),
       "w_up": (352, 1024), "w_down": (352, 1024), "w_ple_gate": (128, 1024), "w_ple_proj": (256, 128),
       "lb_param": (4, 64)}
TRANSPOSED = ("w_gate", "w_up")
SMALL = {"g_mix": (0, 1024), "g_qa": (1024, 256), "g_kva": (1280, 256), "g_qn": (1536, 192), "g_kn": (1792, 192),
         "g_hgo": (2048, 512), "g_ffn": (2560, 1024), "g_ple": (3584, 1024)}
LOSS_OFF = 4608
GAIN_VEC = LOSS_OFF + LANES
Z_SEGMENTS = ((0, 256, Z_CQ), (256, 512, Z_CKV), (512, 576, Z_KR), (576, 1088, Z_HQ), (1088, 1600, Z_HFF),
              (1600, 2112, Z_HFB), (2112, 2624, Z_HI), (2624, 3136, Z_HG))

VMEM_LIMIT = 56 * 1024 * 1024
MESH = pl.DeviceIdType.MESH


def _cp(sem=None, vmem=None):
    return pltpu.CompilerParams(dimension_semantics=sem, vmem_limit_bytes=vmem)


def _const_spec(shape):
    nd = len(shape)
    return pl.BlockSpec(shape, lambda *_: (0,) * nd, pipeline_mode=pl.Buffered(1))


def _acc_spec(shape):
    nd = len(shape)
    return pl.BlockSpec(shape, lambda *_: (0,) * nd)


def _sigmoid(x):
    return jax.nn.sigmoid(x)


def _dot(a, b):
    return jnp.dot(a, b, preferred_element_type=f32)


def _dot_nt(a, b):
    return lax.dot_general(a, b, (((1,), (1,)), ((), ())), preferred_element_type=f32)


def _dot_tn(a, b):
    return lax.dot_general(a, b, (((0,), (0,)), ((), ())), preferred_element_type=f32)


def _rms_fwd(x, g, width):
    r = lax.rsqrt(jnp.sum(x * x, axis=-1, keepdims=True) * (1.0 / width) + EPS)
    return x * r * g, r


def _rms_bwd(dy, x, r, g, width):
    u = dy * g
    dx = r * u - x * (r * r * r) * (jnp.sum(u * x, axis=-1, keepdims=True) * (1.0 / width))
    return dx, dy * x * r


class _Both:
    def __init__(self, *copies):
        self.copies = copies

    def start(self):
        for cp in self.copies:
            cp.start()

    def wait(self):
        for cp in self.copies:
            cp.wait()


def _rope(b, c, sa, sb):
    return b * c + pltpu.roll(b, 32, 1) * sa + pltpu.roll(b, 96, 1) * sb


def _all_gather(shards, dtypes, name):
    n = len(shards)

    def body(*refs):
        in_refs, out_refs, stage = refs[:n], refs[n:2 * n], refs[2 * n:3 * n]
        send_sems, recv_sems, local_sems = refs[3 * n:]
        for w in range(n):
            stage[w][...] = in_refs[w][...].astype(stage[w].dtype)
        x, y, c = lax.axis_index("x"), lax.axis_index("y"), lax.axis_index("c")
        me, sibling = (x, y, c), (x, y, 1 - c)
        chips = [(1 - x, y), (x, 1 - y), (1 - x, 1 - y)]

        def slot(w, px, py, pc):
            return out_refs[w].at[4 * px + 2 * py + pc]

        def copy(w, k, block, to, src=None):
            return pltpu.make_async_remote_copy(
                src_ref=slot(w, *block) if src is None else src, dst_ref=slot(w, *block),
                send_sem=send_sems.at[w, k], recv_sem=recv_sems.at[w, k], device_id=to, device_id_type=MESH)

        first = []
        for j, chip in enumerate(chips):
            first += [copy(w, 1 + j, me, (*chip, c), src=stage[w]) for w in range(n)]
        first += [copy(w, 0, me, sibling, src=stage[w]) for w in range(n)]
        mine = [pltpu.make_async_copy(stage[w], slot(w, *me), local_sems.at[w]) for w in range(n)]
        for cp in first + mine:
            cp.start()
        passed = []
        for j, chip in enumerate(chips):
            for w in range(n):
                copy(w, 1 + j, (*chip, c), me).wait_recv()
                passed.append(copy(w, 4 + j, (*chip, c), sibling))
                passed[-1].start()
        for w in range(n):
            copy(w, 0, sibling, me).wait_recv()
        for j, chip in enumerate(chips):
            for w in range(n):
                copy(w, 4 + j, (*chip, 1 - c), me).wait_recv()
        for cp in first + passed:
            cp.wait_send()
        for cp in mine:
            cp.wait()

    return pl.pallas_call(
        body, name=name,
        out_shape=[jax.ShapeDtypeStruct((N_DEV, *s.shape), dt) for s, dt in zip(shards, dtypes)],
        in_specs=[pl.BlockSpec(memory_space=pltpu.VMEM)] * n,
        out_specs=[pl.BlockSpec(memory_space=pl.ANY)] * n,
        scratch_shapes=[pltpu.VMEM(s.shape, dt) for s, dt in zip(shards, dtypes)]
        + [pltpu.SemaphoreType.DMA((n, 7)), pltpu.SemaphoreType.DMA((n, 7)), pltpu.SemaphoreType.DMA((n,))],
        compiler_params=_cp(None, VMEM_LIMIT),
    )(*shards)


N_PEERS = N_DEV - 1
HBM_SPEC = pl.BlockSpec(memory_space=pltpu.HBM)
SEM_SPEC = pl.BlockSpec(memory_space=pltpu.SEMAPHORE)
DATAFLOW = pltpu.SideEffectType.DATAFLOW_SIDE_EFFECTING


def _me():
    return 4 * lax.axis_index("x") + 2 * lax.axis_index("y") + lax.axis_index("c")


def _peer(k):
    x, y, c = lax.axis_index("x"), lax.axis_index("y"), lax.axis_index("c")
    px = 1 - x if k & 4 else x
    py = 1 - y if k & 2 else y
    pc = 1 - c if k & 1 else c
    return (px, py, pc), 4 * px + 2 * py + pc


def _exchange_copies(src_refs, land_refs, send_sems, recv_sems, gather):
    cps = []
    me = _me()
    for k in range(1, N_DEV):
        peer, peer_idx = _peer(k)
        for w, land in enumerate(land_refs):
            src = land.at[me] if gather else src_refs[w].at[peer_idx]
            dst = land.at[me] if gather else land.at[k - 1]
            cps.append(pltpu.make_async_remote_copy(
                src_ref=src, dst_ref=dst, send_sem=send_sems.at[N_PEERS * w + k - 1], recv_sem=recv_sems.at[N_PEERS * w + k - 1],
                device_id=peer, device_id_type=MESH))
    return cps


def _exchange_start(srcs, lands, name):
    n_src, n = len(srcs), len(lands)

    def body(*refs):
        src_refs, land_refs = refs[:n_src], refs[n_src:n_src + n]
        send_sems, recv_sems = refs[n_src + n], refs[n_src + n + 1]
        token = refs[-1]
        for cp in _exchange_copies(src_refs, land_refs, send_sems, recv_sems, gather=not n_src):
            cp.start()
        token[...] = jnp.zeros_like(token)

    arrays = [pltpu.with_memory_space_constraint(a, pltpu.HBM) for a in (*srcs, *lands)]
    outs = pl.pallas_call(
        body, name=name,
        out_shape=(pltpu.SemaphoreType.DMA((n * N_PEERS,)), pltpu.SemaphoreType.DMA((n * N_PEERS,)),
                   *[pltpu.HBM(a.shape, a.dtype) for a in arrays], jax.ShapeDtypeStruct((8, LANES), f32)),
        in_specs=[HBM_SPEC] * len(arrays),
        out_specs=(SEM_SPEC, SEM_SPEC, *[HBM_SPEC] * len(arrays), pl.BlockSpec(memory_space=pltpu.VMEM)),
        input_output_aliases={i: 2 + i for i in range(len(arrays))},
        compiler_params=pltpu.CompilerParams(has_side_effects=DATAFLOW),
    )(*arrays)
    return (outs[0], outs[1], outs[2:2 + n_src], outs[2 + n_src:2 + n_src + n]), outs[-1]


def _exchange_wait(state, after, name):
    send_sems, recv_sems, srcs, lands = state
    n_src, n = len(srcs), len(lands)

    def body(*refs):
        src_refs, land_refs = refs[:n_src], refs[n_src:n_src + n]
        send_ref, recv_ref = refs[n_src + n], refs[n_src + n + 1]
        for cp in _exchange_copies(src_refs, land_refs, send_ref, recv_ref, gather=not n_src):
            cp.wait_send()
            cp.wait_recv()

    arrays = (*srcs, *lands)
    outs = pl.pallas_call(
        body, name=name,
        out_shape=tuple(pltpu.HBM(a.shape, a.dtype) for a in arrays),
        in_specs=[HBM_SPEC] * len(arrays) + [SEM_SPEC, SEM_SPEC] + [pl.BlockSpec(memory_space=pl.ANY)] * len(after),
        out_specs=tuple([HBM_SPEC] * len(arrays)),
        input_output_aliases={i: i for i in range(len(arrays))},
        compiler_params=pltpu.CompilerParams(has_side_effects=DATAFLOW),
    )(*arrays, send_sems, recv_sems, *after)
    return outs[:n_src], outs[n_src:]


def _cast_to_slot(shards, me_idx, after):
    n = len(shards)

    def body(i_ref, *refs):
        for w in range(n):
            refs[n + 1 + w][...] = refs[w][...].astype(bf16)

    return pl.pallas_call(
        body, name="cast_to_slot",
        grid_spec=pltpu.PrefetchScalarGridSpec(
            num_scalar_prefetch=1, grid=(1,),
            in_specs=[pl.BlockSpec(s.shape, lambda i, m: (0, 0)) for s in shards] + [pl.BlockSpec(memory_space=pl.ANY)],
            out_specs=[pl.BlockSpec((None, *s.shape), lambda i, m: (m[0], 0, 0)) for s in shards]),
        out_shape=[jax.ShapeDtypeStruct((N_DEV, *s.shape), bf16) for s in shards],
        compiler_params=_cp(("arbitrary",), VMEM_LIMIT),
    )(me_idx, *shards, after)


def _row_block(rows, n_blocks):
    return (rows // n_blocks, True) if rows % (16 * n_blocks) == 0 else (rows, False)


def _adam_math(w, g, m, v):
    m = ADAM_B1 * m + (1.0 - ADAM_B1) * g
    v = ADAM_B2 * v + (1.0 - ADAM_B2) * (g * g)
    m_hat = m / (1.0 - ADAM_B1 ** ADAM_STEP)
    v_hat = v / (1.0 - ADAM_B2 ** ADAM_STEP)
    delta = -ADAM_LR * (m_hat / (jnp.sqrt(v_hat) + ADAM_EPS) + ADAM_WD * w)
    return delta, m, v


def _adam_shards(me_idx, blocks, lands, ws, ms, vs, n_blocks, name, after=()):
    n = len(blocks)

    def body(i_ref, *refs):
        ins, outs = refs[:5 * n], refs[5 * n + len(after):]
        for w in range(n):
            g_ref, b_ref, w_ref, m_ref, v_ref = (ins[t * n + w] for t in range(5))
            g = g_ref[...].astype(f32)
            for k in range(N_PEERS):
                g = g + b_ref[k].astype(f32)
            if len(w_ref.shape) == 2:
                pieces = [(slice(None), g)]
            else:
                pieces = [(a, g[2 * a:2 * a + 2]) for a in range(2)]
            for at, gp in pieces:
                vals = (gp,) + _adam_math(w_ref[at], gp, m_ref[at], v_ref[at])
                for t, val in enumerate(vals):
                    outs[4 * w + t][at] = val

    specs = [[] for _ in range(5)]
    out_specs, out_shape = [], []
    for g, wt in zip(blocks, ws):
        rows, cols = g.shape[1:]
        rb, cut = _row_block(rows, n_blocks)
        specs[0].append(pl.BlockSpec((None, rb, cols), functools.partial(lambda i, s, cut: (s[0], i if cut else 0, 0), cut=cut)))
        specs[1].append(pl.BlockSpec((N_PEERS, rb, cols), functools.partial(lambda i, s, cut: (0, i if cut else 0, 0), cut=cut)))
        if wt.ndim == 2:
            shard = pl.BlockSpec((rb, cols), functools.partial(lambda i, s, cut: (i if cut else 0, 0), cut=cut))
        elif wt.shape[0] == 1:
            shard = pl.BlockSpec((None, rb, cols), functools.partial(lambda i, s, cut: (0, i if cut else 0, 0), cut=cut))
        else:
            shard = pl.BlockSpec(wt.shape, functools.partial(lambda i, s, nd: (0,) * nd, nd=wt.ndim))
        for t in (2, 3, 4):
            specs[t].append(shard)
        out_specs += [shard] * 4
        out_shape += [jax.ShapeDtypeStruct(wt.shape, f32)] * 4
    outs = pl.pallas_call(
        body, name=name,
        grid_spec=pltpu.PrefetchScalarGridSpec(
            num_scalar_prefetch=1, grid=(n_blocks,), in_specs=sum(specs, []) + [pl.BlockSpec(memory_space=pl.ANY)] * len(after),
            out_specs=out_specs),
        out_shape=out_shape,
        compiler_params=_cp(("arbitrary",), VMEM_LIMIT),
    )(me_idx, *blocks, *lands, *ws, *ms, *vs, *after)
    return [outs[4 * w:4 * w + 4] for w in range(n)]


def _adam_gains(parts, ws, ms, vs):
    n = len(ws)

    def body(p_ref, *refs):
        ins, outs = refs[:3 * n], refs[3 * n:]
        g_all = p_ref[0]
        for k in range(1, N_DEV):
            g_all = g_all + p_ref[k]
        for w, (off, lanes) in enumerate(SMALL.values()):
            w_ref, m_ref, v_ref = ins[w], ins[n + w], ins[2 * n + w]
            if len(w_ref.shape) == 2:
                pieces = [(slice(None), off, lanes)]
            else:
                pieces = [((slice(None), h), off + LANES * h, LANES) for h in range(w_ref.shape[1])]
            for at, o, ln in pieces:
                g = g_all[:, o:o + ln]
                vals = (g,) + _adam_math(w_ref[at], g, m_ref[at], v_ref[at])
                for t, val in enumerate(vals):
                    outs[4 * w + t][at] = val
        outs[4 * n][...] = g_all[:, LOSS_OFF:LOSS_OFF + LANES]

    out_shape = sum([[jax.ShapeDtypeStruct(w.shape, f32)] * 4 for w in ws], []) + [jax.ShapeDtypeStruct((1, LANES), f32)]
    outs = pl.pallas_call(body, name="adamw_gains", out_shape=out_shape)(parts, *ws, *ms, *vs)
    return [outs[4 * w:4 * w + 4] for w in range(n)], outs[4 * n]


def _fwd_in(x, g_mix, wz, tm):
    s, d = x.shape

    def body(x_ref, g_ref, w_ref, h_ref, z_ref):
        h, _ = _rms_fwd(x_ref[...], g_ref[...], d)
        hb = h.astype(bf16)
        h_ref[...] = hb
        z_ref[...] = _dot(hb, w_ref[...])

    return pl.pallas_call(
        body, name="fwd_in", grid=(s // tm,),
        in_specs=[pl.BlockSpec((tm, d), lambda i: (i, 0)), _const_spec((1, d)), _const_spec((d, Z_W))],
        out_specs=[pl.BlockSpec((tm, d), lambda i: (i, 0)), pl.BlockSpec((tm, Z_W), lambda i: (i, 0))],
        out_shape=[jax.ShapeDtypeStruct((s, d), bf16), jax.ShapeDtypeStruct((s, Z_W), f32)],
        compiler_params=_cp(("parallel",), VMEM_LIMIT),
    )(x, g_mix, wz)


def _mla_qk_fwd(cq, ckv, g_qa, g_kva, wqb, wkvb):
    cqn, rq = _rms_fwd(cq, g_qa, Q_LORA)
    ckvn, rkv = _rms_fwd(ckv, g_kva, KV_LORA)
    cqn_b, ckvn_b = cqn.astype(bf16), ckvn.astype(bf16)
    q0 = _dot(cqn_b, wqb)
    kv0 = _dot(ckvn_b, wkvb)
    return cqn_b, rq, ckvn_b, rkv, q0, kv0


def _fwd_mla_proj(z, cosb, sina, sinb, g_qa, g_kva, wqb, wkvb, g_qn, g_kn, tm):
    s = z.shape[0]
    hh = MLA_HEADS

    def body(cq_ref, ckv_ref, kr_ref, c_ref, sa_ref, sb_ref, gqa_ref, gkva_ref, wqb_ref, wkvb_ref, gqn_ref, gkn_ref,
             q_ref, k_ref, v_ref):
        _, _, _, _, q0, kv0 = _mla_qk_fwd(cq_ref[...], ckv_ref[...], gqa_ref[...], gkva_ref[...], wqb_ref[...], wkvb_ref[...])
        kr = kr_ref[...]
        c, sa, sb = c_ref[...], sa_ref[...], sb_ref[...]
        gqn, gkn = gqn_ref[...], gkn_ref[...]
        kr_sq = jnp.sum(kr * kr, axis=-1, keepdims=True)
        for h in range(hh):
            qh = q0[:, QK_PAD * h:QK_PAD * (h + 1)]
            qn, _ = _rms_fwd(qh, gqn, QK_HEAD)
            q_ref[h, :, 0:128] = qn[:, 0:128].astype(bf16)
            q_ref[h, :, 128:256] = _rope(qn[:, 128:256], c, sa, sb).astype(bf16)
            kn_ = kv0[:, 256 * h:256 * h + 128]
            rk = lax.rsqrt((jnp.sum(kn_ * kn_, axis=-1, keepdims=True) + kr_sq) * (1.0 / QK_HEAD) + EPS)
            k_ref[h, :, 0:128] = (kn_ * rk * gkn[:, 0:128]).astype(bf16)
            k_ref[h, :, 128:256] = _rope(kr * rk * gkn[:, 128:256], c, sa, sb).astype(bf16)
            v_ref[h] = kv0[:, 256 * h + 128:256 * h + 256].astype(bf16)

    row128 = pl.BlockSpec((tm, 128), lambda i: (i, 0))
    return pl.pallas_call(
        body, name="fwd_mla_proj", grid=(s // tm,),
        in_specs=[pl.BlockSpec((tm, 256), lambda i: (i, Z_CQ // 256)), pl.BlockSpec((tm, 256), lambda i: (i, Z_CKV // 256)),
                  pl.BlockSpec((tm, 128), lambda i: (i, Z_KR // 128)), row128, row128, row128,
                  _const_spec((1, 256)), _const_spec((1, 256)), _const_spec((256, 1024)), _const_spec((256, 1024)),
                  _const_spec((1, 256)), _const_spec((1, 256))],
        out_specs=[pl.BlockSpec((hh, tm, QK_PAD), lambda i: (0, i, 0)), pl.BlockSpec((hh, tm, QK_PAD), lambda i: (0, i, 0)),
                   pl.BlockSpec((hh, tm, V_HEAD), lambda i: (0, i, 0))],
        out_shape=[jax.ShapeDtypeStruct((hh, s, QK_PAD), bf16), jax.ShapeDtypeStruct((hh, s, QK_PAD), bf16),
                   jax.ShapeDtypeStruct((hh, s, V_HEAD), bf16)],
        compiler_params=_cp(("parallel",), VMEM_LIMIT),
    )(z, z, z, cosb, sina, sinb, g_qa, g_kva, wqb, wkvb, g_qn, g_kn)


def _fwd_attn(q, k, v, tq):
    hh, s, _ = q.shape

    n_sub = max(1, tq // ATTN_SUB_ROWS)

    def body(q_ref, k_ref, v_ref, o_ref, o32_ref):
        for t in range(n_sub):
            rows = slice(t * (tq // n_sub), (t + 1) * (tq // n_sub))
            sc = _dot_nt(q_ref[rows, :], k_ref[...])
            p = jnp.exp2((sc - jnp.max(sc, axis=-1, keepdims=True)) * (ATTN_SCALE * LOG2_E))
            l = jnp.sum(p, axis=-1, keepdims=True)
            o = _dot(p.astype(bf16), v_ref[...]) * (1.0 / l)
            o_ref[rows, :] = o.astype(bf16)
            o32_ref[rows, :] = o

    out = pl.BlockSpec((tq, V_HEAD), lambda h, i: (i, h))
    return pl.pallas_call(
        body, name="fwd_attn", grid=(hh, s // tq),
        in_specs=[pl.BlockSpec((None, tq, QK_PAD), lambda h, i: (h, i, 0)),
                  pl.BlockSpec((None, s, QK_PAD), lambda h, i: (h, 0, 0)),
                  pl.BlockSpec((None, s, V_HEAD), lambda h, i: (h, 0, 0))],
        out_specs=[out, out],
        out_shape=[jax.ShapeDtypeStruct((s, hh * V_HEAD), bf16), jax.ShapeDtypeStruct((s, hh * V_HEAD), f32)],
        compiler_params=_cp(("parallel", "parallel"), VMEM_LIMIT),
    )(q, k, v)


def _split3(x):
    hi = x.astype(bf16)
    r1 = x - hi.astype(f32)
    mid = r1.astype(bf16)
    lo = (r1 - mid.astype(f32)).astype(bf16)
    return jnp.concatenate([hi, mid, lo], axis=-1)


def _tri_sum(tri, x):
    y = _dot(tri, _split3(x))
    return y[:, 0:128] + y[:, 128:256] + y[:, 256:384]


GLA_GROUP = 4
GLA_ROWS = GLA_GROUP * CHUNK
GLA_HEADS_PER_STEP = 2


def _gla_masks(rev):
    row = lax.broadcasted_iota(jnp.int32, (GLA_ROWS, GLA_ROWS), 0)
    col = lax.broadcasted_iota(jnp.int32, (GLA_ROWS, GLA_ROWS), 1)
    shift = CHUNK.bit_length() - 1
    same = (jnp.right_shift(row, shift) == jnp.right_shift(col, shift)).astype(f32)
    lower, upper = (row >= col).astype(f32) * same, (row <= col).astype(f32) * same
    keep, keep_t = (upper, lower) if rev else (lower, upper)
    chunk_of = jnp.right_shift(lax.broadcasted_iota(jnp.int32, (GLA_ROWS, 1), 0), shift)
    return keep, keep.astype(bf16), keep_t.astype(bf16), [(chunk_of == c).astype(f32) for c in range(GLA_GROUP)]


def _gla_gates(hq, hf, lower):
    sg = _sigmoid(hf)
    f = lower + (1.0 - lower) * sg
    return hq * _sigmoid(hq), 1.0 - f, jnp.log(f), f, sg


def _gla_last_mid(b, rev):
    b3 = b.reshape(GLA_GROUP, CHUNK, 128)
    last, mid = (0, CHUNK // 2) if rev else (CHUNK - 1, CHUNK // 2 - 1)
    return b3[:, last:last + 1, :], b3[:, mid:mid + 1, :]


def _gla_per_row(per_chunk):
    return jnp.broadcast_to(per_chunk, (GLA_GROUP, CHUNK, 128)).reshape(GLA_ROWS, 128)


def _gla_block_diag(x, row_masks):
    return jnp.concatenate([(x * m).astype(bf16) for m in row_masks], axis=-1)


def _gla_diag(y):
    return jnp.concatenate([y[CHUNK * c:CHUNK * (c + 1), 128 * c:128 * (c + 1)] for c in range(GLA_GROUP)], axis=0)


def _gla_rows(n, n_groups, rev):
    ne = n_groups - 1 - n if rev else n
    return pl.ds(pl.multiple_of(ne * GLA_ROWS, GLA_ROWS), GLA_ROWS), ne * GLA_GROUP


def _gla_scan_order(rev):
    return tuple(reversed(range(GLA_GROUP))) if rev else tuple(range(GLA_GROUP))


def _fwd_gla(z, lb4):
    s = z.shape[0]
    n_groups = s // GLA_ROWS
    assert n_groups % 2 == 0
    hp = GLA_HEADS_PER_STEP
    chains = [(hh, rev) for hh in range(hp) for rev in (False, True)]

    def body(hq_ref, hff_ref, hfb_ref, hi_ref, lb_ref, o_ref, b_ref, states_ref, st_ref, stage_ref, b_stage, sems):
        st_ref[...] = jnp.zeros_like(st_ref)
        masks = {rev: _gla_masks(rev) for rev in (False, True)}
        lowers = [_sigmoid(lb_ref[int(rev):int(rev) + 1, 128 * hh:128 * (hh + 1)]
                           - lb_ref[2 + int(rev):3 + int(rev), 128 * hh:128 * (hh + 1)]) for hh, rev in chains]

        def states_out(slot, ci, chunk0):
            hh, rev = chains[ci]
            head = pl.program_id(0) * hp + hh
            rows = pl.ds(pl.multiple_of(chunk0 * CHUNK, GLA_ROWS), GLA_ROWS)
            return _Both(
                pltpu.make_async_copy(stage_ref.at[slot, ci], states_ref.at[head, int(rev), pl.ds(chunk0, GLA_GROUP)],
                                      sems.at[slot, ci]),
                pltpu.make_async_copy(b_stage.at[slot, ci], b_ref.at[int(rev), rows, pl.ds(pl.multiple_of(head * 128, 128), 128)],
                                      sems.at[slot, len(chains) + ci]))

        def make_step(first):
            def step(n, carry):
                slot = n % 2

                @pl.when(n >= 2)
                def _():
                    for ci in range(len(chains)):
                        states_out(slot, ci, 0).wait()

                for ci, (hh, rev) in enumerate(chains):
                    cols = slice(128 * hh, 128 * (hh + 1))
                    rows, chunk0 = _gla_rows(n, n_groups, rev)
                    maskf, tri, _, row_masks = masks[rev]
                    hf_ref = hfb_ref if rev else hff_ref
                    q, k, logf, _, _ = _gla_gates(hq_ref[rows, cols], hf_ref[rows, cols], lowers[ci])
                    vb = hi_ref[rows, cols].astype(bf16)
                    b = _tri_sum(tri, logf)
                    b_stage[slot, ci] = b
                    b_last3, b_mid3 = _gla_last_mid(b, rev)
                    b_last, b_mid = _gla_per_row(b_last3), _gla_per_row(b_mid3)
                    qi = (q * jnp.exp(b - b_mid)).astype(bf16)
                    ki = (k * jnp.exp(b_mid - b)).astype(bf16)
                    a = (_dot_nt(qi, ki) * maskf).astype(bf16)
                    kv = _dot_tn(vb, _gla_block_diag(k * jnp.exp(b_last - b), row_masks))
                    decay3 = jnp.exp(b_last3)
                    st = st_ref[ci]
                    before = [None] * GLA_GROUP
                    for c in _gla_scan_order(rev):
                        stage_ref[slot, ci, c] = st
                        before[c] = st.astype(bf16)
                        st = st * decay3[c] + kv[:, 128 * c:128 * (c + 1)]
                    st_ref[ci] = st
                    states_out(slot, ci, chunk0).start()
                    inter = _dot_nt((q * jnp.exp(b)).astype(bf16), jnp.concatenate(before, axis=0))
                    o = _dot(a, vb) + _gla_diag(inter)
                    if first:
                        o_ref[rows, cols] = o
                    else:
                        o_ref[rows, cols] += o
                return carry
            return step

        lax.fori_loop(0, n_groups // 2, make_step(True), 0)
        lax.fori_loop(n_groups // 2, n_groups, make_step(False), 0)
        for slot in range(2):
            for ci in range(len(chains)):
                states_out(slot, ci, 0).wait()

    w = 128 * hp
    col = lambda base: pl.BlockSpec((s, w), lambda h: (0, base // w + h))
    return pl.pallas_call(
        body, name="fwd_gla", grid=(HG_HEADS // hp,),
        in_specs=[col(Z_HQ), col(Z_HFF), col(Z_HFB), col(Z_HI), pl.BlockSpec((4, w), lambda h: (0, h))],
        out_specs=[pl.BlockSpec((s, w), lambda h: (0, h)), pl.BlockSpec(memory_space=pl.ANY), pl.BlockSpec(memory_space=pl.ANY)],
        out_shape=[jax.ShapeDtypeStruct((s, HG_HEADS * 128), f32), jax.ShapeDtypeStruct((2, s, HG_HEADS * 128), f32),
                   jax.ShapeDtypeStruct((HG_HEADS, 2, s // CHUNK, 128, 128), f32)],
        scratch_shapes=[pltpu.VMEM((len(chains), 128, 128), f32), pltpu.VMEM((2, len(chains), GLA_GROUP, 128, 128), f32),
                        pltpu.VMEM((2, len(chains), GLA_ROWS, 128), f32), pltpu.SemaphoreType.DMA((2, 2 * len(chains)))],
        compiler_params=_cp(("parallel",), VMEM_LIMIT),
    )(z, z, z, z, lb4)


def _hg_out(o, hg, g_hgo):
    outs, ons, rs = [], [], []
    for h in range(HG_HEADS):
        oh = o[:, 128 * h:128 * (h + 1)]
        on, r = _rms_fwd(oh, g_hgo[:, 128 * h:128 * (h + 1)], 128)
        ons.append(on)
        rs.append(r)
    on = jnp.concatenate(ons, axis=-1)
    sg = _sigmoid(hg)
    return on * (hg * sg), on, rs, sg


def _fwd_mix(a, o, z, g_hgo, x, w_o, tm):
    s, d = x.shape

    def body(a_ref, o_ref, hg_ref, g_ref, x_ref, w_ref, x2_ref, cat_ref):
        r, _, _, _ = _hg_out(o_ref[...], hg_ref[...], g_ref[...])
        cat = jnp.concatenate([a_ref[...], r.astype(bf16)], axis=-1)
        cat_ref[...] = cat
        x2_ref[...] = x_ref[...] + _dot(cat, w_ref[...])

    row512 = pl.BlockSpec((tm, 512), lambda i: (i, 0))
    rowd = pl.BlockSpec((tm, d), lambda i: (i, 0))
    return pl.pallas_call(
        body, name="fwd_mix", grid=(s // tm,),
        in_specs=[row512, row512, pl.BlockSpec((tm, 512), lambda i: (i, Z_HG // 512)), _const_spec((1, 512)), rowd,
                  _const_spec((d, d))],
        out_specs=[rowd, rowd],
        out_shape=[jax.ShapeDtypeStruct((s, d), f32), jax.ShapeDtypeStruct((s, d), bf16)],
        compiler_params=_cp(("parallel",), VMEM_LIMIT),
    )(a, o, z, g_hgo, x, w_o)


def _fwd_ffn(x2, g_ffn, w_gate, w_up, w_down, tm):
    s, d = x2.shape

    def body(x_ref, g_ref, wg_ref, wu_ref, wd_ref, x3_ref, gp_ref, up_ref):
        x = x_ref[...]
        h, _ = _rms_fwd(x, g_ref[...], d)
        hb = h.astype(bf16)
        gp = _dot_nt(hb, wg_ref[...])
        up = _dot_nt(hb, wu_ref[...])
        gp_ref[...] = gp.astype(bf16)
        up_ref[...] = up.astype(bf16)
        act = (gp * _sigmoid(gp) * up).astype(bf16)
        x3_ref[...] = x + _dot(act, wd_ref[...])

    rowd = pl.BlockSpec((tm, d), lambda i: (i, 0))
    rowf = pl.BlockSpec((tm, D_FF), lambda i: (i, 0))
    return pl.pallas_call(
        body, name="fwd_ffn", grid=(s // tm,),
        in_specs=[rowd, _const_spec((1, d)), _const_spec((D_FF, d)), _const_spec((D_FF, d)), _const_spec((D_FF, d))],
        out_specs=[rowd, rowf, rowf],
        out_shape=[jax.ShapeDtypeStruct((s, d), f32), jax.ShapeDtypeStruct((s, D_FF), bf16),
                   jax.ShapeDtypeStruct((s, D_FF), bf16)],
        compiler_params=_cp(("parallel",), VMEM_LIMIT),
    )(x2, g_ffn, w_gate, w_up, w_down)


def _ple_loss_fwd_bwd(x3, g_ple, w_pg, p, w_pp, target, tm):
    s, d = x3.shape

    def body(x_ref, g_ref, wg_ref, p_ref, wp_ref, t_ref, dx_ref, h_ref, dpre_ref, dpp_ref, dg_ref, loss_ref):
        @pl.when(pl.program_id(0) == 0)
        def _():
            dg_ref[...] = jnp.zeros_like(dg_ref)
            loss_ref[...] = jnp.zeros_like(loss_ref)

        x = x_ref[...]
        g = g_ref[...]
        h, r = _rms_fwd(x, g, d)
        hb = h.astype(bf16)
        gate = _sigmoid(_dot(hb, wg_ref[...]))
        pp = _dot(p_ref[...].astype(bf16), wp_ref[...])
        e = x + gate * pp - t_ref[...]
        loss_ref[...] += 0.5 * jnp.sum(e * e) * (1.0 / d)
        dy = e * (1.0 / d)
        dpre = (dy * pp * gate * (1.0 - gate)).astype(bf16)
        dx, dgx = _rms_bwd(_dot_nt(dpre, wg_ref[...]), x, r, g, d)
        dx_ref[...] = dy + dx
        dg_ref[...] += jnp.sum(dgx, axis=0, keepdims=True)
        h_ref[...] = hb
        dpre_ref[...] = dpre
        dpp_ref[...] = (dy * gate).astype(bf16)

    rowd = pl.BlockSpec((tm, d), lambda i: (i, 0))
    return pl.pallas_call(
        body, name="ple_loss_fwd_bwd", grid=(s // tm,),
        in_specs=[rowd, _const_spec((1, d)), _const_spec((d, d)), pl.BlockSpec((tm, PLE_DIM), lambda i: (i, 0)),
                  _const_spec((PLE_DIM, d)), rowd],
        out_specs=[rowd, rowd, rowd, rowd, _acc_spec((1, d)), _acc_spec((8, 128))],
        out_shape=[jax.ShapeDtypeStruct((s, d), f32), jax.ShapeDtypeStruct((s, d), bf16), jax.ShapeDtypeStruct((s, d), bf16),
                   jax.ShapeDtypeStruct((s, d), bf16), jax.ShapeDtypeStruct((1, d), f32), jax.ShapeDtypeStruct((8, 128), f32)],
        compiler_params=_cp(("arbitrary",), VMEM_LIMIT),
    )(x3, g_ple, w_pg, p, w_pp, target)


def _bwd_ffn_hidden(d3, x2, gp, up, g_ffn, w_down, tm, tf):
    s, d = x2.shape

    def body(d3_ref, x_ref, gp_ref, up_ref, g_ref, wd_ref, h_ref, act_ref, dgp_ref, dup_ref, d3b_ref):
        @pl.when(pl.program_id(1) == 0)
        def _():
            h, _ = _rms_fwd(x_ref[...], g_ref[...], d)
            h_ref[...] = h.astype(bf16)
            d3b_ref[...] = d3_ref[...].astype(bf16)

        gp, up = gp_ref[...].astype(f32), up_ref[...].astype(f32)
        sg = _sigmoid(gp)
        silu = gp * sg
        act_ref[...] = (silu * up).astype(bf16)
        dact = _dot_nt(d3b_ref[...], wd_ref[...])
        dgp_ref[...] = (dact * up * (sg * (1.0 + gp * (1.0 - sg)))).astype(bf16)
        dup_ref[...] = (dact * silu).astype(bf16)

    rowd = pl.BlockSpec((tm, d), lambda i, f: (i, 0))
    rowf = pl.BlockSpec((tm, tf), lambda i, f: (i, f))
    return pl.pallas_call(
        body, name="bwd_ffn_hidden", grid=(s // tm, D_FF // tf),
        in_specs=[rowd, rowd, rowf, rowf, _const_spec((1, d)), pl.BlockSpec((tf, d), lambda i, f: (f, 0))],
        out_specs=[rowd, rowf, rowf, rowf],
        out_shape=[jax.ShapeDtypeStruct((s, d), bf16)] + [jax.ShapeDtypeStruct((s, D_FF), bf16)] * 3,
        scratch_shapes=[pltpu.VMEM((tm, d), bf16)],
        compiler_params=_cp(("parallel", "arbitrary"), VMEM_LIMIT),
    )(d3, x2, gp, up, g_ffn, w_down)


def _bwd_ffn_in(d3, x2, dgp, dup, g_ffn, w_gate, w_up, tm):
    s, d = x2.shape

    def body(d3_ref, x_ref, dgp_ref, dup_ref, g_ref, wg_ref, wu_ref, d2_ref, dg_ref):
        @pl.when(pl.program_id(0) == 0)
        def _():
            dg_ref[...] = jnp.zeros_like(dg_ref)

        x, g = x_ref[...], g_ref[...]
        dh = _dot(dgp_ref[...], wg_ref[...]) + _dot(dup_ref[...], wu_ref[...])
        r = lax.rsqrt(jnp.sum(x * x, axis=-1, keepdims=True) * (1.0 / d) + EPS)
        dx, dgx = _rms_bwd(dh, x, r, g, d)
        d2_ref[...] = d3_ref[...] + dx
        dg_ref[...] += jnp.sum(dgx, axis=0, keepdims=True)

    rowd = pl.BlockSpec((tm, d), lambda i: (i, 0))
    rowf = pl.BlockSpec((tm, D_FF), lambda i: (i, 0))
    return pl.pallas_call(
        body, name="bwd_ffn_in", grid=(s // tm,),
        in_specs=[rowd, rowd, rowf, rowf, _const_spec((1, d)), _const_spec((D_FF, d)), _const_spec((D_FF, d))],
        out_specs=[rowd, _acc_spec((1, d))],
        out_shape=[jax.ShapeDtypeStruct((s, d), f32), jax.ShapeDtypeStruct((1, d), f32)],
        compiler_params=_cp(("arbitrary",), VMEM_LIMIT),
    )(d3, x2, dgp, dup, g_ffn, w_gate, w_up)


def _bwd_mix(d2, w_o, o, z, g_hgo, tm):
    s, d = d2.shape

    def body(d2_ref, w_ref, o_ref, hg_ref, g_ref, da_ref, do_ref, dhg_ref, dg_ref):
        @pl.when(pl.program_id(0) == 0)
        def _():
            dg_ref[...] = jnp.zeros_like(dg_ref)

        dcat = _dot_nt(d2_ref[...].astype(bf16), w_ref[...])
        da_ref[...] = dcat[:, 0:512].astype(bf16)
        dr = dcat[:, 512:1024]
        o, hg, g = o_ref[...], hg_ref[...], g_ref[...]
        _, on, rs, sg = _hg_out(o, hg, g)
        dhg_ref[...] = (dr * on * (sg * (1.0 + hg * (1.0 - sg)))).astype(bf16)
        don = dr * (hg * sg)
        dgs = []
        for h in range(HG_HEADS):
            cols = slice(128 * h, 128 * (h + 1))
            dx, dgx = _rms_bwd(don[:, cols], o[:, cols], rs[h], g[:, cols], 128)
            do_ref[:, cols] = dx
            dgs.append(jnp.sum(dgx, axis=0, keepdims=True))
        dg_ref[...] += jnp.concatenate(dgs, axis=-1)

    row512 = pl.BlockSpec((tm, 512), lambda i: (i, 0))
    return pl.pallas_call(
        body, name="bwd_mix", grid=(s // tm,),
        in_specs=[pl.BlockSpec((tm, d), lambda i: (i, 0)), _const_spec((d, d)), row512,
                  pl.BlockSpec((tm, 512), lambda i: (i, Z_HG // 512)), _const_spec((1, 512))],
        out_specs=[row512, row512, row512, _acc_spec((1, 512))],
        out_shape=[jax.ShapeDtypeStruct((s, 512), bf16), jax.ShapeDtypeStruct((s, 512), f32), jax.ShapeDtypeStruct((s, 512), bf16),
                   jax.ShapeDtypeStruct((1, 512), f32)],
        compiler_params=_cp(("arbitrary",), VMEM_LIMIT),
    )(d2, w_o, o, z, g_hgo)


def _bwd_gla(z, lb4, do, b_fwd, states):
    s = z.shape[0]
    n_chunks = s // CHUNK
    n_groups = s // GLA_ROWS
    assert n_groups % 2 == 0

    def body(hq_ref, hff_ref, hfb_ref, hi_ref, lb_ref, do_ref, b_all, st_all, dhq_ref, dhff_ref, dhfb_ref, dhi_ref, dlb_ref,
             dst_ref, dq_acc, dv_acc, dlow_ref):
        dirs = (False, True)
        masks = [_gla_masks(rev) for rev in dirs]
        lowers = [_sigmoid(lb_ref[int(rev):int(rev) + 1, :] - lb_ref[2 + int(rev):3 + int(rev), :]) for rev in dirs]
        hf_refs, dhf_refs = (hff_ref, hfb_ref), (dhff_ref, dhfb_ref)

        dst_ref[...] = jnp.zeros_like(dst_ref)
        dlow_ref[...] = jnp.zeros_like(dlow_ref)

        def make_bwd_step(first):
            def bwd_step(j, carry):
                n = n_groups - 1 - j
                for d, rev in enumerate(dirs):
                    maskf, _, tri_t, row_masks = masks[d]
                    lower = lowers[d]
                    rows, chunk0 = _gla_rows(n, n_groups, rev)
                    hq, hf = hq_ref[rows, :], hf_refs[d][rows, :]
                    q, k, _, f, sg = _gla_gates(hq, hf, lower)
                    v = hi_ref[rows, :]
                    dout = do_ref[rows, :]
                    b = b_all[d, rows, :]
                    b_last3, b_mid3 = _gla_last_mid(b, rev)
                    b_last, b_mid = _gla_per_row(b_last3), _gla_per_row(b_mid3)
                    e1, e2, e3, e4 = jnp.exp(b - b_mid), jnp.exp(b_mid - b), jnp.exp(b_last - b), jnp.exp(b)
                    decay3 = jnp.exp(b_last3)
                    qi, ki, kt, qt = q * e1, k * e2, k * e3, q * e4
                    qib, kib, ktb = qi.astype(bf16), ki.astype(bf16), kt.astype(bf16)
                    vb, dob = v.astype(bf16), dout.astype(bf16)
                    a = (_dot_nt(qib, kib) * maskf).astype(bf16)
                    da = (_dot_nt(dob, vb) * maskf).astype(bf16)
                    dqi = _dot(da, kib)
                    dki = _dot_tn(da, qib)
                    into_state = _dot_tn(dob, _gla_block_diag(qt, row_masks))
                    dst = dst_ref[d]
                    sts, dsts, ddecay = [None] * GLA_GROUP, [None] * GLA_GROUP, [None] * GLA_GROUP
                    for c in reversed(_gla_scan_order(rev)):
                        sts[c] = st_all[d, chunk0 + c]
                        dsts[c] = dst.astype(bf16)
                        ddecay[c] = jnp.sum(dst * sts[c], axis=0, keepdims=True)[None]
                        dst = dst * decay3[c] + into_state[:, 128 * c:128 * (c + 1)]
                    dst_ref[d] = dst
                    dv = _dot_tn(a, dob) + _gla_diag(_dot_nt(ktb, jnp.concatenate(dsts, axis=0)))
                    dqt = _gla_diag(_dot(dob, jnp.concatenate([x.astype(bf16) for x in sts], axis=-1)))
                    dkt = _gla_diag(_dot(vb, jnp.concatenate(dsts, axis=-1)))
                    dq = dqi * e1 + dqt * e4
                    dk = dki * e2 + dkt * e3
                    db = dqi * qi - dki * ki + dqt * qt - dkt * kt
                    dlast3 = (jnp.sum((dkt * kt).reshape(GLA_GROUP, CHUNK, 128), axis=1, keepdims=True)
                              + jnp.concatenate(ddecay, axis=0) * decay3)
                    dlogf = _tri_sum(tri_t, db) + _gla_per_row(dlast3)
                    df = dlogf / f - dk
                    dhf_refs[d][rows, :] = (df * (1.0 - lower) * sg * (1.0 - sg)).astype(bf16)
                    dlow_ref[d:d + 1, :] += jnp.sum(df * (1.0 - sg), axis=0, keepdims=True)
                    sq = _sigmoid(hq)
                    dhq = dq * (sq * (1.0 + hq * (1.0 - sq)))
                    if first:
                        dq_acc[rows, :] = dhq
                        dv_acc[rows, :] = dv
                    else:
                        dhq_ref[rows, :] = (dq_acc[rows, :] + dhq).astype(bf16)
                        dhi_ref[rows, :] = (dv_acc[rows, :] + dv).astype(bf16)
                return carry
            return bwd_step

        lax.fori_loop(0, n_groups // 2, make_bwd_step(True), 0)
        lax.fori_loop(n_groups // 2, n_groups, make_bwd_step(False), 0)

        for d in range(2):
            dl = dlow_ref[d:d + 1, :] * lowers[d] * (1.0 - lowers[d])
            dlb_ref[d:d + 1, :] = dl
            dlb_ref[2 + d:3 + d, :] = -dl

    col = lambda base: pl.BlockSpec((s, 128), lambda h: (0, base // 128 + h))
    return pl.pallas_call(
        body, name="bwd_gla", grid=(HG_HEADS,),
        in_specs=[col(Z_HQ), col(Z_HFF), col(Z_HFB), col(Z_HI), pl.BlockSpec((4, 128), lambda h: (0, h)), col(0),
                  pl.BlockSpec((2, s, 128), lambda h: (0, 0, h)),
                  pl.BlockSpec((None, 2, n_chunks, 128, 128), lambda h: (h, 0, 0, 0, 0), pipeline_mode=pl.Buffered(1))],
        out_specs=[col(0), col(0), col(0), col(0), pl.BlockSpec((4, 128), lambda h: (0, h))],
        out_shape=[jax.ShapeDtypeStruct((s, 512), bf16)] * 4 + [jax.ShapeDtypeStruct((4, 512), f32)],
        scratch_shapes=[pltpu.VMEM((2, 128, 128), f32), pltpu.VMEM((s, 128), f32), pltpu.VMEM((s, 128), f32),
                        pltpu.VMEM((2, 128), f32)],
        compiler_params=_cp(("parallel",), VMEM_LIMIT),
    )(z, z, z, z, lb4, do, b_fwd, states)


def _bwd_attn(q, k, v, da, a32, tq):
    hh, s, _ = q.shape

    n_sub = max(1, tq // ATTN_SUB_ROWS)

    def body(q_ref, k_ref, v_ref, do_ref, o_ref, dq_ref, dk_ref, dv_ref, p_all, ds_all, dol_ref, dkt_ref, dvt_ref):
        @pl.when(pl.program_id(1) == 0)
        def _():
            dkt_ref[...] = jnp.zeros_like(dkt_ref)
            dvt_ref[...] = jnp.zeros_like(dvt_ref)

        kb, vb = k_ref[...], v_ref[...]
        for t in range(n_sub):
            rows = slice(t * (tq // n_sub), (t + 1) * (tq // n_sub))
            sc = _dot_nt(q_ref[rows, :], kb)
            p = jnp.exp2((sc - jnp.max(sc, axis=-1, keepdims=True)) * (ATTN_SCALE * LOG2_E))
            inv_l = 1.0 / jnp.sum(p, axis=-1, keepdims=True)
            pb = p.astype(bf16)
            dob = do_ref[rows, :]
            dof = dob.astype(f32)
            delta = jnp.sum(dof * o_ref[rows, :], axis=-1, keepdims=True)
            ds = pb * ((_dot_nt(dob, vb) - delta) * inv_l).astype(bf16)
            dq_ref[rows, :] = _dot(ds, kb) * ATTN_SCALE
            p_all[rows, :] = pb
            ds_all[rows, :] = ds
            dol_ref[rows, :] = (dof * inv_l).astype(bf16)
        dkt_ref[...] += _dot_tn(q_ref[...], ds_all[...])
        dvt_ref[...] += _dot_tn(dol_ref[...], p_all[...])

        @pl.when(pl.program_id(1) == s // tq - 1)
        def _():
            dk_ref[...] = dkt_ref[...].T * ATTN_SCALE
            dv_ref[...] = dvt_ref[...].T

    return pl.pallas_call(
        body, name="bwd_attn", grid=(hh, s // tq),
        in_specs=[pl.BlockSpec((None, tq, QK_PAD), lambda h, i: (h, i, 0)),
                  pl.BlockSpec((None, s, QK_PAD), lambda h, i: (h, 0, 0)),
                  pl.BlockSpec((None, s, V_HEAD), lambda h, i: (h, 0, 0)),
                  pl.BlockSpec((tq, V_HEAD), lambda h, i: (i, h)), pl.BlockSpec((tq, V_HEAD), lambda h, i: (i, h))],
        out_specs=[pl.BlockSpec((None, tq, QK_PAD), lambda h, i: (h, i, 0)),
                   pl.BlockSpec((None, s, QK_PAD), lambda h, i: (h, 0, 0)),
                   pl.BlockSpec((None, s, V_HEAD), lambda h, i: (h, 0, 0))],
        out_shape=[jax.ShapeDtypeStruct((hh, s, QK_PAD), f32), jax.ShapeDtypeStruct((hh, s, QK_PAD), f32),
                   jax.ShapeDtypeStruct((hh, s, V_HEAD), f32)],
        scratch_shapes=[pltpu.VMEM((tq, s), bf16), pltpu.VMEM((tq, s), bf16), pltpu.VMEM((tq, V_HEAD), bf16),
                        pltpu.VMEM((QK_PAD, s), f32), pltpu.VMEM((V_HEAD, s), f32)],
        compiler_params=_cp(("parallel", "arbitrary"), VMEM_LIMIT),
    )(q, k, v, da, a32)


def _bwd_mla_proj(z, dq, dk, dv, cosb, sina, sinb, g_qa, g_kva, wqb, wkvb, g_qn, g_kn, tm):
    s = z.shape[0]
    hh = MLA_HEADS

    def body(cq_ref, ckv_ref, kr_ref, dq_ref, dk_ref, dv_ref, c_ref, sa_ref, sb_ref, gqa_ref, gkva_ref, wqb_ref, wkvb_ref,
             gqn_ref, gkn_ref, dz_ref, cqn_ref, ckvn_ref, dq0_ref, dkv0_ref, dgqa_ref, dgkva_ref, dgqn_ref, dgkn_ref):
        @pl.when(pl.program_id(0) == 0)
        def _():
            for r in (dgqa_ref, dgkva_ref, dgqn_ref, dgkn_ref):
                r[...] = jnp.zeros_like(r)

        cq, ckv, kr = cq_ref[...], ckv_ref[...], kr_ref[...]
        gqa, gkva, gqn, gkn = gqa_ref[...], gkva_ref[...], gqn_ref[...], gkn_ref[...]
        cqn_b, rq, ckvn_b, rkv, q0, kv0 = _mla_qk_fwd(cq, ckv, gqa, gkva, wqb_ref[...], wkvb_ref[...])
        cqn_ref[...] = cqn_b
        ckvn_ref[...] = ckvn_b
        c, sa, sb = c_ref[...], -sa_ref[...], -sb_ref[...]
        kr_sq = jnp.sum(kr * kr, axis=-1, keepdims=True)
        dkr = jnp.zeros_like(kr)
        dgqn = jnp.zeros((1, QK_PAD), f32)
        dgkn = jnp.zeros((1, QK_PAD), f32)
        for h in range(hh):
            qh = q0[:, QK_PAD * h:QK_PAD * (h + 1)]
            rh = lax.rsqrt(jnp.sum(qh * qh, axis=-1, keepdims=True) * (1.0 / QK_HEAD) + EPS)
            dqh = dq_ref[h]
            dqn = jnp.concatenate([dqh[:, 0:128], _rope(dqh[:, 128:256], c, sa, sb)], axis=-1)
            dq0h, dgx = _rms_bwd(dqn, qh, rh, gqn, QK_HEAD)
            dq0_ref[:, QK_PAD * h:QK_PAD * (h + 1)] = dq0h.astype(bf16)
            dgqn = dgqn + jnp.sum(dgx, axis=0, keepdims=True)

            kn_ = kv0[:, 256 * h:256 * h + 128]
            k0 = jnp.concatenate([kn_, kr], axis=-1)
            rk = lax.rsqrt((jnp.sum(kn_ * kn_, axis=-1, keepdims=True) + kr_sq) * (1.0 / QK_HEAD) + EPS)
            dkh = dk_ref[h]
            dkn = jnp.concatenate([dkh[:, 0:128], _rope(dkh[:, 128:256], c, sa, sb)], axis=-1)
            dk0, dgx = _rms_bwd(dkn, k0, rk, gkn, QK_HEAD)
            dgkn = dgkn + jnp.sum(dgx, axis=0, keepdims=True)
            dkv0_ref[:, 256 * h:256 * h + 128] = dk0[:, 0:128].astype(bf16)
            dkv0_ref[:, 256 * h + 128:256 * h + 256] = dv_ref[h].astype(bf16)
            dkr = dkr + dk0[:, 128:256]
        dgqn_ref[...] += dgqn
        dgkn_ref[...] += dgkn
        dcq, dgx = _rms_bwd(_dot_nt(dq0_ref[...], wqb_ref[...]), cq, rq, gqa, Q_LORA)
        dgqa_ref[...] += jnp.sum(dgx, axis=0, keepdims=True)
        dckv, dgx = _rms_bwd(_dot_nt(dkv0_ref[...], wkvb_ref[...]), ckv, rkv, gkva, KV_LORA)
        dgkva_ref[...] += jnp.sum(dgx, axis=0, keepdims=True)
        dz_ref[:, 0:256] = dcq.astype(bf16)
        dz_ref[:, 256:512] = dckv.astype(bf16)
        dz_ref[:, 512:640] = dkr.astype(bf16)

    row128 = pl.BlockSpec((tm, 128), lambda i: (i, 0))
    row256 = pl.BlockSpec((tm, 256), lambda i: (i, 0))
    row1024 = pl.BlockSpec((tm, 1024), lambda i: (i, 0))
    hd = lambda w: pl.BlockSpec((hh, tm, w), lambda i: (0, i, 0))
    return pl.pallas_call(
        body, name="bwd_mla_proj", grid=(s // tm,),
        in_specs=[pl.BlockSpec((tm, 256), lambda i: (i, Z_CQ // 256)), pl.BlockSpec((tm, 256), lambda i: (i, Z_CKV // 256)),
                  pl.BlockSpec((tm, 128), lambda i: (i, Z_KR // 128)), hd(QK_PAD), hd(QK_PAD), hd(V_HEAD),
                  row128, row128, row128,
                  _const_spec((1, 256)), _const_spec((1, 256)), _const_spec((256, 1024)), _const_spec((256, 1024)),
                  _const_spec((1, 256)), _const_spec((1, 256))],
        out_specs=[pl.BlockSpec((tm, 640), lambda i: (i, 0)), row256, row256, row1024, row1024,
                   _acc_spec((1, 256)), _acc_spec((1, 256)), _acc_spec((1, 256)), _acc_spec((1, 256))],
        out_shape=[jax.ShapeDtypeStruct((s, 640), bf16), jax.ShapeDtypeStruct((s, 256), bf16), jax.ShapeDtypeStruct((s, 256), bf16),
                   jax.ShapeDtypeStruct((s, 1024), bf16), jax.ShapeDtypeStruct((s, 1024), bf16)]
        + [jax.ShapeDtypeStruct((1, 256), f32)] * 4,
        compiler_params=_cp(("arbitrary",), VMEM_LIMIT),
    )(z, z, z, dq, dk, dv, cosb, sina, sinb, g_qa, g_kva, wqb, wkvb, g_qn, g_kn)


def _bwd_in(segments, wz, x, g_mix, d2, tm):
    s, d = x.shape
    n_seg = len(segments)

    def body(*refs):
        dz_refs, w_refs = refs[:n_seg], refs[n_seg:2 * n_seg]
        x_ref, g_ref, d2_ref, gx_ref, dg_ref = refs[2 * n_seg:]

        @pl.when(pl.program_id(0) == 0)
        def _():
            dg_ref[...] = jnp.zeros_like(dg_ref)

        dh = _dot_nt(dz_refs[0][...], w_refs[0][...])
        for a_ref, w_ref in zip(dz_refs[1:], w_refs[1:]):
            dh = dh + _dot_nt(a_ref[...], w_ref[...])
        x, g = x_ref[...], g_ref[...]
        r = lax.rsqrt(jnp.sum(x * x, axis=-1, keepdims=True) * (1.0 / d) + EPS)
        dx, dgx = _rms_bwd(dh, x, r, g, d)
        gx_ref[...] = d2_ref[...] + dx
        dg_ref[...] += jnp.sum(dgx, axis=0, keepdims=True)

    rowd = pl.BlockSpec((tm, d), lambda i: (i, 0))
    dz_specs = [pl.BlockSpec((tm, w), functools.partial(lambda i, j: (i, j), j=ja)) for _, w, ja, _ in segments]
    w_specs = [pl.BlockSpec((d, w), functools.partial(lambda i, j: (0, j), j=jw), pipeline_mode=pl.Buffered(1))
               for _, w, _, jw in segments]
    return pl.pallas_call(
        body, name="bwd_in", grid=(s // tm,),
        in_specs=dz_specs + w_specs + [rowd, _const_spec((1, d)), rowd],
        out_specs=[rowd, _acc_spec((1, d))],
        out_shape=[jax.ShapeDtypeStruct((s, d), f32), jax.ShapeDtypeStruct((1, d), f32)],
        compiler_params=_cp(("arbitrary",), VMEM_LIMIT),
    )(*[a for a, _, _, _ in segments], *([wz] * n_seg), x, g_mix, d2)


def _pick_tile(n, cap):
    best = None
    for t in range(LANES, cap + 1, LANES):
        if n % t == 0:
            best = t
    return best if best is not None else n


def _mm_tn_many(a, bs, name, tm, transposed=False):
    kk, m = a.shape
    n_b = len(bs)
    tk = min(1024, kk)
    n_k = kk // tk

    def body(a_ref, *refs):
        b_refs, o_refs, acc_refs = refs[:n_b], refs[n_b:2 * n_b], refs[2 * n_b:]

        @pl.when(pl.program_id(1) == 0)
        def _():
            for acc in acc_refs:
                acc[...] = jnp.zeros_like(acc)
        a_blk = a_ref[...].astype(bf16)
        for b_ref, acc in zip(b_refs, acc_refs):
            acc[...] += _dot_tn(a_blk, b_ref[...].astype(bf16))

        @pl.when(pl.program_id(1) == n_k - 1)
        def _():
            for o_ref, acc in zip(o_refs, acc_refs):
                o_ref[...] = (acc[...].T if transposed else acc[...]).astype(bf16)

    if transposed:
        out_specs = [pl.BlockSpec((b.shape[1], tm), lambda i, k: (0, i)) for b in bs]
        out_shape = [jax.ShapeDtypeStruct((b.shape[1], m), bf16) for b in bs]
    else:
        out_specs = [pl.BlockSpec((tm, b.shape[1]), lambda i, k: (i, 0)) for b in bs]
        out_shape = [jax.ShapeDtypeStruct((m, b.shape[1]), bf16) for b in bs]
    return pl.pallas_call(
        body, name=name, grid=(m // tm, n_k),
        in_specs=[pl.BlockSpec((tk, tm), lambda i, k: (k, i))] + [pl.BlockSpec((tk, b.shape[1]), lambda i, k: (k, 0)) for b in bs],
        out_specs=out_specs,
        out_shape=out_shape,
        scratch_shapes=[pltpu.VMEM((tm, b.shape[1]), f32) for b in bs],
        compiler_params=_cp(("parallel", "arbitrary"), VMEM_LIMIT),
    )(a, *bs)


def _mm_tn(a, b, name):
    kk, m = a.shape
    _, n = b.shape
    tm = _pick_tile(m, 1408)
    tn = _pick_tile(n, 1408)
    tk = min(1024, kk)

    n_k = kk // tk

    def body(a_ref, b_ref, o_ref, acc_ref):
        @pl.when(pl.program_id(2) == 0)
        def _():
            acc_ref[...] = jnp.zeros_like(acc_ref)
        acc_ref[...] += _dot_tn(a_ref[...].astype(bf16), b_ref[...].astype(bf16))

        @pl.when(pl.program_id(2) == n_k - 1)
        def _():
            o_ref[...] = acc_ref[...].astype(bf16)

    return pl.pallas_call(
        body, name=name, grid=(m // tm, n // tn, n_k),
        in_specs=[pl.BlockSpec((tk, tm), lambda i, j, k: (k, i)), pl.BlockSpec((tk, tn), lambda i, j, k: (k, j))],
        out_specs=pl.BlockSpec((tm, tn), lambda i, j, k: (i, j)),
        out_shape=jax.ShapeDtypeStruct((m, n), bf16),
        scratch_shapes=[pltpu.VMEM((tm, tn), f32)],
        compiler_params=_cp(("parallel", "parallel", "arbitrary"), VMEM_LIMIT),
    )(a, b)


def _rope_tables(positions):
    inv_freq = ROPE_THETA ** (-jnp.arange(0, QK_ROPE, 2, dtype=f32) / QK_ROPE)
    ang = positions.astype(f32)[:, None] * inv_freq
    cos, sin = jnp.cos(ang), jnp.sin(ang)
    zero = jnp.zeros_like(cos)
    return (jnp.concatenate([cos, cos, zero, zero], axis=1), jnp.concatenate([zero, sin, zero, zero], axis=1),
            jnp.concatenate([-sin, zero, zero, zero], axis=1))


def _pad256(g):
    return jnp.pad(g.reshape(1, QK_HEAD), ((0, 0), (0, QK_PAD - QK_HEAD)))


RELAYOUT_BLOCKS = 8
FIRST = ("w_in", "w_qb", "w_kvb", "lb_param")
SECOND = ("w_o", "w_gate", "w_up", "w_down", "w_ple_gate", "w_ple_proj")
ROW_SHARDED = ("w_o", "w_down", "w_ple_gate")


def _col_moves(j):
    lo = BIG["w_in"][1] * j
    w_in = [(max(lo, a) - lo, min(lo + BIG["w_in"][1], b) - lo, d + max(lo, a) - a)
            for a, b, d in Z_SEGMENTS if max(lo, a) < min(lo + BIG["w_in"][1], b)]
    head, half = divmod(j, 2)
    whole = lambda n: [(0, BIG[n][1], BIG[n][1] * j)]
    return {"w_in": w_in, "w_qb": [(0, 96, QK_PAD * head + 96 * half)], "w_kvb": whole("w_kvb"),
            "w_ple_proj": whole("w_ple_proj"), "lb_param": whole("lb_param")}


def _kernel_width(name):
    return {"w_in": Z_W, "w_qb": MLA_HEADS * QK_PAD}.get(name, N_DEV * BIG[name][1])


def _relayout_specs(names, by_dev):
    specs = []
    for n in names:
        rows, cols = BIG[n]
        if n == "lb_param":
            specs.append(_acc_spec((N_DEV, rows, cols) if by_dev else (rows, _kernel_width(n))))
        elif by_dev:
            specs.append(pl.BlockSpec((N_DEV, rows // RELAYOUT_BLOCKS, cols), lambda i: (0, i, 0)))
        else:
            specs.append(pl.BlockSpec((rows // RELAYOUT_BLOCKS, _kernel_width(n)), lambda i: (i, 0)))
    return specs


def _weights_in(gathered, names, name):
    n = len(names)

    def body(*refs):
        ins, outs = dict(zip(names, refs[:n])), dict(zip(names, refs[n:]))
        if "w_in" in outs:
            outs["w_in"][:, Z_KR + QK_ROPE:Z_W] = jnp.zeros((outs["w_in"].shape[0], Z_W - Z_KR - QK_ROPE), bf16)
        if "w_qb" in outs:
            for h in range(MLA_HEADS):
                outs["w_qb"][:, QK_PAD * h + QK_HEAD:QK_PAD * (h + 1)] = jnp.zeros((outs["w_qb"].shape[0], QK_PAD - QK_HEAD), bf16)
        for j in range(N_DEV):
            for wn, moves in _col_moves(j).items():
                if wn in outs:
                    for s0, s1, d0 in moves:
                        outs[wn][:, d0:d0 + s1 - s0] = ins[wn][j, :, s0:s1]

    outs = pl.pallas_call(
        body, name=name, grid=(RELAYOUT_BLOCKS,), in_specs=_relayout_specs(names, True), out_specs=_relayout_specs(names, False),
        out_shape=[jax.ShapeDtypeStruct((BIG[wn][0], _kernel_width(wn)), gathered[wn].dtype) for wn in names],
        compiler_params=_cp(("arbitrary",), VMEM_LIMIT),
    )(*[gathered[wn] for wn in names])
    return dict(zip(names, outs))


def _grads_out(sources, names, name):
    pieces = [(wn, start, arr) for wn in names for start, arr in sources[wn]]
    n_in = len(pieces)

    def body(*refs):
        outs = dict(zip(names, refs[n_in:]))

        def cols(wn, c0, c1):
            for (pn, start, arr), ref in zip(pieces, refs[:n_in]):
                if pn == wn and start <= c0 and c1 <= start + arr.shape[1]:
                    return ref[:, c0 - start:c1 - start]

        for j in range(N_DEV):
            for wn, moves in _col_moves(j).items():
                if wn in outs:
                    for s0, s1, d0 in moves:
                        outs[wn][j, :, s0:s1] = cols(wn, d0, d0 + s1 - s0).astype(bf16)

    in_specs = [_acc_spec(arr.shape) if wn == "lb_param" else pl.BlockSpec((arr.shape[0] // RELAYOUT_BLOCKS, arr.shape[1]), lambda i: (i, 0))
                for wn, _, arr in pieces]
    outs = pl.pallas_call(
        body, name=name, grid=(RELAYOUT_BLOCKS,), in_specs=in_specs, out_specs=_relayout_specs(names, True),
        out_shape=[jax.ShapeDtypeStruct((N_DEV, *BIG[wn]), bf16) for wn in names],
        compiler_params=_cp(("arbitrary",), VMEM_LIMIT),
    )(*[arr for _, _, arr in pieces])
    return dict(zip(names, outs))


def kernel(x, p, positions, g_mix, w_in, g_qa, g_kva, w_qb, w_kvb, g_qn, g_kn, lb_param, g_hgo, w_o, g_ffn, w_gate, w_up, w_down, g_ple, w_ple_gate, w_ple_proj, loss_target, m_g_mix, m_w_in, m_g_qa, m_g_kva, m_w_qb, m_w_kvb, m_g_qn, m_g_kn, m_lb_param, m_g_hgo, m_w_o, m_g_ffn, m_w_gate, m_w_up, m_w_down, m_g_ple, m_w_ple_gate, m_w_ple_proj, v_g_mix, v_w_in, v_g_qa, v_g_kva, v_w_qb, v_w_kvb, v_g_qn, v_g_kn, v_lb_param, v_g_hgo, v_w_o, v_g_ffn, v_w_gate, v_w_up, v_w_down, v_g_ple, v_w_ple_gate, v_w_ple_proj):
    w_all = dict(g_mix=g_mix, g_qa=g_qa, g_kva=g_kva, g_qn=g_qn, g_kn=g_kn, g_hgo=g_hgo, g_ffn=g_ffn, g_ple=g_ple,
                 w_in=w_in, w_qb=w_qb, w_kvb=w_kvb, w_o=w_o, w_gate=w_gate, w_up=w_up, w_down=w_down,
                 w_ple_gate=w_ple_gate, w_ple_proj=w_ple_proj, lb_param=lb_param)
    m_all = dict(g_mix=m_g_mix, g_qa=m_g_qa, g_kva=m_g_kva, g_qn=m_g_qn, g_kn=m_g_kn, g_hgo=m_g_hgo, g_ffn=m_g_ffn,
                 g_ple=m_g_ple, w_in=m_w_in, w_qb=m_w_qb, w_kvb=m_w_kvb, w_o=m_w_o, w_gate=m_w_gate, w_up=m_w_up,
                 w_down=m_w_down, w_ple_gate=m_w_ple_gate, w_ple_proj=m_w_ple_proj, lb_param=m_lb_param)
    v_all = dict(g_mix=v_g_mix, g_qa=v_g_qa, g_kva=v_g_kva, g_qn=v_g_qn, g_kn=v_g_kn, g_hgo=v_g_hgo, g_ffn=v_g_ffn,
                 g_ple=v_g_ple, w_in=v_w_in, w_qb=v_w_qb, w_kvb=v_w_kvb, w_o=v_w_o, w_gate=v_w_gate, w_up=v_w_up,
                 w_down=v_w_down, w_ple_gate=v_w_ple_gate, w_ple_proj=v_w_ple_proj, lb_param=v_lb_param)
    me_idx = jnp.stack([_me()]).astype(jnp.int32)
    x, p, positions, target = x[0], p[0, 0], positions[0], loss_target[0]
    s = x.shape[0]
    tm, tm_ffn, tq_f, tq_b = min(512, s), min(1024, s), min(2048, s), min(1024, s)
    g_mix, g_qa, g_kva, g_qn, g_kn, g_hgo, g_ffn, g_ple = (w_all[n].reshape(1, -1) for n in SMALL)
    g_qn_p, g_kn_p = _pad256(g_qn), _pad256(g_kn)
    cosb, sina, sinb = _rope_tables(positions)
    as_shard = lambda n, a: a[0].T if n in TRANSPOSED else a.reshape(BIG[n])
    shard = lambda n: as_shard(n, w_all[n])

    first = _all_gather([shard(n) for n in FIRST], [f32 if n == "lb_param" else bf16 for n in FIRST], "ag_first")
    lands = _cast_to_slot([shard(n) for n in SECOND], me_idx, first[0])
    ag2, token = _exchange_start([], lands, "ag_second_start")
    wk = _weights_in(dict(zip(FIRST, first)), FIRST, "weights_in_first")
    wz, wqb, wkvb, lb4 = (wk[n] for n in FIRST)

    h1, z = _fwd_in(x, g_mix, wz, tm)
    q, k, v = _fwd_mla_proj(z, cosb + token[0, 0], sina, sinb, g_qa, g_kva, wqb, wkvb, g_qn_p, g_kn_p, tm)
    a, a32 = _fwd_attn(q, k, v, tq_f)
    o, gla_b, gla_states = _fwd_gla(z, lb4)

    second = dict(zip(SECOND, _exchange_wait(ag2, [a, o], "ag_second_wait")[1]))
    w_pp = _weights_in(second, ("w_ple_proj",), "weights_in_second")["w_ple_proj"]
    w_o, w_down, w_pg, w_gate, w_up = (second[n].reshape(N_DEV * BIG[n][0], BIG[n][1]) for n in ROW_SHARDED + TRANSPOSED)

    x2, cat = _fwd_mix(a, o, z, g_hgo, x, w_o, tm)
    x3, gp, up = _fwd_ffn(x2, g_ffn, w_gate, w_up, w_down, min(256, s))
    d3, h3, dpre, dpp, dg_ple, loss_tile = _ple_loss_fwd_bwd(x3, g_ple, w_pg, p, w_pp, target, tm)
    h2, act, dgp, dup = _bwd_ffn_hidden(d3, x2, gp, up, g_ffn, w_down, tm, D_FF // 2)
    d2, dg_ffn = _bwd_ffn_in(d3, x2, dgp, dup, g_ffn, w_gate, w_up, tm)

    gw_gate, gw_up = _mm_tn_many(h2, [dgp, dup], "dw_gate_up", 512, transposed=True)
    blocks = _grads_out({"w_ple_proj": [(0, _mm_tn(p, dpp, "dw_ple_proj"))]}, ("w_ple_proj",), "grads_out_second")
    row_grads = {"w_o": _mm_tn(cat, d2, "dw_o"), "w_down": _mm_tn(act, d3, "dw_down"), "w_ple_gate": _mm_tn(h3, dpre, "dw_ple_gate"),
                 "w_gate": gw_gate, "w_up": gw_up}
    blocks.update({n: g.reshape(N_DEV, *BIG[n]) for n, g in row_grads.items()})
    empty = lambda names: [lax.empty((N_PEERS, *BIG[n]), bf16) for n in names]
    rs2, token = _exchange_start([blocks[n] for n in SECOND], empty(SECOND), "rs_second_start")

    da, do, dz_hg, dg_hgo = _bwd_mix(d2, w_o, o, z, g_hgo + token[0, 0], tm)
    dz_hq, dz_hff, dz_hfb, dz_hi, dlb4 = _bwd_gla(z, lb4, do, gla_b, gla_states)
    dq, dk, dv = _bwd_attn(q, k, v, da, a32, tq_b)
    dz_mla, cqn, ckvn, dq0, dkv0, dg_qa, dg_kva, dg_qn, dg_kn = _bwd_mla_proj(
        z, dq, dk, dv, cosb, sina, sinb, g_qa, g_kva, wqb, wkvb, g_qn_p, g_kn_p, tm)

    gz = list(zip((Z_HQ, Z_HFF, Z_HFB, Z_HI, Z_HG, Z_CQ),
                  _mm_tn_many(h1, [dz_hq, dz_hff, dz_hfb, dz_hi, dz_hg, dz_mla], "dw_in", 1024)))
    blocks1 = _grads_out({"w_in": gz, "w_qb": [(0, _mm_tn(cqn, dq0, "dw_qb"))], "w_kvb": [(0, _mm_tn(ckvn, dkv0, "dw_kvb"))],
                          "lb_param": [(0, dlb4)]}, FIRST, "grads_out_first")
    rs1, token = _exchange_start([blocks1[n] for n in FIRST], empty(FIRST), "rs_first_start")

    result = {}

    def adam(names, lands, src, n_blocks, after=()):
        given = lambda arrs: [arrs[n][0].T if n in TRANSPOSED else arrs[n] for n in names]
        outs = _adam_shards(me_idx, [src[n] for n in names], lands, given(w_all), given(m_all), given(v_all), n_blocks,
                            "adamw_" + names[0], after)
        for n, o in zip(names, outs):
            result[n] = [t.T[None] for t in o] if n in TRANSPOSED else o
        return outs[0][0]

    blocks2, lands2 = (dict(zip(SECOND, arrs)) for arrs in _exchange_wait(rs2, [token], "rs_second_wait"))
    by2 = ("w_down",) + TRANSPOSED
    by8 = tuple(n for n in SECOND if n not in by2)
    done = [adam(by8, [lands2[n] for n in by8], blocks2, 8), adam(by2, [lands2[n] for n in by2], blocks2, 2)]

    segments = [(dz_hq, 512, 0, Z_HQ // 512), (dz_hff, 512, 0, Z_HFF // 512), (dz_hfb, 512, 0, Z_HFB // 512),
                (dz_hi, 512, 0, Z_HI // 512), (dz_hg, 512, 0, Z_HG // 512), (dz_mla, 640, 0, Z_CQ // 640)]
    grad_x, dg_mix = _bwd_in(segments, wz, x, g_mix + token[0, 0], d2, tm)
    dgains = (dg_mix, dg_qa, dg_kva, dg_qn, dg_kn, dg_hgo, dg_ffn, dg_ple)

    vec = jnp.concatenate(list(dgains) + [loss_tile[0:1]], axis=1)
    parts = _all_gather([vec], [f32], "ag_gains")[0]
    outs, loss_row = _adam_gains(parts, [w_all[n] for n in SMALL], [m_all[n] for n in SMALL], [v_all[n] for n in SMALL])
    result.update(zip(SMALL, outs))

    blocks1, lands1 = _exchange_wait(rs1, [grad_x, loss_row, *done], "rs_first_wait")
    adam(FIRST, lands1, dict(zip(FIRST, blocks1)), 8)

    order = ("g_mix", "w_in", "g_qa", "g_kva", "w_qb", "w_kvb", "g_qn", "g_kn", "lb_param", "g_hgo", "w_o", "g_ffn",
             "w_gate", "w_up", "w_down", "g_ple", "w_ple_gate", "w_ple_proj")
    return (loss_row[0, 0], grad_x[None], *[result[n][k] for k in range(4) for n in order])
```

```python
import functools
import math

import jax
import jax.numpy as jnp
from jax import lax
from jax.experimental import pallas as pl
from jax.experimental.pallas import tpu as pltpu

f32 = jnp.float32
bf16 = jnp.bfloat16

N_DEV = 8
MLA_HEADS = 4
QK_NOPE = 128
QK_ROPE = 64
QK_HEAD = QK_NOPE + QK_ROPE
QK_PAD = 256
V_HEAD = 128
Q_LORA = 256
KV_LORA = 256
HG_HEADS = 4
CHUNK = 64
D_FF = 2816
PLE_DIM = 256
ROPE_THETA = 10000.0
EPS = 1e-6
ATTN_SCALE = QK_HEAD ** -0.5
LOG2_E = math.log2(math.e)
ATTN_SUB_ROWS = 256
Z_HQ, Z_HFF, Z_HFB, Z_HI, Z_HG, Z_CQ, Z_CKV, Z_KR, Z_W = 0, 512, 1024, 1536, 2048, 2560, 2816, 3072, 3200

ADAM_LR, ADAM_B1, ADAM_B2, ADAM_EPS, ADAM_WD, ADAM_STEP = 0.001, 0.9, 0.999, 1e-08, 0.01, 10

LANES = 128
BIG = {"w_in": (1024, 392), "w_qb": (256, 96), "w_kvb": (256, 128), "w_o": (128, 1024), "w_gate": (352, 1024),
       "w_up": (352, 1024), "w_down": (352, 1024), "w_ple_gate": (128, 1024), "w_ple_proj": (256, 128),
       "lb_param": (4, 64)}
TRANSPOSED = ("w_gate", "w_up")
SMALL = {"g_mix": (0, 1024), "g_qa": (1024, 256), "g_kva": (1280, 256), "g_qn": (1536, 192), "g_kn": (1792, 192),
         "g_hgo": (2048, 512), "g_ffn": (2560, 1024), "g_ple": (3584, 1024)}
LOSS_OFF = 4608
GAIN_VEC = LOSS_OFF + LANES
Z_SEGMENTS = ((0, 256, Z_CQ), (256, 512, Z_CKV), (512, 576, Z_KR), (576, 1088, Z_HQ), (1088, 1600, Z_HFF),
              (1600, 2112, Z_HFB), (2112, 2624, Z_HI), (2624, 3136, Z_HG))

VMEM_LIMIT = 56 * 1024 * 1024
MESH = pl.DeviceIdType.MESH


def _cp(sem=None, vmem=None):
    return pltpu.CompilerParams(dimension_semantics=sem, vmem_limit_bytes=vmem)


def _const_spec(shape):
    nd = len(shape)
    return pl.BlockSpec(shape, lambda *_: (0,) * nd, pipeline_mode=pl.Buffered(1))


def _acc_spec(shape):
    nd = len(shape)
    return pl.BlockSpec(shape, lambda *_: (0,) * nd)


def _sigmoid(x):
    return jax.nn.sigmoid(x)


def _dot(a, b):
    return jnp.dot(a, b, preferred_element_type=f32)


def _dot_nt(a, b):
    return lax.dot_general(a, b, (((1,), (1,)), ((), ())), preferred_element_type=f32)


def _dot_tn(a, b):
    return lax.dot_general(a, b, (((0,), (0,)), ((), ())), preferred_element_type=f32)


def _rms_fwd(x, g, width):
    r = lax.rsqrt(jnp.sum(x * x, axis=-1, keepdims=True) * (1.0 / width) + EPS)
    return x * r * g, r


def _rms_bwd(dy, x, r, g, width):
    u = dy * g
    dx = r * u - x * (r * r * r) * (jnp.sum(u * x, axis=-1, keepdims=True) * (1.0 / width))
    return dx, dy * x * r


class _Both:
    def __init__(self, *copies):
        self.copies = copies

    def start(self):
        for cp in self.copies:
            cp.start()

    def wait(self):
        for cp in self.copies:
            cp.wait()


def _rope(b, c, sa, sb):
    return b * c + pltpu.roll(b, 32, 1) * sa + pltpu.roll(b, 96, 1) * sb


def _all_gather(shards, dtypes, name):
    n = len(shards)

    def body(*refs):
        in_refs, out_refs, stage = refs[:n], refs[n:2 * n], refs[2 * n:3 * n]
        send_sems, recv_sems, local_sems = refs[3 * n:]
        for w in range(n):
            stage[w][...] = in_refs[w][...].astype(stage[w].dtype)
        x, y, c = lax.axis_index("x"), lax.axis_index("y"), lax.axis_index("c")
        me, sibling = (x, y, c), (x, y, 1 - c)
        chips = [(1 - x, y), (x, 1 - y), (1 - x, 1 - y)]

        def slot(w, px, py, pc):
            return out_refs[w].at[4 * px + 2 * py + pc]

        def copy(w, k, block, to, src=None):
            return pltpu.make_async_remote_copy(
                src_ref=slot(w, *block) if src is None else src, dst_ref=slot(w, *block),
                send_sem=send_sems.at[w, k], recv_sem=recv_sems.at[w, k], device_id=to, device_id_type=MESH)

        first = []
        for j, chip in enumerate(chips):
            first += [copy(w, 1 + j, me, (*chip, c), src=stage[w]) for w in range(n)]
        first += [copy(w, 0, me, sibling, src=stage[w]) for w in range(n)]
        mine = [pltpu.make_async_copy(stage[w], slot(w, *me), local_sems.at[w]) for w in range(n)]
        for cp in first + mine:
            cp.start()
        passed = []
        for j, chip in enumerate(chips):
            for w in range(n):
                copy(w, 1 + j, (*chip, c), me).wait_recv()
                passed.append(copy(w, 4 + j, (*chip, c), sibling))
                passed[-1].start()
        for w in range(n):
            copy(w, 0, sibling, me).wait_recv()
        for j, chip in enumerate(chips):
            for w in range(n):
                copy(w, 4 + j, (*chip, 1 - c), me).wait_recv()
        for cp in first + passed:
            cp.wait_send()
        for cp in mine:
            cp.wait()

    return pl.pallas_call(
        body, name=name,
        out_shape=[jax.ShapeDtypeStruct((N_DEV, *s.shape), dt) for s, dt in zip(shards, dtypes)],
        in_specs=[pl.BlockSpec(memory_space=pltpu.VMEM)] * n,
        out_specs=[pl.BlockSpec(memory_space=pl.ANY)] * n,
        scratch_shapes=[pltpu.VMEM(s.shape, dt) for s, dt in zip(shards, dtypes)]
        + [pltpu.SemaphoreType.DMA((n, 7)), pltpu.SemaphoreType.DMA((n, 7)), pltpu.SemaphoreType.DMA((n,))],
        compiler_params=_cp(None, VMEM_LIMIT),
    )(*shards)


N_PEERS = N_DEV - 1
HBM_SPEC = pl.BlockSpec(memory_space=pltpu.HBM)
SEM_SPEC = pl.BlockSpec(memory_space=pltpu.SEMAPHORE)
DATAFLOW = pltpu.SideEffectType.DATAFLOW_SIDE_EFFECTING


def _me():
    return 4 * lax.axis_index("x") + 2 * lax.axis_index("y") + lax.axis_index("c")


def _peer(k):
    x, y, c = lax.axis_index("x"), lax.axis_index("y"), lax.axis_index("c")
    px = 1 - x if k & 4 else x
    py = 1 - y if k & 2 else y
    pc = 1 - c if k & 1 else c
    return (px, py, pc), 4 * px + 2 * py + pc


def _exchange_copies(src_refs, land_refs, send_sems, recv_sems, gather):
    cps = []
    me = _me()
    for k in range(1, N_DEV):
        peer, peer_idx = _peer(k)
        for w, land in enumerate(land_refs):
            src = land.at[me] if gather else src_refs[w].at[peer_idx]
            dst = land.at[me] if gather else land.at[k - 1]
            cps.append(pltpu.make_async_remote_copy(
                src_ref=src, dst_ref=dst, send_sem=send_sems.at[N_PEERS * w + k - 1], recv_sem=recv_sems.at[N_PEERS * w + k - 1],
                device_id=peer, device_id_type=MESH))
    return cps


def _exchange_start(srcs, lands, name):
    n_src, n = len(srcs), len(lands)

    def body(*refs):
        src_refs, land_refs = refs[:n_src], refs[n_src:n_src + n]
        send_sems, recv_sems = refs[n_src + n], refs[n_src + n + 1]
        token = refs[-1]
        for cp in _exchange_copies(src_refs, land_refs, send_sems, recv_sems, gather=not n_src):
            cp.start()
        token[...] = jnp.zeros_like(token)

    arrays = [pltpu.with_memory_space_constraint(a, pltpu.HBM) for a in (*srcs, *lands)]
    outs = pl.pallas_call(
        body, name=name,
        out_shape=(pltpu.SemaphoreType.DMA((n * N_PEERS,)), pltpu.SemaphoreType.DMA((n * N_PEERS,)),
                   *[pltpu.HBM(a.shape, a.dtype) for a in arrays], jax.ShapeDtypeStruct((8, LANES), f32)),
        in_specs=[HBM_SPEC] * len(arrays),
        out_specs=(SEM_SPEC, SEM_SPEC, *[HBM_SPEC] * len(arrays), pl.BlockSpec(memory_space=pltpu.VMEM)),
        input_output_aliases={i: 2 + i for i in range(len(arrays))},
        compiler_params=pltpu.CompilerParams(has_side_effects=DATAFLOW),
    )(*arrays)
    return (outs[0], outs[1], outs[2:2 + n_src], outs[2 + n_src:2 + n_src + n]), outs[-1]


def _exchange_wait(state, after, name):
    send_sems, recv_sems, srcs, lands = state
    n_src, n = len(srcs), len(lands)

    def body(*refs):
        src_refs, land_refs = refs[:n_src], refs[n_src:n_src + n]
        send_ref, recv_ref = refs[n_src + n], refs[n_src + n + 1]
        for cp in _exchange_copies(src_refs, land_refs, send_ref, recv_ref, gather=not n_src):
            cp.wait_send()
            cp.wait_recv()

    arrays = (*srcs, *lands)
    outs = pl.pallas_call(
        body, name=name,
        out_shape=tuple(pltpu.HBM(a.shape, a.dtype) for a in arrays),
        in_specs=[HBM_SPEC] * len(arrays) + [SEM_SPEC, SEM_SPEC] + [pl.BlockSpec(memory_space=pl.ANY)] * len(after),
        out_specs=tuple([HBM_SPEC] * len(arrays)),
        input_output_aliases={i: i for i in range(len(arrays))},
        compiler_params=pltpu.CompilerParams(has_side_effects=DATAFLOW),
    )(*arrays, send_sems, recv_sems, *after)
    return outs[:n_src], outs[n_src:]


def _cast_to_slot(shards, me_idx, after):
    n = len(shards)

    def body(i_ref, *refs):
        for w in range(n):
            refs[n + 1 + w][...] = refs[w][...].astype(bf16)

    return pl.pallas_call(
        body, name="cast_to_slot",
        grid_spec=pltpu.PrefetchScalarGridSpec(
            num_scalar_prefetch=1, grid=(1,),
            in_specs=[pl.BlockSpec(s.shape, lambda i, m: (0, 0)) for s in shards] + [pl.BlockSpec(memory_space=pl.ANY)],
            out_specs=[pl.BlockSpec((None, *s.shape), lambda i, m: (m[0], 0, 0)) for s in shards]),
        out_shape=[jax.ShapeDtypeStruct((N_DEV, *s.shape), bf16) for s in shards],
        compiler_params=_cp(("arbitrary",), VMEM_LIMIT),
    )(me_idx, *shards, after)


def _row_block(rows, n_blocks):
    return (rows // n_blocks, True) if rows % (16 * n_blocks) == 0 else (rows, False)


def _adam_math(w, g, m, v):
    m = ADAM_B1 * m + (1.0 - ADAM_B1) * g
    v = ADAM_B2 * v + (1.0 - ADAM_B2) * (g * g)
    m_hat = m / (1.0 - ADAM_B1 ** ADAM_STEP)
    v_hat = v / (1.0 - ADAM_B2 ** ADAM_STEP)
    delta = -ADAM_LR * (m_hat / (jnp.sqrt(v_hat) + ADAM_EPS) + ADAM_WD * w)
    return delta, m, v


def _adam_shards(me_idx, blocks, lands, ws, ms, vs, n_blocks, name, after=()):
    n = len(blocks)

    def body(i_ref, *refs):
        ins, outs = refs[:5 * n], refs[5 * n + len(after):]
        for w in range(n):
            g_ref, b_ref, w_ref, m_ref, v_ref = (ins[t * n + w] for t in range(5))
            g = g_ref[...].astype(f32)
            for k in range(N_PEERS):
                g = g + b_ref[k].astype(f32)
            if len(w_ref.shape) == 2:
                pieces = [(slice(None), g)]
            else:
                pieces = [(a, g[2 * a:2 * a + 2]) for a in range(2)]
            for at, gp in pieces:
                vals = (gp,) + _adam_math(w_ref[at], gp, m_ref[at], v_ref[at])
                for t, val in enumerate(vals):
                    outs[4 * w + t][at] = val

    specs = [[] for _ in range(5)]
    out_specs, out_shape = [], []
    for g, wt in zip(blocks, ws):
        rows, cols = g.shape[1:]
        rb, cut = _row_block(rows, n_blocks)
        specs[0].append(pl.BlockSpec((None, rb, cols), functools.partial(lambda i, s, cut: (s[0], i if cut else 0, 0), cut=cut)))
        specs[1].append(pl.BlockSpec((N_PEERS, rb, cols), functools.partial(lambda i, s, cut: (0, i if cut else 0, 0), cut=cut)))
        if wt.ndim == 2:
            shard = pl.BlockSpec((rb, cols), functools.partial(lambda i, s, cut: (i if cut else 0, 0), cut=cut))
        elif wt.shape[0] == 1:
            shard = pl.BlockSpec((None, rb, cols), functools.partial(lambda i, s, cut: (0, i if cut else 0, 0), cut=cut))
        else:
            shard = pl.BlockSpec(wt.shape, functools.partial(lambda i, s, nd: (0,) * nd, nd=wt.ndim))
        for t in (2, 3, 4):
            specs[t].append(shard)
        out_specs += [shard] * 4
        out_shape += [jax.ShapeDtypeStruct(wt.shape, f32)] * 4
    outs = pl.pallas_call(
        body, name=name,
        grid_spec=pltpu.PrefetchScalarGridSpec(
            num_scalar_prefetch=1, grid=(n_blocks,), in_specs=sum(specs, []) + [pl.BlockSpec(memory_space=pl.ANY)] * len(after),
            out_specs=out_specs),
        out_shape=out_shape,
        compiler_params=_cp(("arbitrary",), VMEM_LIMIT),
    )(me_idx, *blocks, *lands, *ws, *ms, *vs, *after)
    return [outs[4 * w:4 * w + 4] for w in range(n)]


def _adam_gains(parts, ws, ms, vs):
    n = len(ws)

    def body(p_ref, *refs):
        ins, outs = refs[:3 * n], refs[3 * n:]
        g_all = p_ref[0]
        for k in range(1, N_DEV):
            g_all = g_all + p_ref[k]
        for w, (off, lanes) in enumerate(SMALL.values()):
            w_ref, m_ref, v_ref = ins[w], ins[n + w], ins[2 * n + w]
            if len(w_ref.shape) == 2:
                pieces = [(slice(None), off, lanes)]
            else:
                pieces = [((slice(None), h), off + LANES * h, LANES) for h in range(w_ref.shape[1])]
            for at, o, ln in pieces:
                g = g_all[:, o:o + ln]
                vals = (g,) + _adam_math(w_ref[at], g, m_ref[at], v_ref[at])
                for t, val in enumerate(vals):
                    outs[4 * w + t][at] = val
        outs[4 * n][...] = g_all[:, LOSS_OFF:LOSS_OFF + LANES]

    out_shape = sum([[jax.ShapeDtypeStruct(w.shape, f32)] * 4 for w in ws], []) + [jax.ShapeDtypeStruct((1, LANES), f32)]
    outs = pl.pallas_call(body, name="adamw_gains", out_shape=out_shape)(parts, *ws, *ms, *vs)
    return [outs[4 * w:4 * w + 4] for w in range(n)], outs[4 * n]


def _fwd_in(x, g_mix, wz, tm):
    s, d = x.shape

    def body(x_ref, g_ref, w_ref, h_ref, z_ref):
        h, _ = _rms_fwd(x_ref[...], g_ref[...], d)
        hb = h.astype(bf16)
        h_ref[...] = hb
        z_ref[...] = _dot(hb, w_ref[...])

    return pl.pallas_call(
        body, name="fwd_in", grid=(s // tm,),
        in_specs=[pl.BlockSpec((tm, d), lambda i: (i, 0)), _const_spec((1, d)), _const_spec((d, Z_W))],
        out_specs=[pl.BlockSpec((tm, d), lambda i: (i, 0)), pl.BlockSpec((tm, Z_W), lambda i: (i, 0))],
        out_shape=[jax.ShapeDtypeStruct((s, d), bf16), jax.ShapeDtypeStruct((s, Z_W), f32)],
        compiler_params=_cp(("parallel",), VMEM_LIMIT),
    )(x, g_mix, wz)


def _mla_qk_fwd(cq, ckv, g_qa, g_kva, wqb, wkvb):
    cqn, rq = _rms_fwd(cq, g_qa, Q_LORA)
    ckvn, rkv = _rms_fwd(ckv, g_kva, KV_LORA)
    cqn_b, ckvn_b = cqn.astype(bf16), ckvn.astype(bf16)
    q0 = _dot(cqn_b, wqb)
    kv0 = _dot(ckvn_b, wkvb)
    return cqn_b, rq, ckvn_b, rkv, q0, kv0


def _fwd_mla_proj(z, cosb, sina, sinb, g_qa, g_kva, wqb, wkvb, g_qn, g_kn, tm):
    s = z.shape[0]
    hh = MLA_HEADS

    def body(cq_ref, ckv_ref, kr_ref, c_ref, sa_ref, sb_ref, gqa_ref, gkva_ref, wqb_ref, wkvb_ref, gqn_ref, gkn_ref,
             q_ref, k_ref, v_ref):
        _, _, _, _, q0, kv0 = _mla_qk_fwd(cq_ref[...], ckv_ref[...], gqa_ref[...], gkva_ref[...], wqb_ref[...], wkvb_ref[...])
        kr = kr_ref[...]
        c, sa, sb = c_ref[...], sa_ref[...], sb_ref[...]
        gqn, gkn = gqn_ref[...], gkn_ref[...]
        kr_sq = jnp.sum(kr * kr, axis=-1, keepdims=True)
        for h in range(hh):
            qh = q0[:, QK_PAD * h:QK_PAD * (h + 1)]
            qn, _ = _rms_fwd(qh, gqn, QK_HEAD)
            q_ref[h, :, 0:128] = qn[:, 0:128].astype(bf16)
            q_ref[h, :, 128:256] = _rope(qn[:, 128:256], c, sa, sb).astype(bf16)
            kn_ = kv0[:, 256 * h:256 * h + 128]
            rk = lax.rsqrt((jnp.sum(kn_ * kn_, axis=-1, keepdims=True) + kr_sq) * (1.0 / QK_HEAD) + EPS)
            k_ref[h, :, 0:128] = (kn_ * rk * gkn[:, 0:128]).astype(bf16)
            k_ref[h, :, 128:256] = _rope(kr * rk * gkn[:, 128:256], c, sa, sb).astype(bf16)
            v_ref[h] = kv0[:, 256 * h + 128:256 * h + 256].astype(bf16)

    row128 = pl.BlockSpec((tm, 128), lambda i: (i, 0))
    return pl.pallas_call(
        body, name="fwd_mla_proj", grid=(s // tm,),
        in_specs=[pl.BlockSpec((tm, 256), lambda i: (i, Z_CQ // 256)), pl.BlockSpec((tm, 256), lambda i: (i, Z_CKV // 256)),
                  pl.BlockSpec((tm, 128), lambda i: (i, Z_KR // 128)), row128, row128, row128,
                  _const_spec((1, 256)), _const_spec((1, 256)), _const_spec((256, 1024)), _const_spec((256, 1024)),
                  _const_spec((1, 256)), _const_spec((1, 256))],
        out_specs=[pl.BlockSpec((hh, tm, QK_PAD), lambda i: (0, i, 0)), pl.BlockSpec((hh, tm, QK_PAD), lambda i: (0, i, 0)),
                   pl.BlockSpec((hh, tm, V_HEAD), lambda i: (0, i, 0))],
        out_shape=[jax.ShapeDtypeStruct((hh, s, QK_PAD), bf16), jax.ShapeDtypeStruct((hh, s, QK_PAD), bf16),
                   jax.ShapeDtypeStruct((hh, s, V_HEAD), bf16)],
        compiler_params=_cp(("parallel",), VMEM_LIMIT),
    )(z, z, z, cosb, sina, sinb, g_qa, g_kva, wqb, wkvb, g_qn, g_kn)


def _fwd_attn(q, k, v, tq):
    hh, s, _ = q.shape

    n_sub = max(1, tq // ATTN_SUB_ROWS)

    def body(q_ref, k_ref, v_ref, o_ref, o32_ref):
        for t in range(n_sub):
            rows = slice(t * (tq // n_sub), (t + 1) * (tq // n_sub))
            sc = _dot_nt(q_ref[rows, :], k_ref[...])
            p = jnp.exp2((sc - jnp.max(sc, axis=-1, keepdims=True)) * (ATTN_SCALE * LOG2_E))
            l = jnp.sum(p, axis=-1, keepdims=True)
            o = _dot(p.astype(bf16), v_ref[...]) * (1.0 / l)
            o_ref[rows, :] = o.astype(bf16)
            o32_ref[rows, :] = o

    out = pl.BlockSpec((tq, V_HEAD), lambda h, i: (i, h))
    return pl.pallas_call(
        body, name="fwd_attn", grid=(hh, s // tq),
        in_specs=[pl.BlockSpec((None, tq, QK_PAD), lambda h, i: (h, i, 0)),
                  pl.BlockSpec((None, s, QK_PAD), lambda h, i: (h, 0, 0)),
                  pl.BlockSpec((None, s, V_HEAD), lambda h, i: (h, 0, 0))],
        out_specs=[out, out],
        out_shape=[jax.ShapeDtypeStruct((s, hh * V_HEAD), bf16), jax.ShapeDtypeStruct((s, hh * V_HEAD), f32)],
        compiler_params=_cp(("parallel", "parallel"), VMEM_LIMIT),
    )(q, k, v)


def _split3(x):
    hi = x.astype(bf16)
    r1 = x - hi.astype(f32)
    mid = r1.astype(bf16)
    lo = (r1 - mid.astype(f32)).astype(bf16)
    return jnp.concatenate([hi, mid, lo], axis=-1)


def _tri_sum(tri, x):
    y = _dot(tri, _split3(x))
    return y[:, 0:128] + y[:, 128:256] + y[:, 256:384]


GLA_GROUP = 4
GLA_ROWS = GLA_GROUP * CHUNK
GLA_HEADS_PER_STEP = 2


def _gla_masks(rev):
    row = lax.broadcasted_iota(jnp.int32, (GLA_ROWS, GLA_ROWS), 0)
    col = lax.broadcasted_iota(jnp.int32, (GLA_ROWS, GLA_ROWS), 1)
    shift = CHUNK.bit_length() - 1
    same = (jnp.right_shift(row, shift) == jnp.right_shift(col, shift)).astype(f32)
    lower, upper = (row >= col).astype(f32) * same, (row <= col).astype(f32) * same
    keep, keep_t = (upper, lower) if rev else (lower, upper)
    chunk_of = jnp.right_shift(lax.broadcasted_iota(jnp.int32, (GLA_ROWS, 1), 0), shift)
    return keep, keep.astype(bf16), keep_t.astype(bf16), [(chunk_of == c).astype(f32) for c in range(GLA_GROUP)]


def _gla_gates(hq, hf, lower):
    sg = _sigmoid(hf)
    f = lower + (1.0 - lower) * sg
    return hq * _sigmoid(hq), 1.0 - f, jnp.log(f), f, sg


def _gla_last_mid(b, rev):
    b3 = b.reshape(GLA_GROUP, CHUNK, 128)
    last, mid = (0, CHUNK // 2) if rev else (CHUNK - 1, CHUNK // 2 - 1)
    return b3[:, last:last + 1, :], b3[:, mid:mid + 1, :]


def _gla_per_row(per_chunk):
    return jnp.broadcast_to(per_chunk, (GLA_GROUP, CHUNK, 128)).reshape(GLA_ROWS, 128)


def _gla_block_diag(x, row_masks):
    return jnp.concatenate([(x * m).astype(bf16) for m in row_masks], axis=-1)


def _gla_diag(y):
    return jnp.concatenate([y[CHUNK * c:CHUNK * (c + 1), 128 * c:128 * (c + 1)] for c in range(GLA_GROUP)], axis=0)


def _gla_rows(n, n_groups, rev):
    ne = n_groups - 1 - n if rev else n
    return pl.ds(pl.multiple_of(ne * GLA_ROWS, GLA_ROWS), GLA_ROWS), ne * GLA_GROUP


def _gla_scan_order(rev):
    return tuple(reversed(range(GLA_GROUP))) if rev else tuple(range(GLA_GROUP))


def _fwd_gla(z, lb4):
    s = z.shape[0]
    n_groups = s // GLA_ROWS
    assert n_groups % 2 == 0
    hp = GLA_HEADS_PER_STEP
    chains = [(hh, rev) for hh in range(hp) for rev in (False, True)]

    def body(hq_ref, hff_ref, hfb_ref, hi_ref, lb_ref, o_ref, b_ref, states_ref, st_ref, stage_ref, b_stage, sems):
        st_ref[...] = jnp.zeros_like(st_ref)
        masks = {rev: _gla_masks(rev) for rev in (False, True)}
        lowers = [_sigmoid(lb_ref[int(rev):int(rev) + 1, 128 * hh:128 * (hh + 1)]
                           - lb_ref[2 + int(rev):3 + int(rev), 128 * hh:128 * (hh + 1)]) for hh, rev in chains]

        def states_out(slot, ci, chunk0):
            hh, rev = chains[ci]
            head = pl.program_id(0) * hp + hh
            rows = pl.ds(pl.multiple_of(chunk0 * CHUNK, GLA_ROWS), GLA_ROWS)
            return _Both(
                pltpu.make_async_copy(stage_ref.at[slot, ci], states_ref.at[head, int(rev), pl.ds(chunk0, GLA_GROUP)],
                                      sems.at[slot, ci]),
                pltpu.make_async_copy(b_stage.at[slot, ci], b_ref.at[int(rev), rows, pl.ds(pl.multiple_of(head * 128, 128), 128)],
                                      sems.at[slot, len(chains) + ci]))

        def make_step(first):
            def step(n, carry):
                slot = n % 2

                @pl.when(n >= 2)
                def _():
                    for ci in range(len(chains)):
                        states_out(slot, ci, 0).wait()

                for ci, (hh, rev) in enumerate(chains):
                    cols = slice(128 * hh, 128 * (hh + 1))
                    rows, chunk0 = _gla_rows(n, n_groups, rev)
                    maskf, tri, _, row_masks = masks[rev]
                    hf_ref = hfb_ref if rev else hff_ref
                    q, k, logf, _, _ = _gla_gates(hq_ref[rows, cols], hf_ref[rows, cols], lowers[ci])
                    vb = hi_ref[rows, cols].astype(bf16)
                    b = _tri_sum(tri, logf)
                    b_stage[slot, ci] = b
                    b_last3, b_mid3 = _gla_last_mid(b, rev)
                    b_last, b_mid = _gla_per_row(b_last3), _gla_per_row(b_mid3)
                    qi = (q * jnp.exp(b - b_mid)).astype(bf16)
                    ki = (k * jnp.exp(b_mid - b)).astype(bf16)
                    a = (_dot_nt(qi, ki) * maskf).astype(bf16)
                    kv = _dot_tn(vb, _gla_block_diag(k * jnp.exp(b_last - b), row_masks))
                    decay3 = jnp.exp(b_last3)
                    st = st_ref[ci]
                    before = [None] * GLA_GROUP
                    for c in _gla_scan_order(rev):
                        stage_ref[slot, ci, c] = st
                        before[c] = st.astype(bf16)
                        st = st * decay3[c] + kv[:, 128 * c:128 * (c + 1)]
                    st_ref[ci] = st
                    states_out(slot, ci, chunk0).start()
                    inter = _dot_nt((q * jnp.exp(b)).astype(bf16), jnp.concatenate(before, axis=0))
                    o = _dot(a, vb) + _gla_diag(inter)
                    if first:
                        o_ref[rows, cols] = o
                    else:
                        o_ref[rows, cols] += o
                return carry
            return step

        lax.fori_loop(0, n_groups // 2, make_step(True), 0)
        lax.fori_loop(n_groups // 2, n_groups, make_step(False), 0)
        for slot in range(2):
            for ci in range(len(chains)):
                states_out(slot, ci, 0).wait()

    w = 128 * hp
    col = lambda base: pl.BlockSpec((s, w), lambda h: (0, base // w + h))
    return pl.pallas_call(
        body, name="fwd_gla", grid=(HG_HEADS // hp,),
        in_specs=[col(Z_HQ), col(Z_HFF), col(Z_HFB), col(Z_HI), pl.BlockSpec((4, w), lambda h: (0, h))],
        out_specs=[pl.BlockSpec((s, w), lambda h: (0, h)), pl.BlockSpec(memory_space=pl.ANY), pl.BlockSpec(memory_space=pl.ANY)],
        out_shape=[jax.ShapeDtypeStruct((s, HG_HEADS * 128), f32), jax.ShapeDtypeStruct((2, s, HG_HEADS * 128), f32),
                   jax.ShapeDtypeStruct((HG_HEADS, 2, s // CHUNK, 128, 128), f32)],
        scratch_shapes=[pltpu.VMEM((len(chains), 128, 128), f32), pltpu.VMEM((2, len(chains), GLA_GROUP, 128, 128), f32),
                        pltpu.VMEM((2, len(chains), GLA_ROWS, 128), f32), pltpu.SemaphoreType.DMA((2, 2 * len(chains)))],
        compiler_params=_cp(("parallel",), VMEM_LIMIT),
    )(z, z, z, z, lb4)


def _hg_out(o, hg, g_hgo):
    outs, ons, rs = [], [], []
    for h in range(HG_HEADS):
        oh = o[:, 128 * h:128 * (h + 1)]
        on, r = _rms_fwd(oh, g_hgo[:, 128 * h:128 * (h + 1)], 128)
        ons.append(on)
        rs.append(r)
    on = jnp.concatenate(ons, axis=-1)
    sg = _sigmoid(hg)
    return on * (hg * sg), on, rs, sg


def _fwd_mix(a, o, z, g_hgo, x, w_o, tm):
    s, d = x.shape

    def body(a_ref, o_ref, hg_ref, g_ref, x_ref, w_ref, x2_ref, cat_ref):
        r, _, _, _ = _hg_out(o_ref[...], hg_ref[...], g_ref[...])
        cat = jnp.concatenate([a_ref[...], r.astype(bf16)], axis=-1)
        cat_ref[...] = cat
        x2_ref[...] = x_ref[...] + _dot(cat, w_ref[...])

    row512 = pl.BlockSpec((tm, 512), lambda i: (i, 0))
    rowd = pl.BlockSpec((tm, d), lambda i: (i, 0))
    return pl.pallas_call(
        body, name="fwd_mix", grid=(s // tm,),
        in_specs=[row512, row512, pl.BlockSpec((tm, 512), lambda i: (i, Z_HG // 512)), _const_spec((1, 512)), rowd,
                  _const_spec((d, d))],
        out_specs=[rowd, rowd],
        out_shape=[jax.ShapeDtypeStruct((s, d), f32), jax.ShapeDtypeStruct((s, d), bf16)],
        compiler_params=_cp(("parallel",), VMEM_LIMIT),
    )(a, o, z, g_hgo, x, w_o)


def _fwd_ffn(x2, g_ffn, w_gate, w_up, w_down, tm):
    s, d = x2.shape

    def body(x_ref, g_ref, wg_ref, wu_ref, wd_ref, x3_ref, gp_ref, up_ref):
        x = x_ref[...]
        h, _ = _rms_fwd(x, g_ref[...], d)
        hb = h.astype(bf16)
        gp = _dot_nt(hb, wg_ref[...])
        up = _dot_nt(hb, wu_ref[...])
        gp_ref[...] = gp.astype(bf16)
        up_ref[...] = up.astype(bf16)
        act = (gp * _sigmoid(gp) * up).astype(bf16)
        x3_ref[...] = x + _dot(act, wd_ref[...])

    rowd = pl.BlockSpec((tm, d), lambda i: (i, 0))
    rowf = pl.BlockSpec((tm, D_FF), lambda i: (i, 0))
    return pl.pallas_call(
        body, name="fwd_ffn", grid=(s // tm,),
        in_specs=[rowd, _const_spec((1, d)), _const_spec((D_FF, d)), _const_spec((D_FF, d)), _const_spec((D_FF, d))],
        out_specs=[rowd, rowf, rowf],
        out_shape=[jax.ShapeDtypeStruct((s, d), f32), jax.ShapeDtypeStruct((s, D_FF), bf16),
                   jax.ShapeDtypeStruct((s, D_FF), bf16)],
        compiler_params=_cp(("parallel",), VMEM_LIMIT),
    )(x2, g_ffn, w_gate, w_up, w_down)


def _ple_loss_fwd_bwd(x3, g_ple, w_pg, p, w_pp, target, tm):
    s, d = x3.shape

    def body(x_ref, g_ref, wg_ref, p_ref, wp_ref, t_ref, dx_ref, h_ref, dpre_ref, dpp_ref, dg_ref, loss_ref):
        @pl.when(pl.program_id(0) == 0)
        def _():
            dg_ref[...] = jnp.zeros_like(dg_ref)
            loss_ref[...] = jnp.zeros_like(loss_ref)

        x = x_ref[...]
        g = g_ref[...]
        h, r = _rms_fwd(x, g, d)
        hb = h.astype(bf16)
        gate = _sigmoid(_dot(hb, wg_ref[...]))
        pp = _dot(p_ref[...].astype(bf16), wp_ref[...])
        e = x + gate * pp - t_ref[...]
        loss_ref[...] += 0.5 * jnp.sum(e * e) * (1.0 / d)
        dy = e * (1.0 / d)
        dpre = (dy * pp * gate * (1.0 - gate)).astype(bf16)
        dx, dgx = _rms_bwd(_dot_nt(dpre, wg_ref[...]), x, r, g, d)
        dx_ref[...] = dy + dx
        dg_ref[...] += jnp.sum(dgx, axis=0, keepdims=True)
        h_ref[...] = hb
        dpre_ref[...] = dpre
        dpp_ref[...] = (dy * gate).astype(bf16)

    rowd = pl.BlockSpec((tm, d), lambda i: (i, 0))
    return pl.pallas_call(
        body, name="ple_loss_fwd_bwd", grid=(s // tm,),
        in_specs=[rowd, _const_spec((1, d)), _const_spec((d, d)), pl.BlockSpec((tm, PLE_DIM), lambda i: (i, 0)),
                  _const_spec((PLE_DIM, d)), rowd],
        out_specs=[rowd, rowd, rowd, rowd, _acc_spec((1, d)), _acc_spec((8, 128))],
        out_shape=[jax.ShapeDtypeStruct((s, d), f32), jax.ShapeDtypeStruct((s, d), bf16), jax.ShapeDtypeStruct((s, d), bf16),
                   jax.ShapeDtypeStruct((s, d), bf16), jax.ShapeDtypeStruct((1, d), f32), jax.ShapeDtypeStruct((8, 128), f32)],
        compiler_params=_cp(("arbitrary",), VMEM_LIMIT),
    )(x3, g_ple, w_pg, p, w_pp, target)


def _bwd_ffn_hidden(d3, x2, gp, up, g_ffn, w_down, tm, tf):
    s, d = x2.shape

    def body(d3_ref, x_ref, gp_ref, up_ref, g_ref, wd_ref, h_ref, act_ref, dgp_ref, dup_ref, d3b_ref):
        @pl.when(pl.program_id(1) == 0)
        def _():
            h, _ = _rms_fwd(x_ref[...], g_ref[...], d)
            h_ref[...] = h.astype(bf16)
            d3b_ref[...] = d3_ref[...].astype(bf16)

        gp, up = gp_ref[...].astype(f32), up_ref[...].astype(f32)
        sg = _sigmoid(gp)
        silu = gp * sg
        act_ref[...] = (silu * up).astype(bf16)
        dact = _dot_nt(d3b_ref[...], wd_ref[...])
        dgp_ref[...] = (dact * up * (sg * (1.0 + gp * (1.0 - sg)))).astype(bf16)
        dup_ref[...] = (dact * silu).astype(bf16)

    rowd = pl.BlockSpec((tm, d), lambda i, f: (i, 0))
    rowf = pl.BlockSpec((tm, tf), lambda i, f: (i, f))
    return pl.pallas_call(
        body, name="bwd_ffn_hidden", grid=(s // tm, D_FF // tf),
        in_specs=[rowd, rowd, rowf, rowf, _const_spec((1, d)), pl.BlockSpec((tf, d), lambda i, f: (f, 0))],
        out_specs=[rowd, rowf, rowf, rowf],
        out_shape=[jax.ShapeDtypeStruct((s, d), bf16)] + [jax.ShapeDtypeStruct((s, D_FF), bf16)] * 3,
        scratch_shapes=[pltpu.VMEM((tm, d), bf16)],
        compiler_params=_cp(("parallel", "arbitrary"), VMEM_LIMIT),
    )(d3, x2, gp, up, g_ffn, w_down)


def _bwd_ffn_in(d3, x2, dgp, dup, g_ffn, w_gate, w_up, tm):
    s, d = x2.shape

    def body(d3_ref, x_ref, dgp_ref, dup_ref, g_ref, wg_ref, wu_ref, d2_ref, dg_ref):
        @pl.when(pl.program_id(0) == 0)
        def _():
            dg_ref[...] = jnp.zeros_like(dg_ref)

        x, g = x_ref[...], g_ref[...]
        dh = _dot(dgp_ref[...], wg_ref[...]) + _dot(dup_ref[...], wu_ref[...])
        r = lax.rsqrt(jnp.sum(x * x, axis=-1, keepdims=True) * (1.0 / d) + EPS)
        dx, dgx = _rms_bwd(dh, x, r, g, d)
        d2_ref[...] = d3_ref[...] + dx
        dg_ref[...] += jnp.sum(dgx, axis=0, keepdims=True)

    rowd = pl.BlockSpec((tm, d), lambda i: (i, 0))
    rowf = pl.BlockSpec((tm, D_FF), lambda i: (i, 0))
    return pl.pallas_call(
        body, name="bwd_ffn_in", grid=(s // tm,),
        in_specs=[rowd, rowd, rowf, rowf, _const_spec((1, d)), _const_spec((D_FF, d)), _const_spec((D_FF, d))],
        out_specs=[rowd, _acc_spec((1, d))],
        out_shape=[jax.ShapeDtypeStruct((s, d), f32), jax.ShapeDtypeStruct((1, d), f32)],
        compiler_params=_cp(("arbitrary",), VMEM_LIMIT),
    )(d3, x2, dgp, dup, g_ffn, w_gate, w_up)


def _bwd_mix(d2, w_o, o, z, g_hgo, tm):
    s, d = d2.shape

    def body(d2_ref, w_ref, o_ref, hg_ref, g_ref, da_ref, do_ref, dhg_ref, dg_ref):
        @pl.when(pl.program_id(0) == 0)
        def _():
            dg_ref[...] = jnp.zeros_like(dg_ref)

        dcat = _dot_nt(d2_ref[...].astype(bf16), w_ref[...])
        da_ref[...] = dcat[:, 0:512].astype(bf16)
        dr = dcat[:, 512:1024]
        o, hg, g = o_ref[...], hg_ref[...], g_ref[...]
        _, on, rs, sg = _hg_out(o, hg, g)
        dhg_ref[...] = (dr * on * (sg * (1.0 + hg * (1.0 - sg)))).astype(bf16)
        don = dr * (hg * sg)
        dgs = []
        for h in range(HG_HEADS):
            cols = slice(128 * h, 128 * (h + 1))
            dx, dgx = _rms_bwd(don[:, cols], o[:, cols], rs[h], g[:, cols], 128)
            do_ref[:, cols] = dx
            dgs.append(jnp.sum(dgx, axis=0, keepdims=True))
        dg_ref[...] += jnp.concatenate(dgs, axis=-1)

    row512 = pl.BlockSpec((tm, 512), lambda i: (i, 0))
    return pl.pallas_call(
        body, name="bwd_mix", grid=(s // tm,),
        in_specs=[pl.BlockSpec((tm, d), lambda i: (i, 0)), _const_spec((d, d)), row512,
                  pl.BlockSpec((tm, 512), lambda i: (i, Z_HG // 512)), _const_spec((1, 512))],
        out_specs=[row512, row512, row512, _acc_spec((1, 512))],
        out_shape=[jax.ShapeDtypeStruct((s, 512), bf16), jax.ShapeDtypeStruct((s, 512), f32), jax.ShapeDtypeStruct((s, 512), bf16),
                   jax.ShapeDtypeStruct((1, 512), f32)],
        compiler_params=_cp(("arbitrary",), VMEM_LIMIT),
    )(d2, w_o, o, z, g_hgo)


def _bwd_gla(z, lb4, do, b_fwd, states):
    s = z.shape[0]
    n_chunks = s // CHUNK
    n_groups = s // GLA_ROWS
    assert n_groups % 2 == 0

    def body(hq_ref, hff_ref, hfb_ref, hi_ref, lb_ref, do_ref, b_all, st_all, dhq_ref, dhff_ref, dhfb_ref, dhi_ref, dlb_ref,
             dst_ref, dq_acc, dv_acc, dlow_ref):
        dirs = (False, True)
        masks = [_gla_masks(rev) for rev in dirs]
        lowers = [_sigmoid(lb_ref[int(rev):int(rev) + 1, :] - lb_ref[2 + int(rev):3 + int(rev), :]) for rev in dirs]
        hf_refs, dhf_refs = (hff_ref, hfb_ref), (dhff_ref, dhfb_ref)

        dst_ref[...] = jnp.zeros_like(dst_ref)
        dlow_ref[...] = jnp.zeros_like(dlow_ref)

        def make_bwd_step(first):
            def bwd_step(j, carry):
                n = n_groups - 1 - j
                for d, rev in enumerate(dirs):
                    maskf, _, tri_t, row_masks = masks[d]
                    lower = lowers[d]
                    rows, chunk0 = _gla_rows(n, n_groups, rev)
                    hq, hf = hq_ref[rows, :], hf_refs[d][rows, :]
                    q, k, _, f, sg = _gla_gates(hq, hf, lower)
                    v = hi_ref[rows, :]
                    dout = do_ref[rows, :]
                    b = b_all[d, rows, :]
                    b_last3, b_mid3 = _gla_last_mid(b, rev)
                    b_last, b_mid = _gla_per_row(b_last3), _gla_per_row(b_mid3)
                    e1, e2, e3, e4 = jnp.exp(b - b_mid), jnp.exp(b_mid - b), jnp.exp(b_last - b), jnp.exp(b)
                    decay3 = jnp.exp(b_last3)
                    qi, ki, kt, qt = q * e1, k * e2, k * e3, q * e4
                    qib, kib, ktb = qi.astype(bf16), ki.astype(bf16), kt.astype(bf16)
                    vb, dob = v.astype(bf16), dout.astype(bf16)
                    a = (_dot_nt(qib, kib) * maskf).astype(bf16)
                    da = (_dot_nt(dob, vb) * maskf).astype(bf16)
                    dqi = _dot(da, kib)
                    dki = _dot_tn(da, qib)
                    into_state = _dot_tn(dob, _gla_block_diag(qt, row_masks))
                    dst = dst_ref[d]
                    sts, dsts, ddecay = [None] * GLA_GROUP, [None] * GLA_GROUP, [None] * GLA_GROUP
                    for c in reversed(_gla_scan_order(rev)):
                        sts[c] = st_all[d, chunk0 + c]
                        dsts[c] = dst.astype(bf16)
                        ddecay[c] = jnp.sum(dst * sts[c], axis=0, keepdims=True)[None]
                        dst = dst * decay3[c] + into_state[:, 128 * c:128 * (c + 1)]
                    dst_ref[d] = dst
                    dv = _dot_tn(a, dob) + _gla_diag(_dot_nt(ktb, jnp.concatenate(dsts, axis=0)))
                    dqt = _gla_diag(_dot(dob, jnp.concatenate([x.astype(bf16) for x in sts], axis=-1)))
                    dkt = _gla_diag(_dot(vb, jnp.concatenate(dsts, axis=-1)))
                    dq = dqi * e1 + dqt * e4
                    dk = dki * e2 + dkt * e3
                    db = dqi * qi - dki * ki + dqt * qt - dkt * kt
                    dlast3 = (jnp.sum((dkt * kt).reshape(GLA_GROUP, CHUNK, 128), axis=1, keepdims=True)
                              + jnp.concatenate(ddecay, axis=0) * decay3)
                    dlogf = _tri_sum(tri_t, db) + _gla_per_row(dlast3)
                    df = dlogf / f - dk
                    dhf_refs[d][rows, :] = (df * (1.0 - lower) * sg * (1.0 - sg)).astype(bf16)
                    dlow_ref[d:d + 1, :] += jnp.sum(df * (1.0 - sg), axis=0, keepdims=True)
                    sq = _sigmoid(hq)
                    dhq = dq * (sq * (1.0 + hq * (1.0 - sq)))
                    if first:
                        dq_acc[rows, :] = dhq
                        dv_acc[rows, :] = dv
                    else:
                        dhq_ref[rows, :] = (dq_acc[rows, :] + dhq).astype(bf16)
                        dhi_ref[rows, :] = (dv_acc[rows, :] + dv).astype(bf16)
                return carry
            return bwd_step

        lax.fori_loop(0, n_groups // 2, make_bwd_step(True), 0)
        lax.fori_loop(n_groups // 2, n_groups, make_bwd_step(False), 0)

        for d in range(2):
            dl = dlow_ref[d:d + 1, :] * lowers[d] * (1.0 - lowers[d])
            dlb_ref[d:d + 1, :] = dl
            dlb_ref[2 + d:3 + d, :] = -dl

    col = lambda base: pl.BlockSpec((s, 128), lambda h: (0, base // 128 + h))
    return pl.pallas_call(
        body, name="bwd_gla", grid=(HG_HEADS,),
        in_specs=[col(Z_HQ), col(Z_HFF), col(Z_HFB), col(Z_HI), pl.BlockSpec((4, 128), lambda h: (0, h)), col(0),
                  pl.BlockSpec((2, s, 128), lambda h: (0, 0, h)),
                  pl.BlockSpec((None, 2, n_chunks, 128, 128), lambda h: (h, 0, 0, 0, 0), pipeline_mode=pl.Buffered(1))],
        out_specs=[col(0), col(0), col(0), col(0), pl.BlockSpec((4, 128), lambda h: (0, h))],
        out_shape=[jax.ShapeDtypeStruct((s, 512), bf16)] * 4 + [jax.ShapeDtypeStruct((4, 512), f32)],
        scratch_shapes=[pltpu.VMEM((2, 128, 128), f32), pltpu.VMEM((s, 128), f32), pltpu.VMEM((s, 128), f32),
                        pltpu.VMEM((2, 128), f32)],
        compiler_params=_cp(("parallel",), VMEM_LIMIT),
    )(z, z, z, z, lb4, do, b_fwd, states)


def _bwd_attn(q, k, v, da, a32, tq):
    hh, s, _ = q.shape

    n_sub = max(1, tq // ATTN_SUB_ROWS)

    def body(q_ref, k_ref, v_ref, do_ref, o_ref, dq_ref, dk_ref, dv_ref, p_all, ds_all, dol_ref, dkt_ref, dvt_ref):
        @pl.when(pl.program_id(1) == 0)
        def _():
            dkt_ref[...] = jnp.zeros_like(dkt_ref)
            dvt_ref[...] = jnp.zeros_like(dvt_ref)

        kb, vb = k_ref[...], v_ref[...]
        for t in range(n_sub):
            rows = slice(t * (tq // n_sub), (t + 1) * (tq // n_sub))
            sc = _dot_nt(q_ref[rows, :], kb)
            p = jnp.exp2((sc - jnp.max(sc, axis=-1, keepdims=True)) * (ATTN_SCALE * LOG2_E))
            inv_l = 1.0 / jnp.sum(p, axis=-1, keepdims=True)
            pb = p.astype(bf16)
            dob = do_ref[rows, :]
            dof = dob.astype(f32)
            delta = jnp.sum(dof * o_ref[rows, :], axis=-1, keepdims=True)
            ds = pb * ((_dot_nt(dob, vb) - delta) * inv_l).astype(bf16)
            dq_ref[rows, :] = _dot(ds, kb) * ATTN_SCALE
            p_all[rows, :] = pb
            ds_all[rows, :] = ds
            dol_ref[rows, :] = (dof * inv_l).astype(bf16)
        dkt_ref[...] += _dot_tn(q_ref[...], ds_all[...])
        dvt_ref[...] += _dot_tn(dol_ref[...], p_all[...])

        @pl.when(pl.program_id(1) == s // tq - 1)
        def _():
            dk_ref[...] = dkt_ref[...].T * ATTN_SCALE
            dv_ref[...] = dvt_ref[...].T

    return pl.pallas_call(
        body, name="bwd_attn", grid=(hh, s // tq),
        in_specs=[pl.BlockSpec((None, tq, QK_PAD), lambda h, i: (h, i, 0)),
                  pl.BlockSpec((None, s, QK_PAD), lambda h, i: (h, 0, 0)),
                  pl.BlockSpec((None, s, V_HEAD), lambda h, i: (h, 0, 0)),
                  pl.BlockSpec((tq, V_HEAD), lambda h, i: (i, h)), pl.BlockSpec((tq, V_HEAD), lambda h, i: (i, h))],
        out_specs=[pl.BlockSpec((None, tq, QK_PAD), lambda h, i: (h, i, 0)),
                   pl.BlockSpec((None, s, QK_PAD), lambda h, i: (h, 0, 0)),
                   pl.BlockSpec((None, s, V_HEAD), lambda h, i: (h, 0, 0))],
        out_shape=[jax.ShapeDtypeStruct((hh, s, QK_PAD), f32), jax.ShapeDtypeStruct((hh, s, QK_PAD), f32),
                   jax.ShapeDtypeStruct((hh, s, V_HEAD), f32)],
        scratch_shapes=[pltpu.VMEM((tq, s), bf16), pltpu.VMEM((tq, s), bf16), pltpu.VMEM((tq, V_HEAD), bf16),
                        pltpu.VMEM((QK_PAD, s), f32), pltpu.VMEM((V_HEAD, s), f32)],
        compiler_params=_cp(("parallel", "arbitrary"), VMEM_LIMIT),
    )(q, k, v, da, a32)


def _bwd_mla_proj(z, dq, dk, dv, cosb, sina, sinb, g_qa, g_kva, wqb, wkvb, g_qn, g_kn, tm):
    s = z.shape[0]
    hh = MLA_HEADS

    def body(cq_ref, ckv_ref, kr_ref, dq_ref, dk_ref, dv_ref, c_ref, sa_ref, sb_ref, gqa_ref, gkva_ref, wqb_ref, wkvb_ref,
             gqn_ref, gkn_ref, dz_ref, cqn_ref, ckvn_ref, dq0_ref, dkv0_ref, dgqa_ref, dgkva_ref, dgqn_ref, dgkn_ref):
        @pl.when(pl.program_id(0) == 0)
        def _():
            for r in (dgqa_ref, dgkva_ref, dgqn_ref, dgkn_ref):
                r[...] = jnp.zeros_like(r)

        cq, ckv, kr = cq_ref[...], ckv_ref[...], kr_ref[...]
        gqa, gkva, gqn, gkn = gqa_ref[...], gkva_ref[...], gqn_ref[...], gkn_ref[...]
        cqn_b, rq, ckvn_b, rkv, q0, kv0 = _mla_qk_fwd(cq, ckv, gqa, gkva, wqb_ref[...], wkvb_ref[...])
        cqn_ref[...] = cqn_b
        ckvn_ref[...] = ckvn_b
        c, sa, sb = c_ref[...], -sa_ref[...], -sb_ref[...]
        kr_sq = jnp.sum(kr * kr, axis=-1, keepdims=True)
        dkr = jnp.zeros_like(kr)
        dgqn = jnp.zeros((1, QK_PAD), f32)
        dgkn = jnp.zeros((1, QK_PAD), f32)
        for h in range(hh):
            qh = q0[:, QK_PAD * h:QK_PAD * (h + 1)]
            rh = lax.rsqrt(jnp.sum(qh * qh, axis=-1, keepdims=True) * (1.0 / QK_HEAD) + EPS)
            dqh = dq_ref[h]
            dqn = jnp.concatenate([dqh[:, 0:128], _rope(dqh[:, 128:256], c, sa, sb)], axis=-1)
            dq0h, dgx = _rms_bwd(dqn, qh, rh, gqn, QK_HEAD)
            dq0_ref[:, QK_PAD * h:QK_PAD * (h + 1)] = dq0h.astype(bf16)
            dgqn = dgqn + jnp.sum(dgx, axis=0, keepdims=True)

            kn_ = kv0[:, 256 * h:256 * h + 128]
            k0 = jnp.concatenate([kn_, kr], axis=-1)
            rk = lax.rsqrt((jnp.sum(kn_ * kn_, axis=-1, keepdims=True) + kr_sq) * (1.0 / QK_HEAD) + EPS)
            dkh = dk_ref[h]
            dkn = jnp.concatenate([dkh[:, 0:128], _rope(dkh[:, 128:256], c, sa, sb)], axis=-1)
            dk0, dgx = _rms_bwd(dkn, k0, rk, gkn, QK_HEAD)
            dgkn = dgkn + jnp.sum(dgx, axis=0, keepdims=True)
            dkv0_ref[:, 256 * h:256 * h + 128] = dk0[:, 0:128].astype(bf16)
            dkv0_ref[:, 256 * h + 128:256 * h + 256] = dv_ref[h].astype(bf16)
            dkr = dkr + dk0[:, 128:256]
        dgqn_ref[...] += dgqn
        dgkn_ref[...] += dgkn
        dcq, dgx = _rms_bwd(_dot_nt(dq0_ref[...], wqb_ref[...]), cq, rq, gqa, Q_LORA)
        dgqa_ref[...] += jnp.sum(dgx, axis=0, keepdims=True)
        dckv, dgx = _rms_bwd(_dot_nt(dkv0_ref[...], wkvb_ref[...]), ckv, rkv, gkva, KV_LORA)
        dgkva_ref[...] += jnp.sum(dgx, axis=0, keepdims=True)
        dz_ref[:, 0:256] = dcq.astype(bf16)
        dz_ref[:, 256:512] = dckv.astype(bf16)
        dz_ref[:, 512:640] = dkr.astype(bf16)

    row128 = pl.BlockSpec((tm, 128), lambda i: (i, 0))
    row256 = pl.BlockSpec((tm, 256), lambda i: (i, 0))
    row1024 = pl.BlockSpec((tm, 1024), lambda i: (i, 0))
    hd = lambda w: pl.BlockSpec((hh, tm, w), lambda i: (0, i, 0))
    return pl.pallas_call(
        body, name="bwd_mla_proj", grid=(s // tm,),
        in_specs=[pl.BlockSpec((tm, 256), lambda i: (i, Z_CQ // 256)), pl.BlockSpec((tm, 256), lambda i: (i, Z_CKV // 256)),
                  pl.BlockSpec((tm, 128), lambda i: (i, Z_KR // 128)), hd(QK_PAD), hd(QK_PAD), hd(V_HEAD),
                  row128, row128, row128,
                  _const_spec((1, 256)), _const_spec((1, 256)), _const_spec((256, 1024)), _const_spec((256, 1024)),
                  _const_spec((1, 256)), _const_spec((1, 256))],
        out_specs=[pl.BlockSpec((tm, 640), lambda i: (i, 0)), row256, row256, row1024, row1024,
                   _acc_spec((1, 256)), _acc_spec((1, 256)), _acc_spec((1, 256)), _acc_spec((1, 256))],
        out_shape=[jax.ShapeDtypeStruct((s, 640), bf16), jax.ShapeDtypeStruct((s, 256), bf16), jax.ShapeDtypeStruct((s, 256), bf16),
                   jax.ShapeDtypeStruct((s, 1024), bf16), jax.ShapeDtypeStruct((s, 1024), bf16)]
        + [jax.ShapeDtypeStruct((1, 256), f32)] * 4,
        compiler_params=_cp(("arbitrary",), VMEM_LIMIT),
    )(z, z, z, dq, dk, dv, cosb, sina, sinb, g_qa, g_kva, wqb, wkvb, g_qn, g_kn)


def _bwd_in(segments, wz, x, g_mix, d2, tm):
    s, d = x.shape
    n_seg = len(segments)

    def body(*refs):
        dz_refs, w_refs = refs[:n_seg], refs[n_seg:2 * n_seg]
        x_ref, g_ref, d2_ref, gx_ref, dg_ref = refs[2 * n_seg:]

        @pl.when(pl.program_id(0) == 0)
        def _():
            dg_ref[...] = jnp.zeros_like(dg_ref)

        dh = _dot_nt(dz_refs[0][...], w_refs[0][...])
        for a_ref, w_ref in zip(dz_refs[1:], w_refs[1:]):
            dh = dh + _dot_nt(a_ref[...], w_ref[...])
        x, g = x_ref[...], g_ref[...]
        r = lax.rsqrt(jnp.sum(x * x, axis=-1, keepdims=True) * (1.0 / d) + EPS)
        dx, dgx = _rms_bwd(dh, x, r, g, d)
        gx_ref[...] = d2_ref[...] + dx
        dg_ref[...] += jnp.sum(dgx, axis=0, keepdims=True)

    rowd = pl.BlockSpec((tm, d), lambda i: (i, 0))
    dz_specs = [pl.BlockSpec((tm, w), functools.partial(lambda i, j: (i, j), j=ja)) for _, w, ja, _ in segments]
    w_specs = [pl.BlockSpec((d, w), functools.partial(lambda i, j: (0, j), j=jw), pipeline_mode=pl.Buffered(1))
               for _, w, _, jw in segments]
    return pl.pallas_call(
        body, name="bwd_in", grid=(s // tm,),
        in_specs=dz_specs + w_specs + [rowd, _const_spec((1, d)), rowd],
        out_specs=[rowd, _acc_spec((1, d))],
        out_shape=[jax.ShapeDtypeStruct((s, d), f32), jax.ShapeDtypeStruct((1, d), f32)],
        compiler_params=_cp(("arbitrary",), VMEM_LIMIT),
    )(*[a for a, _, _, _ in segments], *([wz] * n_seg), x, g_mix, d2)


def _pick_tile(n, cap):
    best = None
    for t in range(LANES, cap + 1, LANES):
        if n % t == 0:
            best = t
    return best if best is not None else n


def _mm_tn_many(a, bs, name, tm, transposed=False):
    kk, m = a.shape
    n_b = len(bs)
    tk = min(1024, kk)
    n_k = kk // tk

    def body(a_ref, *refs):
        b_refs, o_refs, acc_refs = refs[:n_b], refs[n_b:2 * n_b], refs[2 * n_b:]

        @pl.when(pl.program_id(1) == 0)
        def _():
            for acc in acc_refs:
                acc[...] = jnp.zeros_like(acc)
        a_blk = a_ref[...].astype(bf16)
        for b_ref, acc in zip(b_refs, acc_refs):
            acc[...] += _dot_tn(a_blk, b_ref[...].astype(bf16))

        @pl.when(pl.program_id(1) == n_k - 1)
        def _():
            for o_ref, acc in zip(o_refs, acc_refs):
                o_ref[...] = (acc[...].T if transposed else acc[...]).astype(bf16)

    if transposed:
        out_specs = [pl.BlockSpec((b.shape[1], tm), lambda i, k: (0, i)) for b in bs]
        out_shape = [jax.ShapeDtypeStruct((b.shape[1], m), bf16) for b in bs]
    else:
        out_specs = [pl.BlockSpec((tm, b.shape[1]), lambda i, k: (i, 0)) for b in bs]
        out_shape = [jax.ShapeDtypeStruct((m, b.shape[1]), bf16) for b in bs]
    return pl.pallas_call(
        body, name=name, grid=(m // tm, n_k),
        in_specs=[pl.BlockSpec((tk, tm), lambda i, k: (k, i))] + [pl.BlockSpec((tk, b.shape[1]), lambda i, k: (k, 0)) for b in bs],
        out_specs=out_specs,
        out_shape=out_shape,
        scratch_shapes=[pltpu.VMEM((tm, b.shape[1]), f32) for b in bs],
        compiler_params=_cp(("parallel", "arbitrary"), VMEM_LIMIT),
    )(a, *bs)


def _mm_tn(a, b, name):
    kk, m = a.shape
    _, n = b.shape
    tm = _pick_tile(m, 1408)
    tn = _pick_tile(n, 1408)
    tk = min(1024, kk)

    n_k = kk // tk

    def body(a_ref, b_ref, o_ref, acc_ref):
        @pl.when(pl.program_id(2) == 0)
        def _():
            acc_ref[...] = jnp.zeros_like(acc_ref)
        acc_ref[...] += _dot_tn(a_ref[...].astype(bf16), b_ref[...].astype(bf16))

        @pl.when(pl.program_id(2) == n_k - 1)
        def _():
            o_ref[...] = acc_ref[...].astype(bf16)

    return pl.pallas_call(
        body, name=name, grid=(m // tm, n // tn, n_k),
        in_specs=[pl.BlockSpec((tk, tm), lambda i, j, k: (k, i)), pl.BlockSpec((tk, tn), lambda i, j, k: (k, j))],
        out_specs=pl.BlockSpec((tm, tn), lambda i, j, k: (i, j)),
        out_shape=jax.ShapeDtypeStruct((m, n), bf16),
        scratch_shapes=[pltpu.VMEM((tm, tn), f32)],
        compiler_params=_cp(("parallel", "parallel", "arbitrary"), VMEM_LIMIT),
    )(a, b)


def _rope_tables(positions):
    inv_freq = ROPE_THETA ** (-jnp.arange(0, QK_ROPE, 2, dtype=f32) / QK_ROPE)
    ang = positions.astype(f32)[:, None] * inv_freq
    cos, sin = jnp.cos(ang), jnp.sin(ang)
    zero = jnp.zeros_like(cos)
    return (jnp.concatenate([cos, cos, zero, zero], axis=1), jnp.concatenate([zero, sin, zero, zero], axis=1),
            jnp.concatenate([-sin, zero, zero, zero], axis=1))


def _pad256(g):
    return jnp.pad(g.reshape(1, QK_HEAD), ((0, 0), (0, QK_PAD - QK_HEAD)))


RELAYOUT_BLOCKS = 8
FIRST = ("w_in", "w_qb", "w_kvb", "lb_param")
SECOND = ("w_o", "w_gate", "w_up", "w_down", "w_ple_gate", "w_ple_proj")
ROW_SHARDED = ("w_o", "w_down", "w_ple_gate")


def _col_moves(j):
    lo = BIG["w_in"][1] * j
    w_in = [(max(lo, a) - lo, min(lo + BIG["w_in"][1], b) - lo, d + max(lo, a) - a)
            for a, b, d in Z_SEGMENTS if max(lo, a) < min(lo + BIG["w_in"][1], b)]
    head, half = divmod(j, 2)
    whole = lambda n: [(0, BIG[n][1], BIG[n][1] * j)]
    return {"w_in": w_in, "w_qb": [(0, 96, QK_PAD * head + 96 * half)], "w_kvb": whole("w_kvb"),
            "w_ple_proj": whole("w_ple_proj"), "lb_param": whole("lb_param")}


def _kernel_width(name):
    return {"w_in": Z_W, "w_qb": MLA_HEADS * QK_PAD}.get(name, N_DEV * BIG[name][1])


def _relayout_specs(names, by_dev):
    specs = []
    for n in names:
        rows, cols = BIG[n]
        if n == "lb_param":
            specs.append(_acc_spec((N_DEV, rows, cols) if by_dev else (rows, _kernel_width(n))))
        elif by_dev:
            specs.append(pl.BlockSpec((N_DEV, rows // RELAYOUT_BLOCKS, cols), lambda i: (0, i, 0)))
        else:
            specs.append(pl.BlockSpec((rows // RELAYOUT_BLOCKS, _kernel_width(n)), lambda i: (i, 0)))
    return specs


def _weights_in(gathered, names, name):
    n = len(names)

    def body(*refs):
        ins, outs = dict(zip(names, refs[:n])), dict(zip(names, refs[n:]))
        if "w_in" in outs:
            outs["w_in"][:, Z_KR + QK_ROPE:Z_W] = jnp.zeros((outs["w_in"].shape[0], Z_W - Z_KR - QK_ROPE), bf16)
        if "w_qb" in outs:
            for h in range(MLA_HEADS):
                outs["w_qb"][:, QK_PAD * h + QK_HEAD:QK_PAD * (h + 1)] = jnp.zeros((outs["w_qb"].shape[0], QK_PAD - QK_HEAD), bf16)
        for j in range(N_DEV):
            for wn, moves in _col_moves(j).items():
                if wn in outs:
                    for s0, s1, d0 in moves:
                        outs[wn][:, d0:d0 + s1 - s0] = ins[wn][j, :, s0:s1]

    outs = pl.pallas_call(
        body, name=name, grid=(RELAYOUT_BLOCKS,), in_specs=_relayout_specs(names, True), out_specs=_relayout_specs(names, False),
        out_shape=[jax.ShapeDtypeStruct((BIG[wn][0], _kernel_width(wn)), gathered[wn].dtype) for wn in names],
        compiler_params=_cp(("arbitrary",), VMEM_LIMIT),
    )(*[gathered[wn] for wn in names])
    return dict(zip(names, outs))


def _grads_out(sources, names, name):
    pieces = [(wn, start, arr) for wn in names for start, arr in sources[wn]]
    n_in = len(pieces)

    def body(*refs):
        outs = dict(zip(names, refs[n_in:]))

        def cols(wn, c0, c1):
            for (pn, start, arr), ref in zip(pieces, refs[:n_in]):
                if pn == wn and start <= c0 and c1 <= start + arr.shape[1]:
                    return ref[:, c0 - start:c1 - start]

        for j in range(N_DEV):
            for wn, moves in _col_moves(j).items():
                if wn in outs:
                    for s0, s1, d0 in moves:
                        outs[wn][j, :, s0:s1] = cols(wn, d0, d0 + s1 - s0).astype(bf16)

    in_specs = [_acc_spec(arr.shape) if wn == "lb_param" else pl.BlockSpec((arr.shape[0] // RELAYOUT_BLOCKS, arr.shape[1]), lambda i: (i, 0))
                for wn, _, arr in pieces]
    outs = pl.pallas_call(
        body, name=name, grid=(RELAYOUT_BLOCKS,), in_specs=in_specs, out_specs=_relayout_specs(names, True),
        out_shape=[jax.ShapeDtypeStruct((N_DEV, *BIG[wn]), bf16) for wn in names],
        compiler_params=_cp(("arbitrary",), VMEM_LIMIT),
    )(*[arr for _, _, arr in pieces])
    return dict(zip(names, outs))


def kernel(x, p, positions, g_mix, w_in, g_qa, g_kva, w_qb, w_kvb, g_qn, g_kn, lb_param, g_hgo, w_o, g_ffn, w_gate, w_up, w_down, g_ple, w_ple_gate, w_ple_proj, loss_target, m_g_mix, m_w_in, m_g_qa, m_g_kva, m_w_qb, m_w_kvb, m_g_qn, m_g_kn, m_lb_param, m_g_hgo, m_w_o, m_g_ffn, m_w_gate, m_w_up, m_w_down, m_g_ple, m_w_ple_gate, m_w_ple_proj, v_g_mix, v_w_in, v_g_qa, v_g_kva, v_w_qb, v_w_kvb, v_g_qn, v_g_kn, v_lb_param, v_g_hgo, v_w_o, v_g_ffn, v_w_gate, v_w_up, v_w_down, v_g_ple, v_w_ple_gate, v_w_ple_proj):
    w_all = dict(g_mix=g_mix, g_qa=g_qa, g_kva=g_kva, g_qn=g_qn, g_kn=g_kn, g_hgo=g_hgo, g_ffn=g_ffn, g_ple=g_ple,
                 w_in=w_in, w_qb=w_qb, w_kvb=w_kvb, w_o=w_o, w_gate=w_gate, w_up=w_up, w_down=w_down,
                 w_ple_gate=w_ple_gate, w_ple_proj=w_ple_proj, lb_param=lb_param)
    m_all = dict(g_mix=m_g_mix, g_qa=m_g_qa, g_kva=m_g_kva, g_qn=m_g_qn, g_kn=m_g_kn, g_hgo=m_g_hgo, g_ffn=m_g_ffn,
                 g_ple=m_g_ple, w_in=m_w_in, w_qb=m_w_qb, w_kvb=m_w_kvb, w_o=m_w_o, w_gate=m_w_gate, w_up=m_w_up,
                 w_down=m_w_down, w_ple_gate=m_w_ple_gate, w_ple_proj=m_w_ple_proj, lb_param=m_lb_param)
    v_all = dict(g_mix=v_g_mix, g_qa=v_g_qa, g_kva=v_g_kva, g_qn=v_g_qn, g_kn=v_g_kn, g_hgo=v_g_hgo, g_ffn=v_g_ffn,
                 g_ple=v_g_ple, w_in=v_w_in, w_qb=v_w_qb, w_kvb=v_w_kvb, w_o=v_w_o, w_gate=v_w_gate, w_up=v_w_up,
                 w_down=v_w_down, w_ple_gate=v_w_ple_gate, w_ple_proj=v_w_ple_proj, lb_param=v_lb_param)
    me_idx = jnp.stack([_me()]).astype(jnp.int32)
    x, p, positions, target = x[0], p[0, 0], positions[0], loss_target[0]
    s = x.shape[0]
    tm, tm_ffn, tq_f, tq_b = min(512, s), min(1024, s), min(2048, s), min(1024, s)
    g_mix, g_qa, g_kva, g_qn, g_kn, g_hgo, g_ffn, g_ple = (w_all[n].reshape(1, -1) for n in SMALL)
    g_qn_p, g_kn_p = _pad256(g_qn), _pad256(g_kn)
    cosb, sina, sinb = _rope_tables(positions)
    as_shard = lambda n, a: a[0].T if n in TRANSPOSED else a.reshape(BIG[n])
    shard = lambda n: as_shard(n, w_all[n])

    first = _all_gather([shard(n) for n in FIRST], [f32 if n == "lb_param" else bf16 for n in FIRST], "ag_first")
    lands = _cast_to_slot([shard(n) for n in SECOND], me_idx, first[0])
    ag2, token = _exchange_start([], lands, "ag_second_start")
    wk = _weights_in(dict(zip(FIRST, first)), FIRST, "weights_in_first")
    wz, wqb, wkvb, lb4 = (wk[n] for n in FIRST)

    h1, z = _fwd_in(x, g_mix, wz, tm)
    q, k, v = _fwd_mla_proj(z, cosb + token[0, 0], sina, sinb, g_qa, g_kva, wqb, wkvb, g_qn_p, g_kn_p, tm)
    a, a32 = _fwd_attn(q, k, v, tq_f)
    o, gla_b, gla_states = _fwd_gla(z, lb4)

    second = dict(zip(SECOND, _exchange_wait(ag2, [a, o], "ag_second_wait")[1]))
    w_pp = _weights_in(second, ("w_ple_proj",), "weights_in_second")["w_ple_proj"]
    w_o, w_down, w_pg, w_gate, w_up = (second[n].reshape(N_DEV * BIG[n][0], BIG[n][1]) for n in ROW_SHARDED + TRANSPOSED)

    x2, cat = _fwd_mix(a, o, z, g_hgo, x, w_o, tm)
    x3, gp, up = _fwd_ffn(x2, g_ffn, w_gate, w_up, w_down, tm)
    d3, h3, dpre, dpp, dg_ple, loss_tile = _ple_loss_fwd_bwd(x3, g_ple, w_pg, p, w_pp, target, tm)
    h2, act, dgp, dup = _bwd_ffn_hidden(d3, x2, gp, up, g_ffn, w_down, tm, D_FF // 2)
    d2, dg_ffn = _bwd_ffn_in(d3, x2, dgp, dup, g_ffn, w_gate, w_up, tm)

    gw_gate, gw_up = _mm_tn_many(h2, [dgp, dup], "dw_gate_up", 512, transposed=True)
    blocks = _grads_out({"w_ple_proj": [(0, _mm_tn(p, dpp, "dw_ple_proj"))]}, ("w_ple_proj",), "grads_out_second")
    row_grads = {"w_o": _mm_tn(cat, d2, "dw_o"), "w_down": _mm_tn(act, d3, "dw_down"), "w_ple_gate": _mm_tn(h3, dpre, "dw_ple_gate"),
                 "w_gate": gw_gate, "w_up": gw_up}
    blocks.update({n: g.reshape(N_DEV, *BIG[n]) for n, g in row_grads.items()})
    empty = lambda names: [lax.empty((N_PEERS, *BIG[n]), bf16) for n in names]
    rs2, token = _exchange_start([blocks[n] for n in SECOND], empty(SECOND), "rs_second_start")

    da, do, dz_hg, dg_hgo = _bwd_mix(d2, w_o, o, z, g_hgo + token[0, 0], tm)
    dz_hq, dz_hff, dz_hfb, dz_hi, dlb4 = _bwd_gla(z, lb4, do, gla_b, gla_states)
    dq, dk, dv = _bwd_attn(q, k, v, da, a32, tq_b)
    dz_mla, cqn, ckvn, dq0, dkv0, dg_qa, dg_kva, dg_qn, dg_kn = _bwd_mla_proj(
        z, dq, dk, dv, cosb, sina, sinb, g_qa, g_kva, wqb, wkvb, g_qn_p, g_kn_p, tm)

    gz = list(zip((Z_HQ, Z_HFF, Z_HFB, Z_HI, Z_HG, Z_CQ),
                  _mm_tn_many(h1, [dz_hq, dz_hff, dz_hfb, dz_hi, dz_hg, dz_mla], "dw_in", 1024)))
    blocks1 = _grads_out({"w_in": gz, "w_qb": [(0, _mm_tn(cqn, dq0, "dw_qb"))], "w_kvb": [(0, _mm_tn(ckvn, dkv0, "dw_kvb"))],
                          "lb_param": [(0, dlb4)]}, FIRST, "grads_out_first")
    rs1, token = _exchange_start([blocks1[n] for n in FIRST], empty(FIRST), "rs_first_start")

    result = {}

    def adam(names, lands, src, n_blocks, after=()):
        given = lambda arrs: [arrs[n][0].T if n in TRANSPOSED else arrs[n] for n in names]
        outs = _adam_shards(me_idx, [src[n] for n in names], lands, given(w_all), given(m_all), given(v_all), n_blocks,
                            "adamw_" + names[0], after)
        for n, o in zip(names, outs):
            result[n] = [t.T[None] for t in o] if n in TRANSPOSED else o
        return outs[0][0]

    blocks2, lands2 = (dict(zip(SECOND, arrs)) for arrs in _exchange_wait(rs2, [token], "rs_second_wait"))
    by2 = ("w_down",) + TRANSPOSED
    by8 = tuple(n for n in SECOND if n not in by2)
    done = [adam(by8, [lands2[n] for n in by8], blocks2, 8), adam(by2, [lands2[n] for n in by2], blocks2, 2)]

    segments = [(dz_hq, 512, 0, Z_HQ // 512), (dz_hff, 512, 0, Z_HFF // 512), (dz_hfb, 512, 0, Z_HFB // 512),
                (dz_hi, 512, 0, Z_HI // 512), (dz_hg, 512, 0, Z_HG // 512), (dz_mla, 640, 0, Z_CQ // 640)]
    grad_x, dg_mix = _bwd_in(segments, wz, x, g_mix + token[0, 0], d2, tm)
    dgains = (dg_mix, dg_qa, dg_kva, dg_qn, dg_kn, dg_hgo, dg_ffn, dg_ple)

    vec = jnp.concatenate(list(dgains) + [loss_tile[0:1]], axis=1)
    parts = _all_gather([vec], [f32], "ag_gains")[0]
    outs, loss_row = _adam_gains(parts, [w_all[n] for n in SMALL], [m_all[n] for n in SMALL], [v_all[n] for n in SMALL])
    result.update(zip(SMALL, outs))

    blocks1, lands1 = _exchange_wait(rs1, [grad_x, loss_row, *done], "rs_first_wait")
    adam(FIRST, lands1, dict(zip(FIRST, blocks1)), 8)

    order = ("g_mix", "w_in", "g_qa", "g_kva", "w_qb", "w_kvb", "g_qn", "g_kn", "lb_param", "g_hgo", "w_o", "g_ffn",
             "w_gate", "w_up", "w_down", "g_ple", "w_ple_gate", "w_ple_proj")
    return (loss_row[0, 0], grad_x[None], *[result[n][k] for k in range(4) for n in order])
```

```python
import functools
import math

import jax
import jax.numpy as jnp
from jax import lax
from jax.experimental import pallas as pl
from jax.experimental.pallas import tpu as pltpu

f32 = jnp.float32
bf16 = jnp.bfloat16

N_DEV = 8
MLA_HEADS = 4
QK_NOPE = 128
QK_ROPE = 64
QK_HEAD = QK_NOPE + QK_ROPE
QK_PAD = 256
V_HEAD = 128
Q_LORA = 256
KV_LORA = 256
HG_HEADS = 4
CHUNK = 64
D_FF = 2816
PLE_DIM = 256
ROPE_THETA = 10000.0
EPS = 1e-6
ATTN_SCALE = QK_HEAD ** -0.5
LOG2_E = math.log2(math.e)
ATTN_SUB_ROWS = 256
Z_HQ, Z_HFF, Z_HFB, Z_HI, Z_HG, Z_CQ, Z_CKV, Z_KR, Z_W = 0, 512, 1024, 1536, 2048, 2560, 2816, 3072, 3200

ADAM_LR, ADAM_B1, ADAM_B2, ADAM_EPS, ADAM_WD, ADAM_STEP = 0.001, 0.9, 0.999, 1e-08, 0.01, 10

LANES = 128
BIG = {"w_in": (1024, 392), "w_qb": (256, 96), "w_kvb": (256, 128), "w_o": (128, 1024), "w_gate": (352, 1024),
       "w_up": (352, 1024), "w_down": (352, 1024), "w_ple_gate": (128, 1024), "w_ple_proj": (256, 128),
       "lb_param": (4, 64)}
TRANSPOSED = ("w_gate", "w_up")
SMALL = {"g_mix": (0, 1024), "g_qa": (1024, 256), "g_kva": (1280, 256), "g_qn": (1536, 192), "g_kn": (1792, 192),
         "g_hgo": (2048, 512), "g_ffn": (2560, 1024), "g_ple": (3584, 1024)}
LOSS_OFF = 4608
GAIN_VEC = LOSS_OFF + LANES
Z_SEGMENTS = ((0, 256, Z_CQ), (256, 512, Z_CKV), (512, 576, Z_KR), (576, 1088, Z_HQ), (1088, 1600, Z_HFF),
              (1600, 2112, Z_HFB), (2112, 2624, Z_HI), (2624, 3136, Z_HG))

VMEM_LIMIT = 56 * 1024 * 1024
MESH = pl.DeviceIdType.MESH


def _cp(sem=None, vmem=None):
    return pltpu.CompilerParams(dimension_semantics=sem, vmem_limit_bytes=vmem)


def _const_spec(shape):
    nd = len(shape)
    return pl.BlockSpec(shape, lambda *_: (0,) * nd, pipeline_mode=pl.Buffered(1))


def _acc_spec(shape):
    nd = len(shape)
    return pl.BlockSpec(shape, lambda *_: (0,) * nd)


def _sigmoid(x):
    return jax.nn.sigmoid(x)


def _dot(a, b):
    return jnp.dot(a, b, preferred_element_type=f32)


def _dot_nt(a, b):
    return lax.dot_general(a, b, (((1,), (1,)), ((), ())), preferred_element_type=f32)


def _dot_tn(a, b):
    return lax.dot_general(a, b, (((0,), (0,)), ((), ())), preferred_element_type=f32)


def _rms_fwd(x, g, width):
    r = lax.rsqrt(jnp.sum(x * x, axis=-1, keepdims=True) * (1.0 / width) + EPS)
    return x * r * g, r


def _rms_bwd(dy, x, r, g, width):
    u = dy * g
    dx = r * u - x * (r * r * r) * (jnp.sum(u * x, axis=-1, keepdims=True) * (1.0 / width))
    return dx, dy * x * r


class _Both:
    def __init__(self, *copies):
        self.copies = copies

    def start(self):
        for cp in self.copies:
            cp.start()

    def wait(self):
        for cp in self.copies:
            cp.wait()


def _rope(b, c, sa, sb):
    return b * c + pltpu.roll(b, 32, 1) * sa + pltpu.roll(b, 96, 1) * sb


def _all_gather(shards, dtypes, name):
    n = len(shards)

    def body(*refs):
        in_refs, out_refs, stage = refs[:n], refs[n:2 * n], refs[2 * n:3 * n]
        send_sems, recv_sems, local_sems = refs[3 * n:]
        for w in range(n):
            stage[w][...] = in_refs[w][...].astype(stage[w].dtype)
        x, y, c = lax.axis_index("x"), lax.axis_index("y"), lax.axis_index("c")
        me, sibling = (x, y, c), (x, y, 1 - c)
        chips = [(1 - x, y), (x, 1 - y), (1 - x, 1 - y)]

        def slot(w, px, py, pc):
            return out_refs[w].at[4 * px + 2 * py + pc]

        def copy(w, k, block, to, src=None):
            return pltpu.make_async_remote_copy(
                src_ref=slot(w, *block) if src is None else src, dst_ref=slot(w, *block),
                send_sem=send_sems.at[w, k], recv_sem=recv_sems.at[w, k], device_id=to, device_id_type=MESH)

        first = []
        for j, chip in enumerate(chips):
            first += [copy(w, 1 + j, me, (*chip, c), src=stage[w]) for w in range(n)]
        first += [copy(w, 0, me, sibling, src=stage[w]) for w in range(n)]
        mine = [pltpu.make_async_copy(stage[w], slot(w, *me), local_sems.at[w]) for w in range(n)]
        for cp in first + mine:
            cp.start()
        passed = []
        for j, chip in enumerate(chips):
            for w in range(n):
                copy(w, 1 + j, (*chip, c), me).wait_recv()
                passed.append(copy(w, 4 + j, (*chip, c), sibling))
                passed[-1].start()
        for w in range(n):
            copy(w, 0, sibling, me).wait_recv()
        for j, chip in enumerate(chips):
            for w in range(n):
                copy(w, 4 + j, (*chip, 1 - c), me).wait_recv()
        for cp in first + passed:
            cp.wait_send()
        for cp in mine:
            cp.wait()

    return pl.pallas_call(
        body, name=name,
        out_shape=[jax.ShapeDtypeStruct((N_DEV, *s.shape), dt) for s, dt in zip(shards, dtypes)],
        in_specs=[pl.BlockSpec(memory_space=pltpu.VMEM)] * n,
        out_specs=[pl.BlockSpec(memory_space=pl.ANY)] * n,
        scratch_shapes=[pltpu.VMEM(s.shape, dt) for s, dt in zip(shards, dtypes)]
        + [pltpu.SemaphoreType.DMA((n, 7)), pltpu.SemaphoreType.DMA((n, 7)), pltpu.SemaphoreType.DMA((n,))],
        compiler_params=_cp(None, VMEM_LIMIT),
    )(*shards)


N_PEERS = N_DEV - 1
HBM_SPEC = pl.BlockSpec(memory_space=pltpu.HBM)
SEM_SPEC = pl.BlockSpec(memory_space=pltpu.SEMAPHORE)
DATAFLOW = pltpu.SideEffectType.DATAFLOW_SIDE_EFFECTING


def _me():
    return 4 * lax.axis_index("x") + 2 * lax.axis_index("y") + lax.axis_index("c")


def _peer(k):
    x, y, c = lax.axis_index("x"), lax.axis_index("y"), lax.axis_index("c")
    px = 1 - x if k & 4 else x
    py = 1 - y if k & 2 else y
    pc = 1 - c if k & 1 else c
    return (px, py, pc), 4 * px + 2 * py + pc


def _exchange_copies(src_refs, land_refs, send_sems, recv_sems, gather):
    cps = []
    me = _me()
    for k in range(1, N_DEV):
        peer, peer_idx = _peer(k)
        for w, land in enumerate(land_refs):
            src = land.at[me] if gather else src_refs[w].at[peer_idx]
            dst = land.at[me] if gather else land.at[k - 1]
            cps.append(pltpu.make_async_remote_copy(
                src_ref=src, dst_ref=dst, send_sem=send_sems.at[N_PEERS * w + k - 1], recv_sem=recv_sems.at[N_PEERS * w + k - 1],
                device_id=peer, device_id_type=MESH))
    return cps


def _exchange_start(srcs, lands, name):
    n_src, n = len(srcs), len(lands)

    def body(*refs):
        src_refs, land_refs = refs[:n_src], refs[n_src:n_src + n]
        send_sems, recv_sems = refs[n_src + n], refs[n_src + n + 1]
        token = refs[-1]
        for cp in _exchange_copies(src_refs, land_refs, send_sems, recv_sems, gather=not n_src):
            cp.start()
        token[...] = jnp.zeros_like(token)

    arrays = [pltpu.with_memory_space_constraint(a, pltpu.HBM) for a in (*srcs, *lands)]
    outs = pl.pallas_call(
        body, name=name,
        out_shape=(pltpu.SemaphoreType.DMA((n * N_PEERS,)), pltpu.SemaphoreType.DMA((n * N_PEERS,)),
                   *[pltpu.HBM(a.shape, a.dtype) for a in arrays], jax.ShapeDtypeStruct((8, LANES), f32)),
        in_specs=[HBM_SPEC] * len(arrays),
        out_specs=(SEM_SPEC, SEM_SPEC, *[HBM_SPEC] * len(arrays), pl.BlockSpec(memory_space=pltpu.VMEM)),
        input_output_aliases={i: 2 + i for i in range(len(arrays))},
        compiler_params=pltpu.CompilerParams(has_side_effects=DATAFLOW),
    )(*arrays)
    return (outs[0], outs[1], outs[2:2 + n_src], outs[2 + n_src:2 + n_src + n]), outs[-1]


def _exchange_wait(state, after, name):
    send_sems, recv_sems, srcs, lands = state
    n_src, n = len(srcs), len(lands)

    def body(*refs):
        src_refs, land_refs = refs[:n_src], refs[n_src:n_src + n]
        send_ref, recv_ref = refs[n_src + n], refs[n_src + n + 1]
        for cp in _exchange_copies(src_refs, land_refs, send_ref, recv_ref, gather=not n_src):
            cp.wait_send()
            cp.wait_recv()

    arrays = (*srcs, *lands)
    outs = pl.pallas_call(
        body, name=name,
        out_shape=tuple(pltpu.HBM(a.shape, a.dtype) for a in arrays),
        in_specs=[HBM_SPEC] * len(arrays) + [SEM_SPEC, SEM_SPEC] + [pl.BlockSpec(memory_space=pl.ANY)] * len(after),
        out_specs=tuple([HBM_SPEC] * len(arrays)),
        input_output_aliases={i: i for i in range(len(arrays))},
        compiler_params=pltpu.CompilerParams(has_side_effects=DATAFLOW),
    )(*arrays, send_sems, recv_sems, *after)
    return outs[:n_src], outs[n_src:]


def _cast_to_slot(shards, me_idx, after):
    n = len(shards)

    def body(i_ref, *refs):
        for w in range(n):
            refs[n + 1 + w][...] = refs[w][...].astype(bf16)

    return pl.pallas_call(
        body, name="cast_to_slot",
        grid_spec=pltpu.PrefetchScalarGridSpec(
            num_scalar_prefetch=1, grid=(1,),
            in_specs=[pl.BlockSpec(s.shape, lambda i, m: (0, 0)) for s in shards] + [pl.BlockSpec(memory_space=pl.ANY)],
            out_specs=[pl.BlockSpec((None, *s.shape), lambda i, m: (m[0], 0, 0)) for s in shards]),
        out_shape=[jax.ShapeDtypeStruct((N_DEV, *s.shape), bf16) for s in shards],
        compiler_params=_cp(("arbitrary",), VMEM_LIMIT),
    )(me_idx, *shards, after)


def _row_block(rows, n_blocks):
    return (rows // n_blocks, True) if rows % (16 * n_blocks) == 0 else (rows, False)


def _adam_math(w, g, m, v):
    m = ADAM_B1 * m + (1.0 - ADAM_B1) * g
    v = ADAM_B2 * v + (1.0 - ADAM_B2) * (g * g)
    m_hat = m / (1.0 - ADAM_B1 ** ADAM_STEP)
    v_hat = v / (1.0 - ADAM_B2 ** ADAM_STEP)
    delta = -ADAM_LR * (m_hat / (jnp.sqrt(v_hat) + ADAM_EPS) + ADAM_WD * w)
    return delta, m, v


def _adam_shards(me_idx, blocks, lands, ws, ms, vs, n_blocks, name, after=()):
    n = len(blocks)
    turned = [wt.ndim == 2 and wt.shape != g.shape[1:] for g, wt in zip(blocks, ws)]

    def body(i_ref, *refs):
        ins, outs = refs[:5 * n], refs[5 * n + len(after):]
        for w in range(n):
            g_ref, b_ref, w_ref, m_ref, v_ref = (ins[t * n + w] for t in range(5))
            g = g_ref[...].astype(f32)
            for k in range(N_PEERS):
                g = g + b_ref[k].astype(f32)
            if turned[w]:
                rb, cols = g.shape
                pad = -cols % LANES
                g = jnp.concatenate([g, jnp.zeros((rb, pad), f32)], axis=-1).T[:cols, :]
            if len(w_ref.shape) == 2:
                pieces = [(slice(None), g)]
            else:
                pieces = [(a, g[2 * a:2 * a + 2]) for a in range(2)]
            for at, gp in pieces:
                vals = (gp,) + _adam_math(w_ref[at], gp, m_ref[at], v_ref[at])
                for t, val in enumerate(vals):
                    outs[4 * w + t][at] = val

    specs = [[] for _ in range(5)]
    out_specs, out_shape = [], []
    for g, wt, turn in zip(blocks, ws, turned):
        rows, cols = g.shape[1:]
        rb, cut = _row_block(rows, n_blocks)
        specs[0].append(pl.BlockSpec((None, rb, cols), functools.partial(lambda i, s, cut: (s[0], i if cut else 0, 0), cut=cut)))
        specs[1].append(pl.BlockSpec((N_PEERS, rb, cols), functools.partial(lambda i, s, cut: (0, i if cut else 0, 0), cut=cut)))
        if turn:
            assert cut and rb % LANES == 0
            shard = pl.BlockSpec((cols, rb), lambda i, s: (0, i))
        elif wt.ndim == 2:
            shard = pl.BlockSpec((rb, cols), functools.partial(lambda i, s, cut: (i if cut else 0, 0), cut=cut))
        elif wt.shape[0] == 1:
            shard = pl.BlockSpec((None, rb, cols), functools.partial(lambda i, s, cut: (0, i if cut else 0, 0), cut=cut))
        else:
            shard = pl.BlockSpec(wt.shape, functools.partial(lambda i, s, nd: (0,) * nd, nd=wt.ndim))
        for t in (2, 3, 4):
            specs[t].append(shard)
        out_specs += [shard] * 4
        out_shape += [jax.ShapeDtypeStruct(wt.shape, f32)] * 4
    outs = pl.pallas_call(
        body, name=name,
        grid_spec=pltpu.PrefetchScalarGridSpec(
            num_scalar_prefetch=1, grid=(n_blocks,), in_specs=sum(specs, []) + [pl.BlockSpec(memory_space=pl.ANY)] * len(after),
            out_specs=out_specs),
        out_shape=out_shape,
        compiler_params=_cp(("arbitrary",), VMEM_LIMIT),
    )(me_idx, *blocks, *lands, *ws, *ms, *vs, *after)
    return [outs[4 * w:4 * w + 4] for w in range(n)]


def _adam_gains(parts, ws, ms, vs):
    n = len(ws)

    def body(p_ref, *refs):
        ins, outs = refs[:3 * n], refs[3 * n:]
        g_all = p_ref[0]
        for k in range(1, N_DEV):
            g_all = g_all + p_ref[k]
        for w, (off, lanes) in enumerate(SMALL.values()):
            w_ref, m_ref, v_ref = ins[w], ins[n + w], ins[2 * n + w]
            if len(w_ref.shape) == 2:
                pieces = [(slice(None), off, lanes)]
            else:
                pieces = [((slice(None), h), off + LANES * h, LANES) for h in range(w_ref.shape[1])]
            for at, o, ln in pieces:
                g = g_all[:, o:o + ln]
                vals = (g,) + _adam_math(w_ref[at], g, m_ref[at], v_ref[at])
                for t, val in enumerate(vals):
                    outs[4 * w + t][at] = val
        outs[4 * n][...] = g_all[:, LOSS_OFF:LOSS_OFF + LANES]

    out_shape = sum([[jax.ShapeDtypeStruct(w.shape, f32)] * 4 for w in ws], []) + [jax.ShapeDtypeStruct((1, LANES), f32)]
    outs = pl.pallas_call(body, name="adamw_gains", out_shape=out_shape)(parts, *ws, *ms, *vs)
    return [outs[4 * w:4 * w + 4] for w in range(n)], outs[4 * n]


def _fwd_in(x, g_mix, wz, tm):
    s, d = x.shape

    def body(x_ref, g_ref, w_ref, h_ref, z_ref):
        h, _ = _rms_fwd(x_ref[...], g_ref[...], d)
        hb = h.astype(bf16)
        h_ref[...] = hb
        z_ref[...] = _dot(hb, w_ref[...])

    return pl.pallas_call(
        body, name="fwd_in", grid=(s // tm,),
        in_specs=[pl.BlockSpec((tm, d), lambda i: (i, 0)), _const_spec((1, d)), _const_spec((d, Z_W))],
        out_specs=[pl.BlockSpec((tm, d), lambda i: (i, 0)), pl.BlockSpec((tm, Z_W), lambda i: (i, 0))],
        out_shape=[jax.ShapeDtypeStruct((s, d), bf16), jax.ShapeDtypeStruct((s, Z_W), f32)],
        compiler_params=_cp(("parallel",), VMEM_LIMIT),
    )(x, g_mix, wz)


def _mla_qk_fwd(cq, ckv, g_qa, g_kva, wqb, wkvb):
    cqn, rq = _rms_fwd(cq, g_qa, Q_LORA)
    ckvn, rkv = _rms_fwd(ckv, g_kva, KV_LORA)
    cqn_b, ckvn_b = cqn.astype(bf16), ckvn.astype(bf16)
    q0 = _dot(cqn_b, wqb)
    kv0 = _dot(ckvn_b, wkvb)
    return cqn_b, rq, ckvn_b, rkv, q0, kv0


def _fwd_mla_proj(z, cosb, sina, sinb, g_qa, g_kva, wqb, wkvb, g_qn, g_kn, tm):
    s = z.shape[0]
    hh = MLA_HEADS

    def body(cq_ref, ckv_ref, kr_ref, c_ref, sa_ref, sb_ref, gqa_ref, gkva_ref, wqb_ref, wkvb_ref, gqn_ref, gkn_ref,
             q_ref, k_ref, v_ref):
        _, _, _, _, q0, kv0 = _mla_qk_fwd(cq_ref[...], ckv_ref[...], gqa_ref[...], gkva_ref[...], wqb_ref[...], wkvb_ref[...])
        kr = kr_ref[...]
        c, sa, sb = c_ref[...], sa_ref[...], sb_ref[...]
        gqn, gkn = gqn_ref[...], gkn_ref[...]
        kr_sq = jnp.sum(kr * kr, axis=-1, keepdims=True)
        for h in range(hh):
            qh = q0[:, QK_PAD * h:QK_PAD * (h + 1)]
            qn, _ = _rms_fwd(qh, gqn, QK_HEAD)
            q_ref[h, :, 0:128] = qn[:, 0:128].astype(bf16)
            q_ref[h, :, 128:256] = _rope(qn[:, 128:256], c, sa, sb).astype(bf16)
            kn_ = kv0[:, 256 * h:256 * h + 128]
            rk = lax.rsqrt((jnp.sum(kn_ * kn_, axis=-1, keepdims=True) + kr_sq) * (1.0 / QK_HEAD) + EPS)
            k_ref[h, :, 0:128] = (kn_ * rk * gkn[:, 0:128]).astype(bf16)
            k_ref[h, :, 128:256] = _rope(kr * rk * gkn[:, 128:256], c, sa, sb).astype(bf16)
            v_ref[h] = kv0[:, 256 * h + 128:256 * h + 256].astype(bf16)

    row128 = pl.BlockSpec((tm, 128), lambda i: (i, 0))
    return pl.pallas_call(
        body, name="fwd_mla_proj", grid=(s // tm,),
        in_specs=[pl.BlockSpec((tm, 256), lambda i: (i, Z_CQ // 256)), pl.BlockSpec((tm, 256), lambda i: (i, Z_CKV // 256)),
                  pl.BlockSpec((tm, 128), lambda i: (i, Z_KR // 128)), row128, row128, row128,
                  _const_spec((1, 256)), _const_spec((1, 256)), _const_spec((256, 1024)), _const_spec((256, 1024)),
                  _const_spec((1, 256)), _const_spec((1, 256))],
        out_specs=[pl.BlockSpec((hh, tm, QK_PAD), lambda i: (0, i, 0)), pl.BlockSpec((hh, tm, QK_PAD), lambda i: (0, i, 0)),
                   pl.BlockSpec((hh, tm, V_HEAD), lambda i: (0, i, 0))],
        out_shape=[jax.ShapeDtypeStruct((hh, s, QK_PAD), bf16), jax.ShapeDtypeStruct((hh, s, QK_PAD), bf16),
                   jax.ShapeDtypeStruct((hh, s, V_HEAD), bf16)],
        compiler_params=_cp(("parallel",), VMEM_LIMIT),
    )(z, z, z, cosb, sina, sinb, g_qa, g_kva, wqb, wkvb, g_qn, g_kn)


def _fwd_attn(q, k, v, tq):
    hh, s, _ = q.shape

    n_sub = max(1, tq // ATTN_SUB_ROWS)

    def body(q_ref, k_ref, v_ref, o_ref, o32_ref):
        for t in range(n_sub):
            rows = slice(t * (tq // n_sub), (t + 1) * (tq // n_sub))
            sc = _dot_nt(q_ref[rows, :], k_ref[...])
            p = jnp.exp2((sc - jnp.max(sc, axis=-1, keepdims=True)) * (ATTN_SCALE * LOG2_E))
            l = jnp.sum(p, axis=-1, keepdims=True)
            o = _dot(p.astype(bf16), v_ref[...]) * (1.0 / l)
            o_ref[rows, :] = o.astype(bf16)
            o32_ref[rows, :] = o

    out = pl.BlockSpec((tq, V_HEAD), lambda h, i: (i, h))
    return pl.pallas_call(
        body, name="fwd_attn", grid=(hh, s // tq),
        in_specs=[pl.BlockSpec((None, tq, QK_PAD), lambda h, i: (h, i, 0)),
                  pl.BlockSpec((None, s, QK_PAD), lambda h, i: (h, 0, 0)),
                  pl.BlockSpec((None, s, V_HEAD), lambda h, i: (h, 0, 0))],
        out_specs=[out, out],
        out_shape=[jax.ShapeDtypeStruct((s, hh * V_HEAD), bf16), jax.ShapeDtypeStruct((s, hh * V_HEAD), f32)],
        compiler_params=_cp(("parallel", "parallel"), VMEM_LIMIT),
    )(q, k, v)


def _split3(x):
    hi = x.astype(bf16)
    r1 = x - hi.astype(f32)
    mid = r1.astype(bf16)
    lo = (r1 - mid.astype(f32)).astype(bf16)
    return jnp.concatenate([hi, mid, lo], axis=-1)


def _tri_sum(tri, x):
    y = _dot(tri, _split3(x))
    return y[:, 0:128] + y[:, 128:256] + y[:, 256:384]


GLA_GROUP = 4
GLA_ROWS = GLA_GROUP * CHUNK
GLA_HEADS_PER_STEP = 2


def _gla_masks(rev):
    row = lax.broadcasted_iota(jnp.int32, (GLA_ROWS, GLA_ROWS), 0)
    col = lax.broadcasted_iota(jnp.int32, (GLA_ROWS, GLA_ROWS), 1)
    shift = CHUNK.bit_length() - 1
    same = (jnp.right_shift(row, shift) == jnp.right_shift(col, shift)).astype(f32)
    lower, upper = (row >= col).astype(f32) * same, (row <= col).astype(f32) * same
    keep, keep_t = (upper, lower) if rev else (lower, upper)
    chunk_of = jnp.right_shift(lax.broadcasted_iota(jnp.int32, (GLA_ROWS, 1), 0), shift)
    return keep, keep.astype(bf16), keep_t.astype(bf16), [(chunk_of == c).astype(f32) for c in range(GLA_GROUP)]


def _gla_gates(hq, hf, lower):
    sg = _sigmoid(hf)
    f = lower + (1.0 - lower) * sg
    return hq * _sigmoid(hq), 1.0 - f, jnp.log(f), f, sg


def _gla_last_mid(b, rev):
    b3 = b.reshape(GLA_GROUP, CHUNK, 128)
    last, mid = (0, CHUNK // 2) if rev else (CHUNK - 1, CHUNK // 2 - 1)
    return b3[:, last:last + 1, :], b3[:, mid:mid + 1, :]


def _gla_per_row(per_chunk):
    return jnp.broadcast_to(per_chunk, (GLA_GROUP, CHUNK, 128)).reshape(GLA_ROWS, 128)


def _gla_block_diag(x, row_masks):
    return jnp.concatenate([(x * m).astype(bf16) for m in row_masks], axis=-1)


def _gla_diag(y):
    return jnp.concatenate([y[CHUNK * c:CHUNK * (c + 1), 128 * c:128 * (c + 1)] for c in range(GLA_GROUP)], axis=0)


def _gla_rows(n, n_groups, rev):
    ne = n_groups - 1 - n if rev else n
    return pl.ds(pl.multiple_of(ne * GLA_ROWS, GLA_ROWS), GLA_ROWS), ne * GLA_GROUP


def _gla_scan_order(rev):
    return tuple(reversed(range(GLA_GROUP))) if rev else tuple(range(GLA_GROUP))


def _fwd_gla(z, lb4):
    s = z.shape[0]
    n_groups = s // GLA_ROWS
    assert n_groups % 2 == 0
    hp = GLA_HEADS_PER_STEP
    chains = [(hh, rev) for hh in range(hp) for rev in (False, True)]

    def body(hq_ref, hff_ref, hfb_ref, hi_ref, lb_ref, o_ref, b_ref, states_ref, st_ref, stage_ref, b_stage, sems):
        st_ref[...] = jnp.zeros_like(st_ref)
        masks = {rev: _gla_masks(rev) for rev in (False, True)}
        lowers = [_sigmoid(lb_ref[int(rev):int(rev) + 1, 128 * hh:128 * (hh + 1)]
                           - lb_ref[2 + int(rev):3 + int(rev), 128 * hh:128 * (hh + 1)]) for hh, rev in chains]

        def states_out(slot, ci, chunk0):
            hh, rev = chains[ci]
            head = pl.program_id(0) * hp + hh
            rows = pl.ds(pl.multiple_of(chunk0 * CHUNK, GLA_ROWS), GLA_ROWS)
            return _Both(
                pltpu.make_async_copy(stage_ref.at[slot, ci], states_ref.at[head, int(rev), pl.ds(chunk0, GLA_GROUP)],
                                      sems.at[slot, ci]),
                pltpu.make_async_copy(b_stage.at[slot, ci], b_ref.at[int(rev), rows, pl.ds(pl.multiple_of(head * 128, 128), 128)],
                                      sems.at[slot, len(chains) + ci]))

        def make_step(first):
            def step(n, carry):
                slot = n % 2

                @pl.when(n >= 2)
                def _():
                    for ci in range(len(chains)):
                        states_out(slot, ci, 0).wait()

                for ci, (hh, rev) in enumerate(chains):
                    cols = slice(128 * hh, 128 * (hh + 1))
                    rows, chunk0 = _gla_rows(n, n_groups, rev)
                    maskf, tri, _, row_masks = masks[rev]
                    hf_ref = hfb_ref if rev else hff_ref
                    q, k, logf, _, _ = _gla_gates(hq_ref[rows, cols], hf_ref[rows, cols], lowers[ci])
                    vb = hi_ref[rows, cols].astype(bf16)
                    b = _tri_sum(tri, logf)
                    b_stage[slot, ci] = b
                    b_last3, b_mid3 = _gla_last_mid(b, rev)
                    b_last, b_mid = _gla_per_row(b_last3), _gla_per_row(b_mid3)
                    qi = (q * jnp.exp(b - b_mid)).astype(bf16)
                    ki = (k * jnp.exp(b_mid - b)).astype(bf16)
                    a = (_dot_nt(qi, ki) * maskf).astype(bf16)
                    kv = _dot_tn(vb, _gla_block_diag(k * jnp.exp(b_last - b), row_masks))
                    decay3 = jnp.exp(b_last3)
                    st = st_ref[ci]
                    before = [None] * GLA_GROUP
                    for c in _gla_scan_order(rev):
                        stage_ref[slot, ci, c] = st
                        before[c] = st.astype(bf16)
                        st = st * decay3[c] + kv[:, 128 * c:128 * (c + 1)]
                    st_ref[ci] = st
                    states_out(slot, ci, chunk0).start()
                    inter = _dot_nt((q * jnp.exp(b)).astype(bf16), jnp.concatenate(before, axis=0))
                    o = _dot(a, vb) + _gla_diag(inter)
                    if first:
                        o_ref[rows, cols] = o
                    else:
                        o_ref[rows, cols] += o
                return carry
            return step

        lax.fori_loop(0, n_groups // 2, make_step(True), 0)
        lax.fori_loop(n_groups // 2, n_groups, make_step(False), 0)
        for slot in range(2):
            for ci in range(len(chains)):
                states_out(slot, ci, 0).wait()

    w = 128 * hp
    col = lambda base: pl.BlockSpec((s, w), lambda h: (0, base // w + h))
    return pl.pallas_call(
        body, name="fwd_gla", grid=(HG_HEADS // hp,),
        in_specs=[col(Z_HQ), col(Z_HFF), col(Z_HFB), col(Z_HI), pl.BlockSpec((4, w), lambda h: (0, h))],
        out_specs=[pl.BlockSpec((s, w), lambda h: (0, h)), pl.BlockSpec(memory_space=pl.ANY), pl.BlockSpec(memory_space=pl.ANY)],
        out_shape=[jax.ShapeDtypeStruct((s, HG_HEADS * 128), f32), jax.ShapeDtypeStruct((2, s, HG_HEADS * 128), f32),
                   jax.ShapeDtypeStruct((HG_HEADS, 2, s // CHUNK, 128, 128), f32)],
        scratch_shapes=[pltpu.VMEM((len(chains), 128, 128), f32), pltpu.VMEM((2, len(chains), GLA_GROUP, 128, 128), f32),
                        pltpu.VMEM((2, len(chains), GLA_ROWS, 128), f32), pltpu.SemaphoreType.DMA((2, 2 * len(chains)))],
        compiler_params=_cp(("parallel",), VMEM_LIMIT),
    )(z, z, z, z, lb4)


def _hg_out(o, hg, g_hgo):
    outs, ons, rs = [], [], []
    for h in range(HG_HEADS):
        oh = o[:, 128 * h:128 * (h + 1)]
        on, r = _rms_fwd(oh, g_hgo[:, 128 * h:128 * (h + 1)], 128)
        ons.append(on)
        rs.append(r)
    on = jnp.concatenate(ons, axis=-1)
    sg = _sigmoid(hg)
    return on * (hg * sg), on, rs, sg


def _fwd_mix(a, o, z, g_hgo, x, w_o, tm):
    s, d = x.shape

    def body(a_ref, o_ref, hg_ref, g_ref, x_ref, w_ref, x2_ref, cat_ref):
        r, _, _, _ = _hg_out(o_ref[...], hg_ref[...], g_ref[...])
        cat = jnp.concatenate([a_ref[...], r.astype(bf16)], axis=-1)
        cat_ref[...] = cat
        x2_ref[...] = x_ref[...] + _dot(cat, w_ref[...])

    row512 = pl.BlockSpec((tm, 512), lambda i: (i, 0))
    rowd = pl.BlockSpec((tm, d), lambda i: (i, 0))
    return pl.pallas_call(
        body, name="fwd_mix", grid=(s // tm,),
        in_specs=[row512, row512, pl.BlockSpec((tm, 512), lambda i: (i, Z_HG // 512)), _const_spec((1, 512)), rowd,
                  _const_spec((d, d))],
        out_specs=[rowd, rowd],
        out_shape=[jax.ShapeDtypeStruct((s, d), f32), jax.ShapeDtypeStruct((s, d), bf16)],
        compiler_params=_cp(("parallel",), VMEM_LIMIT),
    )(a, o, z, g_hgo, x, w_o)


def _fwd_ffn(x2, g_ffn, w_gate, w_up, w_down, tm):
    s, d = x2.shape

    def body(x_ref, g_ref, wg_ref, wu_ref, wd_ref, x3_ref, gp_ref, up_ref):
        x = x_ref[...]
        h, _ = _rms_fwd(x, g_ref[...], d)
        hb = h.astype(bf16)
        gp = _dot_nt(hb, wg_ref[...])
        up = _dot_nt(hb, wu_ref[...])
        gp_ref[...] = gp.astype(bf16)
        up_ref[...] = up.astype(bf16)
        act = (gp * _sigmoid(gp) * up).astype(bf16)
        x3_ref[...] = x + _dot(act, wd_ref[...])

    rowd = pl.BlockSpec((tm, d), lambda i: (i, 0))
    rowf = pl.BlockSpec((tm, D_FF), lambda i: (i, 0))
    return pl.pallas_call(
        body, name="fwd_ffn", grid=(s // tm,),
        in_specs=[rowd, _const_spec((1, d)), _const_spec((D_FF, d)), _const_spec((D_FF, d)), _const_spec((D_FF, d))],
        out_specs=[rowd, rowf, rowf],
        out_shape=[jax.ShapeDtypeStruct((s, d), f32), jax.ShapeDtypeStruct((s, D_FF), bf16),
                   jax.ShapeDtypeStruct((s, D_FF), bf16)],
        compiler_params=_cp(("parallel",), VMEM_LIMIT),
    )(x2, g_ffn, w_gate, w_up, w_down)


def _ple_loss_fwd_bwd(x3, g_ple, w_pg, p, w_pp, target, tm):
    s, d = x3.shape

    def body(x_ref, g_ref, wg_ref, p_ref, wp_ref, t_ref, dx_ref, h_ref, dpre_ref, dpp_ref, dg_ref, loss_ref):
        @pl.when(pl.program_id(0) == 0)
        def _():
            dg_ref[...] = jnp.zeros_like(dg_ref)
            loss_ref[...] = jnp.zeros_like(loss_ref)

        x = x_ref[...]
        g = g_ref[...]
        h, r = _rms_fwd(x, g, d)
        hb = h.astype(bf16)
        gate = _sigmoid(_dot(hb, wg_ref[...]))
        pp = _dot(p_ref[...].astype(bf16), wp_ref[...])
        e = x + gate * pp - t_ref[...]
        loss_ref[...] += 0.5 * jnp.sum(e * e) * (1.0 / d)
        dy = e * (1.0 / d)
        dpre = (dy * pp * gate * (1.0 - gate)).astype(bf16)
        dx, dgx = _rms_bwd(_dot_nt(dpre, wg_ref[...]), x, r, g, d)
        dx_ref[...] = dy + dx
        dg_ref[...] += jnp.sum(dgx, axis=0, keepdims=True)
        h_ref[...] = hb
        dpre_ref[...] = dpre
        dpp_ref[...] = (dy * gate).astype(bf16)

    rowd = pl.BlockSpec((tm, d), lambda i: (i, 0))
    return pl.pallas_call(
        body, name="ple_loss_fwd_bwd", grid=(s // tm,),
        in_specs=[rowd, _const_spec((1, d)), _const_spec((d, d)), pl.BlockSpec((tm, PLE_DIM), lambda i: (i, 0)),
                  _const_spec((PLE_DIM, d)), rowd],
        out_specs=[rowd, rowd, rowd, rowd, _acc_spec((1, d)), _acc_spec((8, 128))],
        out_shape=[jax.ShapeDtypeStruct((s, d), f32), jax.ShapeDtypeStruct((s, d), bf16), jax.ShapeDtypeStruct((s, d), bf16),
                   jax.ShapeDtypeStruct((s, d), bf16), jax.ShapeDtypeStruct((1, d), f32), jax.ShapeDtypeStruct((8, 128), f32)],
        compiler_params=_cp(("arbitrary",), VMEM_LIMIT),
    )(x3, g_ple, w_pg, p, w_pp, target)


def _bwd_ffn_hidden(d3, x2, gp, up, g_ffn, w_down, tm, tf):
    s, d = x2.shape

    def body(d3_ref, x_ref, gp_ref, up_ref, g_ref, wd_ref, h_ref, act_ref, dgp_ref, dup_ref, d3b_ref):
        @pl.when(pl.program_id(1) == 0)
        def _():
            h, _ = _rms_fwd(x_ref[...], g_ref[...], d)
            h_ref[...] = h.astype(bf16)
            d3b_ref[...] = d3_ref[...].astype(bf16)

        gp, up = gp_ref[...].astype(f32), up_ref[...].astype(f32)
        sg = _sigmoid(gp)
        silu = gp * sg
        act_ref[...] = (silu * up).astype(bf16)
        dact = _dot_nt(d3b_ref[...], wd_ref[...])
        dgp_ref[...] = (dact * up * (sg * (1.0 + gp * (1.0 - sg)))).astype(bf16)
        dup_ref[...] = (dact * silu).astype(bf16)

    rowd = pl.BlockSpec((tm, d), lambda i, f: (i, 0))
    rowf = pl.BlockSpec((tm, tf), lambda i, f: (i, f))
    return pl.pallas_call(
        body, name="bwd_ffn_hidden", grid=(s // tm, D_FF // tf),
        in_specs=[rowd, rowd, rowf, rowf, _const_spec((1, d)), pl.BlockSpec((tf, d), lambda i, f: (f, 0))],
        out_specs=[rowd, rowf, rowf, rowf],
        out_shape=[jax.ShapeDtypeStruct((s, d), bf16)] + [jax.ShapeDtypeStruct((s, D_FF), bf16)] * 3,
        scratch_shapes=[pltpu.VMEM((tm, d), bf16)],
        compiler_params=_cp(("parallel", "arbitrary"), VMEM_LIMIT),
    )(d3, x2, gp, up, g_ffn, w_down)


def _bwd_ffn_in(d3, x2, dgp, dup, g_ffn, w_gate, w_up, tm):
    s, d = x2.shape

    def body(d3_ref, x_ref, dgp_ref, dup_ref, g_ref, wg_ref, wu_ref, d2_ref, dg_ref):
        @pl.when(pl.program_id(0) == 0)
        def _():
            dg_ref[...] = jnp.zeros_like(dg_ref)

        x, g = x_ref[...], g_ref[...]
        dh = _dot(dgp_ref[...], wg_ref[...]) + _dot(dup_ref[...], wu_ref[...])
        r = lax.rsqrt(jnp.sum(x * x, axis=-1, keepdims=True) * (1.0 / d) + EPS)
        dx, dgx = _rms_bwd(dh, x, r, g, d)
        d2_ref[...] = d3_ref[...] + dx
        dg_ref[...] += jnp.sum(dgx, axis=0, keepdims=True)

    rowd = pl.BlockSpec((tm, d), lambda i: (i, 0))
    rowf = pl.BlockSpec((tm, D_FF), lambda i: (i, 0))
    return pl.pallas_call(
        body, name="bwd_ffn_in", grid=(s // tm,),
        in_specs=[rowd, rowd, rowf, rowf, _const_spec((1, d)), _const_spec((D_FF, d)), _const_spec((D_FF, d))],
        out_specs=[rowd, _acc_spec((1, d))],
        out_shape=[jax.ShapeDtypeStruct((s, d), f32), jax.ShapeDtypeStruct((1, d), f32)],
        compiler_params=_cp(("arbitrary",), VMEM_LIMIT),
    )(d3, x2, dgp, dup, g_ffn, w_gate, w_up)


def _bwd_mix(d2, w_o, o, z, g_hgo, tm):
    s, d = d2.shape

    def body(d2_ref, w_ref, o_ref, hg_ref, g_ref, da_ref, do_ref, dhg_ref, dg_ref):
        @pl.when(pl.program_id(0) == 0)
        def _():
            dg_ref[...] = jnp.zeros_like(dg_ref)

        dcat = _dot_nt(d2_ref[...].astype(bf16), w_ref[...])
        da_ref[...] = dcat[:, 0:512].astype(bf16)
        dr = dcat[:, 512:1024]
        o, hg, g = o_ref[...], hg_ref[...], g_ref[...]
        _, on, rs, sg = _hg_out(o, hg, g)
        dhg_ref[...] = (dr * on * (sg * (1.0 + hg * (1.0 - sg)))).astype(bf16)
        don = dr * (hg * sg)
        dgs = []
        for h in range(HG_HEADS):
            cols = slice(128 * h, 128 * (h + 1))
            dx, dgx = _rms_bwd(don[:, cols], o[:, cols], rs[h], g[:, cols], 128)
            do_ref[:, cols] = dx
            dgs.append(jnp.sum(dgx, axis=0, keepdims=True))
        dg_ref[...] += jnp.concatenate(dgs, axis=-1)

    row512 = pl.BlockSpec((tm, 512), lambda i: (i, 0))
    return pl.pallas_call(
        body, name="bwd_mix", grid=(s // tm,),
        in_specs=[pl.BlockSpec((tm, d), lambda i: (i, 0)), _const_spec((d, d)), row512,
                  pl.BlockSpec((tm, 512), lambda i: (i, Z_HG // 512)), _const_spec((1, 512))],
        out_specs=[row512, row512, row512, _acc_spec((1, 512))],
        out_shape=[jax.ShapeDtypeStruct((s, 512), bf16), jax.ShapeDtypeStruct((s, 512), f32), jax.ShapeDtypeStruct((s, 512), bf16),
                   jax.ShapeDtypeStruct((1, 512), f32)],
        compiler_params=_cp(("arbitrary",), VMEM_LIMIT),
    )(d2, w_o, o, z, g_hgo)


def _bwd_gla(z, lb4, do, b_fwd, states):
    s = z.shape[0]
    n_chunks = s // CHUNK
    n_groups = s // GLA_ROWS
    assert n_groups % 2 == 0

    def body(hq_ref, hff_ref, hfb_ref, hi_ref, lb_ref, do_ref, b_all, st_all, dhq_ref, dhff_ref, dhfb_ref, dhi_ref, dlb_ref,
             dst_ref, dq_acc, dv_acc, dlow_ref):
        dirs = (False, True)
        masks = [_gla_masks(rev) for rev in dirs]
        lowers = [_sigmoid(lb_ref[int(rev):int(rev) + 1, :] - lb_ref[2 + int(rev):3 + int(rev), :]) for rev in dirs]
        hf_refs, dhf_refs = (hff_ref, hfb_ref), (dhff_ref, dhfb_ref)

        dst_ref[...] = jnp.zeros_like(dst_ref)
        dlow_ref[...] = jnp.zeros_like(dlow_ref)

        def make_bwd_step(first):
            def bwd_step(j, carry):
                n = n_groups - 1 - j
                for d, rev in enumerate(dirs):
                    maskf, _, tri_t, row_masks = masks[d]
                    lower = lowers[d]
                    rows, chunk0 = _gla_rows(n, n_groups, rev)
                    hq, hf = hq_ref[rows, :], hf_refs[d][rows, :]
                    q, k, _, f, sg = _gla_gates(hq, hf, lower)
                    v = hi_ref[rows, :]
                    dout = do_ref[rows, :]
                    b = b_all[d, rows, :]
                    b_last3, b_mid3 = _gla_last_mid(b, rev)
                    b_last, b_mid = _gla_per_row(b_last3), _gla_per_row(b_mid3)
                    e1, e2, e3, e4 = jnp.exp(b - b_mid), jnp.exp(b_mid - b), jnp.exp(b_last - b), jnp.exp(b)
                    decay3 = jnp.exp(b_last3)
                    qi, ki, kt, qt = q * e1, k * e2, k * e3, q * e4
                    qib, kib, ktb = qi.astype(bf16), ki.astype(bf16), kt.astype(bf16)
                    vb, dob = v.astype(bf16), dout.astype(bf16)
                    a = (_dot_nt(qib, kib) * maskf).astype(bf16)
                    da = (_dot_nt(dob, vb) * maskf).astype(bf16)
                    dqi = _dot(da, kib)
                    dki = _dot_tn(da, qib)
                    into_state = _dot_tn(dob, _gla_block_diag(qt, row_masks))
                    dst = dst_ref[d]
                    sts, dsts, ddecay = [None] * GLA_GROUP, [None] * GLA_GROUP, [None] * GLA_GROUP
                    for c in reversed(_gla_scan_order(rev)):
                        sts[c] = st_all[d, chunk0 + c]
                        dsts[c] = dst.astype(bf16)
                        ddecay[c] = jnp.sum(dst * sts[c], axis=0, keepdims=True)[None]
                        dst = dst * decay3[c] + into_state[:, 128 * c:128 * (c + 1)]
                    dst_ref[d] = dst
                    dv = _dot_tn(a, dob) + _gla_diag(_dot_nt(ktb, jnp.concatenate(dsts, axis=0)))
                    dqt = _gla_diag(_dot(dob, jnp.concatenate([x.astype(bf16) for x in sts], axis=-1)))
                    dkt = _gla_diag(_dot(vb, jnp.concatenate(dsts, axis=-1)))
                    dq = dqi * e1 + dqt * e4
                    dk = dki * e2 + dkt * e3
                    db = dqi * qi - dki * ki + dqt * qt - dkt * kt
                    dlast3 = (jnp.sum((dkt * kt).reshape(GLA_GROUP, CHUNK, 128), axis=1, keepdims=True)
                              + jnp.concatenate(ddecay, axis=0) * decay3)
                    dlogf = _tri_sum(tri_t, db) + _gla_per_row(dlast3)
                    df = dlogf / f - dk
                    dhf_refs[d][rows, :] = (df * (1.0 - lower) * sg * (1.0 - sg)).astype(bf16)
                    dlow_ref[d:d + 1, :] += jnp.sum(df * (1.0 - sg), axis=0, keepdims=True)
                    sq = _sigmoid(hq)
                    dhq = dq * (sq * (1.0 + hq * (1.0 - sq)))
                    if first:
                        dq_acc[rows, :] = dhq
                        dv_acc[rows, :] = dv
                    else:
                        dhq_ref[rows, :] = (dq_acc[rows, :] + dhq).astype(bf16)
                        dhi_ref[rows, :] = (dv_acc[rows, :] + dv).astype(bf16)
                return carry
            return bwd_step

        lax.fori_loop(0, n_groups // 2, make_bwd_step(True), 0)
        lax.fori_loop(n_groups // 2, n_groups, make_bwd_step(False), 0)

        for d in range(2):
            dl = dlow_ref[d:d + 1, :] * lowers[d] * (1.0 - lowers[d])
            dlb_ref[d:d + 1, :] = dl
            dlb_ref[2 + d:3 + d, :] = -dl

    col = lambda base: pl.BlockSpec((s, 128), lambda h: (0, base // 128 + h))
    return pl.pallas_call(
        body, name="bwd_gla", grid=(HG_HEADS,),
        in_specs=[col(Z_HQ), col(Z_HFF), col(Z_HFB), col(Z_HI), pl.BlockSpec((4, 128), lambda h: (0, h)), col(0),
                  pl.BlockSpec((2, s, 128), lambda h: (0, 0, h)),
                  pl.BlockSpec((None, 2, n_chunks, 128, 128), lambda h: (h, 0, 0, 0, 0), pipeline_mode=pl.Buffered(1))],
        out_specs=[col(0), col(0), col(0), col(0), pl.BlockSpec((4, 128), lambda h: (0, h))],
        out_shape=[jax.ShapeDtypeStruct((s, 512), bf16)] * 4 + [jax.ShapeDtypeStruct((4, 512), f32)],
        scratch_shapes=[pltpu.VMEM((2, 128, 128), f32), pltpu.VMEM((s, 128), f32), pltpu.VMEM((s, 128), f32),
                        pltpu.VMEM((2, 128), f32)],
        compiler_params=_cp(("parallel",), VMEM_LIMIT),
    )(z, z, z, z, lb4, do, b_fwd, states)


def _bwd_attn(q, k, v, da, a32, tq):
    hh, s, _ = q.shape

    n_sub = max(1, tq // ATTN_SUB_ROWS)

    def body(q_ref, k_ref, v_ref, do_ref, o_ref, dq_ref, dk_ref, dv_ref, p_all, ds_all, dol_ref, dkt_ref, dvt_ref):
        @pl.when(pl.program_id(1) == 0)
        def _():
            dkt_ref[...] = jnp.zeros_like(dkt_ref)
            dvt_ref[...] = jnp.zeros_like(dvt_ref)

        kb, vb = k_ref[...], v_ref[...]
        for t in range(n_sub):
            rows = slice(t * (tq // n_sub), (t + 1) * (tq // n_sub))
            sc = _dot_nt(q_ref[rows, :], kb)
            p = jnp.exp2((sc - jnp.max(sc, axis=-1, keepdims=True)) * (ATTN_SCALE * LOG2_E))
            inv_l = 1.0 / jnp.sum(p, axis=-1, keepdims=True)
            pb = p.astype(bf16)
            dob = do_ref[rows, :]
            dof = dob.astype(f32)
            delta = jnp.sum(dof * o_ref[rows, :], axis=-1, keepdims=True)
            ds = pb * ((_dot_nt(dob, vb) - delta) * inv_l).astype(bf16)
            dq_ref[rows, :] = _dot(ds, kb) * ATTN_SCALE
            p_all[rows, :] = pb
            ds_all[rows, :] = ds
            dol_ref[rows, :] = (dof * inv_l).astype(bf16)
        dkt_ref[...] += _dot_tn(q_ref[...], ds_all[...])
        dvt_ref[...] += _dot_tn(dol_ref[...], p_all[...])

        @pl.when(pl.program_id(1) == s // tq - 1)
        def _():
            dk_ref[...] = dkt_ref[...].T * ATTN_SCALE
            dv_ref[...] = dvt_ref[...].T

    return pl.pallas_call(
        body, name="bwd_attn", grid=(hh, s // tq),
        in_specs=[pl.BlockSpec((None, tq, QK_PAD), lambda h, i: (h, i, 0)),
                  pl.BlockSpec((None, s, QK_PAD), lambda h, i: (h, 0, 0)),
                  pl.BlockSpec((None, s, V_HEAD), lambda h, i: (h, 0, 0)),
                  pl.BlockSpec((tq, V_HEAD), lambda h, i: (i, h)), pl.BlockSpec((tq, V_HEAD), lambda h, i: (i, h))],
        out_specs=[pl.BlockSpec((None, tq, QK_PAD), lambda h, i: (h, i, 0)),
                   pl.BlockSpec((None, s, QK_PAD), lambda h, i: (h, 0, 0)),
                   pl.BlockSpec((None, s, V_HEAD), lambda h, i: (h, 0, 0))],
        out_shape=[jax.ShapeDtypeStruct((hh, s, QK_PAD), f32), jax.ShapeDtypeStruct((hh, s, QK_PAD), f32),
                   jax.ShapeDtypeStruct((hh, s, V_HEAD), f32)],
        scratch_shapes=[pltpu.VMEM((tq, s), bf16), pltpu.VMEM((tq, s), bf16), pltpu.VMEM((tq, V_HEAD), bf16),
                        pltpu.VMEM((QK_PAD, s), f32), pltpu.VMEM((V_HEAD, s), f32)],
        compiler_params=_cp(("parallel", "arbitrary"), VMEM_LIMIT),
    )(q, k, v, da, a32)


def _bwd_mla_proj(z, dq, dk, dv, cosb, sina, sinb, g_qa, g_kva, wqb, wkvb, g_qn, g_kn, tm):
    s = z.shape[0]
    hh = MLA_HEADS

    def body(cq_ref, ckv_ref, kr_ref, dq_ref, dk_ref, dv_ref, c_ref, sa_ref, sb_ref, gqa_ref, gkva_ref, wqb_ref, wkvb_ref,
             gqn_ref, gkn_ref, dz_ref, cqn_ref, ckvn_ref, dq0_ref, dkv0_ref, dgqa_ref, dgkva_ref, dgqn_ref, dgkn_ref):
        @pl.when(pl.program_id(0) == 0)
        def _():
            for r in (dgqa_ref, dgkva_ref, dgqn_ref, dgkn_ref):
                r[...] = jnp.zeros_like(r)

        cq, ckv, kr = cq_ref[...], ckv_ref[...], kr_ref[...]
        gqa, gkva, gqn, gkn = gqa_ref[...], gkva_ref[...], gqn_ref[...], gkn_ref[...]
        cqn_b, rq, ckvn_b, rkv, q0, kv0 = _mla_qk_fwd(cq, ckv, gqa, gkva, wqb_ref[...], wkvb_ref[...])
        cqn_ref[...] = cqn_b
        ckvn_ref[...] = ckvn_b
        c, sa, sb = c_ref[...], -sa_ref[...], -sb_ref[...]
        kr_sq = jnp.sum(kr * kr, axis=-1, keepdims=True)
        dkr = jnp.zeros_like(kr)
        dgqn = jnp.zeros((1, QK_PAD), f32)
        dgkn = jnp.zeros((1, QK_PAD), f32)
        for h in range(hh):
            qh = q0[:, QK_PAD * h:QK_PAD * (h + 1)]
            rh = lax.rsqrt(jnp.sum(qh * qh, axis=-1, keepdims=True) * (1.0 / QK_HEAD) + EPS)
            dqh = dq_ref[h]
            dqn = jnp.concatenate([dqh[:, 0:128], _rope(dqh[:, 128:256], c, sa, sb)], axis=-1)
            dq0h, dgx = _rms_bwd(dqn, qh, rh, gqn, QK_HEAD)
            dq0_ref[:, QK_PAD * h:QK_PAD * (h + 1)] = dq0h.astype(bf16)
            dgqn = dgqn + jnp.sum(dgx, axis=0, keepdims=True)

            kn_ = kv0[:, 256 * h:256 * h + 128]
            k0 = jnp.concatenate([kn_, kr], axis=-1)
            rk = lax.rsqrt((jnp.sum(kn_ * kn_, axis=-1, keepdims=True) + kr_sq) * (1.0 / QK_HEAD) + EPS)
            dkh = dk_ref[h]
            dkn = jnp.concatenate([dkh[:, 0:128], _rope(dkh[:, 128:256], c, sa, sb)], axis=-1)
            dk0, dgx = _rms_bwd(dkn, k0, rk, gkn, QK_HEAD)
            dgkn = dgkn + jnp.sum(dgx, axis=0, keepdims=True)
            dkv0_ref[:, 256 * h:256 * h + 128] = dk0[:, 0:128].astype(bf16)
            dkv0_ref[:, 256 * h + 128:256 * h + 256] = dv_ref[h].astype(bf16)
            dkr = dkr + dk0[:, 128:256]
        dgqn_ref[...] += dgqn
        dgkn_ref[...] += dgkn
        dcq, dgx = _rms_bwd(_dot_nt(dq0_ref[...], wqb_ref[...]), cq, rq, gqa, Q_LORA)
        dgqa_ref[...] += jnp.sum(dgx, axis=0, keepdims=True)
        dckv, dgx = _rms_bwd(_dot_nt(dkv0_ref[...], wkvb_ref[...]), ckv, rkv, gkva, KV_LORA)
        dgkva_ref[...] += jnp.sum(dgx, axis=0, keepdims=True)
        dz_ref[:, 0:256] = dcq.astype(bf16)
        dz_ref[:, 256:512] = dckv.astype(bf16)
        dz_ref[:, 512:640] = dkr.astype(bf16)

    row128 = pl.BlockSpec((tm, 128), lambda i: (i, 0))
    row256 = pl.BlockSpec((tm, 256), lambda i: (i, 0))
    row1024 = pl.BlockSpec((tm, 1024), lambda i: (i, 0))
    hd = lambda w: pl.BlockSpec((hh, tm, w), lambda i: (0, i, 0))
    return pl.pallas_call(
        body, name="bwd_mla_proj", grid=(s // tm,),
        in_specs=[pl.BlockSpec((tm, 256), lambda i: (i, Z_CQ // 256)), pl.BlockSpec((tm, 256), lambda i: (i, Z_CKV // 256)),
                  pl.BlockSpec((tm, 128), lambda i: (i, Z_KR // 128)), hd(QK_PAD), hd(QK_PAD), hd(V_HEAD),
                  row128, row128, row128,
                  _const_spec((1, 256)), _const_spec((1, 256)), _const_spec((256, 1024)), _const_spec((256, 1024)),
                  _const_spec((1, 256)), _const_spec((1, 256))],
        out_specs=[pl.BlockSpec((tm, 640), lambda i: (i, 0)), row256, row256, row1024, row1024,
                   _acc_spec((1, 256)), _acc_spec((1, 256)), _acc_spec((1, 256)), _acc_spec((1, 256))],
        out_shape=[jax.ShapeDtypeStruct((s, 640), bf16), jax.ShapeDtypeStruct((s, 256), bf16), jax.ShapeDtypeStruct((s, 256), bf16),
                   jax.ShapeDtypeStruct((s, 1024), bf16), jax.ShapeDtypeStruct((s, 1024), bf16)]
        + [jax.ShapeDtypeStruct((1, 256), f32)] * 4,
        compiler_params=_cp(("arbitrary",), VMEM_LIMIT),
    )(z, z, z, dq, dk, dv, cosb, sina, sinb, g_qa, g_kva, wqb, wkvb, g_qn, g_kn)


def _bwd_in(segments, wz, x, g_mix, d2, tm):
    s, d = x.shape
    n_seg = len(segments)

    def body(*refs):
        dz_refs, w_refs = refs[:n_seg], refs[n_seg:2 * n_seg]
        x_ref, g_ref, d2_ref, gx_ref, dg_ref = refs[2 * n_seg:]

        @pl.when(pl.program_id(0) == 0)
        def _():
            dg_ref[...] = jnp.zeros_like(dg_ref)

        dh = _dot_nt(dz_refs[0][...], w_refs[0][...])
        for a_ref, w_ref in zip(dz_refs[1:], w_refs[1:]):
            dh = dh + _dot_nt(a_ref[...], w_ref[...])
        x, g = x_ref[...], g_ref[...]
        r = lax.rsqrt(jnp.sum(x * x, axis=-1, keepdims=True) * (1.0 / d) + EPS)
        dx, dgx = _rms_bwd(dh, x, r, g, d)
        gx_ref[...] = d2_ref[...] + dx
        dg_ref[...] += jnp.sum(dgx, axis=0, keepdims=True)

    rowd = pl.BlockSpec((tm, d), lambda i: (i, 0))
    dz_specs = [pl.BlockSpec((tm, w), functools.partial(lambda i, j: (i, j), j=ja)) for _, w, ja, _ in segments]
    w_specs = [pl.BlockSpec((d, w), functools.partial(lambda i, j: (0, j), j=jw), pipeline_mode=pl.Buffered(1))
               for _, w, _, jw in segments]
    return pl.pallas_call(
        body, name="bwd_in", grid=(s // tm,),
        in_specs=dz_specs + w_specs + [rowd, _const_spec((1, d)), rowd],
        out_specs=[rowd, _acc_spec((1, d))],
        out_shape=[jax.ShapeDtypeStruct((s, d), f32), jax.ShapeDtypeStruct((1, d), f32)],
        compiler_params=_cp(("arbitrary",), VMEM_LIMIT),
    )(*[a for a, _, _, _ in segments], *([wz] * n_seg), x, g_mix, d2)


def _pick_tile(n, cap):
    best = None
    for t in range(LANES, cap + 1, LANES):
        if n % t == 0:
            best = t
    return best if best is not None else n


def _mm_tn_many(a, bs, name, tm, transposed=False):
    kk, m = a.shape
    n_b = len(bs)
    tk = min(1024, kk)
    n_k = kk // tk

    def body(a_ref, *refs):
        b_refs, o_refs, acc_refs = refs[:n_b], refs[n_b:2 * n_b], refs[2 * n_b:]

        @pl.when(pl.program_id(1) == 0)
        def _():
            for acc in acc_refs:
                acc[...] = jnp.zeros_like(acc)
        a_blk = a_ref[...].astype(bf16)
        for b_ref, acc in zip(b_refs, acc_refs):
            acc[...] += _dot_tn(a_blk, b_ref[...].astype(bf16))

        @pl.when(pl.program_id(1) == n_k - 1)
        def _():
            for o_ref, acc in zip(o_refs, acc_refs):
                o_ref[...] = (acc[...].T if transposed else acc[...]).astype(bf16)

    if transposed:
        out_specs = [pl.BlockSpec((b.shape[1], tm), lambda i, k: (0, i)) for b in bs]
        out_shape = [jax.ShapeDtypeStruct((b.shape[1], m), bf16) for b in bs]
    else:
        out_specs = [pl.BlockSpec((tm, b.shape[1]), lambda i, k: (i, 0)) for b in bs]
        out_shape = [jax.ShapeDtypeStruct((m, b.shape[1]), bf16) for b in bs]
    return pl.pallas_call(
        body, name=name, grid=(m // tm, n_k),
        in_specs=[pl.BlockSpec((tk, tm), lambda i, k: (k, i))] + [pl.BlockSpec((tk, b.shape[1]), lambda i, k: (k, 0)) for b in bs],
        out_specs=out_specs,
        out_shape=out_shape,
        scratch_shapes=[pltpu.VMEM((tm, b.shape[1]), f32) for b in bs],
        compiler_params=_cp(("parallel", "arbitrary"), VMEM_LIMIT),
    )(a, *bs)


def _mm_tn(a, b, name):
    kk, m = a.shape
    _, n = b.shape
    tm = _pick_tile(m, 1408)
    tn = _pick_tile(n, 1408)
    tk = min(1024, kk)

    n_k = kk // tk

    def body(a_ref, b_ref, o_ref, acc_ref):
        @pl.when(pl.program_id(2) == 0)
        def _():
            acc_ref[...] = jnp.zeros_like(acc_ref)
        acc_ref[...] += _dot_tn(a_ref[...].astype(bf16), b_ref[...].astype(bf16))

        @pl.when(pl.program_id(2) == n_k - 1)
        def _():
            o_ref[...] = acc_ref[...].astype(bf16)

    return pl.pallas_call(
        body, name=name, grid=(m // tm, n // tn, n_k),
        in_specs=[pl.BlockSpec((tk, tm), lambda i, j, k: (k, i)), pl.BlockSpec((tk, tn), lambda i, j, k: (k, j))],
        out_specs=pl.BlockSpec((tm, tn), lambda i, j, k: (i, j)),
        out_shape=jax.ShapeDtypeStruct((m, n), bf16),
        scratch_shapes=[pltpu.VMEM((tm, tn), f32)],
        compiler_params=_cp(("parallel", "parallel", "arbitrary"), VMEM_LIMIT),
    )(a, b)


def _rope_tables(positions):
    inv_freq = ROPE_THETA ** (-jnp.arange(0, QK_ROPE, 2, dtype=f32) / QK_ROPE)
    ang = positions.astype(f32)[:, None] * inv_freq
    cos, sin = jnp.cos(ang), jnp.sin(ang)
    zero = jnp.zeros_like(cos)
    return (jnp.concatenate([cos, cos, zero, zero], axis=1), jnp.concatenate([zero, sin, zero, zero], axis=1),
            jnp.concatenate([-sin, zero, zero, zero], axis=1))


def _pad256(g):
    return jnp.pad(g.reshape(1, QK_HEAD), ((0, 0), (0, QK_PAD - QK_HEAD)))


RELAYOUT_BLOCKS = 8
FIRST = ("w_in", "w_qb", "w_kvb", "lb_param")
SECOND = ("w_o", "w_gate", "w_up", "w_down", "w_ple_gate", "w_ple_proj")
ROW_SHARDED = ("w_o", "w_down", "w_ple_gate")


def _col_moves(j):
    lo = BIG["w_in"][1] * j
    w_in = [(max(lo, a) - lo, min(lo + BIG["w_in"][1], b) - lo, d + max(lo, a) - a)
            for a, b, d in Z_SEGMENTS if max(lo, a) < min(lo + BIG["w_in"][1], b)]
    head, half = divmod(j, 2)
    whole = lambda n: [(0, BIG[n][1], BIG[n][1] * j)]
    return {"w_in": w_in, "w_qb": [(0, 96, QK_PAD * head + 96 * half)], "w_kvb": whole("w_kvb"),
            "w_ple_proj": whole("w_ple_proj"), "lb_param": whole("lb_param")}


def _kernel_width(name):
    return {"w_in": Z_W, "w_qb": MLA_HEADS * QK_PAD}.get(name, N_DEV * BIG[name][1])


def _relayout_specs(names, by_dev):
    specs = []
    for n in names:
        rows, cols = BIG[n]
        if n == "lb_param":
            specs.append(_acc_spec((N_DEV, rows, cols) if by_dev else (rows, _kernel_width(n))))
        elif by_dev:
            specs.append(pl.BlockSpec((N_DEV, rows // RELAYOUT_BLOCKS, cols), lambda i: (0, i, 0)))
        else:
            specs.append(pl.BlockSpec((rows // RELAYOUT_BLOCKS, _kernel_width(n)), lambda i: (i, 0)))
    return specs


def _weights_in(gathered, names, name):
    n = len(names)

    def body(*refs):
        ins, outs = dict(zip(names, refs[:n])), dict(zip(names, refs[n:]))
        if "w_in" in outs:
            outs["w_in"][:, Z_KR + QK_ROPE:Z_W] = jnp.zeros((outs["w_in"].shape[0], Z_W - Z_KR - QK_ROPE), bf16)
        if "w_qb" in outs:
            for h in range(MLA_HEADS):
                outs["w_qb"][:, QK_PAD * h + QK_HEAD:QK_PAD * (h + 1)] = jnp.zeros((outs["w_qb"].shape[0], QK_PAD - QK_HEAD), bf16)
        for j in range(N_DEV):
            for wn, moves in _col_moves(j).items():
                if wn in outs:
                    for s0, s1, d0 in moves:
                        outs[wn][:, d0:d0 + s1 - s0] = ins[wn][j, :, s0:s1]

    outs = pl.pallas_call(
        body, name=name, grid=(RELAYOUT_BLOCKS,), in_specs=_relayout_specs(names, True), out_specs=_relayout_specs(names, False),
        out_shape=[jax.ShapeDtypeStruct((BIG[wn][0], _kernel_width(wn)), gathered[wn].dtype) for wn in names],
        compiler_params=_cp(("arbitrary",), VMEM_LIMIT),
    )(*[gathered[wn] for wn in names])
    return dict(zip(names, outs))


def _grads_out(sources, names, name):
    pieces = [(wn, start, arr) for wn in names for start, arr in sources[wn]]
    n_in = len(pieces)

    def body(*refs):
        outs = dict(zip(names, refs[n_in:]))

        def cols(wn, c0, c1):
            for (pn, start, arr), ref in zip(pieces, refs[:n_in]):
                if pn == wn and start <= c0 and c1 <= start + arr.shape[1]:
                    return ref[:, c0 - start:c1 - start]

        for j in range(N_DEV):
            for wn, moves in _col_moves(j).items():
                if wn in outs:
                    for s0, s1, d0 in moves:
                        outs[wn][j, :, s0:s1] = cols(wn, d0, d0 + s1 - s0).astype(bf16)

    in_specs = [_acc_spec(arr.shape) if wn == "lb_param" else pl.BlockSpec((arr.shape[0] // RELAYOUT_BLOCKS, arr.shape[1]), lambda i: (i, 0))
                for wn, _, arr in pieces]
    outs = pl.pallas_call(
        body, name=name, grid=(RELAYOUT_BLOCKS,), in_specs=in_specs, out_specs=_relayout_specs(names, True),
        out_shape=[jax.ShapeDtypeStruct((N_DEV, *BIG[wn]), bf16) for wn in names],
        compiler_params=_cp(("arbitrary",), VMEM_LIMIT),
    )(*[arr for _, _, arr in pieces])
    return dict(zip(names, outs))


def kernel(x, p, positions, g_mix, w_in, g_qa, g_kva, w_qb, w_kvb, g_qn, g_kn, lb_param, g_hgo, w_o, g_ffn, w_gate, w_up, w_down, g_ple, w_ple_gate, w_ple_proj, loss_target, m_g_mix, m_w_in, m_g_qa, m_g_kva, m_w_qb, m_w_kvb, m_g_qn, m_g_kn, m_lb_param, m_g_hgo, m_w_o, m_g_ffn, m_w_gate, m_w_up, m_w_down, m_g_ple, m_w_ple_gate, m_w_ple_proj, v_g_mix, v_w_in, v_g_qa, v_g_kva, v_w_qb, v_w_kvb, v_g_qn, v_g_kn, v_lb_param, v_g_hgo, v_w_o, v_g_ffn, v_w_gate, v_w_up, v_w_down, v_g_ple, v_w_ple_gate, v_w_ple_proj):
    w_all = dict(g_mix=g_mix, g_qa=g_qa, g_kva=g_kva, g_qn=g_qn, g_kn=g_kn, g_hgo=g_hgo, g_ffn=g_ffn, g_ple=g_ple,
                 w_in=w_in, w_qb=w_qb, w_kvb=w_kvb, w_o=w_o, w_gate=w_gate, w_up=w_up, w_down=w_down,
                 w_ple_gate=w_ple_gate, w_ple_proj=w_ple_proj, lb_param=lb_param)
    m_all = dict(g_mix=m_g_mix, g_qa=m_g_qa, g_kva=m_g_kva, g_qn=m_g_qn, g_kn=m_g_kn, g_hgo=m_g_hgo, g_ffn=m_g_ffn,
                 g_ple=m_g_ple, w_in=m_w_in, w_qb=m_w_qb, w_kvb=m_w_kvb, w_o=m_w_o, w_gate=m_w_gate, w_up=m_w_up,
                 w_down=m_w_down, w_ple_gate=m_w_ple_gate, w_ple_proj=m_w_ple_proj, lb_param=m_lb_param)
    v_all = dict(g_mix=v_g_mix, g_qa=v_g_qa, g_kva=v_g_kva, g_qn=v_g_qn, g_kn=v_g_kn, g_hgo=v_g_hgo, g_ffn=v_g_ffn,
                 g_ple=v_g_ple, w_in=v_w_in, w_qb=v_w_qb, w_kvb=v_w_kvb, w_o=v_w_o, w_gate=v_w_gate, w_up=v_w_up,
                 w_down=v_w_down, w_ple_gate=v_w_ple_gate, w_ple_proj=v_w_ple_proj, lb_param=v_lb_param)
    me_idx = jnp.stack([_me()]).astype(jnp.int32)
    x, p, positions, target = x[0], p[0, 0], positions[0], loss_target[0]
    s = x.shape[0]
    tm, tm_ffn, tq_f, tq_b = min(512, s), min(1024, s), min(2048, s), min(1024, s)
    g_mix, g_qa, g_kva, g_qn, g_kn, g_hgo, g_ffn, g_ple = (w_all[n].reshape(1, -1) for n in SMALL)
    g_qn_p, g_kn_p = _pad256(g_qn), _pad256(g_kn)
    cosb, sina, sinb = _rope_tables(positions)
    as_shard = lambda n, a: a[0].T if n in TRANSPOSED else a.reshape(BIG[n])
    shard = lambda n: as_shard(n, w_all[n])

    first = _all_gather([shard(n) for n in FIRST], [f32 if n == "lb_param" else bf16 for n in FIRST], "ag_first")
    lands = _cast_to_slot([shard(n) for n in SECOND], me_idx, first[0])
    ag2, token = _exchange_start([], lands, "ag_second_start")
    wk = _weights_in(dict(zip(FIRST, first)), FIRST, "weights_in_first")
    wz, wqb, wkvb, lb4 = (wk[n] for n in FIRST)

    h1, z = _fwd_in(x, g_mix, wz, tm)
    q, k, v = _fwd_mla_proj(z, cosb + token[0, 0], sina, sinb, g_qa, g_kva, wqb, wkvb, g_qn_p, g_kn_p, tm)
    a, a32 = _fwd_attn(q, k, v, tq_f)
    o, gla_b, gla_states = _fwd_gla(z, lb4)

    second = dict(zip(SECOND, _exchange_wait(ag2, [a, o], "ag_second_wait")[1]))
    w_pp = _weights_in(second, ("w_ple_proj",), "weights_in_second")["w_ple_proj"]
    w_o, w_down, w_pg, w_gate, w_up = (second[n].reshape(N_DEV * BIG[n][0], BIG[n][1]) for n in ROW_SHARDED + TRANSPOSED)

    x2, cat = _fwd_mix(a, o, z, g_hgo, x, w_o, tm)
    x3, gp, up = _fwd_ffn(x2, g_ffn, w_gate, w_up, w_down, tm)
    d3, h3, dpre, dpp, dg_ple, loss_tile = _ple_loss_fwd_bwd(x3, g_ple, w_pg, p, w_pp, target, tm)
    h2, act, dgp, dup = _bwd_ffn_hidden(d3, x2, gp, up, g_ffn, w_down, tm, D_FF // 2)
    d2, dg_ffn = _bwd_ffn_in(d3, x2, dgp, dup, g_ffn, w_gate, w_up, tm)

    gw_gate, gw_up = _mm_tn_many(h2, [dgp, dup], "dw_gate_up", 512, transposed=True)
    blocks = _grads_out({"w_ple_proj": [(0, _mm_tn(p, dpp, "dw_ple_proj"))]}, ("w_ple_proj",), "grads_out_second")
    row_grads = {"w_o": _mm_tn(cat, d2, "dw_o"), "w_down": _mm_tn(act, d3, "dw_down"), "w_ple_gate": _mm_tn(h3, dpre, "dw_ple_gate"),
                 "w_gate": gw_gate, "w_up": gw_up}
    blocks.update({n: g.reshape(N_DEV, *BIG[n]) for n, g in row_grads.items()})
    empty = lambda names: [lax.empty((N_PEERS, *BIG[n]), bf16) for n in names]
    rs2, token = _exchange_start([blocks[n] for n in SECOND], empty(SECOND), "rs_second_start")

    da, do, dz_hg, dg_hgo = _bwd_mix(d2, w_o, o, z, g_hgo + token[0, 0], tm)
    dz_hq, dz_hff, dz_hfb, dz_hi, dlb4 = _bwd_gla(z, lb4, do, gla_b, gla_states)
    dq, dk, dv = _bwd_attn(q, k, v, da, a32, tq_b)
    dz_mla, cqn, ckvn, dq0, dkv0, dg_qa, dg_kva, dg_qn, dg_kn = _bwd_mla_proj(
        z, dq, dk, dv, cosb, sina, sinb, g_qa, g_kva, wqb, wkvb, g_qn_p, g_kn_p, tm)

    gz = list(zip((Z_HQ, Z_HFF, Z_HFB, Z_HI, Z_HG, Z_CQ),
                  _mm_tn_many(h1, [dz_hq, dz_hff, dz_hfb, dz_hi, dz_hg, dz_mla], "dw_in", 1024)))
    blocks1 = _grads_out({"w_in": gz, "w_qb": [(0, _mm_tn(cqn, dq0, "dw_qb"))], "w_kvb": [(0, _mm_tn(ckvn, dkv0, "dw_kvb"))],
                          "lb_param": [(0, dlb4)]}, FIRST, "grads_out_first")
    rs1, token = _exchange_start([blocks1[n] for n in FIRST], empty(FIRST), "rs_first_start")

    result = {}

    def adam(names, lands, src, n_blocks, after=()):
        flipped = TRANSPOSED + ("w_in",)
        given = lambda arrs: [arrs[n][0].T if n in flipped else arrs[n] for n in names]
        outs = _adam_shards(me_idx, [src[n] for n in names], lands, given(w_all), given(m_all), given(v_all), n_blocks,
                            "adamw_" + names[0], after)
        for n, o in zip(names, outs):
            result[n] = [t.T[None] for t in o] if n in flipped else o
        return outs[0][0]

    blocks2, lands2 = (dict(zip(SECOND, arrs)) for arrs in _exchange_wait(rs2, [token], "rs_second_wait"))
    by2 = ("w_down",) + TRANSPOSED
    by8 = tuple(n for n in SECOND if n not in by2)
    done = [adam(by8, [lands2[n] for n in by8], blocks2, 8), adam(by2, [lands2[n] for n in by2], blocks2, 2)]

    segments = [(dz_hq, 512, 0, Z_HQ // 512), (dz_hff, 512, 0, Z_HFF // 512), (dz_hfb, 512, 0, Z_HFB // 512),
                (dz_hi, 512, 0, Z_HI // 512), (dz_hg, 512, 0, Z_HG // 512), (dz_mla, 640, 0, Z_CQ // 640)]
    grad_x, dg_mix = _bwd_in(segments, wz, x, g_mix + token[0, 0], d2, tm)
    dgains = (dg_mix, dg_qa, dg_kva, dg_qn, dg_kn, dg_hgo, dg_ffn, dg_ple)

    vec = jnp.concatenate(list(dgains) + [loss_tile[0:1]], axis=1)
    parts = _all_gather([vec], [f32], "ag_gains")[0]
    outs, loss_row = _adam_gains(parts, [w_all[n] for n in SMALL], [m_all[n] for n in SMALL], [v_all[n] for n in SMALL])
    result.update(zip(SMALL, outs))

    blocks1, lands1 = _exchange_wait(rs1, [grad_x, loss_row, *done], "rs_first_wait")
    adam(FIRST, lands1, dict(zip(FIRST, blocks1)), 8)

    order = ("g_mix", "w_in", "g_qa", "g_kva", "w_qb", "w_kvb", "g_qn", "g_kn", "lb_param", "g_hgo", "w_o", "g_ffn",
             "w_gate", "w_up", "w_down", "g_ple", "w_ple_gate", "w_ple_proj")
    return (loss_row[0, 0], grad_x[None], *[result[n][k] for k in range(4) for n in order])
```

```python
import functools
import math

import jax
import jax.numpy as jnp
from jax import lax
from jax.experimental import pallas as pl
from jax.experimental.pallas import tpu as pltpu

f32 = jnp.float32
bf16 = jnp.bfloat16

N_DEV = 8
MLA_HEADS = 4
QK_NOPE = 128
QK_ROPE = 64
QK_HEAD = QK_NOPE + QK_ROPE
QK_PAD = 256
V_HEAD = 128
Q_LORA = 256
KV_LORA = 256
HG_HEADS = 4
CHUNK = 64
D_FF = 2816
PLE_DIM = 256
ROPE_THETA = 10000.0
EPS = 1e-6
ATTN_SCALE = QK_HEAD ** -0.5
LOG2_E = math.log2(math.e)
ATTN_SUB_ROWS = 256
Z_HQ, Z_HFF, Z_HFB, Z_HI, Z_HG, Z_CQ, Z_CKV, Z_KR, Z_W = 0, 512, 1024, 1536, 2048, 2560, 2816, 3072, 3200

ADAM_LR, ADAM_B1, ADAM_B2, ADAM_EPS, ADAM_WD, ADAM_STEP = 0.001, 0.9, 0.999, 1e-08, 0.01, 10

LANES = 128
BIG = {"w_in": (1024, 392), "w_qb": (256, 96), "w_kvb": (256, 128), "w_o": (128, 1024), "w_gate": (352, 1024),
       "w_up": (352, 1024), "w_down": (352, 1024), "w_ple_gate": (128, 1024), "w_ple_proj": (256, 128),
       "lb_param": (4, 64)}
TRANSPOSED = ("w_gate", "w_up")
SMALL = {"g_mix": (0, 1024), "g_qa": (1024, 256), "g_kva": (1280, 256), "g_qn": (1536, 192), "g_kn": (1792, 192),
         "g_hgo": (2048, 512), "g_ffn": (2560, 1024), "g_ple": (3584, 1024)}
LOSS_OFF = 4608
GAIN_VEC = LOSS_OFF + LANES
Z_SEGMENTS = ((0, 256, Z_CQ), (256, 512, Z_CKV), (512, 576, Z_KR), (576, 1088, Z_HQ), (1088, 1600, Z_HFF),
              (1600, 2112, Z_HFB), (2112, 2624, Z_HI), (2624, 3136, Z_HG))

VMEM_LIMIT = 56 * 1024 * 1024
MESH = pl.DeviceIdType.MESH


def _cp(sem=None, vmem=None):
    return pltpu.CompilerParams(dimension_semantics=sem, vmem_limit_bytes=vmem)


def _const_spec(shape):
    nd = len(shape)
    return pl.BlockSpec(shape, lambda *_: (0,) * nd, pipeline_mode=pl.Buffered(1))


def _acc_spec(shape):
    nd = len(shape)
    return pl.BlockSpec(shape, lambda *_: (0,) * nd)


def _sigmoid(x):
    return jax.nn.sigmoid(x)


def _dot(a, b):
    return jnp.dot(a, b, preferred_element_type=f32)


def _dot_nt(a, b):
    return lax.dot_general(a, b, (((1,), (1,)), ((), ())), preferred_element_type=f32)


def _dot_tn(a, b):
    return lax.dot_general(a, b, (((0,), (0,)), ((), ())), preferred_element_type=f32)


def _rms_fwd(x, g, width):
    r = lax.rsqrt(jnp.sum(x * x, axis=-1, keepdims=True) * (1.0 / width) + EPS)
    return x * r * g, r


def _rms_bwd(dy, x, r, g, width):
    u = dy * g
    dx = r * u - x * (r * r * r) * (jnp.sum(u * x, axis=-1, keepdims=True) * (1.0 / width))
    return dx, dy * x * r


class _Both:
    def __init__(self, *copies):
        self.copies = copies

    def start(self):
        for cp in self.copies:
            cp.start()

    def wait(self):
        for cp in self.copies:
            cp.wait()


def _rope(b, c, sa, sb):
    return b * c + pltpu.roll(b, 32, 1) * sa + pltpu.roll(b, 96, 1) * sb


def _all_gather(shards, dtypes, name):
    n = len(shards)

    def body(*refs):
        in_refs, out_refs, stage = refs[:n], refs[n:2 * n], refs[2 * n:3 * n]
        send_sems, recv_sems, local_sems = refs[3 * n:]
        for w in range(n):
            stage[w][...] = in_refs[w][...].astype(stage[w].dtype)
        x, y, c = lax.axis_index("x"), lax.axis_index("y"), lax.axis_index("c")
        me, sibling = (x, y, c), (x, y, 1 - c)
        chips = [(1 - x, y), (x, 1 - y), (1 - x, 1 - y)]

        def slot(w, px, py, pc):
            return out_refs[w].at[4 * px + 2 * py + pc]

        def copy(w, k, block, to, src=None):
            return pltpu.make_async_remote_copy(
                src_ref=slot(w, *block) if src is None else src, dst_ref=slot(w, *block),
                send_sem=send_sems.at[w, k], recv_sem=recv_sems.at[w, k], device_id=to, device_id_type=MESH)

        first = []
        for j, chip in enumerate(chips):
            first += [copy(w, 1 + j, me, (*chip, c), src=stage[w]) for w in range(n)]
        first += [copy(w, 0, me, sibling, src=stage[w]) for w in range(n)]
        mine = [pltpu.make_async_copy(stage[w], slot(w, *me), local_sems.at[w]) for w in range(n)]
        for cp in first + mine:
            cp.start()
        passed = []
        for j, chip in enumerate(chips):
            for w in range(n):
                copy(w, 1 + j, (*chip, c), me).wait_recv()
                passed.append(copy(w, 4 + j, (*chip, c), sibling))
                passed[-1].start()
        for w in range(n):
            copy(w, 0, sibling, me).wait_recv()
        for j, chip in enumerate(chips):
            for w in range(n):
                copy(w, 4 + j, (*chip, 1 - c), me).wait_recv()
        for cp in first + passed:
            cp.wait_send()
        for cp in mine:
            cp.wait()

    return pl.pallas_call(
        body, name=name,
        out_shape=[jax.ShapeDtypeStruct((N_DEV, *s.shape), dt) for s, dt in zip(shards, dtypes)],
        in_specs=[pl.BlockSpec(memory_space=pltpu.VMEM)] * n,
        out_specs=[pl.BlockSpec(memory_space=pl.ANY)] * n,
        scratch_shapes=[pltpu.VMEM(s.shape, dt) for s, dt in zip(shards, dtypes)]
        + [pltpu.SemaphoreType.DMA((n, 7)), pltpu.SemaphoreType.DMA((n, 7)), pltpu.SemaphoreType.DMA((n,))],
        compiler_params=_cp(None, VMEM_LIMIT),
    )(*shards)


N_PEERS = N_DEV - 1
HBM_SPEC = pl.BlockSpec(memory_space=pltpu.HBM)
SEM_SPEC = pl.BlockSpec(memory_space=pltpu.SEMAPHORE)
DATAFLOW = pltpu.SideEffectType.DATAFLOW_SIDE_EFFECTING


def _me():
    return 4 * lax.axis_index("x") + 2 * lax.axis_index("y") + lax.axis_index("c")


def _peer(k):
    x, y, c = lax.axis_index("x"), lax.axis_index("y"), lax.axis_index("c")
    px = 1 - x if k & 4 else x
    py = 1 - y if k & 2 else y
    pc = 1 - c if k & 1 else c
    return (px, py, pc), 4 * px + 2 * py + pc


def _exchange_copies(src_refs, land_refs, send_sems, recv_sems, gather):
    cps = []
    me = _me()
    for k in range(1, N_DEV):
        peer, peer_idx = _peer(k)
        for w, land in enumerate(land_refs):
            src = land.at[me] if gather else src_refs[w].at[peer_idx]
            dst = land.at[me] if gather else land.at[k - 1]
            cps.append(pltpu.make_async_remote_copy(
                src_ref=src, dst_ref=dst, send_sem=send_sems.at[N_PEERS * w + k - 1], recv_sem=recv_sems.at[N_PEERS * w + k - 1],
                device_id=peer, device_id_type=MESH))
    return cps


def _exchange_start(srcs, lands, name):
    n_src, n = len(srcs), len(lands)

    def body(*refs):
        src_refs, land_refs = refs[:n_src], refs[n_src:n_src + n]
        send_sems, recv_sems = refs[n_src + n], refs[n_src + n + 1]
        token = refs[-1]
        for cp in _exchange_copies(src_refs, land_refs, send_sems, recv_sems, gather=not n_src):
            cp.start()
        token[...] = jnp.zeros_like(token)

    arrays = [pltpu.with_memory_space_constraint(a, pltpu.HBM) for a in (*srcs, *lands)]
    outs = pl.pallas_call(
        body, name=name,
        out_shape=(pltpu.SemaphoreType.DMA((n * N_PEERS,)), pltpu.SemaphoreType.DMA((n * N_PEERS,)),
                   *[pltpu.HBM(a.shape, a.dtype) for a in arrays], jax.ShapeDtypeStruct((8, LANES), f32)),
        in_specs=[HBM_SPEC] * len(arrays),
        out_specs=(SEM_SPEC, SEM_SPEC, *[HBM_SPEC] * len(arrays), pl.BlockSpec(memory_space=pltpu.VMEM)),
        input_output_aliases={i: 2 + i for i in range(len(arrays))},
        compiler_params=pltpu.CompilerParams(has_side_effects=DATAFLOW),
    )(*arrays)
    return (outs[0], outs[1], outs[2:2 + n_src], outs[2 + n_src:2 + n_src + n]), outs[-1]


def _exchange_wait(state, after, name):
    send_sems, recv_sems, srcs, lands = state
    n_src, n = len(srcs), len(lands)

    def body(*refs):
        src_refs, land_refs = refs[:n_src], refs[n_src:n_src + n]
        send_ref, recv_ref = refs[n_src + n], refs[n_src + n + 1]
        for cp in _exchange_copies(src_refs, land_refs, send_ref, recv_ref, gather=not n_src):
            cp.wait_send()
            cp.wait_recv()

    arrays = (*srcs, *lands)
    outs = pl.pallas_call(
        body, name=name,
        out_shape=tuple(pltpu.HBM(a.shape, a.dtype) for a in arrays),
        in_specs=[HBM_SPEC] * len(arrays) + [SEM_SPEC, SEM_SPEC] + [pl.BlockSpec(memory_space=pl.ANY)] * len(after),
        out_specs=tuple([HBM_SPEC] * len(arrays)),
        input_output_aliases={i: i for i in range(len(arrays))},
        compiler_params=pltpu.CompilerParams(has_side_effects=DATAFLOW),
    )(*arrays, send_sems, recv_sems, *after)
    return outs[:n_src], outs[n_src:]


def _cast_to_slot(shards, me_idx, after):
    n = len(shards)

    def body(i_ref, *refs):
        for w in range(n):
            refs[n + 1 + w][...] = refs[w][...].astype(bf16)

    return pl.pallas_call(
        body, name="cast_to_slot",
        grid_spec=pltpu.PrefetchScalarGridSpec(
            num_scalar_prefetch=1, grid=(1,),
            in_specs=[pl.BlockSpec(s.shape, lambda i, m: (0, 0)) for s in shards] + [pl.BlockSpec(memory_space=pl.ANY)],
            out_specs=[pl.BlockSpec((None, *s.shape), lambda i, m: (m[0], 0, 0)) for s in shards]),
        out_shape=[jax.ShapeDtypeStruct((N_DEV, *s.shape), bf16) for s in shards],
        compiler_params=_cp(("arbitrary",), VMEM_LIMIT),
    )(me_idx, *shards, after)


def _row_block(rows, n_blocks):
    return (rows // n_blocks, True) if rows % (16 * n_blocks) == 0 else (rows, False)


def _adam_math(w, g, m, v):
    m = ADAM_B1 * m + (1.0 - ADAM_B1) * g
    v = ADAM_B2 * v + (1.0 - ADAM_B2) * (g * g)
    m_hat = m / (1.0 - ADAM_B1 ** ADAM_STEP)
    v_hat = v / (1.0 - ADAM_B2 ** ADAM_STEP)
    delta = -ADAM_LR * (m_hat / (jnp.sqrt(v_hat) + ADAM_EPS) + ADAM_WD * w)
    return delta, m, v


def _adam_shards(me_idx, blocks, lands, ws, ms, vs, n_blocks, name, after=()):
    n = len(blocks)
    turned = [wt.ndim == 2 and wt.shape != g.shape[1:] for g, wt in zip(blocks, ws)]

    def body(i_ref, *refs):
        ins, outs = refs[:5 * n], refs[5 * n + len(after):]
        for w in range(n):
            g_ref, b_ref, w_ref, m_ref, v_ref = (ins[t * n + w] for t in range(5))
            g = g_ref[...].astype(f32)
            for k in range(N_PEERS):
                g = g + b_ref[k].astype(f32)
            if turned[w]:
                rb, cols = g.shape
                pad = -cols % LANES
                g = jnp.concatenate([g, jnp.zeros((rb, pad), f32)], axis=-1).T[:cols, :]
            if len(w_ref.shape) == 2:
                pieces = [(slice(None), g)]
            else:
                pieces = [(a, g[2 * a:2 * a + 2]) for a in range(2)]
            for at, gp in pieces:
                vals = (gp,) + _adam_math(w_ref[at], gp, m_ref[at], v_ref[at])
                for t, val in enumerate(vals):
                    outs[4 * w + t][at] = val

    specs = [[] for _ in range(5)]
    out_specs, out_shape = [], []
    for g, wt, turn in zip(blocks, ws, turned):
        rows, cols = g.shape[1:]
        rb, cut = _row_block(rows, n_blocks)
        specs[0].append(pl.BlockSpec((None, rb, cols), functools.partial(lambda i, s, cut: (s[0], i if cut else 0, 0), cut=cut)))
        specs[1].append(pl.BlockSpec((N_PEERS, rb, cols), functools.partial(lambda i, s, cut: (0, i if cut else 0, 0), cut=cut)))
        if turn:
            assert cut and rb % LANES == 0
            shard = pl.BlockSpec((cols, rb), lambda i, s: (0, i))
        elif wt.ndim == 2:
            shard = pl.BlockSpec((rb, cols), functools.partial(lambda i, s, cut: (i if cut else 0, 0), cut=cut))
        elif wt.shape[0] == 1:
            shard = pl.BlockSpec((None, rb, cols), functools.partial(lambda i, s, cut: (0, i if cut else 0, 0), cut=cut))
        else:
            shard = pl.BlockSpec(wt.shape, functools.partial(lambda i, s, nd: (0,) * nd, nd=wt.ndim))
        for t in (2, 3, 4):
            specs[t].append(shard)
        out_specs += [shard] * 4
        out_shape += [jax.ShapeDtypeStruct(wt.shape, f32)] * 4
    outs = pl.pallas_call(
        body, name=name,
        grid_spec=pltpu.PrefetchScalarGridSpec(
            num_scalar_prefetch=1, grid=(n_blocks,), in_specs=sum(specs, []) + [pl.BlockSpec(memory_space=pl.ANY)] * len(after),
            out_specs=out_specs),
        out_shape=out_shape,
        compiler_params=_cp(("arbitrary",), VMEM_LIMIT),
    )(me_idx, *blocks, *lands, *ws, *ms, *vs, *after)
    return [outs[4 * w:4 * w + 4] for w in range(n)]


def _adam_gains(parts, ws, ms, vs):
    n = len(ws)

    def body(p_ref, *refs):
        ins, outs = refs[:3 * n], refs[3 * n:]
        g_all = p_ref[0]
        for k in range(1, N_DEV):
            g_all = g_all + p_ref[k]
        for w, (off, lanes) in enumerate(SMALL.values()):
            w_ref, m_ref, v_ref = ins[w], ins[n + w], ins[2 * n + w]
            if len(w_ref.shape) == 2:
                pieces = [(slice(None), off, lanes)]
            else:
                pieces = [((slice(None), h), off + LANES * h, LANES) for h in range(w_ref.shape[1])]
            for at, o, ln in pieces:
                g = g_all[:, o:o + ln]
                vals = (g,) + _adam_math(w_ref[at], g, m_ref[at], v_ref[at])
                for t, val in enumerate(vals):
                    outs[4 * w + t][at] = val
        outs[4 * n][...] = g_all[:, LOSS_OFF:LOSS_OFF + LANES]

    out_shape = sum([[jax.ShapeDtypeStruct(w.shape, f32)] * 4 for w in ws], []) + [jax.ShapeDtypeStruct((1, LANES), f32)]
    outs = pl.pallas_call(body, name="adamw_gains", out_shape=out_shape)(parts, *ws, *ms, *vs)
    return [outs[4 * w:4 * w + 4] for w in range(n)], outs[4 * n]


def _fwd_in(x, g_mix, wz, tm):
    s, d = x.shape

    def body(x_ref, g_ref, w_ref, h_ref, z_ref):
        h, _ = _rms_fwd(x_ref[...], g_ref[...], d)
        hb = h.astype(bf16)
        h_ref[...] = hb
        z_ref[...] = _dot(hb, w_ref[...])

    return pl.pallas_call(
        body, name="fwd_in", grid=(s // tm,),
        in_specs=[pl.BlockSpec((tm, d), lambda i: (i, 0)), _const_spec((1, d)), _const_spec((d, Z_W))],
        out_specs=[pl.BlockSpec((tm, d), lambda i: (i, 0)), pl.BlockSpec((tm, Z_W), lambda i: (i, 0))],
        out_shape=[jax.ShapeDtypeStruct((s, d), bf16), jax.ShapeDtypeStruct((s, Z_W), f32)],
        compiler_params=_cp(("parallel",), VMEM_LIMIT),
    )(x, g_mix, wz)


def _mla_qk_fwd(cq, ckv, g_qa, g_kva, wqb, wkvb):
    cqn, rq = _rms_fwd(cq, g_qa, Q_LORA)
    ckvn, rkv = _rms_fwd(ckv, g_kva, KV_LORA)
    cqn_b, ckvn_b = cqn.astype(bf16), ckvn.astype(bf16)
    q0 = _dot(cqn_b, wqb)
    kv0 = _dot(ckvn_b, wkvb)
    return cqn_b, rq, ckvn_b, rkv, q0, kv0


def _fwd_mla_proj(z, cosb, sina, sinb, g_qa, g_kva, wqb, wkvb, g_qn, g_kn, tm):
    s = z.shape[0]
    hh = MLA_HEADS

    def body(cq_ref, ckv_ref, kr_ref, c_ref, sa_ref, sb_ref, gqa_ref, gkva_ref, wqb_ref, wkvb_ref, gqn_ref, gkn_ref,
             q_ref, k_ref, v_ref):
        _, _, _, _, q0, kv0 = _mla_qk_fwd(cq_ref[...], ckv_ref[...], gqa_ref[...], gkva_ref[...], wqb_ref[...], wkvb_ref[...])
        kr = kr_ref[...]
        c, sa, sb = c_ref[...], sa_ref[...], sb_ref[...]
        gqn, gkn = gqn_ref[...], gkn_ref[...]
        kr_sq = jnp.sum(kr * kr, axis=-1, keepdims=True)
        for h in range(hh):
            qh = q0[:, QK_PAD * h:QK_PAD * (h + 1)]
            qn, _ = _rms_fwd(qh, gqn, QK_HEAD)
            q_ref[h, :, 0:128] = qn[:, 0:128].astype(bf16)
            q_ref[h, :, 128:256] = _rope(qn[:, 128:256], c, sa, sb).astype(bf16)
            kn_ = kv0[:, 256 * h:256 * h + 128]
            rk = lax.rsqrt((jnp.sum(kn_ * kn_, axis=-1, keepdims=True) + kr_sq) * (1.0 / QK_HEAD) + EPS)
            k_ref[h, :, 0:128] = (kn_ * rk * gkn[:, 0:128]).astype(bf16)
            k_ref[h, :, 128:256] = _rope(kr * rk * gkn[:, 128:256], c, sa, sb).astype(bf16)
            v_ref[h] = kv0[:, 256 * h + 128:256 * h + 256].astype(bf16)

    row128 = pl.BlockSpec((tm, 128), lambda i: (i, 0))
    return pl.pallas_call(
        body, name="fwd_mla_proj", grid=(s // tm,),
        in_specs=[pl.BlockSpec((tm, 256), lambda i: (i, Z_CQ // 256)), pl.BlockSpec((tm, 256), lambda i: (i, Z_CKV // 256)),
                  pl.BlockSpec((tm, 128), lambda i: (i, Z_KR // 128)), row128, row128, row128,
                  _const_spec((1, 256)), _const_spec((1, 256)), _const_spec((256, 1024)), _const_spec((256, 1024)),
                  _const_spec((1, 256)), _const_spec((1, 256))],
        out_specs=[pl.BlockSpec((hh, tm, QK_PAD), lambda i: (0, i, 0)), pl.BlockSpec((hh, tm, QK_PAD), lambda i: (0, i, 0)),
                   pl.BlockSpec((hh, tm, V_HEAD), lambda i: (0, i, 0))],
        out_shape=[jax.ShapeDtypeStruct((hh, s, QK_PAD), bf16), jax.ShapeDtypeStruct((hh, s, QK_PAD), bf16),
                   jax.ShapeDtypeStruct((hh, s, V_HEAD), bf16)],
        compiler_params=_cp(("parallel",), VMEM_LIMIT),
    )(z, z, z, cosb, sina, sinb, g_qa, g_kva, wqb, wkvb, g_qn, g_kn)


def _fwd_attn(q, k, v, tq):
    hh, s, _ = q.shape

    n_sub = max(1, tq // ATTN_SUB_ROWS)

    def body(q_ref, k_ref, v_ref, o_ref, o32_ref):
        for t in range(n_sub):
            rows = slice(t * (tq // n_sub), (t + 1) * (tq // n_sub))
            sc = _dot_nt(q_ref[rows, :], k_ref[...])
            p = jnp.exp2((sc - jnp.max(sc, axis=-1, keepdims=True)) * (ATTN_SCALE * LOG2_E))
            l = jnp.sum(p, axis=-1, keepdims=True)
            o = _dot(p.astype(bf16), v_ref[...]) * (1.0 / l)
            o_ref[rows, :] = o.astype(bf16)
            o32_ref[rows, :] = o

    out = pl.BlockSpec((tq, V_HEAD), lambda h, i: (i, h))
    return pl.pallas_call(
        body, name="fwd_attn", grid=(hh, s // tq),
        in_specs=[pl.BlockSpec((None, tq, QK_PAD), lambda h, i: (h, i, 0)),
                  pl.BlockSpec((None, s, QK_PAD), lambda h, i: (h, 0, 0)),
                  pl.BlockSpec((None, s, V_HEAD), lambda h, i: (h, 0, 0))],
        out_specs=[out, out],
        out_shape=[jax.ShapeDtypeStruct((s, hh * V_HEAD), bf16), jax.ShapeDtypeStruct((s, hh * V_HEAD), f32)],
        compiler_params=_cp(("parallel", "parallel"), VMEM_LIMIT),
    )(q, k, v)


def _split3(x):
    hi = x.astype(bf16)
    r1 = x - hi.astype(f32)
    mid = r1.astype(bf16)
    lo = (r1 - mid.astype(f32)).astype(bf16)
    return jnp.concatenate([hi, mid, lo], axis=-1)


def _tri_sum(tri, x):
    y = _dot(tri, _split3(x))
    return y[:, 0:128] + y[:, 128:256] + y[:, 256:384]


GLA_GROUP = 4
GLA_ROWS = GLA_GROUP * CHUNK
GLA_HEADS_PER_STEP = 2


def _gla_masks(rev):
    row = lax.broadcasted_iota(jnp.int32, (GLA_ROWS, GLA_ROWS), 0)
    col = lax.broadcasted_iota(jnp.int32, (GLA_ROWS, GLA_ROWS), 1)
    shift = CHUNK.bit_length() - 1
    same = (jnp.right_shift(row, shift) == jnp.right_shift(col, shift)).astype(f32)
    lower, upper = (row >= col).astype(f32) * same, (row <= col).astype(f32) * same
    keep, keep_t = (upper, lower) if rev else (lower, upper)
    chunk_of = jnp.right_shift(lax.broadcasted_iota(jnp.int32, (GLA_ROWS, 1), 0), shift)
    return keep, keep.astype(bf16), keep_t.astype(bf16), [(chunk_of == c).astype(f32) for c in range(GLA_GROUP)]


def _gla_gates(hq, hf, lower):
    sg = _sigmoid(hf)
    f = lower + (1.0 - lower) * sg
    return hq * _sigmoid(hq), 1.0 - f, jnp.log(f), f, sg


def _gla_last_mid(b, rev):
    b3 = b.reshape(GLA_GROUP, CHUNK, 128)
    last, mid = (0, CHUNK // 2) if rev else (CHUNK - 1, CHUNK // 2 - 1)
    return b3[:, last:last + 1, :], b3[:, mid:mid + 1, :]


def _gla_per_row(per_chunk):
    return jnp.broadcast_to(per_chunk, (GLA_GROUP, CHUNK, 128)).reshape(GLA_ROWS, 128)


def _gla_block_diag(x, row_masks):
    return jnp.concatenate([(x * m).astype(bf16) for m in row_masks], axis=-1)


def _gla_diag(y):
    return jnp.concatenate([y[CHUNK * c:CHUNK * (c + 1), 128 * c:128 * (c + 1)] for c in range(GLA_GROUP)], axis=0)


def _gla_rows(n, n_groups, rev):
    ne = n_groups - 1 - n if rev else n
    return pl.ds(pl.multiple_of(ne * GLA_ROWS, GLA_ROWS), GLA_ROWS), ne * GLA_GROUP


def _gla_scan_order(rev):
    return tuple(reversed(range(GLA_GROUP))) if rev else tuple(range(GLA_GROUP))


def _fwd_gla(z, lb4):
    s = z.shape[0]
    n_groups = s // GLA_ROWS
    assert n_groups % 2 == 0
    hp = GLA_HEADS_PER_STEP
    chains = [(hh, rev) for hh in range(hp) for rev in (False, True)]

    def body(hq_ref, hff_ref, hfb_ref, hi_ref, lb_ref, o_ref, b_ref, states_ref, st_ref, stage_ref, b_stage, sems):
        st_ref[...] = jnp.zeros_like(st_ref)
        masks = {rev: _gla_masks(rev) for rev in (False, True)}
        lowers = [_sigmoid(lb_ref[int(rev):int(rev) + 1, 128 * hh:128 * (hh + 1)]
                           - lb_ref[2 + int(rev):3 + int(rev), 128 * hh:128 * (hh + 1)]) for hh, rev in chains]

        def states_out(slot, ci, chunk0):
            hh, rev = chains[ci]
            head = pl.program_id(0) * hp + hh
            rows = pl.ds(pl.multiple_of(chunk0 * CHUNK, GLA_ROWS), GLA_ROWS)
            return _Both(
                pltpu.make_async_copy(stage_ref.at[slot, ci], states_ref.at[head, int(rev), pl.ds(chunk0, GLA_GROUP)],
                                      sems.at[slot, ci]),
                pltpu.make_async_copy(b_stage.at[slot, ci], b_ref.at[int(rev), rows, pl.ds(pl.multiple_of(head * 128, 128), 128)],
                                      sems.at[slot, len(chains) + ci]))

        def make_step(first):
            def step(n, carry):
                slot = n % 2

                @pl.when(n >= 2)
                def _():
                    for ci in range(len(chains)):
                        states_out(slot, ci, 0).wait()

                for ci, (hh, rev) in enumerate(chains):
                    cols = slice(128 * hh, 128 * (hh + 1))
                    rows, chunk0 = _gla_rows(n, n_groups, rev)
                    maskf, tri, _, row_masks = masks[rev]
                    hf_ref = hfb_ref if rev else hff_ref
                    q, k, logf, _, _ = _gla_gates(hq_ref[rows, cols], hf_ref[rows, cols], lowers[ci])
                    vb = hi_ref[rows, cols].astype(bf16)
                    b = _tri_sum(tri, logf)
                    b_stage[slot, ci] = b
                    b_last3, b_mid3 = _gla_last_mid(b, rev)
                    b_last, b_mid = _gla_per_row(b_last3), _gla_per_row(b_mid3)
                    qi = (q * jnp.exp(b - b_mid)).astype(bf16)
                    ki = (k * jnp.exp(b_mid - b)).astype(bf16)
                    a = (_dot_nt(qi, ki) * maskf).astype(bf16)
                    kv = _dot_tn(vb, _gla_block_diag(k * jnp.exp(b_last - b), row_masks))
                    decay3 = jnp.exp(b_last3)
                    st = st_ref[ci]
                    before = [None] * GLA_GROUP
                    for c in _gla_scan_order(rev):
                        stage_ref[slot, ci, c] = st
                        before[c] = st.astype(bf16)
                        st = st * decay3[c] + kv[:, 128 * c:128 * (c + 1)]
                    st_ref[ci] = st
                    states_out(slot, ci, chunk0).start()
                    inter = _dot_nt((q * jnp.exp(b)).astype(bf16), jnp.concatenate(before, axis=0))
                    o = _dot(a, vb) + _gla_diag(inter)
                    if first:
                        o_ref[rows, cols] = o
                    else:
                        o_ref[rows, cols] += o
                return carry
            return step

        lax.fori_loop(0, n_groups // 2, make_step(True), 0)
        lax.fori_loop(n_groups // 2, n_groups, make_step(False), 0)
        for slot in range(2):
            for ci in range(len(chains)):
                states_out(slot, ci, 0).wait()

    w = 128 * hp
    col = lambda base: pl.BlockSpec((s, w), lambda h: (0, base // w + h))
    return pl.pallas_call(
        body, name="fwd_gla", grid=(HG_HEADS // hp,),
        in_specs=[col(Z_HQ), col(Z_HFF), col(Z_HFB), col(Z_HI), pl.BlockSpec((4, w), lambda h: (0, h))],
        out_specs=[pl.BlockSpec((s, w), lambda h: (0, h)), pl.BlockSpec(memory_space=pl.ANY), pl.BlockSpec(memory_space=pl.ANY)],
        out_shape=[jax.ShapeDtypeStruct((s, HG_HEADS * 128), f32), jax.ShapeDtypeStruct((2, s, HG_HEADS * 128), f32),
                   jax.ShapeDtypeStruct((HG_HEADS, 2, s // CHUNK, 128, 128), f32)],
        scratch_shapes=[pltpu.VMEM((len(chains), 128, 128), f32), pltpu.VMEM((2, len(chains), GLA_GROUP, 128, 128), f32),
                        pltpu.VMEM((2, len(chains), GLA_ROWS, 128), f32), pltpu.SemaphoreType.DMA((2, 2 * len(chains)))],
        compiler_params=_cp(("parallel",), VMEM_LIMIT),
    )(z, z, z, z, lb4)


def _hg_out(o, hg, g_hgo):
    outs, ons, rs = [], [], []
    for h in range(HG_HEADS):
        oh = o[:, 128 * h:128 * (h + 1)]
        on, r = _rms_fwd(oh, g_hgo[:, 128 * h:128 * (h + 1)], 128)
        ons.append(on)
        rs.append(r)
    on = jnp.concatenate(ons, axis=-1)
    sg = _sigmoid(hg)
    return on * (hg * sg), on, rs, sg


def _fwd_mix(a, o, z, g_hgo, x, w_o, tm):
    s, d = x.shape

    def body(a_ref, o_ref, hg_ref, g_ref, x_ref, w_ref, x2_ref, cat_ref):
        r, _, _, _ = _hg_out(o_ref[...], hg_ref[...], g_ref[...])
        cat = jnp.concatenate([a_ref[...], r.astype(bf16)], axis=-1)
        cat_ref[...] = cat
        x2_ref[...] = x_ref[...] + _dot(cat, w_ref[...])

    row512 = pl.BlockSpec((tm, 512), lambda i: (i, 0))
    rowd = pl.BlockSpec((tm, d), lambda i: (i, 0))
    return pl.pallas_call(
        body, name="fwd_mix", grid=(s // tm,),
        in_specs=[row512, row512, pl.BlockSpec((tm, 512), lambda i: (i, Z_HG // 512)), _const_spec((1, 512)), rowd,
                  _const_spec((d, d))],
        out_specs=[rowd, rowd],
        out_shape=[jax.ShapeDtypeStruct((s, d), f32), jax.ShapeDtypeStruct((s, d), bf16)],
        compiler_params=_cp(("parallel",), VMEM_LIMIT),
    )(a, o, z, g_hgo, x, w_o)


def _fwd_ffn(x2, g_ffn, w_gate, w_up, w_down, tm):
    s, d = x2.shape

    def body(x_ref, g_ref, wg_ref, wu_ref, wd_ref, x3_ref, gp_ref, up_ref):
        x = x_ref[...]
        h, _ = _rms_fwd(x, g_ref[...], d)
        hb = h.astype(bf16)
        gp = _dot_nt(hb, wg_ref[...])
        up = _dot_nt(hb, wu_ref[...])
        gp_ref[...] = gp.astype(bf16)
        up_ref[...] = up.astype(bf16)
        act = (gp * _sigmoid(gp) * up).astype(bf16)
        x3_ref[...] = x + _dot(act, wd_ref[...])

    rowd = pl.BlockSpec((tm, d), lambda i: (i, 0))
    rowf = pl.BlockSpec((tm, D_FF), lambda i: (i, 0))
    return pl.pallas_call(
        body, name="fwd_ffn", grid=(s // tm,),
        in_specs=[rowd, _const_spec((1, d)), _const_spec((D_FF, d)), _const_spec((D_FF, d)), _const_spec((D_FF, d))],
        out_specs=[rowd, rowf, rowf],
        out_shape=[jax.ShapeDtypeStruct((s, d), f32), jax.ShapeDtypeStruct((s, D_FF), bf16),
                   jax.ShapeDtypeStruct((s, D_FF), bf16)],
        compiler_params=_cp(("parallel",), VMEM_LIMIT),
    )(x2, g_ffn, w_gate, w_up, w_down)


def _ple_loss_fwd_bwd(x3, g_ple, w_pg, p, w_pp, target, tm):
    s, d = x3.shape

    def body(x_ref, g_ref, wg_ref, p_ref, wp_ref, t_ref, dx_ref, h_ref, dpre_ref, dpp_ref, dg_ref, loss_ref):
        @pl.when(pl.program_id(0) == 0)
        def _():
            dg_ref[...] = jnp.zeros_like(dg_ref)
            loss_ref[...] = jnp.zeros_like(loss_ref)

        x = x_ref[...]
        g = g_ref[...]
        h, r = _rms_fwd(x, g, d)
        hb = h.astype(bf16)
        gate = _sigmoid(_dot(hb, wg_ref[...]))
        pp = _dot(p_ref[...].astype(bf16), wp_ref[...])
        e = x + gate * pp - t_ref[...]
        loss_ref[...] += 0.5 * jnp.sum(e * e) * (1.0 / d)
        dy = e * (1.0 / d)
        dpre = (dy * pp * gate * (1.0 - gate)).astype(bf16)
        dx, dgx = _rms_bwd(_dot_nt(dpre, wg_ref[...]), x, r, g, d)
        dx_ref[...] = dy + dx
        dg_ref[...] += jnp.sum(dgx, axis=0, keepdims=True)
        h_ref[...] = hb
        dpre_ref[...] = dpre
        dpp_ref[...] = (dy * gate).astype(bf16)

    rowd = pl.BlockSpec((tm, d), lambda i: (i, 0))
    return pl.pallas_call(
        body, name="ple_loss_fwd_bwd", grid=(s // tm,),
        in_specs=[rowd, _const_spec((1, d)), _const_spec((d, d)), pl.BlockSpec((tm, PLE_DIM), lambda i: (i, 0)),
                  _const_spec((PLE_DIM, d)), rowd],
        out_specs=[rowd, rowd, rowd, rowd, _acc_spec((1, d)), _acc_spec((8, 128))],
        out_shape=[jax.ShapeDtypeStruct((s, d), f32), jax.ShapeDtypeStruct((s, d), bf16), jax.ShapeDtypeStruct((s, d), bf16),
                   jax.ShapeDtypeStruct((s, d), bf16), jax.ShapeDtypeStruct((1, d), f32), jax.ShapeDtypeStruct((8, 128), f32)],
        compiler_params=_cp(("arbitrary",), VMEM_LIMIT),
    )(x3, g_ple, w_pg, p, w_pp, target)


def _bwd_ffn_hidden(d3, gp, up, w_down, tm, tf):
    s, d = d3.shape

    def body(d3_ref, gp_ref, up_ref, wd_ref, act_ref, dgp_ref, dup_ref):
        gp, up = gp_ref[...].astype(f32), up_ref[...].astype(f32)
        sg = _sigmoid(gp)
        silu = gp * sg
        act_ref[...] = (silu * up).astype(bf16)
        dact = _dot_nt(d3_ref[...].astype(bf16), wd_ref[...])
        dgp_ref[...] = (dact * up * (sg * (1.0 + gp * (1.0 - sg)))).astype(bf16)
        dup_ref[...] = (dact * silu).astype(bf16)

    rowf = pl.BlockSpec((tm, tf), lambda f, i: (i, f))
    return pl.pallas_call(
        body, name="bwd_ffn_hidden", grid=(D_FF // tf, s // tm),
        in_specs=[pl.BlockSpec((tm, d), lambda f, i: (i, 0)), rowf, rowf, pl.BlockSpec((tf, d), lambda f, i: (f, 0))],
        out_specs=[rowf, rowf, rowf],
        out_shape=[jax.ShapeDtypeStruct((s, D_FF), bf16)] * 3,
        compiler_params=_cp(("parallel", "parallel"), VMEM_LIMIT),
    )(d3, gp, up, w_down)


def _bwd_ffn_in(d3, x2, dgp, dup, g_ffn, w_gate, w_up, tm):
    s, d = x2.shape

    def body(d3_ref, x_ref, dgp_ref, dup_ref, g_ref, wg_ref, wu_ref, d2_ref, h_ref, dg_ref):
        @pl.when(pl.program_id(0) == 0)
        def _():
            dg_ref[...] = jnp.zeros_like(dg_ref)

        x, g = x_ref[...], g_ref[...]
        dh = _dot(dgp_ref[...], wg_ref[...]) + _dot(dup_ref[...], wu_ref[...])
        h, r = _rms_fwd(x, g, d)
        h_ref[...] = h.astype(bf16)
        dx, dgx = _rms_bwd(dh, x, r, g, d)
        d2_ref[...] = d3_ref[...] + dx
        dg_ref[...] += jnp.sum(dgx, axis=0, keepdims=True)

    rowd = pl.BlockSpec((tm, d), lambda i: (i, 0))
    rowf = pl.BlockSpec((tm, D_FF), lambda i: (i, 0))
    return pl.pallas_call(
        body, name="bwd_ffn_in", grid=(s // tm,),
        in_specs=[rowd, rowd, rowf, rowf, _const_spec((1, d)), _const_spec((D_FF, d)), _const_spec((D_FF, d))],
        out_specs=[rowd, rowd, _acc_spec((1, d))],
        out_shape=[jax.ShapeDtypeStruct((s, d), f32), jax.ShapeDtypeStruct((s, d), bf16), jax.ShapeDtypeStruct((1, d), f32)],
        compiler_params=_cp(("arbitrary",), VMEM_LIMIT),
    )(d3, x2, dgp, dup, g_ffn, w_gate, w_up)


def _bwd_mix(d2, w_o, o, z, g_hgo, tm):
    s, d = d2.shape

    def body(d2_ref, w_ref, o_ref, hg_ref, g_ref, da_ref, do_ref, dhg_ref, dg_ref):
        @pl.when(pl.program_id(0) == 0)
        def _():
            dg_ref[...] = jnp.zeros_like(dg_ref)

        dcat = _dot_nt(d2_ref[...].astype(bf16), w_ref[...])
        da_ref[...] = dcat[:, 0:512].astype(bf16)
        dr = dcat[:, 512:1024]
        o, hg, g = o_ref[...], hg_ref[...], g_ref[...]
        _, on, rs, sg = _hg_out(o, hg, g)
        dhg_ref[...] = (dr * on * (sg * (1.0 + hg * (1.0 - sg)))).astype(bf16)
        don = dr * (hg * sg)
        dgs = []
        for h in range(HG_HEADS):
            cols = slice(128 * h, 128 * (h + 1))
            dx, dgx = _rms_bwd(don[:, cols], o[:, cols], rs[h], g[:, cols], 128)
            do_ref[:, cols] = dx
            dgs.append(jnp.sum(dgx, axis=0, keepdims=True))
        dg_ref[...] += jnp.concatenate(dgs, axis=-1)

    row512 = pl.BlockSpec((tm, 512), lambda i: (i, 0))
    return pl.pallas_call(
        body, name="bwd_mix", grid=(s // tm,),
        in_specs=[pl.BlockSpec((tm, d), lambda i: (i, 0)), _const_spec((d, d)), row512,
                  pl.BlockSpec((tm, 512), lambda i: (i, Z_HG // 512)), _const_spec((1, 512))],
        out_specs=[row512, row512, row512, _acc_spec((1, 512))],
        out_shape=[jax.ShapeDtypeStruct((s, 512), bf16), jax.ShapeDtypeStruct((s, 512), f32), jax.ShapeDtypeStruct((s, 512), bf16),
                   jax.ShapeDtypeStruct((1, 512), f32)],
        compiler_params=_cp(("arbitrary",), VMEM_LIMIT),
    )(d2, w_o, o, z, g_hgo)


def _bwd_gla(z, lb4, do, b_fwd, states):
    s = z.shape[0]
    n_chunks = s // CHUNK
    n_groups = s // GLA_ROWS
    assert n_groups % 2 == 0

    def body(hq_ref, hff_ref, hfb_ref, hi_ref, lb_ref, do_ref, b_all, st_all, dhq_ref, dhff_ref, dhfb_ref, dhi_ref, dlb_ref,
             dst_ref, dq_acc, dv_acc, dlow_ref):
        dirs = (False, True)
        masks = [_gla_masks(rev) for rev in dirs]
        lowers = [_sigmoid(lb_ref[int(rev):int(rev) + 1, :] - lb_ref[2 + int(rev):3 + int(rev), :]) for rev in dirs]
        hf_refs, dhf_refs = (hff_ref, hfb_ref), (dhff_ref, dhfb_ref)

        dst_ref[...] = jnp.zeros_like(dst_ref)
        dlow_ref[...] = jnp.zeros_like(dlow_ref)

        def make_bwd_step(first):
            def bwd_step(j, carry):
                n = n_groups - 1 - j
                for d, rev in enumerate(dirs):
                    maskf, _, tri_t, row_masks = masks[d]
                    lower = lowers[d]
                    rows, chunk0 = _gla_rows(n, n_groups, rev)
                    hq, hf = hq_ref[rows, :], hf_refs[d][rows, :]
                    q, k, _, f, sg = _gla_gates(hq, hf, lower)
                    v = hi_ref[rows, :]
                    dout = do_ref[rows, :]
                    b = b_all[d, rows, :]
                    b_last3, b_mid3 = _gla_last_mid(b, rev)
                    b_last, b_mid = _gla_per_row(b_last3), _gla_per_row(b_mid3)
                    e1, e2, e3, e4 = jnp.exp(b - b_mid), jnp.exp(b_mid - b), jnp.exp(b_last - b), jnp.exp(b)
                    decay3 = jnp.exp(b_last3)
                    qi, ki, kt, qt = q * e1, k * e2, k * e3, q * e4
                    qib, kib, ktb = qi.astype(bf16), ki.astype(bf16), kt.astype(bf16)
                    vb, dob = v.astype(bf16), dout.astype(bf16)
                    a = (_dot_nt(qib, kib) * maskf).astype(bf16)
                    da = (_dot_nt(dob, vb) * maskf).astype(bf16)
                    dqi = _dot(da, kib)
                    dki = _dot_tn(da, qib)
                    into_state = _dot_tn(dob, _gla_block_diag(qt, row_masks))
                    dst = dst_ref[d]
                    sts, dsts, ddecay = [None] * GLA_GROUP, [None] * GLA_GROUP, [None] * GLA_GROUP
                    for c in reversed(_gla_scan_order(rev)):
                        sts[c] = st_all[d, chunk0 + c]
                        dsts[c] = dst.astype(bf16)
                        ddecay[c] = jnp.sum(dst * sts[c], axis=0, keepdims=True)[None]
                        dst = dst * decay3[c] + into_state[:, 128 * c:128 * (c + 1)]
                    dst_ref[d] = dst
                    dv = _dot_tn(a, dob) + _gla_diag(_dot_nt(ktb, jnp.concatenate(dsts, axis=0)))
                    dqt = _gla_diag(_dot(dob, jnp.concatenate([x.astype(bf16) for x in sts], axis=-1)))
                    dkt = _gla_diag(_dot(vb, jnp.concatenate(dsts, axis=-1)))
                    dq = dqi * e1 + dqt * e4
                    dk = dki * e2 + dkt * e3
                    db = dqi * qi - dki * ki + dqt * qt - dkt * kt
                    dlast3 = (jnp.sum((dkt * kt).reshape(GLA_GROUP, CHUNK, 128), axis=1, keepdims=True)
                              + jnp.concatenate(ddecay, axis=0) * decay3)
                    dlogf = _tri_sum(tri_t, db) + _gla_per_row(dlast3)
                    df = dlogf / f - dk
                    dhf_refs[d][rows, :] = (df * (1.0 - lower) * sg * (1.0 - sg)).astype(bf16)
                    dlow_ref[d:d + 1, :] += jnp.sum(df * (1.0 - sg), axis=0, keepdims=True)
                    sq = _sigmoid(hq)
                    dhq = dq * (sq * (1.0 + hq * (1.0 - sq)))
                    if first:
                        dq_acc[rows, :] = dhq
                        dv_acc[rows, :] = dv
                    else:
                        dhq_ref[rows, :] = (dq_acc[rows, :] + dhq).astype(bf16)
                        dhi_ref[rows, :] = (dv_acc[rows, :] + dv).astype(bf16)
                return carry
            return bwd_step

        lax.fori_loop(0, n_groups // 2, make_bwd_step(True), 0)
        lax.fori_loop(n_groups // 2, n_groups, make_bwd_step(False), 0)

        for d in range(2):
            dl = dlow_ref[d:d + 1, :] * lowers[d] * (1.0 - lowers[d])
            dlb_ref[d:d + 1, :] = dl
            dlb_ref[2 + d:3 + d, :] = -dl

    col = lambda base: pl.BlockSpec((s, 128), lambda h: (0, base // 128 + h))
    return pl.pallas_call(
        body, name="bwd_gla", grid=(HG_HEADS,),
        in_specs=[col(Z_HQ), col(Z_HFF), col(Z_HFB), col(Z_HI), pl.BlockSpec((4, 128), lambda h: (0, h)), col(0),
                  pl.BlockSpec((2, s, 128), lambda h: (0, 0, h)),
                  pl.BlockSpec((None, 2, n_chunks, 128, 128), lambda h: (h, 0, 0, 0, 0), pipeline_mode=pl.Buffered(1))],
        out_specs=[col(0), col(0), col(0), col(0), pl.BlockSpec((4, 128), lambda h: (0, h))],
        out_shape=[jax.ShapeDtypeStruct((s, 512), bf16)] * 4 + [jax.ShapeDtypeStruct((4, 512), f32)],
        scratch_shapes=[pltpu.VMEM((2, 128, 128), f32), pltpu.VMEM((s, 128), f32), pltpu.VMEM((s, 128), f32),
                        pltpu.VMEM((2, 128), f32)],
        compiler_params=_cp(("parallel",), VMEM_LIMIT),
    )(z, z, z, z, lb4, do, b_fwd, states)


def _bwd_attn(q, k, v, da, a32, tq):
    hh, s, _ = q.shape

    n_sub = max(1, tq // ATTN_SUB_ROWS)

    def body(q_ref, k_ref, v_ref, do_ref, o_ref, dq_ref, dk_ref, dv_ref, p_all, ds_all, dol_ref, dkt_ref, dvt_ref):
        @pl.when(pl.program_id(1) == 0)
        def _():
            dkt_ref[...] = jnp.zeros_like(dkt_ref)
            dvt_ref[...] = jnp.zeros_like(dvt_ref)

        kb, vb = k_ref[...], v_ref[...]
        for t in range(n_sub):
            rows = slice(t * (tq // n_sub), (t + 1) * (tq // n_sub))
            sc = _dot_nt(q_ref[rows, :], kb)
            p = jnp.exp2((sc - jnp.max(sc, axis=-1, keepdims=True)) * (ATTN_SCALE * LOG2_E))
            inv_l = 1.0 / jnp.sum(p, axis=-1, keepdims=True)
            pb = p.astype(bf16)
            dob = do_ref[rows, :]
            dof = dob.astype(f32)
            delta = jnp.sum(dof * o_ref[rows, :], axis=-1, keepdims=True)
            ds = pb * ((_dot_nt(dob, vb) - delta) * inv_l).astype(bf16)
            dq_ref[rows, :] = _dot(ds, kb) * ATTN_SCALE
            p_all[rows, :] = pb
            ds_all[rows, :] = ds
            dol_ref[rows, :] = (dof * inv_l).astype(bf16)
        dkt_ref[...] += _dot_tn(q_ref[...], ds_all[...])
        dvt_ref[...] += _dot_tn(dol_ref[...], p_all[...])

        @pl.when(pl.program_id(1) == s // tq - 1)
        def _():
            dk_ref[...] = dkt_ref[...].T * ATTN_SCALE
            dv_ref[...] = dvt_ref[...].T

    return pl.pallas_call(
        body, name="bwd_attn", grid=(hh, s // tq),
        in_specs=[pl.BlockSpec((None, tq, QK_PAD), lambda h, i: (h, i, 0)),
                  pl.BlockSpec((None, s, QK_PAD), lambda h, i: (h, 0, 0)),
                  pl.BlockSpec((None, s, V_HEAD), lambda h, i: (h, 0, 0)),
                  pl.BlockSpec((tq, V_HEAD), lambda h, i: (i, h)), pl.BlockSpec((tq, V_HEAD), lambda h, i: (i, h))],
        out_specs=[pl.BlockSpec((None, tq, QK_PAD), lambda h, i: (h, i, 0)),
                   pl.BlockSpec((None, s, QK_PAD), lambda h, i: (h, 0, 0)),
                   pl.BlockSpec((None, s, V_HEAD), lambda h, i: (h, 0, 0))],
        out_shape=[jax.ShapeDtypeStruct((hh, s, QK_PAD), f32), jax.ShapeDtypeStruct((hh, s, QK_PAD), f32),
                   jax.ShapeDtypeStruct((hh, s, V_HEAD), f32)],
        scratch_shapes=[pltpu.VMEM((tq, s), bf16), pltpu.VMEM((tq, s), bf16), pltpu.VMEM((tq, V_HEAD), bf16),
                        pltpu.VMEM((QK_PAD, s), f32), pltpu.VMEM((V_HEAD, s), f32)],
        compiler_params=_cp(("parallel", "arbitrary"), VMEM_LIMIT),
    )(q, k, v, da, a32)


def _bwd_mla_proj(z, dq, dk, dv, cosb, sina, sinb, g_qa, g_kva, wqb, wkvb, g_qn, g_kn, tm):
    s = z.shape[0]
    hh = MLA_HEADS

    def body(cq_ref, ckv_ref, kr_ref, dq_ref, dk_ref, dv_ref, c_ref, sa_ref, sb_ref, gqa_ref, gkva_ref, wqb_ref, wkvb_ref,
             gqn_ref, gkn_ref, dz_ref, cqn_ref, ckvn_ref, dq0_ref, dkv0_ref, dgqa_ref, dgkva_ref, dgqn_ref, dgkn_ref):
        @pl.when(pl.program_id(0) == 0)
        def _():
            for r in (dgqa_ref, dgkva_ref, dgqn_ref, dgkn_ref):
                r[...] = jnp.zeros_like(r)

        cq, ckv, kr = cq_ref[...], ckv_ref[...], kr_ref[...]
        gqa, gkva, gqn, gkn = gqa_ref[...], gkva_ref[...], gqn_ref[...], gkn_ref[...]
        cqn_b, rq, ckvn_b, rkv, q0, kv0 = _mla_qk_fwd(cq, ckv, gqa, gkva, wqb_ref[...], wkvb_ref[...])
        cqn_ref[...] = cqn_b
        ckvn_ref[...] = ckvn_b
        c, sa, sb = c_ref[...], -sa_ref[...], -sb_ref[...]
        kr_sq = jnp.sum(kr * kr, axis=-1, keepdims=True)
        dkr = jnp.zeros_like(kr)
        dgqn = jnp.zeros((1, QK_PAD), f32)
        dgkn = jnp.zeros((1, QK_PAD), f32)
        for h in range(hh):
            qh = q0[:, QK_PAD * h:QK_PAD * (h + 1)]
            rh = lax.rsqrt(jnp.sum(qh * qh, axis=-1, keepdims=True) * (1.0 / QK_HEAD) + EPS)
            dqh = dq_ref[h]
            dqn = jnp.concatenate([dqh[:, 0:128], _rope(dqh[:, 128:256], c, sa, sb)], axis=-1)
            dq0h, dgx = _rms_bwd(dqn, qh, rh, gqn, QK_HEAD)
            dq0_ref[:, QK_PAD * h:QK_PAD * (h + 1)] = dq0h.astype(bf16)
            dgqn = dgqn + jnp.sum(dgx, axis=0, keepdims=True)

            kn_ = kv0[:, 256 * h:256 * h + 128]
            k0 = jnp.concatenate([kn_, kr], axis=-1)
            rk = lax.rsqrt((jnp.sum(kn_ * kn_, axis=-1, keepdims=True) + kr_sq) * (1.0 / QK_HEAD) + EPS)
            dkh = dk_ref[h]
            dkn = jnp.concatenate([dkh[:, 0:128], _rope(dkh[:, 128:256], c, sa, sb)], axis=-1)
            dk0, dgx = _rms_bwd(dkn, k0, rk, gkn, QK_HEAD)
            dgkn = dgkn + jnp.sum(dgx, axis=0, keepdims=True)
            dkv0_ref[:, 256 * h:256 * h + 128] = dk0[:, 0:128].astype(bf16)
            dkv0_ref[:, 256 * h + 128:256 * h + 256] = dv_ref[h].astype(bf16)
            dkr = dkr + dk0[:, 128:256]
        dgqn_ref[...] += dgqn
        dgkn_ref[...] += dgkn
        dcq, dgx = _rms_bwd(_dot_nt(dq0_ref[...], wqb_ref[...]), cq, rq, gqa, Q_LORA)
        dgqa_ref[...] += jnp.sum(dgx, axis=0, keepdims=True)
        dckv, dgx = _rms_bwd(_dot_nt(dkv0_ref[...], wkvb_ref[...]), ckv, rkv, gkva, KV_LORA)
        dgkva_ref[...] += jnp.sum(dgx, axis=0, keepdims=True)
        dz_ref[:, 0:256] = dcq.astype(bf16)
        dz_ref[:, 256:512] = dckv.astype(bf16)
        dz_ref[:, 512:640] = dkr.astype(bf16)

    row128 = pl.BlockSpec((tm, 128), lambda i: (i, 0))
    row256 = pl.BlockSpec((tm, 256), lambda i: (i, 0))
    row1024 = pl.BlockSpec((tm, 1024), lambda i: (i, 0))
    hd = lambda w: pl.BlockSpec((hh, tm, w), lambda i: (0, i, 0))
    return pl.pallas_call(
        body, name="bwd_mla_proj", grid=(s // tm,),
        in_specs=[pl.BlockSpec((tm, 256), lambda i: (i, Z_CQ // 256)), pl.BlockSpec((tm, 256), lambda i: (i, Z_CKV // 256)),
                  pl.BlockSpec((tm, 128), lambda i: (i, Z_KR // 128)), hd(QK_PAD), hd(QK_PAD), hd(V_HEAD),
                  row128, row128, row128,
                  _const_spec((1, 256)), _const_spec((1, 256)), _const_spec((256, 1024)), _const_spec((256, 1024)),
                  _const_spec((1, 256)), _const_spec((1, 256))],
        out_specs=[pl.BlockSpec((tm, 640), lambda i: (i, 0)), row256, row256, row1024, row1024,
                   _acc_spec((1, 256)), _acc_spec((1, 256)), _acc_spec((1, 256)), _acc_spec((1, 256))],
        out_shape=[jax.ShapeDtypeStruct((s, 640), bf16), jax.ShapeDtypeStruct((s, 256), bf16), jax.ShapeDtypeStruct((s, 256), bf16),
                   jax.ShapeDtypeStruct((s, 1024), bf16), jax.ShapeDtypeStruct((s, 1024), bf16)]
        + [jax.ShapeDtypeStruct((1, 256), f32)] * 4,
        compiler_params=_cp(("arbitrary",), VMEM_LIMIT),
    )(z, z, z, dq, dk, dv, cosb, sina, sinb, g_qa, g_kva, wqb, wkvb, g_qn, g_kn)


def _bwd_in(segments, wz, x, g_mix, d2, tm):
    s, d = x.shape
    n_seg = len(segments)

    def body(*refs):
        dz_refs, w_refs = refs[:n_seg], refs[n_seg:2 * n_seg]
        x_ref, g_ref, d2_ref, gx_ref, dg_ref = refs[2 * n_seg:]

        @pl.when(pl.program_id(0) == 0)
        def _():
            dg_ref[...] = jnp.zeros_like(dg_ref)

        dh = _dot_nt(dz_refs[0][...], w_refs[0][...])
        for a_ref, w_ref in zip(dz_refs[1:], w_refs[1:]):
            dh = dh + _dot_nt(a_ref[...], w_ref[...])
        x, g = x_ref[...], g_ref[...]
        r = lax.rsqrt(jnp.sum(x * x, axis=-1, keepdims=True) * (1.0 / d) + EPS)
        dx, dgx = _rms_bwd(dh, x, r, g, d)
        gx_ref[...] = d2_ref[...] + dx
        dg_ref[...] += jnp.sum(dgx, axis=0, keepdims=True)

    rowd = pl.BlockSpec((tm, d), lambda i: (i, 0))
    dz_specs = [pl.BlockSpec((tm, w), functools.partial(lambda i, j: (i, j), j=ja)) for _, w, ja, _ in segments]
    w_specs = [pl.BlockSpec((d, w), functools.partial(lambda i, j: (0, j), j=jw), pipeline_mode=pl.Buffered(1))
               for _, w, _, jw in segments]
    return pl.pallas_call(
        body, name="bwd_in", grid=(s // tm,),
        in_specs=dz_specs + w_specs + [rowd, _const_spec((1, d)), rowd],
        out_specs=[rowd, _acc_spec((1, d))],
        out_shape=[jax.ShapeDtypeStruct((s, d), f32), jax.ShapeDtypeStruct((1, d), f32)],
        compiler_params=_cp(("arbitrary",), VMEM_LIMIT),
    )(*[a for a, _, _, _ in segments], *([wz] * n_seg), x, g_mix, d2)


def _pick_tile(n, cap):
    best = None
    for t in range(LANES, cap + 1, LANES):
        if n % t == 0:
            best = t
    return best if best is not None else n


def _mm_tn_many(a, bs, name, tm, transposed=False):
    kk, m = a.shape
    n_b = len(bs)
    tk = min(1024, kk)
    n_k = kk // tk

    def body(a_ref, *refs):
        b_refs, o_refs, acc_refs = refs[:n_b], refs[n_b:2 * n_b], refs[2 * n_b:]

        @pl.when(pl.program_id(1) == 0)
        def _():
            for acc in acc_refs:
                acc[...] = jnp.zeros_like(acc)
        a_blk = a_ref[...].astype(bf16)
        for b_ref, acc in zip(b_refs, acc_refs):
            acc[...] += _dot_tn(a_blk, b_ref[...].astype(bf16))

        @pl.when(pl.program_id(1) == n_k - 1)
        def _():
            for o_ref, acc in zip(o_refs, acc_refs):
                o_ref[...] = (acc[...].T if transposed else acc[...]).astype(bf16)

    if transposed:
        out_specs = [pl.BlockSpec((b.shape[1], tm), lambda i, k: (0, i)) for b in bs]
        out_shape = [jax.ShapeDtypeStruct((b.shape[1], m), bf16) for b in bs]
    else:
        out_specs = [pl.BlockSpec((tm, b.shape[1]), lambda i, k: (i, 0)) for b in bs]
        out_shape = [jax.ShapeDtypeStruct((m, b.shape[1]), bf16) for b in bs]
    return pl.pallas_call(
        body, name=name, grid=(m // tm, n_k),
        in_specs=[pl.BlockSpec((tk, tm), lambda i, k: (k, i))] + [pl.BlockSpec((tk, b.shape[1]), lambda i, k: (k, 0)) for b in bs],
        out_specs=out_specs,
        out_shape=out_shape,
        scratch_shapes=[pltpu.VMEM((tm, b.shape[1]), f32) for b in bs],
        compiler_params=_cp(("parallel", "arbitrary"), VMEM_LIMIT),
    )(a, *bs)


def _mm_tn(a, b, name):
    kk, m = a.shape
    _, n = b.shape
    tm = _pick_tile(m, 1408)
    tn = _pick_tile(n, 1408)
    tk = min(1024, kk)

    n_k = kk // tk

    def body(a_ref, b_ref, o_ref, acc_ref):
        @pl.when(pl.program_id(2) == 0)
        def _():
            acc_ref[...] = jnp.zeros_like(acc_ref)
        acc_ref[...] += _dot_tn(a_ref[...].astype(bf16), b_ref[...].astype(bf16))

        @pl.when(pl.program_id(2) == n_k - 1)
        def _():
            o_ref[...] = acc_ref[...].astype(bf16)

    return pl.pallas_call(
        body, name=name, grid=(m // tm, n // tn, n_k),
        in_specs=[pl.BlockSpec((tk, tm), lambda i, j, k: (k, i)), pl.BlockSpec((tk, tn), lambda i, j, k: (k, j))],
        out_specs=pl.BlockSpec((tm, tn), lambda i, j, k: (i, j)),
        out_shape=jax.ShapeDtypeStruct((m, n), bf16),
        scratch_shapes=[pltpu.VMEM((tm, tn), f32)],
        compiler_params=_cp(("parallel", "parallel", "arbitrary"), VMEM_LIMIT),
    )(a, b)


def _rope_tables(positions):
    inv_freq = ROPE_THETA ** (-jnp.arange(0, QK_ROPE, 2, dtype=f32) / QK_ROPE)
    ang = positions.astype(f32)[:, None] * inv_freq
    cos, sin = jnp.cos(ang), jnp.sin(ang)
    zero = jnp.zeros_like(cos)
    return (jnp.concatenate([cos, cos, zero, zero], axis=1), jnp.concatenate([zero, sin, zero, zero], axis=1),
            jnp.concatenate([-sin, zero, zero, zero], axis=1))


def _pad256(g):
    return jnp.pad(g.reshape(1, QK_HEAD), ((0, 0), (0, QK_PAD - QK_HEAD)))


RELAYOUT_BLOCKS = 8
FIRST = ("w_in", "w_qb", "w_kvb", "lb_param")
SECOND = ("w_o", "w_gate", "w_up", "w_down", "w_ple_gate", "w_ple_proj")
ROW_SHARDED = ("w_o", "w_down", "w_ple_gate")


def _col_moves(j):
    lo = BIG["w_in"][1] * j
    w_in = [(max(lo, a) - lo, min(lo + BIG["w_in"][1], b) - lo, d + max(lo, a) - a)
            for a, b, d in Z_SEGMENTS if max(lo, a) < min(lo + BIG["w_in"][1], b)]
    head, half = divmod(j, 2)
    whole = lambda n: [(0, BIG[n][1], BIG[n][1] * j)]
    return {"w_in": w_in, "w_qb": [(0, 96, QK_PAD * head + 96 * half)], "w_kvb": whole("w_kvb"),
            "w_ple_proj": whole("w_ple_proj"), "lb_param": whole("lb_param")}


def _kernel_width(name):
    return {"w_in": Z_W, "w_qb": MLA_HEADS * QK_PAD}.get(name, N_DEV * BIG[name][1])


def _relayout_specs(names, by_dev):
    specs = []
    for n in names:
        rows, cols = BIG[n]
        if n == "lb_param":
            specs.append(_acc_spec((N_DEV, rows, cols) if by_dev else (rows, _kernel_width(n))))
        elif by_dev:
            specs.append(pl.BlockSpec((N_DEV, rows // RELAYOUT_BLOCKS, cols), lambda i: (0, i, 0)))
        else:
            specs.append(pl.BlockSpec((rows // RELAYOUT_BLOCKS, _kernel_width(n)), lambda i: (i, 0)))
    return specs


def _weights_in(gathered, names, name):
    n = len(names)

    def body(*refs):
        ins, outs = dict(zip(names, refs[:n])), dict(zip(names, refs[n:]))
        if "w_in" in outs:
            outs["w_in"][:, Z_KR + QK_ROPE:Z_W] = jnp.zeros((outs["w_in"].shape[0], Z_W - Z_KR - QK_ROPE), bf16)
        if "w_qb" in outs:
            for h in range(MLA_HEADS):
                outs["w_qb"][:, QK_PAD * h + QK_HEAD:QK_PAD * (h + 1)] = jnp.zeros((outs["w_qb"].shape[0], QK_PAD - QK_HEAD), bf16)
        for j in range(N_DEV):
            for wn, moves in _col_moves(j).items():
                if wn in outs:
                    for s0, s1, d0 in moves:
                        outs[wn][:, d0:d0 + s1 - s0] = ins[wn][j, :, s0:s1]

    outs = pl.pallas_call(
        body, name=name, grid=(RELAYOUT_BLOCKS,), in_specs=_relayout_specs(names, True), out_specs=_relayout_specs(names, False),
        out_shape=[jax.ShapeDtypeStruct((BIG[wn][0], _kernel_width(wn)), gathered[wn].dtype) for wn in names],
        compiler_params=_cp(("arbitrary",), VMEM_LIMIT),
    )(*[gathered[wn] for wn in names])
    return dict(zip(names, outs))


def _grads_out(sources, names, name):
    pieces = [(wn, start, arr) for wn in names for start, arr in sources[wn]]
    n_in = len(pieces)

    def body(*refs):
        outs = dict(zip(names, refs[n_in:]))

        def cols(wn, c0, c1):
            for (pn, start, arr), ref in zip(pieces, refs[:n_in]):
                if pn == wn and start <= c0 and c1 <= start + arr.shape[1]:
                    return ref[:, c0 - start:c1 - start]

        for j in range(N_DEV):
            for wn, moves in _col_moves(j).items():
                if wn in outs:
                    for s0, s1, d0 in moves:
                        outs[wn][j, :, s0:s1] = cols(wn, d0, d0 + s1 - s0).astype(bf16)

    in_specs = [_acc_spec(arr.shape) if wn == "lb_param" else pl.BlockSpec((arr.shape[0] // RELAYOUT_BLOCKS, arr.shape[1]), lambda i: (i, 0))
                for wn, _, arr in pieces]
    outs = pl.pallas_call(
        body, name=name, grid=(RELAYOUT_BLOCKS,), in_specs=in_specs, out_specs=_relayout_specs(names, True),
        out_shape=[jax.ShapeDtypeStruct((N_DEV, *BIG[wn]), bf16) for wn in names],
        compiler_params=_cp(("arbitrary",), VMEM_LIMIT),
    )(*[arr for _, _, arr in pieces])
    return dict(zip(names, outs))


def kernel(x, p, positions, g_mix, w_in, g_qa, g_kva, w_qb, w_kvb, g_qn, g_kn, lb_param, g_hgo, w_o, g_ffn, w_gate, w_up, w_down, g_ple, w_ple_gate, w_ple_proj, loss_target, m_g_mix, m_w_in, m_g_qa, m_g_kva, m_w_qb, m_w_kvb, m_g_qn, m_g_kn, m_lb_param, m_g_hgo, m_w_o, m_g_ffn, m_w_gate, m_w_up, m_w_down, m_g_ple, m_w_ple_gate, m_w_ple_proj, v_g_mix, v_w_in, v_g_qa, v_g_kva, v_w_qb, v_w_kvb, v_g_qn, v_g_kn, v_lb_param, v_g_hgo, v_w_o, v_g_ffn, v_w_gate, v_w_up, v_w_down, v_g_ple, v_w_ple_gate, v_w_ple_proj):
    w_all = dict(g_mix=g_mix, g_qa=g_qa, g_kva=g_kva, g_qn=g_qn, g_kn=g_kn, g_hgo=g_hgo, g_ffn=g_ffn, g_ple=g_ple,
                 w_in=w_in, w_qb=w_qb, w_kvb=w_kvb, w_o=w_o, w_gate=w_gate, w_up=w_up, w_down=w_down,
                 w_ple_gate=w_ple_gate, w_ple_proj=w_ple_proj, lb_param=lb_param)
    m_all = dict(g_mix=m_g_mix, g_qa=m_g_qa, g_kva=m_g_kva, g_qn=m_g_qn, g_kn=m_g_kn, g_hgo=m_g_hgo, g_ffn=m_g_ffn,
                 g_ple=m_g_ple, w_in=m_w_in, w_qb=m_w_qb, w_kvb=m_w_kvb, w_o=m_w_o, w_gate=m_w_gate, w_up=m_w_up,
                 w_down=m_w_down, w_ple_gate=m_w_ple_gate, w_ple_proj=m_w_ple_proj, lb_param=m_lb_param)
    v_all = dict(g_mix=v_g_mix, g_qa=v_g_qa, g_kva=v_g_kva, g_qn=v_g_qn, g_kn=v_g_kn, g_hgo=v_g_hgo, g_ffn=v_g_ffn,
                 g_ple=v_g_ple, w_in=v_w_in, w_qb=v_w_qb, w_kvb=v_w_kvb, w_o=v_w_o, w_gate=v_w_gate, w_up=v_w_up,
                 w_down=v_w_down, w_ple_gate=v_w_ple_gate, w_ple_proj=v_w_ple_proj, lb_param=v_lb_param)
    me_idx = jnp.stack([_me()]).astype(jnp.int32)
    x, p, positions, target = x[0], p[0, 0], positions[0], loss_target[0]
    s = x.shape[0]
    tm, tm_ffn, tq_f, tq_b = min(512, s), min(1024, s), min(2048, s), min(1024, s)
    g_mix, g_qa, g_kva, g_qn, g_kn, g_hgo, g_ffn, g_ple = (w_all[n].reshape(1, -1) for n in SMALL)
    g_qn_p, g_kn_p = _pad256(g_qn), _pad256(g_kn)
    cosb, sina, sinb = _rope_tables(positions)
    as_shard = lambda n, a: a[0].T if n in TRANSPOSED else a.reshape(BIG[n])
    shard = lambda n: as_shard(n, w_all[n])

    first = _all_gather([shard(n) for n in FIRST], [f32 if n == "lb_param" else bf16 for n in FIRST], "ag_first")
    lands = _cast_to_slot([shard(n) for n in SECOND], me_idx, first[0])
    ag2, token = _exchange_start([], lands, "ag_second_start")
    wk = _weights_in(dict(zip(FIRST, first)), FIRST, "weights_in_first")
    wz, wqb, wkvb, lb4 = (wk[n] for n in FIRST)

    h1, z = _fwd_in(x, g_mix, wz, tm)
    q, k, v = _fwd_mla_proj(z, cosb + token[0, 0], sina, sinb, g_qa, g_kva, wqb, wkvb, g_qn_p, g_kn_p, tm)
    a, a32 = _fwd_attn(q, k, v, tq_f)
    o, gla_b, gla_states = _fwd_gla(z, lb4)

    second = dict(zip(SECOND, _exchange_wait(ag2, [a, o], "ag_second_wait")[1]))
    w_pp = _weights_in(second, ("w_ple_proj",), "weights_in_second")["w_ple_proj"]
    w_o, w_down, w_pg, w_gate, w_up = (second[n].reshape(N_DEV * BIG[n][0], BIG[n][1]) for n in ROW_SHARDED + TRANSPOSED)

    x2, cat = _fwd_mix(a, o, z, g_hgo, x, w_o, tm)
    x3, gp, up = _fwd_ffn(x2, g_ffn, w_gate, w_up, w_down, tm)
    d3, h3, dpre, dpp, dg_ple, loss_tile = _ple_loss_fwd_bwd(x3, g_ple, w_pg, p, w_pp, target, tm)
    act, dgp, dup = _bwd_ffn_hidden(d3, gp, up, w_down, tm, D_FF // 2)
    d2, h2, dg_ffn = _bwd_ffn_in(d3, x2, dgp, dup, g_ffn, w_gate, w_up, tm)

    gw_gate, gw_up = _mm_tn_many(h2, [dgp, dup], "dw_gate_up", 512, transposed=True)
    blocks = _grads_out({"w_ple_proj": [(0, _mm_tn(p, dpp, "dw_ple_proj"))]}, ("w_ple_proj",), "grads_out_second")
    row_grads = {"w_o": _mm_tn(cat, d2, "dw_o"), "w_down": _mm_tn(act, d3, "dw_down"), "w_ple_gate": _mm_tn(h3, dpre, "dw_ple_gate"),
                 "w_gate": gw_gate, "w_up": gw_up}
    blocks.update({n: g.reshape(N_DEV, *BIG[n]) for n, g in row_grads.items()})
    empty = lambda names: [lax.empty((N_PEERS, *BIG[n]), bf16) for n in names]
    rs2, token = _exchange_start([blocks[n] for n in SECOND], empty(SECOND), "rs_second_start")

    da, do, dz_hg, dg_hgo = _bwd_mix(d2, w_o, o, z, g_hgo + token[0, 0], tm)
    dz_hq, dz_hff, dz_hfb, dz_hi, dlb4 = _bwd_gla(z, lb4, do, gla_b, gla_states)
    dq, dk, dv = _bwd_attn(q, k, v, da, a32, tq_b)
    dz_mla, cqn, ckvn, dq0, dkv0, dg_qa, dg_kva, dg_qn, dg_kn = _bwd_mla_proj(
        z, dq, dk, dv, cosb, sina, sinb, g_qa, g_kva, wqb, wkvb, g_qn_p, g_kn_p, tm)

    gz = list(zip((Z_HQ, Z_HFF, Z_HFB, Z_HI, Z_HG, Z_CQ),
                  _mm_tn_many(h1, [dz_hq, dz_hff, dz_hfb, dz_hi, dz_hg, dz_mla], "dw_in", 1024)))
    blocks1 = _grads_out({"w_in": gz, "w_qb": [(0, _mm_tn(cqn, dq0, "dw_qb"))], "w_kvb": [(0, _mm_tn(ckvn, dkv0, "dw_kvb"))],
                          "lb_param": [(0, dlb4)]}, FIRST, "grads_out_first")
    rs1, token = _exchange_start([blocks1[n] for n in FIRST], empty(FIRST), "rs_first_start")

    result = {}

    def adam(names, lands, src, n_blocks, after=()):
        flipped = TRANSPOSED + ("w_in",)
        given = lambda arrs: [arrs[n][0].T if n in flipped else arrs[n] for n in names]
        outs = _adam_shards(me_idx, [src[n] for n in names], lands, given(w_all), given(m_all), given(v_all), n_blocks,
                            "adamw_" + names[0], after)
        for n, o in zip(names, outs):
            result[n] = [t.T[None] for t in o] if n in flipped else o
        return outs[0][0]

    blocks2, lands2 = (dict(zip(SECOND, arrs)) for arrs in _exchange_wait(rs2, [token], "rs_second_wait"))
    by2 = ("w_down",) + TRANSPOSED
    by8 = tuple(n for n in SECOND if n not in by2)
    done = [adam(by8, [lands2[n] for n in by8], blocks2, 8), adam(by2, [lands2[n] for n in by2], blocks2, 2)]

    segments = [(dz_hq, 512, 0, Z_HQ // 512), (dz_hff, 512, 0, Z_HFF // 512), (dz_hfb, 512, 0, Z_HFB // 512),
                (dz_hi, 512, 0, Z_HI // 512), (dz_hg, 512, 0, Z_HG // 512), (dz_mla, 640, 0, Z_CQ // 640)]
    grad_x, dg_mix = _bwd_in(segments, wz, x, g_mix + token[0, 0], d2, tm)
    dgains = (dg_mix, dg_qa, dg_kva, dg_qn, dg_kn, dg_hgo, dg_ffn, dg_ple)

    vec = jnp.concatenate(list(dgains) + [loss_tile[0:1]], axis=1)
    parts = _all_gather([vec], [f32], "ag_gains")[0]
    outs, loss_row = _adam_gains(parts, [w_all[n] for n in SMALL], [m_all[n] for n in SMALL], [v_all[n] for n in SMALL])
    result.update(zip(SMALL, outs))

    blocks1, lands1 = _exchange_wait(rs1, [grad_x, loss_row, *done], "rs_first_wait")
    adam(FIRST, lands1, dict(zip(FIRST, blocks1)), 8)

    order = ("g_mix", "w_in", "g_qa", "g_kva", "w_qb", "w_kvb", "g_qn", "g_kn", "lb_param", "g_hgo", "w_o", "g_ffn",
             "w_gate", "w_up", "w_down", "g_ple", "w_ple_gate", "w_ple_proj")
    return (loss_row[0, 0], grad_x[None], *[result[n][k] for k in range(4) for n in order])
```

```python
import functools
import math

import jax
import jax.numpy as jnp
from jax import lax
from jax.experimental import pallas as pl
from jax.experimental.pallas import tpu as pltpu

f32 = jnp.float32
bf16 = jnp.bfloat16

N_DEV = 8
MLA_HEADS = 4
QK_NOPE = 128
QK_ROPE = 64
QK_HEAD = QK_NOPE + QK_ROPE
QK_PAD = 256
V_HEAD = 128
Q_LORA = 256
KV_LORA = 256
HG_HEADS = 4
CHUNK = 64
D_FF = 2816
PLE_DIM = 256
ROPE_THETA = 10000.0
EPS = 1e-6
ATTN_SCALE = QK_HEAD ** -0.5
LOG2_E = math.log2(math.e)
ATTN_SUB_ROWS = 256
Z_HQ, Z_HFF, Z_HFB, Z_HI, Z_HG, Z_CQ, Z_CKV, Z_KR, Z_W = 0, 512, 1024, 1536, 2048, 2560, 2816, 3072, 3200

ADAM_LR, ADAM_B1, ADAM_B2, ADAM_EPS, ADAM_WD, ADAM_STEP = 0.001, 0.9, 0.999, 1e-08, 0.01, 10

LANES = 128
BIG = {"w_in": (1024, 392), "w_qb": (256, 96), "w_kvb": (256, 128), "w_o": (128, 1024), "w_gate": (352, 1024),
       "w_up": (352, 1024), "w_down": (352, 1024), "w_ple_gate": (128, 1024), "w_ple_proj": (256, 128),
       "lb_param": (4, 64)}
TRANSPOSED = ("w_gate", "w_up")
SMALL = {"g_mix": (0, 1024), "g_qa": (1024, 256), "g_kva": (1280, 256), "g_qn": (1536, 192), "g_kn": (1792, 192),
         "g_hgo": (2048, 512), "g_ffn": (2560, 1024), "g_ple": (3584, 1024)}
LOSS_OFF = 4608
GAIN_VEC = LOSS_OFF + LANES
Z_SEGMENTS = ((0, 256, Z_CQ), (256, 512, Z_CKV), (512, 576, Z_KR), (576, 1088, Z_HQ), (1088, 1600, Z_HFF),
              (1600, 2112, Z_HFB), (2112, 2624, Z_HI), (2624, 3136, Z_HG))

VMEM_LIMIT = 56 * 1024 * 1024
MESH = pl.DeviceIdType.MESH


def _cp(sem=None, vmem=None):
    return pltpu.CompilerParams(dimension_semantics=sem, vmem_limit_bytes=vmem)


def _const_spec(shape):
    nd = len(shape)
    return pl.BlockSpec(shape, lambda *_: (0,) * nd, pipeline_mode=pl.Buffered(1))


def _acc_spec(shape):
    nd = len(shape)
    return pl.BlockSpec(shape, lambda *_: (0,) * nd)


def _sigmoid(x):
    return jax.nn.sigmoid(x)


def _dot(a, b):
    return jnp.dot(a, b, preferred_element_type=f32)


def _dot_nt(a, b):
    return lax.dot_general(a, b, (((1,), (1,)), ((), ())), preferred_element_type=f32)


def _dot_tn(a, b):
    return lax.dot_general(a, b, (((0,), (0,)), ((), ())), preferred_element_type=f32)


def _rms_fwd(x, g, width):
    r = lax.rsqrt(jnp.sum(x * x, axis=-1, keepdims=True) * (1.0 / width) + EPS)
    return x * r * g, r


def _rms_bwd(dy, x, r, g, width):
    u = dy * g
    dx = r * u - x * (r * r * r) * (jnp.sum(u * x, axis=-1, keepdims=True) * (1.0 / width))
    return dx, dy * x * r


class _Both:
    def __init__(self, *copies):
        self.copies = copies

    def start(self):
        for cp in self.copies:
            cp.start()

    def wait(self):
        for cp in self.copies:
            cp.wait()


def _rope(b, c, sa, sb):
    return b * c + pltpu.roll(b, 32, 1) * sa + pltpu.roll(b, 96, 1) * sb


def _all_gather(shards, dtypes, name):
    n = len(shards)

    def body(*refs):
        in_refs, out_refs, stage = refs[:n], refs[n:2 * n], refs[2 * n:3 * n]
        send_sems, recv_sems, local_sems = refs[3 * n:]
        for w in range(n):
            stage[w][...] = in_refs[w][...].astype(stage[w].dtype)
        x, y, c = lax.axis_index("x"), lax.axis_index("y"), lax.axis_index("c")
        me, sibling = (x, y, c), (x, y, 1 - c)
        chips = [(1 - x, y), (x, 1 - y), (1 - x, 1 - y)]

        def slot(w, px, py, pc):
            return out_refs[w].at[4 * px + 2 * py + pc]

        def copy(w, k, block, to, src=None):
            return pltpu.make_async_remote_copy(
                src_ref=slot(w, *block) if src is None else src, dst_ref=slot(w, *block),
                send_sem=send_sems.at[w, k], recv_sem=recv_sems.at[w, k], device_id=to, device_id_type=MESH)

        first = []
        for j, chip in enumerate(chips):
            first += [copy(w, 1 + j, me, (*chip, c), src=stage[w]) for w in range(n)]
        first += [copy(w, 0, me, sibling, src=stage[w]) for w in range(n)]
        mine = [pltpu.make_async_copy(stage[w], slot(w, *me), local_sems.at[w]) for w in range(n)]
        for cp in first + mine:
            cp.start()
        passed = []
        for j, chip in enumerate(chips):
            for w in range(n):
                copy(w, 1 + j, (*chip, c), me).wait_recv()
                passed.append(copy(w, 4 + j, (*chip, c), sibling))
                passed[-1].start()
        for w in range(n):
            copy(w, 0, sibling, me).wait_recv()
        for j, chip in enumerate(chips):
            for w in range(n):
                copy(w, 4 + j, (*chip, 1 - c), me).wait_recv()
        for cp in first + passed:
            cp.wait_send()
        for cp in mine:
            cp.wait()

    return pl.pallas_call(
        body, name=name,
        out_shape=[jax.ShapeDtypeStruct((N_DEV, *s.shape), dt) for s, dt in zip(shards, dtypes)],
        in_specs=[pl.BlockSpec(memory_space=pltpu.VMEM)] * n,
        out_specs=[pl.BlockSpec(memory_space=pl.ANY)] * n,
        scratch_shapes=[pltpu.VMEM(s.shape, dt) for s, dt in zip(shards, dtypes)]
        + [pltpu.SemaphoreType.DMA((n, 7)), pltpu.SemaphoreType.DMA((n, 7)), pltpu.SemaphoreType.DMA((n,))],
        compiler_params=_cp(None, VMEM_LIMIT),
    )(*shards)


N_PEERS = N_DEV - 1
HBM_SPEC = pl.BlockSpec(memory_space=pltpu.HBM)
SEM_SPEC = pl.BlockSpec(memory_space=pltpu.SEMAPHORE)
DATAFLOW = pltpu.SideEffectType.DATAFLOW_SIDE_EFFECTING


def _me():
    return 4 * lax.axis_index("x") + 2 * lax.axis_index("y") + lax.axis_index("c")


def _peer(k):
    x, y, c = lax.axis_index("x"), lax.axis_index("y"), lax.axis_index("c")
    px = 1 - x if k & 4 else x
    py = 1 - y if k & 2 else y
    pc = 1 - c if k & 1 else c
    return (px, py, pc), 4 * px + 2 * py + pc


def _exchange_copies(src_refs, land_refs, send_sems, recv_sems, gather):
    cps = []
    me = _me()
    for k in range(1, N_DEV):
        peer, peer_idx = _peer(k)
        for w, land in enumerate(land_refs):
            src = land.at[me] if gather else src_refs[w].at[peer_idx]
            dst = land.at[me] if gather else land.at[k - 1]
            cps.append(pltpu.make_async_remote_copy(
                src_ref=src, dst_ref=dst, send_sem=send_sems.at[N_PEERS * w + k - 1], recv_sem=recv_sems.at[N_PEERS * w + k - 1],
                device_id=peer, device_id_type=MESH))
    return cps


def _exchange_start(srcs, lands, name):
    n_src, n = len(srcs), len(lands)

    def body(*refs):
        src_refs, land_refs = refs[:n_src], refs[n_src:n_src + n]
        send_sems, recv_sems = refs[n_src + n], refs[n_src + n + 1]
        token = refs[-1]
        for cp in _exchange_copies(src_refs, land_refs, send_sems, recv_sems, gather=not n_src):
            cp.start()
        token[...] = jnp.zeros_like(token)

    arrays = [pltpu.with_memory_space_constraint(a, pltpu.HBM) for a in (*srcs, *lands)]
    outs = pl.pallas_call(
        body, name=name,
        out_shape=(pltpu.SemaphoreType.DMA((n * N_PEERS,)), pltpu.SemaphoreType.DMA((n * N_PEERS,)),
                   *[pltpu.HBM(a.shape, a.dtype) for a in arrays], jax.ShapeDtypeStruct((8, LANES), f32)),
        in_specs=[HBM_SPEC] * len(arrays),
        out_specs=(SEM_SPEC, SEM_SPEC, *[HBM_SPEC] * len(arrays), pl.BlockSpec(memory_space=pltpu.VMEM)),
        input_output_aliases={i: 2 + i for i in range(len(arrays))},
        compiler_params=pltpu.CompilerParams(has_side_effects=DATAFLOW),
    )(*arrays)
    return (outs[0], outs[1], outs[2:2 + n_src], outs[2 + n_src:2 + n_src + n]), outs[-1]


def _exchange_wait(state, after, name):
    send_sems, recv_sems, srcs, lands = state
    n_src, n = len(srcs), len(lands)

    def body(*refs):
        src_refs, land_refs = refs[:n_src], refs[n_src:n_src + n]
        send_ref, recv_ref = refs[n_src + n], refs[n_src + n + 1]
        for cp in _exchange_copies(src_refs, land_refs, send_ref, recv_ref, gather=not n_src):
            cp.wait_send()
            cp.wait_recv()

    arrays = (*srcs, *lands)
    outs = pl.pallas_call(
        body, name=name,
        out_shape=tuple(pltpu.HBM(a.shape, a.dtype) for a in arrays),
        in_specs=[HBM_SPEC] * len(arrays) + [SEM_SPEC, SEM_SPEC] + [pl.BlockSpec(memory_space=pl.ANY)] * len(after),
        out_specs=tuple([HBM_SPEC] * len(arrays)),
        input_output_aliases={i: i for i in range(len(arrays))},
        compiler_params=pltpu.CompilerParams(has_side_effects=DATAFLOW),
    )(*arrays, send_sems, recv_sems, *after)
    return outs[:n_src], outs[n_src:]


def _cast_to_slot(shards, me_idx, after):
    n = len(shards)

    def body(i_ref, *refs):
        for w in range(n):
            refs[n + 1 + w][...] = refs[w][...].astype(bf16)

    return pl.pallas_call(
        body, name="cast_to_slot",
        grid_spec=pltpu.PrefetchScalarGridSpec(
            num_scalar_prefetch=1, grid=(1,),
            in_specs=[pl.BlockSpec(s.shape, lambda i, m: (0, 0)) for s in shards] + [pl.BlockSpec(memory_space=pl.ANY)],
            out_specs=[pl.BlockSpec((None, *s.shape), lambda i, m: (m[0], 0, 0)) for s in shards]),
        out_shape=[jax.ShapeDtypeStruct((N_DEV, *s.shape), bf16) for s in shards],
        compiler_params=_cp(("arbitrary",), VMEM_LIMIT),
    )(me_idx, *shards, after)


def _row_block(rows, n_blocks):
    return (rows // n_blocks, True) if rows % (16 * n_blocks) == 0 else (rows, False)


def _adam_math(w, g, m, v):
    m = ADAM_B1 * m + (1.0 - ADAM_B1) * g
    v = ADAM_B2 * v + (1.0 - ADAM_B2) * (g * g)
    m_hat = m / (1.0 - ADAM_B1 ** ADAM_STEP)
    v_hat = v / (1.0 - ADAM_B2 ** ADAM_STEP)
    delta = -ADAM_LR * (m_hat / (jnp.sqrt(v_hat) + ADAM_EPS) + ADAM_WD * w)
    return delta, m, v


def _adam_shards(me_idx, blocks, lands, ws, ms, vs, n_blocks, name, after=()):
    n = len(blocks)
    turned = [wt.ndim == 2 and wt.shape != g.shape[1:] for g, wt in zip(blocks, ws)]

    def body(i_ref, *refs):
        ins, outs = refs[:5 * n], refs[5 * n + len(after):]
        for w in range(n):
            g_ref, b_ref, w_ref, m_ref, v_ref = (ins[t * n + w] for t in range(5))
            g = g_ref[...].astype(f32)
            for k in range(N_PEERS):
                g = g + b_ref[k].astype(f32)
            if turned[w]:
                rb, cols = g.shape
                pad = -cols % LANES
                g = jnp.concatenate([g, jnp.zeros((rb, pad), f32)], axis=-1).T[:cols, :]
            if len(w_ref.shape) == 2:
                pieces = [(slice(None), g)]
            else:
                pieces = [(a, g[2 * a:2 * a + 2]) for a in range(2)]
            for at, gp in pieces:
                vals = (gp,) + _adam_math(w_ref[at], gp, m_ref[at], v_ref[at])
                for t, val in enumerate(vals):
                    outs[4 * w + t][at] = val

    specs = [[] for _ in range(5)]
    out_specs, out_shape = [], []
    for g, wt, turn in zip(blocks, ws, turned):
        rows, cols = g.shape[1:]
        rb, cut = _row_block(rows, n_blocks)
        specs[0].append(pl.BlockSpec((None, rb, cols), functools.partial(lambda i, s, cut: (s[0], i if cut else 0, 0), cut=cut)))
        specs[1].append(pl.BlockSpec((N_PEERS, rb, cols), functools.partial(lambda i, s, cut: (0, i if cut else 0, 0), cut=cut)))
        if turn:
            assert cut and rb % LANES == 0
            shard = pl.BlockSpec((cols, rb), lambda i, s: (0, i))
        elif wt.ndim == 2:
            shard = pl.BlockSpec((rb, cols), functools.partial(lambda i, s, cut: (i if cut else 0, 0), cut=cut))
        elif wt.shape[0] == 1:
            shard = pl.BlockSpec((None, rb, cols), functools.partial(lambda i, s, cut: (0, i if cut else 0, 0), cut=cut))
        else:
            shard = pl.BlockSpec(wt.shape, functools.partial(lambda i, s, nd: (0,) * nd, nd=wt.ndim))
        for t in (2, 3, 4):
            specs[t].append(shard)
        out_specs += [shard] * 4
        out_shape += [jax.ShapeDtypeStruct(wt.shape, f32)] * 4
    outs = pl.pallas_call(
        body, name=name,
        grid_spec=pltpu.PrefetchScalarGridSpec(
            num_scalar_prefetch=1, grid=(n_blocks,), in_specs=sum(specs, []) + [pl.BlockSpec(memory_space=pl.ANY)] * len(after),
            out_specs=out_specs),
        out_shape=out_shape,
        compiler_params=_cp(("arbitrary",), VMEM_LIMIT),
    )(me_idx, *blocks, *lands, *ws, *ms, *vs, *after)
    return [outs[4 * w:4 * w + 4] for w in range(n)]


def _adam_gains(parts, ws, ms, vs):
    n = len(ws)

    def body(p_ref, *refs):
        ins, outs = refs[:3 * n], refs[3 * n:]
        g_all = p_ref[0]
        for k in range(1, N_DEV):
            g_all = g_all + p_ref[k]
        for w, (off, lanes) in enumerate(SMALL.values()):
            w_ref, m_ref, v_ref = ins[w], ins[n + w], ins[2 * n + w]
            if len(w_ref.shape) == 2:
                pieces = [(slice(None), off, lanes)]
            else:
                pieces = [((slice(None), h), off + LANES * h, LANES) for h in range(w_ref.shape[1])]
            for at, o, ln in pieces:
                g = g_all[:, o:o + ln]
                vals = (g,) + _adam_math(w_ref[at], g, m_ref[at], v_ref[at])
                for t, val in enumerate(vals):
                    outs[4 * w + t][at] = val
        outs[4 * n][...] = g_all[:, LOSS_OFF:LOSS_OFF + LANES]

    out_shape = sum([[jax.ShapeDtypeStruct(w.shape, f32)] * 4 for w in ws], []) + [jax.ShapeDtypeStruct((1, LANES), f32)]
    outs = pl.pallas_call(body, name="adamw_gains", out_shape=out_shape)(parts, *ws, *ms, *vs)
    return [outs[4 * w:4 * w + 4] for w in range(n)], outs[4 * n]


def _fwd_in(x, g_mix, wz, tm):
    s, d = x.shape

    def body(x_ref, g_ref, w_ref, h_ref, z_ref):
        h, _ = _rms_fwd(x_ref[...], g_ref[...], d)
        hb = h.astype(bf16)
        h_ref[...] = hb
        z_ref[...] = _dot(hb, w_ref[...])

    return pl.pallas_call(
        body, name="fwd_in", grid=(s // tm,),
        in_specs=[pl.BlockSpec((tm, d), lambda i: (i, 0)), _const_spec((1, d)), _const_spec((d, Z_W))],
        out_specs=[pl.BlockSpec((tm, d), lambda i: (i, 0)), pl.BlockSpec((tm, Z_W), lambda i: (i, 0))],
        out_shape=[jax.ShapeDtypeStruct((s, d), bf16), jax.ShapeDtypeStruct((s, Z_W), f32)],
        compiler_params=_cp(("parallel",), VMEM_LIMIT),
    )(x, g_mix, wz)


def _mla_qk_fwd(cq, ckv, g_qa, g_kva, wqb, wkvb):
    cqn, rq = _rms_fwd(cq, g_qa, Q_LORA)
    ckvn, rkv = _rms_fwd(ckv, g_kva, KV_LORA)
    cqn_b, ckvn_b = cqn.astype(bf16), ckvn.astype(bf16)
    q0 = _dot(cqn_b, wqb)
    kv0 = _dot(ckvn_b, wkvb)
    return cqn_b, rq, ckvn_b, rkv, q0, kv0


def _fwd_mla_proj(z, cosb, sina, sinb, g_qa, g_kva, wqb, wkvb, g_qn, g_kn, tm):
    s = z.shape[0]
    hh = MLA_HEADS

    def body(cq_ref, ckv_ref, kr_ref, c_ref, sa_ref, sb_ref, gqa_ref, gkva_ref, wqb_ref, wkvb_ref, gqn_ref, gkn_ref,
             q_ref, k_ref, v_ref):
        _, _, _, _, q0, kv0 = _mla_qk_fwd(cq_ref[...], ckv_ref[...], gqa_ref[...], gkva_ref[...], wqb_ref[...], wkvb_ref[...])
        kr = kr_ref[...]
        c, sa, sb = c_ref[...], sa_ref[...], sb_ref[...]
        gqn, gkn = gqn_ref[...], gkn_ref[...]
        kr_sq = jnp.sum(kr * kr, axis=-1, keepdims=True)
        for h in range(hh):
            qh = q0[:, QK_PAD * h:QK_PAD * (h + 1)]
            qn, _ = _rms_fwd(qh, gqn, QK_HEAD)
            q_ref[h, :, 0:128] = qn[:, 0:128].astype(bf16)
            q_ref[h, :, 128:256] = _rope(qn[:, 128:256], c, sa, sb).astype(bf16)
            kn_ = kv0[:, 256 * h:256 * h + 128]
            rk = lax.rsqrt((jnp.sum(kn_ * kn_, axis=-1, keepdims=True) + kr_sq) * (1.0 / QK_HEAD) + EPS)
            k_ref[h, :, 0:128] = (kn_ * rk * gkn[:, 0:128]).astype(bf16)
            k_ref[h, :, 128:256] = _rope(kr * rk * gkn[:, 128:256], c, sa, sb).astype(bf16)
            v_ref[h] = kv0[:, 256 * h + 128:256 * h + 256].astype(bf16)

    row128 = pl.BlockSpec((tm, 128), lambda i: (i, 0))
    return pl.pallas_call(
        body, name="fwd_mla_proj", grid=(s // tm,),
        in_specs=[pl.BlockSpec((tm, 256), lambda i: (i, Z_CQ // 256)), pl.BlockSpec((tm, 256), lambda i: (i, Z_CKV // 256)),
                  pl.BlockSpec((tm, 128), lambda i: (i, Z_KR // 128)), row128, row128, row128,
                  _const_spec((1, 256)), _const_spec((1, 256)), _const_spec((256, 1024)), _const_spec((256, 1024)),
                  _const_spec((1, 256)), _const_spec((1, 256))],
        out_specs=[pl.BlockSpec((hh, tm, QK_PAD), lambda i: (0, i, 0)), pl.BlockSpec((hh, tm, QK_PAD), lambda i: (0, i, 0)),
                   pl.BlockSpec((hh, tm, V_HEAD), lambda i: (0, i, 0))],
        out_shape=[jax.ShapeDtypeStruct((hh, s, QK_PAD), bf16), jax.ShapeDtypeStruct((hh, s, QK_PAD), bf16),
                   jax.ShapeDtypeStruct((hh, s, V_HEAD), bf16)],
        compiler_params=_cp(("parallel",), VMEM_LIMIT),
    )(z, z, z, cosb, sina, sinb, g_qa, g_kva, wqb, wkvb, g_qn, g_kn)


def _fwd_attn(q, k, v, tq):
    hh, s, _ = q.shape

    n_sub = max(1, tq // ATTN_SUB_ROWS)

    def body(q_ref, k_ref, v_ref, o_ref, o32_ref):
        for t in range(n_sub):
            rows = slice(t * (tq // n_sub), (t + 1) * (tq // n_sub))
            sc = _dot_nt(q_ref[rows, :], k_ref[...])
            p = jnp.exp2((sc - jnp.max(sc, axis=-1, keepdims=True)) * (ATTN_SCALE * LOG2_E))
            l = jnp.sum(p, axis=-1, keepdims=True)
            o = _dot(p.astype(bf16), v_ref[...]) * (1.0 / l)
            o_ref[rows, :] = o.astype(bf16)
            o32_ref[rows, :] = o

    out = pl.BlockSpec((tq, V_HEAD), lambda h, i: (i, h))
    return pl.pallas_call(
        body, name="fwd_attn", grid=(hh, s // tq),
        in_specs=[pl.BlockSpec((None, tq, QK_PAD), lambda h, i: (h, i, 0)),
                  pl.BlockSpec((None, s, QK_PAD), lambda h, i: (h, 0, 0)),
                  pl.BlockSpec((None, s, V_HEAD), lambda h, i: (h, 0, 0))],
        out_specs=[out, out],
        out_shape=[jax.ShapeDtypeStruct((s, hh * V_HEAD), bf16), jax.ShapeDtypeStruct((s, hh * V_HEAD), f32)],
        compiler_params=_cp(("parallel", "parallel"), VMEM_LIMIT),
    )(q, k, v)


def _split3(x):
    hi = x.astype(bf16)
    r1 = x - hi.astype(f32)
    mid = r1.astype(bf16)
    lo = (r1 - mid.astype(f32)).astype(bf16)
    return jnp.concatenate([hi, mid, lo], axis=-1)


def _tri_sum(tri, x):
    y = _dot(tri, _split3(x))
    return y[:, 0:128] + y[:, 128:256] + y[:, 256:384]


GLA_GROUP = 4
GLA_ROWS = GLA_GROUP * CHUNK
GLA_HEADS_PER_STEP = 2


def _gla_masks(rev):
    row = lax.broadcasted_iota(jnp.int32, (GLA_ROWS, GLA_ROWS), 0)
    col = lax.broadcasted_iota(jnp.int32, (GLA_ROWS, GLA_ROWS), 1)
    shift = CHUNK.bit_length() - 1
    same = (jnp.right_shift(row, shift) == jnp.right_shift(col, shift)).astype(f32)
    lower, upper = (row >= col).astype(f32) * same, (row <= col).astype(f32) * same
    keep, keep_t = (upper, lower) if rev else (lower, upper)
    chunk_of = jnp.right_shift(lax.broadcasted_iota(jnp.int32, (GLA_ROWS, 1), 0), shift)
    return keep, keep.astype(bf16), keep_t.astype(bf16), [(chunk_of == c).astype(f32) for c in range(GLA_GROUP)]


def _gla_gates(hq, hf, lower):
    sg = _sigmoid(hf)
    f = lower + (1.0 - lower) * sg
    return hq * _sigmoid(hq), 1.0 - f, jnp.log(f), f, sg


def _gla_last_mid(b, rev):
    b3 = b.reshape(GLA_GROUP, CHUNK, 128)
    last, mid = (0, CHUNK // 2) if rev else (CHUNK - 1, CHUNK // 2 - 1)
    return b3[:, last:last + 1, :], b3[:, mid:mid + 1, :]


def _gla_per_row(per_chunk):
    return jnp.broadcast_to(per_chunk, (GLA_GROUP, CHUNK, 128)).reshape(GLA_ROWS, 128)


def _gla_block_diag(x, row_masks):
    return jnp.concatenate([(x * m).astype(bf16) for m in row_masks], axis=-1)


def _gla_diag(y):
    return jnp.concatenate([y[CHUNK * c:CHUNK * (c + 1), 128 * c:128 * (c + 1)] for c in range(GLA_GROUP)], axis=0)


def _gla_rows(n, n_groups, rev):
    ne = n_groups - 1 - n if rev else n
    return pl.ds(pl.multiple_of(ne * GLA_ROWS, GLA_ROWS), GLA_ROWS), ne * GLA_GROUP


def _gla_scan_order(rev):
    return tuple(reversed(range(GLA_GROUP))) if rev else tuple(range(GLA_GROUP))


def _fwd_gla(z, lb4):
    s = z.shape[0]
    n_groups = s // GLA_ROWS
    assert n_groups % 2 == 0
    hp = GLA_HEADS_PER_STEP
    chains = [(hh, rev) for hh in range(hp) for rev in (False, True)]

    def body(hq_ref, hff_ref, hfb_ref, hi_ref, lb_ref, o_ref, b_ref, states_ref, st_ref, stage_ref, b_stage, sems):
        st_ref[...] = jnp.zeros_like(st_ref)
        masks = {rev: _gla_masks(rev) for rev in (False, True)}
        lowers = [_sigmoid(lb_ref[int(rev):int(rev) + 1, 128 * hh:128 * (hh + 1)]
                           - lb_ref[2 + int(rev):3 + int(rev), 128 * hh:128 * (hh + 1)]) for hh, rev in chains]

        def states_out(slot, ci, chunk0):
            hh, rev = chains[ci]
            head = pl.program_id(0) * hp + hh
            rows = pl.ds(pl.multiple_of(chunk0 * CHUNK, GLA_ROWS), GLA_ROWS)
            return _Both(
                pltpu.make_async_copy(stage_ref.at[slot, ci], states_ref.at[head, int(rev), pl.ds(chunk0, GLA_GROUP)],
                                      sems.at[slot, ci]),
                pltpu.make_async_copy(b_stage.at[slot, ci], b_ref.at[int(rev), rows, pl.ds(pl.multiple_of(head * 128, 128), 128)],
                                      sems.at[slot, len(chains) + ci]))

        def make_step(first):
            def step(n, carry):
                slot = n % 2

                @pl.when(n >= 2)
                def _():
                    for ci in range(len(chains)):
                        states_out(slot, ci, 0).wait()

                for ci, (hh, rev) in enumerate(chains):
                    cols = slice(128 * hh, 128 * (hh + 1))
                    rows, chunk0 = _gla_rows(n, n_groups, rev)
                    maskf, tri, _, row_masks = masks[rev]
                    hf_ref = hfb_ref if rev else hff_ref
                    q, k, logf, _, _ = _gla_gates(hq_ref[rows, cols], hf_ref[rows, cols], lowers[ci])
                    vb = hi_ref[rows, cols].astype(bf16)
                    b = _tri_sum(tri, logf)
                    b_stage[slot, ci] = b
                    b_last3, b_mid3 = _gla_last_mid(b, rev)
                    b_last, b_mid = _gla_per_row(b_last3), _gla_per_row(b_mid3)
                    qi = (q * jnp.exp(b - b_mid)).astype(bf16)
                    ki = (k * jnp.exp(b_mid - b)).astype(bf16)
                    a = (_dot_nt(qi, ki) * maskf).astype(bf16)
                    kv = _dot_tn(vb, _gla_block_diag(k * jnp.exp(b_last - b), row_masks))
                    decay3 = jnp.exp(b_last3)
                    st = st_ref[ci]
                    before = [None] * GLA_GROUP
                    for c in _gla_scan_order(rev):
                        stage_ref[slot, ci, c] = st
                        before[c] = st.astype(bf16)
                        st = st * decay3[c] + kv[:, 128 * c:128 * (c + 1)]
                    st_ref[ci] = st
                    states_out(slot, ci, chunk0).start()
                    inter = _dot_nt((q * jnp.exp(b)).astype(bf16), jnp.concatenate(before, axis=0))
                    o = _dot(a, vb) + _gla_diag(inter)
                    if first:
                        o_ref[rows, cols] = o
                    else:
                        o_ref[rows, cols] += o
                return carry
            return step

        lax.fori_loop(0, n_groups // 2, make_step(True), 0)
        lax.fori_loop(n_groups // 2, n_groups, make_step(False), 0)
        for slot in range(2):
            for ci in range(len(chains)):
                states_out(slot, ci, 0).wait()

    w = 128 * hp
    col = lambda base: pl.BlockSpec((s, w), lambda h: (0, base // w + h))
    return pl.pallas_call(
        body, name="fwd_gla", grid=(HG_HEADS // hp,),
        in_specs=[col(Z_HQ), col(Z_HFF), col(Z_HFB), col(Z_HI), pl.BlockSpec((4, w), lambda h: (0, h))],
        out_specs=[pl.BlockSpec((s, w), lambda h: (0, h)), pl.BlockSpec(memory_space=pl.ANY), pl.BlockSpec(memory_space=pl.ANY)],
        out_shape=[jax.ShapeDtypeStruct((s, HG_HEADS * 128), f32), jax.ShapeDtypeStruct((2, s, HG_HEADS * 128), f32),
                   jax.ShapeDtypeStruct((HG_HEADS, 2, s // CHUNK, 128, 128), f32)],
        scratch_shapes=[pltpu.VMEM((len(chains), 128, 128), f32), pltpu.VMEM((2, len(chains), GLA_GROUP, 128, 128), f32),
                        pltpu.VMEM((2, len(chains), GLA_ROWS, 128), f32), pltpu.SemaphoreType.DMA((2, 2 * len(chains)))],
        compiler_params=_cp(("parallel",), VMEM_LIMIT),
    )(z, z, z, z, lb4)


def _hg_out(o, hg, g_hgo):
    outs, ons, rs = [], [], []
    for h in range(HG_HEADS):
        oh = o[:, 128 * h:128 * (h + 1)]
        on, r = _rms_fwd(oh, g_hgo[:, 128 * h:128 * (h + 1)], 128)
        ons.append(on)
        rs.append(r)
    on = jnp.concatenate(ons, axis=-1)
    sg = _sigmoid(hg)
    return on * (hg * sg), on, rs, sg


def _fwd_mix(a, o, z, g_hgo, x, w_o, tm):
    s, d = x.shape

    def body(a_ref, o_ref, hg_ref, g_ref, x_ref, w_ref, x2_ref, cat_ref):
        r, _, _, _ = _hg_out(o_ref[...], hg_ref[...], g_ref[...])
        cat = jnp.concatenate([a_ref[...], r.astype(bf16)], axis=-1)
        cat_ref[...] = cat
        x2_ref[...] = x_ref[...] + _dot(cat, w_ref[...])

    row512 = pl.BlockSpec((tm, 512), lambda i: (i, 0))
    rowd = pl.BlockSpec((tm, d), lambda i: (i, 0))
    return pl.pallas_call(
        body, name="fwd_mix", grid=(s // tm,),
        in_specs=[row512, row512, pl.BlockSpec((tm, 512), lambda i: (i, Z_HG // 512)), _const_spec((1, 512)), rowd,
                  _const_spec((d, d))],
        out_specs=[rowd, rowd],
        out_shape=[jax.ShapeDtypeStruct((s, d), f32), jax.ShapeDtypeStruct((s, d), bf16)],
        compiler_params=_cp(("parallel",), VMEM_LIMIT),
    )(a, o, z, g_hgo, x, w_o)


def _fwd_ffn(x2, g_ffn, w_gate, w_up, w_down, tm):
    s, d = x2.shape

    def body(x_ref, g_ref, wg_ref, wu_ref, wd_ref, x3_ref, gp_ref, up_ref):
        x = x_ref[...]
        h, _ = _rms_fwd(x, g_ref[...], d)
        hb = h.astype(bf16)
        gp = _dot_nt(hb, wg_ref[...])
        up = _dot_nt(hb, wu_ref[...])
        gp_ref[...] = gp.astype(bf16)
        up_ref[...] = up.astype(bf16)
        act = (gp * _sigmoid(gp) * up).astype(bf16)
        x3_ref[...] = x + _dot(act, wd_ref[...])

    rowd = pl.BlockSpec((tm, d), lambda i: (i, 0))
    rowf = pl.BlockSpec((tm, D_FF), lambda i: (i, 0))
    return pl.pallas_call(
        body, name="fwd_ffn", grid=(s // tm,),
        in_specs=[rowd, _const_spec((1, d)), _const_spec((D_FF, d)), _const_spec((D_FF, d)), _const_spec((D_FF, d))],
        out_specs=[rowd, rowf, rowf],
        out_shape=[jax.ShapeDtypeStruct((s, d), f32), jax.ShapeDtypeStruct((s, D_FF), bf16),
                   jax.ShapeDtypeStruct((s, D_FF), bf16)],
        compiler_params=_cp(("parallel",), VMEM_LIMIT),
    )(x2, g_ffn, w_gate, w_up, w_down)


def _ple_loss_fwd_bwd(x3, g_ple, w_pg, p, w_pp, target, tm):
    s, d = x3.shape

    def body(x_ref, g_ref, wg_ref, p_ref, wp_ref, t_ref, dx_ref, h_ref, dpre_ref, dpp_ref, dg_ref, loss_ref):
        @pl.when(pl.program_id(0) == 0)
        def _():
            dg_ref[...] = jnp.zeros_like(dg_ref)
            loss_ref[...] = jnp.zeros_like(loss_ref)

        x = x_ref[...]
        g = g_ref[...]
        h, r = _rms_fwd(x, g, d)
        hb = h.astype(bf16)
        gate = _sigmoid(_dot(hb, wg_ref[...]))
        pp = _dot(p_ref[...].astype(bf16), wp_ref[...])
        e = x + gate * pp - t_ref[...]
        loss_ref[...] += 0.5 * jnp.sum(e * e) * (1.0 / d)
        dy = e * (1.0 / d)
        dpre = (dy * pp * gate * (1.0 - gate)).astype(bf16)
        dx, dgx = _rms_bwd(_dot_nt(dpre, wg_ref[...]), x, r, g, d)
        dx_ref[...] = dy + dx
        dg_ref[...] += jnp.sum(dgx, axis=0, keepdims=True)
        h_ref[...] = hb
        dpre_ref[...] = dpre
        dpp_ref[...] = (dy * gate).astype(bf16)

    rowd = pl.BlockSpec((tm, d), lambda i: (i, 0))
    return pl.pallas_call(
        body, name="ple_loss_fwd_bwd", grid=(s // tm,),
        in_specs=[rowd, _const_spec((1, d)), _const_spec((d, d)), pl.BlockSpec((tm, PLE_DIM), lambda i: (i, 0)),
                  _const_spec((PLE_DIM, d)), rowd],
        out_specs=[rowd, rowd, rowd, rowd, _acc_spec((1, d)), _acc_spec((8, 128))],
        out_shape=[jax.ShapeDtypeStruct((s, d), f32), jax.ShapeDtypeStruct((s, d), bf16), jax.ShapeDtypeStruct((s, d), bf16),
                   jax.ShapeDtypeStruct((s, d), bf16), jax.ShapeDtypeStruct((1, d), f32), jax.ShapeDtypeStruct((8, 128), f32)],
        compiler_params=_cp(("arbitrary",), VMEM_LIMIT),
    )(x3, g_ple, w_pg, p, w_pp, target)


def _bwd_ffn_hidden(d3, gp, up, w_down, tm, tf):
    s, d = d3.shape

    def body(d3_ref, gp_ref, up_ref, wd_ref, act_ref, dgp_ref, dup_ref):
        gp, up = gp_ref[...].astype(f32), up_ref[...].astype(f32)
        sg = _sigmoid(gp)
        silu = gp * sg
        act_ref[...] = (silu * up).astype(bf16)
        dact = _dot_nt(d3_ref[...].astype(bf16), wd_ref[...])
        dgp_ref[...] = (dact * up * (sg * (1.0 + gp * (1.0 - sg)))).astype(bf16)
        dup_ref[...] = (dact * silu).astype(bf16)

    rowf = pl.BlockSpec((tm, tf), lambda f, i: (i, f))
    return pl.pallas_call(
        body, name="bwd_ffn_hidden", grid=(D_FF // tf, s // tm),
        in_specs=[pl.BlockSpec((tm, d), lambda f, i: (i, 0)), rowf, rowf, pl.BlockSpec((tf, d), lambda f, i: (f, 0))],
        out_specs=[rowf, rowf, rowf],
        out_shape=[jax.ShapeDtypeStruct((s, D_FF), bf16)] * 3,
        compiler_params=_cp(("parallel", "parallel"), VMEM_LIMIT),
    )(d3, gp, up, w_down)


def _bwd_ffn_in(d3, x2, dgp, dup, g_ffn, w_gate, w_up, tm):
    s, d = x2.shape

    def body(d3_ref, x_ref, dgp_ref, dup_ref, g_ref, wg_ref, wu_ref, d2_ref, h_ref, dg_ref):
        @pl.when(pl.program_id(0) == 0)
        def _():
            dg_ref[...] = jnp.zeros_like(dg_ref)

        x, g = x_ref[...], g_ref[...]
        dh = _dot(dgp_ref[...], wg_ref[...]) + _dot(dup_ref[...], wu_ref[...])
        h, r = _rms_fwd(x, g, d)
        h_ref[...] = h.astype(bf16)
        dx, dgx = _rms_bwd(dh, x, r, g, d)
        d2_ref[...] = d3_ref[...] + dx
        dg_ref[...] += jnp.sum(dgx, axis=0, keepdims=True)

    rowd = pl.BlockSpec((tm, d), lambda i: (i, 0))
    rowf = pl.BlockSpec((tm, D_FF), lambda i: (i, 0))
    return pl.pallas_call(
        body, name="bwd_ffn_in", grid=(s // tm,),
        in_specs=[rowd, rowd, rowf, rowf, _const_spec((1, d)), _const_spec((D_FF, d)), _const_spec((D_FF, d))],
        out_specs=[rowd, rowd, _acc_spec((1, d))],
        out_shape=[jax.ShapeDtypeStruct((s, d), f32), jax.ShapeDtypeStruct((s, d), bf16), jax.ShapeDtypeStruct((1, d), f32)],
        compiler_params=_cp(("arbitrary",), VMEM_LIMIT),
    )(d3, x2, dgp, dup, g_ffn, w_gate, w_up)


def _bwd_mix(d2, w_o, o, z, g_hgo, tm):
    s, d = d2.shape

    def body(d2_ref, w_ref, o_ref, hg_ref, g_ref, da_ref, do_ref, dhg_ref, dg_ref):
        @pl.when(pl.program_id(0) == 0)
        def _():
            dg_ref[...] = jnp.zeros_like(dg_ref)

        dcat = _dot_nt(d2_ref[...].astype(bf16), w_ref[...])
        da_ref[...] = dcat[:, 0:512].astype(bf16)
        dr = dcat[:, 512:1024]
        o, hg, g = o_ref[...], hg_ref[...], g_ref[...]
        _, on, rs, sg = _hg_out(o, hg, g)
        dhg_ref[...] = (dr * on * (sg * (1.0 + hg * (1.0 - sg)))).astype(bf16)
        don = dr * (hg * sg)
        dgs = []
        for h in range(HG_HEADS):
            cols = slice(128 * h, 128 * (h + 1))
            dx, dgx = _rms_bwd(don[:, cols], o[:, cols], rs[h], g[:, cols], 128)
            do_ref[:, cols] = dx
            dgs.append(jnp.sum(dgx, axis=0, keepdims=True))
        dg_ref[...] += jnp.concatenate(dgs, axis=-1)

    row512 = pl.BlockSpec((tm, 512), lambda i: (i, 0))
    return pl.pallas_call(
        body, name="bwd_mix", grid=(s // tm,),
        in_specs=[pl.BlockSpec((tm, d), lambda i: (i, 0)), _const_spec((d, d)), row512,
                  pl.BlockSpec((tm, 512), lambda i: (i, Z_HG // 512)), _const_spec((1, 512))],
        out_specs=[row512, row512, row512, _acc_spec((1, 512))],
        out_shape=[jax.ShapeDtypeStruct((s, 512), bf16), jax.ShapeDtypeStruct((s, 512), f32), jax.ShapeDtypeStruct((s, 512), bf16),
                   jax.ShapeDtypeStruct((1, 512), f32)],
        compiler_params=_cp(("arbitrary",), VMEM_LIMIT),
    )(d2, w_o, o, z, g_hgo)


def _bwd_gla(z, lb4, do, b_fwd, states):
    s = z.shape[0]
    n_chunks = s // CHUNK
    n_groups = s // GLA_ROWS
    assert n_groups % 2 == 0

    def body(hq_ref, hff_ref, hfb_ref, hi_ref, lb_ref, do_ref, b_all, st_all, dhq_ref, dhff_ref, dhfb_ref, dhi_ref, dlb_ref,
             dst_ref, dq_acc, dv_acc, dlow_ref):
        dirs = (False, True)
        masks = [_gla_masks(rev) for rev in dirs]
        lowers = [_sigmoid(lb_ref[int(rev):int(rev) + 1, :] - lb_ref[2 + int(rev):3 + int(rev), :]) for rev in dirs]
        hf_refs, dhf_refs = (hff_ref, hfb_ref), (dhff_ref, dhfb_ref)

        dst_ref[...] = jnp.zeros_like(dst_ref)
        dlow_ref[...] = jnp.zeros_like(dlow_ref)

        def make_bwd_step(first):
            def bwd_step(j, carry):
                n = n_groups - 1 - j
                for d, rev in enumerate(dirs):
                    maskf, _, tri_t, row_masks = masks[d]
                    lower = lowers[d]
                    rows, chunk0 = _gla_rows(n, n_groups, rev)
                    hq, hf = hq_ref[rows, :], hf_refs[d][rows, :]
                    q, k, _, f, sg = _gla_gates(hq, hf, lower)
                    v = hi_ref[rows, :]
                    dout = do_ref[rows, :]
                    b = b_all[d, rows, :]
                    b_last3, b_mid3 = _gla_last_mid(b, rev)
                    b_last, b_mid = _gla_per_row(b_last3), _gla_per_row(b_mid3)
                    e1, e2, e3, e4 = jnp.exp(b - b_mid), jnp.exp(b_mid - b), jnp.exp(b_last - b), jnp.exp(b)
                    decay3 = jnp.exp(b_last3)
                    qi, ki, kt, qt = q * e1, k * e2, k * e3, q * e4
                    qib, kib, ktb = qi.astype(bf16), ki.astype(bf16), kt.astype(bf16)
                    vb, dob = v.astype(bf16), dout.astype(bf16)
                    a = (_dot_nt(qib, kib) * maskf).astype(bf16)
                    da = (_dot_nt(dob, vb) * maskf).astype(bf16)
                    dqi = _dot(da, kib)
                    dki = _dot_tn(da, qib)
                    into_state = _dot_tn(dob, _gla_block_diag(qt, row_masks))
                    dst = dst_ref[d]
                    sts, dsts, ddecay = [None] * GLA_GROUP, [None] * GLA_GROUP, [None] * GLA_GROUP
                    for c in reversed(_gla_scan_order(rev)):
                        sts[c] = st_all[d, chunk0 + c]
                        dsts[c] = dst.astype(bf16)
                        ddecay[c] = jnp.sum(dst * sts[c], axis=0, keepdims=True)[None]
                        dst = dst * decay3[c] + into_state[:, 128 * c:128 * (c + 1)]
                    dst_ref[d] = dst
                    dv = _dot_tn(a, dob) + _gla_diag(_dot_nt(ktb, jnp.concatenate(dsts, axis=0)))
                    dqt = _gla_diag(_dot(dob, jnp.concatenate([x.astype(bf16) for x in sts], axis=-1)))
                    dkt = _gla_diag(_dot(vb, jnp.concatenate(dsts, axis=-1)))
                    dq = dqi * e1 + dqt * e4
                    dk = dki * e2 + dkt * e3
                    db = dqi * qi - dki * ki + dqt * qt - dkt * kt
                    dlast3 = (jnp.sum((dkt * kt).reshape(GLA_GROUP, CHUNK, 128), axis=1, keepdims=True)
                              + jnp.concatenate(ddecay, axis=0) * decay3)
                    dlogf = _tri_sum(tri_t, db) + _gla_per_row(dlast3)
                    df = dlogf / f - dk
                    dhf_refs[d][rows, :] = (df * (1.0 - lower) * sg * (1.0 - sg)).astype(bf16)
                    dlow_ref[d:d + 1, :] += jnp.sum(df * (1.0 - sg), axis=0, keepdims=True)
                    sq = _sigmoid(hq)
                    dhq = dq * (sq * (1.0 + hq * (1.0 - sq)))
                    if first:
                        dq_acc[rows, :] = dhq
                        dv_acc[rows, :] = dv
                    else:
                        dhq_ref[rows, :] = (dq_acc[rows, :] + dhq).astype(bf16)
                        dhi_ref[rows, :] = (dv_acc[rows, :] + dv).astype(bf16)
                return carry
            return bwd_step

        lax.fori_loop(0, n_groups // 2, make_bwd_step(True), 0, unroll=2)
        lax.fori_loop(n_groups // 2, n_groups, make_bwd_step(False), 0, unroll=2)

        for d in range(2):
            dl = dlow_ref[d:d + 1, :] * lowers[d] * (1.0 - lowers[d])
            dlb_ref[d:d + 1, :] = dl
            dlb_ref[2 + d:3 + d, :] = -dl

    col = lambda base: pl.BlockSpec((s, 128), lambda h: (0, base // 128 + h))
    return pl.pallas_call(
        body, name="bwd_gla", grid=(HG_HEADS,),
        in_specs=[col(Z_HQ), col(Z_HFF), col(Z_HFB), col(Z_HI), pl.BlockSpec((4, 128), lambda h: (0, h)), col(0),
                  pl.BlockSpec((2, s, 128), lambda h: (0, 0, h)),
                  pl.BlockSpec((None, 2, n_chunks, 128, 128), lambda h: (h, 0, 0, 0, 0), pipeline_mode=pl.Buffered(1))],
        out_specs=[col(0), col(0), col(0), col(0), pl.BlockSpec((4, 128), lambda h: (0, h))],
        out_shape=[jax.ShapeDtypeStruct((s, 512), bf16)] * 4 + [jax.ShapeDtypeStruct((4, 512), f32)],
        scratch_shapes=[pltpu.VMEM((2, 128, 128), f32), pltpu.VMEM((s, 128), f32), pltpu.VMEM((s, 128), f32),
                        pltpu.VMEM((2, 128), f32)],
        compiler_params=_cp(("parallel",), VMEM_LIMIT),
    )(z, z, z, z, lb4, do, b_fwd, states)


def _bwd_attn(q, k, v, da, a32, tq):
    hh, s, _ = q.shape

    n_sub = max(1, tq // ATTN_SUB_ROWS)

    def body(q_ref, k_ref, v_ref, do_ref, o_ref, dq_ref, dk_ref, dv_ref, p_all, ds_all, dol_ref, dkt_ref, dvt_ref):
        @pl.when(pl.program_id(1) == 0)
        def _():
            dkt_ref[...] = jnp.zeros_like(dkt_ref)
            dvt_ref[...] = jnp.zeros_like(dvt_ref)

        kb, vb = k_ref[...], v_ref[...]
        for t in range(n_sub):
            rows = slice(t * (tq // n_sub), (t + 1) * (tq // n_sub))
            sc = _dot_nt(q_ref[rows, :], kb)
            p = jnp.exp2((sc - jnp.max(sc, axis=-1, keepdims=True)) * (ATTN_SCALE * LOG2_E))
            inv_l = 1.0 / jnp.sum(p, axis=-1, keepdims=True)
            pb = p.astype(bf16)
            dob = do_ref[rows, :]
            dof = dob.astype(f32)
            delta = jnp.sum(dof * o_ref[rows, :], axis=-1, keepdims=True)
            ds = pb * ((_dot_nt(dob, vb) - delta) * inv_l).astype(bf16)
            dq_ref[rows, :] = _dot(ds, kb) * ATTN_SCALE
            p_all[rows, :] = pb
            ds_all[rows, :] = ds
            dol_ref[rows, :] = (dof * inv_l).astype(bf16)
        dkt_ref[...] += _dot_tn(q_ref[...], ds_all[...])
        dvt_ref[...] += _dot_tn(dol_ref[...], p_all[...])

        @pl.when(pl.program_id(1) == s // tq - 1)
        def _():
            dk_ref[...] = dkt_ref[...].T * ATTN_SCALE
            dv_ref[...] = dvt_ref[...].T

    return pl.pallas_call(
        body, name="bwd_attn", grid=(hh, s // tq),
        in_specs=[pl.BlockSpec((None, tq, QK_PAD), lambda h, i: (h, i, 0)),
                  pl.BlockSpec((None, s, QK_PAD), lambda h, i: (h, 0, 0)),
                  pl.BlockSpec((None, s, V_HEAD), lambda h, i: (h, 0, 0)),
                  pl.BlockSpec((tq, V_HEAD), lambda h, i: (i, h)), pl.BlockSpec((tq, V_HEAD), lambda h, i: (i, h))],
        out_specs=[pl.BlockSpec((None, tq, QK_PAD), lambda h, i: (h, i, 0)),
                   pl.BlockSpec((None, s, QK_PAD), lambda h, i: (h, 0, 0)),
                   pl.BlockSpec((None, s, V_HEAD), lambda h, i: (h, 0, 0))],
        out_shape=[jax.ShapeDtypeStruct((hh, s, QK_PAD), f32), jax.ShapeDtypeStruct((hh, s, QK_PAD), f32),
                   jax.ShapeDtypeStruct((hh, s, V_HEAD), f32)],
        scratch_shapes=[pltpu.VMEM((tq, s), bf16), pltpu.VMEM((tq, s), bf16), pltpu.VMEM((tq, V_HEAD), bf16),
                        pltpu.VMEM((QK_PAD, s), f32), pltpu.VMEM((V_HEAD, s), f32)],
        compiler_params=_cp(("parallel", "arbitrary"), VMEM_LIMIT),
    )(q, k, v, da, a32)


def _bwd_mla_proj(z, dq, dk, dv, cosb, sina, sinb, g_qa, g_kva, wqb, wkvb, g_qn, g_kn, tm):
    s = z.shape[0]
    hh = MLA_HEADS

    def body(cq_ref, ckv_ref, kr_ref, dq_ref, dk_ref, dv_ref, c_ref, sa_ref, sb_ref, gqa_ref, gkva_ref, wqb_ref, wkvb_ref,
             gqn_ref, gkn_ref, dz_ref, cqn_ref, ckvn_ref, dq0_ref, dkv0_ref, dgqa_ref, dgkva_ref, dgqn_ref, dgkn_ref):
        @pl.when(pl.program_id(0) == 0)
        def _():
            for r in (dgqa_ref, dgkva_ref, dgqn_ref, dgkn_ref):
                r[...] = jnp.zeros_like(r)

        cq, ckv, kr = cq_ref[...], ckv_ref[...], kr_ref[...]
        gqa, gkva, gqn, gkn = gqa_ref[...], gkva_ref[...], gqn_ref[...], gkn_ref[...]
        cqn_b, rq, ckvn_b, rkv, q0, kv0 = _mla_qk_fwd(cq, ckv, gqa, gkva, wqb_ref[...], wkvb_ref[...])
        cqn_ref[...] = cqn_b
        ckvn_ref[...] = ckvn_b
        c, sa, sb = c_ref[...], -sa_ref[...], -sb_ref[...]
        kr_sq = jnp.sum(kr * kr, axis=-1, keepdims=True)
        dkr = jnp.zeros_like(kr)
        dgqn = jnp.zeros((1, QK_PAD), f32)
        dgkn = jnp.zeros((1, QK_PAD), f32)
        for h in range(hh):
            qh = q0[:, QK_PAD * h:QK_PAD * (h + 1)]
            rh = lax.rsqrt(jnp.sum(qh * qh, axis=-1, keepdims=True) * (1.0 / QK_HEAD) + EPS)
            dqh = dq_ref[h]
            dqn = jnp.concatenate([dqh[:, 0:128], _rope(dqh[:, 128:256], c, sa, sb)], axis=-1)
            dq0h, dgx = _rms_bwd(dqn, qh, rh, gqn, QK_HEAD)
            dq0_ref[:, QK_PAD * h:QK_PAD * (h + 1)] = dq0h.astype(bf16)
            dgqn = dgqn + jnp.sum(dgx, axis=0, keepdims=True)

            kn_ = kv0[:, 256 * h:256 * h + 128]
            k0 = jnp.concatenate([kn_, kr], axis=-1)
            rk = lax.rsqrt((jnp.sum(kn_ * kn_, axis=-1, keepdims=True) + kr_sq) * (1.0 / QK_HEAD) + EPS)
            dkh = dk_ref[h]
            dkn = jnp.concatenate([dkh[:, 0:128], _rope(dkh[:, 128:256], c, sa, sb)], axis=-1)
            dk0, dgx = _rms_bwd(dkn, k0, rk, gkn, QK_HEAD)
            dgkn = dgkn + jnp.sum(dgx, axis=0, keepdims=True)
            dkv0_ref[:, 256 * h:256 * h + 128] = dk0[:, 0:128].astype(bf16)
            dkv0_ref[:, 256 * h + 128:256 * h + 256] = dv_ref[h].astype(bf16)
            dkr = dkr + dk0[:, 128:256]
        dgqn_ref[...] += dgqn
        dgkn_ref[...] += dgkn
        dcq, dgx = _rms_bwd(_dot_nt(dq0_ref[...], wqb_ref[...]), cq, rq, gqa, Q_LORA)
        dgqa_ref[...] += jnp.sum(dgx, axis=0, keepdims=True)
        dckv, dgx = _rms_bwd(_dot_nt(dkv0_ref[...], wkvb_ref[...]), ckv, rkv, gkva, KV_LORA)
        dgkva_ref[...] += jnp.sum(dgx, axis=0, keepdims=True)
        dz_ref[:, 0:256] = dcq.astype(bf16)
        dz_ref[:, 256:512] = dckv.astype(bf16)
        dz_ref[:, 512:640] = dkr.astype(bf16)

    row128 = pl.BlockSpec((tm, 128), lambda i: (i, 0))
    row256 = pl.BlockSpec((tm, 256), lambda i: (i, 0))
    row1024 = pl.BlockSpec((tm, 1024), lambda i: (i, 0))
    hd = lambda w: pl.BlockSpec((hh, tm, w), lambda i: (0, i, 0))
    return pl.pallas_call(
        body, name="bwd_mla_proj", grid=(s // tm,),
        in_specs=[pl.BlockSpec((tm, 256), lambda i: (i, Z_CQ // 256)), pl.BlockSpec((tm, 256), lambda i: (i, Z_CKV // 256)),
                  pl.BlockSpec((tm, 128), lambda i: (i, Z_KR // 128)), hd(QK_PAD), hd(QK_PAD), hd(V_HEAD),
                  row128, row128, row128,
                  _const_spec((1, 256)), _const_spec((1, 256)), _const_spec((256, 1024)), _const_spec((256, 1024)),
                  _const_spec((1, 256)), _const_spec((1, 256))],
        out_specs=[pl.BlockSpec((tm, 640), lambda i: (i, 0)), row256, row256, row1024, row1024,
                   _acc_spec((1, 256)), _acc_spec((1, 256)), _acc_spec((1, 256)), _acc_spec((1, 256))],
        out_shape=[jax.ShapeDtypeStruct((s, 640), bf16), jax.ShapeDtypeStruct((s, 256), bf16), jax.ShapeDtypeStruct((s, 256), bf16),
                   jax.ShapeDtypeStruct((s, 1024), bf16), jax.ShapeDtypeStruct((s, 1024), bf16)]
        + [jax.ShapeDtypeStruct((1, 256), f32)] * 4,
        compiler_params=_cp(("arbitrary",), VMEM_LIMIT),
    )(z, z, z, dq, dk, dv, cosb, sina, sinb, g_qa, g_kva, wqb, wkvb, g_qn, g_kn)


def _bwd_in(segments, wz, x, g_mix, d2, tm):
    s, d = x.shape
    n_seg = len(segments)

    def body(*refs):
        dz_refs, w_refs = refs[:n_seg], refs[n_seg:2 * n_seg]
        x_ref, g_ref, d2_ref, gx_ref, dg_ref = refs[2 * n_seg:]

        @pl.when(pl.program_id(0) == 0)
        def _():
            dg_ref[...] = jnp.zeros_like(dg_ref)

        dh = _dot_nt(dz_refs[0][...], w_refs[0][...])
        for a_ref, w_ref in zip(dz_refs[1:], w_refs[1:]):
            dh = dh + _dot_nt(a_ref[...], w_ref[...])
        x, g = x_ref[...], g_ref[...]
        r = lax.rsqrt(jnp.sum(x * x, axis=-1, keepdims=True) * (1.0 / d) + EPS)
        dx, dgx = _rms_bwd(dh, x, r, g, d)
        gx_ref[...] = d2_ref[...] + dx
        dg_ref[...] += jnp.sum(dgx, axis=0, keepdims=True)

    rowd = pl.BlockSpec((tm, d), lambda i: (i, 0))
    dz_specs = [pl.BlockSpec((tm, w), functools.partial(lambda i, j: (i, j), j=ja)) for _, w, ja, _ in segments]
    w_specs = [pl.BlockSpec((d, w), functools.partial(lambda i, j: (0, j), j=jw), pipeline_mode=pl.Buffered(1))
               for _, w, _, jw in segments]
    return pl.pallas_call(
        body, name="bwd_in", grid=(s // tm,),
        in_specs=dz_specs + w_specs + [rowd, _const_spec((1, d)), rowd],
        out_specs=[rowd, _acc_spec((1, d))],
        out_shape=[jax.ShapeDtypeStruct((s, d), f32), jax.ShapeDtypeStruct((1, d), f32)],
        compiler_params=_cp(("arbitrary",), VMEM_LIMIT),
    )(*[a for a, _, _, _ in segments], *([wz] * n_seg), x, g_mix, d2)


def _pick_tile(n, cap):
    best = None
    for t in range(LANES, cap + 1, LANES):
        if n % t == 0:
            best = t
    return best if best is not None else n


def _mm_tn_many(a, bs, name, tm, transposed=False):
    kk, m = a.shape
    n_b = len(bs)
    tk = min(1024, kk)
    n_k = kk // tk

    def body(a_ref, *refs):
        b_refs, o_refs, acc_refs = refs[:n_b], refs[n_b:2 * n_b], refs[2 * n_b:]

        @pl.when(pl.program_id(1) == 0)
        def _():
            for acc in acc_refs:
                acc[...] = jnp.zeros_like(acc)
        a_blk = a_ref[...].astype(bf16)
        for b_ref, acc in zip(b_refs, acc_refs):
            acc[...] += _dot_tn(a_blk, b_ref[...].astype(bf16))

        @pl.when(pl.program_id(1) == n_k - 1)
        def _():
            for o_ref, acc in zip(o_refs, acc_refs):
                o_ref[...] = (acc[...].T if transposed else acc[...]).astype(bf16)

    if transposed:
        out_specs = [pl.BlockSpec((b.shape[1], tm), lambda i, k: (0, i)) for b in bs]
        out_shape = [jax.ShapeDtypeStruct((b.shape[1], m), bf16) for b in bs]
    else:
        out_specs = [pl.BlockSpec((tm, b.shape[1]), lambda i, k: (i, 0)) for b in bs]
        out_shape = [jax.ShapeDtypeStruct((m, b.shape[1]), bf16) for b in bs]
    return pl.pallas_call(
        body, name=name, grid=(m // tm, n_k),
        in_specs=[pl.BlockSpec((tk, tm), lambda i, k: (k, i))] + [pl.BlockSpec((tk, b.shape[1]), lambda i, k: (k, 0)) for b in bs],
        out_specs=out_specs,
        out_shape=out_shape,
        scratch_shapes=[pltpu.VMEM((tm, b.shape[1]), f32) for b in bs],
        compiler_params=_cp(("parallel", "arbitrary"), VMEM_LIMIT),
    )(a, *bs)


def _mm_tn(a, b, name):
    kk, m = a.shape
    _, n = b.shape
    tm = _pick_tile(m, 1408)
    tn = _pick_tile(n, 1408)
    tk = min(1024, kk)

    n_k = kk // tk

    def body(a_ref, b_ref, o_ref, acc_ref):
        @pl.when(pl.program_id(2) == 0)
        def _():
            acc_ref[...] = jnp.zeros_like(acc_ref)
        acc_ref[...] += _dot_tn(a_ref[...].astype(bf16), b_ref[...].astype(bf16))

        @pl.when(pl.program_id(2) == n_k - 1)
        def _():
            o_ref[...] = acc_ref[...].astype(bf16)

    return pl.pallas_call(
        body, name=name, grid=(m // tm, n // tn, n_k),
        in_specs=[pl.BlockSpec((tk, tm), lambda i, j, k: (k, i)), pl.BlockSpec((tk, tn), lambda i, j, k: (k, j))],
        out_specs=pl.BlockSpec((tm, tn), lambda i, j, k: (i, j)),
        out_shape=jax.ShapeDtypeStruct((m, n), bf16),
        scratch_shapes=[pltpu.VMEM((tm, tn), f32)],
        compiler_params=_cp(("parallel", "parallel", "arbitrary"), VMEM_LIMIT),
    )(a, b)


def _rope_tables(positions):
    inv_freq = ROPE_THETA ** (-jnp.arange(0, QK_ROPE, 2, dtype=f32) / QK_ROPE)
    ang = positions.astype(f32)[:, None] * inv_freq
    cos, sin = jnp.cos(ang), jnp.sin(ang)
    zero = jnp.zeros_like(cos)
    return (jnp.concatenate([cos, cos, zero, zero], axis=1), jnp.concatenate([zero, sin, zero, zero], axis=1),
            jnp.concatenate([-sin, zero, zero, zero], axis=1))


def _pad256(g):
    return jnp.pad(g.reshape(1, QK_HEAD), ((0, 0), (0, QK_PAD - QK_HEAD)))


RELAYOUT_BLOCKS = 8
FIRST = ("w_in", "w_qb", "w_kvb", "lb_param")
SECOND = ("w_o", "w_gate", "w_up", "w_down", "w_ple_gate", "w_ple_proj")
ROW_SHARDED = ("w_o", "w_down", "w_ple_gate")


def _col_moves(j):
    lo = BIG["w_in"][1] * j
    w_in = [(max(lo, a) - lo, min(lo + BIG["w_in"][1], b) - lo, d + max(lo, a) - a)
            for a, b, d in Z_SEGMENTS if max(lo, a) < min(lo + BIG["w_in"][1], b)]
    head, half = divmod(j, 2)
    whole = lambda n: [(0, BIG[n][1], BIG[n][1] * j)]
    return {"w_in": w_in, "w_qb": [(0, 96, QK_PAD * head + 96 * half)], "w_kvb": whole("w_kvb"),
            "w_ple_proj": whole("w_ple_proj"), "lb_param": whole("lb_param")}


def _kernel_width(name):
    return {"w_in": Z_W, "w_qb": MLA_HEADS * QK_PAD}.get(name, N_DEV * BIG[name][1])


def _relayout_specs(names, by_dev):
    specs = []
    for n in names:
        rows, cols = BIG[n]
        if n == "lb_param":
            specs.append(_acc_spec((N_DEV, rows, cols) if by_dev else (rows, _kernel_width(n))))
        elif by_dev:
            specs.append(pl.BlockSpec((N_DEV, rows // RELAYOUT_BLOCKS, cols), lambda i: (0, i, 0)))
        else:
            specs.append(pl.BlockSpec((rows // RELAYOUT_BLOCKS, _kernel_width(n)), lambda i: (i, 0)))
    return specs


def _weights_in(gathered, names, name):
    n = len(names)

    def body(*refs):
        ins, outs = dict(zip(names, refs[:n])), dict(zip(names, refs[n:]))
        if "w_in" in outs:
            outs["w_in"][:, Z_KR + QK_ROPE:Z_W] = jnp.zeros((outs["w_in"].shape[0], Z_W - Z_KR - QK_ROPE), bf16)
        if "w_qb" in outs:
            for h in range(MLA_HEADS):
                outs["w_qb"][:, QK_PAD * h + QK_HEAD:QK_PAD * (h + 1)] = jnp.zeros((outs["w_qb"].shape[0], QK_PAD - QK_HEAD), bf16)
        for j in range(N_DEV):
            for wn, moves in _col_moves(j).items():
                if wn in outs:
                    for s0, s1, d0 in moves:
                        outs[wn][:, d0:d0 + s1 - s0] = ins[wn][j, :, s0:s1]

    outs = pl.pallas_call(
        body, name=name, grid=(RELAYOUT_BLOCKS,), in_specs=_relayout_specs(names, True), out_specs=_relayout_specs(names, False),
        out_shape=[jax.ShapeDtypeStruct((BIG[wn][0], _kernel_width(wn)), gathered[wn].dtype) for wn in names],
        compiler_params=_cp(("arbitrary",), VMEM_LIMIT),
    )(*[gathered[wn] for wn in names])
    return dict(zip(names, outs))


def _grads_out(sources, names, name):
    pieces = [(wn, start, arr) for wn in names for start, arr in sources[wn]]
    n_in = len(pieces)

    def body(*refs):
        outs = dict(zip(names, refs[n_in:]))

        def cols(wn, c0, c1):
            for (pn, start, arr), ref in zip(pieces, refs[:n_in]):
                if pn == wn and start <= c0 and c1 <= start + arr.shape[1]:
                    return ref[:, c0 - start:c1 - start]

        for j in range(N_DEV):
            for wn, moves in _col_moves(j).items():
                if wn in outs:
                    for s0, s1, d0 in moves:
                        outs[wn][j, :, s0:s1] = cols(wn, d0, d0 + s1 - s0).astype(bf16)

    in_specs = [_acc_spec(arr.shape) if wn == "lb_param" else pl.BlockSpec((arr.shape[0] // RELAYOUT_BLOCKS, arr.shape[1]), lambda i: (i, 0))
                for wn, _, arr in pieces]
    outs = pl.pallas_call(
        body, name=name, grid=(RELAYOUT_BLOCKS,), in_specs=in_specs, out_specs=_relayout_specs(names, True),
        out_shape=[jax.ShapeDtypeStruct((N_DEV, *BIG[wn]), bf16) for wn in names],
        compiler_params=_cp(("arbitrary",), VMEM_LIMIT),
    )(*[arr for _, _, arr in pieces])
    return dict(zip(names, outs))


def kernel(x, p, positions, g_mix, w_in, g_qa, g_kva, w_qb, w_kvb, g_qn, g_kn, lb_param, g_hgo, w_o, g_ffn, w_gate, w_up, w_down, g_ple, w_ple_gate, w_ple_proj, loss_target, m_g_mix, m_w_in, m_g_qa, m_g_kva, m_w_qb, m_w_kvb, m_g_qn, m_g_kn, m_lb_param, m_g_hgo, m_w_o, m_g_ffn, m_w_gate, m_w_up, m_w_down, m_g_ple, m_w_ple_gate, m_w_ple_proj, v_g_mix, v_w_in, v_g_qa, v_g_kva, v_w_qb, v_w_kvb, v_g_qn, v_g_kn, v_lb_param, v_g_hgo, v_w_o, v_g_ffn, v_w_gate, v_w_up, v_w_down, v_g_ple, v_w_ple_gate, v_w_ple_proj):
    w_all = dict(g_mix=g_mix, g_qa=g_qa, g_kva=g_kva, g_qn=g_qn, g_kn=g_kn, g_hgo=g_hgo, g_ffn=g_ffn, g_ple=g_ple,
                 w_in=w_in, w_qb=w_qb, w_kvb=w_kvb, w_o=w_o, w_gate=w_gate, w_up=w_up, w_down=w_down,
                 w_ple_gate=w_ple_gate, w_ple_proj=w_ple_proj, lb_param=lb_param)
    m_all = dict(g_mix=m_g_mix, g_qa=m_g_qa, g_kva=m_g_kva, g_qn=m_g_qn, g_kn=m_g_kn, g_hgo=m_g_hgo, g_ffn=m_g_ffn,
                 g_ple=m_g_ple, w_in=m_w_in, w_qb=m_w_qb, w_kvb=m_w_kvb, w_o=m_w_o, w_gate=m_w_gate, w_up=m_w_up,
                 w_down=m_w_down, w_ple_gate=m_w_ple_gate, w_ple_proj=m_w_ple_proj, lb_param=m_lb_param)
    v_all = dict(g_mix=v_g_mix, g_qa=v_g_qa, g_kva=v_g_kva, g_qn=v_g_qn, g_kn=v_g_kn, g_hgo=v_g_hgo, g_ffn=v_g_ffn,
                 g_ple=v_g_ple, w_in=v_w_in, w_qb=v_w_qb, w_kvb=v_w_kvb, w_o=v_w_o, w_gate=v_w_gate, w_up=v_w_up,
                 w_down=v_w_down, w_ple_gate=v_w_ple_gate, w_ple_proj=v_w_ple_proj, lb_param=v_lb_param)
    me_idx = jnp.stack([_me()]).astype(jnp.int32)
    x, p, positions, target = x[0], p[0, 0], positions[0], loss_target[0]
    s = x.shape[0]
    tm, tm_ffn, tq_f, tq_b = min(512, s), min(1024, s), min(2048, s), min(1024, s)
    g_mix, g_qa, g_kva, g_qn, g_kn, g_hgo, g_ffn, g_ple = (w_all[n].reshape(1, -1) for n in SMALL)
    g_qn_p, g_kn_p = _pad256(g_qn), _pad256(g_kn)
    cosb, sina, sinb = _rope_tables(positions)
    as_shard = lambda n, a: a[0].T if n in TRANSPOSED else a.reshape(BIG[n])
    shard = lambda n: as_shard(n, w_all[n])

    first = _all_gather([shard(n) for n in FIRST], [f32 if n == "lb_param" else bf16 for n in FIRST], "ag_first")
    lands = _cast_to_slot([shard(n) for n in SECOND], me_idx, first[0])
    ag2, token = _exchange_start([], lands, "ag_second_start")
    wk = _weights_in(dict(zip(FIRST, first)), FIRST, "weights_in_first")
    wz, wqb, wkvb, lb4 = (wk[n] for n in FIRST)

    h1, z = _fwd_in(x, g_mix, wz, tm)
    q, k, v = _fwd_mla_proj(z, cosb + token[0, 0], sina, sinb, g_qa, g_kva, wqb, wkvb, g_qn_p, g_kn_p, tm)
    a, a32 = _fwd_attn(q, k, v, tq_f)
    o, gla_b, gla_states = _fwd_gla(z, lb4)

    second = dict(zip(SECOND, _exchange_wait(ag2, [a, o], "ag_second_wait")[1]))
    w_pp = _weights_in(second, ("w_ple_proj",), "weights_in_second")["w_ple_proj"]
    w_o, w_down, w_pg, w_gate, w_up = (second[n].reshape(N_DEV * BIG[n][0], BIG[n][1]) for n in ROW_SHARDED + TRANSPOSED)

    x2, cat = _fwd_mix(a, o, z, g_hgo, x, w_o, tm)
    x3, gp, up = _fwd_ffn(x2, g_ffn, w_gate, w_up, w_down, tm)
    d3, h3, dpre, dpp, dg_ple, loss_tile = _ple_loss_fwd_bwd(x3, g_ple, w_pg, p, w_pp, target, tm)
    act, dgp, dup = _bwd_ffn_hidden(d3, gp, up, w_down, tm, D_FF // 2)
    d2, h2, dg_ffn = _bwd_ffn_in(d3, x2, dgp, dup, g_ffn, w_gate, w_up, tm)

    gw_gate, gw_up = _mm_tn_many(h2, [dgp, dup], "dw_gate_up", 512, transposed=True)
    blocks = _grads_out({"w_ple_proj": [(0, _mm_tn(p, dpp, "dw_ple_proj"))]}, ("w_ple_proj",), "grads_out_second")
    row_grads = {"w_o": _mm_tn(cat, d2, "dw_o"), "w_down": _mm_tn(act, d3, "dw_down"), "w_ple_gate": _mm_tn(h3, dpre, "dw_ple_gate"),
                 "w_gate": gw_gate, "w_up": gw_up}
    blocks.update({n: g.reshape(N_DEV, *BIG[n]) for n, g in row_grads.items()})
    empty = lambda names: [lax.empty((N_PEERS, *BIG[n]), bf16) for n in names]
    rs2, token = _exchange_start([blocks[n] for n in SECOND], empty(SECOND), "rs_second_start")

    da, do, dz_hg, dg_hgo = _bwd_mix(d2, w_o, o, z, g_hgo + token[0, 0], tm)
    dz_hq, dz_hff, dz_hfb, dz_hi, dlb4 = _bwd_gla(z, lb4, do, gla_b, gla_states)
    dq, dk, dv = _bwd_attn(q, k, v, da, a32, tq_b)
    dz_mla, cqn, ckvn, dq0, dkv0, dg_qa, dg_kva, dg_qn, dg_kn = _bwd_mla_proj(
        z, dq, dk, dv, cosb, sina, sinb, g_qa, g_kva, wqb, wkvb, g_qn_p, g_kn_p, tm)

    gz = list(zip((Z_HQ, Z_HFF, Z_HFB, Z_HI, Z_HG, Z_CQ),
                  _mm_tn_many(h1, [dz_hq, dz_hff, dz_hfb, dz_hi, dz_hg, dz_mla], "dw_in", 1024)))
    blocks1 = _grads_out({"w_in": gz, "w_qb": [(0, _mm_tn(cqn, dq0, "dw_qb"))], "w_kvb": [(0, _mm_tn(ckvn, dkv0, "dw_kvb"))],
                          "lb_param": [(0, dlb4)]}, FIRST, "grads_out_first")
    rs1, token = _exchange_start([blocks1[n] for n in FIRST], empty(FIRST), "rs_first_start")

    result = {}

    def adam(names, lands, src, n_blocks, after=()):
        flipped = TRANSPOSED + ("w_in",)
        given = lambda arrs: [arrs[n][0].T if n in flipped else arrs[n] for n in names]
        outs = _adam_shards(me_idx, [src[n] for n in names], lands, given(w_all), given(m_all), given(v_all), n_blocks,
                            "adamw_" + names[0], after)
        for n, o in zip(names, outs):
            result[n] = [t.T[None] for t in o] if n in flipped else o
        return outs[0][0]

    blocks2, lands2 = (dict(zip(SECOND, arrs)) for arrs in _exchange_wait(rs2, [token], "rs_second_wait"))
    by2 = ("w_down",) + TRANSPOSED
    by8 = tuple(n for n in SECOND if n not in by2)
    done = [adam(by8, [lands2[n] for n in by8], blocks2, 8), adam(by2, [lands2[n] for n in by2], blocks2, 2)]

    segments = [(dz_hq, 512, 0, Z_HQ // 512), (dz_hff, 512, 0, Z_HFF // 512), (dz_hfb, 512, 0, Z_HFB // 512),
                (dz_hi, 512, 0, Z_HI // 512), (dz_hg, 512, 0, Z_HG // 512), (dz_mla, 640, 0, Z_CQ // 640)]
    grad_x, dg_mix = _bwd_in(segments, wz, x, g_mix + token[0, 0], d2, tm)
    dgains = (dg_mix, dg_qa, dg_kva, dg_qn, dg_kn, dg_hgo, dg_ffn, dg_ple)

    vec = jnp.concatenate(list(dgains) + [loss_tile[0:1]], axis=1)
    parts = _all_gather([vec], [f32], "ag_gains")[0]
    outs, loss_row = _adam_gains(parts, [w_all[n] for n in SMALL], [m_all[n] for n in SMALL], [v_all[n] for n in SMALL])
    result.update(zip(SMALL, outs))

    blocks1, lands1 = _exchange_wait(rs1, [grad_x, loss_row, *done], "rs_first_wait")
    adam(FIRST, lands1, dict(zip(FIRST, blocks1)), 8)

    order = ("g_mix", "w_in", "g_qa", "g_kva", "w_qb", "w_kvb", "g_qn", "g_kn", "lb_param", "g_hgo", "w_o", "g_ffn",
             "w_gate", "w_up", "w_down", "g_ple", "w_ple_gate", "w_ple_proj")
    return (loss_row[0, 0], grad_x[None], *[result[n][k] for k in range(4) for n in order])
```

```python
import functools
import math

import jax
import jax.numpy as jnp
from jax import lax
from jax.experimental import pallas as pl
from jax.experimental.pallas import tpu as pltpu

f32 = jnp.float32
bf16 = jnp.bfloat16

N_DEV = 8
MLA_HEADS = 4
QK_NOPE = 128
QK_ROPE = 64
QK_HEAD = QK_NOPE + QK_ROPE
QK_PAD = 256
V_HEAD = 128
Q_LORA = 256
KV_LORA = 256
HG_HEADS = 4
CHUNK = 64
D_FF = 2816
PLE_DIM = 256
ROPE_THETA = 10000.0
EPS = 1e-6
ATTN_SCALE = QK_HEAD ** -0.5
LOG2_E = math.log2(math.e)
ATTN_SUB_ROWS = 256
Z_HQ, Z_HFF, Z_HFB, Z_HI, Z_HG, Z_CQ, Z_CKV, Z_KR, Z_W = 0, 512, 1024, 1536, 2048, 2560, 2816, 3072, 3200

ADAM_LR, ADAM_B1, ADAM_B2, ADAM_EPS, ADAM_WD, ADAM_STEP = 0.001, 0.9, 0.999, 1e-08, 0.01, 10

LANES = 128
BIG = {"w_in": (392, 1024), "w_qb": (256, 96), "w_kvb": (256, 128), "w_o": (128, 1024), "w_gate": (352, 1024),
       "w_up": (352, 1024), "w_down": (352, 1024), "w_ple_gate": (128, 1024), "w_ple_proj": (256, 128),
       "lb_param": (4, 64)}
TRANSPOSED = ("w_gate", "w_up", "w_in")
SMALL = {"g_mix": (0, 1024), "g_qa": (1024, 256), "g_kva": (1280, 256), "g_qn": (1536, 192), "g_kn": (1792, 192),
         "g_hgo": (2048, 512), "g_ffn": (2560, 1024), "g_ple": (3584, 1024)}
LOSS_OFF = 4608
GAIN_VEC = LOSS_OFF + LANES
Z_SEGMENTS = ((0, 256, Z_CQ), (256, 512, Z_CKV), (512, 576, Z_KR), (576, 1088, Z_HQ), (1088, 1600, Z_HFF),
              (1600, 2112, Z_HFB), (2112, 2624, Z_HI), (2624, 3136, Z_HG))

VMEM_LIMIT = 56 * 1024 * 1024
MESH = pl.DeviceIdType.MESH


def _cp(sem=None, vmem=None):
    return pltpu.CompilerParams(dimension_semantics=sem, vmem_limit_bytes=vmem)


def _const_spec(shape):
    nd = len(shape)
    return pl.BlockSpec(shape, lambda *_: (0,) * nd, pipeline_mode=pl.Buffered(1))


def _acc_spec(shape):
    nd = len(shape)
    return pl.BlockSpec(shape, lambda *_: (0,) * nd)


def _sigmoid(x):
    return jax.nn.sigmoid(x)


def _dot(a, b):
    return jnp.dot(a, b, preferred_element_type=f32)


def _dot_nt(a, b):
    return lax.dot_general(a, b, (((1,), (1,)), ((), ())), preferred_element_type=f32)


def _dot_tn(a, b):
    return lax.dot_general(a, b, (((0,), (0,)), ((), ())), preferred_element_type=f32)


def _rms_fwd(x, g, width):
    r = lax.rsqrt(jnp.sum(x * x, axis=-1, keepdims=True) * (1.0 / width) + EPS)
    return x * r * g, r


def _rms_bwd(dy, x, r, g, width):
    u = dy * g
    dx = r * u - x * (r * r * r) * (jnp.sum(u * x, axis=-1, keepdims=True) * (1.0 / width))
    return dx, dy * x * r


class _Both:
    def __init__(self, *copies):
        self.copies = copies

    def start(self):
        for cp in self.copies:
            cp.start()

    def wait(self):
        for cp in self.copies:
            cp.wait()


def _rope(b, c, sa, sb):
    return b * c + pltpu.roll(b, 32, 1) * sa + pltpu.roll(b, 96, 1) * sb


def _all_gather(shards, dtypes, name):
    n = len(shards)

    def body(*refs):
        in_refs, out_refs, stage = refs[:n], refs[n:2 * n], refs[2 * n:3 * n]
        send_sems, recv_sems, local_sems = refs[3 * n:]
        for w in range(n):
            stage[w][...] = in_refs[w][...].astype(stage[w].dtype)
        x, y, c = lax.axis_index("x"), lax.axis_index("y"), lax.axis_index("c")
        me, sibling = (x, y, c), (x, y, 1 - c)
        chips = [(1 - x, y), (x, 1 - y), (1 - x, 1 - y)]

        def slot(w, px, py, pc):
            return out_refs[w].at[4 * px + 2 * py + pc]

        def copy(w, k, block, to, src=None):
            return pltpu.make_async_remote_copy(
                src_ref=slot(w, *block) if src is None else src, dst_ref=slot(w, *block),
                send_sem=send_sems.at[w, k], recv_sem=recv_sems.at[w, k], device_id=to, device_id_type=MESH)

        first = []
        for j, chip in enumerate(chips):
            first += [copy(w, 1 + j, me, (*chip, c), src=stage[w]) for w in range(n)]
        first += [copy(w, 0, me, sibling, src=stage[w]) for w in range(n)]
        mine = [pltpu.make_async_copy(stage[w], slot(w, *me), local_sems.at[w]) for w in range(n)]
        for cp in first + mine:
            cp.start()
        passed = []
        for j, chip in enumerate(chips):
            for w in range(n):
                copy(w, 1 + j, (*chip, c), me).wait_recv()
                passed.append(copy(w, 4 + j, (*chip, c), sibling))
                passed[-1].start()
        for w in range(n):
            copy(w, 0, sibling, me).wait_recv()
        for j, chip in enumerate(chips):
            for w in range(n):
                copy(w, 4 + j, (*chip, 1 - c), me).wait_recv()
        for cp in first + passed:
            cp.wait_send()
        for cp in mine:
            cp.wait()

    return pl.pallas_call(
        body, name=name,
        out_shape=[jax.ShapeDtypeStruct((N_DEV, *s.shape), dt) for s, dt in zip(shards, dtypes)],
        in_specs=[pl.BlockSpec(memory_space=pltpu.VMEM)] * n,
        out_specs=[pl.BlockSpec(memory_space=pl.ANY)] * n,
        scratch_shapes=[pltpu.VMEM(s.shape, dt) for s, dt in zip(shards, dtypes)]
        + [pltpu.SemaphoreType.DMA((n, 7)), pltpu.SemaphoreType.DMA((n, 7)), pltpu.SemaphoreType.DMA((n,))],
        compiler_params=_cp(None, VMEM_LIMIT),
    )(*shards)


N_PEERS = N_DEV - 1
HBM_SPEC = pl.BlockSpec(memory_space=pltpu.HBM)
SEM_SPEC = pl.BlockSpec(memory_space=pltpu.SEMAPHORE)
DATAFLOW = pltpu.SideEffectType.DATAFLOW_SIDE_EFFECTING


def _me():
    return 4 * lax.axis_index("x") + 2 * lax.axis_index("y") + lax.axis_index("c")


def _peer(k):
    x, y, c = lax.axis_index("x"), lax.axis_index("y"), lax.axis_index("c")
    px = 1 - x if k & 4 else x
    py = 1 - y if k & 2 else y
    pc = 1 - c if k & 1 else c
    return (px, py, pc), 4 * px + 2 * py + pc


def _exchange_copies(src_refs, land_refs, send_sems, recv_sems, gather):
    cps = []
    me = _me()
    for k in range(1, N_DEV):
        peer, peer_idx = _peer(k)
        for w, land in enumerate(land_refs):
            src = land.at[me] if gather else src_refs[w].at[peer_idx]
            dst = land.at[me] if gather else land.at[k - 1]
            cps.append(pltpu.make_async_remote_copy(
                src_ref=src, dst_ref=dst, send_sem=send_sems.at[N_PEERS * w + k - 1], recv_sem=recv_sems.at[N_PEERS * w + k - 1],
                device_id=peer, device_id_type=MESH))
    return cps


def _exchange_start(srcs, lands, name):
    n_src, n = len(srcs), len(lands)

    def body(*refs):
        src_refs, land_refs = refs[:n_src], refs[n_src:n_src + n]
        send_sems, recv_sems = refs[n_src + n], refs[n_src + n + 1]
        token = refs[-1]
        for cp in _exchange_copies(src_refs, land_refs, send_sems, recv_sems, gather=not n_src):
            cp.start()
        token[...] = jnp.zeros_like(token)

    arrays = [pltpu.with_memory_space_constraint(a, pltpu.HBM) for a in (*srcs, *lands)]
    outs = pl.pallas_call(
        body, name=name,
        out_shape=(pltpu.SemaphoreType.DMA((n * N_PEERS,)), pltpu.SemaphoreType.DMA((n * N_PEERS,)),
                   *[pltpu.HBM(a.shape, a.dtype) for a in arrays], jax.ShapeDtypeStruct((8, LANES), f32)),
        in_specs=[HBM_SPEC] * len(arrays),
        out_specs=(SEM_SPEC, SEM_SPEC, *[HBM_SPEC] * len(arrays), pl.BlockSpec(memory_space=pltpu.VMEM)),
        input_output_aliases={i: 2 + i for i in range(len(arrays))},
        compiler_params=pltpu.CompilerParams(has_side_effects=DATAFLOW),
    )(*arrays)
    return (outs[0], outs[1], outs[2:2 + n_src], outs[2 + n_src:2 + n_src + n]), outs[-1]


def _exchange_wait(state, after, name):
    send_sems, recv_sems, srcs, lands = state
    n_src, n = len(srcs), len(lands)

    def body(*refs):
        src_refs, land_refs = refs[:n_src], refs[n_src:n_src + n]
        send_ref, recv_ref = refs[n_src + n], refs[n_src + n + 1]
        for cp in _exchange_copies(src_refs, land_refs, send_ref, recv_ref, gather=not n_src):
            cp.wait_send()
            cp.wait_recv()

    arrays = (*srcs, *lands)
    outs = pl.pallas_call(
        body, name=name,
        out_shape=tuple(pltpu.HBM(a.shape, a.dtype) for a in arrays),
        in_specs=[HBM_SPEC] * len(arrays) + [SEM_SPEC, SEM_SPEC] + [pl.BlockSpec(memory_space=pl.ANY)] * len(after),
        out_specs=tuple([HBM_SPEC] * len(arrays)),
        input_output_aliases={i: i for i in range(len(arrays))},
        compiler_params=pltpu.CompilerParams(has_side_effects=DATAFLOW),
    )(*arrays, send_sems, recv_sems, *after)
    return outs[:n_src], outs[n_src:]


def _cast_to_slot(shards, me_idx, after):
    n = len(shards)

    def body(i_ref, *refs):
        for w in range(n):
            refs[n + 1 + w][...] = refs[w][...].astype(bf16)

    return pl.pallas_call(
        body, name="cast_to_slot",
        grid_spec=pltpu.PrefetchScalarGridSpec(
            num_scalar_prefetch=1, grid=(1,),
            in_specs=[pl.BlockSpec(s.shape, lambda i, m: (0, 0)) for s in shards] + [pl.BlockSpec(memory_space=pl.ANY)],
            out_specs=[pl.BlockSpec((None, *s.shape), lambda i, m: (m[0], 0, 0)) for s in shards]),
        out_shape=[jax.ShapeDtypeStruct((N_DEV, *s.shape), bf16) for s in shards],
        compiler_params=_cp(("arbitrary",), VMEM_LIMIT),
    )(me_idx, *shards, after)


def _row_block(rows, n_blocks):
    return (rows // n_blocks, True) if rows % (16 * n_blocks) == 0 else (rows, False)


def _adam_math(w, g, m, v):
    m = ADAM_B1 * m + (1.0 - ADAM_B1) * g
    v = ADAM_B2 * v + (1.0 - ADAM_B2) * (g * g)
    m_hat = m / (1.0 - ADAM_B1 ** ADAM_STEP)
    v_hat = v / (1.0 - ADAM_B2 ** ADAM_STEP)
    delta = -ADAM_LR * (m_hat / (jnp.sqrt(v_hat) + ADAM_EPS) + ADAM_WD * w)
    return delta, m, v


def _adam_shards(me_idx, blocks, lands, ws, ms, vs, n_blocks, name, after=()):
    n = len(blocks)

    def body(i_ref, *refs):
        ins, outs = refs[:5 * n], refs[5 * n + len(after):]
        for w in range(n):
            g_ref, b_ref, w_ref, m_ref, v_ref = (ins[t * n + w] for t in range(5))
            g = g_ref[...].astype(f32)
            for k in range(N_PEERS):
                g = g + b_ref[k].astype(f32)
            if len(w_ref.shape) == 2:
                pieces = [(slice(None), g)]
            else:
                pieces = [(a, g[2 * a:2 * a + 2]) for a in range(2)]
            for at, gp in pieces:
                vals = (gp,) + _adam_math(w_ref[at], gp, m_ref[at], v_ref[at])
                for t, val in enumerate(vals):
                    outs[4 * w + t][at] = val

    specs = [[] for _ in range(5)]
    out_specs, out_shape = [], []
    for g, wt in zip(blocks, ws):
        rows, cols = g.shape[1:]
        rb, cut = _row_block(rows, n_blocks)
        if not cut and wt.ndim == 2 and cols % (LANES * n_blocks) == 0:
            cb = cols // n_blocks
            specs[0].append(pl.BlockSpec((None, rows, cb), lambda i, s: (s[0], 0, i)))
            specs[1].append(pl.BlockSpec((N_PEERS, rows, cb), lambda i, s: (0, 0, i)))
            shard = pl.BlockSpec((rows, cb), lambda i, s: (0, i))
            for t in (2, 3, 4):
                specs[t].append(shard)
            out_specs += [shard] * 4
            out_shape += [jax.ShapeDtypeStruct(wt.shape, f32)] * 4
            continue
        specs[0].append(pl.BlockSpec((None, rb, cols), functools.partial(lambda i, s, cut: (s[0], i if cut else 0, 0), cut=cut)))
        specs[1].append(pl.BlockSpec((N_PEERS, rb, cols), functools.partial(lambda i, s, cut: (0, i if cut else 0, 0), cut=cut)))
        if wt.ndim == 2:
            shard = pl.BlockSpec((rb, cols), functools.partial(lambda i, s, cut: (i if cut else 0, 0), cut=cut))
        elif wt.shape[0] == 1:
            shard = pl.BlockSpec((None, rb, cols), functools.partial(lambda i, s, cut: (0, i if cut else 0, 0), cut=cut))
        else:
            shard = pl.BlockSpec(wt.shape, functools.partial(lambda i, s, nd: (0,) * nd, nd=wt.ndim))
        for t in (2, 3, 4):
            specs[t].append(shard)
        out_specs += [shard] * 4
        out_shape += [jax.ShapeDtypeStruct(wt.shape, f32)] * 4
    outs = pl.pallas_call(
        body, name=name,
        grid_spec=pltpu.PrefetchScalarGridSpec(
            num_scalar_prefetch=1, grid=(n_blocks,), in_specs=sum(specs, []) + [pl.BlockSpec(memory_space=pl.ANY)] * len(after),
            out_specs=out_specs),
        out_shape=out_shape,
        compiler_params=_cp(("arbitrary",), VMEM_LIMIT),
    )(me_idx, *blocks, *lands, *ws, *ms, *vs, *after)
    return [outs[4 * w:4 * w + 4] for w in range(n)]


def _adam_gains(parts, ws, ms, vs):
    n = len(ws)

    def body(p_ref, *refs):
        ins, outs = refs[:3 * n], refs[3 * n:]
        g_all = p_ref[0]
        for k in range(1, N_DEV):
            g_all = g_all + p_ref[k]
        for w, (off, lanes) in enumerate(SMALL.values()):
            w_ref, m_ref, v_ref = ins[w], ins[n + w], ins[2 * n + w]
            if len(w_ref.shape) == 2:
                pieces = [(slice(None), off, lanes)]
            else:
                pieces = [((slice(None), h), off + LANES * h, LANES) for h in range(w_ref.shape[1])]
            for at, o, ln in pieces:
                g = g_all[:, o:o + ln]
                vals = (g,) + _adam_math(w_ref[at], g, m_ref[at], v_ref[at])
                for t, val in enumerate(vals):
                    outs[4 * w + t][at] = val
        outs[4 * n][...] = g_all[:, LOSS_OFF:LOSS_OFF + LANES]

    out_shape = sum([[jax.ShapeDtypeStruct(w.shape, f32)] * 4 for w in ws], []) + [jax.ShapeDtypeStruct((1, LANES), f32)]
    outs = pl.pallas_call(body, name="adamw_gains", out_shape=out_shape)(parts, *ws, *ms, *vs)
    return [outs[4 * w:4 * w + 4] for w in range(n)], outs[4 * n]


def _fwd_in(x, g_mix, wz, tm):
    s, d = x.shape

    def body(x_ref, g_ref, w_ref, h_ref, z_ref):
        h, _ = _rms_fwd(x_ref[...], g_ref[...], d)
        hb = h.astype(bf16)
        h_ref[...] = hb
        z_ref[...] = _dot_nt(hb, w_ref[...])

    return pl.pallas_call(
        body, name="fwd_in", grid=(s // tm,),
        in_specs=[pl.BlockSpec((tm, d), lambda i: (i, 0)), _const_spec((1, d)), _const_spec((Z_W, d))],
        out_specs=[pl.BlockSpec((tm, d), lambda i: (i, 0)), pl.BlockSpec((tm, Z_W), lambda i: (i, 0))],
        out_shape=[jax.ShapeDtypeStruct((s, d), bf16), jax.ShapeDtypeStruct((s, Z_W), f32)],
        compiler_params=_cp(("parallel",), VMEM_LIMIT),
    )(x, g_mix, wz)


def _mla_qk_fwd(cq, ckv, g_qa, g_kva, wqb, wkvb):
    cqn, rq = _rms_fwd(cq, g_qa, Q_LORA)
    ckvn, rkv = _rms_fwd(ckv, g_kva, KV_LORA)
    cqn_b, ckvn_b = cqn.astype(bf16), ckvn.astype(bf16)
    q0 = _dot(cqn_b, wqb)
    kv0 = _dot(ckvn_b, wkvb)
    return cqn_b, rq, ckvn_b, rkv, q0, kv0


def _fwd_mla_proj(z, cosb, sina, sinb, g_qa, g_kva, wqb, wkvb, g_qn, g_kn, tm):
    s = z.shape[0]
    hh = MLA_HEADS

    def body(cq_ref, ckv_ref, kr_ref, c_ref, sa_ref, sb_ref, gqa_ref, gkva_ref, wqb_ref, wkvb_ref, gqn_ref, gkn_ref,
             q_ref, k_ref, v_ref):
        _, _, _, _, q0, kv0 = _mla_qk_fwd(cq_ref[...], ckv_ref[...], gqa_ref[...], gkva_ref[...], wqb_ref[...], wkvb_ref[...])
        kr = kr_ref[...]
        c, sa, sb = c_ref[...], sa_ref[...], sb_ref[...]
        gqn, gkn = gqn_ref[...], gkn_ref[...]
        kr_sq = jnp.sum(kr * kr, axis=-1, keepdims=True)
        for h in range(hh):
            qh = q0[:, QK_PAD * h:QK_PAD * (h + 1)]
            qn, _ = _rms_fwd(qh, gqn, QK_HEAD)
            q_ref[h, :, 0:128] = qn[:, 0:128].astype(bf16)
            q_ref[h, :, 128:256] = _rope(qn[:, 128:256], c, sa, sb).astype(bf16)
            kn_ = kv0[:, 256 * h:256 * h + 128]
            rk = lax.rsqrt((jnp.sum(kn_ * kn_, axis=-1, keepdims=True) + kr_sq) * (1.0 / QK_HEAD) + EPS)
            k_ref[h, :, 0:128] = (kn_ * rk * gkn[:, 0:128]).astype(bf16)
            k_ref[h, :, 128:256] = _rope(kr * rk * gkn[:, 128:256], c, sa, sb).astype(bf16)
            v_ref[h] = kv0[:, 256 * h + 128:256 * h + 256].astype(bf16)

    row128 = pl.BlockSpec((tm, 128), lambda i: (i, 0))
    return pl.pallas_call(
        body, name="fwd_mla_proj", grid=(s // tm,),
        in_specs=[pl.BlockSpec((tm, 256), lambda i: (i, Z_CQ // 256)), pl.BlockSpec((tm, 256), lambda i: (i, Z_CKV // 256)),
                  pl.BlockSpec((tm, 128), lambda i: (i, Z_KR // 128)), row128, row128, row128,
                  _const_spec((1, 256)), _const_spec((1, 256)), _const_spec((256, 1024)), _const_spec((256, 1024)),
                  _const_spec((1, 256)), _const_spec((1, 256))],
        out_specs=[pl.BlockSpec((hh, tm, QK_PAD), lambda i: (0, i, 0)), pl.BlockSpec((hh, tm, QK_PAD), lambda i: (0, i, 0)),
                   pl.BlockSpec((hh, tm, V_HEAD), lambda i: (0, i, 0))],
        out_shape=[jax.ShapeDtypeStruct((hh, s, QK_PAD), bf16), jax.ShapeDtypeStruct((hh, s, QK_PAD), bf16),
                   jax.ShapeDtypeStruct((hh, s, V_HEAD), bf16)],
        compiler_params=_cp(("parallel",), VMEM_LIMIT),
    )(z, z, z, cosb, sina, sinb, g_qa, g_kva, wqb, wkvb, g_qn, g_kn)


def _fwd_attn(q, k, v, tq):
    hh, s, _ = q.shape

    n_sub = max(1, tq // ATTN_SUB_ROWS)

    def body(q_ref, k_ref, v_ref, o_ref, o32_ref):
        for t in range(n_sub):
            rows = slice(t * (tq // n_sub), (t + 1) * (tq // n_sub))
            sc = _dot_nt(q_ref[rows, :], k_ref[...])
            p = jnp.exp2((sc - jnp.max(sc, axis=-1, keepdims=True)) * (ATTN_SCALE * LOG2_E))
            l = jnp.sum(p, axis=-1, keepdims=True)
            o = _dot(p.astype(bf16), v_ref[...]) * (1.0 / l)
            o_ref[rows, :] = o.astype(bf16)
            o32_ref[rows, :] = o

    out = pl.BlockSpec((tq, V_HEAD), lambda h, i: (i, h))
    return pl.pallas_call(
        body, name="fwd_attn", grid=(hh, s // tq),
        in_specs=[pl.BlockSpec((None, tq, QK_PAD), lambda h, i: (h, i, 0)),
                  pl.BlockSpec((None, s, QK_PAD), lambda h, i: (h, 0, 0)),
                  pl.BlockSpec((None, s, V_HEAD), lambda h, i: (h, 0, 0))],
        out_specs=[out, out],
        out_shape=[jax.ShapeDtypeStruct((s, hh * V_HEAD), bf16), jax.ShapeDtypeStruct((s, hh * V_HEAD), f32)],
        compiler_params=_cp(("parallel", "parallel"), VMEM_LIMIT),
    )(q, k, v)


def _split3(x):
    hi = x.astype(bf16)
    r1 = x - hi.astype(f32)
    mid = r1.astype(bf16)
    lo = (r1 - mid.astype(f32)).astype(bf16)
    return jnp.concatenate([hi, mid, lo], axis=-1)


def _tri_sum(tri, x):
    y = _dot(tri, _split3(x))
    return y[:, 0:128] + y[:, 128:256] + y[:, 256:384]


GLA_GROUP = 4
GLA_ROWS = GLA_GROUP * CHUNK
GLA_HEADS_PER_STEP = 2


def _gla_masks(rev):
    row = lax.broadcasted_iota(jnp.int32, (GLA_ROWS, GLA_ROWS), 0)
    col = lax.broadcasted_iota(jnp.int32, (GLA_ROWS, GLA_ROWS), 1)
    shift = CHUNK.bit_length() - 1
    same = (jnp.right_shift(row, shift) == jnp.right_shift(col, shift)).astype(f32)
    lower, upper = (row >= col).astype(f32) * same, (row <= col).astype(f32) * same
    keep, keep_t = (upper, lower) if rev else (lower, upper)
    chunk_of = jnp.right_shift(lax.broadcasted_iota(jnp.int32, (GLA_ROWS, 1), 0), shift)
    return keep, keep.astype(bf16), keep_t.astype(bf16), [(chunk_of == c).astype(f32) for c in range(GLA_GROUP)]


def _gla_gates(hq, hf, lower):
    sg = _sigmoid(hf)
    f = lower + (1.0 - lower) * sg
    return hq * _sigmoid(hq), 1.0 - f, jnp.log(f), f, sg


def _gla_last_mid(b, rev):
    b3 = b.reshape(GLA_GROUP, CHUNK, 128)
    last, mid = (0, CHUNK // 2) if rev else (CHUNK - 1, CHUNK // 2 - 1)
    return b3[:, last:last + 1, :], b3[:, mid:mid + 1, :]


def _gla_per_row(per_chunk):
    return jnp.broadcast_to(per_chunk, (GLA_GROUP, CHUNK, 128)).reshape(GLA_ROWS, 128)


def _gla_block_diag(x, row_masks):
    return jnp.concatenate([(x * m).astype(bf16) for m in row_masks], axis=-1)


def _gla_diag(y):
    return jnp.concatenate([y[CHUNK * c:CHUNK * (c + 1), 128 * c:128 * (c + 1)] for c in range(GLA_GROUP)], axis=0)


def _gla_rows(n, n_groups, rev):
    ne = n_groups - 1 - n if rev else n
    return pl.ds(pl.multiple_of(ne * GLA_ROWS, GLA_ROWS), GLA_ROWS), ne * GLA_GROUP


def _gla_scan_order(rev):
    return tuple(reversed(range(GLA_GROUP))) if rev else tuple(range(GLA_GROUP))


def _fwd_gla(z, lb4):
    s = z.shape[0]
    n_groups = s // GLA_ROWS
    assert n_groups % 2 == 0
    hp = GLA_HEADS_PER_STEP
    chains = [(hh, rev) for hh in range(hp) for rev in (False, True)]

    def body(hq_ref, hff_ref, hfb_ref, hi_ref, lb_ref, o_ref, b_ref, states_ref, st_ref, stage_ref, b_stage, sems):
        st_ref[...] = jnp.zeros_like(st_ref)
        masks = {rev: _gla_masks(rev) for rev in (False, True)}
        lowers = [_sigmoid(lb_ref[int(rev):int(rev) + 1, 128 * hh:128 * (hh + 1)]
                           - lb_ref[2 + int(rev):3 + int(rev), 128 * hh:128 * (hh + 1)]) for hh, rev in chains]

        def states_out(slot, ci, chunk0):
            hh, rev = chains[ci]
            head = pl.program_id(0) * hp + hh
            rows = pl.ds(pl.multiple_of(chunk0 * CHUNK, GLA_ROWS), GLA_ROWS)
            return _Both(
                pltpu.make_async_copy(stage_ref.at[slot, ci], states_ref.at[head, int(rev), pl.ds(chunk0, GLA_GROUP)],
                                      sems.at[slot, ci]),
                pltpu.make_async_copy(b_stage.at[slot, ci], b_ref.at[int(rev), rows, pl.ds(pl.multiple_of(head * 128, 128), 128)],
                                      sems.at[slot, len(chains) + ci]))

        def make_step(first):
            def step(n, carry):
                slot = n % 2

                @pl.when(n >= 2)
                def _():
                    for ci in range(len(chains)):
                        states_out(slot, ci, 0).wait()

                for ci, (hh, rev) in enumerate(chains):
                    cols = slice(128 * hh, 128 * (hh + 1))
                    rows, chunk0 = _gla_rows(n, n_groups, rev)
                    maskf, tri, _, row_masks = masks[rev]
                    hf_ref = hfb_ref if rev else hff_ref
                    q, k, logf, _, _ = _gla_gates(hq_ref[rows, cols], hf_ref[rows, cols], lowers[ci])
                    vb = hi_ref[rows, cols].astype(bf16)
                    b = _tri_sum(tri, logf)
                    b_stage[slot, ci] = b
                    b_last3, b_mid3 = _gla_last_mid(b, rev)
                    b_last, b_mid = _gla_per_row(b_last3), _gla_per_row(b_mid3)
                    qi = (q * jnp.exp(b - b_mid)).astype(bf16)
                    ki = (k * jnp.exp(b_mid - b)).astype(bf16)
                    a = (_dot_nt(qi, ki) * maskf).astype(bf16)
                    kv = _dot_tn(vb, _gla_block_diag(k * jnp.exp(b_last - b), row_masks))
                    decay3 = jnp.exp(b_last3)
                    st = st_ref[ci]
                    before = [None] * GLA_GROUP
                    for c in _gla_scan_order(rev):
                        stage_ref[slot, ci, c] = st
                        before[c] = st.astype(bf16)
                        st = st * decay3[c] + kv[:, 128 * c:128 * (c + 1)]
                    st_ref[ci] = st
                    states_out(slot, ci, chunk0).start()
                    inter = _dot_nt((q * jnp.exp(b)).astype(bf16), jnp.concatenate(before, axis=0))
                    o = _dot(a, vb) + _gla_diag(inter)
                    if first:
                        o_ref[rows, cols] = o
                    else:
                        o_ref[rows, cols] += o
                return carry
            return step

        lax.fori_loop(0, n_groups // 2, make_step(True), 0)
        lax.fori_loop(n_groups // 2, n_groups, make_step(False), 0)
        for slot in range(2):
            for ci in range(len(chains)):
                states_out(slot, ci, 0).wait()

    w = 128 * hp
    col = lambda base: pl.BlockSpec((s, w), lambda h: (0, base // w + h))
    return pl.pallas_call(
        body, name="fwd_gla", grid=(HG_HEADS // hp,),
        in_specs=[col(Z_HQ), col(Z_HFF), col(Z_HFB), col(Z_HI), pl.BlockSpec((4, w), lambda h: (0, h))],
        out_specs=[pl.BlockSpec((s, w), lambda h: (0, h)), pl.BlockSpec(memory_space=pl.ANY), pl.BlockSpec(memory_space=pl.ANY)],
        out_shape=[jax.ShapeDtypeStruct((s, HG_HEADS * 128), f32), jax.ShapeDtypeStruct((2, s, HG_HEADS * 128), f32),
                   jax.ShapeDtypeStruct((HG_HEADS, 2, s // CHUNK, 128, 128), f32)],
        scratch_shapes=[pltpu.VMEM((len(chains), 128, 128), f32), pltpu.VMEM((2, len(chains), GLA_GROUP, 128, 128), f32),
                        pltpu.VMEM((2, len(chains), GLA_ROWS, 128), f32), pltpu.SemaphoreType.DMA((2, 2 * len(chains)))],
        compiler_params=_cp(("parallel",), VMEM_LIMIT),
    )(z, z, z, z, lb4)


def _hg_out(o, hg, g_hgo):
    outs, ons, rs = [], [], []
    for h in range(HG_HEADS):
        oh = o[:, 128 * h:128 * (h + 1)]
        on, r = _rms_fwd(oh, g_hgo[:, 128 * h:128 * (h + 1)], 128)
        ons.append(on)
        rs.append(r)
    on = jnp.concatenate(ons, axis=-1)
    sg = _sigmoid(hg)
    return on * (hg * sg), on, rs, sg


def _fwd_mix(a, o, z, g_hgo, x, w_o, tm):
    s, d = x.shape

    def body(a_ref, o_ref, hg_ref, g_ref, x_ref, w_ref, x2_ref, cat_ref):
        r, _, _, _ = _hg_out(o_ref[...], hg_ref[...], g_ref[...])
        cat = jnp.concatenate([a_ref[...], r.astype(bf16)], axis=-1)
        cat_ref[...] = cat
        x2_ref[...] = x_ref[...] + _dot(cat, w_ref[...])

    row512 = pl.BlockSpec((tm, 512), lambda i: (i, 0))
    rowd = pl.BlockSpec((tm, d), lambda i: (i, 0))
    return pl.pallas_call(
        body, name="fwd_mix", grid=(s // tm,),
        in_specs=[row512, row512, pl.BlockSpec((tm, 512), lambda i: (i, Z_HG // 512)), _const_spec((1, 512)), rowd,
                  _const_spec((d, d))],
        out_specs=[rowd, rowd],
        out_shape=[jax.ShapeDtypeStruct((s, d), f32), jax.ShapeDtypeStruct((s, d), bf16)],
        compiler_params=_cp(("parallel",), VMEM_LIMIT),
    )(a, o, z, g_hgo, x, w_o)


def _fwd_ffn(x2, g_ffn, w_gate, w_up, w_down, tm):
    s, d = x2.shape

    def body(x_ref, g_ref, wg_ref, wu_ref, wd_ref, x3_ref, gp_ref, up_ref):
        x = x_ref[...]
        h, _ = _rms_fwd(x, g_ref[...], d)
        hb = h.astype(bf16)
        gp = _dot_nt(hb, wg_ref[...])
        up = _dot_nt(hb, wu_ref[...])
        gp_ref[...] = gp.astype(bf16)
        up_ref[...] = up.astype(bf16)
        act = (gp * _sigmoid(gp) * up).astype(bf16)
        x3_ref[...] = x + _dot(act, wd_ref[...])

    rowd = pl.BlockSpec((tm, d), lambda i: (i, 0))
    rowf = pl.BlockSpec((tm, D_FF), lambda i: (i, 0))
    return pl.pallas_call(
        body, name="fwd_ffn", grid=(s // tm,),
        in_specs=[rowd, _const_spec((1, d)), _const_spec((D_FF, d)), _const_spec((D_FF, d)), _const_spec((D_FF, d))],
        out_specs=[rowd, rowf, rowf],
        out_shape=[jax.ShapeDtypeStruct((s, d), f32), jax.ShapeDtypeStruct((s, D_FF), bf16),
                   jax.ShapeDtypeStruct((s, D_FF), bf16)],
        compiler_params=_cp(("parallel",), VMEM_LIMIT),
    )(x2, g_ffn, w_gate, w_up, w_down)


def _ple_loss_fwd_bwd(x3, g_ple, w_pg, p, w_pp, target, tm):
    s, d = x3.shape

    def body(x_ref, g_ref, wg_ref, p_ref, wp_ref, t_ref, dx_ref, h_ref, dpre_ref, dpp_ref, dg_ref, loss_ref):
        @pl.when(pl.program_id(0) == 0)
        def _():
            dg_ref[...] = jnp.zeros_like(dg_ref)
            loss_ref[...] = jnp.zeros_like(loss_ref)

        x = x_ref[...]
        g = g_ref[...]
        h, r = _rms_fwd(x, g, d)
        hb = h.astype(bf16)
        gate = _sigmoid(_dot(hb, wg_ref[...]))
        pp = _dot(p_ref[...].astype(bf16), wp_ref[...])
        e = x + gate * pp - t_ref[...]
        loss_ref[...] += 0.5 * jnp.sum(e * e) * (1.0 / d)
        dy = e * (1.0 / d)
        dpre = (dy * pp * gate * (1.0 - gate)).astype(bf16)
        dx, dgx = _rms_bwd(_dot_nt(dpre, wg_ref[...]), x, r, g, d)
        dx_ref[...] = dy + dx
        dg_ref[...] += jnp.sum(dgx, axis=0, keepdims=True)
        h_ref[...] = hb
        dpre_ref[...] = dpre
        dpp_ref[...] = (dy * gate).astype(bf16)

    rowd = pl.BlockSpec((tm, d), lambda i: (i, 0))
    return pl.pallas_call(
        body, name="ple_loss_fwd_bwd", grid=(s // tm,),
        in_specs=[rowd, _const_spec((1, d)), _const_spec((d, d)), pl.BlockSpec((tm, PLE_DIM), lambda i: (i, 0)),
                  _const_spec((PLE_DIM, d)), rowd],
        out_specs=[rowd, rowd, rowd, rowd, _acc_spec((1, d)), _acc_spec((8, 128))],
        out_shape=[jax.ShapeDtypeStruct((s, d), f32), jax.ShapeDtypeStruct((s, d), bf16), jax.ShapeDtypeStruct((s, d), bf16),
                   jax.ShapeDtypeStruct((s, d), bf16), jax.ShapeDtypeStruct((1, d), f32), jax.ShapeDtypeStruct((8, 128), f32)],
        compiler_params=_cp(("arbitrary",), VMEM_LIMIT),
    )(x3, g_ple, w_pg, p, w_pp, target)


def _bwd_ffn_hidden(d3, gp, up, w_down, tm, tf):
    s, d = d3.shape

    def body(d3_ref, gp_ref, up_ref, wd_ref, act_ref, dgp_ref, dup_ref):
        gp, up = gp_ref[...].astype(f32), up_ref[...].astype(f32)
        sg = _sigmoid(gp)
        silu = gp * sg
        act_ref[...] = (silu * up).astype(bf16)
        dact = _dot_nt(d3_ref[...].astype(bf16), wd_ref[...])
        dgp_ref[...] = (dact * up * (sg * (1.0 + gp * (1.0 - sg)))).astype(bf16)
        dup_ref[...] = (dact * silu).astype(bf16)

    rowf = pl.BlockSpec((tm, tf), lambda f, i: (i, f))
    return pl.pallas_call(
        body, name="bwd_ffn_hidden", grid=(D_FF // tf, s // tm),
        in_specs=[pl.BlockSpec((tm, d), lambda f, i: (i, 0)), rowf, rowf, pl.BlockSpec((tf, d), lambda f, i: (f, 0))],
        out_specs=[rowf, rowf, rowf],
        out_shape=[jax.ShapeDtypeStruct((s, D_FF), bf16)] * 3,
        compiler_params=_cp(("parallel", "parallel"), VMEM_LIMIT),
    )(d3, gp, up, w_down)


def _bwd_ffn_in(d3, x2, dgp, dup, g_ffn, w_gate, w_up, tm):
    s, d = x2.shape

    def body(d3_ref, x_ref, dgp_ref, dup_ref, g_ref, wg_ref, wu_ref, d2_ref, h_ref, dg_ref):
        @pl.when(pl.program_id(0) == 0)
        def _():
            dg_ref[...] = jnp.zeros_like(dg_ref)

        x, g = x_ref[...], g_ref[...]
        dh = _dot(dgp_ref[...], wg_ref[...]) + _dot(dup_ref[...], wu_ref[...])
        h, r = _rms_fwd(x, g, d)
        h_ref[...] = h.astype(bf16)
        dx, dgx = _rms_bwd(dh, x, r, g, d)
        d2_ref[...] = d3_ref[...] + dx
        dg_ref[...] += jnp.sum(dgx, axis=0, keepdims=True)

    rowd = pl.BlockSpec((tm, d), lambda i: (i, 0))
    rowf = pl.BlockSpec((tm, D_FF), lambda i: (i, 0))
    return pl.pallas_call(
        body, name="bwd_ffn_in", grid=(s // tm,),
        in_specs=[rowd, rowd, rowf, rowf, _const_spec((1, d)), _const_spec((D_FF, d)), _const_spec((D_FF, d))],
        out_specs=[rowd, rowd, _acc_spec((1, d))],
        out_shape=[jax.ShapeDtypeStruct((s, d), f32), jax.ShapeDtypeStruct((s, d), bf16), jax.ShapeDtypeStruct((1, d), f32)],
        compiler_params=_cp(("arbitrary",), VMEM_LIMIT),
    )(d3, x2, dgp, dup, g_ffn, w_gate, w_up)


def _bwd_mix(d2, w_o, o, z, g_hgo, tm):
    s, d = d2.shape

    def body(d2_ref, w_ref, o_ref, hg_ref, g_ref, da_ref, do_ref, dhg_ref, dg_ref):
        @pl.when(pl.program_id(0) == 0)
        def _():
            dg_ref[...] = jnp.zeros_like(dg_ref)

        dcat = _dot_nt(d2_ref[...].astype(bf16), w_ref[...])
        da_ref[...] = dcat[:, 0:512].astype(bf16)
        dr = dcat[:, 512:1024]
        o, hg, g = o_ref[...], hg_ref[...], g_ref[...]
        _, on, rs, sg = _hg_out(o, hg, g)
        dhg_ref[...] = (dr * on * (sg * (1.0 + hg * (1.0 - sg)))).astype(bf16)
        don = dr * (hg * sg)
        dgs = []
        for h in range(HG_HEADS):
            cols = slice(128 * h, 128 * (h + 1))
            dx, dgx = _rms_bwd(don[:, cols], o[:, cols], rs[h], g[:, cols], 128)
            do_ref[:, cols] = dx
            dgs.append(jnp.sum(dgx, axis=0, keepdims=True))
        dg_ref[...] += jnp.concatenate(dgs, axis=-1)

    row512 = pl.BlockSpec((tm, 512), lambda i: (i, 0))
    return pl.pallas_call(
        body, name="bwd_mix", grid=(s // tm,),
        in_specs=[pl.BlockSpec((tm, d), lambda i: (i, 0)), _const_spec((d, d)), row512,
                  pl.BlockSpec((tm, 512), lambda i: (i, Z_HG // 512)), _const_spec((1, 512))],
        out_specs=[row512, row512, row512, _acc_spec((1, 512))],
        out_shape=[jax.ShapeDtypeStruct((s, 512), bf16), jax.ShapeDtypeStruct((s, 512), f32), jax.ShapeDtypeStruct((s, 512), bf16),
                   jax.ShapeDtypeStruct((1, 512), f32)],
        compiler_params=_cp(("arbitrary",), VMEM_LIMIT),
    )(d2, w_o, o, z, g_hgo)


def _bwd_gla(z, lb4, do, b_fwd, states):
    s = z.shape[0]
    n_chunks = s // CHUNK
    n_groups = s // GLA_ROWS
    assert n_groups % 2 == 0

    def body(hq_ref, hff_ref, hfb_ref, hi_ref, lb_ref, do_ref, b_all, st_all, dhq_ref, dhff_ref, dhfb_ref, dhi_ref, dlb_ref,
             dst_ref, dq_acc, dv_acc, dlow_ref):
        dirs = (False, True)
        masks = [_gla_masks(rev) for rev in dirs]
        lowers = [_sigmoid(lb_ref[int(rev):int(rev) + 1, :] - lb_ref[2 + int(rev):3 + int(rev), :]) for rev in dirs]
        hf_refs, dhf_refs = (hff_ref, hfb_ref), (dhff_ref, dhfb_ref)

        dst_ref[...] = jnp.zeros_like(dst_ref)
        dlow_ref[...] = jnp.zeros_like(dlow_ref)

        def make_bwd_step(first):
            def bwd_step(j, carry):
                n = n_groups - 1 - j
                for d, rev in enumerate(dirs):
                    maskf, _, tri_t, row_masks = masks[d]
                    lower = lowers[d]
                    rows, chunk0 = _gla_rows(n, n_groups, rev)
                    hq, hf = hq_ref[rows, :], hf_refs[d][rows, :]
                    q, k, _, f, sg = _gla_gates(hq, hf, lower)
                    v = hi_ref[rows, :]
                    dout = do_ref[rows, :]
                    b = b_all[d, rows, :]
                    b_last3, b_mid3 = _gla_last_mid(b, rev)
                    b_last, b_mid = _gla_per_row(b_last3), _gla_per_row(b_mid3)
                    e1, e2, e3, e4 = jnp.exp(b - b_mid), jnp.exp(b_mid - b), jnp.exp(b_last - b), jnp.exp(b)
                    decay3 = jnp.exp(b_last3)
                    qi, ki, kt, qt = q * e1, k * e2, k * e3, q * e4
                    qib, kib, ktb = qi.astype(bf16), ki.astype(bf16), kt.astype(bf16)
                    vb, dob = v.astype(bf16), dout.astype(bf16)
                    a = (_dot_nt(qib, kib) * maskf).astype(bf16)
                    da = (_dot_nt(dob, vb) * maskf).astype(bf16)
                    dqi = _dot(da, kib)
                    dki = _dot_tn(da, qib)
                    into_state = _dot_tn(dob, _gla_block_diag(qt, row_masks))
                    dst = dst_ref[d]
                    sts, dsts, ddecay = [None] * GLA_GROUP, [None] * GLA_GROUP, [None] * GLA_GROUP
                    for c in reversed(_gla_scan_order(rev)):
                        sts[c] = st_all[d, chunk0 + c]
                        dsts[c] = dst.astype(bf16)
                        ddecay[c] = jnp.sum(dst * sts[c], axis=0, keepdims=True)[None]
                        dst = dst * decay3[c] + into_state[:, 128 * c:128 * (c + 1)]
                    dst_ref[d] = dst
                    dv = _dot_tn(a, dob) + _gla_diag(_dot_nt(ktb, jnp.concatenate(dsts, axis=0)))
                    dqt = _gla_diag(_dot(dob, jnp.concatenate([x.astype(bf16) for x in sts], axis=-1)))
                    dkt = _gla_diag(_dot(vb, jnp.concatenate(dsts, axis=-1)))
                    dq = dqi * e1 + dqt * e4
                    dk = dki * e2 + dkt * e3
                    db = dqi * qi - dki * ki + dqt * qt - dkt * kt
                    dlast3 = (jnp.sum((dkt * kt).reshape(GLA_GROUP, CHUNK, 128), axis=1, keepdims=True)
                              + jnp.concatenate(ddecay, axis=0) * decay3)
                    dlogf = _tri_sum(tri_t, db) + _gla_per_row(dlast3)
                    df = dlogf / f - dk
                    dhf_refs[d][rows, :] = (df * (1.0 - lower) * sg * (1.0 - sg)).astype(bf16)
                    dlow_ref[d:d + 1, :] += jnp.sum(df * (1.0 - sg), axis=0, keepdims=True)
                    sq = _sigmoid(hq)
                    dhq = dq * (sq * (1.0 + hq * (1.0 - sq)))
                    if first:
                        dq_acc[rows, :] = dhq
                        dv_acc[rows, :] = dv
                    else:
                        dhq_ref[rows, :] = (dq_acc[rows, :] + dhq).astype(bf16)
                        dhi_ref[rows, :] = (dv_acc[rows, :] + dv).astype(bf16)
                return carry
            return bwd_step

        lax.fori_loop(0, n_groups // 2, make_bwd_step(True), 0, unroll=2)
        lax.fori_loop(n_groups // 2, n_groups, make_bwd_step(False), 0, unroll=2)

        for d in range(2):
            dl = dlow_ref[d:d + 1, :] * lowers[d] * (1.0 - lowers[d])
            dlb_ref[d:d + 1, :] = dl
            dlb_ref[2 + d:3 + d, :] = -dl

    col = lambda base: pl.BlockSpec((s, 128), lambda h: (0, base // 128 + h))
    return pl.pallas_call(
        body, name="bwd_gla", grid=(HG_HEADS,),
        in_specs=[col(Z_HQ), col(Z_HFF), col(Z_HFB), col(Z_HI), pl.BlockSpec((4, 128), lambda h: (0, h)), col(0),
                  pl.BlockSpec((2, s, 128), lambda h: (0, 0, h)),
                  pl.BlockSpec((None, 2, n_chunks, 128, 128), lambda h: (h, 0, 0, 0, 0), pipeline_mode=pl.Buffered(1))],
        out_specs=[col(0), col(0), col(0), col(0), pl.BlockSpec((4, 128), lambda h: (0, h))],
        out_shape=[jax.ShapeDtypeStruct((s, 512), bf16)] * 4 + [jax.ShapeDtypeStruct((4, 512), f32)],
        scratch_shapes=[pltpu.VMEM((2, 128, 128), f32), pltpu.VMEM((s, 128), f32), pltpu.VMEM((s, 128), f32),
                        pltpu.VMEM((2, 128), f32)],
        compiler_params=_cp(("parallel",), VMEM_LIMIT),
    )(z, z, z, z, lb4, do, b_fwd, states)


def _bwd_attn(q, k, v, da, a32, tq):
    hh, s, _ = q.shape

    n_sub = max(1, tq // ATTN_SUB_ROWS)

    def body(q_ref, k_ref, v_ref, do_ref, o_ref, dq_ref, dk_ref, dv_ref, p_all, ds_all, dol_ref, dkt_ref, dvt_ref):
        @pl.when(pl.program_id(1) == 0)
        def _():
            dkt_ref[...] = jnp.zeros_like(dkt_ref)
            dvt_ref[...] = jnp.zeros_like(dvt_ref)

        kb, vb = k_ref[...], v_ref[...]
        for t in range(n_sub):
            rows = slice(t * (tq // n_sub), (t + 1) * (tq // n_sub))
            sc = _dot_nt(q_ref[rows, :], kb)
            p = jnp.exp2((sc - jnp.max(sc, axis=-1, keepdims=True)) * (ATTN_SCALE * LOG2_E))
            inv_l = 1.0 / jnp.sum(p, axis=-1, keepdims=True)
            p_all[rows, :] = p.astype(bf16)
            dob = do_ref[rows, :]
            dof = dob.astype(f32)
            delta = jnp.sum(dof * o_ref[rows, :], axis=-1, keepdims=True)
            ds_all[rows, :] = p_all[rows, :] * ((_dot_nt(dob, vb) - delta) * inv_l).astype(bf16)
            dq_ref[rows, :] = _dot(ds_all[rows, :], kb) * ATTN_SCALE
            dol_ref[rows, :] = (dof * inv_l).astype(bf16)
        dkt_ref[...] += _dot_tn(q_ref[...], ds_all[...])
        dvt_ref[...] += _dot_tn(dol_ref[...], p_all[...])

        @pl.when(pl.program_id(1) == s // tq - 1)
        def _():
            dk_ref[...] = dkt_ref[...].T * ATTN_SCALE
            dv_ref[...] = dvt_ref[...].T

    return pl.pallas_call(
        body, name="bwd_attn", grid=(hh, s // tq),
        in_specs=[pl.BlockSpec((None, tq, QK_PAD), lambda h, i: (h, i, 0)),
                  pl.BlockSpec((None, s, QK_PAD), lambda h, i: (h, 0, 0)),
                  pl.BlockSpec((None, s, V_HEAD), lambda h, i: (h, 0, 0)),
                  pl.BlockSpec((tq, V_HEAD), lambda h, i: (i, h)), pl.BlockSpec((tq, V_HEAD), lambda h, i: (i, h))],
        out_specs=[pl.BlockSpec((None, tq, QK_PAD), lambda h, i: (h, i, 0)),
                   pl.BlockSpec((None, s, QK_PAD), lambda h, i: (h, 0, 0)),
                   pl.BlockSpec((None, s, V_HEAD), lambda h, i: (h, 0, 0))],
        out_shape=[jax.ShapeDtypeStruct((hh, s, QK_PAD), f32), jax.ShapeDtypeStruct((hh, s, QK_PAD), f32),
                   jax.ShapeDtypeStruct((hh, s, V_HEAD), f32)],
        scratch_shapes=[pltpu.VMEM((tq, s), bf16), pltpu.VMEM((tq, s), bf16), pltpu.VMEM((tq, V_HEAD), bf16),
                        pltpu.VMEM((QK_PAD, s), f32), pltpu.VMEM((V_HEAD, s), f32)],
        compiler_params=_cp(("parallel", "arbitrary"), VMEM_LIMIT),
    )(q, k, v, da, a32)


def _bwd_mla_proj(z, dq, dk, dv, cosb, sina, sinb, g_qa, g_kva, wqb, wkvb, g_qn, g_kn, tm):
    s = z.shape[0]
    hh = MLA_HEADS

    def body(cq_ref, ckv_ref, kr_ref, dq_ref, dk_ref, dv_ref, c_ref, sa_ref, sb_ref, gqa_ref, gkva_ref, wqb_ref, wkvb_ref,
             gqn_ref, gkn_ref, dz_ref, cqn_ref, ckvn_ref, dq0_ref, dkv0_ref, dgqa_ref, dgkva_ref, dgqn_ref, dgkn_ref):
        @pl.when(pl.program_id(0) == 0)
        def _():
            for r in (dgqa_ref, dgkva_ref, dgqn_ref, dgkn_ref):
                r[...] = jnp.zeros_like(r)

        cq, ckv, kr = cq_ref[...], ckv_ref[...], kr_ref[...]
        gqa, gkva, gqn, gkn = gqa_ref[...], gkva_ref[...], gqn_ref[...], gkn_ref[...]
        cqn_b, rq, ckvn_b, rkv, q0, kv0 = _mla_qk_fwd(cq, ckv, gqa, gkva, wqb_ref[...], wkvb_ref[...])
        cqn_ref[...] = cqn_b
        ckvn_ref[...] = ckvn_b
        c, sa, sb = c_ref[...], -sa_ref[...], -sb_ref[...]
        kr_sq = jnp.sum(kr * kr, axis=-1, keepdims=True)
        dkr = jnp.zeros_like(kr)
        dgqn = jnp.zeros((1, QK_PAD), f32)
        dgkn = jnp.zeros((1, QK_PAD), f32)
        for h in range(hh):
            qh = q0[:, QK_PAD * h:QK_PAD * (h + 1)]
            rh = lax.rsqrt(jnp.sum(qh * qh, axis=-1, keepdims=True) * (1.0 / QK_HEAD) + EPS)
            dqh = dq_ref[h]
            dqn = jnp.concatenate([dqh[:, 0:128], _rope(dqh[:, 128:256], c, sa, sb)], axis=-1)
            dq0h, dgx = _rms_bwd(dqn, qh, rh, gqn, QK_HEAD)
            dq0_ref[:, QK_PAD * h:QK_PAD * (h + 1)] = dq0h.astype(bf16)
            dgqn = dgqn + jnp.sum(dgx, axis=0, keepdims=True)

            kn_ = kv0[:, 256 * h:256 * h + 128]
            k0 = jnp.concatenate([kn_, kr], axis=-1)
            rk = lax.rsqrt((jnp.sum(kn_ * kn_, axis=-1, keepdims=True) + kr_sq) * (1.0 / QK_HEAD) + EPS)
            dkh = dk_ref[h]
            dkn = jnp.concatenate([dkh[:, 0:128], _rope(dkh[:, 128:256], c, sa, sb)], axis=-1)
            dk0, dgx = _rms_bwd(dkn, k0, rk, gkn, QK_HEAD)
            dgkn = dgkn + jnp.sum(dgx, axis=0, keepdims=True)
            dkv0_ref[:, 256 * h:256 * h + 128] = dk0[:, 0:128].astype(bf16)
            dkv0_ref[:, 256 * h + 128:256 * h + 256] = dv_ref[h].astype(bf16)
            dkr = dkr + dk0[:, 128:256]
        dgqn_ref[...] += dgqn
        dgkn_ref[...] += dgkn
        dcq, dgx = _rms_bwd(_dot_nt(dq0_ref[...], wqb_ref[...]), cq, rq, gqa, Q_LORA)
        dgqa_ref[...] += jnp.sum(dgx, axis=0, keepdims=True)
        dckv, dgx = _rms_bwd(_dot_nt(dkv0_ref[...], wkvb_ref[...]), ckv, rkv, gkva, KV_LORA)
        dgkva_ref[...] += jnp.sum(dgx, axis=0, keepdims=True)
        dz_ref[:, 0:256] = dcq.astype(bf16)
        dz_ref[:, 256:512] = dckv.astype(bf16)
        dz_ref[:, 512:640] = dkr.astype(bf16)

    row128 = pl.BlockSpec((tm, 128), lambda i: (i, 0))
    row256 = pl.BlockSpec((tm, 256), lambda i: (i, 0))
    row1024 = pl.BlockSpec((tm, 1024), lambda i: (i, 0))
    hd = lambda w: pl.BlockSpec((hh, tm, w), lambda i: (0, i, 0))
    return pl.pallas_call(
        body, name="bwd_mla_proj", grid=(s // tm,),
        in_specs=[pl.BlockSpec((tm, 256), lambda i: (i, Z_CQ // 256)), pl.BlockSpec((tm, 256), lambda i: (i, Z_CKV // 256)),
                  pl.BlockSpec((tm, 128), lambda i: (i, Z_KR // 128)), hd(QK_PAD), hd(QK_PAD), hd(V_HEAD),
                  row128, row128, row128,
                  _const_spec((1, 256)), _const_spec((1, 256)), _const_spec((256, 1024)), _const_spec((256, 1024)),
                  _const_spec((1, 256)), _const_spec((1, 256))],
        out_specs=[pl.BlockSpec((tm, 640), lambda i: (i, 0)), row256, row256, row1024, row1024,
                   _acc_spec((1, 256)), _acc_spec((1, 256)), _acc_spec((1, 256)), _acc_spec((1, 256))],
        out_shape=[jax.ShapeDtypeStruct((s, 640), bf16), jax.ShapeDtypeStruct((s, 256), bf16), jax.ShapeDtypeStruct((s, 256), bf16),
                   jax.ShapeDtypeStruct((s, 1024), bf16), jax.ShapeDtypeStruct((s, 1024), bf16)]
        + [jax.ShapeDtypeStruct((1, 256), f32)] * 4,
        compiler_params=_cp(("arbitrary",), VMEM_LIMIT),
    )(z, z, z, dq, dk, dv, cosb, sina, sinb, g_qa, g_kva, wqb, wkvb, g_qn, g_kn)


def _bwd_in(segments, wz, x, g_mix, d2, tm):
    s, d = x.shape
    n_seg = len(segments)

    def body(*refs):
        dz_refs, w_refs = refs[:n_seg], refs[n_seg:2 * n_seg]
        x_ref, g_ref, d2_ref, gx_ref, dg_ref = refs[2 * n_seg:]

        @pl.when(pl.program_id(0) == 0)
        def _():
            dg_ref[...] = jnp.zeros_like(dg_ref)

        dh = _dot(dz_refs[0][...], w_refs[0][...])
        for a_ref, w_ref in zip(dz_refs[1:], w_refs[1:]):
            dh = dh + _dot(a_ref[...], w_ref[...])
        x, g = x_ref[...], g_ref[...]
        r = lax.rsqrt(jnp.sum(x * x, axis=-1, keepdims=True) * (1.0 / d) + EPS)
        dx, dgx = _rms_bwd(dh, x, r, g, d)
        gx_ref[...] = d2_ref[...] + dx
        dg_ref[...] += jnp.sum(dgx, axis=0, keepdims=True)

    rowd = pl.BlockSpec((tm, d), lambda i: (i, 0))
    dz_specs = [pl.BlockSpec((tm, w), functools.partial(lambda i, j: (i, j), j=ja)) for _, w, ja, _ in segments]
    w_specs = [pl.BlockSpec((w, d), functools.partial(lambda i, j: (j, 0), j=jw), pipeline_mode=pl.Buffered(1))
               for _, w, _, jw in segments]
    return pl.pallas_call(
        body, name="bwd_in", grid=(s // tm,),
        in_specs=dz_specs + w_specs + [rowd, _const_spec((1, d)), rowd],
        out_specs=[rowd, _acc_spec((1, d))],
        out_shape=[jax.ShapeDtypeStruct((s, d), f32), jax.ShapeDtypeStruct((1, d), f32)],
        compiler_params=_cp(("arbitrary",), VMEM_LIMIT),
    )(*[a for a, _, _, _ in segments], *([wz] * n_seg), x, g_mix, d2)


def _pick_tile(n, cap):
    best = None
    for t in range(LANES, cap + 1, LANES):
        if n % t == 0:
            best = t
    return best if best is not None else n


def _mm_tn_many(a, bs, name, tm, transposed=False):
    kk, m = a.shape
    n_b = len(bs)
    tk = min(1024, kk)
    n_k = kk // tk

    def body(a_ref, *refs):
        b_refs, o_refs, acc_refs = refs[:n_b], refs[n_b:2 * n_b], refs[2 * n_b:]

        @pl.when(pl.program_id(1) == 0)
        def _():
            for acc in acc_refs:
                acc[...] = jnp.zeros_like(acc)
        a_blk = a_ref[...].astype(bf16)
        for b_ref, acc in zip(b_refs, acc_refs):
            acc[...] += _dot_tn(a_blk, b_ref[...].astype(bf16))

        @pl.when(pl.program_id(1) == n_k - 1)
        def _():
            for o_ref, acc in zip(o_refs, acc_refs):
                o_ref[...] = (acc[...].T if transposed else acc[...]).astype(bf16)

    if transposed:
        out_specs = [pl.BlockSpec((b.shape[1], tm), lambda i, k: (0, i)) for b in bs]
        out_shape = [jax.ShapeDtypeStruct((b.shape[1], m), bf16) for b in bs]
    else:
        out_specs = [pl.BlockSpec((tm, b.shape[1]), lambda i, k: (i, 0)) for b in bs]
        out_shape = [jax.ShapeDtypeStruct((m, b.shape[1]), bf16) for b in bs]
    return pl.pallas_call(
        body, name=name, grid=(m // tm, n_k),
        in_specs=[pl.BlockSpec((tk, tm), lambda i, k: (k, i))] + [pl.BlockSpec((tk, b.shape[1]), lambda i, k: (k, 0)) for b in bs],
        out_specs=out_specs,
        out_shape=out_shape,
        scratch_shapes=[pltpu.VMEM((tm, b.shape[1]), f32) for b in bs],
        compiler_params=_cp(("parallel", "arbitrary"), VMEM_LIMIT),
    )(a, *bs)


def _mm_tn(a, b, name):
    kk, m = a.shape
    _, n = b.shape
    tm = _pick_tile(m, 1408)
    tn = _pick_tile(n, 1408)
    tk = min(1024, kk)

    n_k = kk // tk

    def body(a_ref, b_ref, o_ref, acc_ref):
        @pl.when(pl.program_id(2) == 0)
        def _():
            acc_ref[...] = jnp.zeros_like(acc_ref)
        acc_ref[...] += _dot_tn(a_ref[...].astype(bf16), b_ref[...].astype(bf16))

        @pl.when(pl.program_id(2) == n_k - 1)
        def _():
            o_ref[...] = acc_ref[...].astype(bf16)

    return pl.pallas_call(
        body, name=name, grid=(m // tm, n // tn, n_k),
        in_specs=[pl.BlockSpec((tk, tm), lambda i, j, k: (k, i)), pl.BlockSpec((tk, tn), lambda i, j, k: (k, j))],
        out_specs=pl.BlockSpec((tm, tn), lambda i, j, k: (i, j)),
        out_shape=jax.ShapeDtypeStruct((m, n), bf16),
        scratch_shapes=[pltpu.VMEM((tm, tn), f32)],
        compiler_params=_cp(("parallel", "parallel", "arbitrary"), VMEM_LIMIT),
    )(a, b)


def _rope_tables(positions):
    inv_freq = ROPE_THETA ** (-jnp.arange(0, QK_ROPE, 2, dtype=f32) / QK_ROPE)
    ang = positions.astype(f32)[:, None] * inv_freq
    cos, sin = jnp.cos(ang), jnp.sin(ang)
    zero = jnp.zeros_like(cos)
    return (jnp.concatenate([cos, cos, zero, zero], axis=1), jnp.concatenate([zero, sin, zero, zero], axis=1),
            jnp.concatenate([-sin, zero, zero, zero], axis=1))


def _pad256(g):
    return jnp.pad(g.reshape(1, QK_HEAD), ((0, 0), (0, QK_PAD - QK_HEAD)))


RELAYOUT_BLOCKS = 8
FIRST = ("w_in", "w_qb", "w_kvb", "lb_param")
SECOND = ("w_o", "w_gate", "w_up", "w_down", "w_ple_gate", "w_ple_proj")
ROW_SHARDED = ("w_o", "w_down", "w_ple_gate")


def _col_moves(j):
    width = BIG["w_in"][0]
    lo = width * j
    w_in = [(max(lo, a) - lo, min(lo + width, b) - lo, d + max(lo, a) - a)
            for a, b, d in Z_SEGMENTS if max(lo, a) < min(lo + width, b)]
    head, half = divmod(j, 2)
    whole = lambda n: [(0, BIG[n][1], BIG[n][1] * j)]
    return {"w_in": w_in, "w_qb": [(0, 96, QK_PAD * head + 96 * half)], "w_kvb": whole("w_kvb"),
            "w_ple_proj": whole("w_ple_proj"), "lb_param": whole("lb_param")}


def _kernel_shape(name):
    rows, cols = BIG[name]
    if name == "w_in":
        return (Z_W, cols)
    return (rows, MLA_HEADS * QK_PAD if name == "w_qb" else N_DEV * cols)


def _relayout_specs(names, by_dev):
    specs = []
    for n in names:
        rows, cols = BIG[n]
        if n == "lb_param":
            specs.append(_acc_spec((N_DEV, rows, cols) if by_dev else _kernel_shape(n)))
        elif n == "w_in":
            cb = cols // RELAYOUT_BLOCKS
            specs.append(pl.BlockSpec((N_DEV, rows, cb), lambda i: (0, 0, i)) if by_dev else pl.BlockSpec((Z_W, cb), lambda i: (0, i)))
        elif by_dev:
            specs.append(pl.BlockSpec((N_DEV, rows // RELAYOUT_BLOCKS, cols), lambda i: (0, i, 0)))
        else:
            specs.append(pl.BlockSpec((rows // RELAYOUT_BLOCKS, _kernel_shape(n)[1]), lambda i: (i, 0)))
    return specs


def _weights_in(gathered, names, name):
    n = len(names)

    def body(*refs):
        ins, outs = dict(zip(names, refs[:n])), dict(zip(names, refs[n:]))
        if "w_in" in outs:
            outs["w_in"][Z_KR + QK_ROPE:Z_W, :] = jnp.zeros((Z_W - Z_KR - QK_ROPE, outs["w_in"].shape[1]), bf16)
        if "w_qb" in outs:
            for h in range(MLA_HEADS):
                outs["w_qb"][:, QK_PAD * h + QK_HEAD:QK_PAD * (h + 1)] = jnp.zeros((outs["w_qb"].shape[0], QK_PAD - QK_HEAD), bf16)
        for j in range(N_DEV):
            for wn, moves in _col_moves(j).items():
                if wn in outs:
                    for s0, s1, d0 in moves:
                        if wn == "w_in":
                            outs[wn][d0:d0 + s1 - s0, :] = ins[wn][j, s0:s1, :]
                        else:
                            outs[wn][:, d0:d0 + s1 - s0] = ins[wn][j, :, s0:s1]

    outs = pl.pallas_call(
        body, name=name, grid=(RELAYOUT_BLOCKS,), in_specs=_relayout_specs(names, True), out_specs=_relayout_specs(names, False),
        out_shape=[jax.ShapeDtypeStruct(_kernel_shape(wn), gathered[wn].dtype) for wn in names],
        compiler_params=_cp(("arbitrary",), VMEM_LIMIT),
    )(*[gathered[wn] for wn in names])
    return dict(zip(names, outs))


def _grads_out(sources, names, name):
    pieces = [(wn, start, arr) for wn in names for start, arr in sources[wn]]
    n_in = len(pieces)

    def body(*refs):
        outs = dict(zip(names, refs[n_in:]))

        def cols(wn, c0, c1):
            for (pn, start, arr), ref in zip(pieces, refs[:n_in]):
                if pn == wn and start <= c0 and c1 <= start + arr.shape[0 if wn == "w_in" else 1]:
                    return ref[c0 - start:c1 - start, :] if wn == "w_in" else ref[:, c0 - start:c1 - start]

        for j in range(N_DEV):
            for wn, moves in _col_moves(j).items():
                if wn in outs:
                    for s0, s1, d0 in moves:
                        if wn == "w_in":
                            outs[wn][j, s0:s1, :] = cols(wn, d0, d0 + s1 - s0).astype(bf16)
                        else:
                            outs[wn][j, :, s0:s1] = cols(wn, d0, d0 + s1 - s0).astype(bf16)

    def in_spec(wn, arr):
        if wn == "lb_param":
            return _acc_spec(arr.shape)
        if wn == "w_in":
            return pl.BlockSpec((arr.shape[0], arr.shape[1] // RELAYOUT_BLOCKS), lambda i: (0, i))
        return pl.BlockSpec((arr.shape[0] // RELAYOUT_BLOCKS, arr.shape[1]), lambda i: (i, 0))

    in_specs = [in_spec(wn, arr) for wn, _, arr in pieces]
    outs = pl.pallas_call(
        body, name=name, grid=(RELAYOUT_BLOCKS,), in_specs=in_specs, out_specs=_relayout_specs(names, True),
        out_shape=[jax.ShapeDtypeStruct((N_DEV, *BIG[wn]), bf16) for wn in names],
        compiler_params=_cp(("arbitrary",), VMEM_LIMIT),
    )(*[arr for _, _, arr in pieces])
    return dict(zip(names, outs))


def kernel(x, p, positions, g_mix, w_in, g_qa, g_kva, w_qb, w_kvb, g_qn, g_kn, lb_param, g_hgo, w_o, g_ffn, w_gate, w_up, w_down, g_ple, w_ple_gate, w_ple_proj, loss_target, m_g_mix, m_w_in, m_g_qa, m_g_kva, m_w_qb, m_w_kvb, m_g_qn, m_g_kn, m_lb_param, m_g_hgo, m_w_o, m_g_ffn, m_w_gate, m_w_up, m_w_down, m_g_ple, m_w_ple_gate, m_w_ple_proj, v_g_mix, v_w_in, v_g_qa, v_g_kva, v_w_qb, v_w_kvb, v_g_qn, v_g_kn, v_lb_param, v_g_hgo, v_w_o, v_g_ffn, v_w_gate, v_w_up, v_w_down, v_g_ple, v_w_ple_gate, v_w_ple_proj):
    w_all = dict(g_mix=g_mix, g_qa=g_qa, g_kva=g_kva, g_qn=g_qn, g_kn=g_kn, g_hgo=g_hgo, g_ffn=g_ffn, g_ple=g_ple,
                 w_in=w_in, w_qb=w_qb, w_kvb=w_kvb, w_o=w_o, w_gate=w_gate, w_up=w_up, w_down=w_down,
                 w_ple_gate=w_ple_gate, w_ple_proj=w_ple_proj, lb_param=lb_param)
    m_all = dict(g_mix=m_g_mix, g_qa=m_g_qa, g_kva=m_g_kva, g_qn=m_g_qn, g_kn=m_g_kn, g_hgo=m_g_hgo, g_ffn=m_g_ffn,
                 g_ple=m_g_ple, w_in=m_w_in, w_qb=m_w_qb, w_kvb=m_w_kvb, w_o=m_w_o, w_gate=m_w_gate, w_up=m_w_up,
                 w_down=m_w_down, w_ple_gate=m_w_ple_gate, w_ple_proj=m_w_ple_proj, lb_param=m_lb_param)
    v_all = dict(g_mix=v_g_mix, g_qa=v_g_qa, g_kva=v_g_kva, g_qn=v_g_qn, g_kn=v_g_kn, g_hgo=v_g_hgo, g_ffn=v_g_ffn,
                 g_ple=v_g_ple, w_in=v_w_in, w_qb=v_w_qb, w_kvb=v_w_kvb, w_o=v_w_o, w_gate=v_w_gate, w_up=v_w_up,
                 w_down=v_w_down, w_ple_gate=v_w_ple_gate, w_ple_proj=v_w_ple_proj, lb_param=v_lb_param)
    me_idx = jnp.stack([_me()]).astype(jnp.int32)
    x, p, positions, target = x[0], p[0, 0], positions[0], loss_target[0]
    s = x.shape[0]
    tm, tm_ffn, tq_f, tq_b = min(512, s), min(1024, s), min(2048, s), min(1024, s)
    g_mix, g_qa, g_kva, g_qn, g_kn, g_hgo, g_ffn, g_ple = (w_all[n].reshape(1, -1) for n in SMALL)
    g_qn_p, g_kn_p = _pad256(g_qn), _pad256(g_kn)
    cosb, sina, sinb = _rope_tables(positions)
    as_shard = lambda n, a: a[0].T if n in TRANSPOSED else a.reshape(BIG[n])
    shard = lambda n: as_shard(n, w_all[n])

    first = _all_gather([shard(n) for n in FIRST], [f32 if n == "lb_param" else bf16 for n in FIRST], "ag_first")
    lands = _cast_to_slot([shard(n) for n in SECOND], me_idx, first[0])
    ag2, token = _exchange_start([], lands, "ag_second_start")
    wk = _weights_in(dict(zip(FIRST, first)), FIRST, "weights_in_first")
    wz, wqb, wkvb, lb4 = (wk[n] for n in FIRST)

    h1, z = _fwd_in(x, g_mix, wz, tm)
    q, k, v = _fwd_mla_proj(z, cosb + token[0, 0], sina, sinb, g_qa, g_kva, wqb, wkvb, g_qn_p, g_kn_p, tm)
    a, a32 = _fwd_attn(q, k, v, tq_f)
    o, gla_b, gla_states = _fwd_gla(z, lb4)

    second = dict(zip(SECOND, _exchange_wait(ag2, [a, o], "ag_second_wait")[1]))
    w_pp = _weights_in(second, ("w_ple_proj",), "weights_in_second")["w_ple_proj"]
    w_o, w_down, w_pg, w_gate, w_up = (second[n].reshape(N_DEV * BIG[n][0], BIG[n][1]) for n in ROW_SHARDED + ("w_gate", "w_up"))

    x2, cat = _fwd_mix(a, o, z, g_hgo, x, w_o, tm)
    x3, gp, up = _fwd_ffn(x2, g_ffn, w_gate, w_up, w_down, tm)
    d3, h3, dpre, dpp, dg_ple, loss_tile = _ple_loss_fwd_bwd(x3, g_ple, w_pg, p, w_pp, target, tm)
    act, dgp, dup = _bwd_ffn_hidden(d3, gp, up, w_down, tm, D_FF // 2)
    d2, h2, dg_ffn = _bwd_ffn_in(d3, x2, dgp, dup, g_ffn, w_gate, w_up, tm)

    gw_gate, gw_up = _mm_tn_many(h2, [dgp, dup], "dw_gate_up", 512, transposed=True)
    blocks = _grads_out({"w_ple_proj": [(0, _mm_tn(p, dpp, "dw_ple_proj"))]}, ("w_ple_proj",), "grads_out_second")
    row_grads = {"w_o": _mm_tn(cat, d2, "dw_o"), "w_down": _mm_tn(act, d3, "dw_down"), "w_ple_gate": _mm_tn(h3, dpre, "dw_ple_gate"),
                 "w_gate": gw_gate, "w_up": gw_up}
    blocks.update({n: g.reshape(N_DEV, *BIG[n]) for n, g in row_grads.items()})
    empty = lambda names: [lax.empty((N_PEERS, *BIG[n]), bf16) for n in names]
    rs2, token = _exchange_start([blocks[n] for n in SECOND], empty(SECOND), "rs_second_start")

    da, do, dz_hg, dg_hgo = _bwd_mix(d2, w_o, o, z, g_hgo + token[0, 0], tm)
    dz_hq, dz_hff, dz_hfb, dz_hi, dlb4 = _bwd_gla(z, lb4, do, gla_b, gla_states)
    dq, dk, dv = _bwd_attn(q, k, v, da, a32, tq_b)
    dz_mla, cqn, ckvn, dq0, dkv0, dg_qa, dg_kva, dg_qn, dg_kn = _bwd_mla_proj(
        z, dq, dk, dv, cosb, sina, sinb, g_qa, g_kva, wqb, wkvb, g_qn_p, g_kn_p, tm)

    gz = list(zip((Z_HQ, Z_HFF, Z_HFB, Z_HI, Z_HG, Z_CQ),
                  _mm_tn_many(h1, [dz_hq, dz_hff, dz_hfb, dz_hi, dz_hg, dz_mla], "dw_in", 1024, transposed=True)))
    blocks1 = _grads_out({"w_in": gz, "w_qb": [(0, _mm_tn(cqn, dq0, "dw_qb"))], "w_kvb": [(0, _mm_tn(ckvn, dkv0, "dw_kvb"))],
                          "lb_param": [(0, dlb4)]}, FIRST, "grads_out_first")
    rs1, token = _exchange_start([blocks1[n] for n in FIRST], empty(FIRST), "rs_first_start")

    result = {}

    def adam(names, lands, src, n_blocks, after=()):
        flipped = TRANSPOSED
        given = lambda arrs: [arrs[n][0].T if n in flipped else arrs[n] for n in names]
        outs = _adam_shards(me_idx, [src[n] for n in names], lands, given(w_all), given(m_all), given(v_all), n_blocks,
                            "adamw_" + names[0], after)
        for n, o in zip(names, outs):
            result[n] = [t.T[None] for t in o] if n in flipped else o
        return outs[0][0]

    blocks2, lands2 = (dict(zip(SECOND, arrs)) for arrs in _exchange_wait(rs2, [token], "rs_second_wait"))
    by2 = ("w_down", "w_gate", "w_up")
    by8 = tuple(n for n in SECOND if n not in by2)
    done = [adam(by8, [lands2[n] for n in by8], blocks2, 8), adam(by2, [lands2[n] for n in by2], blocks2, 2)]

    segments = [(dz_hq, 512, 0, Z_HQ // 512), (dz_hff, 512, 0, Z_HFF // 512), (dz_hfb, 512, 0, Z_HFB // 512),
                (dz_hi, 512, 0, Z_HI // 512), (dz_hg, 512, 0, Z_HG // 512), (dz_mla, 640, 0, Z_CQ // 640)]
    grad_x, dg_mix = _bwd_in(segments, wz, x, g_mix + token[0, 0], d2, tm)
    dgains = (dg_mix, dg_qa, dg_kva, dg_qn, dg_kn, dg_hgo, dg_ffn, dg_ple)

    vec = jnp.concatenate(list(dgains) + [loss_tile[0:1]], axis=1)
    parts = _all_gather([vec], [f32], "ag_gains")[0]
    outs, loss_row = _adam_gains(parts, [w_all[n] for n in SMALL], [m_all[n] for n in SMALL], [v_all[n] for n in SMALL])
    result.update(zip(SMALL, outs))

    blocks1, lands1 = _exchange_wait(rs1, [grad_x, loss_row, *done], "rs_first_wait")
    adam(FIRST, lands1, dict(zip(FIRST, blocks1)), 8)

    order = ("g_mix", "w_in", "g_qa", "g_kva", "w_qb", "w_kvb", "g_qn", "g_kn", "lb_param", "g_hgo", "w_o", "g_ffn",
             "w_gate", "w_up", "w_down", "g_ple", "w_ple_gate", "w_ple_proj")
    return (loss_row[0, 0], grad_x[None], *[result[n][k] for k in range(4) for n in order])
```

```python
import functools
import math

import jax
import jax.numpy as jnp
from jax import lax
from jax.experimental import pallas as pl
from jax.experimental.pallas import tpu as pltpu

f32 = jnp.float32
bf16 = jnp.bfloat16

N_DEV = 8
MLA_HEADS = 4
QK_NOPE = 128
QK_ROPE = 64
QK_HEAD = QK_NOPE + QK_ROPE
QK_PAD = 256
V_HEAD = 128
Q_LORA = 256
KV_LORA = 256
HG_HEADS = 4
CHUNK = 64
D_FF = 2816
PLE_DIM = 256
ROPE_THETA = 10000.0
EPS = 1e-6
ATTN_SCALE = QK_HEAD ** -0.5
LOG2_E = math.log2(math.e)
ATTN_SUB_ROWS = 256
Z_HQ, Z_HFF, Z_HFB, Z_HI, Z_HG, Z_CQ, Z_CKV, Z_KR, Z_W = 0, 512, 1024, 1536, 2048, 2560, 2816, 3072, 3200

ADAM_LR, ADAM_B1, ADAM_B2, ADAM_EPS, ADAM_WD, ADAM_STEP = 0.001, 0.9, 0.999, 1e-08, 0.01, 10

LANES = 128
BIG = {"w_in": (392, 1024), "w_qb": (256, 96), "w_kvb": (256, 128), "w_o": (128, 1024), "w_gate": (352, 1024),
       "w_up": (352, 1024), "w_down": (352, 1024), "w_ple_gate": (128, 1024), "w_ple_proj": (256, 128),
       "lb_param": (4, 64)}
TRANSPOSED = ("w_gate", "w_up", "w_in")
SMALL = {"g_mix": (0, 1024), "g_qa": (1024, 256), "g_kva": (1280, 256), "g_qn": (1536, 192), "g_kn": (1792, 192),
         "g_hgo": (2048, 512), "g_ffn": (2560, 1024), "g_ple": (3584, 1024)}
LOSS_OFF = 4608
GAIN_VEC = LOSS_OFF + LANES
Z_SEGMENTS = ((0, 256, Z_CQ), (256, 512, Z_CKV), (512, 576, Z_KR), (576, 1088, Z_HQ), (1088, 1600, Z_HFF),
              (1600, 2112, Z_HFB), (2112, 2624, Z_HI), (2624, 3136, Z_HG))

VMEM_LIMIT = 56 * 1024 * 1024
MESH = pl.DeviceIdType.MESH


def _cp(sem=None, vmem=None):
    return pltpu.CompilerParams(dimension_semantics=sem, vmem_limit_bytes=vmem)


def _const_spec(shape):
    nd = len(shape)
    return pl.BlockSpec(shape, lambda *_: (0,) * nd, pipeline_mode=pl.Buffered(1))


def _acc_spec(shape):
    nd = len(shape)
    return pl.BlockSpec(shape, lambda *_: (0,) * nd)


def _sigmoid(x):
    return jax.nn.sigmoid(x)


def _dot(a, b):
    return jnp.dot(a, b, preferred_element_type=f32)


def _dot_nt(a, b):
    return lax.dot_general(a, b, (((1,), (1,)), ((), ())), preferred_element_type=f32)


def _dot_tn(a, b):
    return lax.dot_general(a, b, (((0,), (0,)), ((), ())), preferred_element_type=f32)


def _rms_fwd(x, g, width):
    r = lax.rsqrt(jnp.sum(x * x, axis=-1, keepdims=True) * (1.0 / width) + EPS)
    return x * r * g, r


def _rms_bwd(dy, x, r, g, width):
    u = dy * g
    dx = r * u - x * (r * r * r) * (jnp.sum(u * x, axis=-1, keepdims=True) * (1.0 / width))
    return dx, dy * x * r


class _Both:
    def __init__(self, *copies):
        self.copies = copies

    def start(self):
        for cp in self.copies:
            cp.start()

    def wait(self):
        for cp in self.copies:
            cp.wait()


def _rope(b, c, sa, sb):
    return b * c + pltpu.roll(b, 32, 1) * sa + pltpu.roll(b, 96, 1) * sb


def _all_gather(shards, dtypes, name):
    n = len(shards)

    def body(*refs):
        in_refs, out_refs, stage = refs[:n], refs[n:2 * n], refs[2 * n:3 * n]
        send_sems, recv_sems, local_sems = refs[3 * n:]
        for w in range(n):
            stage[w][...] = in_refs[w][...].astype(stage[w].dtype)
        x, y, c = lax.axis_index("x"), lax.axis_index("y"), lax.axis_index("c")
        me, sibling = (x, y, c), (x, y, 1 - c)
        chips = [(1 - x, y), (x, 1 - y), (1 - x, 1 - y)]

        def slot(w, px, py, pc):
            return out_refs[w].at[4 * px + 2 * py + pc]

        def copy(w, k, block, to, src=None):
            return pltpu.make_async_remote_copy(
                src_ref=slot(w, *block) if src is None else src, dst_ref=slot(w, *block),
                send_sem=send_sems.at[w, k], recv_sem=recv_sems.at[w, k], device_id=to, device_id_type=MESH)

        first = []
        for j, chip in enumerate(chips):
            first += [copy(w, 1 + j, me, (*chip, c), src=stage[w]) for w in range(n)]
        first += [copy(w, 0, me, sibling, src=stage[w]) for w in range(n)]
        mine = [pltpu.make_async_copy(stage[w], slot(w, *me), local_sems.at[w]) for w in range(n)]
        for cp in first + mine:
            cp.start()
        passed = []
        for j, chip in enumerate(chips):
            for w in range(n):
                copy(w, 1 + j, (*chip, c), me).wait_recv()
                passed.append(copy(w, 4 + j, (*chip, c), sibling))
                passed[-1].start()
        for w in range(n):
            copy(w, 0, sibling, me).wait_recv()
        for j, chip in enumerate(chips):
            for w in range(n):
                copy(w, 4 + j, (*chip, 1 - c), me).wait_recv()
        for cp in first + passed:
            cp.wait_send()
        for cp in mine:
            cp.wait()

    return pl.pallas_call(
        body, name=name,
        out_shape=[jax.ShapeDtypeStruct((N_DEV, *s.shape), dt) for s, dt in zip(shards, dtypes)],
        in_specs=[pl.BlockSpec(memory_space=pltpu.VMEM)] * n,
        out_specs=[pl.BlockSpec(memory_space=pl.ANY)] * n,
        scratch_shapes=[pltpu.VMEM(s.shape, dt) for s, dt in zip(shards, dtypes)]
        + [pltpu.SemaphoreType.DMA((n, 7)), pltpu.SemaphoreType.DMA((n, 7)), pltpu.SemaphoreType.DMA((n,))],
        compiler_params=_cp(None, VMEM_LIMIT),
    )(*shards)


N_PEERS = N_DEV - 1
HBM_SPEC = pl.BlockSpec(memory_space=pltpu.HBM)
SEM_SPEC = pl.BlockSpec(memory_space=pltpu.SEMAPHORE)
DATAFLOW = pltpu.SideEffectType.DATAFLOW_SIDE_EFFECTING


def _me():
    return 4 * lax.axis_index("x") + 2 * lax.axis_index("y") + lax.axis_index("c")


def _peer(k):
    x, y, c = lax.axis_index("x"), lax.axis_index("y"), lax.axis_index("c")
    px = 1 - x if k & 4 else x
    py = 1 - y if k & 2 else y
    pc = 1 - c if k & 1 else c
    return (px, py, pc), 4 * px + 2 * py + pc


def _exchange_copies(src_refs, land_refs, send_sems, recv_sems, gather):
    cps = []
    me = _me()
    for k in range(1, N_DEV):
        peer, peer_idx = _peer(k)
        for w, land in enumerate(land_refs):
            src = land.at[me] if gather else src_refs[w].at[peer_idx]
            dst = land.at[me] if gather else land.at[k - 1]
            cps.append(pltpu.make_async_remote_copy(
                src_ref=src, dst_ref=dst, send_sem=send_sems.at[N_PEERS * w + k - 1], recv_sem=recv_sems.at[N_PEERS * w + k - 1],
                device_id=peer, device_id_type=MESH))
    return cps


def _exchange_start(srcs, lands, name):
    n_src, n = len(srcs), len(lands)

    def body(*refs):
        src_refs, land_refs = refs[:n_src], refs[n_src:n_src + n]
        send_sems, recv_sems = refs[n_src + n], refs[n_src + n + 1]
        token = refs[-1]
        for cp in _exchange_copies(src_refs, land_refs, send_sems, recv_sems, gather=not n_src):
            cp.start()
        token[...] = jnp.zeros_like(token)

    arrays = [pltpu.with_memory_space_constraint(a, pltpu.HBM) for a in (*srcs, *lands)]
    outs = pl.pallas_call(
        body, name=name,
        out_shape=(pltpu.SemaphoreType.DMA((n * N_PEERS,)), pltpu.SemaphoreType.DMA((n * N_PEERS,)),
                   *[pltpu.HBM(a.shape, a.dtype) for a in arrays], jax.ShapeDtypeStruct((8, LANES), f32)),
        in_specs=[HBM_SPEC] * len(arrays),
        out_specs=(SEM_SPEC, SEM_SPEC, *[HBM_SPEC] * len(arrays), pl.BlockSpec(memory_space=pltpu.VMEM)),
        input_output_aliases={i: 2 + i for i in range(len(arrays))},
        compiler_params=pltpu.CompilerParams(has_side_effects=DATAFLOW),
    )(*arrays)
    return (outs[0], outs[1], outs[2:2 + n_src], outs[2 + n_src:2 + n_src + n]), outs[-1]


def _exchange_wait(state, after, name):
    send_sems, recv_sems, srcs, lands = state
    n_src, n = len(srcs), len(lands)

    def body(*refs):
        src_refs, land_refs = refs[:n_src], refs[n_src:n_src + n]
        send_ref, recv_ref = refs[n_src + n], refs[n_src + n + 1]
        for cp in _exchange_copies(src_refs, land_refs, send_ref, recv_ref, gather=not n_src):
            cp.wait_send()
            cp.wait_recv()

    arrays = (*srcs, *lands)
    outs = pl.pallas_call(
        body, name=name,
        out_shape=tuple(pltpu.HBM(a.shape, a.dtype) for a in arrays),
        in_specs=[HBM_SPEC] * len(arrays) + [SEM_SPEC, SEM_SPEC] + [pl.BlockSpec(memory_space=pl.ANY)] * len(after),
        out_specs=tuple([HBM_SPEC] * len(arrays)),
        input_output_aliases={i: i for i in range(len(arrays))},
        compiler_params=pltpu.CompilerParams(has_side_effects=DATAFLOW),
    )(*arrays, send_sems, recv_sems, *after)
    return outs[:n_src], outs[n_src:]


def _cast_to_slot(shards, me_idx, after):
    n = len(shards)

    def body(i_ref, *refs):
        for w in range(n):
            refs[n + 1 + w][...] = refs[w][...].astype(bf16)

    return pl.pallas_call(
        body, name="cast_to_slot",
        grid_spec=pltpu.PrefetchScalarGridSpec(
            num_scalar_prefetch=1, grid=(1,),
            in_specs=[pl.BlockSpec(s.shape, lambda i, m: (0, 0)) for s in shards] + [pl.BlockSpec(memory_space=pl.ANY)],
            out_specs=[pl.BlockSpec((None, *s.shape), lambda i, m: (m[0], 0, 0)) for s in shards]),
        out_shape=[jax.ShapeDtypeStruct((N_DEV, *s.shape), bf16) for s in shards],
        compiler_params=_cp(("arbitrary",), VMEM_LIMIT),
    )(me_idx, *shards, after)


def _row_block(rows, n_blocks):
    return (rows // n_blocks, True) if rows % (16 * n_blocks) == 0 else (rows, False)


def _adam_math(w, g, m, v):
    m = ADAM_B1 * m + (1.0 - ADAM_B1) * g
    v = ADAM_B2 * v + (1.0 - ADAM_B2) * (g * g)
    m_hat = m / (1.0 - ADAM_B1 ** ADAM_STEP)
    v_hat = v / (1.0 - ADAM_B2 ** ADAM_STEP)
    delta = -ADAM_LR * (m_hat / (jnp.sqrt(v_hat) + ADAM_EPS) + ADAM_WD * w)
    return delta, m, v


def _adam_shards(me_idx, blocks, lands, ws, ms, vs, n_blocks, name, after=()):
    n = len(blocks)

    def body(i_ref, *refs):
        ins, outs = refs[:5 * n], refs[5 * n + len(after):]
        for w in range(n):
            g_ref, b_ref, w_ref, m_ref, v_ref = (ins[t * n + w] for t in range(5))
            g = g_ref[...].astype(f32)
            for k in range(N_PEERS):
                g = g + b_ref[k].astype(f32)
            if len(w_ref.shape) == 2:
                pieces = [(slice(None), g)]
            else:
                pieces = [(a, g[2 * a:2 * a + 2]) for a in range(2)]
            for at, gp in pieces:
                vals = (gp,) + _adam_math(w_ref[at], gp, m_ref[at], v_ref[at])
                for t, val in enumerate(vals):
                    outs[4 * w + t][at] = val

    specs = [[] for _ in range(5)]
    out_specs, out_shape = [], []
    for g, wt in zip(blocks, ws):
        rows, cols = g.shape[1:]
        rb, cut = _row_block(rows, n_blocks)
        if not cut and wt.ndim == 2 and cols % (LANES * n_blocks) == 0:
            cb = cols // n_blocks
            specs[0].append(pl.BlockSpec((None, rows, cb), lambda i, s: (s[0], 0, i)))
            specs[1].append(pl.BlockSpec((N_PEERS, rows, cb), lambda i, s: (0, 0, i)))
            shard = pl.BlockSpec((rows, cb), lambda i, s: (0, i))
            for t in (2, 3, 4):
                specs[t].append(shard)
            out_specs += [shard] * 4
            out_shape += [jax.ShapeDtypeStruct(wt.shape, f32)] * 4
            continue
        specs[0].append(pl.BlockSpec((None, rb, cols), functools.partial(lambda i, s, cut: (s[0], i if cut else 0, 0), cut=cut)))
        specs[1].append(pl.BlockSpec((N_PEERS, rb, cols), functools.partial(lambda i, s, cut: (0, i if cut else 0, 0), cut=cut)))
        if wt.ndim == 2:
            shard = pl.BlockSpec((rb, cols), functools.partial(lambda i, s, cut: (i if cut else 0, 0), cut=cut))
        elif wt.shape[0] == 1:
            shard = pl.BlockSpec((None, rb, cols), functools.partial(lambda i, s, cut: (0, i if cut else 0, 0), cut=cut))
        else:
            shard = pl.BlockSpec(wt.shape, functools.partial(lambda i, s, nd: (0,) * nd, nd=wt.ndim))
        for t in (2, 3, 4):
            specs[t].append(shard)
        out_specs += [shard] * 4
        out_shape += [jax.ShapeDtypeStruct(wt.shape, f32)] * 4
    outs = pl.pallas_call(
        body, name=name,
        grid_spec=pltpu.PrefetchScalarGridSpec(
            num_scalar_prefetch=1, grid=(n_blocks,), in_specs=sum(specs, []) + [pl.BlockSpec(memory_space=pl.ANY)] * len(after),
            out_specs=out_specs),
        out_shape=out_shape,
        compiler_params=_cp(("arbitrary",), VMEM_LIMIT),
    )(me_idx, *blocks, *lands, *ws, *ms, *vs, *after)
    return [outs[4 * w:4 * w + 4] for w in range(n)]


def _adam_gains(parts, ws, ms, vs):
    n = len(ws)

    def body(p_ref, *refs):
        ins, outs = refs[:3 * n], refs[3 * n:]
        g_all = p_ref[0]
        for k in range(1, N_DEV):
            g_all = g_all + p_ref[k]
        for w, (off, lanes) in enumerate(SMALL.values()):
            w_ref, m_ref, v_ref = ins[w], ins[n + w], ins[2 * n + w]
            if len(w_ref.shape) == 2:
                pieces = [(slice(None), off, lanes)]
            else:
                pieces = [((slice(None), h), off + LANES * h, LANES) for h in range(w_ref.shape[1])]
            for at, o, ln in pieces:
                g = g_all[:, o:o + ln]
                vals = (g,) + _adam_math(w_ref[at], g, m_ref[at], v_ref[at])
                for t, val in enumerate(vals):
                    outs[4 * w + t][at] = val
        outs[4 * n][...] = g_all[:, LOSS_OFF:LOSS_OFF + LANES]

    out_shape = sum([[jax.ShapeDtypeStruct(w.shape, f32)] * 4 for w in ws], []) + [jax.ShapeDtypeStruct((1, LANES), f32)]
    outs = pl.pallas_call(body, name="adamw_gains", out_shape=out_shape)(parts, *ws, *ms, *vs)
    return [outs[4 * w:4 * w + 4] for w in range(n)], outs[4 * n]


def _fwd_in(x, g_mix, wz, tm):
    s, d = x.shape

    def body(x_ref, g_ref, w_ref, h_ref, z_ref):
        h, _ = _rms_fwd(x_ref[...], g_ref[...], d)
        hb = h.astype(bf16)
        h_ref[...] = hb
        z_ref[...] = _dot_nt(hb, w_ref[...])

    return pl.pallas_call(
        body, name="fwd_in", grid=(s // tm,),
        in_specs=[pl.BlockSpec((tm, d), lambda i: (i, 0)), _const_spec((1, d)), _const_spec((Z_W, d))],
        out_specs=[pl.BlockSpec((tm, d), lambda i: (i, 0)), pl.BlockSpec((tm, Z_W), lambda i: (i, 0))],
        out_shape=[jax.ShapeDtypeStruct((s, d), bf16), jax.ShapeDtypeStruct((s, Z_W), f32)],
        compiler_params=_cp(("parallel",), VMEM_LIMIT),
    )(x, g_mix, wz)


def _mla_qk_fwd(cq, ckv, g_qa, g_kva, wqb, wkvb):
    cqn, rq = _rms_fwd(cq, g_qa, Q_LORA)
    ckvn, rkv = _rms_fwd(ckv, g_kva, KV_LORA)
    cqn_b, ckvn_b = cqn.astype(bf16), ckvn.astype(bf16)
    q0 = _dot(cqn_b, wqb)
    kv0 = _dot(ckvn_b, wkvb)
    return cqn_b, rq, ckvn_b, rkv, q0, kv0


def _fwd_mla_proj(z, cosb, sina, sinb, g_qa, g_kva, wqb, wkvb, g_qn, g_kn, tm):
    s = z.shape[0]
    hh = MLA_HEADS

    def body(cq_ref, ckv_ref, kr_ref, c_ref, sa_ref, sb_ref, gqa_ref, gkva_ref, wqb_ref, wkvb_ref, gqn_ref, gkn_ref,
             q_ref, k_ref, v_ref):
        _, _, _, _, q0, kv0 = _mla_qk_fwd(cq_ref[...], ckv_ref[...], gqa_ref[...], gkva_ref[...], wqb_ref[...], wkvb_ref[...])
        kr = kr_ref[...]
        c, sa, sb = c_ref[...], sa_ref[...], sb_ref[...]
        gqn, gkn = gqn_ref[...], gkn_ref[...]
        kr_sq = jnp.sum(kr * kr, axis=-1, keepdims=True)
        for h in range(hh):
            qh = q0[:, QK_PAD * h:QK_PAD * (h + 1)]
            qn, _ = _rms_fwd(qh, gqn, QK_HEAD)
            q_ref[h, :, 0:128] = qn[:, 0:128].astype(bf16)
            q_ref[h, :, 128:256] = _rope(qn[:, 128:256], c, sa, sb).astype(bf16)
            kn_ = kv0[:, 256 * h:256 * h + 128]
            rk = lax.rsqrt((jnp.sum(kn_ * kn_, axis=-1, keepdims=True) + kr_sq) * (1.0 / QK_HEAD) + EPS)
            k_ref[h, :, 0:128] = (kn_ * rk * gkn[:, 0:128]).astype(bf16)
            k_ref[h, :, 128:256] = _rope(kr * rk * gkn[:, 128:256], c, sa, sb).astype(bf16)
            v_ref[h] = kv0[:, 256 * h + 128:256 * h + 256].astype(bf16)

    row128 = pl.BlockSpec((tm, 128), lambda i: (i, 0))
    return pl.pallas_call(
        body, name="fwd_mla_proj", grid=(s // tm,),
        in_specs=[pl.BlockSpec((tm, 256), lambda i: (i, Z_CQ // 256)), pl.BlockSpec((tm, 256), lambda i: (i, Z_CKV // 256)),
                  pl.BlockSpec((tm, 128), lambda i: (i, Z_KR // 128)), row128, row128, row128,
                  _const_spec((1, 256)), _const_spec((1, 256)), _const_spec((256, 1024)), _const_spec((256, 1024)),
                  _const_spec((1, 256)), _const_spec((1, 256))],
        out_specs=[pl.BlockSpec((hh, tm, QK_PAD), lambda i: (0, i, 0)), pl.BlockSpec((hh, tm, QK_PAD), lambda i: (0, i, 0)),
                   pl.BlockSpec((hh, tm, V_HEAD), lambda i: (0, i, 0))],
        out_shape=[jax.ShapeDtypeStruct((hh, s, QK_PAD), bf16), jax.ShapeDtypeStruct((hh, s, QK_PAD), bf16),
                   jax.ShapeDtypeStruct((hh, s, V_HEAD), bf16)],
        compiler_params=_cp(("parallel",), VMEM_LIMIT),
    )(z, z, z, cosb, sina, sinb, g_qa, g_kva, wqb, wkvb, g_qn, g_kn)


def _fwd_attn(q, k, v, tq, after):
    hh, s, _ = q.shape

    n_sub = max(1, tq // ATTN_SUB_ROWS)

    def body(q_ref, k_ref, v_ref, after_ref, o_ref, o32_ref):
        for t in range(n_sub):
            rows = slice(t * (tq // n_sub), (t + 1) * (tq // n_sub))
            sc = _dot_nt(q_ref[rows, :], k_ref[...])
            p = jnp.exp2((sc - jnp.max(sc, axis=-1, keepdims=True)) * (ATTN_SCALE * LOG2_E))
            l = jnp.sum(p, axis=-1, keepdims=True)
            o = _dot(p.astype(bf16), v_ref[...]) * (1.0 / l)
            o_ref[rows, :] = o.astype(bf16)
            o32_ref[rows, :] = o

    out = pl.BlockSpec((tq, V_HEAD), lambda h, i: (i, h))
    return pl.pallas_call(
        body, name="fwd_attn", grid=(hh, s // tq),
        in_specs=[pl.BlockSpec((None, tq, QK_PAD), lambda h, i: (h, i, 0)),
                  pl.BlockSpec((None, s, QK_PAD), lambda h, i: (h, 0, 0)),
                  pl.BlockSpec((None, s, V_HEAD), lambda h, i: (h, 0, 0)), pl.BlockSpec(memory_space=pl.ANY)],
        out_specs=[out, out],
        out_shape=[jax.ShapeDtypeStruct((s, hh * V_HEAD), bf16), jax.ShapeDtypeStruct((s, hh * V_HEAD), f32)],
        compiler_params=_cp(("parallel", "parallel"), VMEM_LIMIT),
    )(q, k, v, after)


def _split3(x):
    hi = x.astype(bf16)
    r1 = x - hi.astype(f32)
    mid = r1.astype(bf16)
    lo = (r1 - mid.astype(f32)).astype(bf16)
    return jnp.concatenate([hi, mid, lo], axis=-1)


def _tri_sum(tri, x):
    y = _dot(tri, _split3(x))
    return y[:, 0:128] + y[:, 128:256] + y[:, 256:384]


GLA_GROUP = 4
GLA_ROWS = GLA_GROUP * CHUNK
GLA_HEADS_PER_STEP = 2


def _gla_masks(rev):
    row = lax.broadcasted_iota(jnp.int32, (GLA_ROWS, GLA_ROWS), 0)
    col = lax.broadcasted_iota(jnp.int32, (GLA_ROWS, GLA_ROWS), 1)
    shift = CHUNK.bit_length() - 1
    same = (jnp.right_shift(row, shift) == jnp.right_shift(col, shift)).astype(f32)
    lower, upper = (row >= col).astype(f32) * same, (row <= col).astype(f32) * same
    keep, keep_t = (upper, lower) if rev else (lower, upper)
    chunk_of = jnp.right_shift(lax.broadcasted_iota(jnp.int32, (GLA_ROWS, 1), 0), shift)
    return keep, keep.astype(bf16), keep_t.astype(bf16), [(chunk_of == c).astype(f32) for c in range(GLA_GROUP)]


def _gla_gates(hq, hf, lower):
    sg = _sigmoid(hf)
    f = lower + (1.0 - lower) * sg
    return hq * _sigmoid(hq), 1.0 - f, jnp.log(f), f, sg


def _gla_last_mid(b, rev):
    b3 = b.reshape(GLA_GROUP, CHUNK, 128)
    last, mid = (0, CHUNK // 2) if rev else (CHUNK - 1, CHUNK // 2 - 1)
    return b3[:, last:last + 1, :], b3[:, mid:mid + 1, :]


def _gla_per_row(per_chunk):
    return jnp.broadcast_to(per_chunk, (GLA_GROUP, CHUNK, 128)).reshape(GLA_ROWS, 128)


def _gla_block_diag(x, row_masks):
    return jnp.concatenate([(x * m).astype(bf16) for m in row_masks], axis=-1)


def _gla_diag(y):
    return jnp.concatenate([y[CHUNK * c:CHUNK * (c + 1), 128 * c:128 * (c + 1)] for c in range(GLA_GROUP)], axis=0)


def _gla_rows(n, n_groups, rev):
    ne = n_groups - 1 - n if rev else n
    return pl.ds(pl.multiple_of(ne * GLA_ROWS, GLA_ROWS), GLA_ROWS), ne * GLA_GROUP


def _gla_scan_order(rev):
    return tuple(reversed(range(GLA_GROUP))) if rev else tuple(range(GLA_GROUP))


def _fwd_gla(z, lb4):
    s = z.shape[0]
    n_groups = s // GLA_ROWS
    assert n_groups % 2 == 0
    hp = GLA_HEADS_PER_STEP
    chains = [(hh, rev) for hh in range(hp) for rev in (False, True)]

    def body(hq_ref, hff_ref, hfb_ref, hi_ref, lb_ref, o_ref, b_ref, states_ref, st_ref, stage_ref, b_stage, sems):
        st_ref[...] = jnp.zeros_like(st_ref)
        masks = {rev: _gla_masks(rev) for rev in (False, True)}
        lowers = [_sigmoid(lb_ref[int(rev):int(rev) + 1, 128 * hh:128 * (hh + 1)]
                           - lb_ref[2 + int(rev):3 + int(rev), 128 * hh:128 * (hh + 1)]) for hh, rev in chains]

        def states_out(slot, ci, chunk0):
            hh, rev = chains[ci]
            head = pl.program_id(0) * hp + hh
            rows = pl.ds(pl.multiple_of(chunk0 * CHUNK, GLA_ROWS), GLA_ROWS)
            return _Both(
                pltpu.make_async_copy(stage_ref.at[slot, ci], states_ref.at[head, int(rev), pl.ds(chunk0, GLA_GROUP)],
                                      sems.at[slot, ci]),
                pltpu.make_async_copy(b_stage.at[slot, ci], b_ref.at[int(rev), rows, pl.ds(pl.multiple_of(head * 128, 128), 128)],
                                      sems.at[slot, len(chains) + ci]))

        def make_step(first):
            def step(n, carry):
                slot = n % 2

                @pl.when(n >= 2)
                def _():
                    for ci in range(len(chains)):
                        states_out(slot, ci, 0).wait()

                for ci, (hh, rev) in enumerate(chains):
                    cols = slice(128 * hh, 128 * (hh + 1))
                    rows, chunk0 = _gla_rows(n, n_groups, rev)
                    maskf, tri, _, row_masks = masks[rev]
                    hf_ref = hfb_ref if rev else hff_ref
                    q, k, logf, _, _ = _gla_gates(hq_ref[rows, cols], hf_ref[rows, cols], lowers[ci])
                    vb = hi_ref[rows, cols].astype(bf16)
                    b = _tri_sum(tri, logf)
                    b_stage[slot, ci] = b
                    b_last3, b_mid3 = _gla_last_mid(b, rev)
                    b_last, b_mid = _gla_per_row(b_last3), _gla_per_row(b_mid3)
                    qi = (q * jnp.exp(b - b_mid)).astype(bf16)
                    ki = (k * jnp.exp(b_mid - b)).astype(bf16)
                    a = (_dot_nt(qi, ki) * maskf).astype(bf16)
                    kv = _dot_tn(vb, _gla_block_diag(k * jnp.exp(b_last - b), row_masks))
                    decay3 = jnp.exp(b_last3)
                    st = st_ref[ci]
                    before = [None] * GLA_GROUP
                    for c in _gla_scan_order(rev):
                        stage_ref[slot, ci, c] = st
                        before[c] = st.astype(bf16)
                        st = st * decay3[c] + kv[:, 128 * c:128 * (c + 1)]
                    st_ref[ci] = st
                    states_out(slot, ci, chunk0).start()
                    inter = _dot_nt((q * jnp.exp(b)).astype(bf16), jnp.concatenate(before, axis=0))
                    o = _dot(a, vb) + _gla_diag(inter)
                    if first:
                        o_ref[rows, cols] = o
                    else:
                        o_ref[rows, cols] += o
                return carry
            return step

        lax.fori_loop(0, n_groups // 2, make_step(True), 0)
        lax.fori_loop(n_groups // 2, n_groups, make_step(False), 0)
        for slot in range(2):
            for ci in range(len(chains)):
                states_out(slot, ci, 0).wait()

    w = 128 * hp
    col = lambda base: pl.BlockSpec((s, w), lambda h: (0, base // w + h))
    return pl.pallas_call(
        body, name="fwd_gla", grid=(HG_HEADS // hp,),
        in_specs=[col(Z_HQ), col(Z_HFF), col(Z_HFB), col(Z_HI), pl.BlockSpec((4, w), lambda h: (0, h))],
        out_specs=[pl.BlockSpec((s, w), lambda h: (0, h)), pl.BlockSpec(memory_space=pl.ANY), pl.BlockSpec(memory_space=pl.ANY)],
        out_shape=[jax.ShapeDtypeStruct((s, HG_HEADS * 128), f32), jax.ShapeDtypeStruct((2, s, HG_HEADS * 128), f32),
                   jax.ShapeDtypeStruct((HG_HEADS, 2, s // CHUNK, 128, 128), f32)],
        scratch_shapes=[pltpu.VMEM((len(chains), 128, 128), f32), pltpu.VMEM((2, len(chains), GLA_GROUP, 128, 128), f32),
                        pltpu.VMEM((2, len(chains), GLA_ROWS, 128), f32), pltpu.SemaphoreType.DMA((2, 2 * len(chains)))],
        compiler_params=_cp(("parallel",), VMEM_LIMIT),
    )(z, z, z, z, lb4)


def _hg_out(o, hg, g_hgo):
    outs, ons, rs = [], [], []
    for h in range(HG_HEADS):
        oh = o[:, 128 * h:128 * (h + 1)]
        on, r = _rms_fwd(oh, g_hgo[:, 128 * h:128 * (h + 1)], 128)
        ons.append(on)
        rs.append(r)
    on = jnp.concatenate(ons, axis=-1)
    sg = _sigmoid(hg)
    return on * (hg * sg), on, rs, sg


def _fwd_mix(a, o, z, g_hgo, x, w_o, tm):
    s, d = x.shape

    def body(a_ref, o_ref, hg_ref, g_ref, x_ref, w_ref, x2_ref, cat_ref):
        r, _, _, _ = _hg_out(o_ref[...], hg_ref[...], g_ref[...])
        cat = jnp.concatenate([a_ref[...], r.astype(bf16)], axis=-1)
        cat_ref[...] = cat
        x2_ref[...] = x_ref[...] + _dot(cat, w_ref[...])

    row512 = pl.BlockSpec((tm, 512), lambda i: (i, 0))
    rowd = pl.BlockSpec((tm, d), lambda i: (i, 0))
    return pl.pallas_call(
        body, name="fwd_mix", grid=(s // tm,),
        in_specs=[row512, row512, pl.BlockSpec((tm, 512), lambda i: (i, Z_HG // 512)), _const_spec((1, 512)), rowd,
                  _const_spec((d, d))],
        out_specs=[rowd, rowd],
        out_shape=[jax.ShapeDtypeStruct((s, d), f32), jax.ShapeDtypeStruct((s, d), bf16)],
        compiler_params=_cp(("parallel",), VMEM_LIMIT),
    )(a, o, z, g_hgo, x, w_o)


def _fwd_ffn(x2, g_ffn, w_gate, w_up, w_down, tm):
    s, d = x2.shape

    def body(x_ref, g_ref, wg_ref, wu_ref, wd_ref, x3_ref, gp_ref, up_ref):
        x = x_ref[...]
        h, _ = _rms_fwd(x, g_ref[...], d)
        hb = h.astype(bf16)
        gp = _dot_nt(hb, wg_ref[...])
        up = _dot_nt(hb, wu_ref[...])
        gp_ref[...] = gp.astype(bf16)
        up_ref[...] = up.astype(bf16)
        act = (gp * _sigmoid(gp) * up).astype(bf16)
        x3_ref[...] = x + _dot(act, wd_ref[...])

    rowd = pl.BlockSpec((tm, d), lambda i: (i, 0))
    rowf = pl.BlockSpec((tm, D_FF), lambda i: (i, 0))
    return pl.pallas_call(
        body, name="fwd_ffn", grid=(s // tm,),
        in_specs=[rowd, _const_spec((1, d)), _const_spec((D_FF, d)), _const_spec((D_FF, d)), _const_spec((D_FF, d))],
        out_specs=[rowd, rowf, rowf],
        out_shape=[jax.ShapeDtypeStruct((s, d), f32), jax.ShapeDtypeStruct((s, D_FF), bf16),
                   jax.ShapeDtypeStruct((s, D_FF), bf16)],
        compiler_params=_cp(("parallel",), VMEM_LIMIT),
    )(x2, g_ffn, w_gate, w_up, w_down)


def _ple_loss_fwd_bwd(x3, g_ple, w_pg, p, w_pp, target, tm):
    s, d = x3.shape

    def body(x_ref, g_ref, wg_ref, p_ref, wp_ref, t_ref, dx_ref, h_ref, dpre_ref, dpp_ref, dg_ref, loss_ref):
        @pl.when(pl.program_id(0) == 0)
        def _():
            dg_ref[...] = jnp.zeros_like(dg_ref)
            loss_ref[...] = jnp.zeros_like(loss_ref)

        x = x_ref[...]
        g = g_ref[...]
        h, r = _rms_fwd(x, g, d)
        hb = h.astype(bf16)
        gate = _sigmoid(_dot(hb, wg_ref[...]))
        pp = _dot(p_ref[...].astype(bf16), wp_ref[...])
        e = x + gate * pp - t_ref[...]
        loss_ref[...] += 0.5 * jnp.sum(e * e) * (1.0 / d)
        dy = e * (1.0 / d)
        dpre = (dy * pp * gate * (1.0 - gate)).astype(bf16)
        dx, dgx = _rms_bwd(_dot_nt(dpre, wg_ref[...]), x, r, g, d)
        dx_ref[...] = dy + dx
        dg_ref[...] += jnp.sum(dgx, axis=0, keepdims=True)
        h_ref[...] = hb
        dpre_ref[...] = dpre
        dpp_ref[...] = (dy * gate).astype(bf16)

    rowd = pl.BlockSpec((tm, d), lambda i: (i, 0))
    return pl.pallas_call(
        body, name="ple_loss_fwd_bwd", grid=(s // tm,),
        in_specs=[rowd, _const_spec((1, d)), _const_spec((d, d)), pl.BlockSpec((tm, PLE_DIM), lambda i: (i, 0)),
                  _const_spec((PLE_DIM, d)), rowd],
        out_specs=[rowd, rowd, rowd, rowd, _acc_spec((1, d)), _acc_spec((8, 128))],
        out_shape=[jax.ShapeDtypeStruct((s, d), f32), jax.ShapeDtypeStruct((s, d), bf16), jax.ShapeDtypeStruct((s, d), bf16),
                   jax.ShapeDtypeStruct((s, d), bf16), jax.ShapeDtypeStruct((1, d), f32), jax.ShapeDtypeStruct((8, 128), f32)],
        compiler_params=_cp(("arbitrary",), VMEM_LIMIT),
    )(x3, g_ple, w_pg, p, w_pp, target)


def _bwd_ffn_hidden(d3, gp, up, w_down, tm, tf):
    s, d = d3.shape

    def body(d3_ref, gp_ref, up_ref, wd_ref, act_ref, dgp_ref, dup_ref):
        gp, up = gp_ref[...].astype(f32), up_ref[...].astype(f32)
        sg = _sigmoid(gp)
        silu = gp * sg
        act_ref[...] = (silu * up).astype(bf16)
        dact = _dot_nt(d3_ref[...].astype(bf16), wd_ref[...])
        dgp_ref[...] = (dact * up * (sg * (1.0 + gp * (1.0 - sg)))).astype(bf16)
        dup_ref[...] = (dact * silu).astype(bf16)

    rowf = pl.BlockSpec((tm, tf), lambda f, i: (i, f))
    return pl.pallas_call(
        body, name="bwd_ffn_hidden", grid=(D_FF // tf, s // tm),
        in_specs=[pl.BlockSpec((tm, d), lambda f, i: (i, 0)), rowf, rowf, pl.BlockSpec((tf, d), lambda f, i: (f, 0))],
        out_specs=[rowf, rowf, rowf],
        out_shape=[jax.ShapeDtypeStruct((s, D_FF), bf16)] * 3,
        compiler_params=_cp(("parallel", "parallel"), VMEM_LIMIT),
    )(d3, gp, up, w_down)


def _bwd_ffn_in(d3, x2, dgp, dup, g_ffn, w_gate, w_up, tm):
    s, d = x2.shape

    def body(d3_ref, x_ref, dgp_ref, dup_ref, g_ref, wg_ref, wu_ref, d2_ref, h_ref, dg_ref):
        @pl.when(pl.program_id(0) == 0)
        def _():
            dg_ref[...] = jnp.zeros_like(dg_ref)

        x, g = x_ref[...], g_ref[...]
        dh = _dot(dgp_ref[...], wg_ref[...]) + _dot(dup_ref[...], wu_ref[...])
        h, r = _rms_fwd(x, g, d)
        h_ref[...] = h.astype(bf16)
        dx, dgx = _rms_bwd(dh, x, r, g, d)
        d2_ref[...] = d3_ref[...] + dx
        dg_ref[...] += jnp.sum(dgx, axis=0, keepdims=True)

    rowd = pl.BlockSpec((tm, d), lambda i: (i, 0))
    rowf = pl.BlockSpec((tm, D_FF), lambda i: (i, 0))
    return pl.pallas_call(
        body, name="bwd_ffn_in", grid=(s // tm,),
        in_specs=[rowd, rowd, rowf, rowf, _const_spec((1, d)), _const_spec((D_FF, d)), _const_spec((D_FF, d))],
        out_specs=[rowd, rowd, _acc_spec((1, d))],
        out_shape=[jax.ShapeDtypeStruct((s, d), f32), jax.ShapeDtypeStruct((s, d), bf16), jax.ShapeDtypeStruct((1, d), f32)],
        compiler_params=_cp(("arbitrary",), VMEM_LIMIT),
    )(d3, x2, dgp, dup, g_ffn, w_gate, w_up)


def _bwd_mix(d2, w_o, o, z, g_hgo, tm):
    s, d = d2.shape

    def body(d2_ref, w_ref, o_ref, hg_ref, g_ref, da_ref, do_ref, dhg_ref, dg_ref):
        @pl.when(pl.program_id(0) == 0)
        def _():
            dg_ref[...] = jnp.zeros_like(dg_ref)

        dcat = _dot_nt(d2_ref[...].astype(bf16), w_ref[...])
        da_ref[...] = dcat[:, 0:512].astype(bf16)
        dr = dcat[:, 512:1024]
        o, hg, g = o_ref[...], hg_ref[...], g_ref[...]
        _, on, rs, sg = _hg_out(o, hg, g)
        dhg_ref[...] = (dr * on * (sg * (1.0 + hg * (1.0 - sg)))).astype(bf16)
        don = dr * (hg * sg)
        dgs = []
        for h in range(HG_HEADS):
            cols = slice(128 * h, 128 * (h + 1))
            dx, dgx = _rms_bwd(don[:, cols], o[:, cols], rs[h], g[:, cols], 128)
            do_ref[:, cols] = dx
            dgs.append(jnp.sum(dgx, axis=0, keepdims=True))
        dg_ref[...] += jnp.concatenate(dgs, axis=-1)

    row512 = pl.BlockSpec((tm, 512), lambda i: (i, 0))
    return pl.pallas_call(
        body, name="bwd_mix", grid=(s // tm,),
        in_specs=[pl.BlockSpec((tm, d), lambda i: (i, 0)), _const_spec((d, d)), row512,
                  pl.BlockSpec((tm, 512), lambda i: (i, Z_HG // 512)), _const_spec((1, 512))],
        out_specs=[row512, row512, row512, _acc_spec((1, 512))],
        out_shape=[jax.ShapeDtypeStruct((s, 512), bf16), jax.ShapeDtypeStruct((s, 512), f32), jax.ShapeDtypeStruct((s, 512), bf16),
                   jax.ShapeDtypeStruct((1, 512), f32)],
        compiler_params=_cp(("arbitrary",), VMEM_LIMIT),
    )(d2, w_o, o, z, g_hgo)


def _bwd_gla(z, lb4, do, b_fwd, states):
    s = z.shape[0]
    n_chunks = s // CHUNK
    n_groups = s // GLA_ROWS
    assert n_groups % 2 == 0

    def body(hq_ref, hff_ref, hfb_ref, hi_ref, lb_ref, do_ref, b_all, st_all, dhq_ref, dhff_ref, dhfb_ref, dhi_ref, dlb_ref,
             dst_ref, dq_acc, dv_acc, dlow_ref):
        dirs = (False, True)
        masks = [_gla_masks(rev) for rev in dirs]
        lowers = [_sigmoid(lb_ref[int(rev):int(rev) + 1, :] - lb_ref[2 + int(rev):3 + int(rev), :]) for rev in dirs]
        hf_refs, dhf_refs = (hff_ref, hfb_ref), (dhff_ref, dhfb_ref)

        dst_ref[...] = jnp.zeros_like(dst_ref)
        dlow_ref[...] = jnp.zeros_like(dlow_ref)

        def make_bwd_step(first):
            def bwd_step(j, carry):
                n = n_groups - 1 - j
                for d, rev in enumerate(dirs):
                    maskf, _, tri_t, row_masks = masks[d]
                    lower = lowers[d]
                    rows, chunk0 = _gla_rows(n, n_groups, rev)
                    hq, hf = hq_ref[rows, :], hf_refs[d][rows, :]
                    q, k, _, f, sg = _gla_gates(hq, hf, lower)
                    v = hi_ref[rows, :]
                    dout = do_ref[rows, :]
                    b = b_all[d, rows, :]
                    b_last3, b_mid3 = _gla_last_mid(b, rev)
                    b_last, b_mid = _gla_per_row(b_last3), _gla_per_row(b_mid3)
                    e1, e2, e3, e4 = jnp.exp(b - b_mid), jnp.exp(b_mid - b), jnp.exp(b_last - b), jnp.exp(b)
                    decay3 = jnp.exp(b_last3)
                    qi, ki, kt, qt = q * e1, k * e2, k * e3, q * e4
                    qib, kib, ktb = qi.astype(bf16), ki.astype(bf16), kt.astype(bf16)
                    vb, dob = v.astype(bf16), dout.astype(bf16)
                    a = (_dot_nt(qib, kib) * maskf).astype(bf16)
                    da = (_dot_nt(dob, vb) * maskf).astype(bf16)
                    dqi = _dot(da, kib)
                    dki = _dot_tn(da, qib)
                    into_state = _dot_tn(dob, _gla_block_diag(qt, row_masks))
                    dst = dst_ref[d]
                    sts, dsts, ddecay = [None] * GLA_GROUP, [None] * GLA_GROUP, [None] * GLA_GROUP
                    for c in reversed(_gla_scan_order(rev)):
                        sts[c] = st_all[d, chunk0 + c]
                        dsts[c] = dst.astype(bf16)
                        ddecay[c] = jnp.sum(dst * sts[c], axis=0, keepdims=True)[None]
                        dst = dst * decay3[c] + into_state[:, 128 * c:128 * (c + 1)]
                    dst_ref[d] = dst
                    dv = _dot_tn(a, dob) + _gla_diag(_dot_nt(ktb, jnp.concatenate(dsts, axis=0)))
                    dqt = _gla_diag(_dot(dob, jnp.concatenate([x.astype(bf16) for x in sts], axis=-1)))
                    dkt = _gla_diag(_dot(vb, jnp.concatenate(dsts, axis=-1)))
                    dq = dqi * e1 + dqt * e4
                    dk = dki * e2 + dkt * e3
                    db = dqi * qi - dki * ki + dqt * qt - dkt * kt
                    dlast3 = (jnp.sum((dkt * kt).reshape(GLA_GROUP, CHUNK, 128), axis=1, keepdims=True)
                              + jnp.concatenate(ddecay, axis=0) * decay3)
                    dlogf = _tri_sum(tri_t, db) + _gla_per_row(dlast3)
                    df = dlogf / f - dk
                    dhf_refs[d][rows, :] = (df * (1.0 - lower) * sg * (1.0 - sg)).astype(bf16)
                    dlow_ref[d:d + 1, :] += jnp.sum(df * (1.0 - sg), axis=0, keepdims=True)
                    sq = _sigmoid(hq)
                    dhq = dq * (sq * (1.0 + hq * (1.0 - sq)))
                    if first:
                        dq_acc[rows, :] = dhq
                        dv_acc[rows, :] = dv
                    else:
                        dhq_ref[rows, :] = (dq_acc[rows, :] + dhq).astype(bf16)
                        dhi_ref[rows, :] = (dv_acc[rows, :] + dv).astype(bf16)
                return carry
            return bwd_step

        lax.fori_loop(0, n_groups // 2, make_bwd_step(True), 0, unroll=2)
        lax.fori_loop(n_groups // 2, n_groups, make_bwd_step(False), 0, unroll=2)

        for d in range(2):
            dl = dlow_ref[d:d + 1, :] * lowers[d] * (1.0 - lowers[d])
            dlb_ref[d:d + 1, :] = dl
            dlb_ref[2 + d:3 + d, :] = -dl

    col = lambda base: pl.BlockSpec((s, 128), lambda h: (0, base // 128 + h))
    return pl.pallas_call(
        body, name="bwd_gla", grid=(HG_HEADS,),
        in_specs=[col(Z_HQ), col(Z_HFF), col(Z_HFB), col(Z_HI), pl.BlockSpec((4, 128), lambda h: (0, h)), col(0),
                  pl.BlockSpec((2, s, 128), lambda h: (0, 0, h)),
                  pl.BlockSpec((None, 2, n_chunks, 128, 128), lambda h: (h, 0, 0, 0, 0), pipeline_mode=pl.Buffered(1))],
        out_specs=[col(0), col(0), col(0), col(0), pl.BlockSpec((4, 128), lambda h: (0, h))],
        out_shape=[jax.ShapeDtypeStruct((s, 512), bf16)] * 4 + [jax.ShapeDtypeStruct((4, 512), f32)],
        scratch_shapes=[pltpu.VMEM((2, 128, 128), f32), pltpu.VMEM((s, 128), f32), pltpu.VMEM((s, 128), f32),
                        pltpu.VMEM((2, 128), f32)],
        compiler_params=_cp(("parallel",), VMEM_LIMIT),
    )(z, z, z, z, lb4, do, b_fwd, states)


def _bwd_attn(q, k, v, da, a32, tq):
    hh, s, _ = q.shape

    n_sub = max(1, tq // ATTN_SUB_ROWS)

    def body(q_ref, k_ref, v_ref, do_ref, o_ref, dq_ref, dk_ref, dv_ref, p_all, ds_all, dol_ref, dkt_ref, dvt_ref):
        @pl.when(pl.program_id(1) == 0)
        def _():
            dkt_ref[...] = jnp.zeros_like(dkt_ref)
            dvt_ref[...] = jnp.zeros_like(dvt_ref)

        kb, vb = k_ref[...], v_ref[...]
        for t in range(n_sub):
            rows = slice(t * (tq // n_sub), (t + 1) * (tq // n_sub))
            sc = _dot_nt(q_ref[rows, :], kb)
            p = jnp.exp2((sc - jnp.max(sc, axis=-1, keepdims=True)) * (ATTN_SCALE * LOG2_E))
            inv_l = 1.0 / jnp.sum(p, axis=-1, keepdims=True)
            p_all[rows, :] = p.astype(bf16)
            dob = do_ref[rows, :]
            dof = dob.astype(f32)
            delta = jnp.sum(dof * o_ref[rows, :], axis=-1, keepdims=True)
            ds_all[rows, :] = p_all[rows, :] * ((_dot_nt(dob, vb) - delta) * inv_l).astype(bf16)
            dq_ref[rows, :] = _dot(ds_all[rows, :], kb) * ATTN_SCALE
            dol_ref[rows, :] = (dof * inv_l).astype(bf16)
        dkt_ref[...] += _dot_tn(q_ref[...], ds_all[...])
        dvt_ref[...] += _dot_tn(dol_ref[...], p_all[...])

        @pl.when(pl.program_id(1) == s // tq - 1)
        def _():
            dk_ref[...] = dkt_ref[...].T * ATTN_SCALE
            dv_ref[...] = dvt_ref[...].T

    return pl.pallas_call(
        body, name="bwd_attn", grid=(hh, s // tq),
        in_specs=[pl.BlockSpec((None, tq, QK_PAD), lambda h, i: (h, i, 0)),
                  pl.BlockSpec((None, s, QK_PAD), lambda h, i: (h, 0, 0)),
                  pl.BlockSpec((None, s, V_HEAD), lambda h, i: (h, 0, 0)),
                  pl.BlockSpec((tq, V_HEAD), lambda h, i: (i, h)), pl.BlockSpec((tq, V_HEAD), lambda h, i: (i, h))],
        out_specs=[pl.BlockSpec((None, tq, QK_PAD), lambda h, i: (h, i, 0)),
                   pl.BlockSpec((None, s, QK_PAD), lambda h, i: (h, 0, 0)),
                   pl.BlockSpec((None, s, V_HEAD), lambda h, i: (h, 0, 0))],
        out_shape=[jax.ShapeDtypeStruct((hh, s, QK_PAD), f32), jax.ShapeDtypeStruct((hh, s, QK_PAD), f32),
                   jax.ShapeDtypeStruct((hh, s, V_HEAD), f32)],
        scratch_shapes=[pltpu.VMEM((tq, s), bf16), pltpu.VMEM((tq, s), bf16), pltpu.VMEM((tq, V_HEAD), bf16),
                        pltpu.VMEM((QK_PAD, s), f32), pltpu.VMEM((V_HEAD, s), f32)],
        compiler_params=_cp(("parallel", "arbitrary"), VMEM_LIMIT),
    )(q, k, v, da, a32)


def _bwd_mla_proj(z, dq, dk, dv, cosb, sina, sinb, g_qa, g_kva, wqb, wkvb, g_qn, g_kn, tm):
    s = z.shape[0]
    hh = MLA_HEADS

    def body(cq_ref, ckv_ref, kr_ref, dq_ref, dk_ref, dv_ref, c_ref, sa_ref, sb_ref, gqa_ref, gkva_ref, wqb_ref, wkvb_ref,
             gqn_ref, gkn_ref, dz_ref, cqn_ref, ckvn_ref, dq0_ref, dkv0_ref, dgqa_ref, dgkva_ref, dgqn_ref, dgkn_ref):
        @pl.when(pl.program_id(0) == 0)
        def _():
            for r in (dgqa_ref, dgkva_ref, dgqn_ref, dgkn_ref):
                r[...] = jnp.zeros_like(r)

        cq, ckv, kr = cq_ref[...], ckv_ref[...], kr_ref[...]
        gqa, gkva, gqn, gkn = gqa_ref[...], gkva_ref[...], gqn_ref[...], gkn_ref[...]
        cqn_b, rq, ckvn_b, rkv, q0, kv0 = _mla_qk_fwd(cq, ckv, gqa, gkva, wqb_ref[...], wkvb_ref[...])
        cqn_ref[...] = cqn_b
        ckvn_ref[...] = ckvn_b
        c, sa, sb = c_ref[...], -sa_ref[...], -sb_ref[...]
        kr_sq = jnp.sum(kr * kr, axis=-1, keepdims=True)
        dkr = jnp.zeros_like(kr)
        dgqn = jnp.zeros((1, QK_PAD), f32)
        dgkn = jnp.zeros((1, QK_PAD), f32)
        for h in range(hh):
            qh = q0[:, QK_PAD * h:QK_PAD * (h + 1)]
            rh = lax.rsqrt(jnp.sum(qh * qh, axis=-1, keepdims=True) * (1.0 / QK_HEAD) + EPS)
            dqh = dq_ref[h]
            dqn = jnp.concatenate([dqh[:, 0:128], _rope(dqh[:, 128:256], c, sa, sb)], axis=-1)
            dq0h, dgx = _rms_bwd(dqn, qh, rh, gqn, QK_HEAD)
            dq0_ref[:, QK_PAD * h:QK_PAD * (h + 1)] = dq0h.astype(bf16)
            dgqn = dgqn + jnp.sum(dgx, axis=0, keepdims=True)

            kn_ = kv0[:, 256 * h:256 * h + 128]
            k0 = jnp.concatenate([kn_, kr], axis=-1)
            rk = lax.rsqrt((jnp.sum(kn_ * kn_, axis=-1, keepdims=True) + kr_sq) * (1.0 / QK_HEAD) + EPS)
            dkh = dk_ref[h]
            dkn = jnp.concatenate([dkh[:, 0:128], _rope(dkh[:, 128:256], c, sa, sb)], axis=-1)
            dk0, dgx = _rms_bwd(dkn, k0, rk, gkn, QK_HEAD)
            dgkn = dgkn + jnp.sum(dgx, axis=0, keepdims=True)
            dkv0_ref[:, 256 * h:256 * h + 128] = dk0[:, 0:128].astype(bf16)
            dkv0_ref[:, 256 * h + 128:256 * h + 256] = dv_ref[h].astype(bf16)
            dkr = dkr + dk0[:, 128:256]
        dgqn_ref[...] += dgqn
        dgkn_ref[...] += dgkn
        dcq, dgx = _rms_bwd(_dot_nt(dq0_ref[...], wqb_ref[...]), cq, rq, gqa, Q_LORA)
        dgqa_ref[...] += jnp.sum(dgx, axis=0, keepdims=True)
        dckv, dgx = _rms_bwd(_dot_nt(dkv0_ref[...], wkvb_ref[...]), ckv, rkv, gkva, KV_LORA)
        dgkva_ref[...] += jnp.sum(dgx, axis=0, keepdims=True)
        dz_ref[:, 0:256] = dcq.astype(bf16)
        dz_ref[:, 256:512] = dckv.astype(bf16)
        dz_ref[:, 512:640] = dkr.astype(bf16)

    row128 = pl.BlockSpec((tm, 128), lambda i: (i, 0))
    row256 = pl.BlockSpec((tm, 256), lambda i: (i, 0))
    row1024 = pl.BlockSpec((tm, 1024), lambda i: (i, 0))
    hd = lambda w: pl.BlockSpec((hh, tm, w), lambda i: (0, i, 0))
    return pl.pallas_call(
        body, name="bwd_mla_proj", grid=(s // tm,),
        in_specs=[pl.BlockSpec((tm, 256), lambda i: (i, Z_CQ // 256)), pl.BlockSpec((tm, 256), lambda i: (i, Z_CKV // 256)),
                  pl.BlockSpec((tm, 128), lambda i: (i, Z_KR // 128)), hd(QK_PAD), hd(QK_PAD), hd(V_HEAD),
                  row128, row128, row128,
                  _const_spec((1, 256)), _const_spec((1, 256)), _const_spec((256, 1024)), _const_spec((256, 1024)),
                  _const_spec((1, 256)), _const_spec((1, 256))],
        out_specs=[pl.BlockSpec((tm, 640), lambda i: (i, 0)), row256, row256, row1024, row1024,
                   _acc_spec((1, 256)), _acc_spec((1, 256)), _acc_spec((1, 256)), _acc_spec((1, 256))],
        out_shape=[jax.ShapeDtypeStruct((s, 640), bf16), jax.ShapeDtypeStruct((s, 256), bf16), jax.ShapeDtypeStruct((s, 256), bf16),
                   jax.ShapeDtypeStruct((s, 1024), bf16), jax.ShapeDtypeStruct((s, 1024), bf16)]
        + [jax.ShapeDtypeStruct((1, 256), f32)] * 4,
        compiler_params=_cp(("arbitrary",), VMEM_LIMIT),
    )(z, z, z, dq, dk, dv, cosb, sina, sinb, g_qa, g_kva, wqb, wkvb, g_qn, g_kn)


def _bwd_in(segments, wz, x, g_mix, d2, tm):
    s, d = x.shape
    n_seg = len(segments)

    def body(*refs):
        dz_refs, w_refs = refs[:n_seg], refs[n_seg:2 * n_seg]
        x_ref, g_ref, d2_ref, gx_ref, dg_ref = refs[2 * n_seg:]

        @pl.when(pl.program_id(0) == 0)
        def _():
            dg_ref[...] = jnp.zeros_like(dg_ref)

        dh = _dot(dz_refs[0][...], w_refs[0][...])
        for a_ref, w_ref in zip(dz_refs[1:], w_refs[1:]):
            dh = dh + _dot(a_ref[...], w_ref[...])
        x, g = x_ref[...], g_ref[...]
        r = lax.rsqrt(jnp.sum(x * x, axis=-1, keepdims=True) * (1.0 / d) + EPS)
        dx, dgx = _rms_bwd(dh, x, r, g, d)
        gx_ref[...] = d2_ref[...] + dx
        dg_ref[...] += jnp.sum(dgx, axis=0, keepdims=True)

    rowd = pl.BlockSpec((tm, d), lambda i: (i, 0))
    dz_specs = [pl.BlockSpec((tm, w), functools.partial(lambda i, j: (i, j), j=ja)) for _, w, ja, _ in segments]
    w_specs = [pl.BlockSpec((w, d), functools.partial(lambda i, j: (j, 0), j=jw), pipeline_mode=pl.Buffered(1))
               for _, w, _, jw in segments]
    return pl.pallas_call(
        body, name="bwd_in", grid=(s // tm,),
        in_specs=dz_specs + w_specs + [rowd, _const_spec((1, d)), rowd],
        out_specs=[rowd, _acc_spec((1, d))],
        out_shape=[jax.ShapeDtypeStruct((s, d), f32), jax.ShapeDtypeStruct((1, d), f32)],
        compiler_params=_cp(("arbitrary",), VMEM_LIMIT),
    )(*[a for a, _, _, _ in segments], *([wz] * n_seg), x, g_mix, d2)


def _pick_tile(n, cap):
    best = None
    for t in range(LANES, cap + 1, LANES):
        if n % t == 0:
            best = t
    return best if best is not None else n


def _mm_tn_many(a, bs, name, tm, transposed=False):
    kk, m = a.shape
    n_b = len(bs)
    tk = min(1024, kk)
    n_k = kk // tk

    def body(a_ref, *refs):
        b_refs, o_refs, acc_refs = refs[:n_b], refs[n_b:2 * n_b], refs[2 * n_b:]

        @pl.when(pl.program_id(1) == 0)
        def _():
            for acc in acc_refs:
                acc[...] = jnp.zeros_like(acc)
        a_blk = a_ref[...].astype(bf16)
        for b_ref, acc in zip(b_refs, acc_refs):
            acc[...] += _dot_tn(a_blk, b_ref[...].astype(bf16))

        @pl.when(pl.program_id(1) == n_k - 1)
        def _():
            for o_ref, acc in zip(o_refs, acc_refs):
                o_ref[...] = (acc[...].T if transposed else acc[...]).astype(bf16)

    if transposed:
        out_specs = [pl.BlockSpec((b.shape[1], tm), lambda i, k: (0, i)) for b in bs]
        out_shape = [jax.ShapeDtypeStruct((b.shape[1], m), bf16) for b in bs]
    else:
        out_specs = [pl.BlockSpec((tm, b.shape[1]), lambda i, k: (i, 0)) for b in bs]
        out_shape = [jax.ShapeDtypeStruct((m, b.shape[1]), bf16) for b in bs]
    return pl.pallas_call(
        body, name=name, grid=(m // tm, n_k),
        in_specs=[pl.BlockSpec((tk, tm), lambda i, k: (k, i))] + [pl.BlockSpec((tk, b.shape[1]), lambda i, k: (k, 0)) for b in bs],
        out_specs=out_specs,
        out_shape=out_shape,
        scratch_shapes=[pltpu.VMEM((tm, b.shape[1]), f32) for b in bs],
        compiler_params=_cp(("parallel", "arbitrary"), VMEM_LIMIT),
    )(a, *bs)


def _mm_tn(a, b, name):
    kk, m = a.shape
    _, n = b.shape
    tm = _pick_tile(m, 1408)
    tn = _pick_tile(n, 1408)
    tk = min(1024, kk)

    n_k = kk // tk

    def body(a_ref, b_ref, o_ref, acc_ref):
        @pl.when(pl.program_id(2) == 0)
        def _():
            acc_ref[...] = jnp.zeros_like(acc_ref)
        acc_ref[...] += _dot_tn(a_ref[...].astype(bf16), b_ref[...].astype(bf16))

        @pl.when(pl.program_id(2) == n_k - 1)
        def _():
            o_ref[...] = acc_ref[...].astype(bf16)

    return pl.pallas_call(
        body, name=name, grid=(m // tm, n // tn, n_k),
        in_specs=[pl.BlockSpec((tk, tm), lambda i, j, k: (k, i)), pl.BlockSpec((tk, tn), lambda i, j, k: (k, j))],
        out_specs=pl.BlockSpec((tm, tn), lambda i, j, k: (i, j)),
        out_shape=jax.ShapeDtypeStruct((m, n), bf16),
        scratch_shapes=[pltpu.VMEM((tm, tn), f32)],
        compiler_params=_cp(("parallel", "parallel", "arbitrary"), VMEM_LIMIT),
    )(a, b)


def _rope_tables(positions):
    inv_freq = ROPE_THETA ** (-jnp.arange(0, QK_ROPE, 2, dtype=f32) / QK_ROPE)
    ang = positions.astype(f32)[:, None] * inv_freq
    cos, sin = jnp.cos(ang), jnp.sin(ang)
    zero = jnp.zeros_like(cos)
    return (jnp.concatenate([cos, cos, zero, zero], axis=1), jnp.concatenate([zero, sin, zero, zero], axis=1),
            jnp.concatenate([-sin, zero, zero, zero], axis=1))


def _pad256(g):
    return jnp.pad(g.reshape(1, QK_HEAD), ((0, 0), (0, QK_PAD - QK_HEAD)))


RELAYOUT_BLOCKS = 8
FIRST = ("w_in", "w_qb", "w_kvb", "lb_param")
SECOND = ("w_o", "w_gate", "w_up", "w_down", "w_ple_gate", "w_ple_proj")
ROW_SHARDED = ("w_o", "w_down", "w_ple_gate")


def _col_moves(j):
    width = BIG["w_in"][0]
    lo = width * j
    w_in = [(max(lo, a) - lo, min(lo + width, b) - lo, d + max(lo, a) - a)
            for a, b, d in Z_SEGMENTS if max(lo, a) < min(lo + width, b)]
    head, half = divmod(j, 2)
    whole = lambda n: [(0, BIG[n][1], BIG[n][1] * j)]
    return {"w_in": w_in, "w_qb": [(0, 96, QK_PAD * head + 96 * half)], "w_kvb": whole("w_kvb"),
            "w_ple_proj": whole("w_ple_proj"), "lb_param": whole("lb_param")}


def _kernel_shape(name):
    rows, cols = BIG[name]
    if name == "w_in":
        return (Z_W, cols)
    return (rows, MLA_HEADS * QK_PAD if name == "w_qb" else N_DEV * cols)


def _relayout_specs(names, by_dev):
    specs = []
    for n in names:
        rows, cols = BIG[n]
        if n == "lb_param":
            specs.append(_acc_spec((N_DEV, rows, cols) if by_dev else _kernel_shape(n)))
        elif n == "w_in":
            cb = cols // RELAYOUT_BLOCKS
            specs.append(pl.BlockSpec((N_DEV, rows, cb), lambda i: (0, 0, i)) if by_dev else pl.BlockSpec((Z_W, cb), lambda i: (0, i)))
        elif by_dev:
            specs.append(pl.BlockSpec((N_DEV, rows // RELAYOUT_BLOCKS, cols), lambda i: (0, i, 0)))
        else:
            specs.append(pl.BlockSpec((rows // RELAYOUT_BLOCKS, _kernel_shape(n)[1]), lambda i: (i, 0)))
    return specs


def _weights_in(gathered, names, name):
    n = len(names)

    def body(*refs):
        ins, outs = dict(zip(names, refs[:n])), dict(zip(names, refs[n:]))
        if "w_in" in outs:
            outs["w_in"][Z_KR + QK_ROPE:Z_W, :] = jnp.zeros((Z_W - Z_KR - QK_ROPE, outs["w_in"].shape[1]), bf16)
        if "w_qb" in outs:
            for h in range(MLA_HEADS):
                outs["w_qb"][:, QK_PAD * h + QK_HEAD:QK_PAD * (h + 1)] = jnp.zeros((outs["w_qb"].shape[0], QK_PAD - QK_HEAD), bf16)
        for j in range(N_DEV):
            for wn, moves in _col_moves(j).items():
                if wn in outs:
                    for s0, s1, d0 in moves:
                        if wn == "w_in":
                            outs[wn][d0:d0 + s1 - s0, :] = ins[wn][j, s0:s1, :]
                        else:
                            outs[wn][:, d0:d0 + s1 - s0] = ins[wn][j, :, s0:s1]

    outs = pl.pallas_call(
        body, name=name, grid=(RELAYOUT_BLOCKS,), in_specs=_relayout_specs(names, True), out_specs=_relayout_specs(names, False),
        out_shape=[jax.ShapeDtypeStruct(_kernel_shape(wn), gathered[wn].dtype) for wn in names],
        compiler_params=_cp(("arbitrary",), VMEM_LIMIT),
    )(*[gathered[wn] for wn in names])
    return dict(zip(names, outs))


def _grads_out(sources, names, name):
    pieces = [(wn, start, arr) for wn in names for start, arr in sources[wn]]
    n_in = len(pieces)

    def body(*refs):
        outs = dict(zip(names, refs[n_in:]))

        def cols(wn, c0, c1):
            for (pn, start, arr), ref in zip(pieces, refs[:n_in]):
                if pn == wn and start <= c0 and c1 <= start + arr.shape[0 if wn == "w_in" else 1]:
                    return ref[c0 - start:c1 - start, :] if wn == "w_in" else ref[:, c0 - start:c1 - start]

        for j in range(N_DEV):
            for wn, moves in _col_moves(j).items():
                if wn in outs:
                    for s0, s1, d0 in moves:
                        if wn == "w_in":
                            outs[wn][j, s0:s1, :] = cols(wn, d0, d0 + s1 - s0).astype(bf16)
                        else:
                            outs[wn][j, :, s0:s1] = cols(wn, d0, d0 + s1 - s0).astype(bf16)

    def in_spec(wn, arr):
        if wn == "lb_param":
            return _acc_spec(arr.shape)
        if wn == "w_in":
            return pl.BlockSpec((arr.shape[0], arr.shape[1] // RELAYOUT_BLOCKS), lambda i: (0, i))
        return pl.BlockSpec((arr.shape[0] // RELAYOUT_BLOCKS, arr.shape[1]), lambda i: (i, 0))

    in_specs = [in_spec(wn, arr) for wn, _, arr in pieces]
    outs = pl.pallas_call(
        body, name=name, grid=(RELAYOUT_BLOCKS,), in_specs=in_specs, out_specs=_relayout_specs(names, True),
        out_shape=[jax.ShapeDtypeStruct((N_DEV, *BIG[wn]), bf16) for wn in names],
        compiler_params=_cp(("arbitrary",), VMEM_LIMIT),
    )(*[arr for _, _, arr in pieces])
    return dict(zip(names, outs))


def kernel(x, p, positions, g_mix, w_in, g_qa, g_kva, w_qb, w_kvb, g_qn, g_kn, lb_param, g_hgo, w_o, g_ffn, w_gate, w_up, w_down, g_ple, w_ple_gate, w_ple_proj, loss_target, m_g_mix, m_w_in, m_g_qa, m_g_kva, m_w_qb, m_w_kvb, m_g_qn, m_g_kn, m_lb_param, m_g_hgo, m_w_o, m_g_ffn, m_w_gate, m_w_up, m_w_down, m_g_ple, m_w_ple_gate, m_w_ple_proj, v_g_mix, v_w_in, v_g_qa, v_g_kva, v_w_qb, v_w_kvb, v_g_qn, v_g_kn, v_lb_param, v_g_hgo, v_w_o, v_g_ffn, v_w_gate, v_w_up, v_w_down, v_g_ple, v_w_ple_gate, v_w_ple_proj):
    w_all = dict(g_mix=g_mix, g_qa=g_qa, g_kva=g_kva, g_qn=g_qn, g_kn=g_kn, g_hgo=g_hgo, g_ffn=g_ffn, g_ple=g_ple,
                 w_in=w_in, w_qb=w_qb, w_kvb=w_kvb, w_o=w_o, w_gate=w_gate, w_up=w_up, w_down=w_down,
                 w_ple_gate=w_ple_gate, w_ple_proj=w_ple_proj, lb_param=lb_param)
    m_all = dict(g_mix=m_g_mix, g_qa=m_g_qa, g_kva=m_g_kva, g_qn=m_g_qn, g_kn=m_g_kn, g_hgo=m_g_hgo, g_ffn=m_g_ffn,
                 g_ple=m_g_ple, w_in=m_w_in, w_qb=m_w_qb, w_kvb=m_w_kvb, w_o=m_w_o, w_gate=m_w_gate, w_up=m_w_up,
                 w_down=m_w_down, w_ple_gate=m_w_ple_gate, w_ple_proj=m_w_ple_proj, lb_param=m_lb_param)
    v_all = dict(g_mix=v_g_mix, g_qa=v_g_qa, g_kva=v_g_kva, g_qn=v_g_qn, g_kn=v_g_kn, g_hgo=v_g_hgo, g_ffn=v_g_ffn,
                 g_ple=v_g_ple, w_in=v_w_in, w_qb=v_w_qb, w_kvb=v_w_kvb, w_o=v_w_o, w_gate=v_w_gate, w_up=v_w_up,
                 w_down=v_w_down, w_ple_gate=v_w_ple_gate, w_ple_proj=v_w_ple_proj, lb_param=v_lb_param)
    me_idx = jnp.stack([_me()]).astype(jnp.int32)
    x, p, positions, target = x[0], p[0, 0], positions[0], loss_target[0]
    s = x.shape[0]
    tm, tm_ffn, tq_f, tq_b = min(512, s), min(1024, s), min(2048, s), min(1024, s)
    g_mix, g_qa, g_kva, g_qn, g_kn, g_hgo, g_ffn, g_ple = (w_all[n].reshape(1, -1) for n in SMALL)
    g_qn_p, g_kn_p = _pad256(g_qn), _pad256(g_kn)
    cosb, sina, sinb = _rope_tables(positions)
    as_shard = lambda n, a: a[0].T if n in TRANSPOSED else a.reshape(BIG[n])
    shard = lambda n: as_shard(n, w_all[n])

    first = _all_gather([shard(n) for n in FIRST], [f32 if n == "lb_param" else bf16 for n in FIRST], "ag_first")
    lands = _cast_to_slot([shard(n) for n in SECOND], me_idx, first[0])
    ag2, token = _exchange_start([], lands, "ag_second_start")
    wk = _weights_in(dict(zip(FIRST, first)), FIRST, "weights_in_first")
    wz, wqb, wkvb, lb4 = (wk[n] for n in FIRST)

    h1, z = _fwd_in(x, g_mix, wz, tm)
    q, k, v = _fwd_mla_proj(z, cosb + token[0, 0], sina, sinb, g_qa, g_kva, wqb, wkvb, g_qn_p, g_kn_p, tm)
    o, gla_b, gla_states = _fwd_gla(z, lb4)
    a, a32 = _fwd_attn(q, k, v, tq_f, o)

    second = dict(zip(SECOND, _exchange_wait(ag2, [a, o], "ag_second_wait")[1]))
    w_pp = _weights_in(second, ("w_ple_proj",), "weights_in_second")["w_ple_proj"]
    w_o, w_down, w_pg, w_gate, w_up = (second[n].reshape(N_DEV * BIG[n][0], BIG[n][1]) for n in ROW_SHARDED + ("w_gate", "w_up"))

    x2, cat = _fwd_mix(a, o, z, g_hgo, x, w_o, tm)
    x3, gp, up = _fwd_ffn(x2, g_ffn, w_gate, w_up, w_down, tm)
    d3, h3, dpre, dpp, dg_ple, loss_tile = _ple_loss_fwd_bwd(x3, g_ple, w_pg, p, w_pp, target, tm)
    act, dgp, dup = _bwd_ffn_hidden(d3, gp, up, w_down, tm, D_FF // 2)
    d2, h2, dg_ffn = _bwd_ffn_in(d3, x2, dgp, dup, g_ffn, w_gate, w_up, tm)

    gw_gate, gw_up = _mm_tn_many(h2, [dgp, dup], "dw_gate_up", 512, transposed=True)
    blocks = _grads_out({"w_ple_proj": [(0, _mm_tn(p, dpp, "dw_ple_proj"))]}, ("w_ple_proj",), "grads_out_second")
    row_grads = {"w_o": _mm_tn(cat, d2, "dw_o"), "w_down": _mm_tn(act, d3, "dw_down"), "w_ple_gate": _mm_tn(h3, dpre, "dw_ple_gate"),
                 "w_gate": gw_gate, "w_up": gw_up}
    blocks.update({n: g.reshape(N_DEV, *BIG[n]) for n, g in row_grads.items()})
    empty = lambda names: [lax.empty((N_PEERS, *BIG[n]), bf16) for n in names]
    rs2, token = _exchange_start([blocks[n] for n in SECOND], empty(SECOND), "rs_second_start")

    da, do, dz_hg, dg_hgo = _bwd_mix(d2, w_o, o, z, g_hgo + token[0, 0], tm)
    dz_hq, dz_hff, dz_hfb, dz_hi, dlb4 = _bwd_gla(z, lb4, do, gla_b, gla_states)
    dq, dk, dv = _bwd_attn(q, k, v, da, a32, tq_b)
    dz_mla, cqn, ckvn, dq0, dkv0, dg_qa, dg_kva, dg_qn, dg_kn = _bwd_mla_proj(
        z, dq, dk, dv, cosb, sina, sinb, g_qa, g_kva, wqb, wkvb, g_qn_p, g_kn_p, tm)

    gz = list(zip((Z_HQ, Z_HFF, Z_HFB, Z_HI, Z_HG, Z_CQ),
                  _mm_tn_many(h1, [dz_hq, dz_hff, dz_hfb, dz_hi, dz_hg, dz_mla], "dw_in", 1024, transposed=True)))
    blocks1 = _grads_out({"w_in": gz, "w_qb": [(0, _mm_tn(cqn, dq0, "dw_qb"))], "w_kvb": [(0, _mm_tn(ckvn, dkv0, "dw_kvb"))],
                          "lb_param": [(0, dlb4)]}, FIRST, "grads_out_first")
    rs1, token = _exchange_start([blocks1[n] for n in FIRST], empty(FIRST), "rs_first_start")

    result = {}

    def adam(names, lands, src, n_blocks, after=()):
        flipped = TRANSPOSED
        given = lambda arrs: [arrs[n][0].T if n in flipped else arrs[n] for n in names]
        outs = _adam_shards(me_idx, [src[n] for n in names], lands, given(w_all), given(m_all), given(v_all), n_blocks,
                            "adamw_" + names[0], after)
        for n, o in zip(names, outs):
            result[n] = [t.T[None] for t in o] if n in flipped else o
        return outs[0][0]

    blocks2, lands2 = (dict(zip(SECOND, arrs)) for arrs in _exchange_wait(rs2, [token], "rs_second_wait"))
    by2 = ("w_down", "w_gate", "w_up")
    by8 = tuple(n for n in SECOND if n not in by2)
    done = [adam(by8, [lands2[n] for n in by8], blocks2, 8), adam(by2, [lands2[n] for n in by2], blocks2, 2)]

    segments = [(dz_hq, 512, 0, Z_HQ // 512), (dz_hff, 512, 0, Z_HFF // 512), (dz_hfb, 512, 0, Z_HFB // 512),
                (dz_hi, 512, 0, Z_HI // 512), (dz_hg, 512, 0, Z_HG // 512), (dz_mla, 640, 0, Z_CQ // 640)]
    grad_x, dg_mix = _bwd_in(segments, wz, x, g_mix + token[0, 0], d2, tm)
    dgains = (dg_mix, dg_qa, dg_kva, dg_qn, dg_kn, dg_hgo, dg_ffn, dg_ple)

    vec = jnp.concatenate(list(dgains) + [loss_tile[0:1]], axis=1)
    parts = _all_gather([vec], [f32], "ag_gains")[0]
    outs, loss_row = _adam_gains(parts, [w_all[n] for n in SMALL], [m_all[n] for n in SMALL], [v_all[n] for n in SMALL])
    result.update(zip(SMALL, outs))

    blocks1, lands1 = _exchange_wait(rs1, [grad_x, loss_row, *done], "rs_first_wait")
    adam(FIRST, lands1, dict(zip(FIRST, blocks1)), 8)

    order = ("g_mix", "w_in", "g_qa", "g_kva", "w_qb", "w_kvb", "g_qn", "g_kn", "lb_param", "g_hgo", "w_o", "g_ffn",
             "w_gate", "w_up", "w_down", "g_ple", "w_ple_gate", "w_ple_proj")
    return (loss_row[0, 0], grad_x[None], *[result[n][k] for k in range(4) for n in order])
```

```python
import functools
import math

import jax
import jax.numpy as jnp
from jax import lax
from jax.experimental import pallas as pl
from jax.experimental.pallas import tpu as pltpu

f32 = jnp.float32
bf16 = jnp.bfloat16

N_DEV = 8
MLA_HEADS = 4
QK_NOPE = 128
QK_ROPE = 64
QK_HEAD = QK_NOPE + QK_ROPE
QK_PAD = 256
V_HEAD = 128
Q_LORA = 256
KV_LORA = 256
HG_HEADS = 4
CHUNK = 64
D_FF = 2816
PLE_DIM = 256
ROPE_THETA = 10000.0
EPS = 1e-6
ATTN_SCALE = QK_HEAD ** -0.5
LOG2_E = math.log2(math.e)
ATTN_SUB_ROWS = 256
Z_HQ, Z_HFF, Z_HFB, Z_HI, Z_HG, Z_CQ, Z_CKV, Z_KR, Z_W = 0, 512, 1024, 1536, 2048, 2560, 2816, 3072, 3200

ADAM_LR, ADAM_B1, ADAM_B2, ADAM_EPS, ADAM_WD, ADAM_STEP = 0.001, 0.9, 0.999, 1e-08, 0.01, 10

LANES = 128
BIG = {"w_in": (392, 1024), "w_qb": (256, 96), "w_kvb": (256, 128), "w_o": (128, 1024), "w_gate": (352, 1024),
       "w_up": (352, 1024), "w_down": (352, 1024), "w_ple_gate": (128, 1024), "w_ple_proj": (256, 128),
       "lb_param": (4, 64)}
TRANSPOSED = ("w_gate", "w_up", "w_in")
SMALL = {"g_mix": (0, 1024), "g_qa": (1024, 256), "g_kva": (1280, 256), "g_qn": (1536, 192), "g_kn": (1792, 192),
         "g_hgo": (2048, 512), "g_ffn": (2560, 1024), "g_ple": (3584, 1024)}
LOSS_OFF = 4608
GAIN_VEC = LOSS_OFF + LANES
Z_SEGMENTS = ((0, 256, Z_CQ), (256, 512, Z_CKV), (512, 576, Z_KR), (576, 1088, Z_HQ), (1088, 1600, Z_HFF),
              (1600, 2112, Z_HFB), (2112, 2624, Z_HI), (2624, 3136, Z_HG))

VMEM_LIMIT = 56 * 1024 * 1024
MESH = pl.DeviceIdType.MESH


def _cp(sem=None, vmem=None):
    return pltpu.CompilerParams(dimension_semantics=sem, vmem_limit_bytes=vmem)


def _const_spec(shape):
    nd = len(shape)
    return pl.BlockSpec(shape, lambda *_: (0,) * nd, pipeline_mode=pl.Buffered(1))


def _acc_spec(shape):
    nd = len(shape)
    return pl.BlockSpec(shape, lambda *_: (0,) * nd)


def _sigmoid(x):
    return jax.nn.sigmoid(x)


def _dot(a, b):
    return jnp.dot(a, b, preferred_element_type=f32)


def _dot_nt(a, b):
    return lax.dot_general(a, b, (((1,), (1,)), ((), ())), preferred_element_type=f32)


def _dot_tn(a, b):
    return lax.dot_general(a, b, (((0,), (0,)), ((), ())), preferred_element_type=f32)


def _rms_fwd(x, g, width):
    r = lax.rsqrt(jnp.sum(x * x, axis=-1, keepdims=True) * (1.0 / width) + EPS)
    return x * r * g, r


def _rms_bwd(dy, x, r, g, width):
    u = dy * g
    dx = r * u - x * (r * r * r) * (jnp.sum(u * x, axis=-1, keepdims=True) * (1.0 / width))
    return dx, dy * x * r


class _Both:
    def __init__(self, *copies):
        self.copies = copies

    def start(self):
        for cp in self.copies:
            cp.start()

    def wait(self):
        for cp in self.copies:
            cp.wait()


def _rope(b, c, sa, sb):
    return b * c + pltpu.roll(b, 32, 1) * sa + pltpu.roll(b, 96, 1) * sb


def _all_gather(shards, dtypes, name):
    n = len(shards)

    def body(*refs):
        in_refs, out_refs, stage = refs[:n], refs[n:2 * n], refs[2 * n:3 * n]
        send_sems, recv_sems, local_sems = refs[3 * n:]
        for w in range(n):
            stage[w][...] = in_refs[w][...].astype(stage[w].dtype)
        x, y, c = lax.axis_index("x"), lax.axis_index("y"), lax.axis_index("c")
        me, sibling = (x, y, c), (x, y, 1 - c)
        chips = [(1 - x, y), (x, 1 - y), (1 - x, 1 - y)]

        def slot(w, px, py, pc):
            return out_refs[w].at[4 * px + 2 * py + pc]

        def copy(w, k, block, to, src=None):
            return pltpu.make_async_remote_copy(
                src_ref=slot(w, *block) if src is None else src, dst_ref=slot(w, *block),
                send_sem=send_sems.at[w, k], recv_sem=recv_sems.at[w, k], device_id=to, device_id_type=MESH)

        first = []
        for j, chip in enumerate(chips):
            first += [copy(w, 1 + j, me, (*chip, c), src=stage[w]) for w in range(n)]
        first += [copy(w, 0, me, sibling, src=stage[w]) for w in range(n)]
        mine = [pltpu.make_async_copy(stage[w], slot(w, *me), local_sems.at[w]) for w in range(n)]
        for cp in first + mine:
            cp.start()
        passed = []
        for j, chip in enumerate(chips):
            for w in range(n):
                copy(w, 1 + j, (*chip, c), me).wait_recv()
                passed.append(copy(w, 4 + j, (*chip, c), sibling))
                passed[-1].start()
        for w in range(n):
            copy(w, 0, sibling, me).wait_recv()
        for j, chip in enumerate(chips):
            for w in range(n):
                copy(w, 4 + j, (*chip, 1 - c), me).wait_recv()
        for cp in first + passed:
            cp.wait_send()
        for cp in mine:
            cp.wait()

    return pl.pallas_call(
        body, name=name,
        out_shape=[jax.ShapeDtypeStruct((N_DEV, *s.shape), dt) for s, dt in zip(shards, dtypes)],
        in_specs=[pl.BlockSpec(memory_space=pltpu.VMEM)] * n,
        out_specs=[pl.BlockSpec(memory_space=pl.ANY)] * n,
        scratch_shapes=[pltpu.VMEM(s.shape, dt) for s, dt in zip(shards, dtypes)]
        + [pltpu.SemaphoreType.DMA((n, 7)), pltpu.SemaphoreType.DMA((n, 7)), pltpu.SemaphoreType.DMA((n,))],
        compiler_params=_cp(None, VMEM_LIMIT),
    )(*shards)


N_PEERS = N_DEV - 1
HBM_SPEC = pl.BlockSpec(memory_space=pltpu.HBM)
SEM_SPEC = pl.BlockSpec(memory_space=pltpu.SEMAPHORE)
DATAFLOW = pltpu.SideEffectType.DATAFLOW_SIDE_EFFECTING


def _me():
    return 4 * lax.axis_index("x") + 2 * lax.axis_index("y") + lax.axis_index("c")


def _peer(k):
    x, y, c = lax.axis_index("x"), lax.axis_index("y"), lax.axis_index("c")
    px = 1 - x if k & 4 else x
    py = 1 - y if k & 2 else y
    pc = 1 - c if k & 1 else c
    return (px, py, pc), 4 * px + 2 * py + pc


def _exchange_copies(src_refs, land_refs, send_sems, recv_sems, gather):
    cps = []
    me = _me()
    for k in range(1, N_DEV):
        peer, peer_idx = _peer(k)
        for w, land in enumerate(land_refs):
            src = land.at[me] if gather else src_refs[w].at[peer_idx]
            dst = land.at[me] if gather else land.at[k - 1]
            cps.append(pltpu.make_async_remote_copy(
                src_ref=src, dst_ref=dst, send_sem=send_sems.at[N_PEERS * w + k - 1], recv_sem=recv_sems.at[N_PEERS * w + k - 1],
                device_id=peer, device_id_type=MESH))
    return cps


def _exchange_start(srcs, lands, name):
    n_src, n = len(srcs), len(lands)

    def body(*refs):
        src_refs, land_refs = refs[:n_src], refs[n_src:n_src + n]
        send_sems, recv_sems = refs[n_src + n], refs[n_src + n + 1]
        token = refs[-1]
        for cp in _exchange_copies(src_refs, land_refs, send_sems, recv_sems, gather=not n_src):
            cp.start()
        token[...] = jnp.zeros_like(token)

    arrays = [pltpu.with_memory_space_constraint(a, pltpu.HBM) for a in (*srcs, *lands)]
    outs = pl.pallas_call(
        body, name=name,
        out_shape=(pltpu.SemaphoreType.DMA((n * N_PEERS,)), pltpu.SemaphoreType.DMA((n * N_PEERS,)),
                   *[pltpu.HBM(a.shape, a.dtype) for a in arrays], jax.ShapeDtypeStruct((8, LANES), f32)),
        in_specs=[HBM_SPEC] * len(arrays),
        out_specs=(SEM_SPEC, SEM_SPEC, *[HBM_SPEC] * len(arrays), pl.BlockSpec(memory_space=pltpu.VMEM)),
        input_output_aliases={i: 2 + i for i in range(len(arrays))},
        compiler_params=pltpu.CompilerParams(has_side_effects=DATAFLOW),
    )(*arrays)
    return (outs[0], outs[1], outs[2:2 + n_src], outs[2 + n_src:2 + n_src + n]), outs[-1]


def _exchange_wait(state, after, name):
    send_sems, recv_sems, srcs, lands = state
    n_src, n = len(srcs), len(lands)

    def body(*refs):
        src_refs, land_refs = refs[:n_src], refs[n_src:n_src + n]
        send_ref, recv_ref = refs[n_src + n], refs[n_src + n + 1]
        for cp in _exchange_copies(src_refs, land_refs, send_ref, recv_ref, gather=not n_src):
            cp.wait_send()
            cp.wait_recv()

    arrays = (*srcs, *lands)
    outs = pl.pallas_call(
        body, name=name,
        out_shape=tuple(pltpu.HBM(a.shape, a.dtype) for a in arrays),
        in_specs=[HBM_SPEC] * len(arrays) + [SEM_SPEC, SEM_SPEC] + [pl.BlockSpec(memory_space=pl.ANY)] * len(after),
        out_specs=tuple([HBM_SPEC] * len(arrays)),
        input_output_aliases={i: i for i in range(len(arrays))},
        compiler_params=pltpu.CompilerParams(has_side_effects=DATAFLOW),
    )(*arrays, send_sems, recv_sems, *after)
    return outs[:n_src], outs[n_src:]


def _cast_to_slot(shards, me_idx, after):
    n = len(shards)

    def body(i_ref, *refs):
        for w in range(n):
            refs[n + 1 + w][...] = refs[w][...].astype(bf16)

    return pl.pallas_call(
        body, name="cast_to_slot",
        grid_spec=pltpu.PrefetchScalarGridSpec(
            num_scalar_prefetch=1, grid=(1,),
            in_specs=[pl.BlockSpec(s.shape, lambda i, m: (0, 0)) for s in shards] + [pl.BlockSpec(memory_space=pl.ANY)],
            out_specs=[pl.BlockSpec((None, *s.shape), lambda i, m: (m[0], 0, 0)) for s in shards]),
        out_shape=[jax.ShapeDtypeStruct((N_DEV, *s.shape), bf16) for s in shards],
        compiler_params=_cp(("arbitrary",), VMEM_LIMIT),
    )(me_idx, *shards, after)


def _row_block(rows, n_blocks):
    return (rows // n_blocks, True) if rows % (16 * n_blocks) == 0 else (rows, False)


def _adam_math(w, g, m, v):
    m = ADAM_B1 * m + (1.0 - ADAM_B1) * g
    v = ADAM_B2 * v + (1.0 - ADAM_B2) * (g * g)
    m_hat = m / (1.0 - ADAM_B1 ** ADAM_STEP)
    v_hat = v / (1.0 - ADAM_B2 ** ADAM_STEP)
    delta = -ADAM_LR * (m_hat / (jnp.sqrt(v_hat) + ADAM_EPS) + ADAM_WD * w)
    return delta, m, v


def _adam_shards(me_idx, blocks, lands, ws, ms, vs, n_blocks, name, after=()):
    n = len(blocks)

    def body(i_ref, *refs):
        ins, outs = refs[:5 * n], refs[5 * n + len(after):]
        for w in range(n):
            g_ref, b_ref, w_ref, m_ref, v_ref = (ins[t * n + w] for t in range(5))
            g = g_ref[...].astype(f32)
            for k in range(N_PEERS):
                g = g + b_ref[k].astype(f32)
            if len(w_ref.shape) == 2:
                pieces = [(slice(None), g)]
            else:
                pieces = [(a, g[2 * a:2 * a + 2]) for a in range(2)]
            for at, gp in pieces:
                vals = (gp,) + _adam_math(w_ref[at], gp, m_ref[at], v_ref[at])
                for t, val in enumerate(vals):
                    outs[4 * w + t][at] = val

    specs = [[] for _ in range(5)]
    out_specs, out_shape = [], []
    for g, wt in zip(blocks, ws):
        rows, cols = g.shape[1:]
        rb, cut = _row_block(rows, n_blocks)
        if not cut and wt.ndim == 2 and cols % (LANES * n_blocks) == 0:
            cb = cols // n_blocks
            specs[0].append(pl.BlockSpec((None, rows, cb), lambda i, s: (s[0], 0, i)))
            specs[1].append(pl.BlockSpec((N_PEERS, rows, cb), lambda i, s: (0, 0, i)))
            shard = pl.BlockSpec((rows, cb), lambda i, s: (0, i))
            for t in (2, 3, 4):
                specs[t].append(shard)
            out_specs += [shard] * 4
            out_shape += [jax.ShapeDtypeStruct(wt.shape, f32)] * 4
            continue
        specs[0].append(pl.BlockSpec((None, rb, cols), functools.partial(lambda i, s, cut: (s[0], i if cut else 0, 0), cut=cut)))
        specs[1].append(pl.BlockSpec((N_PEERS, rb, cols), functools.partial(lambda i, s, cut: (0, i if cut else 0, 0), cut=cut)))
        if wt.ndim == 2:
            shard = pl.BlockSpec((rb, cols), functools.partial(lambda i, s, cut: (i if cut else 0, 0), cut=cut))
        elif wt.shape[0] == 1:
            shard = pl.BlockSpec((None, rb, cols), functools.partial(lambda i, s, cut: (0, i if cut else 0, 0), cut=cut))
        else:
            shard = pl.BlockSpec(wt.shape, functools.partial(lambda i, s, nd: (0,) * nd, nd=wt.ndim))
        for t in (2, 3, 4):
            specs[t].append(shard)
        out_specs += [shard] * 4
        out_shape += [jax.ShapeDtypeStruct(wt.shape, f32)] * 4
    outs = pl.pallas_call(
        body, name=name,
        grid_spec=pltpu.PrefetchScalarGridSpec(
            num_scalar_prefetch=1, grid=(n_blocks,), in_specs=sum(specs, []) + [pl.BlockSpec(memory_space=pl.ANY)] * len(after),
            out_specs=out_specs),
        out_shape=out_shape,
        compiler_params=_cp(("arbitrary",), VMEM_LIMIT),
    )(me_idx, *blocks, *lands, *ws, *ms, *vs, *after)
    return [outs[4 * w:4 * w + 4] for w in range(n)]


def _adam_gains(parts, ws, ms, vs):
    n = len(ws)

    def body(p_ref, *refs):
        ins, outs = refs[:3 * n], refs[3 * n:]
        g_all = p_ref[0]
        for k in range(1, N_DEV):
            g_all = g_all + p_ref[k]
        for w, (off, lanes) in enumerate(SMALL.values()):
            w_ref, m_ref, v_ref = ins[w], ins[n + w], ins[2 * n + w]
            if len(w_ref.shape) == 2:
                pieces = [(slice(None), off, lanes)]
            else:
                pieces = [((slice(None), h), off + LANES * h, LANES) for h in range(w_ref.shape[1])]
            for at, o, ln in pieces:
                g = g_all[:, o:o + ln]
                vals = (g,) + _adam_math(w_ref[at], g, m_ref[at], v_ref[at])
                for t, val in enumerate(vals):
                    outs[4 * w + t][at] = val
        outs[4 * n][...] = g_all[:, LOSS_OFF:LOSS_OFF + LANES]

    out_shape = sum([[jax.ShapeDtypeStruct(w.shape, f32)] * 4 for w in ws], []) + [jax.ShapeDtypeStruct((1, LANES), f32)]
    outs = pl.pallas_call(body, name="adamw_gains", out_shape=out_shape)(parts, *ws, *ms, *vs)
    return [outs[4 * w:4 * w + 4] for w in range(n)], outs[4 * n]


def _fwd_in(x, g_mix, wz, tm):
    s, d = x.shape

    def body(x_ref, g_ref, w_ref, h_ref, z_ref):
        h, _ = _rms_fwd(x_ref[...], g_ref[...], d)
        hb = h.astype(bf16)
        h_ref[...] = hb
        z_ref[...] = _dot_nt(hb, w_ref[...])

    return pl.pallas_call(
        body, name="fwd_in", grid=(s // tm,),
        in_specs=[pl.BlockSpec((tm, d), lambda i: (i, 0)), _const_spec((1, d)), _const_spec((Z_W, d))],
        out_specs=[pl.BlockSpec((tm, d), lambda i: (i, 0)), pl.BlockSpec((tm, Z_W), lambda i: (i, 0))],
        out_shape=[jax.ShapeDtypeStruct((s, d), bf16), jax.ShapeDtypeStruct((s, Z_W), f32)],
        compiler_params=_cp(("parallel",), VMEM_LIMIT),
    )(x, g_mix, wz)


def _mla_qk_fwd(cq, ckv, g_qa, g_kva, wqb, wkvb):
    cqn, rq = _rms_fwd(cq, g_qa, Q_LORA)
    ckvn, rkv = _rms_fwd(ckv, g_kva, KV_LORA)
    cqn_b, ckvn_b = cqn.astype(bf16), ckvn.astype(bf16)
    q0 = _dot(cqn_b, wqb)
    kv0 = _dot(ckvn_b, wkvb)
    return cqn_b, rq, ckvn_b, rkv, q0, kv0


def _fwd_mla_proj(z, cosb, sina, sinb, g_qa, g_kva, wqb, wkvb, g_qn, g_kn, tm):
    s = z.shape[0]
    hh = MLA_HEADS

    def body(cq_ref, ckv_ref, kr_ref, c_ref, sa_ref, sb_ref, gqa_ref, gkva_ref, wqb_ref, wkvb_ref, gqn_ref, gkn_ref,
             q_ref, k_ref, v_ref):
        _, _, _, _, q0, kv0 = _mla_qk_fwd(cq_ref[...], ckv_ref[...], gqa_ref[...], gkva_ref[...], wqb_ref[...], wkvb_ref[...])
        kr = kr_ref[...]
        c, sa, sb = c_ref[...], sa_ref[...], sb_ref[...]
        gqn, gkn = gqn_ref[...], gkn_ref[...]
        kr_sq = jnp.sum(kr * kr, axis=-1, keepdims=True)
        for h in range(hh):
            qh = q0[:, QK_PAD * h:QK_PAD * (h + 1)]
            qn, _ = _rms_fwd(qh, gqn, QK_HEAD)
            q_ref[h, :, 0:128] = qn[:, 0:128].astype(bf16)
            q_ref[h, :, 128:256] = _rope(qn[:, 128:256], c, sa, sb).astype(bf16)
            kn_ = kv0[:, 256 * h:256 * h + 128]
            rk = lax.rsqrt((jnp.sum(kn_ * kn_, axis=-1, keepdims=True) + kr_sq) * (1.0 / QK_HEAD) + EPS)
            k_ref[h, :, 0:128] = (kn_ * rk * gkn[:, 0:128]).astype(bf16)
            k_ref[h, :, 128:256] = _rope(kr * rk * gkn[:, 128:256], c, sa, sb).astype(bf16)
            v_ref[h] = kv0[:, 256 * h + 128:256 * h + 256].astype(bf16)

    row128 = pl.BlockSpec((tm, 128), lambda i: (i, 0))
    return pl.pallas_call(
        body, name="fwd_mla_proj", grid=(s // tm,),
        in_specs=[pl.BlockSpec((tm, 256), lambda i: (i, Z_CQ // 256)), pl.BlockSpec((tm, 256), lambda i: (i, Z_CKV // 256)),
                  pl.BlockSpec((tm, 128), lambda i: (i, Z_KR // 128)), row128, row128, row128,
                  _const_spec((1, 256)), _const_spec((1, 256)), _const_spec((256, 1024)), _const_spec((256, 1024)),
                  _const_spec((1, 256)), _const_spec((1, 256))],
        out_specs=[pl.BlockSpec((hh, tm, QK_PAD), lambda i: (0, i, 0)), pl.BlockSpec((hh, tm, QK_PAD), lambda i: (0, i, 0)),
                   pl.BlockSpec((hh, tm, V_HEAD), lambda i: (0, i, 0))],
        out_shape=[jax.ShapeDtypeStruct((hh, s, QK_PAD), bf16), jax.ShapeDtypeStruct((hh, s, QK_PAD), bf16),
                   jax.ShapeDtypeStruct((hh, s, V_HEAD), bf16)],
        compiler_params=_cp(("parallel",), VMEM_LIMIT),
    )(z, z, z, cosb, sina, sinb, g_qa, g_kva, wqb, wkvb, g_qn, g_kn)


def _fwd_attn(q, k, v, tq, after):
    hh, s, _ = q.shape

    n_sub = max(1, tq // ATTN_SUB_ROWS)

    def body(q_ref, k_ref, v_ref, after_ref, o_ref, o32_ref):
        for t in range(n_sub):
            rows = slice(t * (tq // n_sub), (t + 1) * (tq // n_sub))
            sc = _dot_nt(q_ref[rows, :], k_ref[...])
            p = jnp.exp2((sc - jnp.max(sc, axis=-1, keepdims=True)) * (ATTN_SCALE * LOG2_E))
            l = jnp.sum(p, axis=-1, keepdims=True)
            o = _dot(p.astype(bf16), v_ref[...]) * (1.0 / l)
            o_ref[rows, :] = o.astype(bf16)
            o32_ref[rows, :] = o

    out = pl.BlockSpec((tq, V_HEAD), lambda h, i: (i, h))
    return pl.pallas_call(
        body, name="fwd_attn", grid=(hh, s // tq),
        in_specs=[pl.BlockSpec((None, tq, QK_PAD), lambda h, i: (h, i, 0)),
                  pl.BlockSpec((None, s, QK_PAD), lambda h, i: (h, 0, 0)),
                  pl.BlockSpec((None, s, V_HEAD), lambda h, i: (h, 0, 0)), pl.BlockSpec(memory_space=pl.ANY)],
        out_specs=[out, out],
        out_shape=[jax.ShapeDtypeStruct((s, hh * V_HEAD), bf16), jax.ShapeDtypeStruct((s, hh * V_HEAD), f32)],
        compiler_params=_cp(("parallel", "parallel"), VMEM_LIMIT),
    )(q, k, v, after)


def _split3(x):
    hi = x.astype(bf16)
    r1 = x - hi.astype(f32)
    mid = r1.astype(bf16)
    lo = (r1 - mid.astype(f32)).astype(bf16)
    return jnp.concatenate([hi, mid, lo], axis=-1)


def _tri_sum(tri, x):
    y = _dot(tri, _split3(x))
    return y[:, 0:128] + y[:, 128:256] + y[:, 256:384]


GLA_GROUP = 4
GLA_ROWS = GLA_GROUP * CHUNK
GLA_HEADS_PER_STEP = 2


def _gla_masks(rev):
    row = lax.broadcasted_iota(jnp.int32, (GLA_ROWS, GLA_ROWS), 0)
    col = lax.broadcasted_iota(jnp.int32, (GLA_ROWS, GLA_ROWS), 1)
    shift = CHUNK.bit_length() - 1
    same = (jnp.right_shift(row, shift) == jnp.right_shift(col, shift)).astype(f32)
    lower, upper = (row >= col).astype(f32) * same, (row <= col).astype(f32) * same
    keep, keep_t = (upper, lower) if rev else (lower, upper)
    chunk_of = jnp.right_shift(lax.broadcasted_iota(jnp.int32, (GLA_ROWS, 1), 0), shift)
    return keep, keep.astype(bf16), keep_t.astype(bf16), [(chunk_of == c).astype(f32) for c in range(GLA_GROUP)]


def _gla_gates(hq, hf, lower):
    sg = _sigmoid(hf)
    f = lower + (1.0 - lower) * sg
    return hq * _sigmoid(hq), 1.0 - f, jnp.log(f), f, sg


def _gla_last_mid(b, rev):
    b3 = b.reshape(GLA_GROUP, CHUNK, 128)
    last, mid = (0, CHUNK // 2) if rev else (CHUNK - 1, CHUNK // 2 - 1)
    return b3[:, last:last + 1, :], b3[:, mid:mid + 1, :]


def _gla_per_row(per_chunk):
    return jnp.broadcast_to(per_chunk, (GLA_GROUP, CHUNK, 128)).reshape(GLA_ROWS, 128)


def _gla_block_diag(x, row_masks):
    return jnp.concatenate([(x * m).astype(bf16) for m in row_masks], axis=-1)


def _gla_diag(y):
    return jnp.concatenate([y[CHUNK * c:CHUNK * (c + 1), 128 * c:128 * (c + 1)] for c in range(GLA_GROUP)], axis=0)


def _gla_rows(n, n_groups, rev):
    ne = n_groups - 1 - n if rev else n
    return pl.ds(pl.multiple_of(ne * GLA_ROWS, GLA_ROWS), GLA_ROWS), ne * GLA_GROUP


def _gla_scan_order(rev):
    return tuple(reversed(range(GLA_GROUP))) if rev else tuple(range(GLA_GROUP))


def _fwd_gla(z, lb4):
    s = z.shape[0]
    n_groups = s // GLA_ROWS
    assert n_groups % 2 == 0
    hp = GLA_HEADS_PER_STEP
    chains = [(hh, rev) for hh in range(hp) for rev in (False, True)]

    def body(hq_ref, hff_ref, hfb_ref, hi_ref, lb_ref, o_ref, b_ref, states_ref, st_ref, stage_ref, b_stage, sems):
        st_ref[...] = jnp.zeros_like(st_ref)
        masks = {rev: _gla_masks(rev) for rev in (False, True)}
        lowers = [_sigmoid(lb_ref[int(rev):int(rev) + 1, 128 * hh:128 * (hh + 1)]
                           - lb_ref[2 + int(rev):3 + int(rev), 128 * hh:128 * (hh + 1)]) for hh, rev in chains]

        def states_out(slot, ci, chunk0):
            hh, rev = chains[ci]
            head = pl.program_id(0) * hp + hh
            rows = pl.ds(pl.multiple_of(chunk0 * CHUNK, GLA_ROWS), GLA_ROWS)
            return _Both(
                pltpu.make_async_copy(stage_ref.at[slot, ci], states_ref.at[head, int(rev), pl.ds(chunk0, GLA_GROUP)],
                                      sems.at[slot, ci]),
                pltpu.make_async_copy(b_stage.at[slot, ci], b_ref.at[int(rev), rows, pl.ds(pl.multiple_of(head * 128, 128), 128)],
                                      sems.at[slot, len(chains) + ci]))

        def make_step(first):
            def step(n, carry):
                slot = n % 2

                @pl.when(n >= 2)
                def _():
                    for ci in range(len(chains)):
                        states_out(slot, ci, 0).wait()

                for ci, (hh, rev) in enumerate(chains):
                    cols = slice(128 * hh, 128 * (hh + 1))
                    rows, chunk0 = _gla_rows(n, n_groups, rev)
                    maskf, tri, _, row_masks = masks[rev]
                    hf_ref = hfb_ref if rev else hff_ref
                    q, k, logf, _, _ = _gla_gates(hq_ref[rows, cols], hf_ref[rows, cols], lowers[ci])
                    vb = hi_ref[rows, cols].astype(bf16)
                    b = _tri_sum(tri, logf)
                    b_stage[slot, ci] = b
                    b_last3, b_mid3 = _gla_last_mid(b, rev)
                    b_last, b_mid = _gla_per_row(b_last3), _gla_per_row(b_mid3)
                    qi = (q * jnp.exp(b - b_mid)).astype(bf16)
                    ki = (k * jnp.exp(b_mid - b)).astype(bf16)
                    a = (_dot_nt(qi, ki) * maskf).astype(bf16)
                    kv = _dot_tn(vb, _gla_block_diag(k * jnp.exp(b_last - b), row_masks))
                    decay3 = jnp.exp(b_last3)
                    st = st_ref[ci]
                    before = [None] * GLA_GROUP
                    for c in _gla_scan_order(rev):
                        stage_ref[slot, ci, c] = st
                        before[c] = st.astype(bf16)
                        st = st * decay3[c] + kv[:, 128 * c:128 * (c + 1)]
                    st_ref[ci] = st
                    states_out(slot, ci, chunk0).start()
                    inter = _dot_nt((q * jnp.exp(b)).astype(bf16), jnp.concatenate(before, axis=0))
                    o = _dot(a, vb) + _gla_diag(inter)
                    if first:
                        o_ref[rows, cols] = o
                    else:
                        o_ref[rows, cols] += o
                return carry
            return step

        lax.fori_loop(0, n_groups // 2, make_step(True), 0)
        lax.fori_loop(n_groups // 2, n_groups, make_step(False), 0)
        for slot in range(2):
            for ci in range(len(chains)):
                states_out(slot, ci, 0).wait()

    w = 128 * hp
    col = lambda base: pl.BlockSpec((s, w), lambda h: (0, base // w + h))
    return pl.pallas_call(
        body, name="fwd_gla", grid=(HG_HEADS // hp,),
        in_specs=[col(Z_HQ), col(Z_HFF), col(Z_HFB), col(Z_HI), pl.BlockSpec((4, w), lambda h: (0, h))],
        out_specs=[pl.BlockSpec((s, w), lambda h: (0, h)), pl.BlockSpec(memory_space=pl.ANY), pl.BlockSpec(memory_space=pl.ANY)],
        out_shape=[jax.ShapeDtypeStruct((s, HG_HEADS * 128), f32), jax.ShapeDtypeStruct((2, s, HG_HEADS * 128), f32),
                   jax.ShapeDtypeStruct((HG_HEADS, 2, s // CHUNK, 128, 128), f32)],
        scratch_shapes=[pltpu.VMEM((len(chains), 128, 128), f32), pltpu.VMEM((2, len(chains), GLA_GROUP, 128, 128), f32),
                        pltpu.VMEM((2, len(chains), GLA_ROWS, 128), f32), pltpu.SemaphoreType.DMA((2, 2 * len(chains)))],
        compiler_params=_cp(("parallel",), VMEM_LIMIT),
    )(z, z, z, z, lb4)


def _hg_out(o, hg, g_hgo):
    outs, ons, rs = [], [], []
    for h in range(HG_HEADS):
        oh = o[:, 128 * h:128 * (h + 1)]
        on, r = _rms_fwd(oh, g_hgo[:, 128 * h:128 * (h + 1)], 128)
        ons.append(on)
        rs.append(r)
    on = jnp.concatenate(ons, axis=-1)
    sg = _sigmoid(hg)
    return on * (hg * sg), on, rs, sg


def _fwd_mix(a, o, z, g_hgo, x, w_o, tm):
    s, d = x.shape

    def body(a_ref, o_ref, hg_ref, g_ref, x_ref, w_ref, x2_ref, cat_ref):
        r, _, _, _ = _hg_out(o_ref[...], hg_ref[...], g_ref[...])
        cat = jnp.concatenate([a_ref[...], r.astype(bf16)], axis=-1)
        cat_ref[...] = cat
        x2_ref[...] = x_ref[...] + _dot(cat, w_ref[...])

    row512 = pl.BlockSpec((tm, 512), lambda i: (i, 0))
    rowd = pl.BlockSpec((tm, d), lambda i: (i, 0))
    return pl.pallas_call(
        body, name="fwd_mix", grid=(s // tm,),
        in_specs=[row512, row512, pl.BlockSpec((tm, 512), lambda i: (i, Z_HG // 512)), _const_spec((1, 512)), rowd,
                  _const_spec((d, d))],
        out_specs=[rowd, rowd],
        out_shape=[jax.ShapeDtypeStruct((s, d), f32), jax.ShapeDtypeStruct((s, d), bf16)],
        compiler_params=_cp(("parallel",), VMEM_LIMIT),
    )(a, o, z, g_hgo, x, w_o)


def _fwd_ffn(x2, g_ffn, w_gate, w_up, w_down, tm):
    s, d = x2.shape

    def body(x_ref, g_ref, wg_ref, wu_ref, wd_ref, x3_ref, gp_ref, up_ref):
        x = x_ref[...]
        h, _ = _rms_fwd(x, g_ref[...], d)
        hb = h.astype(bf16)
        gp = _dot_nt(hb, wg_ref[...])
        up = _dot_nt(hb, wu_ref[...])
        gp_ref[...] = gp.astype(bf16)
        up_ref[...] = up.astype(bf16)
        act = (gp * _sigmoid(gp) * up).astype(bf16)
        x3_ref[...] = x + _dot(act, wd_ref[...])

    rowd = pl.BlockSpec((tm, d), lambda i: (i, 0))
    rowf = pl.BlockSpec((tm, D_FF), lambda i: (i, 0))
    return pl.pallas_call(
        body, name="fwd_ffn", grid=(s // tm,),
        in_specs=[rowd, _const_spec((1, d)), _const_spec((D_FF, d)), _const_spec((D_FF, d)), _const_spec((D_FF, d))],
        out_specs=[rowd, rowf, rowf],
        out_shape=[jax.ShapeDtypeStruct((s, d), f32), jax.ShapeDtypeStruct((s, D_FF), bf16),
                   jax.ShapeDtypeStruct((s, D_FF), bf16)],
        compiler_params=_cp(("parallel",), VMEM_LIMIT),
    )(x2, g_ffn, w_gate, w_up, w_down)


def _ple_loss_fwd_bwd(x3, g_ple, w_pg, p, w_pp, target, tm):
    s, d = x3.shape

    def body(x_ref, g_ref, wg_ref, p_ref, wp_ref, t_ref, dx_ref, h_ref, dpre_ref, dpp_ref, dg_ref, loss_ref):
        @pl.when(pl.program_id(0) == 0)
        def _():
            dg_ref[...] = jnp.zeros_like(dg_ref)
            loss_ref[...] = jnp.zeros_like(loss_ref)

        x = x_ref[...]
        g = g_ref[...]
        h, r = _rms_fwd(x, g, d)
        hb = h.astype(bf16)
        gate = _sigmoid(_dot(hb, wg_ref[...]))
        pp = _dot(p_ref[...].astype(bf16), wp_ref[...])
        e = x + gate * pp - t_ref[...]
        loss_ref[...] += 0.5 * jnp.sum(e * e) * (1.0 / d)
        dy = e * (1.0 / d)
        dpre = (dy * pp * gate * (1.0 - gate)).astype(bf16)
        dx, dgx = _rms_bwd(_dot_nt(dpre, wg_ref[...]), x, r, g, d)
        dx_ref[...] = dy + dx
        dg_ref[...] += jnp.sum(dgx, axis=0, keepdims=True)
        h_ref[...] = hb
        dpre_ref[...] = dpre
        dpp_ref[...] = (dy * gate).astype(bf16)

    rowd = pl.BlockSpec((tm, d), lambda i: (i, 0))
    return pl.pallas_call(
        body, name="ple_loss_fwd_bwd", grid=(s // tm,),
        in_specs=[rowd, _const_spec((1, d)), _const_spec((d, d)), pl.BlockSpec((tm, PLE_DIM), lambda i: (i, 0)),
                  _const_spec((PLE_DIM, d)), rowd],
        out_specs=[rowd, rowd, rowd, rowd, _acc_spec((1, d)), _acc_spec((8, 128))],
        out_shape=[jax.ShapeDtypeStruct((s, d), f32), jax.ShapeDtypeStruct((s, d), bf16), jax.ShapeDtypeStruct((s, d), bf16),
                   jax.ShapeDtypeStruct((s, d), bf16), jax.ShapeDtypeStruct((1, d), f32), jax.ShapeDtypeStruct((8, 128), f32)],
        compiler_params=_cp(("arbitrary",), VMEM_LIMIT),
    )(x3, g_ple, w_pg, p, w_pp, target)


def _bwd_ffn_hidden(d3, gp, up, w_down, tm, tf):
    s, d = d3.shape

    def body(d3_ref, gp_ref, up_ref, wd_ref, act_ref, dgp_ref, dup_ref):
        gp, up = gp_ref[...].astype(f32), up_ref[...].astype(f32)
        sg = _sigmoid(gp)
        silu = gp * sg
        act_ref[...] = (silu * up).astype(bf16)
        dact = _dot_nt(d3_ref[...].astype(bf16), wd_ref[...])
        dgp_ref[...] = (dact * up * (sg * (1.0 + gp * (1.0 - sg)))).astype(bf16)
        dup_ref[...] = (dact * silu).astype(bf16)

    rowf = pl.BlockSpec((tm, tf), lambda f, i: (i, f))
    return pl.pallas_call(
        body, name="bwd_ffn_hidden", grid=(D_FF // tf, s // tm),
        in_specs=[pl.BlockSpec((tm, d), lambda f, i: (i, 0)), rowf, rowf, pl.BlockSpec((tf, d), lambda f, i: (f, 0))],
        out_specs=[rowf, rowf, rowf],
        out_shape=[jax.ShapeDtypeStruct((s, D_FF), bf16)] * 3,
        compiler_params=_cp(("parallel", "parallel"), VMEM_LIMIT),
    )(d3, gp, up, w_down)


def _bwd_ffn_in(d3, x2, dgp, dup, g_ffn, w_gate, w_up, tm):
    s, d = x2.shape

    def body(d3_ref, x_ref, dgp_ref, dup_ref, g_ref, wg_ref, wu_ref, d2_ref, h_ref, dg_ref):
        @pl.when(pl.program_id(0) == 0)
        def _():
            dg_ref[...] = jnp.zeros_like(dg_ref)

        x, g = x_ref[...], g_ref[...]
        dh = _dot(dgp_ref[...], wg_ref[...]) + _dot(dup_ref[...], wu_ref[...])
        h, r = _rms_fwd(x, g, d)
        h_ref[...] = h.astype(bf16)
        dx, dgx = _rms_bwd(dh, x, r, g, d)
        d2_ref[...] = d3_ref[...] + dx
        dg_ref[...] += jnp.sum(dgx, axis=0, keepdims=True)

    rowd = pl.BlockSpec((tm, d), lambda i: (i, 0))
    rowf = pl.BlockSpec((tm, D_FF), lambda i: (i, 0))
    return pl.pallas_call(
        body, name="bwd_ffn_in", grid=(s // tm,),
        in_specs=[rowd, rowd, rowf, rowf, _const_spec((1, d)), _const_spec((D_FF, d)), _const_spec((D_FF, d))],
        out_specs=[rowd, rowd, _acc_spec((1, d))],
        out_shape=[jax.ShapeDtypeStruct((s, d), f32), jax.ShapeDtypeStruct((s, d), bf16), jax.ShapeDtypeStruct((1, d), f32)],
        compiler_params=_cp(("arbitrary",), VMEM_LIMIT),
    )(d3, x2, dgp, dup, g_ffn, w_gate, w_up)


def _bwd_mix(d2, w_o, o, z, g_hgo, tm):
    s, d = d2.shape

    def body(d2_ref, w_ref, o_ref, hg_ref, g_ref, da_ref, do_ref, dhg_ref, dg_ref):
        @pl.when(pl.program_id(0) == 0)
        def _():
            dg_ref[...] = jnp.zeros_like(dg_ref)

        dcat = _dot_nt(d2_ref[...].astype(bf16), w_ref[...])
        da_ref[...] = dcat[:, 0:512].astype(bf16)
        dr = dcat[:, 512:1024]
        o, hg, g = o_ref[...], hg_ref[...], g_ref[...]
        _, on, rs, sg = _hg_out(o, hg, g)
        dhg_ref[...] = (dr * on * (sg * (1.0 + hg * (1.0 - sg)))).astype(bf16)
        don = dr * (hg * sg)
        dgs = []
        for h in range(HG_HEADS):
            cols = slice(128 * h, 128 * (h + 1))
            dx, dgx = _rms_bwd(don[:, cols], o[:, cols], rs[h], g[:, cols], 128)
            do_ref[:, cols] = dx
            dgs.append(jnp.sum(dgx, axis=0, keepdims=True))
        dg_ref[...] += jnp.concatenate(dgs, axis=-1)

    row512 = pl.BlockSpec((tm, 512), lambda i: (i, 0))
    return pl.pallas_call(
        body, name="bwd_mix", grid=(s // tm,),
        in_specs=[pl.BlockSpec((tm, d), lambda i: (i, 0)), _const_spec((d, d)), row512,
                  pl.BlockSpec((tm, 512), lambda i: (i, Z_HG // 512)), _const_spec((1, 512))],
        out_specs=[row512, row512, row512, _acc_spec((1, 512))],
        out_shape=[jax.ShapeDtypeStruct((s, 512), bf16), jax.ShapeDtypeStruct((s, 512), f32), jax.ShapeDtypeStruct((s, 512), bf16),
                   jax.ShapeDtypeStruct((1, 512), f32)],
        compiler_params=_cp(("arbitrary",), VMEM_LIMIT),
    )(d2, w_o, o, z, g_hgo)


def _bwd_gla(z, lb4, do, b_fwd, states):
    s = z.shape[0]
    n_chunks = s // CHUNK
    n_groups = s // GLA_ROWS
    assert n_groups % 2 == 0

    def body(hq_ref, hff_ref, hfb_ref, hi_ref, lb_ref, do_ref, b_all, st_all, dhq_ref, dhff_ref, dhfb_ref, dhi_ref, dlb_ref,
             dst_ref, dq_acc, dv_acc, dlow_ref):
        dirs = (False, True)
        masks = [_gla_masks(rev) for rev in dirs]
        lowers = [_sigmoid(lb_ref[int(rev):int(rev) + 1, :] - lb_ref[2 + int(rev):3 + int(rev), :]) for rev in dirs]
        hf_refs, dhf_refs = (hff_ref, hfb_ref), (dhff_ref, dhfb_ref)

        dst_ref[...] = jnp.zeros_like(dst_ref)
        dlow_ref[...] = jnp.zeros_like(dlow_ref)

        def make_bwd_step(first):
            def bwd_step(j, carry):
                n = n_groups - 1 - j
                for d, rev in enumerate(dirs):
                    maskf, _, tri_t, row_masks = masks[d]
                    lower = lowers[d]
                    rows, chunk0 = _gla_rows(n, n_groups, rev)
                    hq, hf = hq_ref[rows, :], hf_refs[d][rows, :]
                    q, k, _, f, sg = _gla_gates(hq, hf, lower)
                    v = hi_ref[rows, :]
                    dout = do_ref[rows, :]
                    b = b_all[d, rows, :]
                    b_last3, b_mid3 = _gla_last_mid(b, rev)
                    b_last, b_mid = _gla_per_row(b_last3), _gla_per_row(b_mid3)
                    e1, e2, e3, e4 = jnp.exp(b - b_mid), jnp.exp(b_mid - b), jnp.exp(b_last - b), jnp.exp(b)
                    decay3 = jnp.exp(b_last3)
                    qi, ki, kt, qt = q * e1, k * e2, k * e3, q * e4
                    qib, kib, ktb = qi.astype(bf16), ki.astype(bf16), kt.astype(bf16)
                    vb, dob = v.astype(bf16), dout.astype(bf16)
                    a = (_dot_nt(qib, kib) * maskf).astype(bf16)
                    da = (_dot_nt(dob, vb) * maskf).astype(bf16)
                    dqi = _dot(da, kib)
                    dki = _dot_tn(da, qib)
                    into_state = _dot_tn(dob, _gla_block_diag(qt, row_masks))
                    dst = dst_ref[d]
                    sts, dsts, ddecay = [None] * GLA_GROUP, [None] * GLA_GROUP, [None] * GLA_GROUP
                    for c in reversed(_gla_scan_order(rev)):
                        sts[c] = st_all[d, chunk0 + c]
                        dsts[c] = dst.astype(bf16)
                        ddecay[c] = jnp.sum(dst * sts[c], axis=0, keepdims=True)[None]
                        dst = dst * decay3[c] + into_state[:, 128 * c:128 * (c + 1)]
                    dst_ref[d] = dst
                    dv = _dot_tn(a, dob) + _gla_diag(_dot_nt(ktb, jnp.concatenate(dsts, axis=0)))
                    dqt = _gla_diag(_dot(dob, jnp.concatenate([x.astype(bf16) for x in sts], axis=-1)))
                    dkt = _gla_diag(_dot(vb, jnp.concatenate(dsts, axis=-1)))
                    dq = dqi * e1 + dqt * e4
                    dk = dki * e2 + dkt * e3
                    db = dqi * qi - dki * ki + dqt * qt - dkt * kt
                    dlast3 = (jnp.sum((dkt * kt).reshape(GLA_GROUP, CHUNK, 128), axis=1, keepdims=True)
                              + jnp.concatenate(ddecay, axis=0) * decay3)
                    dlogf = _tri_sum(tri_t, db) + _gla_per_row(dlast3)
                    df = dlogf / f - dk
                    dhf_refs[d][rows, :] = (df * (1.0 - lower) * sg * (1.0 - sg)).astype(bf16)
                    dlow_ref[d:d + 1, :] += jnp.sum(df * (1.0 - sg), axis=0, keepdims=True)
                    sq = _sigmoid(hq)
                    dhq = dq * (sq * (1.0 + hq * (1.0 - sq)))
                    if first:
                        dq_acc[rows, :] = dhq
                        dv_acc[rows, :] = dv
                    else:
                        dhq_ref[rows, :] = (dq_acc[rows, :] + dhq).astype(bf16)
                        dhi_ref[rows, :] = (dv_acc[rows, :] + dv).astype(bf16)
                return carry
            return bwd_step

        lax.fori_loop(0, n_groups // 2, make_bwd_step(True), 0, unroll=2)
        lax.fori_loop(n_groups // 2, n_groups, make_bwd_step(False), 0, unroll=2)

        for d in range(2):
            dl = dlow_ref[d:d + 1, :] * lowers[d] * (1.0 - lowers[d])
            dlb_ref[d:d + 1, :] = dl
            dlb_ref[2 + d:3 + d, :] = -dl

    col = lambda base: pl.BlockSpec((s, 128), lambda h: (0, base // 128 + h))
    return pl.pallas_call(
        body, name="bwd_gla", grid=(HG_HEADS,),
        in_specs=[col(Z_HQ), col(Z_HFF), col(Z_HFB), col(Z_HI), pl.BlockSpec((4, 128), lambda h: (0, h)), col(0),
                  pl.BlockSpec((2, s, 128), lambda h: (0, 0, h)),
                  pl.BlockSpec((None, 2, n_chunks, 128, 128), lambda h: (h, 0, 0, 0, 0), pipeline_mode=pl.Buffered(1))],
        out_specs=[col(0), col(0), col(0), col(0), pl.BlockSpec((4, 128), lambda h: (0, h))],
        out_shape=[jax.ShapeDtypeStruct((s, 512), bf16)] * 4 + [jax.ShapeDtypeStruct((4, 512), f32)],
        scratch_shapes=[pltpu.VMEM((2, 128, 128), f32), pltpu.VMEM((s, 128), f32), pltpu.VMEM((s, 128), f32),
                        pltpu.VMEM((2, 128), f32)],
        compiler_params=_cp(("parallel",), VMEM_LIMIT),
    )(z, z, z, z, lb4, do, b_fwd, states)


def _bwd_attn(q, k, v, da, a32, tq):
    hh, s, _ = q.shape

    n_sub = max(1, tq // ATTN_SUB_ROWS)

    def body(q_ref, k_ref, v_ref, do_ref, o_ref, dq_ref, dk_ref, dv_ref, p_all, ds_all, dol_ref, dkt_ref, dvt_ref):
        @pl.when(pl.program_id(1) == 0)
        def _():
            dkt_ref[...] = jnp.zeros_like(dkt_ref)
            dvt_ref[...] = jnp.zeros_like(dvt_ref)

        kb, vb = k_ref[...], v_ref[...]
        for t in range(n_sub):
            rows = slice(t * (tq // n_sub), (t + 1) * (tq // n_sub))
            sc = _dot_nt(q_ref[rows, :], kb)
            p = jnp.exp2((sc - jnp.max(sc, axis=-1, keepdims=True)) * (ATTN_SCALE * LOG2_E))
            inv_l = 1.0 / jnp.sum(p, axis=-1, keepdims=True)
            p_all[rows, :] = p.astype(bf16)
            dob = do_ref[rows, :]
            dof = dob.astype(f32)
            delta = jnp.sum(dof * o_ref[rows, :], axis=-1, keepdims=True)
            ds_all[rows, :] = p_all[rows, :] * ((_dot_nt(dob, vb) - delta) * inv_l).astype(bf16)
            dq_ref[rows, :] = _dot(ds_all[rows, :], kb) * ATTN_SCALE
            dol_ref[rows, :] = (dof * inv_l).astype(bf16)
        dkt_ref[...] += _dot_tn(q_ref[...], ds_all[...])
        dvt_ref[...] += _dot_tn(dol_ref[...], p_all[...])

        @pl.when(pl.program_id(1) == s // tq - 1)
        def _():
            dk_ref[...] = dkt_ref[...].T * ATTN_SCALE
            dv_ref[...] = dvt_ref[...].T

    return pl.pallas_call(
        body, name="bwd_attn", grid=(hh, s // tq),
        in_specs=[pl.BlockSpec((None, tq, QK_PAD), lambda h, i: (h, i, 0)),
                  pl.BlockSpec((None, s, QK_PAD), lambda h, i: (h, 0, 0)),
                  pl.BlockSpec((None, s, V_HEAD), lambda h, i: (h, 0, 0)),
                  pl.BlockSpec((tq, V_HEAD), lambda h, i: (i, h)), pl.BlockSpec((tq, V_HEAD), lambda h, i: (i, h))],
        out_specs=[pl.BlockSpec((None, tq, QK_PAD), lambda h, i: (h, i, 0)),
                   pl.BlockSpec((None, s, QK_PAD), lambda h, i: (h, 0, 0)),
                   pl.BlockSpec((None, s, V_HEAD), lambda h, i: (h, 0, 0))],
        out_shape=[jax.ShapeDtypeStruct((hh, s, QK_PAD), f32), jax.ShapeDtypeStruct((hh, s, QK_PAD), f32),
                   jax.ShapeDtypeStruct((hh, s, V_HEAD), f32)],
        scratch_shapes=[pltpu.VMEM((tq, s), bf16), pltpu.VMEM((tq, s), bf16), pltpu.VMEM((tq, V_HEAD), bf16),
                        pltpu.VMEM((QK_PAD, s), f32), pltpu.VMEM((V_HEAD, s), f32)],
        compiler_params=_cp(("parallel", "arbitrary"), VMEM_LIMIT),
    )(q, k, v, da, a32)


def _bwd_mla_proj(z, dq, dk, dv, cosb, sina, sinb, g_qa, g_kva, wqb, wkvb, g_qn, g_kn, tm):
    s = z.shape[0]
    hh = MLA_HEADS

    def body(cq_ref, ckv_ref, kr_ref, dq_ref, dk_ref, dv_ref, c_ref, sa_ref, sb_ref, gqa_ref, gkva_ref, wqb_ref, wkvb_ref,
             gqn_ref, gkn_ref, dz_ref, gwqb_ref, gwkvb_ref, dgqa_ref, dgkva_ref, dgqn_ref, dgkn_ref, dq0_ref, dkv0_ref):
        @pl.when(pl.program_id(0) == 0)
        def _():
            for r in (gwqb_ref, gwkvb_ref, dgqa_ref, dgkva_ref, dgqn_ref, dgkn_ref):
                r[...] = jnp.zeros_like(r)

        cq, ckv, kr = cq_ref[...], ckv_ref[...], kr_ref[...]
        gqa, gkva, gqn, gkn = gqa_ref[...], gkva_ref[...], gqn_ref[...], gkn_ref[...]
        cqn_b, rq, ckvn_b, rkv, q0, kv0 = _mla_qk_fwd(cq, ckv, gqa, gkva, wqb_ref[...], wkvb_ref[...])
        c, sa, sb = c_ref[...], -sa_ref[...], -sb_ref[...]
        kr_sq = jnp.sum(kr * kr, axis=-1, keepdims=True)
        dkr = jnp.zeros_like(kr)
        dgqn = jnp.zeros((1, QK_PAD), f32)
        dgkn = jnp.zeros((1, QK_PAD), f32)
        for h in range(hh):
            qh = q0[:, QK_PAD * h:QK_PAD * (h + 1)]
            rh = lax.rsqrt(jnp.sum(qh * qh, axis=-1, keepdims=True) * (1.0 / QK_HEAD) + EPS)
            dqh = dq_ref[h]
            dqn = jnp.concatenate([dqh[:, 0:128], _rope(dqh[:, 128:256], c, sa, sb)], axis=-1)
            dq0h, dgx = _rms_bwd(dqn, qh, rh, gqn, QK_HEAD)
            dq0_ref[:, QK_PAD * h:QK_PAD * (h + 1)] = dq0h.astype(bf16)
            dgqn = dgqn + jnp.sum(dgx, axis=0, keepdims=True)

            kn_ = kv0[:, 256 * h:256 * h + 128]
            k0 = jnp.concatenate([kn_, kr], axis=-1)
            rk = lax.rsqrt((jnp.sum(kn_ * kn_, axis=-1, keepdims=True) + kr_sq) * (1.0 / QK_HEAD) + EPS)
            dkh = dk_ref[h]
            dkn = jnp.concatenate([dkh[:, 0:128], _rope(dkh[:, 128:256], c, sa, sb)], axis=-1)
            dk0, dgx = _rms_bwd(dkn, k0, rk, gkn, QK_HEAD)
            dgkn = dgkn + jnp.sum(dgx, axis=0, keepdims=True)
            dkv0_ref[:, 256 * h:256 * h + 128] = dk0[:, 0:128].astype(bf16)
            dkv0_ref[:, 256 * h + 128:256 * h + 256] = dv_ref[h].astype(bf16)
            dkr = dkr + dk0[:, 128:256]
        dgqn_ref[...] += dgqn
        dgkn_ref[...] += dgkn
        gwqb_ref[...] += _dot_tn(cqn_b, dq0_ref[...])
        gwkvb_ref[...] += _dot_tn(ckvn_b, dkv0_ref[...])
        dcq, dgx = _rms_bwd(_dot_nt(dq0_ref[...], wqb_ref[...]), cq, rq, gqa, Q_LORA)
        dgqa_ref[...] += jnp.sum(dgx, axis=0, keepdims=True)
        dckv, dgx = _rms_bwd(_dot_nt(dkv0_ref[...], wkvb_ref[...]), ckv, rkv, gkva, KV_LORA)
        dgkva_ref[...] += jnp.sum(dgx, axis=0, keepdims=True)
        dz_ref[:, 0:256] = dcq.astype(bf16)
        dz_ref[:, 256:512] = dckv.astype(bf16)
        dz_ref[:, 512:640] = dkr.astype(bf16)

    row128 = pl.BlockSpec((tm, 128), lambda i: (i, 0))
    hd = lambda w: pl.BlockSpec((hh, tm, w), lambda i: (0, i, 0))
    return pl.pallas_call(
        body, name="bwd_mla_proj", grid=(s // tm,),
        in_specs=[pl.BlockSpec((tm, 256), lambda i: (i, Z_CQ // 256)), pl.BlockSpec((tm, 256), lambda i: (i, Z_CKV // 256)),
                  pl.BlockSpec((tm, 128), lambda i: (i, Z_KR // 128)), hd(QK_PAD), hd(QK_PAD), hd(V_HEAD),
                  row128, row128, row128,
                  _const_spec((1, 256)), _const_spec((1, 256)), _const_spec((256, 1024)), _const_spec((256, 1024)),
                  _const_spec((1, 256)), _const_spec((1, 256))],
        out_specs=[pl.BlockSpec((tm, 640), lambda i: (i, 0)), _acc_spec((256, 1024)), _acc_spec((256, 1024)),
                   _acc_spec((1, 256)), _acc_spec((1, 256)), _acc_spec((1, 256)), _acc_spec((1, 256))],
        out_shape=[jax.ShapeDtypeStruct((s, 640), bf16), jax.ShapeDtypeStruct((256, 1024), f32), jax.ShapeDtypeStruct((256, 1024), f32)]
        + [jax.ShapeDtypeStruct((1, 256), f32)] * 4,
        scratch_shapes=[pltpu.VMEM((tm, 1024), bf16), pltpu.VMEM((tm, 1024), bf16)],
        compiler_params=_cp(("arbitrary",), VMEM_LIMIT),
    )(z, z, z, dq, dk, dv, cosb, sina, sinb, g_qa, g_kva, wqb, wkvb, g_qn, g_kn)


def _bwd_in(segments, wz, x, g_mix, d2, tm):
    s, d = x.shape
    n_seg = len(segments)

    def body(*refs):
        dz_refs, w_refs = refs[:n_seg], refs[n_seg:2 * n_seg]
        x_ref, g_ref, d2_ref, gx_ref, dg_ref = refs[2 * n_seg:]

        @pl.when(pl.program_id(0) == 0)
        def _():
            dg_ref[...] = jnp.zeros_like(dg_ref)

        dh = _dot(dz_refs[0][...], w_refs[0][...])
        for a_ref, w_ref in zip(dz_refs[1:], w_refs[1:]):
            dh = dh + _dot(a_ref[...], w_ref[...])
        x, g = x_ref[...], g_ref[...]
        r = lax.rsqrt(jnp.sum(x * x, axis=-1, keepdims=True) * (1.0 / d) + EPS)
        dx, dgx = _rms_bwd(dh, x, r, g, d)
        gx_ref[...] = d2_ref[...] + dx
        dg_ref[...] += jnp.sum(dgx, axis=0, keepdims=True)

    rowd = pl.BlockSpec((tm, d), lambda i: (i, 0))
    dz_specs = [pl.BlockSpec((tm, w), functools.partial(lambda i, j: (i, j), j=ja)) for _, w, ja, _ in segments]
    w_specs = [pl.BlockSpec((w, d), functools.partial(lambda i, j: (j, 0), j=jw), pipeline_mode=pl.Buffered(1))
               for _, w, _, jw in segments]
    return pl.pallas_call(
        body, name="bwd_in", grid=(s // tm,),
        in_specs=dz_specs + w_specs + [rowd, _const_spec((1, d)), rowd],
        out_specs=[rowd, _acc_spec((1, d))],
        out_shape=[jax.ShapeDtypeStruct((s, d), f32), jax.ShapeDtypeStruct((1, d), f32)],
        compiler_params=_cp(("arbitrary",), VMEM_LIMIT),
    )(*[a for a, _, _, _ in segments], *([wz] * n_seg), x, g_mix, d2)


def _pick_tile(n, cap):
    best = None
    for t in range(LANES, cap + 1, LANES):
        if n % t == 0:
            best = t
    return best if best is not None else n


def _mm_tn_many(a, bs, name, tm, transposed=False):
    kk, m = a.shape
    n_b = len(bs)
    tk = min(1024, kk)
    n_k = kk // tk

    def body(a_ref, *refs):
        b_refs, o_refs, acc_refs = refs[:n_b], refs[n_b:2 * n_b], refs[2 * n_b:]

        @pl.when(pl.program_id(1) == 0)
        def _():
            for acc in acc_refs:
                acc[...] = jnp.zeros_like(acc)
        a_blk = a_ref[...].astype(bf16)
        for b_ref, acc in zip(b_refs, acc_refs):
            acc[...] += _dot_tn(a_blk, b_ref[...].astype(bf16))

        @pl.when(pl.program_id(1) == n_k - 1)
        def _():
            for o_ref, acc in zip(o_refs, acc_refs):
                o_ref[...] = (acc[...].T if transposed else acc[...]).astype(bf16)

    if transposed:
        out_specs = [pl.BlockSpec((b.shape[1], tm), lambda i, k: (0, i)) for b in bs]
        out_shape = [jax.ShapeDtypeStruct((b.shape[1], m), bf16) for b in bs]
    else:
        out_specs = [pl.BlockSpec((tm, b.shape[1]), lambda i, k: (i, 0)) for b in bs]
        out_shape = [jax.ShapeDtypeStruct((m, b.shape[1]), bf16) for b in bs]
    return pl.pallas_call(
        body, name=name, grid=(m // tm, n_k),
        in_specs=[pl.BlockSpec((tk, tm), lambda i, k: (k, i))] + [pl.BlockSpec((tk, b.shape[1]), lambda i, k: (k, 0)) for b in bs],
        out_specs=out_specs,
        out_shape=out_shape,
        scratch_shapes=[pltpu.VMEM((tm, b.shape[1]), f32) for b in bs],
        compiler_params=_cp(("parallel", "arbitrary"), VMEM_LIMIT),
    )(a, *bs)


def _mm_tn(a, b, name):
    kk, m = a.shape
    _, n = b.shape
    tm = _pick_tile(m, 1408)
    tn = _pick_tile(n, 1408)
    tk = min(1024, kk)

    n_k = kk // tk

    def body(a_ref, b_ref, o_ref, acc_ref):
        @pl.when(pl.program_id(2) == 0)
        def _():
            acc_ref[...] = jnp.zeros_like(acc_ref)
        acc_ref[...] += _dot_tn(a_ref[...].astype(bf16), b_ref[...].astype(bf16))

        @pl.when(pl.program_id(2) == n_k - 1)
        def _():
            o_ref[...] = acc_ref[...].astype(bf16)

    return pl.pallas_call(
        body, name=name, grid=(m // tm, n // tn, n_k),
        in_specs=[pl.BlockSpec((tk, tm), lambda i, j, k: (k, i)), pl.BlockSpec((tk, tn), lambda i, j, k: (k, j))],
        out_specs=pl.BlockSpec((tm, tn), lambda i, j, k: (i, j)),
        out_shape=jax.ShapeDtypeStruct((m, n), bf16),
        scratch_shapes=[pltpu.VMEM((tm, tn), f32)],
        compiler_params=_cp(("parallel", "parallel", "arbitrary"), VMEM_LIMIT),
    )(a, b)


def _rope_tables(positions):
    inv_freq = ROPE_THETA ** (-jnp.arange(0, QK_ROPE, 2, dtype=f32) / QK_ROPE)
    ang = positions.astype(f32)[:, None] * inv_freq
    cos, sin = jnp.cos(ang), jnp.sin(ang)
    zero = jnp.zeros_like(cos)
    return (jnp.concatenate([cos, cos, zero, zero], axis=1), jnp.concatenate([zero, sin, zero, zero], axis=1),
            jnp.concatenate([-sin, zero, zero, zero], axis=1))


def _pad256(g):
    return jnp.pad(g.reshape(1, QK_HEAD), ((0, 0), (0, QK_PAD - QK_HEAD)))


RELAYOUT_BLOCKS = 8
FIRST = ("w_in", "w_qb", "w_kvb", "lb_param")
SECOND = ("w_o", "w_gate", "w_up", "w_down", "w_ple_gate", "w_ple_proj")
ROW_SHARDED = ("w_o", "w_down", "w_ple_gate")


def _col_moves(j):
    width = BIG["w_in"][0]
    lo = width * j
    w_in = [(max(lo, a) - lo, min(lo + width, b) - lo, d + max(lo, a) - a)
            for a, b, d in Z_SEGMENTS if max(lo, a) < min(lo + width, b)]
    head, half = divmod(j, 2)
    whole = lambda n: [(0, BIG[n][1], BIG[n][1] * j)]
    return {"w_in": w_in, "w_qb": [(0, 96, QK_PAD * head + 96 * half)], "w_kvb": whole("w_kvb"),
            "w_ple_proj": whole("w_ple_proj"), "lb_param": whole("lb_param")}


def _kernel_shape(name):
    rows, cols = BIG[name]
    if name == "w_in":
        return (Z_W, cols)
    return (rows, MLA_HEADS * QK_PAD if name == "w_qb" else N_DEV * cols)


def _relayout_specs(names, by_dev):
    specs = []
    for n in names:
        rows, cols = BIG[n]
        if n == "lb_param":
            specs.append(_acc_spec((N_DEV, rows, cols) if by_dev else _kernel_shape(n)))
        elif n == "w_in":
            cb = cols // RELAYOUT_BLOCKS
            specs.append(pl.BlockSpec((N_DEV, rows, cb), lambda i: (0, 0, i)) if by_dev else pl.BlockSpec((Z_W, cb), lambda i: (0, i)))
        elif by_dev:
            specs.append(pl.BlockSpec((N_DEV, rows // RELAYOUT_BLOCKS, cols), lambda i: (0, i, 0)))
        else:
            specs.append(pl.BlockSpec((rows // RELAYOUT_BLOCKS, _kernel_shape(n)[1]), lambda i: (i, 0)))
    return specs


def _weights_in(gathered, names, name):
    n = len(names)

    def body(*refs):
        ins, outs = dict(zip(names, refs[:n])), dict(zip(names, refs[n:]))
        if "w_in" in outs:
            outs["w_in"][Z_KR + QK_ROPE:Z_W, :] = jnp.zeros((Z_W - Z_KR - QK_ROPE, outs["w_in"].shape[1]), bf16)
        if "w_qb" in outs:
            for h in range(MLA_HEADS):
                outs["w_qb"][:, QK_PAD * h + QK_HEAD:QK_PAD * (h + 1)] = jnp.zeros((outs["w_qb"].shape[0], QK_PAD - QK_HEAD), bf16)
        for j in range(N_DEV):
            for wn, moves in _col_moves(j).items():
                if wn in outs:
                    for s0, s1, d0 in moves:
                        if wn == "w_in":
                            outs[wn][d0:d0 + s1 - s0, :] = ins[wn][j, s0:s1, :]
                        else:
                            outs[wn][:, d0:d0 + s1 - s0] = ins[wn][j, :, s0:s1]

    outs = pl.pallas_call(
        body, name=name, grid=(RELAYOUT_BLOCKS,), in_specs=_relayout_specs(names, True), out_specs=_relayout_specs(names, False),
        out_shape=[jax.ShapeDtypeStruct(_kernel_shape(wn), gathered[wn].dtype) for wn in names],
        compiler_params=_cp(("arbitrary",), VMEM_LIMIT),
    )(*[gathered[wn] for wn in names])
    return dict(zip(names, outs))


def _grads_out(sources, names, name):
    pieces = [(wn, start, arr) for wn in names for start, arr in sources[wn]]
    n_in = len(pieces)

    def body(*refs):
        outs = dict(zip(names, refs[n_in:]))

        def cols(wn, c0, c1):
            for (pn, start, arr), ref in zip(pieces, refs[:n_in]):
                if pn == wn and start <= c0 and c1 <= start + arr.shape[0 if wn == "w_in" else 1]:
                    return ref[c0 - start:c1 - start, :] if wn == "w_in" else ref[:, c0 - start:c1 - start]

        for j in range(N_DEV):
            for wn, moves in _col_moves(j).items():
                if wn in outs:
                    for s0, s1, d0 in moves:
                        if wn == "w_in":
                            outs[wn][j, s0:s1, :] = cols(wn, d0, d0 + s1 - s0).astype(bf16)
                        else:
                            outs[wn][j, :, s0:s1] = cols(wn, d0, d0 + s1 - s0).astype(bf16)

    def in_spec(wn, arr):
        if wn == "lb_param":
            return _acc_spec(arr.shape)
        if wn == "w_in":
            return pl.BlockSpec((arr.shape[0], arr.shape[1] // RELAYOUT_BLOCKS), lambda i: (0, i))
        return pl.BlockSpec((arr.shape[0] // RELAYOUT_BLOCKS, arr.shape[1]), lambda i: (i, 0))

    in_specs = [in_spec(wn, arr) for wn, _, arr in pieces]
    outs = pl.pallas_call(
        body, name=name, grid=(RELAYOUT_BLOCKS,), in_specs=in_specs, out_specs=_relayout_specs(names, True),
        out_shape=[jax.ShapeDtypeStruct((N_DEV, *BIG[wn]), bf16) for wn in names],
        compiler_params=_cp(("arbitrary",), VMEM_LIMIT),
    )(*[arr for _, _, arr in pieces])
    return dict(zip(names, outs))


def kernel(x, p, positions, g_mix, w_in, g_qa, g_kva, w_qb, w_kvb, g_qn, g_kn, lb_param, g_hgo, w_o, g_ffn, w_gate, w_up, w_down, g_ple, w_ple_gate, w_ple_proj, loss_target, m_g_mix, m_w_in, m_g_qa, m_g_kva, m_w_qb, m_w_kvb, m_g_qn, m_g_kn, m_lb_param, m_g_hgo, m_w_o, m_g_ffn, m_w_gate, m_w_up, m_w_down, m_g_ple, m_w_ple_gate, m_w_ple_proj, v_g_mix, v_w_in, v_g_qa, v_g_kva, v_w_qb, v_w_kvb, v_g_qn, v_g_kn, v_lb_param, v_g_hgo, v_w_o, v_g_ffn, v_w_gate, v_w_up, v_w_down, v_g_ple, v_w_ple_gate, v_w_ple_proj):
    w_all = dict(g_mix=g_mix, g_qa=g_qa, g_kva=g_kva, g_qn=g_qn, g_kn=g_kn, g_hgo=g_hgo, g_ffn=g_ffn, g_ple=g_ple,
                 w_in=w_in, w_qb=w_qb, w_kvb=w_kvb, w_o=w_o, w_gate=w_gate, w_up=w_up, w_down=w_down,
                 w_ple_gate=w_ple_gate, w_ple_proj=w_ple_proj, lb_param=lb_param)
    m_all = dict(g_mix=m_g_mix, g_qa=m_g_qa, g_kva=m_g_kva, g_qn=m_g_qn, g_kn=m_g_kn, g_hgo=m_g_hgo, g_ffn=m_g_ffn,
                 g_ple=m_g_ple, w_in=m_w_in, w_qb=m_w_qb, w_kvb=m_w_kvb, w_o=m_w_o, w_gate=m_w_gate, w_up=m_w_up,
                 w_down=m_w_down, w_ple_gate=m_w_ple_gate, w_ple_proj=m_w_ple_proj, lb_param=m_lb_param)
    v_all = dict(g_mix=v_g_mix, g_qa=v_g_qa, g_kva=v_g_kva, g_qn=v_g_qn, g_kn=v_g_kn, g_hgo=v_g_hgo, g_ffn=v_g_ffn,
                 g_ple=v_g_ple, w_in=v_w_in, w_qb=v_w_qb, w_kvb=v_w_kvb, w_o=v_w_o, w_gate=v_w_gate, w_up=v_w_up,
                 w_down=v_w_down, w_ple_gate=v_w_ple_gate, w_ple_proj=v_w_ple_proj, lb_param=v_lb_param)
    me_idx = jnp.stack([_me()]).astype(jnp.int32)
    x, p, positions, target = x[0], p[0, 0], positions[0], loss_target[0]
    s = x.shape[0]
    tm, tm_ffn, tq_f, tq_b = min(512, s), min(1024, s), min(2048, s), min(1024, s)
    g_mix, g_qa, g_kva, g_qn, g_kn, g_hgo, g_ffn, g_ple = (w_all[n].reshape(1, -1) for n in SMALL)
    g_qn_p, g_kn_p = _pad256(g_qn), _pad256(g_kn)
    cosb, sina, sinb = _rope_tables(positions)
    as_shard = lambda n, a: a[0].T if n in TRANSPOSED else a.reshape(BIG[n])
    shard = lambda n: as_shard(n, w_all[n])

    first = _all_gather([shard(n) for n in FIRST], [f32 if n == "lb_param" else bf16 for n in FIRST], "ag_first")
    lands = _cast_to_slot([shard(n) for n in SECOND], me_idx, first[0])
    ag2, token = _exchange_start([], lands, "ag_second_start")
    wk = _weights_in(dict(zip(FIRST, first)), FIRST, "weights_in_first")
    wz, wqb, wkvb, lb4 = (wk[n] for n in FIRST)

    h1, z = _fwd_in(x, g_mix, wz, tm)
    q, k, v = _fwd_mla_proj(z, cosb + token[0, 0], sina, sinb, g_qa, g_kva, wqb, wkvb, g_qn_p, g_kn_p, tm)
    o, gla_b, gla_states = _fwd_gla(z, lb4)
    a, a32 = _fwd_attn(q, k, v, tq_f, o)

    second = dict(zip(SECOND, _exchange_wait(ag2, [a, o], "ag_second_wait")[1]))
    w_pp = _weights_in(second, ("w_ple_proj",), "weights_in_second")["w_ple_proj"]
    w_o, w_down, w_pg, w_gate, w_up = (second[n].reshape(N_DEV * BIG[n][0], BIG[n][1]) for n in ROW_SHARDED + ("w_gate", "w_up"))

    x2, cat = _fwd_mix(a, o, z, g_hgo, x, w_o, tm)
    x3, gp, up = _fwd_ffn(x2, g_ffn, w_gate, w_up, w_down, tm)
    d3, h3, dpre, dpp, dg_ple, loss_tile = _ple_loss_fwd_bwd(x3, g_ple, w_pg, p, w_pp, target, tm)
    act, dgp, dup = _bwd_ffn_hidden(d3, gp, up, w_down, tm, D_FF // 2)
    d2, h2, dg_ffn = _bwd_ffn_in(d3, x2, dgp, dup, g_ffn, w_gate, w_up, tm)

    gw_gate, gw_up = _mm_tn_many(h2, [dgp, dup], "dw_gate_up", 512, transposed=True)
    blocks = _grads_out({"w_ple_proj": [(0, _mm_tn(p, dpp, "dw_ple_proj"))]}, ("w_ple_proj",), "grads_out_second")
    row_grads = {"w_o": _mm_tn(cat, d2, "dw_o"), "w_down": _mm_tn(act, d3, "dw_down"), "w_ple_gate": _mm_tn(h3, dpre, "dw_ple_gate"),
                 "w_gate": gw_gate, "w_up": gw_up}
    blocks.update({n: g.reshape(N_DEV, *BIG[n]) for n, g in row_grads.items()})
    empty = lambda names: [lax.empty((N_PEERS, *BIG[n]), bf16) for n in names]
    rs2, token = _exchange_start([blocks[n] for n in SECOND], empty(SECOND), "rs_second_start")

    da, do, dz_hg, dg_hgo = _bwd_mix(d2, w_o, o, z, g_hgo + token[0, 0], tm)
    dz_hq, dz_hff, dz_hfb, dz_hi, dlb4 = _bwd_gla(z, lb4, do, gla_b, gla_states)
    dq, dk, dv = _bwd_attn(q, k, v, da, a32, tq_b)
    dz_mla, gw_qb, gw_kvb, dg_qa, dg_kva, dg_qn, dg_kn = _bwd_mla_proj(
        z, dq, dk, dv, cosb, sina, sinb, g_qa, g_kva, wqb, wkvb, g_qn_p, g_kn_p, tm)

    gz = list(zip((Z_HQ, Z_HFF, Z_HFB, Z_HI, Z_HG, Z_CQ),
                  _mm_tn_many(h1, [dz_hq, dz_hff, dz_hfb, dz_hi, dz_hg, dz_mla], "dw_in", 1024, transposed=True)))
    blocks1 = _grads_out({"w_in": gz, "w_qb": [(0, gw_qb)], "w_kvb": [(0, gw_kvb)],
                          "lb_param": [(0, dlb4)]}, FIRST, "grads_out_first")
    rs1, token = _exchange_start([blocks1[n] for n in FIRST], empty(FIRST), "rs_first_start")

    result = {}

    def adam(names, lands, src, n_blocks, after=()):
        flipped = TRANSPOSED
        given = lambda arrs: [arrs[n][0].T if n in flipped else arrs[n] for n in names]
        outs = _adam_shards(me_idx, [src[n] for n in names], lands, given(w_all), given(m_all), given(v_all), n_blocks,
                            "adamw_" + names[0], after)
        for n, o in zip(names, outs):
            result[n] = [t.T[None] for t in o] if n in flipped else o
        return outs[0][0]

    blocks2, lands2 = (dict(zip(SECOND, arrs)) for arrs in _exchange_wait(rs2, [token], "rs_second_wait"))
    by2 = ("w_down", "w_gate", "w_up")
    by8 = tuple(n for n in SECOND if n not in by2)
    done = [adam(by8, [lands2[n] for n in by8], blocks2, 8), adam(by2, [lands2[n] for n in by2], blocks2, 2)]

    segments = [(dz_hq, 512, 0, Z_HQ // 512), (dz_hff, 512, 0, Z_HFF // 512), (dz_hfb, 512, 0, Z_HFB // 512),
                (dz_hi, 512, 0, Z_HI // 512), (dz_hg, 512, 0, Z_HG // 512), (dz_mla, 640, 0, Z_CQ // 640)]
    grad_x, dg_mix = _bwd_in(segments, wz, x, g_mix + token[0, 0], d2, tm)
    dgains = (dg_mix, dg_qa, dg_kva, dg_qn, dg_kn, dg_hgo, dg_ffn, dg_ple)

    vec = jnp.concatenate(list(dgains) + [loss_tile[0:1]], axis=1)
    parts = _all_gather([vec], [f32], "ag_gains")[0]
    outs, loss_row = _adam_gains(parts, [w_all[n] for n in SMALL], [m_all[n] for n in SMALL], [v_all[n] for n in SMALL])
    result.update(zip(SMALL, outs))

    blocks1, lands1 = _exchange_wait(rs1, [grad_x, loss_row, *done], "rs_first_wait")
    adam(FIRST, lands1, dict(zip(FIRST, blocks1)), 8)

    order = ("g_mix", "w_in", "g_qa", "g_kva", "w_qb", "w_kvb", "g_qn", "g_kn", "lb_param", "g_hgo", "w_o", "g_ffn",
             "w_gate", "w_up", "w_down", "g_ple", "w_ple_gate", "w_ple_proj")
    return (loss_row[0, 0], grad_x[None], *[result[n][k] for k in range(4) for n in order])
```

```python
import functools
import math

import jax
import jax.numpy as jnp
from jax import lax
from jax.experimental import pallas as pl
from jax.experimental.pallas import tpu as pltpu

f32 = jnp.float32
bf16 = jnp.bfloat16

N_DEV = 8
MLA_HEADS = 4
QK_NOPE = 128
QK_ROPE = 64
QK_HEAD = QK_NOPE + QK_ROPE
QK_PAD = 256
V_HEAD = 128
Q_LORA = 256
KV_LORA = 256
HG_HEADS = 4
CHUNK = 64
D_FF = 2816
PLE_DIM = 256
ROPE_THETA = 10000.0
EPS = 1e-6
ATTN_SCALE = QK_HEAD ** -0.5
LOG2_E = math.log2(math.e)
ATTN_SUB_ROWS = 256
Z_HQ, Z_HFF, Z_HFB, Z_HI, Z_HG, Z_CQ, Z_CKV, Z_KR, Z_W = 0, 512, 1024, 1536, 2048, 2560, 2816, 3072, 3200

ADAM_LR, ADAM_B1, ADAM_B2, ADAM_EPS, ADAM_WD, ADAM_STEP = 0.001, 0.9, 0.999, 1e-08, 0.01, 10

LANES = 128
BIG = {"w_in": (392, 1024), "w_qb": (256, 96), "w_kvb": (256, 128), "w_o": (128, 1024), "w_gate": (352, 1024),
       "w_up": (352, 1024), "w_down": (352, 1024), "w_ple_gate": (128, 1024), "w_ple_proj": (256, 128),
       "lb_param": (4, 64)}
TRANSPOSED = ("w_gate", "w_up", "w_in")
SMALL = {"g_mix": (0, 1024), "g_qa": (1024, 256), "g_kva": (1280, 256), "g_qn": (1536, 192), "g_kn": (1792, 192),
         "g_hgo": (2048, 512), "g_ffn": (2560, 1024), "g_ple": (3584, 1024)}
LOSS_OFF = 4608
GAIN_VEC = LOSS_OFF + LANES
Z_SEGMENTS = ((0, 256, Z_CQ), (256, 512, Z_CKV), (512, 576, Z_KR), (576, 1088, Z_HQ), (1088, 1600, Z_HFF),
              (1600, 2112, Z_HFB), (2112, 2624, Z_HI), (2624, 3136, Z_HG))

VMEM_LIMIT = 56 * 1024 * 1024
MESH = pl.DeviceIdType.MESH


def _cp(sem=None, vmem=None):
    return pltpu.CompilerParams(dimension_semantics=sem, vmem_limit_bytes=vmem)


def _const_spec(shape):
    nd = len(shape)
    return pl.BlockSpec(shape, lambda *_: (0,) * nd, pipeline_mode=pl.Buffered(1))


def _acc_spec(shape):
    nd = len(shape)
    return pl.BlockSpec(shape, lambda *_: (0,) * nd)


def _sigmoid(x):
    return jax.nn.sigmoid(x)


def _dot(a, b):
    return jnp.dot(a, b, preferred_element_type=f32)


def _dot_nt(a, b):
    return lax.dot_general(a, b, (((1,), (1,)), ((), ())), preferred_element_type=f32)


def _dot_tn(a, b):
    return lax.dot_general(a, b, (((0,), (0,)), ((), ())), preferred_element_type=f32)


def _rms_fwd(x, g, width):
    r = lax.rsqrt(jnp.sum(x * x, axis=-1, keepdims=True) * (1.0 / width) + EPS)
    return x * r * g, r


def _rms_bwd(dy, x, r, g, width):
    u = dy * g
    dx = r * u - x * (r * r * r) * (jnp.sum(u * x, axis=-1, keepdims=True) * (1.0 / width))
    return dx, dy * x * r


class _Both:
    def __init__(self, *copies):
        self.copies = copies

    def start(self):
        for cp in self.copies:
            cp.start()

    def wait(self):
        for cp in self.copies:
            cp.wait()


def _rope(b, c, sa, sb):
    return b * c + pltpu.roll(b, 32, 1) * sa + pltpu.roll(b, 96, 1) * sb


def _all_gather(shards, dtypes, name):
    n = len(shards)

    def body(*refs):
        in_refs, out_refs, stage = refs[:n], refs[n:2 * n], refs[2 * n:3 * n]
        send_sems, recv_sems, local_sems = refs[3 * n:]
        for w in range(n):
            stage[w][...] = in_refs[w][...].astype(stage[w].dtype)
        x, y, c = lax.axis_index("x"), lax.axis_index("y"), lax.axis_index("c")
        me, sibling = (x, y, c), (x, y, 1 - c)
        chips = [(1 - x, y), (x, 1 - y), (1 - x, 1 - y)]

        def slot(w, px, py, pc):
            return out_refs[w].at[4 * px + 2 * py + pc]

        def copy(w, k, block, to, src=None):
            return pltpu.make_async_remote_copy(
                src_ref=slot(w, *block) if src is None else src, dst_ref=slot(w, *block),
                send_sem=send_sems.at[w, k], recv_sem=recv_sems.at[w, k], device_id=to, device_id_type=MESH)

        first = []
        for j, chip in enumerate(chips):
            first += [copy(w, 1 + j, me, (*chip, c), src=stage[w]) for w in range(n)]
        first += [copy(w, 0, me, sibling, src=stage[w]) for w in range(n)]
        mine = [pltpu.make_async_copy(stage[w], slot(w, *me), local_sems.at[w]) for w in range(n)]
        for cp in first + mine:
            cp.start()
        passed = []
        for j, chip in enumerate(chips):
            for w in range(n):
                copy(w, 1 + j, (*chip, c), me).wait_recv()
                passed.append(copy(w, 4 + j, (*chip, c), sibling))
                passed[-1].start()
        for w in range(n):
            copy(w, 0, sibling, me).wait_recv()
        for j, chip in enumerate(chips):
            for w in range(n):
                copy(w, 4 + j, (*chip, 1 - c), me).wait_recv()
        for cp in first + passed:
            cp.wait_send()
        for cp in mine:
            cp.wait()

    return pl.pallas_call(
        body, name=name,
        out_shape=[jax.ShapeDtypeStruct((N_DEV, *s.shape), dt) for s, dt in zip(shards, dtypes)],
        in_specs=[pl.BlockSpec(memory_space=pltpu.VMEM)] * n,
        out_specs=[pl.BlockSpec(memory_space=pl.ANY)] * n,
        scratch_shapes=[pltpu.VMEM(s.shape, dt) for s, dt in zip(shards, dtypes)]
        + [pltpu.SemaphoreType.DMA((n, 7)), pltpu.SemaphoreType.DMA((n, 7)), pltpu.SemaphoreType.DMA((n,))],
        compiler_params=_cp(None, VMEM_LIMIT),
    )(*shards)


N_PEERS = N_DEV - 1
HBM_SPEC = pl.BlockSpec(memory_space=pltpu.HBM)
SEM_SPEC = pl.BlockSpec(memory_space=pltpu.SEMAPHORE)
DATAFLOW = pltpu.SideEffectType.DATAFLOW_SIDE_EFFECTING


def _me():
    return 4 * lax.axis_index("x") + 2 * lax.axis_index("y") + lax.axis_index("c")


def _peer(k):
    x, y, c = lax.axis_index("x"), lax.axis_index("y"), lax.axis_index("c")
    px = 1 - x if k & 4 else x
    py = 1 - y if k & 2 else y
    pc = 1 - c if k & 1 else c
    return (px, py, pc), 4 * px + 2 * py + pc


def _exchange_copies(src_refs, land_refs, send_sems, recv_sems, gather):
    cps = []
    me = _me()
    for k in range(1, N_DEV):
        peer, peer_idx = _peer(k)
        for w, land in enumerate(land_refs):
            src = land.at[me] if gather else src_refs[w].at[peer_idx]
            dst = land.at[me] if gather else land.at[k - 1]
            cps.append(pltpu.make_async_remote_copy(
                src_ref=src, dst_ref=dst, send_sem=send_sems.at[N_PEERS * w + k - 1], recv_sem=recv_sems.at[N_PEERS * w + k - 1],
                device_id=peer, device_id_type=MESH))
    return cps


def _exchange_start(srcs, lands, name):
    n_src, n = len(srcs), len(lands)

    def body(*refs):
        src_refs, land_refs = refs[:n_src], refs[n_src:n_src + n]
        send_sems, recv_sems = refs[n_src + n], refs[n_src + n + 1]
        token = refs[-1]
        for cp in _exchange_copies(src_refs, land_refs, send_sems, recv_sems, gather=not n_src):
            cp.start()
        token[...] = jnp.zeros_like(token)

    arrays = [pltpu.with_memory_space_constraint(a, pltpu.HBM) for a in (*srcs, *lands)]
    outs = pl.pallas_call(
        body, name=name,
        out_shape=(pltpu.SemaphoreType.DMA((n * N_PEERS,)), pltpu.SemaphoreType.DMA((n * N_PEERS,)),
                   *[pltpu.HBM(a.shape, a.dtype) for a in arrays], jax.ShapeDtypeStruct((8, LANES), f32)),
        in_specs=[HBM_SPEC] * len(arrays),
        out_specs=(SEM_SPEC, SEM_SPEC, *[HBM_SPEC] * len(arrays), pl.BlockSpec(memory_space=pltpu.VMEM)),
        input_output_aliases={i: 2 + i for i in range(len(arrays))},
        compiler_params=pltpu.CompilerParams(has_side_effects=DATAFLOW),
    )(*arrays)
    return (outs[0], outs[1], outs[2:2 + n_src], outs[2 + n_src:2 + n_src + n]), outs[-1]


def _exchange_wait(state, after, name):
    send_sems, recv_sems, srcs, lands = state
    n_src, n = len(srcs), len(lands)

    def body(*refs):
        src_refs, land_refs = refs[:n_src], refs[n_src:n_src + n]
        send_ref, recv_ref = refs[n_src + n], refs[n_src + n + 1]
        for cp in _exchange_copies(src_refs, land_refs, send_ref, recv_ref, gather=not n_src):
            cp.wait_send()
            cp.wait_recv()

    arrays = (*srcs, *lands)
    outs = pl.pallas_call(
        body, name=name,
        out_shape=tuple(pltpu.HBM(a.shape, a.dtype) for a in arrays),
        in_specs=[HBM_SPEC] * len(arrays) + [SEM_SPEC, SEM_SPEC] + [pl.BlockSpec(memory_space=pl.ANY)] * len(after),
        out_specs=tuple([HBM_SPEC] * len(arrays)),
        input_output_aliases={i: i for i in range(len(arrays))},
        compiler_params=pltpu.CompilerParams(has_side_effects=DATAFLOW),
    )(*arrays, send_sems, recv_sems, *after)
    return outs[:n_src], outs[n_src:]


def _cast_to_slot(shards, me_idx, after):
    n = len(shards)

    def body(i_ref, *refs):
        for w in range(n):
            refs[n + 1 + w][...] = refs[w][...].astype(bf16)

    return pl.pallas_call(
        body, name="cast_to_slot",
        grid_spec=pltpu.PrefetchScalarGridSpec(
            num_scalar_prefetch=1, grid=(1,),
            in_specs=[pl.BlockSpec(s.shape, lambda i, m: (0, 0)) for s in shards] + [pl.BlockSpec(memory_space=pl.ANY)],
            out_specs=[pl.BlockSpec((None, *s.shape), lambda i, m: (m[0], 0, 0)) for s in shards]),
        out_shape=[jax.ShapeDtypeStruct((N_DEV, *s.shape), bf16) for s in shards],
        compiler_params=_cp(("arbitrary",), VMEM_LIMIT),
    )(me_idx, *shards, after)


def _row_block(rows, n_blocks):
    return (rows // n_blocks, True) if rows % (16 * n_blocks) == 0 else (rows, False)


def _adam_math(w, g, m, v):
    m = ADAM_B1 * m + (1.0 - ADAM_B1) * g
    v = ADAM_B2 * v + (1.0 - ADAM_B2) * (g * g)
    m_hat = m / (1.0 - ADAM_B1 ** ADAM_STEP)
    v_hat = v / (1.0 - ADAM_B2 ** ADAM_STEP)
    delta = -ADAM_LR * (m_hat / (jnp.sqrt(v_hat) + ADAM_EPS) + ADAM_WD * w)
    return delta, m, v


def _adam_shards(me_idx, blocks, lands, ws, ms, vs, n_blocks, name, after=()):
    n = len(blocks)

    def body(i_ref, *refs):
        ins, outs = refs[:5 * n], refs[5 * n + len(after):]
        for w in range(n):
            g_ref, b_ref, w_ref, m_ref, v_ref = (ins[t * n + w] for t in range(5))
            g = g_ref[...].astype(f32)
            for k in range(N_PEERS):
                g = g + b_ref[k].astype(f32)
            if len(w_ref.shape) == 2:
                pieces = [(slice(None), g)]
            else:
                pieces = [(a, g[2 * a:2 * a + 2]) for a in range(2)]
            for at, gp in pieces:
                vals = (gp,) + _adam_math(w_ref[at], gp, m_ref[at], v_ref[at])
                for t, val in enumerate(vals):
                    outs[4 * w + t][at] = val

    specs = [[] for _ in range(5)]
    out_specs, out_shape = [], []
    for g, wt in zip(blocks, ws):
        rows, cols = g.shape[1:]
        rb, cut = _row_block(rows, n_blocks)
        if not cut and wt.ndim == 2 and cols % (LANES * n_blocks) == 0:
            cb = cols // n_blocks
            specs[0].append(pl.BlockSpec((None, rows, cb), lambda i, s: (s[0], 0, i)))
            specs[1].append(pl.BlockSpec((N_PEERS, rows, cb), lambda i, s: (0, 0, i)))
            shard = pl.BlockSpec((rows, cb), lambda i, s: (0, i))
            for t in (2, 3, 4):
                specs[t].append(shard)
            out_specs += [shard] * 4
            out_shape += [jax.ShapeDtypeStruct(wt.shape, f32)] * 4
            continue
        specs[0].append(pl.BlockSpec((None, rb, cols), functools.partial(lambda i, s, cut: (s[0], i if cut else 0, 0), cut=cut)))
        specs[1].append(pl.BlockSpec((N_PEERS, rb, cols), functools.partial(lambda i, s, cut: (0, i if cut else 0, 0), cut=cut)))
        if wt.ndim == 2:
            shard = pl.BlockSpec((rb, cols), functools.partial(lambda i, s, cut: (i if cut else 0, 0), cut=cut))
        elif wt.shape[0] == 1:
            shard = pl.BlockSpec((None, rb, cols), functools.partial(lambda i, s, cut: (0, i if cut else 0, 0), cut=cut))
        else:
            shard = pl.BlockSpec(wt.shape, functools.partial(lambda i, s, nd: (0,) * nd, nd=wt.ndim))
        for t in (2, 3, 4):
            specs[t].append(shard)
        out_specs += [shard] * 4
        out_shape += [jax.ShapeDtypeStruct(wt.shape, f32)] * 4
    outs = pl.pallas_call(
        body, name=name,
        grid_spec=pltpu.PrefetchScalarGridSpec(
            num_scalar_prefetch=1, grid=(n_blocks,), in_specs=sum(specs, []) + [pl.BlockSpec(memory_space=pl.ANY)] * len(after),
            out_specs=out_specs),
        out_shape=out_shape,
        compiler_params=_cp(("arbitrary",), VMEM_LIMIT),
    )(me_idx, *blocks, *lands, *ws, *ms, *vs, *after)
    return [outs[4 * w:4 * w + 4] for w in range(n)]


def _adam_gains(parts, ws, ms, vs):
    n = len(ws)

    def body(p_ref, *refs):
        ins, outs = refs[:3 * n], refs[3 * n:]
        g_all = p_ref[0]
        for k in range(1, N_DEV):
            g_all = g_all + p_ref[k]
        for w, (off, lanes) in enumerate(SMALL.values()):
            w_ref, m_ref, v_ref = ins[w], ins[n + w], ins[2 * n + w]
            if len(w_ref.shape) == 2:
                pieces = [(slice(None), off, lanes)]
            else:
                pieces = [((slice(None), h), off + LANES * h, LANES) for h in range(w_ref.shape[1])]
            for at, o, ln in pieces:
                g = g_all[:, o:o + ln]
                vals = (g,) + _adam_math(w_ref[at], g, m_ref[at], v_ref[at])
                for t, val in enumerate(vals):
                    outs[4 * w + t][at] = val
        outs[4 * n][...] = g_all[:, LOSS_OFF:LOSS_OFF + LANES]

    out_shape = sum([[jax.ShapeDtypeStruct(w.shape, f32)] * 4 for w in ws], []) + [jax.ShapeDtypeStruct((1, LANES), f32)]
    outs = pl.pallas_call(body, name="adamw_gains", out_shape=out_shape)(parts, *ws, *ms, *vs)
    return [outs[4 * w:4 * w + 4] for w in range(n)], outs[4 * n]


def _fwd_in(x, g_mix, wz, tm):
    s, d = x.shape

    def body(x_ref, g_ref, w_ref, h_ref, z_ref):
        h, _ = _rms_fwd(x_ref[...], g_ref[...], d)
        hb = h.astype(bf16)
        h_ref[...] = hb
        z_ref[...] = _dot_nt(hb, w_ref[...])

    return pl.pallas_call(
        body, name="fwd_in", grid=(s // tm,),
        in_specs=[pl.BlockSpec((tm, d), lambda i: (i, 0)), _const_spec((1, d)), _const_spec((Z_W, d))],
        out_specs=[pl.BlockSpec((tm, d), lambda i: (i, 0)), pl.BlockSpec((tm, Z_W), lambda i: (i, 0))],
        out_shape=[jax.ShapeDtypeStruct((s, d), bf16), jax.ShapeDtypeStruct((s, Z_W), f32)],
        compiler_params=_cp(("parallel",), VMEM_LIMIT),
    )(x, g_mix, wz)


def _mla_qk_fwd(cq, ckv, g_qa, g_kva, wqb, wkvb):
    cqn, rq = _rms_fwd(cq, g_qa, Q_LORA)
    ckvn, rkv = _rms_fwd(ckv, g_kva, KV_LORA)
    cqn_b, ckvn_b = cqn.astype(bf16), ckvn.astype(bf16)
    q0 = _dot(cqn_b, wqb)
    kv0 = _dot(ckvn_b, wkvb)
    return cqn_b, rq, ckvn_b, rkv, q0, kv0


def _fwd_mla_proj(z, cosb, sina, sinb, g_qa, g_kva, wqb, wkvb, g_qn, g_kn, tm):
    s = z.shape[0]
    hh = MLA_HEADS

    def body(cq_ref, ckv_ref, kr_ref, c_ref, sa_ref, sb_ref, gqa_ref, gkva_ref, wqb_ref, wkvb_ref, gqn_ref, gkn_ref,
             q_ref, k_ref, v_ref):
        _, _, _, _, q0, kv0 = _mla_qk_fwd(cq_ref[...], ckv_ref[...], gqa_ref[...], gkva_ref[...], wqb_ref[...], wkvb_ref[...])
        kr = kr_ref[...]
        c, sa, sb = c_ref[...], sa_ref[...], sb_ref[...]
        gqn, gkn = gqn_ref[...], gkn_ref[...]
        kr_sq = jnp.sum(kr * kr, axis=-1, keepdims=True)
        for h in range(hh):
            qh = q0[:, QK_PAD * h:QK_PAD * (h + 1)]
            qn, _ = _rms_fwd(qh, gqn, QK_HEAD)
            q_ref[h, :, 0:128] = qn[:, 0:128].astype(bf16)
            q_ref[h, :, 128:256] = _rope(qn[:, 128:256], c, sa, sb).astype(bf16)
            kn_ = kv0[:, 256 * h:256 * h + 128]
            rk = lax.rsqrt((jnp.sum(kn_ * kn_, axis=-1, keepdims=True) + kr_sq) * (1.0 / QK_HEAD) + EPS)
            k_ref[h, :, 0:128] = (kn_ * rk * gkn[:, 0:128]).astype(bf16)
            k_ref[h, :, 128:256] = _rope(kr * rk * gkn[:, 128:256], c, sa, sb).astype(bf16)
            v_ref[h] = kv0[:, 256 * h + 128:256 * h + 256].astype(bf16)

    row128 = pl.BlockSpec((tm, 128), lambda i: (i, 0))
    return pl.pallas_call(
        body, name="fwd_mla_proj", grid=(s // tm,),
        in_specs=[pl.BlockSpec((tm, 256), lambda i: (i, Z_CQ // 256)), pl.BlockSpec((tm, 256), lambda i: (i, Z_CKV // 256)),
                  pl.BlockSpec((tm, 128), lambda i: (i, Z_KR // 128)), row128, row128, row128,
                  _const_spec((1, 256)), _const_spec((1, 256)), _const_spec((256, 1024)), _const_spec((256, 1024)),
                  _const_spec((1, 256)), _const_spec((1, 256))],
        out_specs=[pl.BlockSpec((hh, tm, QK_PAD), lambda i: (0, i, 0)), pl.BlockSpec((hh, tm, QK_PAD), lambda i: (0, i, 0)),
                   pl.BlockSpec((hh, tm, V_HEAD), lambda i: (0, i, 0))],
        out_shape=[jax.ShapeDtypeStruct((hh, s, QK_PAD), bf16), jax.ShapeDtypeStruct((hh, s, QK_PAD), bf16),
                   jax.ShapeDtypeStruct((hh, s, V_HEAD), bf16)],
        compiler_params=_cp(("parallel",), VMEM_LIMIT),
    )(z, z, z, cosb, sina, sinb, g_qa, g_kva, wqb, wkvb, g_qn, g_kn)


def _fwd_attn(q, k, v, tq, after):
    hh, s, _ = q.shape

    n_sub = max(1, tq // ATTN_SUB_ROWS)

    def body(q_ref, k_ref, v_ref, after_ref, o_ref, o32_ref):
        for t in range(n_sub):
            rows = slice(t * (tq // n_sub), (t + 1) * (tq // n_sub))
            sc = _dot_nt(q_ref[rows, :], k_ref[...])
            p = jnp.exp2((sc - jnp.max(sc, axis=-1, keepdims=True)) * (ATTN_SCALE * LOG2_E))
            l = jnp.sum(p, axis=-1, keepdims=True)
            o = _dot(p.astype(bf16), v_ref[...]) * (1.0 / l)
            o_ref[rows, :] = o.astype(bf16)
            o32_ref[rows, :] = o

    out = pl.BlockSpec((tq, V_HEAD), lambda h, i: (i, h))
    return pl.pallas_call(
        body, name="fwd_attn", grid=(hh, s // tq),
        in_specs=[pl.BlockSpec((None, tq, QK_PAD), lambda h, i: (h, i, 0)),
                  pl.BlockSpec((None, s, QK_PAD), lambda h, i: (h, 0, 0)),
                  pl.BlockSpec((None, s, V_HEAD), lambda h, i: (h, 0, 0)), pl.BlockSpec(memory_space=pl.ANY)],
        out_specs=[out, out],
        out_shape=[jax.ShapeDtypeStruct((s, hh * V_HEAD), bf16), jax.ShapeDtypeStruct((s, hh * V_HEAD), f32)],
        compiler_params=_cp(("parallel", "parallel"), VMEM_LIMIT),
    )(q, k, v, after)


def _split3(x):
    hi = x.astype(bf16)
    r1 = x - hi.astype(f32)
    mid = r1.astype(bf16)
    lo = (r1 - mid.astype(f32)).astype(bf16)
    return jnp.concatenate([hi, mid, lo], axis=-1)


def _tri_sum(tri, x):
    y = _dot(tri, _split3(x))
    return y[:, 0:128] + y[:, 128:256] + y[:, 256:384]


GLA_GROUP = 4
GLA_ROWS = GLA_GROUP * CHUNK
GLA_HEADS_PER_STEP = 2


def _gla_masks(rev):
    row = lax.broadcasted_iota(jnp.int32, (GLA_ROWS, GLA_ROWS), 0)
    col = lax.broadcasted_iota(jnp.int32, (GLA_ROWS, GLA_ROWS), 1)
    shift = CHUNK.bit_length() - 1
    same = (jnp.right_shift(row, shift) == jnp.right_shift(col, shift)).astype(f32)
    lower, upper = (row >= col).astype(f32) * same, (row <= col).astype(f32) * same
    keep, keep_t = (upper, lower) if rev else (lower, upper)
    chunk_of = jnp.right_shift(lax.broadcasted_iota(jnp.int32, (GLA_ROWS, 1), 0), shift)
    return keep, keep.astype(bf16), keep_t.astype(bf16), [(chunk_of == c).astype(f32) for c in range(GLA_GROUP)]


def _gla_gates(hq, hf, lower):
    sg = _sigmoid(hf)
    f = lower + (1.0 - lower) * sg
    return hq * _sigmoid(hq), 1.0 - f, jnp.log(f), f, sg


def _gla_last_mid(b, rev):
    b3 = b.reshape(GLA_GROUP, CHUNK, 128)
    last, mid = (0, CHUNK // 2) if rev else (CHUNK - 1, CHUNK // 2 - 1)
    return b3[:, last:last + 1, :], b3[:, mid:mid + 1, :]


def _gla_per_row(per_chunk):
    return jnp.broadcast_to(per_chunk, (GLA_GROUP, CHUNK, 128)).reshape(GLA_ROWS, 128)


def _gla_block_diag(x, row_masks):
    return jnp.concatenate([(x * m).astype(bf16) for m in row_masks], axis=-1)


def _gla_diag(y):
    return jnp.concatenate([y[CHUNK * c:CHUNK * (c + 1), 128 * c:128 * (c + 1)] for c in range(GLA_GROUP)], axis=0)


def _gla_rows(n, n_groups, rev):
    ne = n_groups - 1 - n if rev else n
    return pl.ds(pl.multiple_of(ne * GLA_ROWS, GLA_ROWS), GLA_ROWS), ne * GLA_GROUP


def _gla_scan_order(rev):
    return tuple(reversed(range(GLA_GROUP))) if rev else tuple(range(GLA_GROUP))


def _fwd_gla(z, lb4):
    s = z.shape[0]
    n_groups = s // GLA_ROWS
    assert n_groups % 2 == 0
    hp = GLA_HEADS_PER_STEP
    chains = [(hh, rev) for hh in range(hp) for rev in (False, True)]

    def body(hq_ref, hff_ref, hfb_ref, hi_ref, lb_ref, o_ref, b_ref, states_ref, st_ref, stage_ref, b_stage, sems):
        st_ref[...] = jnp.zeros_like(st_ref)
        masks = {rev: _gla_masks(rev) for rev in (False, True)}
        lowers = [_sigmoid(lb_ref[int(rev):int(rev) + 1, 128 * hh:128 * (hh + 1)]
                           - lb_ref[2 + int(rev):3 + int(rev), 128 * hh:128 * (hh + 1)]) for hh, rev in chains]

        def states_out(slot, ci, chunk0):
            hh, rev = chains[ci]
            head = pl.program_id(0) * hp + hh
            rows = pl.ds(pl.multiple_of(chunk0 * CHUNK, GLA_ROWS), GLA_ROWS)
            return _Both(
                pltpu.make_async_copy(stage_ref.at[slot, ci], states_ref.at[head, int(rev), pl.ds(chunk0, GLA_GROUP)],
                                      sems.at[slot, ci]),
                pltpu.make_async_copy(b_stage.at[slot, ci], b_ref.at[int(rev), rows, pl.ds(pl.multiple_of(head * 128, 128), 128)],
                                      sems.at[slot, len(chains) + ci]))

        def make_step(first):
            def step(n, carry):
                slot = n % 2

                @pl.when(n >= 2)
                def _():
                    for ci in range(len(chains)):
                        states_out(slot, ci, 0).wait()

                for ci, (hh, rev) in enumerate(chains):
                    cols = slice(128 * hh, 128 * (hh + 1))
                    rows, chunk0 = _gla_rows(n, n_groups, rev)
                    maskf, tri, _, row_masks = masks[rev]
                    hf_ref = hfb_ref if rev else hff_ref
                    q, k, logf, _, _ = _gla_gates(hq_ref[rows, cols], hf_ref[rows, cols], lowers[ci])
                    vb = hi_ref[rows, cols].astype(bf16)
                    b = _tri_sum(tri, logf)
                    b_stage[slot, ci] = b
                    b_last3, b_mid3 = _gla_last_mid(b, rev)
                    b_last, b_mid = _gla_per_row(b_last3), _gla_per_row(b_mid3)
                    qi = (q * jnp.exp(b - b_mid)).astype(bf16)
                    ki = (k * jnp.exp(b_mid - b)).astype(bf16)
                    a = (_dot_nt(qi, ki) * maskf).astype(bf16)
                    kv = _dot_tn(vb, _gla_block_diag(k * jnp.exp(b_last - b), row_masks))
                    decay3 = jnp.exp(b_last3)
                    st = st_ref[ci]
                    before = [None] * GLA_GROUP
                    for c in _gla_scan_order(rev):
                        stage_ref[slot, ci, c] = st
                        before[c] = st.astype(bf16)
                        st = st * decay3[c] + kv[:, 128 * c:128 * (c + 1)]
                    st_ref[ci] = st
                    states_out(slot, ci, chunk0).start()
                    inter = _dot_nt((q * jnp.exp(b)).astype(bf16), jnp.concatenate(before, axis=0))
                    o = _dot(a, vb) + _gla_diag(inter)
                    if first:
                        o_ref[rows, cols] = o
                    else:
                        o_ref[rows, cols] += o
                return carry
            return step

        lax.fori_loop(0, n_groups // 2, make_step(True), 0)
        lax.fori_loop(n_groups // 2, n_groups, make_step(False), 0)
        for slot in range(2):
            for ci in range(len(chains)):
                states_out(slot, ci, 0).wait()

    w = 128 * hp
    col = lambda base: pl.BlockSpec((s, w), lambda h: (0, base // w + h))
    return pl.pallas_call(
        body, name="fwd_gla", grid=(HG_HEADS // hp,),
        in_specs=[col(Z_HQ), col(Z_HFF), col(Z_HFB), col(Z_HI), pl.BlockSpec((4, w), lambda h: (0, h))],
        out_specs=[pl.BlockSpec((s, w), lambda h: (0, h)), pl.BlockSpec(memory_space=pl.ANY), pl.BlockSpec(memory_space=pl.ANY)],
        out_shape=[jax.ShapeDtypeStruct((s, HG_HEADS * 128), f32), jax.ShapeDtypeStruct((2, s, HG_HEADS * 128), f32),
                   jax.ShapeDtypeStruct((HG_HEADS, 2, s // CHUNK, 128, 128), f32)],
        scratch_shapes=[pltpu.VMEM((len(chains), 128, 128), f32), pltpu.VMEM((2, len(chains), GLA_GROUP, 128, 128), f32),
                        pltpu.VMEM((2, len(chains), GLA_ROWS, 128), f32), pltpu.SemaphoreType.DMA((2, 2 * len(chains)))],
        compiler_params=_cp(("parallel",), VMEM_LIMIT),
    )(z, z, z, z, lb4)


def _hg_out(o, hg, g_hgo):
    outs, ons, rs = [], [], []
    for h in range(HG_HEADS):
        oh = o[:, 128 * h:128 * (h + 1)]
        on, r = _rms_fwd(oh, g_hgo[:, 128 * h:128 * (h + 1)], 128)
        ons.append(on)
        rs.append(r)
    on = jnp.concatenate(ons, axis=-1)
    sg = _sigmoid(hg)
    return on * (hg * sg), on, rs, sg


def _fwd_mix(a, o, z, g_hgo, x, w_o, tm):
    s, d = x.shape

    def body(a_ref, o_ref, hg_ref, g_ref, x_ref, w_ref, x2_ref, cat_ref):
        r, _, _, _ = _hg_out(o_ref[...], hg_ref[...], g_ref[...])
        cat = jnp.concatenate([a_ref[...], r.astype(bf16)], axis=-1)
        cat_ref[...] = cat
        x2_ref[...] = x_ref[...] + _dot(cat, w_ref[...])

    row512 = pl.BlockSpec((tm, 512), lambda i: (i, 0))
    rowd = pl.BlockSpec((tm, d), lambda i: (i, 0))
    return pl.pallas_call(
        body, name="fwd_mix", grid=(s // tm,),
        in_specs=[row512, row512, pl.BlockSpec((tm, 512), lambda i: (i, Z_HG // 512)), _const_spec((1, 512)), rowd,
                  _const_spec((d, d))],
        out_specs=[rowd, rowd],
        out_shape=[jax.ShapeDtypeStruct((s, d), f32), jax.ShapeDtypeStruct((s, d), bf16)],
        compiler_params=_cp(("parallel",), VMEM_LIMIT),
    )(a, o, z, g_hgo, x, w_o)


def _fwd_ffn(x2, g_ffn, w_gate, w_up, w_down, tm):
    s, d = x2.shape

    def body(x_ref, g_ref, wg_ref, wu_ref, wd_ref, x3_ref, gp_ref, up_ref):
        x = x_ref[...]
        h, _ = _rms_fwd(x, g_ref[...], d)
        hb = h.astype(bf16)
        gp = _dot_nt(hb, wg_ref[...])
        up = _dot_nt(hb, wu_ref[...])
        gp_ref[...] = gp.astype(bf16)
        up_ref[...] = up.astype(bf16)
        act = (gp * _sigmoid(gp) * up).astype(bf16)
        x3_ref[...] = x + _dot(act, wd_ref[...])

    rowd = pl.BlockSpec((tm, d), lambda i: (i, 0))
    rowf = pl.BlockSpec((tm, D_FF), lambda i: (i, 0))
    return pl.pallas_call(
        body, name="fwd_ffn", grid=(s // tm,),
        in_specs=[rowd, _const_spec((1, d)), _const_spec((D_FF, d)), _const_spec((D_FF, d)), _const_spec((D_FF, d))],
        out_specs=[rowd, rowf, rowf],
        out_shape=[jax.ShapeDtypeStruct((s, d), f32), jax.ShapeDtypeStruct((s, D_FF), bf16),
                   jax.ShapeDtypeStruct((s, D_FF), bf16)],
        compiler_params=_cp(("parallel",), VMEM_LIMIT),
    )(x2, g_ffn, w_gate, w_up, w_down)


def _ple_loss_fwd_bwd(x3, g_ple, w_pg, p, w_pp, target, tm):
    s, d = x3.shape

    def body(x_ref, g_ref, wg_ref, p_ref, wp_ref, t_ref, dx_ref, gwg_ref, gwp_ref, dg_ref, loss_ref, acc_ref):
        @pl.when(pl.program_id(0) == 0)
        def _():
            for r_ in (acc_ref, gwp_ref, dg_ref, loss_ref):
                r_[...] = jnp.zeros_like(r_)

        x = x_ref[...]
        g = g_ref[...]
        h, r = _rms_fwd(x, g, d)
        hb = h.astype(bf16)
        pb = p_ref[...].astype(bf16)
        gate = _sigmoid(_dot(hb, wg_ref[...]))
        pp = _dot(pb, wp_ref[...])
        e = x + gate * pp - t_ref[...]
        loss_ref[...] += 0.5 * jnp.sum(e * e) * (1.0 / d)
        dy = e * (1.0 / d)
        dpre = (dy * pp * gate * (1.0 - gate)).astype(bf16)
        dx, dgx = _rms_bwd(_dot_nt(dpre, wg_ref[...]), x, r, g, d)
        dx_ref[...] = dy + dx
        dg_ref[...] += jnp.sum(dgx, axis=0, keepdims=True)
        acc_ref[...] += _dot_tn(hb, dpre)
        gwp_ref[...] += _dot_tn(pb, (dy * gate).astype(bf16))

        @pl.when(pl.program_id(0) == s // tm - 1)
        def _():
            gwg_ref[...] = acc_ref[...].astype(bf16)

    rowd = pl.BlockSpec((tm, d), lambda i: (i, 0))
    return pl.pallas_call(
        body, name="ple_loss_fwd_bwd", grid=(s // tm,),
        in_specs=[rowd, _const_spec((1, d)), _const_spec((d, d)), pl.BlockSpec((tm, PLE_DIM), lambda i: (i, 0)),
                  _const_spec((PLE_DIM, d)), rowd],
        out_specs=[rowd, _acc_spec((d, d)), _acc_spec((PLE_DIM, d)), _acc_spec((1, d)), _acc_spec((8, 128))],
        out_shape=[jax.ShapeDtypeStruct((s, d), f32), jax.ShapeDtypeStruct((d, d), bf16), jax.ShapeDtypeStruct((PLE_DIM, d), f32),
                   jax.ShapeDtypeStruct((1, d), f32), jax.ShapeDtypeStruct((8, 128), f32)],
        scratch_shapes=[pltpu.VMEM((d, d), f32)],
        compiler_params=_cp(("arbitrary",), VMEM_LIMIT),
    )(x3, g_ple, w_pg, p, w_pp, target)


def _bwd_ffn_hidden(d3, gp, up, w_down, tm, tf):
    s, d = d3.shape

    def body(d3_ref, gp_ref, up_ref, wd_ref, act_ref, dgp_ref, dup_ref):
        gp, up = gp_ref[...].astype(f32), up_ref[...].astype(f32)
        sg = _sigmoid(gp)
        silu = gp * sg
        act_ref[...] = (silu * up).astype(bf16)
        dact = _dot_nt(d3_ref[...].astype(bf16), wd_ref[...])
        dgp_ref[...] = (dact * up * (sg * (1.0 + gp * (1.0 - sg)))).astype(bf16)
        dup_ref[...] = (dact * silu).astype(bf16)

    rowf = pl.BlockSpec((tm, tf), lambda f, i: (i, f))
    return pl.pallas_call(
        body, name="bwd_ffn_hidden", grid=(D_FF // tf, s // tm),
        in_specs=[pl.BlockSpec((tm, d), lambda f, i: (i, 0)), rowf, rowf, pl.BlockSpec((tf, d), lambda f, i: (f, 0))],
        out_specs=[rowf, rowf, rowf],
        out_shape=[jax.ShapeDtypeStruct((s, D_FF), bf16)] * 3,
        compiler_params=_cp(("parallel", "parallel"), VMEM_LIMIT),
    )(d3, gp, up, w_down)


def _bwd_ffn_in(d3, x2, dgp, dup, g_ffn, w_gate, w_up, tm):
    s, d = x2.shape

    def body(d3_ref, x_ref, dgp_ref, dup_ref, g_ref, wg_ref, wu_ref, d2_ref, h_ref, dg_ref):
        @pl.when(pl.program_id(0) == 0)
        def _():
            dg_ref[...] = jnp.zeros_like(dg_ref)

        x, g = x_ref[...], g_ref[...]
        dh = _dot(dgp_ref[...], wg_ref[...]) + _dot(dup_ref[...], wu_ref[...])
        h, r = _rms_fwd(x, g, d)
        h_ref[...] = h.astype(bf16)
        dx, dgx = _rms_bwd(dh, x, r, g, d)
        d2_ref[...] = d3_ref[...] + dx
        dg_ref[...] += jnp.sum(dgx, axis=0, keepdims=True)

    rowd = pl.BlockSpec((tm, d), lambda i: (i, 0))
    rowf = pl.BlockSpec((tm, D_FF), lambda i: (i, 0))
    return pl.pallas_call(
        body, name="bwd_ffn_in", grid=(s // tm,),
        in_specs=[rowd, rowd, rowf, rowf, _const_spec((1, d)), _const_spec((D_FF, d)), _const_spec((D_FF, d))],
        out_specs=[rowd, rowd, _acc_spec((1, d))],
        out_shape=[jax.ShapeDtypeStruct((s, d), f32), jax.ShapeDtypeStruct((s, d), bf16), jax.ShapeDtypeStruct((1, d), f32)],
        compiler_params=_cp(("arbitrary",), VMEM_LIMIT),
    )(d3, x2, dgp, dup, g_ffn, w_gate, w_up)


def _bwd_mix(d2, w_o, o, z, g_hgo, cat, tm):
    s, d = d2.shape

    def body(d2_ref, w_ref, o_ref, hg_ref, g_ref, cat_ref, da_ref, do_ref, dhg_ref, dg_ref, gw_ref, acc_ref):
        @pl.when(pl.program_id(0) == 0)
        def _():
            dg_ref[...] = jnp.zeros_like(dg_ref)
            acc_ref[...] = jnp.zeros_like(acc_ref)

        d2b = d2_ref[...].astype(bf16)
        acc_ref[...] += _dot_tn(cat_ref[...], d2b)

        @pl.when(pl.program_id(0) == s // tm - 1)
        def _():
            gw_ref[...] = acc_ref[...].astype(bf16)
        dcat = _dot_nt(d2b, w_ref[...])
        da_ref[...] = dcat[:, 0:512].astype(bf16)
        dr = dcat[:, 512:1024]
        o, hg, g = o_ref[...], hg_ref[...], g_ref[...]
        _, on, rs, sg = _hg_out(o, hg, g)
        dhg_ref[...] = (dr * on * (sg * (1.0 + hg * (1.0 - sg)))).astype(bf16)
        don = dr * (hg * sg)
        dgs = []
        for h in range(HG_HEADS):
            cols = slice(128 * h, 128 * (h + 1))
            dx, dgx = _rms_bwd(don[:, cols], o[:, cols], rs[h], g[:, cols], 128)
            do_ref[:, cols] = dx
            dgs.append(jnp.sum(dgx, axis=0, keepdims=True))
        dg_ref[...] += jnp.concatenate(dgs, axis=-1)

    row512 = pl.BlockSpec((tm, 512), lambda i: (i, 0))
    return pl.pallas_call(
        body, name="bwd_mix", grid=(s // tm,),
        in_specs=[pl.BlockSpec((tm, d), lambda i: (i, 0)), _const_spec((d, d)), row512,
                  pl.BlockSpec((tm, 512), lambda i: (i, Z_HG // 512)), _const_spec((1, 512)), pl.BlockSpec((tm, d), lambda i: (i, 0))],
        out_specs=[row512, row512, row512, _acc_spec((1, 512)), _acc_spec((d, d))],
        out_shape=[jax.ShapeDtypeStruct((s, 512), bf16), jax.ShapeDtypeStruct((s, 512), f32), jax.ShapeDtypeStruct((s, 512), bf16),
                   jax.ShapeDtypeStruct((1, 512), f32), jax.ShapeDtypeStruct((d, d), bf16)],
        scratch_shapes=[pltpu.VMEM((d, d), f32)],
        compiler_params=_cp(("arbitrary",), VMEM_LIMIT),
    )(d2, w_o, o, z, g_hgo, cat)


def _bwd_gla(z, lb4, do, b_fwd, states):
    s = z.shape[0]
    n_chunks = s // CHUNK
    n_groups = s // GLA_ROWS
    assert n_groups % 2 == 0

    def body(hq_ref, hff_ref, hfb_ref, hi_ref, lb_ref, do_ref, b_all, st_all, dhq_ref, dhff_ref, dhfb_ref, dhi_ref, dlb_ref,
             dst_ref, dq_acc, dv_acc, dlow_ref):
        dirs = (False, True)
        masks = [_gla_masks(rev) for rev in dirs]
        lowers = [_sigmoid(lb_ref[int(rev):int(rev) + 1, :] - lb_ref[2 + int(rev):3 + int(rev), :]) for rev in dirs]
        hf_refs, dhf_refs = (hff_ref, hfb_ref), (dhff_ref, dhfb_ref)

        dst_ref[...] = jnp.zeros_like(dst_ref)
        dlow_ref[...] = jnp.zeros_like(dlow_ref)

        def make_bwd_step(first):
            def bwd_step(j, carry):
                n = n_groups - 1 - j
                for d, rev in enumerate(dirs):
                    maskf, _, tri_t, row_masks = masks[d]
                    lower = lowers[d]
                    rows, chunk0 = _gla_rows(n, n_groups, rev)
                    hq, hf = hq_ref[rows, :], hf_refs[d][rows, :]
                    q, k, _, f, sg = _gla_gates(hq, hf, lower)
                    v = hi_ref[rows, :]
                    dout = do_ref[rows, :]
                    b = b_all[d, rows, :]
                    b_last3, b_mid3 = _gla_last_mid(b, rev)
                    b_last, b_mid = _gla_per_row(b_last3), _gla_per_row(b_mid3)
                    e1, e2, e3, e4 = jnp.exp(b - b_mid), jnp.exp(b_mid - b), jnp.exp(b_last - b), jnp.exp(b)
                    decay3 = jnp.exp(b_last3)
                    qi, ki, kt, qt = q * e1, k * e2, k * e3, q * e4
                    qib, kib, ktb = qi.astype(bf16), ki.astype(bf16), kt.astype(bf16)
                    vb, dob = v.astype(bf16), dout.astype(bf16)
                    a = (_dot_nt(qib, kib) * maskf).astype(bf16)
                    da = (_dot_nt(dob, vb) * maskf).astype(bf16)
                    dqi = _dot(da, kib)
                    dki = _dot_tn(da, qib)
                    into_state = _dot_tn(dob, _gla_block_diag(qt, row_masks))
                    dst = dst_ref[d]
                    sts, dsts, ddecay = [None] * GLA_GROUP, [None] * GLA_GROUP, [None] * GLA_GROUP
                    for c in reversed(_gla_scan_order(rev)):
                        sts[c] = st_all[d, chunk0 + c]
                        dsts[c] = dst.astype(bf16)
                        ddecay[c] = jnp.sum(dst * sts[c], axis=0, keepdims=True)[None]
                        dst = dst * decay3[c] + into_state[:, 128 * c:128 * (c + 1)]
                    dst_ref[d] = dst
                    dv = _dot_tn(a, dob) + _gla_diag(_dot_nt(ktb, jnp.concatenate(dsts, axis=0)))
                    dqt = _gla_diag(_dot(dob, jnp.concatenate([x.astype(bf16) for x in sts], axis=-1)))
                    dkt = _gla_diag(_dot(vb, jnp.concatenate(dsts, axis=-1)))
                    dq = dqi * e1 + dqt * e4
                    dk = dki * e2 + dkt * e3
                    db = dqi * qi - dki * ki + dqt * qt - dkt * kt
                    dlast3 = (jnp.sum((dkt * kt).reshape(GLA_GROUP, CHUNK, 128), axis=1, keepdims=True)
                              + jnp.concatenate(ddecay, axis=0) * decay3)
                    dlogf = _tri_sum(tri_t, db) + _gla_per_row(dlast3)
                    df = dlogf / f - dk
                    dhf_refs[d][rows, :] = (df * (1.0 - lower) * sg * (1.0 - sg)).astype(bf16)
                    dlow_ref[d:d + 1, :] += jnp.sum(df * (1.0 - sg), axis=0, keepdims=True)
                    sq = _sigmoid(hq)
                    dhq = dq * (sq * (1.0 + hq * (1.0 - sq)))
                    if first:
                        dq_acc[rows, :] = dhq
                        dv_acc[rows, :] = dv
                    else:
                        dhq_ref[rows, :] = (dq_acc[rows, :] + dhq).astype(bf16)
                        dhi_ref[rows, :] = (dv_acc[rows, :] + dv).astype(bf16)
                return carry
            return bwd_step

        lax.fori_loop(0, n_groups // 2, make_bwd_step(True), 0, unroll=2)
        lax.fori_loop(n_groups // 2, n_groups, make_bwd_step(False), 0, unroll=2)

        for d in range(2):
            dl = dlow_ref[d:d + 1, :] * lowers[d] * (1.0 - lowers[d])
            dlb_ref[d:d + 1, :] = dl
            dlb_ref[2 + d:3 + d, :] = -dl

    col = lambda base: pl.BlockSpec((s, 128), lambda h: (0, base // 128 + h))
    return pl.pallas_call(
        body, name="bwd_gla", grid=(HG_HEADS,),
        in_specs=[col(Z_HQ), col(Z_HFF), col(Z_HFB), col(Z_HI), pl.BlockSpec((4, 128), lambda h: (0, h)), col(0),
                  pl.BlockSpec((2, s, 128), lambda h: (0, 0, h)),
                  pl.BlockSpec((None, 2, n_chunks, 128, 128), lambda h: (h, 0, 0, 0, 0), pipeline_mode=pl.Buffered(1))],
        out_specs=[col(0), col(0), col(0), col(0), pl.BlockSpec((4, 128), lambda h: (0, h))],
        out_shape=[jax.ShapeDtypeStruct((s, 512), bf16)] * 4 + [jax.ShapeDtypeStruct((4, 512), f32)],
        scratch_shapes=[pltpu.VMEM((2, 128, 128), f32), pltpu.VMEM((s, 128), f32), pltpu.VMEM((s, 128), f32),
                        pltpu.VMEM((2, 128), f32)],
        compiler_params=_cp(("parallel",), VMEM_LIMIT),
    )(z, z, z, z, lb4, do, b_fwd, states)


def _bwd_attn(q, k, v, da, a32, tq, after):
    hh, s, _ = q.shape

    n_sub = max(1, tq // ATTN_SUB_ROWS)

    def body(q_ref, k_ref, v_ref, do_ref, o_ref, after_ref, dq_ref, dk_ref, dv_ref, p_all, ds_all, dol_ref, dkt_ref, dvt_ref):
        @pl.when(pl.program_id(1) == 0)
        def _():
            dkt_ref[...] = jnp.zeros_like(dkt_ref)
            dvt_ref[...] = jnp.zeros_like(dvt_ref)

        kb, vb = k_ref[...], v_ref[...]
        for t in range(n_sub):
            rows = slice(t * (tq // n_sub), (t + 1) * (tq // n_sub))
            sc = _dot_nt(q_ref[rows, :], kb)
            p = jnp.exp2((sc - jnp.max(sc, axis=-1, keepdims=True)) * (ATTN_SCALE * LOG2_E))
            inv_l = 1.0 / jnp.sum(p, axis=-1, keepdims=True)
            p_all[rows, :] = p.astype(bf16)
            dob = do_ref[rows, :]
            dof = dob.astype(f32)
            delta = jnp.sum(dof * o_ref[rows, :], axis=-1, keepdims=True)
            ds_all[rows, :] = p_all[rows, :] * ((_dot_nt(dob, vb) - delta) * inv_l).astype(bf16)
            dq_ref[rows, :] = _dot(ds_all[rows, :], kb) * ATTN_SCALE
            dol_ref[rows, :] = (dof * inv_l).astype(bf16)
        dkt_ref[...] += _dot_tn(q_ref[...], ds_all[...])
        dvt_ref[...] += _dot_tn(dol_ref[...], p_all[...])

        @pl.when(pl.program_id(1) == s // tq - 1)
        def _():
            dk_ref[...] = dkt_ref[...].T * ATTN_SCALE
            dv_ref[...] = dvt_ref[...].T

    return pl.pallas_call(
        body, name="bwd_attn", grid=(hh, s // tq),
        in_specs=[pl.BlockSpec((None, tq, QK_PAD), lambda h, i: (h, i, 0)),
                  pl.BlockSpec((None, s, QK_PAD), lambda h, i: (h, 0, 0)),
                  pl.BlockSpec((None, s, V_HEAD), lambda h, i: (h, 0, 0)),
                  pl.BlockSpec((tq, V_HEAD), lambda h, i: (i, h)), pl.BlockSpec((tq, V_HEAD), lambda h, i: (i, h)),
                  pl.BlockSpec(memory_space=pl.ANY)],
        out_specs=[pl.BlockSpec((None, tq, QK_PAD), lambda h, i: (h, i, 0)),
                   pl.BlockSpec((None, s, QK_PAD), lambda h, i: (h, 0, 0)),
                   pl.BlockSpec((None, s, V_HEAD), lambda h, i: (h, 0, 0))],
        out_shape=[jax.ShapeDtypeStruct((hh, s, QK_PAD), f32), jax.ShapeDtypeStruct((hh, s, QK_PAD), f32),
                   jax.ShapeDtypeStruct((hh, s, V_HEAD), f32)],
        scratch_shapes=[pltpu.VMEM((tq, s), bf16), pltpu.VMEM((tq, s), bf16), pltpu.VMEM((tq, V_HEAD), bf16),
                        pltpu.VMEM((QK_PAD, s), f32), pltpu.VMEM((V_HEAD, s), f32)],
        compiler_params=_cp(("parallel", "arbitrary"), VMEM_LIMIT),
    )(q, k, v, da, a32, after)


def _bwd_mla_proj(z, dq, dk, dv, cosb, sina, sinb, g_qa, g_kva, wqb, wkvb, g_qn, g_kn, tm):
    s = z.shape[0]
    hh = MLA_HEADS

    def body(cq_ref, ckv_ref, kr_ref, dq_ref, dk_ref, dv_ref, c_ref, sa_ref, sb_ref, gqa_ref, gkva_ref, wqb_ref, wkvb_ref,
             gqn_ref, gkn_ref, dz_ref, gwqb_ref, gwkvb_ref, dgqa_ref, dgkva_ref, dgqn_ref, dgkn_ref, dq0_ref, dkv0_ref):
        @pl.when(pl.program_id(0) == 0)
        def _():
            for r in (gwqb_ref, gwkvb_ref, dgqa_ref, dgkva_ref, dgqn_ref, dgkn_ref):
                r[...] = jnp.zeros_like(r)

        cq, ckv, kr = cq_ref[...], ckv_ref[...], kr_ref[...]
        gqa, gkva, gqn, gkn = gqa_ref[...], gkva_ref[...], gqn_ref[...], gkn_ref[...]
        cqn_b, rq, ckvn_b, rkv, q0, kv0 = _mla_qk_fwd(cq, ckv, gqa, gkva, wqb_ref[...], wkvb_ref[...])
        c, sa, sb = c_ref[...], -sa_ref[...], -sb_ref[...]
        kr_sq = jnp.sum(kr * kr, axis=-1, keepdims=True)
        dkr = jnp.zeros_like(kr)
        dgqn = jnp.zeros((1, QK_PAD), f32)
        dgkn = jnp.zeros((1, QK_PAD), f32)
        for h in range(hh):
            qh = q0[:, QK_PAD * h:QK_PAD * (h + 1)]
            rh = lax.rsqrt(jnp.sum(qh * qh, axis=-1, keepdims=True) * (1.0 / QK_HEAD) + EPS)
            dqh = dq_ref[h]
            dqn = jnp.concatenate([dqh[:, 0:128], _rope(dqh[:, 128:256], c, sa, sb)], axis=-1)
            dq0h, dgx = _rms_bwd(dqn, qh, rh, gqn, QK_HEAD)
            dq0_ref[:, QK_PAD * h:QK_PAD * (h + 1)] = dq0h.astype(bf16)
            dgqn = dgqn + jnp.sum(dgx, axis=0, keepdims=True)

            kn_ = kv0[:, 256 * h:256 * h + 128]
            k0 = jnp.concatenate([kn_, kr], axis=-1)
            rk = lax.rsqrt((jnp.sum(kn_ * kn_, axis=-1, keepdims=True) + kr_sq) * (1.0 / QK_HEAD) + EPS)
            dkh = dk_ref[h]
            dkn = jnp.concatenate([dkh[:, 0:128], _rope(dkh[:, 128:256], c, sa, sb)], axis=-1)
            dk0, dgx = _rms_bwd(dkn, k0, rk, gkn, QK_HEAD)
            dgkn = dgkn + jnp.sum(dgx, axis=0, keepdims=True)
            dkv0_ref[:, 256 * h:256 * h + 128] = dk0[:, 0:128].astype(bf16)
            dkv0_ref[:, 256 * h + 128:256 * h + 256] = dv_ref[h].astype(bf16)
            dkr = dkr + dk0[:, 128:256]
        dgqn_ref[...] += dgqn
        dgkn_ref[...] += dgkn
        gwqb_ref[...] += _dot_tn(cqn_b, dq0_ref[...])
        gwkvb_ref[...] += _dot_tn(ckvn_b, dkv0_ref[...])
        dcq, dgx = _rms_bwd(_dot_nt(dq0_ref[...], wqb_ref[...]), cq, rq, gqa, Q_LORA)
        dgqa_ref[...] += jnp.sum(dgx, axis=0, keepdims=True)
        dckv, dgx = _rms_bwd(_dot_nt(dkv0_ref[...], wkvb_ref[...]), ckv, rkv, gkva, KV_LORA)
        dgkva_ref[...] += jnp.sum(dgx, axis=0, keepdims=True)
        dz_ref[:, 0:256] = dcq.astype(bf16)
        dz_ref[:, 256:512] = dckv.astype(bf16)
        dz_ref[:, 512:640] = dkr.astype(bf16)

    row128 = pl.BlockSpec((tm, 128), lambda i: (i, 0))
    hd = lambda w: pl.BlockSpec((hh, tm, w), lambda i: (0, i, 0))
    return pl.pallas_call(
        body, name="bwd_mla_proj", grid=(s // tm,),
        in_specs=[pl.BlockSpec((tm, 256), lambda i: (i, Z_CQ // 256)), pl.BlockSpec((tm, 256), lambda i: (i, Z_CKV // 256)),
                  pl.BlockSpec((tm, 128), lambda i: (i, Z_KR // 128)), hd(QK_PAD), hd(QK_PAD), hd(V_HEAD),
                  row128, row128, row128,
                  _const_spec((1, 256)), _const_spec((1, 256)), _const_spec((256, 1024)), _const_spec((256, 1024)),
                  _const_spec((1, 256)), _const_spec((1, 256))],
        out_specs=[pl.BlockSpec((tm, 640), lambda i: (i, 0)), _acc_spec((256, 1024)), _acc_spec((256, 1024)),
                   _acc_spec((1, 256)), _acc_spec((1, 256)), _acc_spec((1, 256)), _acc_spec((1, 256))],
        out_shape=[jax.ShapeDtypeStruct((s, 640), bf16), jax.ShapeDtypeStruct((256, 1024), f32), jax.ShapeDtypeStruct((256, 1024), f32)]
        + [jax.ShapeDtypeStruct((1, 256), f32)] * 4,
        scratch_shapes=[pltpu.VMEM((tm, 1024), bf16), pltpu.VMEM((tm, 1024), bf16)],
        compiler_params=_cp(("arbitrary",), VMEM_LIMIT),
    )(z, z, z, dq, dk, dv, cosb, sina, sinb, g_qa, g_kva, wqb, wkvb, g_qn, g_kn)


def _bwd_in(segments, wz, x, g_mix, d2, tm):
    s, d = x.shape
    n_seg = len(segments)

    def body(*refs):
        dz_refs, w_refs = refs[:n_seg], refs[n_seg:2 * n_seg]
        x_ref, g_ref, d2_ref, gx_ref, dg_ref = refs[2 * n_seg:]

        @pl.when(pl.program_id(0) == 0)
        def _():
            dg_ref[...] = jnp.zeros_like(dg_ref)

        dh = _dot(dz_refs[0][...], w_refs[0][...])
        for a_ref, w_ref in zip(dz_refs[1:], w_refs[1:]):
            dh = dh + _dot(a_ref[...], w_ref[...])
        x, g = x_ref[...], g_ref[...]
        r = lax.rsqrt(jnp.sum(x * x, axis=-1, keepdims=True) * (1.0 / d) + EPS)
        dx, dgx = _rms_bwd(dh, x, r, g, d)
        gx_ref[...] = d2_ref[...] + dx
        dg_ref[...] += jnp.sum(dgx, axis=0, keepdims=True)

    rowd = pl.BlockSpec((tm, d), lambda i: (i, 0))
    dz_specs = [pl.BlockSpec((tm, w), functools.partial(lambda i, j: (i, j), j=ja)) for _, w, ja, _ in segments]
    w_specs = [pl.BlockSpec((w, d), functools.partial(lambda i, j: (j, 0), j=jw), pipeline_mode=pl.Buffered(1))
               for _, w, _, jw in segments]
    return pl.pallas_call(
        body, name="bwd_in", grid=(s // tm,),
        in_specs=dz_specs + w_specs + [rowd, _const_spec((1, d)), rowd],
        out_specs=[rowd, _acc_spec((1, d))],
        out_shape=[jax.ShapeDtypeStruct((s, d), f32), jax.ShapeDtypeStruct((1, d), f32)],
        compiler_params=_cp(("arbitrary",), VMEM_LIMIT),
    )(*[a for a, _, _, _ in segments], *([wz] * n_seg), x, g_mix, d2)


def _pick_tile(n, cap):
    best = None
    for t in range(LANES, cap + 1, LANES):
        if n % t == 0:
            best = t
    return best if best is not None else n


def _mm_tn_many(a, bs, name, tm, transposed=False):
    kk, m = a.shape
    n_b = len(bs)
    tk = min(1024, kk)
    n_k = kk // tk

    def body(a_ref, *refs):
        b_refs, o_refs, acc_refs = refs[:n_b], refs[n_b:2 * n_b], refs[2 * n_b:]

        @pl.when(pl.program_id(1) == 0)
        def _():
            for acc in acc_refs:
                acc[...] = jnp.zeros_like(acc)
        a_blk = a_ref[...].astype(bf16)
        for b_ref, acc in zip(b_refs, acc_refs):
            acc[...] += _dot_tn(a_blk, b_ref[...].astype(bf16))

        @pl.when(pl.program_id(1) == n_k - 1)
        def _():
            for o_ref, acc in zip(o_refs, acc_refs):
                o_ref[...] = (acc[...].T if transposed else acc[...]).astype(bf16)

    if transposed:
        out_specs = [pl.BlockSpec((b.shape[1], tm), lambda i, k: (0, i)) for b in bs]
        out_shape = [jax.ShapeDtypeStruct((b.shape[1], m), bf16) for b in bs]
    else:
        out_specs = [pl.BlockSpec((tm, b.shape[1]), lambda i, k: (i, 0)) for b in bs]
        out_shape = [jax.ShapeDtypeStruct((m, b.shape[1]), bf16) for b in bs]
    return pl.pallas_call(
        body, name=name, grid=(m // tm, n_k),
        in_specs=[pl.BlockSpec((tk, tm), lambda i, k: (k, i))] + [pl.BlockSpec((tk, b.shape[1]), lambda i, k: (k, 0)) for b in bs],
        out_specs=out_specs,
        out_shape=out_shape,
        scratch_shapes=[pltpu.VMEM((tm, b.shape[1]), f32) for b in bs],
        compiler_params=_cp(("parallel", "arbitrary"), VMEM_LIMIT),
    )(a, *bs)


def _mm_tn(a, b, name):
    kk, m = a.shape
    _, n = b.shape
    tm = _pick_tile(m, 1408)
    tn = _pick_tile(n, 1408)
    tk = min(1024, kk)

    n_k = kk // tk

    def body(a_ref, b_ref, o_ref, acc_ref):
        @pl.when(pl.program_id(2) == 0)
        def _():
            acc_ref[...] = jnp.zeros_like(acc_ref)
        acc_ref[...] += _dot_tn(a_ref[...].astype(bf16), b_ref[...].astype(bf16))

        @pl.when(pl.program_id(2) == n_k - 1)
        def _():
            o_ref[...] = acc_ref[...].astype(bf16)

    return pl.pallas_call(
        body, name=name, grid=(m // tm, n // tn, n_k),
        in_specs=[pl.BlockSpec((tk, tm), lambda i, j, k: (k, i)), pl.BlockSpec((tk, tn), lambda i, j, k: (k, j))],
        out_specs=pl.BlockSpec((tm, tn), lambda i, j, k: (i, j)),
        out_shape=jax.ShapeDtypeStruct((m, n), bf16),
        scratch_shapes=[pltpu.VMEM((tm, tn), f32)],
        compiler_params=_cp(("parallel", "parallel", "arbitrary"), VMEM_LIMIT),
    )(a, b)


def _rope_tables(positions):
    inv_freq = ROPE_THETA ** (-jnp.arange(0, QK_ROPE, 2, dtype=f32) / QK_ROPE)
    ang = positions.astype(f32)[:, None] * inv_freq
    cos, sin = jnp.cos(ang), jnp.sin(ang)
    zero = jnp.zeros_like(cos)
    return (jnp.concatenate([cos, cos, zero, zero], axis=1), jnp.concatenate([zero, sin, zero, zero], axis=1),
            jnp.concatenate([-sin, zero, zero, zero], axis=1))


def _pad256(g):
    return jnp.pad(g.reshape(1, QK_HEAD), ((0, 0), (0, QK_PAD - QK_HEAD)))


RELAYOUT_BLOCKS = 8
FIRST = ("w_in", "w_qb", "w_kvb", "lb_param")
SECOND = ("w_o", "w_gate", "w_up", "w_down", "w_ple_gate", "w_ple_proj")
ROW_SHARDED = ("w_o", "w_down", "w_ple_gate")


def _col_moves(j):
    width = BIG["w_in"][0]
    lo = width * j
    w_in = [(max(lo, a) - lo, min(lo + width, b) - lo, d + max(lo, a) - a)
            for a, b, d in Z_SEGMENTS if max(lo, a) < min(lo + width, b)]
    head, half = divmod(j, 2)
    whole = lambda n: [(0, BIG[n][1], BIG[n][1] * j)]
    return {"w_in": w_in, "w_qb": [(0, 96, QK_PAD * head + 96 * half)], "w_kvb": whole("w_kvb"),
            "w_ple_proj": whole("w_ple_proj"), "lb_param": whole("lb_param")}


def _kernel_shape(name):
    rows, cols = BIG[name]
    if name == "w_in":
        return (Z_W, cols)
    return (rows, MLA_HEADS * QK_PAD if name == "w_qb" else N_DEV * cols)


def _relayout_specs(names, by_dev):
    specs = []
    for n in names:
        rows, cols = BIG[n]
        if n == "lb_param":
            specs.append(_acc_spec((N_DEV, rows, cols) if by_dev else _kernel_shape(n)))
        elif n == "w_in":
            cb = cols // RELAYOUT_BLOCKS
            specs.append(pl.BlockSpec((N_DEV, rows, cb), lambda i: (0, 0, i)) if by_dev else pl.BlockSpec((Z_W, cb), lambda i: (0, i)))
        elif by_dev:
            specs.append(pl.BlockSpec((N_DEV, rows // RELAYOUT_BLOCKS, cols), lambda i: (0, i, 0)))
        else:
            specs.append(pl.BlockSpec((rows // RELAYOUT_BLOCKS, _kernel_shape(n)[1]), lambda i: (i, 0)))
    return specs


def _weights_in(gathered, names, name):
    n = len(names)

    def body(*refs):
        ins, outs = dict(zip(names, refs[:n])), dict(zip(names, refs[n:]))
        if "w_in" in outs:
            outs["w_in"][Z_KR + QK_ROPE:Z_W, :] = jnp.zeros((Z_W - Z_KR - QK_ROPE, outs["w_in"].shape[1]), bf16)
        if "w_qb" in outs:
            for h in range(MLA_HEADS):
                outs["w_qb"][:, QK_PAD * h + QK_HEAD:QK_PAD * (h + 1)] = jnp.zeros((outs["w_qb"].shape[0], QK_PAD - QK_HEAD), bf16)
        for j in range(N_DEV):
            for wn, moves in _col_moves(j).items():
                if wn in outs:
                    for s0, s1, d0 in moves:
                        if wn == "w_in":
                            outs[wn][d0:d0 + s1 - s0, :] = ins[wn][j, s0:s1, :]
                        else:
                            outs[wn][:, d0:d0 + s1 - s0] = ins[wn][j, :, s0:s1]

    outs = pl.pallas_call(
        body, name=name, grid=(RELAYOUT_BLOCKS,), in_specs=_relayout_specs(names, True), out_specs=_relayout_specs(names, False),
        out_shape=[jax.ShapeDtypeStruct(_kernel_shape(wn), gathered[wn].dtype) for wn in names],
        compiler_params=_cp(("arbitrary",), VMEM_LIMIT),
    )(*[gathered[wn] for wn in names])
    return dict(zip(names, outs))


def _grads_out(sources, names, name):
    pieces = [(wn, start, arr) for wn in names for start, arr in sources[wn]]
    n_in = len(pieces)

    def body(*refs):
        outs = dict(zip(names, refs[n_in:]))

        def cols(wn, c0, c1):
            for (pn, start, arr), ref in zip(pieces, refs[:n_in]):
                if pn == wn and start <= c0 and c1 <= start + arr.shape[0 if wn == "w_in" else 1]:
                    return ref[c0 - start:c1 - start, :] if wn == "w_in" else ref[:, c0 - start:c1 - start]

        for j in range(N_DEV):
            for wn, moves in _col_moves(j).items():
                if wn in outs:
                    for s0, s1, d0 in moves:
                        if wn == "w_in":
                            outs[wn][j, s0:s1, :] = cols(wn, d0, d0 + s1 - s0).astype(bf16)
                        else:
                            outs[wn][j, :, s0:s1] = cols(wn, d0, d0 + s1 - s0).astype(bf16)

    def in_spec(wn, arr):
        if wn == "lb_param":
            return _acc_spec(arr.shape)
        if wn == "w_in":
            return pl.BlockSpec((arr.shape[0], arr.shape[1] // RELAYOUT_BLOCKS), lambda i: (0, i))
        return pl.BlockSpec((arr.shape[0] // RELAYOUT_BLOCKS, arr.shape[1]), lambda i: (i, 0))

    in_specs = [in_spec(wn, arr) for wn, _, arr in pieces]
    outs = pl.pallas_call(
        body, name=name, grid=(RELAYOUT_BLOCKS,), in_specs=in_specs, out_specs=_relayout_specs(names, True),
        out_shape=[jax.ShapeDtypeStruct((N_DEV, *BIG[wn]), bf16) for wn in names],
        compiler_params=_cp(("arbitrary",), VMEM_LIMIT),
    )(*[arr for _, _, arr in pieces])
    return dict(zip(names, outs))


def kernel(x, p, positions, g_mix, w_in, g_qa, g_kva, w_qb, w_kvb, g_qn, g_kn, lb_param, g_hgo, w_o, g_ffn, w_gate, w_up, w_down, g_ple, w_ple_gate, w_ple_proj, loss_target, m_g_mix, m_w_in, m_g_qa, m_g_kva, m_w_qb, m_w_kvb, m_g_qn, m_g_kn, m_lb_param, m_g_hgo, m_w_o, m_g_ffn, m_w_gate, m_w_up, m_w_down, m_g_ple, m_w_ple_gate, m_w_ple_proj, v_g_mix, v_w_in, v_g_qa, v_g_kva, v_w_qb, v_w_kvb, v_g_qn, v_g_kn, v_lb_param, v_g_hgo, v_w_o, v_g_ffn, v_w_gate, v_w_up, v_w_down, v_g_ple, v_w_ple_gate, v_w_ple_proj):
    w_all = dict(g_mix=g_mix, g_qa=g_qa, g_kva=g_kva, g_qn=g_qn, g_kn=g_kn, g_hgo=g_hgo, g_ffn=g_ffn, g_ple=g_ple,
                 w_in=w_in, w_qb=w_qb, w_kvb=w_kvb, w_o=w_o, w_gate=w_gate, w_up=w_up, w_down=w_down,
                 w_ple_gate=w_ple_gate, w_ple_proj=w_ple_proj, lb_param=lb_param)
    m_all = dict(g_mix=m_g_mix, g_qa=m_g_qa, g_kva=m_g_kva, g_qn=m_g_qn, g_kn=m_g_kn, g_hgo=m_g_hgo, g_ffn=m_g_ffn,
                 g_ple=m_g_ple, w_in=m_w_in, w_qb=m_w_qb, w_kvb=m_w_kvb, w_o=m_w_o, w_gate=m_w_gate, w_up=m_w_up,
                 w_down=m_w_down, w_ple_gate=m_w_ple_gate, w_ple_proj=m_w_ple_proj, lb_param=m_lb_param)
    v_all = dict(g_mix=v_g_mix, g_qa=v_g_qa, g_kva=v_g_kva, g_qn=v_g_qn, g_kn=v_g_kn, g_hgo=v_g_hgo, g_ffn=v_g_ffn,
                 g_ple=v_g_ple, w_in=v_w_in, w_qb=v_w_qb, w_kvb=v_w_kvb, w_o=v_w_o, w_gate=v_w_gate, w_up=v_w_up,
                 w_down=v_w_down, w_ple_gate=v_w_ple_gate, w_ple_proj=v_w_ple_proj, lb_param=v_lb_param)
    me_idx = jnp.stack([_me()]).astype(jnp.int32)
    x, p, positions, target = x[0], p[0, 0], positions[0], loss_target[0]
    s = x.shape[0]
    tm, tm_ffn, tq_f, tq_b = min(512, s), min(1024, s), min(2048, s), min(1024, s)
    g_mix, g_qa, g_kva, g_qn, g_kn, g_hgo, g_ffn, g_ple = (w_all[n].reshape(1, -1) for n in SMALL)
    g_qn_p, g_kn_p = _pad256(g_qn), _pad256(g_kn)
    cosb, sina, sinb = _rope_tables(positions)
    as_shard = lambda n, a: a[0].T if n in TRANSPOSED else a.reshape(BIG[n])
    shard = lambda n: as_shard(n, w_all[n])

    first = _all_gather([shard(n) for n in FIRST], [f32 if n == "lb_param" else bf16 for n in FIRST], "ag_first")
    lands = _cast_to_slot([shard(n) for n in SECOND], me_idx, first[0])
    ag2, token = _exchange_start([], lands, "ag_second_start")
    wk = _weights_in(dict(zip(FIRST, first)), FIRST, "weights_in_first")
    wz, wqb, wkvb, lb4 = (wk[n] for n in FIRST)

    h1, z = _fwd_in(x, g_mix, wz, tm)
    q, k, v = _fwd_mla_proj(z, cosb + token[0, 0], sina, sinb, g_qa, g_kva, wqb, wkvb, g_qn_p, g_kn_p, tm)
    o, gla_b, gla_states = _fwd_gla(z, lb4)
    a, a32 = _fwd_attn(q, k, v, tq_f, o)

    second = dict(zip(SECOND, _exchange_wait(ag2, [a, o], "ag_second_wait")[1]))
    w_pp = _weights_in(second, ("w_ple_proj",), "weights_in_second")["w_ple_proj"]
    w_o, w_down, w_pg, w_gate, w_up = (second[n].reshape(N_DEV * BIG[n][0], BIG[n][1]) for n in ROW_SHARDED + ("w_gate", "w_up"))

    x2, cat = _fwd_mix(a, o, z, g_hgo, x, w_o, tm)
    x3, gp, up = _fwd_ffn(x2, g_ffn, w_gate, w_up, w_down, tm)
    d3, gw_pg, gw_pp, dg_ple, loss_tile = _ple_loss_fwd_bwd(x3, g_ple, w_pg, p, w_pp, target, tm)
    act, dgp, dup = _bwd_ffn_hidden(d3, gp, up, w_down, tm, D_FF // 2)
    d2, h2, dg_ffn = _bwd_ffn_in(d3, x2, dgp, dup, g_ffn, w_gate, w_up, tm)
    da, do, dz_hg, dg_hgo, gw_o = _bwd_mix(d2, w_o, o, z, g_hgo, cat, tm)

    gw_gate, gw_up = _mm_tn_many(h2, [dgp, dup], "dw_gate_up", 512, transposed=True)
    blocks = _grads_out({"w_ple_proj": [(0, gw_pp)]}, ("w_ple_proj",), "grads_out_second")
    row_grads = {"w_o": gw_o, "w_down": _mm_tn(act, d3, "dw_down"), "w_ple_gate": gw_pg, "w_gate": gw_gate, "w_up": gw_up}
    blocks.update({n: g.reshape(N_DEV, *BIG[n]) for n, g in row_grads.items()})
    empty = lambda names: [lax.empty((N_PEERS, *BIG[n]), bf16) for n in names]
    rs2, token = _exchange_start([blocks[n] for n in SECOND], empty(SECOND), "rs_second_start")

    dq, dk, dv = _bwd_attn(q, k, v, da, a32, tq_b, token)
    dz_hq, dz_hff, dz_hfb, dz_hi, dlb4 = _bwd_gla(z, lb4 + token[0, 0], do, gla_b, gla_states)
    dz_mla, gw_qb, gw_kvb, dg_qa, dg_kva, dg_qn, dg_kn = _bwd_mla_proj(
        z, dq, dk, dv, cosb, sina, sinb, g_qa, g_kva, wqb, wkvb, g_qn_p, g_kn_p, tm)

    gz = list(zip((Z_HQ, Z_HFF, Z_HFB, Z_HI, Z_HG, Z_CQ),
                  _mm_tn_many(h1, [dz_hq, dz_hff, dz_hfb, dz_hi, dz_hg, dz_mla], "dw_in", 1024, transposed=True)))
    blocks1 = _grads_out({"w_in": gz, "w_qb": [(0, gw_qb)], "w_kvb": [(0, gw_kvb)],
                          "lb_param": [(0, dlb4)]}, FIRST, "grads_out_first")
    rs1, token = _exchange_start([blocks1[n] for n in FIRST], empty(FIRST), "rs_first_start")

    result = {}

    def adam(names, lands, src, n_blocks, after=()):
        flipped = TRANSPOSED
        given = lambda arrs: [arrs[n][0].T if n in flipped else arrs[n] for n in names]
        outs = _adam_shards(me_idx, [src[n] for n in names], lands, given(w_all), given(m_all), given(v_all), n_blocks,
                            "adamw_" + names[0], after)
        for n, o in zip(names, outs):
            result[n] = [t.T[None] for t in o] if n in flipped else o
        return outs[0][0]

    blocks2, lands2 = (dict(zip(SECOND, arrs)) for arrs in _exchange_wait(rs2, [token], "rs_second_wait"))
    by2 = ("w_down", "w_gate", "w_up")
    by8 = tuple(n for n in SECOND if n not in by2)
    done = [adam(by8, [lands2[n] for n in by8], blocks2, 8), adam(by2, [lands2[n] for n in by2], blocks2, 2)]

    segments = [(dz_hq, 512, 0, Z_HQ // 512), (dz_hff, 512, 0, Z_HFF // 512), (dz_hfb, 512, 0, Z_HFB // 512),
                (dz_hi, 512, 0, Z_HI // 512), (dz_hg, 512, 0, Z_HG // 512), (dz_mla, 640, 0, Z_CQ // 640)]
    grad_x, dg_mix = _bwd_in(segments, wz, x, g_mix + token[0, 0], d2, tm)
    dgains = (dg_mix, dg_qa, dg_kva, dg_qn, dg_kn, dg_hgo, dg_ffn, dg_ple)

    vec = jnp.concatenate(list(dgains) + [loss_tile[0:1]], axis=1)
    parts = _all_gather([vec], [f32], "ag_gains")[0]
    outs, loss_row = _adam_gains(parts, [w_all[n] for n in SMALL], [m_all[n] for n in SMALL], [v_all[n] for n in SMALL])
    result.update(zip(SMALL, outs))

    blocks1, lands1 = _exchange_wait(rs1, [grad_x, loss_row, *done], "rs_first_wait")
    adam(FIRST, lands1, dict(zip(FIRST, blocks1)), 8)

    order = ("g_mix", "w_in", "g_qa", "g_kva", "w_qb", "w_kvb", "g_qn", "g_kn", "lb_param", "g_hgo", "w_o", "g_ffn",
             "w_gate", "w_up", "w_down", "g_ple", "w_ple_gate", "w_ple_proj")
    return (loss_row[0, 0], grad_x[None], *[result[n][k] for k in range(4) for n in order])
```

```python
import functools
import math

import jax
import jax.numpy as jnp
from jax import lax
from jax.experimental import pallas as pl
from jax.experimental.pallas import tpu as pltpu

f32 = jnp.float32
bf16 = jnp.bfloat16

N_DEV = 8
MLA_HEADS = 4
QK_NOPE = 128
QK_ROPE = 64
QK_HEAD = QK_NOPE + QK_ROPE
QK_PAD = 256
V_HEAD = 128
Q_LORA = 256
KV_LORA = 256
HG_HEADS = 4
CHUNK = 64
D_FF = 2816
PLE_DIM = 256
ROPE_THETA = 10000.0
EPS = 1e-6
ATTN_SCALE = QK_HEAD ** -0.5
LOG2_E = math.log2(math.e)
ATTN_SUB_ROWS = 256
Z_HQ, Z_HFF, Z_HFB, Z_HI, Z_HG, Z_CQ, Z_CKV, Z_KR, Z_W = 0, 512, 1024, 1536, 2048, 2560, 2816, 3072, 3200

ADAM_LR, ADAM_B1, ADAM_B2, ADAM_EPS, ADAM_WD, ADAM_STEP = 0.001, 0.9, 0.999, 1e-08, 0.01, 10

LANES = 128
BIG = {"w_in": (392, 1024), "w_qb": (256, 96), "w_kvb": (256, 128), "w_o": (128, 1024), "w_gate": (352, 1024),
       "w_up": (352, 1024), "w_down": (352, 1024), "w_ple_gate": (128, 1024), "w_ple_proj": (256, 128),
       "lb_param": (4, 64)}
TRANSPOSED = ("w_gate", "w_up", "w_in")
SMALL = {"g_mix": (0, 1024), "g_qa": (1024, 256), "g_kva": (1280, 256), "g_qn": (1536, 192), "g_kn": (1792, 192),
         "g_hgo": (2048, 512), "g_ffn": (2560, 1024), "g_ple": (3584, 1024)}
LOSS_OFF = 4608
GAIN_VEC = LOSS_OFF + LANES
Z_SEGMENTS = ((0, 256, Z_CQ), (256, 512, Z_CKV), (512, 576, Z_KR), (576, 1088, Z_HQ), (1088, 1600, Z_HFF),
              (1600, 2112, Z_HFB), (2112, 2624, Z_HI), (2624, 3136, Z_HG))

VMEM_LIMIT = 56 * 1024 * 1024
MESH = pl.DeviceIdType.MESH


def _cp(sem=None, vmem=None):
    return pltpu.CompilerParams(dimension_semantics=sem, vmem_limit_bytes=vmem)


def _const_spec(shape):
    nd = len(shape)
    return pl.BlockSpec(shape, lambda *_: (0,) * nd, pipeline_mode=pl.Buffered(1))


def _acc_spec(shape):
    nd = len(shape)
    return pl.BlockSpec(shape, lambda *_: (0,) * nd)


def _sigmoid(x):
    return jax.nn.sigmoid(x)


def _dot(a, b):
    return jnp.dot(a, b, preferred_element_type=f32)


def _dot_nt(a, b):
    return lax.dot_general(a, b, (((1,), (1,)), ((), ())), preferred_element_type=f32)


def _dot_tn(a, b):
    return lax.dot_general(a, b, (((0,), (0,)), ((), ())), preferred_element_type=f32)


def _rms_fwd(x, g, width):
    r = lax.rsqrt(jnp.sum(x * x, axis=-1, keepdims=True) * (1.0 / width) + EPS)
    return x * r * g, r


def _rms_bwd(dy, x, r, g, width):
    u = dy * g
    dx = r * u - x * (r * r * r) * (jnp.sum(u * x, axis=-1, keepdims=True) * (1.0 / width))
    return dx, dy * x * r


class _Both:
    def __init__(self, *copies):
        self.copies = copies

    def start(self):
        for cp in self.copies:
            cp.start()

    def wait(self):
        for cp in self.copies:
            cp.wait()


def _rope(b, c, sa, sb):
    return b * c + pltpu.roll(b, 32, 1) * sa + pltpu.roll(b, 96, 1) * sb


def _all_gather(shards, dtypes, name):
    n = len(shards)

    def body(*refs):
        in_refs, out_refs, stage = refs[:n], refs[n:2 * n], refs[2 * n:3 * n]
        send_sems, recv_sems, local_sems = refs[3 * n:]
        for w in range(n):
            stage[w][...] = in_refs[w][...].astype(stage[w].dtype)
        x, y, c = lax.axis_index("x"), lax.axis_index("y"), lax.axis_index("c")
        me, sibling = (x, y, c), (x, y, 1 - c)
        chips = [(1 - x, y), (x, 1 - y), (1 - x, 1 - y)]

        def slot(w, px, py, pc):
            return out_refs[w].at[4 * px + 2 * py + pc]

        def copy(w, k, block, to, src=None):
            return pltpu.make_async_remote_copy(
                src_ref=slot(w, *block) if src is None else src, dst_ref=slot(w, *block),
                send_sem=send_sems.at[w, k], recv_sem=recv_sems.at[w, k], device_id=to, device_id_type=MESH)

        first = []
        for j, chip in enumerate(chips):
            first += [copy(w, 1 + j, me, (*chip, c), src=stage[w]) for w in range(n)]
        first += [copy(w, 0, me, sibling, src=stage[w]) for w in range(n)]
        mine = [pltpu.make_async_copy(stage[w], slot(w, *me), local_sems.at[w]) for w in range(n)]
        for cp in first + mine:
            cp.start()
        passed = []
        for j, chip in enumerate(chips):
            for w in range(n):
                copy(w, 1 + j, (*chip, c), me).wait_recv()
                passed.append(copy(w, 4 + j, (*chip, c), sibling))
                passed[-1].start()
        for w in range(n):
            copy(w, 0, sibling, me).wait_recv()
        for j, chip in enumerate(chips):
            for w in range(n):
                copy(w, 4 + j, (*chip, 1 - c), me).wait_recv()
        for cp in first + passed:
            cp.wait_send()
        for cp in mine:
            cp.wait()

    return pl.pallas_call(
        body, name=name,
        out_shape=[jax.ShapeDtypeStruct((N_DEV, *s.shape), dt) for s, dt in zip(shards, dtypes)],
        in_specs=[pl.BlockSpec(memory_space=pltpu.VMEM)] * n,
        out_specs=[pl.BlockSpec(memory_space=pl.ANY)] * n,
        scratch_shapes=[pltpu.VMEM(s.shape, dt) for s, dt in zip(shards, dtypes)]
        + [pltpu.SemaphoreType.DMA((n, 7)), pltpu.SemaphoreType.DMA((n, 7)), pltpu.SemaphoreType.DMA((n,))],
        compiler_params=_cp(None, VMEM_LIMIT),
    )(*shards)


N_PEERS = N_DEV - 1
HBM_SPEC = pl.BlockSpec(memory_space=pltpu.HBM)
SEM_SPEC = pl.BlockSpec(memory_space=pltpu.SEMAPHORE)
DATAFLOW = pltpu.SideEffectType.DATAFLOW_SIDE_EFFECTING


def _me():
    return 4 * lax.axis_index("x") + 2 * lax.axis_index("y") + lax.axis_index("c")


def _peer(k):
    x, y, c = lax.axis_index("x"), lax.axis_index("y"), lax.axis_index("c")
    px = 1 - x if k & 4 else x
    py = 1 - y if k & 2 else y
    pc = 1 - c if k & 1 else c
    return (px, py, pc), 4 * px + 2 * py + pc


def _exchange_copies(src_refs, land_refs, send_sems, recv_sems, gather):
    cps = []
    me = _me()
    for k in range(1, N_DEV):
        peer, peer_idx = _peer(k)
        for w, land in enumerate(land_refs):
            src = land.at[me] if gather else src_refs[w].at[peer_idx]
            dst = land.at[me] if gather else land.at[k - 1]
            cps.append(pltpu.make_async_remote_copy(
                src_ref=src, dst_ref=dst, send_sem=send_sems.at[N_PEERS * w + k - 1], recv_sem=recv_sems.at[N_PEERS * w + k - 1],
                device_id=peer, device_id_type=MESH))
    return cps


def _exchange_start(srcs, lands, name):
    n_src, n = len(srcs), len(lands)

    def body(*refs):
        src_refs, land_refs = refs[:n_src], refs[n_src:n_src + n]
        send_sems, recv_sems = refs[n_src + n], refs[n_src + n + 1]
        token = refs[-1]
        for cp in _exchange_copies(src_refs, land_refs, send_sems, recv_sems, gather=not n_src):
            cp.start()
        token[...] = jnp.zeros_like(token)

    arrays = [pltpu.with_memory_space_constraint(a, pltpu.HBM) for a in (*srcs, *lands)]
    outs = pl.pallas_call(
        body, name=name,
        out_shape=(pltpu.SemaphoreType.DMA((n * N_PEERS,)), pltpu.SemaphoreType.DMA((n * N_PEERS,)),
                   *[pltpu.HBM(a.shape, a.dtype) for a in arrays], jax.ShapeDtypeStruct((8, LANES), f32)),
        in_specs=[HBM_SPEC] * len(arrays),
        out_specs=(SEM_SPEC, SEM_SPEC, *[HBM_SPEC] * len(arrays), pl.BlockSpec(memory_space=pltpu.VMEM)),
        input_output_aliases={i: 2 + i for i in range(len(arrays))},
        compiler_params=pltpu.CompilerParams(has_side_effects=DATAFLOW),
    )(*arrays)
    return (outs[0], outs[1], outs[2:2 + n_src], outs[2 + n_src:2 + n_src + n]), outs[-1]


def _exchange_wait(state, after, name):
    send_sems, recv_sems, srcs, lands = state
    n_src, n = len(srcs), len(lands)

    def body(*refs):
        src_refs, land_refs = refs[:n_src], refs[n_src:n_src + n]
        send_ref, recv_ref = refs[n_src + n], refs[n_src + n + 1]
        for cp in _exchange_copies(src_refs, land_refs, send_ref, recv_ref, gather=not n_src):
            cp.wait_send()
            cp.wait_recv()

    arrays = (*srcs, *lands)
    outs = pl.pallas_call(
        body, name=name,
        out_shape=tuple(pltpu.HBM(a.shape, a.dtype) for a in arrays),
        in_specs=[HBM_SPEC] * len(arrays) + [SEM_SPEC, SEM_SPEC] + [pl.BlockSpec(memory_space=pl.ANY)] * len(after),
        out_specs=tuple([HBM_SPEC] * len(arrays)),
        input_output_aliases={i: i for i in range(len(arrays))},
        compiler_params=pltpu.CompilerParams(has_side_effects=DATAFLOW),
    )(*arrays, send_sems, recv_sems, *after)
    return outs[:n_src], outs[n_src:]


def _cast_to_slot(shards, me_idx, after):
    n = len(shards)

    def body(i_ref, *refs):
        for w in range(n):
            refs[n + 1 + w][...] = refs[w][...].astype(bf16)

    return pl.pallas_call(
        body, name="cast_to_slot",
        grid_spec=pltpu.PrefetchScalarGridSpec(
            num_scalar_prefetch=1, grid=(1,),
            in_specs=[pl.BlockSpec(s.shape, lambda i, m: (0, 0)) for s in shards] + [pl.BlockSpec(memory_space=pl.ANY)],
            out_specs=[pl.BlockSpec((None, *s.shape), lambda i, m: (m[0], 0, 0)) for s in shards]),
        out_shape=[jax.ShapeDtypeStruct((N_DEV, *s.shape), bf16) for s in shards],
        compiler_params=_cp(("arbitrary",), VMEM_LIMIT),
    )(me_idx, *shards, after)


def _row_block(rows, n_blocks):
    return (rows // n_blocks, True) if rows % (16 * n_blocks) == 0 else (rows, False)


def _adam_math(w, g, m, v):
    m = ADAM_B1 * m + (1.0 - ADAM_B1) * g
    v = ADAM_B2 * v + (1.0 - ADAM_B2) * (g * g)
    m_hat = m / (1.0 - ADAM_B1 ** ADAM_STEP)
    v_hat = v / (1.0 - ADAM_B2 ** ADAM_STEP)
    delta = -ADAM_LR * (m_hat / (jnp.sqrt(v_hat) + ADAM_EPS) + ADAM_WD * w)
    return delta, m, v


def _adam_shards(me_idx, blocks, lands, ws, ms, vs, n_blocks, name, after=()):
    n = len(blocks)

    def body(i_ref, *refs):
        ins, outs = refs[:5 * n], refs[5 * n + len(after):]
        for w in range(n):
            g_ref, b_ref, w_ref, m_ref, v_ref = (ins[t * n + w] for t in range(5))
            g = g_ref[...].astype(f32)
            for k in range(N_PEERS):
                g = g + b_ref[k].astype(f32)
            if len(w_ref.shape) == 2:
                pieces = [(slice(None), g)]
            else:
                pieces = [(a, g[2 * a:2 * a + 2]) for a in range(2)]
            for at, gp in pieces:
                vals = (gp,) + _adam_math(w_ref[at], gp, m_ref[at], v_ref[at])
                for t, val in enumerate(vals):
                    outs[4 * w + t][at] = val

    specs = [[] for _ in range(5)]
    out_specs, out_shape = [], []
    for g, wt in zip(blocks, ws):
        rows, cols = g.shape[1:]
        rb, cut = _row_block(rows, n_blocks)
        if not cut and wt.ndim == 2 and cols % (LANES * n_blocks) == 0:
            cb = cols // n_blocks
            specs[0].append(pl.BlockSpec((None, rows, cb), lambda i, s: (s[0], 0, i)))
            specs[1].append(pl.BlockSpec((N_PEERS, rows, cb), lambda i, s: (0, 0, i)))
            shard = pl.BlockSpec((rows, cb), lambda i, s: (0, i))
            for t in (2, 3, 4):
                specs[t].append(shard)
            out_specs += [shard] * 4
            out_shape += [jax.ShapeDtypeStruct(wt.shape, f32)] * 4
            continue
        specs[0].append(pl.BlockSpec((None, rb, cols), functools.partial(lambda i, s, cut: (s[0], i if cut else 0, 0), cut=cut)))
        specs[1].append(pl.BlockSpec((N_PEERS, rb, cols), functools.partial(lambda i, s, cut: (0, i if cut else 0, 0), cut=cut)))
        if wt.ndim == 2:
            shard = pl.BlockSpec((rb, cols), functools.partial(lambda i, s, cut: (i if cut else 0, 0), cut=cut))
        elif wt.shape[0] == 1:
            shard = pl.BlockSpec((None, rb, cols), functools.partial(lambda i, s, cut: (0, i if cut else 0, 0), cut=cut))
        else:
            shard = pl.BlockSpec(wt.shape, functools.partial(lambda i, s, nd: (0,) * nd, nd=wt.ndim))
        for t in (2, 3, 4):
            specs[t].append(shard)
        out_specs += [shard] * 4
        out_shape += [jax.ShapeDtypeStruct(wt.shape, f32)] * 4
    outs = pl.pallas_call(
        body, name=name,
        grid_spec=pltpu.PrefetchScalarGridSpec(
            num_scalar_prefetch=1, grid=(n_blocks,), in_specs=sum(specs, []) + [pl.BlockSpec(memory_space=pl.ANY)] * len(after),
            out_specs=out_specs),
        out_shape=out_shape,
        compiler_params=_cp(("arbitrary",), VMEM_LIMIT),
    )(me_idx, *blocks, *lands, *ws, *ms, *vs, *after)
    return [outs[4 * w:4 * w + 4] for w in range(n)]


def _adam_gains(parts, ws, ms, vs):
    n = len(ws)

    def body(p_ref, *refs):
        ins, outs = refs[:3 * n], refs[3 * n:]
        g_all = p_ref[0]
        for k in range(1, N_DEV):
            g_all = g_all + p_ref[k]
        for w, (off, lanes) in enumerate(SMALL.values()):
            w_ref, m_ref, v_ref = ins[w], ins[n + w], ins[2 * n + w]
            if len(w_ref.shape) == 2:
                pieces = [(slice(None), off, lanes)]
            else:
                pieces = [((slice(None), h), off + LANES * h, LANES) for h in range(w_ref.shape[1])]
            for at, o, ln in pieces:
                g = g_all[:, o:o + ln]
                vals = (g,) + _adam_math(w_ref[at], g, m_ref[at], v_ref[at])
                for t, val in enumerate(vals):
                    outs[4 * w + t][at] = val
        outs[4 * n][...] = g_all[:, LOSS_OFF:LOSS_OFF + LANES]

    out_shape = sum([[jax.ShapeDtypeStruct(w.shape, f32)] * 4 for w in ws], []) + [jax.ShapeDtypeStruct((1, LANES), f32)]
    outs = pl.pallas_call(body, name="adamw_gains", out_shape=out_shape)(parts, *ws, *ms, *vs)
    return [outs[4 * w:4 * w + 4] for w in range(n)], outs[4 * n]


def _fwd_in(x, g_mix, wz, tm):
    s, d = x.shape

    def body(x_ref, g_ref, w_ref, h_ref, z_ref):
        h, _ = _rms_fwd(x_ref[...], g_ref[...], d)
        hb = h.astype(bf16)
        h_ref[...] = hb
        z_ref[...] = _dot_nt(hb, w_ref[...])

    return pl.pallas_call(
        body, name="fwd_in", grid=(s // tm,),
        in_specs=[pl.BlockSpec((tm, d), lambda i: (i, 0)), _const_spec((1, d)), _const_spec((Z_W, d))],
        out_specs=[pl.BlockSpec((tm, d), lambda i: (i, 0)), pl.BlockSpec((tm, Z_W), lambda i: (i, 0))],
        out_shape=[jax.ShapeDtypeStruct((s, d), bf16), jax.ShapeDtypeStruct((s, Z_W), f32)],
        compiler_params=_cp(("parallel",), VMEM_LIMIT),
    )(x, g_mix, wz)


def _mla_qk_fwd(cq, ckv, g_qa, g_kva, wqb, wkvb):
    cqn, rq = _rms_fwd(cq, g_qa, Q_LORA)
    ckvn, rkv = _rms_fwd(ckv, g_kva, KV_LORA)
    cqn_b, ckvn_b = cqn.astype(bf16), ckvn.astype(bf16)
    q0 = _dot(cqn_b, wqb)
    kv0 = _dot(ckvn_b, wkvb)
    return cqn_b, rq, ckvn_b, rkv, q0, kv0


def _fwd_mla_proj(z, cosb, sina, sinb, g_qa, g_kva, wqb, wkvb, g_qn, g_kn, tm):
    s = z.shape[0]
    hh = MLA_HEADS

    def body(cq_ref, ckv_ref, kr_ref, c_ref, sa_ref, sb_ref, gqa_ref, gkva_ref, wqb_ref, wkvb_ref, gqn_ref, gkn_ref,
             q_ref, k_ref, v_ref):
        _, _, _, _, q0, kv0 = _mla_qk_fwd(cq_ref[...], ckv_ref[...], gqa_ref[...], gkva_ref[...], wqb_ref[...], wkvb_ref[...])
        kr = kr_ref[...]
        c, sa, sb = c_ref[...], sa_ref[...], sb_ref[...]
        gqn, gkn = gqn_ref[...], gkn_ref[...]
        kr_sq = jnp.sum(kr * kr, axis=-1, keepdims=True)
        for h in range(hh):
            qh = q0[:, QK_PAD * h:QK_PAD * (h + 1)]
            qn, _ = _rms_fwd(qh, gqn, QK_HEAD)
            q_ref[h, :, 0:128] = qn[:, 0:128].astype(bf16)
            q_ref[h, :, 128:256] = _rope(qn[:, 128:256], c, sa, sb).astype(bf16)
            kn_ = kv0[:, 256 * h:256 * h + 128]
            rk = lax.rsqrt((jnp.sum(kn_ * kn_, axis=-1, keepdims=True) + kr_sq) * (1.0 / QK_HEAD) + EPS)
            k_ref[h, :, 0:128] = (kn_ * rk * gkn[:, 0:128]).astype(bf16)
            k_ref[h, :, 128:256] = _rope(kr * rk * gkn[:, 128:256], c, sa, sb).astype(bf16)
            v_ref[h] = kv0[:, 256 * h + 128:256 * h + 256].astype(bf16)

    row128 = pl.BlockSpec((tm, 128), lambda i: (i, 0))
    return pl.pallas_call(
        body, name="fwd_mla_proj", grid=(s // tm,),
        in_specs=[pl.BlockSpec((tm, 256), lambda i: (i, Z_CQ // 256)), pl.BlockSpec((tm, 256), lambda i: (i, Z_CKV // 256)),
                  pl.BlockSpec((tm, 128), lambda i: (i, Z_KR // 128)), row128, row128, row128,
                  _const_spec((1, 256)), _const_spec((1, 256)), _const_spec((256, 1024)), _const_spec((256, 1024)),
                  _const_spec((1, 256)), _const_spec((1, 256))],
        out_specs=[pl.BlockSpec((hh, tm, QK_PAD), lambda i: (0, i, 0)), pl.BlockSpec((hh, tm, QK_PAD), lambda i: (0, i, 0)),
                   pl.BlockSpec((hh, tm, V_HEAD), lambda i: (0, i, 0))],
        out_shape=[jax.ShapeDtypeStruct((hh, s, QK_PAD), bf16), jax.ShapeDtypeStruct((hh, s, QK_PAD), bf16),
                   jax.ShapeDtypeStruct((hh, s, V_HEAD), bf16)],
        compiler_params=_cp(("parallel",), VMEM_LIMIT),
    )(z, z, z, cosb, sina, sinb, g_qa, g_kva, wqb, wkvb, g_qn, g_kn)


def _fwd_attn(q, k, v, tq, after):
    hh, s, _ = q.shape

    n_sub = max(1, tq // ATTN_SUB_ROWS)

    def body(q_ref, k_ref, v_ref, after_ref, o_ref, o32_ref):
        for t in range(n_sub):
            rows = slice(t * (tq // n_sub), (t + 1) * (tq // n_sub))
            sc = _dot_nt(q_ref[rows, :], k_ref[...])
            p = jnp.exp2((sc - jnp.max(sc, axis=-1, keepdims=True)) * (ATTN_SCALE * LOG2_E))
            l = jnp.sum(p, axis=-1, keepdims=True)
            o = _dot(p.astype(bf16), v_ref[...]) * (1.0 / l)
            o_ref[rows, :] = o.astype(bf16)
            o32_ref[rows, :] = o

    out = pl.BlockSpec((tq, V_HEAD), lambda h, i: (i, h))
    return pl.pallas_call(
        body, name="fwd_attn", grid=(hh, s // tq),
        in_specs=[pl.BlockSpec((None, tq, QK_PAD), lambda h, i: (h, i, 0)),
                  pl.BlockSpec((None, s, QK_PAD), lambda h, i: (h, 0, 0)),
                  pl.BlockSpec((None, s, V_HEAD), lambda h, i: (h, 0, 0)), pl.BlockSpec(memory_space=pl.ANY)],
        out_specs=[out, out],
        out_shape=[jax.ShapeDtypeStruct((s, hh * V_HEAD), bf16), jax.ShapeDtypeStruct((s, hh * V_HEAD), f32)],
        compiler_params=_cp(("parallel", "parallel"), VMEM_LIMIT),
    )(q, k, v, after)


def _split3(x):
    hi = x.astype(bf16)
    r1 = x - hi.astype(f32)
    mid = r1.astype(bf16)
    lo = (r1 - mid.astype(f32)).astype(bf16)
    return jnp.concatenate([hi, mid, lo], axis=-1)


def _tri_sum(tri, x):
    y = _dot(tri, _split3(x))
    return y[:, 0:128] + y[:, 128:256] + y[:, 256:384]


GLA_GROUP = 4
GLA_ROWS = GLA_GROUP * CHUNK
GLA_HEADS_PER_STEP = 2


def _gla_masks(rev):
    row = lax.broadcasted_iota(jnp.int32, (GLA_ROWS, GLA_ROWS), 0)
    col = lax.broadcasted_iota(jnp.int32, (GLA_ROWS, GLA_ROWS), 1)
    shift = CHUNK.bit_length() - 1
    same = (jnp.right_shift(row, shift) == jnp.right_shift(col, shift)).astype(f32)
    lower, upper = (row >= col).astype(f32) * same, (row <= col).astype(f32) * same
    keep, keep_t = (upper, lower) if rev else (lower, upper)
    chunk_of = jnp.right_shift(lax.broadcasted_iota(jnp.int32, (GLA_ROWS, 1), 0), shift)
    return keep, keep.astype(bf16), keep_t.astype(bf16), [(chunk_of == c).astype(f32) for c in range(GLA_GROUP)]


def _gla_gates(hq, hf, lower):
    sg = _sigmoid(hf)
    f = lower + (1.0 - lower) * sg
    return hq * _sigmoid(hq), 1.0 - f, jnp.log(f), f, sg


def _gla_last_mid(b, rev):
    b3 = b.reshape(GLA_GROUP, CHUNK, 128)
    last, mid = (0, CHUNK // 2) if rev else (CHUNK - 1, CHUNK // 2 - 1)
    return b3[:, last:last + 1, :], b3[:, mid:mid + 1, :]


def _gla_per_row(per_chunk):
    return jnp.broadcast_to(per_chunk, (GLA_GROUP, CHUNK, 128)).reshape(GLA_ROWS, 128)


def _gla_block_diag(x, row_masks):
    return jnp.concatenate([(x * m).astype(bf16) for m in row_masks], axis=-1)


def _gla_diag(y):
    return jnp.concatenate([y[CHUNK * c:CHUNK * (c + 1), 128 * c:128 * (c + 1)] for c in range(GLA_GROUP)], axis=0)


def _gla_rows(n, n_groups, rev):
    ne = n_groups - 1 - n if rev else n
    return pl.ds(pl.multiple_of(ne * GLA_ROWS, GLA_ROWS), GLA_ROWS), ne * GLA_GROUP


def _gla_scan_order(rev):
    return tuple(reversed(range(GLA_GROUP))) if rev else tuple(range(GLA_GROUP))


def _fwd_gla(z, lb4):
    s = z.shape[0]
    n_groups = s // GLA_ROWS
    assert n_groups % 2 == 0
    hp = GLA_HEADS_PER_STEP
    chains = [(hh, rev) for hh in range(hp) for rev in (False, True)]

    def body(hq_ref, hff_ref, hfb_ref, hi_ref, lb_ref, o_ref, b_ref, states_ref, st_ref, stage_ref, b_stage, sems):
        st_ref[...] = jnp.zeros_like(st_ref)
        masks = {rev: _gla_masks(rev) for rev in (False, True)}
        lowers = [_sigmoid(lb_ref[int(rev):int(rev) + 1, 128 * hh:128 * (hh + 1)]
                           - lb_ref[2 + int(rev):3 + int(rev), 128 * hh:128 * (hh + 1)]) for hh, rev in chains]

        def states_out(slot, ci, chunk0):
            hh, rev = chains[ci]
            head = pl.program_id(0) * hp + hh
            rows = pl.ds(pl.multiple_of(chunk0 * CHUNK, GLA_ROWS), GLA_ROWS)
            return _Both(
                pltpu.make_async_copy(stage_ref.at[slot, ci], states_ref.at[head, int(rev), pl.ds(chunk0, GLA_GROUP)],
                                      sems.at[slot, ci]),
                pltpu.make_async_copy(b_stage.at[slot, ci], b_ref.at[int(rev), rows, pl.ds(pl.multiple_of(head * 128, 128), 128)],
                                      sems.at[slot, len(chains) + ci]))

        def make_step(first):
            def step(n, carry):
                slot = n % 2

                @pl.when(n >= 2)
                def _():
                    for ci in range(len(chains)):
                        states_out(slot, ci, 0).wait()

                for ci, (hh, rev) in enumerate(chains):
                    cols = slice(128 * hh, 128 * (hh + 1))
                    rows, chunk0 = _gla_rows(n, n_groups, rev)
                    maskf, tri, _, row_masks = masks[rev]
                    hf_ref = hfb_ref if rev else hff_ref
                    q, k, logf, _, _ = _gla_gates(hq_ref[rows, cols], hf_ref[rows, cols], lowers[ci])
                    vb = hi_ref[rows, cols].astype(bf16)
                    b = _tri_sum(tri, logf)
                    b_stage[slot, ci] = b
                    b_last3, b_mid3 = _gla_last_mid(b, rev)
                    b_last, b_mid = _gla_per_row(b_last3), _gla_per_row(b_mid3)
                    qi = (q * jnp.exp(b - b_mid)).astype(bf16)
                    ki = (k * jnp.exp(b_mid - b)).astype(bf16)
                    a = (_dot_nt(qi, ki) * maskf).astype(bf16)
                    kv = _dot_tn(vb, _gla_block_diag(k * jnp.exp(b_last - b), row_masks))
                    decay3 = jnp.exp(b_last3)
                    st = st_ref[ci]
                    before = [None] * GLA_GROUP
                    for c in _gla_scan_order(rev):
                        stage_ref[slot, ci, c] = st
                        before[c] = st.astype(bf16)
                        st = st * decay3[c] + kv[:, 128 * c:128 * (c + 1)]
                    st_ref[ci] = st
                    states_out(slot, ci, chunk0).start()
                    inter = _dot_nt((q * jnp.exp(b)).astype(bf16), jnp.concatenate(before, axis=0))
                    o = _dot(a, vb) + _gla_diag(inter)
                    if first:
                        o_ref[rows, cols] = o
                    else:
                        o_ref[rows, cols] += o
                return carry
            return step

        lax.fori_loop(0, n_groups // 2, make_step(True), 0)
        lax.fori_loop(n_groups // 2, n_groups, make_step(False), 0)
        for slot in range(2):
            for ci in range(len(chains)):
                states_out(slot, ci, 0).wait()

    w = 128 * hp
    col = lambda base: pl.BlockSpec((s, w), lambda h: (0, base // w + h))
    return pl.pallas_call(
        body, name="fwd_gla", grid=(HG_HEADS // hp,),
        in_specs=[col(Z_HQ), col(Z_HFF), col(Z_HFB), col(Z_HI), pl.BlockSpec((4, w), lambda h: (0, h))],
        out_specs=[pl.BlockSpec((s, w), lambda h: (0, h)), pl.BlockSpec(memory_space=pl.ANY), pl.BlockSpec(memory_space=pl.ANY)],
        out_shape=[jax.ShapeDtypeStruct((s, HG_HEADS * 128), f32), jax.ShapeDtypeStruct((2, s, HG_HEADS * 128), f32),
                   jax.ShapeDtypeStruct((HG_HEADS, 2, s // CHUNK, 128, 128), f32)],
        scratch_shapes=[pltpu.VMEM((len(chains), 128, 128), f32), pltpu.VMEM((2, len(chains), GLA_GROUP, 128, 128), f32),
                        pltpu.VMEM((2, len(chains), GLA_ROWS, 128), f32), pltpu.SemaphoreType.DMA((2, 2 * len(chains)))],
        compiler_params=_cp(("parallel",), VMEM_LIMIT),
    )(z, z, z, z, lb4)


def _hg_out(o, hg, g_hgo):
    outs, ons, rs = [], [], []
    for h in range(HG_HEADS):
        oh = o[:, 128 * h:128 * (h + 1)]
        on, r = _rms_fwd(oh, g_hgo[:, 128 * h:128 * (h + 1)], 128)
        ons.append(on)
        rs.append(r)
    on = jnp.concatenate(ons, axis=-1)
    sg = _sigmoid(hg)
    return on * (hg * sg), on, rs, sg


def _fwd_mix(a, o, z, g_hgo, x, w_o, tm):
    s, d = x.shape

    def body(a_ref, o_ref, hg_ref, g_ref, x_ref, w_ref, x2_ref, cat_ref):
        r, _, _, _ = _hg_out(o_ref[...], hg_ref[...], g_ref[...])
        cat = jnp.concatenate([a_ref[...], r.astype(bf16)], axis=-1)
        cat_ref[...] = cat
        x2_ref[...] = x_ref[...] + _dot(cat, w_ref[...])

    row512 = pl.BlockSpec((tm, 512), lambda i: (i, 0))
    rowd = pl.BlockSpec((tm, d), lambda i: (i, 0))
    return pl.pallas_call(
        body, name="fwd_mix", grid=(s // tm,),
        in_specs=[row512, row512, pl.BlockSpec((tm, 512), lambda i: (i, Z_HG // 512)), _const_spec((1, 512)), rowd,
                  _const_spec((d, d))],
        out_specs=[rowd, rowd],
        out_shape=[jax.ShapeDtypeStruct((s, d), f32), jax.ShapeDtypeStruct((s, d), bf16)],
        compiler_params=_cp(("parallel",), VMEM_LIMIT),
    )(a, o, z, g_hgo, x, w_o)


def _fwd_ffn(x2, g_ffn, w_gate, w_up, w_down, tm):
    s, d = x2.shape

    def body(x_ref, g_ref, wg_ref, wu_ref, wd_ref, x3_ref, gp_ref, up_ref):
        x = x_ref[...]
        h, _ = _rms_fwd(x, g_ref[...], d)
        hb = h.astype(bf16)
        gp = _dot_nt(hb, wg_ref[...])
        up = _dot_nt(hb, wu_ref[...])
        gp_ref[...] = gp.astype(bf16)
        up_ref[...] = up.astype(bf16)
        act = (gp * _sigmoid(gp) * up).astype(bf16)
        x3_ref[...] = x + _dot(act, wd_ref[...])

    rowd = pl.BlockSpec((tm, d), lambda i: (i, 0))
    rowf = pl.BlockSpec((tm, D_FF), lambda i: (i, 0))
    return pl.pallas_call(
        body, name="fwd_ffn", grid=(s // tm,),
        in_specs=[rowd, _const_spec((1, d)), _const_spec((D_FF, d)), _const_spec((D_FF, d)), _const_spec((D_FF, d))],
        out_specs=[rowd, rowf, rowf],
        out_shape=[jax.ShapeDtypeStruct((s, d), f32), jax.ShapeDtypeStruct((s, D_FF), bf16),
                   jax.ShapeDtypeStruct((s, D_FF), bf16)],
        compiler_params=_cp(("parallel",), VMEM_LIMIT),
    )(x2, g_ffn, w_gate, w_up, w_down)


def _ple_loss_fwd_bwd(x3, g_ple, w_pg, p, w_pp, target, tm):
    s, d = x3.shape
    cols = BIG["w_ple_proj"][1]

    def body(x_ref, g_ref, wg_ref, p_ref, wp_ref, t_ref, dx_ref, gwg_ref, gwp_ref, dg_ref, loss_ref, acc_ref, accp_ref):
        @pl.when(pl.program_id(0) == 0)
        def _():
            for r_ in (acc_ref, accp_ref, dg_ref, loss_ref):
                r_[...] = jnp.zeros_like(r_)

        x = x_ref[...]
        g = g_ref[...]
        h, r = _rms_fwd(x, g, d)
        hb = h.astype(bf16)
        pb = p_ref[...].astype(bf16)
        gate = _sigmoid(_dot(hb, wg_ref[...]))
        pp = jnp.concatenate([_dot(pb, wp_ref[j]) for j in range(N_DEV)], axis=-1)
        e = x + gate * pp - t_ref[...]
        loss_ref[...] += 0.5 * jnp.sum(e * e) * (1.0 / d)
        dy = e * (1.0 / d)
        dpre = (dy * pp * gate * (1.0 - gate)).astype(bf16)
        dx, dgx = _rms_bwd(_dot_nt(dpre, wg_ref[...]), x, r, g, d)
        dx_ref[...] = dy + dx
        dg_ref[...] += jnp.sum(dgx, axis=0, keepdims=True)
        acc_ref[...] += _dot_tn(hb, dpre)
        accp_ref[...] += _dot_tn(pb, (dy * gate).astype(bf16))

        @pl.when(pl.program_id(0) == s // tm - 1)
        def _():
            gwg_ref[...] = acc_ref[...].astype(bf16)
            for j in range(N_DEV):
                gwp_ref[j] = accp_ref[:, cols * j:cols * (j + 1)].astype(bf16)

    rowd = pl.BlockSpec((tm, d), lambda i: (i, 0))
    return pl.pallas_call(
        body, name="ple_loss_fwd_bwd", grid=(s // tm,),
        in_specs=[rowd, _const_spec((1, d)), _const_spec((d, d)), pl.BlockSpec((tm, PLE_DIM), lambda i: (i, 0)),
                  _const_spec((N_DEV, PLE_DIM, cols)), rowd],
        out_specs=[rowd, _acc_spec((d, d)), _acc_spec((N_DEV, PLE_DIM, cols)), _acc_spec((1, d)), _acc_spec((8, 128))],
        out_shape=[jax.ShapeDtypeStruct((s, d), f32), jax.ShapeDtypeStruct((d, d), bf16),
                   jax.ShapeDtypeStruct((N_DEV, PLE_DIM, cols), bf16), jax.ShapeDtypeStruct((1, d), f32), jax.ShapeDtypeStruct((8, 128), f32)],
        scratch_shapes=[pltpu.VMEM((d, d), f32), pltpu.VMEM((PLE_DIM, d), f32)],
        compiler_params=_cp(("arbitrary",), VMEM_LIMIT),
    )(x3, g_ple, w_pg, p, w_pp, target)


def _bwd_ffn_hidden(d3, gp, up, w_down, tm, tf):
    s, d = d3.shape

    def body(d3_ref, gp_ref, up_ref, wd_ref, act_ref, dgp_ref, dup_ref):
        gp, up = gp_ref[...].astype(f32), up_ref[...].astype(f32)
        sg = _sigmoid(gp)
        silu = gp * sg
        act_ref[...] = (silu * up).astype(bf16)
        dact = _dot_nt(d3_ref[...].astype(bf16), wd_ref[...])
        dgp_ref[...] = (dact * up * (sg * (1.0 + gp * (1.0 - sg)))).astype(bf16)
        dup_ref[...] = (dact * silu).astype(bf16)

    rowf = pl.BlockSpec((tm, tf), lambda f, i: (i, f))
    return pl.pallas_call(
        body, name="bwd_ffn_hidden", grid=(D_FF // tf, s // tm),
        in_specs=[pl.BlockSpec((tm, d), lambda f, i: (i, 0)), rowf, rowf, pl.BlockSpec((tf, d), lambda f, i: (f, 0))],
        out_specs=[rowf, rowf, rowf],
        out_shape=[jax.ShapeDtypeStruct((s, D_FF), bf16)] * 3,
        compiler_params=_cp(("parallel", "parallel"), VMEM_LIMIT),
    )(d3, gp, up, w_down)


def _bwd_ffn_in(d3, x2, dgp, dup, g_ffn, w_gate, w_up, tm):
    s, d = x2.shape

    def body(d3_ref, x_ref, dgp_ref, dup_ref, g_ref, wg_ref, wu_ref, d2_ref, h_ref, dg_ref):
        @pl.when(pl.program_id(0) == 0)
        def _():
            dg_ref[...] = jnp.zeros_like(dg_ref)

        x, g = x_ref[...], g_ref[...]
        dh = _dot(dgp_ref[...], wg_ref[...]) + _dot(dup_ref[...], wu_ref[...])
        h, r = _rms_fwd(x, g, d)
        h_ref[...] = h.astype(bf16)
        dx, dgx = _rms_bwd(dh, x, r, g, d)
        d2_ref[...] = d3_ref[...] + dx
        dg_ref[...] += jnp.sum(dgx, axis=0, keepdims=True)

    rowd = pl.BlockSpec((tm, d), lambda i: (i, 0))
    rowf = pl.BlockSpec((tm, D_FF), lambda i: (i, 0))
    return pl.pallas_call(
        body, name="bwd_ffn_in", grid=(s // tm,),
        in_specs=[rowd, rowd, rowf, rowf, _const_spec((1, d)), _const_spec((D_FF, d)), _const_spec((D_FF, d))],
        out_specs=[rowd, rowd, _acc_spec((1, d))],
        out_shape=[jax.ShapeDtypeStruct((s, d), f32), jax.ShapeDtypeStruct((s, d), bf16), jax.ShapeDtypeStruct((1, d), f32)],
        compiler_params=_cp(("arbitrary",), VMEM_LIMIT),
    )(d3, x2, dgp, dup, g_ffn, w_gate, w_up)


def _bwd_mix(d2, w_o, o, z, g_hgo, cat, tm):
    s, d = d2.shape

    def body(d2_ref, w_ref, o_ref, hg_ref, g_ref, cat_ref, da_ref, do_ref, dhg_ref, dg_ref, gw_ref, acc_ref):
        @pl.when(pl.program_id(0) == 0)
        def _():
            dg_ref[...] = jnp.zeros_like(dg_ref)
            acc_ref[...] = jnp.zeros_like(acc_ref)

        d2b = d2_ref[...].astype(bf16)
        acc_ref[...] += _dot_tn(cat_ref[...], d2b)

        @pl.when(pl.program_id(0) == s // tm - 1)
        def _():
            gw_ref[...] = acc_ref[...].astype(bf16)
        dcat = _dot_nt(d2b, w_ref[...])
        da_ref[...] = dcat[:, 0:512].astype(bf16)
        dr = dcat[:, 512:1024]
        o, hg, g = o_ref[...], hg_ref[...], g_ref[...]
        _, on, rs, sg = _hg_out(o, hg, g)
        dhg_ref[...] = (dr * on * (sg * (1.0 + hg * (1.0 - sg)))).astype(bf16)
        don = dr * (hg * sg)
        dgs = []
        for h in range(HG_HEADS):
            cols = slice(128 * h, 128 * (h + 1))
            dx, dgx = _rms_bwd(don[:, cols], o[:, cols], rs[h], g[:, cols], 128)
            do_ref[:, cols] = dx
            dgs.append(jnp.sum(dgx, axis=0, keepdims=True))
        dg_ref[...] += jnp.concatenate(dgs, axis=-1)

    row512 = pl.BlockSpec((tm, 512), lambda i: (i, 0))
    return pl.pallas_call(
        body, name="bwd_mix", grid=(s // tm,),
        in_specs=[pl.BlockSpec((tm, d), lambda i: (i, 0)), _const_spec((d, d)), row512,
                  pl.BlockSpec((tm, 512), lambda i: (i, Z_HG // 512)), _const_spec((1, 512)), pl.BlockSpec((tm, d), lambda i: (i, 0))],
        out_specs=[row512, row512, row512, _acc_spec((1, 512)), _acc_spec((d, d))],
        out_shape=[jax.ShapeDtypeStruct((s, 512), bf16), jax.ShapeDtypeStruct((s, 512), f32), jax.ShapeDtypeStruct((s, 512), bf16),
                   jax.ShapeDtypeStruct((1, 512), f32), jax.ShapeDtypeStruct((d, d), bf16)],
        scratch_shapes=[pltpu.VMEM((d, d), f32)],
        compiler_params=_cp(("arbitrary",), VMEM_LIMIT),
    )(d2, w_o, o, z, g_hgo, cat)


def _bwd_gla(z, lb4, do, b_fwd, states):
    s = z.shape[0]
    n_chunks = s // CHUNK
    n_groups = s // GLA_ROWS
    assert n_groups % 2 == 0

    def body(hq_ref, hff_ref, hfb_ref, hi_ref, lb_ref, do_ref, b_all, st_all, dhq_ref, dhff_ref, dhfb_ref, dhi_ref, dlb_ref,
             dst_ref, dq_acc, dv_acc, dlow_ref):
        dirs = (False, True)
        masks = [_gla_masks(rev) for rev in dirs]
        lowers = [_sigmoid(lb_ref[int(rev):int(rev) + 1, :] - lb_ref[2 + int(rev):3 + int(rev), :]) for rev in dirs]
        hf_refs, dhf_refs = (hff_ref, hfb_ref), (dhff_ref, dhfb_ref)

        dst_ref[...] = jnp.zeros_like(dst_ref)
        dlow_ref[...] = jnp.zeros_like(dlow_ref)

        def make_bwd_step(first):
            def bwd_step(j, carry):
                n = n_groups - 1 - j
                for d, rev in enumerate(dirs):
                    maskf, _, tri_t, row_masks = masks[d]
                    lower = lowers[d]
                    rows, chunk0 = _gla_rows(n, n_groups, rev)
                    hq, hf = hq_ref[rows, :], hf_refs[d][rows, :]
                    q, k, _, f, sg = _gla_gates(hq, hf, lower)
                    v = hi_ref[rows, :]
                    dout = do_ref[rows, :]
                    b = b_all[d, rows, :]
                    b_last3, b_mid3 = _gla_last_mid(b, rev)
                    b_last, b_mid = _gla_per_row(b_last3), _gla_per_row(b_mid3)
                    e1, e2, e3, e4 = jnp.exp(b - b_mid), jnp.exp(b_mid - b), jnp.exp(b_last - b), jnp.exp(b)
                    decay3 = jnp.exp(b_last3)
                    qi, ki, kt, qt = q * e1, k * e2, k * e3, q * e4
                    qib, kib, ktb = qi.astype(bf16), ki.astype(bf16), kt.astype(bf16)
                    vb, dob = v.astype(bf16), dout.astype(bf16)
                    a = (_dot_nt(qib, kib) * maskf).astype(bf16)
                    da = (_dot_nt(dob, vb) * maskf).astype(bf16)
                    dqi = _dot(da, kib)
                    dki = _dot_tn(da, qib)
                    into_state = _dot_tn(dob, _gla_block_diag(qt, row_masks))
                    dst = dst_ref[d]
                    sts, dsts, ddecay = [None] * GLA_GROUP, [None] * GLA_GROUP, [None] * GLA_GROUP
                    for c in reversed(_gla_scan_order(rev)):
                        sts[c] = st_all[d, chunk0 + c]
                        dsts[c] = dst.astype(bf16)
                        ddecay[c] = jnp.sum(dst * sts[c], axis=0, keepdims=True)[None]
                        dst = dst * decay3[c] + into_state[:, 128 * c:128 * (c + 1)]
                    dst_ref[d] = dst
                    dv = _dot_tn(a, dob) + _gla_diag(_dot_nt(ktb, jnp.concatenate(dsts, axis=0)))
                    dqt = _gla_diag(_dot(dob, jnp.concatenate([x.astype(bf16) for x in sts], axis=-1)))
                    dkt = _gla_diag(_dot(vb, jnp.concatenate(dsts, axis=-1)))
                    dq = dqi * e1 + dqt * e4
                    dk = dki * e2 + dkt * e3
                    db = dqi * qi - dki * ki + dqt * qt - dkt * kt
                    dlast3 = (jnp.sum((dkt * kt).reshape(GLA_GROUP, CHUNK, 128), axis=1, keepdims=True)
                              + jnp.concatenate(ddecay, axis=0) * decay3)
                    dlogf = _tri_sum(tri_t, db) + _gla_per_row(dlast3)
                    df = dlogf / f - dk
                    dhf_refs[d][rows, :] = (df * (1.0 - lower) * sg * (1.0 - sg)).astype(bf16)
                    dlow_ref[d:d + 1, :] += jnp.sum(df * (1.0 - sg), axis=0, keepdims=True)
                    sq = _sigmoid(hq)
                    dhq = dq * (sq * (1.0 + hq * (1.0 - sq)))
                    if first:
                        dq_acc[rows, :] = dhq
                        dv_acc[rows, :] = dv
                    else:
                        dhq_ref[rows, :] = (dq_acc[rows, :] + dhq).astype(bf16)
                        dhi_ref[rows, :] = (dv_acc[rows, :] + dv).astype(bf16)
                return carry
            return bwd_step

        lax.fori_loop(0, n_groups // 2, make_bwd_step(True), 0, unroll=2)
        lax.fori_loop(n_groups // 2, n_groups, make_bwd_step(False), 0, unroll=2)

        for d in range(2):
            dl = dlow_ref[d:d + 1, :] * lowers[d] * (1.0 - lowers[d])
            dlb_ref[d:d + 1, :] = dl
            dlb_ref[2 + d:3 + d, :] = -dl

    col = lambda base: pl.BlockSpec((s, 128), lambda h: (0, base // 128 + h))
    return pl.pallas_call(
        body, name="bwd_gla", grid=(HG_HEADS,),
        in_specs=[col(Z_HQ), col(Z_HFF), col(Z_HFB), col(Z_HI), pl.BlockSpec((4, 128), lambda h: (0, h)), col(0),
                  pl.BlockSpec((2, s, 128), lambda h: (0, 0, h)),
                  pl.BlockSpec((None, 2, n_chunks, 128, 128), lambda h: (h, 0, 0, 0, 0), pipeline_mode=pl.Buffered(1))],
        out_specs=[col(0), col(0), col(0), col(0), pl.BlockSpec((4, 128), lambda h: (0, h))],
        out_shape=[jax.ShapeDtypeStruct((s, 512), bf16)] * 4 + [jax.ShapeDtypeStruct((4, 512), f32)],
        scratch_shapes=[pltpu.VMEM((2, 128, 128), f32), pltpu.VMEM((s, 128), f32), pltpu.VMEM((s, 128), f32),
                        pltpu.VMEM((2, 128), f32)],
        compiler_params=_cp(("parallel",), VMEM_LIMIT),
    )(z, z, z, z, lb4, do, b_fwd, states)


def _bwd_attn(q, k, v, da, a32, tq, after):
    hh, s, _ = q.shape

    n_sub = max(1, tq // ATTN_SUB_ROWS)

    def body(q_ref, k_ref, v_ref, do_ref, o_ref, after_ref, dq_ref, dk_ref, dv_ref, p_all, ds_all, dol_ref, dkt_ref, dvt_ref):
        @pl.when(pl.program_id(1) == 0)
        def _():
            dkt_ref[...] = jnp.zeros_like(dkt_ref)
            dvt_ref[...] = jnp.zeros_like(dvt_ref)

        kb, vb = k_ref[...], v_ref[...]
        for t in range(n_sub):
            rows = slice(t * (tq // n_sub), (t + 1) * (tq // n_sub))
            sc = _dot_nt(q_ref[rows, :], kb)
            p = jnp.exp2((sc - jnp.max(sc, axis=-1, keepdims=True)) * (ATTN_SCALE * LOG2_E))
            inv_l = 1.0 / jnp.sum(p, axis=-1, keepdims=True)
            p_all[rows, :] = p.astype(bf16)
            dob = do_ref[rows, :]
            dof = dob.astype(f32)
            delta = jnp.sum(dof * o_ref[rows, :], axis=-1, keepdims=True)
            ds_all[rows, :] = p_all[rows, :] * ((_dot_nt(dob, vb) - delta) * inv_l).astype(bf16)
            dq_ref[rows, :] = _dot(ds_all[rows, :], kb) * ATTN_SCALE
            dol_ref[rows, :] = (dof * inv_l).astype(bf16)
        dkt_ref[...] += _dot_tn(q_ref[...], ds_all[...])
        dvt_ref[...] += _dot_tn(dol_ref[...], p_all[...])

        @pl.when(pl.program_id(1) == s // tq - 1)
        def _():
            dk_ref[...] = dkt_ref[...].T * ATTN_SCALE
            dv_ref[...] = dvt_ref[...].T

    return pl.pallas_call(
        body, name="bwd_attn", grid=(hh, s // tq),
        in_specs=[pl.BlockSpec((None, tq, QK_PAD), lambda h, i: (h, i, 0)),
                  pl.BlockSpec((None, s, QK_PAD), lambda h, i: (h, 0, 0)),
                  pl.BlockSpec((None, s, V_HEAD), lambda h, i: (h, 0, 0)),
                  pl.BlockSpec((tq, V_HEAD), lambda h, i: (i, h)), pl.BlockSpec((tq, V_HEAD), lambda h, i: (i, h)),
                  pl.BlockSpec(memory_space=pl.ANY)],
        out_specs=[pl.BlockSpec((None, tq, QK_PAD), lambda h, i: (h, i, 0)),
                   pl.BlockSpec((None, s, QK_PAD), lambda h, i: (h, 0, 0)),
                   pl.BlockSpec((None, s, V_HEAD), lambda h, i: (h, 0, 0))],
        out_shape=[jax.ShapeDtypeStruct((hh, s, QK_PAD), f32), jax.ShapeDtypeStruct((hh, s, QK_PAD), f32),
                   jax.ShapeDtypeStruct((hh, s, V_HEAD), f32)],
        scratch_shapes=[pltpu.VMEM((tq, s), bf16), pltpu.VMEM((tq, s), bf16), pltpu.VMEM((tq, V_HEAD), bf16),
                        pltpu.VMEM((QK_PAD, s), f32), pltpu.VMEM((V_HEAD, s), f32)],
        compiler_params=_cp(("parallel", "arbitrary"), VMEM_LIMIT),
    )(q, k, v, da, a32, after)


def _bwd_mla_proj(z, dq, dk, dv, cosb, sina, sinb, g_qa, g_kva, wqb, wkvb, g_qn, g_kn, tm):
    s = z.shape[0]
    hh = MLA_HEADS

    def body(cq_ref, ckv_ref, kr_ref, dq_ref, dk_ref, dv_ref, c_ref, sa_ref, sb_ref, gqa_ref, gkva_ref, wqb_ref, wkvb_ref,
             gqn_ref, gkn_ref, dz_ref, gwqb_ref, gwkvb_ref, dgqa_ref, dgkva_ref, dgqn_ref, dgkn_ref, dq0_ref, dkv0_ref):
        @pl.when(pl.program_id(0) == 0)
        def _():
            for r in (gwqb_ref, gwkvb_ref, dgqa_ref, dgkva_ref, dgqn_ref, dgkn_ref):
                r[...] = jnp.zeros_like(r)

        cq, ckv, kr = cq_ref[...], ckv_ref[...], kr_ref[...]
        gqa, gkva, gqn, gkn = gqa_ref[...], gkva_ref[...], gqn_ref[...], gkn_ref[...]
        cqn_b, rq, ckvn_b, rkv, q0, kv0 = _mla_qk_fwd(cq, ckv, gqa, gkva, wqb_ref[...], wkvb_ref[...])
        c, sa, sb = c_ref[...], -sa_ref[...], -sb_ref[...]
        kr_sq = jnp.sum(kr * kr, axis=-1, keepdims=True)
        dkr = jnp.zeros_like(kr)
        dgqn = jnp.zeros((1, QK_PAD), f32)
        dgkn = jnp.zeros((1, QK_PAD), f32)
        for h in range(hh):
            qh = q0[:, QK_PAD * h:QK_PAD * (h + 1)]
            rh = lax.rsqrt(jnp.sum(qh * qh, axis=-1, keepdims=True) * (1.0 / QK_HEAD) + EPS)
            dqh = dq_ref[h]
            dqn = jnp.concatenate([dqh[:, 0:128], _rope(dqh[:, 128:256], c, sa, sb)], axis=-1)
            dq0h, dgx = _rms_bwd(dqn, qh, rh, gqn, QK_HEAD)
            dq0_ref[:, QK_PAD * h:QK_PAD * (h + 1)] = dq0h.astype(bf16)
            dgqn = dgqn + jnp.sum(dgx, axis=0, keepdims=True)

            kn_ = kv0[:, 256 * h:256 * h + 128]
            k0 = jnp.concatenate([kn_, kr], axis=-1)
            rk = lax.rsqrt((jnp.sum(kn_ * kn_, axis=-1, keepdims=True) + kr_sq) * (1.0 / QK_HEAD) + EPS)
            dkh = dk_ref[h]
            dkn = jnp.concatenate([dkh[:, 0:128], _rope(dkh[:, 128:256], c, sa, sb)], axis=-1)
            dk0, dgx = _rms_bwd(dkn, k0, rk, gkn, QK_HEAD)
            dgkn = dgkn + jnp.sum(dgx, axis=0, keepdims=True)
            dkv0_ref[:, 256 * h:256 * h + 128] = dk0[:, 0:128].astype(bf16)
            dkv0_ref[:, 256 * h + 128:256 * h + 256] = dv_ref[h].astype(bf16)
            dkr = dkr + dk0[:, 128:256]
        dgqn_ref[...] += dgqn
        dgkn_ref[...] += dgkn
        gwqb_ref[...] += _dot_tn(cqn_b, dq0_ref[...])
        gwkvb_ref[...] += _dot_tn(ckvn_b, dkv0_ref[...])
        dcq, dgx = _rms_bwd(_dot_nt(dq0_ref[...], wqb_ref[...]), cq, rq, gqa, Q_LORA)
        dgqa_ref[...] += jnp.sum(dgx, axis=0, keepdims=True)
        dckv, dgx = _rms_bwd(_dot_nt(dkv0_ref[...], wkvb_ref[...]), ckv, rkv, gkva, KV_LORA)
        dgkva_ref[...] += jnp.sum(dgx, axis=0, keepdims=True)
        dz_ref[:, 0:256] = dcq.astype(bf16)
        dz_ref[:, 256:512] = dckv.astype(bf16)
        dz_ref[:, 512:640] = dkr.astype(bf16)

    row128 = pl.BlockSpec((tm, 128), lambda i: (i, 0))
    hd = lambda w: pl.BlockSpec((hh, tm, w), lambda i: (0, i, 0))
    return pl.pallas_call(
        body, name="bwd_mla_proj", grid=(s // tm,),
        in_specs=[pl.BlockSpec((tm, 256), lambda i: (i, Z_CQ // 256)), pl.BlockSpec((tm, 256), lambda i: (i, Z_CKV // 256)),
                  pl.BlockSpec((tm, 128), lambda i: (i, Z_KR // 128)), hd(QK_PAD), hd(QK_PAD), hd(V_HEAD),
                  row128, row128, row128,
                  _const_spec((1, 256)), _const_spec((1, 256)), _const_spec((256, 1024)), _const_spec((256, 1024)),
                  _const_spec((1, 256)), _const_spec((1, 256))],
        out_specs=[pl.BlockSpec((tm, 640), lambda i: (i, 0)), _acc_spec((256, 1024)), _acc_spec((256, 1024)),
                   _acc_spec((1, 256)), _acc_spec((1, 256)), _acc_spec((1, 256)), _acc_spec((1, 256))],
        out_shape=[jax.ShapeDtypeStruct((s, 640), bf16), jax.ShapeDtypeStruct((256, 1024), f32), jax.ShapeDtypeStruct((256, 1024), f32)]
        + [jax.ShapeDtypeStruct((1, 256), f32)] * 4,
        scratch_shapes=[pltpu.VMEM((tm, 1024), bf16), pltpu.VMEM((tm, 1024), bf16)],
        compiler_params=_cp(("arbitrary",), VMEM_LIMIT),
    )(z, z, z, dq, dk, dv, cosb, sina, sinb, g_qa, g_kva, wqb, wkvb, g_qn, g_kn)


def _bwd_in(segments, wz, x, g_mix, d2, tm):
    s, d = x.shape
    n_seg = len(segments)

    def body(*refs):
        dz_refs, w_refs = refs[:n_seg], refs[n_seg:2 * n_seg]
        x_ref, g_ref, d2_ref, gx_ref, dg_ref = refs[2 * n_seg:]

        @pl.when(pl.program_id(0) == 0)
        def _():
            dg_ref[...] = jnp.zeros_like(dg_ref)

        dh = _dot(dz_refs[0][...], w_refs[0][...])
        for a_ref, w_ref in zip(dz_refs[1:], w_refs[1:]):
            dh = dh + _dot(a_ref[...], w_ref[...])
        x, g = x_ref[...], g_ref[...]
        r = lax.rsqrt(jnp.sum(x * x, axis=-1, keepdims=True) * (1.0 / d) + EPS)
        dx, dgx = _rms_bwd(dh, x, r, g, d)
        gx_ref[...] = d2_ref[...] + dx
        dg_ref[...] += jnp.sum(dgx, axis=0, keepdims=True)

    rowd = pl.BlockSpec((tm, d), lambda i: (i, 0))
    dz_specs = [pl.BlockSpec((tm, w), functools.partial(lambda i, j: (i, j), j=ja)) for _, w, ja, _ in segments]
    w_specs = [pl.BlockSpec((w, d), functools.partial(lambda i, j: (j, 0), j=jw), pipeline_mode=pl.Buffered(1))
               for _, w, _, jw in segments]
    return pl.pallas_call(
        body, name="bwd_in", grid=(s // tm,),
        in_specs=dz_specs + w_specs + [rowd, _const_spec((1, d)), rowd],
        out_specs=[rowd, _acc_spec((1, d))],
        out_shape=[jax.ShapeDtypeStruct((s, d), f32), jax.ShapeDtypeStruct((1, d), f32)],
        compiler_params=_cp(("arbitrary",), VMEM_LIMIT),
    )(*[a for a, _, _, _ in segments], *([wz] * n_seg), x, g_mix, d2)


def _pick_tile(n, cap):
    best = None
    for t in range(LANES, cap + 1, LANES):
        if n % t == 0:
            best = t
    return best if best is not None else n


def _mm_tn_many(a, bs, name, tm, transposed=False):
    kk, m = a.shape
    n_b = len(bs)
    tk = min(1024, kk)
    n_k = kk // tk

    def body(a_ref, *refs):
        b_refs, o_refs, acc_refs = refs[:n_b], refs[n_b:2 * n_b], refs[2 * n_b:]

        @pl.when(pl.program_id(1) == 0)
        def _():
            for acc in acc_refs:
                acc[...] = jnp.zeros_like(acc)
        a_blk = a_ref[...].astype(bf16)
        for b_ref, acc in zip(b_refs, acc_refs):
            acc[...] += _dot_tn(a_blk, b_ref[...].astype(bf16))

        @pl.when(pl.program_id(1) == n_k - 1)
        def _():
            for o_ref, acc in zip(o_refs, acc_refs):
                o_ref[...] = (acc[...].T if transposed else acc[...]).astype(bf16)

    if transposed:
        out_specs = [pl.BlockSpec((b.shape[1], tm), lambda i, k: (0, i)) for b in bs]
        out_shape = [jax.ShapeDtypeStruct((b.shape[1], m), bf16) for b in bs]
    else:
        out_specs = [pl.BlockSpec((tm, b.shape[1]), lambda i, k: (i, 0)) for b in bs]
        out_shape = [jax.ShapeDtypeStruct((m, b.shape[1]), bf16) for b in bs]
    return pl.pallas_call(
        body, name=name, grid=(m // tm, n_k),
        in_specs=[pl.BlockSpec((tk, tm), lambda i, k: (k, i))] + [pl.BlockSpec((tk, b.shape[1]), lambda i, k: (k, 0)) for b in bs],
        out_specs=out_specs,
        out_shape=out_shape,
        scratch_shapes=[pltpu.VMEM((tm, b.shape[1]), f32) for b in bs],
        compiler_params=_cp(("parallel", "arbitrary"), VMEM_LIMIT),
    )(a, *bs)


def _mm_tn(a, b, name):
    kk, m = a.shape
    _, n = b.shape
    tm = _pick_tile(m, 1408)
    tn = _pick_tile(n, 1408)
    tk = min(1024, kk)

    n_k = kk // tk

    def body(a_ref, b_ref, o_ref, acc_ref):
        @pl.when(pl.program_id(2) == 0)
        def _():
            acc_ref[...] = jnp.zeros_like(acc_ref)
        acc_ref[...] += _dot_tn(a_ref[...].astype(bf16), b_ref[...].astype(bf16))

        @pl.when(pl.program_id(2) == n_k - 1)
        def _():
            o_ref[...] = acc_ref[...].astype(bf16)

    return pl.pallas_call(
        body, name=name, grid=(m // tm, n // tn, n_k),
        in_specs=[pl.BlockSpec((tk, tm), lambda i, j, k: (k, i)), pl.BlockSpec((tk, tn), lambda i, j, k: (k, j))],
        out_specs=pl.BlockSpec((tm, tn), lambda i, j, k: (i, j)),
        out_shape=jax.ShapeDtypeStruct((m, n), bf16),
        scratch_shapes=[pltpu.VMEM((tm, tn), f32)],
        compiler_params=_cp(("parallel", "parallel", "arbitrary"), VMEM_LIMIT),
    )(a, b)


def _rope_tables(positions):
    inv_freq = ROPE_THETA ** (-jnp.arange(0, QK_ROPE, 2, dtype=f32) / QK_ROPE)
    ang = positions.astype(f32)[:, None] * inv_freq
    cos, sin = jnp.cos(ang), jnp.sin(ang)
    zero = jnp.zeros_like(cos)
    return (jnp.concatenate([cos, cos, zero, zero], axis=1), jnp.concatenate([zero, sin, zero, zero], axis=1),
            jnp.concatenate([-sin, zero, zero, zero], axis=1))


def _pad256(g):
    return jnp.pad(g.reshape(1, QK_HEAD), ((0, 0), (0, QK_PAD - QK_HEAD)))


RELAYOUT_BLOCKS = 8
FIRST = ("w_in", "w_qb", "w_kvb", "lb_param")
SECOND = ("w_o", "w_gate", "w_up", "w_down", "w_ple_gate", "w_ple_proj")
ROW_SHARDED = ("w_o", "w_down", "w_ple_gate")


def _col_moves(j):
    width = BIG["w_in"][0]
    lo = width * j
    w_in = [(max(lo, a) - lo, min(lo + width, b) - lo, d + max(lo, a) - a)
            for a, b, d in Z_SEGMENTS if max(lo, a) < min(lo + width, b)]
    head, half = divmod(j, 2)
    whole = lambda n: [(0, BIG[n][1], BIG[n][1] * j)]
    return {"w_in": w_in, "w_qb": [(0, 96, QK_PAD * head + 96 * half)], "w_kvb": whole("w_kvb"),
            "w_ple_proj": whole("w_ple_proj"), "lb_param": whole("lb_param")}


def _kernel_shape(name):
    rows, cols = BIG[name]
    if name == "w_in":
        return (Z_W, cols)
    return (rows, MLA_HEADS * QK_PAD if name == "w_qb" else N_DEV * cols)


def _relayout_specs(names, by_dev):
    specs = []
    for n in names:
        rows, cols = BIG[n]
        if n == "lb_param":
            specs.append(_acc_spec((N_DEV, rows, cols) if by_dev else _kernel_shape(n)))
        elif n == "w_in":
            cb = cols // RELAYOUT_BLOCKS
            specs.append(pl.BlockSpec((N_DEV, rows, cb), lambda i: (0, 0, i)) if by_dev else pl.BlockSpec((Z_W, cb), lambda i: (0, i)))
        elif by_dev:
            specs.append(pl.BlockSpec((N_DEV, rows // RELAYOUT_BLOCKS, cols), lambda i: (0, i, 0)))
        else:
            specs.append(pl.BlockSpec((rows // RELAYOUT_BLOCKS, _kernel_shape(n)[1]), lambda i: (i, 0)))
    return specs


def _weights_in(gathered, names, name):
    n = len(names)

    def body(*refs):
        ins, outs = dict(zip(names, refs[:n])), dict(zip(names, refs[n:]))
        if "w_in" in outs:
            outs["w_in"][Z_KR + QK_ROPE:Z_W, :] = jnp.zeros((Z_W - Z_KR - QK_ROPE, outs["w_in"].shape[1]), bf16)
        if "w_qb" in outs:
            for h in range(MLA_HEADS):
                outs["w_qb"][:, QK_PAD * h + QK_HEAD:QK_PAD * (h + 1)] = jnp.zeros((outs["w_qb"].shape[0], QK_PAD - QK_HEAD), bf16)
        for j in range(N_DEV):
            for wn, moves in _col_moves(j).items():
                if wn in outs:
                    for s0, s1, d0 in moves:
                        if wn == "w_in":
                            outs[wn][d0:d0 + s1 - s0, :] = ins[wn][j, s0:s1, :]
                        else:
                            outs[wn][:, d0:d0 + s1 - s0] = ins[wn][j, :, s0:s1]

    outs = pl.pallas_call(
        body, name=name, grid=(RELAYOUT_BLOCKS,), in_specs=_relayout_specs(names, True), out_specs=_relayout_specs(names, False),
        out_shape=[jax.ShapeDtypeStruct(_kernel_shape(wn), gathered[wn].dtype) for wn in names],
        compiler_params=_cp(("arbitrary",), VMEM_LIMIT),
    )(*[gathered[wn] for wn in names])
    return dict(zip(names, outs))


def _grads_out(sources, names, name):
    pieces = [(wn, start, arr) for wn in names for start, arr in sources[wn]]
    n_in = len(pieces)

    def body(*refs):
        outs = dict(zip(names, refs[n_in:]))

        def cols(wn, c0, c1):
            for (pn, start, arr), ref in zip(pieces, refs[:n_in]):
                if pn == wn and start <= c0 and c1 <= start + arr.shape[0 if wn == "w_in" else 1]:
                    return ref[c0 - start:c1 - start, :] if wn == "w_in" else ref[:, c0 - start:c1 - start]

        for j in range(N_DEV):
            for wn, moves in _col_moves(j).items():
                if wn in outs:
                    for s0, s1, d0 in moves:
                        if wn == "w_in":
                            outs[wn][j, s0:s1, :] = cols(wn, d0, d0 + s1 - s0).astype(bf16)
                        else:
                            outs[wn][j, :, s0:s1] = cols(wn, d0, d0 + s1 - s0).astype(bf16)

    def in_spec(wn, arr):
        if wn == "lb_param":
            return _acc_spec(arr.shape)
        if wn == "w_in":
            return pl.BlockSpec((arr.shape[0], arr.shape[1] // RELAYOUT_BLOCKS), lambda i: (0, i))
        return pl.BlockSpec((arr.shape[0] // RELAYOUT_BLOCKS, arr.shape[1]), lambda i: (i, 0))

    in_specs = [in_spec(wn, arr) for wn, _, arr in pieces]
    outs = pl.pallas_call(
        body, name=name, grid=(RELAYOUT_BLOCKS,), in_specs=in_specs, out_specs=_relayout_specs(names, True),
        out_shape=[jax.ShapeDtypeStruct((N_DEV, *BIG[wn]), bf16) for wn in names],
        compiler_params=_cp(("arbitrary",), VMEM_LIMIT),
    )(*[arr for _, _, arr in pieces])
    return dict(zip(names, outs))


def kernel(x, p, positions, g_mix, w_in, g_qa, g_kva, w_qb, w_kvb, g_qn, g_kn, lb_param, g_hgo, w_o, g_ffn, w_gate, w_up, w_down, g_ple, w_ple_gate, w_ple_proj, loss_target, m_g_mix, m_w_in, m_g_qa, m_g_kva, m_w_qb, m_w_kvb, m_g_qn, m_g_kn, m_lb_param, m_g_hgo, m_w_o, m_g_ffn, m_w_gate, m_w_up, m_w_down, m_g_ple, m_w_ple_gate, m_w_ple_proj, v_g_mix, v_w_in, v_g_qa, v_g_kva, v_w_qb, v_w_kvb, v_g_qn, v_g_kn, v_lb_param, v_g_hgo, v_w_o, v_g_ffn, v_w_gate, v_w_up, v_w_down, v_g_ple, v_w_ple_gate, v_w_ple_proj):
    w_all = dict(g_mix=g_mix, g_qa=g_qa, g_kva=g_kva, g_qn=g_qn, g_kn=g_kn, g_hgo=g_hgo, g_ffn=g_ffn, g_ple=g_ple,
                 w_in=w_in, w_qb=w_qb, w_kvb=w_kvb, w_o=w_o, w_gate=w_gate, w_up=w_up, w_down=w_down,
                 w_ple_gate=w_ple_gate, w_ple_proj=w_ple_proj, lb_param=lb_param)
    m_all = dict(g_mix=m_g_mix, g_qa=m_g_qa, g_kva=m_g_kva, g_qn=m_g_qn, g_kn=m_g_kn, g_hgo=m_g_hgo, g_ffn=m_g_ffn,
                 g_ple=m_g_ple, w_in=m_w_in, w_qb=m_w_qb, w_kvb=m_w_kvb, w_o=m_w_o, w_gate=m_w_gate, w_up=m_w_up,
                 w_down=m_w_down, w_ple_gate=m_w_ple_gate, w_ple_proj=m_w_ple_proj, lb_param=m_lb_param)
    v_all = dict(g_mix=v_g_mix, g_qa=v_g_qa, g_kva=v_g_kva, g_qn=v_g_qn, g_kn=v_g_kn, g_hgo=v_g_hgo, g_ffn=v_g_ffn,
                 g_ple=v_g_ple, w_in=v_w_in, w_qb=v_w_qb, w_kvb=v_w_kvb, w_o=v_w_o, w_gate=v_w_gate, w_up=v_w_up,
                 w_down=v_w_down, w_ple_gate=v_w_ple_gate, w_ple_proj=v_w_ple_proj, lb_param=v_lb_param)
    me_idx = jnp.stack([_me()]).astype(jnp.int32)
    x, p, positions, target = x[0], p[0, 0], positions[0], loss_target[0]
    s = x.shape[0]
    tm, tm_ffn, tq_f, tq_b = min(512, s), min(1024, s), min(2048, s), min(1024, s)
    g_mix, g_qa, g_kva, g_qn, g_kn, g_hgo, g_ffn, g_ple = (w_all[n].reshape(1, -1) for n in SMALL)
    g_qn_p, g_kn_p = _pad256(g_qn), _pad256(g_kn)
    cosb, sina, sinb = _rope_tables(positions)
    as_shard = lambda n, a: a[0].T if n in TRANSPOSED else a.reshape(BIG[n])
    shard = lambda n: as_shard(n, w_all[n])

    first = _all_gather([shard(n) for n in FIRST], [f32 if n == "lb_param" else bf16 for n in FIRST], "ag_first")
    lands = _cast_to_slot([shard(n) for n in SECOND], me_idx, first[0])
    ag2, token = _exchange_start([], lands, "ag_second_start")
    wk = _weights_in(dict(zip(FIRST, first)), FIRST, "weights_in_first")
    wz, wqb, wkvb, lb4 = (wk[n] for n in FIRST)

    h1, z = _fwd_in(x, g_mix, wz, tm)
    q, k, v = _fwd_mla_proj(z, cosb + token[0, 0], sina, sinb, g_qa, g_kva, wqb, wkvb, g_qn_p, g_kn_p, tm)
    o, gla_b, gla_states = _fwd_gla(z, lb4)
    a, a32 = _fwd_attn(q, k, v, tq_f, o)

    second = dict(zip(SECOND, _exchange_wait(ag2, [a, o], "ag_second_wait")[1]))
    w_pp = second["w_ple_proj"]
    w_o, w_down, w_pg, w_gate, w_up = (second[n].reshape(N_DEV * BIG[n][0], BIG[n][1]) for n in ROW_SHARDED + ("w_gate", "w_up"))

    x2, cat = _fwd_mix(a, o, z, g_hgo, x, w_o, tm)
    x3, gp, up = _fwd_ffn(x2, g_ffn, w_gate, w_up, w_down, tm)
    d3, gw_pg, gw_pp, dg_ple, loss_tile = _ple_loss_fwd_bwd(x3, g_ple, w_pg, p, w_pp, target, tm)
    act, dgp, dup = _bwd_ffn_hidden(d3, gp, up, w_down, tm, D_FF // 2)
    d2, h2, dg_ffn = _bwd_ffn_in(d3, x2, dgp, dup, g_ffn, w_gate, w_up, tm)
    da, do, dz_hg, dg_hgo, gw_o = _bwd_mix(d2, w_o, o, z, g_hgo, cat, tm)

    gw_gate, gw_up = _mm_tn_many(h2, [dgp, dup], "dw_gate_up", 512, transposed=True)
    blocks = {"w_ple_proj": gw_pp}
    row_grads = {"w_o": gw_o, "w_down": _mm_tn(act, d3, "dw_down"), "w_ple_gate": gw_pg, "w_gate": gw_gate, "w_up": gw_up}
    blocks.update({n: g.reshape(N_DEV, *BIG[n]) for n, g in row_grads.items()})
    empty = lambda names: [lax.empty((N_PEERS, *BIG[n]), bf16) for n in names]
    rs2, token = _exchange_start([blocks[n] for n in SECOND], empty(SECOND), "rs_second_start")

    dq, dk, dv = _bwd_attn(q, k, v, da, a32, tq_b, token)
    dz_hq, dz_hff, dz_hfb, dz_hi, dlb4 = _bwd_gla(z, lb4 + token[0, 0], do, gla_b, gla_states)
    dz_mla, gw_qb, gw_kvb, dg_qa, dg_kva, dg_qn, dg_kn = _bwd_mla_proj(
        z, dq, dk, dv, cosb, sina, sinb, g_qa, g_kva, wqb, wkvb, g_qn_p, g_kn_p, tm)

    gz = list(zip((Z_HQ, Z_HFF, Z_HFB, Z_HI, Z_HG, Z_CQ),
                  _mm_tn_many(h1, [dz_hq, dz_hff, dz_hfb, dz_hi, dz_hg, dz_mla], "dw_in", 1024, transposed=True)))
    blocks1 = _grads_out({"w_in": gz, "w_qb": [(0, gw_qb)], "w_kvb": [(0, gw_kvb)],
                          "lb_param": [(0, dlb4)]}, FIRST, "grads_out_first")
    rs1, token = _exchange_start([blocks1[n] for n in FIRST], empty(FIRST), "rs_first_start")

    result = {}

    def adam(names, lands, src, n_blocks, after=()):
        flipped = TRANSPOSED
        given = lambda arrs: [arrs[n][0].T if n in flipped else arrs[n] for n in names]
        outs = _adam_shards(me_idx, [src[n] for n in names], lands, given(w_all), given(m_all), given(v_all), n_blocks,
                            "adamw_" + names[0], after)
        for n, o in zip(names, outs):
            result[n] = [t.T[None] for t in o] if n in flipped else o
        return outs[0][0]

    blocks2, lands2 = (dict(zip(SECOND, arrs)) for arrs in _exchange_wait(rs2, [token], "rs_second_wait"))
    by2 = ("w_down", "w_gate", "w_up")
    by8 = tuple(n for n in SECOND if n not in by2)
    done = [adam(by8, [lands2[n] for n in by8], blocks2, 8), adam(by2, [lands2[n] for n in by2], blocks2, 2)]

    segments = [(dz_hq, 512, 0, Z_HQ // 512), (dz_hff, 512, 0, Z_HFF // 512), (dz_hfb, 512, 0, Z_HFB // 512),
                (dz_hi, 512, 0, Z_HI // 512), (dz_hg, 512, 0, Z_HG // 512), (dz_mla, 640, 0, Z_CQ // 640)]
    grad_x, dg_mix = _bwd_in(segments, wz, x, g_mix + token[0, 0], d2, tm)
    dgains = (dg_mix, dg_qa, dg_kva, dg_qn, dg_kn, dg_hgo, dg_ffn, dg_ple)

    vec = jnp.concatenate(list(dgains) + [loss_tile[0:1]], axis=1)
    parts = _all_gather([vec], [f32], "ag_gains")[0]
    outs, loss_row = _adam_gains(parts, [w_all[n] for n in SMALL], [m_all[n] for n in SMALL], [v_all[n] for n in SMALL])
    result.update(zip(SMALL, outs))

    blocks1, lands1 = _exchange_wait(rs1, [grad_x, loss_row, *done], "rs_first_wait")
    adam(FIRST, lands1, dict(zip(FIRST, blocks1)), 8)

    order = ("g_mix", "w_in", "g_qa", "g_kva", "w_qb", "w_kvb", "g_qn", "g_kn", "lb_param", "g_hgo", "w_o", "g_ffn",
             "w_gate", "w_up", "w_down", "g_ple", "w_ple_gate", "w_ple_proj")
    return (loss_row[0, 0], grad_x[None], *[result[n][k] for k in range(4) for n in order])
```

```python
import functools
import math

import jax
import jax.numpy as jnp
from jax import lax
from jax.experimental import pallas as pl
from jax.experimental.pallas import tpu as pltpu

f32 = jnp.float32
bf16 = jnp.bfloat16

N_DEV = 8
MLA_HEADS = 4
QK_NOPE = 128
QK_ROPE = 64
QK_HEAD = QK_NOPE + QK_ROPE
QK_PAD = 256
V_HEAD = 128
Q_LORA = 256
KV_LORA = 256
HG_HEADS = 4
CHUNK = 64
D_FF = 2816
PLE_DIM = 256
ROPE_THETA = 10000.0
EPS = 1e-6
ATTN_SCALE = QK_HEAD ** -0.5
LOG2_E = math.log2(math.e)
ATTN_SUB_ROWS = 256
Z_HQ, Z_HFF, Z_HFB, Z_HI, Z_HG, Z_CQ, Z_CKV, Z_KR, Z_W = 0, 512, 1024, 1536, 2048, 2560, 2816, 3072, 3200

ADAM_LR, ADAM_B1, ADAM_B2, ADAM_EPS, ADAM_WD, ADAM_STEP = 0.001, 0.9, 0.999, 1e-08, 0.01, 10

LANES = 128
BIG = {"w_in": (392, 1024), "w_qb": (256, 96), "w_kvb": (256, 128), "w_o": (128, 1024), "w_gate": (352, 1024),
       "w_up": (352, 1024), "w_down": (352, 1024), "w_ple_gate": (128, 1024), "w_ple_proj": (256, 128),
       "lb_param": (4, 64)}
TRANSPOSED = ("w_gate", "w_up", "w_in")
SMALL = {"g_mix": (0, 1024), "g_qa": (1024, 256), "g_kva": (1280, 256), "g_qn": (1536, 192), "g_kn": (1792, 192),
         "g_hgo": (2048, 512), "g_ffn": (2560, 1024), "g_ple": (3584, 1024)}
LOSS_OFF = 4608
GAIN_VEC = LOSS_OFF + LANES
Z_SEGMENTS = ((0, 256, Z_CQ), (256, 512, Z_CKV), (512, 576, Z_KR), (576, 1088, Z_HQ), (1088, 1600, Z_HFF),
              (1600, 2112, Z_HFB), (2112, 2624, Z_HI), (2624, 3136, Z_HG))

VMEM_LIMIT = 56 * 1024 * 1024
MESH = pl.DeviceIdType.MESH


def _cp(sem=None, vmem=None):
    return pltpu.CompilerParams(dimension_semantics=sem, vmem_limit_bytes=vmem)


def _const_spec(shape):
    nd = len(shape)
    return pl.BlockSpec(shape, lambda *_: (0,) * nd, pipeline_mode=pl.Buffered(1))


def _acc_spec(shape):
    nd = len(shape)
    return pl.BlockSpec(shape, lambda *_: (0,) * nd)


def _sigmoid(x):
    return jax.nn.sigmoid(x)


def _dot(a, b):
    return jnp.dot(a, b, preferred_element_type=f32)


def _dot_nt(a, b):
    return lax.dot_general(a, b, (((1,), (1,)), ((), ())), preferred_element_type=f32)


def _dot_tn(a, b):
    return lax.dot_general(a, b, (((0,), (0,)), ((), ())), preferred_element_type=f32)


def _rms_fwd(x, g, width):
    r = lax.rsqrt(jnp.sum(x * x, axis=-1, keepdims=True) * (1.0 / width) + EPS)
    return x * r * g, r


def _rms_bwd(dy, x, r, g, width):
    u = dy * g
    dx = r * u - x * (r * r * r) * (jnp.sum(u * x, axis=-1, keepdims=True) * (1.0 / width))
    return dx, dy * x * r


class _Both:
    def __init__(self, *copies):
        self.copies = copies

    def start(self):
        for cp in self.copies:
            cp.start()

    def wait(self):
        for cp in self.copies:
            cp.wait()


def _rope(b, c, sa, sb):
    return b * c + pltpu.roll(b, 32, 1) * sa + pltpu.roll(b, 96, 1) * sb


def _all_gather(shards, dtypes, name):
    n = len(shards)

    def body(*refs):
        in_refs, out_refs, stage = refs[:n], refs[n:2 * n], refs[2 * n:3 * n]
        send_sems, recv_sems, local_sems = refs[3 * n:]
        for w in range(n):
            stage[w][...] = in_refs[w][...].astype(stage[w].dtype)
        x, y, c = lax.axis_index("x"), lax.axis_index("y"), lax.axis_index("c")
        me, sibling = (x, y, c), (x, y, 1 - c)
        chips = [(1 - x, y), (x, 1 - y), (1 - x, 1 - y)]

        def slot(w, px, py, pc):
            return out_refs[w].at[4 * px + 2 * py + pc]

        def copy(w, k, block, to, src=None):
            return pltpu.make_async_remote_copy(
                src_ref=slot(w, *block) if src is None else src, dst_ref=slot(w, *block),
                send_sem=send_sems.at[w, k], recv_sem=recv_sems.at[w, k], device_id=to, device_id_type=MESH)

        first = []
        for j, chip in enumerate(chips):
            first += [copy(w, 1 + j, me, (*chip, c), src=stage[w]) for w in range(n)]
        first += [copy(w, 0, me, sibling, src=stage[w]) for w in range(n)]
        mine = [pltpu.make_async_copy(stage[w], slot(w, *me), local_sems.at[w]) for w in range(n)]
        for cp in first + mine:
            cp.start()
        passed = []
        for j, chip in enumerate(chips):
            for w in range(n):
                copy(w, 1 + j, (*chip, c), me).wait_recv()
                passed.append(copy(w, 4 + j, (*chip, c), sibling))
                passed[-1].start()
        for w in range(n):
            copy(w, 0, sibling, me).wait_recv()
        for j, chip in enumerate(chips):
            for w in range(n):
                copy(w, 4 + j, (*chip, 1 - c), me).wait_recv()
        for cp in first + passed:
            cp.wait_send()
        for cp in mine:
            cp.wait()

    return pl.pallas_call(
        body, name=name,
        out_shape=[jax.ShapeDtypeStruct((N_DEV, *s.shape), dt) for s, dt in zip(shards, dtypes)],
        in_specs=[pl.BlockSpec(memory_space=pltpu.VMEM)] * n,
        out_specs=[pl.BlockSpec(memory_space=pl.ANY)] * n,
        scratch_shapes=[pltpu.VMEM(s.shape, dt) for s, dt in zip(shards, dtypes)]
        + [pltpu.SemaphoreType.DMA((n, 7)), pltpu.SemaphoreType.DMA((n, 7)), pltpu.SemaphoreType.DMA((n,))],
        compiler_params=_cp(None, VMEM_LIMIT),
    )(*shards)


N_PEERS = N_DEV - 1
HBM_SPEC = pl.BlockSpec(memory_space=pltpu.HBM)
SEM_SPEC = pl.BlockSpec(memory_space=pltpu.SEMAPHORE)
DATAFLOW = pltpu.SideEffectType.DATAFLOW_SIDE_EFFECTING


def _me():
    return 4 * lax.axis_index("x") + 2 * lax.axis_index("y") + lax.axis_index("c")


def _peer(k):
    x, y, c = lax.axis_index("x"), lax.axis_index("y"), lax.axis_index("c")
    px = 1 - x if k & 4 else x
    py = 1 - y if k & 2 else y
    pc = 1 - c if k & 1 else c
    return (px, py, pc), 4 * px + 2 * py + pc


def _exchange_copies(src_refs, land_refs, send_sems, recv_sems, gather):
    cps = []
    me = _me()
    for k in range(1, N_DEV):
        peer, peer_idx = _peer(k)
        for w, land in enumerate(land_refs):
            src = land.at[me] if gather else src_refs[w].at[peer_idx]
            dst = land.at[me] if gather else land.at[k - 1]
            cps.append(pltpu.make_async_remote_copy(
                src_ref=src, dst_ref=dst, send_sem=send_sems.at[N_PEERS * w + k - 1], recv_sem=recv_sems.at[N_PEERS * w + k - 1],
                device_id=peer, device_id_type=MESH))
    return cps


def _exchange_start(srcs, lands, name):
    n_src, n = len(srcs), len(lands)

    def body(*refs):
        src_refs, land_refs = refs[:n_src], refs[n_src:n_src + n]
        send_sems, recv_sems = refs[n_src + n], refs[n_src + n + 1]
        token = refs[-1]
        for cp in _exchange_copies(src_refs, land_refs, send_sems, recv_sems, gather=not n_src):
            cp.start()
        token[...] = jnp.zeros_like(token)

    arrays = [pltpu.with_memory_space_constraint(a, pltpu.HBM) for a in (*srcs, *lands)]
    outs = pl.pallas_call(
        body, name=name,
        out_shape=(pltpu.SemaphoreType.DMA((n * N_PEERS,)), pltpu.SemaphoreType.DMA((n * N_PEERS,)),
                   *[pltpu.HBM(a.shape, a.dtype) for a in arrays], jax.ShapeDtypeStruct((8, LANES), f32)),
        in_specs=[HBM_SPEC] * len(arrays),
        out_specs=(SEM_SPEC, SEM_SPEC, *[HBM_SPEC] * len(arrays), pl.BlockSpec(memory_space=pltpu.VMEM)),
        input_output_aliases={i: 2 + i for i in range(len(arrays))},
        compiler_params=pltpu.CompilerParams(has_side_effects=DATAFLOW),
    )(*arrays)
    return (outs[0], outs[1], outs[2:2 + n_src], outs[2 + n_src:2 + n_src + n]), outs[-1]


def _exchange_wait(state, after, name):
    send_sems, recv_sems, srcs, lands = state
    n_src, n = len(srcs), len(lands)

    def body(*refs):
        src_refs, land_refs = refs[:n_src], refs[n_src:n_src + n]
        send_ref, recv_ref = refs[n_src + n], refs[n_src + n + 1]
        for cp in _exchange_copies(src_refs, land_refs, send_ref, recv_ref, gather=not n_src):
            cp.wait_send()
            cp.wait_recv()

    arrays = (*srcs, *lands)
    outs = pl.pallas_call(
        body, name=name,
        out_shape=tuple(pltpu.HBM(a.shape, a.dtype) for a in arrays),
        in_specs=[HBM_SPEC] * len(arrays) + [SEM_SPEC, SEM_SPEC] + [pl.BlockSpec(memory_space=pl.ANY)] * len(after),
        out_specs=tuple([HBM_SPEC] * len(arrays)),
        input_output_aliases={i: i for i in range(len(arrays))},
        compiler_params=pltpu.CompilerParams(has_side_effects=DATAFLOW),
    )(*arrays, send_sems, recv_sems, *after)
    return outs[:n_src], outs[n_src:]


def _cast_to_slot(shards, me_idx, after):
    n = len(shards)

    def body(i_ref, *refs):
        for w in range(n):
            refs[n + 1 + w][...] = refs[w][...].astype(bf16)

    return pl.pallas_call(
        body, name="cast_to_slot",
        grid_spec=pltpu.PrefetchScalarGridSpec(
            num_scalar_prefetch=1, grid=(1,),
            in_specs=[pl.BlockSpec(s.shape, lambda i, m: (0, 0)) for s in shards] + [pl.BlockSpec(memory_space=pl.ANY)],
            out_specs=[pl.BlockSpec((None, *s.shape), lambda i, m: (m[0], 0, 0)) for s in shards]),
        out_shape=[jax.ShapeDtypeStruct((N_DEV, *s.shape), bf16) for s in shards],
        compiler_params=_cp(("arbitrary",), VMEM_LIMIT),
    )(me_idx, *shards, after)


def _row_block(rows, n_blocks):
    return (rows // n_blocks, True) if rows % (16 * n_blocks) == 0 else (rows, False)


def _adam_math(w, g, m, v):
    m = ADAM_B1 * m + (1.0 - ADAM_B1) * g
    v = ADAM_B2 * v + (1.0 - ADAM_B2) * (g * g)
    m_hat = m / (1.0 - ADAM_B1 ** ADAM_STEP)
    v_hat = v / (1.0 - ADAM_B2 ** ADAM_STEP)
    delta = -ADAM_LR * (m_hat / (jnp.sqrt(v_hat) + ADAM_EPS) + ADAM_WD * w)
    return delta, m, v


def _adam_shards(me_idx, blocks, lands, ws, ms, vs, n_blocks, name, after=()):
    n = len(blocks)

    def body(i_ref, *refs):
        ins, outs = refs[:5 * n], refs[5 * n + len(after):]
        for w in range(n):
            g_ref, b_ref, w_ref, m_ref, v_ref = (ins[t * n + w] for t in range(5))
            g = g_ref[...].astype(f32)
            for k in range(N_PEERS):
                g = g + b_ref[k].astype(f32)
            if len(w_ref.shape) == 2:
                pieces = [(slice(None), g)]
            else:
                pieces = [(a, g[2 * a:2 * a + 2]) for a in range(2)]
            for at, gp in pieces:
                vals = (gp,) + _adam_math(w_ref[at], gp, m_ref[at], v_ref[at])
                for t, val in enumerate(vals):
                    outs[4 * w + t][at] = val

    specs = [[] for _ in range(5)]
    out_specs, out_shape = [], []
    for g, wt in zip(blocks, ws):
        rows, cols = g.shape[1:]
        rb, cut = _row_block(rows, n_blocks)
        if not cut and wt.ndim == 2 and cols % (LANES * n_blocks) == 0:
            cb = cols // n_blocks
            specs[0].append(pl.BlockSpec((None, rows, cb), lambda i, s: (s[0], 0, i)))
            specs[1].append(pl.BlockSpec((N_PEERS, rows, cb), lambda i, s: (0, 0, i)))
            shard = pl.BlockSpec((rows, cb), lambda i, s: (0, i))
            for t in (2, 3, 4):
                specs[t].append(shard)
            out_specs += [shard] * 4
            out_shape += [jax.ShapeDtypeStruct(wt.shape, f32)] * 4
            continue
        specs[0].append(pl.BlockSpec((None, rb, cols), functools.partial(lambda i, s, cut: (s[0], i if cut else 0, 0), cut=cut)))
        specs[1].append(pl.BlockSpec((N_PEERS, rb, cols), functools.partial(lambda i, s, cut: (0, i if cut else 0, 0), cut=cut)))
        if wt.ndim == 2:
            shard = pl.BlockSpec((rb, cols), functools.partial(lambda i, s, cut: (i if cut else 0, 0), cut=cut))
        elif wt.shape[0] == 1:
            shard = pl.BlockSpec((None, rb, cols), functools.partial(lambda i, s, cut: (0, i if cut else 0, 0), cut=cut))
        else:
            shard = pl.BlockSpec(wt.shape, functools.partial(lambda i, s, nd: (0,) * nd, nd=wt.ndim))
        for t in (2, 3, 4):
            specs[t].append(shard)
        out_specs += [shard] * 4
        out_shape += [jax.ShapeDtypeStruct(wt.shape, f32)] * 4
    outs = pl.pallas_call(
        body, name=name,
        grid_spec=pltpu.PrefetchScalarGridSpec(
            num_scalar_prefetch=1, grid=(n_blocks,), in_specs=sum(specs, []) + [pl.BlockSpec(memory_space=pl.ANY)] * len(after),
            out_specs=out_specs),
        out_shape=out_shape,
        compiler_params=_cp(("arbitrary",), VMEM_LIMIT),
    )(me_idx, *blocks, *lands, *ws, *ms, *vs, *after)
    return [outs[4 * w:4 * w + 4] for w in range(n)]


def _adam_gains(parts, ws, ms, vs):
    n = len(ws)

    def body(p_ref, *refs):
        ins, outs = refs[:3 * n], refs[3 * n:]
        g_all = p_ref[0]
        for k in range(1, N_DEV):
            g_all = g_all + p_ref[k]
        for w, (off, lanes) in enumerate(SMALL.values()):
            w_ref, m_ref, v_ref = ins[w], ins[n + w], ins[2 * n + w]
            if len(w_ref.shape) == 2:
                pieces = [(slice(None), off, lanes)]
            else:
                pieces = [((slice(None), h), off + LANES * h, LANES) for h in range(w_ref.shape[1])]
            for at, o, ln in pieces:
                g = g_all[:, o:o + ln]
                vals = (g,) + _adam_math(w_ref[at], g, m_ref[at], v_ref[at])
                for t, val in enumerate(vals):
                    outs[4 * w + t][at] = val
        outs[4 * n][...] = g_all[:, LOSS_OFF:LOSS_OFF + LANES]

    out_shape = sum([[jax.ShapeDtypeStruct(w.shape, f32)] * 4 for w in ws], []) + [jax.ShapeDtypeStruct((1, LANES), f32)]
    outs = pl.pallas_call(body, name="adamw_gains", out_shape=out_shape)(parts, *ws, *ms, *vs)
    return [outs[4 * w:4 * w + 4] for w in range(n)], outs[4 * n]


def _fwd_in(x, g_mix, wz, tm):
    s, d = x.shape

    def body(x_ref, g_ref, w_ref, h_ref, z_ref):
        h, _ = _rms_fwd(x_ref[...], g_ref[...], d)
        hb = h.astype(bf16)
        h_ref[...] = hb
        z_ref[...] = _dot_nt(hb, w_ref[...])

    return pl.pallas_call(
        body, name="fwd_in", grid=(s // tm,),
        in_specs=[pl.BlockSpec((tm, d), lambda i: (i, 0)), _const_spec((1, d)), _const_spec((Z_W, d))],
        out_specs=[pl.BlockSpec((tm, d), lambda i: (i, 0)), pl.BlockSpec((tm, Z_W), lambda i: (i, 0))],
        out_shape=[jax.ShapeDtypeStruct((s, d), bf16), jax.ShapeDtypeStruct((s, Z_W), f32)],
        compiler_params=_cp(("parallel",), VMEM_LIMIT),
    )(x, g_mix, wz)


def _mla_qk_fwd(cq, ckv, g_qa, g_kva, wqb, wkvb):
    cqn, rq = _rms_fwd(cq, g_qa, Q_LORA)
    ckvn, rkv = _rms_fwd(ckv, g_kva, KV_LORA)
    cqn_b, ckvn_b = cqn.astype(bf16), ckvn.astype(bf16)
    q0 = _dot(cqn_b, wqb)
    kv0 = _dot(ckvn_b, wkvb)
    return cqn_b, rq, ckvn_b, rkv, q0, kv0


def _fwd_mla_proj(z, cosb, sina, sinb, g_qa, g_kva, wqb, wkvb, g_qn, g_kn, tm):
    s = z.shape[0]
    hh = MLA_HEADS

    def body(cq_ref, ckv_ref, kr_ref, c_ref, sa_ref, sb_ref, gqa_ref, gkva_ref, wqb_ref, wkvb_ref, gqn_ref, gkn_ref,
             q_ref, k_ref, v_ref):
        _, _, _, _, q0, kv0 = _mla_qk_fwd(cq_ref[...], ckv_ref[...], gqa_ref[...], gkva_ref[...], wqb_ref[...], wkvb_ref[...])
        kr = kr_ref[...]
        c, sa, sb = c_ref[...], sa_ref[...], sb_ref[...]
        gqn, gkn = gqn_ref[...], gkn_ref[...]
        kr_sq = jnp.sum(kr * kr, axis=-1, keepdims=True)
        for h in range(hh):
            qh = q0[:, QK_PAD * h:QK_PAD * (h + 1)]
            qn, _ = _rms_fwd(qh, gqn, QK_HEAD)
            q_ref[h, :, 0:128] = qn[:, 0:128].astype(bf16)
            q_ref[h, :, 128:256] = _rope(qn[:, 128:256], c, sa, sb).astype(bf16)
            kn_ = kv0[:, 256 * h:256 * h + 128]
            rk = lax.rsqrt((jnp.sum(kn_ * kn_, axis=-1, keepdims=True) + kr_sq) * (1.0 / QK_HEAD) + EPS)
            k_ref[h, :, 0:128] = (kn_ * rk * gkn[:, 0:128]).astype(bf16)
            k_ref[h, :, 128:256] = _rope(kr * rk * gkn[:, 128:256], c, sa, sb).astype(bf16)
            v_ref[h] = kv0[:, 256 * h + 128:256 * h + 256].astype(bf16)

    row128 = pl.BlockSpec((tm, 128), lambda i: (i, 0))
    return pl.pallas_call(
        body, name="fwd_mla_proj", grid=(s // tm,),
        in_specs=[pl.BlockSpec((tm, 256), lambda i: (i, Z_CQ // 256)), pl.BlockSpec((tm, 256), lambda i: (i, Z_CKV // 256)),
                  pl.BlockSpec((tm, 128), lambda i: (i, Z_KR // 128)), row128, row128, row128,
                  _const_spec((1, 256)), _const_spec((1, 256)), _const_spec((256, 1024)), _const_spec((256, 1024)),
                  _const_spec((1, 256)), _const_spec((1, 256))],
        out_specs=[pl.BlockSpec((hh, tm, QK_PAD), lambda i: (0, i, 0)), pl.BlockSpec((hh, tm, QK_PAD), lambda i: (0, i, 0)),
                   pl.BlockSpec((hh, tm, V_HEAD), lambda i: (0, i, 0))],
        out_shape=[jax.ShapeDtypeStruct((hh, s, QK_PAD), bf16), jax.ShapeDtypeStruct((hh, s, QK_PAD), bf16),
                   jax.ShapeDtypeStruct((hh, s, V_HEAD), bf16)],
        compiler_params=_cp(("parallel",), VMEM_LIMIT),
    )(z, z, z, cosb, sina, sinb, g_qa, g_kva, wqb, wkvb, g_qn, g_kn)


def _fwd_attn(q, k, v, tq, after):
    hh, s, _ = q.shape

    n_sub = max(1, tq // ATTN_SUB_ROWS)

    def body(q_ref, k_ref, v_ref, after_ref, o_ref, o32_ref):
        for t in range(n_sub):
            rows = slice(t * (tq // n_sub), (t + 1) * (tq // n_sub))
            sc = _dot_nt(q_ref[rows, :], k_ref[...])
            p = jnp.exp2((sc - jnp.max(sc, axis=-1, keepdims=True)) * (ATTN_SCALE * LOG2_E))
            l = jnp.sum(p, axis=-1, keepdims=True)
            o = _dot(p.astype(bf16), v_ref[...]) * (1.0 / l)
            o_ref[rows, :] = o.astype(bf16)
            o32_ref[rows, :] = o

    out = pl.BlockSpec((tq, V_HEAD), lambda h, i: (i, h))
    return pl.pallas_call(
        body, name="fwd_attn", grid=(hh, s // tq),
        in_specs=[pl.BlockSpec((None, tq, QK_PAD), lambda h, i: (h, i, 0)),
                  pl.BlockSpec((None, s, QK_PAD), lambda h, i: (h, 0, 0)),
                  pl.BlockSpec((None, s, V_HEAD), lambda h, i: (h, 0, 0)), pl.BlockSpec(memory_space=pl.ANY)],
        out_specs=[out, out],
        out_shape=[jax.ShapeDtypeStruct((s, hh * V_HEAD), bf16), jax.ShapeDtypeStruct((s, hh * V_HEAD), f32)],
        compiler_params=_cp(("parallel", "parallel"), VMEM_LIMIT),
    )(q, k, v, after)


def _split3(x):
    hi = x.astype(bf16)
    r1 = x - hi.astype(f32)
    mid = r1.astype(bf16)
    lo = (r1 - mid.astype(f32)).astype(bf16)
    return jnp.concatenate([hi, mid, lo], axis=-1)


def _tri_sum(tri, x):
    y = _dot(tri, _split3(x))
    return y[:, 0:128] + y[:, 128:256] + y[:, 256:384]


GLA_GROUP = 4
GLA_ROWS = GLA_GROUP * CHUNK
GLA_HEADS_PER_STEP = 2


def _gla_masks(rev):
    row = lax.broadcasted_iota(jnp.int32, (GLA_ROWS, GLA_ROWS), 0)
    col = lax.broadcasted_iota(jnp.int32, (GLA_ROWS, GLA_ROWS), 1)
    shift = CHUNK.bit_length() - 1
    same = (jnp.right_shift(row, shift) == jnp.right_shift(col, shift)).astype(f32)
    lower, upper = (row >= col).astype(f32) * same, (row <= col).astype(f32) * same
    keep, keep_t = (upper, lower) if rev else (lower, upper)
    chunk_of = jnp.right_shift(lax.broadcasted_iota(jnp.int32, (GLA_ROWS, 1), 0), shift)
    return keep, keep.astype(bf16), keep_t.astype(bf16), [(chunk_of == c).astype(f32) for c in range(GLA_GROUP)]


def _gla_gates(hq, hf, lower):
    sg = _sigmoid(hf)
    f = lower + (1.0 - lower) * sg
    return hq * _sigmoid(hq), 1.0 - f, jnp.log(f), f, sg


def _gla_last_mid(b, rev):
    b3 = b.reshape(GLA_GROUP, CHUNK, 128)
    last, mid = (0, CHUNK // 2) if rev else (CHUNK - 1, CHUNK // 2 - 1)
    return b3[:, last:last + 1, :], b3[:, mid:mid + 1, :]


def _gla_per_row(per_chunk):
    return jnp.broadcast_to(per_chunk, (GLA_GROUP, CHUNK, 128)).reshape(GLA_ROWS, 128)


def _gla_block_diag(x, row_masks):
    return jnp.concatenate([(x * m).astype(bf16) for m in row_masks], axis=-1)


def _gla_diag(y):
    return jnp.concatenate([y[CHUNK * c:CHUNK * (c + 1), 128 * c:128 * (c + 1)] for c in range(GLA_GROUP)], axis=0)


def _gla_rows(n, n_groups, rev):
    ne = n_groups - 1 - n if rev else n
    return pl.ds(pl.multiple_of(ne * GLA_ROWS, GLA_ROWS), GLA_ROWS), ne * GLA_GROUP


def _gla_scan_order(rev):
    return tuple(reversed(range(GLA_GROUP))) if rev else tuple(range(GLA_GROUP))


def _fwd_gla(z, lb4):
    s = z.shape[0]
    n_groups = s // GLA_ROWS
    assert n_groups % 2 == 0
    hp = GLA_HEADS_PER_STEP
    chains = [(hh, rev) for hh in range(hp) for rev in (False, True)]

    def body(hq_ref, hff_ref, hfb_ref, hi_ref, lb_ref, o_ref, b_ref, states_ref, st_ref, stage_ref, b_stage, sems):
        st_ref[...] = jnp.zeros_like(st_ref)
        masks = {rev: _gla_masks(rev) for rev in (False, True)}
        lowers = [_sigmoid(lb_ref[int(rev):int(rev) + 1, 128 * hh:128 * (hh + 1)]
                           - lb_ref[2 + int(rev):3 + int(rev), 128 * hh:128 * (hh + 1)]) for hh, rev in chains]

        def states_out(slot, ci, chunk0):
            hh, rev = chains[ci]
            head = pl.program_id(0) * hp + hh
            rows = pl.ds(pl.multiple_of(chunk0 * CHUNK, GLA_ROWS), GLA_ROWS)
            return _Both(
                pltpu.make_async_copy(stage_ref.at[slot, ci], states_ref.at[head, int(rev), pl.ds(chunk0, GLA_GROUP)],
                                      sems.at[slot, ci]),
                pltpu.make_async_copy(b_stage.at[slot, ci], b_ref.at[int(rev), rows, pl.ds(pl.multiple_of(head * 128, 128), 128)],
                                      sems.at[slot, len(chains) + ci]))

        def make_step(first):
            def step(n, carry):
                slot = n % 2

                @pl.when(n >= 2)
                def _():
                    for ci in range(len(chains)):
                        states_out(slot, ci, 0).wait()

                for ci, (hh, rev) in enumerate(chains):
                    cols = slice(128 * hh, 128 * (hh + 1))
                    rows, chunk0 = _gla_rows(n, n_groups, rev)
                    maskf, tri, _, row_masks = masks[rev]
                    hf_ref = hfb_ref if rev else hff_ref
                    q, k, logf, _, _ = _gla_gates(hq_ref[rows, cols], hf_ref[rows, cols], lowers[ci])
                    vb = hi_ref[rows, cols].astype(bf16)
                    b = _tri_sum(tri, logf)
                    b_stage[slot, ci] = b
                    b_last3, b_mid3 = _gla_last_mid(b, rev)
                    b_last, b_mid = _gla_per_row(b_last3), _gla_per_row(b_mid3)
                    qi = (q * jnp.exp(b - b_mid)).astype(bf16)
                    ki = (k * jnp.exp(b_mid - b)).astype(bf16)
                    a = (_dot_nt(qi, ki) * maskf).astype(bf16)
                    kv = _dot_tn(vb, _gla_block_diag(k * jnp.exp(b_last - b), row_masks))
                    decay3 = jnp.exp(b_last3)
                    st = st_ref[ci]
                    before = [None] * GLA_GROUP
                    for c in _gla_scan_order(rev):
                        stage_ref[slot, ci, c] = st
                        before[c] = st.astype(bf16)
                        st = st * decay3[c] + kv[:, 128 * c:128 * (c + 1)]
                    st_ref[ci] = st
                    states_out(slot, ci, chunk0).start()
                    inter = _dot_nt((q * jnp.exp(b)).astype(bf16), jnp.concatenate(before, axis=0))
                    o = _dot(a, vb) + _gla_diag(inter)
                    if first:
                        o_ref[rows, cols] = o
                    else:
                        o_ref[rows, cols] += o
                return carry
            return step

        lax.fori_loop(0, n_groups // 2, make_step(True), 0)
        lax.fori_loop(n_groups // 2, n_groups, make_step(False), 0)
        for slot in range(2):
            for ci in range(len(chains)):
                states_out(slot, ci, 0).wait()

    w = 128 * hp
    col = lambda base: pl.BlockSpec((s, w), lambda h: (0, base // w + h))
    return pl.pallas_call(
        body, name="fwd_gla", grid=(HG_HEADS // hp,),
        in_specs=[col(Z_HQ), col(Z_HFF), col(Z_HFB), col(Z_HI), pl.BlockSpec((4, w), lambda h: (0, h))],
        out_specs=[pl.BlockSpec((s, w), lambda h: (0, h)), pl.BlockSpec(memory_space=pl.ANY), pl.BlockSpec(memory_space=pl.ANY)],
        out_shape=[jax.ShapeDtypeStruct((s, HG_HEADS * 128), f32), jax.ShapeDtypeStruct((2, s, HG_HEADS * 128), f32),
                   jax.ShapeDtypeStruct((HG_HEADS, 2, s // CHUNK, 128, 128), f32)],
        scratch_shapes=[pltpu.VMEM((len(chains), 128, 128), f32), pltpu.VMEM((2, len(chains), GLA_GROUP, 128, 128), f32),
                        pltpu.VMEM((2, len(chains), GLA_ROWS, 128), f32), pltpu.SemaphoreType.DMA((2, 2 * len(chains)))],
        compiler_params=_cp(("parallel",), VMEM_LIMIT),
    )(z, z, z, z, lb4)


def _hg_out(o, hg, g_hgo):
    outs, ons, rs = [], [], []
    for h in range(HG_HEADS):
        oh = o[:, 128 * h:128 * (h + 1)]
        on, r = _rms_fwd(oh, g_hgo[:, 128 * h:128 * (h + 1)], 128)
        ons.append(on)
        rs.append(r)
    on = jnp.concatenate(ons, axis=-1)
    sg = _sigmoid(hg)
    return on * (hg * sg), on, rs, sg


def _fwd_mix(a, o, z, g_hgo, x, w_o, tm):
    s, d = x.shape

    def body(a_ref, o_ref, hg_ref, g_ref, x_ref, w_ref, x2_ref, cat_ref):
        r, _, _, _ = _hg_out(o_ref[...], hg_ref[...], g_ref[...])
        cat = jnp.concatenate([a_ref[...], r.astype(bf16)], axis=-1)
        cat_ref[...] = cat
        x2_ref[...] = x_ref[...] + _dot(cat, w_ref[...])

    row512 = pl.BlockSpec((tm, 512), lambda i: (i, 0))
    rowd = pl.BlockSpec((tm, d), lambda i: (i, 0))
    return pl.pallas_call(
        body, name="fwd_mix", grid=(s // tm,),
        in_specs=[row512, row512, pl.BlockSpec((tm, 512), lambda i: (i, Z_HG // 512)), _const_spec((1, 512)), rowd,
                  _const_spec((d, d))],
        out_specs=[rowd, rowd],
        out_shape=[jax.ShapeDtypeStruct((s, d), f32), jax.ShapeDtypeStruct((s, d), bf16)],
        compiler_params=_cp(("parallel",), VMEM_LIMIT),
    )(a, o, z, g_hgo, x, w_o)


def _fwd_ffn(x2, g_ffn, w_gate, w_up, w_down, tm):
    s, d = x2.shape

    def body(x_ref, g_ref, wg_ref, wu_ref, wd_ref, x3_ref, gp_ref, up_ref):
        x = x_ref[...]
        h, _ = _rms_fwd(x, g_ref[...], d)
        hb = h.astype(bf16)
        gp = _dot_nt(hb, wg_ref[...])
        up = _dot_nt(hb, wu_ref[...])
        gp_ref[...] = gp.astype(bf16)
        up_ref[...] = up.astype(bf16)
        act = (gp * _sigmoid(gp) * up).astype(bf16)
        x3_ref[...] = x + _dot(act, wd_ref[...])

    rowd = pl.BlockSpec((tm, d), lambda i: (i, 0))
    rowf = pl.BlockSpec((tm, D_FF), lambda i: (i, 0))
    return pl.pallas_call(
        body, name="fwd_ffn", grid=(s // tm,),
        in_specs=[rowd, _const_spec((1, d)), _const_spec((D_FF, d)), _const_spec((D_FF, d)), _const_spec((D_FF, d))],
        out_specs=[rowd, rowf, rowf],
        out_shape=[jax.ShapeDtypeStruct((s, d), f32), jax.ShapeDtypeStruct((s, D_FF), bf16),
                   jax.ShapeDtypeStruct((s, D_FF), bf16)],
        compiler_params=_cp(("parallel",), VMEM_LIMIT),
    )(x2, g_ffn, w_gate, w_up, w_down)


def _ple_loss_fwd_bwd(x3, g_ple, w_pg, p, w_pp, target, tm):
    s, d = x3.shape
    cols = BIG["w_ple_proj"][1]

    def body(x_ref, g_ref, wg_ref, p_ref, wp_ref, t_ref, dx_ref, gwg_ref, gwp_ref, dg_ref, loss_ref, acc_ref, accp_ref):
        @pl.when(pl.program_id(0) == 0)
        def _():
            for r_ in (acc_ref, accp_ref, dg_ref, loss_ref):
                r_[...] = jnp.zeros_like(r_)

        x = x_ref[...]
        g = g_ref[...]
        h, r = _rms_fwd(x, g, d)
        hb = h.astype(bf16)
        pb = p_ref[...].astype(bf16)
        gate = _sigmoid(_dot(hb, wg_ref[...]))
        pp = jnp.concatenate([_dot(pb, wp_ref[j]) for j in range(N_DEV)], axis=-1)
        e = x + gate * pp - t_ref[...]
        loss_ref[...] += 0.5 * jnp.sum(e * e) * (1.0 / d)
        dy = e * (1.0 / d)
        dpre = (dy * pp * gate * (1.0 - gate)).astype(bf16)
        dx, dgx = _rms_bwd(_dot_nt(dpre, wg_ref[...]), x, r, g, d)
        dx_ref[...] = dy + dx
        dg_ref[...] += jnp.sum(dgx, axis=0, keepdims=True)
        acc_ref[...] += _dot_tn(hb, dpre)
        accp_ref[...] += _dot_tn(pb, (dy * gate).astype(bf16))

        @pl.when(pl.program_id(0) == s // tm - 1)
        def _():
            gwg_ref[...] = acc_ref[...].astype(bf16)
            for j in range(N_DEV):
                gwp_ref[j] = accp_ref[:, cols * j:cols * (j + 1)].astype(bf16)

    rowd = pl.BlockSpec((tm, d), lambda i: (i, 0))
    return pl.pallas_call(
        body, name="ple_loss_fwd_bwd", grid=(s // tm,),
        in_specs=[rowd, _const_spec((1, d)), _const_spec((d, d)), pl.BlockSpec((tm, PLE_DIM), lambda i: (i, 0)),
                  _const_spec((N_DEV, PLE_DIM, cols)), rowd],
        out_specs=[rowd, _acc_spec((d, d)), _acc_spec((N_DEV, PLE_DIM, cols)), _acc_spec((1, d)), _acc_spec((8, 128))],
        out_shape=[jax.ShapeDtypeStruct((s, d), f32), jax.ShapeDtypeStruct((d, d), bf16),
                   jax.ShapeDtypeStruct((N_DEV, PLE_DIM, cols), bf16), jax.ShapeDtypeStruct((1, d), f32), jax.ShapeDtypeStruct((8, 128), f32)],
        scratch_shapes=[pltpu.VMEM((d, d), f32), pltpu.VMEM((PLE_DIM, d), f32)],
        compiler_params=_cp(("arbitrary",), VMEM_LIMIT),
    )(x3, g_ple, w_pg, p, w_pp, target)


def _bwd_ffn_hidden(d3, gp, up, w_down, tm, tf):
    s, d = d3.shape
    n_i = s // tm

    def body(d3_ref, gp_ref, up_ref, wd_ref, dgp_ref, dup_ref, gw_ref, acc_ref):
        i = pl.program_id(1)

        @pl.when(i == 0)
        def _():
            acc_ref[...] = jnp.zeros_like(acc_ref)

        gp, up = gp_ref[...].astype(f32), up_ref[...].astype(f32)
        sg = _sigmoid(gp)
        silu = gp * sg
        d3b = d3_ref[...].astype(bf16)
        acc_ref[...] += _dot_tn((silu * up).astype(bf16), d3b)
        dact = _dot_nt(d3b, wd_ref[...])
        dgp_ref[...] = (dact * up * (sg * (1.0 + gp * (1.0 - sg)))).astype(bf16)
        dup_ref[...] = (dact * silu).astype(bf16)

        @pl.when(i == n_i - 1)
        def _():
            gw_ref[...] = acc_ref[...].astype(bf16)

    rowf = pl.BlockSpec((tm, tf), lambda f, i: (i, f))
    wrow = pl.BlockSpec((tf, d), lambda f, i: (f, 0))
    return pl.pallas_call(
        body, name="bwd_ffn_hidden", grid=(D_FF // tf, n_i),
        in_specs=[pl.BlockSpec((tm, d), lambda f, i: (i, 0)), rowf, rowf, wrow],
        out_specs=[rowf, rowf, wrow],
        out_shape=[jax.ShapeDtypeStruct((s, D_FF), bf16)] * 2 + [jax.ShapeDtypeStruct((D_FF, d), bf16)],
        scratch_shapes=[pltpu.VMEM((tf, d), f32)],
        compiler_params=_cp(("parallel", "arbitrary"), VMEM_LIMIT),
    )(d3, gp, up, w_down)


def _bwd_ffn_in(d3, x2, dgp, dup, g_ffn, w_gate, w_up, tm):
    s, d = x2.shape

    def body(d3_ref, x_ref, dgp_ref, dup_ref, g_ref, wg_ref, wu_ref, d2_ref, h_ref, dg_ref):
        @pl.when(pl.program_id(0) == 0)
        def _():
            dg_ref[...] = jnp.zeros_like(dg_ref)

        x, g = x_ref[...], g_ref[...]
        dh = _dot(dgp_ref[...], wg_ref[...]) + _dot(dup_ref[...], wu_ref[...])
        h, r = _rms_fwd(x, g, d)
        h_ref[...] = h.astype(bf16)
        dx, dgx = _rms_bwd(dh, x, r, g, d)
        d2_ref[...] = d3_ref[...] + dx
        dg_ref[...] += jnp.sum(dgx, axis=0, keepdims=True)

    rowd = pl.BlockSpec((tm, d), lambda i: (i, 0))
    rowf = pl.BlockSpec((tm, D_FF), lambda i: (i, 0))
    return pl.pallas_call(
        body, name="bwd_ffn_in", grid=(s // tm,),
        in_specs=[rowd, rowd, rowf, rowf, _const_spec((1, d)), _const_spec((D_FF, d)), _const_spec((D_FF, d))],
        out_specs=[rowd, rowd, _acc_spec((1, d))],
        out_shape=[jax.ShapeDtypeStruct((s, d), f32), jax.ShapeDtypeStruct((s, d), bf16), jax.ShapeDtypeStruct((1, d), f32)],
        compiler_params=_cp(("arbitrary",), VMEM_LIMIT),
    )(d3, x2, dgp, dup, g_ffn, w_gate, w_up)


def _bwd_mix(d2, w_o, o, z, g_hgo, cat, tm):
    s, d = d2.shape

    def body(d2_ref, w_ref, o_ref, hg_ref, g_ref, cat_ref, da_ref, do_ref, dhg_ref, dg_ref, gw_ref, acc_ref):
        @pl.when(pl.program_id(0) == 0)
        def _():
            dg_ref[...] = jnp.zeros_like(dg_ref)
            acc_ref[...] = jnp.zeros_like(acc_ref)

        d2b = d2_ref[...].astype(bf16)
        acc_ref[...] += _dot_tn(cat_ref[...], d2b)

        @pl.when(pl.program_id(0) == s // tm - 1)
        def _():
            gw_ref[...] = acc_ref[...].astype(bf16)
        dcat = _dot_nt(d2b, w_ref[...])
        da_ref[...] = dcat[:, 0:512].astype(bf16)
        dr = dcat[:, 512:1024]
        o, hg, g = o_ref[...], hg_ref[...], g_ref[...]
        _, on, rs, sg = _hg_out(o, hg, g)
        dhg_ref[...] = (dr * on * (sg * (1.0 + hg * (1.0 - sg)))).astype(bf16)
        don = dr * (hg * sg)
        dgs = []
        for h in range(HG_HEADS):
            cols = slice(128 * h, 128 * (h + 1))
            dx, dgx = _rms_bwd(don[:, cols], o[:, cols], rs[h], g[:, cols], 128)
            do_ref[:, cols] = dx
            dgs.append(jnp.sum(dgx, axis=0, keepdims=True))
        dg_ref[...] += jnp.concatenate(dgs, axis=-1)

    row512 = pl.BlockSpec((tm, 512), lambda i: (i, 0))
    return pl.pallas_call(
        body, name="bwd_mix", grid=(s // tm,),
        in_specs=[pl.BlockSpec((tm, d), lambda i: (i, 0)), _const_spec((d, d)), row512,
                  pl.BlockSpec((tm, 512), lambda i: (i, Z_HG // 512)), _const_spec((1, 512)), pl.BlockSpec((tm, d), lambda i: (i, 0))],
        out_specs=[row512, row512, row512, _acc_spec((1, 512)), _acc_spec((d, d))],
        out_shape=[jax.ShapeDtypeStruct((s, 512), bf16), jax.ShapeDtypeStruct((s, 512), f32), jax.ShapeDtypeStruct((s, 512), bf16),
                   jax.ShapeDtypeStruct((1, 512), f32), jax.ShapeDtypeStruct((d, d), bf16)],
        scratch_shapes=[pltpu.VMEM((d, d), f32)],
        compiler_params=_cp(("arbitrary",), VMEM_LIMIT),
    )(d2, w_o, o, z, g_hgo, cat)


def _bwd_gla(z, lb4, do, b_fwd, states):
    s = z.shape[0]
    n_chunks = s // CHUNK
    n_groups = s // GLA_ROWS
    assert n_groups % 2 == 0

    def body(hq_ref, hff_ref, hfb_ref, hi_ref, lb_ref, do_ref, b_all, st_all, dhq_ref, dhff_ref, dhfb_ref, dhi_ref, dlb_ref,
             dst_ref, dq_acc, dv_acc, dlow_ref):
        dirs = (False, True)
        masks = [_gla_masks(rev) for rev in dirs]
        lowers = [_sigmoid(lb_ref[int(rev):int(rev) + 1, :] - lb_ref[2 + int(rev):3 + int(rev), :]) for rev in dirs]
        hf_refs, dhf_refs = (hff_ref, hfb_ref), (dhff_ref, dhfb_ref)

        dst_ref[...] = jnp.zeros_like(dst_ref)
        dlow_ref[...] = jnp.zeros_like(dlow_ref)

        def make_bwd_step(first):
            def bwd_step(j, carry):
                n = n_groups - 1 - j
                for d, rev in enumerate(dirs):
                    maskf, _, tri_t, row_masks = masks[d]
                    lower = lowers[d]
                    rows, chunk0 = _gla_rows(n, n_groups, rev)
                    hq, hf = hq_ref[rows, :], hf_refs[d][rows, :]
                    q, k, _, f, sg = _gla_gates(hq, hf, lower)
                    v = hi_ref[rows, :]
                    dout = do_ref[rows, :]
                    b = b_all[d, rows, :]
                    b_last3, b_mid3 = _gla_last_mid(b, rev)
                    b_last, b_mid = _gla_per_row(b_last3), _gla_per_row(b_mid3)
                    e1, e2, e3, e4 = jnp.exp(b - b_mid), jnp.exp(b_mid - b), jnp.exp(b_last - b), jnp.exp(b)
                    decay3 = jnp.exp(b_last3)
                    qi, ki, kt, qt = q * e1, k * e2, k * e3, q * e4
                    qib, kib, ktb = qi.astype(bf16), ki.astype(bf16), kt.astype(bf16)
                    vb, dob = v.astype(bf16), dout.astype(bf16)
                    a = (_dot_nt(qib, kib) * maskf).astype(bf16)
                    da = (_dot_nt(dob, vb) * maskf).astype(bf16)
                    dqi = _dot(da, kib)
                    dki = _dot_tn(da, qib)
                    into_state = _dot_tn(dob, _gla_block_diag(qt, row_masks))
                    dst = dst_ref[d]
                    sts, dsts, ddecay = [None] * GLA_GROUP, [None] * GLA_GROUP, [None] * GLA_GROUP
                    for c in reversed(_gla_scan_order(rev)):
                        sts[c] = st_all[d, chunk0 + c]
                        dsts[c] = dst.astype(bf16)
                        ddecay[c] = jnp.sum(dst * sts[c], axis=0, keepdims=True)[None]
                        dst = dst * decay3[c] + into_state[:, 128 * c:128 * (c + 1)]
                    dst_ref[d] = dst
                    dv = _dot_tn(a, dob) + _gla_diag(_dot_nt(ktb, jnp.concatenate(dsts, axis=0)))
                    dqt = _gla_diag(_dot(dob, jnp.concatenate([x.astype(bf16) for x in sts], axis=-1)))
                    dkt = _gla_diag(_dot(vb, jnp.concatenate(dsts, axis=-1)))
                    dq = dqi * e1 + dqt * e4
                    dk = dki * e2 + dkt * e3
                    db = dqi * qi - dki * ki + dqt * qt - dkt * kt
                    dlast3 = (jnp.sum((dkt * kt).reshape(GLA_GROUP, CHUNK, 128), axis=1, keepdims=True)
                              + jnp.concatenate(ddecay, axis=0) * decay3)
                    dlogf = _tri_sum(tri_t, db) + _gla_per_row(dlast3)
                    df = dlogf / f - dk
                    dhf_refs[d][rows, :] = (df * (1.0 - lower) * sg * (1.0 - sg)).astype(bf16)
                    dlow_ref[d:d + 1, :] += jnp.sum(df * (1.0 - sg), axis=0, keepdims=True)
                    sq = _sigmoid(hq)
                    dhq = dq * (sq * (1.0 + hq * (1.0 - sq)))
                    if first:
                        dq_acc[rows, :] = dhq
                        dv_acc[rows, :] = dv
                    else:
                        dhq_ref[rows, :] = (dq_acc[rows, :] + dhq).astype(bf16)
                        dhi_ref[rows, :] = (dv_acc[rows, :] + dv).astype(bf16)
                return carry
            return bwd_step

        lax.fori_loop(0, n_groups // 2, make_bwd_step(True), 0, unroll=2)
        lax.fori_loop(n_groups // 2, n_groups, make_bwd_step(False), 0, unroll=2)

        for d in range(2):
            dl = dlow_ref[d:d + 1, :] * lowers[d] * (1.0 - lowers[d])
            dlb_ref[d:d + 1, :] = dl
            dlb_ref[2 + d:3 + d, :] = -dl

    col = lambda base: pl.BlockSpec((s, 128), lambda h: (0, base // 128 + h))
    return pl.pallas_call(
        body, name="bwd_gla", grid=(HG_HEADS,),
        in_specs=[col(Z_HQ), col(Z_HFF), col(Z_HFB), col(Z_HI), pl.BlockSpec((4, 128), lambda h: (0, h)), col(0),
                  pl.BlockSpec((2, s, 128), lambda h: (0, 0, h)),
                  pl.BlockSpec((None, 2, n_chunks, 128, 128), lambda h: (h, 0, 0, 0, 0), pipeline_mode=pl.Buffered(1))],
        out_specs=[col(0), col(0), col(0), col(0), pl.BlockSpec((4, 128), lambda h: (0, h))],
        out_shape=[jax.ShapeDtypeStruct((s, 512), bf16)] * 4 + [jax.ShapeDtypeStruct((4, 512), f32)],
        scratch_shapes=[pltpu.VMEM((2, 128, 128), f32), pltpu.VMEM((s, 128), f32), pltpu.VMEM((s, 128), f32),
                        pltpu.VMEM((2, 128), f32)],
        compiler_params=_cp(("parallel",), VMEM_LIMIT),
    )(z, z, z, z, lb4, do, b_fwd, states)


def _bwd_attn(q, k, v, da, a32, tq, after):
    hh, s, _ = q.shape

    n_sub = max(1, tq // ATTN_SUB_ROWS)

    def body(q_ref, k_ref, v_ref, do_ref, o_ref, after_ref, dq_ref, dk_ref, dv_ref, p_all, ds_all, dol_ref, dkt_ref, dvt_ref):
        @pl.when(pl.program_id(1) == 0)
        def _():
            dkt_ref[...] = jnp.zeros_like(dkt_ref)
            dvt_ref[...] = jnp.zeros_like(dvt_ref)

        kb, vb = k_ref[...], v_ref[...]
        for t in range(n_sub):
            rows = slice(t * (tq // n_sub), (t + 1) * (tq // n_sub))
            sc = _dot_nt(q_ref[rows, :], kb)
            p = jnp.exp2((sc - jnp.max(sc, axis=-1, keepdims=True)) * (ATTN_SCALE * LOG2_E))
            inv_l = 1.0 / jnp.sum(p, axis=-1, keepdims=True)
            p_all[rows, :] = p.astype(bf16)
            dob = do_ref[rows, :]
            dof = dob.astype(f32)
            delta = jnp.sum(dof * o_ref[rows, :], axis=-1, keepdims=True)
            ds_all[rows, :] = p_all[rows, :] * ((_dot_nt(dob, vb) - delta) * inv_l).astype(bf16)
            dq_ref[rows, :] = _dot(ds_all[rows, :], kb) * ATTN_SCALE
            dol_ref[rows, :] = (dof * inv_l).astype(bf16)
        dkt_ref[...] += _dot_tn(q_ref[...], ds_all[...])
        dvt_ref[...] += _dot_tn(dol_ref[...], p_all[...])

        @pl.when(pl.program_id(1) == s // tq - 1)
        def _():
            dk_ref[...] = dkt_ref[...].T * ATTN_SCALE
            dv_ref[...] = dvt_ref[...].T

    return pl.pallas_call(
        body, name="bwd_attn", grid=(hh, s // tq),
        in_specs=[pl.BlockSpec((None, tq, QK_PAD), lambda h, i: (h, i, 0)),
                  pl.BlockSpec((None, s, QK_PAD), lambda h, i: (h, 0, 0)),
                  pl.BlockSpec((None, s, V_HEAD), lambda h, i: (h, 0, 0)),
                  pl.BlockSpec((tq, V_HEAD), lambda h, i: (i, h)), pl.BlockSpec((tq, V_HEAD), lambda h, i: (i, h)),
                  pl.BlockSpec(memory_space=pl.ANY)],
        out_specs=[pl.BlockSpec((None, tq, QK_PAD), lambda h, i: (h, i, 0)),
                   pl.BlockSpec((None, s, QK_PAD), lambda h, i: (h, 0, 0)),
                   pl.BlockSpec((None, s, V_HEAD), lambda h, i: (h, 0, 0))],
        out_shape=[jax.ShapeDtypeStruct((hh, s, QK_PAD), f32), jax.ShapeDtypeStruct((hh, s, QK_PAD), f32),
                   jax.ShapeDtypeStruct((hh, s, V_HEAD), f32)],
        scratch_shapes=[pltpu.VMEM((tq, s), bf16), pltpu.VMEM((tq, s), bf16), pltpu.VMEM((tq, V_HEAD), bf16),
                        pltpu.VMEM((QK_PAD, s), f32), pltpu.VMEM((V_HEAD, s), f32)],
        compiler_params=_cp(("parallel", "arbitrary"), VMEM_LIMIT),
    )(q, k, v, da, a32, after)


def _bwd_mla_proj(z, dq, dk, dv, cosb, sina, sinb, g_qa, g_kva, wqb, wkvb, g_qn, g_kn, tm):
    s = z.shape[0]
    hh = MLA_HEADS

    def body(cq_ref, ckv_ref, kr_ref, dq_ref, dk_ref, dv_ref, c_ref, sa_ref, sb_ref, gqa_ref, gkva_ref, wqb_ref, wkvb_ref,
             gqn_ref, gkn_ref, dz_ref, gwqb_ref, gwkvb_ref, dgqa_ref, dgkva_ref, dgqn_ref, dgkn_ref, dq0_ref, dkv0_ref):
        @pl.when(pl.program_id(0) == 0)
        def _():
            for r in (gwqb_ref, gwkvb_ref, dgqa_ref, dgkva_ref, dgqn_ref, dgkn_ref):
                r[...] = jnp.zeros_like(r)

        cq, ckv, kr = cq_ref[...], ckv_ref[...], kr_ref[...]
        gqa, gkva, gqn, gkn = gqa_ref[...], gkva_ref[...], gqn_ref[...], gkn_ref[...]
        cqn_b, rq, ckvn_b, rkv, q0, kv0 = _mla_qk_fwd(cq, ckv, gqa, gkva, wqb_ref[...], wkvb_ref[...])
        c, sa, sb = c_ref[...], -sa_ref[...], -sb_ref[...]
        kr_sq = jnp.sum(kr * kr, axis=-1, keepdims=True)
        dkr = jnp.zeros_like(kr)
        dgqn = jnp.zeros((1, QK_PAD), f32)
        dgkn = jnp.zeros((1, QK_PAD), f32)
        for h in range(hh):
            qh = q0[:, QK_PAD * h:QK_PAD * (h + 1)]
            rh = lax.rsqrt(jnp.sum(qh * qh, axis=-1, keepdims=True) * (1.0 / QK_HEAD) + EPS)
            dqh = dq_ref[h]
            dqn = jnp.concatenate([dqh[:, 0:128], _rope(dqh[:, 128:256], c, sa, sb)], axis=-1)
            dq0h, dgx = _rms_bwd(dqn, qh, rh, gqn, QK_HEAD)
            dq0_ref[:, QK_PAD * h:QK_PAD * (h + 1)] = dq0h.astype(bf16)
            dgqn = dgqn + jnp.sum(dgx, axis=0, keepdims=True)

            kn_ = kv0[:, 256 * h:256 * h + 128]
            k0 = jnp.concatenate([kn_, kr], axis=-1)
            rk = lax.rsqrt((jnp.sum(kn_ * kn_, axis=-1, keepdims=True) + kr_sq) * (1.0 / QK_HEAD) + EPS)
            dkh = dk_ref[h]
            dkn = jnp.concatenate([dkh[:, 0:128], _rope(dkh[:, 128:256], c, sa, sb)], axis=-1)
            dk0, dgx = _rms_bwd(dkn, k0, rk, gkn, QK_HEAD)
            dgkn = dgkn + jnp.sum(dgx, axis=0, keepdims=True)
            dkv0_ref[:, 256 * h:256 * h + 128] = dk0[:, 0:128].astype(bf16)
            dkv0_ref[:, 256 * h + 128:256 * h + 256] = dv_ref[h].astype(bf16)
            dkr = dkr + dk0[:, 128:256]
        dgqn_ref[...] += dgqn
        dgkn_ref[...] += dgkn
        gwqb_ref[...] += _dot_tn(cqn_b, dq0_ref[...])
        gwkvb_ref[...] += _dot_tn(ckvn_b, dkv0_ref[...])
        dcq, dgx = _rms_bwd(_dot_nt(dq0_ref[...], wqb_ref[...]), cq, rq, gqa, Q_LORA)
        dgqa_ref[...] += jnp.sum(dgx, axis=0, keepdims=True)
        dckv, dgx = _rms_bwd(_dot_nt(dkv0_ref[...], wkvb_ref[...]), ckv, rkv, gkva, KV_LORA)
        dgkva_ref[...] += jnp.sum(dgx, axis=0, keepdims=True)
        dz_ref[:, 0:256] = dcq.astype(bf16)
        dz_ref[:, 256:512] = dckv.astype(bf16)
        dz_ref[:, 512:640] = dkr.astype(bf16)

    row128 = pl.BlockSpec((tm, 128), lambda i: (i, 0))
    hd = lambda w: pl.BlockSpec((hh, tm, w), lambda i: (0, i, 0))
    return pl.pallas_call(
        body, name="bwd_mla_proj", grid=(s // tm,),
        in_specs=[pl.BlockSpec((tm, 256), lambda i: (i, Z_CQ // 256)), pl.BlockSpec((tm, 256), lambda i: (i, Z_CKV // 256)),
                  pl.BlockSpec((tm, 128), lambda i: (i, Z_KR // 128)), hd(QK_PAD), hd(QK_PAD), hd(V_HEAD),
                  row128, row128, row128,
                  _const_spec((1, 256)), _const_spec((1, 256)), _const_spec((256, 1024)), _const_spec((256, 1024)),
                  _const_spec((1, 256)), _const_spec((1, 256))],
        out_specs=[pl.BlockSpec((tm, 640), lambda i: (i, 0)), _acc_spec((256, 1024)), _acc_spec((256, 1024)),
                   _acc_spec((1, 256)), _acc_spec((1, 256)), _acc_spec((1, 256)), _acc_spec((1, 256))],
        out_shape=[jax.ShapeDtypeStruct((s, 640), bf16), jax.ShapeDtypeStruct((256, 1024), f32), jax.ShapeDtypeStruct((256, 1024), f32)]
        + [jax.ShapeDtypeStruct((1, 256), f32)] * 4,
        scratch_shapes=[pltpu.VMEM((tm, 1024), bf16), pltpu.VMEM((tm, 1024), bf16)],
        compiler_params=_cp(("arbitrary",), VMEM_LIMIT),
    )(z, z, z, dq, dk, dv, cosb, sina, sinb, g_qa, g_kva, wqb, wkvb, g_qn, g_kn)


def _bwd_in(segments, wz, x, g_mix, d2, tm):
    s, d = x.shape
    n_seg = len(segments)

    def body(*refs):
        dz_refs, w_refs = refs[:n_seg], refs[n_seg:2 * n_seg]
        x_ref, g_ref, d2_ref, gx_ref, dg_ref = refs[2 * n_seg:]

        @pl.when(pl.program_id(0) == 0)
        def _():
            dg_ref[...] = jnp.zeros_like(dg_ref)

        dh = _dot(dz_refs[0][...], w_refs[0][...])
        for a_ref, w_ref in zip(dz_refs[1:], w_refs[1:]):
            dh = dh + _dot(a_ref[...], w_ref[...])
        x, g = x_ref[...], g_ref[...]
        r = lax.rsqrt(jnp.sum(x * x, axis=-1, keepdims=True) * (1.0 / d) + EPS)
        dx, dgx = _rms_bwd(dh, x, r, g, d)
        gx_ref[...] = d2_ref[...] + dx
        dg_ref[...] += jnp.sum(dgx, axis=0, keepdims=True)

    rowd = pl.BlockSpec((tm, d), lambda i: (i, 0))
    dz_specs = [pl.BlockSpec((tm, w), functools.partial(lambda i, j: (i, j), j=ja)) for _, w, ja, _ in segments]
    w_specs = [pl.BlockSpec((w, d), functools.partial(lambda i, j: (j, 0), j=jw), pipeline_mode=pl.Buffered(1))
               for _, w, _, jw in segments]
    return pl.pallas_call(
        body, name="bwd_in", grid=(s // tm,),
        in_specs=dz_specs + w_specs + [rowd, _const_spec((1, d)), rowd],
        out_specs=[rowd, _acc_spec((1, d))],
        out_shape=[jax.ShapeDtypeStruct((s, d), f32), jax.ShapeDtypeStruct((1, d), f32)],
        compiler_params=_cp(("arbitrary",), VMEM_LIMIT),
    )(*[a for a, _, _, _ in segments], *([wz] * n_seg), x, g_mix, d2)


def _mm_tn_many(a, bs, name, tm, transposed=False):
    kk, m = a.shape
    n_b = len(bs)
    tk = min(1024, kk)
    n_k = kk // tk

    def body(a_ref, *refs):
        b_refs, o_refs, acc_refs = refs[:n_b], refs[n_b:2 * n_b], refs[2 * n_b:]

        @pl.when(pl.program_id(1) == 0)
        def _():
            for acc in acc_refs:
                acc[...] = jnp.zeros_like(acc)
        a_blk = a_ref[...].astype(bf16)
        for b_ref, acc in zip(b_refs, acc_refs):
            acc[...] += _dot_tn(a_blk, b_ref[...].astype(bf16))

        @pl.when(pl.program_id(1) == n_k - 1)
        def _():
            for o_ref, acc in zip(o_refs, acc_refs):
                o_ref[...] = (acc[...].T if transposed else acc[...]).astype(bf16)

    if transposed:
        out_specs = [pl.BlockSpec((b.shape[1], tm), lambda i, k: (0, i)) for b in bs]
        out_shape = [jax.ShapeDtypeStruct((b.shape[1], m), bf16) for b in bs]
    else:
        out_specs = [pl.BlockSpec((tm, b.shape[1]), lambda i, k: (i, 0)) for b in bs]
        out_shape = [jax.ShapeDtypeStruct((m, b.shape[1]), bf16) for b in bs]
    return pl.pallas_call(
        body, name=name, grid=(m // tm, n_k),
        in_specs=[pl.BlockSpec((tk, tm), lambda i, k: (k, i))] + [pl.BlockSpec((tk, b.shape[1]), lambda i, k: (k, 0)) for b in bs],
        out_specs=out_specs,
        out_shape=out_shape,
        scratch_shapes=[pltpu.VMEM((tm, b.shape[1]), f32) for b in bs],
        compiler_params=_cp(("parallel", "arbitrary"), VMEM_LIMIT),
    )(a, *bs)


def _rope_tables(positions):
    inv_freq = ROPE_THETA ** (-jnp.arange(0, QK_ROPE, 2, dtype=f32) / QK_ROPE)
    ang = positions.astype(f32)[:, None] * inv_freq
    cos, sin = jnp.cos(ang), jnp.sin(ang)
    zero = jnp.zeros_like(cos)
    return (jnp.concatenate([cos, cos, zero, zero], axis=1), jnp.concatenate([zero, sin, zero, zero], axis=1),
            jnp.concatenate([-sin, zero, zero, zero], axis=1))


def _pad256(g):
    return jnp.pad(g.reshape(1, QK_HEAD), ((0, 0), (0, QK_PAD - QK_HEAD)))


RELAYOUT_BLOCKS = 8
FIRST = ("w_in", "w_qb", "w_kvb", "lb_param")
SECOND = ("w_o", "w_gate", "w_up", "w_down", "w_ple_gate", "w_ple_proj")
ROW_SHARDED = ("w_o", "w_down", "w_ple_gate")


def _col_moves(j):
    width = BIG["w_in"][0]
    lo = width * j
    w_in = [(max(lo, a) - lo, min(lo + width, b) - lo, d + max(lo, a) - a)
            for a, b, d in Z_SEGMENTS if max(lo, a) < min(lo + width, b)]
    head, half = divmod(j, 2)
    whole = lambda n: [(0, BIG[n][1], BIG[n][1] * j)]
    return {"w_in": w_in, "w_qb": [(0, 96, QK_PAD * head + 96 * half)], "w_kvb": whole("w_kvb"),
            "w_ple_proj": whole("w_ple_proj"), "lb_param": whole("lb_param")}


def _kernel_shape(name):
    rows, cols = BIG[name]
    if name == "w_in":
        return (Z_W, cols)
    return (rows, MLA_HEADS * QK_PAD if name == "w_qb" else N_DEV * cols)


def _relayout_specs(names, by_dev):
    specs = []
    for n in names:
        rows, cols = BIG[n]
        if n == "lb_param":
            specs.append(_acc_spec((N_DEV, rows, cols) if by_dev else _kernel_shape(n)))
        elif n == "w_in":
            cb = cols // RELAYOUT_BLOCKS
            specs.append(pl.BlockSpec((N_DEV, rows, cb), lambda i: (0, 0, i)) if by_dev else pl.BlockSpec((Z_W, cb), lambda i: (0, i)))
        elif by_dev:
            specs.append(pl.BlockSpec((N_DEV, rows // RELAYOUT_BLOCKS, cols), lambda i: (0, i, 0)))
        else:
            specs.append(pl.BlockSpec((rows // RELAYOUT_BLOCKS, _kernel_shape(n)[1]), lambda i: (i, 0)))
    return specs


def _weights_in(gathered, names, name):
    n = len(names)

    def body(*refs):
        ins, outs = dict(zip(names, refs[:n])), dict(zip(names, refs[n:]))
        if "w_in" in outs:
            outs["w_in"][Z_KR + QK_ROPE:Z_W, :] = jnp.zeros((Z_W - Z_KR - QK_ROPE, outs["w_in"].shape[1]), bf16)
        if "w_qb" in outs:
            for h in range(MLA_HEADS):
                outs["w_qb"][:, QK_PAD * h + QK_HEAD:QK_PAD * (h + 1)] = jnp.zeros((outs["w_qb"].shape[0], QK_PAD - QK_HEAD), bf16)
        for j in range(N_DEV):
            for wn, moves in _col_moves(j).items():
                if wn in outs:
                    for s0, s1, d0 in moves:
                        if wn == "w_in":
                            outs[wn][d0:d0 + s1 - s0, :] = ins[wn][j, s0:s1, :]
                        else:
                            outs[wn][:, d0:d0 + s1 - s0] = ins[wn][j, :, s0:s1]

    outs = pl.pallas_call(
        body, name=name, grid=(RELAYOUT_BLOCKS,), in_specs=_relayout_specs(names, True), out_specs=_relayout_specs(names, False),
        out_shape=[jax.ShapeDtypeStruct(_kernel_shape(wn), gathered[wn].dtype) for wn in names],
        compiler_params=_cp(("arbitrary",), VMEM_LIMIT),
    )(*[gathered[wn] for wn in names])
    return dict(zip(names, outs))


def _grads_out(sources, names, name):
    pieces = [(wn, start, arr) for wn in names for start, arr in sources[wn]]
    n_in = len(pieces)

    def body(*refs):
        outs = dict(zip(names, refs[n_in:]))

        def cols(wn, c0, c1):
            for (pn, start, arr), ref in zip(pieces, refs[:n_in]):
                if pn == wn and start <= c0 and c1 <= start + arr.shape[0 if wn == "w_in" else 1]:
                    return ref[c0 - start:c1 - start, :] if wn == "w_in" else ref[:, c0 - start:c1 - start]

        for j in range(N_DEV):
            for wn, moves in _col_moves(j).items():
                if wn in outs:
                    for s0, s1, d0 in moves:
                        if wn == "w_in":
                            outs[wn][j, s0:s1, :] = cols(wn, d0, d0 + s1 - s0).astype(bf16)
                        else:
                            outs[wn][j, :, s0:s1] = cols(wn, d0, d0 + s1 - s0).astype(bf16)

    def in_spec(wn, arr):
        if wn == "lb_param":
            return _acc_spec(arr.shape)
        if wn == "w_in":
            return pl.BlockSpec((arr.shape[0], arr.shape[1] // RELAYOUT_BLOCKS), lambda i: (0, i))
        return pl.BlockSpec((arr.shape[0] // RELAYOUT_BLOCKS, arr.shape[1]), lambda i: (i, 0))

    in_specs = [in_spec(wn, arr) for wn, _, arr in pieces]
    outs = pl.pallas_call(
        body, name=name, grid=(RELAYOUT_BLOCKS,), in_specs=in_specs, out_specs=_relayout_specs(names, True),
        out_shape=[jax.ShapeDtypeStruct((N_DEV, *BIG[wn]), bf16) for wn in names],
        compiler_params=_cp(("arbitrary",), VMEM_LIMIT),
    )(*[arr for _, _, arr in pieces])
    return dict(zip(names, outs))


def kernel(x, p, positions, g_mix, w_in, g_qa, g_kva, w_qb, w_kvb, g_qn, g_kn, lb_param, g_hgo, w_o, g_ffn, w_gate, w_up, w_down, g_ple, w_ple_gate, w_ple_proj, loss_target, m_g_mix, m_w_in, m_g_qa, m_g_kva, m_w_qb, m_w_kvb, m_g_qn, m_g_kn, m_lb_param, m_g_hgo, m_w_o, m_g_ffn, m_w_gate, m_w_up, m_w_down, m_g_ple, m_w_ple_gate, m_w_ple_proj, v_g_mix, v_w_in, v_g_qa, v_g_kva, v_w_qb, v_w_kvb, v_g_qn, v_g_kn, v_lb_param, v_g_hgo, v_w_o, v_g_ffn, v_w_gate, v_w_up, v_w_down, v_g_ple, v_w_ple_gate, v_w_ple_proj):
    w_all = dict(g_mix=g_mix, g_qa=g_qa, g_kva=g_kva, g_qn=g_qn, g_kn=g_kn, g_hgo=g_hgo, g_ffn=g_ffn, g_ple=g_ple,
                 w_in=w_in, w_qb=w_qb, w_kvb=w_kvb, w_o=w_o, w_gate=w_gate, w_up=w_up, w_down=w_down,
                 w_ple_gate=w_ple_gate, w_ple_proj=w_ple_proj, lb_param=lb_param)
    m_all = dict(g_mix=m_g_mix, g_qa=m_g_qa, g_kva=m_g_kva, g_qn=m_g_qn, g_kn=m_g_kn, g_hgo=m_g_hgo, g_ffn=m_g_ffn,
                 g_ple=m_g_ple, w_in=m_w_in, w_qb=m_w_qb, w_kvb=m_w_kvb, w_o=m_w_o, w_gate=m_w_gate, w_up=m_w_up,
                 w_down=m_w_down, w_ple_gate=m_w_ple_gate, w_ple_proj=m_w_ple_proj, lb_param=m_lb_param)
    v_all = dict(g_mix=v_g_mix, g_qa=v_g_qa, g_kva=v_g_kva, g_qn=v_g_qn, g_kn=v_g_kn, g_hgo=v_g_hgo, g_ffn=v_g_ffn,
                 g_ple=v_g_ple, w_in=v_w_in, w_qb=v_w_qb, w_kvb=v_w_kvb, w_o=v_w_o, w_gate=v_w_gate, w_up=v_w_up,
                 w_down=v_w_down, w_ple_gate=v_w_ple_gate, w_ple_proj=v_w_ple_proj, lb_param=v_lb_param)
    me_idx = jnp.stack([_me()]).astype(jnp.int32)
    x, p, positions, target = x[0], p[0, 0], positions[0], loss_target[0]
    s = x.shape[0]
    tm, tm_ffn, tq_f, tq_b = min(512, s), min(1024, s), min(2048, s), min(1024, s)
    g_mix, g_qa, g_kva, g_qn, g_kn, g_hgo, g_ffn, g_ple = (w_all[n].reshape(1, -1) for n in SMALL)
    g_qn_p, g_kn_p = _pad256(g_qn), _pad256(g_kn)
    cosb, sina, sinb = _rope_tables(positions)
    as_shard = lambda n, a: a[0].T if n in TRANSPOSED else a.reshape(BIG[n])
    shard = lambda n: as_shard(n, w_all[n])

    first = _all_gather([shard(n) for n in FIRST], [f32 if n == "lb_param" else bf16 for n in FIRST], "ag_first")
    lands = _cast_to_slot([shard(n) for n in SECOND], me_idx, first[0])
    ag2, token = _exchange_start([], lands, "ag_second_start")
    wk = _weights_in(dict(zip(FIRST, first)), FIRST, "weights_in_first")
    wz, wqb, wkvb, lb4 = (wk[n] for n in FIRST)

    h1, z = _fwd_in(x, g_mix, wz, tm)
    q, k, v = _fwd_mla_proj(z, cosb + token[0, 0], sina, sinb, g_qa, g_kva, wqb, wkvb, g_qn_p, g_kn_p, tm)
    o, gla_b, gla_states = _fwd_gla(z, lb4)
    a, a32 = _fwd_attn(q, k, v, tq_f, o)

    second = dict(zip(SECOND, _exchange_wait(ag2, [a, o], "ag_second_wait")[1]))
    w_pp = second["w_ple_proj"]
    w_o, w_down, w_pg, w_gate, w_up = (second[n].reshape(N_DEV * BIG[n][0], BIG[n][1]) for n in ROW_SHARDED + ("w_gate", "w_up"))

    x2, cat = _fwd_mix(a, o, z, g_hgo, x, w_o, tm)
    x3, gp, up = _fwd_ffn(x2, g_ffn, w_gate, w_up, w_down, tm)
    d3, gw_pg, gw_pp, dg_ple, loss_tile = _ple_loss_fwd_bwd(x3, g_ple, w_pg, p, w_pp, target, tm)
    dgp, dup, gw_down = _bwd_ffn_hidden(d3, gp, up, w_down, tm, D_FF // 2)
    d2, h2, dg_ffn = _bwd_ffn_in(d3, x2, dgp, dup, g_ffn, w_gate, w_up, tm)
    da, do, dz_hg, dg_hgo, gw_o = _bwd_mix(d2, w_o, o, z, g_hgo, cat, tm)

    gw_gate, gw_up = _mm_tn_many(h2, [dgp, dup], "dw_gate_up", 512, transposed=True)
    blocks = {"w_ple_proj": gw_pp}
    row_grads = {"w_o": gw_o, "w_down": gw_down, "w_ple_gate": gw_pg, "w_gate": gw_gate, "w_up": gw_up}
    blocks.update({n: g.reshape(N_DEV, *BIG[n]) for n, g in row_grads.items()})
    empty = lambda names: [lax.empty((N_PEERS, *BIG[n]), bf16) for n in names]
    rs2, token = _exchange_start([blocks[n] for n in SECOND], empty(SECOND), "rs_second_start")

    dq, dk, dv = _bwd_attn(q, k, v, da, a32, tq_b, token)
    dz_hq, dz_hff, dz_hfb, dz_hi, dlb4 = _bwd_gla(z, lb4 + token[0, 0], do, gla_b, gla_states)
    dz_mla, gw_qb, gw_kvb, dg_qa, dg_kva, dg_qn, dg_kn = _bwd_mla_proj(
        z, dq, dk, dv, cosb, sina, sinb, g_qa, g_kva, wqb, wkvb, g_qn_p, g_kn_p, tm)

    gz = list(zip((Z_HQ, Z_HFF, Z_HFB, Z_HI, Z_HG, Z_CQ),
                  _mm_tn_many(h1, [dz_hq, dz_hff, dz_hfb, dz_hi, dz_hg, dz_mla], "dw_in", 1024, transposed=True)))
    blocks1 = _grads_out({"w_in": gz, "w_qb": [(0, gw_qb)], "w_kvb": [(0, gw_kvb)],
                          "lb_param": [(0, dlb4)]}, FIRST, "grads_out_first")
    rs1, token = _exchange_start([blocks1[n] for n in FIRST], empty(FIRST), "rs_first_start")

    result = {}

    def adam(names, lands, src, n_blocks, after=()):
        flipped = TRANSPOSED
        given = lambda arrs: [arrs[n][0].T if n in flipped else arrs[n] for n in names]
        outs = _adam_shards(me_idx, [src[n] for n in names], lands, given(w_all), given(m_all), given(v_all), n_blocks,
                            "adamw_" + names[0], after)
        for n, o in zip(names, outs):
            result[n] = [t.T[None] for t in o] if n in flipped else o
        return outs[0][0]

    blocks2, lands2 = (dict(zip(SECOND, arrs)) for arrs in _exchange_wait(rs2, [token], "rs_second_wait"))
    by2 = ("w_down", "w_gate", "w_up")
    by8 = tuple(n for n in SECOND if n not in by2)
    done = [adam(by8, [lands2[n] for n in by8], blocks2, 8), adam(by2, [lands2[n] for n in by2], blocks2, 2)]

    segments = [(dz_hq, 512, 0, Z_HQ // 512), (dz_hff, 512, 0, Z_HFF // 512), (dz_hfb, 512, 0, Z_HFB // 512),
                (dz_hi, 512, 0, Z_HI // 512), (dz_hg, 512, 0, Z_HG // 512), (dz_mla, 640, 0, Z_CQ // 640)]
    grad_x, dg_mix = _bwd_in(segments, wz, x, g_mix + token[0, 0], d2, tm)
    dgains = (dg_mix, dg_qa, dg_kva, dg_qn, dg_kn, dg_hgo, dg_ffn, dg_ple)

    vec = jnp.concatenate(list(dgains) + [loss_tile[0:1]], axis=1)
    parts = _all_gather([vec], [f32], "ag_gains")[0]
    outs, loss_row = _adam_gains(parts, [w_all[n] for n in SMALL], [m_all[n] for n in SMALL], [v_all[n] for n in SMALL])
    result.update(zip(SMALL, outs))

    blocks1, lands1 = _exchange_wait(rs1, [grad_x, loss_row, *done], "rs_first_wait")
    adam(FIRST, lands1, dict(zip(FIRST, blocks1)), 8)

    order = ("g_mix", "w_in", "g_qa", "g_kva", "w_qb", "w_kvb", "g_qn", "g_kn", "lb_param", "g_hgo", "w_o", "g_ffn",
             "w_gate", "w_up", "w_down", "g_ple", "w_ple_gate", "w_ple_proj")
    return (loss_row[0, 0], grad_x[None], *[result[n][k] for k in range(4) for n in order])
```

```python
import functools
import math

import jax
import jax.numpy as jnp
from jax import lax
from jax.experimental import pallas as pl
from jax.experimental.pallas import tpu as pltpu

f32 = jnp.float32
bf16 = jnp.bfloat16

N_DEV = 8
MLA_HEADS = 4
QK_NOPE = 128
QK_ROPE = 64
QK_HEAD = QK_NOPE + QK_ROPE
QK_PAD = 256
V_HEAD = 128
Q_LORA = 256
KV_LORA = 256
HG_HEADS = 4
CHUNK = 64
D_FF = 2816
PLE_DIM = 256
ROPE_THETA = 10000.0
EPS = 1e-6
ATTN_SCALE = QK_HEAD ** -0.5
LOG2_E = math.log2(math.e)
ATTN_SUB_ROWS = 256
Z_HQ, Z_HFF, Z_HFB, Z_HI, Z_HG, Z_CQ, Z_CKV, Z_KR, Z_W = 0, 512, 1024, 1536, 2048, 2560, 2816, 3072, 3200

ADAM_LR, ADAM_B1, ADAM_B2, ADAM_EPS, ADAM_WD, ADAM_STEP = 0.001, 0.9, 0.999, 1e-08, 0.01, 10

LANES = 128
BIG = {"w_in": (392, 1024), "w_qb": (256, 96), "w_kvb": (256, 128), "w_o": (128, 1024), "w_gate": (352, 1024),
       "w_up": (352, 1024), "w_down": (352, 1024), "w_ple_gate": (128, 1024), "w_ple_proj": (256, 128),
       "lb_param": (4, 64)}
TRANSPOSED = ("w_gate", "w_up", "w_in")
SMALL = {"g_mix": (0, 1024), "g_qa": (1024, 256), "g_kva": (1280, 256), "g_qn": (1536, 192), "g_kn": (1792, 192),
         "g_hgo": (2048, 512), "g_ffn": (2560, 1024), "g_ple": (3584, 1024)}
LOSS_OFF = 4608
GAIN_VEC = LOSS_OFF + LANES
Z_SEGMENTS = ((0, 256, Z_CQ), (256, 512, Z_CKV), (512, 576, Z_KR), (576, 1088, Z_HQ), (1088, 1600, Z_HFF),
              (1600, 2112, Z_HFB), (2112, 2624, Z_HI), (2624, 3136, Z_HG))

VMEM_LIMIT = 56 * 1024 * 1024
MESH = pl.DeviceIdType.MESH


def _cp(sem=None, vmem=None):
    return pltpu.CompilerParams(dimension_semantics=sem, vmem_limit_bytes=vmem)


def _const_spec(shape):
    nd = len(shape)
    return pl.BlockSpec(shape, lambda *_: (0,) * nd, pipeline_mode=pl.Buffered(1))


def _acc_spec(shape):
    nd = len(shape)
    return pl.BlockSpec(shape, lambda *_: (0,) * nd)


def _sigmoid(x):
    return jax.nn.sigmoid(x)


def _dot(a, b):
    return jnp.dot(a, b, preferred_element_type=f32)


def _dot_nt(a, b):
    return lax.dot_general(a, b, (((1,), (1,)), ((), ())), preferred_element_type=f32)


def _dot_tn(a, b):
    return lax.dot_general(a, b, (((0,), (0,)), ((), ())), preferred_element_type=f32)


def _rms_fwd(x, g, width):
    r = lax.rsqrt(jnp.sum(x * x, axis=-1, keepdims=True) * (1.0 / width) + EPS)
    return x * r * g, r


def _rms_bwd(dy, x, r, g, width):
    u = dy * g
    dx = r * u - x * (r * r * r) * (jnp.sum(u * x, axis=-1, keepdims=True) * (1.0 / width))
    return dx, dy * x * r


class _Both:
    def __init__(self, *copies):
        self.copies = copies

    def start(self):
        for cp in self.copies:
            cp.start()

    def wait(self):
        for cp in self.copies:
            cp.wait()


def _rope(b, c, sa, sb):
    return b * c + pltpu.roll(b, 32, 1) * sa + pltpu.roll(b, 96, 1) * sb


def _all_gather(shards, dtypes, name):
    n = len(shards)

    def body(*refs):
        in_refs, out_refs, stage = refs[:n], refs[n:2 * n], refs[2 * n:3 * n]
        send_sems, recv_sems, local_sems = refs[3 * n:]
        for w in range(n):
            stage[w][...] = in_refs[w][...].astype(stage[w].dtype)
        x, y, c = lax.axis_index("x"), lax.axis_index("y"), lax.axis_index("c")
        me, sibling = (x, y, c), (x, y, 1 - c)
        chips = [(1 - x, y), (x, 1 - y), (1 - x, 1 - y)]

        def slot(w, px, py, pc):
            return out_refs[w].at[4 * px + 2 * py + pc]

        def copy(w, k, block, to, src=None):
            return pltpu.make_async_remote_copy(
                src_ref=slot(w, *block) if src is None else src, dst_ref=slot(w, *block),
                send_sem=send_sems.at[w, k], recv_sem=recv_sems.at[w, k], device_id=to, device_id_type=MESH)

        first = []
        for j, chip in enumerate(chips):
            first += [copy(w, 1 + j, me, (*chip, c), src=stage[w]) for w in range(n)]
        first += [copy(w, 0, me, sibling, src=stage[w]) for w in range(n)]
        mine = [pltpu.make_async_copy(stage[w], slot(w, *me), local_sems.at[w]) for w in range(n)]
        for cp in first + mine:
            cp.start()
        passed = []
        for j, chip in enumerate(chips):
            for w in range(n):
                copy(w, 1 + j, (*chip, c), me).wait_recv()
                passed.append(copy(w, 4 + j, (*chip, c), sibling))
                passed[-1].start()
        for w in range(n):
            copy(w, 0, sibling, me).wait_recv()
        for j, chip in enumerate(chips):
            for w in range(n):
                copy(w, 4 + j, (*chip, 1 - c), me).wait_recv()
        for cp in first + passed:
            cp.wait_send()
        for cp in mine:
            cp.wait()

    return pl.pallas_call(
        body, name=name,
        out_shape=[jax.ShapeDtypeStruct((N_DEV, *s.shape), dt) for s, dt in zip(shards, dtypes)],
        in_specs=[pl.BlockSpec(memory_space=pltpu.VMEM)] * n,
        out_specs=[pl.BlockSpec(memory_space=pl.ANY)] * n,
        scratch_shapes=[pltpu.VMEM(s.shape, dt) for s, dt in zip(shards, dtypes)]
        + [pltpu.SemaphoreType.DMA((n, 7)), pltpu.SemaphoreType.DMA((n, 7)), pltpu.SemaphoreType.DMA((n,))],
        compiler_params=_cp(None, VMEM_LIMIT),
    )(*shards)


N_PEERS = N_DEV - 1
HBM_SPEC = pl.BlockSpec(memory_space=pltpu.HBM)
SEM_SPEC = pl.BlockSpec(memory_space=pltpu.SEMAPHORE)
DATAFLOW = pltpu.SideEffectType.DATAFLOW_SIDE_EFFECTING


def _me():
    return 4 * lax.axis_index("x") + 2 * lax.axis_index("y") + lax.axis_index("c")


def _peer(k):
    x, y, c = lax.axis_index("x"), lax.axis_index("y"), lax.axis_index("c")
    px = 1 - x if k & 4 else x
    py = 1 - y if k & 2 else y
    pc = 1 - c if k & 1 else c
    return (px, py, pc), 4 * px + 2 * py + pc


def _exchange_copies(src_refs, land_refs, send_sems, recv_sems, gather):
    cps = []
    me = _me()
    for k in range(1, N_DEV):
        peer, peer_idx = _peer(k)
        for w, land in enumerate(land_refs):
            src = land.at[me] if gather else src_refs[w].at[peer_idx]
            dst = land.at[me] if gather else land.at[k - 1]
            cps.append(pltpu.make_async_remote_copy(
                src_ref=src, dst_ref=dst, send_sem=send_sems.at[N_PEERS * w + k - 1], recv_sem=recv_sems.at[N_PEERS * w + k - 1],
                device_id=peer, device_id_type=MESH))
    return cps


def _exchange_start(srcs, lands, name):
    n_src, n = len(srcs), len(lands)

    def body(*refs):
        src_refs, land_refs = refs[:n_src], refs[n_src:n_src + n]
        send_sems, recv_sems = refs[n_src + n], refs[n_src + n + 1]
        token = refs[-1]
        for cp in _exchange_copies(src_refs, land_refs, send_sems, recv_sems, gather=not n_src):
            cp.start()
        token[...] = jnp.zeros_like(token)

    arrays = [pltpu.with_memory_space_constraint(a, pltpu.HBM) for a in (*srcs, *lands)]
    outs = pl.pallas_call(
        body, name=name,
        out_shape=(pltpu.SemaphoreType.DMA((n * N_PEERS,)), pltpu.SemaphoreType.DMA((n * N_PEERS,)),
                   *[pltpu.HBM(a.shape, a.dtype) for a in arrays], jax.ShapeDtypeStruct((8, LANES), f32)),
        in_specs=[HBM_SPEC] * len(arrays),
        out_specs=(SEM_SPEC, SEM_SPEC, *[HBM_SPEC] * len(arrays), pl.BlockSpec(memory_space=pltpu.VMEM)),
        input_output_aliases={i: 2 + i for i in range(len(arrays))},
        compiler_params=pltpu.CompilerParams(has_side_effects=DATAFLOW),
    )(*arrays)
    return (outs[0], outs[1], outs[2:2 + n_src], outs[2 + n_src:2 + n_src + n]), outs[-1]


def _exchange_wait(state, after, name):
    send_sems, recv_sems, srcs, lands = state
    n_src, n = len(srcs), len(lands)

    def body(*refs):
        src_refs, land_refs = refs[:n_src], refs[n_src:n_src + n]
        send_ref, recv_ref = refs[n_src + n], refs[n_src + n + 1]
        for cp in _exchange_copies(src_refs, land_refs, send_ref, recv_ref, gather=not n_src):
            cp.wait_send()
            cp.wait_recv()

    arrays = (*srcs, *lands)
    outs = pl.pallas_call(
        body, name=name,
        out_shape=tuple(pltpu.HBM(a.shape, a.dtype) for a in arrays),
        in_specs=[HBM_SPEC] * len(arrays) + [SEM_SPEC, SEM_SPEC] + [pl.BlockSpec(memory_space=pl.ANY)] * len(after),
        out_specs=tuple([HBM_SPEC] * len(arrays)),
        input_output_aliases={i: i for i in range(len(arrays))},
        compiler_params=pltpu.CompilerParams(has_side_effects=DATAFLOW),
    )(*arrays, send_sems, recv_sems, *after)
    return outs[:n_src], outs[n_src:]


def _cast_to_slot(shards, me_idx, after):
    n = len(shards)

    def body(i_ref, *refs):
        for w in range(n):
            refs[n + 1 + w][...] = refs[w][...].astype(bf16)

    return pl.pallas_call(
        body, name="cast_to_slot",
        grid_spec=pltpu.PrefetchScalarGridSpec(
            num_scalar_prefetch=1, grid=(1,),
            in_specs=[pl.BlockSpec(s.shape, lambda i, m: (0, 0)) for s in shards] + [pl.BlockSpec(memory_space=pl.ANY)],
            out_specs=[pl.BlockSpec((None, *s.shape), lambda i, m: (m[0], 0, 0)) for s in shards]),
        out_shape=[jax.ShapeDtypeStruct((N_DEV, *s.shape), bf16) for s in shards],
        compiler_params=_cp(("arbitrary",), VMEM_LIMIT),
    )(me_idx, *shards, after)


def _row_block(rows, n_blocks):
    return (rows // n_blocks, True) if rows % (16 * n_blocks) == 0 else (rows, False)


def _adam_math(w, g, m, v):
    m = ADAM_B1 * m + (1.0 - ADAM_B1) * g
    v = ADAM_B2 * v + (1.0 - ADAM_B2) * (g * g)
    m_hat = m / (1.0 - ADAM_B1 ** ADAM_STEP)
    v_hat = v / (1.0 - ADAM_B2 ** ADAM_STEP)
    delta = -ADAM_LR * (m_hat / (jnp.sqrt(v_hat) + ADAM_EPS) + ADAM_WD * w)
    return delta, m, v


def _adam_shards(me_idx, blocks, lands, ws, ms, vs, n_blocks, name, after=()):
    n = len(blocks)

    def body(i_ref, *refs):
        ins, outs = refs[:5 * n], refs[5 * n + len(after):]
        for w in range(n):
            g_ref, b_ref, w_ref, m_ref, v_ref = (ins[t * n + w] for t in range(5))
            g = g_ref[...].astype(f32)
            for k in range(N_PEERS):
                g = g + b_ref[k].astype(f32)
            if len(w_ref.shape) == 2:
                pieces = [(slice(None), g)]
            else:
                pieces = [(a, g[2 * a:2 * a + 2]) for a in range(2)]
            for at, gp in pieces:
                vals = (gp,) + _adam_math(w_ref[at], gp, m_ref[at], v_ref[at])
                for t, val in enumerate(vals):
                    outs[4 * w + t][at] = val

    specs = [[] for _ in range(5)]
    out_specs, out_shape = [], []
    for g, wt in zip(blocks, ws):
        rows, cols = g.shape[1:]
        rb, cut = _row_block(rows, n_blocks)
        if not cut and wt.ndim == 2 and cols % (LANES * n_blocks) == 0:
            cb = cols // n_blocks
            specs[0].append(pl.BlockSpec((None, rows, cb), lambda i, s: (s[0], 0, i)))
            specs[1].append(pl.BlockSpec((N_PEERS, rows, cb), lambda i, s: (0, 0, i)))
            shard = pl.BlockSpec((rows, cb), lambda i, s: (0, i))
            for t in (2, 3, 4):
                specs[t].append(shard)
            out_specs += [shard] * 4
            out_shape += [jax.ShapeDtypeStruct(wt.shape, f32)] * 4
            continue
        specs[0].append(pl.BlockSpec((None, rb, cols), functools.partial(lambda i, s, cut: (s[0], i if cut else 0, 0), cut=cut)))
        specs[1].append(pl.BlockSpec((N_PEERS, rb, cols), functools.partial(lambda i, s, cut: (0, i if cut else 0, 0), cut=cut)))
        if wt.ndim == 2:
            shard = pl.BlockSpec((rb, cols), functools.partial(lambda i, s, cut: (i if cut else 0, 0), cut=cut))
        elif wt.shape[0] == 1:
            shard = pl.BlockSpec((None, rb, cols), functools.partial(lambda i, s, cut: (0, i if cut else 0, 0), cut=cut))
        else:
            shard = pl.BlockSpec(wt.shape, functools.partial(lambda i, s, nd: (0,) * nd, nd=wt.ndim))
        for t in (2, 3, 4):
            specs[t].append(shard)
        out_specs += [shard] * 4
        out_shape += [jax.ShapeDtypeStruct(wt.shape, f32)] * 4
    outs = pl.pallas_call(
        body, name=name,
        grid_spec=pltpu.PrefetchScalarGridSpec(
            num_scalar_prefetch=1, grid=(n_blocks,), in_specs=sum(specs, []) + [pl.BlockSpec(memory_space=pl.ANY)] * len(after),
            out_specs=out_specs),
        out_shape=out_shape,
        compiler_params=_cp(("arbitrary",), VMEM_LIMIT),
    )(me_idx, *blocks, *lands, *ws, *ms, *vs, *after)
    return [outs[4 * w:4 * w + 4] for w in range(n)]


def _adam_gains(parts, ws, ms, vs):
    n = len(ws)

    def body(p_ref, *refs):
        ins, outs = refs[:3 * n], refs[3 * n:]
        g_all = p_ref[0]
        for k in range(1, N_DEV):
            g_all = g_all + p_ref[k]
        for w, (off, lanes) in enumerate(SMALL.values()):
            w_ref, m_ref, v_ref = ins[w], ins[n + w], ins[2 * n + w]
            if len(w_ref.shape) == 2:
                pieces = [(slice(None), off, lanes)]
            else:
                pieces = [((slice(None), h), off + LANES * h, LANES) for h in range(w_ref.shape[1])]
            for at, o, ln in pieces:
                g = g_all[:, o:o + ln]
                vals = (g,) + _adam_math(w_ref[at], g, m_ref[at], v_ref[at])
                for t, val in enumerate(vals):
                    outs[4 * w + t][at] = val
        outs[4 * n][...] = g_all[:, LOSS_OFF:LOSS_OFF + LANES]

    out_shape = sum([[jax.ShapeDtypeStruct(w.shape, f32)] * 4 for w in ws], []) + [jax.ShapeDtypeStruct((1, LANES), f32)]
    outs = pl.pallas_call(body, name="adamw_gains", out_shape=out_shape)(parts, *ws, *ms, *vs)
    return [outs[4 * w:4 * w + 4] for w in range(n)], outs[4 * n]


def _fwd_in(x, g_mix, wz, tm):
    s, d = x.shape

    def body(x_ref, g_ref, w_ref, h_ref, z_ref):
        h, _ = _rms_fwd(x_ref[...], g_ref[...], d)
        hb = h.astype(bf16)
        h_ref[...] = hb
        z_ref[...] = _dot_nt(hb, w_ref[...])

    return pl.pallas_call(
        body, name="fwd_in", grid=(s // tm,),
        in_specs=[pl.BlockSpec((tm, d), lambda i: (i, 0)), _const_spec((1, d)), _const_spec((Z_W, d))],
        out_specs=[pl.BlockSpec((tm, d), lambda i: (i, 0)), pl.BlockSpec((tm, Z_W), lambda i: (i, 0))],
        out_shape=[jax.ShapeDtypeStruct((s, d), bf16), jax.ShapeDtypeStruct((s, Z_W), f32)],
        compiler_params=_cp(("parallel",), VMEM_LIMIT),
    )(x, g_mix, wz)


def _mla_qk_fwd(cq, ckv, g_qa, g_kva, wqb, wkvb):
    cqn, rq = _rms_fwd(cq, g_qa, Q_LORA)
    ckvn, rkv = _rms_fwd(ckv, g_kva, KV_LORA)
    cqn_b, ckvn_b = cqn.astype(bf16), ckvn.astype(bf16)
    q0 = _dot(cqn_b, wqb)
    kv0 = _dot(ckvn_b, wkvb)
    return cqn_b, rq, ckvn_b, rkv, q0, kv0


def _fwd_mla_proj(z, cosb, sina, sinb, g_qa, g_kva, wqb, wkvb, g_qn, g_kn, tm):
    s = z.shape[0]
    hh = MLA_HEADS

    def body(cq_ref, ckv_ref, kr_ref, c_ref, sa_ref, sb_ref, gqa_ref, gkva_ref, wqb_ref, wkvb_ref, gqn_ref, gkn_ref,
             q_ref, k_ref, v_ref):
        _, _, _, _, q0, kv0 = _mla_qk_fwd(cq_ref[...], ckv_ref[...], gqa_ref[...], gkva_ref[...], wqb_ref[...], wkvb_ref[...])
        kr = kr_ref[...]
        c, sa, sb = c_ref[...], sa_ref[...], sb_ref[...]
        gqn, gkn = gqn_ref[...], gkn_ref[...]
        kr_sq = jnp.sum(kr * kr, axis=-1, keepdims=True)
        for h in range(hh):
            qh = q0[:, QK_PAD * h:QK_PAD * (h + 1)]
            qn, _ = _rms_fwd(qh, gqn, QK_HEAD)
            q_ref[h, :, 0:128] = qn[:, 0:128].astype(bf16)
            q_ref[h, :, 128:256] = _rope(qn[:, 128:256], c, sa, sb).astype(bf16)
            kn_ = kv0[:, 256 * h:256 * h + 128]
            rk = lax.rsqrt((jnp.sum(kn_ * kn_, axis=-1, keepdims=True) + kr_sq) * (1.0 / QK_HEAD) + EPS)
            k_ref[h, :, 0:128] = (kn_ * rk * gkn[:, 0:128]).astype(bf16)
            k_ref[h, :, 128:256] = _rope(kr * rk * gkn[:, 128:256], c, sa, sb).astype(bf16)
            v_ref[h] = kv0[:, 256 * h + 128:256 * h + 256].astype(bf16)

    row128 = pl.BlockSpec((tm, 128), lambda i: (i, 0))
    return pl.pallas_call(
        body, name="fwd_mla_proj", grid=(s // tm,),
        in_specs=[pl.BlockSpec((tm, 256), lambda i: (i, Z_CQ // 256)), pl.BlockSpec((tm, 256), lambda i: (i, Z_CKV // 256)),
                  pl.BlockSpec((tm, 128), lambda i: (i, Z_KR // 128)), row128, row128, row128,
                  _const_spec((1, 256)), _const_spec((1, 256)), _const_spec((256, 1024)), _const_spec((256, 1024)),
                  _const_spec((1, 256)), _const_spec((1, 256))],
        out_specs=[pl.BlockSpec((hh, tm, QK_PAD), lambda i: (0, i, 0)), pl.BlockSpec((hh, tm, QK_PAD), lambda i: (0, i, 0)),
                   pl.BlockSpec((hh, tm, V_HEAD), lambda i: (0, i, 0))],
        out_shape=[jax.ShapeDtypeStruct((hh, s, QK_PAD), bf16), jax.ShapeDtypeStruct((hh, s, QK_PAD), bf16),
                   jax.ShapeDtypeStruct((hh, s, V_HEAD), bf16)],
        compiler_params=_cp(("parallel",), VMEM_LIMIT),
    )(z, z, z, cosb, sina, sinb, g_qa, g_kva, wqb, wkvb, g_qn, g_kn)


def _fwd_attn(q, k, v, tq, after):
    hh, s, _ = q.shape

    n_sub = max(1, tq // ATTN_SUB_ROWS)

    def body(q_ref, k_ref, v_ref, after_ref, o_ref, o32_ref):
        for t in range(n_sub):
            rows = slice(t * (tq // n_sub), (t + 1) * (tq // n_sub))
            sc = _dot_nt(q_ref[rows, :], k_ref[...])
            p = jnp.exp2((sc - jnp.max(sc, axis=-1, keepdims=True)) * (ATTN_SCALE * LOG2_E))
            l = jnp.sum(p, axis=-1, keepdims=True)
            o = _dot(p.astype(bf16), v_ref[...]) * (1.0 / l)
            o_ref[rows, :] = o.astype(bf16)
            o32_ref[rows, :] = o

    out = pl.BlockSpec((tq, V_HEAD), lambda h, i: (i, h))
    return pl.pallas_call(
        body, name="fwd_attn", grid=(hh, s // tq),
        in_specs=[pl.BlockSpec((None, tq, QK_PAD), lambda h, i: (h, i, 0)),
                  pl.BlockSpec((None, s, QK_PAD), lambda h, i: (h, 0, 0)),
                  pl.BlockSpec((None, s, V_HEAD), lambda h, i: (h, 0, 0)), pl.BlockSpec(memory_space=pl.ANY)],
        out_specs=[out, out],
        out_shape=[jax.ShapeDtypeStruct((s, hh * V_HEAD), bf16), jax.ShapeDtypeStruct((s, hh * V_HEAD), f32)],
        compiler_params=_cp(("parallel", "parallel"), VMEM_LIMIT),
    )(q, k, v, after)


def _split3(x):
    hi = x.astype(bf16)
    r1 = x - hi.astype(f32)
    mid = r1.astype(bf16)
    lo = (r1 - mid.astype(f32)).astype(bf16)
    return jnp.concatenate([hi, mid, lo], axis=-1)


def _tri_sum(tri, x):
    y = _dot(tri, _split3(x))
    return y[:, 0:128] + y[:, 128:256] + y[:, 256:384]


GLA_GROUP = 4
GLA_ROWS = GLA_GROUP * CHUNK
GLA_HEADS_PER_STEP = 2


def _gla_masks(rev):
    row = lax.broadcasted_iota(jnp.int32, (GLA_ROWS, GLA_ROWS), 0)
    col = lax.broadcasted_iota(jnp.int32, (GLA_ROWS, GLA_ROWS), 1)
    shift = CHUNK.bit_length() - 1
    same = (jnp.right_shift(row, shift) == jnp.right_shift(col, shift)).astype(f32)
    lower, upper = (row >= col).astype(f32) * same, (row <= col).astype(f32) * same
    keep, keep_t = (upper, lower) if rev else (lower, upper)
    chunk_of = jnp.right_shift(lax.broadcasted_iota(jnp.int32, (GLA_ROWS, 1), 0), shift)
    return keep, keep.astype(bf16), keep_t.astype(bf16), [(chunk_of == c).astype(f32) for c in range(GLA_GROUP)]


def _gla_gates(hq, hf, lower):
    sg = _sigmoid(hf)
    f = lower + (1.0 - lower) * sg
    return hq * _sigmoid(hq), 1.0 - f, jnp.log(f), f, sg


def _gla_last_mid(b, rev):
    b3 = b.reshape(GLA_GROUP, CHUNK, 128)
    last, mid = (0, CHUNK // 2) if rev else (CHUNK - 1, CHUNK // 2 - 1)
    return b3[:, last:last + 1, :], b3[:, mid:mid + 1, :]


def _gla_per_row(per_chunk):
    return jnp.broadcast_to(per_chunk, (GLA_GROUP, CHUNK, 128)).reshape(GLA_ROWS, 128)


def _gla_block_diag(x, row_masks):
    return jnp.concatenate([(x * m).astype(bf16) for m in row_masks], axis=-1)


def _gla_diag(y):
    return jnp.concatenate([y[CHUNK * c:CHUNK * (c + 1), 128 * c:128 * (c + 1)] for c in range(GLA_GROUP)], axis=0)


def _gla_rows(n, n_groups, rev):
    ne = n_groups - 1 - n if rev else n
    return pl.ds(pl.multiple_of(ne * GLA_ROWS, GLA_ROWS), GLA_ROWS), ne * GLA_GROUP


def _gla_scan_order(rev):
    return tuple(reversed(range(GLA_GROUP))) if rev else tuple(range(GLA_GROUP))


def _fwd_gla(z, lb4):
    s = z.shape[0]
    n_groups = s // GLA_ROWS
    assert n_groups % 2 == 0
    hp = GLA_HEADS_PER_STEP
    chains = [(hh, rev) for hh in range(hp) for rev in (False, True)]

    def body(hq_ref, hff_ref, hfb_ref, hi_ref, lb_ref, o_ref, b_ref, states_ref, st_ref, stage_ref, b_stage, sems):
        st_ref[...] = jnp.zeros_like(st_ref)
        masks = {rev: _gla_masks(rev) for rev in (False, True)}
        lowers = [_sigmoid(lb_ref[int(rev):int(rev) + 1, 128 * hh:128 * (hh + 1)]
                           - lb_ref[2 + int(rev):3 + int(rev), 128 * hh:128 * (hh + 1)]) for hh, rev in chains]

        def states_out(slot, ci, chunk0):
            hh, rev = chains[ci]
            head = pl.program_id(0) * hp + hh
            rows = pl.ds(pl.multiple_of(chunk0 * CHUNK, GLA_ROWS), GLA_ROWS)
            return _Both(
                pltpu.make_async_copy(stage_ref.at[slot, ci], states_ref.at[head, int(rev), pl.ds(chunk0, GLA_GROUP)],
                                      sems.at[slot, ci]),
                pltpu.make_async_copy(b_stage.at[slot, ci], b_ref.at[int(rev), rows, pl.ds(pl.multiple_of(head * 128, 128), 128)],
                                      sems.at[slot, len(chains) + ci]))

        def make_step(first):
            def step(n, carry):
                slot = n % 2

                @pl.when(n >= 2)
                def _():
                    for ci in range(len(chains)):
                        states_out(slot, ci, 0).wait()

                for ci, (hh, rev) in enumerate(chains):
                    cols = slice(128 * hh, 128 * (hh + 1))
                    rows, chunk0 = _gla_rows(n, n_groups, rev)
                    maskf, tri, _, row_masks = masks[rev]
                    hf_ref = hfb_ref if rev else hff_ref
                    q, k, logf, _, _ = _gla_gates(hq_ref[rows, cols], hf_ref[rows, cols], lowers[ci])
                    vb = hi_ref[rows, cols].astype(bf16)
                    b = _tri_sum(tri, logf)
                    b_stage[slot, ci] = b
                    b_last3, b_mid3 = _gla_last_mid(b, rev)
                    b_last, b_mid = _gla_per_row(b_last3), _gla_per_row(b_mid3)
                    qi = (q * jnp.exp(b - b_mid)).astype(bf16)
                    ki = (k * jnp.exp(b_mid - b)).astype(bf16)
                    a = (_dot_nt(qi, ki) * maskf).astype(bf16)
                    kv = _dot_tn(vb, _gla_block_diag(k * jnp.exp(b_last - b), row_masks))
                    decay3 = jnp.exp(b_last3)
                    st = st_ref[ci]
                    before = [None] * GLA_GROUP
                    for c in _gla_scan_order(rev):
                        stage_ref[slot, ci, c] = st
                        before[c] = st.astype(bf16)
                        st = st * decay3[c] + kv[:, 128 * c:128 * (c + 1)]
                    st_ref[ci] = st
                    states_out(slot, ci, chunk0).start()
                    inter = _dot_nt((q * jnp.exp(b)).astype(bf16), jnp.concatenate(before, axis=0))
                    o = _dot(a, vb) + _gla_diag(inter)
                    if first:
                        o_ref[rows, cols] = o
                    else:
                        o_ref[rows, cols] += o
                return carry
            return step

        lax.fori_loop(0, n_groups // 2, make_step(True), 0)
        lax.fori_loop(n_groups // 2, n_groups, make_step(False), 0)
        for slot in range(2):
            for ci in range(len(chains)):
                states_out(slot, ci, 0).wait()

    w = 128 * hp
    col = lambda base: pl.BlockSpec((s, w), lambda h: (0, base // w + h))
    return pl.pallas_call(
        body, name="fwd_gla", grid=(HG_HEADS // hp,),
        in_specs=[col(Z_HQ), col(Z_HFF), col(Z_HFB), col(Z_HI), pl.BlockSpec((4, w), lambda h: (0, h))],
        out_specs=[pl.BlockSpec((s, w), lambda h: (0, h)), pl.BlockSpec(memory_space=pl.ANY), pl.BlockSpec(memory_space=pl.ANY)],
        out_shape=[jax.ShapeDtypeStruct((s, HG_HEADS * 128), f32), jax.ShapeDtypeStruct((2, s, HG_HEADS * 128), f32),
                   jax.ShapeDtypeStruct((HG_HEADS, 2, s // CHUNK, 128, 128), f32)],
        scratch_shapes=[pltpu.VMEM((len(chains), 128, 128), f32), pltpu.VMEM((2, len(chains), GLA_GROUP, 128, 128), f32),
                        pltpu.VMEM((2, len(chains), GLA_ROWS, 128), f32), pltpu.SemaphoreType.DMA((2, 2 * len(chains)))],
        compiler_params=_cp(("parallel",), VMEM_LIMIT),
    )(z, z, z, z, lb4)


def _hg_out(o, hg, g_hgo):
    outs, ons, rs = [], [], []
    for h in range(HG_HEADS):
        oh = o[:, 128 * h:128 * (h + 1)]
        on, r = _rms_fwd(oh, g_hgo[:, 128 * h:128 * (h + 1)], 128)
        ons.append(on)
        rs.append(r)
    on = jnp.concatenate(ons, axis=-1)
    sg = _sigmoid(hg)
    return on * (hg * sg), on, rs, sg


def _fwd_mix(a, o, z, g_hgo, x, w_o, tm):
    s, d = x.shape

    def body(a_ref, o_ref, hg_ref, g_ref, x_ref, w_ref, x2_ref, cat_ref):
        r, _, _, _ = _hg_out(o_ref[...], hg_ref[...], g_ref[...])
        cat = jnp.concatenate([a_ref[...], r.astype(bf16)], axis=-1)
        cat_ref[...] = cat
        x2_ref[...] = x_ref[...] + _dot(cat, w_ref[...])

    row512 = pl.BlockSpec((tm, 512), lambda i: (i, 0))
    rowd = pl.BlockSpec((tm, d), lambda i: (i, 0))
    return pl.pallas_call(
        body, name="fwd_mix", grid=(s // tm,),
        in_specs=[row512, row512, pl.BlockSpec((tm, 512), lambda i: (i, Z_HG // 512)), _const_spec((1, 512)), rowd,
                  _const_spec((d, d))],
        out_specs=[rowd, rowd],
        out_shape=[jax.ShapeDtypeStruct((s, d), f32), jax.ShapeDtypeStruct((s, d), bf16)],
        compiler_params=_cp(("parallel",), VMEM_LIMIT),
    )(a, o, z, g_hgo, x, w_o)


def _fwd_ffn(x2, g_ffn, w_gate, w_up, w_down, tm):
    s, d = x2.shape

    def body(x_ref, g_ref, wg_ref, wu_ref, wd_ref, x3_ref, gp_ref, up_ref):
        x = x_ref[...]
        h, _ = _rms_fwd(x, g_ref[...], d)
        hb = h.astype(bf16)
        gp = _dot_nt(hb, wg_ref[...])
        up = _dot_nt(hb, wu_ref[...])
        gp_ref[...] = gp.astype(bf16)
        up_ref[...] = up.astype(bf16)
        act = (gp * _sigmoid(gp) * up).astype(bf16)
        x3_ref[...] = x + _dot(act, wd_ref[...])

    rowd = pl.BlockSpec((tm, d), lambda i: (i, 0))
    rowf = pl.BlockSpec((tm, D_FF), lambda i: (i, 0))
    return pl.pallas_call(
        body, name="fwd_ffn", grid=(s // tm,),
        in_specs=[rowd, _const_spec((1, d)), _const_spec((D_FF, d)), _const_spec((D_FF, d)), _const_spec((D_FF, d))],
        out_specs=[rowd, rowf, rowf],
        out_shape=[jax.ShapeDtypeStruct((s, d), f32), jax.ShapeDtypeStruct((s, D_FF), bf16),
                   jax.ShapeDtypeStruct((s, D_FF), bf16)],
        compiler_params=_cp(("parallel",), VMEM_LIMIT),
    )(x2, g_ffn, w_gate, w_up, w_down)


def _ple_loss_fwd_bwd(x3, g_ple, w_pg, p, w_pp, target, tm):
    s, d = x3.shape
    cols = BIG["w_ple_proj"][1]

    def body(x_ref, g_ref, wg_ref, p_ref, wp_ref, t_ref, dx_ref, gwg_ref, gwp_ref, dg_ref, loss_ref, acc_ref, accp_ref):
        @pl.when(pl.program_id(0) == 0)
        def _():
            for r_ in (acc_ref, accp_ref, dg_ref, loss_ref):
                r_[...] = jnp.zeros_like(r_)

        x = x_ref[...]
        g = g_ref[...]
        h, r = _rms_fwd(x, g, d)
        hb = h.astype(bf16)
        pb = p_ref[...].astype(bf16)
        gate = _sigmoid(_dot(hb, wg_ref[...]))
        pp = jnp.concatenate([_dot(pb, wp_ref[j]) for j in range(N_DEV)], axis=-1)
        e = x + gate * pp - t_ref[...]
        loss_ref[...] += 0.5 * jnp.sum(e * e) * (1.0 / d)
        dy = e * (1.0 / d)
        dpre = (dy * pp * gate * (1.0 - gate)).astype(bf16)
        dx, dgx = _rms_bwd(_dot_nt(dpre, wg_ref[...]), x, r, g, d)
        dx_ref[...] = dy + dx
        dg_ref[...] += jnp.sum(dgx, axis=0, keepdims=True)
        acc_ref[...] += _dot_tn(hb, dpre)
        accp_ref[...] += _dot_tn(pb, (dy * gate).astype(bf16))

        @pl.when(pl.program_id(0) == s // tm - 1)
        def _():
            gwg_ref[...] = acc_ref[...].astype(bf16)
            for j in range(N_DEV):
                gwp_ref[j] = accp_ref[:, cols * j:cols * (j + 1)].astype(bf16)

    rowd = pl.BlockSpec((tm, d), lambda i: (i, 0))
    return pl.pallas_call(
        body, name="ple_loss_fwd_bwd", grid=(s // tm,),
        in_specs=[rowd, _const_spec((1, d)), _const_spec((d, d)), pl.BlockSpec((tm, PLE_DIM), lambda i: (i, 0)),
                  _const_spec((N_DEV, PLE_DIM, cols)), rowd],
        out_specs=[rowd, _acc_spec((d, d)), _acc_spec((N_DEV, PLE_DIM, cols)), _acc_spec((1, d)), _acc_spec((8, 128))],
        out_shape=[jax.ShapeDtypeStruct((s, d), f32), jax.ShapeDtypeStruct((d, d), bf16),
                   jax.ShapeDtypeStruct((N_DEV, PLE_DIM, cols), bf16), jax.ShapeDtypeStruct((1, d), f32), jax.ShapeDtypeStruct((8, 128), f32)],
        scratch_shapes=[pltpu.VMEM((d, d), f32), pltpu.VMEM((PLE_DIM, d), f32)],
        compiler_params=_cp(("arbitrary",), VMEM_LIMIT),
    )(x3, g_ple, w_pg, p, w_pp, target)


def _bwd_ffn_hidden(d3, gp, up, w_down, tm, tf):
    s, d = d3.shape
    n_i = s // tm

    def body(d3_ref, gp_ref, up_ref, wd_ref, dgp_ref, dup_ref, gw_ref, acc_ref):
        i = pl.program_id(1)

        @pl.when(i == 0)
        def _():
            acc_ref[...] = jnp.zeros_like(acc_ref)

        gp, up = gp_ref[...].astype(f32), up_ref[...].astype(f32)
        sg = _sigmoid(gp)
        silu = gp * sg
        d3b = d3_ref[...].astype(bf16)
        acc_ref[...] += _dot_tn((silu * up).astype(bf16), d3b)
        dact = _dot_nt(d3b, wd_ref[...])
        dgp_ref[...] = (dact * up * (sg * (1.0 + gp * (1.0 - sg)))).astype(bf16)
        dup_ref[...] = (dact * silu).astype(bf16)

        @pl.when(i == n_i - 1)
        def _():
            gw_ref[...] = acc_ref[...].astype(bf16)

    rowf = pl.BlockSpec((tm, tf), lambda f, i: (i, f))
    wrow = pl.BlockSpec((tf, d), lambda f, i: (f, 0))
    return pl.pallas_call(
        body, name="bwd_ffn_hidden", grid=(D_FF // tf, n_i),
        in_specs=[pl.BlockSpec((tm, d), lambda f, i: (i, 0)), rowf, rowf, wrow],
        out_specs=[rowf, rowf, wrow],
        out_shape=[jax.ShapeDtypeStruct((s, D_FF), bf16)] * 2 + [jax.ShapeDtypeStruct((D_FF, d), bf16)],
        scratch_shapes=[pltpu.VMEM((tf, d), f32)],
        compiler_params=_cp(("parallel", "arbitrary"), VMEM_LIMIT),
    )(d3, gp, up, w_down)


def _bwd_ffn_in(d3, x2, dgp, dup, g_ffn, w_gate, w_up, tm):
    s, d = x2.shape

    def body(d3_ref, x_ref, dgp_ref, dup_ref, g_ref, wg_ref, wu_ref, d2_ref, h_ref, dg_ref):
        @pl.when(pl.program_id(0) == 0)
        def _():
            dg_ref[...] = jnp.zeros_like(dg_ref)

        x, g = x_ref[...], g_ref[...]
        dh = _dot(dgp_ref[...], wg_ref[...]) + _dot(dup_ref[...], wu_ref[...])
        h, r = _rms_fwd(x, g, d)
        h_ref[...] = h.astype(bf16)
        dx, dgx = _rms_bwd(dh, x, r, g, d)
        d2_ref[...] = d3_ref[...] + dx
        dg_ref[...] += jnp.sum(dgx, axis=0, keepdims=True)

    rowd = pl.BlockSpec((tm, d), lambda i: (i, 0))
    rowf = pl.BlockSpec((tm, D_FF), lambda i: (i, 0))
    return pl.pallas_call(
        body, name="bwd_ffn_in", grid=(s // tm,),
        in_specs=[rowd, rowd, rowf, rowf, _const_spec((1, d)), _const_spec((D_FF, d)), _const_spec((D_FF, d))],
        out_specs=[rowd, rowd, _acc_spec((1, d))],
        out_shape=[jax.ShapeDtypeStruct((s, d), f32), jax.ShapeDtypeStruct((s, d), bf16), jax.ShapeDtypeStruct((1, d), f32)],
        compiler_params=_cp(("arbitrary",), VMEM_LIMIT),
    )(d3, x2, dgp, dup, g_ffn, w_gate, w_up)


def _bwd_mix(d2, w_o, o, z, g_hgo, cat, tm):
    s, d = d2.shape

    def body(d2_ref, w_ref, o_ref, hg_ref, g_ref, cat_ref, da_ref, do_ref, dhg_ref, dg_ref, gw_ref, acc_ref):
        @pl.when(pl.program_id(0) == 0)
        def _():
            dg_ref[...] = jnp.zeros_like(dg_ref)
            acc_ref[...] = jnp.zeros_like(acc_ref)

        d2b = d2_ref[...].astype(bf16)
        acc_ref[...] += _dot_tn(cat_ref[...], d2b)

        @pl.when(pl.program_id(0) == s // tm - 1)
        def _():
            gw_ref[...] = acc_ref[...].astype(bf16)
        dcat = _dot_nt(d2b, w_ref[...])
        da_ref[...] = dcat[:, 0:512].astype(bf16)
        dr = dcat[:, 512:1024]
        o, hg, g = o_ref[...], hg_ref[...], g_ref[...]
        _, on, rs, sg = _hg_out(o, hg, g)
        dhg_ref[...] = (dr * on * (sg * (1.0 + hg * (1.0 - sg)))).astype(bf16)
        don = dr * (hg * sg)
        dgs = []
        for h in range(HG_HEADS):
            cols = slice(128 * h, 128 * (h + 1))
            dx, dgx = _rms_bwd(don[:, cols], o[:, cols], rs[h], g[:, cols], 128)
            do_ref[:, cols] = dx
            dgs.append(jnp.sum(dgx, axis=0, keepdims=True))
        dg_ref[...] += jnp.concatenate(dgs, axis=-1)

    row512 = pl.BlockSpec((tm, 512), lambda i: (i, 0))
    return pl.pallas_call(
        body, name="bwd_mix", grid=(s // tm,),
        in_specs=[pl.BlockSpec((tm, d), lambda i: (i, 0)), _const_spec((d, d)), row512,
                  pl.BlockSpec((tm, 512), lambda i: (i, Z_HG // 512)), _const_spec((1, 512)), pl.BlockSpec((tm, d), lambda i: (i, 0))],
        out_specs=[row512, row512, row512, _acc_spec((1, 512)), _acc_spec((d, d))],
        out_shape=[jax.ShapeDtypeStruct((s, 512), bf16), jax.ShapeDtypeStruct((s, 512), f32), jax.ShapeDtypeStruct((s, 512), bf16),
                   jax.ShapeDtypeStruct((1, 512), f32), jax.ShapeDtypeStruct((d, d), bf16)],
        scratch_shapes=[pltpu.VMEM((d, d), f32)],
        compiler_params=_cp(("arbitrary",), VMEM_LIMIT),
    )(d2, w_o, o, z, g_hgo, cat)


def _bwd_gla(z, lb4, do, b_fwd, states):
    s = z.shape[0]
    n_chunks = s // CHUNK
    n_groups = s // GLA_ROWS
    assert n_groups % 2 == 0

    def body(hq_ref, hff_ref, hfb_ref, hi_ref, lb_ref, do_ref, b_all, st_all, dhq_ref, dhff_ref, dhfb_ref, dhi_ref, dlb_ref,
             dst_ref, dq_acc, dv_acc, dlow_ref):
        dirs = (False, True)
        masks = [_gla_masks(rev) for rev in dirs]
        lowers = [_sigmoid(lb_ref[int(rev):int(rev) + 1, :] - lb_ref[2 + int(rev):3 + int(rev), :]) for rev in dirs]
        hf_refs, dhf_refs = (hff_ref, hfb_ref), (dhff_ref, dhfb_ref)

        dst_ref[...] = jnp.zeros_like(dst_ref)
        dlow_ref[...] = jnp.zeros_like(dlow_ref)

        def make_bwd_step(first):
            def bwd_step(j, carry):
                n = n_groups - 1 - j
                for d, rev in enumerate(dirs):
                    maskf, _, tri_t, row_masks = masks[d]
                    lower = lowers[d]
                    rows, chunk0 = _gla_rows(n, n_groups, rev)
                    hq, hf = hq_ref[rows, :], hf_refs[d][rows, :]
                    q, k, _, f, sg = _gla_gates(hq, hf, lower)
                    v = hi_ref[rows, :]
                    dout = do_ref[rows, :]
                    b = b_all[d, rows, :]
                    b_last3, b_mid3 = _gla_last_mid(b, rev)
                    b_last, b_mid = _gla_per_row(b_last3), _gla_per_row(b_mid3)
                    e1, e2, e3, e4 = jnp.exp(b - b_mid), jnp.exp(b_mid - b), jnp.exp(b_last - b), jnp.exp(b)
                    decay3 = jnp.exp(b_last3)
                    qi, ki, kt, qt = q * e1, k * e2, k * e3, q * e4
                    qib, kib, ktb = qi.astype(bf16), ki.astype(bf16), kt.astype(bf16)
                    vb, dob = v.astype(bf16), dout.astype(bf16)
                    a = (_dot_nt(qib, kib) * maskf).astype(bf16)
                    da = (_dot_nt(dob, vb) * maskf).astype(bf16)
                    dqi = _dot(da, kib)
                    dki = _dot_tn(da, qib)
                    into_state = _dot_tn(dob, _gla_block_diag(qt, row_masks))
                    dst = dst_ref[d]
                    sts, dsts, ddecay = [None] * GLA_GROUP, [None] * GLA_GROUP, [None] * GLA_GROUP
                    for c in reversed(_gla_scan_order(rev)):
                        sts[c] = st_all[d, chunk0 + c]
                        dsts[c] = dst.astype(bf16)
                        ddecay[c] = jnp.sum(dst * sts[c], axis=0, keepdims=True)[None]
                        dst = dst * decay3[c] + into_state[:, 128 * c:128 * (c + 1)]
                    dst_ref[d] = dst
                    dv = _dot_tn(a, dob) + _gla_diag(_dot_nt(ktb, jnp.concatenate(dsts, axis=0)))
                    dqt = _gla_diag(_dot(dob, jnp.concatenate([x.astype(bf16) for x in sts], axis=-1)))
                    dkt = _gla_diag(_dot(vb, jnp.concatenate(dsts, axis=-1)))
                    dq = dqi * e1 + dqt * e4
                    dk = dki * e2 + dkt * e3
                    db = dqi * qi - dki * ki + dqt * qt - dkt * kt
                    dlast3 = (jnp.sum((dkt * kt).reshape(GLA_GROUP, CHUNK, 128), axis=1, keepdims=True)
                              + jnp.concatenate(ddecay, axis=0) * decay3)
                    dlogf = _tri_sum(tri_t, db) + _gla_per_row(dlast3)
                    df = dlogf / f - dk
                    dhf_refs[d][rows, :] = (df * (1.0 - lower) * sg * (1.0 - sg)).astype(bf16)
                    dlow_ref[d:d + 1, :] += jnp.sum(df * (1.0 - sg), axis=0, keepdims=True)
                    sq = _sigmoid(hq)
                    dhq = dq * (sq * (1.0 + hq * (1.0 - sq)))
                    if first:
                        dq_acc[rows, :] = dhq
                        dv_acc[rows, :] = dv
                    else:
                        dhq_ref[rows, :] = (dq_acc[rows, :] + dhq).astype(bf16)
                        dhi_ref[rows, :] = (dv_acc[rows, :] + dv).astype(bf16)
                return carry
            return bwd_step

        unroll = 4 if (n_groups // 2) % 4 == 0 else 2
        lax.fori_loop(0, n_groups // 2, make_bwd_step(True), 0, unroll=unroll)
        lax.fori_loop(n_groups // 2, n_groups, make_bwd_step(False), 0, unroll=unroll)

        for d in range(2):
            dl = dlow_ref[d:d + 1, :] * lowers[d] * (1.0 - lowers[d])
            dlb_ref[d:d + 1, :] = dl
            dlb_ref[2 + d:3 + d, :] = -dl

    col = lambda base: pl.BlockSpec((s, 128), lambda h: (0, base // 128 + h))
    return pl.pallas_call(
        body, name="bwd_gla", grid=(HG_HEADS,),
        in_specs=[col(Z_HQ), col(Z_HFF), col(Z_HFB), col(Z_HI), pl.BlockSpec((4, 128), lambda h: (0, h)), col(0),
                  pl.BlockSpec((2, s, 128), lambda h: (0, 0, h)),
                  pl.BlockSpec((None, 2, n_chunks, 128, 128), lambda h: (h, 0, 0, 0, 0), pipeline_mode=pl.Buffered(1))],
        out_specs=[col(0), col(0), col(0), col(0), pl.BlockSpec((4, 128), lambda h: (0, h))],
        out_shape=[jax.ShapeDtypeStruct((s, 512), bf16)] * 4 + [jax.ShapeDtypeStruct((4, 512), f32)],
        scratch_shapes=[pltpu.VMEM((2, 128, 128), f32), pltpu.VMEM((s, 128), f32), pltpu.VMEM((s, 128), f32),
                        pltpu.VMEM((2, 128), f32)],
        compiler_params=_cp(("parallel",), VMEM_LIMIT),
    )(z, z, z, z, lb4, do, b_fwd, states)


def _bwd_attn(q, k, v, da, a32, tq, after):
    hh, s, _ = q.shape

    n_sub = max(1, tq // ATTN_SUB_ROWS)

    def body(q_ref, k_ref, v_ref, do_ref, o_ref, after_ref, dq_ref, dk_ref, dv_ref, p_all, ds_all, dol_ref, dkt_ref, dvt_ref):
        @pl.when(pl.program_id(1) == 0)
        def _():
            dkt_ref[...] = jnp.zeros_like(dkt_ref)
            dvt_ref[...] = jnp.zeros_like(dvt_ref)

        kb, vb = k_ref[...], v_ref[...]
        for t in range(n_sub):
            rows = slice(t * (tq // n_sub), (t + 1) * (tq // n_sub))
            sc = _dot_nt(q_ref[rows, :], kb)
            p = jnp.exp2((sc - jnp.max(sc, axis=-1, keepdims=True)) * (ATTN_SCALE * LOG2_E))
            inv_l = 1.0 / jnp.sum(p, axis=-1, keepdims=True)
            p_all[rows, :] = p.astype(bf16)
            dob = do_ref[rows, :]
            dof = dob.astype(f32)
            delta = jnp.sum(dof * o_ref[rows, :], axis=-1, keepdims=True)
            ds_all[rows, :] = p_all[rows, :] * ((_dot_nt(dob, vb) - delta) * inv_l).astype(bf16)
            dq_ref[rows, :] = _dot(ds_all[rows, :], kb) * ATTN_SCALE
            dol_ref[rows, :] = (dof * inv_l).astype(bf16)
        dkt_ref[...] += _dot_tn(q_ref[...], ds_all[...])
        dvt_ref[...] += _dot_tn(dol_ref[...], p_all[...])

        @pl.when(pl.program_id(1) == s // tq - 1)
        def _():
            dk_ref[...] = dkt_ref[...].T * ATTN_SCALE
            dv_ref[...] = dvt_ref[...].T

    return pl.pallas_call(
        body, name="bwd_attn", grid=(hh, s // tq),
        in_specs=[pl.BlockSpec((None, tq, QK_PAD), lambda h, i: (h, i, 0)),
                  pl.BlockSpec((None, s, QK_PAD), lambda h, i: (h, 0, 0)),
                  pl.BlockSpec((None, s, V_HEAD), lambda h, i: (h, 0, 0)),
                  pl.BlockSpec((tq, V_HEAD), lambda h, i: (i, h)), pl.BlockSpec((tq, V_HEAD), lambda h, i: (i, h)),
                  pl.BlockSpec(memory_space=pl.ANY)],
        out_specs=[pl.BlockSpec((None, tq, QK_PAD), lambda h, i: (h, i, 0)),
                   pl.BlockSpec((None, s, QK_PAD), lambda h, i: (h, 0, 0)),
                   pl.BlockSpec((None, s, V_HEAD), lambda h, i: (h, 0, 0))],
        out_shape=[jax.ShapeDtypeStruct((hh, s, QK_PAD), f32), jax.ShapeDtypeStruct((hh, s, QK_PAD), f32),
                   jax.ShapeDtypeStruct((hh, s, V_HEAD), f32)],
        scratch_shapes=[pltpu.VMEM((tq, s), bf16), pltpu.VMEM((tq, s), bf16), pltpu.VMEM((tq, V_HEAD), bf16),
                        pltpu.VMEM((QK_PAD, s), f32), pltpu.VMEM((V_HEAD, s), f32)],
        compiler_params=_cp(("parallel", "arbitrary"), VMEM_LIMIT),
    )(q, k, v, da, a32, after)


def _bwd_mla_proj(z, dq, dk, dv, cosb, sina, sinb, g_qa, g_kva, wqb, wkvb, g_qn, g_kn, tm):
    s = z.shape[0]
    hh = MLA_HEADS

    def body(cq_ref, ckv_ref, kr_ref, dq_ref, dk_ref, dv_ref, c_ref, sa_ref, sb_ref, gqa_ref, gkva_ref, wqb_ref, wkvb_ref,
             gqn_ref, gkn_ref, dz_ref, gwqb_ref, gwkvb_ref, dgqa_ref, dgkva_ref, dgqn_ref, dgkn_ref, dq0_ref, dkv0_ref):
        @pl.when(pl.program_id(0) == 0)
        def _():
            for r in (gwqb_ref, gwkvb_ref, dgqa_ref, dgkva_ref, dgqn_ref, dgkn_ref):
                r[...] = jnp.zeros_like(r)

        cq, ckv, kr = cq_ref[...], ckv_ref[...], kr_ref[...]
        gqa, gkva, gqn, gkn = gqa_ref[...], gkva_ref[...], gqn_ref[...], gkn_ref[...]
        cqn_b, rq, ckvn_b, rkv, q0, kv0 = _mla_qk_fwd(cq, ckv, gqa, gkva, wqb_ref[...], wkvb_ref[...])
        c, sa, sb = c_ref[...], -sa_ref[...], -sb_ref[...]
        kr_sq = jnp.sum(kr * kr, axis=-1, keepdims=True)
        dkr = jnp.zeros_like(kr)
        dgqn = jnp.zeros((1, QK_PAD), f32)
        dgkn = jnp.zeros((1, QK_PAD), f32)
        for h in range(hh):
            qh = q0[:, QK_PAD * h:QK_PAD * (h + 1)]
            rh = lax.rsqrt(jnp.sum(qh * qh, axis=-1, keepdims=True) * (1.0 / QK_HEAD) + EPS)
            dqh = dq_ref[h]
            dqn = jnp.concatenate([dqh[:, 0:128], _rope(dqh[:, 128:256], c, sa, sb)], axis=-1)
            dq0h, dgx = _rms_bwd(dqn, qh, rh, gqn, QK_HEAD)
            dq0_ref[:, QK_PAD * h:QK_PAD * (h + 1)] = dq0h.astype(bf16)
            dgqn = dgqn + jnp.sum(dgx, axis=0, keepdims=True)

            kn_ = kv0[:, 256 * h:256 * h + 128]
            k0 = jnp.concatenate([kn_, kr], axis=-1)
            rk = lax.rsqrt((jnp.sum(kn_ * kn_, axis=-1, keepdims=True) + kr_sq) * (1.0 / QK_HEAD) + EPS)
            dkh = dk_ref[h]
            dkn = jnp.concatenate([dkh[:, 0:128], _rope(dkh[:, 128:256], c, sa, sb)], axis=-1)
            dk0, dgx = _rms_bwd(dkn, k0, rk, gkn, QK_HEAD)
            dgkn = dgkn + jnp.sum(dgx, axis=0, keepdims=True)
            dkv0_ref[:, 256 * h:256 * h + 128] = dk0[:, 0:128].astype(bf16)
            dkv0_ref[:, 256 * h + 128:256 * h + 256] = dv_ref[h].astype(bf16)
            dkr = dkr + dk0[:, 128:256]
        dgqn_ref[...] += dgqn
        dgkn_ref[...] += dgkn
        gwqb_ref[...] += _dot_tn(cqn_b, dq0_ref[...])
        gwkvb_ref[...] += _dot_tn(ckvn_b, dkv0_ref[...])
        dcq, dgx = _rms_bwd(_dot_nt(dq0_ref[...], wqb_ref[...]), cq, rq, gqa, Q_LORA)
        dgqa_ref[...] += jnp.sum(dgx, axis=0, keepdims=True)
        dckv, dgx = _rms_bwd(_dot_nt(dkv0_ref[...], wkvb_ref[...]), ckv, rkv, gkva, KV_LORA)
        dgkva_ref[...] += jnp.sum(dgx, axis=0, keepdims=True)
        dz_ref[:, 0:256] = dcq.astype(bf16)
        dz_ref[:, 256:512] = dckv.astype(bf16)
        dz_ref[:, 512:640] = dkr.astype(bf16)

    row128 = pl.BlockSpec((tm, 128), lambda i: (i, 0))
    hd = lambda w: pl.BlockSpec((hh, tm, w), lambda i: (0, i, 0))
    return pl.pallas_call(
        body, name="bwd_mla_proj", grid=(s // tm,),
        in_specs=[pl.BlockSpec((tm, 256), lambda i: (i, Z_CQ // 256)), pl.BlockSpec((tm, 256), lambda i: (i, Z_CKV // 256)),
                  pl.BlockSpec((tm, 128), lambda i: (i, Z_KR // 128)), hd(QK_PAD), hd(QK_PAD), hd(V_HEAD),
                  row128, row128, row128,
                  _const_spec((1, 256)), _const_spec((1, 256)), _const_spec((256, 1024)), _const_spec((256, 1024)),
                  _const_spec((1, 256)), _const_spec((1, 256))],
        out_specs=[pl.BlockSpec((tm, 640), lambda i: (i, 0)), _acc_spec((256, 1024)), _acc_spec((256, 1024)),
                   _acc_spec((1, 256)), _acc_spec((1, 256)), _acc_spec((1, 256)), _acc_spec((1, 256))],
        out_shape=[jax.ShapeDtypeStruct((s, 640), bf16), jax.ShapeDtypeStruct((256, 1024), f32), jax.ShapeDtypeStruct((256, 1024), f32)]
        + [jax.ShapeDtypeStruct((1, 256), f32)] * 4,
        scratch_shapes=[pltpu.VMEM((tm, 1024), bf16), pltpu.VMEM((tm, 1024), bf16)],
        compiler_params=_cp(("arbitrary",), VMEM_LIMIT),
    )(z, z, z, dq, dk, dv, cosb, sina, sinb, g_qa, g_kva, wqb, wkvb, g_qn, g_kn)


def _bwd_in(segments, wz, x, g_mix, d2, tm):
    s, d = x.shape
    n_seg = len(segments)

    def body(*refs):
        dz_refs, w_refs = refs[:n_seg], refs[n_seg:2 * n_seg]
        x_ref, g_ref, d2_ref, gx_ref, dg_ref = refs[2 * n_seg:]

        @pl.when(pl.program_id(0) == 0)
        def _():
            dg_ref[...] = jnp.zeros_like(dg_ref)

        dh = _dot(dz_refs[0][...], w_refs[0][...])
        for a_ref, w_ref in zip(dz_refs[1:], w_refs[1:]):
            dh = dh + _dot(a_ref[...], w_ref[...])
        x, g = x_ref[...], g_ref[...]
        r = lax.rsqrt(jnp.sum(x * x, axis=-1, keepdims=True) * (1.0 / d) + EPS)
        dx, dgx = _rms_bwd(dh, x, r, g, d)
        gx_ref[...] = d2_ref[...] + dx
        dg_ref[...] += jnp.sum(dgx, axis=0, keepdims=True)

    rowd = pl.BlockSpec((tm, d), lambda i: (i, 0))
    dz_specs = [pl.BlockSpec((tm, w), functools.partial(lambda i, j: (i, j), j=ja)) for _, w, ja, _ in segments]
    w_specs = [pl.BlockSpec((w, d), functools.partial(lambda i, j: (j, 0), j=jw), pipeline_mode=pl.Buffered(1))
               for _, w, _, jw in segments]
    return pl.pallas_call(
        body, name="bwd_in", grid=(s // tm,),
        in_specs=dz_specs + w_specs + [rowd, _const_spec((1, d)), rowd],
        out_specs=[rowd, _acc_spec((1, d))],
        out_shape=[jax.ShapeDtypeStruct((s, d), f32), jax.ShapeDtypeStruct((1, d), f32)],
        compiler_params=_cp(("arbitrary",), VMEM_LIMIT),
    )(*[a for a, _, _, _ in segments], *([wz] * n_seg), x, g_mix, d2)


def _mm_tn_many(a, bs, name, tm, transposed=False):
    kk, m = a.shape
    n_b = len(bs)
    tk = min(1024, kk)
    n_k = kk // tk

    def body(a_ref, *refs):
        b_refs, o_refs, acc_refs = refs[:n_b], refs[n_b:2 * n_b], refs[2 * n_b:]

        @pl.when(pl.program_id(1) == 0)
        def _():
            for acc in acc_refs:
                acc[...] = jnp.zeros_like(acc)
        a_blk = a_ref[...].astype(bf16)
        for b_ref, acc in zip(b_refs, acc_refs):
            acc[...] += _dot_tn(a_blk, b_ref[...].astype(bf16))

        @pl.when(pl.program_id(1) == n_k - 1)
        def _():
            for o_ref, acc in zip(o_refs, acc_refs):
                o_ref[...] = (acc[...].T if transposed else acc[...]).astype(bf16)

    if transposed:
        out_specs = [pl.BlockSpec((b.shape[1], tm), lambda i, k: (0, i)) for b in bs]
        out_shape = [jax.ShapeDtypeStruct((b.shape[1], m), bf16) for b in bs]
    else:
        out_specs = [pl.BlockSpec((tm, b.shape[1]), lambda i, k: (i, 0)) for b in bs]
        out_shape = [jax.ShapeDtypeStruct((m, b.shape[1]), bf16) for b in bs]
    return pl.pallas_call(
        body, name=name, grid=(m // tm, n_k),
        in_specs=[pl.BlockSpec((tk, tm), lambda i, k: (k, i))] + [pl.BlockSpec((tk, b.shape[1]), lambda i, k: (k, 0)) for b in bs],
        out_specs=out_specs,
        out_shape=out_shape,
        scratch_shapes=[pltpu.VMEM((tm, b.shape[1]), f32) for b in bs],
        compiler_params=_cp(("parallel", "arbitrary"), VMEM_LIMIT),
    )(a, *bs)


def _rope_tables(positions):
    inv_freq = ROPE_THETA ** (-jnp.arange(0, QK_ROPE, 2, dtype=f32) / QK_ROPE)
    ang = positions.astype(f32)[:, None] * inv_freq
    cos, sin = jnp.cos(ang), jnp.sin(ang)
    zero = jnp.zeros_like(cos)
    return (jnp.concatenate([cos, cos, zero, zero], axis=1), jnp.concatenate([zero, sin, zero, zero], axis=1),
            jnp.concatenate([-sin, zero, zero, zero], axis=1))


def _pad256(g):
    return jnp.pad(g.reshape(1, QK_HEAD), ((0, 0), (0, QK_PAD - QK_HEAD)))


RELAYOUT_BLOCKS = 8
FIRST = ("w_in", "w_qb", "w_kvb", "lb_param")
SECOND = ("w_o", "w_gate", "w_up", "w_down", "w_ple_gate", "w_ple_proj")
ROW_SHARDED = ("w_o", "w_down", "w_ple_gate")


def _col_moves(j):
    width = BIG["w_in"][0]
    lo = width * j
    w_in = [(max(lo, a) - lo, min(lo + width, b) - lo, d + max(lo, a) - a)
            for a, b, d in Z_SEGMENTS if max(lo, a) < min(lo + width, b)]
    head, half = divmod(j, 2)
    whole = lambda n: [(0, BIG[n][1], BIG[n][1] * j)]
    return {"w_in": w_in, "w_qb": [(0, 96, QK_PAD * head + 96 * half)], "w_kvb": whole("w_kvb"),
            "w_ple_proj": whole("w_ple_proj"), "lb_param": whole("lb_param")}


def _kernel_shape(name):
    rows, cols = BIG[name]
    if name == "w_in":
        return (Z_W, cols)
    return (rows, MLA_HEADS * QK_PAD if name == "w_qb" else N_DEV * cols)


def _relayout_specs(names, by_dev):
    specs = []
    for n in names:
        rows, cols = BIG[n]
        if n == "lb_param":
            specs.append(_acc_spec((N_DEV, rows, cols) if by_dev else _kernel_shape(n)))
        elif n == "w_in":
            cb = cols // RELAYOUT_BLOCKS
            specs.append(pl.BlockSpec((N_DEV, rows, cb), lambda i: (0, 0, i)) if by_dev else pl.BlockSpec((Z_W, cb), lambda i: (0, i)))
        elif by_dev:
            specs.append(pl.BlockSpec((N_DEV, rows // RELAYOUT_BLOCKS, cols), lambda i: (0, i, 0)))
        else:
            specs.append(pl.BlockSpec((rows // RELAYOUT_BLOCKS, _kernel_shape(n)[1]), lambda i: (i, 0)))
    return specs


def _weights_in(gathered, names, name):
    n = len(names)

    def body(*refs):
        ins, outs = dict(zip(names, refs[:n])), dict(zip(names, refs[n:]))
        if "w_in" in outs:
            outs["w_in"][Z_KR + QK_ROPE:Z_W, :] = jnp.zeros((Z_W - Z_KR - QK_ROPE, outs["w_in"].shape[1]), bf16)
        if "w_qb" in outs:
            for h in range(MLA_HEADS):
                outs["w_qb"][:, QK_PAD * h + QK_HEAD:QK_PAD * (h + 1)] = jnp.zeros((outs["w_qb"].shape[0], QK_PAD - QK_HEAD), bf16)
        for j in range(N_DEV):
            for wn, moves in _col_moves(j).items():
                if wn in outs:
                    for s0, s1, d0 in moves:
                        if wn == "w_in":
                            outs[wn][d0:d0 + s1 - s0, :] = ins[wn][j, s0:s1, :]
                        else:
                            outs[wn][:, d0:d0 + s1 - s0] = ins[wn][j, :, s0:s1]

    outs = pl.pallas_call(
        body, name=name, grid=(RELAYOUT_BLOCKS,), in_specs=_relayout_specs(names, True), out_specs=_relayout_specs(names, False),
        out_shape=[jax.ShapeDtypeStruct(_kernel_shape(wn), gathered[wn].dtype) for wn in names],
        compiler_params=_cp(("arbitrary",), VMEM_LIMIT),
    )(*[gathered[wn] for wn in names])
    return dict(zip(names, outs))


def _grads_out(sources, names, name):
    pieces = [(wn, start, arr) for wn in names for start, arr in sources[wn]]
    n_in = len(pieces)

    def body(*refs):
        outs = dict(zip(names, refs[n_in:]))

        def cols(wn, c0, c1):
            for (pn, start, arr), ref in zip(pieces, refs[:n_in]):
                if pn == wn and start <= c0 and c1 <= start + arr.shape[0 if wn == "w_in" else 1]:
                    return ref[c0 - start:c1 - start, :] if wn == "w_in" else ref[:, c0 - start:c1 - start]

        for j in range(N_DEV):
            for wn, moves in _col_moves(j).items():
                if wn in outs:
                    for s0, s1, d0 in moves:
                        if wn == "w_in":
                            outs[wn][j, s0:s1, :] = cols(wn, d0, d0 + s1 - s0).astype(bf16)
                        else:
                            outs[wn][j, :, s0:s1] = cols(wn, d0, d0 + s1 - s0).astype(bf16)

    def in_spec(wn, arr):
        if wn == "lb_param":
            return _acc_spec(arr.shape)
        if wn == "w_in":
            return pl.BlockSpec((arr.shape[0], arr.shape[1] // RELAYOUT_BLOCKS), lambda i: (0, i))
        return pl.BlockSpec((arr.shape[0] // RELAYOUT_BLOCKS, arr.shape[1]), lambda i: (i, 0))

    in_specs = [in_spec(wn, arr) for wn, _, arr in pieces]
    outs = pl.pallas_call(
        body, name=name, grid=(RELAYOUT_BLOCKS,), in_specs=in_specs, out_specs=_relayout_specs(names, True),
        out_shape=[jax.ShapeDtypeStruct((N_DEV, *BIG[wn]), bf16) for wn in names],
        compiler_params=_cp(("arbitrary",), VMEM_LIMIT),
    )(*[arr for _, _, arr in pieces])
    return dict(zip(names, outs))


def kernel(x, p, positions, g_mix, w_in, g_qa, g_kva, w_qb, w_kvb, g_qn, g_kn, lb_param, g_hgo, w_o, g_ffn, w_gate, w_up, w_down, g_ple, w_ple_gate, w_ple_proj, loss_target, m_g_mix, m_w_in, m_g_qa, m_g_kva, m_w_qb, m_w_kvb, m_g_qn, m_g_kn, m_lb_param, m_g_hgo, m_w_o, m_g_ffn, m_w_gate, m_w_up, m_w_down, m_g_ple, m_w_ple_gate, m_w_ple_proj, v_g_mix, v_w_in, v_g_qa, v_g_kva, v_w_qb, v_w_kvb, v_g_qn, v_g_kn, v_lb_param, v_g_hgo, v_w_o, v_g_ffn, v_w_gate, v_w_up, v_w_down, v_g_ple, v_w_ple_gate, v_w_ple_proj):
    w_all = dict(g_mix=g_mix, g_qa=g_qa, g_kva=g_kva, g_qn=g_qn, g_kn=g_kn, g_hgo=g_hgo, g_ffn=g_ffn, g_ple=g_ple,
                 w_in=w_in, w_qb=w_qb, w_kvb=w_kvb, w_o=w_o, w_gate=w_gate, w_up=w_up, w_down=w_down,
                 w_ple_gate=w_ple_gate, w_ple_proj=w_ple_proj, lb_param=lb_param)
    m_all = dict(g_mix=m_g_mix, g_qa=m_g_qa, g_kva=m_g_kva, g_qn=m_g_qn, g_kn=m_g_kn, g_hgo=m_g_hgo, g_ffn=m_g_ffn,
                 g_ple=m_g_ple, w_in=m_w_in, w_qb=m_w_qb, w_kvb=m_w_kvb, w_o=m_w_o, w_gate=m_w_gate, w_up=m_w_up,
                 w_down=m_w_down, w_ple_gate=m_w_ple_gate, w_ple_proj=m_w_ple_proj, lb_param=m_lb_param)
    v_all = dict(g_mix=v_g_mix, g_qa=v_g_qa, g_kva=v_g_kva, g_qn=v_g_qn, g_kn=v_g_kn, g_hgo=v_g_hgo, g_ffn=v_g_ffn,
                 g_ple=v_g_ple, w_in=v_w_in, w_qb=v_w_qb, w_kvb=v_w_kvb, w_o=v_w_o, w_gate=v_w_gate, w_up=v_w_up,
                 w_down=v_w_down, w_ple_gate=v_w_ple_gate, w_ple_proj=v_w_ple_proj, lb_param=v_lb_param)
    me_idx = jnp.stack([_me()]).astype(jnp.int32)
    x, p, positions, target = x[0], p[0, 0], positions[0], loss_target[0]
    s = x.shape[0]
    tm, tm_ffn, tq_f, tq_b = min(512, s), min(1024, s), min(2048, s), min(1024, s)
    g_mix, g_qa, g_kva, g_qn, g_kn, g_hgo, g_ffn, g_ple = (w_all[n].reshape(1, -1) for n in SMALL)
    g_qn_p, g_kn_p = _pad256(g_qn), _pad256(g_kn)
    cosb, sina, sinb = _rope_tables(positions)
    as_shard = lambda n, a: a[0].T if n in TRANSPOSED else a.reshape(BIG[n])
    shard = lambda n: as_shard(n, w_all[n])

    first = _all_gather([shard(n) for n in FIRST], [f32 if n == "lb_param" else bf16 for n in FIRST], "ag_first")
    lands = _cast_to_slot([shard(n) for n in SECOND], me_idx, first[0])
    ag2, token = _exchange_start([], lands, "ag_second_start")
    wk = _weights_in(dict(zip(FIRST, first)), FIRST, "weights_in_first")
    wz, wqb, wkvb, lb4 = (wk[n] for n in FIRST)

    h1, z = _fwd_in(x, g_mix, wz, tm)
    q, k, v = _fwd_mla_proj(z, cosb + token[0, 0], sina, sinb, g_qa, g_kva, wqb, wkvb, g_qn_p, g_kn_p, tm)
    o, gla_b, gla_states = _fwd_gla(z, lb4)
    a, a32 = _fwd_attn(q, k, v, tq_f, o)

    second = dict(zip(SECOND, _exchange_wait(ag2, [a, o], "ag_second_wait")[1]))
    w_pp = second["w_ple_proj"]
    w_o, w_down, w_pg, w_gate, w_up = (second[n].reshape(N_DEV * BIG[n][0], BIG[n][1]) for n in ROW_SHARDED + ("w_gate", "w_up"))

    x2, cat = _fwd_mix(a, o, z, g_hgo, x, w_o, tm)
    x3, gp, up = _fwd_ffn(x2, g_ffn, w_gate, w_up, w_down, tm)
    d3, gw_pg, gw_pp, dg_ple, loss_tile = _ple_loss_fwd_bwd(x3, g_ple, w_pg, p, w_pp, target, tm)
    dgp, dup, gw_down = _bwd_ffn_hidden(d3, gp, up, w_down, tm, D_FF // 2)
    d2, h2, dg_ffn = _bwd_ffn_in(d3, x2, dgp, dup, g_ffn, w_gate, w_up, tm)
    da, do, dz_hg, dg_hgo, gw_o = _bwd_mix(d2, w_o, o, z, g_hgo, cat, tm)

    gw_gate, gw_up = _mm_tn_many(h2, [dgp, dup], "dw_gate_up", 512, transposed=True)
    blocks = {"w_ple_proj": gw_pp}
    row_grads = {"w_o": gw_o, "w_down": gw_down, "w_ple_gate": gw_pg, "w_gate": gw_gate, "w_up": gw_up}
    blocks.update({n: g.reshape(N_DEV, *BIG[n]) for n, g in row_grads.items()})
    empty = lambda names: [lax.empty((N_PEERS, *BIG[n]), bf16) for n in names]
    rs2, token = _exchange_start([blocks[n] for n in SECOND], empty(SECOND), "rs_second_start")

    dq, dk, dv = _bwd_attn(q, k, v, da, a32, tq_b, token)
    dz_hq, dz_hff, dz_hfb, dz_hi, dlb4 = _bwd_gla(z, lb4 + token[0, 0], do, gla_b, gla_states)
    dz_mla, gw_qb, gw_kvb, dg_qa, dg_kva, dg_qn, dg_kn = _bwd_mla_proj(
        z, dq, dk, dv, cosb, sina, sinb, g_qa, g_kva, wqb, wkvb, g_qn_p, g_kn_p, tm)

    gz = list(zip((Z_HQ, Z_HFF, Z_HFB, Z_HI, Z_HG, Z_CQ),
                  _mm_tn_many(h1, [dz_hq, dz_hff, dz_hfb, dz_hi, dz_hg, dz_mla], "dw_in", 1024, transposed=True)))
    blocks1 = _grads_out({"w_in": gz, "w_qb": [(0, gw_qb)], "w_kvb": [(0, gw_kvb)],
                          "lb_param": [(0, dlb4)]}, FIRST, "grads_out_first")
    rs1, token = _exchange_start([blocks1[n] for n in FIRST], empty(FIRST), "rs_first_start")

    result = {}

    def adam(names, lands, src, n_blocks, after=()):
        flipped = TRANSPOSED
        given = lambda arrs: [arrs[n][0].T if n in flipped else arrs[n] for n in names]
        outs = _adam_shards(me_idx, [src[n] for n in names], lands, given(w_all), given(m_all), given(v_all), n_blocks,
                            "adamw_" + names[0], after)
        for n, o in zip(names, outs):
            result[n] = [t.T[None] for t in o] if n in flipped else o
        return outs[0][0]

    blocks2, lands2 = (dict(zip(SECOND, arrs)) for arrs in _exchange_wait(rs2, [token], "rs_second_wait"))
    by2 = ("w_down", "w_gate", "w_up")
    by8 = tuple(n for n in SECOND if n not in by2)
    done = [adam(by8, [lands2[n] for n in by8], blocks2, 8), adam(by2, [lands2[n] for n in by2], blocks2, 2)]

    segments = [(dz_hq, 512, 0, Z_HQ // 512), (dz_hff, 512, 0, Z_HFF // 512), (dz_hfb, 512, 0, Z_HFB // 512),
                (dz_hi, 512, 0, Z_HI // 512), (dz_hg, 512, 0, Z_HG // 512), (dz_mla, 640, 0, Z_CQ // 640)]
    grad_x, dg_mix = _bwd_in(segments, wz, x, g_mix + token[0, 0], d2, tm)
    dgains = (dg_mix, dg_qa, dg_kva, dg_qn, dg_kn, dg_hgo, dg_ffn, dg_ple)

    vec = jnp.concatenate(list(dgains) + [loss_tile[0:1]], axis=1)
    parts = _all_gather([vec], [f32], "ag_gains")[0]
    outs, loss_row = _adam_gains(parts, [w_all[n] for n in SMALL], [m_all[n] for n in SMALL], [v_all[n] for n in SMALL])
    result.update(zip(SMALL, outs))

    blocks1, lands1 = _exchange_wait(rs1, [grad_x, loss_row, *done], "rs_first_wait")
    adam(FIRST, lands1, dict(zip(FIRST, blocks1)), 8)

    order = ("g_mix", "w_in", "g_qa", "g_kva", "w_qb", "w_kvb", "g_qn", "g_kn", "lb_param", "g_hgo", "w_o", "g_ffn",
             "w_gate", "w_up", "w_down", "g_ple", "w_ple_gate", "w_ple_proj")
    return (loss_row[0, 0], grad_x[None], *[result[n][k] for k in range(4) for n in order])
```

```python
import functools
import math

import jax
import jax.numpy as jnp
from jax import lax
from jax.experimental import pallas as pl
from jax.experimental.pallas import tpu as pltpu

f32 = jnp.float32
bf16 = jnp.bfloat16

N_DEV = 8
MLA_HEADS = 4
QK_NOPE = 128
QK_ROPE = 64
QK_HEAD = QK_NOPE + QK_ROPE
QK_PAD = 256
V_HEAD = 128
Q_LORA = 256
KV_LORA = 256
HG_HEADS = 4
CHUNK = 64
D_FF = 2816
PLE_DIM = 256
ROPE_THETA = 10000.0
EPS = 1e-6
ATTN_SCALE = QK_HEAD ** -0.5
LOG2_E = math.log2(math.e)
ATTN_SUB_ROWS = 256
Z_HQ, Z_HFF, Z_HFB, Z_HI, Z_HG, Z_CQ, Z_CKV, Z_KR, Z_W = 0, 512, 1024, 1536, 2048, 2560, 2816, 3072, 3200

ADAM_LR, ADAM_B1, ADAM_B2, ADAM_EPS, ADAM_WD, ADAM_STEP = 0.001, 0.9, 0.999, 1e-08, 0.01, 10

LANES = 128
BIG = {"w_in": (392, 1024), "w_qb": (256, 96), "w_kvb": (256, 128), "w_o": (128, 1024), "w_gate": (352, 1024),
       "w_up": (352, 1024), "w_down": (352, 1024), "w_ple_gate": (128, 1024), "w_ple_proj": (256, 128),
       "lb_param": (4, 64)}
TRANSPOSED = ("w_gate", "w_up", "w_in")
SMALL = {"g_mix": (0, 1024), "g_qa": (1024, 256), "g_kva": (1280, 256), "g_qn": (1536, 192), "g_kn": (1792, 192),
         "g_hgo": (2048, 512), "g_ffn": (2560, 1024), "g_ple": (3584, 1024)}
LOSS_OFF = 4608
GAIN_VEC = LOSS_OFF + LANES
Z_SEGMENTS = ((0, 256, Z_CQ), (256, 512, Z_CKV), (512, 576, Z_KR), (576, 1088, Z_HQ), (1088, 1600, Z_HFF),
              (1600, 2112, Z_HFB), (2112, 2624, Z_HI), (2624, 3136, Z_HG))

VMEM_LIMIT = 56 * 1024 * 1024
MESH = pl.DeviceIdType.MESH


def _cp(sem=None, vmem=None):
    return pltpu.CompilerParams(dimension_semantics=sem, vmem_limit_bytes=vmem)


def _const_spec(shape):
    nd = len(shape)
    return pl.BlockSpec(shape, lambda *_: (0,) * nd, pipeline_mode=pl.Buffered(1))


def _acc_spec(shape):
    nd = len(shape)
    return pl.BlockSpec(shape, lambda *_: (0,) * nd)


def _sigmoid(x):
    return jax.nn.sigmoid(x)


def _dot(a, b):
    return jnp.dot(a, b, preferred_element_type=f32)


def _dot_nt(a, b):
    return lax.dot_general(a, b, (((1,), (1,)), ((), ())), preferred_element_type=f32)


def _dot_tn(a, b):
    return lax.dot_general(a, b, (((0,), (0,)), ((), ())), preferred_element_type=f32)


def _rms_fwd(x, g, width):
    r = lax.rsqrt(jnp.sum(x * x, axis=-1, keepdims=True) * (1.0 / width) + EPS)
    return x * r * g, r


def _rms_bwd(dy, x, r, g, width):
    u = dy * g
    dx = r * u - x * (r * r * r) * (jnp.sum(u * x, axis=-1, keepdims=True) * (1.0 / width))
    return dx, dy * x * r


class _Both:
    def __init__(self, *copies):
        self.copies = copies

    def start(self):
        for cp in self.copies:
            cp.start()

    def wait(self):
        for cp in self.copies:
            cp.wait()


def _rope(b, c, sa, sb):
    return b * c + pltpu.roll(b, 32, 1) * sa + pltpu.roll(b, 96, 1) * sb


def _all_gather(shards, dtypes, name):
    n = len(shards)

    def body(*refs):
        in_refs, out_refs, stage = refs[:n], refs[n:2 * n], refs[2 * n:3 * n]
        send_sems, recv_sems, local_sems = refs[3 * n:]
        for w in range(n):
            stage[w][...] = in_refs[w][...].astype(stage[w].dtype)
        x, y, c = lax.axis_index("x"), lax.axis_index("y"), lax.axis_index("c")
        me, sibling = (x, y, c), (x, y, 1 - c)
        chips = [(1 - x, y), (x, 1 - y), (1 - x, 1 - y)]

        def slot(w, px, py, pc):
            return out_refs[w].at[4 * px + 2 * py + pc]

        def copy(w, k, block, to, src=None):
            return pltpu.make_async_remote_copy(
                src_ref=slot(w, *block) if src is None else src, dst_ref=slot(w, *block),
                send_sem=send_sems.at[w, k], recv_sem=recv_sems.at[w, k], device_id=to, device_id_type=MESH)

        first = []
        for j, chip in enumerate(chips):
            first += [copy(w, 1 + j, me, (*chip, c), src=stage[w]) for w in range(n)]
        first += [copy(w, 0, me, sibling, src=stage[w]) for w in range(n)]
        mine = [pltpu.make_async_copy(stage[w], slot(w, *me), local_sems.at[w]) for w in range(n)]
        for cp in first + mine:
            cp.start()
        passed = []
        for j, chip in enumerate(chips):
            for w in range(n):
                copy(w, 1 + j, (*chip, c), me).wait_recv()
                passed.append(copy(w, 4 + j, (*chip, c), sibling))
                passed[-1].start()
        for w in range(n):
            copy(w, 0, sibling, me).wait_recv()
        for j, chip in enumerate(chips):
            for w in range(n):
                copy(w, 4 + j, (*chip, 1 - c), me).wait_recv()
        for cp in first + passed:
            cp.wait_send()
        for cp in mine:
            cp.wait()

    return pl.pallas_call(
        body, name=name,
        out_shape=[jax.ShapeDtypeStruct((N_DEV, *s.shape), dt) for s, dt in zip(shards, dtypes)],
        in_specs=[pl.BlockSpec(memory_space=pltpu.VMEM)] * n,
        out_specs=[pl.BlockSpec(memory_space=pl.ANY)] * n,
        scratch_shapes=[pltpu.VMEM(s.shape, dt) for s, dt in zip(shards, dtypes)]
        + [pltpu.SemaphoreType.DMA((n, 7)), pltpu.SemaphoreType.DMA((n, 7)), pltpu.SemaphoreType.DMA((n,))],
        compiler_params=_cp(None, VMEM_LIMIT),
    )(*shards)


N_PEERS = N_DEV - 1
HBM_SPEC = pl.BlockSpec(memory_space=pltpu.HBM)
SEM_SPEC = pl.BlockSpec(memory_space=pltpu.SEMAPHORE)
DATAFLOW = pltpu.SideEffectType.DATAFLOW_SIDE_EFFECTING


def _me():
    return 4 * lax.axis_index("x") + 2 * lax.axis_index("y") + lax.axis_index("c")


def _peer(k):
    x, y, c = lax.axis_index("x"), lax.axis_index("y"), lax.axis_index("c")
    px = 1 - x if k & 4 else x
    py = 1 - y if k & 2 else y
    pc = 1 - c if k & 1 else c
    return (px, py, pc), 4 * px + 2 * py + pc


def _exchange_copies(src_refs, land_refs, send_sems, recv_sems, gather):
    cps = []
    me = _me()
    for k in range(1, N_DEV):
        peer, peer_idx = _peer(k)
        for w, land in enumerate(land_refs):
            src = land.at[me] if gather else src_refs[w].at[peer_idx]
            dst = land.at[me] if gather else land.at[k - 1]
            cps.append(pltpu.make_async_remote_copy(
                src_ref=src, dst_ref=dst, send_sem=send_sems.at[N_PEERS * w + k - 1], recv_sem=recv_sems.at[N_PEERS * w + k - 1],
                device_id=peer, device_id_type=MESH))
    return cps


def _exchange_start(srcs, lands, name):
    n_src, n = len(srcs), len(lands)

    def body(*refs):
        src_refs, land_refs = refs[:n_src], refs[n_src:n_src + n]
        send_sems, recv_sems = refs[n_src + n], refs[n_src + n + 1]
        token = refs[-1]
        for cp in _exchange_copies(src_refs, land_refs, send_sems, recv_sems, gather=not n_src):
            cp.start()
        token[...] = jnp.zeros_like(token)

    arrays = [pltpu.with_memory_space_constraint(a, pltpu.HBM) for a in (*srcs, *lands)]
    outs = pl.pallas_call(
        body, name=name,
        out_shape=(pltpu.SemaphoreType.DMA((n * N_PEERS,)), pltpu.SemaphoreType.DMA((n * N_PEERS,)),
                   *[pltpu.HBM(a.shape, a.dtype) for a in arrays], jax.ShapeDtypeStruct((8, LANES), f32)),
        in_specs=[HBM_SPEC] * len(arrays),
        out_specs=(SEM_SPEC, SEM_SPEC, *[HBM_SPEC] * len(arrays), pl.BlockSpec(memory_space=pltpu.VMEM)),
        input_output_aliases={i: 2 + i for i in range(len(arrays))},
        compiler_params=pltpu.CompilerParams(has_side_effects=DATAFLOW),
    )(*arrays)
    return (outs[0], outs[1], outs[2:2 + n_src], outs[2 + n_src:2 + n_src + n]), outs[-1]


def _exchange_wait(state, after, name):
    send_sems, recv_sems, srcs, lands = state
    n_src, n = len(srcs), len(lands)

    def body(*refs):
        src_refs, land_refs = refs[:n_src], refs[n_src:n_src + n]
        send_ref, recv_ref = refs[n_src + n], refs[n_src + n + 1]
        for cp in _exchange_copies(src_refs, land_refs, send_ref, recv_ref, gather=not n_src):
            cp.wait_send()
            cp.wait_recv()

    arrays = (*srcs, *lands)
    outs = pl.pallas_call(
        body, name=name,
        out_shape=tuple(pltpu.HBM(a.shape, a.dtype) for a in arrays),
        in_specs=[HBM_SPEC] * len(arrays) + [SEM_SPEC, SEM_SPEC] + [pl.BlockSpec(memory_space=pl.ANY)] * len(after),
        out_specs=tuple([HBM_SPEC] * len(arrays)),
        input_output_aliases={i: i for i in range(len(arrays))},
        compiler_params=pltpu.CompilerParams(has_side_effects=DATAFLOW),
    )(*arrays, send_sems, recv_sems, *after)
    return outs[:n_src], outs[n_src:]


def _cast_to_slot(shards, me_idx, after):
    n = len(shards)

    def body(i_ref, *refs):
        for w in range(n):
            refs[n + 1 + w][...] = refs[w][...].astype(bf16)

    return pl.pallas_call(
        body, name="cast_to_slot",
        grid_spec=pltpu.PrefetchScalarGridSpec(
            num_scalar_prefetch=1, grid=(1,),
            in_specs=[pl.BlockSpec(s.shape, lambda i, m: (0, 0)) for s in shards] + [pl.BlockSpec(memory_space=pl.ANY)],
            out_specs=[pl.BlockSpec((None, *s.shape), lambda i, m: (m[0], 0, 0)) for s in shards]),
        out_shape=[jax.ShapeDtypeStruct((N_DEV, *s.shape), bf16) for s in shards],
        compiler_params=_cp(("arbitrary",), VMEM_LIMIT),
    )(me_idx, *shards, after)


def _row_block(rows, n_blocks):
    return (rows // n_blocks, True) if rows % (16 * n_blocks) == 0 else (rows, False)


def _adam_math(w, g, m, v):
    m = ADAM_B1 * m + (1.0 - ADAM_B1) * g
    v = ADAM_B2 * v + (1.0 - ADAM_B2) * (g * g)
    m_hat = m / (1.0 - ADAM_B1 ** ADAM_STEP)
    v_hat = v / (1.0 - ADAM_B2 ** ADAM_STEP)
    delta = -ADAM_LR * (m_hat / (jnp.sqrt(v_hat) + ADAM_EPS) + ADAM_WD * w)
    return delta, m, v


def _adam_shards(me_idx, blocks, lands, ws, ms, vs, n_blocks, name, after=()):
    n = len(blocks)

    def body(i_ref, *refs):
        ins, outs = refs[:5 * n], refs[5 * n + len(after):]
        for w in range(n):
            g_ref, b_ref, w_ref, m_ref, v_ref = (ins[t * n + w] for t in range(5))
            g = g_ref[...].astype(f32)
            for k in range(N_PEERS):
                g = g + b_ref[k].astype(f32)
            if len(w_ref.shape) == 2:
                pieces = [(slice(None), g)]
            else:
                pieces = [(a, g[2 * a:2 * a + 2]) for a in range(2)]
            for at, gp in pieces:
                vals = (gp,) + _adam_math(w_ref[at], gp, m_ref[at], v_ref[at])
                for t, val in enumerate(vals):
                    outs[4 * w + t][at] = val

    specs = [[] for _ in range(5)]
    out_specs, out_shape = [], []
    for g, wt in zip(blocks, ws):
        rows, cols = g.shape[1:]
        rb, cut = _row_block(rows, n_blocks)
        if not cut and wt.ndim == 2 and cols % (LANES * n_blocks) == 0:
            cb = cols // n_blocks
            specs[0].append(pl.BlockSpec((None, rows, cb), lambda i, s: (s[0], 0, i)))
            specs[1].append(pl.BlockSpec((N_PEERS, rows, cb), lambda i, s: (0, 0, i)))
            shard = pl.BlockSpec((rows, cb), lambda i, s: (0, i))
            for t in (2, 3, 4):
                specs[t].append(shard)
            out_specs += [shard] * 4
            out_shape += [jax.ShapeDtypeStruct(wt.shape, f32)] * 4
            continue
        specs[0].append(pl.BlockSpec((None, rb, cols), functools.partial(lambda i, s, cut: (s[0], i if cut else 0, 0), cut=cut)))
        specs[1].append(pl.BlockSpec((N_PEERS, rb, cols), functools.partial(lambda i, s, cut: (0, i if cut else 0, 0), cut=cut)))
        if wt.ndim == 2:
            shard = pl.BlockSpec((rb, cols), functools.partial(lambda i, s, cut: (i if cut else 0, 0), cut=cut))
        elif wt.shape[0] == 1:
            shard = pl.BlockSpec((None, rb, cols), functools.partial(lambda i, s, cut: (0, i if cut else 0, 0), cut=cut))
        else:
            shard = pl.BlockSpec(wt.shape, functools.partial(lambda i, s, nd: (0,) * nd, nd=wt.ndim))
        for t in (2, 3, 4):
            specs[t].append(shard)
        out_specs += [shard] * 4
        out_shape += [jax.ShapeDtypeStruct(wt.shape, f32)] * 4
    outs = pl.pallas_call(
        body, name=name,
        grid_spec=pltpu.PrefetchScalarGridSpec(
            num_scalar_prefetch=1, grid=(n_blocks,), in_specs=sum(specs, []) + [pl.BlockSpec(memory_space=pl.ANY)] * len(after),
            out_specs=out_specs),
        out_shape=out_shape,
        compiler_params=_cp(("arbitrary",), VMEM_LIMIT),
    )(me_idx, *blocks, *lands, *ws, *ms, *vs, *after)
    return [outs[4 * w:4 * w + 4] for w in range(n)]


def _adam_gains(parts, ws, ms, vs):
    n = len(ws)

    def body(p_ref, *refs):
        ins, outs = refs[:3 * n], refs[3 * n:]
        g_all = p_ref[0]
        for k in range(1, N_DEV):
            g_all = g_all + p_ref[k]
        for w, (off, lanes) in enumerate(SMALL.values()):
            w_ref, m_ref, v_ref = ins[w], ins[n + w], ins[2 * n + w]
            if len(w_ref.shape) == 2:
                pieces = [(slice(None), off, lanes)]
            else:
                pieces = [((slice(None), h), off + LANES * h, LANES) for h in range(w_ref.shape[1])]
            for at, o, ln in pieces:
                g = g_all[:, o:o + ln]
                vals = (g,) + _adam_math(w_ref[at], g, m_ref[at], v_ref[at])
                for t, val in enumerate(vals):
                    outs[4 * w + t][at] = val
        outs[4 * n][...] = g_all[:, LOSS_OFF:LOSS_OFF + LANES]

    out_shape = sum([[jax.ShapeDtypeStruct(w.shape, f32)] * 4 for w in ws], []) + [jax.ShapeDtypeStruct((1, LANES), f32)]
    outs = pl.pallas_call(body, name="adamw_gains", out_shape=out_shape)(parts, *ws, *ms, *vs)
    return [outs[4 * w:4 * w + 4] for w in range(n)], outs[4 * n]


def _fwd_in(x, g_mix, wz, tm):
    s, d = x.shape

    def body(x_ref, g_ref, w_ref, h_ref, z_ref):
        h, _ = _rms_fwd(x_ref[...], g_ref[...], d)
        hb = h.astype(bf16)
        h_ref[...] = hb
        z_ref[...] = _dot_nt(hb, w_ref[...])

    return pl.pallas_call(
        body, name="fwd_in", grid=(s // tm,),
        in_specs=[pl.BlockSpec((tm, d), lambda i: (i, 0)), _const_spec((1, d)), _const_spec((Z_W, d))],
        out_specs=[pl.BlockSpec((tm, d), lambda i: (i, 0)), pl.BlockSpec((tm, Z_W), lambda i: (i, 0))],
        out_shape=[jax.ShapeDtypeStruct((s, d), bf16), jax.ShapeDtypeStruct((s, Z_W), f32)],
        compiler_params=_cp(("parallel",), VMEM_LIMIT),
    )(x, g_mix, wz)


def _mla_qk_fwd(cq, ckv, g_qa, g_kva, wqb, wkvb):
    cqn, rq = _rms_fwd(cq, g_qa, Q_LORA)
    ckvn, rkv = _rms_fwd(ckv, g_kva, KV_LORA)
    cqn_b, ckvn_b = cqn.astype(bf16), ckvn.astype(bf16)
    q0 = _dot(cqn_b, wqb)
    kv0 = _dot(ckvn_b, wkvb)
    return cqn_b, rq, ckvn_b, rkv, q0, kv0


def _fwd_mla_proj(z, cosb, sina, sinb, g_qa, g_kva, wqb, wkvb, g_qn, g_kn, tm):
    s = z.shape[0]
    hh = MLA_HEADS

    def body(cq_ref, ckv_ref, kr_ref, c_ref, sa_ref, sb_ref, gqa_ref, gkva_ref, wqb_ref, wkvb_ref, gqn_ref, gkn_ref,
             q_ref, k_ref, v_ref):
        _, _, _, _, q0, kv0 = _mla_qk_fwd(cq_ref[...], ckv_ref[...], gqa_ref[...], gkva_ref[...], wqb_ref[...], wkvb_ref[...])
        kr = kr_ref[...]
        c, sa, sb = c_ref[...], sa_ref[...], sb_ref[...]
        gqn, gkn = gqn_ref[...], gkn_ref[...]
        kr_sq = jnp.sum(kr * kr, axis=-1, keepdims=True)
        for h in range(hh):
            qh = q0[:, QK_PAD * h:QK_PAD * (h + 1)]
            qn, _ = _rms_fwd(qh, gqn, QK_HEAD)
            q_ref[h, :, 0:128] = qn[:, 0:128].astype(bf16)
            q_ref[h, :, 128:256] = _rope(qn[:, 128:256], c, sa, sb).astype(bf16)
            kn_ = kv0[:, 256 * h:256 * h + 128]
            rk = lax.rsqrt((jnp.sum(kn_ * kn_, axis=-1, keepdims=True) + kr_sq) * (1.0 / QK_HEAD) + EPS)
            k_ref[h, :, 0:128] = (kn_ * rk * gkn[:, 0:128]).astype(bf16)
            k_ref[h, :, 128:256] = _rope(kr * rk * gkn[:, 128:256], c, sa, sb).astype(bf16)
            v_ref[h] = kv0[:, 256 * h + 128:256 * h + 256].astype(bf16)

    row128 = pl.BlockSpec((tm, 128), lambda i: (i, 0))
    return pl.pallas_call(
        body, name="fwd_mla_proj", grid=(s // tm,),
        in_specs=[pl.BlockSpec((tm, 256), lambda i: (i, Z_CQ // 256)), pl.BlockSpec((tm, 256), lambda i: (i, Z_CKV // 256)),
                  pl.BlockSpec((tm, 128), lambda i: (i, Z_KR // 128)), row128, row128, row128,
                  _const_spec((1, 256)), _const_spec((1, 256)), _const_spec((256, 1024)), _const_spec((256, 1024)),
                  _const_spec((1, 256)), _const_spec((1, 256))],
        out_specs=[pl.BlockSpec((hh, tm, QK_PAD), lambda i: (0, i, 0)), pl.BlockSpec((hh, tm, QK_PAD), lambda i: (0, i, 0)),
                   pl.BlockSpec((hh, tm, V_HEAD), lambda i: (0, i, 0))],
        out_shape=[jax.ShapeDtypeStruct((hh, s, QK_PAD), bf16), jax.ShapeDtypeStruct((hh, s, QK_PAD), bf16),
                   jax.ShapeDtypeStruct((hh, s, V_HEAD), bf16)],
        compiler_params=_cp(("parallel",), VMEM_LIMIT),
    )(z, z, z, cosb, sina, sinb, g_qa, g_kva, wqb, wkvb, g_qn, g_kn)


def _fwd_attn(q, k, v, tq, after):
    hh, s, _ = q.shape

    n_sub = max(1, tq // ATTN_SUB_ROWS)

    def body(q_ref, k_ref, v_ref, after_ref, o_ref, o32_ref):
        for t in range(n_sub):
            rows = slice(t * (tq // n_sub), (t + 1) * (tq // n_sub))
            sc = _dot_nt(q_ref[rows, :], k_ref[...])
            p = jnp.exp2((sc - jnp.max(sc, axis=-1, keepdims=True)) * (ATTN_SCALE * LOG2_E))
            l = jnp.sum(p, axis=-1, keepdims=True)
            o = _dot(p.astype(bf16), v_ref[...]) * (1.0 / l)
            o_ref[rows, :] = o.astype(bf16)
            o32_ref[rows, :] = o

    out = pl.BlockSpec((tq, V_HEAD), lambda h, i: (i, h))
    return pl.pallas_call(
        body, name="fwd_attn", grid=(hh, s // tq),
        in_specs=[pl.BlockSpec((None, tq, QK_PAD), lambda h, i: (h, i, 0)),
                  pl.BlockSpec((None, s, QK_PAD), lambda h, i: (h, 0, 0)),
                  pl.BlockSpec((None, s, V_HEAD), lambda h, i: (h, 0, 0)), pl.BlockSpec(memory_space=pl.ANY)],
        out_specs=[out, out],
        out_shape=[jax.ShapeDtypeStruct((s, hh * V_HEAD), bf16), jax.ShapeDtypeStruct((s, hh * V_HEAD), f32)],
        compiler_params=_cp(("parallel", "parallel"), VMEM_LIMIT),
    )(q, k, v, after)


def _split3(x):
    hi = x.astype(bf16)
    r1 = x - hi.astype(f32)
    mid = r1.astype(bf16)
    lo = (r1 - mid.astype(f32)).astype(bf16)
    return jnp.concatenate([hi, mid, lo], axis=-1)


def _tri_sum(tri, x):
    y = _dot(tri, _split3(x))
    return y[:, 0:128] + y[:, 128:256] + y[:, 256:384]


GLA_GROUP = 4
GLA_ROWS = GLA_GROUP * CHUNK
GLA_HEADS_PER_STEP = 2


def _gla_masks(rev):
    row = lax.broadcasted_iota(jnp.int32, (GLA_ROWS, GLA_ROWS), 0)
    col = lax.broadcasted_iota(jnp.int32, (GLA_ROWS, GLA_ROWS), 1)
    shift = CHUNK.bit_length() - 1
    same = (jnp.right_shift(row, shift) == jnp.right_shift(col, shift)).astype(f32)
    lower, upper = (row >= col).astype(f32) * same, (row <= col).astype(f32) * same
    keep, keep_t = (upper, lower) if rev else (lower, upper)
    chunk_of = jnp.right_shift(lax.broadcasted_iota(jnp.int32, (GLA_ROWS, 1), 0), shift)
    return keep, keep.astype(bf16), keep_t.astype(bf16), [(chunk_of == c).astype(f32) for c in range(GLA_GROUP)]


def _gla_gates(hq, hf, lower):
    sg = _sigmoid(hf)
    f = lower + (1.0 - lower) * sg
    return hq * _sigmoid(hq), 1.0 - f, jnp.log(f), f, sg


def _gla_last_mid(b, rev):
    b3 = b.reshape(GLA_GROUP, CHUNK, 128)
    last, mid = (0, CHUNK // 2) if rev else (CHUNK - 1, CHUNK // 2 - 1)
    return b3[:, last:last + 1, :], b3[:, mid:mid + 1, :]


def _gla_per_row(per_chunk):
    return jnp.broadcast_to(per_chunk, (GLA_GROUP, CHUNK, 128)).reshape(GLA_ROWS, 128)


def _gla_block_diag(x, row_masks):
    return jnp.concatenate([(x * m).astype(bf16) for m in row_masks], axis=-1)


def _gla_diag(y):
    return jnp.concatenate([y[CHUNK * c:CHUNK * (c + 1), 128 * c:128 * (c + 1)] for c in range(GLA_GROUP)], axis=0)


def _gla_rows(n, n_groups, rev):
    ne = n_groups - 1 - n if rev else n
    return pl.ds(pl.multiple_of(ne * GLA_ROWS, GLA_ROWS), GLA_ROWS), ne * GLA_GROUP


def _gla_scan_order(rev):
    return tuple(reversed(range(GLA_GROUP))) if rev else tuple(range(GLA_GROUP))


def _fwd_gla(z, lb4):
    s = z.shape[0]
    n_groups = s // GLA_ROWS
    assert n_groups % 2 == 0
    hp = GLA_HEADS_PER_STEP
    chains = [(hh, rev) for hh in range(hp) for rev in (False, True)]

    def body(hq_ref, hff_ref, hfb_ref, hi_ref, lb_ref, o_ref, b_ref, states_ref, st_ref, stage_ref, b_stage, sems):
        st_ref[...] = jnp.zeros_like(st_ref)
        masks = {rev: _gla_masks(rev) for rev in (False, True)}
        lowers = [_sigmoid(lb_ref[int(rev):int(rev) + 1, 128 * hh:128 * (hh + 1)]
                           - lb_ref[2 + int(rev):3 + int(rev), 128 * hh:128 * (hh + 1)]) for hh, rev in chains]

        def states_out(slot, ci, chunk0):
            hh, rev = chains[ci]
            head = pl.program_id(0) * hp + hh
            rows = pl.ds(pl.multiple_of(chunk0 * CHUNK, GLA_ROWS), GLA_ROWS)
            return _Both(
                pltpu.make_async_copy(stage_ref.at[slot, ci], states_ref.at[head, int(rev), pl.ds(chunk0, GLA_GROUP)],
                                      sems.at[slot, ci]),
                pltpu.make_async_copy(b_stage.at[slot, ci], b_ref.at[int(rev), rows, pl.ds(pl.multiple_of(head * 128, 128), 128)],
                                      sems.at[slot, len(chains) + ci]))

        def make_step(first):
            def step(n, carry):
                slot = n % 2

                @pl.when(n >= 2)
                def _():
                    for ci in range(len(chains)):
                        states_out(slot, ci, 0).wait()

                for ci, (hh, rev) in enumerate(chains):
                    cols = slice(128 * hh, 128 * (hh + 1))
                    rows, chunk0 = _gla_rows(n, n_groups, rev)
                    maskf, tri, _, row_masks = masks[rev]
                    hf_ref = hfb_ref if rev else hff_ref
                    q, k, logf, _, _ = _gla_gates(hq_ref[rows, cols], hf_ref[rows, cols], lowers[ci])
                    vb = hi_ref[rows, cols].astype(bf16)
                    b = _tri_sum(tri, logf)
                    b_stage[slot, ci] = b
                    b_last3, b_mid3 = _gla_last_mid(b, rev)
                    b_last, b_mid = _gla_per_row(b_last3), _gla_per_row(b_mid3)
                    qi = (q * jnp.exp(b - b_mid)).astype(bf16)
                    ki = (k * jnp.exp(b_mid - b)).astype(bf16)
                    a = (_dot_nt(qi, ki) * maskf).astype(bf16)
                    kv = _dot_tn(vb, _gla_block_diag(k * jnp.exp(b_last - b), row_masks))
                    decay3 = jnp.exp(b_last3)
                    st = st_ref[ci]
                    before = [None] * GLA_GROUP
                    for c in _gla_scan_order(rev):
                        stage_ref[slot, ci, c] = st
                        before[c] = st.astype(bf16)
                        st = st * decay3[c] + kv[:, 128 * c:128 * (c + 1)]
                    st_ref[ci] = st
                    states_out(slot, ci, chunk0).start()
                    inter = _dot_nt((q * jnp.exp(b)).astype(bf16), jnp.concatenate(before, axis=0))
                    o = _dot(a, vb) + _gla_diag(inter)
                    if first:
                        o_ref[rows, cols] = o
                    else:
                        o_ref[rows, cols] += o
                return carry
            return step

        lax.fori_loop(0, n_groups // 2, make_step(True), 0)
        lax.fori_loop(n_groups // 2, n_groups, make_step(False), 0)
        for slot in range(2):
            for ci in range(len(chains)):
                states_out(slot, ci, 0).wait()

    w = 128 * hp
    col = lambda base: pl.BlockSpec((s, w), lambda h: (0, base // w + h))
    return pl.pallas_call(
        body, name="fwd_gla", grid=(HG_HEADS // hp,),
        in_specs=[col(Z_HQ), col(Z_HFF), col(Z_HFB), col(Z_HI), pl.BlockSpec((4, w), lambda h: (0, h))],
        out_specs=[pl.BlockSpec((s, w), lambda h: (0, h)), pl.BlockSpec(memory_space=pl.ANY), pl.BlockSpec(memory_space=pl.ANY)],
        out_shape=[jax.ShapeDtypeStruct((s, HG_HEADS * 128), f32), jax.ShapeDtypeStruct((2, s, HG_HEADS * 128), f32),
                   jax.ShapeDtypeStruct((HG_HEADS, 2, s // CHUNK, 128, 128), f32)],
        scratch_shapes=[pltpu.VMEM((len(chains), 128, 128), f32), pltpu.VMEM((2, len(chains), GLA_GROUP, 128, 128), f32),
                        pltpu.VMEM((2, len(chains), GLA_ROWS, 128), f32), pltpu.SemaphoreType.DMA((2, 2 * len(chains)))],
        compiler_params=_cp(("parallel",), VMEM_LIMIT),
    )(z, z, z, z, lb4)


def _hg_out(o, hg, g_hgo):
    outs, ons, rs = [], [], []
    for h in range(HG_HEADS):
        oh = o[:, 128 * h:128 * (h + 1)]
        on, r = _rms_fwd(oh, g_hgo[:, 128 * h:128 * (h + 1)], 128)
        ons.append(on)
        rs.append(r)
    on = jnp.concatenate(ons, axis=-1)
    sg = _sigmoid(hg)
    return on * (hg * sg), on, rs, sg


def _fwd_mix(a, o, z, g_hgo, x, w_o, tm):
    s, d = x.shape

    def body(a_ref, o_ref, hg_ref, g_ref, x_ref, w_ref, x2_ref, cat_ref):
        r, _, _, _ = _hg_out(o_ref[...], hg_ref[...], g_ref[...])
        cat = jnp.concatenate([a_ref[...], r.astype(bf16)], axis=-1)
        cat_ref[...] = cat
        x2_ref[...] = x_ref[...] + _dot(cat, w_ref[...])

    row512 = pl.BlockSpec((tm, 512), lambda i: (i, 0))
    rowd = pl.BlockSpec((tm, d), lambda i: (i, 0))
    return pl.pallas_call(
        body, name="fwd_mix", grid=(s // tm,),
        in_specs=[row512, row512, pl.BlockSpec((tm, 512), lambda i: (i, Z_HG // 512)), _const_spec((1, 512)), rowd,
                  _const_spec((d, d))],
        out_specs=[rowd, rowd],
        out_shape=[jax.ShapeDtypeStruct((s, d), f32), jax.ShapeDtypeStruct((s, d), bf16)],
        compiler_params=_cp(("parallel",), VMEM_LIMIT),
    )(a, o, z, g_hgo, x, w_o)


def _fwd_ffn(x2, g_ffn, w_gate, w_up, w_down, tm):
    s, d = x2.shape

    def body(x_ref, g_ref, wg_ref, wu_ref, wd_ref, x3_ref, gp_ref, up_ref):
        x = x_ref[...]
        h, _ = _rms_fwd(x, g_ref[...], d)
        hb = h.astype(bf16)
        gp = _dot_nt(hb, wg_ref[...])
        up = _dot_nt(hb, wu_ref[...])
        gp_ref[...] = gp.astype(bf16)
        up_ref[...] = up.astype(bf16)
        act = (gp * _sigmoid(gp) * up).astype(bf16)
        x3_ref[...] = x + _dot(act, wd_ref[...])

    rowd = pl.BlockSpec((tm, d), lambda i: (i, 0))
    rowf = pl.BlockSpec((tm, D_FF), lambda i: (i, 0))
    return pl.pallas_call(
        body, name="fwd_ffn", grid=(s // tm,),
        in_specs=[rowd, _const_spec((1, d)), _const_spec((D_FF, d)), _const_spec((D_FF, d)), _const_spec((D_FF, d))],
        out_specs=[rowd, rowf, rowf],
        out_shape=[jax.ShapeDtypeStruct((s, d), f32), jax.ShapeDtypeStruct((s, D_FF), bf16),
                   jax.ShapeDtypeStruct((s, D_FF), bf16)],
        compiler_params=_cp(("parallel",), VMEM_LIMIT),
    )(x2, g_ffn, w_gate, w_up, w_down)


def _ple_loss_fwd_bwd(x3, g_ple, w_pg, p, w_pp, target, tm):
    s, d = x3.shape
    cols = BIG["w_ple_proj"][1]

    def body(x_ref, g_ref, wg_ref, p_ref, wp_ref, t_ref, dx_ref, gwg_ref, gwp_ref, dg_ref, loss_ref, acc_ref, accp_ref):
        @pl.when(pl.program_id(0) == 0)
        def _():
            for r_ in (acc_ref, accp_ref, dg_ref, loss_ref):
                r_[...] = jnp.zeros_like(r_)

        x = x_ref[...]
        g = g_ref[...]
        h, r = _rms_fwd(x, g, d)
        hb = h.astype(bf16)
        pb = p_ref[...].astype(bf16)
        gate = _sigmoid(_dot(hb, wg_ref[...]))
        pp = jnp.concatenate([_dot(pb, wp_ref[j]) for j in range(N_DEV)], axis=-1)
        e = x + gate * pp - t_ref[...]
        loss_ref[...] += 0.5 * jnp.sum(e * e) * (1.0 / d)
        dy = e * (1.0 / d)
        dpre = (dy * pp * gate * (1.0 - gate)).astype(bf16)
        dx, dgx = _rms_bwd(_dot_nt(dpre, wg_ref[...]), x, r, g, d)
        dx_ref[...] = dy + dx
        dg_ref[...] += jnp.sum(dgx, axis=0, keepdims=True)
        acc_ref[...] += _dot_tn(hb, dpre)
        accp_ref[...] += _dot_tn(pb, (dy * gate).astype(bf16))

        @pl.when(pl.program_id(0) == s // tm - 1)
        def _():
            gwg_ref[...] = acc_ref[...].astype(bf16)
            for j in range(N_DEV):
                gwp_ref[j] = accp_ref[:, cols * j:cols * (j + 1)].astype(bf16)

    rowd = pl.BlockSpec((tm, d), lambda i: (i, 0))
    return pl.pallas_call(
        body, name="ple_loss_fwd_bwd", grid=(s // tm,),
        in_specs=[rowd, _const_spec((1, d)), _const_spec((d, d)), pl.BlockSpec((tm, PLE_DIM), lambda i: (i, 0)),
                  _const_spec((N_DEV, PLE_DIM, cols)), rowd],
        out_specs=[rowd, _acc_spec((d, d)), _acc_spec((N_DEV, PLE_DIM, cols)), _acc_spec((1, d)), _acc_spec((8, 128))],
        out_shape=[jax.ShapeDtypeStruct((s, d), f32), jax.ShapeDtypeStruct((d, d), bf16),
                   jax.ShapeDtypeStruct((N_DEV, PLE_DIM, cols), bf16), jax.ShapeDtypeStruct((1, d), f32), jax.ShapeDtypeStruct((8, 128), f32)],
        scratch_shapes=[pltpu.VMEM((d, d), f32), pltpu.VMEM((PLE_DIM, d), f32)],
        compiler_params=_cp(("arbitrary",), VMEM_LIMIT),
    )(x3, g_ple, w_pg, p, w_pp, target)


def _bwd_ffn_hidden(d3, gp, up, w_down, tm, tf):
    s, d = d3.shape
    n_i = s // tm

    def body(d3_ref, gp_ref, up_ref, wd_ref, dgp_ref, dup_ref, gw_ref, acc_ref):
        i = pl.program_id(1)

        @pl.when(i == 0)
        def _():
            acc_ref[...] = jnp.zeros_like(acc_ref)

        gp, up = gp_ref[...].astype(f32), up_ref[...].astype(f32)
        sg = _sigmoid(gp)
        silu = gp * sg
        d3b = d3_ref[...].astype(bf16)
        acc_ref[...] += _dot_tn((silu * up).astype(bf16), d3b)
        dact = _dot_nt(d3b, wd_ref[...])
        dgp_ref[...] = (dact * up * (sg * (1.0 + gp * (1.0 - sg)))).astype(bf16)
        dup_ref[...] = (dact * silu).astype(bf16)

        @pl.when(i == n_i - 1)
        def _():
            gw_ref[...] = acc_ref[...].astype(bf16)

    rowf = pl.BlockSpec((tm, tf), lambda f, i: (i, f))
    wrow = pl.BlockSpec((tf, d), lambda f, i: (f, 0))
    return pl.pallas_call(
        body, name="bwd_ffn_hidden", grid=(D_FF // tf, n_i),
        in_specs=[pl.BlockSpec((tm, d), lambda f, i: (i, 0)), rowf, rowf, wrow],
        out_specs=[rowf, rowf, wrow],
        out_shape=[jax.ShapeDtypeStruct((s, D_FF), bf16)] * 2 + [jax.ShapeDtypeStruct((D_FF, d), bf16)],
        scratch_shapes=[pltpu.VMEM((tf, d), f32)],
        compiler_params=_cp(("parallel", "arbitrary"), VMEM_LIMIT),
    )(d3, gp, up, w_down)


def _bwd_ffn_in(d3, x2, dgp, dup, g_ffn, w_gate, w_up, tm):
    s, d = x2.shape

    def body(d3_ref, x_ref, dgp_ref, dup_ref, g_ref, wg_ref, wu_ref, d2_ref, h_ref, dg_ref):
        @pl.when(pl.program_id(0) == 0)
        def _():
            dg_ref[...] = jnp.zeros_like(dg_ref)

        x, g = x_ref[...], g_ref[...]
        dh = _dot(dgp_ref[...], wg_ref[...]) + _dot(dup_ref[...], wu_ref[...])
        h, r = _rms_fwd(x, g, d)
        h_ref[...] = h.astype(bf16)
        dx, dgx = _rms_bwd(dh, x, r, g, d)
        d2_ref[...] = d3_ref[...] + dx
        dg_ref[...] += jnp.sum(dgx, axis=0, keepdims=True)

    rowd = pl.BlockSpec((tm, d), lambda i: (i, 0))
    rowf = pl.BlockSpec((tm, D_FF), lambda i: (i, 0))
    return pl.pallas_call(
        body, name="bwd_ffn_in", grid=(s // tm,),
        in_specs=[rowd, rowd, rowf, rowf, _const_spec((1, d)), _const_spec((D_FF, d)), _const_spec((D_FF, d))],
        out_specs=[rowd, rowd, _acc_spec((1, d))],
        out_shape=[jax.ShapeDtypeStruct((s, d), f32), jax.ShapeDtypeStruct((s, d), bf16), jax.ShapeDtypeStruct((1, d), f32)],
        compiler_params=_cp(("arbitrary",), VMEM_LIMIT),
    )(d3, x2, dgp, dup, g_ffn, w_gate, w_up)


def _bwd_mix(d2, w_o, o, z, g_hgo, cat, tm):
    s, d = d2.shape

    def body(d2_ref, w_ref, o_ref, hg_ref, g_ref, cat_ref, da_ref, do_ref, dhg_ref, dg_ref, gw_ref, acc_ref):
        @pl.when(pl.program_id(0) == 0)
        def _():
            dg_ref[...] = jnp.zeros_like(dg_ref)
            acc_ref[...] = jnp.zeros_like(acc_ref)

        d2b = d2_ref[...].astype(bf16)
        acc_ref[...] += _dot_tn(cat_ref[...], d2b)

        @pl.when(pl.program_id(0) == s // tm - 1)
        def _():
            gw_ref[...] = acc_ref[...].astype(bf16)
        dcat = _dot_nt(d2b, w_ref[...])
        da_ref[...] = dcat[:, 0:512].astype(bf16)
        dr = dcat[:, 512:1024]
        o, hg, g = o_ref[...], hg_ref[...], g_ref[...]
        _, on, rs, sg = _hg_out(o, hg, g)
        dhg_ref[...] = (dr * on * (sg * (1.0 + hg * (1.0 - sg)))).astype(bf16)
        don = dr * (hg * sg)
        dgs = []
        for h in range(HG_HEADS):
            cols = slice(128 * h, 128 * (h + 1))
            dx, dgx = _rms_bwd(don[:, cols], o[:, cols], rs[h], g[:, cols], 128)
            do_ref[:, cols] = dx
            dgs.append(jnp.sum(dgx, axis=0, keepdims=True))
        dg_ref[...] += jnp.concatenate(dgs, axis=-1)

    row512 = pl.BlockSpec((tm, 512), lambda i: (i, 0))
    return pl.pallas_call(
        body, name="bwd_mix", grid=(s // tm,),
        in_specs=[pl.BlockSpec((tm, d), lambda i: (i, 0)), _const_spec((d, d)), row512,
                  pl.BlockSpec((tm, 512), lambda i: (i, Z_HG // 512)), _const_spec((1, 512)), pl.BlockSpec((tm, d), lambda i: (i, 0))],
        out_specs=[row512, row512, row512, _acc_spec((1, 512)), _acc_spec((d, d))],
        out_shape=[jax.ShapeDtypeStruct((s, 512), bf16), jax.ShapeDtypeStruct((s, 512), f32), jax.ShapeDtypeStruct((s, 512), bf16),
                   jax.ShapeDtypeStruct((1, 512), f32), jax.ShapeDtypeStruct((d, d), bf16)],
        scratch_shapes=[pltpu.VMEM((d, d), f32)],
        compiler_params=_cp(("arbitrary",), VMEM_LIMIT),
    )(d2, w_o, o, z, g_hgo, cat)


def _bwd_gla(z, lb4, do, b_fwd, states):
    s = z.shape[0]
    n_chunks = s // CHUNK
    n_groups = s // GLA_ROWS
    assert n_groups % 2 == 0

    def body(hq_ref, hff_ref, hfb_ref, hi_ref, lb_ref, do_ref, b_all, st_all, dhq_ref, dhff_ref, dhfb_ref, dhi_ref, dlb_ref,
             dst_ref, dq_acc, dv_acc, dlow_ref):
        dirs = (False, True)
        masks = [_gla_masks(rev) for rev in dirs]
        lowers = [_sigmoid(lb_ref[int(rev):int(rev) + 1, :] - lb_ref[2 + int(rev):3 + int(rev), :]) for rev in dirs]
        hf_refs, dhf_refs = (hff_ref, hfb_ref), (dhff_ref, dhfb_ref)

        dst_ref[...] = jnp.zeros_like(dst_ref)
        dlow_ref[...] = jnp.zeros_like(dlow_ref)

        def make_bwd_step(first):
            def bwd_step(j, carry):
                n = n_groups - 1 - j
                for d, rev in enumerate(dirs):
                    maskf, _, tri_t, row_masks = masks[d]
                    lower = lowers[d]
                    rows, chunk0 = _gla_rows(n, n_groups, rev)
                    hq, hf = hq_ref[rows, :], hf_refs[d][rows, :]
                    q, k, _, f, sg = _gla_gates(hq, hf, lower)
                    v = hi_ref[rows, :]
                    dout = do_ref[rows, :]
                    b = b_all[d, rows, :]
                    b_last3, b_mid3 = _gla_last_mid(b, rev)
                    b_last, b_mid = _gla_per_row(b_last3), _gla_per_row(b_mid3)
                    e1, e2, e3, e4 = jnp.exp(b - b_mid), jnp.exp(b_mid - b), jnp.exp(b_last - b), jnp.exp(b)
                    decay3 = jnp.exp(b_last3)
                    qi, ki, kt, qt = q * e1, k * e2, k * e3, q * e4
                    qib, kib, ktb = qi.astype(bf16), ki.astype(bf16), kt.astype(bf16)
                    vb, dob = v.astype(bf16), dout.astype(bf16)
                    a = (_dot_nt(qib, kib) * maskf).astype(bf16)
                    da = (_dot_nt(dob, vb) * maskf).astype(bf16)
                    dqi = _dot(da, kib)
                    dki = _dot_tn(da, qib)
                    into_state = _dot_tn(dob, _gla_block_diag(qt, row_masks))
                    dst = dst_ref[d]
                    sts, dsts, ddecay = [None] * GLA_GROUP, [None] * GLA_GROUP, [None] * GLA_GROUP
                    for c in reversed(_gla_scan_order(rev)):
                        sts[c] = st_all[d, chunk0 + c]
                        dsts[c] = dst.astype(bf16)
                        ddecay[c] = jnp.sum(dst * sts[c], axis=0, keepdims=True)[None]
                        dst = dst * decay3[c] + into_state[:, 128 * c:128 * (c + 1)]
                    dst_ref[d] = dst
                    dv = _dot_tn(a, dob) + _gla_diag(_dot_nt(ktb, jnp.concatenate(dsts, axis=0)))
                    dqt = _gla_diag(_dot(dob, jnp.concatenate([x.astype(bf16) for x in sts], axis=-1)))
                    dkt = _gla_diag(_dot(vb, jnp.concatenate(dsts, axis=-1)))
                    dq = dqi * e1 + dqt * e4
                    dk = dki * e2 + dkt * e3
                    db = dqi * qi - dki * ki + dqt * qt - dkt * kt
                    dlast3 = (jnp.sum((dkt * kt).reshape(GLA_GROUP, CHUNK, 128), axis=1, keepdims=True)
                              + jnp.concatenate(ddecay, axis=0) * decay3)
                    dlogf = _tri_sum(tri_t, db) + _gla_per_row(dlast3)
                    df = dlogf / f - dk
                    dhf_refs[d][rows, :] = (df * (1.0 - lower) * sg * (1.0 - sg)).astype(bf16)
                    dlow_ref[d:d + 1, :] += jnp.sum(df * (1.0 - sg), axis=0, keepdims=True)
                    sq = _sigmoid(hq)
                    dhq = dq * (sq * (1.0 + hq * (1.0 - sq)))
                    if first:
                        dq_acc[rows, :] = dhq
                        dv_acc[rows, :] = dv
                    else:
                        dhq_ref[rows, :] = (dq_acc[rows, :] + dhq).astype(bf16)
                        dhi_ref[rows, :] = (dv_acc[rows, :] + dv).astype(bf16)
                return carry
            return bwd_step

        unroll = 4 if (n_groups // 2) % 4 == 0 else 2
        lax.fori_loop(0, n_groups // 2, make_bwd_step(True), 0, unroll=unroll)
        lax.fori_loop(n_groups // 2, n_groups, make_bwd_step(False), 0, unroll=unroll)

        for d in range(2):
            dl = dlow_ref[d:d + 1, :] * lowers[d] * (1.0 - lowers[d])
            dlb_ref[d:d + 1, :] = dl
            dlb_ref[2 + d:3 + d, :] = -dl

    col = lambda base: pl.BlockSpec((s, 128), lambda h: (0, base // 128 + h))
    return pl.pallas_call(
        body, name="bwd_gla", grid=(HG_HEADS,),
        in_specs=[col(Z_HQ), col(Z_HFF), col(Z_HFB), col(Z_HI), pl.BlockSpec((4, 128), lambda h: (0, h)), col(0),
                  pl.BlockSpec((2, s, 128), lambda h: (0, 0, h)),
                  pl.BlockSpec((None, 2, n_chunks, 128, 128), lambda h: (h, 0, 0, 0, 0), pipeline_mode=pl.Buffered(1))],
        out_specs=[col(0), col(0), col(0), col(0), pl.BlockSpec((4, 128), lambda h: (0, h))],
        out_shape=[jax.ShapeDtypeStruct((s, 512), bf16)] * 4 + [jax.ShapeDtypeStruct((4, 512), f32)],
        scratch_shapes=[pltpu.VMEM((2, 128, 128), f32), pltpu.VMEM((s, 128), f32), pltpu.VMEM((s, 128), f32),
                        pltpu.VMEM((2, 128), f32)],
        compiler_params=_cp(("parallel",), VMEM_LIMIT),
    )(z, z, z, z, lb4, do, b_fwd, states)


def _bwd_attn(q, k, v, da, a32, tq, after):
    hh, s, _ = q.shape

    n_sub = max(1, tq // ATTN_SUB_ROWS)

    def body(q_ref, k_ref, v_ref, do_ref, o_ref, after_ref, dq_ref, dk_ref, dv_ref, p_all, ds_all, dol_ref, dkt_ref, dvt_ref):
        @pl.when(pl.program_id(1) == 0)
        def _():
            dkt_ref[...] = jnp.zeros_like(dkt_ref)
            dvt_ref[...] = jnp.zeros_like(dvt_ref)

        kb, vb = k_ref[...], v_ref[...]
        for t in range(n_sub):
            rows = slice(t * (tq // n_sub), (t + 1) * (tq // n_sub))
            sc = _dot_nt(q_ref[rows, :], kb)
            p = jnp.exp2((sc - jnp.max(sc, axis=-1, keepdims=True)) * (ATTN_SCALE * LOG2_E))
            inv_l = 1.0 / jnp.sum(p, axis=-1, keepdims=True)
            p_all[rows, :] = p.astype(bf16)
            dob = do_ref[rows, :]
            dof = dob.astype(f32)
            delta = jnp.sum(dof * o_ref[rows, :], axis=-1, keepdims=True)
            ds_all[rows, :] = p_all[rows, :] * ((_dot_nt(dob, vb) - delta) * inv_l).astype(bf16)
            dq_ref[rows, :] = _dot(ds_all[rows, :], kb) * ATTN_SCALE
            dol_ref[rows, :] = (dof * inv_l).astype(bf16)
        dkt_ref[...] += _dot_tn(q_ref[...], ds_all[...])
        dvt_ref[...] += _dot_tn(dol_ref[...], p_all[...])

        @pl.when(pl.program_id(1) == s // tq - 1)
        def _():
            dk_ref[...] = dkt_ref[...].T * ATTN_SCALE
            dv_ref[...] = dvt_ref[...].T

    return pl.pallas_call(
        body, name="bwd_attn", grid=(hh, s // tq),
        in_specs=[pl.BlockSpec((None, tq, QK_PAD), lambda h, i: (h, i, 0)),
                  pl.BlockSpec((None, s, QK_PAD), lambda h, i: (h, 0, 0)),
                  pl.BlockSpec((None, s, V_HEAD), lambda h, i: (h, 0, 0)),
                  pl.BlockSpec((tq, V_HEAD), lambda h, i: (i, h)), pl.BlockSpec((tq, V_HEAD), lambda h, i: (i, h)),
                  pl.BlockSpec(memory_space=pl.ANY)],
        out_specs=[pl.BlockSpec((None, tq, QK_PAD), lambda h, i: (h, i, 0)),
                   pl.BlockSpec((None, s, QK_PAD), lambda h, i: (h, 0, 0)),
                   pl.BlockSpec((None, s, V_HEAD), lambda h, i: (h, 0, 0))],
        out_shape=[jax.ShapeDtypeStruct((hh, s, QK_PAD), f32), jax.ShapeDtypeStruct((hh, s, QK_PAD), f32),
                   jax.ShapeDtypeStruct((hh, s, V_HEAD), f32)],
        scratch_shapes=[pltpu.VMEM((tq, s), bf16), pltpu.VMEM((tq, s), bf16), pltpu.VMEM((tq, V_HEAD), bf16),
                        pltpu.VMEM((QK_PAD, s), f32), pltpu.VMEM((V_HEAD, s), f32)],
        compiler_params=_cp(("parallel", "arbitrary"), VMEM_LIMIT),
    )(q, k, v, da, a32, after)


def _bwd_mla_proj(z, dq, dk, dv, cosb, sina, sinb, g_qa, g_kva, wqb, wkvb, g_qn, g_kn, tm):
    s = z.shape[0]
    hh = MLA_HEADS

    def body(cq_ref, ckv_ref, kr_ref, dq_ref, dk_ref, dv_ref, c_ref, sa_ref, sb_ref, gqa_ref, gkva_ref, wqb_ref, wkvb_ref,
             gqn_ref, gkn_ref, dz_ref, gwqb_ref, gwkvb_ref, dgqa_ref, dgkva_ref, dgqn_ref, dgkn_ref, dq0_ref, dkv0_ref):
        @pl.when(pl.program_id(0) == 0)
        def _():
            for r in (gwqb_ref, gwkvb_ref, dgqa_ref, dgkva_ref, dgqn_ref, dgkn_ref):
                r[...] = jnp.zeros_like(r)

        cq, ckv, kr = cq_ref[...], ckv_ref[...], kr_ref[...]
        gqa, gkva, gqn, gkn = gqa_ref[...], gkva_ref[...], gqn_ref[...], gkn_ref[...]
        cqn_b, rq, ckvn_b, rkv, q0, kv0 = _mla_qk_fwd(cq, ckv, gqa, gkva, wqb_ref[...], wkvb_ref[...])
        c, sa, sb = c_ref[...], -sa_ref[...], -sb_ref[...]
        kr_sq = jnp.sum(kr * kr, axis=-1, keepdims=True)
        dkr = jnp.zeros_like(kr)
        dgqn = jnp.zeros((1, QK_PAD), f32)
        dgkn = jnp.zeros((1, QK_PAD), f32)
        for h in range(hh):
            qh = q0[:, QK_PAD * h:QK_PAD * (h + 1)]
            rh = lax.rsqrt(jnp.sum(qh * qh, axis=-1, keepdims=True) * (1.0 / QK_HEAD) + EPS)
            dqh = dq_ref[h]
            dqn = jnp.concatenate([dqh[:, 0:128], _rope(dqh[:, 128:256], c, sa, sb)], axis=-1)
            dq0h, dgx = _rms_bwd(dqn, qh, rh, gqn, QK_HEAD)
            dq0_ref[:, QK_PAD * h:QK_PAD * (h + 1)] = dq0h.astype(bf16)
            dgqn = dgqn + jnp.sum(dgx, axis=0, keepdims=True)

            kn_ = kv0[:, 256 * h:256 * h + 128]
            k0 = jnp.concatenate([kn_, kr], axis=-1)
            rk = lax.rsqrt((jnp.sum(kn_ * kn_, axis=-1, keepdims=True) + kr_sq) * (1.0 / QK_HEAD) + EPS)
            dkh = dk_ref[h]
            dkn = jnp.concatenate([dkh[:, 0:128], _rope(dkh[:, 128:256], c, sa, sb)], axis=-1)
            dk0, dgx = _rms_bwd(dkn, k0, rk, gkn, QK_HEAD)
            dgkn = dgkn + jnp.sum(dgx, axis=0, keepdims=True)
            dkv0_ref[:, 256 * h:256 * h + 128] = dk0[:, 0:128].astype(bf16)
            dkv0_ref[:, 256 * h + 128:256 * h + 256] = dv_ref[h].astype(bf16)
            dkr = dkr + dk0[:, 128:256]
        dgqn_ref[...] += dgqn
        dgkn_ref[...] += dgkn
        gwqb_ref[...] += _dot_tn(cqn_b, dq0_ref[...])
        gwkvb_ref[...] += _dot_tn(ckvn_b, dkv0_ref[...])
        dcq, dgx = _rms_bwd(_dot_nt(dq0_ref[...], wqb_ref[...]), cq, rq, gqa, Q_LORA)
        dgqa_ref[...] += jnp.sum(dgx, axis=0, keepdims=True)
        dckv, dgx = _rms_bwd(_dot_nt(dkv0_ref[...], wkvb_ref[...]), ckv, rkv, gkva, KV_LORA)
        dgkva_ref[...] += jnp.sum(dgx, axis=0, keepdims=True)
        dz_ref[:, 0:256] = dcq.astype(bf16)
        dz_ref[:, 256:512] = dckv.astype(bf16)
        dz_ref[:, 512:640] = dkr.astype(bf16)

    row128 = pl.BlockSpec((tm, 128), lambda i: (i, 0))
    hd = lambda w: pl.BlockSpec((hh, tm, w), lambda i: (0, i, 0))
    return pl.pallas_call(
        body, name="bwd_mla_proj", grid=(s // tm,),
        in_specs=[pl.BlockSpec((tm, 256), lambda i: (i, Z_CQ // 256)), pl.BlockSpec((tm, 256), lambda i: (i, Z_CKV // 256)),
                  pl.BlockSpec((tm, 128), lambda i: (i, Z_KR // 128)), hd(QK_PAD), hd(QK_PAD), hd(V_HEAD),
                  row128, row128, row128,
                  _const_spec((1, 256)), _const_spec((1, 256)), _const_spec((256, 1024)), _const_spec((256, 1024)),
                  _const_spec((1, 256)), _const_spec((1, 256))],
        out_specs=[pl.BlockSpec((tm, 640), lambda i: (i, 0)), _acc_spec((256, 1024)), _acc_spec((256, 1024)),
                   _acc_spec((1, 256)), _acc_spec((1, 256)), _acc_spec((1, 256)), _acc_spec((1, 256))],
        out_shape=[jax.ShapeDtypeStruct((s, 640), bf16), jax.ShapeDtypeStruct((256, 1024), f32), jax.ShapeDtypeStruct((256, 1024), f32)]
        + [jax.ShapeDtypeStruct((1, 256), f32)] * 4,
        scratch_shapes=[pltpu.VMEM((tm, 1024), bf16), pltpu.VMEM((tm, 1024), bf16)],
        compiler_params=_cp(("arbitrary",), VMEM_LIMIT),
    )(z, z, z, dq, dk, dv, cosb, sina, sinb, g_qa, g_kva, wqb, wkvb, g_qn, g_kn)


def _bwd_in(segments, wz, x, g_mix, d2, tm):
    s, d = x.shape
    n_seg = len(segments)

    def body(*refs):
        dz_refs, w_refs = refs[:n_seg], refs[n_seg:2 * n_seg]
        x_ref, g_ref, d2_ref, gx_ref, dg_ref = refs[2 * n_seg:]

        @pl.when(pl.program_id(0) == 0)
        def _():
            dg_ref[...] = jnp.zeros_like(dg_ref)

        dh = _dot(dz_refs[0][...], w_refs[0][...])
        for a_ref, w_ref in zip(dz_refs[1:], w_refs[1:]):
            dh = dh + _dot(a_ref[...], w_ref[...])
        x, g = x_ref[...], g_ref[...]
        r = lax.rsqrt(jnp.sum(x * x, axis=-1, keepdims=True) * (1.0 / d) + EPS)
        dx, dgx = _rms_bwd(dh, x, r, g, d)
        gx_ref[...] = d2_ref[...] + dx
        dg_ref[...] += jnp.sum(dgx, axis=0, keepdims=True)

    rowd = pl.BlockSpec((tm, d), lambda i: (i, 0))
    dz_specs = [pl.BlockSpec((tm, w), functools.partial(lambda i, j: (i, j), j=ja)) for _, w, ja, _ in segments]
    w_specs = [pl.BlockSpec((w, d), functools.partial(lambda i, j: (j, 0), j=jw), pipeline_mode=pl.Buffered(1))
               for _, w, _, jw in segments]
    return pl.pallas_call(
        body, name="bwd_in", grid=(s // tm,),
        in_specs=dz_specs + w_specs + [rowd, _const_spec((1, d)), rowd],
        out_specs=[rowd, _acc_spec((1, d))],
        out_shape=[jax.ShapeDtypeStruct((s, d), f32), jax.ShapeDtypeStruct((1, d), f32)],
        compiler_params=_cp(("arbitrary",), VMEM_LIMIT),
    )(*[a for a, _, _, _ in segments], *([wz] * n_seg), x, g_mix, d2)


def _mm_tn_many(a, bs, name, tm, transposed=False):
    kk, m = a.shape
    n_b = len(bs)
    tk = min(1024, kk)
    n_k = kk // tk

    def body(a_ref, *refs):
        b_refs, o_refs, acc_refs = refs[:n_b], refs[n_b:2 * n_b], refs[2 * n_b:]

        @pl.when(pl.program_id(1) == 0)
        def _():
            for acc in acc_refs:
                acc[...] = jnp.zeros_like(acc)
        a_blk = a_ref[...].astype(bf16)
        for b_ref, acc in zip(b_refs, acc_refs):
            acc[...] += _dot_tn(a_blk, b_ref[...].astype(bf16))

        @pl.when(pl.program_id(1) == n_k - 1)
        def _():
            for o_ref, acc in zip(o_refs, acc_refs):
                o_ref[...] = (acc[...].T if transposed else acc[...]).astype(bf16)

    if transposed:
        out_specs = [pl.BlockSpec((b.shape[1], tm), lambda i, k: (0, i)) for b in bs]
        out_shape = [jax.ShapeDtypeStruct((b.shape[1], m), bf16) for b in bs]
    else:
        out_specs = [pl.BlockSpec((tm, b.shape[1]), lambda i, k: (i, 0)) for b in bs]
        out_shape = [jax.ShapeDtypeStruct((m, b.shape[1]), bf16) for b in bs]
    return pl.pallas_call(
        body, name=name, grid=(m // tm, n_k),
        in_specs=[pl.BlockSpec((tk, tm), lambda i, k: (k, i))] + [pl.BlockSpec((tk, b.shape[1]), lambda i, k: (k, 0)) for b in bs],
        out_specs=out_specs,
        out_shape=out_shape,
        scratch_shapes=[pltpu.VMEM((tm, b.shape[1]), f32) for b in bs],
        compiler_params=_cp(("parallel", "arbitrary"), VMEM_LIMIT),
    )(a, *bs)


def _rope_tables(positions):
    inv_freq = ROPE_THETA ** (-jnp.arange(0, QK_ROPE, 2, dtype=f32) / QK_ROPE)
    ang = positions.astype(f32)[:, None] * inv_freq
    cos, sin = jnp.cos(ang), jnp.sin(ang)
    zero = jnp.zeros_like(cos)
    return (jnp.concatenate([cos, cos, zero, zero], axis=1), jnp.concatenate([zero, sin, zero, zero], axis=1),
            jnp.concatenate([-sin, zero, zero, zero], axis=1))


def _pad256(g):
    return jnp.pad(g.reshape(1, QK_HEAD), ((0, 0), (0, QK_PAD - QK_HEAD)))


RELAYOUT_BLOCKS = 8
FIRST = ("w_in", "w_qb", "w_kvb", "lb_param")
SECOND = ("w_o", "w_gate", "w_up", "w_down", "w_ple_gate", "w_ple_proj")
ROW_SHARDED = ("w_o", "w_down", "w_ple_gate")


def _col_moves(j):
    width = BIG["w_in"][0]
    lo = width * j
    w_in = [(max(lo, a) - lo, min(lo + width, b) - lo, d + max(lo, a) - a)
            for a, b, d in Z_SEGMENTS if max(lo, a) < min(lo + width, b)]
    head, half = divmod(j, 2)
    whole = lambda n: [(0, BIG[n][1], BIG[n][1] * j)]
    return {"w_in": w_in, "w_qb": [(0, 96, QK_PAD * head + 96 * half)], "w_kvb": whole("w_kvb"),
            "w_ple_proj": whole("w_ple_proj"), "lb_param": whole("lb_param")}


def _kernel_shape(name):
    rows, cols = BIG[name]
    if name == "w_in":
        return (Z_W, cols)
    return (rows, MLA_HEADS * QK_PAD if name == "w_qb" else N_DEV * cols)


def _relayout_specs(names, by_dev):
    specs = []
    for n in names:
        rows, cols = BIG[n]
        if n == "lb_param":
            specs.append(_acc_spec((N_DEV, rows, cols) if by_dev else _kernel_shape(n)))
        elif n == "w_in":
            cb = cols // RELAYOUT_BLOCKS
            specs.append(pl.BlockSpec((N_DEV, rows, cb), lambda i: (0, 0, i)) if by_dev else pl.BlockSpec((Z_W, cb), lambda i: (0, i)))
        elif by_dev:
            specs.append(pl.BlockSpec((N_DEV, rows // RELAYOUT_BLOCKS, cols), lambda i: (0, i, 0)))
        else:
            specs.append(pl.BlockSpec((rows // RELAYOUT_BLOCKS, _kernel_shape(n)[1]), lambda i: (i, 0)))
    return specs


def _weights_in(gathered, names, name):
    n = len(names)

    def body(*refs):
        ins, outs = dict(zip(names, refs[:n])), dict(zip(names, refs[n:]))
        if "w_in" in outs:
            outs["w_in"][Z_KR + QK_ROPE:Z_W, :] = jnp.zeros((Z_W - Z_KR - QK_ROPE, outs["w_in"].shape[1]), bf16)
        if "w_qb" in outs:
            for h in range(MLA_HEADS):
                outs["w_qb"][:, QK_PAD * h + QK_HEAD:QK_PAD * (h + 1)] = jnp.zeros((outs["w_qb"].shape[0], QK_PAD - QK_HEAD), bf16)
        for j in range(N_DEV):
            for wn, moves in _col_moves(j).items():
                if wn in outs:
                    for s0, s1, d0 in moves:
                        if wn == "w_in":
                            outs[wn][d0:d0 + s1 - s0, :] = ins[wn][j, s0:s1, :]
                        else:
                            outs[wn][:, d0:d0 + s1 - s0] = ins[wn][j, :, s0:s1]

    outs = pl.pallas_call(
        body, name=name, grid=(RELAYOUT_BLOCKS,), in_specs=_relayout_specs(names, True), out_specs=_relayout_specs(names, False),
        out_shape=[jax.ShapeDtypeStruct(_kernel_shape(wn), gathered[wn].dtype) for wn in names],
        compiler_params=_cp(("arbitrary",), VMEM_LIMIT),
    )(*[gathered[wn] for wn in names])
    return dict(zip(names, outs))


def _grads_out(sources, names, name):
    pieces = [(wn, start, arr) for wn in names for start, arr in sources[wn]]
    n_in = len(pieces)

    def body(*refs):
        outs = dict(zip(names, refs[n_in:]))

        def cols(wn, c0, c1):
            for (pn, start, arr), ref in zip(pieces, refs[:n_in]):
                if pn == wn and start <= c0 and c1 <= start + arr.shape[0 if wn == "w_in" else 1]:
                    return ref[c0 - start:c1 - start, :] if wn == "w_in" else ref[:, c0 - start:c1 - start]

        for j in range(N_DEV):
            for wn, moves in _col_moves(j).items():
                if wn in outs:
                    for s0, s1, d0 in moves:
                        if wn == "w_in":
                            outs[wn][j, s0:s1, :] = cols(wn, d0, d0 + s1 - s0).astype(bf16)
                        else:
                            outs[wn][j, :, s0:s1] = cols(wn, d0, d0 + s1 - s0).astype(bf16)

    def in_spec(wn, arr):
        if wn == "lb_param":
            return _acc_spec(arr.shape)
        if wn == "w_in":
            return pl.BlockSpec((arr.shape[0], arr.shape[1] // RELAYOUT_BLOCKS), lambda i: (0, i))
        return pl.BlockSpec((arr.shape[0] // RELAYOUT_BLOCKS, arr.shape[1]), lambda i: (i, 0))

    in_specs = [in_spec(wn, arr) for wn, _, arr in pieces]
    outs = pl.pallas_call(
        body, name=name, grid=(RELAYOUT_BLOCKS,), in_specs=in_specs, out_specs=_relayout_specs(names, True),
        out_shape=[jax.ShapeDtypeStruct((N_DEV, *BIG[wn]), bf16) for wn in names],
        compiler_params=_cp(("arbitrary",), VMEM_LIMIT),
    )(*[arr for _, _, arr in pieces])
    return dict(zip(names, outs))


def kernel(x, p, positions, g_mix, w_in, g_qa, g_kva, w_qb, w_kvb, g_qn, g_kn, lb_param, g_hgo, w_o, g_ffn, w_gate, w_up, w_down, g_ple, w_ple_gate, w_ple_proj, loss_target, m_g_mix, m_w_in, m_g_qa, m_g_kva, m_w_qb, m_w_kvb, m_g_qn, m_g_kn, m_lb_param, m_g_hgo, m_w_o, m_g_ffn, m_w_gate, m_w_up, m_w_down, m_g_ple, m_w_ple_gate, m_w_ple_proj, v_g_mix, v_w_in, v_g_qa, v_g_kva, v_w_qb, v_w_kvb, v_g_qn, v_g_kn, v_lb_param, v_g_hgo, v_w_o, v_g_ffn, v_w_gate, v_w_up, v_w_down, v_g_ple, v_w_ple_gate, v_w_ple_proj):
    w_all = dict(g_mix=g_mix, g_qa=g_qa, g_kva=g_kva, g_qn=g_qn, g_kn=g_kn, g_hgo=g_hgo, g_ffn=g_ffn, g_ple=g_ple,
                 w_in=w_in, w_qb=w_qb, w_kvb=w_kvb, w_o=w_o, w_gate=w_gate, w_up=w_up, w_down=w_down,
                 w_ple_gate=w_ple_gate, w_ple_proj=w_ple_proj, lb_param=lb_param)
    m_all = dict(g_mix=m_g_mix, g_qa=m_g_qa, g_kva=m_g_kva, g_qn=m_g_qn, g_kn=m_g_kn, g_hgo=m_g_hgo, g_ffn=m_g_ffn,
                 g_ple=m_g_ple, w_in=m_w_in, w_qb=m_w_qb, w_kvb=m_w_kvb, w_o=m_w_o, w_gate=m_w_gate, w_up=m_w_up,
                 w_down=m_w_down, w_ple_gate=m_w_ple_gate, w_ple_proj=m_w_ple_proj, lb_param=m_lb_param)
    v_all = dict(g_mix=v_g_mix, g_qa=v_g_qa, g_kva=v_g_kva, g_qn=v_g_qn, g_kn=v_g_kn, g_hgo=v_g_hgo, g_ffn=v_g_ffn,
                 g_ple=v_g_ple, w_in=v_w_in, w_qb=v_w_qb, w_kvb=v_w_kvb, w_o=v_w_o, w_gate=v_w_gate, w_up=v_w_up,
                 w_down=v_w_down, w_ple_gate=v_w_ple_gate, w_ple_proj=v_w_ple_proj, lb_param=v_lb_param)
    me_idx = jnp.stack([_me()]).astype(jnp.int32)
    x, p, positions, target = x[0], p[0, 0], positions[0], loss_target[0]
    s = x.shape[0]
    tm, tm_ffn, tq_f, tq_b = min(512, s), min(1024, s), min(2048, s), min(1024, s)
    g_mix, g_qa, g_kva, g_qn, g_kn, g_hgo, g_ffn, g_ple = (w_all[n].reshape(1, -1) for n in SMALL)
    g_qn_p, g_kn_p = _pad256(g_qn), _pad256(g_kn)
    cosb, sina, sinb = _rope_tables(positions)
    as_shard = lambda n, a: a[0].T if n in TRANSPOSED else a.reshape(BIG[n])
    shard = lambda n: as_shard(n, w_all[n])

    first = _all_gather([shard(n) for n in FIRST], [f32 if n == "lb_param" else bf16 for n in FIRST], "ag_first")
    lands = _cast_to_slot([shard(n) for n in SECOND], me_idx, first[0])
    ag2, token = _exchange_start([], lands, "ag_second_start")
    wk = _weights_in(dict(zip(FIRST, first)), FIRST, "weights_in_first")
    wz, wqb, wkvb, lb4 = (wk[n] for n in FIRST)

    h1, z = _fwd_in(x, g_mix, wz, tm)
    q, k, v = _fwd_mla_proj(z, cosb, sina, sinb, g_qa + token[0, 0], g_kva, wqb, wkvb, g_qn_p, g_kn_p, tm)
    o, gla_b, gla_states = _fwd_gla(z, lb4)
    a, a32 = _fwd_attn(q, k, v, tq_f, o)

    second = dict(zip(SECOND, _exchange_wait(ag2, [a, o], "ag_second_wait")[1]))
    w_pp = second["w_ple_proj"]
    w_o, w_down, w_pg, w_gate, w_up = (second[n].reshape(N_DEV * BIG[n][0], BIG[n][1]) for n in ROW_SHARDED + ("w_gate", "w_up"))

    x2, cat = _fwd_mix(a, o, z, g_hgo, x, w_o, tm)
    x3, gp, up = _fwd_ffn(x2, g_ffn, w_gate, w_up, w_down, tm)
    d3, gw_pg, gw_pp, dg_ple, loss_tile = _ple_loss_fwd_bwd(x3, g_ple, w_pg, p, w_pp, target, tm)
    dgp, dup, gw_down = _bwd_ffn_hidden(d3, gp, up, w_down, tm, D_FF // 2)
    d2, h2, dg_ffn = _bwd_ffn_in(d3, x2, dgp, dup, g_ffn, w_gate, w_up, tm)
    da, do, dz_hg, dg_hgo, gw_o = _bwd_mix(d2, w_o, o, z, g_hgo, cat, tm)

    gw_gate, gw_up = _mm_tn_many(h2, [dgp, dup], "dw_gate_up", 512, transposed=True)
    blocks = {"w_ple_proj": gw_pp}
    row_grads = {"w_o": gw_o, "w_down": gw_down, "w_ple_gate": gw_pg, "w_gate": gw_gate, "w_up": gw_up}
    blocks.update({n: g.reshape(N_DEV, *BIG[n]) for n, g in row_grads.items()})
    empty = lambda names: [lax.empty((N_PEERS, *BIG[n]), bf16) for n in names]
    rs2, token = _exchange_start([blocks[n] for n in SECOND], empty(SECOND), "rs_second_start")

    dq, dk, dv = _bwd_attn(q, k, v, da, a32, tq_b, token)
    dz_hq, dz_hff, dz_hfb, dz_hi, dlb4 = _bwd_gla(z, lb4 + token[0, 0], do, gla_b, gla_states)
    dz_mla, gw_qb, gw_kvb, dg_qa, dg_kva, dg_qn, dg_kn = _bwd_mla_proj(
        z, dq, dk, dv, cosb, sina, sinb, g_qa, g_kva, wqb, wkvb, g_qn_p, g_kn_p, tm)

    gz = list(zip((Z_HQ, Z_HFF, Z_HFB, Z_HI, Z_HG, Z_CQ),
                  _mm_tn_many(h1, [dz_hq, dz_hff, dz_hfb, dz_hi, dz_hg, dz_mla], "dw_in", 1024, transposed=True)))
    blocks1 = _grads_out({"w_in": gz, "w_qb": [(0, gw_qb)], "w_kvb": [(0, gw_kvb)],
                          "lb_param": [(0, dlb4)]}, FIRST, "grads_out_first")
    rs1, token = _exchange_start([blocks1[n] for n in FIRST], empty(FIRST), "rs_first_start")

    result = {}

    def adam(names, lands, src, n_blocks, after=()):
        flipped = TRANSPOSED
        given = lambda arrs: [arrs[n][0].T if n in flipped else arrs[n] for n in names]
        outs = _adam_shards(me_idx, [src[n] for n in names], lands, given(w_all), given(m_all), given(v_all), n_blocks,
                            "adamw_" + names[0], after)
        for n, o in zip(names, outs):
            result[n] = [t.T[None] for t in o] if n in flipped else o
        return outs[0][0]

    blocks2, lands2 = (dict(zip(SECOND, arrs)) for arrs in _exchange_wait(rs2, [token], "rs_second_wait"))
    by2 = ("w_down", "w_gate", "w_up")
    by8 = tuple(n for n in SECOND if n not in by2)
    done = [adam(by8, [lands2[n] for n in by8], blocks2, 8), adam(by2, [lands2[n] for n in by2], blocks2, 2)]

    segments = [(dz_hq, 512, 0, Z_HQ // 512), (dz_hff, 512, 0, Z_HFF // 512), (dz_hfb, 512, 0, Z_HFB // 512),
                (dz_hi, 512, 0, Z_HI // 512), (dz_hg, 512, 0, Z_HG // 512), (dz_mla, 640, 0, Z_CQ // 640)]
    grad_x, dg_mix = _bwd_in(segments, wz, x, g_mix + token[0, 0], d2, tm)
    dgains = (dg_mix, dg_qa, dg_kva, dg_qn, dg_kn, dg_hgo, dg_ffn, dg_ple)

    vec = jnp.concatenate(list(dgains) + [loss_tile[0:1]], axis=1)
    parts = _all_gather([vec], [f32], "ag_gains")[0]
    outs, loss_row = _adam_gains(parts, [w_all[n] for n in SMALL], [m_all[n] for n in SMALL], [v_all[n] for n in SMALL])
    result.update(zip(SMALL, outs))

    blocks1, lands1 = _exchange_wait(rs1, [grad_x, loss_row, *done], "rs_first_wait")
    adam(FIRST, lands1, dict(zip(FIRST, blocks1)), 8)

    order = ("g_mix", "w_in", "g_qa", "g_kva", "w_qb", "w_kvb", "g_qn", "g_kn", "lb_param", "g_hgo", "w_o", "g_ffn",
             "w_gate", "w_up", "w_down", "g_ple", "w_ple_gate", "w_ple_proj")
    return (loss_row[0, 0], grad_x[None], *[result[n][k] for k in range(4) for n in order])
```

```python
import functools
import math

import jax
import jax.numpy as jnp
from jax import lax
from jax.experimental import pallas as pl
from jax.experimental.pallas import tpu as pltpu

f32 = jnp.float32
bf16 = jnp.bfloat16

N_DEV = 8
MLA_HEADS = 4
QK_NOPE = 128
QK_ROPE = 64
QK_HEAD = QK_NOPE + QK_ROPE
QK_PAD = 256
V_HEAD = 128
Q_LORA = 256
KV_LORA = 256
HG_HEADS = 4
CHUNK = 64
D_FF = 2816
PLE_DIM = 256
ROPE_THETA = 10000.0
EPS = 1e-6
ATTN_SCALE = QK_HEAD ** -0.5
LOG2_E = math.log2(math.e)
ATTN_SUB_ROWS = 256
Z_HQ, Z_HFF, Z_HFB, Z_HI, Z_HG, Z_CQ, Z_CKV, Z_KR, Z_W = 0, 512, 1024, 1536, 2048, 2560, 2816, 3072, 3200

ADAM_LR, ADAM_B1, ADAM_B2, ADAM_EPS, ADAM_WD, ADAM_STEP = 0.001, 0.9, 0.999, 1e-08, 0.01, 10

LANES = 128
BIG = {"w_in": (392, 1024), "w_qb": (256, 96), "w_kvb": (256, 128), "w_o": (128, 1024), "w_gate": (352, 1024),
       "w_up": (352, 1024), "w_down": (352, 1024), "w_ple_gate": (128, 1024), "w_ple_proj": (256, 128),
       "lb_param": (4, 64)}
TRANSPOSED = ("w_gate", "w_up", "w_in")
SMALL = {"g_mix": (0, 1024), "g_qa": (1024, 256), "g_kva": (1280, 256), "g_qn": (1536, 192), "g_kn": (1792, 192),
         "g_hgo": (2048, 512), "g_ffn": (2560, 1024), "g_ple": (3584, 1024)}
LOSS_OFF = 4608
GAIN_VEC = LOSS_OFF + LANES
Z_SEGMENTS = ((0, 256, Z_CQ), (256, 512, Z_CKV), (512, 576, Z_KR), (576, 1088, Z_HQ), (1088, 1600, Z_HFF),
              (1600, 2112, Z_HFB), (2112, 2624, Z_HI), (2624, 3136, Z_HG))

VMEM_LIMIT = 56 * 1024 * 1024
MESH = pl.DeviceIdType.MESH


def _cp(sem=None, vmem=None):
    return pltpu.CompilerParams(dimension_semantics=sem, vmem_limit_bytes=vmem)


def _const_spec(shape):
    nd = len(shape)
    return pl.BlockSpec(shape, lambda *_: (0,) * nd, pipeline_mode=pl.Buffered(1))


def _acc_spec(shape):
    nd = len(shape)
    return pl.BlockSpec(shape, lambda *_: (0,) * nd)


def _sigmoid(x):
    return jax.nn.sigmoid(x)


def _dot(a, b):
    return jnp.dot(a, b, preferred_element_type=f32)


def _dot_nt(a, b):
    return lax.dot_general(a, b, (((1,), (1,)), ((), ())), preferred_element_type=f32)


def _dot_tn(a, b):
    return lax.dot_general(a, b, (((0,), (0,)), ((), ())), preferred_element_type=f32)


def _rms_fwd(x, g, width):
    r = lax.rsqrt(jnp.sum(x * x, axis=-1, keepdims=True) * (1.0 / width) + EPS)
    return x * r * g, r


def _rms_bwd(dy, x, r, g, width):
    u = dy * g
    dx = r * u - x * (r * r * r) * (jnp.sum(u * x, axis=-1, keepdims=True) * (1.0 / width))
    return dx, dy * x * r


class _Both:
    def __init__(self, *copies):
        self.copies = copies

    def start(self):
        for cp in self.copies:
            cp.start()

    def wait(self):
        for cp in self.copies:
            cp.wait()


def _rope(b, c, sa, sb):
    return b * c + pltpu.roll(b, 32, 1) * sa + pltpu.roll(b, 96, 1) * sb


def _all_gather(shards, dtypes, name):
    n = len(shards)

    def body(*refs):
        in_refs, out_refs, stage = refs[:n], refs[n:2 * n], refs[2 * n:3 * n]
        send_sems, recv_sems, local_sems = refs[3 * n:]
        for w in range(n):
            stage[w][...] = in_refs[w][...].astype(stage[w].dtype)
        x, y, c = lax.axis_index("x"), lax.axis_index("y"), lax.axis_index("c")
        me, sibling = (x, y, c), (x, y, 1 - c)
        chips = [(1 - x, y), (x, 1 - y), (1 - x, 1 - y)]

        def slot(w, px, py, pc):
            return out_refs[w].at[4 * px + 2 * py + pc]

        def copy(w, k, block, to, src=None):
            return pltpu.make_async_remote_copy(
                src_ref=slot(w, *block) if src is None else src, dst_ref=slot(w, *block),
                send_sem=send_sems.at[w, k], recv_sem=recv_sems.at[w, k], device_id=to, device_id_type=MESH)

        first = []
        for j, chip in enumerate(chips):
            first += [copy(w, 1 + j, me, (*chip, c), src=stage[w]) for w in range(n)]
        first += [copy(w, 0, me, sibling, src=stage[w]) for w in range(n)]
        mine = [pltpu.make_async_copy(stage[w], slot(w, *me), local_sems.at[w]) for w in range(n)]
        for cp in first + mine:
            cp.start()
        passed = []
        for j, chip in enumerate(chips):
            for w in range(n):
                copy(w, 1 + j, (*chip, c), me).wait_recv()
                passed.append(copy(w, 4 + j, (*chip, c), sibling))
                passed[-1].start()
        for w in range(n):
            copy(w, 0, sibling, me).wait_recv()
        for j, chip in enumerate(chips):
            for w in range(n):
                copy(w, 4 + j, (*chip, 1 - c), me).wait_recv()
        for cp in first + passed:
            cp.wait_send()
        for cp in mine:
            cp.wait()

    return pl.pallas_call(
        body, name=name,
        out_shape=[jax.ShapeDtypeStruct((N_DEV, *s.shape), dt) for s, dt in zip(shards, dtypes)],
        in_specs=[pl.BlockSpec(memory_space=pltpu.VMEM)] * n,
        out_specs=[pl.BlockSpec(memory_space=pl.ANY)] * n,
        scratch_shapes=[pltpu.VMEM(s.shape, dt) for s, dt in zip(shards, dtypes)]
        + [pltpu.SemaphoreType.DMA((n, 7)), pltpu.SemaphoreType.DMA((n, 7)), pltpu.SemaphoreType.DMA((n,))],
        compiler_params=_cp(None, VMEM_LIMIT),
    )(*shards)


N_PEERS = N_DEV - 1
HBM_SPEC = pl.BlockSpec(memory_space=pltpu.HBM)
SEM_SPEC = pl.BlockSpec(memory_space=pltpu.SEMAPHORE)
DATAFLOW = pltpu.SideEffectType.DATAFLOW_SIDE_EFFECTING


def _me():
    return 4 * lax.axis_index("x") + 2 * lax.axis_index("y") + lax.axis_index("c")


def _peer(k):
    x, y, c = lax.axis_index("x"), lax.axis_index("y"), lax.axis_index("c")
    px = 1 - x if k & 4 else x
    py = 1 - y if k & 2 else y
    pc = 1 - c if k & 1 else c
    return (px, py, pc), 4 * px + 2 * py + pc


def _exchange_copies(src_refs, land_refs, send_sems, recv_sems, gather):
    cps = []
    me = _me()
    for k in range(1, N_DEV):
        peer, peer_idx = _peer(k)
        for w, land in enumerate(land_refs):
            src = land.at[me] if gather else src_refs[w].at[peer_idx]
            dst = land.at[me] if gather else land.at[k - 1]
            cps.append(pltpu.make_async_remote_copy(
                src_ref=src, dst_ref=dst, send_sem=send_sems.at[N_PEERS * w + k - 1], recv_sem=recv_sems.at[N_PEERS * w + k - 1],
                device_id=peer, device_id_type=MESH))
    return cps


def _exchange_start(srcs, lands, name):
    n_src, n = len(srcs), len(lands)

    def body(*refs):
        src_refs, land_refs = refs[:n_src], refs[n_src:n_src + n]
        send_sems, recv_sems = refs[n_src + n], refs[n_src + n + 1]
        token = refs[-1]
        for cp in _exchange_copies(src_refs, land_refs, send_sems, recv_sems, gather=not n_src):
            cp.start()
        token[...] = jnp.zeros_like(token)

    arrays = [pltpu.with_memory_space_constraint(a, pltpu.HBM) for a in (*srcs, *lands)]
    outs = pl.pallas_call(
        body, name=name,
        out_shape=(pltpu.SemaphoreType.DMA((n * N_PEERS,)), pltpu.SemaphoreType.DMA((n * N_PEERS,)),
                   *[pltpu.HBM(a.shape, a.dtype) for a in arrays], jax.ShapeDtypeStruct((8, LANES), f32)),
        in_specs=[HBM_SPEC] * len(arrays),
        out_specs=(SEM_SPEC, SEM_SPEC, *[HBM_SPEC] * len(arrays), pl.BlockSpec(memory_space=pltpu.VMEM)),
        input_output_aliases={i: 2 + i for i in range(len(arrays))},
        compiler_params=pltpu.CompilerParams(has_side_effects=DATAFLOW),
    )(*arrays)
    return (outs[0], outs[1], outs[2:2 + n_src], outs[2 + n_src:2 + n_src + n]), outs[-1]


def _exchange_wait(state, after, name):
    send_sems, recv_sems, srcs, lands = state
    n_src, n = len(srcs), len(lands)

    def body(*refs):
        src_refs, land_refs = refs[:n_src], refs[n_src:n_src + n]
        send_ref, recv_ref = refs[n_src + n], refs[n_src + n + 1]
        for cp in _exchange_copies(src_refs, land_refs, send_ref, recv_ref, gather=not n_src):
            cp.wait_send()
            cp.wait_recv()

    arrays = (*srcs, *lands)
    outs = pl.pallas_call(
        body, name=name,
        out_shape=tuple(pltpu.HBM(a.shape, a.dtype) for a in arrays),
        in_specs=[HBM_SPEC] * len(arrays) + [SEM_SPEC, SEM_SPEC] + [pl.BlockSpec(memory_space=pl.ANY)] * len(after),
        out_specs=tuple([HBM_SPEC] * len(arrays)),
        input_output_aliases={i: i for i in range(len(arrays))},
        compiler_params=pltpu.CompilerParams(has_side_effects=DATAFLOW),
    )(*arrays, send_sems, recv_sems, *after)
    return outs[:n_src], outs[n_src:]


def _cast_to_slot(shards, me_idx, after):
    n = len(shards)

    def body(i_ref, *refs):
        for w in range(n):
            refs[n + 1 + w][...] = refs[w][...].astype(bf16)

    return pl.pallas_call(
        body, name="cast_to_slot",
        grid_spec=pltpu.PrefetchScalarGridSpec(
            num_scalar_prefetch=1, grid=(1,),
            in_specs=[pl.BlockSpec(s.shape, lambda i, m: (0, 0)) for s in shards] + [pl.BlockSpec(memory_space=pl.ANY)],
            out_specs=[pl.BlockSpec((None, *s.shape), lambda i, m: (m[0], 0, 0)) for s in shards]),
        out_shape=[jax.ShapeDtypeStruct((N_DEV, *s.shape), bf16) for s in shards],
        compiler_params=_cp(("arbitrary",), VMEM_LIMIT),
    )(me_idx, *shards, after)


def _row_block(rows, n_blocks):
    return (rows // n_blocks, True) if rows % (16 * n_blocks) == 0 else (rows, False)


def _adam_math(w, g, m, v):
    m = ADAM_B1 * m + (1.0 - ADAM_B1) * g
    v = ADAM_B2 * v + (1.0 - ADAM_B2) * (g * g)
    m_hat = m / (1.0 - ADAM_B1 ** ADAM_STEP)
    v_hat = v / (1.0 - ADAM_B2 ** ADAM_STEP)
    delta = -ADAM_LR * (m_hat / (jnp.sqrt(v_hat) + ADAM_EPS) + ADAM_WD * w)
    return delta, m, v


def _adam_shards(me_idx, blocks, lands, ws, ms, vs, n_blocks, name, after=()):
    n = len(blocks)

    def body(i_ref, *refs):
        ins, outs = refs[:5 * n], refs[5 * n + len(after):]
        for w in range(n):
            g_ref, b_ref, w_ref, m_ref, v_ref = (ins[t * n + w] for t in range(5))
            g = g_ref[...].astype(f32)
            for k in range(N_PEERS):
                g = g + b_ref[k].astype(f32)
            if len(w_ref.shape) == 2:
                pieces = [(slice(None), g)]
            else:
                pieces = [(a, g[2 * a:2 * a + 2]) for a in range(2)]
            for at, gp in pieces:
                vals = (gp,) + _adam_math(w_ref[at], gp, m_ref[at], v_ref[at])
                for t, val in enumerate(vals):
                    outs[4 * w + t][at] = val

    specs = [[] for _ in range(5)]
    out_specs, out_shape = [], []
    for g, wt in zip(blocks, ws):
        rows, cols = g.shape[1:]
        rb, cut = _row_block(rows, n_blocks)
        if not cut and wt.ndim == 2 and cols % (LANES * n_blocks) == 0:
            cb = cols // n_blocks
            specs[0].append(pl.BlockSpec((None, rows, cb), lambda i, s: (s[0], 0, i)))
            specs[1].append(pl.BlockSpec((N_PEERS, rows, cb), lambda i, s: (0, 0, i)))
            shard = pl.BlockSpec((rows, cb), lambda i, s: (0, i))
            for t in (2, 3, 4):
                specs[t].append(shard)
            out_specs += [shard] * 4
            out_shape += [jax.ShapeDtypeStruct(wt.shape, f32)] * 4
            continue
        specs[0].append(pl.BlockSpec((None, rb, cols), functools.partial(lambda i, s, cut: (s[0], i if cut else 0, 0), cut=cut)))
        specs[1].append(pl.BlockSpec((N_PEERS, rb, cols), functools.partial(lambda i, s, cut: (0, i if cut else 0, 0), cut=cut)))
        if wt.ndim == 2:
            shard = pl.BlockSpec((rb, cols), functools.partial(lambda i, s, cut: (i if cut else 0, 0), cut=cut))
        elif wt.shape[0] == 1:
            shard = pl.BlockSpec((None, rb, cols), functools.partial(lambda i, s, cut: (0, i if cut else 0, 0), cut=cut))
        else:
            shard = pl.BlockSpec(wt.shape, functools.partial(lambda i, s, nd: (0,) * nd, nd=wt.ndim))
        for t in (2, 3, 4):
            specs[t].append(shard)
        out_specs += [shard] * 4
        out_shape += [jax.ShapeDtypeStruct(wt.shape, f32)] * 4
    outs = pl.pallas_call(
        body, name=name,
        grid_spec=pltpu.PrefetchScalarGridSpec(
            num_scalar_prefetch=1, grid=(n_blocks,), in_specs=sum(specs, []) + [pl.BlockSpec(memory_space=pl.ANY)] * len(after),
            out_specs=out_specs),
        out_shape=out_shape,
        compiler_params=_cp(("arbitrary",), VMEM_LIMIT),
    )(me_idx, *blocks, *lands, *ws, *ms, *vs, *after)
    return [outs[4 * w:4 * w + 4] for w in range(n)]


def _adam_gains(parts, ws, ms, vs):
    n = len(ws)

    def body(p_ref, *refs):
        ins, outs = refs[:3 * n], refs[3 * n:]
        g_all = p_ref[0]
        for k in range(1, N_DEV):
            g_all = g_all + p_ref[k]
        for w, (off, lanes) in enumerate(SMALL.values()):
            w_ref, m_ref, v_ref = ins[w], ins[n + w], ins[2 * n + w]
            if len(w_ref.shape) == 2:
                pieces = [(slice(None), off, lanes)]
            else:
                pieces = [((slice(None), h), off + LANES * h, LANES) for h in range(w_ref.shape[1])]
            for at, o, ln in pieces:
                g = g_all[:, o:o + ln]
                vals = (g,) + _adam_math(w_ref[at], g, m_ref[at], v_ref[at])
                for t, val in enumerate(vals):
                    outs[4 * w + t][at] = val
        outs[4 * n][...] = g_all[:, LOSS_OFF:LOSS_OFF + LANES]

    out_shape = sum([[jax.ShapeDtypeStruct(w.shape, f32)] * 4 for w in ws], []) + [jax.ShapeDtypeStruct((1, LANES), f32)]
    outs = pl.pallas_call(body, name="adamw_gains", out_shape=out_shape)(parts, *ws, *ms, *vs)
    return [outs[4 * w:4 * w + 4] for w in range(n)], outs[4 * n]


def _fwd_in(x, g_mix, wz, tm):
    s, d = x.shape

    def body(x_ref, g_ref, w_ref, h_ref, z_ref):
        h, _ = _rms_fwd(x_ref[...], g_ref[...], d)
        hb = h.astype(bf16)
        h_ref[...] = hb
        z_ref[...] = _dot_nt(hb, w_ref[...])

    return pl.pallas_call(
        body, name="fwd_in", grid=(s // tm,),
        in_specs=[pl.BlockSpec((tm, d), lambda i: (i, 0)), _const_spec((1, d)), _const_spec((Z_W, d))],
        out_specs=[pl.BlockSpec((tm, d), lambda i: (i, 0)), pl.BlockSpec((tm, Z_W), lambda i: (i, 0))],
        out_shape=[jax.ShapeDtypeStruct((s, d), bf16), jax.ShapeDtypeStruct((s, Z_W), f32)],
        compiler_params=_cp(("parallel",), VMEM_LIMIT),
    )(x, g_mix, wz)


def _mla_qk_fwd(cq, ckv, g_qa, g_kva, wqb, wkvb):
    cqn, rq = _rms_fwd(cq, g_qa, Q_LORA)
    ckvn, rkv = _rms_fwd(ckv, g_kva, KV_LORA)
    cqn_b, ckvn_b = cqn.astype(bf16), ckvn.astype(bf16)
    q0 = _dot(cqn_b, wqb)
    kv0 = _dot(ckvn_b, wkvb)
    return cqn_b, rq, ckvn_b, rkv, q0, kv0


def _fwd_mla_proj(z, cosb, sina, sinb, g_qa, g_kva, wqb, wkvb, g_qn, g_kn, tm):
    s = z.shape[0]
    hh = MLA_HEADS

    def body(cq_ref, ckv_ref, kr_ref, c_ref, sa_ref, sb_ref, gqa_ref, gkva_ref, wqb_ref, wkvb_ref, gqn_ref, gkn_ref,
             q_ref, k_ref, v_ref):
        _, _, _, _, q0, kv0 = _mla_qk_fwd(cq_ref[...], ckv_ref[...], gqa_ref[...], gkva_ref[...], wqb_ref[...], wkvb_ref[...])
        kr = kr_ref[...]
        c, sa, sb = c_ref[...], sa_ref[...], sb_ref[...]
        gqn, gkn = gqn_ref[...], gkn_ref[...]
        kr_sq = jnp.sum(kr * kr, axis=-1, keepdims=True)
        for h in range(hh):
            qh = q0[:, QK_PAD * h:QK_PAD * (h + 1)]
            qn, _ = _rms_fwd(qh, gqn, QK_HEAD)
            q_ref[h, :, 0:128] = qn[:, 0:128].astype(bf16)
            q_ref[h, :, 128:256] = _rope(qn[:, 128:256], c, sa, sb).astype(bf16)
            kn_ = kv0[:, 256 * h:256 * h + 128]
            rk = lax.rsqrt((jnp.sum(kn_ * kn_, axis=-1, keepdims=True) + kr_sq) * (1.0 / QK_HEAD) + EPS)
            k_ref[h, :, 0:128] = (kn_ * rk * gkn[:, 0:128]).astype(bf16)
            k_ref[h, :, 128:256] = _rope(kr * rk * gkn[:, 128:256], c, sa, sb).astype(bf16)
            v_ref[h] = kv0[:, 256 * h + 128:256 * h + 256].astype(bf16)

    row128 = pl.BlockSpec((tm, 128), lambda i: (i, 0))
    return pl.pallas_call(
        body, name="fwd_mla_proj", grid=(s // tm,),
        in_specs=[pl.BlockSpec((tm, 256), lambda i: (i, Z_CQ // 256)), pl.BlockSpec((tm, 256), lambda i: (i, Z_CKV // 256)),
                  pl.BlockSpec((tm, 128), lambda i: (i, Z_KR // 128)), row128, row128, row128,
                  _const_spec((1, 256)), _const_spec((1, 256)), _const_spec((256, 1024)), _const_spec((256, 1024)),
                  _const_spec((1, 256)), _const_spec((1, 256))],
        out_specs=[pl.BlockSpec((hh, tm, QK_PAD), lambda i: (0, i, 0)), pl.BlockSpec((hh, tm, QK_PAD), lambda i: (0, i, 0)),
                   pl.BlockSpec((hh, tm, V_HEAD), lambda i: (0, i, 0))],
        out_shape=[jax.ShapeDtypeStruct((hh, s, QK_PAD), bf16), jax.ShapeDtypeStruct((hh, s, QK_PAD), bf16),
                   jax.ShapeDtypeStruct((hh, s, V_HEAD), bf16)],
        compiler_params=_cp(("parallel",), VMEM_LIMIT),
    )(z, z, z, cosb, sina, sinb, g_qa, g_kva, wqb, wkvb, g_qn, g_kn)


def _fwd_attn(q, k, v, tq, after):
    hh, s, _ = q.shape

    n_sub = max(1, tq // ATTN_SUB_ROWS)

    def body(q_ref, k_ref, v_ref, after_ref, o_ref, o32_ref):
        for t in range(n_sub):
            rows = slice(t * (tq // n_sub), (t + 1) * (tq // n_sub))
            sc = _dot_nt(q_ref[rows, :], k_ref[...])
            p = jnp.exp2((sc - jnp.max(sc, axis=-1, keepdims=True)) * (ATTN_SCALE * LOG2_E))
            l = jnp.sum(p, axis=-1, keepdims=True)
            o = _dot(p.astype(bf16), v_ref[...]) * (1.0 / l)
            o_ref[rows, :] = o.astype(bf16)
            o32_ref[rows, :] = o

    out = pl.BlockSpec((tq, V_HEAD), lambda h, i: (i, h))
    return pl.pallas_call(
        body, name="fwd_attn", grid=(hh, s // tq),
        in_specs=[pl.BlockSpec((None, tq, QK_PAD), lambda h, i: (h, i, 0)),
                  pl.BlockSpec((None, s, QK_PAD), lambda h, i: (h, 0, 0)),
                  pl.BlockSpec((None, s, V_HEAD), lambda h, i: (h, 0, 0)), pl.BlockSpec(memory_space=pl.ANY)],
        out_specs=[out, out],
        out_shape=[jax.ShapeDtypeStruct((s, hh * V_HEAD), bf16), jax.ShapeDtypeStruct((s, hh * V_HEAD), f32)],
        compiler_params=_cp(("parallel", "parallel"), VMEM_LIMIT),
    )(q, k, v, after)


def _split3(x):
    hi = x.astype(bf16)
    r1 = x - hi.astype(f32)
    mid = r1.astype(bf16)
    lo = (r1 - mid.astype(f32)).astype(bf16)
    return jnp.concatenate([hi, mid, lo], axis=-1)


def _tri_sum(tri, x):
    y = _dot(tri, _split3(x))
    return y[:, 0:128] + y[:, 128:256] + y[:, 256:384]


GLA_GROUP = 4
GLA_ROWS = GLA_GROUP * CHUNK
GLA_HEADS_PER_STEP = 2


def _gla_masks(rev):
    row = lax.broadcasted_iota(jnp.int32, (GLA_ROWS, GLA_ROWS), 0)
    col = lax.broadcasted_iota(jnp.int32, (GLA_ROWS, GLA_ROWS), 1)
    shift = CHUNK.bit_length() - 1
    same = (jnp.right_shift(row, shift) == jnp.right_shift(col, shift)).astype(f32)
    lower, upper = (row >= col).astype(f32) * same, (row <= col).astype(f32) * same
    keep, keep_t = (upper, lower) if rev else (lower, upper)
    chunk_of = jnp.right_shift(lax.broadcasted_iota(jnp.int32, (GLA_ROWS, 1), 0), shift)
    return keep, keep.astype(bf16), keep_t.astype(bf16), [(chunk_of == c).astype(f32) for c in range(GLA_GROUP)]


def _gla_gates(hq, hf, lower):
    sg = _sigmoid(hf)
    f = lower + (1.0 - lower) * sg
    return hq * _sigmoid(hq), 1.0 - f, jnp.log(f), f, sg


def _gla_last_mid(b, rev):
    b3 = b.reshape(GLA_GROUP, CHUNK, 128)
    last, mid = (0, CHUNK // 2) if rev else (CHUNK - 1, CHUNK // 2 - 1)
    return b3[:, last:last + 1, :], b3[:, mid:mid + 1, :]


def _gla_per_row(per_chunk):
    return jnp.broadcast_to(per_chunk, (GLA_GROUP, CHUNK, 128)).reshape(GLA_ROWS, 128)


def _gla_block_diag(x, row_masks):
    return jnp.concatenate([(x * m).astype(bf16) for m in row_masks], axis=-1)


def _gla_diag(y):
    return jnp.concatenate([y[CHUNK * c:CHUNK * (c + 1), 128 * c:128 * (c + 1)] for c in range(GLA_GROUP)], axis=0)


def _gla_rows(n, n_groups, rev):
    ne = n_groups - 1 - n if rev else n
    return pl.ds(pl.multiple_of(ne * GLA_ROWS, GLA_ROWS), GLA_ROWS), ne * GLA_GROUP


def _gla_scan_order(rev):
    return tuple(reversed(range(GLA_GROUP))) if rev else tuple(range(GLA_GROUP))


def _fwd_gla(z, lb4):
    s = z.shape[0]
    n_groups = s // GLA_ROWS
    assert n_groups % 2 == 0
    hp = GLA_HEADS_PER_STEP
    chains = [(hh, rev) for hh in range(hp) for rev in (False, True)]

    def body(hq_ref, hff_ref, hfb_ref, hi_ref, lb_ref, o_ref, b_ref, states_ref, st_ref, stage_ref, b_stage, sems):
        st_ref[...] = jnp.zeros_like(st_ref)
        masks = {rev: _gla_masks(rev) for rev in (False, True)}
        lowers = [_sigmoid(lb_ref[int(rev):int(rev) + 1, 128 * hh:128 * (hh + 1)]
                           - lb_ref[2 + int(rev):3 + int(rev), 128 * hh:128 * (hh + 1)]) for hh, rev in chains]

        def states_out(slot, ci, chunk0):
            hh, rev = chains[ci]
            head = pl.program_id(0) * hp + hh
            rows = pl.ds(pl.multiple_of(chunk0 * CHUNK, GLA_ROWS), GLA_ROWS)
            return _Both(
                pltpu.make_async_copy(stage_ref.at[slot, ci], states_ref.at[head, int(rev), pl.ds(chunk0, GLA_GROUP)],
                                      sems.at[slot, ci]),
                pltpu.make_async_copy(b_stage.at[slot, ci], b_ref.at[int(rev), rows, pl.ds(pl.multiple_of(head * 128, 128), 128)],
                                      sems.at[slot, len(chains) + ci]))

        def make_step(first):
            def step(n, carry):
                slot = n % 2

                @pl.when(n >= 2)
                def _():
                    for ci in range(len(chains)):
                        states_out(slot, ci, 0).wait()

                for ci, (hh, rev) in enumerate(chains):
                    cols = slice(128 * hh, 128 * (hh + 1))
                    rows, chunk0 = _gla_rows(n, n_groups, rev)
                    maskf, tri, _, row_masks = masks[rev]
                    hf_ref = hfb_ref if rev else hff_ref
                    q, k, logf, _, _ = _gla_gates(hq_ref[rows, cols], hf_ref[rows, cols], lowers[ci])
                    vb = hi_ref[rows, cols].astype(bf16)
                    b = _tri_sum(tri, logf)
                    b_stage[slot, ci] = b
                    b_last3, b_mid3 = _gla_last_mid(b, rev)
                    b_last, b_mid = _gla_per_row(b_last3), _gla_per_row(b_mid3)
                    qi = (q * jnp.exp(b - b_mid)).astype(bf16)
                    ki = (k * jnp.exp(b_mid - b)).astype(bf16)
                    a = (_dot_nt(qi, ki) * maskf).astype(bf16)
                    kv = _dot_tn(vb, _gla_block_diag(k * jnp.exp(b_last - b), row_masks))
                    decay3 = jnp.exp(b_last3)
                    st = st_ref[ci]
                    before = [None] * GLA_GROUP
                    for c in _gla_scan_order(rev):
                        stage_ref[slot, ci, c] = st
                        before[c] = st.astype(bf16)
                        st = st * decay3[c] + kv[:, 128 * c:128 * (c + 1)]
                    st_ref[ci] = st
                    states_out(slot, ci, chunk0).start()
                    inter = _dot_nt((q * jnp.exp(b)).astype(bf16), jnp.concatenate(before, axis=0))
                    o = _dot(a, vb) + _gla_diag(inter)
                    if first:
                        o_ref[rows, cols] = o
                    else:
                        o_ref[rows, cols] += o
                return carry
            return step

        lax.fori_loop(0, n_groups // 2, make_step(True), 0)
        lax.fori_loop(n_groups // 2, n_groups, make_step(False), 0)
        for slot in range(2):
            for ci in range(len(chains)):
                states_out(slot, ci, 0).wait()

    w = 128 * hp
    col = lambda base: pl.BlockSpec((s, w), lambda h: (0, base // w + h))
    return pl.pallas_call(
        body, name="fwd_gla", grid=(HG_HEADS // hp,),
        in_specs=[col(Z_HQ), col(Z_HFF), col(Z_HFB), col(Z_HI), pl.BlockSpec((4, w), lambda h: (0, h))],
        out_specs=[pl.BlockSpec((s, w), lambda h: (0, h)), pl.BlockSpec(memory_space=pl.ANY), pl.BlockSpec(memory_space=pl.ANY)],
        out_shape=[jax.ShapeDtypeStruct((s, HG_HEADS * 128), f32), jax.ShapeDtypeStruct((2, s, HG_HEADS * 128), f32),
                   jax.ShapeDtypeStruct((HG_HEADS, 2, s // CHUNK, 128, 128), f32)],
        scratch_shapes=[pltpu.VMEM((len(chains), 128, 128), f32), pltpu.VMEM((2, len(chains), GLA_GROUP, 128, 128), f32),
                        pltpu.VMEM((2, len(chains), GLA_ROWS, 128), f32), pltpu.SemaphoreType.DMA((2, 2 * len(chains)))],
        compiler_params=_cp(("parallel",), VMEM_LIMIT),
    )(z, z, z, z, lb4)


def _hg_out(o, hg, g_hgo):
    outs, ons, rs = [], [], []
    for h in range(HG_HEADS):
        oh = o[:, 128 * h:128 * (h + 1)]
        on, r = _rms_fwd(oh, g_hgo[:, 128 * h:128 * (h + 1)], 128)
        ons.append(on)
        rs.append(r)
    on = jnp.concatenate(ons, axis=-1)
    sg = _sigmoid(hg)
    return on * (hg * sg), on, rs, sg


def _fwd_mix_ffn(a, o, z, g_hgo, x, w_o, g_ffn, w_gate, w_up, w_down, tm):
    s, d = x.shape

    def body(a_ref, o_ref, hg_ref, gh_ref, x_ref, wo_ref, g_ref, wg_ref, wu_ref, wd_ref,
             x2_ref, cat_ref, x3_ref, gp_ref, up_ref):
        r, _, _, _ = _hg_out(o_ref[...], hg_ref[...], gh_ref[...])
        cat = jnp.concatenate([a_ref[...], r.astype(bf16)], axis=-1)
        cat_ref[...] = cat
        x2 = x_ref[...] + _dot(cat, wo_ref[...])
        x2_ref[...] = x2
        h, _ = _rms_fwd(x2, g_ref[...], d)
        hb = h.astype(bf16)
        gp = _dot_nt(hb, wg_ref[...])
        up = _dot_nt(hb, wu_ref[...])
        gp_ref[...] = gp.astype(bf16)
        up_ref[...] = up.astype(bf16)
        act = (gp * _sigmoid(gp) * up).astype(bf16)
        x3_ref[...] = x2 + _dot(act, wd_ref[...])

    row512 = pl.BlockSpec((tm, 512), lambda i: (i, 0))
    rowd = pl.BlockSpec((tm, d), lambda i: (i, 0))
    rowf = pl.BlockSpec((tm, D_FF), lambda i: (i, 0))
    return pl.pallas_call(
        body, name="fwd_mix_ffn", grid=(s // tm,),
        in_specs=[row512, row512, pl.BlockSpec((tm, 512), lambda i: (i, Z_HG // 512)), _const_spec((1, 512)), rowd,
                  _const_spec((d, d)), _const_spec((1, d)), _const_spec((D_FF, d)), _const_spec((D_FF, d)),
                  _const_spec((D_FF, d))],
        out_specs=[rowd, rowd, rowd, rowf, rowf],
        out_shape=[jax.ShapeDtypeStruct((s, d), f32), jax.ShapeDtypeStruct((s, d), bf16),
                   jax.ShapeDtypeStruct((s, d), f32), jax.ShapeDtypeStruct((s, D_FF), bf16),
                   jax.ShapeDtypeStruct((s, D_FF), bf16)],
        compiler_params=_cp(("parallel",), VMEM_LIMIT),
    )(a, o, z, g_hgo, x, w_o, g_ffn, w_gate, w_up, w_down)


def _ple_loss_fwd_bwd(x3, g_ple, w_pg, p, w_pp, target, tm):
    s, d = x3.shape
    cols = BIG["w_ple_proj"][1]

    def body(x_ref, g_ref, wg_ref, p_ref, wp_ref, t_ref, dx_ref, gwg_ref, gwp_ref, dg_ref, loss_ref, acc_ref, accp_ref):
        @pl.when(pl.program_id(0) == 0)
        def _():
            for r_ in (acc_ref, accp_ref, dg_ref, loss_ref):
                r_[...] = jnp.zeros_like(r_)

        x = x_ref[...]
        g = g_ref[...]
        h, r = _rms_fwd(x, g, d)
        hb = h.astype(bf16)
        pb = p_ref[...].astype(bf16)
        gate = _sigmoid(_dot(hb, wg_ref[...]))
        pp = jnp.concatenate([_dot(pb, wp_ref[j]) for j in range(N_DEV)], axis=-1)
        e = x + gate * pp - t_ref[...]
        loss_ref[...] += 0.5 * jnp.sum(e * e) * (1.0 / d)
        dy = e * (1.0 / d)
        dpre = (dy * pp * gate * (1.0 - gate)).astype(bf16)
        dx, dgx = _rms_bwd(_dot_nt(dpre, wg_ref[...]), x, r, g, d)
        dx_ref[...] = dy + dx
        dg_ref[...] += jnp.sum(dgx, axis=0, keepdims=True)
        acc_ref[...] += _dot_tn(hb, dpre)
        accp_ref[...] += _dot_tn(pb, (dy * gate).astype(bf16))

        @pl.when(pl.program_id(0) == s // tm - 1)
        def _():
            gwg_ref[...] = acc_ref[...].astype(bf16)
            for j in range(N_DEV):
                gwp_ref[j] = accp_ref[:, cols * j:cols * (j + 1)].astype(bf16)

    rowd = pl.BlockSpec((tm, d), lambda i: (i, 0))
    return pl.pallas_call(
        body, name="ple_loss_fwd_bwd", grid=(s // tm,),
        in_specs=[rowd, _const_spec((1, d)), _const_spec((d, d)), pl.BlockSpec((tm, PLE_DIM), lambda i: (i, 0)),
                  _const_spec((N_DEV, PLE_DIM, cols)), rowd],
        out_specs=[rowd, _acc_spec((d, d)), _acc_spec((N_DEV, PLE_DIM, cols)), _acc_spec((1, d)), _acc_spec((8, 128))],
        out_shape=[jax.ShapeDtypeStruct((s, d), f32), jax.ShapeDtypeStruct((d, d), bf16),
                   jax.ShapeDtypeStruct((N_DEV, PLE_DIM, cols), bf16), jax.ShapeDtypeStruct((1, d), f32), jax.ShapeDtypeStruct((8, 128), f32)],
        scratch_shapes=[pltpu.VMEM((d, d), f32), pltpu.VMEM((PLE_DIM, d), f32)],
        compiler_params=_cp(("arbitrary",), VMEM_LIMIT),
    )(x3, g_ple, w_pg, p, w_pp, target)


def _bwd_ffn_hidden(d3, gp, up, w_down, tm, tf):
    s, d = d3.shape
    n_i = s // tm

    def body(d3_ref, gp_ref, up_ref, wd_ref, dgp_ref, dup_ref, gw_ref, acc_ref):
        i = pl.program_id(1)

        @pl.when(i == 0)
        def _():
            acc_ref[...] = jnp.zeros_like(acc_ref)

        gp, up = gp_ref[...].astype(f32), up_ref[...].astype(f32)
        sg = _sigmoid(gp)
        silu = gp * sg
        d3b = d3_ref[...].astype(bf16)
        acc_ref[...] += _dot_tn((silu * up).astype(bf16), d3b)
        dact = _dot_nt(d3b, wd_ref[...])
        dgp_ref[...] = (dact * up * (sg * (1.0 + gp * (1.0 - sg)))).astype(bf16)
        dup_ref[...] = (dact * silu).astype(bf16)

        @pl.when(i == n_i - 1)
        def _():
            gw_ref[...] = acc_ref[...].astype(bf16)

    rowf = pl.BlockSpec((tm, tf), lambda f, i: (i, f))
    wrow = pl.BlockSpec((tf, d), lambda f, i: (f, 0))
    return pl.pallas_call(
        body, name="bwd_ffn_hidden", grid=(D_FF // tf, n_i),
        in_specs=[pl.BlockSpec((tm, d), lambda f, i: (i, 0)), rowf, rowf, wrow],
        out_specs=[rowf, rowf, wrow],
        out_shape=[jax.ShapeDtypeStruct((s, D_FF), bf16)] * 2 + [jax.ShapeDtypeStruct((D_FF, d), bf16)],
        scratch_shapes=[pltpu.VMEM((tf, d), f32)],
        compiler_params=_cp(("parallel", "arbitrary"), VMEM_LIMIT),
    )(d3, gp, up, w_down)


def _bwd_ffn_in(d3, x2, dgp, dup, g_ffn, w_gate, w_up, tm):
    s, d = x2.shape

    def body(d3_ref, x_ref, dgp_ref, dup_ref, g_ref, wg_ref, wu_ref, d2_ref, h_ref, dg_ref):
        @pl.when(pl.program_id(0) == 0)
        def _():
            dg_ref[...] = jnp.zeros_like(dg_ref)

        x, g = x_ref[...], g_ref[...]
        dh = _dot(dgp_ref[...], wg_ref[...]) + _dot(dup_ref[...], wu_ref[...])
        h, r = _rms_fwd(x, g, d)
        h_ref[...] = h.astype(bf16)
        dx, dgx = _rms_bwd(dh, x, r, g, d)
        d2_ref[...] = d3_ref[...] + dx
        dg_ref[...] += jnp.sum(dgx, axis=0, keepdims=True)

    rowd = pl.BlockSpec((tm, d), lambda i: (i, 0))
    rowf = pl.BlockSpec((tm, D_FF), lambda i: (i, 0))
    return pl.pallas_call(
        body, name="bwd_ffn_in", grid=(s // tm,),
        in_specs=[rowd, rowd, rowf, rowf, _const_spec((1, d)), _const_spec((D_FF, d)), _const_spec((D_FF, d))],
        out_specs=[rowd, rowd, _acc_spec((1, d))],
        out_shape=[jax.ShapeDtypeStruct((s, d), f32), jax.ShapeDtypeStruct((s, d), bf16), jax.ShapeDtypeStruct((1, d), f32)],
        compiler_params=_cp(("arbitrary",), VMEM_LIMIT),
    )(d3, x2, dgp, dup, g_ffn, w_gate, w_up)


def _bwd_mix(d2, w_o, o, z, g_hgo, cat, tm):
    s, d = d2.shape

    def body(d2_ref, w_ref, o_ref, hg_ref, g_ref, cat_ref, da_ref, do_ref, dhg_ref, dg_ref, gw_ref, acc_ref):
        @pl.when(pl.program_id(0) == 0)
        def _():
            dg_ref[...] = jnp.zeros_like(dg_ref)
            acc_ref[...] = jnp.zeros_like(acc_ref)

        d2b = d2_ref[...].astype(bf16)
        acc_ref[...] += _dot_tn(cat_ref[...], d2b)

        @pl.when(pl.program_id(0) == s // tm - 1)
        def _():
            gw_ref[...] = acc_ref[...].astype(bf16)
        dcat = _dot_nt(d2b, w_ref[...])
        da_ref[...] = dcat[:, 0:512].astype(bf16)
        dr = dcat[:, 512:1024]
        o, hg, g = o_ref[...], hg_ref[...], g_ref[...]
        _, on, rs, sg = _hg_out(o, hg, g)
        dhg_ref[...] = (dr * on * (sg * (1.0 + hg * (1.0 - sg)))).astype(bf16)
        don = dr * (hg * sg)
        dgs = []
        for h in range(HG_HEADS):
            cols = slice(128 * h, 128 * (h + 1))
            dx, dgx = _rms_bwd(don[:, cols], o[:, cols], rs[h], g[:, cols], 128)
            do_ref[:, cols] = dx
            dgs.append(jnp.sum(dgx, axis=0, keepdims=True))
        dg_ref[...] += jnp.concatenate(dgs, axis=-1)

    row512 = pl.BlockSpec((tm, 512), lambda i: (i, 0))
    return pl.pallas_call(
        body, name="bwd_mix", grid=(s // tm,),
        in_specs=[pl.BlockSpec((tm, d), lambda i: (i, 0)), _const_spec((d, d)), row512,
                  pl.BlockSpec((tm, 512), lambda i: (i, Z_HG // 512)), _const_spec((1, 512)), pl.BlockSpec((tm, d), lambda i: (i, 0))],
        out_specs=[row512, row512, row512, _acc_spec((1, 512)), _acc_spec((d, d))],
        out_shape=[jax.ShapeDtypeStruct((s, 512), bf16), jax.ShapeDtypeStruct((s, 512), f32), jax.ShapeDtypeStruct((s, 512), bf16),
                   jax.ShapeDtypeStruct((1, 512), f32), jax.ShapeDtypeStruct((d, d), bf16)],
        scratch_shapes=[pltpu.VMEM((d, d), f32)],
        compiler_params=_cp(("arbitrary",), VMEM_LIMIT),
    )(d2, w_o, o, z, g_hgo, cat)


def _bwd_gla(z, lb4, do, b_fwd, states):
    s = z.shape[0]
    n_chunks = s // CHUNK
    n_groups = s // GLA_ROWS
    assert n_groups % 2 == 0

    def body(hq_ref, hff_ref, hfb_ref, hi_ref, lb_ref, do_ref, b_all, st_all, dhq_ref, dhff_ref, dhfb_ref, dhi_ref, dlb_ref,
             dst_ref, dq_acc, dv_acc, dlow_ref):
        dirs = (False, True)
        masks = [_gla_masks(rev) for rev in dirs]
        lowers = [_sigmoid(lb_ref[int(rev):int(rev) + 1, :] - lb_ref[2 + int(rev):3 + int(rev), :]) for rev in dirs]
        hf_refs, dhf_refs = (hff_ref, hfb_ref), (dhff_ref, dhfb_ref)

        dst_ref[...] = jnp.zeros_like(dst_ref)
        dlow_ref[...] = jnp.zeros_like(dlow_ref)

        def make_bwd_step(first):
            def bwd_step(j, carry):
                n = n_groups - 1 - j
                for d, rev in enumerate(dirs):
                    maskf, _, tri_t, row_masks = masks[d]
                    lower = lowers[d]
                    rows, chunk0 = _gla_rows(n, n_groups, rev)
                    hq, hf = hq_ref[rows, :], hf_refs[d][rows, :]
                    q, k, _, f, sg = _gla_gates(hq, hf, lower)
                    v = hi_ref[rows, :]
                    dout = do_ref[rows, :]
                    b = b_all[d, rows, :]
                    b_last3, b_mid3 = _gla_last_mid(b, rev)
                    b_last, b_mid = _gla_per_row(b_last3), _gla_per_row(b_mid3)
                    e1, e2, e3, e4 = jnp.exp(b - b_mid), jnp.exp(b_mid - b), jnp.exp(b_last - b), jnp.exp(b)
                    decay3 = jnp.exp(b_last3)
                    qi, ki, kt, qt = q * e1, k * e2, k * e3, q * e4
                    qib, kib, ktb = qi.astype(bf16), ki.astype(bf16), kt.astype(bf16)
                    vb, dob = v.astype(bf16), dout.astype(bf16)
                    a = (_dot_nt(qib, kib) * maskf).astype(bf16)
                    da = (_dot_nt(dob, vb) * maskf).astype(bf16)
                    dqi = _dot(da, kib)
                    dki = _dot_tn(da, qib)
                    into_state = _dot_tn(dob, _gla_block_diag(qt, row_masks))
                    dst = dst_ref[d]
                    sts, dsts, ddecay = [None] * GLA_GROUP, [None] * GLA_GROUP, [None] * GLA_GROUP
                    for c in reversed(_gla_scan_order(rev)):
                        sts[c] = st_all[d, chunk0 + c]
                        dsts[c] = dst.astype(bf16)
                        ddecay[c] = jnp.sum(dst * sts[c], axis=0, keepdims=True)[None]
                        dst = dst * decay3[c] + into_state[:, 128 * c:128 * (c + 1)]
                    dst_ref[d] = dst
                    dv = _dot_tn(a, dob) + _gla_diag(_dot_nt(ktb, jnp.concatenate(dsts, axis=0)))
                    dqt = _gla_diag(_dot(dob, jnp.concatenate([x.astype(bf16) for x in sts], axis=-1)))
                    dkt = _gla_diag(_dot(vb, jnp.concatenate(dsts, axis=-1)))
                    dq = dqi * e1 + dqt * e4
                    dk = dki * e2 + dkt * e3
                    db = dqi * qi - dki * ki + dqt * qt - dkt * kt
                    dlast3 = (jnp.sum((dkt * kt).reshape(GLA_GROUP, CHUNK, 128), axis=1, keepdims=True)
                              + jnp.concatenate(ddecay, axis=0) * decay3)
                    dlogf = _tri_sum(tri_t, db) + _gla_per_row(dlast3)
                    df = dlogf / f - dk
                    dhf_refs[d][rows, :] = (df * (1.0 - lower) * sg * (1.0 - sg)).astype(bf16)
                    dlow_ref[d:d + 1, :] += jnp.sum(df * (1.0 - sg), axis=0, keepdims=True)
                    sq = _sigmoid(hq)
                    dhq = dq * (sq * (1.0 + hq * (1.0 - sq)))
                    if first:
                        dq_acc[rows, :] = dhq
                        dv_acc[rows, :] = dv
                    else:
                        dhq_ref[rows, :] = (dq_acc[rows, :] + dhq).astype(bf16)
                        dhi_ref[rows, :] = (dv_acc[rows, :] + dv).astype(bf16)
                return carry
            return bwd_step

        unroll = 4 if (n_groups // 2) % 4 == 0 else 2
        lax.fori_loop(0, n_groups // 2, make_bwd_step(True), 0, unroll=unroll)
        lax.fori_loop(n_groups // 2, n_groups, make_bwd_step(False), 0, unroll=unroll)

        for d in range(2):
            dl = dlow_ref[d:d + 1, :] * lowers[d] * (1.0 - lowers[d])
            dlb_ref[d:d + 1, :] = dl
            dlb_ref[2 + d:3 + d, :] = -dl

    col = lambda base: pl.BlockSpec((s, 128), lambda h: (0, base // 128 + h))
    return pl.pallas_call(
        body, name="bwd_gla", grid=(HG_HEADS,),
        in_specs=[col(Z_HQ), col(Z_HFF), col(Z_HFB), col(Z_HI), pl.BlockSpec((4, 128), lambda h: (0, h)), col(0),
                  pl.BlockSpec((2, s, 128), lambda h: (0, 0, h)),
                  pl.BlockSpec((None, 2, n_chunks, 128, 128), lambda h: (h, 0, 0, 0, 0), pipeline_mode=pl.Buffered(1))],
        out_specs=[col(0), col(0), col(0), col(0), pl.BlockSpec((4, 128), lambda h: (0, h))],
        out_shape=[jax.ShapeDtypeStruct((s, 512), bf16)] * 4 + [jax.ShapeDtypeStruct((4, 512), f32)],
        scratch_shapes=[pltpu.VMEM((2, 128, 128), f32), pltpu.VMEM((s, 128), f32), pltpu.VMEM((s, 128), f32),
                        pltpu.VMEM((2, 128), f32)],
        compiler_params=_cp(("parallel",), VMEM_LIMIT),
    )(z, z, z, z, lb4, do, b_fwd, states)


def _bwd_attn(q, k, v, da, a32, tq, after):
    hh, s, _ = q.shape

    n_sub = max(1, tq // ATTN_SUB_ROWS)

    def body(q_ref, k_ref, v_ref, do_ref, o_ref, after_ref, dq_ref, dk_ref, dv_ref, p_all, ds_all, dol_ref, dkt_ref, dvt_ref):
        @pl.when(pl.program_id(1) == 0)
        def _():
            dkt_ref[...] = jnp.zeros_like(dkt_ref)
            dvt_ref[...] = jnp.zeros_like(dvt_ref)

        kb, vb = k_ref[...], v_ref[...]
        for t in range(n_sub):
            rows = slice(t * (tq // n_sub), (t + 1) * (tq // n_sub))
            sc = _dot_nt(q_ref[rows, :], kb)
            p = jnp.exp2((sc - jnp.max(sc, axis=-1, keepdims=True)) * (ATTN_SCALE * LOG2_E))
            inv_l = 1.0 / jnp.sum(p, axis=-1, keepdims=True)
            p_all[rows, :] = p.astype(bf16)
            dob = do_ref[rows, :]
            dof = dob.astype(f32)
            delta = jnp.sum(dof * o_ref[rows, :], axis=-1, keepdims=True)
            ds_all[rows, :] = p_all[rows, :] * ((_dot_nt(dob, vb) - delta) * inv_l).astype(bf16)
            dq_ref[rows, :] = _dot(ds_all[rows, :], kb) * ATTN_SCALE
            dol_ref[rows, :] = (dof * inv_l).astype(bf16)
        dkt_ref[...] += _dot_tn(q_ref[...], ds_all[...])
        dvt_ref[...] += _dot_tn(dol_ref[...], p_all[...])

        @pl.when(pl.program_id(1) == s // tq - 1)
        def _():
            dk_ref[...] = dkt_ref[...].T * ATTN_SCALE
            dv_ref[...] = dvt_ref[...].T

    return pl.pallas_call(
        body, name="bwd_attn", grid=(hh, s // tq),
        in_specs=[pl.BlockSpec((None, tq, QK_PAD), lambda h, i: (h, i, 0)),
                  pl.BlockSpec((None, s, QK_PAD), lambda h, i: (h, 0, 0)),
                  pl.BlockSpec((None, s, V_HEAD), lambda h, i: (h, 0, 0)),
                  pl.BlockSpec((tq, V_HEAD), lambda h, i: (i, h)), pl.BlockSpec((tq, V_HEAD), lambda h, i: (i, h)),
                  pl.BlockSpec(memory_space=pl.ANY)],
        out_specs=[pl.BlockSpec((None, tq, QK_PAD), lambda h, i: (h, i, 0)),
                   pl.BlockSpec((None, s, QK_PAD), lambda h, i: (h, 0, 0)),
                   pl.BlockSpec((None, s, V_HEAD), lambda h, i: (h, 0, 0))],
        out_shape=[jax.ShapeDtypeStruct((hh, s, QK_PAD), f32), jax.ShapeDtypeStruct((hh, s, QK_PAD), f32),
                   jax.ShapeDtypeStruct((hh, s, V_HEAD), f32)],
        scratch_shapes=[pltpu.VMEM((tq, s), bf16), pltpu.VMEM((tq, s), bf16), pltpu.VMEM((tq, V_HEAD), bf16),
                        pltpu.VMEM((QK_PAD, s), f32), pltpu.VMEM((V_HEAD, s), f32)],
        compiler_params=_cp(("parallel", "arbitrary"), VMEM_LIMIT),
    )(q, k, v, da, a32, after)


def _bwd_mla_proj(z, dq, dk, dv, cosb, sina, sinb, g_qa, g_kva, wqb, wkvb, g_qn, g_kn, tm):
    s = z.shape[0]
    hh = MLA_HEADS

    def body(cq_ref, ckv_ref, kr_ref, dq_ref, dk_ref, dv_ref, c_ref, sa_ref, sb_ref, gqa_ref, gkva_ref, wqb_ref, wkvb_ref,
             gqn_ref, gkn_ref, dz_ref, gwqb_ref, gwkvb_ref, dgqa_ref, dgkva_ref, dgqn_ref, dgkn_ref, dq0_ref, dkv0_ref):
        @pl.when(pl.program_id(0) == 0)
        def _():
            for r in (gwqb_ref, gwkvb_ref, dgqa_ref, dgkva_ref, dgqn_ref, dgkn_ref):
                r[...] = jnp.zeros_like(r)

        cq, ckv, kr = cq_ref[...], ckv_ref[...], kr_ref[...]
        gqa, gkva, gqn, gkn = gqa_ref[...], gkva_ref[...], gqn_ref[...], gkn_ref[...]
        cqn_b, rq, ckvn_b, rkv, q0, kv0 = _mla_qk_fwd(cq, ckv, gqa, gkva, wqb_ref[...], wkvb_ref[...])
        c, sa, sb = c_ref[...], -sa_ref[...], -sb_ref[...]
        kr_sq = jnp.sum(kr * kr, axis=-1, keepdims=True)
        dkr = jnp.zeros_like(kr)
        dgqn = jnp.zeros((1, QK_PAD), f32)
        dgkn = jnp.zeros((1, QK_PAD), f32)
        for h in range(hh):
            qh = q0[:, QK_PAD * h:QK_PAD * (h + 1)]
            rh = lax.rsqrt(jnp.sum(qh * qh, axis=-1, keepdims=True) * (1.0 / QK_HEAD) + EPS)
            dqh = dq_ref[h]
            dqn = jnp.concatenate([dqh[:, 0:128], _rope(dqh[:, 128:256], c, sa, sb)], axis=-1)
            dq0h, dgx = _rms_bwd(dqn, qh, rh, gqn, QK_HEAD)
            dq0_ref[:, QK_PAD * h:QK_PAD * (h + 1)] = dq0h.astype(bf16)
            dgqn = dgqn + jnp.sum(dgx, axis=0, keepdims=True)

            kn_ = kv0[:, 256 * h:256 * h + 128]
            k0 = jnp.concatenate([kn_, kr], axis=-1)
            rk = lax.rsqrt((jnp.sum(kn_ * kn_, axis=-1, keepdims=True) + kr_sq) * (1.0 / QK_HEAD) + EPS)
            dkh = dk_ref[h]
            dkn = jnp.concatenate([dkh[:, 0:128], _rope(dkh[:, 128:256], c, sa, sb)], axis=-1)
            dk0, dgx = _rms_bwd(dkn, k0, rk, gkn, QK_HEAD)
            dgkn = dgkn + jnp.sum(dgx, axis=0, keepdims=True)
            dkv0_ref[:, 256 * h:256 * h + 128] = dk0[:, 0:128].astype(bf16)
            dkv0_ref[:, 256 * h + 128:256 * h + 256] = dv_ref[h].astype(bf16)
            dkr = dkr + dk0[:, 128:256]
        dgqn_ref[...] += dgqn
        dgkn_ref[...] += dgkn
        gwqb_ref[...] += _dot_tn(cqn_b, dq0_ref[...])
        gwkvb_ref[...] += _dot_tn(ckvn_b, dkv0_ref[...])
        dcq, dgx = _rms_bwd(_dot_nt(dq0_ref[...], wqb_ref[...]), cq, rq, gqa, Q_LORA)
        dgqa_ref[...] += jnp.sum(dgx, axis=0, keepdims=True)
        dckv, dgx = _rms_bwd(_dot_nt(dkv0_ref[...], wkvb_ref[...]), ckv, rkv, gkva, KV_LORA)
        dgkva_ref[...] += jnp.sum(dgx, axis=0, keepdims=True)
        dz_ref[:, 0:256] = dcq.astype(bf16)
        dz_ref[:, 256:512] = dckv.astype(bf16)
        dz_ref[:, 512:640] = dkr.astype(bf16)

    row128 = pl.BlockSpec((tm, 128), lambda i: (i, 0))
    hd = lambda w: pl.BlockSpec((hh, tm, w), lambda i: (0, i, 0))
    return pl.pallas_call(
        body, name="bwd_mla_proj", grid=(s // tm,),
        in_specs=[pl.BlockSpec((tm, 256), lambda i: (i, Z_CQ // 256)), pl.BlockSpec((tm, 256), lambda i: (i, Z_CKV // 256)),
                  pl.BlockSpec((tm, 128), lambda i: (i, Z_KR // 128)), hd(QK_PAD), hd(QK_PAD), hd(V_HEAD),
                  row128, row128, row128,
                  _const_spec((1, 256)), _const_spec((1, 256)), _const_spec((256, 1024)), _const_spec((256, 1024)),
                  _const_spec((1, 256)), _const_spec((1, 256))],
        out_specs=[pl.BlockSpec((tm, 640), lambda i: (i, 0)), _acc_spec((256, 1024)), _acc_spec((256, 1024)),
                   _acc_spec((1, 256)), _acc_spec((1, 256)), _acc_spec((1, 256)), _acc_spec((1, 256))],
        out_shape=[jax.ShapeDtypeStruct((s, 640), bf16), jax.ShapeDtypeStruct((256, 1024), f32), jax.ShapeDtypeStruct((256, 1024), f32)]
        + [jax.ShapeDtypeStruct((1, 256), f32)] * 4,
        scratch_shapes=[pltpu.VMEM((tm, 1024), bf16), pltpu.VMEM((tm, 1024), bf16)],
        compiler_params=_cp(("arbitrary",), VMEM_LIMIT),
    )(z, z, z, dq, dk, dv, cosb, sina, sinb, g_qa, g_kva, wqb, wkvb, g_qn, g_kn)


def _bwd_in(segments, wz, x, g_mix, d2, tm):
    s, d = x.shape
    n_seg = len(segments)

    def body(*refs):
        dz_refs, w_refs = refs[:n_seg], refs[n_seg:2 * n_seg]
        x_ref, g_ref, d2_ref, gx_ref, dg_ref = refs[2 * n_seg:]

        @pl.when(pl.program_id(0) == 0)
        def _():
            dg_ref[...] = jnp.zeros_like(dg_ref)

        dh = _dot(dz_refs[0][...], w_refs[0][...])
        for a_ref, w_ref in zip(dz_refs[1:], w_refs[1:]):
            dh = dh + _dot(a_ref[...], w_ref[...])
        x, g = x_ref[...], g_ref[...]
        r = lax.rsqrt(jnp.sum(x * x, axis=-1, keepdims=True) * (1.0 / d) + EPS)
        dx, dgx = _rms_bwd(dh, x, r, g, d)
        gx_ref[...] = d2_ref[...] + dx
        dg_ref[...] += jnp.sum(dgx, axis=0, keepdims=True)

    rowd = pl.BlockSpec((tm, d), lambda i: (i, 0))
    dz_specs = [pl.BlockSpec((tm, w), functools.partial(lambda i, j: (i, j), j=ja)) for _, w, ja, _ in segments]
    w_specs = [pl.BlockSpec((w, d), functools.partial(lambda i, j: (j, 0), j=jw), pipeline_mode=pl.Buffered(1))
               for _, w, _, jw in segments]
    return pl.pallas_call(
        body, name="bwd_in", grid=(s // tm,),
        in_specs=dz_specs + w_specs + [rowd, _const_spec((1, d)), rowd],
        out_specs=[rowd, _acc_spec((1, d))],
        out_shape=[jax.ShapeDtypeStruct((s, d), f32), jax.ShapeDtypeStruct((1, d), f32)],
        compiler_params=_cp(("arbitrary",), VMEM_LIMIT),
    )(*[a for a, _, _, _ in segments], *([wz] * n_seg), x, g_mix, d2)


def _mm_tn_many(a, bs, name, tm, transposed=False):
    kk, m = a.shape
    n_b = len(bs)
    tk = min(1024, kk)
    n_k = kk // tk

    def body(a_ref, *refs):
        b_refs, o_refs, acc_refs = refs[:n_b], refs[n_b:2 * n_b], refs[2 * n_b:]

        @pl.when(pl.program_id(1) == 0)
        def _():
            for acc in acc_refs:
                acc[...] = jnp.zeros_like(acc)
        a_blk = a_ref[...].astype(bf16)
        for b_ref, acc in zip(b_refs, acc_refs):
            acc[...] += _dot_tn(a_blk, b_ref[...].astype(bf16))

        @pl.when(pl.program_id(1) == n_k - 1)
        def _():
            for o_ref, acc in zip(o_refs, acc_refs):
                o_ref[...] = (acc[...].T if transposed else acc[...]).astype(bf16)

    if transposed:
        out_specs = [pl.BlockSpec((b.shape[1], tm), lambda i, k: (0, i)) for b in bs]
        out_shape = [jax.ShapeDtypeStruct((b.shape[1], m), bf16) for b in bs]
    else:
        out_specs = [pl.BlockSpec((tm, b.shape[1]), lambda i, k: (i, 0)) for b in bs]
        out_shape = [jax.ShapeDtypeStruct((m, b.shape[1]), bf16) for b in bs]
    return pl.pallas_call(
        body, name=name, grid=(m // tm, n_k),
        in_specs=[pl.BlockSpec((tk, tm), lambda i, k: (k, i))] + [pl.BlockSpec((tk, b.shape[1]), lambda i, k: (k, 0)) for b in bs],
        out_specs=out_specs,
        out_shape=out_shape,
        scratch_shapes=[pltpu.VMEM((tm, b.shape[1]), f32) for b in bs],
        compiler_params=_cp(("parallel", "arbitrary"), VMEM_LIMIT),
    )(a, *bs)


def _rope_tables(positions):
    inv_freq = ROPE_THETA ** (-jnp.arange(0, QK_ROPE, 2, dtype=f32) / QK_ROPE)
    ang = positions.astype(f32)[:, None] * inv_freq
    cos, sin = jnp.cos(ang), jnp.sin(ang)
    zero = jnp.zeros_like(cos)
    return (jnp.concatenate([cos, cos, zero, zero], axis=1), jnp.concatenate([zero, sin, zero, zero], axis=1),
            jnp.concatenate([-sin, zero, zero, zero], axis=1))


def _pad256(g):
    return jnp.pad(g.reshape(1, QK_HEAD), ((0, 0), (0, QK_PAD - QK_HEAD)))


RELAYOUT_BLOCKS = 8
FIRST = ("w_in", "w_qb", "w_kvb", "lb_param")
SECOND = ("w_o", "w_gate", "w_up", "w_down", "w_ple_gate", "w_ple_proj")
ROW_SHARDED = ("w_o", "w_down", "w_ple_gate")


def _col_moves(j):
    width = BIG["w_in"][0]
    lo = width * j
    w_in = [(max(lo, a) - lo, min(lo + width, b) - lo, d + max(lo, a) - a)
            for a, b, d in Z_SEGMENTS if max(lo, a) < min(lo + width, b)]
    head, half = divmod(j, 2)
    whole = lambda n: [(0, BIG[n][1], BIG[n][1] * j)]
    return {"w_in": w_in, "w_qb": [(0, 96, QK_PAD * head + 96 * half)], "w_kvb": whole("w_kvb"),
            "w_ple_proj": whole("w_ple_proj"), "lb_param": whole("lb_param")}


def _kernel_shape(name):
    rows, cols = BIG[name]
    if name == "w_in":
        return (Z_W, cols)
    return (rows, MLA_HEADS * QK_PAD if name == "w_qb" else N_DEV * cols)


def _relayout_specs(names, by_dev):
    specs = []
    for n in names:
        rows, cols = BIG[n]
        if n == "lb_param":
            specs.append(_acc_spec((N_DEV, rows, cols) if by_dev else _kernel_shape(n)))
        elif n == "w_in":
            cb = cols // RELAYOUT_BLOCKS
            specs.append(pl.BlockSpec((N_DEV, rows, cb), lambda i: (0, 0, i)) if by_dev else pl.BlockSpec((Z_W, cb), lambda i: (0, i)))
        elif by_dev:
            specs.append(pl.BlockSpec((N_DEV, rows // RELAYOUT_BLOCKS, cols), lambda i: (0, i, 0)))
        else:
            specs.append(pl.BlockSpec((rows // RELAYOUT_BLOCKS, _kernel_shape(n)[1]), lambda i: (i, 0)))
    return specs


def _weights_in(gathered, names, name):
    n = len(names)

    def body(*refs):
        ins, outs = dict(zip(names, refs[:n])), dict(zip(names, refs[n:]))
        if "w_in" in outs:
            outs["w_in"][Z_KR + QK_ROPE:Z_W, :] = jnp.zeros((Z_W - Z_KR - QK_ROPE, outs["w_in"].shape[1]), bf16)
        if "w_qb" in outs:
            for h in range(MLA_HEADS):
                outs["w_qb"][:, QK_PAD * h + QK_HEAD:QK_PAD * (h + 1)] = jnp.zeros((outs["w_qb"].shape[0], QK_PAD - QK_HEAD), bf16)
        for j in range(N_DEV):
            for wn, moves in _col_moves(j).items():
                if wn in outs:
                    for s0, s1, d0 in moves:
                        if wn == "w_in":
                            outs[wn][d0:d0 + s1 - s0, :] = ins[wn][j, s0:s1, :]
                        else:
                            outs[wn][:, d0:d0 + s1 - s0] = ins[wn][j, :, s0:s1]

    outs = pl.pallas_call(
        body, name=name, grid=(RELAYOUT_BLOCKS,), in_specs=_relayout_specs(names, True), out_specs=_relayout_specs(names, False),
        out_shape=[jax.ShapeDtypeStruct(_kernel_shape(wn), gathered[wn].dtype) for wn in names],
        compiler_params=_cp(("arbitrary",), VMEM_LIMIT),
    )(*[gathered[wn] for wn in names])
    return dict(zip(names, outs))


def _grads_out(sources, names, name):
    pieces = [(wn, start, arr) for wn in names for start, arr in sources[wn]]
    n_in = len(pieces)

    def body(*refs):
        outs = dict(zip(names, refs[n_in:]))

        def cols(wn, c0, c1):
            for (pn, start, arr), ref in zip(pieces, refs[:n_in]):
                if pn == wn and start <= c0 and c1 <= start + arr.shape[0 if wn == "w_in" else 1]:
                    return ref[c0 - start:c1 - start, :] if wn == "w_in" else ref[:, c0 - start:c1 - start]

        for j in range(N_DEV):
            for wn, moves in _col_moves(j).items():
                if wn in outs:
                    for s0, s1, d0 in moves:
                        if wn == "w_in":
                            outs[wn][j, s0:s1, :] = cols(wn, d0, d0 + s1 - s0).astype(bf16)
                        else:
                            outs[wn][j, :, s0:s1] = cols(wn, d0, d0 + s1 - s0).astype(bf16)

    def in_spec(wn, arr):
        if wn == "lb_param":
            return _acc_spec(arr.shape)
        if wn == "w_in":
            return pl.BlockSpec((arr.shape[0], arr.shape[1] // RELAYOUT_BLOCKS), lambda i: (0, i))
        return pl.BlockSpec((arr.shape[0] // RELAYOUT_BLOCKS, arr.shape[1]), lambda i: (i, 0))

    in_specs = [in_spec(wn, arr) for wn, _, arr in pieces]
    outs = pl.pallas_call(
        body, name=name, grid=(RELAYOUT_BLOCKS,), in_specs=in_specs, out_specs=_relayout_specs(names, True),
        out_shape=[jax.ShapeDtypeStruct((N_DEV, *BIG[wn]), bf16) for wn in names],
        compiler_params=_cp(("arbitrary",), VMEM_LIMIT),
    )(*[arr for _, _, arr in pieces])
    return dict(zip(names, outs))


def kernel(x, p, positions, g_mix, w_in, g_qa, g_kva, w_qb, w_kvb, g_qn, g_kn, lb_param, g_hgo, w_o, g_ffn, w_gate, w_up, w_down, g_ple, w_ple_gate, w_ple_proj, loss_target, m_g_mix, m_w_in, m_g_qa, m_g_kva, m_w_qb, m_w_kvb, m_g_qn, m_g_kn, m_lb_param, m_g_hgo, m_w_o, m_g_ffn, m_w_gate, m_w_up, m_w_down, m_g_ple, m_w_ple_gate, m_w_ple_proj, v_g_mix, v_w_in, v_g_qa, v_g_kva, v_w_qb, v_w_kvb, v_g_qn, v_g_kn, v_lb_param, v_g_hgo, v_w_o, v_g_ffn, v_w_gate, v_w_up, v_w_down, v_g_ple, v_w_ple_gate, v_w_ple_proj):
    w_all = dict(g_mix=g_mix, g_qa=g_qa, g_kva=g_kva, g_qn=g_qn, g_kn=g_kn, g_hgo=g_hgo, g_ffn=g_ffn, g_ple=g_ple,
                 w_in=w_in, w_qb=w_qb, w_kvb=w_kvb, w_o=w_o, w_gate=w_gate, w_up=w_up, w_down=w_down,
                 w_ple_gate=w_ple_gate, w_ple_proj=w_ple_proj, lb_param=lb_param)
    m_all = dict(g_mix=m_g_mix, g_qa=m_g_qa, g_kva=m_g_kva, g_qn=m_g_qn, g_kn=m_g_kn, g_hgo=m_g_hgo, g_ffn=m_g_ffn,
                 g_ple=m_g_ple, w_in=m_w_in, w_qb=m_w_qb, w_kvb=m_w_kvb, w_o=m_w_o, w_gate=m_w_gate, w_up=m_w_up,
                 w_down=m_w_down, w_ple_gate=m_w_ple_gate, w_ple_proj=m_w_ple_proj, lb_param=m_lb_param)
    v_all = dict(g_mix=v_g_mix, g_qa=v_g_qa, g_kva=v_g_kva, g_qn=v_g_qn, g_kn=v_g_kn, g_hgo=v_g_hgo, g_ffn=v_g_ffn,
                 g_ple=v_g_ple, w_in=v_w_in, w_qb=v_w_qb, w_kvb=v_w_kvb, w_o=v_w_o, w_gate=v_w_gate, w_up=v_w_up,
                 w_down=v_w_down, w_ple_gate=v_w_ple_gate, w_ple_proj=v_w_ple_proj, lb_param=v_lb_param)
    me_idx = jnp.stack([_me()]).astype(jnp.int32)
    x, p, positions, target = x[0], p[0, 0], positions[0], loss_target[0]
    s = x.shape[0]
    tm, tm_ffn, tq_f, tq_b = min(512, s), min(1024, s), min(2048, s), min(1024, s)
    g_mix, g_qa, g_kva, g_qn, g_kn, g_hgo, g_ffn, g_ple = (w_all[n].reshape(1, -1) for n in SMALL)
    g_qn_p, g_kn_p = _pad256(g_qn), _pad256(g_kn)
    cosb, sina, sinb = _rope_tables(positions)
    as_shard = lambda n, a: a[0].T if n in TRANSPOSED else a.reshape(BIG[n])
    shard = lambda n: as_shard(n, w_all[n])

    first = _all_gather([shard(n) for n in FIRST], [f32 if n == "lb_param" else bf16 for n in FIRST], "ag_first")
    lands = _cast_to_slot([shard(n) for n in SECOND], me_idx, first[0])
    ag2, token = _exchange_start([], lands, "ag_second_start")
    wk = _weights_in(dict(zip(FIRST, first)), FIRST, "weights_in_first")
    wz, wqb, wkvb, lb4 = (wk[n] for n in FIRST)

    h1, z = _fwd_in(x, g_mix, wz, tm)
    q, k, v = _fwd_mla_proj(z, cosb + token[0, 0], sina, sinb, g_qa, g_kva, wqb, wkvb, g_qn_p, g_kn_p, tm)
    o, gla_b, gla_states = _fwd_gla(z, lb4)
    a, a32 = _fwd_attn(q, k, v, tq_f, o)

    second = dict(zip(SECOND, _exchange_wait(ag2, [a, o], "ag_second_wait")[1]))
    w_pp = second["w_ple_proj"]
    w_o, w_down, w_pg, w_gate, w_up = (second[n].reshape(N_DEV * BIG[n][0], BIG[n][1]) for n in ROW_SHARDED + ("w_gate", "w_up"))

    x2, cat, x3, gp, up = _fwd_mix_ffn(a, o, z, g_hgo, x, w_o, g_ffn, w_gate, w_up, w_down, tm)
    d3, gw_pg, gw_pp, dg_ple, loss_tile = _ple_loss_fwd_bwd(x3, g_ple, w_pg, p, w_pp, target, tm)
    dgp, dup, gw_down = _bwd_ffn_hidden(d3, gp, up, w_down, tm, D_FF // 2)
    d2, h2, dg_ffn = _bwd_ffn_in(d3, x2, dgp, dup, g_ffn, w_gate, w_up, tm)
    da, do, dz_hg, dg_hgo, gw_o = _bwd_mix(d2, w_o, o, z, g_hgo, cat, tm)

    gw_gate, gw_up = _mm_tn_many(h2, [dgp, dup], "dw_gate_up", 512, transposed=True)
    blocks = {"w_ple_proj": gw_pp}
    row_grads = {"w_o": gw_o, "w_down": gw_down, "w_ple_gate": gw_pg, "w_gate": gw_gate, "w_up": gw_up}
    blocks.update({n: g.reshape(N_DEV, *BIG[n]) for n, g in row_grads.items()})
    empty = lambda names: [lax.empty((N_PEERS, *BIG[n]), bf16) for n in names]
    rs2, token = _exchange_start([blocks[n] for n in SECOND], empty(SECOND), "rs_second_start")

    dq, dk, dv = _bwd_attn(q, k, v, da, a32, tq_b, token)
    dz_hq, dz_hff, dz_hfb, dz_hi, dlb4 = _bwd_gla(z, lb4 + token[0, 0], do, gla_b, gla_states)
    dz_mla, gw_qb, gw_kvb, dg_qa, dg_kva, dg_qn, dg_kn = _bwd_mla_proj(
        z, dq, dk, dv, cosb, sina, sinb, g_qa, g_kva, wqb, wkvb, g_qn_p, g_kn_p, tm)

    gz = list(zip((Z_HQ, Z_HFF, Z_HFB, Z_HI, Z_HG, Z_CQ),
                  _mm_tn_many(h1, [dz_hq, dz_hff, dz_hfb, dz_hi, dz_hg, dz_mla], "dw_in", 1024, transposed=True)))
    blocks1 = _grads_out({"w_in": gz, "w_qb": [(0, gw_qb)], "w_kvb": [(0, gw_kvb)],
                          "lb_param": [(0, dlb4)]}, FIRST, "grads_out_first")
    rs1, token = _exchange_start([blocks1[n] for n in FIRST], empty(FIRST), "rs_first_start")

    result = {}

    def adam(names, lands, src, n_blocks, after=()):
        flipped = TRANSPOSED
        given = lambda arrs: [arrs[n][0].T if n in flipped else arrs[n] for n in names]
        outs = _adam_shards(me_idx, [src[n] for n in names], lands, given(w_all), given(m_all), given(v_all), n_blocks,
                            "adamw_" + names[0], after)
        for n, o in zip(names, outs):
            result[n] = [t.T[None] for t in o] if n in flipped else o
        return outs[0][0]

    blocks2, lands2 = (dict(zip(SECOND, arrs)) for arrs in _exchange_wait(rs2, [token], "rs_second_wait"))
    by2 = ("w_down", "w_gate", "w_up")
    by8 = tuple(n for n in SECOND if n not in by2)
    done = [adam(by8, [lands2[n] for n in by8], blocks2, 8), adam(by2, [lands2[n] for n in by2], blocks2, 2)]

    segments = [(dz_hq, 512, 0, Z_HQ // 512), (dz_hff, 512, 0, Z_HFF // 512), (dz_hfb, 512, 0, Z_HFB // 512),
                (dz_hi, 512, 0, Z_HI // 512), (dz_hg, 512, 0, Z_HG // 512), (dz_mla, 640, 0, Z_CQ // 640)]
    grad_x, dg_mix = _bwd_in(segments, wz, x, g_mix + token[0, 0], d2, tm)
    dgains = (dg_mix, dg_qa, dg_kva, dg_qn, dg_kn, dg_hgo, dg_ffn, dg_ple)

    vec = jnp.concatenate(list(dgains) + [loss_tile[0:1]], axis=1)
    parts = _all_gather([vec], [f32], "ag_gains")[0]
    outs, loss_row = _adam_gains(parts, [w_all[n] for n in SMALL], [m_all[n] for n in SMALL], [v_all[n] for n in SMALL])
    result.update(zip(SMALL, outs))

    blocks1, lands1 = _exchange_wait(rs1, [grad_x, loss_row, *done], "rs_first_wait")
    adam(FIRST, lands1, dict(zip(FIRST, blocks1)), 8)

    order = ("g_mix", "w_in", "g_qa", "g_kva", "w_qb", "w_kvb", "g_qn", "g_kn", "lb_param", "g_hgo", "w_o", "g_ffn",
             "w_gate", "w_up", "w_down", "g_ple", "w_ple_gate", "w_ple_proj")
    return (loss_row[0, 0], grad_x[None], *[result[n][k] for k in range(4) for n in order])
```
